```python
import jax, jax.numpy as jnp
from jax import lax
import numpy as np

D_MODEL = 1024
BATCH = 16
SEQ = 2048
DEPTH = 2

CHUNK = 64
Q_BLOCK = 128
N_BRANCH = 3
ATTN_HEADS = 8
ATTN_HEAD_DIM = 64
ATTN_WIDTH = ATTN_HEADS * ATTN_HEAD_DIM
POOL_WINDOWS = (2, 4, 8, 16)
POOL_GROUPS = 4
POOL_WIDTH = 512
POOL_GROUP_DIM = POOL_WIDTH // POOL_GROUPS
CONV_WIDTH = 512
CONV_K = 3
FFN_HIDDEN = -(-(8 * D_MODEL) // (3 * 256)) * 256
RMS_EPS = 1e-6
NEG_INF = -1e30

IN_SPLITS = (ATTN_WIDTH, ATTN_WIDTH, ATTN_WIDTH, ATTN_HEADS, POOL_WIDTH,
             CONV_WIDTH, CONV_WIDTH, CONV_WIDTH, N_BRANCH * D_MODEL)
IN_COLS = sum(IN_SPLITS)
IN_SPLIT_POINTS = tuple(int(p) for p in np.cumsum(IN_SPLITS)[:-1])

kernel_name = "hybrid_fox_pool_shortconv_gated_block"


def rms_norm(x, g):
    xf = x.astype(jnp.float32)
    y = xf * lax.rsqrt(jnp.mean(xf * xf, axis=-1, keepdims=True) + RMS_EPS)
    return (y * g.astype(jnp.float32)).astype(x.dtype)


def forgetting_attention(q, k, v, f_logit, b_f):
    B, S, H, Dh = q.shape
    log_f = jax.nn.log_sigmoid((f_logit + b_f).astype(jnp.float32))
    F = jnp.cumsum(log_f, axis=1)
    F_k = F.transpose(0, 2, 1)
    nblk = S // Q_BLOCK
    q_blocks = q.reshape(B, nblk, Q_BLOCK, H, Dh).swapaxes(0, 1)
    F_q_blocks = F.reshape(B, nblk, Q_BLOCK, H).swapaxes(0, 1)
    q_pos_blocks = jnp.arange(S, dtype=jnp.int32).reshape(nblk, Q_BLOCK)
    k_pos = jnp.arange(S, dtype=jnp.int32)
    scale = Dh ** -0.5

    def one_block(args):
        q_i, F_q_i, pos_i = args
        logits = jnp.einsum('bqhd,bkhd->bhqk', q_i, k).astype(jnp.float32) * scale
        decay = F_q_i.transpose(0, 2, 1)[..., :, None] - F_k[..., None, :]
        mask = k_pos[None, :] <= pos_i[:, None]
        logits = jnp.where(mask, logits + decay, NEG_INF)
        p = jax.nn.softmax(logits, axis=-1)
        return jnp.einsum('bhqk,bkhd->bqhd', p.astype(v.dtype), v)

    out = lax.map(one_block, (q_blocks, F_q_blocks, q_pos_blocks))
    return out.swapaxes(0, 1).reshape(B, S, H * Dh)


def multiscale_pool(u, pool_w, pool_scale):
    B, S, C = u.shape
    uf = u.astype(jnp.float32)
    c0 = jnp.pad(jnp.cumsum(uf, axis=1), ((0, 0), (1, 0), (0, 0)))
    n_avail = jnp.arange(1, S + 1, dtype=jnp.float32)[:, None]
    outs = []
    for g, w in enumerate(POOL_WINDOWS):
        sl = slice(g * POOL_GROUP_DIM, (g + 1) * POOL_GROUP_DIM)
        cg = c0[..., sl]
        lag = jnp.pad(cg, ((0, 0), (w, 0), (0, 0)))[:, :S + 1]
        mean = (cg - lag)[:, 1:] / jnp.minimum(n_avail, float(w))
        outs.append(mean - uf[..., sl])
    d = jnp.stack(outs, axis=2).astype(u.dtype)
    y = jnp.einsum('bsgc,gcd->bsgd', d, pool_w).reshape(B, S, C)
    return y * pool_scale


def short_gated_conv(v, gate_b, gate_c, conv_w):
    C = v.shape[-1]
    z = gate_c * v
    y = lax.conv_general_dilated(z, conv_w[:, None, :], window_strides=(1,),
                                 padding=[(CONV_K - 1, 0)],
                                 dimension_numbers=('NWC', 'WIO', 'NWC'),
                                 feature_group_count=C)
    return gate_b * y


def _fwd_setup_inputs(seed: int = 0) -> dict:
    key = jax.random.key(seed)
    ks = jax.random.split(key, 18)
    f32 = jnp.float32

    def nrm(k, shape, fan_in):
        return jax.random.normal(k, shape, f32) * (fan_in ** -0.5)

    return {
        "x": jax.random.normal(ks[0], (BATCH, SEQ, D_MODEL), f32),
        "attn_norm": 1.0 + 0.05 * jax.random.normal(ks[1], (DEPTH, D_MODEL), f32),
        "w_in": nrm(ks[2], (DEPTH, D_MODEL, IN_COLS), D_MODEL),
        "b_forget": jax.random.uniform(ks[3], (DEPTH, ATTN_HEADS), f32, 1.0, 4.0),
        "b_gate": 0.1 * jax.random.normal(ks[4], (DEPTH, N_BRANCH * D_MODEL), f32),
        "w_proj_attn": nrm(ks[5], (DEPTH, ATTN_WIDTH, D_MODEL), ATTN_WIDTH),
        "pool_w": nrm(ks[6], (DEPTH, POOL_GROUPS, POOL_GROUP_DIM, POOL_GROUP_DIM), POOL_GROUP_DIM),
        "pool_scale": 1.0 + 0.1 * jax.random.normal(ks[7], (DEPTH, POOL_WIDTH), f32),
        "w_proj_pool": nrm(ks[8], (DEPTH, POOL_WIDTH, D_MODEL), POOL_WIDTH),
        "conv_w": nrm(ks[9], (DEPTH, CONV_K, CONV_WIDTH), CONV_K),
        "w_proj_conv": nrm(ks[10], (DEPTH, CONV_WIDTH, D_MODEL), CONV_WIDTH),
        "w_out": nrm(ks[11], (DEPTH, D_MODEL, D_MODEL), D_MODEL),
        "ffn_norm": 1.0 + 0.05 * jax.random.normal(ks[12], (DEPTH, D_MODEL), f32),
        "w_gate_up": nrm(ks[13], (DEPTH, D_MODEL, 2 * FFN_HIDDEN), D_MODEL),
        "w_down": nrm(ks[14], (DEPTH, FFN_HIDDEN, D_MODEL), FFN_HIDDEN),
        "final_norm": 1.0 + 0.05 * jax.random.normal(ks[15], (D_MODEL,), f32),
    }


def _fwd_reference(x, attn_norm, w_in, b_forget, b_gate, w_proj_attn, pool_w, pool_scale,
              w_proj_pool, conv_w, w_proj_conv, w_out, ffn_norm, w_gate_up, w_down,
              final_norm):
    B, S, D = x.shape
    for l in range(DEPTH):
        h = rms_norm(x, attn_norm[l])
        proj = h @ w_in[l]
        q, k, v, f_logit, u, cv, cb, cc, g = jnp.split(proj, IN_SPLIT_POINTS, axis=-1)
        heads = (B, S, ATTN_HEADS, ATTN_HEAD_DIM)
        y_a = forgetting_attention(q.reshape(heads), k.reshape(heads), v.reshape(heads),
                                   f_logit, b_forget[l]) @ w_proj_attn[l]
        y_b = multiscale_pool(u, pool_w[l], pool_scale[l]) @ w_proj_pool[l]
        y_c = short_gated_conv(cv, cb, cc, conv_w[l]) @ w_proj_conv[l]
        gates = jax.nn.sigmoid(g + b_gate[l]).reshape(B, S, N_BRANCH, D)
        mixed = gates[..., 0, :] * y_a + gates[..., 1, :] * y_b + gates[..., 2, :] * y_c
        x = x + mixed @ w_out[l]
        h = rms_norm(x, ffn_norm[l])
        a, b = jnp.split(h @ w_gate_up[l], 2, axis=-1)
        x = x + (jax.nn.silu(a) * b) @ w_down[l]
    return rms_norm(x, final_norm)


import jax as _jax
import jax.numpy as _jnp

TWIN_FORMAT = 'train_step'
FWD_PARAMS = ['x', 'attn_norm', 'w_in', 'b_forget', 'b_gate', 'w_proj_attn', 'pool_w', 'pool_scale', 'w_proj_pool', 'conv_w', 'w_proj_conv', 'w_out', 'ffn_norm', 'w_gate_up', 'w_down', 'final_norm']
TWIN_WEIGHTS = ['attn_norm', 'w_in', 'b_forget', 'b_gate', 'w_proj_attn', 'pool_w', 'pool_scale', 'w_proj_pool', 'conv_w', 'w_proj_conv', 'w_out', 'ffn_norm', 'w_gate_up', 'w_down', 'final_norm']
TWIN_DIFF_INPUT = 'x'
TWIN_INPUTS = ['x', 'attn_norm', 'w_in', 'b_forget', 'b_gate', 'w_proj_attn', 'pool_w', 'pool_scale', 'w_proj_pool', 'conv_w', 'w_proj_conv', 'w_out', 'ffn_norm', 'w_gate_up', 'w_down', 'final_norm', 'loss_target', 'm_attn_norm', 'm_w_in', 'm_b_forget', 'm_b_gate', 'm_w_proj_attn', 'm_pool_w', 'm_pool_scale', 'm_w_proj_pool', 'm_conv_w', 'm_w_proj_conv', 'm_w_out', 'm_ffn_norm', 'm_w_gate_up', 'm_w_down', 'm_final_norm', 'v_attn_norm', 'v_w_in', 'v_b_forget', 'v_b_gate', 'v_w_proj_attn', 'v_pool_w', 'v_pool_scale', 'v_w_proj_pool', 'v_conv_w', 'v_w_proj_conv', 'v_w_out', 'v_ffn_norm', 'v_w_gate_up', 'v_w_down', 'v_final_norm']
TWIN_OUTPUTS = ['loss', 'grad_x', 'grad_attn_norm', 'grad_w_in', 'grad_b_forget', 'grad_b_gate', 'grad_w_proj_attn', 'grad_pool_w', 'grad_pool_scale', 'grad_w_proj_pool', 'grad_conv_w', 'grad_w_proj_conv', 'grad_w_out', 'grad_ffn_norm', 'grad_w_gate_up', 'grad_w_down', 'grad_final_norm', 'delta_attn_norm', 'delta_w_in', 'delta_b_forget', 'delta_b_gate', 'delta_w_proj_attn', 'delta_pool_w', 'delta_pool_scale', 'delta_w_proj_pool', 'delta_conv_w', 'delta_w_proj_conv', 'delta_w_out', 'delta_ffn_norm', 'delta_w_gate_up', 'delta_w_down', 'delta_final_norm', 'new_m_attn_norm', 'new_m_w_in', 'new_m_b_forget', 'new_m_b_gate', 'new_m_w_proj_attn', 'new_m_pool_w', 'new_m_pool_scale', 'new_m_w_proj_pool', 'new_m_conv_w', 'new_m_w_proj_conv', 'new_m_w_out', 'new_m_ffn_norm', 'new_m_w_gate_up', 'new_m_w_down', 'new_m_final_norm', 'new_v_attn_norm', 'new_v_w_in', 'new_v_b_forget', 'new_v_b_gate', 'new_v_w_proj_attn', 'new_v_pool_w', 'new_v_pool_scale', 'new_v_w_proj_pool', 'new_v_conv_w', 'new_v_w_proj_conv', 'new_v_w_out', 'new_v_ffn_norm', 'new_v_w_gate_up', 'new_v_w_down', 'new_v_final_norm']
TWIN_LEAF_KINDS = {'loss': 'loss', 'grad_x': 'grad_x', 'grad_attn_norm': 'grad_w', 'grad_w_in': 'grad_w', 'grad_b_forget': 'grad_w', 'grad_b_gate': 'grad_w', 'grad_w_proj_attn': 'grad_w', 'grad_pool_w': 'grad_w', 'grad_pool_scale': 'grad_w', 'grad_w_proj_pool': 'grad_w', 'grad_conv_w': 'grad_w', 'grad_w_proj_conv': 'grad_w', 'grad_w_out': 'grad_w', 'grad_ffn_norm': 'grad_w', 'grad_w_gate_up': 'grad_w', 'grad_w_down': 'grad_w', 'grad_final_norm': 'grad_w', 'delta_attn_norm': 'delta_w', 'delta_w_in': 'delta_w', 'delta_b_forget': 'delta_w', 'delta_b_gate': 'delta_w', 'delta_w_proj_attn': 'delta_w', 'delta_pool_w': 'delta_w', 'delta_pool_scale': 'delta_w', 'delta_w_proj_pool': 'delta_w', 'delta_conv_w': 'delta_w', 'delta_w_proj_conv': 'delta_w', 'delta_w_out': 'delta_w', 'delta_ffn_norm': 'delta_w', 'delta_w_gate_up': 'delta_w', 'delta_w_down': 'delta_w', 'delta_final_norm': 'delta_w', 'new_m_attn_norm': 'new_m', 'new_m_w_in': 'new_m', 'new_m_b_forget': 'new_m', 'new_m_b_gate': 'new_m', 'new_m_w_proj_attn': 'new_m', 'new_m_pool_w': 'new_m', 'new_m_pool_scale': 'new_m', 'new_m_w_proj_pool': 'new_m', 'new_m_conv_w': 'new_m', 'new_m_w_proj_conv': 'new_m', 'new_m_w_out': 'new_m', 'new_m_ffn_norm': 'new_m', 'new_m_w_gate_up': 'new_m', 'new_m_w_down': 'new_m', 'new_m_final_norm': 'new_m', 'new_v_attn_norm': 'new_v', 'new_v_w_in': 'new_v', 'new_v_b_forget': 'new_v', 'new_v_b_gate': 'new_v', 'new_v_w_proj_attn': 'new_v', 'new_v_pool_w': 'new_v', 'new_v_pool_scale': 'new_v', 'new_v_w_proj_pool': 'new_v', 'new_v_conv_w': 'new_v', 'new_v_w_proj_conv': 'new_v', 'new_v_w_out': 'new_v', 'new_v_ffn_norm': 'new_v', 'new_v_w_gate_up': 'new_v', 'new_v_w_down': 'new_v', 'new_v_final_norm': 'new_v'}


def _forward(args):
    return _fwd_reference(*[args[k] for k in FWD_PARAMS])


def _output_shape():
    out = _jax.eval_shape(lambda: _forward(_fwd_setup_inputs(0)))
    return out.shape, out.dtype

N_MICROBATCH = 1
ADAM_LR = 0.001
ADAM_B1 = 0.9
ADAM_B2 = 0.999
ADAM_EPS = 1e-08
ADAM_WD = 0.01
ADAM_STEP = 10
PER_EXAMPLE_BATCH_AXIS = {'x': 0, 'loss_target': 0}
SHARED_INPUTS = []
_WEIGHT_DTYPES = {'attn_norm': _jnp.float32, 'w_in': _jnp.float32, 'b_forget': _jnp.float32, 'b_gate': _jnp.float32, 'w_proj_attn': _jnp.float32, 'pool_w': _jnp.float32, 'pool_scale': _jnp.float32, 'w_proj_pool': _jnp.float32, 'conv_w': _jnp.float32, 'w_proj_conv': _jnp.float32, 'w_out': _jnp.float32, 'ffn_norm': _jnp.float32, 'w_gate_up': _jnp.float32, 'w_down': _jnp.float32, 'final_norm': _jnp.float32}
MOMENT_SCALE = {'attn_norm': 1.814104e-01, 'w_in': 7.086911e-02, 'b_forget': 1.686086e-01, 'b_gate': 2.666217e-02, 'w_proj_attn': 3.331999e-02, 'pool_w': 1.072783e-01, 'pool_scale': 1.120231e-01, 'w_proj_pool': 7.645414e-02, 'conv_w': 1.229584e-01, 'w_proj_conv': 8.652017e-02, 'w_out': 1.194430e-01, 'ffn_norm': 1.168939e-01, 'w_gate_up': 4.876311e-02, 'w_down': 7.980238e-02, 'final_norm': 3.196489e+01}


def _to_microbatches(a, axis):
    t = _jnp.moveaxis(a, axis, 0)
    t = t.reshape((N_MICROBATCH, t.shape[0] // N_MICROBATCH) + t.shape[1:])
    return _jnp.moveaxis(t, 1, axis + 1)


def setup_inputs(seed: int = 0) -> dict:
    inp = _fwd_setup_inputs(seed)
    key = _jax.random.fold_in(_jax.random.key(seed), 7919)
    shape, _ = _output_shape()
    out = dict(inp)
    out["loss_target"] = _jax.random.normal(_jax.random.fold_in(key, 0), shape, _jnp.float32)
    for i, name in enumerate(TWIN_WEIGHTS):
        w = inp[name].astype(_jnp.float32)
        if MOMENT_SCALE is None:
            s = _jnp.sqrt(_jnp.mean(_jnp.square(w)) + 1e-30)
        else:
            s = MOMENT_SCALE[name]
        km, kv = _jax.random.split(_jax.random.fold_in(key, i + 1))
        out[name] = w
        out["m_" + name] = s * _jax.random.normal(km, w.shape, _jnp.float32)
        out["v_" + name] = (s * s) * _jax.random.uniform(kv, w.shape, _jnp.float32, 0.5, 1.5)
    if N_MICROBATCH > 1:
        for name, axis in PER_EXAMPLE_BATCH_AXIS.items():
            out[name] = _to_microbatches(out[name], axis)
    return {'x': out['x'], 'attn_norm': out['attn_norm'], 'w_in': out['w_in'], 'b_forget': out['b_forget'], 'b_gate': out['b_gate'], 'w_proj_attn': out['w_proj_attn'], 'pool_w': out['pool_w'], 'pool_scale': out['pool_scale'], 'w_proj_pool': out['w_proj_pool'], 'conv_w': out['conv_w'], 'w_proj_conv': out['w_proj_conv'], 'w_out': out['w_out'], 'ffn_norm': out['ffn_norm'], 'w_gate_up': out['w_gate_up'], 'w_down': out['w_down'], 'final_norm': out['final_norm'], 'loss_target': out['loss_target'], 'm_attn_norm': out['m_attn_norm'], 'm_w_in': out['m_w_in'], 'm_b_forget': out['m_b_forget'], 'm_b_gate': out['m_b_gate'], 'm_w_proj_attn': out['m_w_proj_attn'], 'm_pool_w': out['m_pool_w'], 'm_pool_scale': out['m_pool_scale'], 'm_w_proj_pool': out['m_w_proj_pool'], 'm_conv_w': out['m_conv_w'], 'm_w_proj_conv': out['m_w_proj_conv'], 'm_w_out': out['m_w_out'], 'm_ffn_norm': out['m_ffn_norm'], 'm_w_gate_up': out['m_w_gate_up'], 'm_w_down': out['m_w_down'], 'm_final_norm': out['m_final_norm'], 'v_attn_norm': out['v_attn_norm'], 'v_w_in': out['v_w_in'], 'v_b_forget': out['v_b_forget'], 'v_b_gate': out['v_b_gate'], 'v_w_proj_attn': out['v_w_proj_attn'], 'v_pool_w': out['v_pool_w'], 'v_pool_scale': out['v_pool_scale'], 'v_w_proj_pool': out['v_w_proj_pool'], 'v_conv_w': out['v_conv_w'], 'v_w_proj_conv': out['v_w_proj_conv'], 'v_w_out': out['v_w_out'], 'v_ffn_norm': out['v_ffn_norm'], 'v_w_gate_up': out['v_w_gate_up'], 'v_w_down': out['v_w_down'], 'v_final_norm': out['v_final_norm']}


def _loss(weights, diff, rest, loss_target):
    with _jax.named_scope("forward"):
        args = {**rest, TWIN_DIFF_INPUT: diff, **{k: w.astype(_WEIGHT_DTYPES[k]) for k, w in weights.items()}}
        y = _forward(args)
    with _jax.named_scope("loss_head"):
        err = _jnp.square(y.astype(_jnp.float32) - loss_target)
        return 0.5 * _jnp.sum(_jnp.mean(err, axis=-1)) if err.ndim else 0.5 * err


def _adamw(w, g, m, v):
    m = ADAM_B1 * m + (1.0 - ADAM_B1) * g
    v = ADAM_B2 * v + (1.0 - ADAM_B2) * _jnp.square(g)
    m_hat = m / (1.0 - ADAM_B1 ** ADAM_STEP)
    v_hat = v / (1.0 - ADAM_B2 ** ADAM_STEP)
    delta = -ADAM_LR * (m_hat / (_jnp.sqrt(v_hat) + ADAM_EPS) + ADAM_WD * w)
    return delta, m, v


def reference(x, attn_norm, w_in, b_forget, b_gate, w_proj_attn, pool_w, pool_scale, w_proj_pool, conv_w, w_proj_conv, w_out, ffn_norm, w_gate_up, w_down, final_norm, loss_target, m_attn_norm, m_w_in, m_b_forget, m_b_gate, m_w_proj_attn, m_pool_w, m_pool_scale, m_w_proj_pool, m_conv_w, m_w_proj_conv, m_w_out, m_ffn_norm, m_w_gate_up, m_w_down, m_final_norm, v_attn_norm, v_w_in, v_b_forget, v_b_gate, v_w_proj_attn, v_pool_w, v_pool_scale, v_w_proj_pool, v_conv_w, v_w_proj_conv, v_w_out, v_ffn_norm, v_w_gate_up, v_w_down, v_final_norm):
    given = dict(x=x, attn_norm=attn_norm, w_in=w_in, b_forget=b_forget, b_gate=b_gate, w_proj_attn=w_proj_attn, pool_w=pool_w, pool_scale=pool_scale, w_proj_pool=w_proj_pool, conv_w=conv_w, w_proj_conv=w_proj_conv, w_out=w_out, ffn_norm=ffn_norm, w_gate_up=w_gate_up, w_down=w_down, final_norm=final_norm, loss_target=loss_target, m_attn_norm=m_attn_norm, m_w_in=m_w_in, m_b_forget=m_b_forget, m_b_gate=m_b_gate, m_w_proj_attn=m_w_proj_attn, m_pool_w=m_pool_w, m_pool_scale=m_pool_scale, m_w_proj_pool=m_w_proj_pool, m_conv_w=m_conv_w, m_w_proj_conv=m_w_proj_conv, m_w_out=m_w_out, m_ffn_norm=m_ffn_norm, m_w_gate_up=m_w_gate_up, m_w_down=m_w_down, m_final_norm=m_final_norm, v_attn_norm=v_attn_norm, v_w_in=v_w_in, v_b_forget=v_b_forget, v_b_gate=v_b_gate, v_w_proj_attn=v_w_proj_attn, v_pool_w=v_pool_w, v_pool_scale=v_pool_scale, v_w_proj_pool=v_w_proj_pool, v_conv_w=v_conv_w, v_w_proj_conv=v_w_proj_conv, v_w_out=v_w_out, v_ffn_norm=v_ffn_norm, v_w_gate_up=v_w_gate_up, v_w_down=v_w_down, v_final_norm=v_final_norm)
    weights = {n: given[n] for n in TWIN_WEIGHTS}
    shared = {n: given[n] for n in SHARED_INPUTS}
    per_example = {n: given[n] for n in ['x']}
    grad_fn = _jax.value_and_grad(_loss, argnums=(0, 1))

    def one_microbatch(ex, loss_target):
        ex = dict(ex)
        diff = ex.pop(TWIN_DIFF_INPUT)
        return grad_fn(weights, diff, {**shared, **ex}, loss_target)

    if N_MICROBATCH == 1:
        loss, (grad_w, grad_x) = one_microbatch(per_example, given["loss_target"])
    else:
        def body(carry, xs):
            loss_sum, grad_sum = carry
            l_k, (gw_k, gx_k) = one_microbatch(xs[0], xs[1])
            with _jax.named_scope("update"):
                return (loss_sum + l_k, _jax.tree.map(_jnp.add, grad_sum, gw_k)), gx_k

        init = (_jnp.zeros((), _jnp.float32), _jax.tree.map(_jnp.zeros_like, weights))
        (loss, grad_w), grad_x = _jax.lax.scan(body, init, (per_example, given["loss_target"]))
    with _jax.named_scope("update"):
        delta_w, new_m, new_v = {}, {}, {}
        for n in TWIN_WEIGHTS:
            delta_w[n], new_m[n], new_v[n] = _adamw(weights[n], grad_w[n], given["m_" + n], given["v_" + n])
    return (loss, grad_x, *[grad_w[n] for n in TWIN_WEIGHTS], *[delta_w[n] for n in TWIN_WEIGHTS],
            *[new_m[n] for n in TWIN_WEIGHTS], *[new_v[n] for n in TWIN_WEIGHTS])
```

```python
import functools

import numpy as np
import jax
import jax.numpy as jnp
from jax import lax
from jax.experimental import pallas as pl
from jax.experimental.pallas import tpu as pltpu

F32 = jnp.float32
BF16 = jnp.bfloat16

N_DEV = 8
D_MODEL = 1024
DEPTH = 2
N_HEADS = 8
HEAD_DIM = 64
BRANCH_W = 512
POOL_WINDOWS = (2, 4, 8, 16)
POOL_GD = 128
CONV_K = 3
FFN_HIDDEN = 2816
GATE_W = 3 * D_MODEL
IN_COLS = 6664
MAIN_COLS = GATE_W + 7 * BRANCH_W
RMS_EPS = 1e-6
NEG_INF = -1e30

ADAM_LR = 0.001
ADAM_B1 = 0.9
ADAM_B2 = 0.999
ADAM_EPS = 1e-08
ADAM_WD = 0.01
ADAM_STEP = 10

LANES = 128
VMEM_LIMIT = 56 * 1024 * 1024
ATT_BLK = 256
CUM_BLK = 256

OFF_G, OFF_Q, OFF_K, OFF_V, OFF_U, OFF_CV, OFF_CB, OFF_CC = (
    0, 3072, 3584, 4096, 4608, 5120, 5632, 6144)


def _cp(sem=None):
    return pltpu.CompilerParams(dimension_semantics=sem, vmem_limit_bytes=VMEM_LIMIT)


def _sigmoid(z):
    return 1.0 / (1.0 + jnp.exp(-z))


def _matmul(a, b, *, mode, out_dtype, name, tm=512, tn=512, tk=512, residual=None):
    if mode == "nn":
        (M, K), N = a.shape, b.shape[1]
    elif mode == "nt":
        (M, K), N = a.shape, b.shape[0]
    else:
        (K, M), N = a.shape, b.shape[1]
    tm, tn, tk = min(tm, M), min(tn, N), min(tk, K)
    assert M % tm == 0 and N % tn == 0 and K % tk == 0, (name, M, N, K, tm, tn, tk)
    nk = K // tk
    if mode == "nn":
        a_spec = pl.BlockSpec((tm, tk), lambda i, j, k: (i, k))
        b_spec = pl.BlockSpec((tk, tn), lambda i, j, k: (k, j))
        dims = (((1,), (0,)), ((), ()))
    elif mode == "nt":
        a_spec = pl.BlockSpec((tm, tk), lambda i, j, k: (i, k))
        b_spec = pl.BlockSpec((tn, tk), lambda i, j, k: (j, k))
        dims = (((1,), (1,)), ((), ()))
    else:
        a_spec = pl.BlockSpec((tk, tm), lambda i, j, k: (k, i))
        b_spec = pl.BlockSpec((tk, tn), lambda i, j, k: (k, j))
        dims = (((0,), (0,)), ((), ()))
    o_spec = pl.BlockSpec((tm, tn), lambda i, j, k: (i, j))
    has_res = residual is not None

    def body(*refs):
        if has_res:
            a_ref, b_ref, r_ref, o_ref, acc_ref = refs
        else:
            a_ref, b_ref, o_ref, acc_ref = refs
        k = pl.program_id(2)

        @pl.when(k == 0)
        def _():
            acc_ref[...] = jnp.zeros_like(acc_ref)

        acc_ref[...] += lax.dot_general(a_ref[...].astype(BF16), b_ref[...].astype(BF16), dims,
                                        preferred_element_type=F32)

        @pl.when(k == nk - 1)
        def _():
            acc = acc_ref[...]
            if has_res:
                acc = acc + r_ref[...].astype(F32)
            o_ref[...] = acc.astype(out_dtype)

    in_specs = [a_spec, b_spec] + ([o_spec] if has_res else [])
    args = (a, b) + ((residual,) if has_res else ())
    return pl.pallas_call(
        body, name=name, grid=(M // tm, N // tn, nk), in_specs=in_specs, out_specs=o_spec,
        out_shape=jax.ShapeDtypeStruct((M, N), out_dtype),
        scratch_shapes=[pltpu.VMEM((tm, tn), F32)],
        compiler_params=_cp(("parallel", "parallel", "arbitrary")),
    )(*args)


def _rms_fwd(x, g, name):
    T, Dm = x.shape
    tm = min(512, T)

    def body(x_ref, g_ref, h_ref):
        xf = x_ref[...]
        r = lax.rsqrt(jnp.mean(xf * xf, axis=-1, keepdims=True) + RMS_EPS)
        h_ref[...] = ((xf * r) * g_ref[...]).astype(BF16)

    return pl.pallas_call(
        body, name=name, grid=(T // tm,),
        in_specs=[pl.BlockSpec((tm, Dm), lambda i: (i, 0)), pl.BlockSpec((1, Dm), lambda i: (0, 0))],
        out_specs=pl.BlockSpec((tm, Dm), lambda i: (i, 0)),
        out_shape=jax.ShapeDtypeStruct((T, Dm), BF16),
        compiler_params=_cp(("parallel",)),
    )(x, g.reshape(1, Dm))


def _rms_bwd(x, g, dh, dres, name):
    T, Dm = x.shape
    tm = min(512, T)

    def body(x_ref, g_ref, dh_ref, dres_ref, dx_ref, dxb_ref, dg_ref):
        i = pl.program_id(0)
        xf = x_ref[...]
        r = lax.rsqrt(jnp.mean(xf * xf, axis=-1, keepdims=True) + RMS_EPS)
        xn = xf * r
        dhf = dh_ref[...].astype(F32)
        dxn = dhf * g_ref[...]
        c = jnp.mean(dxn * xn, axis=-1, keepdims=True)
        dx = dres_ref[...] + r * (dxn - xn * c)
        dx_ref[...] = dx
        dxb_ref[...] = dx.astype(BF16)
        part = jnp.sum(dhf * xn, axis=0, keepdims=True)

        @pl.when(i == 0)
        def _():
            dg_ref[...] = part

        @pl.when(i > 0)
        def _():
            dg_ref[...] += part

    row = pl.BlockSpec((tm, Dm), lambda i: (i, 0))
    vec = pl.BlockSpec((1, Dm), lambda i: (0, 0))
    return pl.pallas_call(
        body, name=name, grid=(T // tm,), in_specs=[row, vec, row, row], out_specs=[row, row, vec],
        out_shape=[jax.ShapeDtypeStruct((T, Dm), F32), jax.ShapeDtypeStruct((T, Dm), BF16),
                   jax.ShapeDtypeStruct((1, Dm), F32)],
        compiler_params=_cp(("arbitrary",)),
    )(x, g.reshape(1, Dm), dh, dres)


def _loss_head(x, g, target, name):
    T, Dm = x.shape
    tm = min(512, T)

    def body(x_ref, g_ref, t_ref, loss_ref, dx_ref, dxb_ref, dg_ref):
        i = pl.program_id(0)
        xf = x_ref[...]
        gv = g_ref[...]
        r = lax.rsqrt(jnp.mean(xf * xf, axis=-1, keepdims=True) + RMS_EPS)
        xn = xf * r
        diff = xn * gv - t_ref[...]
        per_tok = jnp.mean(diff * diff, axis=-1, keepdims=True)
        lpart = 0.5 * jnp.sum(per_tok, axis=0, keepdims=True) + jnp.zeros((1, LANES), F32)
        dy = diff * (1.0 / Dm)
        dxn = dy * gv
        c = jnp.mean(dxn * xn, axis=-1, keepdims=True)
        dx = r * (dxn - xn * c)
        dx_ref[...] = dx
        dxb_ref[...] = dx.astype(BF16)
        part = jnp.sum(dy * xn, axis=0, keepdims=True)

        @pl.when(i == 0)
        def _():
            dg_ref[...] = part
            loss_ref[...] = lpart

        @pl.when(i > 0)
        def _():
            dg_ref[...] += part
            loss_ref[...] += lpart

    row = pl.BlockSpec((tm, Dm), lambda i: (i, 0))
    vec = pl.BlockSpec((1, Dm), lambda i: (0, 0))
    lsp = pl.BlockSpec((1, LANES), lambda i: (0, 0))
    return pl.pallas_call(
        body, name=name, grid=(T // tm,), in_specs=[row, vec, row], out_specs=[lsp, row, row, vec],
        out_shape=[jax.ShapeDtypeStruct((1, LANES), F32), jax.ShapeDtypeStruct((T, Dm), F32),
                   jax.ShapeDtypeStruct((T, Dm), BF16), jax.ShapeDtypeStruct((1, Dm), F32)],
        compiler_params=_cp(("arbitrary",)),
    )(x, g.reshape(1, Dm), target)


def _split_bf16(v):
    hi = v.astype(BF16)
    r1 = v - hi.astype(F32)
    mid = r1.astype(BF16)
    lo = (r1 - mid.astype(F32)).astype(BF16)
    return hi, mid, lo


def _tri_dot(tri, v):
    hi, mid, lo = _split_bf16(v)
    dot = functools.partial(jnp.dot, preferred_element_type=F32)
    return dot(tri, hi) + dot(tri, mid) + dot(tri, lo)


def _log_sigmoid(z):
    return jnp.minimum(z, 0.0) - jnp.log(1.0 + jnp.exp(-jnp.abs(z)))


def _fox_cumsum_fwd(f, bf, n_seq, name):
    T = f.shape[0]
    S = T // n_seq
    c = min(CUM_BLK, S)

    def body(f_ref, b_ref, out_ref):
        ri = lax.broadcasted_iota(jnp.int32, (c, c), 0)
        ci = lax.broadcasted_iota(jnp.int32, (c, c), 1)
        tri = (ri >= ci).astype(BF16)
        carry = jnp.zeros((1, LANES), F32)
        for j in range(S // c):
            lf = _log_sigmoid(f_ref[j * c:(j + 1) * c, :] + b_ref[...])
            out_ref[j * c:(j + 1) * c, :] = _tri_dot(tri, lf) + carry
            carry = carry + jnp.sum(lf, axis=0, keepdims=True)

    blk = pl.BlockSpec((S, LANES), lambda b: (b, 0))
    return pl.pallas_call(
        body, name=name, grid=(n_seq,), in_specs=[blk, pl.BlockSpec((1, LANES), lambda b: (0, 0))],
        out_specs=blk, out_shape=jax.ShapeDtypeStruct((T, LANES), F32),
        compiler_params=_cp(("parallel",)),
    )(f, bf)


def _fox_cumsum_bwd(f, bf, dF, n_seq, name):
    T = f.shape[0]
    S = T // n_seq
    c = min(CUM_BLK, S)

    def body(f_ref, b_ref, dF_ref, df_ref, db_ref):
        b = pl.program_id(0)
        ri = lax.broadcasted_iota(jnp.int32, (c, c), 0)
        ci = lax.broadcasted_iota(jnp.int32, (c, c), 1)
        tri = (ri <= ci).astype(BF16)
        carry = jnp.zeros((1, LANES), F32)
        dbp = jnp.zeros((1, LANES), F32)
        for j in reversed(range(S // c)):
            dFc = dF_ref[j * c:(j + 1) * c, :]
            dlf = _tri_dot(tri, dFc) + carry
            carry = carry + jnp.sum(dFc, axis=0, keepdims=True)
            z = f_ref[j * c:(j + 1) * c, :] + b_ref[...]
            dz = dlf * _sigmoid(-z)
            df_ref[j * c:(j + 1) * c, :] = dz.astype(BF16)
            dbp = dbp + jnp.sum(dz, axis=0, keepdims=True)

        @pl.when(b == 0)
        def _():
            db_ref[...] = dbp

        @pl.when(b > 0)
        def _():
            db_ref[...] += dbp

    blk = pl.BlockSpec((S, LANES), lambda b: (b, 0))
    vec = pl.BlockSpec((1, LANES), lambda b: (0, 0))
    return pl.pallas_call(
        body, name=name, grid=(n_seq,), in_specs=[blk, vec, blk], out_specs=[blk, vec],
        out_shape=[jax.ShapeDtypeStruct((T, LANES), BF16), jax.ShapeDtypeStruct((1, LANES), F32)],
        compiler_params=_cp(("arbitrary",)),
    )(f, bf, dF)


def _pair_masks():
    lane = lax.broadcasted_iota(jnp.int32, (1, LANES), 1)
    lo = lane < HEAD_DIM
    return lo, jnp.logical_not(lo)


def _attn_logits(q, k, fq, fk, sel, mask, scale):
    qm = jnp.where(sel, q, jnp.zeros_like(q))
    s = lax.dot_general(qm, k, (((1,), (1,)), ((), ())), preferred_element_type=F32) * scale
    s = s + fq - fk
    return jnp.where(mask, s, NEG_INF)


def _causal_mask(qi, ki, blk):
    row = qi * blk + lax.broadcasted_iota(jnp.int32, (blk, blk), 0)
    col = ki * blk + lax.broadcasted_iota(jnp.int32, (blk, blk), 1)
    return col <= row


def _attn_fwd(proj, Fq, Fk, n_seq, name):
    T = proj.shape[0]
    S = T // n_seq
    blk = min(ATT_BLK, S)
    nb = S // blk
    scale = HEAD_DIM ** -0.5
    qc, kc, vc = OFF_Q // LANES, OFF_K // LANES, OFF_V // LANES

    def body(q_ref, k_ref, v_ref, fq_ref, fk_ref, o_ref, o32_ref, lse_ref, m_s, l_s, acc_s):
        qi, ki = pl.program_id(2), pl.program_id(3)

        @pl.when(ki == 0)
        def _():
            m_s[...] = jnp.full_like(m_s, NEG_INF)
            l_s[...] = jnp.zeros_like(l_s)
            acc_s[...] = jnp.zeros_like(acc_s)

        @pl.when(ki <= qi)
        def _():
            q, k, v = q_ref[...], k_ref[...], v_ref[...]
            mask = _causal_mask(qi, ki, blk)
            for hh, sel in enumerate(_pair_masks()):
                s = _attn_logits(q, k, fq_ref[hh], fk_ref[hh], sel, mask, scale)
                m_prev = m_s[hh]
                m_new = jnp.maximum(m_prev, jnp.max(s, axis=-1, keepdims=True))
                alpha = jnp.exp(m_prev - m_new)
                p = jnp.exp(s - m_new)
                l_s[hh] = alpha * l_s[hh] + jnp.sum(p, axis=-1, keepdims=True)
                p_hi = p.astype(BF16)
                p_lo = (p - p_hi.astype(F32)).astype(BF16)
                pv = jnp.dot(p_hi, v, preferred_element_type=F32) + jnp.dot(p_lo, v, preferred_element_type=F32)
                acc_s[hh] = alpha * acc_s[hh] + pv
                m_s[hh] = m_new

        @pl.when(ki == qi)
        def _():
            lo, _ = _pair_masks()
            o = jnp.where(lo, acc_s[0] / l_s[0], acc_s[1] / l_s[1])
            o_ref[...] = o.astype(BF16)
            o32_ref[...] = o
            lse_ref[0] = m_s[0] + jnp.log(l_s[0])
            lse_ref[1] = m_s[1] + jnp.log(l_s[1])

    grid = (n_seq, N_HEADS // 2, nb, nb)
    return pl.pallas_call(
        body, name=name, grid=grid,
        in_specs=[
            pl.BlockSpec((blk, LANES), lambda b, j, qi, ki: (b * nb + qi, qc + j)),
            pl.BlockSpec((blk, LANES), lambda b, j, qi, ki: (b * nb + jnp.minimum(ki, qi), kc + j)),
            pl.BlockSpec((blk, LANES), lambda b, j, qi, ki: (b * nb + jnp.minimum(ki, qi), vc + j)),
            pl.BlockSpec((2, blk, 1), lambda b, j, qi, ki: (j, b * nb + qi, 0)),
            pl.BlockSpec((2, 1, blk), lambda b, j, qi, ki: (j, 0, b * nb + jnp.minimum(ki, qi))),
        ],
        out_specs=[
            pl.BlockSpec((blk, LANES), lambda b, j, qi, ki: (b * nb + qi, j)),
            pl.BlockSpec((blk, LANES), lambda b, j, qi, ki: (b * nb + qi, j)),
            pl.BlockSpec((2, blk, 1), lambda b, j, qi, ki: (j, b * nb + qi, 0)),
        ],
        out_shape=[jax.ShapeDtypeStruct((T, BRANCH_W), BF16), jax.ShapeDtypeStruct((T, BRANCH_W), F32),
                   jax.ShapeDtypeStruct((N_HEADS, T, 1), F32)],
        scratch_shapes=[pltpu.VMEM((2, blk, 1), F32), pltpu.VMEM((2, blk, 1), F32),
                        pltpu.VMEM((2, blk, LANES), F32)],
        compiler_params=_cp(("parallel", "parallel", "parallel", "arbitrary")),
    )(proj, proj, proj, Fq, Fk)


def _attn_delta(do, o, name):
    T = do.shape[0]
    tm = min(512, T)

    def body(do_ref, o_ref, d_ref):
        prod = do_ref[...].astype(F32) * o_ref[...].astype(F32)
        lo, hi = _pair_masks()
        for j in range(N_HEADS // 2):
            pj = prod[:, j * LANES:(j + 1) * LANES]
            d_ref[2 * j] = jnp.sum(jnp.where(lo, pj, 0.0), axis=-1, keepdims=True)
            d_ref[2 * j + 1] = jnp.sum(jnp.where(hi, pj, 0.0), axis=-1, keepdims=True)

    row = pl.BlockSpec((tm, BRANCH_W), lambda i: (i, 0))
    return pl.pallas_call(
        body, name=name, grid=(T // tm,), in_specs=[row, row],
        out_specs=pl.BlockSpec((N_HEADS, tm, 1), lambda i: (0, i, 0)),
        out_shape=jax.ShapeDtypeStruct((N_HEADS, T, 1), F32),
        compiler_params=_cp(("parallel",)),
    )(do, o)


def _attn_bwd_dq(proj, do, lse, delta, Fq, Fk, n_seq, name):
    T = proj.shape[0]
    S = T // n_seq
    blk = min(ATT_BLK, S)
    nb = S // blk
    scale = HEAD_DIM ** -0.5
    qc, kc, vc = OFF_Q // LANES, OFF_K // LANES, OFF_V // LANES

    def body(q_ref, k_ref, v_ref, do_ref, lse_ref, dl_ref, fq_ref, fk_ref, dq_ref, acc_s):
        qi, ki = pl.program_id(2), pl.program_id(3)

        @pl.when(ki == 0)
        def _():
            acc_s[...] = jnp.zeros_like(acc_s)

        @pl.when(ki <= qi)
        def _():
            q, k, v, do_ = q_ref[...], k_ref[...], v_ref[...], do_ref[...]
            mask = _causal_mask(qi, ki, blk)
            for hh, sel in enumerate(_pair_masks()):
                s = _attn_logits(q, k, fq_ref[hh], fk_ref[hh], sel, mask, scale)
                p = jnp.exp(s - lse_ref[hh])
                dom = jnp.where(sel, do_, jnp.zeros_like(do_))
                dp = lax.dot_general(dom, v, (((1,), (1,)), ((), ())), preferred_element_type=F32)
                ds = p * (dp - dl_ref[hh])
                acc_s[hh] += jnp.dot(ds.astype(BF16), k, preferred_element_type=F32)

        @pl.when(ki == qi)
        def _():
            lo, _ = _pair_masks()
            dq_ref[...] = (jnp.where(lo, acc_s[0], acc_s[1]) * scale).astype(BF16)

    qmap = lambda b, j, qi, ki: (b * nb + qi, j)
    col1 = pl.BlockSpec((2, blk, 1), lambda b, j, qi, ki: (j, b * nb + qi, 0))
    return pl.pallas_call(
        body, name=name, grid=(n_seq, N_HEADS // 2, nb, nb),
        in_specs=[
            pl.BlockSpec((blk, LANES), lambda b, j, qi, ki: (b * nb + qi, qc + j)),
            pl.BlockSpec((blk, LANES), lambda b, j, qi, ki: (b * nb + jnp.minimum(ki, qi), kc + j)),
            pl.BlockSpec((blk, LANES), lambda b, j, qi, ki: (b * nb + jnp.minimum(ki, qi), vc + j)),
            pl.BlockSpec((blk, LANES), qmap),
            col1, col1, col1,
            pl.BlockSpec((2, 1, blk), lambda b, j, qi, ki: (j, 0, b * nb + jnp.minimum(ki, qi))),
        ],
        out_specs=pl.BlockSpec((blk, LANES), qmap),
        out_shape=jax.ShapeDtypeStruct((T, BRANCH_W), BF16),
        scratch_shapes=[pltpu.VMEM((2, blk, LANES), F32)],
        compiler_params=_cp(("parallel", "parallel", "parallel", "arbitrary")),
    )(proj, proj, proj, do, lse, delta, Fq, Fk)


def _attn_bwd_dkv(proj, do, lse, delta, Fq, Fk, n_seq, name):
    T = proj.shape[0]
    S = T // n_seq
    blk = min(ATT_BLK, S)
    nb = S // blk
    scale = HEAD_DIM ** -0.5
    qc, kc, vc = OFF_Q // LANES, OFF_K // LANES, OFF_V // LANES
    tdot = functools.partial(lax.dot_general, dimension_numbers=(((0,), (0,)), ((), ())),
                             preferred_element_type=F32)

    def body(q_ref, k_ref, v_ref, do_ref, lse_ref, dl_ref, fq_ref, fk_ref, dk_ref, dv_ref, dfk_ref,
             dk_s, dv_s, df_s):
        ki, qi = pl.program_id(2), pl.program_id(3)

        @pl.when(qi == 0)
        def _():
            dk_s[...] = jnp.zeros_like(dk_s)
            dv_s[...] = jnp.zeros_like(dv_s)
            df_s[...] = jnp.zeros_like(df_s)

        @pl.when(qi >= ki)
        def _():
            q, k, v, do_ = q_ref[...], k_ref[...], v_ref[...], do_ref[...]
            mask = _causal_mask(qi, ki, blk)
            for hh, sel in enumerate(_pair_masks()):
                s = _attn_logits(q, k, fq_ref[hh], fk_ref[hh], sel, mask, scale)
                p = jnp.exp(s - lse_ref[hh])
                dv_s[hh] += tdot(p.astype(BF16), do_)
                dom = jnp.where(sel, do_, jnp.zeros_like(do_))
                dp = lax.dot_general(dom, v, (((1,), (1,)), ((), ())), preferred_element_type=F32)
                ds = p * (dp - dl_ref[hh])
                dk_s[hh] += tdot(ds.astype(BF16), q)
                df_s[hh] -= jnp.sum(ds, axis=0, keepdims=True)

        @pl.when(qi == nb - 1)
        def _():
            lo, _ = _pair_masks()
            dk_ref[...] = (jnp.where(lo, dk_s[0], dk_s[1]) * scale).astype(BF16)
            dv_ref[...] = jnp.where(lo, dv_s[0], dv_s[1]).astype(BF16)
            dfk_ref[...] = df_s[...]

    kmap = lambda b, j, ki, qi: (b * nb + ki, j)
    col1 = pl.BlockSpec((2, blk, 1), lambda b, j, ki, qi: (j, b * nb + jnp.maximum(qi, ki), 0))
    rowk = pl.BlockSpec((2, 1, blk), lambda b, j, ki, qi: (j, 0, b * nb + ki))
    return pl.pallas_call(
        body, name=name, grid=(n_seq, N_HEADS // 2, nb, nb),
        in_specs=[
            pl.BlockSpec((blk, LANES), lambda b, j, ki, qi: (b * nb + jnp.maximum(qi, ki), qc + j)),
            pl.BlockSpec((blk, LANES), lambda b, j, ki, qi: (b * nb + ki, kc + j)),
            pl.BlockSpec((blk, LANES), lambda b, j, ki, qi: (b * nb + ki, vc + j)),
            pl.BlockSpec((blk, LANES), lambda b, j, ki, qi: (b * nb + jnp.maximum(qi, ki), j)),
            col1, col1, col1, rowk,
        ],
        out_specs=[pl.BlockSpec((blk, LANES), kmap), pl.BlockSpec((blk, LANES), kmap), rowk],
        out_shape=[jax.ShapeDtypeStruct((T, BRANCH_W), BF16), jax.ShapeDtypeStruct((T, BRANCH_W), BF16),
                   jax.ShapeDtypeStruct((N_HEADS, 1, T), F32)],
        scratch_shapes=[pltpu.VMEM((2, blk, LANES), F32), pltpu.VMEM((2, blk, LANES), F32),
                        pltpu.VMEM((2, 1, blk), F32)],
        compiler_params=_cp(("parallel", "parallel", "parallel", "arbitrary")),
    )(proj, proj, proj, do, lse, delta, Fq, Fk)


def _shift_down(v, k, row):
    return jnp.where(row >= k, pltpu.roll(v, k, 0), 0.0)


def _shift_up(v, k, row, S):
    return jnp.where(row < S - k, pltpu.roll(v, S - k, 0), 0.0)


def _pool_diff(uf, w, row):
    acc, k = uf, 1
    while k < w:
        acc = acc + _shift_down(acc, k, row)
        k *= 2
    n = jnp.minimum(row + 1, w).astype(F32)
    return acc / n - uf


def _pool_fwd(proj, pool_w, pool_scale, n_seq, name):
    T = proj.shape[0]
    S = T // n_seq

    def body(u_ref, w_ref, sc_ref, o_ref):
        g = pl.program_id(1)
        row = lax.broadcasted_iota(jnp.int32, (S, POOL_GD), 0)
        uf = u_ref[...].astype(F32)
        d = _pool_diff(uf, POOL_WINDOWS[0], row)
        for gi in range(1, len(POOL_WINDOWS)):
            d = jnp.where(g == gi, _pool_diff(uf, POOL_WINDOWS[gi], row), d)
        e = jnp.dot(d.astype(BF16), w_ref[0], preferred_element_type=F32)
        o_ref[...] = (e * sc_ref[...]).astype(BF16)

    uc = OFF_U // POOL_GD
    return pl.pallas_call(
        body, name=name, grid=(n_seq, len(POOL_WINDOWS)),
        in_specs=[pl.BlockSpec((S, POOL_GD), lambda b, g: (b, uc + g)),
                  pl.BlockSpec((1, POOL_GD, POOL_GD), lambda b, g: (g, 0, 0)),
                  pl.BlockSpec((1, POOL_GD), lambda b, g: (0, g))],
        out_specs=pl.BlockSpec((S, POOL_GD), lambda b, g: (b, g)),
        out_shape=jax.ShapeDtypeStruct((T, BRANCH_W), BF16),
        compiler_params=_cp(("parallel", "parallel")),
    )(proj, pool_w, pool_scale)


def _pool_bwd(proj, dout, pool_w, pool_scale, n_seq, name):
    T = proj.shape[0]
    S = T // n_seq

    def body(u_ref, do_ref, w_ref, sc_ref, du_ref, dw_ref, dsc_ref):
        g, b = pl.program_id(0), pl.program_id(1)
        row = lax.broadcasted_iota(jnp.int32, (S, POOL_GD), 0)
        uf = u_ref[...].astype(F32)
        d = _pool_diff(uf, POOL_WINDOWS[0], row)
        for gi in range(1, len(POOL_WINDOWS)):
            d = jnp.where(g == gi, _pool_diff(uf, POOL_WINDOWS[gi], row), d)
        db16 = d.astype(BF16)
        w = w_ref[0]
        e = jnp.dot(db16, w, preferred_element_type=F32)
        dof = do_ref[...].astype(F32)
        dsc = jnp.sum(dof * e, axis=0, keepdims=True)
        de = (dof * sc_ref[...]).astype(BF16)
        dd = lax.dot_general(de, w, (((1,), (1,)), ((), ())), preferred_element_type=F32)
        dw = lax.dot_general(db16, de, (((0,), (0,)), ((), ())), preferred_element_type=F32)
        du = jnp.zeros_like(dd)
        for gi, wlen in enumerate(POOL_WINDOWS):
            n = jnp.minimum(row + 1, wlen).astype(F32)
            acc, k = dd / n, 1
            while k < wlen:
                acc = acc + _shift_up(acc, k, row, S)
                k *= 2
            du = jnp.where(g == gi, acc - dd, du)
        du_ref[...] = du.astype(BF16)

        @pl.when(b == 0)
        def _():
            dw_ref[0] = dw
            dsc_ref[...] = dsc

        @pl.when(b > 0)
        def _():
            dw_ref[0] += dw
            dsc_ref[...] += dsc

    uc = OFF_U // POOL_GD
    return pl.pallas_call(
        body, name=name, grid=(len(POOL_WINDOWS), n_seq),
        in_specs=[pl.BlockSpec((S, POOL_GD), lambda g, b: (b, uc + g)),
                  pl.BlockSpec((S, POOL_GD), lambda g, b: (b, g)),
                  pl.BlockSpec((1, POOL_GD, POOL_GD), lambda g, b: (g, 0, 0)),
                  pl.BlockSpec((1, POOL_GD), lambda g, b: (0, g))],
        out_specs=[pl.BlockSpec((S, POOL_GD), lambda g, b: (b, g)),
                   pl.BlockSpec((1, POOL_GD, POOL_GD), lambda g, b: (g, 0, 0)),
                   pl.BlockSpec((1, POOL_GD), lambda g, b: (0, g))],
        out_shape=[jax.ShapeDtypeStruct((T, BRANCH_W), BF16),
                   jax.ShapeDtypeStruct((len(POOL_WINDOWS), POOL_GD, POOL_GD), F32),
                   jax.ShapeDtypeStruct((1, BRANCH_W), F32)],
        compiler_params=_cp(("parallel", "arbitrary")),
    )(proj, dout, pool_w, pool_scale)


def _conv_fwd(proj, conv_w, n_seq, name):
    T = proj.shape[0]
    S = T // n_seq
    nc = BRANCH_W // LANES

    def body(cv_ref, cb_ref, cc_ref, w_ref, o_ref):
        row = lax.broadcasted_iota(jnp.int32, (S, LANES), 0)
        z = cc_ref[...].astype(F32) * cv_ref[...].astype(F32)
        w = w_ref[...]
        y = w[0:1] * _shift_down(z, 2, row) + w[1:2] * _shift_down(z, 1, row) + w[2:3] * z
        o_ref[...] = (cb_ref[...].astype(F32) * y).astype(BF16)

    def col(off):
        return pl.BlockSpec((S, LANES), lambda b, j: (b, off // LANES + j))

    return pl.pallas_call(
        body, name=name, grid=(n_seq, nc),
        in_specs=[col(OFF_CV), col(OFF_CB), col(OFF_CC), pl.BlockSpec((CONV_K, LANES), lambda b, j: (0, j))],
        out_specs=pl.BlockSpec((S, LANES), lambda b, j: (b, j)),
        out_shape=jax.ShapeDtypeStruct((T, BRANCH_W), BF16),
        compiler_params=_cp(("parallel", "parallel")),
    )(proj, proj, proj, conv_w)


def _conv_bwd(proj, dout, conv_w, n_seq, name):
    T = proj.shape[0]
    S = T // n_seq
    nc = BRANCH_W // LANES

    def body(cv_ref, cb_ref, cc_ref, do_ref, w_ref, dcv_ref, dcb_ref, dcc_ref, dw_ref):
        b = pl.program_id(1)
        row = lax.broadcasted_iota(jnp.int32, (S, LANES), 0)
        cv, cb, cc = cv_ref[...].astype(F32), cb_ref[...].astype(F32), cc_ref[...].astype(F32)
        dof = do_ref[...].astype(F32)
        w = w_ref[...]
        z = cc * cv
        z1, z2 = _shift_down(z, 1, row), _shift_down(z, 2, row)
        y = w[0:1] * z2 + w[1:2] * z1 + w[2:3] * z
        dcb_ref[...] = (dof * y).astype(BF16)
        dy = dof * cb
        dz = w[2:3] * dy + w[1:2] * _shift_up(dy, 1, row, S) + w[0:1] * _shift_up(dy, 2, row, S)
        dcc_ref[...] = (dz * cv).astype(BF16)
        dcv_ref[...] = (dz * cc).astype(BF16)
        dws = [jnp.sum(dy * zk, axis=0, keepdims=True) for zk in (z2, z1, z)]

        @pl.when(b == 0)
        def _():
            for kk in range(CONV_K):
                dw_ref[kk:kk + 1, :] = dws[kk]

        @pl.when(b > 0)
        def _():
            for kk in range(CONV_K):
                dw_ref[kk:kk + 1, :] += dws[kk]

    def col(off):
        return pl.BlockSpec((S, LANES), lambda j, b: (b, off // LANES + j))

    out = pl.BlockSpec((S, LANES), lambda j, b: (b, j))
    wsp = pl.BlockSpec((CONV_K, LANES), lambda j, b: (0, j))
    act = jax.ShapeDtypeStruct((T, BRANCH_W), BF16)
    return pl.pallas_call(
        body, name=name, grid=(nc, n_seq),
        in_specs=[col(OFF_CV), col(OFF_CB), col(OFF_CC), out, wsp],
        out_specs=[out, out, out, wsp],
        out_shape=[act, act, act, jax.ShapeDtypeStruct((CONV_K, BRANCH_W), F32)],
        compiler_params=_cp(("parallel", "arbitrary")),
    )(proj, proj, proj, dout, conv_w)


def _mix_fwd(oa, ob, oc, wpa, wpp, wpc, proj, b_gate, name):
    T = oa.shape[0]
    tm = min(256, T)

    def body(oa_ref, ob_ref, oc_ref, wa_ref, wp_ref, wc_ref, g_ref, bg_ref, o_ref):
        acc = jnp.zeros((tm, D_MODEL), F32)
        for i, (x_ref, w_ref) in enumerate(((oa_ref, wa_ref), (ob_ref, wp_ref), (oc_ref, wc_ref))):
            y = jnp.dot(x_ref[...], w_ref[...], preferred_element_type=F32)
            sl = slice(i * D_MODEL, (i + 1) * D_MODEL)
            acc = acc + _sigmoid(g_ref[:, sl].astype(F32) + bg_ref[:, sl]) * y
        o_ref[...] = acc.astype(BF16)

    br = pl.BlockSpec((tm, BRANCH_W), lambda i: (i, 0))
    wsp = pl.BlockSpec((BRANCH_W, D_MODEL), lambda i: (0, 0))
    return pl.pallas_call(
        body, name=name, grid=(T // tm,),
        in_specs=[br, br, br, wsp, wsp, wsp, pl.BlockSpec((tm, GATE_W), lambda i: (i, 0)),
                  pl.BlockSpec((1, GATE_W), lambda i: (0, 0))],
        out_specs=pl.BlockSpec((tm, D_MODEL), lambda i: (i, 0)),
        out_shape=jax.ShapeDtypeStruct((T, D_MODEL), BF16),
        compiler_params=_cp(("parallel",)),
    )(oa, ob, oc, wpa, wpp, wpc, proj, b_gate)


def _mix_bwd(oa, ob, oc, wpa, wpp, wpc, proj, b_gate, dmixed, name):
    T = oa.shape[0]
    tm = min(256, T)

    def body(oa_ref, ob_ref, oc_ref, wa_ref, wp_ref, wc_ref, g_ref, bg_ref, dm_ref,
             dya_ref, dyb_ref, dyc_ref, dg_ref, dbg_ref):
        i0 = pl.program_id(0)
        dm = dm_ref[...].astype(F32)
        parts = []
        for i, (x_ref, w_ref, dy_ref) in enumerate(((oa_ref, wa_ref, dya_ref), (ob_ref, wp_ref, dyb_ref),
                                                    (oc_ref, wc_ref, dyc_ref))):
            y = jnp.dot(x_ref[...], w_ref[...], preferred_element_type=F32)
            sl = slice(i * D_MODEL, (i + 1) * D_MODEL)
            gate = _sigmoid(g_ref[:, sl].astype(F32) + bg_ref[:, sl])
            dy_ref[...] = (dm * gate).astype(BF16)
            dgl = dm * y * gate * (1.0 - gate)
            dg_ref[:, sl] = dgl.astype(BF16)
            parts.append(jnp.sum(dgl, axis=0, keepdims=True))

        @pl.when(i0 == 0)
        def _():
            for i in range(3):
                dbg_ref[:, i * D_MODEL:(i + 1) * D_MODEL] = parts[i]

        @pl.when(i0 > 0)
        def _():
            for i in range(3):
                dbg_ref[:, i * D_MODEL:(i + 1) * D_MODEL] += parts[i]

    br = pl.BlockSpec((tm, BRANCH_W), lambda i: (i, 0))
    wsp = pl.BlockSpec((BRANCH_W, D_MODEL), lambda i: (0, 0))
    row = pl.BlockSpec((tm, D_MODEL), lambda i: (i, 0))
    gsp = pl.BlockSpec((tm, GATE_W), lambda i: (i, 0))
    bsp = pl.BlockSpec((1, GATE_W), lambda i: (0, 0))
    act = jax.ShapeDtypeStruct((T, D_MODEL), BF16)
    return pl.pallas_call(
        body, name=name, grid=(T // tm,),
        in_specs=[br, br, br, wsp, wsp, wsp, gsp, bsp, row],
        out_specs=[row, row, row, gsp, bsp],
        out_shape=[act, act, act, jax.ShapeDtypeStruct((T, GATE_W), BF16),
                   jax.ShapeDtypeStruct((1, GATE_W), F32)],
        compiler_params=_cp(("arbitrary",)),
    )(oa, ob, oc, wpa, wpp, wpc, proj, b_gate, dmixed)


def _swiglu_fwd(ab, name):
    T = ab.shape[0]
    tm, tc = min(512, T), 256
    nc = FFN_HIDDEN // tc

    def body(a_ref, b_ref, o_ref):
        a = a_ref[...].astype(F32)
        o_ref[...] = (a * _sigmoid(a) * b_ref[...].astype(F32)).astype(BF16)

    return pl.pallas_call(
        body, name=name, grid=(T // tm, nc),
        in_specs=[pl.BlockSpec((tm, tc), lambda i, j: (i, j)), pl.BlockSpec((tm, tc), lambda i, j: (i, nc + j))],
        out_specs=pl.BlockSpec((tm, tc), lambda i, j: (i, j)),
        out_shape=jax.ShapeDtypeStruct((T, FFN_HIDDEN), BF16),
        compiler_params=_cp(("parallel", "parallel")),
    )(ab, ab)


def _swiglu_bwd(ab, ds, name):
    T = ab.shape[0]
    tm, tc = min(512, T), 256
    nc = FFN_HIDDEN // tc

    def body(a_ref, b_ref, ds_ref, o_ref):
        half = pl.program_id(1)
        a = a_ref[...].astype(F32)
        b = b_ref[...].astype(F32)
        dsf = ds_ref[...].astype(F32)
        sg = _sigmoid(a)
        da = dsf * b * sg * (1.0 + a * (1.0 - sg))
        db = dsf * a * sg
        o_ref[...] = jnp.where(half == 0, da, db).astype(BF16)

    return pl.pallas_call(
        body, name=name, grid=(T // tm, 2, nc),
        in_specs=[pl.BlockSpec((tm, tc), lambda i, h, j: (i, j)),
                  pl.BlockSpec((tm, tc), lambda i, h, j: (i, nc + j)),
                  pl.BlockSpec((tm, tc), lambda i, h, j: (i, j))],
        out_specs=pl.BlockSpec((tm, tc), lambda i, h, j: (i, h * nc + j)),
        out_shape=jax.ShapeDtypeStruct((T, 2 * FFN_HIDDEN), BF16),
        compiler_params=_cp(("parallel", "parallel", "parallel")),
    )(ab, ab, ds)


def _adamw(w, g, m, v, name):
    R, C = w.shape
    tr = R
    for cand in (256, 352, 128, 64, 8):
        if R > cand and R % cand == 0:
            tr = cand
            break

    def body(w_ref, g_ref, m_ref, v_ref, d_ref, nm_ref, nv_ref):
        gv = g_ref[...]
        nm = ADAM_B1 * m_ref[...] + (1.0 - ADAM_B1) * gv
        nv = ADAM_B2 * v_ref[...] + (1.0 - ADAM_B2) * (gv * gv)
        m_hat = nm / (1.0 - ADAM_B1 ** ADAM_STEP)
        v_hat = nv / (1.0 - ADAM_B2 ** ADAM_STEP)
        d_ref[...] = -ADAM_LR * (m_hat / (jnp.sqrt(v_hat) + ADAM_EPS) + ADAM_WD * w_ref[...])
        nm_ref[...] = nm
        nv_ref[...] = nv

    blk = pl.BlockSpec((tr, C), lambda i: (i, 0))
    sh = jax.ShapeDtypeStruct((R, C), F32)
    return pl.pallas_call(
        body, name=name, grid=(R // tr,), in_specs=[blk] * 4, out_specs=[blk] * 3, out_shape=[sh] * 3,
        compiler_params=_cp(("parallel",)),
    )(w, g, m, v)


def _sum_slabs(x, name):
    n, R, C = x.shape
    tr = R
    for cand in (512, 256, 128, 64, 32, 16, 8):
        if R > cand and R % cand == 0:
            tr = cand
            break

    def body(x_ref, o_ref):
        acc = x_ref[0].astype(F32)
        for j in range(1, n):
            acc = acc + x_ref[j].astype(F32)
        o_ref[...] = acc

    return pl.pallas_call(
        body, name=name, grid=(R // tr,), in_specs=[pl.BlockSpec((n, tr, C), lambda i: (0, i, 0))],
        out_specs=pl.BlockSpec((tr, C), lambda i: (i, 0)), out_shape=jax.ShapeDtypeStruct((R, C), F32),
        compiler_params=_cp(("parallel",)),
    )(x)


def _all_gather(x, name):
    R, C = x.shape

    def body(x_ref, out_ref, send_sems, recv_sems, local_sem):
        x_, y_, c_ = lax.axis_index("x"), lax.axis_index("y"), lax.axis_index("c")
        me, sibling = (x_, y_, c_), (x_, y_, 1 - c_)
        chips = [(1 - x_, y_), (x_, 1 - y_), (1 - x_, 1 - y_)]

        def rows(px, py, pc):
            return out_ref.at[4 * px + 2 * py + pc]

        def copy(k, block, to, src=None):
            return pltpu.make_async_remote_copy(
                src_ref=rows(*block) if src is None else src, dst_ref=rows(*block),
                send_sem=send_sems.at[k], recv_sem=recv_sems.at[k],
                device_id=to, device_id_type=pl.DeviceIdType.MESH)

        mine = pltpu.make_async_copy(x_ref, rows(*me), local_sem)
        mine.start()
        first = [copy(0, me, sibling, src=x_ref)]
        first += [copy(1 + j, me, (*chip, c_), src=x_ref) for j, chip in enumerate(chips)]
        for cp in first:
            cp.start()
        passed = [copy(4 + j, (*chip, c_), sibling) for j, chip in enumerate(chips)]
        for j, chip in enumerate(chips):
            copy(1 + j, (*chip, c_), me).wait_recv()
            passed[j].start()
        copy(0, sibling, me).wait_recv()
        for j, chip in enumerate(chips):
            copy(4 + j, (*chip, 1 - c_), me).wait_recv()
        for cp in first + passed:
            cp.wait_send()
        mine.wait()

    return pl.pallas_call(
        body, name=name, out_shape=jax.ShapeDtypeStruct((N_DEV, R, C), x.dtype),
        in_specs=[pl.BlockSpec(memory_space=pl.ANY)], out_specs=pl.BlockSpec(memory_space=pl.ANY),
        scratch_shapes=[pltpu.SemaphoreType.DMA((7,)), pltpu.SemaphoreType.DMA((7,)), pltpu.SemaphoreType.DMA(())],
    )(x)


def _exchange(send, name):
    n, R, C = send.shape

    def body(s_ref, r_ref, send_sems, recv_sems, local_sem):
        x_, y_, c_ = lax.axis_index("x"), lax.axis_index("y"), lax.axis_index("c")
        me = 4 * x_ + 2 * y_ + c_
        mine = pltpu.make_async_copy(s_ref.at[me], r_ref.at[me], local_sem)
        mine.start()
        sends, recvs = [], []
        for k in range(1, n):
            dx, dy, dc = (k >> 2) & 1, (k >> 1) & 1, k & 1
            px, py, pc = x_ ^ dx, y_ ^ dy, c_ ^ dc
            peer = 4 * px + 2 * py + pc

            def copy(src, dst):
                return pltpu.make_async_remote_copy(
                    src_ref=s_ref.at[src], dst_ref=r_ref.at[dst],
                    send_sem=send_sems.at[k - 1], recv_sem=recv_sems.at[k - 1],
                    device_id=(px, py, pc), device_id_type=pl.DeviceIdType.MESH)

            sends.append(copy(peer, me))
            recvs.append(copy(me, peer))
        for cp in sends:
            cp.start()
        for cp in recvs:
            cp.wait_recv()
        for cp in sends:
            cp.wait_send()
        mine.wait()

    return pl.pallas_call(
        body, name=name, out_shape=jax.ShapeDtypeStruct((n, R, C), send.dtype),
        in_specs=[pl.BlockSpec(memory_space=pl.ANY)], out_specs=pl.BlockSpec(memory_space=pl.ANY),
        scratch_shapes=[pltpu.SemaphoreType.DMA((n - 1,)), pltpu.SemaphoreType.DMA((n - 1,)),
                        pltpu.SemaphoreType.DMA(())],
    )(send)


SHARDED = ("w_in", "w_proj_attn", "w_proj_pool", "w_proj_conv", "conv_w", "w_out", "w_gate_up", "w_down")
REPLICATED = ("attn_norm", "b_forget", "b_gate", "pool_w", "pool_scale", "ffn_norm", "final_norm")
WEIGHT_ORDER = ("attn_norm", "w_in", "b_forget", "b_gate", "w_proj_attn", "pool_w", "pool_scale", "w_proj_pool",
                "conv_w", "w_proj_conv", "w_out", "ffn_norm", "w_gate_up", "w_down", "final_norm")
PACK_ALIGN = LANES * 512


def _pad_flat(parts, align):
    flat = jnp.concatenate([p.reshape(-1) for p in parts])
    pad = (-flat.shape[0]) % align
    return jnp.pad(flat, (0, pad))


def _full_from_shards(name, g):
    if name in ("w_in", "w_gate_up", "w_proj_attn", "w_proj_pool", "w_proj_conv", "conv_w"):
        r = g.shape[1]
        return jnp.transpose(g, (1, 0, 2)).reshape(r, -1)
    return g.reshape(-1, g.shape[-1])


def _shards_from_full(name, full):
    if name in ("w_in", "w_gate_up", "w_proj_attn", "w_proj_pool", "w_proj_conv", "conv_w"):
        r, c = full.shape
        return jnp.transpose(full.reshape(r, N_DEV, c // N_DEV), (1, 0, 2)).reshape(N_DEV, -1)
    return full.reshape(N_DEV, -1)


def _split_w_in(w):
    qkv, f, rest, g = w[:, :1536], w[:, 1536:1544], w[:, 1544:3592], w[:, 3592:]
    main = jnp.concatenate([g, qkv, rest], axis=1)
    wf = jnp.pad(f, ((0, 0), (0, LANES - N_HEADS)))
    return main, wf


def _merge_w_in(dmain, dwf):
    return jnp.concatenate([dmain[:, OFF_Q:OFF_U], dwf[:, :N_HEADS], dmain[:, OFF_U:], dmain[:, :GATE_W]], axis=1)


def _layer_fwd(x, W, n_seq, l):
    T = x.shape[0]
    sfx = f"_l{l}"
    h1 = _rms_fwd(x, W["attn_norm"], "rms1" + sfx)
    proj = _matmul(h1, W["w_main"], mode="nn", out_dtype=BF16, name="proj_main" + sfx, tk=1024)
    f = _matmul(h1, W["w_f"], mode="nn", out_dtype=F32, name="proj_f" + sfx, tk=1024)
    Fc = _fox_cumsum_fwd(f, W["b_forget"], n_seq, "fox_cumsum" + sfx)
    F8 = Fc[:, :N_HEADS].T
    Fq, Fk = F8.reshape(N_HEADS, T, 1), F8.reshape(N_HEADS, 1, T)
    oa, oa32, lse = _attn_fwd(proj, Fq, Fk, n_seq, "attn_fwd" + sfx)
    ob = _pool_fwd(proj, W["pool_w"], W["pool_scale"], n_seq, "pool_fwd" + sfx)
    oc = _conv_fwd(proj, W["conv_w"], n_seq, "conv_fwd" + sfx)
    mixed = _mix_fwd(oa, ob, oc, W["w_proj_attn"], W["w_proj_pool"], W["w_proj_conv"], proj, W["b_gate"],
                     "mix_fwd" + sfx)
    x2 = _matmul(mixed, W["w_out"], mode="nn", out_dtype=F32, name="out_proj" + sfx, tk=1024, residual=x)
    h2 = _rms_fwd(x2, W["ffn_norm"], "rms2" + sfx)
    ab = _matmul(h2, W["w_gate_up"], mode="nn", out_dtype=BF16, name="gate_up" + sfx, tk=1024)
    s = _swiglu_fwd(ab, "swiglu_fwd" + sfx)
    x3 = _matmul(s, W["w_down"], mode="nn", out_dtype=F32, name="down" + sfx, tk=1408, residual=x2)
    saved = dict(x=x, h1=h1, proj=proj, f=f, Fq=Fq, Fk=Fk, oa=oa, oa32=oa32, lse=lse, ob=ob, oc=oc, mixed=mixed, x2=x2,
                 h2=h2, ab=ab, s=s)
    return x3, saved


def _layer_bwd(dx3, dx3b, W, sv, n_seq, l):
    T = dx3.shape[0]
    sfx = f"_l{l}"
    G = {}
    ds = _matmul(dx3b, W["w_down"], mode="nt", out_dtype=BF16, name="d_s" + sfx, tn=1408, tk=1024)
    G["w_down"] = _matmul(sv["s"], dx3b, mode="tn", out_dtype=BF16, name="dw_down" + sfx, tm=256)
    dab = _swiglu_bwd(sv["ab"], ds, "swiglu_bwd" + sfx)
    dh2 = _matmul(dab, W["w_gate_up"], mode="nt", out_dtype=BF16, name="d_h2" + sfx)
    G["w_gate_up"] = _matmul(sv["h2"], dab, mode="tn", out_dtype=BF16, name="dw_gate_up" + sfx)
    dx2, dx2b, G["ffn_norm"] = _rms_bwd(sv["x2"], W["ffn_norm"], dh2, dx3, "rms2_bwd" + sfx)
    dmixed = _matmul(dx2b, W["w_out"], mode="nt", out_dtype=BF16, name="d_mixed" + sfx, tk=1024)
    G["w_out"] = _matmul(sv["mixed"], dx2b, mode="tn", out_dtype=BF16, name="dw_out" + sfx)
    dya, dyb, dyc, dg, G["b_gate"] = _mix_bwd(sv["oa"], sv["ob"], sv["oc"], W["w_proj_attn"], W["w_proj_pool"],
                                              W["w_proj_conv"], sv["proj"], W["b_gate"], dmixed, "mix_bwd" + sfx)
    douts = {}
    for br, dy, o in (("attn", dya, sv["oa"]), ("pool", dyb, sv["ob"]), ("conv", dyc, sv["oc"])):
        douts[br] = _matmul(dy, W["w_proj_" + br], mode="nt", out_dtype=BF16, name=f"d_{br}_out" + sfx, tk=1024)
        G["w_proj_" + br] = _matmul(o, dy, mode="tn", out_dtype=BF16, name=f"dw_proj_{br}" + sfx)
    dcv, dcb, dcc, G["conv_w"] = _conv_bwd(sv["proj"], douts["conv"], W["conv_w"], n_seq, "conv_bwd" + sfx)
    du, G["pool_w"], G["pool_scale"] = _pool_bwd(sv["proj"], douts["pool"], W["pool_w"], W["pool_scale"], n_seq,
                                                 "pool_bwd" + sfx)
    delta = _attn_delta(douts["attn"], sv["oa32"], "attn_delta" + sfx)
    dq = _attn_bwd_dq(sv["proj"], douts["attn"], sv["lse"], delta, sv["Fq"], sv["Fk"], n_seq, "attn_dq" + sfx)
    dk, dv, dFk = _attn_bwd_dkv(sv["proj"], douts["attn"], sv["lse"], delta, sv["Fq"], sv["Fk"], n_seq,
                                "attn_dkv" + sfx)
    dF = jnp.pad(dFk.reshape(N_HEADS, T).T, ((0, 0), (0, LANES - N_HEADS)))
    df, G["b_forget"] = _fox_cumsum_bwd(sv["f"], W["b_forget"], dF, n_seq, "fox_cumsum_bwd" + sfx)
    dproj = jnp.concatenate([dg, dq, dk, dv, du, dcv, dcb, dcc], axis=1)
    dh1 = _matmul(dproj, W["w_main"], mode="nt", out_dtype=F32, name="d_h1_main" + sfx)
    dh1 = _matmul(df, W["w_f"], mode="nt", out_dtype=F32, name="d_h1_f" + sfx, residual=dh1)
    dw_main = _matmul(sv["h1"], dproj, mode="tn", out_dtype=BF16, name="dw_main" + sfx)
    dw_f = _matmul(sv["h1"], df, mode="tn", out_dtype=BF16, name="dw_f" + sfx)
    G["w_in"] = _merge_w_in(dw_main, dw_f)
    dx, dxb, G["attn_norm"] = _rms_bwd(sv["x"], W["attn_norm"], dh1, dx2, "rms1_bwd" + sfx)
    return dx, dxb, G


def _prep_layer_weights(full, l):
    W = {}
    W["w_main"], W["w_f"] = _split_w_in(full["w_in"][l])
    for n in ("w_proj_attn", "w_proj_pool", "w_proj_conv", "w_out", "w_gate_up", "w_down"):
        W[n] = full[n][l]
    W["conv_w"] = full["conv_w"][l].astype(F32)
    W["attn_norm"], W["ffn_norm"] = full["attn_norm"][l], full["ffn_norm"][l]
    W["b_forget"] = jnp.pad(full["b_forget"][l].reshape(1, N_HEADS), ((0, 0), (0, LANES - N_HEADS)))
    W["b_gate"] = full["b_gate"][l].reshape(1, GATE_W)
    W["pool_w"] = full["pool_w"][l].astype(BF16)
    W["pool_scale"] = full["pool_scale"][l].reshape(1, BRANCH_W)
    return W


def _local_step(x, target, full):
    n_seq, S, Dm = x.shape
    T = n_seq * S
    xt = x.reshape(T, Dm)
    Ws = [_prep_layer_weights(full, l) for l in range(DEPTH)]
    saved = []
    for l in range(DEPTH):
        xt, sv = _layer_fwd(xt, Ws[l], n_seq, l)
        saved.append(sv)
    loss, dx, dxb, g_final = _loss_head(xt, full["final_norm"], target.reshape(T, Dm), "loss_head")
    grads = [None] * DEPTH
    for l in reversed(range(DEPTH)):
        dx, dxb, grads[l] = _layer_bwd(dx, dxb, Ws[l], saved[l], n_seq, l)
    return loss, dx.reshape(n_seq, S, Dm), grads, g_final


def _gather_weights(shards):
    sizes = [int(np.prod(shards[n].shape)) for n in SHARDED]
    flat = _pad_flat([shards[n].astype(BF16) for n in SHARDED], PACK_ALIGN)
    gathered = _all_gather(flat.reshape(-1, LANES), "gather_weights").reshape(N_DEV, -1)
    full, off = {}, 0
    for n, sz in zip(SHARDED, sizes):
        g = gathered[:, off:off + sz].reshape((N_DEV,) + shards[n].shape)
        full[n] = [_full_from_shards(n, g[:, l]) for l in range(DEPTH)]
        off += sz
    return full


def _reduce_sharded_grads(grads, shards):
    sizes = [int(np.prod(shards[n].shape)) for n in SHARDED]
    parts = []
    for n in SHARDED:
        per_layer = [_shards_from_full(n, grads[l][n]) for l in range(DEPTH)]
        parts.append(jnp.stack(per_layer, axis=1).reshape(N_DEV, -1).astype(BF16))
    send = jnp.concatenate(parts, axis=1)
    pad = (-send.shape[1]) % PACK_ALIGN
    send = jnp.pad(send, ((0, 0), (0, pad))).reshape(N_DEV, -1, LANES)
    recv = _exchange(send, "exchange_grads")
    total = _sum_slabs(recv, "sum_grads").reshape(-1)
    out, off = {}, 0
    for n, sz in zip(SHARDED, sizes):
        out[n] = total[off:off + sz].reshape(shards[n].shape)
        off += sz
    return out


def _reduce_replicated(parts_in, shapes):
    sizes = [int(np.prod(s)) for s in shapes]
    flat = _pad_flat(parts_in, 64 * LANES)
    gathered = _all_gather(flat.reshape(-1, LANES), "gather_small_grads")
    total = _sum_slabs(gathered, "sum_small_grads").reshape(-1)
    out, off = [], 0
    for s, sz in zip(shapes, sizes):
        out.append(total[off:off + sz].reshape(s))
        off += sz
    return out


def _as_2d(a):
    if a.ndim == 1:
        return a.reshape(1, -1)
    return a.reshape(-1, a.shape[-1])


def kernel(x, attn_norm, w_in, b_forget, b_gate, w_proj_attn, pool_w, pool_scale, w_proj_pool, conv_w, w_proj_conv, w_out, ffn_norm, w_gate_up, w_down, final_norm, loss_target, m_attn_norm, m_w_in, m_b_forget, m_b_gate, m_w_proj_attn, m_pool_w, m_pool_scale, m_w_proj_pool, m_conv_w, m_w_proj_conv, m_w_out, m_ffn_norm, m_w_gate_up, m_w_down, m_final_norm, v_attn_norm, v_w_in, v_b_forget, v_b_gate, v_w_proj_attn, v_pool_w, v_pool_scale, v_w_proj_pool, v_conv_w, v_w_proj_conv, v_w_out, v_ffn_norm, v_w_gate_up, v_w_down, v_final_norm):
    weights = dict(attn_norm=attn_norm, w_in=w_in, b_forget=b_forget, b_gate=b_gate, w_proj_attn=w_proj_attn,
                   pool_w=pool_w, pool_scale=pool_scale, w_proj_pool=w_proj_pool, conv_w=conv_w,
                   w_proj_conv=w_proj_conv, w_out=w_out, ffn_norm=ffn_norm, w_gate_up=w_gate_up, w_down=w_down,
                   final_norm=final_norm)
    moments_m = dict(attn_norm=m_attn_norm, w_in=m_w_in, b_forget=m_b_forget, b_gate=m_b_gate,
                     w_proj_attn=m_w_proj_attn, pool_w=m_pool_w, pool_scale=m_pool_scale, w_proj_pool=m_w_proj_pool,
                     conv_w=m_conv_w, w_proj_conv=m_w_proj_conv, w_out=m_w_out, ffn_norm=m_ffn_norm,
                     w_gate_up=m_w_gate_up, w_down=m_w_down, final_norm=m_final_norm)
    moments_v = dict(attn_norm=v_attn_norm, w_in=v_w_in, b_forget=v_b_forget, b_gate=v_b_gate,
                     w_proj_attn=v_w_proj_attn, pool_w=v_pool_w, pool_scale=v_pool_scale, w_proj_pool=v_w_proj_pool,
                     conv_w=v_conv_w, w_proj_conv=v_w_proj_conv, w_out=v_w_out, ffn_norm=v_ffn_norm,
                     w_gate_up=v_w_gate_up, w_down=v_w_down, final_norm=v_final_norm)

    shards = {n: weights[n] for n in SHARDED}
    full = _gather_weights(shards)
    for n in REPLICATED:
        full[n] = weights[n]

    loss_part, grad_x, grads, g_final = _local_step(x, loss_target, full)

    gw = _reduce_sharded_grads(grads, shards)

    rep_names = [n for n in REPLICATED if n != "final_norm"]
    parts, shapes = [], []
    for n in rep_names:
        per_layer = []
        for l in range(DEPTH):
            g = grads[l][n]
            if n == "b_forget":
                g = g[:, :N_HEADS]
            per_layer.append(g.reshape(weights[n].shape[1:]))
        parts.append(jnp.stack(per_layer, axis=0))
        shapes.append(weights[n].shape)
    parts += [g_final.reshape(-1), loss_part[0, :1]]
    shapes += [final_norm.shape, (1,)]
    summed = _reduce_replicated(parts, shapes)
    for n, s in zip(rep_names + ["final_norm"], summed[:-1]):
        gw[n] = s
    loss = summed[-1].reshape(())

    deltas, new_m, new_v = {}, {}, {}
    for n in WEIGHT_ORDER:
        shape = weights[n].shape
        d, nm, nv = _adamw(_as_2d(weights[n]), _as_2d(gw[n]), _as_2d(moments_m[n]), _as_2d(moments_v[n]),
                           "adamw_" + n)
        deltas[n], new_m[n], new_v[n] = d.reshape(shape), nm.reshape(shape), nv.reshape(shape)

    return (loss, grad_x, *[gw[n] for n in WEIGHT_ORDER], *[deltas[n] for n in WEIGHT_ORDER],
            *[new_m[n] for n in WEIGHT_ORDER], *[new_v[n] for n in WEIGHT_ORDER])
```

```python
import functools

import numpy as np
import jax
import jax.numpy as jnp
from jax import lax
from jax.experimental import pallas as pl
from jax.experimental.pallas import tpu as pltpu

F32 = jnp.float32
BF16 = jnp.bfloat16

N_DEV = 8
D_MODEL = 1024
DEPTH = 2
N_HEADS = 8
HEAD_DIM = 64
BRANCH_W = 512
POOL_WINDOWS = (2, 4, 8, 16)
POOL_GD = 128
CONV_K = 3
FFN_HIDDEN = 2816
GATE_W = 3 * D_MODEL
IN_COLS = 6664
MAIN_COLS = GATE_W + 7 * BRANCH_W
RMS_EPS = 1e-6
NEG_INF = -1e30

ADAM_LR = 0.001
ADAM_B1 = 0.9
ADAM_B2 = 0.999
ADAM_EPS = 1e-08
ADAM_WD = 0.01
ADAM_STEP = 10

LANES = 128
VMEM_LIMIT = 56 * 1024 * 1024
ATT_BLK = 256
CUM_BLK = 256

OFF_G, OFF_Q, OFF_K, OFF_V, OFF_U, OFF_CV, OFF_CB, OFF_CC = (
    0, 3072, 3584, 4096, 4608, 5120, 5632, 6144)


def _cp(sem=None):
    return pltpu.CompilerParams(dimension_semantics=sem, vmem_limit_bytes=VMEM_LIMIT)


def _sigmoid(z):
    return 1.0 / (1.0 + jnp.exp(-z))


def _matmul(a, b, *, mode, out_dtype, name, tm=2048, tn=512, tk=None, residual=None):
    if mode == "nn":
        (M, K), N = a.shape, b.shape[1]
    elif mode == "nt":
        (M, K), N = a.shape, b.shape[0]
    else:
        (K, M), N = a.shape, b.shape[1]
    tm, tn, tk = min(tm, M), min(tn, N), K if tk is None else min(tk, K)
    assert M % tm == 0 and N % tn == 0 and K % tk == 0, (name, M, N, K, tm, tn, tk)
    nk = K // tk
    if mode == "nn":
        a_spec = pl.BlockSpec((tm, tk), lambda i, j, k: (i, k))
        b_spec = pl.BlockSpec((tk, tn), lambda i, j, k: (k, j))
        dims = (((1,), (0,)), ((), ()))
    elif mode == "nt":
        a_spec = pl.BlockSpec((tm, tk), lambda i, j, k: (i, k))
        b_spec = pl.BlockSpec((tn, tk), lambda i, j, k: (j, k))
        dims = (((1,), (1,)), ((), ()))
    else:
        a_spec = pl.BlockSpec((tk, tm), lambda i, j, k: (k, i))
        b_spec = pl.BlockSpec((tk, tn), lambda i, j, k: (k, j))
        dims = (((0,), (0,)), ((), ()))
    o_spec = pl.BlockSpec((tm, tn), lambda i, j, k: (i, j))
    has_res = residual is not None

    def body(*refs):
        a_ref, b_ref = refs[:2]
        r_ref = refs[2] if has_res else None
        o_ref = refs[2 + has_res]

        def finish(acc):
            if has_res:
                acc = acc + r_ref[...].astype(F32)
            o_ref[...] = acc.astype(out_dtype)

        prod = lax.dot_general(a_ref[...], b_ref[...], dims, preferred_element_type=F32)
        if nk == 1:
            finish(prod)
            return
        acc_ref = refs[-1]
        k = pl.program_id(2)

        @pl.when(k == 0)
        def _():
            acc_ref[...] = prod

        @pl.when(jnp.logical_and(k > 0, k < nk - 1))
        def _():
            acc_ref[...] += prod

        @pl.when(k == nk - 1)
        def _():
            finish(acc_ref[...] + prod)

    in_specs = [a_spec, b_spec] + ([o_spec] if has_res else [])
    args = (a, b) + ((residual,) if has_res else ())
    return pl.pallas_call(
        body, name=name, grid=(M // tm, N // tn, nk), in_specs=in_specs, out_specs=o_spec,
        out_shape=jax.ShapeDtypeStruct((M, N), out_dtype),
        scratch_shapes=[pltpu.VMEM((tm, tn), F32)] if nk > 1 else [],
        compiler_params=_cp(("parallel", "parallel", "arbitrary")),
    )(*args)


def _rms_fwd(x, g, name):
    T, Dm = x.shape
    tm = min(512, T)

    def body(x_ref, g_ref, h_ref):
        xf = x_ref[...]
        r = lax.rsqrt(jnp.mean(xf * xf, axis=-1, keepdims=True) + RMS_EPS)
        h_ref[...] = ((xf * r) * g_ref[...]).astype(BF16)

    return pl.pallas_call(
        body, name=name, grid=(T // tm,),
        in_specs=[pl.BlockSpec((tm, Dm), lambda i: (i, 0)), pl.BlockSpec((1, Dm), lambda i: (0, 0))],
        out_specs=pl.BlockSpec((tm, Dm), lambda i: (i, 0)),
        out_shape=jax.ShapeDtypeStruct((T, Dm), BF16),
        compiler_params=_cp(("parallel",)),
    )(x, g.reshape(1, Dm))


def _rms_bwd(x, g, dh, dres, name):
    T, Dm = x.shape
    tm = min(512, T)

    def body(x_ref, g_ref, dh_ref, dres_ref, dx_ref, dxb_ref, dg_ref):
        i = pl.program_id(0)
        xf = x_ref[...]
        r = lax.rsqrt(jnp.mean(xf * xf, axis=-1, keepdims=True) + RMS_EPS)
        xn = xf * r
        dhf = dh_ref[...].astype(F32)
        dxn = dhf * g_ref[...]
        c = jnp.mean(dxn * xn, axis=-1, keepdims=True)
        dx = dres_ref[...] + r * (dxn - xn * c)
        dx_ref[...] = dx
        dxb_ref[...] = dx.astype(BF16)
        part = jnp.sum(dhf * xn, axis=0, keepdims=True)

        @pl.when(i == 0)
        def _():
            dg_ref[...] = part

        @pl.when(i > 0)
        def _():
            dg_ref[...] += part

    row = pl.BlockSpec((tm, Dm), lambda i: (i, 0))
    vec = pl.BlockSpec((1, Dm), lambda i: (0, 0))
    return pl.pallas_call(
        body, name=name, grid=(T // tm,), in_specs=[row, vec, row, row], out_specs=[row, row, vec],
        out_shape=[jax.ShapeDtypeStruct((T, Dm), F32), jax.ShapeDtypeStruct((T, Dm), BF16),
                   jax.ShapeDtypeStruct((1, Dm), F32)],
        compiler_params=_cp(("arbitrary",)),
    )(x, g.reshape(1, Dm), dh, dres)


def _loss_head(x, g, target, name):
    T, Dm = x.shape
    tm = min(512, T)

    def body(x_ref, g_ref, t_ref, loss_ref, dx_ref, dxb_ref, dg_ref):
        i = pl.program_id(0)
        xf = x_ref[...]
        gv = g_ref[...]
        r = lax.rsqrt(jnp.mean(xf * xf, axis=-1, keepdims=True) + RMS_EPS)
        xn = xf * r
        diff = xn * gv - t_ref[...]
        per_tok = jnp.mean(diff * diff, axis=-1, keepdims=True)
        lpart = 0.5 * jnp.sum(per_tok, axis=0, keepdims=True) + jnp.zeros((1, LANES), F32)
        dy = diff * (1.0 / Dm)
        dxn = dy * gv
        c = jnp.mean(dxn * xn, axis=-1, keepdims=True)
        dx = r * (dxn - xn * c)
        dx_ref[...] = dx
        dxb_ref[...] = dx.astype(BF16)
        part = jnp.sum(dy * xn, axis=0, keepdims=True)

        @pl.when(i == 0)
        def _():
            dg_ref[...] = part
            loss_ref[...] = lpart

        @pl.when(i > 0)
        def _():
            dg_ref[...] += part
            loss_ref[...] += lpart

    row = pl.BlockSpec((tm, Dm), lambda i: (i, 0))
    vec = pl.BlockSpec((1, Dm), lambda i: (0, 0))
    lsp = pl.BlockSpec((1, LANES), lambda i: (0, 0))
    return pl.pallas_call(
        body, name=name, grid=(T // tm,), in_specs=[row, vec, row], out_specs=[lsp, row, row, vec],
        out_shape=[jax.ShapeDtypeStruct((1, LANES), F32), jax.ShapeDtypeStruct((T, Dm), F32),
                   jax.ShapeDtypeStruct((T, Dm), BF16), jax.ShapeDtypeStruct((1, Dm), F32)],
        compiler_params=_cp(("arbitrary",)),
    )(x, g.reshape(1, Dm), target)


def _split_bf16(v):
    hi = v.astype(BF16)
    r1 = v - hi.astype(F32)
    mid = r1.astype(BF16)
    lo = (r1 - mid.astype(F32)).astype(BF16)
    return hi, mid, lo


def _tri_dot(tri, v):
    hi, mid, lo = _split_bf16(v)
    dot = functools.partial(jnp.dot, preferred_element_type=F32)
    return dot(tri, hi) + dot(tri, mid) + dot(tri, lo)


def _log_sigmoid(z):
    return jnp.minimum(z, 0.0) - jnp.log(1.0 + jnp.exp(-jnp.abs(z)))


def _fox_cumsum_fwd(f, bf, n_seq, name):
    T = f.shape[0]
    S = T // n_seq
    c = min(CUM_BLK, S)

    def body(f_ref, b_ref, out_ref):
        ri = lax.broadcasted_iota(jnp.int32, (c, c), 0)
        ci = lax.broadcasted_iota(jnp.int32, (c, c), 1)
        tri = (ri >= ci).astype(BF16)
        carry = jnp.zeros((1, LANES), F32)
        for j in range(S // c):
            lf = _log_sigmoid(f_ref[j * c:(j + 1) * c, :] + b_ref[...])
            out_ref[j * c:(j + 1) * c, :] = _tri_dot(tri, lf) + carry
            carry = carry + jnp.sum(lf, axis=0, keepdims=True)

    blk = pl.BlockSpec((S, LANES), lambda b: (b, 0))
    return pl.pallas_call(
        body, name=name, grid=(n_seq,), in_specs=[blk, pl.BlockSpec((1, LANES), lambda b: (0, 0))],
        out_specs=blk, out_shape=jax.ShapeDtypeStruct((T, LANES), F32),
        compiler_params=_cp(("parallel",)),
    )(f, bf)


def _fox_cumsum_bwd(f, bf, dF, n_seq, name):
    T = f.shape[0]
    S = T // n_seq
    c = min(CUM_BLK, S)

    def body(f_ref, b_ref, dF_ref, df_ref, db_ref):
        b = pl.program_id(0)
        ri = lax.broadcasted_iota(jnp.int32, (c, c), 0)
        ci = lax.broadcasted_iota(jnp.int32, (c, c), 1)
        tri = (ri <= ci).astype(BF16)
        carry = jnp.zeros((1, LANES), F32)
        dbp = jnp.zeros((1, LANES), F32)
        for j in reversed(range(S // c)):
            dFc = dF_ref[j * c:(j + 1) * c, :]
            dlf = _tri_dot(tri, dFc) + carry
            carry = carry + jnp.sum(dFc, axis=0, keepdims=True)
            z = f_ref[j * c:(j + 1) * c, :] + b_ref[...]
            dz = dlf * _sigmoid(-z)
            df_ref[j * c:(j + 1) * c, :] = dz.astype(BF16)
            dbp = dbp + jnp.sum(dz, axis=0, keepdims=True)

        @pl.when(b == 0)
        def _():
            db_ref[...] = dbp

        @pl.when(b > 0)
        def _():
            db_ref[...] += dbp

    blk = pl.BlockSpec((S, LANES), lambda b: (b, 0))
    vec = pl.BlockSpec((1, LANES), lambda b: (0, 0))
    return pl.pallas_call(
        body, name=name, grid=(n_seq,), in_specs=[blk, vec, blk], out_specs=[blk, vec],
        out_shape=[jax.ShapeDtypeStruct((T, LANES), BF16), jax.ShapeDtypeStruct((1, LANES), F32)],
        compiler_params=_cp(("arbitrary",)),
    )(f, bf, dF)


def _pair_masks():
    lane = lax.broadcasted_iota(jnp.int32, (1, LANES), 1)
    lo = lane < HEAD_DIM
    return lo, jnp.logical_not(lo)


def _attn_logits(q, k, fq, fk, sel, mask, scale):
    qm = jnp.where(sel, q, jnp.zeros_like(q))
    s = lax.dot_general(qm, k, (((1,), (1,)), ((), ())), preferred_element_type=F32) * scale
    s = s + fq - fk
    return jnp.where(mask, s, NEG_INF)


def _causal_mask(qi, ki, blk):
    row = qi * blk + lax.broadcasted_iota(jnp.int32, (blk, blk), 0)
    col = ki * blk + lax.broadcasted_iota(jnp.int32, (blk, blk), 1)
    return col <= row


def _attn_fwd(proj, Fq, Fk, n_seq, name):
    T = proj.shape[0]
    S = T // n_seq
    blk = min(ATT_BLK, S)
    nb = S // blk
    scale = HEAD_DIM ** -0.5
    qc, kc, vc = OFF_Q // LANES, OFF_K // LANES, OFF_V // LANES

    def body(q_ref, k_ref, v_ref, fq_ref, fk_ref, o_ref, o32_ref, lse_ref, m_s, l_s, acc_s):
        qi, ki = pl.program_id(2), pl.program_id(3)

        @pl.when(ki == 0)
        def _():
            m_s[...] = jnp.full_like(m_s, NEG_INF)
            l_s[...] = jnp.zeros_like(l_s)
            acc_s[...] = jnp.zeros_like(acc_s)

        @pl.when(ki <= qi)
        def _():
            q, k, v = q_ref[...], k_ref[...], v_ref[...]
            mask = _causal_mask(qi, ki, blk)
            for hh, sel in enumerate(_pair_masks()):
                s = _attn_logits(q, k, fq_ref[hh], fk_ref[hh], sel, mask, scale)
                m_prev = m_s[hh]
                m_new = jnp.maximum(m_prev, jnp.max(s, axis=-1, keepdims=True))
                alpha = jnp.exp(m_prev - m_new)
                p = jnp.exp(s - m_new)
                l_s[hh] = alpha * l_s[hh] + jnp.sum(p, axis=-1, keepdims=True)
                p_hi = p.astype(BF16)
                p_lo = (p - p_hi.astype(F32)).astype(BF16)
                pv = jnp.dot(p_hi, v, preferred_element_type=F32) + jnp.dot(p_lo, v, preferred_element_type=F32)
                acc_s[hh] = alpha * acc_s[hh] + pv
                m_s[hh] = m_new

        @pl.when(ki == qi)
        def _():
            lo, _ = _pair_masks()
            o = jnp.where(lo, acc_s[0] / l_s[0], acc_s[1] / l_s[1])
            o_ref[...] = o.astype(BF16)
            o32_ref[...] = o
            lse_ref[0] = m_s[0] + jnp.log(l_s[0])
            lse_ref[1] = m_s[1] + jnp.log(l_s[1])

    grid = (n_seq, N_HEADS // 2, nb, nb)
    return pl.pallas_call(
        body, name=name, grid=grid,
        in_specs=[
            pl.BlockSpec((blk, LANES), lambda b, j, qi, ki: (b * nb + qi, qc + j)),
            pl.BlockSpec((blk, LANES), lambda b, j, qi, ki: (b * nb + jnp.minimum(ki, qi), kc + j)),
            pl.BlockSpec((blk, LANES), lambda b, j, qi, ki: (b * nb + jnp.minimum(ki, qi), vc + j)),
            pl.BlockSpec((2, blk, 1), lambda b, j, qi, ki: (j, b * nb + qi, 0)),
            pl.BlockSpec((2, 1, blk), lambda b, j, qi, ki: (j, 0, b * nb + jnp.minimum(ki, qi))),
        ],
        out_specs=[
            pl.BlockSpec((blk, LANES), lambda b, j, qi, ki: (b * nb + qi, j)),
            pl.BlockSpec((blk, LANES), lambda b, j, qi, ki: (b * nb + qi, j)),
            pl.BlockSpec((2, blk, 1), lambda b, j, qi, ki: (j, b * nb + qi, 0)),
        ],
        out_shape=[jax.ShapeDtypeStruct((T, BRANCH_W), BF16), jax.ShapeDtypeStruct((T, BRANCH_W), F32),
                   jax.ShapeDtypeStruct((N_HEADS, T, 1), F32)],
        scratch_shapes=[pltpu.VMEM((2, blk, 1), F32), pltpu.VMEM((2, blk, 1), F32),
                        pltpu.VMEM((2, blk, LANES), F32)],
        compiler_params=_cp(("parallel", "parallel", "parallel", "arbitrary")),
    )(proj, proj, proj, Fq, Fk)


def _attn_delta(do, o, name):
    T = do.shape[0]
    tm = min(512, T)

    def body(do_ref, o_ref, d_ref):
        prod = do_ref[...].astype(F32) * o_ref[...].astype(F32)
        lo, hi = _pair_masks()
        for j in range(N_HEADS // 2):
            pj = prod[:, j * LANES:(j + 1) * LANES]
            d_ref[2 * j] = jnp.sum(jnp.where(lo, pj, 0.0), axis=-1, keepdims=True)
            d_ref[2 * j + 1] = jnp.sum(jnp.where(hi, pj, 0.0), axis=-1, keepdims=True)

    row = pl.BlockSpec((tm, BRANCH_W), lambda i: (i, 0))
    return pl.pallas_call(
        body, name=name, grid=(T // tm,), in_specs=[row, row],
        out_specs=pl.BlockSpec((N_HEADS, tm, 1), lambda i: (0, i, 0)),
        out_shape=jax.ShapeDtypeStruct((N_HEADS, T, 1), F32),
        compiler_params=_cp(("parallel",)),
    )(do, o)


def _attn_bwd_dq(proj, do, lse, delta, Fq, Fk, n_seq, name):
    T = proj.shape[0]
    S = T // n_seq
    blk = min(ATT_BLK, S)
    nb = S // blk
    scale = HEAD_DIM ** -0.5
    qc, kc, vc = OFF_Q // LANES, OFF_K // LANES, OFF_V // LANES

    def body(q_ref, k_ref, v_ref, do_ref, lse_ref, dl_ref, fq_ref, fk_ref, dq_ref, acc_s):
        qi, ki = pl.program_id(2), pl.program_id(3)

        @pl.when(ki == 0)
        def _():
            acc_s[...] = jnp.zeros_like(acc_s)

        @pl.when(ki <= qi)
        def _():
            q, k, v, do_ = q_ref[...], k_ref[...], v_ref[...], do_ref[...]
            mask = _causal_mask(qi, ki, blk)
            for hh, sel in enumerate(_pair_masks()):
                s = _attn_logits(q, k, fq_ref[hh], fk_ref[hh], sel, mask, scale)
                p = jnp.exp(s - lse_ref[hh])
                dom = jnp.where(sel, do_, jnp.zeros_like(do_))
                dp = lax.dot_general(dom, v, (((1,), (1,)), ((), ())), preferred_element_type=F32)
                ds = p * (dp - dl_ref[hh])
                acc_s[hh] += jnp.dot(ds.astype(BF16), k, preferred_element_type=F32)

        @pl.when(ki == qi)
        def _():
            lo, _ = _pair_masks()
            dq_ref[...] = (jnp.where(lo, acc_s[0], acc_s[1]) * scale).astype(BF16)

    qmap = lambda b, j, qi, ki: (b * nb + qi, j)
    col1 = pl.BlockSpec((2, blk, 1), lambda b, j, qi, ki: (j, b * nb + qi, 0))
    return pl.pallas_call(
        body, name=name, grid=(n_seq, N_HEADS // 2, nb, nb),
        in_specs=[
            pl.BlockSpec((blk, LANES), lambda b, j, qi, ki: (b * nb + qi, qc + j)),
            pl.BlockSpec((blk, LANES), lambda b, j, qi, ki: (b * nb + jnp.minimum(ki, qi), kc + j)),
            pl.BlockSpec((blk, LANES), lambda b, j, qi, ki: (b * nb + jnp.minimum(ki, qi), vc + j)),
            pl.BlockSpec((blk, LANES), qmap),
            col1, col1, col1,
            pl.BlockSpec((2, 1, blk), lambda b, j, qi, ki: (j, 0, b * nb + jnp.minimum(ki, qi))),
        ],
        out_specs=pl.BlockSpec((blk, LANES), qmap),
        out_shape=jax.ShapeDtypeStruct((T, BRANCH_W), BF16),
        scratch_shapes=[pltpu.VMEM((2, blk, LANES), F32)],
        compiler_params=_cp(("parallel", "parallel", "parallel", "arbitrary")),
    )(proj, proj, proj, do, lse, delta, Fq, Fk)


def _attn_bwd_dkv(proj, do, lse, delta, Fq, Fk, n_seq, name):
    T = proj.shape[0]
    S = T // n_seq
    blk = min(ATT_BLK, S)
    nb = S // blk
    scale = HEAD_DIM ** -0.5
    qc, kc, vc = OFF_Q // LANES, OFF_K // LANES, OFF_V // LANES
    tdot = functools.partial(lax.dot_general, dimension_numbers=(((0,), (0,)), ((), ())),
                             preferred_element_type=F32)

    def body(q_ref, k_ref, v_ref, do_ref, lse_ref, dl_ref, fq_ref, fk_ref, dk_ref, dv_ref, dfk_ref,
             dk_s, dv_s, df_s):
        ki, qi = pl.program_id(2), pl.program_id(3)

        @pl.when(qi == 0)
        def _():
            dk_s[...] = jnp.zeros_like(dk_s)
            dv_s[...] = jnp.zeros_like(dv_s)
            df_s[...] = jnp.zeros_like(df_s)

        @pl.when(qi >= ki)
        def _():
            q, k, v, do_ = q_ref[...], k_ref[...], v_ref[...], do_ref[...]
            mask = _causal_mask(qi, ki, blk)
            for hh, sel in enumerate(_pair_masks()):
                s = _attn_logits(q, k, fq_ref[hh], fk_ref[hh], sel, mask, scale)
                p = jnp.exp(s - lse_ref[hh])
                dv_s[hh] += tdot(p.astype(BF16), do_)
                dom = jnp.where(sel, do_, jnp.zeros_like(do_))
                dp = lax.dot_general(dom, v, (((1,), (1,)), ((), ())), preferred_element_type=F32)
                ds = p * (dp - dl_ref[hh])
                dk_s[hh] += tdot(ds.astype(BF16), q)
                df_s[hh] -= jnp.sum(ds, axis=0, keepdims=True)

        @pl.when(qi == nb - 1)
        def _():
            lo, _ = _pair_masks()
            dk_ref[...] = (jnp.where(lo, dk_s[0], dk_s[1]) * scale).astype(BF16)
            dv_ref[...] = jnp.where(lo, dv_s[0], dv_s[1]).astype(BF16)
            dfk_ref[...] = df_s[...]

    kmap = lambda b, j, ki, qi: (b * nb + ki, j)
    col1 = pl.BlockSpec((2, blk, 1), lambda b, j, ki, qi: (j, b * nb + jnp.maximum(qi, ki), 0))
    rowk = pl.BlockSpec((2, 1, blk), lambda b, j, ki, qi: (j, 0, b * nb + ki))
    return pl.pallas_call(
        body, name=name, grid=(n_seq, N_HEADS // 2, nb, nb),
        in_specs=[
            pl.BlockSpec((blk, LANES), lambda b, j, ki, qi: (b * nb + jnp.maximum(qi, ki), qc + j)),
            pl.BlockSpec((blk, LANES), lambda b, j, ki, qi: (b * nb + ki, kc + j)),
            pl.BlockSpec((blk, LANES), lambda b, j, ki, qi: (b * nb + ki, vc + j)),
            pl.BlockSpec((blk, LANES), lambda b, j, ki, qi: (b * nb + jnp.maximum(qi, ki), j)),
            col1, col1, col1, rowk,
        ],
        out_specs=[pl.BlockSpec((blk, LANES), kmap), pl.BlockSpec((blk, LANES), kmap), rowk],
        out_shape=[jax.ShapeDtypeStruct((T, BRANCH_W), BF16), jax.ShapeDtypeStruct((T, BRANCH_W), BF16),
                   jax.ShapeDtypeStruct((N_HEADS, 1, T), F32)],
        scratch_shapes=[pltpu.VMEM((2, blk, LANES), F32), pltpu.VMEM((2, blk, LANES), F32),
                        pltpu.VMEM((2, 1, blk), F32)],
        compiler_params=_cp(("parallel", "parallel", "parallel", "arbitrary")),
    )(proj, proj, proj, do, lse, delta, Fq, Fk)


def _shift_down(v, k, row):
    return jnp.where(row >= k, pltpu.roll(v, k, 0), 0.0)


def _shift_up(v, k, row, S):
    return jnp.where(row < S - k, pltpu.roll(v, S - k, 0), 0.0)


def _pool_diff(uf, w, row):
    acc, k = uf, 1
    while k < w:
        acc = acc + _shift_down(acc, k, row)
        k *= 2
    n = jnp.minimum(row + 1, w).astype(F32)
    return acc / n - uf


def _pool_fwd(proj, pool_w, pool_scale, n_seq, name):
    T = proj.shape[0]
    S = T // n_seq

    def body(u_ref, w_ref, sc_ref, o_ref):
        g = pl.program_id(1)
        row = lax.broadcasted_iota(jnp.int32, (S, POOL_GD), 0)
        uf = u_ref[...].astype(F32)
        d = _pool_diff(uf, POOL_WINDOWS[0], row)
        for gi in range(1, len(POOL_WINDOWS)):
            d = jnp.where(g == gi, _pool_diff(uf, POOL_WINDOWS[gi], row), d)
        e = jnp.dot(d.astype(BF16), w_ref[0], preferred_element_type=F32)
        o_ref[...] = (e * sc_ref[...]).astype(BF16)

    uc = OFF_U // POOL_GD
    return pl.pallas_call(
        body, name=name, grid=(n_seq, len(POOL_WINDOWS)),
        in_specs=[pl.BlockSpec((S, POOL_GD), lambda b, g: (b, uc + g)),
                  pl.BlockSpec((1, POOL_GD, POOL_GD), lambda b, g: (g, 0, 0)),
                  pl.BlockSpec((1, POOL_GD), lambda b, g: (0, g))],
        out_specs=pl.BlockSpec((S, POOL_GD), lambda b, g: (b, g)),
        out_shape=jax.ShapeDtypeStruct((T, BRANCH_W), BF16),
        compiler_params=_cp(("parallel", "parallel")),
    )(proj, pool_w, pool_scale)


def _pool_bwd(proj, dout, pool_w, pool_scale, n_seq, name):
    T = proj.shape[0]
    S = T // n_seq

    def body(u_ref, do_ref, w_ref, sc_ref, du_ref, dw_ref, dsc_ref):
        g, b = pl.program_id(0), pl.program_id(1)
        row = lax.broadcasted_iota(jnp.int32, (S, POOL_GD), 0)
        uf = u_ref[...].astype(F32)
        d = _pool_diff(uf, POOL_WINDOWS[0], row)
        for gi in range(1, len(POOL_WINDOWS)):
            d = jnp.where(g == gi, _pool_diff(uf, POOL_WINDOWS[gi], row), d)
        db16 = d.astype(BF16)
        w = w_ref[0]
        e = jnp.dot(db16, w, preferred_element_type=F32)
        dof = do_ref[...].astype(F32)
        dsc = jnp.sum(dof * e, axis=0, keepdims=True)
        de = (dof * sc_ref[...]).astype(BF16)
        dd = lax.dot_general(de, w, (((1,), (1,)), ((), ())), preferred_element_type=F32)
        dw = lax.dot_general(db16, de, (((0,), (0,)), ((), ())), preferred_element_type=F32)
        du = jnp.zeros_like(dd)
        for gi, wlen in enumerate(POOL_WINDOWS):
            n = jnp.minimum(row + 1, wlen).astype(F32)
            acc, k = dd / n, 1
            while k < wlen:
                acc = acc + _shift_up(acc, k, row, S)
                k *= 2
            du = jnp.where(g == gi, acc - dd, du)
        du_ref[...] = du.astype(BF16)

        @pl.when(b == 0)
        def _():
            dw_ref[0] = dw
            dsc_ref[...] = dsc

        @pl.when(b > 0)
        def _():
            dw_ref[0] += dw
            dsc_ref[...] += dsc

    uc = OFF_U // POOL_GD
    return pl.pallas_call(
        body, name=name, grid=(len(POOL_WINDOWS), n_seq),
        in_specs=[pl.BlockSpec((S, POOL_GD), lambda g, b: (b, uc + g)),
                  pl.BlockSpec((S, POOL_GD), lambda g, b: (b, g)),
                  pl.BlockSpec((1, POOL_GD, POOL_GD), lambda g, b: (g, 0, 0)),
                  pl.BlockSpec((1, POOL_GD), lambda g, b: (0, g))],
        out_specs=[pl.BlockSpec((S, POOL_GD), lambda g, b: (b, g)),
                   pl.BlockSpec((1, POOL_GD, POOL_GD), lambda g, b: (g, 0, 0)),
                   pl.BlockSpec((1, POOL_GD), lambda g, b: (0, g))],
        out_shape=[jax.ShapeDtypeStruct((T, BRANCH_W), BF16),
                   jax.ShapeDtypeStruct((len(POOL_WINDOWS), POOL_GD, POOL_GD), F32),
                   jax.ShapeDtypeStruct((1, BRANCH_W), F32)],
        compiler_params=_cp(("parallel", "arbitrary")),
    )(proj, dout, pool_w, pool_scale)


def _conv_fwd(proj, conv_w, n_seq, name):
    T = proj.shape[0]
    S = T // n_seq
    nc = BRANCH_W // LANES

    def body(cv_ref, cb_ref, cc_ref, w_ref, o_ref):
        row = lax.broadcasted_iota(jnp.int32, (S, LANES), 0)
        z = cc_ref[...].astype(F32) * cv_ref[...].astype(F32)
        w = w_ref[...]
        y = w[0:1] * _shift_down(z, 2, row) + w[1:2] * _shift_down(z, 1, row) + w[2:3] * z
        o_ref[...] = (cb_ref[...].astype(F32) * y).astype(BF16)

    def col(off):
        return pl.BlockSpec((S, LANES), lambda b, j: (b, off // LANES + j))

    return pl.pallas_call(
        body, name=name, grid=(n_seq, nc),
        in_specs=[col(OFF_CV), col(OFF_CB), col(OFF_CC), pl.BlockSpec((CONV_K, LANES), lambda b, j: (0, j))],
        out_specs=pl.BlockSpec((S, LANES), lambda b, j: (b, j)),
        out_shape=jax.ShapeDtypeStruct((T, BRANCH_W), BF16),
        compiler_params=_cp(("parallel", "parallel")),
    )(proj, proj, proj, conv_w)


def _conv_bwd(proj, dout, conv_w, n_seq, name):
    T = proj.shape[0]
    S = T // n_seq
    nc = BRANCH_W // LANES

    def body(cv_ref, cb_ref, cc_ref, do_ref, w_ref, dcv_ref, dcb_ref, dcc_ref, dw_ref):
        b = pl.program_id(1)
        row = lax.broadcasted_iota(jnp.int32, (S, LANES), 0)
        cv, cb, cc = cv_ref[...].astype(F32), cb_ref[...].astype(F32), cc_ref[...].astype(F32)
        dof = do_ref[...].astype(F32)
        w = w_ref[...]
        z = cc * cv
        z1, z2 = _shift_down(z, 1, row), _shift_down(z, 2, row)
        y = w[0:1] * z2 + w[1:2] * z1 + w[2:3] * z
        dcb_ref[...] = (dof * y).astype(BF16)
        dy = dof * cb
        dz = w[2:3] * dy + w[1:2] * _shift_up(dy, 1, row, S) + w[0:1] * _shift_up(dy, 2, row, S)
        dcc_ref[...] = (dz * cv).astype(BF16)
        dcv_ref[...] = (dz * cc).astype(BF16)
        dws = [jnp.sum(dy * zk, axis=0, keepdims=True) for zk in (z2, z1, z)]

        @pl.when(b == 0)
        def _():
            for kk in range(CONV_K):
                dw_ref[kk:kk + 1, :] = dws[kk]

        @pl.when(b > 0)
        def _():
            for kk in range(CONV_K):
                dw_ref[kk:kk + 1, :] += dws[kk]

    def col(off):
        return pl.BlockSpec((S, LANES), lambda j, b: (b, off // LANES + j))

    out = pl.BlockSpec((S, LANES), lambda j, b: (b, j))
    wsp = pl.BlockSpec((CONV_K, LANES), lambda j, b: (0, j))
    act = jax.ShapeDtypeStruct((T, BRANCH_W), BF16)
    return pl.pallas_call(
        body, name=name, grid=(nc, n_seq),
        in_specs=[col(OFF_CV), col(OFF_CB), col(OFF_CC), out, wsp],
        out_specs=[out, out, out, wsp],
        out_shape=[act, act, act, jax.ShapeDtypeStruct((CONV_K, BRANCH_W), F32)],
        compiler_params=_cp(("parallel", "arbitrary")),
    )(proj, proj, proj, dout, conv_w)


def _mix_fwd(oa, ob, oc, wpa, wpp, wpc, proj, b_gate, name):
    T = oa.shape[0]
    tm = min(256, T)

    def body(oa_ref, ob_ref, oc_ref, wa_ref, wp_ref, wc_ref, g_ref, bg_ref, o_ref):
        acc = jnp.zeros((tm, D_MODEL), F32)
        for i, (x_ref, w_ref) in enumerate(((oa_ref, wa_ref), (ob_ref, wp_ref), (oc_ref, wc_ref))):
            y = jnp.dot(x_ref[...], w_ref[...], preferred_element_type=F32)
            sl = slice(i * D_MODEL, (i + 1) * D_MODEL)
            acc = acc + _sigmoid(g_ref[:, sl].astype(F32) + bg_ref[:, sl]) * y
        o_ref[...] = acc.astype(BF16)

    br = pl.BlockSpec((tm, BRANCH_W), lambda i: (i, 0))
    wsp = pl.BlockSpec((BRANCH_W, D_MODEL), lambda i: (0, 0))
    return pl.pallas_call(
        body, name=name, grid=(T // tm,),
        in_specs=[br, br, br, wsp, wsp, wsp, pl.BlockSpec((tm, GATE_W), lambda i: (i, 0)),
                  pl.BlockSpec((1, GATE_W), lambda i: (0, 0))],
        out_specs=pl.BlockSpec((tm, D_MODEL), lambda i: (i, 0)),
        out_shape=jax.ShapeDtypeStruct((T, D_MODEL), BF16),
        compiler_params=_cp(("parallel",)),
    )(oa, ob, oc, wpa, wpp, wpc, proj, b_gate)


def _mix_bwd(oa, ob, oc, wpa, wpp, wpc, proj, b_gate, dmixed, name):
    T = oa.shape[0]
    tm = min(256, T)

    def body(oa_ref, ob_ref, oc_ref, wa_ref, wp_ref, wc_ref, g_ref, bg_ref, dm_ref,
             dya_ref, dyb_ref, dyc_ref, dg_ref, dbg_ref):
        i0 = pl.program_id(0)
        dm = dm_ref[...].astype(F32)
        parts = []
        for i, (x_ref, w_ref, dy_ref) in enumerate(((oa_ref, wa_ref, dya_ref), (ob_ref, wp_ref, dyb_ref),
                                                    (oc_ref, wc_ref, dyc_ref))):
            y = jnp.dot(x_ref[...], w_ref[...], preferred_element_type=F32)
            sl = slice(i * D_MODEL, (i + 1) * D_MODEL)
            gate = _sigmoid(g_ref[:, sl].astype(F32) + bg_ref[:, sl])
            dy_ref[...] = (dm * gate).astype(BF16)
            dgl = dm * y * gate * (1.0 - gate)
            dg_ref[:, sl] = dgl.astype(BF16)
            parts.append(jnp.sum(dgl, axis=0, keepdims=True))

        @pl.when(i0 == 0)
        def _():
            for i in range(3):
                dbg_ref[:, i * D_MODEL:(i + 1) * D_MODEL] = parts[i]

        @pl.when(i0 > 0)
        def _():
            for i in range(3):
                dbg_ref[:, i * D_MODEL:(i + 1) * D_MODEL] += parts[i]

    br = pl.BlockSpec((tm, BRANCH_W), lambda i: (i, 0))
    wsp = pl.BlockSpec((BRANCH_W, D_MODEL), lambda i: (0, 0))
    row = pl.BlockSpec((tm, D_MODEL), lambda i: (i, 0))
    gsp = pl.BlockSpec((tm, GATE_W), lambda i: (i, 0))
    bsp = pl.BlockSpec((1, GATE_W), lambda i: (0, 0))
    act = jax.ShapeDtypeStruct((T, D_MODEL), BF16)
    return pl.pallas_call(
        body, name=name, grid=(T // tm,),
        in_specs=[br, br, br, wsp, wsp, wsp, gsp, bsp, row],
        out_specs=[row, row, row, gsp, bsp],
        out_shape=[act, act, act, jax.ShapeDtypeStruct((T, GATE_W), BF16),
                   jax.ShapeDtypeStruct((1, GATE_W), F32)],
        compiler_params=_cp(("arbitrary",)),
    )(oa, ob, oc, wpa, wpp, wpc, proj, b_gate, dmixed)


def _swiglu_fwd(ab, name):
    T = ab.shape[0]
    tm = min(256, T)

    def body(ab_ref, o_ref):
        a = ab_ref[:, :FFN_HIDDEN].astype(F32)
        o_ref[...] = (a * _sigmoid(a) * ab_ref[:, FFN_HIDDEN:].astype(F32)).astype(BF16)

    return pl.pallas_call(
        body, name=name, grid=(T // tm,),
        in_specs=[pl.BlockSpec((tm, 2 * FFN_HIDDEN), lambda i: (i, 0))],
        out_specs=pl.BlockSpec((tm, FFN_HIDDEN), lambda i: (i, 0)),
        out_shape=jax.ShapeDtypeStruct((T, FFN_HIDDEN), BF16),
        compiler_params=_cp(("parallel",)),
    )(ab)


def _swiglu_bwd(ab, ds, name):
    T = ab.shape[0]
    tm = min(256, T)

    def body(ab_ref, ds_ref, o_ref):
        a = ab_ref[:, :FFN_HIDDEN].astype(F32)
        b = ab_ref[:, FFN_HIDDEN:].astype(F32)
        dsf = ds_ref[...].astype(F32)
        sg = _sigmoid(a)
        o_ref[:, :FFN_HIDDEN] = (dsf * b * sg * (1.0 + a * (1.0 - sg))).astype(BF16)
        o_ref[:, FFN_HIDDEN:] = (dsf * a * sg).astype(BF16)

    full = pl.BlockSpec((tm, 2 * FFN_HIDDEN), lambda i: (i, 0))
    return pl.pallas_call(
        body, name=name, grid=(T // tm,),
        in_specs=[full, pl.BlockSpec((tm, FFN_HIDDEN), lambda i: (i, 0))],
        out_specs=full,
        out_shape=jax.ShapeDtypeStruct((T, 2 * FFN_HIDDEN), BF16),
        compiler_params=_cp(("parallel",)),
    )(ab, ds)


def _adamw(w, g, m, v, name):
    R, C = w.shape
    tr = R
    for cand in (256, 352, 128, 64, 8):
        if R > cand and R % cand == 0:
            tr = cand
            break

    def body(w_ref, g_ref, m_ref, v_ref, d_ref, nm_ref, nv_ref):
        gv = g_ref[...]
        nm = ADAM_B1 * m_ref[...] + (1.0 - ADAM_B1) * gv
        nv = ADAM_B2 * v_ref[...] + (1.0 - ADAM_B2) * (gv * gv)
        m_hat = nm / (1.0 - ADAM_B1 ** ADAM_STEP)
        v_hat = nv / (1.0 - ADAM_B2 ** ADAM_STEP)
        d_ref[...] = -ADAM_LR * (m_hat / (jnp.sqrt(v_hat) + ADAM_EPS) + ADAM_WD * w_ref[...])
        nm_ref[...] = nm
        nv_ref[...] = nv

    blk = pl.BlockSpec((tr, C), lambda i: (i, 0))
    sh = jax.ShapeDtypeStruct((R, C), F32)
    return pl.pallas_call(
        body, name=name, grid=(R // tr,), in_specs=[blk] * 4, out_specs=[blk] * 3, out_shape=[sh] * 3,
        compiler_params=_cp(("parallel",)),
    )(w, g, m, v)


def _sum_slabs(x, name):
    n, R, C = x.shape
    tr = R
    for cand in (512, 256, 128, 64, 32, 16, 8):
        if R > cand and R % cand == 0:
            tr = cand
            break

    def body(x_ref, o_ref):
        acc = x_ref[0].astype(F32)
        for j in range(1, n):
            acc = acc + x_ref[j].astype(F32)
        o_ref[...] = acc

    return pl.pallas_call(
        body, name=name, grid=(R // tr,), in_specs=[pl.BlockSpec((n, tr, C), lambda i: (0, i, 0))],
        out_specs=pl.BlockSpec((tr, C), lambda i: (i, 0)), out_shape=jax.ShapeDtypeStruct((R, C), F32),
        compiler_params=_cp(("parallel",)),
    )(x)


def _multi_gather(xs, layers, name):
    nt = len(xs)
    shapes = [x.shape if lay is None else x.shape[1:] for x, lay in zip(xs, layers)]

    def body(*refs):
        x_refs, out_refs = refs[:nt], refs[nt:2 * nt]
        send_sems, recv_sems, local_sems = refs[2 * nt:]
        x_, y_, c_ = lax.axis_index("x"), lax.axis_index("y"), lax.axis_index("c")
        me, sibling = (x_, y_, c_), (x_, y_, 1 - c_)
        chips = [(1 - x_, y_), (x_, 1 - y_), (1 - x_, 1 - y_)]

        def own_block(t):
            return x_refs[t] if layers[t] is None else x_refs[t].at[layers[t]]

        def copy(t, k, block, to, own=False):
            px, py, pc = block
            dst = out_refs[t].at[4 * px + 2 * py + pc]
            return pltpu.make_async_remote_copy(
                src_ref=own_block(t) if own else dst, dst_ref=dst,
                send_sem=send_sems.at[t, k], recv_sem=recv_sems.at[t, k],
                device_id=to, device_id_type=pl.DeviceIdType.MESH)

        mine, first, passed = [], [], []
        for t in range(nt):
            mine.append(pltpu.make_async_copy(own_block(t), out_refs[t].at[4 * x_ + 2 * y_ + c_], local_sems.at[t]))
            mine[-1].start()
            first.append([copy(t, 1 + j, me, (*chip, c_), own=True) for j, chip in enumerate(chips)]
                         + [copy(t, 0, me, sibling, own=True)])
            for cp in first[-1]:
                cp.start()
        for t in range(nt):
            for j, chip in enumerate(chips):
                copy(t, 1 + j, (*chip, c_), me).wait_recv()
                passed.append(copy(t, 4 + j, (*chip, c_), sibling))
                passed[-1].start()
        for t in range(nt):
            copy(t, 0, sibling, me).wait_recv()
            for j, chip in enumerate(chips):
                copy(t, 4 + j, (*chip, 1 - c_), me).wait_recv()
        for cp in [c for f in first for c in f] + passed:
            cp.wait_send()
        for cp in mine:
            cp.wait()

    hbm = pl.BlockSpec(memory_space=pl.ANY)
    return pl.pallas_call(
        body, name=name, out_shape=[jax.ShapeDtypeStruct((N_DEV,) + tuple(s), x.dtype) for s, x in zip(shapes, xs)],
        in_specs=[hbm] * nt, out_specs=[hbm] * nt,
        scratch_shapes=[pltpu.SemaphoreType.DMA((nt, 7)), pltpu.SemaphoreType.DMA((nt, 7)),
                        pltpu.SemaphoreType.DMA((nt,))],
    )(*xs)


def _multi_exchange(sends, name):
    nt = len(sends)

    def body(*refs):
        s_refs, r_refs = refs[:nt], refs[nt:2 * nt]
        send_sems, recv_sems, local_sems = refs[2 * nt:]
        x_, y_, c_ = lax.axis_index("x"), lax.axis_index("y"), lax.axis_index("c")
        me = 4 * x_ + 2 * y_ + c_
        mine, out, inc = [], [], []
        for t in range(nt):
            mine.append(pltpu.make_async_copy(s_refs[t].at[me], r_refs[t].at[me], local_sems.at[t]))
            mine[-1].start()
        for k in (2, 4, 6, 3, 5, 7, 1):
            px, py, pc = x_ ^ ((k >> 2) & 1), y_ ^ ((k >> 1) & 1), c_ ^ (k & 1)
            peer = 4 * px + 2 * py + pc
            for t in range(nt):
                def copy(src, dst):
                    return pltpu.make_async_remote_copy(
                        src_ref=s_refs[t].at[src], dst_ref=r_refs[t].at[dst],
                        send_sem=send_sems.at[t, k - 1], recv_sem=recv_sems.at[t, k - 1],
                        device_id=(px, py, pc), device_id_type=pl.DeviceIdType.MESH)

                out.append(copy(peer, me))
                inc.append(copy(me, peer))
        for cp in out:
            cp.start()
        for cp in inc:
            cp.wait_recv()
        for cp in out:
            cp.wait_send()
        for cp in mine:
            cp.wait()

    hbm = pl.BlockSpec(memory_space=pl.ANY)
    return pl.pallas_call(
        body, name=name, out_shape=[jax.ShapeDtypeStruct(s.shape, s.dtype) for s in sends],
        in_specs=[hbm] * nt, out_specs=[hbm] * nt,
        scratch_shapes=[pltpu.SemaphoreType.DMA((nt, N_DEV - 1)), pltpu.SemaphoreType.DMA((nt, N_DEV - 1)),
                        pltpu.SemaphoreType.DMA((nt,))],
    )(*sends)


def _runs(mapping):
    runs, c, n = [], 0, len(mapping)
    while c < n:
        if mapping[c] is None:
            c += 1
            continue
        sid, d, lo = mapping[c][0], mapping[c][1] - c, c
        while c < n and mapping[c] is not None and mapping[c][0] == sid and mapping[c][1] - c == d:
            c += 1
        runs.append((lo, c, sid, d))
    return runs


def _tile_plan(mapping, src_widths):
    runs = _runs(mapping)
    plan = []
    for t in range(len(mapping) // LANES):
        pieces = []
        for lo, hi, sid, d in runs:
            lo_t, hi_t = max(lo, t * LANES), min(hi, (t + 1) * LANES)
            if lo_t >= hi_t:
                continue
            a = ((lo_t + d) // LANES) * LANES
            win = min(2 * LANES, src_widths[sid] - a)
            shift = t * LANES + d - a
            pieces.append((sid, a, win, shift, lo_t - t * LANES, hi_t - t * LANES))
        plan.append(pieces)
    return plan


def _reblock(srcs, src_views, outs, out_views, name):
    R = srcs[0].shape[-2]
    tr = min(256, R)
    widths = {sid: srcs[ai].shape[-1] for sid, (ai, _) in src_views.items()}
    plans = [(ai, li, _tile_plan(mapping, widths)) for ai, li, mapping in out_views]
    ns = len(srcs)

    def body(*refs):
        s_refs, o_refs = refs[:ns], refs[ns:]
        cache = {}

        def shift_matrix(win, shift, lo, hi):
            key = (win, shift, lo, hi)
            if key not in cache:
                r = lax.broadcasted_iota(jnp.int32, (win, LANES), 0)
                c = lax.broadcasted_iota(jnp.int32, (win, LANES), 1)
                hit = jnp.logical_and(r - c == shift, jnp.logical_and(c >= lo, c < hi))
                cache[key] = jnp.where(hit, 1.0, 0.0).astype(BF16)
            return cache[key]

        for ai, li, plan in plans:
            for t, pieces in enumerate(plan):
                acc = None
                for sid, a, win, shift, lo, hi in pieces:
                    sa, sl = src_views[sid]
                    src = s_refs[sa][:, a:a + win] if sl is None else s_refs[sa][sl, :, a:a + win]
                    part = jnp.dot(src, shift_matrix(win, shift, lo, hi), preferred_element_type=F32)
                    acc = part if acc is None else acc + part
                val = jnp.zeros((tr, LANES), BF16) if acc is None else acc.astype(BF16)
                if li is None:
                    o_refs[ai][:, t * LANES:(t + 1) * LANES] = val
                else:
                    o_refs[ai][li, :, t * LANES:(t + 1) * LANES] = val

    def spec(shape):
        if len(shape) == 2:
            return pl.BlockSpec((tr, shape[1]), lambda i: (i, 0))
        return pl.BlockSpec((shape[0], tr, shape[2]), lambda i: (0, i, 0))

    return pl.pallas_call(
        body, name=name, grid=(R // tr,), in_specs=[spec(s.shape) for s in srcs],
        out_specs=[spec(s) for s in outs], out_shape=[jax.ShapeDtypeStruct(s, BF16) for s in outs],
        compiler_params=_cp(("parallel",)),
    )(*srcs)


SHARDED = ("w_in", "w_gate_up", "w_proj_attn", "w_proj_pool", "w_proj_conv", "w_out", "w_down")
WEIGHT_ORDER = ("attn_norm", "w_in", "b_forget", "b_gate", "w_proj_attn", "pool_w", "pool_scale", "w_proj_pool",
                "conv_w", "w_proj_conv", "w_out", "ffn_norm", "w_gate_up", "w_down", "final_norm")
IN_SHARD, IN_SHARD_PAD = IN_COLS // N_DEV, 896
GU_SHARD, GU_SHARD_PAD = 2 * FFN_HIDDEN // N_DEV, 768


def _w_in_col(c):
    if c < GATE_W:
        return c + 3592
    if c < OFF_U:
        return c - OFF_Q
    return c - OFF_U + 1544


def _w_in_full(gathered, name):
    main = [divmod(_w_in_col(c), IN_SHARD) for c in range(MAIN_COLS)]
    fcols = [divmod(1536 + c, IN_SHARD) if c < N_HEADS else None for c in range(LANES)]
    R = gathered.shape[1]
    return _reblock([gathered], {i: (0, i) for i in range(N_DEV)}, [(R, MAIN_COLS), (R, LANES)],
                    [(0, None, main), (1, None, fcols)], name)


def _w_in_slabs(dmain, dwf, name):
    inv = {_w_in_col(c): ("m", c) for c in range(MAIN_COLS)}
    inv.update({1536 + c: ("f", c) for c in range(N_HEADS)})
    views = []
    for i in range(N_DEV):
        mapping = [inv[IN_SHARD * i + j] if j < IN_SHARD else None for j in range(IN_SHARD_PAD)]
        views.append((0, i, mapping))
    R = dmain.shape[0]
    return _reblock([dmain, dwf], {"m": (0, None), "f": (1, None)}, [(N_DEV, R, IN_SHARD_PAD)], views, name)[0]


def _w_gu_full(gathered, name):
    mapping = [divmod(c, GU_SHARD) for c in range(2 * FFN_HIDDEN)]
    R = gathered.shape[1]
    return _reblock([gathered], {i: (0, i) for i in range(N_DEV)}, [(R, 2 * FFN_HIDDEN)], [(0, None, mapping)], name)[0]


def _w_gu_slabs(dw, name):
    views = [(0, i, [("w", GU_SHARD * i + j) if j < GU_SHARD else None for j in range(GU_SHARD_PAD)])
             for i in range(N_DEV)]
    R = dw.shape[0]
    return _reblock([dw], {"w": (0, None)}, [(N_DEV, R, GU_SHARD_PAD)], views, name)[0]


def _layer_fwd(x, W, n_seq, l):
    T = x.shape[0]
    sfx = f"_l{l}"
    h1 = _rms_fwd(x, W["attn_norm"], "rms1" + sfx)
    proj = _matmul(h1, W["w_main"], mode="nn", out_dtype=BF16, name="proj_main" + sfx)
    f = _matmul(h1, W["w_f"], mode="nn", out_dtype=F32, name="proj_f" + sfx)
    Fc = _fox_cumsum_fwd(f, W["b_forget"], n_seq, "fox_cumsum" + sfx)
    F8 = Fc[:, :N_HEADS].T
    Fq, Fk = F8.reshape(N_HEADS, T, 1), F8.reshape(N_HEADS, 1, T)
    oa, oa32, lse = _attn_fwd(proj, Fq, Fk, n_seq, "attn_fwd" + sfx)
    ob = _pool_fwd(proj, W["pool_w"], W["pool_scale"], n_seq, "pool_fwd" + sfx)
    oc = _conv_fwd(proj, W["conv_w"], n_seq, "conv_fwd" + sfx)
    mixed = _mix_fwd(oa, ob, oc, W["w_proj_attn"], W["w_proj_pool"], W["w_proj_conv"], proj, W["b_gate"],
                     "mix_fwd" + sfx)
    x2 = _matmul(mixed, W["w_out"], mode="nn", out_dtype=F32, name="out_proj" + sfx, residual=x)
    h2 = _rms_fwd(x2, W["ffn_norm"], "rms2" + sfx)
    ab = _matmul(h2, W["w_gate_up"], mode="nn", out_dtype=BF16, name="gate_up" + sfx)
    s = _swiglu_fwd(ab, "swiglu_fwd" + sfx)
    x3 = _matmul(s, W["w_down"], mode="nn", out_dtype=F32, name="down" + sfx, tm=1024, tn=1024, tk=1408,
                 residual=x2)
    saved = dict(x=x, h1=h1, proj=proj, f=f, Fq=Fq, Fk=Fk, oa=oa, oa32=oa32, lse=lse, ob=ob, oc=oc, mixed=mixed, x2=x2,
                 h2=h2, ab=ab, s=s)
    return x3, saved


def _layer_bwd(dx3, dx3b, W, sv, n_seq, l):
    T = dx3.shape[0]
    sfx = f"_l{l}"
    G = {}
    ds = _matmul(dx3b, W["w_down"], mode="nt", out_dtype=BF16, name="d_s" + sfx, tm=1024, tn=1408)
    G["w_down"] = _matmul(sv["s"], dx3b, mode="tn", out_dtype=BF16, name="dw_down" + sfx, tm=256, tn=1024)
    dab = _swiglu_bwd(sv["ab"], ds, "swiglu_bwd" + sfx)
    dh2 = _matmul(dab, W["w_gate_up"], mode="nt", out_dtype=BF16, name="d_h2" + sfx, tm=1024, tn=1024, tk=1408)
    G["w_gate_up"] = _matmul(sv["h2"], dab, mode="tn", out_dtype=BF16, name="dw_gate_up" + sfx, tm=1024)
    dx2, dx2b, G["ffn_norm"] = _rms_bwd(sv["x2"], W["ffn_norm"], dh2, dx3, "rms2_bwd" + sfx)
    dmixed = _matmul(dx2b, W["w_out"], mode="nt", out_dtype=BF16, name="d_mixed" + sfx)
    G["w_out"] = _matmul(sv["mixed"], dx2b, mode="tn", out_dtype=BF16, name="dw_out" + sfx, tm=1024)
    dya, dyb, dyc, dg, G["b_gate"] = _mix_bwd(sv["oa"], sv["ob"], sv["oc"], W["w_proj_attn"], W["w_proj_pool"],
                                              W["w_proj_conv"], sv["proj"], W["b_gate"], dmixed, "mix_bwd" + sfx)
    douts = {}
    for br, dy, o in (("attn", dya, sv["oa"]), ("pool", dyb, sv["ob"]), ("conv", dyc, sv["oc"])):
        douts[br] = _matmul(dy, W["w_proj_" + br], mode="nt", out_dtype=BF16, name=f"d_{br}_out" + sfx)
        G["w_proj_" + br] = _matmul(o, dy, mode="tn", out_dtype=BF16, name=f"dw_proj_{br}" + sfx, tm=512)
    dcv, dcb, dcc, G["conv_w"] = _conv_bwd(sv["proj"], douts["conv"], W["conv_w"], n_seq, "conv_bwd" + sfx)
    du, G["pool_w"], G["pool_scale"] = _pool_bwd(sv["proj"], douts["pool"], W["pool_w"], W["pool_scale"], n_seq,
                                                 "pool_bwd" + sfx)
    delta = _attn_delta(douts["attn"], sv["oa32"], "attn_delta" + sfx)
    dq = _attn_bwd_dq(sv["proj"], douts["attn"], sv["lse"], delta, sv["Fq"], sv["Fk"], n_seq, "attn_dq" + sfx)
    dk, dv, dFk = _attn_bwd_dkv(sv["proj"], douts["attn"], sv["lse"], delta, sv["Fq"], sv["Fk"], n_seq,
                                "attn_dkv" + sfx)
    dF = jnp.pad(dFk.reshape(N_HEADS, T).T, ((0, 0), (0, LANES - N_HEADS)))
    df, G["b_forget"] = _fox_cumsum_bwd(sv["f"], W["b_forget"], dF, n_seq, "fox_cumsum_bwd" + sfx)
    dproj = jnp.concatenate([dg, dq, dk, dv, du, dcv, dcb, dcc], axis=1)
    dh1 = _matmul(dproj, W["w_main"], mode="nt", out_dtype=F32, name="d_h1_main" + sfx, tm=1024, tn=1024, tk=1664)
    dh1 = _matmul(df, W["w_f"], mode="nt", out_dtype=F32, name="d_h1_f" + sfx, residual=dh1)
    G["w_main"] = _matmul(sv["h1"], dproj, mode="tn", out_dtype=BF16, name="dw_main" + sfx, tm=1024)
    G["w_f"] = _matmul(sv["h1"], df, mode="tn", out_dtype=BF16, name="dw_f" + sfx, tm=1024)
    dx, dxb, G["attn_norm"] = _rms_bwd(sv["x"], W["attn_norm"], dh1, dx2, "rms1_bwd" + sfx)
    return dx, dxb, G


def _replicated_operands(rep, l):
    W = {}
    W["attn_norm"], W["ffn_norm"] = rep["attn_norm"][l], rep["ffn_norm"][l]
    W["b_forget"] = jnp.pad(rep["b_forget"][l].reshape(1, N_HEADS), ((0, 0), (0, LANES - N_HEADS)))
    W["b_gate"] = rep["b_gate"][l].reshape(1, GATE_W)
    W["pool_w"] = rep["pool_w"][l].astype(BF16)
    W["pool_scale"] = rep["pool_scale"][l].reshape(1, BRANCH_W)
    return W


def _local_step(x, target, Ws, final_norm, on_layer_grads=None):
    n_seq, S, Dm = x.shape
    T = n_seq * S
    xt = x.reshape(T, Dm)
    saved = []
    for l in range(DEPTH):
        xt, sv = _layer_fwd(xt, Ws[l], n_seq, l)
        saved.append(sv)
    loss, dx, dxb, g_final = _loss_head(xt, final_norm, target.reshape(T, Dm), "loss_head")
    grads = [None] * DEPTH
    for l in reversed(range(DEPTH)):
        dx, dxb, grads[l] = _layer_bwd(dx, dxb, Ws[l], saved[l], n_seq, l)
        if on_layer_grads is not None:
            grads[l] = on_layer_grads(l, grads[l])
    return loss, dx.reshape(n_seq, S, Dm), grads, g_final


def _padded_shards(weights):
    sh = {n: weights[n].astype(BF16) for n in SHARDED}
    sh["w_in"] = jnp.pad(sh["w_in"], ((0, 0), (0, 0), (0, IN_SHARD_PAD - IN_SHARD)))
    sh["w_gate_up"] = jnp.pad(sh["w_gate_up"], ((0, 0), (0, 0), (0, GU_SHARD_PAD - GU_SHARD)))
    return sh


def _gather_layer(sh, conv_w, l):
    names = list(SHARDED)
    xs, layers = [sh[n] for n in names], [l] * len(names)
    if l == 0:
        xs, layers = xs + [conv_w], layers + [None]
    got = _multi_gather(xs, layers, f"gather_weights_l{l}")
    g = dict(zip(names, got))
    W = {}
    W["w_main"], W["w_f"] = _w_in_full(g["w_in"], f"w_in_full_l{l}")
    W["w_gate_up"] = _w_gu_full(g["w_gate_up"], f"w_gate_up_full_l{l}")
    for n in ("w_proj_attn", "w_proj_pool", "w_proj_conv"):
        W[n] = jnp.transpose(g[n], (1, 0, 2)).reshape(BRANCH_W, D_MODEL)
    W["w_out"] = g["w_out"].reshape(D_MODEL, D_MODEL)
    W["w_down"] = g["w_down"].reshape(FFN_HIDDEN, D_MODEL)
    return W, (got[-1] if l == 0 else None)


def _exchange_layer_grads(G, l):
    names = list(SHARDED)
    slabs = {
        "w_in": _w_in_slabs(G["w_main"], G["w_f"], f"w_in_slabs_l{l}"),
        "w_gate_up": _w_gu_slabs(G["w_gate_up"], f"w_gate_up_slabs_l{l}"),
        "w_out": G["w_out"].reshape(N_DEV, D_MODEL // N_DEV, D_MODEL),
        "w_down": G["w_down"].reshape(N_DEV, FFN_HIDDEN // N_DEV, D_MODEL),
    }
    for n in ("w_proj_attn", "w_proj_pool", "w_proj_conv"):
        slabs[n] = jnp.transpose(G[n].reshape(BRANCH_W, N_DEV, D_MODEL // N_DEV), (1, 0, 2))
    recv = _multi_exchange([slabs[n] for n in names], f"exchange_grads_l{l}")
    out = {n: _sum_slabs(r, f"sum_{n}_l{l}") for n, r in zip(names, recv)}
    out["w_in"] = out["w_in"][:, :IN_SHARD]
    out["w_gate_up"] = out["w_gate_up"][:, :GU_SHARD]
    return out


def _sum_small(xs, name):
    def body(*refs):
        for x_ref, o_ref in zip(refs[:len(xs)], refs[len(xs):]):
            acc = x_ref[0]
            for j in range(1, N_DEV):
                acc = acc + x_ref[j]
            o_ref[...] = acc

    return pl.pallas_call(
        body, name=name, out_shape=[jax.ShapeDtypeStruct(x.shape[1:], F32) for x in xs],
        compiler_params=_cp(),
    )(*xs)


def _as_2d(a):
    if a.ndim == 1:
        return a.reshape(1, -1)
    return a.reshape(-1, a.shape[-1])


def kernel(x, attn_norm, w_in, b_forget, b_gate, w_proj_attn, pool_w, pool_scale, w_proj_pool, conv_w, w_proj_conv, w_out, ffn_norm, w_gate_up, w_down, final_norm, loss_target, m_attn_norm, m_w_in, m_b_forget, m_b_gate, m_w_proj_attn, m_pool_w, m_pool_scale, m_w_proj_pool, m_conv_w, m_w_proj_conv, m_w_out, m_ffn_norm, m_w_gate_up, m_w_down, m_final_norm, v_attn_norm, v_w_in, v_b_forget, v_b_gate, v_w_proj_attn, v_pool_w, v_pool_scale, v_w_proj_pool, v_conv_w, v_w_proj_conv, v_w_out, v_ffn_norm, v_w_gate_up, v_w_down, v_final_norm):
    weights = dict(attn_norm=attn_norm, w_in=w_in, b_forget=b_forget, b_gate=b_gate, w_proj_attn=w_proj_attn,
                   pool_w=pool_w, pool_scale=pool_scale, w_proj_pool=w_proj_pool, conv_w=conv_w,
                   w_proj_conv=w_proj_conv, w_out=w_out, ffn_norm=ffn_norm, w_gate_up=w_gate_up, w_down=w_down,
                   final_norm=final_norm)
    moments_m = dict(attn_norm=m_attn_norm, w_in=m_w_in, b_forget=m_b_forget, b_gate=m_b_gate,
                     w_proj_attn=m_w_proj_attn, pool_w=m_pool_w, pool_scale=m_pool_scale, w_proj_pool=m_w_proj_pool,
                     conv_w=m_conv_w, w_proj_conv=m_w_proj_conv, w_out=m_w_out, ffn_norm=m_ffn_norm,
                     w_gate_up=m_w_gate_up, w_down=m_w_down, final_norm=m_final_norm)
    moments_v = dict(attn_norm=v_attn_norm, w_in=v_w_in, b_forget=v_b_forget, b_gate=v_b_gate,
                     w_proj_attn=v_w_proj_attn, pool_w=v_pool_w, pool_scale=v_pool_scale, w_proj_pool=v_w_proj_pool,
                     conv_w=v_conv_w, w_proj_conv=v_w_proj_conv, w_out=v_w_out, ffn_norm=v_ffn_norm,
                     w_gate_up=v_w_gate_up, w_down=v_w_down, final_norm=v_final_norm)

    sh = _padded_shards(weights)
    Ws, conv_all = [], None
    for l in range(DEPTH):
        W, conv_got = _gather_layer(sh, conv_w, l)
        conv_all = conv_got if conv_got is not None else conv_all
        Ws.append(W)
    for l in range(DEPTH):
        Ws[l].update(_replicated_operands(weights, l))
        Ws[l]["conv_w"] = jnp.transpose(conv_all[:, l], (1, 0, 2)).reshape(CONV_K, BRANCH_W)

    def reduce_layer(l, G):
        out = _exchange_layer_grads(G, l)
        out.update({n: G[n] for n in ("attn_norm", "b_forget", "b_gate", "pool_w", "pool_scale", "ffn_norm",
                                      "conv_w")})
        return out

    loss_part, grad_x, grads, g_final = _local_step(x, loss_target, Ws, final_norm, reduce_layer)
    gw = {n: jnp.stack([grads[l][n] for l in range(DEPTH)]) for n in SHARDED}

    small = ("attn_norm", "b_forget", "b_gate", "pool_w", "pool_scale", "ffn_norm", "conv_w")
    parts = [jnp.stack([grads[l][n] for l in range(DEPTH)]) for n in small] + [g_final, loss_part]
    gathered = _multi_gather(parts, [None] * len(parts), "gather_small_grads")
    summed = _sum_small(gathered, "sum_small_grads")
    for n, s in zip(small, summed):
        gw[n] = s
    gw["attn_norm"], gw["ffn_norm"] = gw["attn_norm"][:, 0], gw["ffn_norm"][:, 0]
    gw["b_forget"] = gw["b_forget"][:, 0, :N_HEADS]
    gw["b_gate"], gw["pool_scale"] = gw["b_gate"][:, 0], gw["pool_scale"][:, 0]
    me = 4 * lax.axis_index("x") + 2 * lax.axis_index("y") + lax.axis_index("c")
    gw["conv_w"] = lax.dynamic_slice_in_dim(gw["conv_w"], me * (BRANCH_W // N_DEV), BRANCH_W // N_DEV, axis=2)
    gw["final_norm"] = summed[-2][0]
    loss = summed[-1][0, 0]

    deltas, new_m, new_v = {}, {}, {}
    for n in WEIGHT_ORDER:
        shape = weights[n].shape
        d, nm, nv = _adamw(_as_2d(weights[n]), _as_2d(gw[n]), _as_2d(moments_m[n]), _as_2d(moments_v[n]),
                           "adamw_" + n)
        deltas[n], new_m[n], new_v[n] = d.reshape(shape), nm.reshape(shape), nv.reshape(shape)

    return (loss, grad_x, *[gw[n] for n in WEIGHT_ORDER], *[deltas[n] for n in WEIGHT_ORDER],
            *[new_m[n] for n in WEIGHT_ORDER], *[new_v[n] for n in WEIGHT_ORDER])
```

```python
import functools

import numpy as np
import jax
import jax.numpy as jnp
from jax import lax
from jax.experimental import pallas as pl
from jax.experimental.pallas import tpu as pltpu

F32 = jnp.float32
BF16 = jnp.bfloat16

N_DEV = 8
D_MODEL = 1024
DEPTH = 2
N_HEADS = 8
HEAD_DIM = 64
BRANCH_W = 512
POOL_WINDOWS = (2, 4, 8, 16)
POOL_GD = 128
CONV_K = 3
FFN_HIDDEN = 2816
GATE_W = 3 * D_MODEL
IN_COLS = 6664
MAIN_COLS = GATE_W + 7 * BRANCH_W
RMS_EPS = 1e-6
NEG_INF = -1e30

ADAM_LR = 0.001
ADAM_B1 = 0.9
ADAM_B2 = 0.999
ADAM_EPS = 1e-08
ADAM_WD = 0.01
ADAM_STEP = 10

LANES = 128
VMEM_LIMIT = 56 * 1024 * 1024
ATT_BLK = 256
CUM_BLK = 256

OFF_G, OFF_Q, OFF_K, OFF_V, OFF_U, OFF_CV, OFF_CB, OFF_CC = (
    0, 3072, 3584, 4096, 4608, 5120, 5632, 6144)


def _cp(sem=None):
    return pltpu.CompilerParams(dimension_semantics=sem, vmem_limit_bytes=VMEM_LIMIT)


def _sigmoid(z):
    return 1.0 / (1.0 + jnp.exp(-z))


def _matmul(a, b, *, mode, out_dtype, name, tm=2048, tn=512, tk=None, residual=None):
    if mode == "nn":
        (M, K), N = a.shape, b.shape[1]
    elif mode == "nt":
        (M, K), N = a.shape, b.shape[0]
    else:
        (K, M), N = a.shape, b.shape[1]
    tm, tn, tk = min(tm, M), min(tn, N), K if tk is None else min(tk, K)
    assert M % tm == 0 and N % tn == 0 and K % tk == 0, (name, M, N, K, tm, tn, tk)
    nk = K // tk
    if mode == "nn":
        a_spec = pl.BlockSpec((tm, tk), lambda i, j, k: (i, k))
        b_spec = pl.BlockSpec((tk, tn), lambda i, j, k: (k, j))
        dims = (((1,), (0,)), ((), ()))
    elif mode == "nt":
        a_spec = pl.BlockSpec((tm, tk), lambda i, j, k: (i, k))
        b_spec = pl.BlockSpec((tn, tk), lambda i, j, k: (j, k))
        dims = (((1,), (1,)), ((), ()))
    else:
        a_spec = pl.BlockSpec((tk, tm), lambda i, j, k: (k, i))
        b_spec = pl.BlockSpec((tk, tn), lambda i, j, k: (k, j))
        dims = (((0,), (0,)), ((), ()))
    o_spec = pl.BlockSpec((tm, tn), lambda i, j, k: (i, j))
    has_res = residual is not None

    def body(*refs):
        a_ref, b_ref = refs[:2]
        r_ref = refs[2] if has_res else None
        o_ref = refs[2 + has_res]

        def finish(acc):
            if has_res:
                acc = acc + r_ref[...].astype(F32)
            o_ref[...] = acc.astype(out_dtype)

        prod = lax.dot_general(a_ref[...], b_ref[...], dims, preferred_element_type=F32)
        if nk == 1:
            finish(prod)
            return
        acc_ref = refs[-1]
        k = pl.program_id(2)

        @pl.when(k == 0)
        def _():
            acc_ref[...] = prod

        @pl.when(jnp.logical_and(k > 0, k < nk - 1))
        def _():
            acc_ref[...] += prod

        @pl.when(k == nk - 1)
        def _():
            finish(acc_ref[...] + prod)

    in_specs = [a_spec, b_spec] + ([o_spec] if has_res else [])
    args = (a, b) + ((residual,) if has_res else ())
    return pl.pallas_call(
        body, name=name, grid=(M // tm, N // tn, nk), in_specs=in_specs, out_specs=o_spec,
        out_shape=jax.ShapeDtypeStruct((M, N), out_dtype),
        scratch_shapes=[pltpu.VMEM((tm, tn), F32)] if nk > 1 else [],
        compiler_params=_cp(("parallel", "parallel", "arbitrary")),
    )(*args)


def _rms_fwd(x, g, name):
    T, Dm = x.shape
    tm = min(512, T)

    def body(x_ref, g_ref, h_ref):
        xf = x_ref[...]
        r = lax.rsqrt(jnp.mean(xf * xf, axis=-1, keepdims=True) + RMS_EPS)
        h_ref[...] = ((xf * r) * g_ref[...]).astype(BF16)

    return pl.pallas_call(
        body, name=name, grid=(T // tm,),
        in_specs=[pl.BlockSpec((tm, Dm), lambda i: (i, 0)), pl.BlockSpec((1, Dm), lambda i: (0, 0))],
        out_specs=pl.BlockSpec((tm, Dm), lambda i: (i, 0)),
        out_shape=jax.ShapeDtypeStruct((T, Dm), BF16),
        compiler_params=_cp(("parallel",)),
    )(x, g.reshape(1, Dm))


def _rms_bwd(x, g, dh, dres, name):
    T, Dm = x.shape
    tm = min(512, T)

    def body(x_ref, g_ref, dh_ref, dres_ref, dx_ref, dxb_ref, dg_ref):
        i = pl.program_id(0)
        xf = x_ref[...]
        r = lax.rsqrt(jnp.mean(xf * xf, axis=-1, keepdims=True) + RMS_EPS)
        xn = xf * r
        dhf = dh_ref[...].astype(F32)
        dxn = dhf * g_ref[...]
        c = jnp.mean(dxn * xn, axis=-1, keepdims=True)
        dx = dres_ref[...] + r * (dxn - xn * c)
        dx_ref[...] = dx
        dxb_ref[...] = dx.astype(BF16)
        part = jnp.sum(dhf * xn, axis=0, keepdims=True)

        @pl.when(i == 0)
        def _():
            dg_ref[...] = part

        @pl.when(i > 0)
        def _():
            dg_ref[...] += part

    row = pl.BlockSpec((tm, Dm), lambda i: (i, 0))
    vec = pl.BlockSpec((1, Dm), lambda i: (0, 0))
    return pl.pallas_call(
        body, name=name, grid=(T // tm,), in_specs=[row, vec, row, row], out_specs=[row, row, vec],
        out_shape=[jax.ShapeDtypeStruct((T, Dm), F32), jax.ShapeDtypeStruct((T, Dm), BF16),
                   jax.ShapeDtypeStruct((1, Dm), F32)],
        compiler_params=_cp(("arbitrary",)),
    )(x, g.reshape(1, Dm), dh, dres)


def _loss_head(x, g, target, name):
    T, Dm = x.shape
    tm = min(512, T)

    def body(x_ref, g_ref, t_ref, loss_ref, dx_ref, dxb_ref, dg_ref):
        i = pl.program_id(0)
        xf = x_ref[...]
        gv = g_ref[...]
        r = lax.rsqrt(jnp.mean(xf * xf, axis=-1, keepdims=True) + RMS_EPS)
        xn = xf * r
        diff = xn * gv - t_ref[...]
        per_tok = jnp.mean(diff * diff, axis=-1, keepdims=True)
        lpart = 0.5 * jnp.sum(per_tok, axis=0, keepdims=True) + jnp.zeros((1, LANES), F32)
        dy = diff * (1.0 / Dm)
        dxn = dy * gv
        c = jnp.mean(dxn * xn, axis=-1, keepdims=True)
        dx = r * (dxn - xn * c)
        dx_ref[...] = dx
        dxb_ref[...] = dx.astype(BF16)
        part = jnp.sum(dy * xn, axis=0, keepdims=True)

        @pl.when(i == 0)
        def _():
            dg_ref[...] = part
            loss_ref[...] = lpart

        @pl.when(i > 0)
        def _():
            dg_ref[...] += part
            loss_ref[...] += lpart

    row = pl.BlockSpec((tm, Dm), lambda i: (i, 0))
    vec = pl.BlockSpec((1, Dm), lambda i: (0, 0))
    lsp = pl.BlockSpec((1, LANES), lambda i: (0, 0))
    return pl.pallas_call(
        body, name=name, grid=(T // tm,), in_specs=[row, vec, row], out_specs=[lsp, row, row, vec],
        out_shape=[jax.ShapeDtypeStruct((1, LANES), F32), jax.ShapeDtypeStruct((T, Dm), F32),
                   jax.ShapeDtypeStruct((T, Dm), BF16), jax.ShapeDtypeStruct((1, Dm), F32)],
        compiler_params=_cp(("arbitrary",)),
    )(x, g.reshape(1, Dm), target)


def _split_bf16(v):
    hi = v.astype(BF16)
    r1 = v - hi.astype(F32)
    mid = r1.astype(BF16)
    lo = (r1 - mid.astype(F32)).astype(BF16)
    return hi, mid, lo


def _tri_dot(tri, v):
    hi, mid, lo = _split_bf16(v)
    dot = functools.partial(jnp.dot, preferred_element_type=F32)
    return dot(tri, hi) + dot(tri, mid) + dot(tri, lo)


def _log_sigmoid(z):
    return jnp.minimum(z, 0.0) - jnp.log(1.0 + jnp.exp(-jnp.abs(z)))


def _fox_cumsum_fwd(f, bf, n_seq, name):
    T = f.shape[0]
    S = T // n_seq
    c = min(CUM_BLK, S)

    def body(f_ref, b_ref, out_ref):
        ri = lax.broadcasted_iota(jnp.int32, (c, c), 0)
        ci = lax.broadcasted_iota(jnp.int32, (c, c), 1)
        tri = (ri >= ci).astype(BF16)
        carry = jnp.zeros((1, LANES), F32)
        for j in range(S // c):
            lf = _log_sigmoid(f_ref[j * c:(j + 1) * c, :] + b_ref[...])
            out_ref[j * c:(j + 1) * c, :] = _tri_dot(tri, lf) + carry
            carry = carry + jnp.sum(lf, axis=0, keepdims=True)

    blk = pl.BlockSpec((S, LANES), lambda b: (b, 0))
    return pl.pallas_call(
        body, name=name, grid=(n_seq,), in_specs=[blk, pl.BlockSpec((1, LANES), lambda b: (0, 0))],
        out_specs=blk, out_shape=jax.ShapeDtypeStruct((T, LANES), F32),
        compiler_params=_cp(("parallel",)),
    )(f, bf)


def _fox_cumsum_bwd(f, bf, dF, n_seq, name):
    T = f.shape[0]
    S = T // n_seq
    c = min(CUM_BLK, S)

    def body(f_ref, b_ref, dF_ref, df_ref, db_ref):
        b = pl.program_id(0)
        ri = lax.broadcasted_iota(jnp.int32, (c, c), 0)
        ci = lax.broadcasted_iota(jnp.int32, (c, c), 1)
        tri = (ri <= ci).astype(BF16)
        carry = jnp.zeros((1, LANES), F32)
        dbp = jnp.zeros((1, LANES), F32)
        for j in reversed(range(S // c)):
            dFc = dF_ref[j * c:(j + 1) * c, :]
            dlf = _tri_dot(tri, dFc) + carry
            carry = carry + jnp.sum(dFc, axis=0, keepdims=True)
            z = f_ref[j * c:(j + 1) * c, :] + b_ref[...]
            dz = dlf * _sigmoid(-z)
            df_ref[j * c:(j + 1) * c, :] = dz.astype(BF16)
            dbp = dbp + jnp.sum(dz, axis=0, keepdims=True)

        @pl.when(b == 0)
        def _():
            db_ref[...] = dbp

        @pl.when(b > 0)
        def _():
            db_ref[...] += dbp

    blk = pl.BlockSpec((S, LANES), lambda b: (b, 0))
    vec = pl.BlockSpec((1, LANES), lambda b: (0, 0))
    return pl.pallas_call(
        body, name=name, grid=(n_seq,), in_specs=[blk, vec, blk], out_specs=[blk, vec],
        out_shape=[jax.ShapeDtypeStruct((T, LANES), BF16), jax.ShapeDtypeStruct((1, LANES), F32)],
        compiler_params=_cp(("arbitrary",)),
    )(f, bf, dF)


def _pair_masks():
    lane = lax.broadcasted_iota(jnp.int32, (1, LANES), 1)
    lo = lane < HEAD_DIM
    return lo, jnp.logical_not(lo)


def _attn_logits(q, k, fq, fk, sel, mask, scale):
    qm = jnp.where(sel, q, jnp.zeros_like(q))
    s = lax.dot_general(qm, k, (((1,), (1,)), ((), ())), preferred_element_type=F32) * scale
    s = s + fq - fk
    return jnp.where(mask, s, NEG_INF)


def _causal_mask(qi, ki, blk):
    row = qi * blk + lax.broadcasted_iota(jnp.int32, (blk, blk), 0)
    col = ki * blk + lax.broadcasted_iota(jnp.int32, (blk, blk), 1)
    return col <= row


def _attn_fwd(proj, Fq, Fk, n_seq, name):
    T = proj.shape[0]
    S = T // n_seq
    blk = min(ATT_BLK, S)
    nb = S // blk
    scale = HEAD_DIM ** -0.5
    qc, kc, vc = OFF_Q // LANES, OFF_K // LANES, OFF_V // LANES

    def body(q_ref, k_ref, v_ref, fq_ref, fk_ref, o_ref, o32_ref, lse_ref, m_s, l_s, acc_s):
        qi, ki = pl.program_id(2), pl.program_id(3)

        @pl.when(ki == 0)
        def _():
            m_s[...] = jnp.full_like(m_s, NEG_INF)
            l_s[...] = jnp.zeros_like(l_s)
            acc_s[...] = jnp.zeros_like(acc_s)

        @pl.when(ki <= qi)
        def _():
            q, k, v = q_ref[...], k_ref[...], v_ref[...]
            mask = _causal_mask(qi, ki, blk)
            for hh, sel in enumerate(_pair_masks()):
                s = _attn_logits(q, k, fq_ref[hh], fk_ref[hh], sel, mask, scale)
                m_prev = m_s[hh]
                m_new = jnp.maximum(m_prev, jnp.max(s, axis=-1, keepdims=True))
                alpha = jnp.exp(m_prev - m_new)
                p = jnp.exp(s - m_new)
                l_s[hh] = alpha * l_s[hh] + jnp.sum(p, axis=-1, keepdims=True)
                p_hi = p.astype(BF16)
                p_lo = (p - p_hi.astype(F32)).astype(BF16)
                pv = jnp.dot(p_hi, v, preferred_element_type=F32) + jnp.dot(p_lo, v, preferred_element_type=F32)
                acc_s[hh] = alpha * acc_s[hh] + pv
                m_s[hh] = m_new

        @pl.when(ki == qi)
        def _():
            lo, _ = _pair_masks()
            o = jnp.where(lo, acc_s[0] / l_s[0], acc_s[1] / l_s[1])
            o_ref[...] = o.astype(BF16)
            o32_ref[...] = o
            lse_ref[0] = m_s[0] + jnp.log(l_s[0])
            lse_ref[1] = m_s[1] + jnp.log(l_s[1])

    grid = (n_seq, N_HEADS // 2, nb, nb)
    return pl.pallas_call(
        body, name=name, grid=grid,
        in_specs=[
            pl.BlockSpec((blk, LANES), lambda b, j, qi, ki: (b * nb + qi, qc + j)),
            pl.BlockSpec((blk, LANES), lambda b, j, qi, ki: (b * nb + jnp.minimum(ki, qi), kc + j)),
            pl.BlockSpec((blk, LANES), lambda b, j, qi, ki: (b * nb + jnp.minimum(ki, qi), vc + j)),
            pl.BlockSpec((2, blk, 1), lambda b, j, qi, ki: (j, b * nb + qi, 0)),
            pl.BlockSpec((2, 1, blk), lambda b, j, qi, ki: (j, 0, b * nb + jnp.minimum(ki, qi))),
        ],
        out_specs=[
            pl.BlockSpec((blk, LANES), lambda b, j, qi, ki: (b * nb + qi, j)),
            pl.BlockSpec((blk, LANES), lambda b, j, qi, ki: (b * nb + qi, j)),
            pl.BlockSpec((2, blk, 1), lambda b, j, qi, ki: (j, b * nb + qi, 0)),
        ],
        out_shape=[jax.ShapeDtypeStruct((T, BRANCH_W), BF16), jax.ShapeDtypeStruct((T, BRANCH_W), F32),
                   jax.ShapeDtypeStruct((N_HEADS, T, 1), F32)],
        scratch_shapes=[pltpu.VMEM((2, blk, 1), F32), pltpu.VMEM((2, blk, 1), F32),
                        pltpu.VMEM((2, blk, LANES), F32)],
        compiler_params=_cp(("parallel", "parallel", "parallel", "arbitrary")),
    )(proj, proj, proj, Fq, Fk)


def _attn_delta(do, o, name):
    T = do.shape[0]
    tm = min(512, T)

    def body(do_ref, o_ref, d_ref):
        prod = do_ref[...].astype(F32) * o_ref[...].astype(F32)
        lo, hi = _pair_masks()
        for j in range(N_HEADS // 2):
            pj = prod[:, j * LANES:(j + 1) * LANES]
            d_ref[2 * j] = jnp.sum(jnp.where(lo, pj, 0.0), axis=-1, keepdims=True)
            d_ref[2 * j + 1] = jnp.sum(jnp.where(hi, pj, 0.0), axis=-1, keepdims=True)

    row = pl.BlockSpec((tm, BRANCH_W), lambda i: (i, 0))
    return pl.pallas_call(
        body, name=name, grid=(T // tm,), in_specs=[row, row],
        out_specs=pl.BlockSpec((N_HEADS, tm, 1), lambda i: (0, i, 0)),
        out_shape=jax.ShapeDtypeStruct((N_HEADS, T, 1), F32),
        compiler_params=_cp(("parallel",)),
    )(do, o)


def _attn_bwd_dq(proj, do, lse, delta, Fq, Fk, n_seq, name):
    T = proj.shape[0]
    S = T // n_seq
    blk = min(ATT_BLK, S)
    nb = S // blk
    scale = HEAD_DIM ** -0.5
    qc, kc, vc = OFF_Q // LANES, OFF_K // LANES, OFF_V // LANES

    def body(q_ref, k_ref, v_ref, do_ref, lse_ref, dl_ref, fq_ref, fk_ref, dq_ref, acc_s):
        qi, ki = pl.program_id(2), pl.program_id(3)

        @pl.when(ki == 0)
        def _():
            acc_s[...] = jnp.zeros_like(acc_s)

        @pl.when(ki <= qi)
        def _():
            q, k, v, do_ = q_ref[...], k_ref[...], v_ref[...], do_ref[...]
            mask = _causal_mask(qi, ki, blk)
            for hh, sel in enumerate(_pair_masks()):
                s = _attn_logits(q, k, fq_ref[hh], fk_ref[hh], sel, mask, scale)
                p = jnp.exp(s - lse_ref[hh])
                dom = jnp.where(sel, do_, jnp.zeros_like(do_))
                dp = lax.dot_general(dom, v, (((1,), (1,)), ((), ())), preferred_element_type=F32)
                ds = p * (dp - dl_ref[hh])
                acc_s[hh] += jnp.dot(ds.astype(BF16), k, preferred_element_type=F32)

        @pl.when(ki == qi)
        def _():
            lo, _ = _pair_masks()
            dq_ref[...] = (jnp.where(lo, acc_s[0], acc_s[1]) * scale).astype(BF16)

    qmap = lambda b, j, qi, ki: (b * nb + qi, j)
    col1 = pl.BlockSpec((2, blk, 1), lambda b, j, qi, ki: (j, b * nb + qi, 0))
    return pl.pallas_call(
        body, name=name, grid=(n_seq, N_HEADS // 2, nb, nb),
        in_specs=[
            pl.BlockSpec((blk, LANES), lambda b, j, qi, ki: (b * nb + qi, qc + j)),
            pl.BlockSpec((blk, LANES), lambda b, j, qi, ki: (b * nb + jnp.minimum(ki, qi), kc + j)),
            pl.BlockSpec((blk, LANES), lambda b, j, qi, ki: (b * nb + jnp.minimum(ki, qi), vc + j)),
            pl.BlockSpec((blk, LANES), qmap),
            col1, col1, col1,
            pl.BlockSpec((2, 1, blk), lambda b, j, qi, ki: (j, 0, b * nb + jnp.minimum(ki, qi))),
        ],
        out_specs=pl.BlockSpec((blk, LANES), qmap),
        out_shape=jax.ShapeDtypeStruct((T, BRANCH_W), BF16),
        scratch_shapes=[pltpu.VMEM((2, blk, LANES), F32)],
        compiler_params=_cp(("parallel", "parallel", "parallel", "arbitrary")),
    )(proj, proj, proj, do, lse, delta, Fq, Fk)


def _attn_bwd_dkv(proj, do, lse, delta, Fq, Fk, n_seq, name):
    T = proj.shape[0]
    S = T // n_seq
    blk = min(ATT_BLK, S)
    nb = S // blk
    scale = HEAD_DIM ** -0.5
    qc, kc, vc = OFF_Q // LANES, OFF_K // LANES, OFF_V // LANES
    tdot = functools.partial(lax.dot_general, dimension_numbers=(((0,), (0,)), ((), ())),
                             preferred_element_type=F32)

    def body(q_ref, k_ref, v_ref, do_ref, lse_ref, dl_ref, fq_ref, fk_ref, dk_ref, dv_ref, dfk_ref,
             dk_s, dv_s, df_s):
        ki, qi = pl.program_id(2), pl.program_id(3)

        @pl.when(qi == 0)
        def _():
            dk_s[...] = jnp.zeros_like(dk_s)
            dv_s[...] = jnp.zeros_like(dv_s)
            df_s[...] = jnp.zeros_like(df_s)

        @pl.when(qi >= ki)
        def _():
            q, k, v, do_ = q_ref[...], k_ref[...], v_ref[...], do_ref[...]
            mask = _causal_mask(qi, ki, blk)
            for hh, sel in enumerate(_pair_masks()):
                s = _attn_logits(q, k, fq_ref[hh], fk_ref[hh], sel, mask, scale)
                p = jnp.exp(s - lse_ref[hh])
                dv_s[hh] += tdot(p.astype(BF16), do_)
                dom = jnp.where(sel, do_, jnp.zeros_like(do_))
                dp = lax.dot_general(dom, v, (((1,), (1,)), ((), ())), preferred_element_type=F32)
                ds = p * (dp - dl_ref[hh])
                dk_s[hh] += tdot(ds.astype(BF16), q)
                df_s[hh] -= jnp.sum(ds, axis=0, keepdims=True)

        @pl.when(qi == nb - 1)
        def _():
            lo, _ = _pair_masks()
            dk_ref[...] = (jnp.where(lo, dk_s[0], dk_s[1]) * scale).astype(BF16)
            dv_ref[...] = jnp.where(lo, dv_s[0], dv_s[1]).astype(BF16)
            dfk_ref[...] = df_s[...]

    kmap = lambda b, j, ki, qi: (b * nb + ki, j)
    col1 = pl.BlockSpec((2, blk, 1), lambda b, j, ki, qi: (j, b * nb + jnp.maximum(qi, ki), 0))
    rowk = pl.BlockSpec((2, 1, blk), lambda b, j, ki, qi: (j, 0, b * nb + ki))
    return pl.pallas_call(
        body, name=name, grid=(n_seq, N_HEADS // 2, nb, nb),
        in_specs=[
            pl.BlockSpec((blk, LANES), lambda b, j, ki, qi: (b * nb + jnp.maximum(qi, ki), qc + j)),
            pl.BlockSpec((blk, LANES), lambda b, j, ki, qi: (b * nb + ki, kc + j)),
            pl.BlockSpec((blk, LANES), lambda b, j, ki, qi: (b * nb + ki, vc + j)),
            pl.BlockSpec((blk, LANES), lambda b, j, ki, qi: (b * nb + jnp.maximum(qi, ki), j)),
            col1, col1, col1, rowk,
        ],
        out_specs=[pl.BlockSpec((blk, LANES), kmap), pl.BlockSpec((blk, LANES), kmap), rowk],
        out_shape=[jax.ShapeDtypeStruct((T, BRANCH_W), BF16), jax.ShapeDtypeStruct((T, BRANCH_W), BF16),
                   jax.ShapeDtypeStruct((N_HEADS, 1, T), F32)],
        scratch_shapes=[pltpu.VMEM((2, blk, LANES), F32), pltpu.VMEM((2, blk, LANES), F32),
                        pltpu.VMEM((2, 1, blk), F32)],
        compiler_params=_cp(("parallel", "parallel", "parallel", "arbitrary")),
    )(proj, proj, proj, do, lse, delta, Fq, Fk)


def _shift_down(v, k, row):
    return jnp.where(row >= k, pltpu.roll(v, k, 0), 0.0)


def _shift_up(v, k, row, S):
    return jnp.where(row < S - k, pltpu.roll(v, S - k, 0), 0.0)


def _pool_diff(uf, w, row):
    acc, k = uf, 1
    while k < w:
        acc = acc + _shift_down(acc, k, row)
        k *= 2
    n = jnp.minimum(row + 1, w).astype(F32)
    return acc / n - uf


def _pool_fwd(proj, pool_w, pool_scale, n_seq, name):
    T = proj.shape[0]
    S = T // n_seq

    def body(u_ref, w_ref, sc_ref, o_ref):
        g = pl.program_id(1)
        row = lax.broadcasted_iota(jnp.int32, (S, POOL_GD), 0)
        uf = u_ref[...].astype(F32)
        d = _pool_diff(uf, POOL_WINDOWS[0], row)
        for gi in range(1, len(POOL_WINDOWS)):
            d = jnp.where(g == gi, _pool_diff(uf, POOL_WINDOWS[gi], row), d)
        e = jnp.dot(d.astype(BF16), w_ref[0], preferred_element_type=F32)
        o_ref[...] = (e * sc_ref[...]).astype(BF16)

    uc = OFF_U // POOL_GD
    return pl.pallas_call(
        body, name=name, grid=(n_seq, len(POOL_WINDOWS)),
        in_specs=[pl.BlockSpec((S, POOL_GD), lambda b, g: (b, uc + g)),
                  pl.BlockSpec((1, POOL_GD, POOL_GD), lambda b, g: (g, 0, 0)),
                  pl.BlockSpec((1, POOL_GD), lambda b, g: (0, g))],
        out_specs=pl.BlockSpec((S, POOL_GD), lambda b, g: (b, g)),
        out_shape=jax.ShapeDtypeStruct((T, BRANCH_W), BF16),
        compiler_params=_cp(("parallel", "parallel")),
    )(proj, pool_w, pool_scale)


def _pool_bwd(proj, dout, pool_w, pool_scale, n_seq, name):
    T = proj.shape[0]
    S = T // n_seq

    def body(u_ref, do_ref, w_ref, sc_ref, du_ref, dw_ref, dsc_ref):
        g, b = pl.program_id(0), pl.program_id(1)
        row = lax.broadcasted_iota(jnp.int32, (S, POOL_GD), 0)
        uf = u_ref[...].astype(F32)
        d = _pool_diff(uf, POOL_WINDOWS[0], row)
        for gi in range(1, len(POOL_WINDOWS)):
            d = jnp.where(g == gi, _pool_diff(uf, POOL_WINDOWS[gi], row), d)
        db16 = d.astype(BF16)
        w = w_ref[0]
        e = jnp.dot(db16, w, preferred_element_type=F32)
        dof = do_ref[...].astype(F32)
        dsc = jnp.sum(dof * e, axis=0, keepdims=True)
        de = (dof * sc_ref[...]).astype(BF16)
        dd = lax.dot_general(de, w, (((1,), (1,)), ((), ())), preferred_element_type=F32)
        dw = lax.dot_general(db16, de, (((0,), (0,)), ((), ())), preferred_element_type=F32)
        du = jnp.zeros_like(dd)
        for gi, wlen in enumerate(POOL_WINDOWS):
            n = jnp.minimum(row + 1, wlen).astype(F32)
            acc, k = dd / n, 1
            while k < wlen:
                acc = acc + _shift_up(acc, k, row, S)
                k *= 2
            du = jnp.where(g == gi, acc - dd, du)
        du_ref[...] = du.astype(BF16)

        @pl.when(b == 0)
        def _():
            dw_ref[0] = dw
            dsc_ref[...] = dsc

        @pl.when(b > 0)
        def _():
            dw_ref[0] += dw
            dsc_ref[...] += dsc

    uc = OFF_U // POOL_GD
    return pl.pallas_call(
        body, name=name, grid=(len(POOL_WINDOWS), n_seq),
        in_specs=[pl.BlockSpec((S, POOL_GD), lambda g, b: (b, uc + g)),
                  pl.BlockSpec((S, POOL_GD), lambda g, b: (b, g)),
                  pl.BlockSpec((1, POOL_GD, POOL_GD), lambda g, b: (g, 0, 0)),
                  pl.BlockSpec((1, POOL_GD), lambda g, b: (0, g))],
        out_specs=[pl.BlockSpec((S, POOL_GD), lambda g, b: (b, g)),
                   pl.BlockSpec((1, POOL_GD, POOL_GD), lambda g, b: (g, 0, 0)),
                   pl.BlockSpec((1, POOL_GD), lambda g, b: (0, g))],
        out_shape=[jax.ShapeDtypeStruct((T, BRANCH_W), BF16),
                   jax.ShapeDtypeStruct((len(POOL_WINDOWS), POOL_GD, POOL_GD), F32),
                   jax.ShapeDtypeStruct((1, BRANCH_W), F32)],
        compiler_params=_cp(("parallel", "arbitrary")),
    )(proj, dout, pool_w, pool_scale)


def _conv_fwd(proj, conv_w, n_seq, name):
    T = proj.shape[0]
    S = T // n_seq
    nc = BRANCH_W // LANES

    def body(cv_ref, cb_ref, cc_ref, w_ref, o_ref):
        row = lax.broadcasted_iota(jnp.int32, (S, LANES), 0)
        z = cc_ref[...].astype(F32) * cv_ref[...].astype(F32)
        w = w_ref[...]
        y = w[0:1] * _shift_down(z, 2, row) + w[1:2] * _shift_down(z, 1, row) + w[2:3] * z
        o_ref[...] = (cb_ref[...].astype(F32) * y).astype(BF16)

    def col(off):
        return pl.BlockSpec((S, LANES), lambda b, j: (b, off // LANES + j))

    return pl.pallas_call(
        body, name=name, grid=(n_seq, nc),
        in_specs=[col(OFF_CV), col(OFF_CB), col(OFF_CC), pl.BlockSpec((CONV_K, LANES), lambda b, j: (0, j))],
        out_specs=pl.BlockSpec((S, LANES), lambda b, j: (b, j)),
        out_shape=jax.ShapeDtypeStruct((T, BRANCH_W), BF16),
        compiler_params=_cp(("parallel", "parallel")),
    )(proj, proj, proj, conv_w)


def _conv_bwd(proj, dout, conv_w, n_seq, name):
    T = proj.shape[0]
    S = T // n_seq
    nc = BRANCH_W // LANES

    def body(cv_ref, cb_ref, cc_ref, do_ref, w_ref, dcv_ref, dcb_ref, dcc_ref, dw_ref):
        b = pl.program_id(1)
        row = lax.broadcasted_iota(jnp.int32, (S, LANES), 0)
        cv, cb, cc = cv_ref[...].astype(F32), cb_ref[...].astype(F32), cc_ref[...].astype(F32)
        dof = do_ref[...].astype(F32)
        w = w_ref[...]
        z = cc * cv
        z1, z2 = _shift_down(z, 1, row), _shift_down(z, 2, row)
        y = w[0:1] * z2 + w[1:2] * z1 + w[2:3] * z
        dcb_ref[...] = (dof * y).astype(BF16)
        dy = dof * cb
        dz = w[2:3] * dy + w[1:2] * _shift_up(dy, 1, row, S) + w[0:1] * _shift_up(dy, 2, row, S)
        dcc_ref[...] = (dz * cv).astype(BF16)
        dcv_ref[...] = (dz * cc).astype(BF16)
        dws = [jnp.sum(dy * zk, axis=0, keepdims=True) for zk in (z2, z1, z)]

        @pl.when(b == 0)
        def _():
            for kk in range(CONV_K):
                dw_ref[kk:kk + 1, :] = dws[kk]

        @pl.when(b > 0)
        def _():
            for kk in range(CONV_K):
                dw_ref[kk:kk + 1, :] += dws[kk]

    def col(off):
        return pl.BlockSpec((S, LANES), lambda j, b: (b, off // LANES + j))

    out = pl.BlockSpec((S, LANES), lambda j, b: (b, j))
    wsp = pl.BlockSpec((CONV_K, LANES), lambda j, b: (0, j))
    act = jax.ShapeDtypeStruct((T, BRANCH_W), BF16)
    return pl.pallas_call(
        body, name=name, grid=(nc, n_seq),
        in_specs=[col(OFF_CV), col(OFF_CB), col(OFF_CC), out, wsp],
        out_specs=[out, out, out, wsp],
        out_shape=[act, act, act, jax.ShapeDtypeStruct((CONV_K, BRANCH_W), F32)],
        compiler_params=_cp(("parallel", "arbitrary")),
    )(proj, proj, proj, dout, conv_w)


def _mix_fwd(oa, ob, oc, wpa, wpp, wpc, proj, b_gate, name):
    T = oa.shape[0]
    tm = min(256, T)

    def body(oa_ref, ob_ref, oc_ref, wa_ref, wp_ref, wc_ref, g_ref, bg_ref, o_ref):
        acc = jnp.zeros((tm, D_MODEL), F32)
        for i, (x_ref, w_ref) in enumerate(((oa_ref, wa_ref), (ob_ref, wp_ref), (oc_ref, wc_ref))):
            y = jnp.dot(x_ref[...], w_ref[...], preferred_element_type=F32)
            sl = slice(i * D_MODEL, (i + 1) * D_MODEL)
            acc = acc + _sigmoid(g_ref[:, sl].astype(F32) + bg_ref[:, sl]) * y
        o_ref[...] = acc.astype(BF16)

    br = pl.BlockSpec((tm, BRANCH_W), lambda i: (i, 0))
    wsp = pl.BlockSpec((BRANCH_W, D_MODEL), lambda i: (0, 0))
    return pl.pallas_call(
        body, name=name, grid=(T // tm,),
        in_specs=[br, br, br, wsp, wsp, wsp, pl.BlockSpec((tm, GATE_W), lambda i: (i, 0)),
                  pl.BlockSpec((1, GATE_W), lambda i: (0, 0))],
        out_specs=pl.BlockSpec((tm, D_MODEL), lambda i: (i, 0)),
        out_shape=jax.ShapeDtypeStruct((T, D_MODEL), BF16),
        compiler_params=_cp(("parallel",)),
    )(oa, ob, oc, wpa, wpp, wpc, proj, b_gate)


def _mix_bwd(oa, ob, oc, wpa, wpp, wpc, proj, b_gate, dmixed, name):
    T = oa.shape[0]
    tm = min(256, T)

    def body(oa_ref, ob_ref, oc_ref, wa_ref, wp_ref, wc_ref, g_ref, bg_ref, dm_ref,
             dya_ref, dyb_ref, dyc_ref, dg_ref, dbg_ref):
        i0 = pl.program_id(0)
        dm = dm_ref[...].astype(F32)
        parts = []
        for i, (x_ref, w_ref, dy_ref) in enumerate(((oa_ref, wa_ref, dya_ref), (ob_ref, wp_ref, dyb_ref),
                                                    (oc_ref, wc_ref, dyc_ref))):
            y = jnp.dot(x_ref[...], w_ref[...], preferred_element_type=F32)
            sl = slice(i * D_MODEL, (i + 1) * D_MODEL)
            gate = _sigmoid(g_ref[:, sl].astype(F32) + bg_ref[:, sl])
            dy_ref[...] = (dm * gate).astype(BF16)
            dgl = dm * y * gate * (1.0 - gate)
            dg_ref[:, sl] = dgl.astype(BF16)
            parts.append(jnp.sum(dgl, axis=0, keepdims=True))

        @pl.when(i0 == 0)
        def _():
            for i in range(3):
                dbg_ref[:, i * D_MODEL:(i + 1) * D_MODEL] = parts[i]

        @pl.when(i0 > 0)
        def _():
            for i in range(3):
                dbg_ref[:, i * D_MODEL:(i + 1) * D_MODEL] += parts[i]

    br = pl.BlockSpec((tm, BRANCH_W), lambda i: (i, 0))
    wsp = pl.BlockSpec((BRANCH_W, D_MODEL), lambda i: (0, 0))
    row = pl.BlockSpec((tm, D_MODEL), lambda i: (i, 0))
    gsp = pl.BlockSpec((tm, GATE_W), lambda i: (i, 0))
    bsp = pl.BlockSpec((1, GATE_W), lambda i: (0, 0))
    act = jax.ShapeDtypeStruct((T, D_MODEL), BF16)
    return pl.pallas_call(
        body, name=name, grid=(T // tm,),
        in_specs=[br, br, br, wsp, wsp, wsp, gsp, bsp, row],
        out_specs=[row, row, row, gsp, bsp],
        out_shape=[act, act, act, jax.ShapeDtypeStruct((T, GATE_W), BF16),
                   jax.ShapeDtypeStruct((1, GATE_W), F32)],
        compiler_params=_cp(("arbitrary",)),
    )(oa, ob, oc, wpa, wpp, wpc, proj, b_gate, dmixed)


def _swiglu_fwd(ab, name):
    T = ab.shape[0]
    tm = min(256, T)

    def body(ab_ref, o_ref):
        a = ab_ref[:, :FFN_HIDDEN].astype(F32)
        o_ref[...] = (a * _sigmoid(a) * ab_ref[:, FFN_HIDDEN:].astype(F32)).astype(BF16)

    return pl.pallas_call(
        body, name=name, grid=(T // tm,),
        in_specs=[pl.BlockSpec((tm, 2 * FFN_HIDDEN), lambda i: (i, 0))],
        out_specs=pl.BlockSpec((tm, FFN_HIDDEN), lambda i: (i, 0)),
        out_shape=jax.ShapeDtypeStruct((T, FFN_HIDDEN), BF16),
        compiler_params=_cp(("parallel",)),
    )(ab)


def _swiglu_bwd(ab, ds, name):
    T = ab.shape[0]
    tm = min(256, T)

    def body(ab_ref, ds_ref, o_ref):
        a = ab_ref[:, :FFN_HIDDEN].astype(F32)
        b = ab_ref[:, FFN_HIDDEN:].astype(F32)
        dsf = ds_ref[...].astype(F32)
        sg = _sigmoid(a)
        o_ref[:, :FFN_HIDDEN] = (dsf * b * sg * (1.0 + a * (1.0 - sg))).astype(BF16)
        o_ref[:, FFN_HIDDEN:] = (dsf * a * sg).astype(BF16)

    full = pl.BlockSpec((tm, 2 * FFN_HIDDEN), lambda i: (i, 0))
    return pl.pallas_call(
        body, name=name, grid=(T // tm,),
        in_specs=[full, pl.BlockSpec((tm, FFN_HIDDEN), lambda i: (i, 0))],
        out_specs=full,
        out_shape=jax.ShapeDtypeStruct((T, 2 * FFN_HIDDEN), BF16),
        compiler_params=_cp(("parallel",)),
    )(ab, ds)


def _adamw(w, g, m, v, name):
    R, C = w.shape
    tr = R
    for cand in (256, 352, 128, 64, 8):
        if R > cand and R % cand == 0:
            tr = cand
            break

    def body(w_ref, g_ref, m_ref, v_ref, d_ref, nm_ref, nv_ref):
        gv = g_ref[...]
        nm = ADAM_B1 * m_ref[...] + (1.0 - ADAM_B1) * gv
        nv = ADAM_B2 * v_ref[...] + (1.0 - ADAM_B2) * (gv * gv)
        m_hat = nm / (1.0 - ADAM_B1 ** ADAM_STEP)
        v_hat = nv / (1.0 - ADAM_B2 ** ADAM_STEP)
        d_ref[...] = -ADAM_LR * (m_hat / (jnp.sqrt(v_hat) + ADAM_EPS) + ADAM_WD * w_ref[...])
        nm_ref[...] = nm
        nv_ref[...] = nv

    blk = pl.BlockSpec((tr, C), lambda i: (i, 0))
    sh = jax.ShapeDtypeStruct((R, C), F32)
    return pl.pallas_call(
        body, name=name, grid=(R // tr,), in_specs=[blk] * 4, out_specs=[blk] * 3, out_shape=[sh] * 3,
        compiler_params=_cp(("parallel",)),
    )(w, g, m, v)


def _sum_slabs(x, name):
    n, R, C = x.shape
    tr = R
    for cand in (512, 256, 128, 64, 32, 16, 8):
        if R > cand and R % cand == 0:
            tr = cand
            break

    def body(x_ref, o_ref):
        acc = x_ref[0].astype(F32)
        for j in range(1, n):
            acc = acc + x_ref[j].astype(F32)
        o_ref[...] = acc

    return pl.pallas_call(
        body, name=name, grid=(R // tr,), in_specs=[pl.BlockSpec((n, tr, C), lambda i: (0, i, 0))],
        out_specs=pl.BlockSpec((tr, C), lambda i: (i, 0)), out_shape=jax.ShapeDtypeStruct((R, C), F32),
        compiler_params=_cp(("parallel",)),
    )(x)


def _multi_gather(xs, layers, name):
    nt = len(xs)
    shapes = [x.shape if lay is None else x.shape[1:] for x, lay in zip(xs, layers)]

    def body(*refs):
        x_refs, out_refs = refs[:nt], refs[nt:2 * nt]
        send_sems, recv_sems, local_sems = refs[2 * nt:]
        x_, y_, c_ = lax.axis_index("x"), lax.axis_index("y"), lax.axis_index("c")
        me, sibling = (x_, y_, c_), (x_, y_, 1 - c_)
        chips = [(1 - x_, y_), (x_, 1 - y_), (1 - x_, 1 - y_)]

        def own_block(t):
            return x_refs[t] if layers[t] is None else x_refs[t].at[layers[t]]

        def copy(t, k, block, to, own=False):
            px, py, pc = block
            dst = out_refs[t].at[4 * px + 2 * py + pc]
            return pltpu.make_async_remote_copy(
                src_ref=own_block(t) if own else dst, dst_ref=dst,
                send_sem=send_sems.at[t, k], recv_sem=recv_sems.at[t, k],
                device_id=to, device_id_type=pl.DeviceIdType.MESH)

        mine, first, passed = [], [], []
        for t in range(nt):
            mine.append(pltpu.make_async_copy(own_block(t), out_refs[t].at[4 * x_ + 2 * y_ + c_], local_sems.at[t]))
            mine[-1].start()
            first.append([copy(t, 1 + j, me, (*chip, c_), own=True) for j, chip in enumerate(chips)]
                         + [copy(t, 0, me, sibling, own=True)])
            for cp in first[-1]:
                cp.start()
        for t in range(nt):
            for j, chip in enumerate(chips):
                copy(t, 1 + j, (*chip, c_), me).wait_recv()
                passed.append(copy(t, 4 + j, (*chip, c_), sibling))
                passed[-1].start()
        for t in range(nt):
            copy(t, 0, sibling, me).wait_recv()
            for j, chip in enumerate(chips):
                copy(t, 4 + j, (*chip, 1 - c_), me).wait_recv()
        for cp in [c for f in first for c in f] + passed:
            cp.wait_send()
        for cp in mine:
            cp.wait()

    hbm = pl.BlockSpec(memory_space=pl.ANY)
    return pl.pallas_call(
        body, name=name, out_shape=[jax.ShapeDtypeStruct((N_DEV,) + tuple(s), x.dtype) for s, x in zip(shapes, xs)],
        in_specs=[hbm] * nt, out_specs=[hbm] * nt,
        scratch_shapes=[pltpu.SemaphoreType.DMA((nt, 7)), pltpu.SemaphoreType.DMA((nt, 7)),
                        pltpu.SemaphoreType.DMA((nt,))],
    )(*xs)


def _multi_exchange(sends, name):
    nt = len(sends)

    def body(*refs):
        s_refs, r_refs = refs[:nt], refs[nt:2 * nt]
        send_sems, recv_sems, local_sems = refs[2 * nt:]
        x_, y_, c_ = lax.axis_index("x"), lax.axis_index("y"), lax.axis_index("c")
        me = 4 * x_ + 2 * y_ + c_
        mine, out, inc = [], [], []
        for t in range(nt):
            mine.append(pltpu.make_async_copy(s_refs[t].at[me], r_refs[t].at[me], local_sems.at[t]))
            mine[-1].start()
        for k in (2, 4, 6, 3, 5, 7, 1):
            px, py, pc = x_ ^ ((k >> 2) & 1), y_ ^ ((k >> 1) & 1), c_ ^ (k & 1)
            peer = 4 * px + 2 * py + pc
            for t in range(nt):
                def copy(src, dst):
                    return pltpu.make_async_remote_copy(
                        src_ref=s_refs[t].at[src], dst_ref=r_refs[t].at[dst],
                        send_sem=send_sems.at[t, k - 1], recv_sem=recv_sems.at[t, k - 1],
                        device_id=(px, py, pc), device_id_type=pl.DeviceIdType.MESH)

                out.append(copy(peer, me))
                inc.append(copy(me, peer))
        for cp in out:
            cp.start()
        for cp in inc:
            cp.wait_recv()
        for cp in out:
            cp.wait_send()
        for cp in mine:
            cp.wait()

    hbm = pl.BlockSpec(memory_space=pl.ANY)
    return pl.pallas_call(
        body, name=name, out_shape=[jax.ShapeDtypeStruct(s.shape, s.dtype) for s in sends],
        in_specs=[hbm] * nt, out_specs=[hbm] * nt,
        scratch_shapes=[pltpu.SemaphoreType.DMA((nt, N_DEV - 1)), pltpu.SemaphoreType.DMA((nt, N_DEV - 1)),
                        pltpu.SemaphoreType.DMA((nt,))],
    )(*sends)


_HBM = pl.BlockSpec(memory_space=pltpu.HBM)
_SEM = pl.BlockSpec(memory_space=pltpu.SEMAPHORE)
_PEER_ORDER = (2, 4, 6, 3, 5, 7, 1)


def _split_copies(src_refs, land_refs, send_sems, recv_sems, layers, per_peer):
    x_, y_, c_ = lax.axis_index("x"), lax.axis_index("y"), lax.axis_index("c")
    me = 4 * x_ + 2 * y_ + c_
    copies = []
    for k in _PEER_ORDER:
        px, py, pc = x_ ^ ((k >> 2) & 1), y_ ^ ((k >> 1) & 1), c_ ^ (k & 1)
        peer = 4 * px + 2 * py + pc
        for t in range(len(src_refs)):
            if per_peer:
                src = src_refs[t].at[peer]
            else:
                src = src_refs[t] if layers[t] is None else src_refs[t].at[layers[t]]
            copies.append(pltpu.make_async_remote_copy(
                src_ref=src, dst_ref=land_refs[t].at[me],
                send_sem=send_sems.at[t * (N_DEV - 1) + k - 1], recv_sem=recv_sems.at[t * (N_DEV - 1) + k - 1],
                device_id=(px, py, pc), device_id_type=pl.DeviceIdType.MESH))
    return copies


def _split_start(srcs, layers, per_peer, after, name):
    nt = len(srcs)
    if per_peer:
        land_shapes = [s.shape for s in srcs]
    else:
        land_shapes = [(N_DEV,) + tuple(s.shape if lay is None else s.shape[1:]) for s, lay in zip(srcs, layers)]

    def body(*refs):
        src_refs, land_refs = refs[:nt], refs[nt:2 * nt]
        send_sems, recv_sems = refs[2 * nt + 1], refs[2 * nt + 2]
        token = refs[-1]
        for cp in _split_copies(src_refs, land_refs, send_sems, recv_sems, layers, per_peer):
            cp.start()
        token[...] = jnp.zeros_like(token)

    lands = [pltpu.with_memory_space_constraint(lax.empty(s, x.dtype), pltpu.HBM) for s, x in zip(land_shapes, srcs)]
    srcs = [pltpu.with_memory_space_constraint(x, pltpu.HBM) for x in srcs]
    out = pl.pallas_call(
        body, name=name,
        out_shape=(pltpu.SemaphoreType.DMA((nt * (N_DEV - 1),)), pltpu.SemaphoreType.DMA((nt * (N_DEV - 1),)),
                   *[pltpu.HBM(x.shape, x.dtype) for x in srcs], *[pltpu.HBM(s, x.dtype) for s, x in zip(land_shapes, srcs)],
                   jax.ShapeDtypeStruct((8, LANES), F32)),
        in_specs=[_HBM] * (2 * nt) + [pl.BlockSpec(memory_space=pl.ANY)],
        out_specs=(_SEM, _SEM, *([_HBM] * (2 * nt)), pl.BlockSpec(memory_space=pltpu.VMEM)),
        input_output_aliases={i: 2 + i for i in range(2 * nt)},
        compiler_params=pltpu.CompilerParams(has_side_effects=pltpu.SideEffectType.DATAFLOW_SIDE_EFFECTING),
    )(*srcs, *lands, after)
    return out[0], out[1], list(out[2:2 + nt]), list(out[2 + nt:2 + 2 * nt]), out[-1]


def _split_wait(started, layers, per_peer, after, name):
    send_sems, recv_sems, srcs, lands, _ = started
    nt = len(srcs)

    def body(*refs):
        src_refs, land_refs = refs[:nt], refs[nt:2 * nt]
        s_sems, r_sems = refs[2 * nt], refs[2 * nt + 1]
        for cp in _split_copies(src_refs, land_refs, s_sems, r_sems, layers, per_peer):
            cp.wait_send()
            cp.wait_recv()

    out = pl.pallas_call(
        body, name=name,
        out_shape=tuple(pltpu.HBM(x.shape, x.dtype) for x in srcs + lands),
        in_specs=[_HBM] * (2 * nt) + [_SEM, _SEM, pl.BlockSpec(memory_space=pl.ANY)],
        out_specs=tuple([_HBM] * (2 * nt)),
        input_output_aliases={i: i for i in range(2 * nt)},
        compiler_params=pltpu.CompilerParams(has_side_effects=pltpu.SideEffectType.DATAFLOW_SIDE_EFFECTING),
    )(*srcs, *lands, send_sems, recv_sems, after)
    return list(out[nt:])


def _with_own(land, own):
    me = 4 * lax.axis_index("x") + 2 * lax.axis_index("y") + lax.axis_index("c")
    return lax.dynamic_update_slice_in_dim(land, own[None], me, axis=0)


def _runs(mapping):
    runs, c, n = [], 0, len(mapping)
    while c < n:
        if mapping[c] is None:
            c += 1
            continue
        sid, d, lo = mapping[c][0], mapping[c][1] - c, c
        while c < n and mapping[c] is not None and mapping[c][0] == sid and mapping[c][1] - c == d:
            c += 1
        runs.append((lo, c, sid, d))
    return runs


def _tile_plan(mapping, src_widths):
    runs = _runs(mapping)
    plan = []
    for t in range(len(mapping) // LANES):
        pieces = []
        for lo, hi, sid, d in runs:
            lo_t, hi_t = max(lo, t * LANES), min(hi, (t + 1) * LANES)
            if lo_t >= hi_t:
                continue
            a = ((lo_t + d) // LANES) * LANES
            win = min(2 * LANES, src_widths[sid] - a)
            shift = t * LANES + d - a
            pieces.append((sid, a, win, shift, lo_t - t * LANES, hi_t - t * LANES))
        plan.append(pieces)
    return plan


def _reblock(srcs, src_views, outs, out_views, name):
    R = srcs[0].shape[-2]
    tr = min(256, R)
    widths = {sid: srcs[ai].shape[-1] for sid, (ai, _) in src_views.items()}
    plans = [(ai, li, _tile_plan(mapping, widths)) for ai, li, mapping in out_views]
    ns = len(srcs)

    def body(*refs):
        s_refs, o_refs = refs[:ns], refs[ns:]
        cache = {}

        def shift_matrix(win, shift, lo, hi):
            key = (win, shift, lo, hi)
            if key not in cache:
                r = lax.broadcasted_iota(jnp.int32, (win, LANES), 0)
                c = lax.broadcasted_iota(jnp.int32, (win, LANES), 1)
                hit = jnp.logical_and(r - c == shift, jnp.logical_and(c >= lo, c < hi))
                cache[key] = jnp.where(hit, 1.0, 0.0).astype(BF16)
            return cache[key]

        for ai, li, plan in plans:
            for t, pieces in enumerate(plan):
                acc = None
                for sid, a, win, shift, lo, hi in pieces:
                    sa, sl = src_views[sid]
                    src = s_refs[sa][:, a:a + win] if sl is None else s_refs[sa][sl, :, a:a + win]
                    part = jnp.dot(src, shift_matrix(win, shift, lo, hi), preferred_element_type=F32)
                    acc = part if acc is None else acc + part
                val = jnp.zeros((tr, LANES), BF16) if acc is None else acc.astype(BF16)
                if li is None:
                    o_refs[ai][:, t * LANES:(t + 1) * LANES] = val
                else:
                    o_refs[ai][li, :, t * LANES:(t + 1) * LANES] = val

    def spec(shape):
        if len(shape) == 2:
            return pl.BlockSpec((tr, shape[1]), lambda i: (i, 0))
        return pl.BlockSpec((shape[0], tr, shape[2]), lambda i: (0, i, 0))

    return pl.pallas_call(
        body, name=name, grid=(R // tr,), in_specs=[spec(s.shape) for s in srcs],
        out_specs=[spec(s) for s in outs], out_shape=[jax.ShapeDtypeStruct(s, BF16) for s in outs],
        compiler_params=_cp(("parallel",)),
    )(*srcs)


SHARDED = ("w_in", "w_gate_up", "w_proj_attn", "w_proj_pool", "w_proj_conv", "w_out", "w_down")
WEIGHT_ORDER = ("attn_norm", "w_in", "b_forget", "b_gate", "w_proj_attn", "pool_w", "pool_scale", "w_proj_pool",
                "conv_w", "w_proj_conv", "w_out", "ffn_norm", "w_gate_up", "w_down", "final_norm")
IN_SHARD, IN_SHARD_PAD = IN_COLS // N_DEV, 896
GU_SHARD, GU_SHARD_PAD = 2 * FFN_HIDDEN // N_DEV, 768


def _w_in_col(c):
    if c < GATE_W:
        return c + 3592
    if c < OFF_U:
        return c - OFF_Q
    return c - OFF_U + 1544


def _w_in_full(gathered, name):
    main = [divmod(_w_in_col(c), IN_SHARD) for c in range(MAIN_COLS)]
    fcols = [divmod(1536 + c, IN_SHARD) if c < N_HEADS else None for c in range(LANES)]
    R = gathered.shape[1]
    return _reblock([gathered], {i: (0, i) for i in range(N_DEV)}, [(R, MAIN_COLS), (R, LANES)],
                    [(0, None, main), (1, None, fcols)], name)


def _w_in_slabs(dmain, dwf, name):
    inv = {_w_in_col(c): ("m", c) for c in range(MAIN_COLS)}
    inv.update({1536 + c: ("f", c) for c in range(N_HEADS)})
    views = []
    for i in range(N_DEV):
        mapping = [inv[IN_SHARD * i + j] if j < IN_SHARD else None for j in range(IN_SHARD_PAD)]
        views.append((0, i, mapping))
    R = dmain.shape[0]
    return _reblock([dmain, dwf], {"m": (0, None), "f": (1, None)}, [(N_DEV, R, IN_SHARD_PAD)], views, name)[0]


def _w_gu_full(gathered, name):
    mapping = [divmod(c, GU_SHARD) for c in range(2 * FFN_HIDDEN)]
    R = gathered.shape[1]
    return _reblock([gathered], {i: (0, i) for i in range(N_DEV)}, [(R, 2 * FFN_HIDDEN)], [(0, None, mapping)], name)[0]


def _w_gu_slabs(dw, name):
    views = [(0, i, [("w", GU_SHARD * i + j) if j < GU_SHARD else None for j in range(GU_SHARD_PAD)])
             for i in range(N_DEV)]
    R = dw.shape[0]
    return _reblock([dw], {"w": (0, None)}, [(N_DEV, R, GU_SHARD_PAD)], views, name)[0]


def _layer_fwd(x, W, n_seq, l):
    T = x.shape[0]
    sfx = f"_l{l}"
    h1 = _rms_fwd(x, W["attn_norm"], "rms1" + sfx)
    proj = _matmul(h1, W["w_main"], mode="nn", out_dtype=BF16, name="proj_main" + sfx)
    f = _matmul(h1, W["w_f"], mode="nn", out_dtype=F32, name="proj_f" + sfx)
    Fc = _fox_cumsum_fwd(f, W["b_forget"], n_seq, "fox_cumsum" + sfx)
    F8 = Fc[:, :N_HEADS].T
    Fq, Fk = F8.reshape(N_HEADS, T, 1), F8.reshape(N_HEADS, 1, T)
    oa, oa32, lse = _attn_fwd(proj, Fq, Fk, n_seq, "attn_fwd" + sfx)
    ob = _pool_fwd(proj, W["pool_w"], W["pool_scale"], n_seq, "pool_fwd" + sfx)
    oc = _conv_fwd(proj, W["conv_w"], n_seq, "conv_fwd" + sfx)
    mixed = _mix_fwd(oa, ob, oc, W["w_proj_attn"], W["w_proj_pool"], W["w_proj_conv"], proj, W["b_gate"],
                     "mix_fwd" + sfx)
    x2 = _matmul(mixed, W["w_out"], mode="nn", out_dtype=F32, name="out_proj" + sfx, residual=x)
    h2 = _rms_fwd(x2, W["ffn_norm"], "rms2" + sfx)
    ab = _matmul(h2, W["w_gate_up"], mode="nn", out_dtype=BF16, name="gate_up" + sfx)
    s = _swiglu_fwd(ab, "swiglu_fwd" + sfx)
    x3 = _matmul(s, W["w_down"], mode="nn", out_dtype=F32, name="down" + sfx, tm=1024, tn=1024, tk=1408,
                 residual=x2)
    saved = dict(x=x, h1=h1, proj=proj, f=f, Fq=Fq, Fk=Fk, oa=oa, oa32=oa32, lse=lse, ob=ob, oc=oc, mixed=mixed, x2=x2,
                 h2=h2, ab=ab, s=s)
    return x3, saved


def _layer_bwd(dx3, dx3b, W, sv, n_seq, l):
    T = dx3.shape[0]
    sfx = f"_l{l}"
    G = {}
    ds = _matmul(dx3b, W["w_down"], mode="nt", out_dtype=BF16, name="d_s" + sfx, tm=1024, tn=1408)
    G["w_down"] = _matmul(sv["s"], dx3b, mode="tn", out_dtype=BF16, name="dw_down" + sfx, tm=256, tn=1024)
    dab = _swiglu_bwd(sv["ab"], ds, "swiglu_bwd" + sfx)
    dh2 = _matmul(dab, W["w_gate_up"], mode="nt", out_dtype=BF16, name="d_h2" + sfx, tm=1024, tn=1024, tk=1408)
    G["w_gate_up"] = _matmul(sv["h2"], dab, mode="tn", out_dtype=BF16, name="dw_gate_up" + sfx, tm=1024)
    dx2, dx2b, G["ffn_norm"] = _rms_bwd(sv["x2"], W["ffn_norm"], dh2, dx3, "rms2_bwd" + sfx)
    dmixed = _matmul(dx2b, W["w_out"], mode="nt", out_dtype=BF16, name="d_mixed" + sfx)
    G["w_out"] = _matmul(sv["mixed"], dx2b, mode="tn", out_dtype=BF16, name="dw_out" + sfx, tm=1024)
    dya, dyb, dyc, dg, G["b_gate"] = _mix_bwd(sv["oa"], sv["ob"], sv["oc"], W["w_proj_attn"], W["w_proj_pool"],
                                              W["w_proj_conv"], sv["proj"], W["b_gate"], dmixed, "mix_bwd" + sfx)
    douts = {}
    for br, dy, o in (("attn", dya, sv["oa"]), ("pool", dyb, sv["ob"]), ("conv", dyc, sv["oc"])):
        douts[br] = _matmul(dy, W["w_proj_" + br], mode="nt", out_dtype=BF16, name=f"d_{br}_out" + sfx)
        G["w_proj_" + br] = _matmul(o, dy, mode="tn", out_dtype=BF16, name=f"dw_proj_{br}" + sfx, tm=512)
    dcv, dcb, dcc, G["conv_w"] = _conv_bwd(sv["proj"], douts["conv"], W["conv_w"], n_seq, "conv_bwd" + sfx)
    du, G["pool_w"], G["pool_scale"] = _pool_bwd(sv["proj"], douts["pool"], W["pool_w"], W["pool_scale"], n_seq,
                                                 "pool_bwd" + sfx)
    delta = _attn_delta(douts["attn"], sv["oa32"], "attn_delta" + sfx)
    dq = _attn_bwd_dq(sv["proj"], douts["attn"], sv["lse"], delta, sv["Fq"], sv["Fk"], n_seq, "attn_dq" + sfx)
    dk, dv, dFk = _attn_bwd_dkv(sv["proj"], douts["attn"], sv["lse"], delta, sv["Fq"], sv["Fk"], n_seq,
                                "attn_dkv" + sfx)
    dF = jnp.pad(dFk.reshape(N_HEADS, T).T, ((0, 0), (0, LANES - N_HEADS)))
    df, G["b_forget"] = _fox_cumsum_bwd(sv["f"], W["b_forget"], dF, n_seq, "fox_cumsum_bwd" + sfx)
    dproj = jnp.concatenate([dg, dq, dk, dv, du, dcv, dcb, dcc], axis=1)
    dh1 = _matmul(dproj, W["w_main"], mode="nt", out_dtype=F32, name="d_h1_main" + sfx, tm=1024, tn=1024, tk=1664)
    dh1 = _matmul(df, W["w_f"], mode="nt", out_dtype=F32, name="d_h1_f" + sfx, residual=dh1)
    G["w_main"] = _matmul(sv["h1"], dproj, mode="tn", out_dtype=BF16, name="dw_main" + sfx, tm=1024)
    G["w_f"] = _matmul(sv["h1"], df, mode="tn", out_dtype=BF16, name="dw_f" + sfx, tm=1024)
    dx, dxb, G["attn_norm"] = _rms_bwd(sv["x"], W["attn_norm"], dh1, dx2, "rms1_bwd" + sfx)
    return dx, dxb, G


def _replicated_operands(rep, l):
    W = {}
    W["attn_norm"], W["ffn_norm"] = rep["attn_norm"][l], rep["ffn_norm"][l]
    W["b_forget"] = jnp.pad(rep["b_forget"][l].reshape(1, N_HEADS), ((0, 0), (0, LANES - N_HEADS)))
    W["b_gate"] = rep["b_gate"][l].reshape(1, GATE_W)
    W["pool_w"] = rep["pool_w"][l].astype(BF16)
    W["pool_scale"] = rep["pool_scale"][l].reshape(1, BRANCH_W)
    return W


def _local_step(x, target, get_W, final_norm, on_layer_grads=None):
    n_seq, S, Dm = x.shape
    T = n_seq * S
    xt = x.reshape(T, Dm)
    saved, Ws = [], []
    for l in range(DEPTH):
        Ws.append(get_W(l, xt))
        xt, sv = _layer_fwd(xt, Ws[l], n_seq, l)
        saved.append(sv)
    loss, dx, dxb, g_final = _loss_head(xt, final_norm, target.reshape(T, Dm), "loss_head")
    grads = [None] * DEPTH
    for l in reversed(range(DEPTH)):
        dx, dxb, grads[l] = _layer_bwd(dx, dxb, Ws[l], saved[l], n_seq, l)
        if on_layer_grads is not None:
            grads[l] = on_layer_grads(l, grads[l], Ws)
    return loss, dx.reshape(n_seq, S, Dm), grads, g_final


def _padded_shards(weights):
    sh = {n: weights[n].astype(BF16) for n in SHARDED}
    sh["w_in"] = jnp.pad(sh["w_in"], ((0, 0), (0, 0), (0, IN_SHARD_PAD - IN_SHARD)))
    sh["w_gate_up"] = jnp.pad(sh["w_gate_up"], ((0, 0), (0, 0), (0, GU_SHARD_PAD - GU_SHARD)))
    return sh


def _gather_layer(sh, conv_w, l):
    names = list(SHARDED)
    xs, layers = [sh[n] for n in names], [l] * len(names)
    if l == 0:
        xs, layers = xs + [conv_w], layers + [None]
    got = _multi_gather(xs, layers, f"gather_weights_l{l}")
    return _full_operands(dict(zip(names, got)), l), (got[-1] if l == 0 else None)


def _full_operands(g, l):
    W = {}
    W["w_main"], W["w_f"] = _w_in_full(g["w_in"], f"w_in_full_l{l}")
    W["w_gate_up"] = _w_gu_full(g["w_gate_up"], f"w_gate_up_full_l{l}")
    for n in ("w_proj_attn", "w_proj_pool", "w_proj_conv"):
        W[n] = jnp.transpose(g[n], (1, 0, 2)).reshape(BRANCH_W, D_MODEL)
    W["w_out"] = g["w_out"].reshape(D_MODEL, D_MODEL)
    W["w_down"] = g["w_down"].reshape(FFN_HIDDEN, D_MODEL)
    return W


def _grad_slabs(G, l):
    slabs = {
        "w_in": _w_in_slabs(G["w_main"], G["w_f"], f"w_in_slabs_l{l}"),
        "w_gate_up": _w_gu_slabs(G["w_gate_up"], f"w_gate_up_slabs_l{l}"),
        "w_out": G["w_out"].reshape(N_DEV, D_MODEL // N_DEV, D_MODEL),
        "w_down": G["w_down"].reshape(N_DEV, FFN_HIDDEN // N_DEV, D_MODEL),
    }
    for n in ("w_proj_attn", "w_proj_pool", "w_proj_conv"):
        slabs[n] = jnp.transpose(G[n].reshape(BRANCH_W, N_DEV, D_MODEL // N_DEV), (1, 0, 2))
    return slabs


def _sum_layer_grads(recv, l):
    out = {n: _sum_slabs(r, f"sum_{n}_l{l}") for n, r in recv.items()}
    out["w_in"] = out["w_in"][:, :IN_SHARD]
    out["w_gate_up"] = out["w_gate_up"][:, :GU_SHARD]
    return out


def _exchange_layer_grads(G, l):
    names = list(SHARDED)
    slabs = _grad_slabs(G, l)
    recv = _multi_exchange([slabs[n] for n in names], f"exchange_grads_l{l}")
    return _sum_layer_grads(dict(zip(names, recv)), l)


def _sum_small(xs, name):
    def body(*refs):
        for x_ref, o_ref in zip(refs[:len(xs)], refs[len(xs):]):
            acc = x_ref[0]
            for j in range(1, N_DEV):
                acc = acc + x_ref[j]
            o_ref[...] = acc

    return pl.pallas_call(
        body, name=name, out_shape=[jax.ShapeDtypeStruct(x.shape[1:], F32) for x in xs],
        compiler_params=_cp(),
    )(*xs)


def _as_2d(a):
    if a.ndim == 1:
        return a.reshape(1, -1)
    return a.reshape(-1, a.shape[-1])


def kernel(x, attn_norm, w_in, b_forget, b_gate, w_proj_attn, pool_w, pool_scale, w_proj_pool, conv_w, w_proj_conv, w_out, ffn_norm, w_gate_up, w_down, final_norm, loss_target, m_attn_norm, m_w_in, m_b_forget, m_b_gate, m_w_proj_attn, m_pool_w, m_pool_scale, m_w_proj_pool, m_conv_w, m_w_proj_conv, m_w_out, m_ffn_norm, m_w_gate_up, m_w_down, m_final_norm, v_attn_norm, v_w_in, v_b_forget, v_b_gate, v_w_proj_attn, v_pool_w, v_pool_scale, v_w_proj_pool, v_conv_w, v_w_proj_conv, v_w_out, v_ffn_norm, v_w_gate_up, v_w_down, v_final_norm):
    weights = dict(attn_norm=attn_norm, w_in=w_in, b_forget=b_forget, b_gate=b_gate, w_proj_attn=w_proj_attn,
                   pool_w=pool_w, pool_scale=pool_scale, w_proj_pool=w_proj_pool, conv_w=conv_w,
                   w_proj_conv=w_proj_conv, w_out=w_out, ffn_norm=ffn_norm, w_gate_up=w_gate_up, w_down=w_down,
                   final_norm=final_norm)
    moments_m = dict(attn_norm=m_attn_norm, w_in=m_w_in, b_forget=m_b_forget, b_gate=m_b_gate,
                     w_proj_attn=m_w_proj_attn, pool_w=m_pool_w, pool_scale=m_pool_scale, w_proj_pool=m_w_proj_pool,
                     conv_w=m_conv_w, w_proj_conv=m_w_proj_conv, w_out=m_w_out, ffn_norm=m_ffn_norm,
                     w_gate_up=m_w_gate_up, w_down=m_w_down, final_norm=m_final_norm)
    moments_v = dict(attn_norm=v_attn_norm, w_in=v_w_in, b_forget=v_b_forget, b_gate=v_b_gate,
                     w_proj_attn=v_w_proj_attn, pool_w=v_pool_w, pool_scale=v_pool_scale, w_proj_pool=v_w_proj_pool,
                     conv_w=v_conv_w, w_proj_conv=v_w_proj_conv, w_out=v_w_out, ffn_norm=v_ffn_norm,
                     w_gate_up=v_w_gate_up, w_down=v_w_down, final_norm=v_final_norm)

    sh = _padded_shards(weights)
    names = list(SHARDED)
    last = DEPTH - 1
    W0, conv_all = _gather_layer(sh, conv_w, 0)
    gather_layers = [last] * len(names)
    gather_started = _split_start([sh[n] for n in names], gather_layers, False, W0["w_out"], "gather_start_l1")
    W0["w_out"] = W0["w_out"] + gather_started[4][0, 0].astype(BF16)

    def get_W(l, xt):
        if l == 0:
            W = W0
        else:
            lands = _split_wait(gather_started, gather_layers, False, xt, "gather_wait_l1")
            W = _full_operands({n: _with_own(land, sh[n][l]) for n, land in zip(names, lands)}, l)
        W.update(_replicated_operands(weights, l))
        W["conv_w"] = jnp.transpose(conv_all[:, l], (1, 0, 2)).reshape(CONV_K, BRANCH_W)
        return W

    pending = {}

    def reduce_layer(l, G, Ws):
        small = {n: G[n] for n in ("attn_norm", "b_forget", "b_gate", "pool_w", "pool_scale", "ffn_norm", "conv_w")}
        if l == last and DEPTH > 1:
            slabs = _grad_slabs(G, l)
            pending["slabs"] = slabs
            pending["started"] = _split_start([slabs[n] for n in names], None, True, slabs["w_in"],
                                              "exchange_start_l1")
            Ws[l - 1]["ffn_norm"] = Ws[l - 1]["ffn_norm"] + pending["started"][4][0, 0]
            return small
        return {**_exchange_layer_grads(G, l), **small}

    loss_part, grad_x, grads, g_final = _local_step(x, loss_target, get_W, final_norm, reduce_layer)
    lands = _split_wait(pending["started"], None, True, grad_x, "exchange_wait_l1")
    me = 4 * lax.axis_index("x") + 2 * lax.axis_index("y") + lax.axis_index("c")
    recv = {n: _with_own(land, lax.dynamic_index_in_dim(pending["slabs"][n], me, 0, keepdims=False))
            for n, land in zip(names, lands)}
    grads[last].update(_sum_layer_grads(recv, last))
    gw = {n: jnp.stack([grads[l][n] for l in range(DEPTH)]) for n in SHARDED}

    small = ("attn_norm", "b_forget", "b_gate", "pool_w", "pool_scale", "ffn_norm", "conv_w")
    parts = [jnp.stack([grads[l][n] for l in range(DEPTH)]) for n in small] + [g_final, loss_part]
    gathered = _multi_gather(parts, [None] * len(parts), "gather_small_grads")
    summed = _sum_small(gathered, "sum_small_grads")
    for n, s in zip(small, summed):
        gw[n] = s
    gw["attn_norm"], gw["ffn_norm"] = gw["attn_norm"][:, 0], gw["ffn_norm"][:, 0]
    gw["b_forget"] = gw["b_forget"][:, 0, :N_HEADS]
    gw["b_gate"], gw["pool_scale"] = gw["b_gate"][:, 0], gw["pool_scale"][:, 0]
    gw["conv_w"] = lax.dynamic_slice_in_dim(gw["conv_w"], me * (BRANCH_W // N_DEV), BRANCH_W // N_DEV, axis=2)
    gw["final_norm"] = summed[-2][0]
    loss = summed[-1][0, 0]

    deltas, new_m, new_v = {}, {}, {}
    for n in WEIGHT_ORDER:
        shape = weights[n].shape
        d, nm, nv = _adamw(_as_2d(weights[n]), _as_2d(gw[n]), _as_2d(moments_m[n]), _as_2d(moments_v[n]),
                           "adamw_" + n)
        deltas[n], new_m[n], new_v[n] = d.reshape(shape), nm.reshape(shape), nv.reshape(shape)

    return (loss, grad_x, *[gw[n] for n in WEIGHT_ORDER], *[deltas[n] for n in WEIGHT_ORDER],
            *[new_m[n] for n in WEIGHT_ORDER], *[new_v[n] for n in WEIGHT_ORDER])
```

```python
import functools

import numpy as np
import jax
import jax.numpy as jnp
from jax import lax
from jax.experimental import pallas as pl
from jax.experimental.pallas import tpu as pltpu

F32 = jnp.float32
BF16 = jnp.bfloat16

N_DEV = 8
D_MODEL = 1024
DEPTH = 2
N_HEADS = 8
HEAD_DIM = 64
BRANCH_W = 512
POOL_WINDOWS = (2, 4, 8, 16)
POOL_GD = 128
CONV_K = 3
FFN_HIDDEN = 2816
GATE_W = 3 * D_MODEL
IN_COLS = 6664
MAIN_COLS = GATE_W + 7 * BRANCH_W
RMS_EPS = 1e-6
NEG_INF = -1e30

ADAM_LR = 0.001
ADAM_B1 = 0.9
ADAM_B2 = 0.999
ADAM_EPS = 1e-08
ADAM_WD = 0.01
ADAM_STEP = 10

LANES = 128
VMEM_LIMIT = 56 * 1024 * 1024
ATT_BLK = 256
CUM_BLK = 256

OFF_G, OFF_Q, OFF_K, OFF_V, OFF_U, OFF_CV, OFF_CB, OFF_CC = (
    0, 3072, 3584, 4096, 4608, 5120, 5632, 6144)


def _cp(sem=None):
    return pltpu.CompilerParams(dimension_semantics=sem, vmem_limit_bytes=VMEM_LIMIT)


def _sigmoid(z):
    return 1.0 / (1.0 + jnp.exp(-z))


def _matmul(a, b, *, mode, out_dtype, name, tm=2048, tn=512, tk=None, residual=None):
    if mode == "nn":
        (M, K), N = a.shape, b.shape[1]
    elif mode == "nt":
        (M, K), N = a.shape, b.shape[0]
    else:
        (K, M), N = a.shape, b.shape[1]
    tm, tn, tk = min(tm, M), min(tn, N), K if tk is None else min(tk, K)
    assert M % tm == 0 and N % tn == 0 and K % tk == 0, (name, M, N, K, tm, tn, tk)
    nk = K // tk
    if mode == "nn":
        a_spec = pl.BlockSpec((tm, tk), lambda i, j, k: (i, k))
        b_spec = pl.BlockSpec((tk, tn), lambda i, j, k: (k, j))
        dims = (((1,), (0,)), ((), ()))
    elif mode == "nt":
        a_spec = pl.BlockSpec((tm, tk), lambda i, j, k: (i, k))
        b_spec = pl.BlockSpec((tn, tk), lambda i, j, k: (j, k))
        dims = (((1,), (1,)), ((), ()))
    else:
        a_spec = pl.BlockSpec((tk, tm), lambda i, j, k: (k, i))
        b_spec = pl.BlockSpec((tk, tn), lambda i, j, k: (k, j))
        dims = (((0,), (0,)), ((), ()))
    o_spec = pl.BlockSpec((tm, tn), lambda i, j, k: (i, j))
    has_res = residual is not None

    def body(*refs):
        a_ref, b_ref = refs[:2]
        r_ref = refs[2] if has_res else None
        o_ref = refs[2 + has_res]

        def finish(acc):
            if has_res:
                acc = acc + r_ref[...].astype(F32)
            o_ref[...] = acc.astype(out_dtype)

        prod = lax.dot_general(a_ref[...], b_ref[...], dims, preferred_element_type=F32)
        if nk == 1:
            finish(prod)
            return
        acc_ref = refs[-1]
        k = pl.program_id(2)

        @pl.when(k == 0)
        def _():
            acc_ref[...] = prod

        @pl.when(jnp.logical_and(k > 0, k < nk - 1))
        def _():
            acc_ref[...] += prod

        @pl.when(k == nk - 1)
        def _():
            finish(acc_ref[...] + prod)

    in_specs = [a_spec, b_spec] + ([o_spec] if has_res else [])
    args = (a, b) + ((residual,) if has_res else ())
    return pl.pallas_call(
        body, name=name, grid=(M // tm, N // tn, nk), in_specs=in_specs, out_specs=o_spec,
        out_shape=jax.ShapeDtypeStruct((M, N), out_dtype),
        scratch_shapes=[pltpu.VMEM((tm, tn), F32)] if nk > 1 else [],
        compiler_params=_cp(("parallel", "parallel", "arbitrary")),
    )(*args)


def _rms_fwd(x, g, name):
    T, Dm = x.shape
    tm = min(512, T)

    def body(x_ref, g_ref, h_ref):
        xf = x_ref[...]
        r = lax.rsqrt(jnp.mean(xf * xf, axis=-1, keepdims=True) + RMS_EPS)
        h_ref[...] = ((xf * r) * g_ref[...]).astype(BF16)

    return pl.pallas_call(
        body, name=name, grid=(T // tm,),
        in_specs=[pl.BlockSpec((tm, Dm), lambda i: (i, 0)), pl.BlockSpec((1, Dm), lambda i: (0, 0))],
        out_specs=pl.BlockSpec((tm, Dm), lambda i: (i, 0)),
        out_shape=jax.ShapeDtypeStruct((T, Dm), BF16),
        compiler_params=_cp(("parallel",)),
    )(x, g.reshape(1, Dm))


def _rms_bwd(x, g, dh, dres, name):
    T, Dm = x.shape
    tm = min(512, T)

    def body(x_ref, g_ref, dh_ref, dres_ref, dx_ref, dxb_ref, dg_ref):
        i = pl.program_id(0)
        xf = x_ref[...]
        r = lax.rsqrt(jnp.mean(xf * xf, axis=-1, keepdims=True) + RMS_EPS)
        xn = xf * r
        dhf = dh_ref[...].astype(F32)
        dxn = dhf * g_ref[...]
        c = jnp.mean(dxn * xn, axis=-1, keepdims=True)
        dx = dres_ref[...] + r * (dxn - xn * c)
        dx_ref[...] = dx
        dxb_ref[...] = dx.astype(BF16)
        part = jnp.sum(dhf * xn, axis=0, keepdims=True)

        @pl.when(i == 0)
        def _():
            dg_ref[...] = part

        @pl.when(i > 0)
        def _():
            dg_ref[...] += part

    row = pl.BlockSpec((tm, Dm), lambda i: (i, 0))
    vec = pl.BlockSpec((1, Dm), lambda i: (0, 0))
    return pl.pallas_call(
        body, name=name, grid=(T // tm,), in_specs=[row, vec, row, row], out_specs=[row, row, vec],
        out_shape=[jax.ShapeDtypeStruct((T, Dm), F32), jax.ShapeDtypeStruct((T, Dm), BF16),
                   jax.ShapeDtypeStruct((1, Dm), F32)],
        compiler_params=_cp(("arbitrary",)),
    )(x, g.reshape(1, Dm), dh, dres)


def _loss_head(x, g, target, name):
    T, Dm = x.shape
    tm = min(512, T)

    def body(x_ref, g_ref, t_ref, loss_ref, dx_ref, dxb_ref, dg_ref):
        i = pl.program_id(0)
        xf = x_ref[...]
        gv = g_ref[...]
        r = lax.rsqrt(jnp.mean(xf * xf, axis=-1, keepdims=True) + RMS_EPS)
        xn = xf * r
        diff = xn * gv - t_ref[...]
        per_tok = jnp.mean(diff * diff, axis=-1, keepdims=True)
        lpart = 0.5 * jnp.sum(per_tok, axis=0, keepdims=True) + jnp.zeros((1, LANES), F32)
        dy = diff * (1.0 / Dm)
        dxn = dy * gv
        c = jnp.mean(dxn * xn, axis=-1, keepdims=True)
        dx = r * (dxn - xn * c)
        dx_ref[...] = dx
        dxb_ref[...] = dx.astype(BF16)
        part = jnp.sum(dy * xn, axis=0, keepdims=True)

        @pl.when(i == 0)
        def _():
            dg_ref[...] = part
            loss_ref[...] = lpart

        @pl.when(i > 0)
        def _():
            dg_ref[...] += part
            loss_ref[...] += lpart

    row = pl.BlockSpec((tm, Dm), lambda i: (i, 0))
    vec = pl.BlockSpec((1, Dm), lambda i: (0, 0))
    lsp = pl.BlockSpec((1, LANES), lambda i: (0, 0))
    return pl.pallas_call(
        body, name=name, grid=(T // tm,), in_specs=[row, vec, row], out_specs=[lsp, row, row, vec],
        out_shape=[jax.ShapeDtypeStruct((1, LANES), F32), jax.ShapeDtypeStruct((T, Dm), F32),
                   jax.ShapeDtypeStruct((T, Dm), BF16), jax.ShapeDtypeStruct((1, Dm), F32)],
        compiler_params=_cp(("arbitrary",)),
    )(x, g.reshape(1, Dm), target)


def _split_bf16(v):
    hi = v.astype(BF16)
    r1 = v - hi.astype(F32)
    mid = r1.astype(BF16)
    lo = (r1 - mid.astype(F32)).astype(BF16)
    return hi, mid, lo


def _tri_dot(tri, v):
    hi, mid, lo = _split_bf16(v)
    dot = functools.partial(jnp.dot, preferred_element_type=F32)
    return dot(tri, hi) + dot(tri, mid) + dot(tri, lo)


def _log_sigmoid(z):
    return jnp.minimum(z, 0.0) - jnp.log(1.0 + jnp.exp(-jnp.abs(z)))


def _fox_cumsum_fwd(f, bf, n_seq, name):
    T = f.shape[0]
    S = T // n_seq
    c = min(CUM_BLK, S)

    def body(f_ref, b_ref, out_ref):
        ri = lax.broadcasted_iota(jnp.int32, (c, c), 0)
        ci = lax.broadcasted_iota(jnp.int32, (c, c), 1)
        tri = (ri >= ci).astype(BF16)
        carry = jnp.zeros((1, LANES), F32)
        for j in range(S // c):
            lf = _log_sigmoid(f_ref[j * c:(j + 1) * c, :] + b_ref[...])
            out_ref[j * c:(j + 1) * c, :] = _tri_dot(tri, lf) + carry
            carry = carry + jnp.sum(lf, axis=0, keepdims=True)

    blk = pl.BlockSpec((S, LANES), lambda b: (b, 0))
    return pl.pallas_call(
        body, name=name, grid=(n_seq,), in_specs=[blk, pl.BlockSpec((1, LANES), lambda b: (0, 0))],
        out_specs=blk, out_shape=jax.ShapeDtypeStruct((T, LANES), F32),
        compiler_params=_cp(("parallel",)),
    )(f, bf)


def _fox_cumsum_bwd(f, bf, dF, n_seq, name):
    T = f.shape[0]
    S = T // n_seq
    c = min(CUM_BLK, S)

    def body(f_ref, b_ref, dF_ref, df_ref, db_ref):
        b = pl.program_id(0)
        ri = lax.broadcasted_iota(jnp.int32, (c, c), 0)
        ci = lax.broadcasted_iota(jnp.int32, (c, c), 1)
        tri = (ri <= ci).astype(BF16)
        carry = jnp.zeros((1, LANES), F32)
        dbp = jnp.zeros((1, LANES), F32)
        for j in reversed(range(S // c)):
            dFc = dF_ref[j * c:(j + 1) * c, :]
            dlf = _tri_dot(tri, dFc) + carry
            carry = carry + jnp.sum(dFc, axis=0, keepdims=True)
            z = f_ref[j * c:(j + 1) * c, :] + b_ref[...]
            dz = dlf * _sigmoid(-z)
            df_ref[j * c:(j + 1) * c, :] = dz.astype(BF16)
            dbp = dbp + jnp.sum(dz, axis=0, keepdims=True)

        @pl.when(b == 0)
        def _():
            db_ref[...] = dbp

        @pl.when(b > 0)
        def _():
            db_ref[...] += dbp

    blk = pl.BlockSpec((S, LANES), lambda b: (b, 0))
    vec = pl.BlockSpec((1, LANES), lambda b: (0, 0))
    return pl.pallas_call(
        body, name=name, grid=(n_seq,), in_specs=[blk, vec, blk], out_specs=[blk, vec],
        out_shape=[jax.ShapeDtypeStruct((T, LANES), BF16), jax.ShapeDtypeStruct((1, LANES), F32)],
        compiler_params=_cp(("arbitrary",)),
    )(f, bf, dF)


def _pair_masks():
    lane = lax.broadcasted_iota(jnp.int32, (1, LANES), 1)
    lo = lane < HEAD_DIM
    return lo, jnp.logical_not(lo)


def _attn_logits(q, k, fq, fk, sel, mask, scale):
    qm = jnp.where(sel, q, jnp.zeros_like(q))
    s = lax.dot_general(qm, k, (((1,), (1,)), ((), ())), preferred_element_type=F32) * scale
    s = s + fq - fk
    return jnp.where(mask, s, NEG_INF)


def _causal_mask(qi, ki, blk):
    row = qi * blk + lax.broadcasted_iota(jnp.int32, (blk, blk), 0)
    col = ki * blk + lax.broadcasted_iota(jnp.int32, (blk, blk), 1)
    return col <= row


def _attn_fwd(proj, Fq, Fk, n_seq, name):
    T = proj.shape[0]
    S = T // n_seq
    blk = min(ATT_BLK, S)
    nb = S // blk
    scale = HEAD_DIM ** -0.5
    qc, kc, vc = OFF_Q // LANES, OFF_K // LANES, OFF_V // LANES

    def body(q_ref, k_ref, v_ref, fq_ref, fk_ref, o_ref, o32_ref, lse_ref, m_s, l_s, acc_s):
        qi, ki = pl.program_id(2), pl.program_id(3)

        @pl.when(ki == 0)
        def _():
            m_s[...] = jnp.full_like(m_s, NEG_INF)
            l_s[...] = jnp.zeros_like(l_s)
            acc_s[...] = jnp.zeros_like(acc_s)

        @pl.when(ki <= qi)
        def _():
            q, k, v = q_ref[...], k_ref[...], v_ref[...]
            mask = _causal_mask(qi, ki, blk)
            for hh, sel in enumerate(_pair_masks()):
                s = _attn_logits(q, k, fq_ref[hh], fk_ref[hh], sel, mask, scale)
                m_prev = m_s[hh]
                m_new = jnp.maximum(m_prev, jnp.max(s, axis=-1, keepdims=True))
                alpha = jnp.exp(m_prev - m_new)
                p = jnp.exp(s - m_new)
                l_s[hh] = alpha * l_s[hh] + jnp.sum(p, axis=-1, keepdims=True)
                p_hi = p.astype(BF16)
                p_lo = (p - p_hi.astype(F32)).astype(BF16)
                pv = jnp.dot(p_hi, v, preferred_element_type=F32) + jnp.dot(p_lo, v, preferred_element_type=F32)
                acc_s[hh] = alpha * acc_s[hh] + pv
                m_s[hh] = m_new

        @pl.when(ki == qi)
        def _():
            lo, _ = _pair_masks()
            o = jnp.where(lo, acc_s[0] / l_s[0], acc_s[1] / l_s[1])
            o_ref[...] = o.astype(BF16)
            o32_ref[...] = o
            lse_ref[0] = m_s[0] + jnp.log(l_s[0])
            lse_ref[1] = m_s[1] + jnp.log(l_s[1])

    grid = (n_seq, N_HEADS // 2, nb, nb)
    return pl.pallas_call(
        body, name=name, grid=grid,
        in_specs=[
            pl.BlockSpec((blk, LANES), lambda b, j, qi, ki: (b * nb + qi, qc + j)),
            pl.BlockSpec((blk, LANES), lambda b, j, qi, ki: (b * nb + jnp.minimum(ki, qi), kc + j)),
            pl.BlockSpec((blk, LANES), lambda b, j, qi, ki: (b * nb + jnp.minimum(ki, qi), vc + j)),
            pl.BlockSpec((2, blk, 1), lambda b, j, qi, ki: (j, b * nb + qi, 0)),
            pl.BlockSpec((2, 1, blk), lambda b, j, qi, ki: (j, 0, b * nb + jnp.minimum(ki, qi))),
        ],
        out_specs=[
            pl.BlockSpec((blk, LANES), lambda b, j, qi, ki: (b * nb + qi, j)),
            pl.BlockSpec((blk, LANES), lambda b, j, qi, ki: (b * nb + qi, j)),
            pl.BlockSpec((2, blk, 1), lambda b, j, qi, ki: (j, b * nb + qi, 0)),
        ],
        out_shape=[jax.ShapeDtypeStruct((T, BRANCH_W), BF16), jax.ShapeDtypeStruct((T, BRANCH_W), F32),
                   jax.ShapeDtypeStruct((N_HEADS, T, 1), F32)],
        scratch_shapes=[pltpu.VMEM((2, blk, 1), F32), pltpu.VMEM((2, blk, 1), F32),
                        pltpu.VMEM((2, blk, LANES), F32)],
        compiler_params=_cp(("parallel", "parallel", "parallel", "arbitrary")),
    )(proj, proj, proj, Fq, Fk)


def _attn_delta(do, o, name):
    T = do.shape[0]
    tm = min(512, T)

    def body(do_ref, o_ref, d_ref):
        prod = do_ref[...].astype(F32) * o_ref[...].astype(F32)
        lo, hi = _pair_masks()
        for j in range(N_HEADS // 2):
            pj = prod[:, j * LANES:(j + 1) * LANES]
            d_ref[2 * j] = jnp.sum(jnp.where(lo, pj, 0.0), axis=-1, keepdims=True)
            d_ref[2 * j + 1] = jnp.sum(jnp.where(hi, pj, 0.0), axis=-1, keepdims=True)

    row = pl.BlockSpec((tm, BRANCH_W), lambda i: (i, 0))
    return pl.pallas_call(
        body, name=name, grid=(T // tm,), in_specs=[row, row],
        out_specs=pl.BlockSpec((N_HEADS, tm, 1), lambda i: (0, i, 0)),
        out_shape=jax.ShapeDtypeStruct((N_HEADS, T, 1), F32),
        compiler_params=_cp(("parallel",)),
    )(do, o)


def _attn_bwd_dq(proj, do, lse, delta, Fq, Fk, n_seq, name):
    T = proj.shape[0]
    S = T // n_seq
    blk = min(ATT_BLK, S)
    nb = S // blk
    scale = HEAD_DIM ** -0.5
    qc, kc, vc = OFF_Q // LANES, OFF_K // LANES, OFF_V // LANES

    def body(q_ref, k_ref, v_ref, do_ref, lse_ref, dl_ref, fq_ref, fk_ref, dq_ref, acc_s):
        qi, ki = pl.program_id(2), pl.program_id(3)

        @pl.when(ki == 0)
        def _():
            acc_s[...] = jnp.zeros_like(acc_s)

        @pl.when(ki <= qi)
        def _():
            q, k, v, do_ = q_ref[...], k_ref[...], v_ref[...], do_ref[...]
            mask = _causal_mask(qi, ki, blk)
            for hh, sel in enumerate(_pair_masks()):
                s = _attn_logits(q, k, fq_ref[hh], fk_ref[hh], sel, mask, scale)
                p = jnp.exp(s - lse_ref[hh])
                dom = jnp.where(sel, do_, jnp.zeros_like(do_))
                dp = lax.dot_general(dom, v, (((1,), (1,)), ((), ())), preferred_element_type=F32)
                ds = p * (dp - dl_ref[hh])
                acc_s[hh] += jnp.dot(ds.astype(BF16), k, preferred_element_type=F32)

        @pl.when(ki == qi)
        def _():
            lo, _ = _pair_masks()
            dq_ref[...] = (jnp.where(lo, acc_s[0], acc_s[1]) * scale).astype(BF16)

    qmap = lambda b, j, qi, ki: (b * nb + qi, j)
    col1 = pl.BlockSpec((2, blk, 1), lambda b, j, qi, ki: (j, b * nb + qi, 0))
    return pl.pallas_call(
        body, name=name, grid=(n_seq, N_HEADS // 2, nb, nb),
        in_specs=[
            pl.BlockSpec((blk, LANES), lambda b, j, qi, ki: (b * nb + qi, qc + j)),
            pl.BlockSpec((blk, LANES), lambda b, j, qi, ki: (b * nb + jnp.minimum(ki, qi), kc + j)),
            pl.BlockSpec((blk, LANES), lambda b, j, qi, ki: (b * nb + jnp.minimum(ki, qi), vc + j)),
            pl.BlockSpec((blk, LANES), qmap),
            col1, col1, col1,
            pl.BlockSpec((2, 1, blk), lambda b, j, qi, ki: (j, 0, b * nb + jnp.minimum(ki, qi))),
        ],
        out_specs=pl.BlockSpec((blk, LANES), qmap),
        out_shape=jax.ShapeDtypeStruct((T, BRANCH_W), BF16),
        scratch_shapes=[pltpu.VMEM((2, blk, LANES), F32)],
        compiler_params=_cp(("parallel", "parallel", "parallel", "arbitrary")),
    )(proj, proj, proj, do, lse, delta, Fq, Fk)


def _attn_bwd_dkv(proj, do, lse, delta, Fq, Fk, n_seq, name):
    T = proj.shape[0]
    S = T // n_seq
    blk = min(ATT_BLK, S)
    nb = S // blk
    scale = HEAD_DIM ** -0.5
    qc, kc, vc = OFF_Q // LANES, OFF_K // LANES, OFF_V // LANES
    tdot = functools.partial(lax.dot_general, dimension_numbers=(((0,), (0,)), ((), ())),
                             preferred_element_type=F32)

    def body(q_ref, k_ref, v_ref, do_ref, lse_ref, dl_ref, fq_ref, fk_ref, dk_ref, dv_ref, dfk_ref,
             dk_s, dv_s, df_s):
        ki, qi = pl.program_id(2), pl.program_id(3)

        @pl.when(qi == 0)
        def _():
            dk_s[...] = jnp.zeros_like(dk_s)
            dv_s[...] = jnp.zeros_like(dv_s)
            df_s[...] = jnp.zeros_like(df_s)

        @pl.when(qi >= ki)
        def _():
            q, k, v, do_ = q_ref[...], k_ref[...], v_ref[...], do_ref[...]
            mask = _causal_mask(qi, ki, blk)
            for hh, sel in enumerate(_pair_masks()):
                s = _attn_logits(q, k, fq_ref[hh], fk_ref[hh], sel, mask, scale)
                p = jnp.exp(s - lse_ref[hh])
                dv_s[hh] += tdot(p.astype(BF16), do_)
                dom = jnp.where(sel, do_, jnp.zeros_like(do_))
                dp = lax.dot_general(dom, v, (((1,), (1,)), ((), ())), preferred_element_type=F32)
                ds = p * (dp - dl_ref[hh])
                dk_s[hh] += tdot(ds.astype(BF16), q)
                df_s[hh] -= jnp.sum(ds, axis=0, keepdims=True)

        @pl.when(qi == nb - 1)
        def _():
            lo, _ = _pair_masks()
            dk_ref[...] = (jnp.where(lo, dk_s[0], dk_s[1]) * scale).astype(BF16)
            dv_ref[...] = jnp.where(lo, dv_s[0], dv_s[1]).astype(BF16)
            dfk_ref[...] = df_s[...]

    kmap = lambda b, j, ki, qi: (b * nb + ki, j)
    col1 = pl.BlockSpec((2, blk, 1), lambda b, j, ki, qi: (j, b * nb + jnp.maximum(qi, ki), 0))
    rowk = pl.BlockSpec((2, 1, blk), lambda b, j, ki, qi: (j, 0, b * nb + ki))
    return pl.pallas_call(
        body, name=name, grid=(n_seq, N_HEADS // 2, nb, nb),
        in_specs=[
            pl.BlockSpec((blk, LANES), lambda b, j, ki, qi: (b * nb + jnp.maximum(qi, ki), qc + j)),
            pl.BlockSpec((blk, LANES), lambda b, j, ki, qi: (b * nb + ki, kc + j)),
            pl.BlockSpec((blk, LANES), lambda b, j, ki, qi: (b * nb + ki, vc + j)),
            pl.BlockSpec((blk, LANES), lambda b, j, ki, qi: (b * nb + jnp.maximum(qi, ki), j)),
            col1, col1, col1, rowk,
        ],
        out_specs=[pl.BlockSpec((blk, LANES), kmap), pl.BlockSpec((blk, LANES), kmap), rowk],
        out_shape=[jax.ShapeDtypeStruct((T, BRANCH_W), BF16), jax.ShapeDtypeStruct((T, BRANCH_W), BF16),
                   jax.ShapeDtypeStruct((N_HEADS, 1, T), F32)],
        scratch_shapes=[pltpu.VMEM((2, blk, LANES), F32), pltpu.VMEM((2, blk, LANES), F32),
                        pltpu.VMEM((2, 1, blk), F32)],
        compiler_params=_cp(("parallel", "parallel", "parallel", "arbitrary")),
    )(proj, proj, proj, do, lse, delta, Fq, Fk)


AUG0 = HEAD_DIM
Q_TILE, K_CHUNK, ROW_GROUP = 512, 256, 64


def _fox_prep(f, bf, proj, n_seq, name):
    T = f.shape[0]
    S = T // n_seq
    c = min(CUM_BLK, S)

    def body(f_ref, b_ref, q_ref, k_ref, qa_ref, ka_ref):
        ri = lax.broadcasted_iota(jnp.int32, (c, c), 0)
        ci = lax.broadcasted_iota(jnp.int32, (c, c), 1)
        tri = (ri >= ci).astype(BF16)
        lane = lax.broadcasted_iota(jnp.int32, (c, LANES), 1)
        carry = jnp.zeros((1, LANES), F32)
        for j in range(S // c):
            rows = slice(j * c, (j + 1) * c)
            lf = _log_sigmoid(f_ref[rows, :] + b_ref[...])
            Fc = _tri_dot(tri, lf) + carry
            carry = carry + jnp.sum(lf, axis=0, keepdims=True)
            for h in range(N_HEADS):
                col = jnp.sum(jnp.where(lane == h, Fc, 0.0), axis=-1, keepdims=True)
                hi = col.astype(BF16).astype(F32)
                r1 = col - hi
                mid = r1.astype(BF16).astype(F32)
                lo = r1 - mid
                ones_q = jnp.logical_and(lane >= AUG0 + 3, lane < AUG0 + 6)
                ones_k = jnp.logical_and(lane >= AUG0, lane < AUG0 + 3)
                aug_q = jnp.where(lane == AUG0, hi, jnp.where(lane == AUG0 + 1, mid, jnp.where(
                    lane == AUG0 + 2, lo, jnp.where(ones_q, 1.0, 0.0))))
                aug_k = jnp.where(lane == AUG0 + 3, -hi, jnp.where(lane == AUG0 + 4, -mid, jnp.where(
                    lane == AUG0 + 5, -lo, jnp.where(ones_k, 1.0, 0.0))))
                pair = slice((h // 2) * LANES, (h // 2 + 1) * LANES)
                qp, kp = q_ref[rows, pair].astype(F32), k_ref[rows, pair].astype(F32)
                if h % 2:
                    qp, kp = pltpu.roll(qp, HEAD_DIM, 1), pltpu.roll(kp, HEAD_DIM, 1)
                out = slice(h * LANES, (h + 1) * LANES)
                qa_ref[rows, out] = jnp.where(lane < HEAD_DIM, qp * (HEAD_DIM ** -0.5), aug_q).astype(BF16)
                ka_ref[rows, out] = jnp.where(lane < HEAD_DIM, kp, aug_k).astype(BF16)

    fblk = pl.BlockSpec((S, LANES), lambda b: (b, 0))
    out = pl.BlockSpec((S, N_HEADS * LANES), lambda b: (b, 0))
    sh = jax.ShapeDtypeStruct((T, N_HEADS * LANES), BF16)
    return pl.pallas_call(
        body, name=name, grid=(n_seq,),
        in_specs=[fblk, pl.BlockSpec((1, LANES), lambda b: (0, 0)),
                  pl.BlockSpec((S, BRANCH_W), lambda b: (b, OFF_Q // BRANCH_W)),
                  pl.BlockSpec((S, BRANCH_W), lambda b: (b, OFF_K // BRANCH_W))],
        out_specs=[out, out], out_shape=[sh, sh],
        compiler_params=_cp(("parallel",)),
    )(f, bf, proj, proj)


def _band_mask(q0, k0, nq, nk):
    row = q0 + lax.broadcasted_iota(jnp.int32, (nq, nk), 0)
    col = k0 + lax.broadcasted_iota(jnp.int32, (nq, nk), 1)
    return col <= row


_NT = (((1,), (1,)), ((), ()))
_TN = (((0,), (0,)), ((), ()))


def _attn_fwd2(qa, ka, proj, n_seq, name):
    T = qa.shape[0]
    S = T // n_seq
    tq, tk, rg = min(Q_TILE, S), min(K_CHUNK, S), ROW_GROUP
    nq, per = S // tq, tq // tk
    vc = OFF_V // LANES

    def body(q_ref, k_ref, v_ref, o_ref, o32_ref, lse_ref, s_s, phi_s, plo_s, a_s, m_s, l_s, acc_s):
        qi = pl.program_id(2)
        m_s[...] = jnp.full_like(m_s, NEG_INF)
        l_s[...] = jnp.zeros_like(l_s)
        acc_s[...] = jnp.zeros_like(acc_s)

        def chunk(kc, masked):
            k0 = pl.multiple_of(kc * tk, tk)
            v = v_ref[pl.ds(k0, tk), :]
            for hh in range(2):
                hl = slice(hh * LANES, (hh + 1) * LANES)
                s_s[...] = lax.dot_general(q_ref[:, hl], k_ref[pl.ds(k0, tk), hl], _NT, preferred_element_type=F32)
                for r in range(tq // rg):
                    rows = slice(r * rg, (r + 1) * rg)
                    s = s_s[rows, :]
                    if masked:
                        s = jnp.where(_band_mask(qi * tq + r * rg, k0, rg, tk), s, NEG_INF)
                    m_prev = m_s[hh, rows]
                    m_new = jnp.maximum(m_prev, jnp.max(s, axis=-1, keepdims=True))
                    alpha = jnp.exp(m_prev - m_new)
                    p = jnp.exp(s - m_new)
                    l_s[hh, rows] = alpha * l_s[hh, rows] + jnp.sum(p, axis=-1, keepdims=True)
                    m_s[hh, rows] = m_new
                    a_s[rows] = alpha
                    p_hi = p.astype(BF16)
                    phi_s[rows, :] = p_hi
                    plo_s[rows, :] = (p - p_hi.astype(F32)).astype(BF16)
                pv = (jnp.dot(phi_s[...], v, preferred_element_type=F32)
                      + jnp.dot(plo_s[...], v, preferred_element_type=F32))
                acc_s[hh] = a_s[...] * acc_s[hh] + pv

        def unmasked(kc, carry):
            chunk(kc, False)
            return carry

        lax.fori_loop(0, qi * per, unmasked, 0)
        for d in range(per):
            chunk(qi * per + d, True)

        lo, _ = _pair_masks()
        o = jnp.where(lo, acc_s[0] / l_s[0], acc_s[1] / l_s[1])
        o_ref[...] = o.astype(BF16)
        o32_ref[...] = o
        lse_ref[0] = m_s[0] + jnp.log(l_s[0])
        lse_ref[1] = m_s[1] + jnp.log(l_s[1])

    qmap = lambda b, j, qi: (b * nq + qi, j)
    return pl.pallas_call(
        body, name=name, grid=(n_seq, N_HEADS // 2, nq),
        in_specs=[pl.BlockSpec((tq, 2 * LANES), qmap),
                  pl.BlockSpec((S, 2 * LANES), lambda b, j, qi: (b, j)),
                  pl.BlockSpec((S, LANES), lambda b, j, qi: (b, vc + j))],
        out_specs=[pl.BlockSpec((tq, LANES), qmap), pl.BlockSpec((tq, LANES), qmap),
                   pl.BlockSpec((2, tq, 1), lambda b, j, qi: (j, b * nq + qi, 0))],
        out_shape=[jax.ShapeDtypeStruct((T, BRANCH_W), BF16), jax.ShapeDtypeStruct((T, BRANCH_W), F32),
                   jax.ShapeDtypeStruct((N_HEADS, T, 1), F32)],
        scratch_shapes=[pltpu.VMEM((tq, tk), F32), pltpu.VMEM((tq, tk), BF16), pltpu.VMEM((tq, tk), BF16),
                        pltpu.VMEM((tq, 1), F32), pltpu.VMEM((2, tq, 1), F32), pltpu.VMEM((2, tq, 1), F32),
                        pltpu.VMEM((2, tq, LANES), F32)],
        compiler_params=_cp(("parallel", "parallel", "parallel")),
    )(qa, ka, proj)


def _attn_bwd_dq2(qa, ka, proj, do, lse, delta, n_seq, name):
    T = qa.shape[0]
    S = T // n_seq
    tq, tk, rg = min(Q_TILE, S), min(K_CHUNK, S), ROW_GROUP
    nq, per = S // tq, tq // tk
    vc = OFF_V // LANES

    def body(q_ref, k_ref, v_ref, do_ref, lse_ref, dl_ref, dq_ref, s_s, dp_s, ds_s, acc_s):
        qi = pl.program_id(2)
        acc_s[...] = jnp.zeros_like(acc_s)
        sels = _pair_masks()

        def chunk(kc, masked):
            k0 = pl.multiple_of(kc * tk, tk)
            v = v_ref[pl.ds(k0, tk), :]
            for hh in range(2):
                hl = slice(hh * LANES, (hh + 1) * LANES)
                kh = k_ref[pl.ds(k0, tk), hl]
                s_s[...] = lax.dot_general(q_ref[:, hl], kh, _NT, preferred_element_type=F32)
                dom = jnp.where(sels[hh], do_ref[...], jnp.zeros_like(do_ref[...]))
                dp_s[...] = lax.dot_general(dom, v, _NT, preferred_element_type=F32)
                for r in range(tq // rg):
                    rows = slice(r * rg, (r + 1) * rg)
                    p = jnp.exp(s_s[rows, :] - lse_ref[hh, rows])
                    if masked:
                        p = jnp.where(_band_mask(qi * tq + r * rg, k0, rg, tk), p, 0.0)
                    ds_s[rows, :] = (p * (dp_s[rows, :] - dl_ref[hh, rows])).astype(BF16)
                acc_s[hh] += jnp.dot(ds_s[...], kh, preferred_element_type=F32)

        def unmasked(kc, carry):
            chunk(kc, False)
            return carry

        lax.fori_loop(0, qi * per, unmasked, 0)
        for d in range(per):
            chunk(qi * per + d, True)
        dq = jnp.where(sels[0], acc_s[0], pltpu.roll(acc_s[1], HEAD_DIM, 1))
        dq_ref[...] = (dq * (HEAD_DIM ** -0.5)).astype(BF16)

    qmap = lambda b, j, qi: (b * nq + qi, j)
    col1 = pl.BlockSpec((2, tq, 1), lambda b, j, qi: (j, b * nq + qi, 0))
    return pl.pallas_call(
        body, name=name, grid=(n_seq, N_HEADS // 2, nq),
        in_specs=[pl.BlockSpec((tq, 2 * LANES), qmap),
                  pl.BlockSpec((S, 2 * LANES), lambda b, j, qi: (b, j)),
                  pl.BlockSpec((S, LANES), lambda b, j, qi: (b, vc + j)),
                  pl.BlockSpec((tq, LANES), qmap), col1, col1],
        out_specs=pl.BlockSpec((tq, LANES), qmap),
        out_shape=jax.ShapeDtypeStruct((T, BRANCH_W), BF16),
        scratch_shapes=[pltpu.VMEM((tq, tk), F32), pltpu.VMEM((tq, tk), F32), pltpu.VMEM((tq, tk), BF16),
                        pltpu.VMEM((2, tq, LANES), F32)],
        compiler_params=_cp(("parallel", "parallel", "parallel")),
    )(qa, ka, proj, do, lse, delta)


def _attn_bwd_dkv2(qa, ka, proj, do, lse, delta, n_seq, name):
    T = qa.shape[0]
    S = T // n_seq
    tkt, tqc, rg = min(Q_TILE, S), min(K_CHUNK, S), ROW_GROUP // 2
    nk, per, nqc = S // tkt, tkt // tqc, S // tqc
    vc = OFF_V // LANES

    def body(q_ref, k_ref, v_ref, do_ref, lse_ref, dl_ref, dk_ref, dv_ref, dfk_ref,
             s_s, dp_s, p_s, ds_s, dk_s, dv_s, df_s):
        ki = pl.program_id(2)
        dk_s[...] = jnp.zeros_like(dk_s)
        dv_s[...] = jnp.zeros_like(dv_s)
        df_s[...] = jnp.zeros_like(df_s)
        sels = _pair_masks()
        v = v_ref[...]

        def chunk(qc, masked):
            q0 = pl.multiple_of(qc * tqc, tqc)
            do_ = do_ref[pl.ds(q0, tqc), :]
            for hh in range(2):
                hl = slice(hh * LANES, (hh + 1) * LANES)
                qh = q_ref[pl.ds(q0, tqc), hl]
                s_s[...] = lax.dot_general(qh, k_ref[:, hl], _NT, preferred_element_type=F32)
                dom = jnp.where(sels[hh], do_, jnp.zeros_like(do_))
                dp_s[...] = lax.dot_general(dom, v, _NT, preferred_element_type=F32)
                dfp = jnp.zeros((1, tkt), F32)
                for r in range(tqc // rg):
                    rows = slice(r * rg, (r + 1) * rg)
                    qrows = pl.ds(q0 + r * rg, rg)
                    p = jnp.exp(s_s[rows, :] - lse_ref[hh, qrows])
                    if masked:
                        p = jnp.where(_band_mask(q0 + r * rg, ki * tkt, rg, tkt), p, 0.0)
                    ds = p * (dp_s[rows, :] - dl_ref[hh, qrows])
                    p_s[rows, :] = p.astype(BF16)
                    ds_s[rows, :] = ds.astype(BF16)
                    dfp = dfp + jnp.sum(ds, axis=0, keepdims=True)
                df_s[hh] -= dfp
                dv_s[hh] += lax.dot_general(p_s[...], do_, _TN, preferred_element_type=F32)
                dk_s[hh] += lax.dot_general(ds_s[...], qh, _TN, preferred_element_type=F32)

        for d in range(per):
            chunk(ki * per + d, True)

        def unmasked(qc, carry):
            chunk(qc, False)
            return carry

        lax.fori_loop((ki + 1) * per, nqc, unmasked, 0)
        dk_ref[...] = jnp.where(sels[0], dk_s[0], pltpu.roll(dk_s[1], HEAD_DIM, 1)).astype(BF16)
        dv_ref[...] = jnp.where(sels[0], dv_s[0], dv_s[1]).astype(BF16)
        dfk_ref[...] = df_s[...]

    kmap = lambda b, j, ki: (b * nk + ki, j)
    col1 = pl.BlockSpec((2, S, 1), lambda b, j, ki: (j, b, 0))
    rowk = pl.BlockSpec((2, 1, tkt), lambda b, j, ki: (j, 0, b * nk + ki))
    return pl.pallas_call(
        body, name=name, grid=(n_seq, N_HEADS // 2, nk),
        in_specs=[pl.BlockSpec((S, 2 * LANES), lambda b, j, ki: (b, j)),
                  pl.BlockSpec((tkt, 2 * LANES), kmap),
                  pl.BlockSpec((tkt, LANES), lambda b, j, ki: (b * nk + ki, vc + j)),
                  pl.BlockSpec((S, LANES), lambda b, j, ki: (b, j)), col1, col1],
        out_specs=[pl.BlockSpec((tkt, LANES), kmap), pl.BlockSpec((tkt, LANES), kmap), rowk],
        out_shape=[jax.ShapeDtypeStruct((T, BRANCH_W), BF16), jax.ShapeDtypeStruct((T, BRANCH_W), BF16),
                   jax.ShapeDtypeStruct((N_HEADS, 1, T), F32)],
        scratch_shapes=[pltpu.VMEM((tqc, tkt), F32), pltpu.VMEM((tqc, tkt), F32), pltpu.VMEM((tqc, tkt), BF16),
                        pltpu.VMEM((tqc, tkt), BF16), pltpu.VMEM((2, tkt, LANES), F32),
                        pltpu.VMEM((2, tkt, LANES), F32), pltpu.VMEM((2, 1, tkt), F32)],
        compiler_params=_cp(("parallel", "parallel", "parallel")),
    )(qa, ka, proj, do, lse, delta)


def _shift_down(v, k, row):
    return jnp.where(row >= k, pltpu.roll(v, k, 0), 0.0)


def _shift_up(v, k, row, S):
    return jnp.where(row < S - k, pltpu.roll(v, S - k, 0), 0.0)


def _pool_diff(uf, w, row):
    acc, k = uf, 1
    while k < w:
        acc = acc + _shift_down(acc, k, row)
        k *= 2
    n = jnp.minimum(row + 1, w).astype(F32)
    return acc / n - uf


def _pool_fwd(proj, pool_w, pool_scale, n_seq, name):
    T = proj.shape[0]
    S = T // n_seq

    def body(u_ref, w_ref, sc_ref, o_ref):
        g = pl.program_id(1)
        row = lax.broadcasted_iota(jnp.int32, (S, POOL_GD), 0)
        uf = u_ref[...].astype(F32)
        d = _pool_diff(uf, POOL_WINDOWS[0], row)
        for gi in range(1, len(POOL_WINDOWS)):
            d = jnp.where(g == gi, _pool_diff(uf, POOL_WINDOWS[gi], row), d)
        e = jnp.dot(d.astype(BF16), w_ref[0], preferred_element_type=F32)
        o_ref[...] = (e * sc_ref[...]).astype(BF16)

    uc = OFF_U // POOL_GD
    return pl.pallas_call(
        body, name=name, grid=(n_seq, len(POOL_WINDOWS)),
        in_specs=[pl.BlockSpec((S, POOL_GD), lambda b, g: (b, uc + g)),
                  pl.BlockSpec((1, POOL_GD, POOL_GD), lambda b, g: (g, 0, 0)),
                  pl.BlockSpec((1, POOL_GD), lambda b, g: (0, g))],
        out_specs=pl.BlockSpec((S, POOL_GD), lambda b, g: (b, g)),
        out_shape=jax.ShapeDtypeStruct((T, BRANCH_W), BF16),
        compiler_params=_cp(("parallel", "parallel")),
    )(proj, pool_w, pool_scale)


def _pool_bwd(proj, dout, pool_w, pool_scale, n_seq, name):
    T = proj.shape[0]
    S = T // n_seq

    def body(u_ref, do_ref, w_ref, sc_ref, du_ref, dw_ref, dsc_ref):
        g, b = pl.program_id(0), pl.program_id(1)
        row = lax.broadcasted_iota(jnp.int32, (S, POOL_GD), 0)
        uf = u_ref[...].astype(F32)
        d = _pool_diff(uf, POOL_WINDOWS[0], row)
        for gi in range(1, len(POOL_WINDOWS)):
            d = jnp.where(g == gi, _pool_diff(uf, POOL_WINDOWS[gi], row), d)
        db16 = d.astype(BF16)
        w = w_ref[0]
        e = jnp.dot(db16, w, preferred_element_type=F32)
        dof = do_ref[...].astype(F32)
        dsc = jnp.sum(dof * e, axis=0, keepdims=True)
        de = (dof * sc_ref[...]).astype(BF16)
        dd = lax.dot_general(de, w, (((1,), (1,)), ((), ())), preferred_element_type=F32)
        dw = lax.dot_general(db16, de, (((0,), (0,)), ((), ())), preferred_element_type=F32)
        du = jnp.zeros_like(dd)
        for gi, wlen in enumerate(POOL_WINDOWS):
            n = jnp.minimum(row + 1, wlen).astype(F32)
            acc, k = dd / n, 1
            while k < wlen:
                acc = acc + _shift_up(acc, k, row, S)
                k *= 2
            du = jnp.where(g == gi, acc - dd, du)
        du_ref[...] = du.astype(BF16)

        @pl.when(b == 0)
        def _():
            dw_ref[0] = dw
            dsc_ref[...] = dsc

        @pl.when(b > 0)
        def _():
            dw_ref[0] += dw
            dsc_ref[...] += dsc

    uc = OFF_U // POOL_GD
    return pl.pallas_call(
        body, name=name, grid=(len(POOL_WINDOWS), n_seq),
        in_specs=[pl.BlockSpec((S, POOL_GD), lambda g, b: (b, uc + g)),
                  pl.BlockSpec((S, POOL_GD), lambda g, b: (b, g)),
                  pl.BlockSpec((1, POOL_GD, POOL_GD), lambda g, b: (g, 0, 0)),
                  pl.BlockSpec((1, POOL_GD), lambda g, b: (0, g))],
        out_specs=[pl.BlockSpec((S, POOL_GD), lambda g, b: (b, g)),
                   pl.BlockSpec((1, POOL_GD, POOL_GD), lambda g, b: (g, 0, 0)),
                   pl.BlockSpec((1, POOL_GD), lambda g, b: (0, g))],
        out_shape=[jax.ShapeDtypeStruct((T, BRANCH_W), BF16),
                   jax.ShapeDtypeStruct((len(POOL_WINDOWS), POOL_GD, POOL_GD), F32),
                   jax.ShapeDtypeStruct((1, BRANCH_W), F32)],
        compiler_params=_cp(("parallel", "arbitrary")),
    )(proj, dout, pool_w, pool_scale)


def _conv_fwd(proj, conv_w, n_seq, name):
    T = proj.shape[0]
    S = T // n_seq
    nc = BRANCH_W // LANES

    def body(cv_ref, cb_ref, cc_ref, w_ref, o_ref):
        row = lax.broadcasted_iota(jnp.int32, (S, LANES), 0)
        z = cc_ref[...].astype(F32) * cv_ref[...].astype(F32)
        w = w_ref[...]
        y = w[0:1] * _shift_down(z, 2, row) + w[1:2] * _shift_down(z, 1, row) + w[2:3] * z
        o_ref[...] = (cb_ref[...].astype(F32) * y).astype(BF16)

    def col(off):
        return pl.BlockSpec((S, LANES), lambda b, j: (b, off // LANES + j))

    return pl.pallas_call(
        body, name=name, grid=(n_seq, nc),
        in_specs=[col(OFF_CV), col(OFF_CB), col(OFF_CC), pl.BlockSpec((CONV_K, LANES), lambda b, j: (0, j))],
        out_specs=pl.BlockSpec((S, LANES), lambda b, j: (b, j)),
        out_shape=jax.ShapeDtypeStruct((T, BRANCH_W), BF16),
        compiler_params=_cp(("parallel", "parallel")),
    )(proj, proj, proj, conv_w)


def _conv_bwd(proj, dout, conv_w, n_seq, name):
    T = proj.shape[0]
    S = T // n_seq
    nc = BRANCH_W // LANES

    def body(cv_ref, cb_ref, cc_ref, do_ref, w_ref, dcv_ref, dcb_ref, dcc_ref, dw_ref):
        b = pl.program_id(1)
        row = lax.broadcasted_iota(jnp.int32, (S, LANES), 0)
        cv, cb, cc = cv_ref[...].astype(F32), cb_ref[...].astype(F32), cc_ref[...].astype(F32)
        dof = do_ref[...].astype(F32)
        w = w_ref[...]
        z = cc * cv
        z1, z2 = _shift_down(z, 1, row), _shift_down(z, 2, row)
        y = w[0:1] * z2 + w[1:2] * z1 + w[2:3] * z
        dcb_ref[...] = (dof * y).astype(BF16)
        dy = dof * cb
        dz = w[2:3] * dy + w[1:2] * _shift_up(dy, 1, row, S) + w[0:1] * _shift_up(dy, 2, row, S)
        dcc_ref[...] = (dz * cv).astype(BF16)
        dcv_ref[...] = (dz * cc).astype(BF16)
        dws = [jnp.sum(dy * zk, axis=0, keepdims=True) for zk in (z2, z1, z)]

        @pl.when(b == 0)
        def _():
            for kk in range(CONV_K):
                dw_ref[kk:kk + 1, :] = dws[kk]

        @pl.when(b > 0)
        def _():
            for kk in range(CONV_K):
                dw_ref[kk:kk + 1, :] += dws[kk]

    def col(off):
        return pl.BlockSpec((S, LANES), lambda j, b: (b, off // LANES + j))

    out = pl.BlockSpec((S, LANES), lambda j, b: (b, j))
    wsp = pl.BlockSpec((CONV_K, LANES), lambda j, b: (0, j))
    act = jax.ShapeDtypeStruct((T, BRANCH_W), BF16)
    return pl.pallas_call(
        body, name=name, grid=(nc, n_seq),
        in_specs=[col(OFF_CV), col(OFF_CB), col(OFF_CC), out, wsp],
        out_specs=[out, out, out, wsp],
        out_shape=[act, act, act, jax.ShapeDtypeStruct((CONV_K, BRANCH_W), F32)],
        compiler_params=_cp(("parallel", "arbitrary")),
    )(proj, proj, proj, dout, conv_w)


def _mix_fwd(oa, ob, oc, wpa, wpp, wpc, proj, b_gate, name):
    T = oa.shape[0]
    tm = min(256, T)

    def body(oa_ref, ob_ref, oc_ref, wa_ref, wp_ref, wc_ref, g_ref, bg_ref, o_ref):
        acc = jnp.zeros((tm, D_MODEL), F32)
        for i, (x_ref, w_ref) in enumerate(((oa_ref, wa_ref), (ob_ref, wp_ref), (oc_ref, wc_ref))):
            y = jnp.dot(x_ref[...], w_ref[...], preferred_element_type=F32)
            sl = slice(i * D_MODEL, (i + 1) * D_MODEL)
            acc = acc + _sigmoid(g_ref[:, sl].astype(F32) + bg_ref[:, sl]) * y
        o_ref[...] = acc.astype(BF16)

    br = pl.BlockSpec((tm, BRANCH_W), lambda i: (i, 0))
    wsp = pl.BlockSpec((BRANCH_W, D_MODEL), lambda i: (0, 0))
    return pl.pallas_call(
        body, name=name, grid=(T // tm,),
        in_specs=[br, br, br, wsp, wsp, wsp, pl.BlockSpec((tm, GATE_W), lambda i: (i, 0)),
                  pl.BlockSpec((1, GATE_W), lambda i: (0, 0))],
        out_specs=pl.BlockSpec((tm, D_MODEL), lambda i: (i, 0)),
        out_shape=jax.ShapeDtypeStruct((T, D_MODEL), BF16),
        compiler_params=_cp(("parallel",)),
    )(oa, ob, oc, wpa, wpp, wpc, proj, b_gate)


def _mix_bwd(oa, ob, oc, wpa, wpp, wpc, proj, b_gate, dmixed, name):
    T = oa.shape[0]
    tm = min(256, T)

    def body(oa_ref, ob_ref, oc_ref, wa_ref, wp_ref, wc_ref, g_ref, bg_ref, dm_ref,
             dya_ref, dyb_ref, dyc_ref, dg_ref, dbg_ref):
        i0 = pl.program_id(0)
        dm = dm_ref[...].astype(F32)
        parts = []
        for i, (x_ref, w_ref, dy_ref) in enumerate(((oa_ref, wa_ref, dya_ref), (ob_ref, wp_ref, dyb_ref),
                                                    (oc_ref, wc_ref, dyc_ref))):
            y = jnp.dot(x_ref[...], w_ref[...], preferred_element_type=F32)
            sl = slice(i * D_MODEL, (i + 1) * D_MODEL)
            gate = _sigmoid(g_ref[:, sl].astype(F32) + bg_ref[:, sl])
            dy_ref[...] = (dm * gate).astype(BF16)
            dgl = dm * y * gate * (1.0 - gate)
            dg_ref[:, sl] = dgl.astype(BF16)
            parts.append(jnp.sum(dgl, axis=0, keepdims=True))

        @pl.when(i0 == 0)
        def _():
            for i in range(3):
                dbg_ref[:, i * D_MODEL:(i + 1) * D_MODEL] = parts[i]

        @pl.when(i0 > 0)
        def _():
            for i in range(3):
                dbg_ref[:, i * D_MODEL:(i + 1) * D_MODEL] += parts[i]

    br = pl.BlockSpec((tm, BRANCH_W), lambda i: (i, 0))
    wsp = pl.BlockSpec((BRANCH_W, D_MODEL), lambda i: (0, 0))
    row = pl.BlockSpec((tm, D_MODEL), lambda i: (i, 0))
    gsp = pl.BlockSpec((tm, GATE_W), lambda i: (i, 0))
    bsp = pl.BlockSpec((1, GATE_W), lambda i: (0, 0))
    act = jax.ShapeDtypeStruct((T, D_MODEL), BF16)
    return pl.pallas_call(
        body, name=name, grid=(T // tm,),
        in_specs=[br, br, br, wsp, wsp, wsp, gsp, bsp, row],
        out_specs=[row, row, row, gsp, bsp],
        out_shape=[act, act, act, jax.ShapeDtypeStruct((T, GATE_W), BF16),
                   jax.ShapeDtypeStruct((1, GATE_W), F32)],
        compiler_params=_cp(("arbitrary",)),
    )(oa, ob, oc, wpa, wpp, wpc, proj, b_gate, dmixed)


def _swiglu_fwd(ab, name):
    T = ab.shape[0]
    tm = min(256, T)

    def body(ab_ref, o_ref):
        a = ab_ref[:, :FFN_HIDDEN].astype(F32)
        o_ref[...] = (a * _sigmoid(a) * ab_ref[:, FFN_HIDDEN:].astype(F32)).astype(BF16)

    return pl.pallas_call(
        body, name=name, grid=(T // tm,),
        in_specs=[pl.BlockSpec((tm, 2 * FFN_HIDDEN), lambda i: (i, 0))],
        out_specs=pl.BlockSpec((tm, FFN_HIDDEN), lambda i: (i, 0)),
        out_shape=jax.ShapeDtypeStruct((T, FFN_HIDDEN), BF16),
        compiler_params=_cp(("parallel",)),
    )(ab)


def _swiglu_bwd(ab, ds, name):
    T = ab.shape[0]
    tm = min(256, T)

    def body(ab_ref, ds_ref, o_ref):
        a = ab_ref[:, :FFN_HIDDEN].astype(F32)
        b = ab_ref[:, FFN_HIDDEN:].astype(F32)
        dsf = ds_ref[...].astype(F32)
        sg = _sigmoid(a)
        o_ref[:, :FFN_HIDDEN] = (dsf * b * sg * (1.0 + a * (1.0 - sg))).astype(BF16)
        o_ref[:, FFN_HIDDEN:] = (dsf * a * sg).astype(BF16)

    full = pl.BlockSpec((tm, 2 * FFN_HIDDEN), lambda i: (i, 0))
    return pl.pallas_call(
        body, name=name, grid=(T // tm,),
        in_specs=[full, pl.BlockSpec((tm, FFN_HIDDEN), lambda i: (i, 0))],
        out_specs=full,
        out_shape=jax.ShapeDtypeStruct((T, 2 * FFN_HIDDEN), BF16),
        compiler_params=_cp(("parallel",)),
    )(ab, ds)


def _adamw(w, g, m, v, name):
    R, C = w.shape
    tr = R
    for cand in (256, 352, 128, 64, 8):
        if R > cand and R % cand == 0:
            tr = cand
            break

    def body(w_ref, g_ref, m_ref, v_ref, d_ref, nm_ref, nv_ref):
        gv = g_ref[...]
        nm = ADAM_B1 * m_ref[...] + (1.0 - ADAM_B1) * gv
        nv = ADAM_B2 * v_ref[...] + (1.0 - ADAM_B2) * (gv * gv)
        m_hat = nm / (1.0 - ADAM_B1 ** ADAM_STEP)
        v_hat = nv / (1.0 - ADAM_B2 ** ADAM_STEP)
        d_ref[...] = -ADAM_LR * (m_hat / (jnp.sqrt(v_hat) + ADAM_EPS) + ADAM_WD * w_ref[...])
        nm_ref[...] = nm
        nv_ref[...] = nv

    blk = pl.BlockSpec((tr, C), lambda i: (i, 0))
    sh = jax.ShapeDtypeStruct((R, C), F32)
    return pl.pallas_call(
        body, name=name, grid=(R // tr,), in_specs=[blk] * 4, out_specs=[blk] * 3, out_shape=[sh] * 3,
        compiler_params=_cp(("parallel",)),
    )(w, g, m, v)


def _sum_slabs(x, name):
    n, R, C = x.shape
    tr = R
    for cand in (512, 256, 128, 64, 32, 16, 8):
        if R > cand and R % cand == 0:
            tr = cand
            break

    def body(x_ref, o_ref):
        acc = x_ref[0].astype(F32)
        for j in range(1, n):
            acc = acc + x_ref[j].astype(F32)
        o_ref[...] = acc

    return pl.pallas_call(
        body, name=name, grid=(R // tr,), in_specs=[pl.BlockSpec((n, tr, C), lambda i: (0, i, 0))],
        out_specs=pl.BlockSpec((tr, C), lambda i: (i, 0)), out_shape=jax.ShapeDtypeStruct((R, C), F32),
        compiler_params=_cp(("parallel",)),
    )(x)


def _multi_gather(xs, layers, name):
    nt = len(xs)
    shapes = [x.shape if lay is None else x.shape[1:] for x, lay in zip(xs, layers)]

    def body(*refs):
        x_refs, out_refs = refs[:nt], refs[nt:2 * nt]
        send_sems, recv_sems, local_sems = refs[2 * nt:]
        x_, y_, c_ = lax.axis_index("x"), lax.axis_index("y"), lax.axis_index("c")
        me, sibling = (x_, y_, c_), (x_, y_, 1 - c_)
        chips = [(1 - x_, y_), (x_, 1 - y_), (1 - x_, 1 - y_)]

        def own_block(t):
            return x_refs[t] if layers[t] is None else x_refs[t].at[layers[t]]

        def copy(t, k, block, to, own=False):
            px, py, pc = block
            dst = out_refs[t].at[4 * px + 2 * py + pc]
            return pltpu.make_async_remote_copy(
                src_ref=own_block(t) if own else dst, dst_ref=dst,
                send_sem=send_sems.at[t, k], recv_sem=recv_sems.at[t, k],
                device_id=to, device_id_type=pl.DeviceIdType.MESH)

        mine, first, passed = [], [], []
        for t in range(nt):
            mine.append(pltpu.make_async_copy(own_block(t), out_refs[t].at[4 * x_ + 2 * y_ + c_], local_sems.at[t]))
            mine[-1].start()
            first.append([copy(t, 1 + j, me, (*chip, c_), own=True) for j, chip in enumerate(chips)]
                         + [copy(t, 0, me, sibling, own=True)])
            for cp in first[-1]:
                cp.start()
        for t in range(nt):
            for j, chip in enumerate(chips):
                copy(t, 1 + j, (*chip, c_), me).wait_recv()
                passed.append(copy(t, 4 + j, (*chip, c_), sibling))
                passed[-1].start()
        for t in range(nt):
            copy(t, 0, sibling, me).wait_recv()
            for j, chip in enumerate(chips):
                copy(t, 4 + j, (*chip, 1 - c_), me).wait_recv()
        for cp in [c for f in first for c in f] + passed:
            cp.wait_send()
        for cp in mine:
            cp.wait()

    hbm = pl.BlockSpec(memory_space=pl.ANY)
    return pl.pallas_call(
        body, name=name, out_shape=[jax.ShapeDtypeStruct((N_DEV,) + tuple(s), x.dtype) for s, x in zip(shapes, xs)],
        in_specs=[hbm] * nt, out_specs=[hbm] * nt,
        scratch_shapes=[pltpu.SemaphoreType.DMA((nt, 7)), pltpu.SemaphoreType.DMA((nt, 7)),
                        pltpu.SemaphoreType.DMA((nt,))],
    )(*xs)


def _multi_exchange(sends, name):
    nt = len(sends)

    def body(*refs):
        s_refs, r_refs = refs[:nt], refs[nt:2 * nt]
        send_sems, recv_sems, local_sems = refs[2 * nt:]
        x_, y_, c_ = lax.axis_index("x"), lax.axis_index("y"), lax.axis_index("c")
        me = 4 * x_ + 2 * y_ + c_
        mine, out, inc = [], [], []
        for t in range(nt):
            mine.append(pltpu.make_async_copy(s_refs[t].at[me], r_refs[t].at[me], local_sems.at[t]))
            mine[-1].start()
        for k in (2, 4, 6, 3, 5, 7, 1):
            px, py, pc = x_ ^ ((k >> 2) & 1), y_ ^ ((k >> 1) & 1), c_ ^ (k & 1)
            peer = 4 * px + 2 * py + pc
            for t in range(nt):
                def copy(src, dst):
                    return pltpu.make_async_remote_copy(
                        src_ref=s_refs[t].at[src], dst_ref=r_refs[t].at[dst],
                        send_sem=send_sems.at[t, k - 1], recv_sem=recv_sems.at[t, k - 1],
                        device_id=(px, py, pc), device_id_type=pl.DeviceIdType.MESH)

                out.append(copy(peer, me))
                inc.append(copy(me, peer))
        for cp in out:
            cp.start()
        for cp in inc:
            cp.wait_recv()
        for cp in out:
            cp.wait_send()
        for cp in mine:
            cp.wait()

    hbm = pl.BlockSpec(memory_space=pl.ANY)
    return pl.pallas_call(
        body, name=name, out_shape=[jax.ShapeDtypeStruct(s.shape, s.dtype) for s in sends],
        in_specs=[hbm] * nt, out_specs=[hbm] * nt,
        scratch_shapes=[pltpu.SemaphoreType.DMA((nt, N_DEV - 1)), pltpu.SemaphoreType.DMA((nt, N_DEV - 1)),
                        pltpu.SemaphoreType.DMA((nt,))],
    )(*sends)


_HBM = pl.BlockSpec(memory_space=pltpu.HBM)
_SEM = pl.BlockSpec(memory_space=pltpu.SEMAPHORE)
_PEER_ORDER = (2, 4, 6, 3, 5, 7, 1)


def _split_copies(src_refs, land_refs, send_sems, recv_sems, layers, per_peer):
    x_, y_, c_ = lax.axis_index("x"), lax.axis_index("y"), lax.axis_index("c")
    me = 4 * x_ + 2 * y_ + c_
    copies = []
    for k in _PEER_ORDER:
        px, py, pc = x_ ^ ((k >> 2) & 1), y_ ^ ((k >> 1) & 1), c_ ^ (k & 1)
        peer = 4 * px + 2 * py + pc
        for t in range(len(src_refs)):
            if per_peer:
                src = src_refs[t].at[peer]
            else:
                src = src_refs[t] if layers[t] is None else src_refs[t].at[layers[t]]
            copies.append(pltpu.make_async_remote_copy(
                src_ref=src, dst_ref=land_refs[t].at[me],
                send_sem=send_sems.at[t * (N_DEV - 1) + k - 1], recv_sem=recv_sems.at[t * (N_DEV - 1) + k - 1],
                device_id=(px, py, pc), device_id_type=pl.DeviceIdType.MESH))
    return copies


def _split_start(srcs, layers, per_peer, after, name):
    nt = len(srcs)
    if per_peer:
        land_shapes = [s.shape for s in srcs]
    else:
        land_shapes = [(N_DEV,) + tuple(s.shape if lay is None else s.shape[1:]) for s, lay in zip(srcs, layers)]

    def body(*refs):
        src_refs, land_refs = refs[:nt], refs[nt:2 * nt]
        send_sems, recv_sems = refs[2 * nt + 1], refs[2 * nt + 2]
        token = refs[-1]
        for cp in _split_copies(src_refs, land_refs, send_sems, recv_sems, layers, per_peer):
            cp.start()
        token[...] = jnp.zeros_like(token)

    lands = [pltpu.with_memory_space_constraint(lax.empty(s, x.dtype), pltpu.HBM) for s, x in zip(land_shapes, srcs)]
    srcs = [pltpu.with_memory_space_constraint(x, pltpu.HBM) for x in srcs]
    out = pl.pallas_call(
        body, name=name,
        out_shape=(pltpu.SemaphoreType.DMA((nt * (N_DEV - 1),)), pltpu.SemaphoreType.DMA((nt * (N_DEV - 1),)),
                   *[pltpu.HBM(x.shape, x.dtype) for x in srcs], *[pltpu.HBM(s, x.dtype) for s, x in zip(land_shapes, srcs)],
                   jax.ShapeDtypeStruct((8, LANES), F32)),
        in_specs=[_HBM] * (2 * nt) + [pl.BlockSpec(memory_space=pl.ANY)],
        out_specs=(_SEM, _SEM, *([_HBM] * (2 * nt)), pl.BlockSpec(memory_space=pltpu.VMEM)),
        input_output_aliases={i: 2 + i for i in range(2 * nt)},
        compiler_params=pltpu.CompilerParams(has_side_effects=pltpu.SideEffectType.DATAFLOW_SIDE_EFFECTING),
    )(*srcs, *lands, after)
    return out[0], out[1], list(out[2:2 + nt]), list(out[2 + nt:2 + 2 * nt]), out[-1]


def _split_wait(started, layers, per_peer, after, name):
    send_sems, recv_sems, srcs, lands, _ = started
    nt = len(srcs)

    def body(*refs):
        src_refs, land_refs = refs[:nt], refs[nt:2 * nt]
        s_sems, r_sems = refs[2 * nt], refs[2 * nt + 1]
        for cp in _split_copies(src_refs, land_refs, s_sems, r_sems, layers, per_peer):
            cp.wait_send()
            cp.wait_recv()

    out = pl.pallas_call(
        body, name=name,
        out_shape=tuple(pltpu.HBM(x.shape, x.dtype) for x in srcs + lands),
        in_specs=[_HBM] * (2 * nt) + [_SEM, _SEM, pl.BlockSpec(memory_space=pl.ANY)],
        out_specs=tuple([_HBM] * (2 * nt)),
        input_output_aliases={i: i for i in range(2 * nt)},
        compiler_params=pltpu.CompilerParams(has_side_effects=pltpu.SideEffectType.DATAFLOW_SIDE_EFFECTING),
    )(*srcs, *lands, send_sems, recv_sems, after)
    return list(out[nt:])


def _with_own(land, own):
    me = 4 * lax.axis_index("x") + 2 * lax.axis_index("y") + lax.axis_index("c")
    return lax.dynamic_update_slice_in_dim(land, own[None], me, axis=0)


def _runs(mapping):
    runs, c, n = [], 0, len(mapping)
    while c < n:
        if mapping[c] is None:
            c += 1
            continue
        sid, d, lo = mapping[c][0], mapping[c][1] - c, c
        while c < n and mapping[c] is not None and mapping[c][0] == sid and mapping[c][1] - c == d:
            c += 1
        runs.append((lo, c, sid, d))
    return runs


def _tile_plan(mapping, src_widths):
    runs = _runs(mapping)
    plan = []
    for t in range(len(mapping) // LANES):
        pieces = []
        for lo, hi, sid, d in runs:
            lo_t, hi_t = max(lo, t * LANES), min(hi, (t + 1) * LANES)
            if lo_t >= hi_t:
                continue
            a = ((lo_t + d) // LANES) * LANES
            win = min(2 * LANES, src_widths[sid] - a)
            shift = t * LANES + d - a
            pieces.append((sid, a, win, shift, lo_t - t * LANES, hi_t - t * LANES))
        plan.append(pieces)
    return plan


def _reblock(srcs, src_views, outs, out_views, name):
    R = srcs[0].shape[-2]
    tr = min(256, R)
    widths = {sid: srcs[ai].shape[-1] for sid, (ai, _) in src_views.items()}
    plans = [(ai, li, _tile_plan(mapping, widths)) for ai, li, mapping in out_views]
    ns = len(srcs)

    def body(*refs):
        s_refs, o_refs = refs[:ns], refs[ns:]
        cache = {}

        def shift_matrix(win, shift, lo, hi):
            key = (win, shift, lo, hi)
            if key not in cache:
                r = lax.broadcasted_iota(jnp.int32, (win, LANES), 0)
                c = lax.broadcasted_iota(jnp.int32, (win, LANES), 1)
                hit = jnp.logical_and(r - c == shift, jnp.logical_and(c >= lo, c < hi))
                cache[key] = jnp.where(hit, 1.0, 0.0).astype(BF16)
            return cache[key]

        for ai, li, plan in plans:
            for t, pieces in enumerate(plan):
                acc = None
                for sid, a, win, shift, lo, hi in pieces:
                    sa, sl = src_views[sid]
                    src = s_refs[sa][:, a:a + win] if sl is None else s_refs[sa][sl, :, a:a + win]
                    part = jnp.dot(src, shift_matrix(win, shift, lo, hi), preferred_element_type=F32)
                    acc = part if acc is None else acc + part
                val = jnp.zeros((tr, LANES), BF16) if acc is None else acc.astype(BF16)
                if li is None:
                    o_refs[ai][:, t * LANES:(t + 1) * LANES] = val
                else:
                    o_refs[ai][li, :, t * LANES:(t + 1) * LANES] = val

    def spec(shape):
        if len(shape) == 2:
            return pl.BlockSpec((tr, shape[1]), lambda i: (i, 0))
        return pl.BlockSpec((shape[0], tr, shape[2]), lambda i: (0, i, 0))

    return pl.pallas_call(
        body, name=name, grid=(R // tr,), in_specs=[spec(s.shape) for s in srcs],
        out_specs=[spec(s) for s in outs], out_shape=[jax.ShapeDtypeStruct(s, BF16) for s in outs],
        compiler_params=_cp(("parallel",)),
    )(*srcs)


SHARDED = ("w_in", "w_gate_up", "w_proj_attn", "w_proj_pool", "w_proj_conv", "w_out", "w_down")
WEIGHT_ORDER = ("attn_norm", "w_in", "b_forget", "b_gate", "w_proj_attn", "pool_w", "pool_scale", "w_proj_pool",
                "conv_w", "w_proj_conv", "w_out", "ffn_norm", "w_gate_up", "w_down", "final_norm")
IN_SHARD, IN_SHARD_PAD = IN_COLS // N_DEV, 896
GU_SHARD, GU_SHARD_PAD = 2 * FFN_HIDDEN // N_DEV, 768


def _w_in_col(c):
    if c < GATE_W:
        return c + 3592
    if c < OFF_U:
        return c - OFF_Q
    return c - OFF_U + 1544


def _w_in_full(gathered, name):
    main = [divmod(_w_in_col(c), IN_SHARD) for c in range(MAIN_COLS)]
    fcols = [divmod(1536 + c, IN_SHARD) if c < N_HEADS else None for c in range(LANES)]
    R = gathered.shape[1]
    return _reblock([gathered], {i: (0, i) for i in range(N_DEV)}, [(R, MAIN_COLS), (R, LANES)],
                    [(0, None, main), (1, None, fcols)], name)


def _w_in_slabs(dmain, dwf, name):
    inv = {_w_in_col(c): ("m", c) for c in range(MAIN_COLS)}
    inv.update({1536 + c: ("f", c) for c in range(N_HEADS)})
    views = []
    for i in range(N_DEV):
        mapping = [inv[IN_SHARD * i + j] if j < IN_SHARD else None for j in range(IN_SHARD_PAD)]
        views.append((0, i, mapping))
    R = dmain.shape[0]
    return _reblock([dmain, dwf], {"m": (0, None), "f": (1, None)}, [(N_DEV, R, IN_SHARD_PAD)], views, name)[0]


def _w_gu_full(gathered, name):
    mapping = [divmod(c, GU_SHARD) for c in range(2 * FFN_HIDDEN)]
    R = gathered.shape[1]
    return _reblock([gathered], {i: (0, i) for i in range(N_DEV)}, [(R, 2 * FFN_HIDDEN)], [(0, None, mapping)], name)[0]


def _w_gu_slabs(dw, name):
    views = [(0, i, [("w", GU_SHARD * i + j) if j < GU_SHARD else None for j in range(GU_SHARD_PAD)])
             for i in range(N_DEV)]
    R = dw.shape[0]
    return _reblock([dw], {"w": (0, None)}, [(N_DEV, R, GU_SHARD_PAD)], views, name)[0]


def _layer_fwd(x, W, n_seq, l):
    T = x.shape[0]
    sfx = f"_l{l}"
    h1 = _rms_fwd(x, W["attn_norm"], "rms1" + sfx)
    proj = _matmul(h1, W["w_main"], mode="nn", out_dtype=BF16, name="proj_main" + sfx)
    f = _matmul(h1, W["w_f"], mode="nn", out_dtype=F32, name="proj_f" + sfx)
    qa, ka = _fox_prep(f, W["b_forget"], proj, n_seq, "fox_prep" + sfx)
    oa, oa32, lse = _attn_fwd2(qa, ka, proj, n_seq, "attn_fwd" + sfx)
    ob = _pool_fwd(proj, W["pool_w"], W["pool_scale"], n_seq, "pool_fwd" + sfx)
    oc = _conv_fwd(proj, W["conv_w"], n_seq, "conv_fwd" + sfx)
    mixed = _mix_fwd(oa, ob, oc, W["w_proj_attn"], W["w_proj_pool"], W["w_proj_conv"], proj, W["b_gate"],
                     "mix_fwd" + sfx)
    x2 = _matmul(mixed, W["w_out"], mode="nn", out_dtype=F32, name="out_proj" + sfx, residual=x)
    h2 = _rms_fwd(x2, W["ffn_norm"], "rms2" + sfx)
    ab = _matmul(h2, W["w_gate_up"], mode="nn", out_dtype=BF16, name="gate_up" + sfx)
    s = _swiglu_fwd(ab, "swiglu_fwd" + sfx)
    x3 = _matmul(s, W["w_down"], mode="nn", out_dtype=F32, name="down" + sfx, tm=1024, tn=1024, tk=1408,
                 residual=x2)
    saved = dict(x=x, h1=h1, proj=proj, f=f, qa=qa, ka=ka, oa=oa, oa32=oa32, lse=lse, ob=ob, oc=oc, mixed=mixed, x2=x2,
                 h2=h2, ab=ab, s=s)
    return x3, saved


def _layer_bwd(dx3, dx3b, W, sv, n_seq, l):
    T = dx3.shape[0]
    sfx = f"_l{l}"
    G = {}
    ds = _matmul(dx3b, W["w_down"], mode="nt", out_dtype=BF16, name="d_s" + sfx, tm=1024, tn=1408)
    G["w_down"] = _matmul(sv["s"], dx3b, mode="tn", out_dtype=BF16, name="dw_down" + sfx, tm=256, tn=1024)
    dab = _swiglu_bwd(sv["ab"], ds, "swiglu_bwd" + sfx)
    dh2 = _matmul(dab, W["w_gate_up"], mode="nt", out_dtype=BF16, name="d_h2" + sfx, tm=1024, tn=1024, tk=1408)
    G["w_gate_up"] = _matmul(sv["h2"], dab, mode="tn", out_dtype=BF16, name="dw_gate_up" + sfx, tm=1024)
    dx2, dx2b, G["ffn_norm"] = _rms_bwd(sv["x2"], W["ffn_norm"], dh2, dx3, "rms2_bwd" + sfx)
    dmixed = _matmul(dx2b, W["w_out"], mode="nt", out_dtype=BF16, name="d_mixed" + sfx)
    G["w_out"] = _matmul(sv["mixed"], dx2b, mode="tn", out_dtype=BF16, name="dw_out" + sfx, tm=1024)
    dya, dyb, dyc, dg, G["b_gate"] = _mix_bwd(sv["oa"], sv["ob"], sv["oc"], W["w_proj_attn"], W["w_proj_pool"],
                                              W["w_proj_conv"], sv["proj"], W["b_gate"], dmixed, "mix_bwd" + sfx)
    douts = {}
    for br, dy, o in (("attn", dya, sv["oa"]), ("pool", dyb, sv["ob"]), ("conv", dyc, sv["oc"])):
        douts[br] = _matmul(dy, W["w_proj_" + br], mode="nt", out_dtype=BF16, name=f"d_{br}_out" + sfx)
        G["w_proj_" + br] = _matmul(o, dy, mode="tn", out_dtype=BF16, name=f"dw_proj_{br}" + sfx, tm=512)
    dcv, dcb, dcc, G["conv_w"] = _conv_bwd(sv["proj"], douts["conv"], W["conv_w"], n_seq, "conv_bwd" + sfx)
    du, G["pool_w"], G["pool_scale"] = _pool_bwd(sv["proj"], douts["pool"], W["pool_w"], W["pool_scale"], n_seq,
                                                 "pool_bwd" + sfx)
    delta = _attn_delta(douts["attn"], sv["oa32"], "attn_delta" + sfx)
    dq = _attn_bwd_dq2(sv["qa"], sv["ka"], sv["proj"], douts["attn"], sv["lse"], delta, n_seq, "attn_dq" + sfx)
    dk, dv, dFk = _attn_bwd_dkv2(sv["qa"], sv["ka"], sv["proj"], douts["attn"], sv["lse"], delta, n_seq,
                                 "attn_dkv" + sfx)
    dF = jnp.pad(dFk.reshape(N_HEADS, T).T, ((0, 0), (0, LANES - N_HEADS)))
    df, G["b_forget"] = _fox_cumsum_bwd(sv["f"], W["b_forget"], dF, n_seq, "fox_cumsum_bwd" + sfx)
    dproj = jnp.concatenate([dg, dq, dk, dv, du, dcv, dcb, dcc], axis=1)
    dh1 = _matmul(dproj, W["w_main"], mode="nt", out_dtype=F32, name="d_h1_main" + sfx, tm=1024, tn=1024, tk=1664)
    dh1 = _matmul(df, W["w_f"], mode="nt", out_dtype=F32, name="d_h1_f" + sfx, residual=dh1)
    G["w_main"] = _matmul(sv["h1"], dproj, mode="tn", out_dtype=BF16, name="dw_main" + sfx, tm=1024)
    G["w_f"] = _matmul(sv["h1"], df, mode="tn", out_dtype=BF16, name="dw_f" + sfx, tm=1024)
    dx, dxb, G["attn_norm"] = _rms_bwd(sv["x"], W["attn_norm"], dh1, dx2, "rms1_bwd" + sfx)
    return dx, dxb, G


def _replicated_operands(rep, l):
    W = {}
    W["attn_norm"], W["ffn_norm"] = rep["attn_norm"][l], rep["ffn_norm"][l]
    W["b_forget"] = jnp.pad(rep["b_forget"][l].reshape(1, N_HEADS), ((0, 0), (0, LANES - N_HEADS)))
    W["b_gate"] = rep["b_gate"][l].reshape(1, GATE_W)
    W["pool_w"] = rep["pool_w"][l].astype(BF16)
    W["pool_scale"] = rep["pool_scale"][l].reshape(1, BRANCH_W)
    return W


def _local_step(x, target, get_W, final_norm, on_layer_grads=None):
    n_seq, S, Dm = x.shape
    T = n_seq * S
    xt = x.reshape(T, Dm)
    saved, Ws = [], []
    for l in range(DEPTH):
        Ws.append(get_W(l, xt))
        xt, sv = _layer_fwd(xt, Ws[l], n_seq, l)
        saved.append(sv)
    loss, dx, dxb, g_final = _loss_head(xt, final_norm, target.reshape(T, Dm), "loss_head")
    grads = [None] * DEPTH
    for l in reversed(range(DEPTH)):
        dx, dxb, grads[l] = _layer_bwd(dx, dxb, Ws[l], saved[l], n_seq, l)
        if on_layer_grads is not None:
            grads[l] = on_layer_grads(l, grads[l], Ws)
    return loss, dx.reshape(n_seq, S, Dm), grads, g_final


def _padded_shards(weights):
    sh = {n: weights[n].astype(BF16) for n in SHARDED}
    sh["w_in"] = jnp.pad(sh["w_in"], ((0, 0), (0, 0), (0, IN_SHARD_PAD - IN_SHARD)))
    sh["w_gate_up"] = jnp.pad(sh["w_gate_up"], ((0, 0), (0, 0), (0, GU_SHARD_PAD - GU_SHARD)))
    return sh


def _gather_layer(sh, conv_w, l):
    names = list(SHARDED)
    xs, layers = [sh[n] for n in names], [l] * len(names)
    if l == 0:
        xs, layers = xs + [conv_w], layers + [None]
    got = _multi_gather(xs, layers, f"gather_weights_l{l}")
    return _full_operands(dict(zip(names, got)), l), (got[-1] if l == 0 else None)


def _full_operands(g, l):
    W = {}
    W["w_main"], W["w_f"] = _w_in_full(g["w_in"], f"w_in_full_l{l}")
    W["w_gate_up"] = _w_gu_full(g["w_gate_up"], f"w_gate_up_full_l{l}")
    for n in ("w_proj_attn", "w_proj_pool", "w_proj_conv"):
        W[n] = jnp.transpose(g[n], (1, 0, 2)).reshape(BRANCH_W, D_MODEL)
    W["w_out"] = g["w_out"].reshape(D_MODEL, D_MODEL)
    W["w_down"] = g["w_down"].reshape(FFN_HIDDEN, D_MODEL)
    return W


def _grad_slabs(G, l):
    slabs = {
        "w_in": _w_in_slabs(G["w_main"], G["w_f"], f"w_in_slabs_l{l}"),
        "w_gate_up": _w_gu_slabs(G["w_gate_up"], f"w_gate_up_slabs_l{l}"),
        "w_out": G["w_out"].reshape(N_DEV, D_MODEL // N_DEV, D_MODEL),
        "w_down": G["w_down"].reshape(N_DEV, FFN_HIDDEN // N_DEV, D_MODEL),
    }
    for n in ("w_proj_attn", "w_proj_pool", "w_proj_conv"):
        slabs[n] = jnp.transpose(G[n].reshape(BRANCH_W, N_DEV, D_MODEL // N_DEV), (1, 0, 2))
    return slabs


def _sum_layer_grads(recv, l):
    out = {n: _sum_slabs(r, f"sum_{n}_l{l}") for n, r in recv.items()}
    out["w_in"] = out["w_in"][:, :IN_SHARD]
    out["w_gate_up"] = out["w_gate_up"][:, :GU_SHARD]
    return out


def _exchange_layer_grads(G, l):
    names = list(SHARDED)
    slabs = _grad_slabs(G, l)
    recv = _multi_exchange([slabs[n] for n in names], f"exchange_grads_l{l}")
    return _sum_layer_grads(dict(zip(names, recv)), l)


def _sum_small(xs, name):
    def body(*refs):
        for x_ref, o_ref in zip(refs[:len(xs)], refs[len(xs):]):
            acc = x_ref[0]
            for j in range(1, N_DEV):
                acc = acc + x_ref[j]
            o_ref[...] = acc

    return pl.pallas_call(
        body, name=name, out_shape=[jax.ShapeDtypeStruct(x.shape[1:], F32) for x in xs],
        compiler_params=_cp(),
    )(*xs)


def _as_2d(a):
    if a.ndim == 1:
        return a.reshape(1, -1)
    return a.reshape(-1, a.shape[-1])


def kernel(x, attn_norm, w_in, b_forget, b_gate, w_proj_attn, pool_w, pool_scale, w_proj_pool, conv_w, w_proj_conv, w_out, ffn_norm, w_gate_up, w_down, final_norm, loss_target, m_attn_norm, m_w_in, m_b_forget, m_b_gate, m_w_proj_attn, m_pool_w, m_pool_scale, m_w_proj_pool, m_conv_w, m_w_proj_conv, m_w_out, m_ffn_norm, m_w_gate_up, m_w_down, m_final_norm, v_attn_norm, v_w_in, v_b_forget, v_b_gate, v_w_proj_attn, v_pool_w, v_pool_scale, v_w_proj_pool, v_conv_w, v_w_proj_conv, v_w_out, v_ffn_norm, v_w_gate_up, v_w_down, v_final_norm):
    weights = dict(attn_norm=attn_norm, w_in=w_in, b_forget=b_forget, b_gate=b_gate, w_proj_attn=w_proj_attn,
                   pool_w=pool_w, pool_scale=pool_scale, w_proj_pool=w_proj_pool, conv_w=conv_w,
                   w_proj_conv=w_proj_conv, w_out=w_out, ffn_norm=ffn_norm, w_gate_up=w_gate_up, w_down=w_down,
                   final_norm=final_norm)
    moments_m = dict(attn_norm=m_attn_norm, w_in=m_w_in, b_forget=m_b_forget, b_gate=m_b_gate,
                     w_proj_attn=m_w_proj_attn, pool_w=m_pool_w, pool_scale=m_pool_scale, w_proj_pool=m_w_proj_pool,
                     conv_w=m_conv_w, w_proj_conv=m_w_proj_conv, w_out=m_w_out, ffn_norm=m_ffn_norm,
                     w_gate_up=m_w_gate_up, w_down=m_w_down, final_norm=m_final_norm)
    moments_v = dict(attn_norm=v_attn_norm, w_in=v_w_in, b_forget=v_b_forget, b_gate=v_b_gate,
                     w_proj_attn=v_w_proj_attn, pool_w=v_pool_w, pool_scale=v_pool_scale, w_proj_pool=v_w_proj_pool,
                     conv_w=v_conv_w, w_proj_conv=v_w_proj_conv, w_out=v_w_out, ffn_norm=v_ffn_norm,
                     w_gate_up=v_w_gate_up, w_down=v_w_down, final_norm=v_final_norm)

    sh = _padded_shards(weights)
    names = list(SHARDED)
    last = DEPTH - 1
    W0, conv_all = _gather_layer(sh, conv_w, 0)
    gather_layers = [last] * len(names)
    gather_started = _split_start([sh[n] for n in names], gather_layers, False, W0["w_out"], "gather_start_l1")
    W0["w_out"] = W0["w_out"] + gather_started[4][0, 0].astype(BF16)

    def get_W(l, xt):
        if l == 0:
            W = W0
        else:
            lands = _split_wait(gather_started, gather_layers, False, xt, "gather_wait_l1")
            W = _full_operands({n: _with_own(land, sh[n][l]) for n, land in zip(names, lands)}, l)
        W.update(_replicated_operands(weights, l))
        W["conv_w"] = jnp.transpose(conv_all[:, l], (1, 0, 2)).reshape(CONV_K, BRANCH_W)
        return W

    pending = {}

    def reduce_layer(l, G, Ws):
        small = {n: G[n] for n in ("attn_norm", "b_forget", "b_gate", "pool_w", "pool_scale", "ffn_norm", "conv_w")}
        if l == last and DEPTH > 1:
            slabs = _grad_slabs(G, l)
            pending["slabs"] = slabs
            pending["started"] = _split_start([slabs[n] for n in names], None, True, slabs["w_in"],
                                              "exchange_start_l1")
            Ws[l - 1]["ffn_norm"] = Ws[l - 1]["ffn_norm"] + pending["started"][4][0, 0]
            return small
        return {**_exchange_layer_grads(G, l), **small}

    loss_part, grad_x, grads, g_final = _local_step(x, loss_target, get_W, final_norm, reduce_layer)
    lands = _split_wait(pending["started"], None, True, grad_x, "exchange_wait_l1")
    me = 4 * lax.axis_index("x") + 2 * lax.axis_index("y") + lax.axis_index("c")
    recv = {n: _with_own(land, lax.dynamic_index_in_dim(pending["slabs"][n], me, 0, keepdims=False))
            for n, land in zip(names, lands)}
    grads[last].update(_sum_layer_grads(recv, last))
    gw = {n: jnp.stack([grads[l][n] for l in range(DEPTH)]) for n in SHARDED}

    small = ("attn_norm", "b_forget", "b_gate", "pool_w", "pool_scale", "ffn_norm", "conv_w")
    parts = [jnp.stack([grads[l][n] for l in range(DEPTH)]) for n in small] + [g_final, loss_part]
    gathered = _multi_gather(parts, [None] * len(parts), "gather_small_grads")
    summed = _sum_small(gathered, "sum_small_grads")
    for n, s in zip(small, summed):
        gw[n] = s
    gw["attn_norm"], gw["ffn_norm"] = gw["attn_norm"][:, 0], gw["ffn_norm"][:, 0]
    gw["b_forget"] = gw["b_forget"][:, 0, :N_HEADS]
    gw["b_gate"], gw["pool_scale"] = gw["b_gate"][:, 0], gw["pool_scale"][:, 0]
    gw["conv_w"] = lax.dynamic_slice_in_dim(gw["conv_w"], me * (BRANCH_W // N_DEV), BRANCH_W // N_DEV, axis=2)
    gw["final_norm"] = summed[-2][0]
    loss = summed[-1][0, 0]

    deltas, new_m, new_v = {}, {}, {}
    for n in WEIGHT_ORDER:
        shape = weights[n].shape
        d, nm, nv = _adamw(_as_2d(weights[n]), _as_2d(gw[n]), _as_2d(moments_m[n]), _as_2d(moments_v[n]),
                           "adamw_" + n)
        deltas[n], new_m[n], new_v[n] = d.reshape(shape), nm.reshape(shape), nv.reshape(shape)

    return (loss, grad_x, *[gw[n] for n in WEIGHT_ORDER], *[deltas[n] for n in WEIGHT_ORDER],
            *[new_m[n] for n in WEIGHT_ORDER], *[new_v[n] for n in WEIGHT_ORDER])
```

```python
import functools

import numpy as np
import jax
import jax.numpy as jnp
from jax import lax
from jax.experimental import pallas as pl
from jax.experimental.pallas import tpu as pltpu

F32 = jnp.float32
BF16 = jnp.bfloat16

N_DEV = 8
D_MODEL = 1024
DEPTH = 2
N_HEADS = 8
HEAD_DIM = 64
BRANCH_W = 512
POOL_WINDOWS = (2, 4, 8, 16)
POOL_GD = 128
CONV_K = 3
FFN_HIDDEN = 2816
GATE_W = 3 * D_MODEL
IN_COLS = 6664
MAIN_COLS = GATE_W + 7 * BRANCH_W
RMS_EPS = 1e-6
NEG_INF = -1e30

ADAM_LR = 0.001
ADAM_B1 = 0.9
ADAM_B2 = 0.999
ADAM_EPS = 1e-08
ADAM_WD = 0.01
ADAM_STEP = 10

LANES = 128
VMEM_LIMIT = 56 * 1024 * 1024
ATT_BLK = 256
CUM_BLK = 256

OFF_G, OFF_Q, OFF_K, OFF_V, OFF_U, OFF_CV, OFF_CB, OFF_CC = (
    0, 3072, 3584, 4096, 4608, 5120, 5632, 6144)


def _cp(sem=None):
    return pltpu.CompilerParams(dimension_semantics=sem, vmem_limit_bytes=VMEM_LIMIT)


def _sigmoid(z):
    return 1.0 / (1.0 + jnp.exp(-z))


def _matmul(a, b, *, mode, out_dtype, name, tm=2048, tn=512, tk=None, residual=None):
    if mode == "nn":
        (M, K), N = a.shape, b.shape[1]
    elif mode == "nt":
        (M, K), N = a.shape, b.shape[0]
    else:
        (K, M), N = a.shape, b.shape[1]
    tm, tn, tk = min(tm, M), min(tn, N), K if tk is None else min(tk, K)
    assert M % tm == 0 and N % tn == 0 and K % tk == 0, (name, M, N, K, tm, tn, tk)
    nk = K // tk
    if mode == "nn":
        a_spec = pl.BlockSpec((tm, tk), lambda i, j, k: (i, k))
        b_spec = pl.BlockSpec((tk, tn), lambda i, j, k: (k, j))
        dims = (((1,), (0,)), ((), ()))
    elif mode == "nt":
        a_spec = pl.BlockSpec((tm, tk), lambda i, j, k: (i, k))
        b_spec = pl.BlockSpec((tn, tk), lambda i, j, k: (j, k))
        dims = (((1,), (1,)), ((), ()))
    else:
        a_spec = pl.BlockSpec((tk, tm), lambda i, j, k: (k, i))
        b_spec = pl.BlockSpec((tk, tn), lambda i, j, k: (k, j))
        dims = (((0,), (0,)), ((), ()))
    o_spec = pl.BlockSpec((tm, tn), lambda i, j, k: (i, j))
    has_res = residual is not None

    def body(*refs):
        a_ref, b_ref = refs[:2]
        r_ref = refs[2] if has_res else None
        o_ref = refs[2 + has_res]

        def finish(acc):
            if has_res:
                acc = acc + r_ref[...].astype(F32)
            o_ref[...] = acc.astype(out_dtype)

        prod = lax.dot_general(a_ref[...], b_ref[...], dims, preferred_element_type=F32)
        if nk == 1:
            finish(prod)
            return
        acc_ref = refs[-1]
        k = pl.program_id(2)

        @pl.when(k == 0)
        def _():
            acc_ref[...] = prod

        @pl.when(jnp.logical_and(k > 0, k < nk - 1))
        def _():
            acc_ref[...] += prod

        @pl.when(k == nk - 1)
        def _():
            finish(acc_ref[...] + prod)

    in_specs = [a_spec, b_spec] + ([o_spec] if has_res else [])
    args = (a, b) + ((residual,) if has_res else ())
    return pl.pallas_call(
        body, name=name, grid=(M // tm, N // tn, nk), in_specs=in_specs, out_specs=o_spec,
        out_shape=jax.ShapeDtypeStruct((M, N), out_dtype),
        scratch_shapes=[pltpu.VMEM((tm, tn), F32)] if nk > 1 else [],
        compiler_params=_cp(("parallel", "parallel", "arbitrary")),
    )(*args)


def _rms_fwd(x, g, name):
    T, Dm = x.shape
    tm = min(512, T)

    def body(x_ref, g_ref, h_ref):
        xf = x_ref[...]
        r = lax.rsqrt(jnp.mean(xf * xf, axis=-1, keepdims=True) + RMS_EPS)
        h_ref[...] = ((xf * r) * g_ref[...]).astype(BF16)

    return pl.pallas_call(
        body, name=name, grid=(T // tm,),
        in_specs=[pl.BlockSpec((tm, Dm), lambda i: (i, 0)), pl.BlockSpec((1, Dm), lambda i: (0, 0))],
        out_specs=pl.BlockSpec((tm, Dm), lambda i: (i, 0)),
        out_shape=jax.ShapeDtypeStruct((T, Dm), BF16),
        compiler_params=_cp(("parallel",)),
    )(x, g.reshape(1, Dm))


def _rms_bwd(x, g, dh, dres, name):
    T, Dm = x.shape
    tm = min(512, T)

    def body(x_ref, g_ref, dh_ref, dres_ref, dx_ref, dxb_ref, dg_ref):
        i = pl.program_id(0)
        xf = x_ref[...]
        r = lax.rsqrt(jnp.mean(xf * xf, axis=-1, keepdims=True) + RMS_EPS)
        xn = xf * r
        dhf = dh_ref[...].astype(F32)
        dxn = dhf * g_ref[...]
        c = jnp.mean(dxn * xn, axis=-1, keepdims=True)
        dx = dres_ref[...] + r * (dxn - xn * c)
        dx_ref[...] = dx
        dxb_ref[...] = dx.astype(BF16)
        part = jnp.sum(dhf * xn, axis=0, keepdims=True)

        @pl.when(i == 0)
        def _():
            dg_ref[...] = part

        @pl.when(i > 0)
        def _():
            dg_ref[...] += part

    row = pl.BlockSpec((tm, Dm), lambda i: (i, 0))
    vec = pl.BlockSpec((1, Dm), lambda i: (0, 0))
    return pl.pallas_call(
        body, name=name, grid=(T // tm,), in_specs=[row, vec, row, row], out_specs=[row, row, vec],
        out_shape=[jax.ShapeDtypeStruct((T, Dm), F32), jax.ShapeDtypeStruct((T, Dm), BF16),
                   jax.ShapeDtypeStruct((1, Dm), F32)],
        compiler_params=_cp(("arbitrary",)),
    )(x, g.reshape(1, Dm), dh, dres)


def _loss_head(x, g, target, name):
    T, Dm = x.shape
    tm = min(512, T)

    def body(x_ref, g_ref, t_ref, loss_ref, dx_ref, dxb_ref, dg_ref):
        i = pl.program_id(0)
        xf = x_ref[...]
        gv = g_ref[...]
        r = lax.rsqrt(jnp.mean(xf * xf, axis=-1, keepdims=True) + RMS_EPS)
        xn = xf * r
        diff = xn * gv - t_ref[...]
        per_tok = jnp.mean(diff * diff, axis=-1, keepdims=True)
        lpart = 0.5 * jnp.sum(per_tok, axis=0, keepdims=True) + jnp.zeros((1, LANES), F32)
        dy = diff * (1.0 / Dm)
        dxn = dy * gv
        c = jnp.mean(dxn * xn, axis=-1, keepdims=True)
        dx = r * (dxn - xn * c)
        dx_ref[...] = dx
        dxb_ref[...] = dx.astype(BF16)
        part = jnp.sum(dy * xn, axis=0, keepdims=True)

        @pl.when(i == 0)
        def _():
            dg_ref[...] = part
            loss_ref[...] = lpart

        @pl.when(i > 0)
        def _():
            dg_ref[...] += part
            loss_ref[...] += lpart

    row = pl.BlockSpec((tm, Dm), lambda i: (i, 0))
    vec = pl.BlockSpec((1, Dm), lambda i: (0, 0))
    lsp = pl.BlockSpec((1, LANES), lambda i: (0, 0))
    return pl.pallas_call(
        body, name=name, grid=(T // tm,), in_specs=[row, vec, row], out_specs=[lsp, row, row, vec],
        out_shape=[jax.ShapeDtypeStruct((1, LANES), F32), jax.ShapeDtypeStruct((T, Dm), F32),
                   jax.ShapeDtypeStruct((T, Dm), BF16), jax.ShapeDtypeStruct((1, Dm), F32)],
        compiler_params=_cp(("arbitrary",)),
    )(x, g.reshape(1, Dm), target)


def _split_bf16(v):
    hi = v.astype(BF16)
    r1 = v - hi.astype(F32)
    mid = r1.astype(BF16)
    lo = (r1 - mid.astype(F32)).astype(BF16)
    return hi, mid, lo


def _tri_dot(tri, v):
    hi, mid, lo = _split_bf16(v)
    dot = functools.partial(jnp.dot, preferred_element_type=F32)
    return dot(tri, hi) + dot(tri, mid) + dot(tri, lo)


def _log_sigmoid(z):
    return jnp.minimum(z, 0.0) - jnp.log(1.0 + jnp.exp(-jnp.abs(z)))


def _fox_cumsum_fwd(f, bf, n_seq, name):
    T = f.shape[0]
    S = T // n_seq
    c = min(CUM_BLK, S)

    def body(f_ref, b_ref, out_ref):
        ri = lax.broadcasted_iota(jnp.int32, (c, c), 0)
        ci = lax.broadcasted_iota(jnp.int32, (c, c), 1)
        tri = (ri >= ci).astype(BF16)
        carry = jnp.zeros((1, LANES), F32)
        for j in range(S // c):
            lf = _log_sigmoid(f_ref[j * c:(j + 1) * c, :] + b_ref[...])
            out_ref[j * c:(j + 1) * c, :] = _tri_dot(tri, lf) + carry
            carry = carry + jnp.sum(lf, axis=0, keepdims=True)

    blk = pl.BlockSpec((S, LANES), lambda b: (b, 0))
    return pl.pallas_call(
        body, name=name, grid=(n_seq,), in_specs=[blk, pl.BlockSpec((1, LANES), lambda b: (0, 0))],
        out_specs=blk, out_shape=jax.ShapeDtypeStruct((T, LANES), F32),
        compiler_params=_cp(("parallel",)),
    )(f, bf)


def _fox_cumsum_bwd(f, bf, dF, n_seq, name):
    T = f.shape[0]
    S = T // n_seq
    c = min(CUM_BLK, S)

    def body(f_ref, b_ref, dF_ref, df_ref, db_ref):
        b = pl.program_id(0)
        ri = lax.broadcasted_iota(jnp.int32, (c, c), 0)
        ci = lax.broadcasted_iota(jnp.int32, (c, c), 1)
        tri = (ri <= ci).astype(BF16)
        carry = jnp.zeros((1, LANES), F32)
        dbp = jnp.zeros((1, LANES), F32)
        for j in reversed(range(S // c)):
            dFc = dF_ref[j * c:(j + 1) * c, :]
            dlf = _tri_dot(tri, dFc) + carry
            carry = carry + jnp.sum(dFc, axis=0, keepdims=True)
            z = f_ref[j * c:(j + 1) * c, :] + b_ref[...]
            dz = dlf * _sigmoid(-z)
            df_ref[j * c:(j + 1) * c, :] = dz.astype(BF16)
            dbp = dbp + jnp.sum(dz, axis=0, keepdims=True)

        @pl.when(b == 0)
        def _():
            db_ref[...] = dbp

        @pl.when(b > 0)
        def _():
            db_ref[...] += dbp

    blk = pl.BlockSpec((S, LANES), lambda b: (b, 0))
    vec = pl.BlockSpec((1, LANES), lambda b: (0, 0))
    return pl.pallas_call(
        body, name=name, grid=(n_seq,), in_specs=[blk, vec, blk], out_specs=[blk, vec],
        out_shape=[jax.ShapeDtypeStruct((T, LANES), BF16), jax.ShapeDtypeStruct((1, LANES), F32)],
        compiler_params=_cp(("arbitrary",)),
    )(f, bf, dF)


def _pair_masks():
    lane = lax.broadcasted_iota(jnp.int32, (1, LANES), 1)
    lo = lane < HEAD_DIM
    return lo, jnp.logical_not(lo)


def _attn_logits(q, k, fq, fk, sel, mask, scale):
    qm = jnp.where(sel, q, jnp.zeros_like(q))
    s = lax.dot_general(qm, k, (((1,), (1,)), ((), ())), preferred_element_type=F32) * scale
    s = s + fq - fk
    return jnp.where(mask, s, NEG_INF)


def _causal_mask(qi, ki, blk):
    row = qi * blk + lax.broadcasted_iota(jnp.int32, (blk, blk), 0)
    col = ki * blk + lax.broadcasted_iota(jnp.int32, (blk, blk), 1)
    return col <= row


def _attn_fwd(proj, Fq, Fk, n_seq, name):
    T = proj.shape[0]
    S = T // n_seq
    blk = min(ATT_BLK, S)
    nb = S // blk
    scale = HEAD_DIM ** -0.5
    qc, kc, vc = OFF_Q // LANES, OFF_K // LANES, OFF_V // LANES

    def body(q_ref, k_ref, v_ref, fq_ref, fk_ref, o_ref, o32_ref, lse_ref, m_s, l_s, acc_s):
        qi, ki = pl.program_id(2), pl.program_id(3)

        @pl.when(ki == 0)
        def _():
            m_s[...] = jnp.full_like(m_s, NEG_INF)
            l_s[...] = jnp.zeros_like(l_s)
            acc_s[...] = jnp.zeros_like(acc_s)

        @pl.when(ki <= qi)
        def _():
            q, k, v = q_ref[...], k_ref[...], v_ref[...]
            mask = _causal_mask(qi, ki, blk)
            for hh, sel in enumerate(_pair_masks()):
                s = _attn_logits(q, k, fq_ref[hh], fk_ref[hh], sel, mask, scale)
                m_prev = m_s[hh]
                m_new = jnp.maximum(m_prev, jnp.max(s, axis=-1, keepdims=True))
                alpha = jnp.exp(m_prev - m_new)
                p = jnp.exp(s - m_new)
                l_s[hh] = alpha * l_s[hh] + jnp.sum(p, axis=-1, keepdims=True)
                p_hi = p.astype(BF16)
                p_lo = (p - p_hi.astype(F32)).astype(BF16)
                pv = jnp.dot(p_hi, v, preferred_element_type=F32) + jnp.dot(p_lo, v, preferred_element_type=F32)
                acc_s[hh] = alpha * acc_s[hh] + pv
                m_s[hh] = m_new

        @pl.when(ki == qi)
        def _():
            lo, _ = _pair_masks()
            o = jnp.where(lo, acc_s[0] / l_s[0], acc_s[1] / l_s[1])
            o_ref[...] = o.astype(BF16)
            o32_ref[...] = o
            lse_ref[0] = m_s[0] + jnp.log(l_s[0])
            lse_ref[1] = m_s[1] + jnp.log(l_s[1])

    grid = (n_seq, N_HEADS // 2, nb, nb)
    return pl.pallas_call(
        body, name=name, grid=grid,
        in_specs=[
            pl.BlockSpec((blk, LANES), lambda b, j, qi, ki: (b * nb + qi, qc + j)),
            pl.BlockSpec((blk, LANES), lambda b, j, qi, ki: (b * nb + jnp.minimum(ki, qi), kc + j)),
            pl.BlockSpec((blk, LANES), lambda b, j, qi, ki: (b * nb + jnp.minimum(ki, qi), vc + j)),
            pl.BlockSpec((2, blk, 1), lambda b, j, qi, ki: (j, b * nb + qi, 0)),
            pl.BlockSpec((2, 1, blk), lambda b, j, qi, ki: (j, 0, b * nb + jnp.minimum(ki, qi))),
        ],
        out_specs=[
            pl.BlockSpec((blk, LANES), lambda b, j, qi, ki: (b * nb + qi, j)),
            pl.BlockSpec((blk, LANES), lambda b, j, qi, ki: (b * nb + qi, j)),
            pl.BlockSpec((2, blk, 1), lambda b, j, qi, ki: (j, b * nb + qi, 0)),
        ],
        out_shape=[jax.ShapeDtypeStruct((T, BRANCH_W), BF16), jax.ShapeDtypeStruct((T, BRANCH_W), F32),
                   jax.ShapeDtypeStruct((N_HEADS, T, 1), F32)],
        scratch_shapes=[pltpu.VMEM((2, blk, 1), F32), pltpu.VMEM((2, blk, 1), F32),
                        pltpu.VMEM((2, blk, LANES), F32)],
        compiler_params=_cp(("parallel", "parallel", "parallel", "arbitrary")),
    )(proj, proj, proj, Fq, Fk)


def _attn_delta(do, o, name):
    T = do.shape[0]
    tm = min(512, T)

    def body(do_ref, o_ref, d_ref):
        prod = do_ref[...].astype(F32) * o_ref[...].astype(F32)
        lo, hi = _pair_masks()
        for j in range(N_HEADS // 2):
            pj = prod[:, j * LANES:(j + 1) * LANES]
            d_ref[2 * j] = jnp.sum(jnp.where(lo, pj, 0.0), axis=-1, keepdims=True)
            d_ref[2 * j + 1] = jnp.sum(jnp.where(hi, pj, 0.0), axis=-1, keepdims=True)

    row = pl.BlockSpec((tm, BRANCH_W), lambda i: (i, 0))
    return pl.pallas_call(
        body, name=name, grid=(T // tm,), in_specs=[row, row],
        out_specs=pl.BlockSpec((N_HEADS, tm, 1), lambda i: (0, i, 0)),
        out_shape=jax.ShapeDtypeStruct((N_HEADS, T, 1), F32),
        compiler_params=_cp(("parallel",)),
    )(do, o)


def _attn_bwd_dq(proj, do, lse, delta, Fq, Fk, n_seq, name):
    T = proj.shape[0]
    S = T // n_seq
    blk = min(ATT_BLK, S)
    nb = S // blk
    scale = HEAD_DIM ** -0.5
    qc, kc, vc = OFF_Q // LANES, OFF_K // LANES, OFF_V // LANES

    def body(q_ref, k_ref, v_ref, do_ref, lse_ref, dl_ref, fq_ref, fk_ref, dq_ref, acc_s):
        qi, ki = pl.program_id(2), pl.program_id(3)

        @pl.when(ki == 0)
        def _():
            acc_s[...] = jnp.zeros_like(acc_s)

        @pl.when(ki <= qi)
        def _():
            q, k, v, do_ = q_ref[...], k_ref[...], v_ref[...], do_ref[...]
            mask = _causal_mask(qi, ki, blk)
            for hh, sel in enumerate(_pair_masks()):
                s = _attn_logits(q, k, fq_ref[hh], fk_ref[hh], sel, mask, scale)
                p = jnp.exp(s - lse_ref[hh])
                dom = jnp.where(sel, do_, jnp.zeros_like(do_))
                dp = lax.dot_general(dom, v, (((1,), (1,)), ((), ())), preferred_element_type=F32)
                ds = p * (dp - dl_ref[hh])
                acc_s[hh] += jnp.dot(ds.astype(BF16), k, preferred_element_type=F32)

        @pl.when(ki == qi)
        def _():
            lo, _ = _pair_masks()
            dq_ref[...] = (jnp.where(lo, acc_s[0], acc_s[1]) * scale).astype(BF16)

    qmap = lambda b, j, qi, ki: (b * nb + qi, j)
    col1 = pl.BlockSpec((2, blk, 1), lambda b, j, qi, ki: (j, b * nb + qi, 0))
    return pl.pallas_call(
        body, name=name, grid=(n_seq, N_HEADS // 2, nb, nb),
        in_specs=[
            pl.BlockSpec((blk, LANES), lambda b, j, qi, ki: (b * nb + qi, qc + j)),
            pl.BlockSpec((blk, LANES), lambda b, j, qi, ki: (b * nb + jnp.minimum(ki, qi), kc + j)),
            pl.BlockSpec((blk, LANES), lambda b, j, qi, ki: (b * nb + jnp.minimum(ki, qi), vc + j)),
            pl.BlockSpec((blk, LANES), qmap),
            col1, col1, col1,
            pl.BlockSpec((2, 1, blk), lambda b, j, qi, ki: (j, 0, b * nb + jnp.minimum(ki, qi))),
        ],
        out_specs=pl.BlockSpec((blk, LANES), qmap),
        out_shape=jax.ShapeDtypeStruct((T, BRANCH_W), BF16),
        scratch_shapes=[pltpu.VMEM((2, blk, LANES), F32)],
        compiler_params=_cp(("parallel", "parallel", "parallel", "arbitrary")),
    )(proj, proj, proj, do, lse, delta, Fq, Fk)


def _attn_bwd_dkv(proj, do, lse, delta, Fq, Fk, n_seq, name):
    T = proj.shape[0]
    S = T // n_seq
    blk = min(ATT_BLK, S)
    nb = S // blk
    scale = HEAD_DIM ** -0.5
    qc, kc, vc = OFF_Q // LANES, OFF_K // LANES, OFF_V // LANES
    tdot = functools.partial(lax.dot_general, dimension_numbers=(((0,), (0,)), ((), ())),
                             preferred_element_type=F32)

    def body(q_ref, k_ref, v_ref, do_ref, lse_ref, dl_ref, fq_ref, fk_ref, dk_ref, dv_ref, dfk_ref,
             dk_s, dv_s, df_s):
        ki, qi = pl.program_id(2), pl.program_id(3)

        @pl.when(qi == 0)
        def _():
            dk_s[...] = jnp.zeros_like(dk_s)
            dv_s[...] = jnp.zeros_like(dv_s)
            df_s[...] = jnp.zeros_like(df_s)

        @pl.when(qi >= ki)
        def _():
            q, k, v, do_ = q_ref[...], k_ref[...], v_ref[...], do_ref[...]
            mask = _causal_mask(qi, ki, blk)
            for hh, sel in enumerate(_pair_masks()):
                s = _attn_logits(q, k, fq_ref[hh], fk_ref[hh], sel, mask, scale)
                p = jnp.exp(s - lse_ref[hh])
                dv_s[hh] += tdot(p.astype(BF16), do_)
                dom = jnp.where(sel, do_, jnp.zeros_like(do_))
                dp = lax.dot_general(dom, v, (((1,), (1,)), ((), ())), preferred_element_type=F32)
                ds = p * (dp - dl_ref[hh])
                dk_s[hh] += tdot(ds.astype(BF16), q)
                df_s[hh] -= jnp.sum(ds, axis=0, keepdims=True)

        @pl.when(qi == nb - 1)
        def _():
            lo, _ = _pair_masks()
            dk_ref[...] = (jnp.where(lo, dk_s[0], dk_s[1]) * scale).astype(BF16)
            dv_ref[...] = jnp.where(lo, dv_s[0], dv_s[1]).astype(BF16)
            dfk_ref[...] = df_s[...]

    kmap = lambda b, j, ki, qi: (b * nb + ki, j)
    col1 = pl.BlockSpec((2, blk, 1), lambda b, j, ki, qi: (j, b * nb + jnp.maximum(qi, ki), 0))
    rowk = pl.BlockSpec((2, 1, blk), lambda b, j, ki, qi: (j, 0, b * nb + ki))
    return pl.pallas_call(
        body, name=name, grid=(n_seq, N_HEADS // 2, nb, nb),
        in_specs=[
            pl.BlockSpec((blk, LANES), lambda b, j, ki, qi: (b * nb + jnp.maximum(qi, ki), qc + j)),
            pl.BlockSpec((blk, LANES), lambda b, j, ki, qi: (b * nb + ki, kc + j)),
            pl.BlockSpec((blk, LANES), lambda b, j, ki, qi: (b * nb + ki, vc + j)),
            pl.BlockSpec((blk, LANES), lambda b, j, ki, qi: (b * nb + jnp.maximum(qi, ki), j)),
            col1, col1, col1, rowk,
        ],
        out_specs=[pl.BlockSpec((blk, LANES), kmap), pl.BlockSpec((blk, LANES), kmap), rowk],
        out_shape=[jax.ShapeDtypeStruct((T, BRANCH_W), BF16), jax.ShapeDtypeStruct((T, BRANCH_W), BF16),
                   jax.ShapeDtypeStruct((N_HEADS, 1, T), F32)],
        scratch_shapes=[pltpu.VMEM((2, blk, LANES), F32), pltpu.VMEM((2, blk, LANES), F32),
                        pltpu.VMEM((2, 1, blk), F32)],
        compiler_params=_cp(("parallel", "parallel", "parallel", "arbitrary")),
    )(proj, proj, proj, do, lse, delta, Fq, Fk)


AUG0 = HEAD_DIM
Q_TILE, K_CHUNK, ROW_GROUP = 512, 256, 64


def _fox_prep(f, bf, proj, n_seq, name):
    T = f.shape[0]
    S = T // n_seq
    c = min(CUM_BLK, S)

    def body(f_ref, b_ref, q_ref, k_ref, qa_ref, ka_ref):
        ri = lax.broadcasted_iota(jnp.int32, (c, c), 0)
        ci = lax.broadcasted_iota(jnp.int32, (c, c), 1)
        tri = (ri >= ci).astype(BF16)
        lane = lax.broadcasted_iota(jnp.int32, (c, LANES), 1)
        carry = jnp.zeros((1, LANES), F32)
        for j in range(S // c):
            rows = slice(j * c, (j + 1) * c)
            lf = _log_sigmoid(f_ref[rows, :] + b_ref[...])
            Fc = _tri_dot(tri, lf) + carry
            carry = carry + jnp.sum(lf, axis=0, keepdims=True)
            for h in range(N_HEADS):
                col = jnp.sum(jnp.where(lane == h, Fc, 0.0), axis=-1, keepdims=True)
                hi = col.astype(BF16).astype(F32)
                r1 = col - hi
                mid = r1.astype(BF16).astype(F32)
                lo = r1 - mid
                ones_q = jnp.logical_and(lane >= AUG0 + 3, lane < AUG0 + 6)
                ones_k = jnp.logical_and(lane >= AUG0, lane < AUG0 + 3)
                aug_q = jnp.where(lane == AUG0, hi, jnp.where(lane == AUG0 + 1, mid, jnp.where(
                    lane == AUG0 + 2, lo, jnp.where(ones_q, 1.0, 0.0))))
                aug_k = jnp.where(lane == AUG0 + 3, -hi, jnp.where(lane == AUG0 + 4, -mid, jnp.where(
                    lane == AUG0 + 5, -lo, jnp.where(ones_k, 1.0, 0.0))))
                pair = slice((h // 2) * LANES, (h // 2 + 1) * LANES)
                qp, kp = q_ref[rows, pair].astype(F32), k_ref[rows, pair].astype(F32)
                if h % 2:
                    qp, kp = pltpu.roll(qp, HEAD_DIM, 1), pltpu.roll(kp, HEAD_DIM, 1)
                out = slice(h * LANES, (h + 1) * LANES)
                qa_ref[rows, out] = jnp.where(lane < HEAD_DIM, qp * (HEAD_DIM ** -0.5), aug_q).astype(BF16)
                ka_ref[rows, out] = jnp.where(lane < HEAD_DIM, kp, aug_k).astype(BF16)

    fblk = pl.BlockSpec((S, LANES), lambda b: (b, 0))
    out = pl.BlockSpec((S, N_HEADS * LANES), lambda b: (b, 0))
    sh = jax.ShapeDtypeStruct((T, N_HEADS * LANES), BF16)
    return pl.pallas_call(
        body, name=name, grid=(n_seq,),
        in_specs=[fblk, pl.BlockSpec((1, LANES), lambda b: (0, 0)),
                  pl.BlockSpec((S, BRANCH_W), lambda b: (b, OFF_Q // BRANCH_W)),
                  pl.BlockSpec((S, BRANCH_W), lambda b: (b, OFF_K // BRANCH_W))],
        out_specs=[out, out], out_shape=[sh, sh],
        compiler_params=_cp(("parallel",)),
    )(f, bf, proj, proj)


def _band_mask(q0, k0, nq, nk):
    row = q0 + lax.broadcasted_iota(jnp.int32, (nq, nk), 0)
    col = k0 + lax.broadcasted_iota(jnp.int32, (nq, nk), 1)
    return col <= row


_NT = (((1,), (1,)), ((), ()))
_TN = (((0,), (0,)), ((), ()))


def _attn_fwd2(qa, ka, proj, n_seq, name):
    T = qa.shape[0]
    S = T // n_seq
    tq, tk, rg = min(Q_TILE, S), min(K_CHUNK, S), ROW_GROUP
    nq, per = S // tq, tq // tk
    vc = OFF_V // LANES

    def body(q_ref, k_ref, v_ref, o_ref, o32_ref, lse_ref, phi_s, plo_s, a_s, m_s, l_s, acc_s):
        qi = pl.program_id(2)
        m_s[...] = jnp.full_like(m_s, NEG_INF)
        l_s[...] = jnp.zeros_like(l_s)
        acc_s[...] = jnp.zeros_like(acc_s)

        def chunk(kc, masked):
            k0 = pl.multiple_of(kc * tk, tk)
            v = v_ref[pl.ds(k0, tk), :]
            for hh in range(2):
                hl = slice(hh * LANES, (hh + 1) * LANES)
                s_all = lax.dot_general(q_ref[:, hl], k_ref[pl.ds(k0, tk), hl], _NT, preferred_element_type=F32)
                for r in range(tq // rg):
                    rows = slice(r * rg, (r + 1) * rg)
                    s = s_all[rows, :]
                    if masked:
                        s = jnp.where(_band_mask(qi * tq + r * rg, k0, rg, tk), s, NEG_INF)
                    m_prev = m_s[hh, rows]
                    m_new = jnp.maximum(m_prev, jnp.max(s, axis=-1, keepdims=True))
                    alpha = jnp.exp(m_prev - m_new)
                    p = jnp.exp(s - m_new)
                    l_s[hh, rows] = alpha * l_s[hh, rows] + jnp.sum(p, axis=-1, keepdims=True)
                    m_s[hh, rows] = m_new
                    a_s[hh, rows] = alpha
                    p_hi = p.astype(BF16)
                    phi_s[hh, rows, :] = p_hi
                    plo_s[hh, rows, :] = (p - p_hi.astype(F32)).astype(BF16)
                pv = (jnp.dot(phi_s[hh], v, preferred_element_type=F32)
                      + jnp.dot(plo_s[hh], v, preferred_element_type=F32))
                acc_s[hh] = a_s[hh] * acc_s[hh] + pv

        def unmasked(kc, carry):
            chunk(kc, False)
            return carry

        lax.fori_loop(0, qi * per, unmasked, 0)
        for d in range(per):
            chunk(qi * per + d, True)

        lo, _ = _pair_masks()
        o = jnp.where(lo, acc_s[0] / l_s[0], acc_s[1] / l_s[1])
        o_ref[...] = o.astype(BF16)
        o32_ref[...] = o
        lse_ref[0] = m_s[0] + jnp.log(l_s[0])
        lse_ref[1] = m_s[1] + jnp.log(l_s[1])

    qmap = lambda b, j, qi: (b * nq + qi, j)
    return pl.pallas_call(
        body, name=name, grid=(n_seq, N_HEADS // 2, nq),
        in_specs=[pl.BlockSpec((tq, 2 * LANES), qmap),
                  pl.BlockSpec((S, 2 * LANES), lambda b, j, qi: (b, j)),
                  pl.BlockSpec((S, LANES), lambda b, j, qi: (b, vc + j))],
        out_specs=[pl.BlockSpec((tq, LANES), qmap), pl.BlockSpec((tq, LANES), qmap),
                   pl.BlockSpec((2, tq, 1), lambda b, j, qi: (j, b * nq + qi, 0))],
        out_shape=[jax.ShapeDtypeStruct((T, BRANCH_W), BF16), jax.ShapeDtypeStruct((T, BRANCH_W), F32),
                   jax.ShapeDtypeStruct((N_HEADS, T, 1), F32)],
        scratch_shapes=[pltpu.VMEM((2, tq, tk), BF16), pltpu.VMEM((2, tq, tk), BF16),
                        pltpu.VMEM((2, tq, 1), F32), pltpu.VMEM((2, tq, 1), F32), pltpu.VMEM((2, tq, 1), F32),
                        pltpu.VMEM((2, tq, LANES), F32)],
        compiler_params=_cp(("parallel", "parallel", "parallel")),
    )(qa, ka, proj)


def _attn_bwd_dq2(qa, ka, proj, do, lse, delta, n_seq, name):
    T = qa.shape[0]
    S = T // n_seq
    tq, tk, rg = min(Q_TILE, S), min(K_CHUNK, S), ROW_GROUP
    nq, per = S // tq, tq // tk
    vc = OFF_V // LANES

    def body(q_ref, k_ref, v_ref, do_ref, lse_ref, dl_ref, dq_ref, ds_s, acc_s):
        qi = pl.program_id(2)
        acc_s[...] = jnp.zeros_like(acc_s)
        sels = _pair_masks()

        def chunk(kc, masked):
            k0 = pl.multiple_of(kc * tk, tk)
            v = v_ref[pl.ds(k0, tk), :]
            for hh in range(2):
                hl = slice(hh * LANES, (hh + 1) * LANES)
                kh = k_ref[pl.ds(k0, tk), hl]
                s_all = lax.dot_general(q_ref[:, hl], kh, _NT, preferred_element_type=F32)
                dom = jnp.where(sels[hh], do_ref[...], jnp.zeros_like(do_ref[...]))
                dp_all = lax.dot_general(dom, v, _NT, preferred_element_type=F32)
                for r in range(tq // rg):
                    rows = slice(r * rg, (r + 1) * rg)
                    p = jnp.exp(s_all[rows, :] - lse_ref[hh, rows])
                    if masked:
                        p = jnp.where(_band_mask(qi * tq + r * rg, k0, rg, tk), p, 0.0)
                    ds_s[hh, rows, :] = (p * (dp_all[rows, :] - dl_ref[hh, rows])).astype(BF16)
                acc_s[hh] += jnp.dot(ds_s[hh], kh, preferred_element_type=F32)

        def unmasked(kc, carry):
            chunk(kc, False)
            return carry

        lax.fori_loop(0, qi * per, unmasked, 0)
        for d in range(per):
            chunk(qi * per + d, True)
        dq = jnp.where(sels[0], acc_s[0], pltpu.roll(acc_s[1], HEAD_DIM, 1))
        dq_ref[...] = (dq * (HEAD_DIM ** -0.5)).astype(BF16)

    qmap = lambda b, j, qi: (b * nq + qi, j)
    col1 = pl.BlockSpec((2, tq, 1), lambda b, j, qi: (j, b * nq + qi, 0))
    return pl.pallas_call(
        body, name=name, grid=(n_seq, N_HEADS // 2, nq),
        in_specs=[pl.BlockSpec((tq, 2 * LANES), qmap),
                  pl.BlockSpec((S, 2 * LANES), lambda b, j, qi: (b, j)),
                  pl.BlockSpec((S, LANES), lambda b, j, qi: (b, vc + j)),
                  pl.BlockSpec((tq, LANES), qmap), col1, col1],
        out_specs=pl.BlockSpec((tq, LANES), qmap),
        out_shape=jax.ShapeDtypeStruct((T, BRANCH_W), BF16),
        scratch_shapes=[pltpu.VMEM((2, tq, tk), BF16), pltpu.VMEM((2, tq, LANES), F32)],
        compiler_params=_cp(("parallel", "parallel", "parallel")),
    )(qa, ka, proj, do, lse, delta)


def _attn_bwd_dkv2(qa, ka, proj, do, lse, delta, n_seq, name):
    T = qa.shape[0]
    S = T // n_seq
    tkt, tqc, rg = min(Q_TILE, S), min(K_CHUNK, S), ROW_GROUP // 2
    nk, per, nqc = S // tkt, tkt // tqc, S // tqc
    vc = OFF_V // LANES

    def body(q_ref, k_ref, v_ref, do_ref, lse_ref, dl_ref, dk_ref, dv_ref, dfk_ref,
             p_s, ds_s, dk_s, dv_s, df_s):
        ki = pl.program_id(2)
        dk_s[...] = jnp.zeros_like(dk_s)
        dv_s[...] = jnp.zeros_like(dv_s)
        df_s[...] = jnp.zeros_like(df_s)
        sels = _pair_masks()
        v = v_ref[...]

        def chunk(qc, masked):
            q0 = pl.multiple_of(qc * tqc, tqc)
            do_ = do_ref[pl.ds(q0, tqc), :]
            for hh in range(2):
                hl = slice(hh * LANES, (hh + 1) * LANES)
                qh = q_ref[pl.ds(q0, tqc), hl]
                s_all = lax.dot_general(qh, k_ref[:, hl], _NT, preferred_element_type=F32)
                dom = jnp.where(sels[hh], do_, jnp.zeros_like(do_))
                dp_all = lax.dot_general(dom, v, _NT, preferred_element_type=F32)
                dfp = jnp.zeros((1, tkt), F32)
                for r in range(tqc // rg):
                    rows = slice(r * rg, (r + 1) * rg)
                    qrows = pl.ds(q0 + r * rg, rg)
                    p = jnp.exp(s_all[rows, :] - lse_ref[hh, qrows])
                    if masked:
                        p = jnp.where(_band_mask(q0 + r * rg, ki * tkt, rg, tkt), p, 0.0)
                    ds = p * (dp_all[rows, :] - dl_ref[hh, qrows])
                    p_s[hh, rows, :] = p.astype(BF16)
                    ds_s[hh, rows, :] = ds.astype(BF16)
                    dfp = dfp + jnp.sum(ds, axis=0, keepdims=True)
                df_s[hh] -= dfp
                dv_s[hh] += lax.dot_general(p_s[hh], do_, _TN, preferred_element_type=F32)
                dk_s[hh] += lax.dot_general(ds_s[hh], qh, _TN, preferred_element_type=F32)

        for d in range(per):
            chunk(ki * per + d, True)

        def unmasked(qc, carry):
            chunk(qc, False)
            return carry

        lax.fori_loop((ki + 1) * per, nqc, unmasked, 0)
        dk_ref[...] = jnp.where(sels[0], dk_s[0], pltpu.roll(dk_s[1], HEAD_DIM, 1)).astype(BF16)
        dv_ref[...] = jnp.where(sels[0], dv_s[0], dv_s[1]).astype(BF16)
        dfk_ref[...] = df_s[...]

    kmap = lambda b, j, ki: (b * nk + ki, j)
    col1 = pl.BlockSpec((2, S, 1), lambda b, j, ki: (j, b, 0))
    rowk = pl.BlockSpec((2, 1, tkt), lambda b, j, ki: (j, 0, b * nk + ki))
    return pl.pallas_call(
        body, name=name, grid=(n_seq, N_HEADS // 2, nk),
        in_specs=[pl.BlockSpec((S, 2 * LANES), lambda b, j, ki: (b, j)),
                  pl.BlockSpec((tkt, 2 * LANES), kmap),
                  pl.BlockSpec((tkt, LANES), lambda b, j, ki: (b * nk + ki, vc + j)),
                  pl.BlockSpec((S, LANES), lambda b, j, ki: (b, j)), col1, col1],
        out_specs=[pl.BlockSpec((tkt, LANES), kmap), pl.BlockSpec((tkt, LANES), kmap), rowk],
        out_shape=[jax.ShapeDtypeStruct((T, BRANCH_W), BF16), jax.ShapeDtypeStruct((T, BRANCH_W), BF16),
                   jax.ShapeDtypeStruct((N_HEADS, 1, T), F32)],
        scratch_shapes=[pltpu.VMEM((2, tqc, tkt), BF16), pltpu.VMEM((2, tqc, tkt), BF16),
                        pltpu.VMEM((2, tkt, LANES), F32),
                        pltpu.VMEM((2, tkt, LANES), F32), pltpu.VMEM((2, 1, tkt), F32)],
        compiler_params=_cp(("parallel", "parallel", "parallel")),
    )(qa, ka, proj, do, lse, delta)


def _shift_down(v, k, row):
    return jnp.where(row >= k, pltpu.roll(v, k, 0), 0.0)


def _shift_up(v, k, row, S):
    return jnp.where(row < S - k, pltpu.roll(v, S - k, 0), 0.0)


def _pool_diff(uf, w, row):
    acc, k = uf, 1
    while k < w:
        acc = acc + _shift_down(acc, k, row)
        k *= 2
    n = jnp.minimum(row + 1, w).astype(F32)
    return acc / n - uf


def _pool_fwd(proj, pool_w, pool_scale, n_seq, name):
    T = proj.shape[0]
    S = T // n_seq

    def body(u_ref, w_ref, sc_ref, o_ref):
        g = pl.program_id(1)
        row = lax.broadcasted_iota(jnp.int32, (S, POOL_GD), 0)
        uf = u_ref[...].astype(F32)
        d = _pool_diff(uf, POOL_WINDOWS[0], row)
        for gi in range(1, len(POOL_WINDOWS)):
            d = jnp.where(g == gi, _pool_diff(uf, POOL_WINDOWS[gi], row), d)
        e = jnp.dot(d.astype(BF16), w_ref[0], preferred_element_type=F32)
        o_ref[...] = (e * sc_ref[...]).astype(BF16)

    uc = OFF_U // POOL_GD
    return pl.pallas_call(
        body, name=name, grid=(n_seq, len(POOL_WINDOWS)),
        in_specs=[pl.BlockSpec((S, POOL_GD), lambda b, g: (b, uc + g)),
                  pl.BlockSpec((1, POOL_GD, POOL_GD), lambda b, g: (g, 0, 0)),
                  pl.BlockSpec((1, POOL_GD), lambda b, g: (0, g))],
        out_specs=pl.BlockSpec((S, POOL_GD), lambda b, g: (b, g)),
        out_shape=jax.ShapeDtypeStruct((T, BRANCH_W), BF16),
        compiler_params=_cp(("parallel", "parallel")),
    )(proj, pool_w, pool_scale)


def _pool_bwd(proj, dout, pool_w, pool_scale, n_seq, name):
    T = proj.shape[0]
    S = T // n_seq

    def body(u_ref, do_ref, w_ref, sc_ref, du_ref, dw_ref, dsc_ref):
        g, b = pl.program_id(0), pl.program_id(1)
        row = lax.broadcasted_iota(jnp.int32, (S, POOL_GD), 0)
        uf = u_ref[...].astype(F32)
        d = _pool_diff(uf, POOL_WINDOWS[0], row)
        for gi in range(1, len(POOL_WINDOWS)):
            d = jnp.where(g == gi, _pool_diff(uf, POOL_WINDOWS[gi], row), d)
        db16 = d.astype(BF16)
        w = w_ref[0]
        e = jnp.dot(db16, w, preferred_element_type=F32)
        dof = do_ref[...].astype(F32)
        dsc = jnp.sum(dof * e, axis=0, keepdims=True)
        de = (dof * sc_ref[...]).astype(BF16)
        dd = lax.dot_general(de, w, (((1,), (1,)), ((), ())), preferred_element_type=F32)
        dw = lax.dot_general(db16, de, (((0,), (0,)), ((), ())), preferred_element_type=F32)
        du = jnp.zeros_like(dd)
        for gi, wlen in enumerate(POOL_WINDOWS):
            n = jnp.minimum(row + 1, wlen).astype(F32)
            acc, k = dd / n, 1
            while k < wlen:
                acc = acc + _shift_up(acc, k, row, S)
                k *= 2
            du = jnp.where(g == gi, acc - dd, du)
        du_ref[...] = du.astype(BF16)

        @pl.when(b == 0)
        def _():
            dw_ref[0] = dw
            dsc_ref[...] = dsc

        @pl.when(b > 0)
        def _():
            dw_ref[0] += dw
            dsc_ref[...] += dsc

    uc = OFF_U // POOL_GD
    return pl.pallas_call(
        body, name=name, grid=(len(POOL_WINDOWS), n_seq),
        in_specs=[pl.BlockSpec((S, POOL_GD), lambda g, b: (b, uc + g)),
                  pl.BlockSpec((S, POOL_GD), lambda g, b: (b, g)),
                  pl.BlockSpec((1, POOL_GD, POOL_GD), lambda g, b: (g, 0, 0)),
                  pl.BlockSpec((1, POOL_GD), lambda g, b: (0, g))],
        out_specs=[pl.BlockSpec((S, POOL_GD), lambda g, b: (b, g)),
                   pl.BlockSpec((1, POOL_GD, POOL_GD), lambda g, b: (g, 0, 0)),
                   pl.BlockSpec((1, POOL_GD), lambda g, b: (0, g))],
        out_shape=[jax.ShapeDtypeStruct((T, BRANCH_W), BF16),
                   jax.ShapeDtypeStruct((len(POOL_WINDOWS), POOL_GD, POOL_GD), F32),
                   jax.ShapeDtypeStruct((1, BRANCH_W), F32)],
        compiler_params=_cp(("parallel", "arbitrary")),
    )(proj, dout, pool_w, pool_scale)


def _conv_fwd(proj, conv_w, n_seq, name):
    T = proj.shape[0]
    S = T // n_seq
    nc = BRANCH_W // LANES

    def body(cv_ref, cb_ref, cc_ref, w_ref, o_ref):
        row = lax.broadcasted_iota(jnp.int32, (S, LANES), 0)
        z = cc_ref[...].astype(F32) * cv_ref[...].astype(F32)
        w = w_ref[...]
        y = w[0:1] * _shift_down(z, 2, row) + w[1:2] * _shift_down(z, 1, row) + w[2:3] * z
        o_ref[...] = (cb_ref[...].astype(F32) * y).astype(BF16)

    def col(off):
        return pl.BlockSpec((S, LANES), lambda b, j: (b, off // LANES + j))

    return pl.pallas_call(
        body, name=name, grid=(n_seq, nc),
        in_specs=[col(OFF_CV), col(OFF_CB), col(OFF_CC), pl.BlockSpec((CONV_K, LANES), lambda b, j: (0, j))],
        out_specs=pl.BlockSpec((S, LANES), lambda b, j: (b, j)),
        out_shape=jax.ShapeDtypeStruct((T, BRANCH_W), BF16),
        compiler_params=_cp(("parallel", "parallel")),
    )(proj, proj, proj, conv_w)


def _conv_bwd(proj, dout, conv_w, n_seq, name):
    T = proj.shape[0]
    S = T // n_seq
    nc = BRANCH_W // LANES

    def body(cv_ref, cb_ref, cc_ref, do_ref, w_ref, dcv_ref, dcb_ref, dcc_ref, dw_ref):
        b = pl.program_id(1)
        row = lax.broadcasted_iota(jnp.int32, (S, LANES), 0)
        cv, cb, cc = cv_ref[...].astype(F32), cb_ref[...].astype(F32), cc_ref[...].astype(F32)
        dof = do_ref[...].astype(F32)
        w = w_ref[...]
        z = cc * cv
        z1, z2 = _shift_down(z, 1, row), _shift_down(z, 2, row)
        y = w[0:1] * z2 + w[1:2] * z1 + w[2:3] * z
        dcb_ref[...] = (dof * y).astype(BF16)
        dy = dof * cb
        dz = w[2:3] * dy + w[1:2] * _shift_up(dy, 1, row, S) + w[0:1] * _shift_up(dy, 2, row, S)
        dcc_ref[...] = (dz * cv).astype(BF16)
        dcv_ref[...] = (dz * cc).astype(BF16)
        dws = [jnp.sum(dy * zk, axis=0, keepdims=True) for zk in (z2, z1, z)]

        @pl.when(b == 0)
        def _():
            for kk in range(CONV_K):
                dw_ref[kk:kk + 1, :] = dws[kk]

        @pl.when(b > 0)
        def _():
            for kk in range(CONV_K):
                dw_ref[kk:kk + 1, :] += dws[kk]

    def col(off):
        return pl.BlockSpec((S, LANES), lambda j, b: (b, off // LANES + j))

    out = pl.BlockSpec((S, LANES), lambda j, b: (b, j))
    wsp = pl.BlockSpec((CONV_K, LANES), lambda j, b: (0, j))
    act = jax.ShapeDtypeStruct((T, BRANCH_W), BF16)
    return pl.pallas_call(
        body, name=name, grid=(nc, n_seq),
        in_specs=[col(OFF_CV), col(OFF_CB), col(OFF_CC), out, wsp],
        out_specs=[out, out, out, wsp],
        out_shape=[act, act, act, jax.ShapeDtypeStruct((CONV_K, BRANCH_W), F32)],
        compiler_params=_cp(("parallel", "arbitrary")),
    )(proj, proj, proj, dout, conv_w)


def _mix_fwd(oa, ob, oc, wpa, wpp, wpc, proj, b_gate, name):
    T = oa.shape[0]
    tm = min(256, T)

    def body(oa_ref, ob_ref, oc_ref, wa_ref, wp_ref, wc_ref, g_ref, bg_ref, o_ref):
        acc = jnp.zeros((tm, D_MODEL), F32)
        for i, (x_ref, w_ref) in enumerate(((oa_ref, wa_ref), (ob_ref, wp_ref), (oc_ref, wc_ref))):
            y = jnp.dot(x_ref[...], w_ref[...], preferred_element_type=F32)
            sl = slice(i * D_MODEL, (i + 1) * D_MODEL)
            acc = acc + _sigmoid(g_ref[:, sl].astype(F32) + bg_ref[:, sl]) * y
        o_ref[...] = acc.astype(BF16)

    br = pl.BlockSpec((tm, BRANCH_W), lambda i: (i, 0))
    wsp = pl.BlockSpec((BRANCH_W, D_MODEL), lambda i: (0, 0))
    return pl.pallas_call(
        body, name=name, grid=(T // tm,),
        in_specs=[br, br, br, wsp, wsp, wsp, pl.BlockSpec((tm, GATE_W), lambda i: (i, 0)),
                  pl.BlockSpec((1, GATE_W), lambda i: (0, 0))],
        out_specs=pl.BlockSpec((tm, D_MODEL), lambda i: (i, 0)),
        out_shape=jax.ShapeDtypeStruct((T, D_MODEL), BF16),
        compiler_params=_cp(("parallel",)),
    )(oa, ob, oc, wpa, wpp, wpc, proj, b_gate)


def _mix_bwd(oa, ob, oc, wpa, wpp, wpc, proj, b_gate, dmixed, name):
    T = oa.shape[0]
    tm = min(256, T)

    def body(oa_ref, ob_ref, oc_ref, wa_ref, wp_ref, wc_ref, g_ref, bg_ref, dm_ref,
             dya_ref, dyb_ref, dyc_ref, dg_ref, dbg_ref):
        i0 = pl.program_id(0)
        dm = dm_ref[...].astype(F32)
        parts = []
        for i, (x_ref, w_ref, dy_ref) in enumerate(((oa_ref, wa_ref, dya_ref), (ob_ref, wp_ref, dyb_ref),
                                                    (oc_ref, wc_ref, dyc_ref))):
            y = jnp.dot(x_ref[...], w_ref[...], preferred_element_type=F32)
            sl = slice(i * D_MODEL, (i + 1) * D_MODEL)
            gate = _sigmoid(g_ref[:, sl].astype(F32) + bg_ref[:, sl])
            dy_ref[...] = (dm * gate).astype(BF16)
            dgl = dm * y * gate * (1.0 - gate)
            dg_ref[:, sl] = dgl.astype(BF16)
            parts.append(jnp.sum(dgl, axis=0, keepdims=True))

        @pl.when(i0 == 0)
        def _():
            for i in range(3):
                dbg_ref[:, i * D_MODEL:(i + 1) * D_MODEL] = parts[i]

        @pl.when(i0 > 0)
        def _():
            for i in range(3):
                dbg_ref[:, i * D_MODEL:(i + 1) * D_MODEL] += parts[i]

    br = pl.BlockSpec((tm, BRANCH_W), lambda i: (i, 0))
    wsp = pl.BlockSpec((BRANCH_W, D_MODEL), lambda i: (0, 0))
    row = pl.BlockSpec((tm, D_MODEL), lambda i: (i, 0))
    gsp = pl.BlockSpec((tm, GATE_W), lambda i: (i, 0))
    bsp = pl.BlockSpec((1, GATE_W), lambda i: (0, 0))
    act = jax.ShapeDtypeStruct((T, D_MODEL), BF16)
    return pl.pallas_call(
        body, name=name, grid=(T // tm,),
        in_specs=[br, br, br, wsp, wsp, wsp, gsp, bsp, row],
        out_specs=[row, row, row, gsp, bsp],
        out_shape=[act, act, act, jax.ShapeDtypeStruct((T, GATE_W), BF16),
                   jax.ShapeDtypeStruct((1, GATE_W), F32)],
        compiler_params=_cp(("arbitrary",)),
    )(oa, ob, oc, wpa, wpp, wpc, proj, b_gate, dmixed)


def _swiglu_fwd(ab, name):
    T = ab.shape[0]
    tm = min(256, T)

    def body(ab_ref, o_ref):
        a = ab_ref[:, :FFN_HIDDEN].astype(F32)
        o_ref[...] = (a * _sigmoid(a) * ab_ref[:, FFN_HIDDEN:].astype(F32)).astype(BF16)

    return pl.pallas_call(
        body, name=name, grid=(T // tm,),
        in_specs=[pl.BlockSpec((tm, 2 * FFN_HIDDEN), lambda i: (i, 0))],
        out_specs=pl.BlockSpec((tm, FFN_HIDDEN), lambda i: (i, 0)),
        out_shape=jax.ShapeDtypeStruct((T, FFN_HIDDEN), BF16),
        compiler_params=_cp(("parallel",)),
    )(ab)


def _swiglu_bwd(ab, ds, name):
    T = ab.shape[0]
    tm = min(256, T)

    def body(ab_ref, ds_ref, o_ref):
        a = ab_ref[:, :FFN_HIDDEN].astype(F32)
        b = ab_ref[:, FFN_HIDDEN:].astype(F32)
        dsf = ds_ref[...].astype(F32)
        sg = _sigmoid(a)
        o_ref[:, :FFN_HIDDEN] = (dsf * b * sg * (1.0 + a * (1.0 - sg))).astype(BF16)
        o_ref[:, FFN_HIDDEN:] = (dsf * a * sg).astype(BF16)

    full = pl.BlockSpec((tm, 2 * FFN_HIDDEN), lambda i: (i, 0))
    return pl.pallas_call(
        body, name=name, grid=(T // tm,),
        in_specs=[full, pl.BlockSpec((tm, FFN_HIDDEN), lambda i: (i, 0))],
        out_specs=full,
        out_shape=jax.ShapeDtypeStruct((T, 2 * FFN_HIDDEN), BF16),
        compiler_params=_cp(("parallel",)),
    )(ab, ds)


def _adamw(w, g, m, v, name):
    R, C = w.shape
    tr = R
    for cand in (256, 352, 128, 64, 8):
        if R > cand and R % cand == 0:
            tr = cand
            break

    def body(w_ref, g_ref, m_ref, v_ref, d_ref, nm_ref, nv_ref):
        gv = g_ref[...]
        nm = ADAM_B1 * m_ref[...] + (1.0 - ADAM_B1) * gv
        nv = ADAM_B2 * v_ref[...] + (1.0 - ADAM_B2) * (gv * gv)
        m_hat = nm / (1.0 - ADAM_B1 ** ADAM_STEP)
        v_hat = nv / (1.0 - ADAM_B2 ** ADAM_STEP)
        d_ref[...] = -ADAM_LR * (m_hat / (jnp.sqrt(v_hat) + ADAM_EPS) + ADAM_WD * w_ref[...])
        nm_ref[...] = nm
        nv_ref[...] = nv

    blk = pl.BlockSpec((tr, C), lambda i: (i, 0))
    sh = jax.ShapeDtypeStruct((R, C), F32)
    return pl.pallas_call(
        body, name=name, grid=(R // tr,), in_specs=[blk] * 4, out_specs=[blk] * 3, out_shape=[sh] * 3,
        compiler_params=_cp(("parallel",)),
    )(w, g, m, v)


def _sum_slabs(x, name):
    n, R, C = x.shape
    tr = R
    for cand in (512, 256, 128, 64, 32, 16, 8):
        if R > cand and R % cand == 0:
            tr = cand
            break

    def body(x_ref, o_ref):
        acc = x_ref[0].astype(F32)
        for j in range(1, n):
            acc = acc + x_ref[j].astype(F32)
        o_ref[...] = acc

    return pl.pallas_call(
        body, name=name, grid=(R // tr,), in_specs=[pl.BlockSpec((n, tr, C), lambda i: (0, i, 0))],
        out_specs=pl.BlockSpec((tr, C), lambda i: (i, 0)), out_shape=jax.ShapeDtypeStruct((R, C), F32),
        compiler_params=_cp(("parallel",)),
    )(x)


def _multi_gather(xs, layers, name):
    nt = len(xs)
    shapes = [x.shape if lay is None else x.shape[1:] for x, lay in zip(xs, layers)]

    def body(*refs):
        x_refs, out_refs = refs[:nt], refs[nt:2 * nt]
        send_sems, recv_sems, local_sems = refs[2 * nt:]
        x_, y_, c_ = lax.axis_index("x"), lax.axis_index("y"), lax.axis_index("c")
        me, sibling = (x_, y_, c_), (x_, y_, 1 - c_)
        chips = [(1 - x_, y_), (x_, 1 - y_), (1 - x_, 1 - y_)]

        def own_block(t):
            return x_refs[t] if layers[t] is None else x_refs[t].at[layers[t]]

        def copy(t, k, block, to, own=False):
            px, py, pc = block
            dst = out_refs[t].at[4 * px + 2 * py + pc]
            return pltpu.make_async_remote_copy(
                src_ref=own_block(t) if own else dst, dst_ref=dst,
                send_sem=send_sems.at[t, k], recv_sem=recv_sems.at[t, k],
                device_id=to, device_id_type=pl.DeviceIdType.MESH)

        mine, first, passed = [], [], []
        for t in range(nt):
            mine.append(pltpu.make_async_copy(own_block(t), out_refs[t].at[4 * x_ + 2 * y_ + c_], local_sems.at[t]))
            mine[-1].start()
            first.append([copy(t, 1 + j, me, (*chip, c_), own=True) for j, chip in enumerate(chips)]
                         + [copy(t, 0, me, sibling, own=True)])
            for cp in first[-1]:
                cp.start()
        for t in range(nt):
            for j, chip in enumerate(chips):
                copy(t, 1 + j, (*chip, c_), me).wait_recv()
                passed.append(copy(t, 4 + j, (*chip, c_), sibling))
                passed[-1].start()
        for t in range(nt):
            copy(t, 0, sibling, me).wait_recv()
            for j, chip in enumerate(chips):
                copy(t, 4 + j, (*chip, 1 - c_), me).wait_recv()
        for cp in [c for f in first for c in f] + passed:
            cp.wait_send()
        for cp in mine:
            cp.wait()

    hbm = pl.BlockSpec(memory_space=pl.ANY)
    return pl.pallas_call(
        body, name=name, out_shape=[jax.ShapeDtypeStruct((N_DEV,) + tuple(s), x.dtype) for s, x in zip(shapes, xs)],
        in_specs=[hbm] * nt, out_specs=[hbm] * nt,
        scratch_shapes=[pltpu.SemaphoreType.DMA((nt, 7)), pltpu.SemaphoreType.DMA((nt, 7)),
                        pltpu.SemaphoreType.DMA((nt,))],
    )(*xs)


def _multi_exchange(sends, name):
    nt = len(sends)

    def body(*refs):
        s_refs, r_refs = refs[:nt], refs[nt:2 * nt]
        send_sems, recv_sems, local_sems = refs[2 * nt:]
        x_, y_, c_ = lax.axis_index("x"), lax.axis_index("y"), lax.axis_index("c")
        me = 4 * x_ + 2 * y_ + c_
        mine, out, inc = [], [], []
        for t in range(nt):
            mine.append(pltpu.make_async_copy(s_refs[t].at[me], r_refs[t].at[me], local_sems.at[t]))
            mine[-1].start()
        for k in (2, 4, 6, 3, 5, 7, 1):
            px, py, pc = x_ ^ ((k >> 2) & 1), y_ ^ ((k >> 1) & 1), c_ ^ (k & 1)
            peer = 4 * px + 2 * py + pc
            for t in range(nt):
                def copy(src, dst):
                    return pltpu.make_async_remote_copy(
                        src_ref=s_refs[t].at[src], dst_ref=r_refs[t].at[dst],
                        send_sem=send_sems.at[t, k - 1], recv_sem=recv_sems.at[t, k - 1],
                        device_id=(px, py, pc), device_id_type=pl.DeviceIdType.MESH)

                out.append(copy(peer, me))
                inc.append(copy(me, peer))
        for cp in out:
            cp.start()
        for cp in inc:
            cp.wait_recv()
        for cp in out:
            cp.wait_send()
        for cp in mine:
            cp.wait()

    hbm = pl.BlockSpec(memory_space=pl.ANY)
    return pl.pallas_call(
        body, name=name, out_shape=[jax.ShapeDtypeStruct(s.shape, s.dtype) for s in sends],
        in_specs=[hbm] * nt, out_specs=[hbm] * nt,
        scratch_shapes=[pltpu.SemaphoreType.DMA((nt, N_DEV - 1)), pltpu.SemaphoreType.DMA((nt, N_DEV - 1)),
                        pltpu.SemaphoreType.DMA((nt,))],
    )(*sends)


_HBM = pl.BlockSpec(memory_space=pltpu.HBM)
_SEM = pl.BlockSpec(memory_space=pltpu.SEMAPHORE)
_PEER_ORDER = (2, 4, 6, 3, 5, 7, 1)


def _split_copies(src_refs, land_refs, send_sems, recv_sems, layers, per_peer):
    x_, y_, c_ = lax.axis_index("x"), lax.axis_index("y"), lax.axis_index("c")
    me = 4 * x_ + 2 * y_ + c_
    copies = []
    for k in _PEER_ORDER:
        px, py, pc = x_ ^ ((k >> 2) & 1), y_ ^ ((k >> 1) & 1), c_ ^ (k & 1)
        peer = 4 * px + 2 * py + pc
        for t in range(len(src_refs)):
            if per_peer:
                src = src_refs[t].at[peer]
            else:
                src = src_refs[t] if layers[t] is None else src_refs[t].at[layers[t]]
            copies.append(pltpu.make_async_remote_copy(
                src_ref=src, dst_ref=land_refs[t].at[me],
                send_sem=send_sems.at[t * (N_DEV - 1) + k - 1], recv_sem=recv_sems.at[t * (N_DEV - 1) + k - 1],
                device_id=(px, py, pc), device_id_type=pl.DeviceIdType.MESH))
    return copies


def _split_start(srcs, layers, per_peer, after, name):
    nt = len(srcs)
    if per_peer:
        land_shapes = [s.shape for s in srcs]
    else:
        land_shapes = [(N_DEV,) + tuple(s.shape if lay is None else s.shape[1:]) for s, lay in zip(srcs, layers)]

    def body(*refs):
        src_refs, land_refs = refs[:nt], refs[nt:2 * nt]
        send_sems, recv_sems = refs[2 * nt + 1], refs[2 * nt + 2]
        token = refs[-1]
        for cp in _split_copies(src_refs, land_refs, send_sems, recv_sems, layers, per_peer):
            cp.start()
        token[...] = jnp.zeros_like(token)

    lands = [pltpu.with_memory_space_constraint(lax.empty(s, x.dtype), pltpu.HBM) for s, x in zip(land_shapes, srcs)]
    srcs = [pltpu.with_memory_space_constraint(x, pltpu.HBM) for x in srcs]
    out = pl.pallas_call(
        body, name=name,
        out_shape=(pltpu.SemaphoreType.DMA((nt * (N_DEV - 1),)), pltpu.SemaphoreType.DMA((nt * (N_DEV - 1),)),
                   *[pltpu.HBM(x.shape, x.dtype) for x in srcs], *[pltpu.HBM(s, x.dtype) for s, x in zip(land_shapes, srcs)],
                   jax.ShapeDtypeStruct((8, LANES), F32)),
        in_specs=[_HBM] * (2 * nt) + [pl.BlockSpec(memory_space=pl.ANY)],
        out_specs=(_SEM, _SEM, *([_HBM] * (2 * nt)), pl.BlockSpec(memory_space=pltpu.VMEM)),
        input_output_aliases={i: 2 + i for i in range(2 * nt)},
        compiler_params=pltpu.CompilerParams(has_side_effects=pltpu.SideEffectType.DATAFLOW_SIDE_EFFECTING),
    )(*srcs, *lands, after)
    return out[0], out[1], list(out[2:2 + nt]), list(out[2 + nt:2 + 2 * nt]), out[-1]


def _split_wait(started, layers, per_peer, after, name):
    send_sems, recv_sems, srcs, lands, _ = started
    nt = len(srcs)

    def body(*refs):
        src_refs, land_refs = refs[:nt], refs[nt:2 * nt]
        s_sems, r_sems = refs[2 * nt], refs[2 * nt + 1]
        for cp in _split_copies(src_refs, land_refs, s_sems, r_sems, layers, per_peer):
            cp.wait_send()
            cp.wait_recv()

    out = pl.pallas_call(
        body, name=name,
        out_shape=tuple(pltpu.HBM(x.shape, x.dtype) for x in srcs + lands),
        in_specs=[_HBM] * (2 * nt) + [_SEM, _SEM, pl.BlockSpec(memory_space=pl.ANY)],
        out_specs=tuple([_HBM] * (2 * nt)),
        input_output_aliases={i: i for i in range(2 * nt)},
        compiler_params=pltpu.CompilerParams(has_side_effects=pltpu.SideEffectType.DATAFLOW_SIDE_EFFECTING),
    )(*srcs, *lands, send_sems, recv_sems, after)
    return list(out[nt:])


def _with_own(land, own):
    me = 4 * lax.axis_index("x") + 2 * lax.axis_index("y") + lax.axis_index("c")
    return lax.dynamic_update_slice_in_dim(land, own[None], me, axis=0)


def _runs(mapping):
    runs, c, n = [], 0, len(mapping)
    while c < n:
        if mapping[c] is None:
            c += 1
            continue
        sid, d, lo = mapping[c][0], mapping[c][1] - c, c
        while c < n and mapping[c] is not None and mapping[c][0] == sid and mapping[c][1] - c == d:
            c += 1
        runs.append((lo, c, sid, d))
    return runs


def _tile_plan(mapping, src_widths):
    runs = _runs(mapping)
    plan = []
    for t in range(len(mapping) // LANES):
        pieces = []
        for lo, hi, sid, d in runs:
            lo_t, hi_t = max(lo, t * LANES), min(hi, (t + 1) * LANES)
            if lo_t >= hi_t:
                continue
            a = ((lo_t + d) // LANES) * LANES
            win = min(2 * LANES, src_widths[sid] - a)
            shift = t * LANES + d - a
            pieces.append((sid, a, win, shift, lo_t - t * LANES, hi_t - t * LANES))
        plan.append(pieces)
    return plan


def _reblock(srcs, src_views, outs, out_views, name):
    R = srcs[0].shape[-2]
    tr = min(256, R)
    widths = {sid: srcs[ai].shape[-1] for sid, (ai, _) in src_views.items()}
    plans = [(ai, li, _tile_plan(mapping, widths)) for ai, li, mapping in out_views]
    ns = len(srcs)

    def body(*refs):
        s_refs, o_refs = refs[:ns], refs[ns:]
        cache = {}

        def shift_matrix(win, shift, lo, hi):
            key = (win, shift, lo, hi)
            if key not in cache:
                r = lax.broadcasted_iota(jnp.int32, (win, LANES), 0)
                c = lax.broadcasted_iota(jnp.int32, (win, LANES), 1)
                hit = jnp.logical_and(r - c == shift, jnp.logical_and(c >= lo, c < hi))
                cache[key] = jnp.where(hit, 1.0, 0.0).astype(BF16)
            return cache[key]

        for ai, li, plan in plans:
            for t, pieces in enumerate(plan):
                acc = None
                for sid, a, win, shift, lo, hi in pieces:
                    sa, sl = src_views[sid]
                    src = s_refs[sa][:, a:a + win] if sl is None else s_refs[sa][sl, :, a:a + win]
                    part = jnp.dot(src, shift_matrix(win, shift, lo, hi), preferred_element_type=F32)
                    acc = part if acc is None else acc + part
                val = jnp.zeros((tr, LANES), BF16) if acc is None else acc.astype(BF16)
                if li is None:
                    o_refs[ai][:, t * LANES:(t + 1) * LANES] = val
                else:
                    o_refs[ai][li, :, t * LANES:(t + 1) * LANES] = val

    def spec(shape):
        if len(shape) == 2:
            return pl.BlockSpec((tr, shape[1]), lambda i: (i, 0))
        return pl.BlockSpec((shape[0], tr, shape[2]), lambda i: (0, i, 0))

    return pl.pallas_call(
        body, name=name, grid=(R // tr,), in_specs=[spec(s.shape) for s in srcs],
        out_specs=[spec(s) for s in outs], out_shape=[jax.ShapeDtypeStruct(s, BF16) for s in outs],
        compiler_params=_cp(("parallel",)),
    )(*srcs)


SHARDED = ("w_in", "w_gate_up", "w_proj_attn", "w_proj_pool", "w_proj_conv", "w_out", "w_down")
WEIGHT_ORDER = ("attn_norm", "w_in", "b_forget", "b_gate", "w_proj_attn", "pool_w", "pool_scale", "w_proj_pool",
                "conv_w", "w_proj_conv", "w_out", "ffn_norm", "w_gate_up", "w_down", "final_norm")
IN_SHARD, IN_SHARD_PAD = IN_COLS // N_DEV, 896
GU_SHARD, GU_SHARD_PAD = 2 * FFN_HIDDEN // N_DEV, 768


def _w_in_col(c):
    if c < GATE_W:
        return c + 3592
    if c < OFF_U:
        return c - OFF_Q
    return c - OFF_U + 1544


def _w_in_full(gathered, name):
    main = [divmod(_w_in_col(c), IN_SHARD) for c in range(MAIN_COLS)]
    fcols = [divmod(1536 + c, IN_SHARD) if c < N_HEADS else None for c in range(LANES)]
    R = gathered.shape[1]
    return _reblock([gathered], {i: (0, i) for i in range(N_DEV)}, [(R, MAIN_COLS), (R, LANES)],
                    [(0, None, main), (1, None, fcols)], name)


def _w_in_slabs(dmain, dwf, name):
    inv = {_w_in_col(c): ("m", c) for c in range(MAIN_COLS)}
    inv.update({1536 + c: ("f", c) for c in range(N_HEADS)})
    views = []
    for i in range(N_DEV):
        mapping = [inv[IN_SHARD * i + j] if j < IN_SHARD else None for j in range(IN_SHARD_PAD)]
        views.append((0, i, mapping))
    R = dmain.shape[0]
    return _reblock([dmain, dwf], {"m": (0, None), "f": (1, None)}, [(N_DEV, R, IN_SHARD_PAD)], views, name)[0]


def _w_gu_full(gathered, name):
    mapping = [divmod(c, GU_SHARD) for c in range(2 * FFN_HIDDEN)]
    R = gathered.shape[1]
    return _reblock([gathered], {i: (0, i) for i in range(N_DEV)}, [(R, 2 * FFN_HIDDEN)], [(0, None, mapping)], name)[0]


def _w_gu_slabs(dw, name):
    views = [(0, i, [("w", GU_SHARD * i + j) if j < GU_SHARD else None for j in range(GU_SHARD_PAD)])
             for i in range(N_DEV)]
    R = dw.shape[0]
    return _reblock([dw], {"w": (0, None)}, [(N_DEV, R, GU_SHARD_PAD)], views, name)[0]


def _layer_fwd(x, W, n_seq, l):
    T = x.shape[0]
    sfx = f"_l{l}"
    h1 = _rms_fwd(x, W["attn_norm"], "rms1" + sfx)
    proj = _matmul(h1, W["w_main"], mode="nn", out_dtype=BF16, name="proj_main" + sfx)
    f = _matmul(h1, W["w_f"], mode="nn", out_dtype=F32, name="proj_f" + sfx)
    qa, ka = _fox_prep(f, W["b_forget"], proj, n_seq, "fox_prep" + sfx)
    oa, oa32, lse = _attn_fwd2(qa, ka, proj, n_seq, "attn_fwd" + sfx)
    if "late" in W:
        W.update(W.pop("late")(oa))
    ob = _pool_fwd(proj, W["pool_w"], W["pool_scale"], n_seq, "pool_fwd" + sfx)
    oc = _conv_fwd(proj, W["conv_w"], n_seq, "conv_fwd" + sfx)
    mixed = _mix_fwd(oa, ob, oc, W["w_proj_attn"], W["w_proj_pool"], W["w_proj_conv"], proj, W["b_gate"],
                     "mix_fwd" + sfx)
    x2 = _matmul(mixed, W["w_out"], mode="nn", out_dtype=F32, name="out_proj" + sfx, residual=x)
    h2 = _rms_fwd(x2, W["ffn_norm"], "rms2" + sfx)
    ab = _matmul(h2, W["w_gate_up"], mode="nn", out_dtype=BF16, name="gate_up" + sfx)
    s = _swiglu_fwd(ab, "swiglu_fwd" + sfx)
    x3 = _matmul(s, W["w_down"], mode="nn", out_dtype=F32, name="down" + sfx, tm=1024, tn=1024, tk=1408,
                 residual=x2)
    saved = dict(x=x, h1=h1, proj=proj, f=f, qa=qa, ka=ka, oa=oa, oa32=oa32, lse=lse, ob=ob, oc=oc, mixed=mixed, x2=x2,
                 h2=h2, ab=ab, s=s)
    return x3, saved


def _layer_bwd(dx3, dx3b, W, sv, n_seq, l, stage=None):
    T = dx3.shape[0]
    sfx = f"_l{l}"
    G = {}
    stage = stage or (lambda l, group, G, W: W)
    ds = _matmul(dx3b, W["w_down"], mode="nt", out_dtype=BF16, name="d_s" + sfx, tm=1024, tn=1408)
    G["w_down"] = _matmul(sv["s"], dx3b, mode="tn", out_dtype=BF16, name="dw_down" + sfx, tm=256, tn=1024)
    dab = _swiglu_bwd(sv["ab"], ds, "swiglu_bwd" + sfx)
    dh2 = _matmul(dab, W["w_gate_up"], mode="nt", out_dtype=BF16, name="d_h2" + sfx, tm=1024, tn=1024, tk=1408)
    G["w_gate_up"] = _matmul(sv["h2"], dab, mode="tn", out_dtype=BF16, name="dw_gate_up" + sfx, tm=1024)
    W = stage(l, "ffn", G, W)
    dx2, dx2b, G["ffn_norm"] = _rms_bwd(sv["x2"], W["ffn_norm"], dh2, dx3, "rms2_bwd" + sfx)
    dmixed = _matmul(dx2b, W["w_out"], mode="nt", out_dtype=BF16, name="d_mixed" + sfx)
    G["w_out"] = _matmul(sv["mixed"], dx2b, mode="tn", out_dtype=BF16, name="dw_out" + sfx, tm=1024)
    dya, dyb, dyc, dg, G["b_gate"] = _mix_bwd(sv["oa"], sv["ob"], sv["oc"], W["w_proj_attn"], W["w_proj_pool"],
                                              W["w_proj_conv"], sv["proj"], W["b_gate"], dmixed, "mix_bwd" + sfx)
    douts = {}
    for br, dy, o in (("attn", dya, sv["oa"]), ("pool", dyb, sv["ob"]), ("conv", dyc, sv["oc"])):
        douts[br] = _matmul(dy, W["w_proj_" + br], mode="nt", out_dtype=BF16, name=f"d_{br}_out" + sfx)
        G["w_proj_" + br] = _matmul(o, dy, mode="tn", out_dtype=BF16, name=f"dw_proj_{br}" + sfx, tm=512)
    W = stage(l, "mix", G, W)
    dcv, dcb, dcc, G["conv_w"] = _conv_bwd(sv["proj"], douts["conv"], W["conv_w"], n_seq, "conv_bwd" + sfx)
    du, G["pool_w"], G["pool_scale"] = _pool_bwd(sv["proj"], douts["pool"], W["pool_w"], W["pool_scale"], n_seq,
                                                 "pool_bwd" + sfx)
    delta = _attn_delta(douts["attn"], sv["oa32"], "attn_delta" + sfx)
    dq = _attn_bwd_dq2(sv["qa"], sv["ka"], sv["proj"], douts["attn"], sv["lse"], delta, n_seq, "attn_dq" + sfx)
    dk, dv, dFk = _attn_bwd_dkv2(sv["qa"], sv["ka"], sv["proj"], douts["attn"], sv["lse"], delta, n_seq,
                                 "attn_dkv" + sfx)
    dF = jnp.pad(dFk.reshape(N_HEADS, T).T, ((0, 0), (0, LANES - N_HEADS)))
    df, G["b_forget"] = _fox_cumsum_bwd(sv["f"], W["b_forget"], dF, n_seq, "fox_cumsum_bwd" + sfx)
    dproj = jnp.concatenate([dg, dq, dk, dv, du, dcv, dcb, dcc], axis=1)
    G["w_main"] = _matmul(sv["h1"], dproj, mode="tn", out_dtype=BF16, name="dw_main" + sfx, tm=1024)
    G["w_f"] = _matmul(sv["h1"], df, mode="tn", out_dtype=BF16, name="dw_f" + sfx, tm=1024)
    W = stage(l, "w_in", G, W)
    dh1 = _matmul(df, W["w_f"], mode="nt", out_dtype=F32, name="d_h1_f" + sfx)
    dh1 = _matmul(dproj, W["w_main"], mode="nt", out_dtype=F32, name="d_h1_main" + sfx, tm=1024, tn=1024, tk=1664,
                  residual=dh1)
    dx, dxb, G["attn_norm"] = _rms_bwd(sv["x"], W["attn_norm"], dh1, dx2, "rms1_bwd" + sfx)
    return dx, dxb, G


def _replicated_operands(rep, l):
    W = {}
    W["attn_norm"], W["ffn_norm"] = rep["attn_norm"][l], rep["ffn_norm"][l]
    W["b_forget"] = jnp.pad(rep["b_forget"][l].reshape(1, N_HEADS), ((0, 0), (0, LANES - N_HEADS)))
    W["b_gate"] = rep["b_gate"][l].reshape(1, GATE_W)
    W["pool_w"] = rep["pool_w"][l].astype(BF16)
    W["pool_scale"] = rep["pool_scale"][l].reshape(1, BRANCH_W)
    return W


def _local_step(x, target, get_W, final_norm, stage=None):
    n_seq, S, Dm = x.shape
    T = n_seq * S
    xt = x.reshape(T, Dm)
    saved, Ws = [], []
    for l in range(DEPTH):
        Ws.append(get_W(l, xt))
        xt, sv = _layer_fwd(xt, Ws[l], n_seq, l)
        saved.append(sv)
    loss, dx, dxb, g_final = _loss_head(xt, final_norm, target.reshape(T, Dm), "loss_head")
    grads = [None] * DEPTH
    for l in reversed(range(DEPTH)):
        dx, dxb, grads[l] = _layer_bwd(dx, dxb, Ws[l], saved[l], n_seq, l, stage)
    return loss, dx.reshape(n_seq, S, Dm), grads, g_final


def _padded_shards(weights):
    sh = {n: weights[n].astype(BF16) for n in SHARDED}
    sh["w_in"] = jnp.pad(sh["w_in"], ((0, 0), (0, 0), (0, IN_SHARD_PAD - IN_SHARD)))
    sh["w_gate_up"] = jnp.pad(sh["w_gate_up"], ((0, 0), (0, 0), (0, GU_SHARD_PAD - GU_SHARD)))
    return sh


def _full_operands(g, l):
    W = {}
    if "w_in" in g:
        W["w_main"], W["w_f"] = _w_in_full(g["w_in"], f"w_in_full_l{l}")
    if "w_gate_up" in g:
        W["w_gate_up"] = _w_gu_full(g["w_gate_up"], f"w_gate_up_full_l{l}")
    for n in ("w_proj_attn", "w_proj_pool", "w_proj_conv"):
        if n in g:
            W[n] = jnp.transpose(g[n], (1, 0, 2)).reshape(BRANCH_W, D_MODEL)
    if "w_out" in g:
        W["w_out"] = g["w_out"].reshape(D_MODEL, D_MODEL)
    if "w_down" in g:
        W["w_down"] = g["w_down"].reshape(FFN_HIDDEN, D_MODEL)
    return W


GRAD_GROUPS = {"ffn": ("w_down", "w_gate_up"),
               "mix": ("w_out", "w_proj_attn", "w_proj_pool", "w_proj_conv"),
               "w_in": ("w_in",)}


def _grad_slabs(G, n, l):
    if n == "w_in":
        return _w_in_slabs(G["w_main"], G["w_f"], f"w_in_slabs_l{l}")
    if n == "w_gate_up":
        return _w_gu_slabs(G["w_gate_up"], f"w_gate_up_slabs_l{l}")
    if n == "w_out":
        return G["w_out"].reshape(N_DEV, D_MODEL // N_DEV, D_MODEL)
    if n == "w_down":
        return G["w_down"].reshape(N_DEV, FFN_HIDDEN // N_DEV, D_MODEL)
    return jnp.transpose(G[n].reshape(BRANCH_W, N_DEV, D_MODEL // N_DEV), (1, 0, 2))


def _sum_layer_grads(recv, l):
    out = {n: _sum_slabs(r, f"sum_{n}_l{l}") for n, r in recv.items()}
    if "w_in" in out:
        out["w_in"] = out["w_in"][:, :IN_SHARD]
    if "w_gate_up" in out:
        out["w_gate_up"] = out["w_gate_up"][:, :GU_SHARD]
    return out


def _sum_small(xs, name):
    def body(*refs):
        for x_ref, o_ref in zip(refs[:len(xs)], refs[len(xs):]):
            acc = x_ref[0]
            for j in range(1, N_DEV):
                acc = acc + x_ref[j]
            o_ref[...] = acc

    return pl.pallas_call(
        body, name=name, out_shape=[jax.ShapeDtypeStruct(x.shape[1:], F32) for x in xs],
        compiler_params=_cp(),
    )(*xs)


def _as_2d(a):
    if a.ndim == 1:
        return a.reshape(1, -1)
    return a.reshape(-1, a.shape[-1])


def kernel(x, attn_norm, w_in, b_forget, b_gate, w_proj_attn, pool_w, pool_scale, w_proj_pool, conv_w, w_proj_conv, w_out, ffn_norm, w_gate_up, w_down, final_norm, loss_target, m_attn_norm, m_w_in, m_b_forget, m_b_gate, m_w_proj_attn, m_pool_w, m_pool_scale, m_w_proj_pool, m_conv_w, m_w_proj_conv, m_w_out, m_ffn_norm, m_w_gate_up, m_w_down, m_final_norm, v_attn_norm, v_w_in, v_b_forget, v_b_gate, v_w_proj_attn, v_pool_w, v_pool_scale, v_w_proj_pool, v_conv_w, v_w_proj_conv, v_w_out, v_ffn_norm, v_w_gate_up, v_w_down, v_final_norm):
    weights = dict(attn_norm=attn_norm, w_in=w_in, b_forget=b_forget, b_gate=b_gate, w_proj_attn=w_proj_attn,
                   pool_w=pool_w, pool_scale=pool_scale, w_proj_pool=w_proj_pool, conv_w=conv_w,
                   w_proj_conv=w_proj_conv, w_out=w_out, ffn_norm=ffn_norm, w_gate_up=w_gate_up, w_down=w_down,
                   final_norm=final_norm)
    moments_m = dict(attn_norm=m_attn_norm, w_in=m_w_in, b_forget=m_b_forget, b_gate=m_b_gate,
                     w_proj_attn=m_w_proj_attn, pool_w=m_pool_w, pool_scale=m_pool_scale, w_proj_pool=m_w_proj_pool,
                     conv_w=m_conv_w, w_proj_conv=m_w_proj_conv, w_out=m_w_out, ffn_norm=m_ffn_norm,
                     w_gate_up=m_w_gate_up, w_down=m_w_down, final_norm=m_final_norm)
    moments_v = dict(attn_norm=v_attn_norm, w_in=v_w_in, b_forget=v_b_forget, b_gate=v_b_gate,
                     w_proj_attn=v_w_proj_attn, pool_w=v_pool_w, pool_scale=v_pool_scale, w_proj_pool=v_w_proj_pool,
                     conv_w=v_conv_w, w_proj_conv=v_w_proj_conv, w_out=v_w_out, ffn_norm=v_ffn_norm,
                     w_gate_up=v_w_gate_up, w_down=v_w_down, final_norm=v_final_norm)

    sh = _padded_shards(weights)
    names = list(SHARDED)
    rest = [n for n in names if n != "w_in"]
    me = 4 * lax.axis_index("x") + 2 * lax.axis_index("y") + lax.axis_index("c")
    w_in0, conv_all = _multi_gather([sh["w_in"], conv_w], [0, None], "gather_w_in_l0")
    rest_layers, all_layers = [0] * len(rest), [1] * len(names)
    rest_started = _split_start([sh[n] for n in rest], rest_layers, False, w_in0, "gather_start_rest_l0")
    l1_started = _split_start([sh[n] for n in names], all_layers, False, rest_started[4], "gather_start_l1")

    def get_W(l, xt):
        if l == 0:
            W = _full_operands({"w_in": w_in0}, 0)

            def late(after):
                lands = _split_wait(rest_started, rest_layers, False, after, "gather_wait_rest_l0")
                return _full_operands({n: _with_own(land, sh[n][0]) for n, land in zip(rest, lands)}, 0)

            W["late"] = late
        else:
            lands = _split_wait(l1_started, all_layers, False, xt, "gather_wait_l1")
            W = _full_operands({n: _with_own(land, sh[n][l]) for n, land in zip(names, lands)}, l)
        W.update(_replicated_operands(weights, l))
        W["conv_w"] = jnp.transpose(conv_all[:, l], (1, 0, 2)).reshape(CONV_K, BRANCH_W)
        if l == 0:
            W["attn_norm"] = W["attn_norm"] + l1_started[4][0, 0]
        return W

    exchanges = []

    def stage(l, group, G, W):
        gnames = GRAD_GROUPS[group]
        slabs = [_grad_slabs(G, n, l) for n in gnames]
        started = _split_start(slabs, None, True, slabs[0], f"exchange_start_{group}_l{l}")
        exchanges.append((l, group, gnames, slabs, started))
        tie = {"ffn": "ffn_norm", "mix": "conv_w", "w_in": "w_f"}[group]
        W = dict(W)
        W[tie] = W[tie] + started[4][0, 0].astype(W[tie].dtype)
        return W

    loss_part, grad_x, grads, g_final = _local_step(x, loss_target, get_W, final_norm, stage)
    after = grad_x
    for l, group, gnames, slabs, started in exchanges:
        lands = _split_wait(started, None, True, after, f"exchange_wait_{group}_l{l}")
        recv = {n: _with_own(land, lax.dynamic_index_in_dim(s, me, 0, keepdims=False))
                for n, land, s in zip(gnames, lands, slabs)}
        grads[l].update(_sum_layer_grads(recv, l))
    gw = {n: jnp.stack([grads[l][n] for l in range(DEPTH)]) for n in SHARDED}

    small = ("attn_norm", "b_forget", "b_gate", "pool_w", "pool_scale", "ffn_norm", "conv_w")
    parts = [jnp.stack([grads[l][n] for l in range(DEPTH)]) for n in small] + [g_final, loss_part]
    gathered = _multi_gather(parts, [None] * len(parts), "gather_small_grads")
    summed = _sum_small(gathered, "sum_small_grads")
    for n, s in zip(small, summed):
        gw[n] = s
    gw["attn_norm"], gw["ffn_norm"] = gw["attn_norm"][:, 0], gw["ffn_norm"][:, 0]
    gw["b_forget"] = gw["b_forget"][:, 0, :N_HEADS]
    gw["b_gate"], gw["pool_scale"] = gw["b_gate"][:, 0], gw["pool_scale"][:, 0]
    gw["conv_w"] = lax.dynamic_slice_in_dim(gw["conv_w"], me * (BRANCH_W // N_DEV), BRANCH_W // N_DEV, axis=2)
    gw["final_norm"] = summed[-2][0]
    loss = summed[-1][0, 0]

    deltas, new_m, new_v = {}, {}, {}
    for n in WEIGHT_ORDER:
        shape = weights[n].shape
        d, nm, nv = _adamw(_as_2d(weights[n]), _as_2d(gw[n]), _as_2d(moments_m[n]), _as_2d(moments_v[n]),
                           "adamw_" + n)
        deltas[n], new_m[n], new_v[n] = d.reshape(shape), nm.reshape(shape), nv.reshape(shape)

    return (loss, grad_x, *[gw[n] for n in WEIGHT_ORDER], *[deltas[n] for n in WEIGHT_ORDER],
            *[new_m[n] for n in WEIGHT_ORDER], *[new_v[n] for n in WEIGHT_ORDER])
```

```python
import functools

import numpy as np
import jax
import jax.numpy as jnp
from jax import lax
from jax.experimental import pallas as pl
from jax.experimental.pallas import tpu as pltpu

F32 = jnp.float32
BF16 = jnp.bfloat16

N_DEV = 8
D_MODEL = 1024
DEPTH = 2
N_HEADS = 8
HEAD_DIM = 64
BRANCH_W = 512
POOL_WINDOWS = (2, 4, 8, 16)
POOL_GD = 128
CONV_K = 3
FFN_HIDDEN = 2816
GATE_W = 3 * D_MODEL
IN_COLS = 6664
MAIN_COLS = GATE_W + 7 * BRANCH_W
RMS_EPS = 1e-6
NEG_INF = -1e30

ADAM_LR = 0.001
ADAM_B1 = 0.9
ADAM_B2 = 0.999
ADAM_EPS = 1e-08
ADAM_WD = 0.01
ADAM_STEP = 10

LANES = 128
VMEM_LIMIT = 56 * 1024 * 1024
ATT_BLK = 256
CUM_BLK = 256

OFF_G, OFF_Q, OFF_K, OFF_V, OFF_U, OFF_CV, OFF_CB, OFF_CC = (
    0, 3072, 3584, 4096, 4608, 5120, 5632, 6144)


def _cp(sem=None):
    return pltpu.CompilerParams(dimension_semantics=sem, vmem_limit_bytes=VMEM_LIMIT)


def _sigmoid(z):
    return 1.0 / (1.0 + jnp.exp(-z))


def _matmul(a, b, *, mode, out_dtype, name, tm=2048, tn=512, tk=None, residual=None):
    if mode == "nn":
        (M, K), N = a.shape, b.shape[1]
    elif mode == "nt":
        (M, K), N = a.shape, b.shape[0]
    else:
        (K, M), N = a.shape, b.shape[1]
    tm, tn, tk = min(tm, M), min(tn, N), K if tk is None else min(tk, K)
    assert M % tm == 0 and N % tn == 0 and K % tk == 0, (name, M, N, K, tm, tn, tk)
    nk = K // tk
    if mode == "nn":
        a_spec = pl.BlockSpec((tm, tk), lambda i, j, k: (i, k))
        b_spec = pl.BlockSpec((tk, tn), lambda i, j, k: (k, j))
        dims = (((1,), (0,)), ((), ()))
    elif mode == "nt":
        a_spec = pl.BlockSpec((tm, tk), lambda i, j, k: (i, k))
        b_spec = pl.BlockSpec((tn, tk), lambda i, j, k: (j, k))
        dims = (((1,), (1,)), ((), ()))
    else:
        a_spec = pl.BlockSpec((tk, tm), lambda i, j, k: (k, i))
        b_spec = pl.BlockSpec((tk, tn), lambda i, j, k: (k, j))
        dims = (((0,), (0,)), ((), ()))
    o_spec = pl.BlockSpec((tm, tn), lambda i, j, k: (i, j))
    has_res = residual is not None

    def body(*refs):
        a_ref, b_ref = refs[:2]
        r_ref = refs[2] if has_res else None
        o_ref = refs[2 + has_res]

        def finish(acc):
            if has_res:
                acc = acc + r_ref[...].astype(F32)
            o_ref[...] = acc.astype(out_dtype)

        prod = lax.dot_general(a_ref[...], b_ref[...], dims, preferred_element_type=F32)
        if nk == 1:
            finish(prod)
            return
        acc_ref = refs[-1]
        k = pl.program_id(2)

        @pl.when(k == 0)
        def _():
            acc_ref[...] = prod

        @pl.when(jnp.logical_and(k > 0, k < nk - 1))
        def _():
            acc_ref[...] += prod

        @pl.when(k == nk - 1)
        def _():
            finish(acc_ref[...] + prod)

    in_specs = [a_spec, b_spec] + ([o_spec] if has_res else [])
    args = (a, b) + ((residual,) if has_res else ())
    return pl.pallas_call(
        body, name=name, grid=(M // tm, N // tn, nk), in_specs=in_specs, out_specs=o_spec,
        out_shape=jax.ShapeDtypeStruct((M, N), out_dtype),
        scratch_shapes=[pltpu.VMEM((tm, tn), F32)] if nk > 1 else [],
        compiler_params=_cp(("parallel", "parallel", "arbitrary")),
    )(*args)


def _rms_fwd(x, g, name):
    T, Dm = x.shape
    tm = min(512, T)

    def body(x_ref, g_ref, h_ref):
        xf = x_ref[...]
        r = lax.rsqrt(jnp.mean(xf * xf, axis=-1, keepdims=True) + RMS_EPS)
        h_ref[...] = ((xf * r) * g_ref[...]).astype(BF16)

    return pl.pallas_call(
        body, name=name, grid=(T // tm,),
        in_specs=[pl.BlockSpec((tm, Dm), lambda i: (i, 0)), pl.BlockSpec((1, Dm), lambda i: (0, 0))],
        out_specs=pl.BlockSpec((tm, Dm), lambda i: (i, 0)),
        out_shape=jax.ShapeDtypeStruct((T, Dm), BF16),
        compiler_params=_cp(("parallel",)),
    )(x, g.reshape(1, Dm))


def _rms_bwd(x, g, dh, dres, name):
    T, Dm = x.shape
    tm = min(512, T)

    def body(x_ref, g_ref, dh_ref, dres_ref, dx_ref, dxb_ref, dg_ref):
        i = pl.program_id(0)
        xf = x_ref[...]
        r = lax.rsqrt(jnp.mean(xf * xf, axis=-1, keepdims=True) + RMS_EPS)
        xn = xf * r
        dhf = dh_ref[...].astype(F32)
        dxn = dhf * g_ref[...]
        c = jnp.mean(dxn * xn, axis=-1, keepdims=True)
        dx = dres_ref[...] + r * (dxn - xn * c)
        dx_ref[...] = dx
        dxb_ref[...] = dx.astype(BF16)
        part = jnp.sum(dhf * xn, axis=0, keepdims=True)

        @pl.when(i == 0)
        def _():
            dg_ref[...] = part

        @pl.when(i > 0)
        def _():
            dg_ref[...] += part

    row = pl.BlockSpec((tm, Dm), lambda i: (i, 0))
    vec = pl.BlockSpec((1, Dm), lambda i: (0, 0))
    return pl.pallas_call(
        body, name=name, grid=(T // tm,), in_specs=[row, vec, row, row], out_specs=[row, row, vec],
        out_shape=[jax.ShapeDtypeStruct((T, Dm), F32), jax.ShapeDtypeStruct((T, Dm), BF16),
                   jax.ShapeDtypeStruct((1, Dm), F32)],
        compiler_params=_cp(("arbitrary",)),
    )(x, g.reshape(1, Dm), dh, dres)


def _loss_head(x, g, target, name):
    T, Dm = x.shape
    tm = min(512, T)

    def body(x_ref, g_ref, t_ref, loss_ref, dx_ref, dxb_ref, dg_ref):
        i = pl.program_id(0)
        xf = x_ref[...]
        gv = g_ref[...]
        r = lax.rsqrt(jnp.mean(xf * xf, axis=-1, keepdims=True) + RMS_EPS)
        xn = xf * r
        diff = xn * gv - t_ref[...]
        per_tok = jnp.mean(diff * diff, axis=-1, keepdims=True)
        lpart = 0.5 * jnp.sum(per_tok, axis=0, keepdims=True) + jnp.zeros((1, LANES), F32)
        dy = diff * (1.0 / Dm)
        dxn = dy * gv
        c = jnp.mean(dxn * xn, axis=-1, keepdims=True)
        dx = r * (dxn - xn * c)
        dx_ref[...] = dx
        dxb_ref[...] = dx.astype(BF16)
        part = jnp.sum(dy * xn, axis=0, keepdims=True)

        @pl.when(i == 0)
        def _():
            dg_ref[...] = part
            loss_ref[...] = lpart

        @pl.when(i > 0)
        def _():
            dg_ref[...] += part
            loss_ref[...] += lpart

    row = pl.BlockSpec((tm, Dm), lambda i: (i, 0))
    vec = pl.BlockSpec((1, Dm), lambda i: (0, 0))
    lsp = pl.BlockSpec((1, LANES), lambda i: (0, 0))
    return pl.pallas_call(
        body, name=name, grid=(T // tm,), in_specs=[row, vec, row], out_specs=[lsp, row, row, vec],
        out_shape=[jax.ShapeDtypeStruct((1, LANES), F32), jax.ShapeDtypeStruct((T, Dm), F32),
                   jax.ShapeDtypeStruct((T, Dm), BF16), jax.ShapeDtypeStruct((1, Dm), F32)],
        compiler_params=_cp(("arbitrary",)),
    )(x, g.reshape(1, Dm), target)


def _split_bf16(v):
    hi = v.astype(BF16)
    r1 = v - hi.astype(F32)
    mid = r1.astype(BF16)
    lo = (r1 - mid.astype(F32)).astype(BF16)
    return hi, mid, lo


def _tri_dot(tri, v):
    hi, mid, lo = _split_bf16(v)
    dot = functools.partial(jnp.dot, preferred_element_type=F32)
    return dot(tri, hi) + dot(tri, mid) + dot(tri, lo)


def _log_sigmoid(z):
    return jnp.minimum(z, 0.0) - jnp.log(1.0 + jnp.exp(-jnp.abs(z)))


def _fox_cumsum_fwd(f, bf, n_seq, name):
    T = f.shape[0]
    S = T // n_seq
    c = min(CUM_BLK, S)

    def body(f_ref, b_ref, out_ref):
        ri = lax.broadcasted_iota(jnp.int32, (c, c), 0)
        ci = lax.broadcasted_iota(jnp.int32, (c, c), 1)
        tri = (ri >= ci).astype(BF16)
        carry = jnp.zeros((1, LANES), F32)
        for j in range(S // c):
            lf = _log_sigmoid(f_ref[j * c:(j + 1) * c, :] + b_ref[...])
            out_ref[j * c:(j + 1) * c, :] = _tri_dot(tri, lf) + carry
            carry = carry + jnp.sum(lf, axis=0, keepdims=True)

    blk = pl.BlockSpec((S, LANES), lambda b: (b, 0))
    return pl.pallas_call(
        body, name=name, grid=(n_seq,), in_specs=[blk, pl.BlockSpec((1, LANES), lambda b: (0, 0))],
        out_specs=blk, out_shape=jax.ShapeDtypeStruct((T, LANES), F32),
        compiler_params=_cp(("parallel",)),
    )(f, bf)


def _fox_cumsum_bwd(f, bf, dF, n_seq, name):
    T = f.shape[0]
    S = T // n_seq
    c = min(CUM_BLK, S)

    def body(f_ref, b_ref, dF_ref, df_ref, db_ref):
        b = pl.program_id(0)
        ri = lax.broadcasted_iota(jnp.int32, (c, c), 0)
        ci = lax.broadcasted_iota(jnp.int32, (c, c), 1)
        tri = (ri <= ci).astype(BF16)
        carry = jnp.zeros((1, LANES), F32)
        dbp = jnp.zeros((1, LANES), F32)
        for j in reversed(range(S // c)):
            dFc = dF_ref[j * c:(j + 1) * c, :]
            dlf = _tri_dot(tri, dFc) + carry
            carry = carry + jnp.sum(dFc, axis=0, keepdims=True)
            z = f_ref[j * c:(j + 1) * c, :] + b_ref[...]
            dz = dlf * _sigmoid(-z)
            df_ref[j * c:(j + 1) * c, :] = dz.astype(BF16)
            dbp = dbp + jnp.sum(dz, axis=0, keepdims=True)

        @pl.when(b == 0)
        def _():
            db_ref[...] = dbp

        @pl.when(b > 0)
        def _():
            db_ref[...] += dbp

    blk = pl.BlockSpec((S, LANES), lambda b: (b, 0))
    vec = pl.BlockSpec((1, LANES), lambda b: (0, 0))
    return pl.pallas_call(
        body, name=name, grid=(n_seq,), in_specs=[blk, vec, blk], out_specs=[blk, vec],
        out_shape=[jax.ShapeDtypeStruct((T, LANES), BF16), jax.ShapeDtypeStruct((1, LANES), F32)],
        compiler_params=_cp(("arbitrary",)),
    )(f, bf, dF)


def _pair_masks():
    lane = lax.broadcasted_iota(jnp.int32, (1, LANES), 1)
    lo = lane < HEAD_DIM
    return lo, jnp.logical_not(lo)


def _attn_logits(q, k, fq, fk, sel, mask, scale):
    qm = jnp.where(sel, q, jnp.zeros_like(q))
    s = lax.dot_general(qm, k, (((1,), (1,)), ((), ())), preferred_element_type=F32) * scale
    s = s + fq - fk
    return jnp.where(mask, s, NEG_INF)


def _causal_mask(qi, ki, blk):
    row = qi * blk + lax.broadcasted_iota(jnp.int32, (blk, blk), 0)
    col = ki * blk + lax.broadcasted_iota(jnp.int32, (blk, blk), 1)
    return col <= row


def _attn_fwd(proj, Fq, Fk, n_seq, name):
    T = proj.shape[0]
    S = T // n_seq
    blk = min(ATT_BLK, S)
    nb = S // blk
    scale = HEAD_DIM ** -0.5
    qc, kc, vc = OFF_Q // LANES, OFF_K // LANES, OFF_V // LANES

    def body(q_ref, k_ref, v_ref, fq_ref, fk_ref, o_ref, o32_ref, lse_ref, m_s, l_s, acc_s):
        qi, ki = pl.program_id(2), pl.program_id(3)

        @pl.when(ki == 0)
        def _():
            m_s[...] = jnp.full_like(m_s, NEG_INF)
            l_s[...] = jnp.zeros_like(l_s)
            acc_s[...] = jnp.zeros_like(acc_s)

        @pl.when(ki <= qi)
        def _():
            q, k, v = q_ref[...], k_ref[...], v_ref[...]
            mask = _causal_mask(qi, ki, blk)
            for hh, sel in enumerate(_pair_masks()):
                s = _attn_logits(q, k, fq_ref[hh], fk_ref[hh], sel, mask, scale)
                m_prev = m_s[hh]
                m_new = jnp.maximum(m_prev, jnp.max(s, axis=-1, keepdims=True))
                alpha = jnp.exp(m_prev - m_new)
                p = jnp.exp(s - m_new)
                l_s[hh] = alpha * l_s[hh] + jnp.sum(p, axis=-1, keepdims=True)
                p_hi = p.astype(BF16)
                p_lo = (p - p_hi.astype(F32)).astype(BF16)
                pv = jnp.dot(p_hi, v, preferred_element_type=F32) + jnp.dot(p_lo, v, preferred_element_type=F32)
                acc_s[hh] = alpha * acc_s[hh] + pv
                m_s[hh] = m_new

        @pl.when(ki == qi)
        def _():
            lo, _ = _pair_masks()
            o = jnp.where(lo, acc_s[0] / l_s[0], acc_s[1] / l_s[1])
            o_ref[...] = o.astype(BF16)
            o32_ref[...] = o
            lse_ref[0] = m_s[0] + jnp.log(l_s[0])
            lse_ref[1] = m_s[1] + jnp.log(l_s[1])

    grid = (n_seq, N_HEADS // 2, nb, nb)
    return pl.pallas_call(
        body, name=name, grid=grid,
        in_specs=[
            pl.BlockSpec((blk, LANES), lambda b, j, qi, ki: (b * nb + qi, qc + j)),
            pl.BlockSpec((blk, LANES), lambda b, j, qi, ki: (b * nb + jnp.minimum(ki, qi), kc + j)),
            pl.BlockSpec((blk, LANES), lambda b, j, qi, ki: (b * nb + jnp.minimum(ki, qi), vc + j)),
            pl.BlockSpec((2, blk, 1), lambda b, j, qi, ki: (j, b * nb + qi, 0)),
            pl.BlockSpec((2, 1, blk), lambda b, j, qi, ki: (j, 0, b * nb + jnp.minimum(ki, qi))),
        ],
        out_specs=[
            pl.BlockSpec((blk, LANES), lambda b, j, qi, ki: (b * nb + qi, j)),
            pl.BlockSpec((blk, LANES), lambda b, j, qi, ki: (b * nb + qi, j)),
            pl.BlockSpec((2, blk, 1), lambda b, j, qi, ki: (j, b * nb + qi, 0)),
        ],
        out_shape=[jax.ShapeDtypeStruct((T, BRANCH_W), BF16), jax.ShapeDtypeStruct((T, BRANCH_W), F32),
                   jax.ShapeDtypeStruct((N_HEADS, T, 1), F32)],
        scratch_shapes=[pltpu.VMEM((2, blk, 1), F32), pltpu.VMEM((2, blk, 1), F32),
                        pltpu.VMEM((2, blk, LANES), F32)],
        compiler_params=_cp(("parallel", "parallel", "parallel", "arbitrary")),
    )(proj, proj, proj, Fq, Fk)


def _attn_delta(do, o, name):
    T = do.shape[0]
    tm = min(512, T)

    def body(do_ref, o_ref, d_ref):
        prod = do_ref[...].astype(F32) * o_ref[...].astype(F32)
        lo, hi = _pair_masks()
        for j in range(N_HEADS // 2):
            pj = prod[:, j * LANES:(j + 1) * LANES]
            d_ref[2 * j] = jnp.sum(jnp.where(lo, pj, 0.0), axis=-1, keepdims=True)
            d_ref[2 * j + 1] = jnp.sum(jnp.where(hi, pj, 0.0), axis=-1, keepdims=True)

    row = pl.BlockSpec((tm, BRANCH_W), lambda i: (i, 0))
    return pl.pallas_call(
        body, name=name, grid=(T // tm,), in_specs=[row, row],
        out_specs=pl.BlockSpec((N_HEADS, tm, 1), lambda i: (0, i, 0)),
        out_shape=jax.ShapeDtypeStruct((N_HEADS, T, 1), F32),
        compiler_params=_cp(("parallel",)),
    )(do, o)


def _attn_bwd_dq(proj, do, lse, delta, Fq, Fk, n_seq, name):
    T = proj.shape[0]
    S = T // n_seq
    blk = min(ATT_BLK, S)
    nb = S // blk
    scale = HEAD_DIM ** -0.5
    qc, kc, vc = OFF_Q // LANES, OFF_K // LANES, OFF_V // LANES

    def body(q_ref, k_ref, v_ref, do_ref, lse_ref, dl_ref, fq_ref, fk_ref, dq_ref, acc_s):
        qi, ki = pl.program_id(2), pl.program_id(3)

        @pl.when(ki == 0)
        def _():
            acc_s[...] = jnp.zeros_like(acc_s)

        @pl.when(ki <= qi)
        def _():
            q, k, v, do_ = q_ref[...], k_ref[...], v_ref[...], do_ref[...]
            mask = _causal_mask(qi, ki, blk)
            for hh, sel in enumerate(_pair_masks()):
                s = _attn_logits(q, k, fq_ref[hh], fk_ref[hh], sel, mask, scale)
                p = jnp.exp(s - lse_ref[hh])
                dom = jnp.where(sel, do_, jnp.zeros_like(do_))
                dp = lax.dot_general(dom, v, (((1,), (1,)), ((), ())), preferred_element_type=F32)
                ds = p * (dp - dl_ref[hh])
                acc_s[hh] += jnp.dot(ds.astype(BF16), k, preferred_element_type=F32)

        @pl.when(ki == qi)
        def _():
            lo, _ = _pair_masks()
            dq_ref[...] = (jnp.where(lo, acc_s[0], acc_s[1]) * scale).astype(BF16)

    qmap = lambda b, j, qi, ki: (b * nb + qi, j)
    col1 = pl.BlockSpec((2, blk, 1), lambda b, j, qi, ki: (j, b * nb + qi, 0))
    return pl.pallas_call(
        body, name=name, grid=(n_seq, N_HEADS // 2, nb, nb),
        in_specs=[
            pl.BlockSpec((blk, LANES), lambda b, j, qi, ki: (b * nb + qi, qc + j)),
            pl.BlockSpec((blk, LANES), lambda b, j, qi, ki: (b * nb + jnp.minimum(ki, qi), kc + j)),
            pl.BlockSpec((blk, LANES), lambda b, j, qi, ki: (b * nb + jnp.minimum(ki, qi), vc + j)),
            pl.BlockSpec((blk, LANES), qmap),
            col1, col1, col1,
            pl.BlockSpec((2, 1, blk), lambda b, j, qi, ki: (j, 0, b * nb + jnp.minimum(ki, qi))),
        ],
        out_specs=pl.BlockSpec((blk, LANES), qmap),
        out_shape=jax.ShapeDtypeStruct((T, BRANCH_W), BF16),
        scratch_shapes=[pltpu.VMEM((2, blk, LANES), F32)],
        compiler_params=_cp(("parallel", "parallel", "parallel", "arbitrary")),
    )(proj, proj, proj, do, lse, delta, Fq, Fk)


def _attn_bwd_dkv(proj, do, lse, delta, Fq, Fk, n_seq, name):
    T = proj.shape[0]
    S = T // n_seq
    blk = min(ATT_BLK, S)
    nb = S // blk
    scale = HEAD_DIM ** -0.5
    qc, kc, vc = OFF_Q // LANES, OFF_K // LANES, OFF_V // LANES
    tdot = functools.partial(lax.dot_general, dimension_numbers=(((0,), (0,)), ((), ())),
                             preferred_element_type=F32)

    def body(q_ref, k_ref, v_ref, do_ref, lse_ref, dl_ref, fq_ref, fk_ref, dk_ref, dv_ref, dfk_ref,
             dk_s, dv_s, df_s):
        ki, qi = pl.program_id(2), pl.program_id(3)

        @pl.when(qi == 0)
        def _():
            dk_s[...] = jnp.zeros_like(dk_s)
            dv_s[...] = jnp.zeros_like(dv_s)
            df_s[...] = jnp.zeros_like(df_s)

        @pl.when(qi >= ki)
        def _():
            q, k, v, do_ = q_ref[...], k_ref[...], v_ref[...], do_ref[...]
            mask = _causal_mask(qi, ki, blk)
            for hh, sel in enumerate(_pair_masks()):
                s = _attn_logits(q, k, fq_ref[hh], fk_ref[hh], sel, mask, scale)
                p = jnp.exp(s - lse_ref[hh])
                dv_s[hh] += tdot(p.astype(BF16), do_)
                dom = jnp.where(sel, do_, jnp.zeros_like(do_))
                dp = lax.dot_general(dom, v, (((1,), (1,)), ((), ())), preferred_element_type=F32)
                ds = p * (dp - dl_ref[hh])
                dk_s[hh] += tdot(ds.astype(BF16), q)
                df_s[hh] -= jnp.sum(ds, axis=0, keepdims=True)

        @pl.when(qi == nb - 1)
        def _():
            lo, _ = _pair_masks()
            dk_ref[...] = (jnp.where(lo, dk_s[0], dk_s[1]) * scale).astype(BF16)
            dv_ref[...] = jnp.where(lo, dv_s[0], dv_s[1]).astype(BF16)
            dfk_ref[...] = df_s[...]

    kmap = lambda b, j, ki, qi: (b * nb + ki, j)
    col1 = pl.BlockSpec((2, blk, 1), lambda b, j, ki, qi: (j, b * nb + jnp.maximum(qi, ki), 0))
    rowk = pl.BlockSpec((2, 1, blk), lambda b, j, ki, qi: (j, 0, b * nb + ki))
    return pl.pallas_call(
        body, name=name, grid=(n_seq, N_HEADS // 2, nb, nb),
        in_specs=[
            pl.BlockSpec((blk, LANES), lambda b, j, ki, qi: (b * nb + jnp.maximum(qi, ki), qc + j)),
            pl.BlockSpec((blk, LANES), lambda b, j, ki, qi: (b * nb + ki, kc + j)),
            pl.BlockSpec((blk, LANES), lambda b, j, ki, qi: (b * nb + ki, vc + j)),
            pl.BlockSpec((blk, LANES), lambda b, j, ki, qi: (b * nb + jnp.maximum(qi, ki), j)),
            col1, col1, col1, rowk,
        ],
        out_specs=[pl.BlockSpec((blk, LANES), kmap), pl.BlockSpec((blk, LANES), kmap), rowk],
        out_shape=[jax.ShapeDtypeStruct((T, BRANCH_W), BF16), jax.ShapeDtypeStruct((T, BRANCH_W), BF16),
                   jax.ShapeDtypeStruct((N_HEADS, 1, T), F32)],
        scratch_shapes=[pltpu.VMEM((2, blk, LANES), F32), pltpu.VMEM((2, blk, LANES), F32),
                        pltpu.VMEM((2, 1, blk), F32)],
        compiler_params=_cp(("parallel", "parallel", "parallel", "arbitrary")),
    )(proj, proj, proj, do, lse, delta, Fq, Fk)


AUG0 = HEAD_DIM
Q_TILE, K_CHUNK, ROW_GROUP = 512, 256, 64


def _fox_prep(f, bf, proj, n_seq, name):
    T = f.shape[0]
    S = T // n_seq
    c = min(CUM_BLK, S)

    def body(f_ref, b_ref, q_ref, k_ref, v_ref, qa_ref, ka_ref, va_ref):
        ri = lax.broadcasted_iota(jnp.int32, (c, c), 0)
        ci = lax.broadcasted_iota(jnp.int32, (c, c), 1)
        tri = (ri >= ci).astype(BF16)
        lane = lax.broadcasted_iota(jnp.int32, (c, LANES), 1)
        carry = jnp.zeros((1, LANES), F32)
        for j in range(S // c):
            rows = slice(j * c, (j + 1) * c)
            lf = _log_sigmoid(f_ref[rows, :] + b_ref[...])
            Fc = _tri_dot(tri, lf) + carry
            carry = carry + jnp.sum(lf, axis=0, keepdims=True)
            for h in range(N_HEADS):
                col = jnp.sum(jnp.where(lane == h, Fc, 0.0), axis=-1, keepdims=True)
                hi = col.astype(BF16).astype(F32)
                r1 = col - hi
                mid = r1.astype(BF16).astype(F32)
                lo = r1 - mid
                ones_q = jnp.logical_and(lane >= AUG0 + 3, lane < AUG0 + 6)
                ones_k = jnp.logical_and(lane >= AUG0, lane < AUG0 + 3)
                aug_q = jnp.where(lane == AUG0, hi, jnp.where(lane == AUG0 + 1, mid, jnp.where(
                    lane == AUG0 + 2, lo, jnp.where(ones_q, 1.0, 0.0))))
                aug_k = jnp.where(lane == AUG0 + 3, -hi, jnp.where(lane == AUG0 + 4, -mid, jnp.where(
                    lane == AUG0 + 5, -lo, jnp.where(ones_k, 1.0, 0.0))))
                pair = slice((h // 2) * LANES, (h // 2 + 1) * LANES)
                qp, kp = q_ref[rows, pair].astype(F32), k_ref[rows, pair].astype(F32)
                vp = v_ref[rows, pair].astype(F32)
                if h % 2:
                    qp, kp, vp = (pltpu.roll(a, HEAD_DIM, 1) for a in (qp, kp, vp))
                out = slice(h * LANES, (h + 1) * LANES)
                qa_ref[rows, out] = jnp.where(lane < HEAD_DIM, qp * (HEAD_DIM ** -0.5), aug_q).astype(BF16)
                ka_ref[rows, out] = jnp.where(lane < HEAD_DIM, kp, aug_k).astype(BF16)
                va_ref[rows, out] = jnp.where(lane < HEAD_DIM, vp, jnp.where(lane == AUG0, 1.0, 0.0)).astype(BF16)

    fblk = pl.BlockSpec((S, LANES), lambda b: (b, 0))
    out = pl.BlockSpec((S, N_HEADS * LANES), lambda b: (b, 0))
    sh = jax.ShapeDtypeStruct((T, N_HEADS * LANES), BF16)
    return pl.pallas_call(
        body, name=name, grid=(n_seq,),
        in_specs=[fblk, pl.BlockSpec((1, LANES), lambda b: (0, 0)),
                  pl.BlockSpec((S, BRANCH_W), lambda b: (b, OFF_Q // BRANCH_W)),
                  pl.BlockSpec((S, BRANCH_W), lambda b: (b, OFF_K // BRANCH_W)),
                  pl.BlockSpec((S, BRANCH_W), lambda b: (b, OFF_V // BRANCH_W))],
        out_specs=[out, out, out], out_shape=[sh, sh, sh],
        compiler_params=_cp(("parallel",)),
    )(f, bf, proj, proj, proj)


def _band_mask(q0, k0, nq, nk):
    row = q0 + lax.broadcasted_iota(jnp.int32, (nq, nk), 0)
    col = k0 + lax.broadcasted_iota(jnp.int32, (nq, nk), 1)
    return col <= row


_NT = (((1,), (1,)), ((), ()))
_TN = (((0,), (0,)), ((), ()))


def _attn_fwd2(qa, ka, va, n_seq, name):
    T = qa.shape[0]
    S = T // n_seq
    tq, tk, rg = min(Q_TILE, S), min(K_CHUNK, S), ROW_GROUP
    nq, per = S // tq, tq // tk

    def body(q_ref, k_ref, v_ref, o_ref, o32_ref, lse_ref, phi_s, plo_s, mp_s, m_s, acc_s):
        qi = pl.program_id(2)
        mp_s[...] = jnp.full_like(mp_s, NEG_INF)
        acc_s[...] = jnp.zeros_like(acc_s)

        def scores(kc, hh):
            k0 = pl.multiple_of(kc * tk, tk)
            hl = slice(hh * LANES, (hh + 1) * LANES)
            return k0, lax.dot_general(q_ref[:, hl], k_ref[pl.ds(k0, tk), hl], _NT, preferred_element_type=F32)

        def max_chunk(kc, masked):
            for hh in range(2):
                k0, s_all = scores(kc, hh)
                for r in range(tq // rg):
                    rows = slice(r * rg, (r + 1) * rg)
                    s = s_all[rows, :]
                    if masked:
                        s = jnp.where(_band_mask(qi * tq + r * rg, k0, rg, tk), s, NEG_INF)
                    part = s[:, :LANES]
                    for c in range(1, tk // LANES):
                        part = jnp.maximum(part, s[:, c * LANES:(c + 1) * LANES])
                    mp_s[hh, rows, :] = jnp.maximum(mp_s[hh, rows, :], part)

        def sum_chunk(kc, masked):
            for hh in range(2):
                k0, s_all = scores(kc, hh)
                hl = slice(hh * LANES, (hh + 1) * LANES)
                v = v_ref[pl.ds(k0, tk), hl]
                for r in range(tq // rg):
                    rows = slice(r * rg, (r + 1) * rg)
                    p = jnp.exp(s_all[rows, :] - m_s[hh, rows])
                    if masked:
                        p = jnp.where(_band_mask(qi * tq + r * rg, k0, rg, tk), p, 0.0)
                    p_hi = p.astype(BF16)
                    phi_s[hh, rows, :] = p_hi
                    plo_s[hh, rows, :] = (p - p_hi.astype(F32)).astype(BF16)
                acc_s[hh] += (jnp.dot(phi_s[hh], v, preferred_element_type=F32)
                              + jnp.dot(plo_s[hh], v, preferred_element_type=F32))

        def sweep(chunk):
            def unmasked(kc, carry):
                chunk(kc, False)
                return carry

            lax.fori_loop(0, qi * per, unmasked, 0)
            for d in range(per):
                chunk(qi * per + d, True)

        sweep(max_chunk)
        m_s[...] = jnp.max(mp_s[...], axis=-1, keepdims=True)
        sweep(sum_chunk)

        lane = lax.broadcasted_iota(jnp.int32, (1, LANES), 1)
        outs = []
        for hh in range(2):
            acc = acc_s[hh]
            l = jnp.sum(jnp.where(lane == AUG0, acc, 0.0), axis=-1, keepdims=True)
            lse_ref[hh] = m_s[hh] + jnp.log(l)
            outs.append(acc / l)
        o = jnp.where(lane < HEAD_DIM, outs[0], pltpu.roll(outs[1], HEAD_DIM, 1))
        o_ref[...] = o.astype(BF16)
        o32_ref[...] = o

    qmap = lambda b, j, qi: (b * nq + qi, j)
    omap = lambda b, j, qi: (b * nq + qi, j)
    kv = pl.BlockSpec((S, 2 * LANES), lambda b, j, qi: (b, j))
    return pl.pallas_call(
        body, name=name, grid=(n_seq, N_HEADS // 2, nq),
        in_specs=[pl.BlockSpec((tq, 2 * LANES), qmap), kv, kv],
        out_specs=[pl.BlockSpec((tq, LANES), omap), pl.BlockSpec((tq, LANES), omap),
                   pl.BlockSpec((2, tq, 1), lambda b, j, qi: (j, b * nq + qi, 0))],
        out_shape=[jax.ShapeDtypeStruct((T, BRANCH_W), BF16), jax.ShapeDtypeStruct((T, BRANCH_W), F32),
                   jax.ShapeDtypeStruct((N_HEADS, T, 1), F32)],
        scratch_shapes=[pltpu.VMEM((2, tq, tk), BF16), pltpu.VMEM((2, tq, tk), BF16),
                        pltpu.VMEM((2, tq, LANES), F32), pltpu.VMEM((2, tq, 1), F32),
                        pltpu.VMEM((2, tq, LANES), F32)],
        compiler_params=_cp(("parallel", "parallel", "parallel")),
    )(qa, ka, va)


def _attn_bwd_dq2(qa, ka, proj, do, lse, delta, n_seq, name):
    T = qa.shape[0]
    S = T // n_seq
    tq, tk, rg = min(Q_TILE, S), min(K_CHUNK, S), ROW_GROUP
    nq, per = S // tq, tq // tk
    vc = OFF_V // LANES

    def body(q_ref, k_ref, v_ref, do_ref, lse_ref, dl_ref, dq_ref, ds_s, acc_s):
        qi = pl.program_id(2)
        acc_s[...] = jnp.zeros_like(acc_s)
        sels = _pair_masks()

        def chunk(kc, masked):
            k0 = pl.multiple_of(kc * tk, tk)
            v = v_ref[pl.ds(k0, tk), :]
            for hh in range(2):
                hl = slice(hh * LANES, (hh + 1) * LANES)
                kh = k_ref[pl.ds(k0, tk), hl]
                s_all = lax.dot_general(q_ref[:, hl], kh, _NT, preferred_element_type=F32)
                dom = jnp.where(sels[hh], do_ref[...], jnp.zeros_like(do_ref[...]))
                dp_all = lax.dot_general(dom, v, _NT, preferred_element_type=F32)
                for r in range(tq // rg):
                    rows = slice(r * rg, (r + 1) * rg)
                    p = jnp.exp(s_all[rows, :] - lse_ref[hh, rows])
                    if masked:
                        p = jnp.where(_band_mask(qi * tq + r * rg, k0, rg, tk), p, 0.0)
                    ds_s[hh, rows, :] = (p * (dp_all[rows, :] - dl_ref[hh, rows])).astype(BF16)
                acc_s[hh] += jnp.dot(ds_s[hh], kh, preferred_element_type=F32)

        def unmasked(kc, carry):
            chunk(kc, False)
            return carry

        lax.fori_loop(0, qi * per, unmasked, 0)
        for d in range(per):
            chunk(qi * per + d, True)
        dq = jnp.where(sels[0], acc_s[0], pltpu.roll(acc_s[1], HEAD_DIM, 1))
        dq_ref[...] = (dq * (HEAD_DIM ** -0.5)).astype(BF16)

    qmap = lambda b, j, qi: (b * nq + qi, j)
    col1 = pl.BlockSpec((2, tq, 1), lambda b, j, qi: (j, b * nq + qi, 0))
    return pl.pallas_call(
        body, name=name, grid=(n_seq, N_HEADS // 2, nq),
        in_specs=[pl.BlockSpec((tq, 2 * LANES), qmap),
                  pl.BlockSpec((S, 2 * LANES), lambda b, j, qi: (b, j)),
                  pl.BlockSpec((S, LANES), lambda b, j, qi: (b, vc + j)),
                  pl.BlockSpec((tq, LANES), qmap), col1, col1],
        out_specs=pl.BlockSpec((tq, LANES), qmap),
        out_shape=jax.ShapeDtypeStruct((T, BRANCH_W), BF16),
        scratch_shapes=[pltpu.VMEM((2, tq, tk), BF16), pltpu.VMEM((2, tq, LANES), F32)],
        compiler_params=_cp(("parallel", "parallel", "parallel")),
    )(qa, ka, proj, do, lse, delta)


def _attn_bwd_dkv2(qa, ka, proj, do, lse, delta, n_seq, name):
    T = qa.shape[0]
    S = T // n_seq
    tkt, tqc, rg = min(Q_TILE, S), min(K_CHUNK, S), ROW_GROUP // 2
    nk, per, nqc = S // tkt, tkt // tqc, S // tqc
    vc = OFF_V // LANES

    def body(q_ref, k_ref, v_ref, do_ref, lse_ref, dl_ref, dk_ref, dv_ref, dfk_ref,
             p_s, ds_s, dk_s, dv_s, df_s):
        ki = pl.program_id(2)
        dk_s[...] = jnp.zeros_like(dk_s)
        dv_s[...] = jnp.zeros_like(dv_s)
        df_s[...] = jnp.zeros_like(df_s)
        sels = _pair_masks()
        v = v_ref[...]

        def chunk(qc, masked):
            q0 = pl.multiple_of(qc * tqc, tqc)
            do_ = do_ref[pl.ds(q0, tqc), :]
            for hh in range(2):
                hl = slice(hh * LANES, (hh + 1) * LANES)
                qh = q_ref[pl.ds(q0, tqc), hl]
                s_all = lax.dot_general(qh, k_ref[:, hl], _NT, preferred_element_type=F32)
                dom = jnp.where(sels[hh], do_, jnp.zeros_like(do_))
                dp_all = lax.dot_general(dom, v, _NT, preferred_element_type=F32)
                dfp = jnp.zeros((1, tkt), F32)
                for r in range(tqc // rg):
                    rows = slice(r * rg, (r + 1) * rg)
                    qrows = pl.ds(q0 + r * rg, rg)
                    p = jnp.exp(s_all[rows, :] - lse_ref[hh, qrows])
                    if masked:
                        p = jnp.where(_band_mask(q0 + r * rg, ki * tkt, rg, tkt), p, 0.0)
                    ds = p * (dp_all[rows, :] - dl_ref[hh, qrows])
                    p_s[hh, rows, :] = p.astype(BF16)
                    ds_s[hh, rows, :] = ds.astype(BF16)
                    dfp = dfp + jnp.sum(ds, axis=0, keepdims=True)
                df_s[hh] -= dfp
                dv_s[hh] += lax.dot_general(p_s[hh], do_, _TN, preferred_element_type=F32)
                dk_s[hh] += lax.dot_general(ds_s[hh], qh, _TN, preferred_element_type=F32)

        for d in range(per):
            chunk(ki * per + d, True)

        def unmasked(qc, carry):
            chunk(qc, False)
            return carry

        lax.fori_loop((ki + 1) * per, nqc, unmasked, 0)
        dk_ref[...] = jnp.where(sels[0], dk_s[0], pltpu.roll(dk_s[1], HEAD_DIM, 1)).astype(BF16)
        dv_ref[...] = jnp.where(sels[0], dv_s[0], dv_s[1]).astype(BF16)
        dfk_ref[...] = df_s[...]

    kmap = lambda b, j, ki: (b * nk + ki, j)
    col1 = pl.BlockSpec((2, S, 1), lambda b, j, ki: (j, b, 0))
    rowk = pl.BlockSpec((2, 1, tkt), lambda b, j, ki: (j, 0, b * nk + ki))
    return pl.pallas_call(
        body, name=name, grid=(n_seq, N_HEADS // 2, nk),
        in_specs=[pl.BlockSpec((S, 2 * LANES), lambda b, j, ki: (b, j)),
                  pl.BlockSpec((tkt, 2 * LANES), kmap),
                  pl.BlockSpec((tkt, LANES), lambda b, j, ki: (b * nk + ki, vc + j)),
                  pl.BlockSpec((S, LANES), lambda b, j, ki: (b, j)), col1, col1],
        out_specs=[pl.BlockSpec((tkt, LANES), kmap), pl.BlockSpec((tkt, LANES), kmap), rowk],
        out_shape=[jax.ShapeDtypeStruct((T, BRANCH_W), BF16), jax.ShapeDtypeStruct((T, BRANCH_W), BF16),
                   jax.ShapeDtypeStruct((N_HEADS, 1, T), F32)],
        scratch_shapes=[pltpu.VMEM((2, tqc, tkt), BF16), pltpu.VMEM((2, tqc, tkt), BF16),
                        pltpu.VMEM((2, tkt, LANES), F32),
                        pltpu.VMEM((2, tkt, LANES), F32), pltpu.VMEM((2, 1, tkt), F32)],
        compiler_params=_cp(("parallel", "parallel", "parallel")),
    )(qa, ka, proj, do, lse, delta)


def _shift_down(v, k, row):
    return jnp.where(row >= k, pltpu.roll(v, k, 0), 0.0)


def _shift_up(v, k, row, S):
    return jnp.where(row < S - k, pltpu.roll(v, S - k, 0), 0.0)


def _pool_diff(uf, w, row):
    acc, k = uf, 1
    while k < w:
        acc = acc + _shift_down(acc, k, row)
        k *= 2
    n = jnp.minimum(row + 1, w).astype(F32)
    return acc / n - uf


def _pool_fwd(proj, pool_w, pool_scale, n_seq, name):
    T = proj.shape[0]
    S = T // n_seq

    def body(u_ref, w_ref, sc_ref, o_ref):
        g = pl.program_id(1)
        row = lax.broadcasted_iota(jnp.int32, (S, POOL_GD), 0)
        uf = u_ref[...].astype(F32)
        d = _pool_diff(uf, POOL_WINDOWS[0], row)
        for gi in range(1, len(POOL_WINDOWS)):
            d = jnp.where(g == gi, _pool_diff(uf, POOL_WINDOWS[gi], row), d)
        e = jnp.dot(d.astype(BF16), w_ref[0], preferred_element_type=F32)
        o_ref[...] = (e * sc_ref[...]).astype(BF16)

    uc = OFF_U // POOL_GD
    return pl.pallas_call(
        body, name=name, grid=(n_seq, len(POOL_WINDOWS)),
        in_specs=[pl.BlockSpec((S, POOL_GD), lambda b, g: (b, uc + g)),
                  pl.BlockSpec((1, POOL_GD, POOL_GD), lambda b, g: (g, 0, 0)),
                  pl.BlockSpec((1, POOL_GD), lambda b, g: (0, g))],
        out_specs=pl.BlockSpec((S, POOL_GD), lambda b, g: (b, g)),
        out_shape=jax.ShapeDtypeStruct((T, BRANCH_W), BF16),
        compiler_params=_cp(("parallel", "parallel")),
    )(proj, pool_w, pool_scale)


def _pool_bwd(proj, dout, pool_w, pool_scale, n_seq, name):
    T = proj.shape[0]
    S = T // n_seq

    def body(u_ref, do_ref, w_ref, sc_ref, du_ref, dw_ref, dsc_ref):
        g, b = pl.program_id(0), pl.program_id(1)
        row = lax.broadcasted_iota(jnp.int32, (S, POOL_GD), 0)
        uf = u_ref[...].astype(F32)
        d = _pool_diff(uf, POOL_WINDOWS[0], row)
        for gi in range(1, len(POOL_WINDOWS)):
            d = jnp.where(g == gi, _pool_diff(uf, POOL_WINDOWS[gi], row), d)
        db16 = d.astype(BF16)
        w = w_ref[0]
        e = jnp.dot(db16, w, preferred_element_type=F32)
        dof = do_ref[...].astype(F32)
        dsc = jnp.sum(dof * e, axis=0, keepdims=True)
        de = (dof * sc_ref[...]).astype(BF16)
        dd = lax.dot_general(de, w, (((1,), (1,)), ((), ())), preferred_element_type=F32)
        dw = lax.dot_general(db16, de, (((0,), (0,)), ((), ())), preferred_element_type=F32)
        du = jnp.zeros_like(dd)
        for gi, wlen in enumerate(POOL_WINDOWS):
            n = jnp.minimum(row + 1, wlen).astype(F32)
            acc, k = dd / n, 1
            while k < wlen:
                acc = acc + _shift_up(acc, k, row, S)
                k *= 2
            du = jnp.where(g == gi, acc - dd, du)
        du_ref[...] = du.astype(BF16)

        @pl.when(b == 0)
        def _():
            dw_ref[0] = dw
            dsc_ref[...] = dsc

        @pl.when(b > 0)
        def _():
            dw_ref[0] += dw
            dsc_ref[...] += dsc

    uc = OFF_U // POOL_GD
    return pl.pallas_call(
        body, name=name, grid=(len(POOL_WINDOWS), n_seq),
        in_specs=[pl.BlockSpec((S, POOL_GD), lambda g, b: (b, uc + g)),
                  pl.BlockSpec((S, POOL_GD), lambda g, b: (b, g)),
                  pl.BlockSpec((1, POOL_GD, POOL_GD), lambda g, b: (g, 0, 0)),
                  pl.BlockSpec((1, POOL_GD), lambda g, b: (0, g))],
        out_specs=[pl.BlockSpec((S, POOL_GD), lambda g, b: (b, g)),
                   pl.BlockSpec((1, POOL_GD, POOL_GD), lambda g, b: (g, 0, 0)),
                   pl.BlockSpec((1, POOL_GD), lambda g, b: (0, g))],
        out_shape=[jax.ShapeDtypeStruct((T, BRANCH_W), BF16),
                   jax.ShapeDtypeStruct((len(POOL_WINDOWS), POOL_GD, POOL_GD), F32),
                   jax.ShapeDtypeStruct((1, BRANCH_W), F32)],
        compiler_params=_cp(("parallel", "arbitrary")),
    )(proj, dout, pool_w, pool_scale)


def _conv_fwd(proj, conv_w, n_seq, name):
    T = proj.shape[0]
    S = T // n_seq
    nc = BRANCH_W // LANES

    def body(cv_ref, cb_ref, cc_ref, w_ref, o_ref):
        row = lax.broadcasted_iota(jnp.int32, (S, LANES), 0)
        z = cc_ref[...].astype(F32) * cv_ref[...].astype(F32)
        w = w_ref[...]
        y = w[0:1] * _shift_down(z, 2, row) + w[1:2] * _shift_down(z, 1, row) + w[2:3] * z
        o_ref[...] = (cb_ref[...].astype(F32) * y).astype(BF16)

    def col(off):
        return pl.BlockSpec((S, LANES), lambda b, j: (b, off // LANES + j))

    return pl.pallas_call(
        body, name=name, grid=(n_seq, nc),
        in_specs=[col(OFF_CV), col(OFF_CB), col(OFF_CC), pl.BlockSpec((CONV_K, LANES), lambda b, j: (0, j))],
        out_specs=pl.BlockSpec((S, LANES), lambda b, j: (b, j)),
        out_shape=jax.ShapeDtypeStruct((T, BRANCH_W), BF16),
        compiler_params=_cp(("parallel", "parallel")),
    )(proj, proj, proj, conv_w)


def _conv_bwd(proj, dout, conv_w, n_seq, name):
    T = proj.shape[0]
    S = T // n_seq
    nc = BRANCH_W // LANES

    def body(cv_ref, cb_ref, cc_ref, do_ref, w_ref, dcv_ref, dcb_ref, dcc_ref, dw_ref):
        b = pl.program_id(1)
        row = lax.broadcasted_iota(jnp.int32, (S, LANES), 0)
        cv, cb, cc = cv_ref[...].astype(F32), cb_ref[...].astype(F32), cc_ref[...].astype(F32)
        dof = do_ref[...].astype(F32)
        w = w_ref[...]
        z = cc * cv
        z1, z2 = _shift_down(z, 1, row), _shift_down(z, 2, row)
        y = w[0:1] * z2 + w[1:2] * z1 + w[2:3] * z
        dcb_ref[...] = (dof * y).astype(BF16)
        dy = dof * cb
        dz = w[2:3] * dy + w[1:2] * _shift_up(dy, 1, row, S) + w[0:1] * _shift_up(dy, 2, row, S)
        dcc_ref[...] = (dz * cv).astype(BF16)
        dcv_ref[...] = (dz * cc).astype(BF16)
        dws = [jnp.sum(dy * zk, axis=0, keepdims=True) for zk in (z2, z1, z)]

        @pl.when(b == 0)
        def _():
            for kk in range(CONV_K):
                dw_ref[kk:kk + 1, :] = dws[kk]

        @pl.when(b > 0)
        def _():
            for kk in range(CONV_K):
                dw_ref[kk:kk + 1, :] += dws[kk]

    def col(off):
        return pl.BlockSpec((S, LANES), lambda j, b: (b, off // LANES + j))

    out = pl.BlockSpec((S, LANES), lambda j, b: (b, j))
    wsp = pl.BlockSpec((CONV_K, LANES), lambda j, b: (0, j))
    act = jax.ShapeDtypeStruct((T, BRANCH_W), BF16)
    return pl.pallas_call(
        body, name=name, grid=(nc, n_seq),
        in_specs=[col(OFF_CV), col(OFF_CB), col(OFF_CC), out, wsp],
        out_specs=[out, out, out, wsp],
        out_shape=[act, act, act, jax.ShapeDtypeStruct((CONV_K, BRANCH_W), F32)],
        compiler_params=_cp(("parallel", "arbitrary")),
    )(proj, proj, proj, dout, conv_w)


def _mix_fwd(oa, ob, oc, wpa, wpp, wpc, proj, b_gate, name):
    T = oa.shape[0]
    tm = min(256, T)

    def body(oa_ref, ob_ref, oc_ref, wa_ref, wp_ref, wc_ref, g_ref, bg_ref, o_ref):
        acc = jnp.zeros((tm, D_MODEL), F32)
        for i, (x_ref, w_ref) in enumerate(((oa_ref, wa_ref), (ob_ref, wp_ref), (oc_ref, wc_ref))):
            y = jnp.dot(x_ref[...], w_ref[...], preferred_element_type=F32)
            sl = slice(i * D_MODEL, (i + 1) * D_MODEL)
            acc = acc + _sigmoid(g_ref[:, sl].astype(F32) + bg_ref[:, sl]) * y
        o_ref[...] = acc.astype(BF16)

    br = pl.BlockSpec((tm, BRANCH_W), lambda i: (i, 0))
    wsp = pl.BlockSpec((BRANCH_W, D_MODEL), lambda i: (0, 0))
    return pl.pallas_call(
        body, name=name, grid=(T // tm,),
        in_specs=[br, br, br, wsp, wsp, wsp, pl.BlockSpec((tm, GATE_W), lambda i: (i, 0)),
                  pl.BlockSpec((1, GATE_W), lambda i: (0, 0))],
        out_specs=pl.BlockSpec((tm, D_MODEL), lambda i: (i, 0)),
        out_shape=jax.ShapeDtypeStruct((T, D_MODEL), BF16),
        compiler_params=_cp(("parallel",)),
    )(oa, ob, oc, wpa, wpp, wpc, proj, b_gate)


def _mix_bwd(oa, ob, oc, wpa, wpp, wpc, proj, b_gate, dmixed, name):
    T = oa.shape[0]
    tm = min(256, T)

    def body(oa_ref, ob_ref, oc_ref, wa_ref, wp_ref, wc_ref, g_ref, bg_ref, dm_ref,
             dya_ref, dyb_ref, dyc_ref, dg_ref, dbg_ref):
        i0 = pl.program_id(0)
        dm = dm_ref[...].astype(F32)
        parts = []
        for i, (x_ref, w_ref, dy_ref) in enumerate(((oa_ref, wa_ref, dya_ref), (ob_ref, wp_ref, dyb_ref),
                                                    (oc_ref, wc_ref, dyc_ref))):
            y = jnp.dot(x_ref[...], w_ref[...], preferred_element_type=F32)
            sl = slice(i * D_MODEL, (i + 1) * D_MODEL)
            gate = _sigmoid(g_ref[:, sl].astype(F32) + bg_ref[:, sl])
            dy_ref[...] = (dm * gate).astype(BF16)
            dgl = dm * y * gate * (1.0 - gate)
            dg_ref[:, sl] = dgl.astype(BF16)
            parts.append(jnp.sum(dgl, axis=0, keepdims=True))

        @pl.when(i0 == 0)
        def _():
            for i in range(3):
                dbg_ref[:, i * D_MODEL:(i + 1) * D_MODEL] = parts[i]

        @pl.when(i0 > 0)
        def _():
            for i in range(3):
                dbg_ref[:, i * D_MODEL:(i + 1) * D_MODEL] += parts[i]

    br = pl.BlockSpec((tm, BRANCH_W), lambda i: (i, 0))
    wsp = pl.BlockSpec((BRANCH_W, D_MODEL), lambda i: (0, 0))
    row = pl.BlockSpec((tm, D_MODEL), lambda i: (i, 0))
    gsp = pl.BlockSpec((tm, GATE_W), lambda i: (i, 0))
    bsp = pl.BlockSpec((1, GATE_W), lambda i: (0, 0))
    act = jax.ShapeDtypeStruct((T, D_MODEL), BF16)
    return pl.pallas_call(
        body, name=name, grid=(T // tm,),
        in_specs=[br, br, br, wsp, wsp, wsp, gsp, bsp, row],
        out_specs=[row, row, row, gsp, bsp],
        out_shape=[act, act, act, jax.ShapeDtypeStruct((T, GATE_W), BF16),
                   jax.ShapeDtypeStruct((1, GATE_W), F32)],
        compiler_params=_cp(("arbitrary",)),
    )(oa, ob, oc, wpa, wpp, wpc, proj, b_gate, dmixed)


def _swiglu_fwd(ab, name):
    T = ab.shape[0]
    tm = min(256, T)

    def body(ab_ref, o_ref):
        a = ab_ref[:, :FFN_HIDDEN].astype(F32)
        o_ref[...] = (a * _sigmoid(a) * ab_ref[:, FFN_HIDDEN:].astype(F32)).astype(BF16)

    return pl.pallas_call(
        body, name=name, grid=(T // tm,),
        in_specs=[pl.BlockSpec((tm, 2 * FFN_HIDDEN), lambda i: (i, 0))],
        out_specs=pl.BlockSpec((tm, FFN_HIDDEN), lambda i: (i, 0)),
        out_shape=jax.ShapeDtypeStruct((T, FFN_HIDDEN), BF16),
        compiler_params=_cp(("parallel",)),
    )(ab)


def _swiglu_bwd(ab, ds, name):
    T = ab.shape[0]
    tm = min(256, T)

    def body(ab_ref, ds_ref, o_ref):
        a = ab_ref[:, :FFN_HIDDEN].astype(F32)
        b = ab_ref[:, FFN_HIDDEN:].astype(F32)
        dsf = ds_ref[...].astype(F32)
        sg = _sigmoid(a)
        o_ref[:, :FFN_HIDDEN] = (dsf * b * sg * (1.0 + a * (1.0 - sg))).astype(BF16)
        o_ref[:, FFN_HIDDEN:] = (dsf * a * sg).astype(BF16)

    full = pl.BlockSpec((tm, 2 * FFN_HIDDEN), lambda i: (i, 0))
    return pl.pallas_call(
        body, name=name, grid=(T // tm,),
        in_specs=[full, pl.BlockSpec((tm, FFN_HIDDEN), lambda i: (i, 0))],
        out_specs=full,
        out_shape=jax.ShapeDtypeStruct((T, 2 * FFN_HIDDEN), BF16),
        compiler_params=_cp(("parallel",)),
    )(ab, ds)


def _adamw(w, g, m, v, name):
    R, C = w.shape
    tr = R
    for cand in (256, 352, 128, 64, 8):
        if R > cand and R % cand == 0:
            tr = cand
            break

    def body(w_ref, g_ref, m_ref, v_ref, d_ref, nm_ref, nv_ref):
        gv = g_ref[...]
        nm = ADAM_B1 * m_ref[...] + (1.0 - ADAM_B1) * gv
        nv = ADAM_B2 * v_ref[...] + (1.0 - ADAM_B2) * (gv * gv)
        m_hat = nm / (1.0 - ADAM_B1 ** ADAM_STEP)
        v_hat = nv / (1.0 - ADAM_B2 ** ADAM_STEP)
        d_ref[...] = -ADAM_LR * (m_hat / (jnp.sqrt(v_hat) + ADAM_EPS) + ADAM_WD * w_ref[...])
        nm_ref[...] = nm
        nv_ref[...] = nv

    blk = pl.BlockSpec((tr, C), lambda i: (i, 0))
    sh = jax.ShapeDtypeStruct((R, C), F32)
    return pl.pallas_call(
        body, name=name, grid=(R // tr,), in_specs=[blk] * 4, out_specs=[blk] * 3, out_shape=[sh] * 3,
        compiler_params=_cp(("parallel",)),
    )(w, g, m, v)


def _sum_slabs(x, name):
    n, R, C = x.shape
    tr = R
    for cand in (512, 256, 128, 64, 32, 16, 8):
        if R > cand and R % cand == 0:
            tr = cand
            break

    def body(x_ref, o_ref):
        acc = x_ref[0].astype(F32)
        for j in range(1, n):
            acc = acc + x_ref[j].astype(F32)
        o_ref[...] = acc

    return pl.pallas_call(
        body, name=name, grid=(R // tr,), in_specs=[pl.BlockSpec((n, tr, C), lambda i: (0, i, 0))],
        out_specs=pl.BlockSpec((tr, C), lambda i: (i, 0)), out_shape=jax.ShapeDtypeStruct((R, C), F32),
        compiler_params=_cp(("parallel",)),
    )(x)


def _multi_gather(xs, layers, name):
    nt = len(xs)
    shapes = [x.shape if lay is None else x.shape[1:] for x, lay in zip(xs, layers)]

    def body(*refs):
        x_refs, out_refs = refs[:nt], refs[nt:2 * nt]
        send_sems, recv_sems, local_sems = refs[2 * nt:]
        x_, y_, c_ = lax.axis_index("x"), lax.axis_index("y"), lax.axis_index("c")
        me, sibling = (x_, y_, c_), (x_, y_, 1 - c_)
        chips = [(1 - x_, y_), (x_, 1 - y_), (1 - x_, 1 - y_)]

        def own_block(t):
            return x_refs[t] if layers[t] is None else x_refs[t].at[layers[t]]

        def copy(t, k, block, to, own=False):
            px, py, pc = block
            dst = out_refs[t].at[4 * px + 2 * py + pc]
            return pltpu.make_async_remote_copy(
                src_ref=own_block(t) if own else dst, dst_ref=dst,
                send_sem=send_sems.at[t, k], recv_sem=recv_sems.at[t, k],
                device_id=to, device_id_type=pl.DeviceIdType.MESH)

        mine, first, passed = [], [], []
        for t in range(nt):
            mine.append(pltpu.make_async_copy(own_block(t), out_refs[t].at[4 * x_ + 2 * y_ + c_], local_sems.at[t]))
            mine[-1].start()
            first.append([copy(t, 1 + j, me, (*chip, c_), own=True) for j, chip in enumerate(chips)]
                         + [copy(t, 0, me, sibling, own=True)])
            for cp in first[-1]:
                cp.start()
        for t in range(nt):
            for j, chip in enumerate(chips):
                copy(t, 1 + j, (*chip, c_), me).wait_recv()
                passed.append(copy(t, 4 + j, (*chip, c_), sibling))
                passed[-1].start()
        for t in range(nt):
            copy(t, 0, sibling, me).wait_recv()
            for j, chip in enumerate(chips):
                copy(t, 4 + j, (*chip, 1 - c_), me).wait_recv()
        for cp in [c for f in first for c in f] + passed:
            cp.wait_send()
        for cp in mine:
            cp.wait()

    hbm = pl.BlockSpec(memory_space=pl.ANY)
    return pl.pallas_call(
        body, name=name, out_shape=[jax.ShapeDtypeStruct((N_DEV,) + tuple(s), x.dtype) for s, x in zip(shapes, xs)],
        in_specs=[hbm] * nt, out_specs=[hbm] * nt,
        scratch_shapes=[pltpu.SemaphoreType.DMA((nt, 7)), pltpu.SemaphoreType.DMA((nt, 7)),
                        pltpu.SemaphoreType.DMA((nt,))],
    )(*xs)


def _multi_exchange(sends, name):
    nt = len(sends)

    def body(*refs):
        s_refs, r_refs = refs[:nt], refs[nt:2 * nt]
        send_sems, recv_sems, local_sems = refs[2 * nt:]
        x_, y_, c_ = lax.axis_index("x"), lax.axis_index("y"), lax.axis_index("c")
        me = 4 * x_ + 2 * y_ + c_
        mine, out, inc = [], [], []
        for t in range(nt):
            mine.append(pltpu.make_async_copy(s_refs[t].at[me], r_refs[t].at[me], local_sems.at[t]))
            mine[-1].start()
        for k in (2, 4, 6, 3, 5, 7, 1):
            px, py, pc = x_ ^ ((k >> 2) & 1), y_ ^ ((k >> 1) & 1), c_ ^ (k & 1)
            peer = 4 * px + 2 * py + pc
            for t in range(nt):
                def copy(src, dst):
                    return pltpu.make_async_remote_copy(
                        src_ref=s_refs[t].at[src], dst_ref=r_refs[t].at[dst],
                        send_sem=send_sems.at[t, k - 1], recv_sem=recv_sems.at[t, k - 1],
                        device_id=(px, py, pc), device_id_type=pl.DeviceIdType.MESH)

                out.append(copy(peer, me))
                inc.append(copy(me, peer))
        for cp in out:
            cp.start()
        for cp in inc:
            cp.wait_recv()
        for cp in out:
            cp.wait_send()
        for cp in mine:
            cp.wait()

    hbm = pl.BlockSpec(memory_space=pl.ANY)
    return pl.pallas_call(
        body, name=name, out_shape=[jax.ShapeDtypeStruct(s.shape, s.dtype) for s in sends],
        in_specs=[hbm] * nt, out_specs=[hbm] * nt,
        scratch_shapes=[pltpu.SemaphoreType.DMA((nt, N_DEV - 1)), pltpu.SemaphoreType.DMA((nt, N_DEV - 1)),
                        pltpu.SemaphoreType.DMA((nt,))],
    )(*sends)


_HBM = pl.BlockSpec(memory_space=pltpu.HBM)
_SEM = pl.BlockSpec(memory_space=pltpu.SEMAPHORE)
_PEER_ORDER = (2, 4, 6, 3, 5, 7, 1)


def _split_copies(src_refs, land_refs, send_sems, recv_sems, layers, per_peer):
    x_, y_, c_ = lax.axis_index("x"), lax.axis_index("y"), lax.axis_index("c")
    me = 4 * x_ + 2 * y_ + c_
    copies = []
    for k in _PEER_ORDER:
        px, py, pc = x_ ^ ((k >> 2) & 1), y_ ^ ((k >> 1) & 1), c_ ^ (k & 1)
        peer = 4 * px + 2 * py + pc
        for t in range(len(src_refs)):
            if per_peer:
                src = src_refs[t].at[peer]
            else:
                src = src_refs[t] if layers[t] is None else src_refs[t].at[layers[t]]
            copies.append(pltpu.make_async_remote_copy(
                src_ref=src, dst_ref=land_refs[t].at[me],
                send_sem=send_sems.at[t * (N_DEV - 1) + k - 1], recv_sem=recv_sems.at[t * (N_DEV - 1) + k - 1],
                device_id=(px, py, pc), device_id_type=pl.DeviceIdType.MESH))
    return copies


def _split_start(srcs, layers, per_peer, after, name):
    nt = len(srcs)
    if per_peer:
        land_shapes = [s.shape for s in srcs]
    else:
        land_shapes = [(N_DEV,) + tuple(s.shape if lay is None else s.shape[1:]) for s, lay in zip(srcs, layers)]

    def body(*refs):
        src_refs, land_refs = refs[:nt], refs[nt:2 * nt]
        send_sems, recv_sems = refs[2 * nt + 1], refs[2 * nt + 2]
        token = refs[-1]
        for cp in _split_copies(src_refs, land_refs, send_sems, recv_sems, layers, per_peer):
            cp.start()
        token[...] = jnp.zeros_like(token)

    lands = [pltpu.with_memory_space_constraint(lax.empty(s, x.dtype), pltpu.HBM) for s, x in zip(land_shapes, srcs)]
    srcs = [pltpu.with_memory_space_constraint(x, pltpu.HBM) for x in srcs]
    out = pl.pallas_call(
        body, name=name,
        out_shape=(pltpu.SemaphoreType.DMA((nt * (N_DEV - 1),)), pltpu.SemaphoreType.DMA((nt * (N_DEV - 1),)),
                   *[pltpu.HBM(x.shape, x.dtype) for x in srcs], *[pltpu.HBM(s, x.dtype) for s, x in zip(land_shapes, srcs)],
                   jax.ShapeDtypeStruct((8, LANES), F32)),
        in_specs=[_HBM] * (2 * nt) + [pl.BlockSpec(memory_space=pl.ANY)],
        out_specs=(_SEM, _SEM, *([_HBM] * (2 * nt)), pl.BlockSpec(memory_space=pltpu.VMEM)),
        input_output_aliases={i: 2 + i for i in range(2 * nt)},
        compiler_params=pltpu.CompilerParams(has_side_effects=pltpu.SideEffectType.DATAFLOW_SIDE_EFFECTING),
    )(*srcs, *lands, after)
    return out[0], out[1], list(out[2:2 + nt]), list(out[2 + nt:2 + 2 * nt]), out[-1]


def _split_wait(started, layers, per_peer, after, name):
    send_sems, recv_sems, srcs, lands, _ = started
    nt = len(srcs)

    def body(*refs):
        src_refs, land_refs = refs[:nt], refs[nt:2 * nt]
        s_sems, r_sems = refs[2 * nt], refs[2 * nt + 1]
        for cp in _split_copies(src_refs, land_refs, s_sems, r_sems, layers, per_peer):
            cp.wait_send()
            cp.wait_recv()

    out = pl.pallas_call(
        body, name=name,
        out_shape=tuple(pltpu.HBM(x.shape, x.dtype) for x in srcs + lands),
        in_specs=[_HBM] * (2 * nt) + [_SEM, _SEM, pl.BlockSpec(memory_space=pl.ANY)],
        out_specs=tuple([_HBM] * (2 * nt)),
        input_output_aliases={i: i for i in range(2 * nt)},
        compiler_params=pltpu.CompilerParams(has_side_effects=pltpu.SideEffectType.DATAFLOW_SIDE_EFFECTING),
    )(*srcs, *lands, send_sems, recv_sems, after)
    return list(out[nt:])


def _with_own(land, own):
    me = 4 * lax.axis_index("x") + 2 * lax.axis_index("y") + lax.axis_index("c")
    return lax.dynamic_update_slice_in_dim(land, own[None], me, axis=0)


def _runs(mapping):
    runs, c, n = [], 0, len(mapping)
    while c < n:
        if mapping[c] is None:
            c += 1
            continue
        sid, d, lo = mapping[c][0], mapping[c][1] - c, c
        while c < n and mapping[c] is not None and mapping[c][0] == sid and mapping[c][1] - c == d:
            c += 1
        runs.append((lo, c, sid, d))
    return runs


def _tile_plan(mapping, src_widths):
    runs = _runs(mapping)
    plan = []
    for t in range(len(mapping) // LANES):
        pieces = []
        for lo, hi, sid, d in runs:
            lo_t, hi_t = max(lo, t * LANES), min(hi, (t + 1) * LANES)
            if lo_t >= hi_t:
                continue
            a = ((lo_t + d) // LANES) * LANES
            win = min(2 * LANES, src_widths[sid] - a)
            shift = t * LANES + d - a
            pieces.append((sid, a, win, shift, lo_t - t * LANES, hi_t - t * LANES))
        plan.append(pieces)
    return plan


def _reblock(srcs, src_views, outs, out_views, name):
    R = srcs[0].shape[-2]
    tr = min(256, R)
    widths = {sid: srcs[ai].shape[-1] for sid, (ai, _) in src_views.items()}
    plans = [(ai, li, _tile_plan(mapping, widths)) for ai, li, mapping in out_views]
    ns = len(srcs)

    def body(*refs):
        s_refs, o_refs = refs[:ns], refs[ns:]
        cache = {}

        def shift_matrix(win, shift, lo, hi):
            key = (win, shift, lo, hi)
            if key not in cache:
                r = lax.broadcasted_iota(jnp.int32, (win, LANES), 0)
                c = lax.broadcasted_iota(jnp.int32, (win, LANES), 1)
                hit = jnp.logical_and(r - c == shift, jnp.logical_and(c >= lo, c < hi))
                cache[key] = jnp.where(hit, 1.0, 0.0).astype(BF16)
            return cache[key]

        for ai, li, plan in plans:
            for t, pieces in enumerate(plan):
                acc = None
                for sid, a, win, shift, lo, hi in pieces:
                    sa, sl = src_views[sid]
                    src = s_refs[sa][:, a:a + win] if sl is None else s_refs[sa][sl, :, a:a + win]
                    part = jnp.dot(src, shift_matrix(win, shift, lo, hi), preferred_element_type=F32)
                    acc = part if acc is None else acc + part
                val = jnp.zeros((tr, LANES), BF16) if acc is None else acc.astype(BF16)
                if li is None:
                    o_refs[ai][:, t * LANES:(t + 1) * LANES] = val
                else:
                    o_refs[ai][li, :, t * LANES:(t + 1) * LANES] = val

    def spec(shape):
        if len(shape) == 2:
            return pl.BlockSpec((tr, shape[1]), lambda i: (i, 0))
        return pl.BlockSpec((shape[0], tr, shape[2]), lambda i: (0, i, 0))

    return pl.pallas_call(
        body, name=name, grid=(R // tr,), in_specs=[spec(s.shape) for s in srcs],
        out_specs=[spec(s) for s in outs], out_shape=[jax.ShapeDtypeStruct(s, BF16) for s in outs],
        compiler_params=_cp(("parallel",)),
    )(*srcs)


SHARDED = ("w_in", "w_gate_up", "w_proj_attn", "w_proj_pool", "w_proj_conv", "w_out", "w_down")
WEIGHT_ORDER = ("attn_norm", "w_in", "b_forget", "b_gate", "w_proj_attn", "pool_w", "pool_scale", "w_proj_pool",
                "conv_w", "w_proj_conv", "w_out", "ffn_norm", "w_gate_up", "w_down", "final_norm")
IN_SHARD, IN_SHARD_PAD = IN_COLS // N_DEV, 896
GU_SHARD, GU_SHARD_PAD = 2 * FFN_HIDDEN // N_DEV, 768


def _w_in_col(c):
    if c < GATE_W:
        return c + 3592
    if c < OFF_U:
        return c - OFF_Q
    return c - OFF_U + 1544


def _w_in_full(gathered, name):
    main = [divmod(_w_in_col(c), IN_SHARD) for c in range(MAIN_COLS)]
    fcols = [divmod(1536 + c, IN_SHARD) if c < N_HEADS else None for c in range(LANES)]
    R = gathered.shape[1]
    return _reblock([gathered], {i: (0, i) for i in range(N_DEV)}, [(R, MAIN_COLS), (R, LANES)],
                    [(0, None, main), (1, None, fcols)], name)


def _w_in_slabs(dmain, dwf, name):
    inv = {_w_in_col(c): ("m", c) for c in range(MAIN_COLS)}
    inv.update({1536 + c: ("f", c) for c in range(N_HEADS)})
    views = []
    for i in range(N_DEV):
        mapping = [inv[IN_SHARD * i + j] if j < IN_SHARD else None for j in range(IN_SHARD_PAD)]
        views.append((0, i, mapping))
    R = dmain.shape[0]
    return _reblock([dmain, dwf], {"m": (0, None), "f": (1, None)}, [(N_DEV, R, IN_SHARD_PAD)], views, name)[0]


def _w_gu_full(gathered, name):
    mapping = [divmod(c, GU_SHARD) for c in range(2 * FFN_HIDDEN)]
    R = gathered.shape[1]
    return _reblock([gathered], {i: (0, i) for i in range(N_DEV)}, [(R, 2 * FFN_HIDDEN)], [(0, None, mapping)], name)[0]


def _w_gu_slabs(dw, name):
    views = [(0, i, [("w", GU_SHARD * i + j) if j < GU_SHARD else None for j in range(GU_SHARD_PAD)])
             for i in range(N_DEV)]
    R = dw.shape[0]
    return _reblock([dw], {"w": (0, None)}, [(N_DEV, R, GU_SHARD_PAD)], views, name)[0]


def _layer_fwd(x, W, n_seq, l):
    T = x.shape[0]
    sfx = f"_l{l}"
    h1 = _rms_fwd(x, W["attn_norm"], "rms1" + sfx)
    proj = _matmul(h1, W["w_main"], mode="nn", out_dtype=BF16, name="proj_main" + sfx)
    f = _matmul(h1, W["w_f"], mode="nn", out_dtype=F32, name="proj_f" + sfx)
    qa, ka, va = _fox_prep(f, W["b_forget"], proj, n_seq, "fox_prep" + sfx)
    oa, oa32, lse = _attn_fwd2(qa, ka, va, n_seq, "attn_fwd" + sfx)
    if "late" in W:
        W.update(W.pop("late")(oa))
    ob = _pool_fwd(proj, W["pool_w"], W["pool_scale"], n_seq, "pool_fwd" + sfx)
    oc = _conv_fwd(proj, W["conv_w"], n_seq, "conv_fwd" + sfx)
    mixed = _mix_fwd(oa, ob, oc, W["w_proj_attn"], W["w_proj_pool"], W["w_proj_conv"], proj, W["b_gate"],
                     "mix_fwd" + sfx)
    x2 = _matmul(mixed, W["w_out"], mode="nn", out_dtype=F32, name="out_proj" + sfx, residual=x)
    h2 = _rms_fwd(x2, W["ffn_norm"], "rms2" + sfx)
    ab = _matmul(h2, W["w_gate_up"], mode="nn", out_dtype=BF16, name="gate_up" + sfx)
    s = _swiglu_fwd(ab, "swiglu_fwd" + sfx)
    x3 = _matmul(s, W["w_down"], mode="nn", out_dtype=F32, name="down" + sfx, tm=1024, tn=1024, tk=1408,
                 residual=x2)
    saved = dict(x=x, h1=h1, proj=proj, f=f, qa=qa, ka=ka, oa=oa, oa32=oa32, lse=lse, ob=ob, oc=oc, mixed=mixed, x2=x2,
                 h2=h2, ab=ab, s=s)
    return x3, saved


def _layer_bwd(dx3, dx3b, W, sv, n_seq, l, stage=None):
    T = dx3.shape[0]
    sfx = f"_l{l}"
    G = {}
    stage = stage or (lambda l, group, G, W: W)
    ds = _matmul(dx3b, W["w_down"], mode="nt", out_dtype=BF16, name="d_s" + sfx, tm=1024, tn=1408)
    G["w_down"] = _matmul(sv["s"], dx3b, mode="tn", out_dtype=BF16, name="dw_down" + sfx, tm=256, tn=1024)
    dab = _swiglu_bwd(sv["ab"], ds, "swiglu_bwd" + sfx)
    dh2 = _matmul(dab, W["w_gate_up"], mode="nt", out_dtype=BF16, name="d_h2" + sfx, tm=1024, tn=1024, tk=1408)
    G["w_gate_up"] = _matmul(sv["h2"], dab, mode="tn", out_dtype=BF16, name="dw_gate_up" + sfx, tm=1024)
    W = stage(l, "ffn", G, W)
    dx2, dx2b, G["ffn_norm"] = _rms_bwd(sv["x2"], W["ffn_norm"], dh2, dx3, "rms2_bwd" + sfx)
    dmixed = _matmul(dx2b, W["w_out"], mode="nt", out_dtype=BF16, name="d_mixed" + sfx)
    G["w_out"] = _matmul(sv["mixed"], dx2b, mode="tn", out_dtype=BF16, name="dw_out" + sfx, tm=1024)
    dya, dyb, dyc, dg, G["b_gate"] = _mix_bwd(sv["oa"], sv["ob"], sv["oc"], W["w_proj_attn"], W["w_proj_pool"],
                                              W["w_proj_conv"], sv["proj"], W["b_gate"], dmixed, "mix_bwd" + sfx)
    douts = {}
    for br, dy, o in (("attn", dya, sv["oa"]), ("pool", dyb, sv["ob"]), ("conv", dyc, sv["oc"])):
        douts[br] = _matmul(dy, W["w_proj_" + br], mode="nt", out_dtype=BF16, name=f"d_{br}_out" + sfx)
        G["w_proj_" + br] = _matmul(o, dy, mode="tn", out_dtype=BF16, name=f"dw_proj_{br}" + sfx, tm=512)
    W = stage(l, "mix", G, W)
    dcv, dcb, dcc, G["conv_w"] = _conv_bwd(sv["proj"], douts["conv"], W["conv_w"], n_seq, "conv_bwd" + sfx)
    du, G["pool_w"], G["pool_scale"] = _pool_bwd(sv["proj"], douts["pool"], W["pool_w"], W["pool_scale"], n_seq,
                                                 "pool_bwd" + sfx)
    delta = _attn_delta(douts["attn"], sv["oa32"], "attn_delta" + sfx)
    dq = _attn_bwd_dq2(sv["qa"], sv["ka"], sv["proj"], douts["attn"], sv["lse"], delta, n_seq, "attn_dq" + sfx)
    dk, dv, dFk = _attn_bwd_dkv2(sv["qa"], sv["ka"], sv["proj"], douts["attn"], sv["lse"], delta, n_seq,
                                 "attn_dkv" + sfx)
    dF = jnp.pad(dFk.reshape(N_HEADS, T).T, ((0, 0), (0, LANES - N_HEADS)))
    df, G["b_forget"] = _fox_cumsum_bwd(sv["f"], W["b_forget"], dF, n_seq, "fox_cumsum_bwd" + sfx)
    dproj = jnp.concatenate([dg, dq, dk, dv, du, dcv, dcb, dcc], axis=1)
    G["w_main"] = _matmul(sv["h1"], dproj, mode="tn", out_dtype=BF16, name="dw_main" + sfx, tm=1024)
    G["w_f"] = _matmul(sv["h1"], df, mode="tn", out_dtype=BF16, name="dw_f" + sfx, tm=1024)
    W = stage(l, "w_in", G, W)
    dh1 = _matmul(df, W["w_f"], mode="nt", out_dtype=F32, name="d_h1_f" + sfx)
    dh1 = _matmul(dproj, W["w_main"], mode="nt", out_dtype=F32, name="d_h1_main" + sfx, tm=1024, tn=1024, tk=1664,
                  residual=dh1)
    dx, dxb, G["attn_norm"] = _rms_bwd(sv["x"], W["attn_norm"], dh1, dx2, "rms1_bwd" + sfx)
    return dx, dxb, G


def _replicated_operands(rep, l):
    W = {}
    W["attn_norm"], W["ffn_norm"] = rep["attn_norm"][l], rep["ffn_norm"][l]
    W["b_forget"] = jnp.pad(rep["b_forget"][l].reshape(1, N_HEADS), ((0, 0), (0, LANES - N_HEADS)))
    W["b_gate"] = rep["b_gate"][l].reshape(1, GATE_W)
    W["pool_w"] = rep["pool_w"][l].astype(BF16)
    W["pool_scale"] = rep["pool_scale"][l].reshape(1, BRANCH_W)
    return W


def _local_step(x, target, get_W, final_norm, stage=None):
    n_seq, S, Dm = x.shape
    T = n_seq * S
    xt = x.reshape(T, Dm)
    saved, Ws = [], []
    for l in range(DEPTH):
        Ws.append(get_W(l, xt))
        xt, sv = _layer_fwd(xt, Ws[l], n_seq, l)
        saved.append(sv)
    loss, dx, dxb, g_final = _loss_head(xt, final_norm, target.reshape(T, Dm), "loss_head")
    grads = [None] * DEPTH
    for l in reversed(range(DEPTH)):
        dx, dxb, grads[l] = _layer_bwd(dx, dxb, Ws[l], saved[l], n_seq, l, stage)
    return loss, dx.reshape(n_seq, S, Dm), grads, g_final


def _padded_shards(weights):
    sh = {n: weights[n].astype(BF16) for n in SHARDED}
    sh["w_in"] = jnp.pad(sh["w_in"], ((0, 0), (0, 0), (0, IN_SHARD_PAD - IN_SHARD)))
    sh["w_gate_up"] = jnp.pad(sh["w_gate_up"], ((0, 0), (0, 0), (0, GU_SHARD_PAD - GU_SHARD)))
    return sh


def _full_operands(g, l):
    W = {}
    if "w_in" in g:
        W["w_main"], W["w_f"] = _w_in_full(g["w_in"], f"w_in_full_l{l}")
    if "w_gate_up" in g:
        W["w_gate_up"] = _w_gu_full(g["w_gate_up"], f"w_gate_up_full_l{l}")
    for n in ("w_proj_attn", "w_proj_pool", "w_proj_conv"):
        if n in g:
            W[n] = jnp.transpose(g[n], (1, 0, 2)).reshape(BRANCH_W, D_MODEL)
    if "w_out" in g:
        W["w_out"] = g["w_out"].reshape(D_MODEL, D_MODEL)
    if "w_down" in g:
        W["w_down"] = g["w_down"].reshape(FFN_HIDDEN, D_MODEL)
    return W


GRAD_GROUPS = {"ffn": ("w_down", "w_gate_up"),
               "mix": ("w_out", "w_proj_attn", "w_proj_pool", "w_proj_conv"),
               "w_in": ("w_in",)}


def _grad_slabs(G, n, l):
    if n == "w_in":
        return _w_in_slabs(G["w_main"], G["w_f"], f"w_in_slabs_l{l}")
    if n == "w_gate_up":
        return _w_gu_slabs(G["w_gate_up"], f"w_gate_up_slabs_l{l}")
    if n == "w_out":
        return G["w_out"].reshape(N_DEV, D_MODEL // N_DEV, D_MODEL)
    if n == "w_down":
        return G["w_down"].reshape(N_DEV, FFN_HIDDEN // N_DEV, D_MODEL)
    return jnp.transpose(G[n].reshape(BRANCH_W, N_DEV, D_MODEL // N_DEV), (1, 0, 2))


def _sum_layer_grads(recv, l):
    out = {n: _sum_slabs(r, f"sum_{n}_l{l}") for n, r in recv.items()}
    if "w_in" in out:
        out["w_in"] = out["w_in"][:, :IN_SHARD]
    if "w_gate_up" in out:
        out["w_gate_up"] = out["w_gate_up"][:, :GU_SHARD]
    return out


def _sum_small(xs, name):
    def body(*refs):
        for x_ref, o_ref in zip(refs[:len(xs)], refs[len(xs):]):
            acc = x_ref[0]
            for j in range(1, N_DEV):
                acc = acc + x_ref[j]
            o_ref[...] = acc

    return pl.pallas_call(
        body, name=name, out_shape=[jax.ShapeDtypeStruct(x.shape[1:], F32) for x in xs],
        compiler_params=_cp(),
    )(*xs)


def _as_2d(a):
    if a.ndim == 1:
        return a.reshape(1, -1)
    return a.reshape(-1, a.shape[-1])


def kernel(x, attn_norm, w_in, b_forget, b_gate, w_proj_attn, pool_w, pool_scale, w_proj_pool, conv_w, w_proj_conv, w_out, ffn_norm, w_gate_up, w_down, final_norm, loss_target, m_attn_norm, m_w_in, m_b_forget, m_b_gate, m_w_proj_attn, m_pool_w, m_pool_scale, m_w_proj_pool, m_conv_w, m_w_proj_conv, m_w_out, m_ffn_norm, m_w_gate_up, m_w_down, m_final_norm, v_attn_norm, v_w_in, v_b_forget, v_b_gate, v_w_proj_attn, v_pool_w, v_pool_scale, v_w_proj_pool, v_conv_w, v_w_proj_conv, v_w_out, v_ffn_norm, v_w_gate_up, v_w_down, v_final_norm):
    weights = dict(attn_norm=attn_norm, w_in=w_in, b_forget=b_forget, b_gate=b_gate, w_proj_attn=w_proj_attn,
                   pool_w=pool_w, pool_scale=pool_scale, w_proj_pool=w_proj_pool, conv_w=conv_w,
                   w_proj_conv=w_proj_conv, w_out=w_out, ffn_norm=ffn_norm, w_gate_up=w_gate_up, w_down=w_down,
                   final_norm=final_norm)
    moments_m = dict(attn_norm=m_attn_norm, w_in=m_w_in, b_forget=m_b_forget, b_gate=m_b_gate,
                     w_proj_attn=m_w_proj_attn, pool_w=m_pool_w, pool_scale=m_pool_scale, w_proj_pool=m_w_proj_pool,
                     conv_w=m_conv_w, w_proj_conv=m_w_proj_conv, w_out=m_w_out, ffn_norm=m_ffn_norm,
                     w_gate_up=m_w_gate_up, w_down=m_w_down, final_norm=m_final_norm)
    moments_v = dict(attn_norm=v_attn_norm, w_in=v_w_in, b_forget=v_b_forget, b_gate=v_b_gate,
                     w_proj_attn=v_w_proj_attn, pool_w=v_pool_w, pool_scale=v_pool_scale, w_proj_pool=v_w_proj_pool,
                     conv_w=v_conv_w, w_proj_conv=v_w_proj_conv, w_out=v_w_out, ffn_norm=v_ffn_norm,
                     w_gate_up=v_w_gate_up, w_down=v_w_down, final_norm=v_final_norm)

    sh = _padded_shards(weights)
    names = list(SHARDED)
    rest = [n for n in names if n != "w_in"]
    me = 4 * lax.axis_index("x") + 2 * lax.axis_index("y") + lax.axis_index("c")
    w_in0, conv_all = _multi_gather([sh["w_in"], conv_w], [0, None], "gather_w_in_l0")
    rest_layers, all_layers = [0] * len(rest), [1] * len(names)
    rest_started = _split_start([sh[n] for n in rest], rest_layers, False, w_in0, "gather_start_rest_l0")
    l1_started = _split_start([sh[n] for n in names], all_layers, False, rest_started[4], "gather_start_l1")

    def get_W(l, xt):
        if l == 0:
            W = _full_operands({"w_in": w_in0}, 0)

            def late(after):
                lands = _split_wait(rest_started, rest_layers, False, after, "gather_wait_rest_l0")
                return _full_operands({n: _with_own(land, sh[n][0]) for n, land in zip(rest, lands)}, 0)

            W["late"] = late
        else:
            lands = _split_wait(l1_started, all_layers, False, xt, "gather_wait_l1")
            W = _full_operands({n: _with_own(land, sh[n][l]) for n, land in zip(names, lands)}, l)
        W.update(_replicated_operands(weights, l))
        W["conv_w"] = jnp.transpose(conv_all[:, l], (1, 0, 2)).reshape(CONV_K, BRANCH_W)
        if l == 0:
            W["attn_norm"] = W["attn_norm"] + l1_started[4][0, 0]
        return W

    exchanges = []

    def stage(l, group, G, W):
        gnames = GRAD_GROUPS[group]
        slabs = [_grad_slabs(G, n, l) for n in gnames]
        started = _split_start(slabs, None, True, slabs[0], f"exchange_start_{group}_l{l}")
        exchanges.append((l, group, gnames, slabs, started))
        tie = {"ffn": "ffn_norm", "mix": "conv_w", "w_in": "w_f"}[group]
        W = dict(W)
        W[tie] = W[tie] + started[4][0, 0].astype(W[tie].dtype)
        return W

    loss_part, grad_x, grads, g_final = _local_step(x, loss_target, get_W, final_norm, stage)
    after = grad_x
    for l, group, gnames, slabs, started in exchanges:
        lands = _split_wait(started, None, True, after, f"exchange_wait_{group}_l{l}")
        recv = {n: _with_own(land, lax.dynamic_index_in_dim(s, me, 0, keepdims=False))
                for n, land, s in zip(gnames, lands, slabs)}
        grads[l].update(_sum_layer_grads(recv, l))
    gw = {n: jnp.stack([grads[l][n] for l in range(DEPTH)]) for n in SHARDED}

    small = ("attn_norm", "b_forget", "b_gate", "pool_w", "pool_scale", "ffn_norm", "conv_w")
    parts = [jnp.stack([grads[l][n] for l in range(DEPTH)]) for n in small] + [g_final, loss_part]
    gathered = _multi_gather(parts, [None] * len(parts), "gather_small_grads")
    summed = _sum_small(gathered, "sum_small_grads")
    for n, s in zip(small, summed):
        gw[n] = s
    gw["attn_norm"], gw["ffn_norm"] = gw["attn_norm"][:, 0], gw["ffn_norm"][:, 0]
    gw["b_forget"] = gw["b_forget"][:, 0, :N_HEADS]
    gw["b_gate"], gw["pool_scale"] = gw["b_gate"][:, 0], gw["pool_scale"][:, 0]
    gw["conv_w"] = lax.dynamic_slice_in_dim(gw["conv_w"], me * (BRANCH_W // N_DEV), BRANCH_W // N_DEV, axis=2)
    gw["final_norm"] = summed[-2][0]
    loss = summed[-1][0, 0]

    deltas, new_m, new_v = {}, {}, {}
    for n in WEIGHT_ORDER:
        shape = weights[n].shape
        d, nm, nv = _adamw(_as_2d(weights[n]), _as_2d(gw[n]), _as_2d(moments_m[n]), _as_2d(moments_v[n]),
                           "adamw_" + n)
        deltas[n], new_m[n], new_v[n] = d.reshape(shape), nm.reshape(shape), nv.reshape(shape)

    return (loss, grad_x, *[gw[n] for n in WEIGHT_ORDER], *[deltas[n] for n in WEIGHT_ORDER],
            *[new_m[n] for n in WEIGHT_ORDER], *[new_v[n] for n in WEIGHT_ORDER])
```

```python
import functools

import numpy as np
import jax
import jax.numpy as jnp
from jax import lax
from jax.experimental import pallas as pl
from jax.experimental.pallas import tpu as pltpu

F32 = jnp.float32
BF16 = jnp.bfloat16

N_DEV = 8
D_MODEL = 1024
DEPTH = 2
N_HEADS = 8
HEAD_DIM = 64
BRANCH_W = 512
POOL_WINDOWS = (2, 4, 8, 16)
POOL_GD = 128
CONV_K = 3
FFN_HIDDEN = 2816
GATE_W = 3 * D_MODEL
IN_COLS = 6664
MAIN_COLS = GATE_W + 7 * BRANCH_W
RMS_EPS = 1e-6
NEG_INF = -1e30

ADAM_LR = 0.001
ADAM_B1 = 0.9
ADAM_B2 = 0.999
ADAM_EPS = 1e-08
ADAM_WD = 0.01
ADAM_STEP = 10

LANES = 128
VMEM_LIMIT = 56 * 1024 * 1024
ATT_BLK = 256
CUM_BLK = 256

OFF_G, OFF_Q, OFF_K, OFF_V, OFF_U, OFF_CV, OFF_CB, OFF_CC = (
    0, 3072, 3584, 4096, 4608, 5120, 5632, 6144)


def _cp(sem=None):
    return pltpu.CompilerParams(dimension_semantics=sem, vmem_limit_bytes=VMEM_LIMIT)


def _sigmoid(z):
    return 1.0 / (1.0 + jnp.exp(-z))


def _matmul(a, b, *, mode, out_dtype, name, tm=2048, tn=512, tk=None, residual=None):
    if mode == "nn":
        (M, K), N = a.shape, b.shape[1]
    elif mode == "nt":
        (M, K), N = a.shape, b.shape[0]
    else:
        (K, M), N = a.shape, b.shape[1]
    tm, tn, tk = min(tm, M), min(tn, N), K if tk is None else min(tk, K)
    assert M % tm == 0 and N % tn == 0 and K % tk == 0, (name, M, N, K, tm, tn, tk)
    nk = K // tk
    if mode == "nn":
        a_spec = pl.BlockSpec((tm, tk), lambda i, j, k: (i, k))
        b_spec = pl.BlockSpec((tk, tn), lambda i, j, k: (k, j))
        dims = (((1,), (0,)), ((), ()))
    elif mode == "nt":
        a_spec = pl.BlockSpec((tm, tk), lambda i, j, k: (i, k))
        b_spec = pl.BlockSpec((tn, tk), lambda i, j, k: (j, k))
        dims = (((1,), (1,)), ((), ()))
    else:
        a_spec = pl.BlockSpec((tk, tm), lambda i, j, k: (k, i))
        b_spec = pl.BlockSpec((tk, tn), lambda i, j, k: (k, j))
        dims = (((0,), (0,)), ((), ()))
    o_spec = pl.BlockSpec((tm, tn), lambda i, j, k: (i, j))
    has_res = residual is not None

    def body(*refs):
        a_ref, b_ref = refs[:2]
        r_ref = refs[2] if has_res else None
        o_ref = refs[2 + has_res]

        def finish(acc):
            if has_res:
                acc = acc + r_ref[...].astype(F32)
            o_ref[...] = acc.astype(out_dtype)

        prod = lax.dot_general(a_ref[...], b_ref[...], dims, preferred_element_type=F32)
        if nk == 1:
            finish(prod)
            return
        acc_ref = refs[-1]
        k = pl.program_id(2)

        @pl.when(k == 0)
        def _():
            acc_ref[...] = prod

        @pl.when(jnp.logical_and(k > 0, k < nk - 1))
        def _():
            acc_ref[...] += prod

        @pl.when(k == nk - 1)
        def _():
            finish(acc_ref[...] + prod)

    in_specs = [a_spec, b_spec] + ([o_spec] if has_res else [])
    args = (a, b) + ((residual,) if has_res else ())
    return pl.pallas_call(
        body, name=name, grid=(M // tm, N // tn, nk), in_specs=in_specs, out_specs=o_spec,
        out_shape=jax.ShapeDtypeStruct((M, N), out_dtype),
        scratch_shapes=[pltpu.VMEM((tm, tn), F32)] if nk > 1 else [],
        compiler_params=_cp(("parallel", "parallel", "arbitrary")),
    )(*args)


def _rms_fwd(x, g, name):
    T, Dm = x.shape
    tm = min(512, T)

    def body(x_ref, g_ref, h_ref):
        xf = x_ref[...]
        r = lax.rsqrt(jnp.mean(xf * xf, axis=-1, keepdims=True) + RMS_EPS)
        h_ref[...] = ((xf * r) * g_ref[...]).astype(BF16)

    return pl.pallas_call(
        body, name=name, grid=(T // tm,),
        in_specs=[pl.BlockSpec((tm, Dm), lambda i: (i, 0)), pl.BlockSpec((1, Dm), lambda i: (0, 0))],
        out_specs=pl.BlockSpec((tm, Dm), lambda i: (i, 0)),
        out_shape=jax.ShapeDtypeStruct((T, Dm), BF16),
        compiler_params=_cp(("parallel",)),
    )(x, g.reshape(1, Dm))


def _rms_bwd(x, g, dh, dres, name):
    T, Dm = x.shape
    tm = min(512, T)

    def body(x_ref, g_ref, dh_ref, dres_ref, dx_ref, dxb_ref, dg_ref):
        i = pl.program_id(0)
        xf = x_ref[...]
        r = lax.rsqrt(jnp.mean(xf * xf, axis=-1, keepdims=True) + RMS_EPS)
        xn = xf * r
        dhf = dh_ref[...].astype(F32)
        dxn = dhf * g_ref[...]
        c = jnp.mean(dxn * xn, axis=-1, keepdims=True)
        dx = dres_ref[...] + r * (dxn - xn * c)
        dx_ref[...] = dx
        dxb_ref[...] = dx.astype(BF16)
        part = jnp.sum(dhf * xn, axis=0, keepdims=True)

        @pl.when(i == 0)
        def _():
            dg_ref[...] = part

        @pl.when(i > 0)
        def _():
            dg_ref[...] += part

    row = pl.BlockSpec((tm, Dm), lambda i: (i, 0))
    vec = pl.BlockSpec((1, Dm), lambda i: (0, 0))
    return pl.pallas_call(
        body, name=name, grid=(T // tm,), in_specs=[row, vec, row, row], out_specs=[row, row, vec],
        out_shape=[jax.ShapeDtypeStruct((T, Dm), F32), jax.ShapeDtypeStruct((T, Dm), BF16),
                   jax.ShapeDtypeStruct((1, Dm), F32)],
        compiler_params=_cp(("arbitrary",)),
    )(x, g.reshape(1, Dm), dh, dres)


def _loss_head(x, g, target, name):
    T, Dm = x.shape
    tm = min(512, T)

    def body(x_ref, g_ref, t_ref, loss_ref, dx_ref, dxb_ref, dg_ref):
        i = pl.program_id(0)
        xf = x_ref[...]
        gv = g_ref[...]
        r = lax.rsqrt(jnp.mean(xf * xf, axis=-1, keepdims=True) + RMS_EPS)
        xn = xf * r
        diff = xn * gv - t_ref[...]
        per_tok = jnp.mean(diff * diff, axis=-1, keepdims=True)
        lpart = 0.5 * jnp.sum(per_tok, axis=0, keepdims=True) + jnp.zeros((1, LANES), F32)
        dy = diff * (1.0 / Dm)
        dxn = dy * gv
        c = jnp.mean(dxn * xn, axis=-1, keepdims=True)
        dx = r * (dxn - xn * c)
        dx_ref[...] = dx
        dxb_ref[...] = dx.astype(BF16)
        part = jnp.sum(dy * xn, axis=0, keepdims=True)

        @pl.when(i == 0)
        def _():
            dg_ref[...] = part
            loss_ref[...] = lpart

        @pl.when(i > 0)
        def _():
            dg_ref[...] += part
            loss_ref[...] += lpart

    row = pl.BlockSpec((tm, Dm), lambda i: (i, 0))
    vec = pl.BlockSpec((1, Dm), lambda i: (0, 0))
    lsp = pl.BlockSpec((1, LANES), lambda i: (0, 0))
    return pl.pallas_call(
        body, name=name, grid=(T // tm,), in_specs=[row, vec, row], out_specs=[lsp, row, row, vec],
        out_shape=[jax.ShapeDtypeStruct((1, LANES), F32), jax.ShapeDtypeStruct((T, Dm), F32),
                   jax.ShapeDtypeStruct((T, Dm), BF16), jax.ShapeDtypeStruct((1, Dm), F32)],
        compiler_params=_cp(("arbitrary",)),
    )(x, g.reshape(1, Dm), target)


def _split_bf16(v):
    hi = v.astype(BF16)
    r1 = v - hi.astype(F32)
    mid = r1.astype(BF16)
    lo = (r1 - mid.astype(F32)).astype(BF16)
    return hi, mid, lo


def _tri_dot(tri, v):
    hi, mid, lo = _split_bf16(v)
    dot = functools.partial(jnp.dot, preferred_element_type=F32)
    return dot(tri, hi) + dot(tri, mid) + dot(tri, lo)


def _log_sigmoid(z):
    return jnp.minimum(z, 0.0) - jnp.log(1.0 + jnp.exp(-jnp.abs(z)))


def _fox_cumsum_fwd(f, bf, n_seq, name):
    T = f.shape[0]
    S = T // n_seq
    c = min(CUM_BLK, S)

    def body(f_ref, b_ref, out_ref):
        ri = lax.broadcasted_iota(jnp.int32, (c, c), 0)
        ci = lax.broadcasted_iota(jnp.int32, (c, c), 1)
        tri = (ri >= ci).astype(BF16)
        carry = jnp.zeros((1, LANES), F32)
        for j in range(S // c):
            lf = _log_sigmoid(f_ref[j * c:(j + 1) * c, :] + b_ref[...])
            out_ref[j * c:(j + 1) * c, :] = _tri_dot(tri, lf) + carry
            carry = carry + jnp.sum(lf, axis=0, keepdims=True)

    blk = pl.BlockSpec((S, LANES), lambda b: (b, 0))
    return pl.pallas_call(
        body, name=name, grid=(n_seq,), in_specs=[blk, pl.BlockSpec((1, LANES), lambda b: (0, 0))],
        out_specs=blk, out_shape=jax.ShapeDtypeStruct((T, LANES), F32),
        compiler_params=_cp(("parallel",)),
    )(f, bf)


def _fox_cumsum_bwd(f, bf, dF, n_seq, name):
    T = f.shape[0]
    S = T // n_seq
    c = min(CUM_BLK, S)

    def body(f_ref, b_ref, dF_ref, df_ref, db_ref):
        b = pl.program_id(0)
        ri = lax.broadcasted_iota(jnp.int32, (c, c), 0)
        ci = lax.broadcasted_iota(jnp.int32, (c, c), 1)
        tri = (ri <= ci).astype(BF16)
        carry = jnp.zeros((1, LANES), F32)
        dbp = jnp.zeros((1, LANES), F32)
        for j in reversed(range(S // c)):
            dFc = dF_ref[j * c:(j + 1) * c, :]
            dlf = _tri_dot(tri, dFc) + carry
            carry = carry + jnp.sum(dFc, axis=0, keepdims=True)
            z = f_ref[j * c:(j + 1) * c, :] + b_ref[...]
            dz = dlf * _sigmoid(-z)
            df_ref[j * c:(j + 1) * c, :] = dz.astype(BF16)
            dbp = dbp + jnp.sum(dz, axis=0, keepdims=True)

        @pl.when(b == 0)
        def _():
            db_ref[...] = dbp

        @pl.when(b > 0)
        def _():
            db_ref[...] += dbp

    blk = pl.BlockSpec((S, LANES), lambda b: (b, 0))
    vec = pl.BlockSpec((1, LANES), lambda b: (0, 0))
    return pl.pallas_call(
        body, name=name, grid=(n_seq,), in_specs=[blk, vec, blk], out_specs=[blk, vec],
        out_shape=[jax.ShapeDtypeStruct((T, LANES), BF16), jax.ShapeDtypeStruct((1, LANES), F32)],
        compiler_params=_cp(("arbitrary",)),
    )(f, bf, dF)


def _pair_masks():
    lane = lax.broadcasted_iota(jnp.int32, (1, LANES), 1)
    lo = lane < HEAD_DIM
    return lo, jnp.logical_not(lo)


def _attn_logits(q, k, fq, fk, sel, mask, scale):
    qm = jnp.where(sel, q, jnp.zeros_like(q))
    s = lax.dot_general(qm, k, (((1,), (1,)), ((), ())), preferred_element_type=F32) * scale
    s = s + fq - fk
    return jnp.where(mask, s, NEG_INF)


def _causal_mask(qi, ki, blk):
    row = qi * blk + lax.broadcasted_iota(jnp.int32, (blk, blk), 0)
    col = ki * blk + lax.broadcasted_iota(jnp.int32, (blk, blk), 1)
    return col <= row


def _attn_fwd(proj, Fq, Fk, n_seq, name):
    T = proj.shape[0]
    S = T // n_seq
    blk = min(ATT_BLK, S)
    nb = S // blk
    scale = HEAD_DIM ** -0.5
    qc, kc, vc = OFF_Q // LANES, OFF_K // LANES, OFF_V // LANES

    def body(q_ref, k_ref, v_ref, fq_ref, fk_ref, o_ref, o32_ref, lse_ref, m_s, l_s, acc_s):
        qi, ki = pl.program_id(2), pl.program_id(3)

        @pl.when(ki == 0)
        def _():
            m_s[...] = jnp.full_like(m_s, NEG_INF)
            l_s[...] = jnp.zeros_like(l_s)
            acc_s[...] = jnp.zeros_like(acc_s)

        @pl.when(ki <= qi)
        def _():
            q, k, v = q_ref[...], k_ref[...], v_ref[...]
            mask = _causal_mask(qi, ki, blk)
            for hh, sel in enumerate(_pair_masks()):
                s = _attn_logits(q, k, fq_ref[hh], fk_ref[hh], sel, mask, scale)
                m_prev = m_s[hh]
                m_new = jnp.maximum(m_prev, jnp.max(s, axis=-1, keepdims=True))
                alpha = jnp.exp(m_prev - m_new)
                p = jnp.exp(s - m_new)
                l_s[hh] = alpha * l_s[hh] + jnp.sum(p, axis=-1, keepdims=True)
                p_hi = p.astype(BF16)
                p_lo = (p - p_hi.astype(F32)).astype(BF16)
                pv = jnp.dot(p_hi, v, preferred_element_type=F32) + jnp.dot(p_lo, v, preferred_element_type=F32)
                acc_s[hh] = alpha * acc_s[hh] + pv
                m_s[hh] = m_new

        @pl.when(ki == qi)
        def _():
            lo, _ = _pair_masks()
            o = jnp.where(lo, acc_s[0] / l_s[0], acc_s[1] / l_s[1])
            o_ref[...] = o.astype(BF16)
            o32_ref[...] = o
            lse_ref[0] = m_s[0] + jnp.log(l_s[0])
            lse_ref[1] = m_s[1] + jnp.log(l_s[1])

    grid = (n_seq, N_HEADS // 2, nb, nb)
    return pl.pallas_call(
        body, name=name, grid=grid,
        in_specs=[
            pl.BlockSpec((blk, LANES), lambda b, j, qi, ki: (b * nb + qi, qc + j)),
            pl.BlockSpec((blk, LANES), lambda b, j, qi, ki: (b * nb + jnp.minimum(ki, qi), kc + j)),
            pl.BlockSpec((blk, LANES), lambda b, j, qi, ki: (b * nb + jnp.minimum(ki, qi), vc + j)),
            pl.BlockSpec((2, blk, 1), lambda b, j, qi, ki: (j, b * nb + qi, 0)),
            pl.BlockSpec((2, 1, blk), lambda b, j, qi, ki: (j, 0, b * nb + jnp.minimum(ki, qi))),
        ],
        out_specs=[
            pl.BlockSpec((blk, LANES), lambda b, j, qi, ki: (b * nb + qi, j)),
            pl.BlockSpec((blk, LANES), lambda b, j, qi, ki: (b * nb + qi, j)),
            pl.BlockSpec((2, blk, 1), lambda b, j, qi, ki: (j, b * nb + qi, 0)),
        ],
        out_shape=[jax.ShapeDtypeStruct((T, BRANCH_W), BF16), jax.ShapeDtypeStruct((T, BRANCH_W), F32),
                   jax.ShapeDtypeStruct((N_HEADS, T, 1), F32)],
        scratch_shapes=[pltpu.VMEM((2, blk, 1), F32), pltpu.VMEM((2, blk, 1), F32),
                        pltpu.VMEM((2, blk, LANES), F32)],
        compiler_params=_cp(("parallel", "parallel", "parallel", "arbitrary")),
    )(proj, proj, proj, Fq, Fk)


def _attn_delta(do, o, name):
    T = do.shape[0]
    tm = min(512, T)

    def body(do_ref, o_ref, d_ref):
        prod = do_ref[...].astype(F32) * o_ref[...].astype(F32)
        lo, hi = _pair_masks()
        for j in range(N_HEADS // 2):
            pj = prod[:, j * LANES:(j + 1) * LANES]
            d_ref[2 * j] = jnp.sum(jnp.where(lo, pj, 0.0), axis=-1, keepdims=True)
            d_ref[2 * j + 1] = jnp.sum(jnp.where(hi, pj, 0.0), axis=-1, keepdims=True)

    row = pl.BlockSpec((tm, BRANCH_W), lambda i: (i, 0))
    return pl.pallas_call(
        body, name=name, grid=(T // tm,), in_specs=[row, row],
        out_specs=pl.BlockSpec((N_HEADS, tm, 1), lambda i: (0, i, 0)),
        out_shape=jax.ShapeDtypeStruct((N_HEADS, T, 1), F32),
        compiler_params=_cp(("parallel",)),
    )(do, o)


def _attn_bwd_dq(proj, do, lse, delta, Fq, Fk, n_seq, name):
    T = proj.shape[0]
    S = T // n_seq
    blk = min(ATT_BLK, S)
    nb = S // blk
    scale = HEAD_DIM ** -0.5
    qc, kc, vc = OFF_Q // LANES, OFF_K // LANES, OFF_V // LANES

    def body(q_ref, k_ref, v_ref, do_ref, lse_ref, dl_ref, fq_ref, fk_ref, dq_ref, acc_s):
        qi, ki = pl.program_id(2), pl.program_id(3)

        @pl.when(ki == 0)
        def _():
            acc_s[...] = jnp.zeros_like(acc_s)

        @pl.when(ki <= qi)
        def _():
            q, k, v, do_ = q_ref[...], k_ref[...], v_ref[...], do_ref[...]
            mask = _causal_mask(qi, ki, blk)
            for hh, sel in enumerate(_pair_masks()):
                s = _attn_logits(q, k, fq_ref[hh], fk_ref[hh], sel, mask, scale)
                p = jnp.exp(s - lse_ref[hh])
                dom = jnp.where(sel, do_, jnp.zeros_like(do_))
                dp = lax.dot_general(dom, v, (((1,), (1,)), ((), ())), preferred_element_type=F32)
                ds = p * (dp - dl_ref[hh])
                acc_s[hh] += jnp.dot(ds.astype(BF16), k, preferred_element_type=F32)

        @pl.when(ki == qi)
        def _():
            lo, _ = _pair_masks()
            dq_ref[...] = (jnp.where(lo, acc_s[0], acc_s[1]) * scale).astype(BF16)

    qmap = lambda b, j, qi, ki: (b * nb + qi, j)
    col1 = pl.BlockSpec((2, blk, 1), lambda b, j, qi, ki: (j, b * nb + qi, 0))
    return pl.pallas_call(
        body, name=name, grid=(n_seq, N_HEADS // 2, nb, nb),
        in_specs=[
            pl.BlockSpec((blk, LANES), lambda b, j, qi, ki: (b * nb + qi, qc + j)),
            pl.BlockSpec((blk, LANES), lambda b, j, qi, ki: (b * nb + jnp.minimum(ki, qi), kc + j)),
            pl.BlockSpec((blk, LANES), lambda b, j, qi, ki: (b * nb + jnp.minimum(ki, qi), vc + j)),
            pl.BlockSpec((blk, LANES), qmap),
            col1, col1, col1,
            pl.BlockSpec((2, 1, blk), lambda b, j, qi, ki: (j, 0, b * nb + jnp.minimum(ki, qi))),
        ],
        out_specs=pl.BlockSpec((blk, LANES), qmap),
        out_shape=jax.ShapeDtypeStruct((T, BRANCH_W), BF16),
        scratch_shapes=[pltpu.VMEM((2, blk, LANES), F32)],
        compiler_params=_cp(("parallel", "parallel", "parallel", "arbitrary")),
    )(proj, proj, proj, do, lse, delta, Fq, Fk)


def _attn_bwd_dkv(proj, do, lse, delta, Fq, Fk, n_seq, name):
    T = proj.shape[0]
    S = T // n_seq
    blk = min(ATT_BLK, S)
    nb = S // blk
    scale = HEAD_DIM ** -0.5
    qc, kc, vc = OFF_Q // LANES, OFF_K // LANES, OFF_V // LANES
    tdot = functools.partial(lax.dot_general, dimension_numbers=(((0,), (0,)), ((), ())),
                             preferred_element_type=F32)

    def body(q_ref, k_ref, v_ref, do_ref, lse_ref, dl_ref, fq_ref, fk_ref, dk_ref, dv_ref, dfk_ref,
             dk_s, dv_s, df_s):
        ki, qi = pl.program_id(2), pl.program_id(3)

        @pl.when(qi == 0)
        def _():
            dk_s[...] = jnp.zeros_like(dk_s)
            dv_s[...] = jnp.zeros_like(dv_s)
            df_s[...] = jnp.zeros_like(df_s)

        @pl.when(qi >= ki)
        def _():
            q, k, v, do_ = q_ref[...], k_ref[...], v_ref[...], do_ref[...]
            mask = _causal_mask(qi, ki, blk)
            for hh, sel in enumerate(_pair_masks()):
                s = _attn_logits(q, k, fq_ref[hh], fk_ref[hh], sel, mask, scale)
                p = jnp.exp(s - lse_ref[hh])
                dv_s[hh] += tdot(p.astype(BF16), do_)
                dom = jnp.where(sel, do_, jnp.zeros_like(do_))
                dp = lax.dot_general(dom, v, (((1,), (1,)), ((), ())), preferred_element_type=F32)
                ds = p * (dp - dl_ref[hh])
                dk_s[hh] += tdot(ds.astype(BF16), q)
                df_s[hh] -= jnp.sum(ds, axis=0, keepdims=True)

        @pl.when(qi == nb - 1)
        def _():
            lo, _ = _pair_masks()
            dk_ref[...] = (jnp.where(lo, dk_s[0], dk_s[1]) * scale).astype(BF16)
            dv_ref[...] = jnp.where(lo, dv_s[0], dv_s[1]).astype(BF16)
            dfk_ref[...] = df_s[...]

    kmap = lambda b, j, ki, qi: (b * nb + ki, j)
    col1 = pl.BlockSpec((2, blk, 1), lambda b, j, ki, qi: (j, b * nb + jnp.maximum(qi, ki), 0))
    rowk = pl.BlockSpec((2, 1, blk), lambda b, j, ki, qi: (j, 0, b * nb + ki))
    return pl.pallas_call(
        body, name=name, grid=(n_seq, N_HEADS // 2, nb, nb),
        in_specs=[
            pl.BlockSpec((blk, LANES), lambda b, j, ki, qi: (b * nb + jnp.maximum(qi, ki), qc + j)),
            pl.BlockSpec((blk, LANES), lambda b, j, ki, qi: (b * nb + ki, kc + j)),
            pl.BlockSpec((blk, LANES), lambda b, j, ki, qi: (b * nb + ki, vc + j)),
            pl.BlockSpec((blk, LANES), lambda b, j, ki, qi: (b * nb + jnp.maximum(qi, ki), j)),
            col1, col1, col1, rowk,
        ],
        out_specs=[pl.BlockSpec((blk, LANES), kmap), pl.BlockSpec((blk, LANES), kmap), rowk],
        out_shape=[jax.ShapeDtypeStruct((T, BRANCH_W), BF16), jax.ShapeDtypeStruct((T, BRANCH_W), BF16),
                   jax.ShapeDtypeStruct((N_HEADS, 1, T), F32)],
        scratch_shapes=[pltpu.VMEM((2, blk, LANES), F32), pltpu.VMEM((2, blk, LANES), F32),
                        pltpu.VMEM((2, 1, blk), F32)],
        compiler_params=_cp(("parallel", "parallel", "parallel", "arbitrary")),
    )(proj, proj, proj, do, lse, delta, Fq, Fk)


AUG0 = HEAD_DIM
Q_TILE, K_CHUNK, ROW_GROUP = 512, 256, 64


def _fox_prep(f, bf, proj, n_seq, name):
    T = f.shape[0]
    S = T // n_seq
    c = min(CUM_BLK, S)

    def body(f_ref, b_ref, q_ref, k_ref, v_ref, qa_ref, ka_ref, va_ref):
        ri = lax.broadcasted_iota(jnp.int32, (c, c), 0)
        ci = lax.broadcasted_iota(jnp.int32, (c, c), 1)
        tri = (ri >= ci).astype(BF16)
        lane = lax.broadcasted_iota(jnp.int32, (c, LANES), 1)
        carry = jnp.zeros((1, LANES), F32)
        for j in range(S // c):
            rows = slice(j * c, (j + 1) * c)
            lf = _log_sigmoid(f_ref[rows, :] + b_ref[...])
            Fc = _tri_dot(tri, lf) + carry
            carry = carry + jnp.sum(lf, axis=0, keepdims=True)
            for h in range(N_HEADS):
                col = jnp.sum(jnp.where(lane == h, Fc, 0.0), axis=-1, keepdims=True)
                hi = col.astype(BF16).astype(F32)
                r1 = col - hi
                mid = r1.astype(BF16).astype(F32)
                lo = r1 - mid
                ones_q = jnp.logical_and(lane >= AUG0 + 3, lane < AUG0 + 6)
                ones_k = jnp.logical_and(lane >= AUG0, lane < AUG0 + 3)
                aug_q = jnp.where(lane == AUG0, hi, jnp.where(lane == AUG0 + 1, mid, jnp.where(
                    lane == AUG0 + 2, lo, jnp.where(ones_q, 1.0, 0.0))))
                aug_k = jnp.where(lane == AUG0 + 3, -hi, jnp.where(lane == AUG0 + 4, -mid, jnp.where(
                    lane == AUG0 + 5, -lo, jnp.where(ones_k, 1.0, 0.0))))
                pair = slice((h // 2) * LANES, (h // 2 + 1) * LANES)
                qp, kp = q_ref[rows, pair].astype(F32), k_ref[rows, pair].astype(F32)
                vp = v_ref[rows, pair].astype(F32)
                if h % 2:
                    qp, kp, vp = (pltpu.roll(a, HEAD_DIM, 1) for a in (qp, kp, vp))
                out = slice(h * LANES, (h + 1) * LANES)
                qa_ref[rows, out] = jnp.where(lane < HEAD_DIM, qp * (HEAD_DIM ** -0.5), aug_q).astype(BF16)
                ka_ref[rows, out] = jnp.where(lane < HEAD_DIM, kp, aug_k).astype(BF16)
                va_ref[rows, out] = jnp.where(lane < HEAD_DIM, vp, jnp.where(lane == AUG0, 1.0, 0.0)).astype(BF16)

    fblk = pl.BlockSpec((S, LANES), lambda b: (b, 0))
    out = pl.BlockSpec((S, N_HEADS * LANES), lambda b: (b, 0))
    sh = jax.ShapeDtypeStruct((T, N_HEADS * LANES), BF16)
    return pl.pallas_call(
        body, name=name, grid=(n_seq,),
        in_specs=[fblk, pl.BlockSpec((1, LANES), lambda b: (0, 0)),
                  pl.BlockSpec((S, BRANCH_W), lambda b: (b, OFF_Q // BRANCH_W)),
                  pl.BlockSpec((S, BRANCH_W), lambda b: (b, OFF_K // BRANCH_W)),
                  pl.BlockSpec((S, BRANCH_W), lambda b: (b, OFF_V // BRANCH_W))],
        out_specs=[out, out, out], out_shape=[sh, sh, sh],
        compiler_params=_cp(("parallel",)),
    )(f, bf, proj, proj, proj)


def _band_mask(q0, k0, nq, nk):
    row = q0 + lax.broadcasted_iota(jnp.int32, (nq, nk), 0)
    col = k0 + lax.broadcasted_iota(jnp.int32, (nq, nk), 1)
    return col <= row


_NT = (((1,), (1,)), ((), ()))
_TN = (((0,), (0,)), ((), ()))


def _attn_fwd2(qa, ka, va, n_seq, name):
    T = qa.shape[0]
    S = T // n_seq
    tq, tk, rg = min(Q_TILE, S), min(K_CHUNK, S), ROW_GROUP
    nq, per = S // tq, tq // tk

    def body(q_ref, k_ref, v_ref, o_ref, o32_ref, lse_ref, phi_s, plo_s, mp_s, m_s, acc_s):
        qi = pl.program_id(2)
        mp_s[...] = jnp.full_like(mp_s, NEG_INF)
        acc_s[...] = jnp.zeros_like(acc_s)

        def scores(kc, hh):
            k0 = pl.multiple_of(kc * tk, tk)
            hl = slice(hh * LANES, (hh + 1) * LANES)
            return k0, lax.dot_general(q_ref[:, hl], k_ref[pl.ds(k0, tk), hl], _NT, preferred_element_type=F32)

        def max_chunk(kc, masked):
            for hh in range(2):
                k0, s_all = scores(kc, hh)
                for r in range(tq // rg):
                    rows = slice(r * rg, (r + 1) * rg)
                    s = s_all[rows, :]
                    if masked:
                        s = jnp.where(_band_mask(qi * tq + r * rg, k0, rg, tk), s, NEG_INF)
                    part = s[:, :LANES]
                    for c in range(1, tk // LANES):
                        part = jnp.maximum(part, s[:, c * LANES:(c + 1) * LANES])
                    mp_s[hh, rows, :] = jnp.maximum(mp_s[hh, rows, :], part)

        def sum_chunk(kc, masked):
            for hh in range(2):
                k0, s_all = scores(kc, hh)
                hl = slice(hh * LANES, (hh + 1) * LANES)
                v = v_ref[pl.ds(k0, tk), hl]
                for r in range(tq // rg):
                    rows = slice(r * rg, (r + 1) * rg)
                    p = jnp.exp(s_all[rows, :] - m_s[hh, rows])
                    if masked:
                        p = jnp.where(_band_mask(qi * tq + r * rg, k0, rg, tk), p, 0.0)
                    p_hi = p.astype(BF16)
                    phi_s[hh, rows, :] = p_hi
                    plo_s[hh, rows, :] = (p - p_hi.astype(F32)).astype(BF16)
                acc_s[hh] += (jnp.dot(phi_s[hh], v, preferred_element_type=F32)
                              + jnp.dot(plo_s[hh], v, preferred_element_type=F32))

        def sweep(chunk):
            def unmasked(kc, carry):
                chunk(kc, False)
                return carry

            lax.fori_loop(0, qi * per, unmasked, 0)
            for d in range(per):
                chunk(qi * per + d, True)

        sweep(max_chunk)
        m_s[...] = jnp.max(mp_s[...], axis=-1, keepdims=True)
        sweep(sum_chunk)

        lane = lax.broadcasted_iota(jnp.int32, (1, LANES), 1)
        outs = []
        for hh in range(2):
            acc = acc_s[hh]
            l = jnp.sum(jnp.where(lane == AUG0, acc, 0.0), axis=-1, keepdims=True)
            lse_ref[hh] = m_s[hh] + jnp.log(l)
            outs.append(acc / l)
        o = jnp.where(lane < HEAD_DIM, outs[0], pltpu.roll(outs[1], HEAD_DIM, 1))
        o_ref[...] = o.astype(BF16)
        o32_ref[...] = o

    qmap = lambda b, j, qi: (b * nq + qi, j)
    omap = lambda b, j, qi: (b * nq + qi, j)
    kv = pl.BlockSpec((S, 2 * LANES), lambda b, j, qi: (b, j))
    return pl.pallas_call(
        body, name=name, grid=(n_seq, N_HEADS // 2, nq),
        in_specs=[pl.BlockSpec((tq, 2 * LANES), qmap), kv, kv],
        out_specs=[pl.BlockSpec((tq, LANES), omap), pl.BlockSpec((tq, LANES), omap),
                   pl.BlockSpec((2, tq, 1), lambda b, j, qi: (j, b * nq + qi, 0))],
        out_shape=[jax.ShapeDtypeStruct((T, BRANCH_W), BF16), jax.ShapeDtypeStruct((T, BRANCH_W), F32),
                   jax.ShapeDtypeStruct((N_HEADS, T, 1), F32)],
        scratch_shapes=[pltpu.VMEM((2, tq, tk), BF16), pltpu.VMEM((2, tq, tk), BF16),
                        pltpu.VMEM((2, tq, LANES), F32), pltpu.VMEM((2, tq, 1), F32),
                        pltpu.VMEM((2, tq, LANES), F32)],
        compiler_params=_cp(("parallel", "parallel", "parallel")),
    )(qa, ka, va)


def _attn_bwd_dq2(qa, ka, proj, do, lse, delta, n_seq, name):
    T = qa.shape[0]
    S = T // n_seq
    tq, tk, rg = min(Q_TILE, S), min(K_CHUNK, S), ROW_GROUP
    nq, per = S // tq, tq // tk
    vc = OFF_V // LANES

    def body(q_ref, k_ref, v_ref, do_ref, lse_ref, dl_ref, dq_ref, ds_s, acc_s):
        qi = pl.program_id(2)
        acc_s[...] = jnp.zeros_like(acc_s)
        sels = _pair_masks()

        def chunk(kc, masked):
            k0 = pl.multiple_of(kc * tk, tk)
            v = v_ref[pl.ds(k0, tk), :]
            for hh in range(2):
                hl = slice(hh * LANES, (hh + 1) * LANES)
                kh = k_ref[pl.ds(k0, tk), hl]
                s_all = lax.dot_general(q_ref[:, hl], kh, _NT, preferred_element_type=F32)
                dom = jnp.where(sels[hh], do_ref[...], jnp.zeros_like(do_ref[...]))
                dp_all = lax.dot_general(dom, v, _NT, preferred_element_type=F32)
                for r in range(tq // rg):
                    rows = slice(r * rg, (r + 1) * rg)
                    p = jnp.exp(s_all[rows, :] - lse_ref[hh, rows])
                    if masked:
                        p = jnp.where(_band_mask(qi * tq + r * rg, k0, rg, tk), p, 0.0)
                    ds_s[hh, rows, :] = (p * (dp_all[rows, :] - dl_ref[hh, rows])).astype(BF16)
                acc_s[hh] += jnp.dot(ds_s[hh], kh, preferred_element_type=F32)

        def unmasked(kc, carry):
            chunk(kc, False)
            return carry

        lax.fori_loop(0, qi * per, unmasked, 0)
        for d in range(per):
            chunk(qi * per + d, True)
        dq = jnp.where(sels[0], acc_s[0], pltpu.roll(acc_s[1], HEAD_DIM, 1))
        dq_ref[...] = (dq * (HEAD_DIM ** -0.5)).astype(BF16)

    qmap = lambda b, j, qi: (b * nq + qi, j)
    col1 = pl.BlockSpec((2, tq, 1), lambda b, j, qi: (j, b * nq + qi, 0))
    return pl.pallas_call(
        body, name=name, grid=(n_seq, N_HEADS // 2, nq),
        in_specs=[pl.BlockSpec((tq, 2 * LANES), qmap),
                  pl.BlockSpec((S, 2 * LANES), lambda b, j, qi: (b, j)),
                  pl.BlockSpec((S, LANES), lambda b, j, qi: (b, vc + j)),
                  pl.BlockSpec((tq, LANES), qmap), col1, col1],
        out_specs=pl.BlockSpec((tq, LANES), qmap),
        out_shape=jax.ShapeDtypeStruct((T, BRANCH_W), BF16),
        scratch_shapes=[pltpu.VMEM((2, tq, tk), BF16), pltpu.VMEM((2, tq, LANES), F32)],
        compiler_params=_cp(("parallel", "parallel", "parallel")),
    )(qa, ka, proj, do, lse, delta)


def _attn_bwd_dkv2(qa, ka, proj, do, lse, delta, n_seq, name):
    T = qa.shape[0]
    S = T // n_seq
    tkt, tqc, rg = min(Q_TILE, S), min(K_CHUNK, S), ROW_GROUP // 2
    nk, per, nqc = S // tkt, tkt // tqc, S // tqc
    vc = OFF_V // LANES

    def body(q_ref, k_ref, v_ref, do_ref, lse_ref, dl_ref, dk_ref, dv_ref, dfk_ref,
             p_s, ds_s, dk_s, dv_s, df_s):
        ki = pl.program_id(2)
        dk_s[...] = jnp.zeros_like(dk_s)
        dv_s[...] = jnp.zeros_like(dv_s)
        df_s[...] = jnp.zeros_like(df_s)
        sels = _pair_masks()
        v = v_ref[...]

        def chunk(qc, masked):
            q0 = pl.multiple_of(qc * tqc, tqc)
            do_ = do_ref[pl.ds(q0, tqc), :]
            for hh in range(2):
                hl = slice(hh * LANES, (hh + 1) * LANES)
                qh = q_ref[pl.ds(q0, tqc), hl]
                s_all = lax.dot_general(qh, k_ref[:, hl], _NT, preferred_element_type=F32)
                dom = jnp.where(sels[hh], do_, jnp.zeros_like(do_))
                dp_all = lax.dot_general(dom, v, _NT, preferred_element_type=F32)
                dfp = jnp.zeros((1, tkt), F32)
                for r in range(tqc // rg):
                    rows = slice(r * rg, (r + 1) * rg)
                    qrows = pl.ds(q0 + r * rg, rg)
                    p = jnp.exp(s_all[rows, :] - lse_ref[hh, qrows])
                    if masked:
                        p = jnp.where(_band_mask(q0 + r * rg, ki * tkt, rg, tkt), p, 0.0)
                    ds = p * (dp_all[rows, :] - dl_ref[hh, qrows])
                    p_s[hh, rows, :] = p.astype(BF16)
                    ds_s[hh, rows, :] = ds.astype(BF16)
                    dfp = dfp + jnp.sum(ds, axis=0, keepdims=True)
                df_s[hh] -= dfp
                dv_s[hh] += lax.dot_general(p_s[hh], do_, _TN, preferred_element_type=F32)
                dk_s[hh] += lax.dot_general(ds_s[hh], qh, _TN, preferred_element_type=F32)

        for d in range(per):
            chunk(ki * per + d, True)

        def unmasked(qc, carry):
            chunk(qc, False)
            return carry

        lax.fori_loop((ki + 1) * per, nqc, unmasked, 0)
        dk_ref[...] = jnp.where(sels[0], dk_s[0], pltpu.roll(dk_s[1], HEAD_DIM, 1)).astype(BF16)
        dv_ref[...] = jnp.where(sels[0], dv_s[0], dv_s[1]).astype(BF16)
        dfk_ref[...] = df_s[...]

    kmap = lambda b, j, ki: (b * nk + ki, j)
    col1 = pl.BlockSpec((2, S, 1), lambda b, j, ki: (j, b, 0))
    rowk = pl.BlockSpec((2, 1, tkt), lambda b, j, ki: (j, 0, b * nk + ki))
    return pl.pallas_call(
        body, name=name, grid=(n_seq, N_HEADS // 2, nk),
        in_specs=[pl.BlockSpec((S, 2 * LANES), lambda b, j, ki: (b, j)),
                  pl.BlockSpec((tkt, 2 * LANES), kmap),
                  pl.BlockSpec((tkt, LANES), lambda b, j, ki: (b * nk + ki, vc + j)),
                  pl.BlockSpec((S, LANES), lambda b, j, ki: (b, j)), col1, col1],
        out_specs=[pl.BlockSpec((tkt, LANES), kmap), pl.BlockSpec((tkt, LANES), kmap), rowk],
        out_shape=[jax.ShapeDtypeStruct((T, BRANCH_W), BF16), jax.ShapeDtypeStruct((T, BRANCH_W), BF16),
                   jax.ShapeDtypeStruct((N_HEADS, 1, T), F32)],
        scratch_shapes=[pltpu.VMEM((2, tqc, tkt), BF16), pltpu.VMEM((2, tqc, tkt), BF16),
                        pltpu.VMEM((2, tkt, LANES), F32),
                        pltpu.VMEM((2, tkt, LANES), F32), pltpu.VMEM((2, 1, tkt), F32)],
        compiler_params=_cp(("parallel", "parallel", "parallel")),
    )(qa, ka, proj, do, lse, delta)


def _shift_down(v, k, row):
    return jnp.where(row >= k, pltpu.roll(v, k, 0), 0.0)


def _shift_up(v, k, row, S):
    return jnp.where(row < S - k, pltpu.roll(v, S - k, 0), 0.0)


def _pool_diff(uf, w, row):
    acc, k = uf, 1
    while k < w:
        acc = acc + _shift_down(acc, k, row)
        k *= 2
    n = jnp.minimum(row + 1, w).astype(F32)
    return acc / n - uf


def _pool_fwd(proj, pool_w, pool_scale, n_seq, name):
    T = proj.shape[0]
    S = T // n_seq

    def body(u_ref, w_ref, sc_ref, o_ref):
        g = pl.program_id(1)
        row = lax.broadcasted_iota(jnp.int32, (S, POOL_GD), 0)
        uf = u_ref[...].astype(F32)
        d = _pool_diff(uf, POOL_WINDOWS[0], row)
        for gi in range(1, len(POOL_WINDOWS)):
            d = jnp.where(g == gi, _pool_diff(uf, POOL_WINDOWS[gi], row), d)
        e = jnp.dot(d.astype(BF16), w_ref[0], preferred_element_type=F32)
        o_ref[...] = (e * sc_ref[...]).astype(BF16)

    uc = OFF_U // POOL_GD
    return pl.pallas_call(
        body, name=name, grid=(n_seq, len(POOL_WINDOWS)),
        in_specs=[pl.BlockSpec((S, POOL_GD), lambda b, g: (b, uc + g)),
                  pl.BlockSpec((1, POOL_GD, POOL_GD), lambda b, g: (g, 0, 0)),
                  pl.BlockSpec((1, POOL_GD), lambda b, g: (0, g))],
        out_specs=pl.BlockSpec((S, POOL_GD), lambda b, g: (b, g)),
        out_shape=jax.ShapeDtypeStruct((T, BRANCH_W), BF16),
        compiler_params=_cp(("parallel", "parallel")),
    )(proj, pool_w, pool_scale)


def _pool_bwd(proj, dout, pool_w, pool_scale, n_seq, name):
    T = proj.shape[0]
    S = T // n_seq

    def body(u_ref, do_ref, w_ref, sc_ref, du_ref, dw_ref, dsc_ref):
        g, b = pl.program_id(0), pl.program_id(1)
        row = lax.broadcasted_iota(jnp.int32, (S, POOL_GD), 0)
        uf = u_ref[...].astype(F32)
        d = _pool_diff(uf, POOL_WINDOWS[0], row)
        for gi in range(1, len(POOL_WINDOWS)):
            d = jnp.where(g == gi, _pool_diff(uf, POOL_WINDOWS[gi], row), d)
        db16 = d.astype(BF16)
        w = w_ref[0]
        e = jnp.dot(db16, w, preferred_element_type=F32)
        dof = do_ref[...].astype(F32)
        dsc = jnp.sum(dof * e, axis=0, keepdims=True)
        de = (dof * sc_ref[...]).astype(BF16)
        dd = lax.dot_general(de, w, (((1,), (1,)), ((), ())), preferred_element_type=F32)
        dw = lax.dot_general(db16, de, (((0,), (0,)), ((), ())), preferred_element_type=F32)
        du = jnp.zeros_like(dd)
        for gi, wlen in enumerate(POOL_WINDOWS):
            n = jnp.minimum(row + 1, wlen).astype(F32)
            acc, k = dd / n, 1
            while k < wlen:
                acc = acc + _shift_up(acc, k, row, S)
                k *= 2
            du = jnp.where(g == gi, acc - dd, du)
        du_ref[...] = du.astype(BF16)

        @pl.when(b == 0)
        def _():
            dw_ref[0] = dw
            dsc_ref[...] = dsc

        @pl.when(b > 0)
        def _():
            dw_ref[0] += dw
            dsc_ref[...] += dsc

    uc = OFF_U // POOL_GD
    return pl.pallas_call(
        body, name=name, grid=(len(POOL_WINDOWS), n_seq),
        in_specs=[pl.BlockSpec((S, POOL_GD), lambda g, b: (b, uc + g)),
                  pl.BlockSpec((S, POOL_GD), lambda g, b: (b, g)),
                  pl.BlockSpec((1, POOL_GD, POOL_GD), lambda g, b: (g, 0, 0)),
                  pl.BlockSpec((1, POOL_GD), lambda g, b: (0, g))],
        out_specs=[pl.BlockSpec((S, POOL_GD), lambda g, b: (b, g)),
                   pl.BlockSpec((1, POOL_GD, POOL_GD), lambda g, b: (g, 0, 0)),
                   pl.BlockSpec((1, POOL_GD), lambda g, b: (0, g))],
        out_shape=[jax.ShapeDtypeStruct((T, BRANCH_W), BF16),
                   jax.ShapeDtypeStruct((len(POOL_WINDOWS), POOL_GD, POOL_GD), F32),
                   jax.ShapeDtypeStruct((1, BRANCH_W), F32)],
        compiler_params=_cp(("parallel", "arbitrary")),
    )(proj, dout, pool_w, pool_scale)


def _conv_fwd(proj, conv_w, n_seq, name):
    T = proj.shape[0]
    S = T // n_seq
    nc = BRANCH_W // LANES

    def body(cv_ref, cb_ref, cc_ref, w_ref, o_ref):
        row = lax.broadcasted_iota(jnp.int32, (S, LANES), 0)
        z = cc_ref[...].astype(F32) * cv_ref[...].astype(F32)
        w = w_ref[...]
        y = w[0:1] * _shift_down(z, 2, row) + w[1:2] * _shift_down(z, 1, row) + w[2:3] * z
        o_ref[...] = (cb_ref[...].astype(F32) * y).astype(BF16)

    def col(off):
        return pl.BlockSpec((S, LANES), lambda b, j: (b, off // LANES + j))

    return pl.pallas_call(
        body, name=name, grid=(n_seq, nc),
        in_specs=[col(OFF_CV), col(OFF_CB), col(OFF_CC), pl.BlockSpec((CONV_K, LANES), lambda b, j: (0, j))],
        out_specs=pl.BlockSpec((S, LANES), lambda b, j: (b, j)),
        out_shape=jax.ShapeDtypeStruct((T, BRANCH_W), BF16),
        compiler_params=_cp(("parallel", "parallel")),
    )(proj, proj, proj, conv_w)


def _conv_bwd(proj, dout, conv_w, n_seq, name):
    T = proj.shape[0]
    S = T // n_seq
    nc = BRANCH_W // LANES

    def body(cv_ref, cb_ref, cc_ref, do_ref, w_ref, dcv_ref, dcb_ref, dcc_ref, dw_ref):
        b = pl.program_id(1)
        row = lax.broadcasted_iota(jnp.int32, (S, LANES), 0)
        cv, cb, cc = cv_ref[...].astype(F32), cb_ref[...].astype(F32), cc_ref[...].astype(F32)
        dof = do_ref[...].astype(F32)
        w = w_ref[...]
        z = cc * cv
        z1, z2 = _shift_down(z, 1, row), _shift_down(z, 2, row)
        y = w[0:1] * z2 + w[1:2] * z1 + w[2:3] * z
        dcb_ref[...] = (dof * y).astype(BF16)
        dy = dof * cb
        dz = w[2:3] * dy + w[1:2] * _shift_up(dy, 1, row, S) + w[0:1] * _shift_up(dy, 2, row, S)
        dcc_ref[...] = (dz * cv).astype(BF16)
        dcv_ref[...] = (dz * cc).astype(BF16)
        dws = [jnp.sum(dy * zk, axis=0, keepdims=True) for zk in (z2, z1, z)]

        @pl.when(b == 0)
        def _():
            for kk in range(CONV_K):
                dw_ref[kk:kk + 1, :] = dws[kk]

        @pl.when(b > 0)
        def _():
            for kk in range(CONV_K):
                dw_ref[kk:kk + 1, :] += dws[kk]

    def col(off):
        return pl.BlockSpec((S, LANES), lambda j, b: (b, off // LANES + j))

    out = pl.BlockSpec((S, LANES), lambda j, b: (b, j))
    wsp = pl.BlockSpec((CONV_K, LANES), lambda j, b: (0, j))
    act = jax.ShapeDtypeStruct((T, BRANCH_W), BF16)
    return pl.pallas_call(
        body, name=name, grid=(nc, n_seq),
        in_specs=[col(OFF_CV), col(OFF_CB), col(OFF_CC), out, wsp],
        out_specs=[out, out, out, wsp],
        out_shape=[act, act, act, jax.ShapeDtypeStruct((CONV_K, BRANCH_W), F32)],
        compiler_params=_cp(("parallel", "arbitrary")),
    )(proj, proj, proj, dout, conv_w)


def _mix_fwd(oa, ob, oc, wpa, wpp, wpc, proj, b_gate, name):
    T = oa.shape[0]
    tm = min(256, T)

    def body(oa_ref, ob_ref, oc_ref, wa_ref, wp_ref, wc_ref, g_ref, bg_ref, o_ref):
        acc = jnp.zeros((tm, D_MODEL), F32)
        for i, (x_ref, w_ref) in enumerate(((oa_ref, wa_ref), (ob_ref, wp_ref), (oc_ref, wc_ref))):
            y = jnp.dot(x_ref[...], w_ref[...], preferred_element_type=F32)
            sl = slice(i * D_MODEL, (i + 1) * D_MODEL)
            acc = acc + _sigmoid(g_ref[:, sl].astype(F32) + bg_ref[:, sl]) * y
        o_ref[...] = acc.astype(BF16)

    br = pl.BlockSpec((tm, BRANCH_W), lambda i: (i, 0))
    wsp = pl.BlockSpec((BRANCH_W, D_MODEL), lambda i: (0, 0))
    return pl.pallas_call(
        body, name=name, grid=(T // tm,),
        in_specs=[br, br, br, wsp, wsp, wsp, pl.BlockSpec((tm, GATE_W), lambda i: (i, 0)),
                  pl.BlockSpec((1, GATE_W), lambda i: (0, 0))],
        out_specs=pl.BlockSpec((tm, D_MODEL), lambda i: (i, 0)),
        out_shape=jax.ShapeDtypeStruct((T, D_MODEL), BF16),
        compiler_params=_cp(("parallel",)),
    )(oa, ob, oc, wpa, wpp, wpc, proj, b_gate)


def _mix_bwd(oa, ob, oc, wpa, wpp, wpc, proj, b_gate, dmixed, name):
    T = oa.shape[0]
    tm = min(256, T)

    def body(oa_ref, ob_ref, oc_ref, wa_ref, wp_ref, wc_ref, g_ref, bg_ref, dm_ref,
             dya_ref, dyb_ref, dyc_ref, dg_ref, dbg_ref):
        i0 = pl.program_id(0)
        dm = dm_ref[...].astype(F32)
        parts = []
        for i, (x_ref, w_ref, dy_ref) in enumerate(((oa_ref, wa_ref, dya_ref), (ob_ref, wp_ref, dyb_ref),
                                                    (oc_ref, wc_ref, dyc_ref))):
            y = jnp.dot(x_ref[...], w_ref[...], preferred_element_type=F32)
            sl = slice(i * D_MODEL, (i + 1) * D_MODEL)
            gate = _sigmoid(g_ref[:, sl].astype(F32) + bg_ref[:, sl])
            dy_ref[...] = (dm * gate).astype(BF16)
            dgl = dm * y * gate * (1.0 - gate)
            dg_ref[:, sl] = dgl.astype(BF16)
            parts.append(jnp.sum(dgl, axis=0, keepdims=True))

        @pl.when(i0 == 0)
        def _():
            for i in range(3):
                dbg_ref[:, i * D_MODEL:(i + 1) * D_MODEL] = parts[i]

        @pl.when(i0 > 0)
        def _():
            for i in range(3):
                dbg_ref[:, i * D_MODEL:(i + 1) * D_MODEL] += parts[i]

    br = pl.BlockSpec((tm, BRANCH_W), lambda i: (i, 0))
    wsp = pl.BlockSpec((BRANCH_W, D_MODEL), lambda i: (0, 0))
    row = pl.BlockSpec((tm, D_MODEL), lambda i: (i, 0))
    gsp = pl.BlockSpec((tm, GATE_W), lambda i: (i, 0))
    bsp = pl.BlockSpec((1, GATE_W), lambda i: (0, 0))
    act = jax.ShapeDtypeStruct((T, D_MODEL), BF16)
    return pl.pallas_call(
        body, name=name, grid=(T // tm,),
        in_specs=[br, br, br, wsp, wsp, wsp, gsp, bsp, row],
        out_specs=[row, row, row, gsp, bsp],
        out_shape=[act, act, act, jax.ShapeDtypeStruct((T, GATE_W), BF16),
                   jax.ShapeDtypeStruct((1, GATE_W), F32)],
        compiler_params=_cp(("arbitrary",)),
    )(oa, ob, oc, wpa, wpp, wpc, proj, b_gate, dmixed)


def _swiglu_fwd(ab, name):
    T = ab.shape[0]
    tm = min(256, T)

    def body(ab_ref, o_ref):
        a = ab_ref[:, :FFN_HIDDEN].astype(F32)
        o_ref[...] = (a * _sigmoid(a) * ab_ref[:, FFN_HIDDEN:].astype(F32)).astype(BF16)

    return pl.pallas_call(
        body, name=name, grid=(T // tm,),
        in_specs=[pl.BlockSpec((tm, 2 * FFN_HIDDEN), lambda i: (i, 0))],
        out_specs=pl.BlockSpec((tm, FFN_HIDDEN), lambda i: (i, 0)),
        out_shape=jax.ShapeDtypeStruct((T, FFN_HIDDEN), BF16),
        compiler_params=_cp(("parallel",)),
    )(ab)


def _swiglu_bwd(ab, ds, name):
    T = ab.shape[0]
    tm = min(256, T)

    def body(ab_ref, ds_ref, o_ref):
        a = ab_ref[:, :FFN_HIDDEN].astype(F32)
        b = ab_ref[:, FFN_HIDDEN:].astype(F32)
        dsf = ds_ref[...].astype(F32)
        sg = _sigmoid(a)
        o_ref[:, :FFN_HIDDEN] = (dsf * b * sg * (1.0 + a * (1.0 - sg))).astype(BF16)
        o_ref[:, FFN_HIDDEN:] = (dsf * a * sg).astype(BF16)

    full = pl.BlockSpec((tm, 2 * FFN_HIDDEN), lambda i: (i, 0))
    return pl.pallas_call(
        body, name=name, grid=(T // tm,),
        in_specs=[full, pl.BlockSpec((tm, FFN_HIDDEN), lambda i: (i, 0))],
        out_specs=full,
        out_shape=jax.ShapeDtypeStruct((T, 2 * FFN_HIDDEN), BF16),
        compiler_params=_cp(("parallel",)),
    )(ab, ds)


def _adamw(w, g, m, v, name):
    R, C = w.shape
    tr = R
    for cand in (256, 352, 128, 64, 8):
        if R > cand and R % cand == 0:
            tr = cand
            break

    def body(w_ref, g_ref, m_ref, v_ref, d_ref, nm_ref, nv_ref):
        gv = g_ref[...]
        nm = ADAM_B1 * m_ref[...] + (1.0 - ADAM_B1) * gv
        nv = ADAM_B2 * v_ref[...] + (1.0 - ADAM_B2) * (gv * gv)
        m_hat = nm / (1.0 - ADAM_B1 ** ADAM_STEP)
        v_hat = nv / (1.0 - ADAM_B2 ** ADAM_STEP)
        d_ref[...] = -ADAM_LR * (m_hat / (jnp.sqrt(v_hat) + ADAM_EPS) + ADAM_WD * w_ref[...])
        nm_ref[...] = nm
        nv_ref[...] = nv

    blk = pl.BlockSpec((tr, C), lambda i: (i, 0))
    sh = jax.ShapeDtypeStruct((R, C), F32)
    return pl.pallas_call(
        body, name=name, grid=(R // tr,), in_specs=[blk] * 4, out_specs=[blk] * 3, out_shape=[sh] * 3,
        compiler_params=_cp(("parallel",)),
    )(w, g, m, v)


def _sum_slabs(x, name):
    n, R, C = x.shape
    tr = R
    for cand in (512, 256, 128, 64, 32, 16, 8):
        if R > cand and R % cand == 0:
            tr = cand
            break

    def body(x_ref, o_ref):
        acc = x_ref[0].astype(F32)
        for j in range(1, n):
            acc = acc + x_ref[j].astype(F32)
        o_ref[...] = acc

    return pl.pallas_call(
        body, name=name, grid=(R // tr,), in_specs=[pl.BlockSpec((n, tr, C), lambda i: (0, i, 0))],
        out_specs=pl.BlockSpec((tr, C), lambda i: (i, 0)), out_shape=jax.ShapeDtypeStruct((R, C), F32),
        compiler_params=_cp(("parallel",)),
    )(x)


def _multi_gather(xs, layers, name):
    nt = len(xs)
    shapes = [x.shape if lay is None else x.shape[1:] for x, lay in zip(xs, layers)]

    def body(*refs):
        x_refs, out_refs = refs[:nt], refs[nt:2 * nt]
        send_sems, recv_sems, local_sems = refs[2 * nt:]
        x_, y_, c_ = lax.axis_index("x"), lax.axis_index("y"), lax.axis_index("c")
        me, sibling = (x_, y_, c_), (x_, y_, 1 - c_)
        chips = [(1 - x_, y_), (x_, 1 - y_), (1 - x_, 1 - y_)]

        def own_block(t):
            return x_refs[t] if layers[t] is None else x_refs[t].at[layers[t]]

        def copy(t, k, block, to, own=False):
            px, py, pc = block
            dst = out_refs[t].at[4 * px + 2 * py + pc]
            return pltpu.make_async_remote_copy(
                src_ref=own_block(t) if own else dst, dst_ref=dst,
                send_sem=send_sems.at[t, k], recv_sem=recv_sems.at[t, k],
                device_id=to, device_id_type=pl.DeviceIdType.MESH)

        mine, first, passed = [], [], []
        for t in range(nt):
            mine.append(pltpu.make_async_copy(own_block(t), out_refs[t].at[4 * x_ + 2 * y_ + c_], local_sems.at[t]))
            mine[-1].start()
            first.append([copy(t, 1 + j, me, (*chip, c_), own=True) for j, chip in enumerate(chips)]
                         + [copy(t, 0, me, sibling, own=True)])
            for cp in first[-1]:
                cp.start()
        for t in range(nt):
            for j, chip in enumerate(chips):
                copy(t, 1 + j, (*chip, c_), me).wait_recv()
                passed.append(copy(t, 4 + j, (*chip, c_), sibling))
                passed[-1].start()
        for t in range(nt):
            copy(t, 0, sibling, me).wait_recv()
            for j, chip in enumerate(chips):
                copy(t, 4 + j, (*chip, 1 - c_), me).wait_recv()
        for cp in [c for f in first for c in f] + passed:
            cp.wait_send()
        for cp in mine:
            cp.wait()

    hbm = pl.BlockSpec(memory_space=pl.ANY)
    return pl.pallas_call(
        body, name=name, out_shape=[jax.ShapeDtypeStruct((N_DEV,) + tuple(s), x.dtype) for s, x in zip(shapes, xs)],
        in_specs=[hbm] * nt, out_specs=[hbm] * nt,
        scratch_shapes=[pltpu.SemaphoreType.DMA((nt, 7)), pltpu.SemaphoreType.DMA((nt, 7)),
                        pltpu.SemaphoreType.DMA((nt,))],
    )(*xs)


def _multi_exchange(sends, name):
    nt = len(sends)

    def body(*refs):
        s_refs, r_refs = refs[:nt], refs[nt:2 * nt]
        send_sems, recv_sems, local_sems = refs[2 * nt:]
        x_, y_, c_ = lax.axis_index("x"), lax.axis_index("y"), lax.axis_index("c")
        me = 4 * x_ + 2 * y_ + c_
        mine, out, inc = [], [], []
        for t in range(nt):
            mine.append(pltpu.make_async_copy(s_refs[t].at[me], r_refs[t].at[me], local_sems.at[t]))
            mine[-1].start()
        for k in (2, 4, 6, 3, 5, 7, 1):
            px, py, pc = x_ ^ ((k >> 2) & 1), y_ ^ ((k >> 1) & 1), c_ ^ (k & 1)
            peer = 4 * px + 2 * py + pc
            for t in range(nt):
                def copy(src, dst):
                    return pltpu.make_async_remote_copy(
                        src_ref=s_refs[t].at[src], dst_ref=r_refs[t].at[dst],
                        send_sem=send_sems.at[t, k - 1], recv_sem=recv_sems.at[t, k - 1],
                        device_id=(px, py, pc), device_id_type=pl.DeviceIdType.MESH)

                out.append(copy(peer, me))
                inc.append(copy(me, peer))
        for cp in out:
            cp.start()
        for cp in inc:
            cp.wait_recv()
        for cp in out:
            cp.wait_send()
        for cp in mine:
            cp.wait()

    hbm = pl.BlockSpec(memory_space=pl.ANY)
    return pl.pallas_call(
        body, name=name, out_shape=[jax.ShapeDtypeStruct(s.shape, s.dtype) for s in sends],
        in_specs=[hbm] * nt, out_specs=[hbm] * nt,
        scratch_shapes=[pltpu.SemaphoreType.DMA((nt, N_DEV - 1)), pltpu.SemaphoreType.DMA((nt, N_DEV - 1)),
                        pltpu.SemaphoreType.DMA((nt,))],
    )(*sends)


_HBM = pl.BlockSpec(memory_space=pltpu.HBM)
_SEM = pl.BlockSpec(memory_space=pltpu.SEMAPHORE)
_PEER_ORDER = (2, 4, 6, 3, 5, 7, 1)


def _split_copies(src_refs, land_refs, send_sems, recv_sems, layers, per_peer):
    x_, y_, c_ = lax.axis_index("x"), lax.axis_index("y"), lax.axis_index("c")
    me = 4 * x_ + 2 * y_ + c_
    copies = []
    for k in _PEER_ORDER:
        px, py, pc = x_ ^ ((k >> 2) & 1), y_ ^ ((k >> 1) & 1), c_ ^ (k & 1)
        peer = 4 * px + 2 * py + pc
        for t in range(len(src_refs)):
            if per_peer:
                src = src_refs[t].at[peer]
            else:
                src = src_refs[t] if layers[t] is None else src_refs[t].at[layers[t]]
            copies.append(pltpu.make_async_remote_copy(
                src_ref=src, dst_ref=land_refs[t].at[me],
                send_sem=send_sems.at[t * (N_DEV - 1) + k - 1], recv_sem=recv_sems.at[t * (N_DEV - 1) + k - 1],
                device_id=(px, py, pc), device_id_type=pl.DeviceIdType.MESH))
    return copies


def _own_copies(src_refs, land_refs, sems, layers, per_peer):
    nt = len(src_refs)
    me = 4 * lax.axis_index("x") + 2 * lax.axis_index("y") + lax.axis_index("c")
    copies = []
    for t in range(nt):
        if per_peer:
            src = src_refs[t].at[me]
        else:
            src = src_refs[t] if layers[t] is None else src_refs[t].at[layers[t]]
        copies.append(pltpu.make_async_copy(src, land_refs[t].at[me], sems.at[nt * (N_DEV - 1) + t]))
    return copies


def _split_start(srcs, layers, per_peer, after, name):
    nt = len(srcs)
    if per_peer:
        land_shapes = [s.shape for s in srcs]
    else:
        land_shapes = [(N_DEV,) + tuple(s.shape if lay is None else s.shape[1:]) for s, lay in zip(srcs, layers)]

    def body(*refs):
        src_refs, land_refs = refs[:nt], refs[nt:2 * nt]
        send_sems, recv_sems = refs[2 * nt + 1], refs[2 * nt + 2]
        token = refs[-1]
        for cp in _split_copies(src_refs, land_refs, send_sems, recv_sems, layers, per_peer):
            cp.start()
        for cp in _own_copies(src_refs, land_refs, send_sems, layers, per_peer):
            cp.start()
        token[...] = jnp.zeros_like(token)

    lands = [pltpu.with_memory_space_constraint(lax.empty(s, x.dtype), pltpu.HBM) for s, x in zip(land_shapes, srcs)]
    srcs = [pltpu.with_memory_space_constraint(x, pltpu.HBM) for x in srcs]
    out = pl.pallas_call(
        body, name=name,
        out_shape=(pltpu.SemaphoreType.DMA((nt * N_DEV,)), pltpu.SemaphoreType.DMA((nt * (N_DEV - 1),)),
                   *[pltpu.HBM(x.shape, x.dtype) for x in srcs], *[pltpu.HBM(s, x.dtype) for s, x in zip(land_shapes, srcs)],
                   jax.ShapeDtypeStruct((8, LANES), F32)),
        in_specs=[_HBM] * (2 * nt) + [pl.BlockSpec(memory_space=pl.ANY)],
        out_specs=(_SEM, _SEM, *([_HBM] * (2 * nt)), pl.BlockSpec(memory_space=pltpu.VMEM)),
        input_output_aliases={i: 2 + i for i in range(2 * nt)},
        compiler_params=pltpu.CompilerParams(has_side_effects=pltpu.SideEffectType.DATAFLOW_SIDE_EFFECTING),
    )(*srcs, *lands, after)
    return out[0], out[1], list(out[2:2 + nt]), list(out[2 + nt:2 + 2 * nt]), out[-1]


def _split_wait(started, layers, per_peer, after, name):
    send_sems, recv_sems, srcs, lands, _ = started
    nt = len(srcs)

    def body(*refs):
        src_refs, land_refs = refs[:nt], refs[nt:2 * nt]
        s_sems, r_sems = refs[2 * nt], refs[2 * nt + 1]
        for cp in _split_copies(src_refs, land_refs, s_sems, r_sems, layers, per_peer):
            cp.wait_send()
            cp.wait_recv()
        for cp in _own_copies(src_refs, land_refs, s_sems, layers, per_peer):
            cp.wait()

    out = pl.pallas_call(
        body, name=name,
        out_shape=tuple(pltpu.HBM(x.shape, x.dtype) for x in srcs + lands),
        in_specs=[_HBM] * (2 * nt) + [_SEM, _SEM, pl.BlockSpec(memory_space=pl.ANY)],
        out_specs=tuple([_HBM] * (2 * nt)),
        input_output_aliases={i: i for i in range(2 * nt)},
        compiler_params=pltpu.CompilerParams(has_side_effects=pltpu.SideEffectType.DATAFLOW_SIDE_EFFECTING),
    )(*srcs, *lands, send_sems, recv_sems, after)
    return list(out[nt:])


def _with_own(land, own):
    me = 4 * lax.axis_index("x") + 2 * lax.axis_index("y") + lax.axis_index("c")
    return lax.dynamic_update_slice_in_dim(land, own[None], me, axis=0)


def _runs(mapping):
    runs, c, n = [], 0, len(mapping)
    while c < n:
        if mapping[c] is None:
            c += 1
            continue
        sid, d, lo = mapping[c][0], mapping[c][1] - c, c
        while c < n and mapping[c] is not None and mapping[c][0] == sid and mapping[c][1] - c == d:
            c += 1
        runs.append((lo, c, sid, d))
    return runs


def _tile_plan(mapping, src_widths):
    runs = _runs(mapping)
    plan = []
    for t in range(len(mapping) // LANES):
        pieces = []
        for lo, hi, sid, d in runs:
            lo_t, hi_t = max(lo, t * LANES), min(hi, (t + 1) * LANES)
            if lo_t >= hi_t:
                continue
            a = ((lo_t + d) // LANES) * LANES
            win = min(2 * LANES, src_widths[sid] - a)
            shift = t * LANES + d - a
            pieces.append((sid, a, win, shift, lo_t - t * LANES, hi_t - t * LANES))
        plan.append(pieces)
    return plan


def _reblock(srcs, src_views, outs, out_views, name):
    R = srcs[0].shape[-2]
    tr = min(256, R)
    widths = {sid: srcs[ai].shape[-1] for sid, (ai, _) in src_views.items()}
    plans = [(ai, li, _tile_plan(mapping, widths)) for ai, li, mapping in out_views]
    ns = len(srcs)

    def body(*refs):
        s_refs, o_refs = refs[:ns], refs[ns:]
        cache = {}

        def shift_matrix(win, shift, lo, hi):
            key = (win, shift, lo, hi)
            if key not in cache:
                r = lax.broadcasted_iota(jnp.int32, (win, LANES), 0)
                c = lax.broadcasted_iota(jnp.int32, (win, LANES), 1)
                hit = jnp.logical_and(r - c == shift, jnp.logical_and(c >= lo, c < hi))
                cache[key] = jnp.where(hit, 1.0, 0.0).astype(BF16)
            return cache[key]

        for ai, li, plan in plans:
            for t, pieces in enumerate(plan):
                acc = None
                for sid, a, win, shift, lo, hi in pieces:
                    sa, sl = src_views[sid]
                    src = s_refs[sa][:, a:a + win] if sl is None else s_refs[sa][sl, :, a:a + win]
                    part = jnp.dot(src, shift_matrix(win, shift, lo, hi), preferred_element_type=F32)
                    acc = part if acc is None else acc + part
                val = jnp.zeros((tr, LANES), BF16) if acc is None else acc.astype(BF16)
                if li is None:
                    o_refs[ai][:, t * LANES:(t + 1) * LANES] = val
                else:
                    o_refs[ai][li, :, t * LANES:(t + 1) * LANES] = val

    def spec(shape):
        if len(shape) == 2:
            return pl.BlockSpec((tr, shape[1]), lambda i: (i, 0))
        return pl.BlockSpec((shape[0], tr, shape[2]), lambda i: (0, i, 0))

    return pl.pallas_call(
        body, name=name, grid=(R // tr,), in_specs=[spec(s.shape) for s in srcs],
        out_specs=[spec(s) for s in outs], out_shape=[jax.ShapeDtypeStruct(s, BF16) for s in outs],
        compiler_params=_cp(("parallel",)),
    )(*srcs)


SHARDED = ("w_in", "w_gate_up", "w_proj_attn", "w_proj_pool", "w_proj_conv", "w_out", "w_down")
WEIGHT_ORDER = ("attn_norm", "w_in", "b_forget", "b_gate", "w_proj_attn", "pool_w", "pool_scale", "w_proj_pool",
                "conv_w", "w_proj_conv", "w_out", "ffn_norm", "w_gate_up", "w_down", "final_norm")
IN_SHARD, IN_SHARD_PAD = IN_COLS // N_DEV, 896
GU_SHARD, GU_SHARD_PAD = 2 * FFN_HIDDEN // N_DEV, 768


def _w_in_col(c):
    if c < GATE_W:
        return c + 3592
    if c < OFF_U:
        return c - OFF_Q
    return c - OFF_U + 1544


def _w_in_full(gathered, name):
    main = [divmod(_w_in_col(c), IN_SHARD) for c in range(MAIN_COLS)]
    fcols = [divmod(1536 + c, IN_SHARD) if c < N_HEADS else None for c in range(LANES)]
    R = gathered.shape[1]
    return _reblock([gathered], {i: (0, i) for i in range(N_DEV)}, [(R, MAIN_COLS), (R, LANES)],
                    [(0, None, main), (1, None, fcols)], name)


def _w_in_slabs(dmain, dwf, name):
    inv = {_w_in_col(c): ("m", c) for c in range(MAIN_COLS)}
    inv.update({1536 + c: ("f", c) for c in range(N_HEADS)})
    views = []
    for i in range(N_DEV):
        mapping = [inv[IN_SHARD * i + j] if j < IN_SHARD else None for j in range(IN_SHARD_PAD)]
        views.append((0, i, mapping))
    R = dmain.shape[0]
    return _reblock([dmain, dwf], {"m": (0, None), "f": (1, None)}, [(N_DEV, R, IN_SHARD_PAD)], views, name)[0]


def _w_gu_full(gathered, name):
    mapping = [divmod(c, GU_SHARD) for c in range(2 * FFN_HIDDEN)]
    R = gathered.shape[1]
    return _reblock([gathered], {i: (0, i) for i in range(N_DEV)}, [(R, 2 * FFN_HIDDEN)], [(0, None, mapping)], name)[0]


def _w_gu_slabs(dw, name):
    views = [(0, i, [("w", GU_SHARD * i + j) if j < GU_SHARD else None for j in range(GU_SHARD_PAD)])
             for i in range(N_DEV)]
    R = dw.shape[0]
    return _reblock([dw], {"w": (0, None)}, [(N_DEV, R, GU_SHARD_PAD)], views, name)[0]


def _layer_fwd(x, W, n_seq, l):
    T = x.shape[0]
    sfx = f"_l{l}"
    h1 = _rms_fwd(x, W["attn_norm"], "rms1" + sfx)
    proj = _matmul(h1, W["w_main"], mode="nn", out_dtype=BF16, name="proj_main" + sfx)
    f = _matmul(h1, W["w_f"], mode="nn", out_dtype=F32, name="proj_f" + sfx)
    qa, ka, va = _fox_prep(f, W["b_forget"], proj, n_seq, "fox_prep" + sfx)
    oa, oa32, lse = _attn_fwd2(qa, ka, va, n_seq, "attn_fwd" + sfx)
    if "late" in W:
        W.update(W.pop("late")(oa))
    ob = _pool_fwd(proj, W["pool_w"], W["pool_scale"], n_seq, "pool_fwd" + sfx)
    oc = _conv_fwd(proj, W["conv_w"], n_seq, "conv_fwd" + sfx)
    mixed = _mix_fwd(oa, ob, oc, W["w_proj_attn"], W["w_proj_pool"], W["w_proj_conv"], proj, W["b_gate"],
                     "mix_fwd" + sfx)
    x2 = _matmul(mixed, W["w_out"], mode="nn", out_dtype=F32, name="out_proj" + sfx, residual=x)
    h2 = _rms_fwd(x2, W["ffn_norm"], "rms2" + sfx)
    ab = _matmul(h2, W["w_gate_up"], mode="nn", out_dtype=BF16, name="gate_up" + sfx)
    s = _swiglu_fwd(ab, "swiglu_fwd" + sfx)
    x3 = _matmul(s, W["w_down"], mode="nn", out_dtype=F32, name="down" + sfx, tm=1024, tn=1024, tk=1408,
                 residual=x2)
    saved = dict(x=x, h1=h1, proj=proj, f=f, qa=qa, ka=ka, oa=oa, oa32=oa32, lse=lse, ob=ob, oc=oc, mixed=mixed, x2=x2,
                 h2=h2, ab=ab, s=s)
    return x3, saved


def _layer_bwd(dx3, dx3b, W, sv, n_seq, l, stage=None):
    T = dx3.shape[0]
    sfx = f"_l{l}"
    G = {}
    stage = stage or (lambda l, group, G, W: W)
    ds = _matmul(dx3b, W["w_down"], mode="nt", out_dtype=BF16, name="d_s" + sfx, tm=1024, tn=1408)
    G["w_down"] = _matmul(sv["s"], dx3b, mode="tn", out_dtype=BF16, name="dw_down" + sfx, tm=256, tn=1024)
    dab = _swiglu_bwd(sv["ab"], ds, "swiglu_bwd" + sfx)
    dh2 = _matmul(dab, W["w_gate_up"], mode="nt", out_dtype=BF16, name="d_h2" + sfx, tm=1024, tn=1024, tk=1408)
    G["w_gate_up"] = _matmul(sv["h2"], dab, mode="tn", out_dtype=BF16, name="dw_gate_up" + sfx, tm=1024)
    W = stage(l, "ffn", G, W)
    dx2, dx2b, G["ffn_norm"] = _rms_bwd(sv["x2"], W["ffn_norm"], dh2, dx3, "rms2_bwd" + sfx)
    dmixed = _matmul(dx2b, W["w_out"], mode="nt", out_dtype=BF16, name="d_mixed" + sfx)
    G["w_out"] = _matmul(sv["mixed"], dx2b, mode="tn", out_dtype=BF16, name="dw_out" + sfx, tm=1024)
    dya, dyb, dyc, dg, G["b_gate"] = _mix_bwd(sv["oa"], sv["ob"], sv["oc"], W["w_proj_attn"], W["w_proj_pool"],
                                              W["w_proj_conv"], sv["proj"], W["b_gate"], dmixed, "mix_bwd" + sfx)
    douts = {}
    for br, dy, o in (("attn", dya, sv["oa"]), ("pool", dyb, sv["ob"]), ("conv", dyc, sv["oc"])):
        douts[br] = _matmul(dy, W["w_proj_" + br], mode="nt", out_dtype=BF16, name=f"d_{br}_out" + sfx)
        G["w_proj_" + br] = _matmul(o, dy, mode="tn", out_dtype=BF16, name=f"dw_proj_{br}" + sfx, tm=512)
    W = stage(l, "mix", G, W)
    dcv, dcb, dcc, G["conv_w"] = _conv_bwd(sv["proj"], douts["conv"], W["conv_w"], n_seq, "conv_bwd" + sfx)
    du, G["pool_w"], G["pool_scale"] = _pool_bwd(sv["proj"], douts["pool"], W["pool_w"], W["pool_scale"], n_seq,
                                                 "pool_bwd" + sfx)
    delta = _attn_delta(douts["attn"], sv["oa32"], "attn_delta" + sfx)
    dq = _attn_bwd_dq2(sv["qa"], sv["ka"], sv["proj"], douts["attn"], sv["lse"], delta, n_seq, "attn_dq" + sfx)
    dk, dv, dFk = _attn_bwd_dkv2(sv["qa"], sv["ka"], sv["proj"], douts["attn"], sv["lse"], delta, n_seq,
                                 "attn_dkv" + sfx)
    dF = jnp.pad(dFk.reshape(N_HEADS, T).T, ((0, 0), (0, LANES - N_HEADS)))
    df, G["b_forget"] = _fox_cumsum_bwd(sv["f"], W["b_forget"], dF, n_seq, "fox_cumsum_bwd" + sfx)
    dproj = jnp.concatenate([dg, dq, dk, dv, du, dcv, dcb, dcc], axis=1)
    G["w_main"] = _matmul(sv["h1"], dproj, mode="tn", out_dtype=BF16, name="dw_main" + sfx, tm=1024)
    G["w_f"] = _matmul(sv["h1"], df, mode="tn", out_dtype=BF16, name="dw_f" + sfx, tm=1024)
    W = stage(l, "w_in", G, W)
    dh1 = _matmul(df, W["w_f"], mode="nt", out_dtype=F32, name="d_h1_f" + sfx)
    dh1 = _matmul(dproj, W["w_main"], mode="nt", out_dtype=F32, name="d_h1_main" + sfx, tm=1024, tn=1024, tk=1664,
                  residual=dh1)
    dx, dxb, G["attn_norm"] = _rms_bwd(sv["x"], W["attn_norm"], dh1, dx2, "rms1_bwd" + sfx)
    return dx, dxb, G


def _replicated_operands(rep, l):
    W = {}
    W["attn_norm"], W["ffn_norm"] = rep["attn_norm"][l], rep["ffn_norm"][l]
    W["b_forget"] = jnp.pad(rep["b_forget"][l].reshape(1, N_HEADS), ((0, 0), (0, LANES - N_HEADS)))
    W["b_gate"] = rep["b_gate"][l].reshape(1, GATE_W)
    W["pool_w"] = rep["pool_w"][l].astype(BF16)
    W["pool_scale"] = rep["pool_scale"][l].reshape(1, BRANCH_W)
    return W


def _local_step(x, target, get_W, final_norm, stage=None):
    n_seq, S, Dm = x.shape
    T = n_seq * S
    xt = x.reshape(T, Dm)
    saved, Ws = [], []
    for l in range(DEPTH):
        Ws.append(get_W(l, xt))
        xt, sv = _layer_fwd(xt, Ws[l], n_seq, l)
        saved.append(sv)
    loss, dx, dxb, g_final = _loss_head(xt, final_norm, target.reshape(T, Dm), "loss_head")
    grads = [None] * DEPTH
    for l in reversed(range(DEPTH)):
        dx, dxb, grads[l] = _layer_bwd(dx, dxb, Ws[l], saved[l], n_seq, l, stage)
    return loss, dx.reshape(n_seq, S, Dm), grads, g_final


def _padded_shards(weights):
    sh = {n: weights[n].astype(BF16) for n in SHARDED}
    sh["w_in"] = jnp.pad(sh["w_in"], ((0, 0), (0, 0), (0, IN_SHARD_PAD - IN_SHARD)))
    sh["w_gate_up"] = jnp.pad(sh["w_gate_up"], ((0, 0), (0, 0), (0, GU_SHARD_PAD - GU_SHARD)))
    return sh


def _full_operands(g, l):
    W = {}
    if "w_in" in g:
        W["w_main"], W["w_f"] = _w_in_full(g["w_in"], f"w_in_full_l{l}")
    if "w_gate_up" in g:
        W["w_gate_up"] = _w_gu_full(g["w_gate_up"], f"w_gate_up_full_l{l}")
    for n in ("w_proj_attn", "w_proj_pool", "w_proj_conv"):
        if n in g:
            W[n] = jnp.transpose(g[n], (1, 0, 2)).reshape(BRANCH_W, D_MODEL)
    if "w_out" in g:
        W["w_out"] = g["w_out"].reshape(D_MODEL, D_MODEL)
    if "w_down" in g:
        W["w_down"] = g["w_down"].reshape(FFN_HIDDEN, D_MODEL)
    return W


GRAD_GROUPS = {"ffn": ("w_down", "w_gate_up"),
               "mix": ("w_out", "w_proj_attn", "w_proj_pool", "w_proj_conv"),
               "w_in": ("w_in",)}


def _grad_slabs(G, n, l):
    if n == "w_in":
        return _w_in_slabs(G["w_main"], G["w_f"], f"w_in_slabs_l{l}")
    if n == "w_gate_up":
        return _w_gu_slabs(G["w_gate_up"], f"w_gate_up_slabs_l{l}")
    if n == "w_out":
        return G["w_out"].reshape(N_DEV, D_MODEL // N_DEV, D_MODEL)
    if n == "w_down":
        return G["w_down"].reshape(N_DEV, FFN_HIDDEN // N_DEV, D_MODEL)
    return jnp.transpose(G[n].reshape(BRANCH_W, N_DEV, D_MODEL // N_DEV), (1, 0, 2))


def _sum_layer_grads(recv, l):
    out = {n: _sum_slabs(r, f"sum_{n}_l{l}") for n, r in recv.items()}
    if "w_in" in out:
        out["w_in"] = out["w_in"][:, :IN_SHARD]
    if "w_gate_up" in out:
        out["w_gate_up"] = out["w_gate_up"][:, :GU_SHARD]
    return out


def _sum_small(xs, name):
    def body(*refs):
        for x_ref, o_ref in zip(refs[:len(xs)], refs[len(xs):]):
            acc = x_ref[0]
            for j in range(1, N_DEV):
                acc = acc + x_ref[j]
            o_ref[...] = acc

    return pl.pallas_call(
        body, name=name, out_shape=[jax.ShapeDtypeStruct(x.shape[1:], F32) for x in xs],
        compiler_params=_cp(),
    )(*xs)


def _as_2d(a):
    if a.ndim == 1:
        return a.reshape(1, -1)
    return a.reshape(-1, a.shape[-1])


def kernel(x, attn_norm, w_in, b_forget, b_gate, w_proj_attn, pool_w, pool_scale, w_proj_pool, conv_w, w_proj_conv, w_out, ffn_norm, w_gate_up, w_down, final_norm, loss_target, m_attn_norm, m_w_in, m_b_forget, m_b_gate, m_w_proj_attn, m_pool_w, m_pool_scale, m_w_proj_pool, m_conv_w, m_w_proj_conv, m_w_out, m_ffn_norm, m_w_gate_up, m_w_down, m_final_norm, v_attn_norm, v_w_in, v_b_forget, v_b_gate, v_w_proj_attn, v_pool_w, v_pool_scale, v_w_proj_pool, v_conv_w, v_w_proj_conv, v_w_out, v_ffn_norm, v_w_gate_up, v_w_down, v_final_norm):
    weights = dict(attn_norm=attn_norm, w_in=w_in, b_forget=b_forget, b_gate=b_gate, w_proj_attn=w_proj_attn,
                   pool_w=pool_w, pool_scale=pool_scale, w_proj_pool=w_proj_pool, conv_w=conv_w,
                   w_proj_conv=w_proj_conv, w_out=w_out, ffn_norm=ffn_norm, w_gate_up=w_gate_up, w_down=w_down,
                   final_norm=final_norm)
    moments_m = dict(attn_norm=m_attn_norm, w_in=m_w_in, b_forget=m_b_forget, b_gate=m_b_gate,
                     w_proj_attn=m_w_proj_attn, pool_w=m_pool_w, pool_scale=m_pool_scale, w_proj_pool=m_w_proj_pool,
                     conv_w=m_conv_w, w_proj_conv=m_w_proj_conv, w_out=m_w_out, ffn_norm=m_ffn_norm,
                     w_gate_up=m_w_gate_up, w_down=m_w_down, final_norm=m_final_norm)
    moments_v = dict(attn_norm=v_attn_norm, w_in=v_w_in, b_forget=v_b_forget, b_gate=v_b_gate,
                     w_proj_attn=v_w_proj_attn, pool_w=v_pool_w, pool_scale=v_pool_scale, w_proj_pool=v_w_proj_pool,
                     conv_w=v_conv_w, w_proj_conv=v_w_proj_conv, w_out=v_w_out, ffn_norm=v_ffn_norm,
                     w_gate_up=v_w_gate_up, w_down=v_w_down, final_norm=v_final_norm)

    sh = _padded_shards(weights)
    names = list(SHARDED)
    rest = [n for n in names if n != "w_in"]
    me = 4 * lax.axis_index("x") + 2 * lax.axis_index("y") + lax.axis_index("c")
    w_in0, conv_all = _multi_gather([sh["w_in"], conv_w], [0, None], "gather_w_in_l0")
    rest_layers, all_layers = [0] * len(rest), [1] * len(names)
    rest_started = _split_start([sh[n] for n in rest], rest_layers, False, w_in0, "gather_start_rest_l0")
    l1_started = _split_start([sh[n] for n in names], all_layers, False, rest_started[4], "gather_start_l1")

    def get_W(l, xt):
        if l == 0:
            W = _full_operands({"w_in": w_in0}, 0)

            def late(after):
                lands = _split_wait(rest_started, rest_layers, False, after, "gather_wait_rest_l0")
                return _full_operands(dict(zip(rest, lands)), 0)

            W["late"] = late
        else:
            lands = _split_wait(l1_started, all_layers, False, xt, "gather_wait_l1")
            W = _full_operands(dict(zip(names, lands)), l)
        W.update(_replicated_operands(weights, l))
        W["conv_w"] = jnp.transpose(conv_all[:, l], (1, 0, 2)).reshape(CONV_K, BRANCH_W)
        if l == 0:
            W["attn_norm"] = W["attn_norm"] + l1_started[4][0, 0]
        return W

    exchanges = []

    def stage(l, group, G, W):
        gnames = GRAD_GROUPS[group]
        slabs = [_grad_slabs(G, n, l) for n in gnames]
        started = _split_start(slabs, None, True, slabs[0], f"exchange_start_{group}_l{l}")
        exchanges.append((l, group, gnames, slabs, started))
        tie = {"ffn": "ffn_norm", "mix": "conv_w", "w_in": "w_f"}[group]
        W = dict(W)
        W[tie] = W[tie] + started[4][0, 0].astype(W[tie].dtype)
        return W

    loss_part, grad_x, grads, g_final = _local_step(x, loss_target, get_W, final_norm, stage)
    after = grad_x
    for l, group, gnames, slabs, started in exchanges:
        lands = _split_wait(started, None, True, after, f"exchange_wait_{group}_l{l}")
        grads[l].update(_sum_layer_grads(dict(zip(gnames, lands)), l))
    gw = {n: jnp.stack([grads[l][n] for l in range(DEPTH)]) for n in SHARDED}

    small = ("attn_norm", "b_forget", "b_gate", "pool_w", "pool_scale", "ffn_norm", "conv_w")
    parts = [jnp.stack([grads[l][n] for l in range(DEPTH)]) for n in small] + [g_final, loss_part]
    gathered = _multi_gather(parts, [None] * len(parts), "gather_small_grads")
    summed = _sum_small(gathered, "sum_small_grads")
    for n, s in zip(small, summed):
        gw[n] = s
    gw["attn_norm"], gw["ffn_norm"] = gw["attn_norm"][:, 0], gw["ffn_norm"][:, 0]
    gw["b_forget"] = gw["b_forget"][:, 0, :N_HEADS]
    gw["b_gate"], gw["pool_scale"] = gw["b_gate"][:, 0], gw["pool_scale"][:, 0]
    gw["conv_w"] = lax.dynamic_slice_in_dim(gw["conv_w"], me * (BRANCH_W // N_DEV), BRANCH_W // N_DEV, axis=2)
    gw["final_norm"] = summed[-2][0]
    loss = summed[-1][0, 0]

    deltas, new_m, new_v = {}, {}, {}
    for n in WEIGHT_ORDER:
        shape = weights[n].shape
        d, nm, nv = _adamw(_as_2d(weights[n]), _as_2d(gw[n]), _as_2d(moments_m[n]), _as_2d(moments_v[n]),
                           "adamw_" + n)
        deltas[n], new_m[n], new_v[n] = d.reshape(shape), nm.reshape(shape), nv.reshape(shape)

    return (loss, grad_x, *[gw[n] for n in WEIGHT_ORDER], *[deltas[n] for n in WEIGHT_ORDER],
            *[new_m[n] for n in WEIGHT_ORDER], *[new_v[n] for n in WEIGHT_ORDER])
```

```python
import functools

import numpy as np
import jax
import jax.numpy as jnp
from jax import lax
from jax.experimental import pallas as pl
from jax.experimental.pallas import tpu as pltpu

F32 = jnp.float32
BF16 = jnp.bfloat16

N_DEV = 8
D_MODEL = 1024
DEPTH = 2
N_HEADS = 8
HEAD_DIM = 64
BRANCH_W = 512
POOL_WINDOWS = (2, 4, 8, 16)
POOL_GD = 128
CONV_K = 3
FFN_HIDDEN = 2816
GATE_W = 3 * D_MODEL
IN_COLS = 6664
MAIN_COLS = GATE_W + 7 * BRANCH_W
RMS_EPS = 1e-6
NEG_INF = -1e30

ADAM_LR = 0.001
ADAM_B1 = 0.9
ADAM_B2 = 0.999
ADAM_EPS = 1e-08
ADAM_WD = 0.01
ADAM_STEP = 10

LANES = 128
VMEM_LIMIT = 56 * 1024 * 1024
ATT_BLK = 256
CUM_BLK = 256

OFF_G, OFF_Q, OFF_K, OFF_V, OFF_U, OFF_CV, OFF_CB, OFF_CC = (
    0, 3072, 3584, 4096, 4608, 5120, 5632, 6144)


def _cp(sem=None):
    return pltpu.CompilerParams(dimension_semantics=sem, vmem_limit_bytes=VMEM_LIMIT)


def _sigmoid(z):
    return 1.0 / (1.0 + jnp.exp(-z))


def _matmul(a, b, *, mode, out_dtype, name, tm=2048, tn=512, tk=None, residual=None):
    if mode == "nn":
        (M, K), N = a.shape, b.shape[1]
    elif mode == "nt":
        (M, K), N = a.shape, b.shape[0]
    else:
        (K, M), N = a.shape, b.shape[1]
    tm, tn, tk = min(tm, M), min(tn, N), K if tk is None else min(tk, K)
    assert M % tm == 0 and N % tn == 0 and K % tk == 0, (name, M, N, K, tm, tn, tk)
    nk = K // tk
    if mode == "nn":
        a_spec = pl.BlockSpec((tm, tk), lambda i, j, k: (i, k))
        b_spec = pl.BlockSpec((tk, tn), lambda i, j, k: (k, j))
        dims = (((1,), (0,)), ((), ()))
    elif mode == "nt":
        a_spec = pl.BlockSpec((tm, tk), lambda i, j, k: (i, k))
        b_spec = pl.BlockSpec((tn, tk), lambda i, j, k: (j, k))
        dims = (((1,), (1,)), ((), ()))
    else:
        a_spec = pl.BlockSpec((tk, tm), lambda i, j, k: (k, i))
        b_spec = pl.BlockSpec((tk, tn), lambda i, j, k: (k, j))
        dims = (((0,), (0,)), ((), ()))
    o_spec = pl.BlockSpec((tm, tn), lambda i, j, k: (i, j))
    has_res = residual is not None

    def body(*refs):
        a_ref, b_ref = refs[:2]
        r_ref = refs[2] if has_res else None
        o_ref = refs[2 + has_res]

        def finish(acc):
            if has_res:
                acc = acc + r_ref[...].astype(F32)
            o_ref[...] = acc.astype(out_dtype)

        prod = lax.dot_general(a_ref[...], b_ref[...], dims, preferred_element_type=F32)
        if nk == 1:
            finish(prod)
            return
        acc_ref = refs[-1]
        k = pl.program_id(2)

        @pl.when(k == 0)
        def _():
            acc_ref[...] = prod

        @pl.when(jnp.logical_and(k > 0, k < nk - 1))
        def _():
            acc_ref[...] += prod

        @pl.when(k == nk - 1)
        def _():
            finish(acc_ref[...] + prod)

    in_specs = [a_spec, b_spec] + ([o_spec] if has_res else [])
    args = (a, b) + ((residual,) if has_res else ())
    return pl.pallas_call(
        body, name=name, grid=(M // tm, N // tn, nk), in_specs=in_specs, out_specs=o_spec,
        out_shape=jax.ShapeDtypeStruct((M, N), out_dtype),
        scratch_shapes=[pltpu.VMEM((tm, tn), F32)] if nk > 1 else [],
        compiler_params=_cp(("parallel", "parallel", "arbitrary")),
    )(*args)


def _rms_fwd(x, g, name):
    T, Dm = x.shape
    tm = min(512, T)

    def body(x_ref, g_ref, h_ref):
        xf = x_ref[...]
        r = lax.rsqrt(jnp.mean(xf * xf, axis=-1, keepdims=True) + RMS_EPS)
        h_ref[...] = ((xf * r) * g_ref[...]).astype(BF16)

    return pl.pallas_call(
        body, name=name, grid=(T // tm,),
        in_specs=[pl.BlockSpec((tm, Dm), lambda i: (i, 0)), pl.BlockSpec((1, Dm), lambda i: (0, 0))],
        out_specs=pl.BlockSpec((tm, Dm), lambda i: (i, 0)),
        out_shape=jax.ShapeDtypeStruct((T, Dm), BF16),
        compiler_params=_cp(("parallel",)),
    )(x, g.reshape(1, Dm))


def _rms_bwd(x, g, dh, dres, name):
    T, Dm = x.shape
    tm = min(512, T)

    def body(x_ref, g_ref, dh_ref, dres_ref, dx_ref, dxb_ref, dg_ref):
        i = pl.program_id(0)
        xf = x_ref[...]
        r = lax.rsqrt(jnp.mean(xf * xf, axis=-1, keepdims=True) + RMS_EPS)
        xn = xf * r
        dhf = dh_ref[...].astype(F32)
        dxn = dhf * g_ref[...]
        c = jnp.mean(dxn * xn, axis=-1, keepdims=True)
        dx = dres_ref[...] + r * (dxn - xn * c)
        dx_ref[...] = dx
        dxb_ref[...] = dx.astype(BF16)
        part = jnp.sum(dhf * xn, axis=0, keepdims=True)

        @pl.when(i == 0)
        def _():
            dg_ref[...] = part

        @pl.when(i > 0)
        def _():
            dg_ref[...] += part

    row = pl.BlockSpec((tm, Dm), lambda i: (i, 0))
    vec = pl.BlockSpec((1, Dm), lambda i: (0, 0))
    return pl.pallas_call(
        body, name=name, grid=(T // tm,), in_specs=[row, vec, row, row], out_specs=[row, row, vec],
        out_shape=[jax.ShapeDtypeStruct((T, Dm), F32), jax.ShapeDtypeStruct((T, Dm), BF16),
                   jax.ShapeDtypeStruct((1, Dm), F32)],
        compiler_params=_cp(("arbitrary",)),
    )(x, g.reshape(1, Dm), dh, dres)


def _loss_head(x, g, target, name):
    T, Dm = x.shape
    tm = min(512, T)

    def body(x_ref, g_ref, t_ref, loss_ref, dx_ref, dxb_ref, dg_ref):
        i = pl.program_id(0)
        xf = x_ref[...]
        gv = g_ref[...]
        r = lax.rsqrt(jnp.mean(xf * xf, axis=-1, keepdims=True) + RMS_EPS)
        xn = xf * r
        diff = xn * gv - t_ref[...]
        per_tok = jnp.mean(diff * diff, axis=-1, keepdims=True)
        lpart = 0.5 * jnp.sum(per_tok, axis=0, keepdims=True) + jnp.zeros((1, LANES), F32)
        dy = diff * (1.0 / Dm)
        dxn = dy * gv
        c = jnp.mean(dxn * xn, axis=-1, keepdims=True)
        dx = r * (dxn - xn * c)
        dx_ref[...] = dx
        dxb_ref[...] = dx.astype(BF16)
        part = jnp.sum(dy * xn, axis=0, keepdims=True)

        @pl.when(i == 0)
        def _():
            dg_ref[...] = part
            loss_ref[...] = lpart

        @pl.when(i > 0)
        def _():
            dg_ref[...] += part
            loss_ref[...] += lpart

    row = pl.BlockSpec((tm, Dm), lambda i: (i, 0))
    vec = pl.BlockSpec((1, Dm), lambda i: (0, 0))
    lsp = pl.BlockSpec((1, LANES), lambda i: (0, 0))
    return pl.pallas_call(
        body, name=name, grid=(T // tm,), in_specs=[row, vec, row], out_specs=[lsp, row, row, vec],
        out_shape=[jax.ShapeDtypeStruct((1, LANES), F32), jax.ShapeDtypeStruct((T, Dm), F32),
                   jax.ShapeDtypeStruct((T, Dm), BF16), jax.ShapeDtypeStruct((1, Dm), F32)],
        compiler_params=_cp(("arbitrary",)),
    )(x, g.reshape(1, Dm), target)


def _split_bf16(v):
    hi = v.astype(BF16)
    r1 = v - hi.astype(F32)
    mid = r1.astype(BF16)
    lo = (r1 - mid.astype(F32)).astype(BF16)
    return hi, mid, lo


def _tri_dot(tri, v):
    hi, mid, lo = _split_bf16(v)
    dot = functools.partial(jnp.dot, preferred_element_type=F32)
    return dot(tri, hi) + dot(tri, mid) + dot(tri, lo)


def _log_sigmoid(z):
    return jnp.minimum(z, 0.0) - jnp.log(1.0 + jnp.exp(-jnp.abs(z)))


def _fox_cumsum_fwd(f, bf, n_seq, name):
    T = f.shape[0]
    S = T // n_seq
    c = min(CUM_BLK, S)

    def body(f_ref, b_ref, out_ref):
        ri = lax.broadcasted_iota(jnp.int32, (c, c), 0)
        ci = lax.broadcasted_iota(jnp.int32, (c, c), 1)
        tri = (ri >= ci).astype(BF16)
        carry = jnp.zeros((1, LANES), F32)
        for j in range(S // c):
            lf = _log_sigmoid(f_ref[j * c:(j + 1) * c, :] + b_ref[...])
            out_ref[j * c:(j + 1) * c, :] = _tri_dot(tri, lf) + carry
            carry = carry + jnp.sum(lf, axis=0, keepdims=True)

    blk = pl.BlockSpec((S, LANES), lambda b: (b, 0))
    return pl.pallas_call(
        body, name=name, grid=(n_seq,), in_specs=[blk, pl.BlockSpec((1, LANES), lambda b: (0, 0))],
        out_specs=blk, out_shape=jax.ShapeDtypeStruct((T, LANES), F32),
        compiler_params=_cp(("parallel",)),
    )(f, bf)


def _fox_cumsum_bwd(f, bf, dF, n_seq, name):
    T = f.shape[0]
    S = T // n_seq
    c = min(CUM_BLK, S)

    def body(f_ref, b_ref, dF_ref, df_ref, db_ref):
        b = pl.program_id(0)
        ri = lax.broadcasted_iota(jnp.int32, (c, c), 0)
        ci = lax.broadcasted_iota(jnp.int32, (c, c), 1)
        tri = (ri <= ci).astype(BF16)
        carry = jnp.zeros((1, LANES), F32)
        dbp = jnp.zeros((1, LANES), F32)
        for j in reversed(range(S // c)):
            dFc = dF_ref[j * c:(j + 1) * c, :]
            dlf = _tri_dot(tri, dFc) + carry
            carry = carry + jnp.sum(dFc, axis=0, keepdims=True)
            z = f_ref[j * c:(j + 1) * c, :] + b_ref[...]
            dz = dlf * _sigmoid(-z)
            df_ref[j * c:(j + 1) * c, :] = dz.astype(BF16)
            dbp = dbp + jnp.sum(dz, axis=0, keepdims=True)

        @pl.when(b == 0)
        def _():
            db_ref[...] = dbp

        @pl.when(b > 0)
        def _():
            db_ref[...] += dbp

    blk = pl.BlockSpec((S, LANES), lambda b: (b, 0))
    vec = pl.BlockSpec((1, LANES), lambda b: (0, 0))
    return pl.pallas_call(
        body, name=name, grid=(n_seq,), in_specs=[blk, vec, blk], out_specs=[blk, vec],
        out_shape=[jax.ShapeDtypeStruct((T, LANES), BF16), jax.ShapeDtypeStruct((1, LANES), F32)],
        compiler_params=_cp(("arbitrary",)),
    )(f, bf, dF)


def _pair_masks():
    lane = lax.broadcasted_iota(jnp.int32, (1, LANES), 1)
    lo = lane < HEAD_DIM
    return lo, jnp.logical_not(lo)


def _attn_logits(q, k, fq, fk, sel, mask, scale):
    qm = jnp.where(sel, q, jnp.zeros_like(q))
    s = lax.dot_general(qm, k, (((1,), (1,)), ((), ())), preferred_element_type=F32) * scale
    s = s + fq - fk
    return jnp.where(mask, s, NEG_INF)


def _causal_mask(qi, ki, blk):
    row = qi * blk + lax.broadcasted_iota(jnp.int32, (blk, blk), 0)
    col = ki * blk + lax.broadcasted_iota(jnp.int32, (blk, blk), 1)
    return col <= row


def _attn_fwd(proj, Fq, Fk, n_seq, name):
    T = proj.shape[0]
    S = T // n_seq
    blk = min(ATT_BLK, S)
    nb = S // blk
    scale = HEAD_DIM ** -0.5
    qc, kc, vc = OFF_Q // LANES, OFF_K // LANES, OFF_V // LANES

    def body(q_ref, k_ref, v_ref, fq_ref, fk_ref, o_ref, o32_ref, lse_ref, m_s, l_s, acc_s):
        qi, ki = pl.program_id(2), pl.program_id(3)

        @pl.when(ki == 0)
        def _():
            m_s[...] = jnp.full_like(m_s, NEG_INF)
            l_s[...] = jnp.zeros_like(l_s)
            acc_s[...] = jnp.zeros_like(acc_s)

        @pl.when(ki <= qi)
        def _():
            q, k, v = q_ref[...], k_ref[...], v_ref[...]
            mask = _causal_mask(qi, ki, blk)
            for hh, sel in enumerate(_pair_masks()):
                s = _attn_logits(q, k, fq_ref[hh], fk_ref[hh], sel, mask, scale)
                m_prev = m_s[hh]
                m_new = jnp.maximum(m_prev, jnp.max(s, axis=-1, keepdims=True))
                alpha = jnp.exp(m_prev - m_new)
                p = jnp.exp(s - m_new)
                l_s[hh] = alpha * l_s[hh] + jnp.sum(p, axis=-1, keepdims=True)
                p_hi = p.astype(BF16)
                p_lo = (p - p_hi.astype(F32)).astype(BF16)
                pv = jnp.dot(p_hi, v, preferred_element_type=F32) + jnp.dot(p_lo, v, preferred_element_type=F32)
                acc_s[hh] = alpha * acc_s[hh] + pv
                m_s[hh] = m_new

        @pl.when(ki == qi)
        def _():
            lo, _ = _pair_masks()
            o = jnp.where(lo, acc_s[0] / l_s[0], acc_s[1] / l_s[1])
            o_ref[...] = o.astype(BF16)
            o32_ref[...] = o
            lse_ref[0] = m_s[0] + jnp.log(l_s[0])
            lse_ref[1] = m_s[1] + jnp.log(l_s[1])

    grid = (n_seq, N_HEADS // 2, nb, nb)
    return pl.pallas_call(
        body, name=name, grid=grid,
        in_specs=[
            pl.BlockSpec((blk, LANES), lambda b, j, qi, ki: (b * nb + qi, qc + j)),
            pl.BlockSpec((blk, LANES), lambda b, j, qi, ki: (b * nb + jnp.minimum(ki, qi), kc + j)),
            pl.BlockSpec((blk, LANES), lambda b, j, qi, ki: (b * nb + jnp.minimum(ki, qi), vc + j)),
            pl.BlockSpec((2, blk, 1), lambda b, j, qi, ki: (j, b * nb + qi, 0)),
            pl.BlockSpec((2, 1, blk), lambda b, j, qi, ki: (j, 0, b * nb + jnp.minimum(ki, qi))),
        ],
        out_specs=[
            pl.BlockSpec((blk, LANES), lambda b, j, qi, ki: (b * nb + qi, j)),
            pl.BlockSpec((blk, LANES), lambda b, j, qi, ki: (b * nb + qi, j)),
            pl.BlockSpec((2, blk, 1), lambda b, j, qi, ki: (j, b * nb + qi, 0)),
        ],
        out_shape=[jax.ShapeDtypeStruct((T, BRANCH_W), BF16), jax.ShapeDtypeStruct((T, BRANCH_W), F32),
                   jax.ShapeDtypeStruct((N_HEADS, T, 1), F32)],
        scratch_shapes=[pltpu.VMEM((2, blk, 1), F32), pltpu.VMEM((2, blk, 1), F32),
                        pltpu.VMEM((2, blk, LANES), F32)],
        compiler_params=_cp(("parallel", "parallel", "parallel", "arbitrary")),
    )(proj, proj, proj, Fq, Fk)


def _attn_delta(do, o, name):
    T = do.shape[0]
    tm = min(512, T)

    def body(do_ref, o_ref, d_ref):
        prod = do_ref[...].astype(F32) * o_ref[...].astype(F32)
        lo, hi = _pair_masks()
        for j in range(N_HEADS // 2):
            pj = prod[:, j * LANES:(j + 1) * LANES]
            d_ref[2 * j] = jnp.sum(jnp.where(lo, pj, 0.0), axis=-1, keepdims=True)
            d_ref[2 * j + 1] = jnp.sum(jnp.where(hi, pj, 0.0), axis=-1, keepdims=True)

    row = pl.BlockSpec((tm, BRANCH_W), lambda i: (i, 0))
    return pl.pallas_call(
        body, name=name, grid=(T // tm,), in_specs=[row, row],
        out_specs=pl.BlockSpec((N_HEADS, tm, 1), lambda i: (0, i, 0)),
        out_shape=jax.ShapeDtypeStruct((N_HEADS, T, 1), F32),
        compiler_params=_cp(("parallel",)),
    )(do, o)


def _attn_bwd_dq(proj, do, lse, delta, Fq, Fk, n_seq, name):
    T = proj.shape[0]
    S = T // n_seq
    blk = min(ATT_BLK, S)
    nb = S // blk
    scale = HEAD_DIM ** -0.5
    qc, kc, vc = OFF_Q // LANES, OFF_K // LANES, OFF_V // LANES

    def body(q_ref, k_ref, v_ref, do_ref, lse_ref, dl_ref, fq_ref, fk_ref, dq_ref, acc_s):
        qi, ki = pl.program_id(2), pl.program_id(3)

        @pl.when(ki == 0)
        def _():
            acc_s[...] = jnp.zeros_like(acc_s)

        @pl.when(ki <= qi)
        def _():
            q, k, v, do_ = q_ref[...], k_ref[...], v_ref[...], do_ref[...]
            mask = _causal_mask(qi, ki, blk)
            for hh, sel in enumerate(_pair_masks()):
                s = _attn_logits(q, k, fq_ref[hh], fk_ref[hh], sel, mask, scale)
                p = jnp.exp(s - lse_ref[hh])
                dom = jnp.where(sel, do_, jnp.zeros_like(do_))
                dp = lax.dot_general(dom, v, (((1,), (1,)), ((), ())), preferred_element_type=F32)
                ds = p * (dp - dl_ref[hh])
                acc_s[hh] += jnp.dot(ds.astype(BF16), k, preferred_element_type=F32)

        @pl.when(ki == qi)
        def _():
            lo, _ = _pair_masks()
            dq_ref[...] = (jnp.where(lo, acc_s[0], acc_s[1]) * scale).astype(BF16)

    qmap = lambda b, j, qi, ki: (b * nb + qi, j)
    col1 = pl.BlockSpec((2, blk, 1), lambda b, j, qi, ki: (j, b * nb + qi, 0))
    return pl.pallas_call(
        body, name=name, grid=(n_seq, N_HEADS // 2, nb, nb),
        in_specs=[
            pl.BlockSpec((blk, LANES), lambda b, j, qi, ki: (b * nb + qi, qc + j)),
            pl.BlockSpec((blk, LANES), lambda b, j, qi, ki: (b * nb + jnp.minimum(ki, qi), kc + j)),
            pl.BlockSpec((blk, LANES), lambda b, j, qi, ki: (b * nb + jnp.minimum(ki, qi), vc + j)),
            pl.BlockSpec((blk, LANES), qmap),
            col1, col1, col1,
            pl.BlockSpec((2, 1, blk), lambda b, j, qi, ki: (j, 0, b * nb + jnp.minimum(ki, qi))),
        ],
        out_specs=pl.BlockSpec((blk, LANES), qmap),
        out_shape=jax.ShapeDtypeStruct((T, BRANCH_W), BF16),
        scratch_shapes=[pltpu.VMEM((2, blk, LANES), F32)],
        compiler_params=_cp(("parallel", "parallel", "parallel", "arbitrary")),
    )(proj, proj, proj, do, lse, delta, Fq, Fk)


def _attn_bwd_dkv(proj, do, lse, delta, Fq, Fk, n_seq, name):
    T = proj.shape[0]
    S = T // n_seq
    blk = min(ATT_BLK, S)
    nb = S // blk
    scale = HEAD_DIM ** -0.5
    qc, kc, vc = OFF_Q // LANES, OFF_K // LANES, OFF_V // LANES
    tdot = functools.partial(lax.dot_general, dimension_numbers=(((0,), (0,)), ((), ())),
                             preferred_element_type=F32)

    def body(q_ref, k_ref, v_ref, do_ref, lse_ref, dl_ref, fq_ref, fk_ref, dk_ref, dv_ref, dfk_ref,
             dk_s, dv_s, df_s):
        ki, qi = pl.program_id(2), pl.program_id(3)

        @pl.when(qi == 0)
        def _():
            dk_s[...] = jnp.zeros_like(dk_s)
            dv_s[...] = jnp.zeros_like(dv_s)
            df_s[...] = jnp.zeros_like(df_s)

        @pl.when(qi >= ki)
        def _():
            q, k, v, do_ = q_ref[...], k_ref[...], v_ref[...], do_ref[...]
            mask = _causal_mask(qi, ki, blk)
            for hh, sel in enumerate(_pair_masks()):
                s = _attn_logits(q, k, fq_ref[hh], fk_ref[hh], sel, mask, scale)
                p = jnp.exp(s - lse_ref[hh])
                dv_s[hh] += tdot(p.astype(BF16), do_)
                dom = jnp.where(sel, do_, jnp.zeros_like(do_))
                dp = lax.dot_general(dom, v, (((1,), (1,)), ((), ())), preferred_element_type=F32)
                ds = p * (dp - dl_ref[hh])
                dk_s[hh] += tdot(ds.astype(BF16), q)
                df_s[hh] -= jnp.sum(ds, axis=0, keepdims=True)

        @pl.when(qi == nb - 1)
        def _():
            lo, _ = _pair_masks()
            dk_ref[...] = (jnp.where(lo, dk_s[0], dk_s[1]) * scale).astype(BF16)
            dv_ref[...] = jnp.where(lo, dv_s[0], dv_s[1]).astype(BF16)
            dfk_ref[...] = df_s[...]

    kmap = lambda b, j, ki, qi: (b * nb + ki, j)
    col1 = pl.BlockSpec((2, blk, 1), lambda b, j, ki, qi: (j, b * nb + jnp.maximum(qi, ki), 0))
    rowk = pl.BlockSpec((2, 1, blk), lambda b, j, ki, qi: (j, 0, b * nb + ki))
    return pl.pallas_call(
        body, name=name, grid=(n_seq, N_HEADS // 2, nb, nb),
        in_specs=[
            pl.BlockSpec((blk, LANES), lambda b, j, ki, qi: (b * nb + jnp.maximum(qi, ki), qc + j)),
            pl.BlockSpec((blk, LANES), lambda b, j, ki, qi: (b * nb + ki, kc + j)),
            pl.BlockSpec((blk, LANES), lambda b, j, ki, qi: (b * nb + ki, vc + j)),
            pl.BlockSpec((blk, LANES), lambda b, j, ki, qi: (b * nb + jnp.maximum(qi, ki), j)),
            col1, col1, col1, rowk,
        ],
        out_specs=[pl.BlockSpec((blk, LANES), kmap), pl.BlockSpec((blk, LANES), kmap), rowk],
        out_shape=[jax.ShapeDtypeStruct((T, BRANCH_W), BF16), jax.ShapeDtypeStruct((T, BRANCH_W), BF16),
                   jax.ShapeDtypeStruct((N_HEADS, 1, T), F32)],
        scratch_shapes=[pltpu.VMEM((2, blk, LANES), F32), pltpu.VMEM((2, blk, LANES), F32),
                        pltpu.VMEM((2, 1, blk), F32)],
        compiler_params=_cp(("parallel", "parallel", "parallel", "arbitrary")),
    )(proj, proj, proj, do, lse, delta, Fq, Fk)


AUG0 = HEAD_DIM
Q_TILE, K_CHUNK, ROW_GROUP = 512, 256, 64


def _fox_prep(f, bf, proj, n_seq, name):
    T = f.shape[0]
    S = T // n_seq
    c = min(CUM_BLK, S)

    def body(f_ref, b_ref, q_ref, k_ref, v_ref, qa_ref, ka_ref, va_ref):
        ri = lax.broadcasted_iota(jnp.int32, (c, c), 0)
        ci = lax.broadcasted_iota(jnp.int32, (c, c), 1)
        tri = (ri >= ci).astype(BF16)
        lane = lax.broadcasted_iota(jnp.int32, (c, LANES), 1)
        carry = jnp.zeros((1, LANES), F32)
        for j in range(S // c):
            rows = slice(j * c, (j + 1) * c)
            lf = _log_sigmoid(f_ref[rows, :] + b_ref[...])
            Fc = _tri_dot(tri, lf) + carry
            carry = carry + jnp.sum(lf, axis=0, keepdims=True)
            for h in range(N_HEADS):
                col = jnp.sum(jnp.where(lane == h, Fc, 0.0), axis=-1, keepdims=True)
                hi = col.astype(BF16).astype(F32)
                r1 = col - hi
                mid = r1.astype(BF16).astype(F32)
                lo = r1 - mid
                ones_q = jnp.logical_and(lane >= AUG0 + 3, lane < AUG0 + 6)
                ones_k = jnp.logical_and(lane >= AUG0, lane < AUG0 + 3)
                aug_q = jnp.where(lane == AUG0, hi, jnp.where(lane == AUG0 + 1, mid, jnp.where(
                    lane == AUG0 + 2, lo, jnp.where(ones_q, 1.0, 0.0))))
                aug_k = jnp.where(lane == AUG0 + 3, -hi, jnp.where(lane == AUG0 + 4, -mid, jnp.where(
                    lane == AUG0 + 5, -lo, jnp.where(ones_k, 1.0, 0.0))))
                pair = slice((h // 2) * LANES, (h // 2 + 1) * LANES)
                qp, kp = q_ref[rows, pair].astype(F32), k_ref[rows, pair].astype(F32)
                vp = v_ref[rows, pair].astype(F32)
                if h % 2:
                    qp, kp, vp = (pltpu.roll(a, HEAD_DIM, 1) for a in (qp, kp, vp))
                out = slice(h * LANES, (h + 1) * LANES)
                qa_ref[rows, out] = jnp.where(lane < HEAD_DIM, qp * (HEAD_DIM ** -0.5), aug_q).astype(BF16)
                ka_ref[rows, out] = jnp.where(lane < HEAD_DIM, kp, aug_k).astype(BF16)
                va_ref[rows, out] = jnp.where(lane < HEAD_DIM, vp, jnp.where(lane == AUG0, 1.0, 0.0)).astype(BF16)

    fblk = pl.BlockSpec((S, LANES), lambda b: (b, 0))
    out = pl.BlockSpec((S, N_HEADS * LANES), lambda b: (b, 0))
    sh = jax.ShapeDtypeStruct((T, N_HEADS * LANES), BF16)
    return pl.pallas_call(
        body, name=name, grid=(n_seq,),
        in_specs=[fblk, pl.BlockSpec((1, LANES), lambda b: (0, 0)),
                  pl.BlockSpec((S, BRANCH_W), lambda b: (b, OFF_Q // BRANCH_W)),
                  pl.BlockSpec((S, BRANCH_W), lambda b: (b, OFF_K // BRANCH_W)),
                  pl.BlockSpec((S, BRANCH_W), lambda b: (b, OFF_V // BRANCH_W))],
        out_specs=[out, out, out], out_shape=[sh, sh, sh],
        compiler_params=_cp(("parallel",)),
    )(f, bf, proj, proj, proj)


def _band_mask(q0, k0, nq, nk):
    row = q0 + lax.broadcasted_iota(jnp.int32, (nq, nk), 0)
    col = k0 + lax.broadcasted_iota(jnp.int32, (nq, nk), 1)
    return col <= row


_NT = (((1,), (1,)), ((), ()))
_TN = (((0,), (0,)), ((), ()))


def _attn_fwd2(qa, ka, va, n_seq, name):
    T = qa.shape[0]
    S = T // n_seq
    tq, tk, rg = min(Q_TILE, S), min(K_CHUNK, S), ROW_GROUP
    nq, per = S // tq, tq // tk

    def body(q_ref, k_ref, v_ref, o_ref, o32_ref, lse_ref, phi_s, plo_s, mp_s, m_s, acc_s):
        qi = pl.program_id(2)
        mp_s[...] = jnp.full_like(mp_s, NEG_INF)
        acc_s[...] = jnp.zeros_like(acc_s)

        def scores(kc, hh):
            k0 = pl.multiple_of(kc * tk, tk)
            hl = slice(hh * LANES, (hh + 1) * LANES)
            return k0, lax.dot_general(q_ref[:, hl], k_ref[pl.ds(k0, tk), hl], _NT, preferred_element_type=F32)

        def max_chunk(kc, masked):
            for hh in range(2):
                k0, s_all = scores(kc, hh)
                for r in range(tq // rg):
                    rows = slice(r * rg, (r + 1) * rg)
                    s = s_all[rows, :]
                    if masked:
                        s = jnp.where(_band_mask(qi * tq + r * rg, k0, rg, tk), s, NEG_INF)
                    part = s[:, :LANES]
                    for c in range(1, tk // LANES):
                        part = jnp.maximum(part, s[:, c * LANES:(c + 1) * LANES])
                    mp_s[hh, rows, :] = jnp.maximum(mp_s[hh, rows, :], part)

        def sum_chunk(kc, masked):
            for hh in range(2):
                k0, s_all = scores(kc, hh)
                hl = slice(hh * LANES, (hh + 1) * LANES)
                v = v_ref[pl.ds(k0, tk), hl]
                for r in range(tq // rg):
                    rows = slice(r * rg, (r + 1) * rg)
                    p = jnp.exp(s_all[rows, :] - m_s[hh, rows])
                    if masked:
                        p = jnp.where(_band_mask(qi * tq + r * rg, k0, rg, tk), p, 0.0)
                    p_hi = p.astype(BF16)
                    phi_s[hh, rows, :] = p_hi
                    plo_s[hh, rows, :] = (p - p_hi.astype(F32)).astype(BF16)
                acc_s[hh] += (jnp.dot(phi_s[hh], v, preferred_element_type=F32)
                              + jnp.dot(plo_s[hh], v, preferred_element_type=F32))

        def sweep(chunk):
            def unmasked(kc, carry):
                chunk(kc, False)
                return carry

            lax.fori_loop(0, qi * per, unmasked, 0)
            for d in range(per):
                chunk(qi * per + d, True)

        sweep(max_chunk)
        m_s[...] = jnp.max(mp_s[...], axis=-1, keepdims=True)
        sweep(sum_chunk)

        lane = lax.broadcasted_iota(jnp.int32, (1, LANES), 1)
        outs = []
        for hh in range(2):
            acc = acc_s[hh]
            l = jnp.sum(jnp.where(lane == AUG0, acc, 0.0), axis=-1, keepdims=True)
            lse_ref[hh] = m_s[hh] + jnp.log(l)
            outs.append(acc / l)
        o = jnp.where(lane < HEAD_DIM, outs[0], pltpu.roll(outs[1], HEAD_DIM, 1))
        o_ref[...] = o.astype(BF16)
        o32_ref[...] = o

    qmap = lambda b, j, qi: (b * nq + qi, j)
    omap = lambda b, j, qi: (b * nq + qi, j)
    kv = pl.BlockSpec((S, 2 * LANES), lambda b, j, qi: (b, j))
    return pl.pallas_call(
        body, name=name, grid=(n_seq, N_HEADS // 2, nq),
        in_specs=[pl.BlockSpec((tq, 2 * LANES), qmap), kv, kv],
        out_specs=[pl.BlockSpec((tq, LANES), omap), pl.BlockSpec((tq, LANES), omap),
                   pl.BlockSpec((2, tq, 1), lambda b, j, qi: (j, b * nq + qi, 0))],
        out_shape=[jax.ShapeDtypeStruct((T, BRANCH_W), BF16), jax.ShapeDtypeStruct((T, BRANCH_W), F32),
                   jax.ShapeDtypeStruct((N_HEADS, T, 1), F32)],
        scratch_shapes=[pltpu.VMEM((2, tq, tk), BF16), pltpu.VMEM((2, tq, tk), BF16),
                        pltpu.VMEM((2, tq, LANES), F32), pltpu.VMEM((2, tq, 1), F32),
                        pltpu.VMEM((2, tq, LANES), F32)],
        compiler_params=_cp(("parallel", "parallel", "parallel")),
    )(qa, ka, va)


def _attn_bwd_dq2(qa, ka, proj, do, lse, delta, n_seq, name):
    T = qa.shape[0]
    S = T // n_seq
    tq, tk, rg = min(Q_TILE, S), min(K_CHUNK, S), ROW_GROUP
    nq, per = S // tq, tq // tk
    vc = OFF_V // LANES

    def body(q_ref, k_ref, v_ref, do_ref, lse_ref, dl_ref, dq_ref, ds_s, acc_s):
        qi = pl.program_id(2)
        acc_s[...] = jnp.zeros_like(acc_s)
        sels = _pair_masks()

        def chunk(kc, masked):
            k0 = pl.multiple_of(kc * tk, tk)
            v = v_ref[pl.ds(k0, tk), :]
            for hh in range(2):
                hl = slice(hh * LANES, (hh + 1) * LANES)
                kh = k_ref[pl.ds(k0, tk), hl]
                s_all = lax.dot_general(q_ref[:, hl], kh, _NT, preferred_element_type=F32)
                dom = jnp.where(sels[hh], do_ref[...], jnp.zeros_like(do_ref[...]))
                dp_all = lax.dot_general(dom, v, _NT, preferred_element_type=F32)
                for r in range(tq // rg):
                    rows = slice(r * rg, (r + 1) * rg)
                    p = jnp.exp(s_all[rows, :] - lse_ref[hh, rows])
                    if masked:
                        p = jnp.where(_band_mask(qi * tq + r * rg, k0, rg, tk), p, 0.0)
                    ds_s[hh, rows, :] = (p * (dp_all[rows, :] - dl_ref[hh, rows])).astype(BF16)
                acc_s[hh] += jnp.dot(ds_s[hh], kh, preferred_element_type=F32)

        def unmasked(kc, carry):
            chunk(kc, False)
            return carry

        lax.fori_loop(0, qi * per, unmasked, 0)
        for d in range(per):
            chunk(qi * per + d, True)
        dq = jnp.where(sels[0], acc_s[0], pltpu.roll(acc_s[1], HEAD_DIM, 1))
        dq_ref[...] = (dq * (HEAD_DIM ** -0.5)).astype(BF16)

    qmap = lambda b, j, qi: (b * nq + qi, j)
    col1 = pl.BlockSpec((2, tq, 1), lambda b, j, qi: (j, b * nq + qi, 0))
    return pl.pallas_call(
        body, name=name, grid=(n_seq, N_HEADS // 2, nq),
        in_specs=[pl.BlockSpec((tq, 2 * LANES), qmap),
                  pl.BlockSpec((S, 2 * LANES), lambda b, j, qi: (b, j)),
                  pl.BlockSpec((S, LANES), lambda b, j, qi: (b, vc + j)),
                  pl.BlockSpec((tq, LANES), qmap), col1, col1],
        out_specs=pl.BlockSpec((tq, LANES), qmap),
        out_shape=jax.ShapeDtypeStruct((T, BRANCH_W), BF16),
        scratch_shapes=[pltpu.VMEM((2, tq, tk), BF16), pltpu.VMEM((2, tq, LANES), F32)],
        compiler_params=_cp(("parallel", "parallel", "parallel")),
    )(qa, ka, proj, do, lse, delta)


def _attn_bwd_dkv2(qa, ka, proj, do, lse, delta, n_seq, name):
    T = qa.shape[0]
    S = T // n_seq
    tkt, tqc, rg = min(Q_TILE, S), min(K_CHUNK, S), ROW_GROUP // 2
    nk, per, nqc = S // tkt, tkt // tqc, S // tqc
    vc = OFF_V // LANES

    def body(q_ref, k_ref, v_ref, do_ref, lse_ref, dl_ref, dk_ref, dv_ref, dfk_ref,
             p_s, ds_s, dk_s, dv_s, df_s):
        ki = pl.program_id(2)
        dk_s[...] = jnp.zeros_like(dk_s)
        dv_s[...] = jnp.zeros_like(dv_s)
        df_s[...] = jnp.zeros_like(df_s)
        sels = _pair_masks()
        v = v_ref[...]

        def chunk(qc, masked):
            q0 = pl.multiple_of(qc * tqc, tqc)
            do_ = do_ref[pl.ds(q0, tqc), :]
            for hh in range(2):
                hl = slice(hh * LANES, (hh + 1) * LANES)
                qh = q_ref[pl.ds(q0, tqc), hl]
                s_all = lax.dot_general(qh, k_ref[:, hl], _NT, preferred_element_type=F32)
                dom = jnp.where(sels[hh], do_, jnp.zeros_like(do_))
                dp_all = lax.dot_general(dom, v, _NT, preferred_element_type=F32)
                dfp = jnp.zeros((1, tkt), F32)
                for r in range(tqc // rg):
                    rows = slice(r * rg, (r + 1) * rg)
                    qrows = pl.ds(q0 + r * rg, rg)
                    p = jnp.exp(s_all[rows, :] - lse_ref[hh, qrows])
                    if masked:
                        p = jnp.where(_band_mask(q0 + r * rg, ki * tkt, rg, tkt), p, 0.0)
                    ds = p * (dp_all[rows, :] - dl_ref[hh, qrows])
                    p_s[hh, rows, :] = p.astype(BF16)
                    ds_s[hh, rows, :] = ds.astype(BF16)
                    dfp = dfp + jnp.sum(ds, axis=0, keepdims=True)
                df_s[hh] -= dfp
                dv_s[hh] += lax.dot_general(p_s[hh], do_, _TN, preferred_element_type=F32)
                dk_s[hh] += lax.dot_general(ds_s[hh], qh, _TN, preferred_element_type=F32)

        for d in range(per):
            chunk(ki * per + d, True)

        def unmasked(qc, carry):
            chunk(qc, False)
            return carry

        lax.fori_loop((ki + 1) * per, nqc, unmasked, 0)
        dk_ref[...] = jnp.where(sels[0], dk_s[0], pltpu.roll(dk_s[1], HEAD_DIM, 1)).astype(BF16)
        dv_ref[...] = jnp.where(sels[0], dv_s[0], dv_s[1]).astype(BF16)
        dfk_ref[...] = df_s[...]

    kmap = lambda b, j, ki: (b * nk + ki, j)
    col1 = pl.BlockSpec((2, S, 1), lambda b, j, ki: (j, b, 0))
    rowk = pl.BlockSpec((2, 1, tkt), lambda b, j, ki: (j, 0, b * nk + ki))
    return pl.pallas_call(
        body, name=name, grid=(n_seq, N_HEADS // 2, nk),
        in_specs=[pl.BlockSpec((S, 2 * LANES), lambda b, j, ki: (b, j)),
                  pl.BlockSpec((tkt, 2 * LANES), kmap),
                  pl.BlockSpec((tkt, LANES), lambda b, j, ki: (b * nk + ki, vc + j)),
                  pl.BlockSpec((S, LANES), lambda b, j, ki: (b, j)), col1, col1],
        out_specs=[pl.BlockSpec((tkt, LANES), kmap), pl.BlockSpec((tkt, LANES), kmap), rowk],
        out_shape=[jax.ShapeDtypeStruct((T, BRANCH_W), BF16), jax.ShapeDtypeStruct((T, BRANCH_W), BF16),
                   jax.ShapeDtypeStruct((N_HEADS, 1, T), F32)],
        scratch_shapes=[pltpu.VMEM((2, tqc, tkt), BF16), pltpu.VMEM((2, tqc, tkt), BF16),
                        pltpu.VMEM((2, tkt, LANES), F32),
                        pltpu.VMEM((2, tkt, LANES), F32), pltpu.VMEM((2, 1, tkt), F32)],
        compiler_params=_cp(("parallel", "parallel", "parallel")),
    )(qa, ka, proj, do, lse, delta)


def _shift_down(v, k, row):
    return jnp.where(row >= k, pltpu.roll(v, k, 0), 0.0)


def _shift_up(v, k, row, S):
    return jnp.where(row < S - k, pltpu.roll(v, S - k, 0), 0.0)


def _pool_diff(uf, w, row):
    acc, k = uf, 1
    while k < w:
        acc = acc + _shift_down(acc, k, row)
        k *= 2
    n = jnp.minimum(row + 1, w).astype(F32)
    return acc / n - uf


def _pool_fwd(proj, pool_w, pool_scale, n_seq, name):
    T = proj.shape[0]
    S = T // n_seq

    def body(u_ref, w_ref, sc_ref, o_ref):
        g = pl.program_id(1)
        row = lax.broadcasted_iota(jnp.int32, (S, POOL_GD), 0)
        uf = u_ref[...].astype(F32)
        d = _pool_diff(uf, POOL_WINDOWS[0], row)
        for gi in range(1, len(POOL_WINDOWS)):
            d = jnp.where(g == gi, _pool_diff(uf, POOL_WINDOWS[gi], row), d)
        e = jnp.dot(d.astype(BF16), w_ref[0], preferred_element_type=F32)
        o_ref[...] = (e * sc_ref[...]).astype(BF16)

    uc = OFF_U // POOL_GD
    return pl.pallas_call(
        body, name=name, grid=(n_seq, len(POOL_WINDOWS)),
        in_specs=[pl.BlockSpec((S, POOL_GD), lambda b, g: (b, uc + g)),
                  pl.BlockSpec((1, POOL_GD, POOL_GD), lambda b, g: (g, 0, 0)),
                  pl.BlockSpec((1, POOL_GD), lambda b, g: (0, g))],
        out_specs=pl.BlockSpec((S, POOL_GD), lambda b, g: (b, g)),
        out_shape=jax.ShapeDtypeStruct((T, BRANCH_W), BF16),
        compiler_params=_cp(("parallel", "parallel")),
    )(proj, pool_w, pool_scale)


def _pool_bwd(proj, dout, pool_w, pool_scale, n_seq, name):
    T = proj.shape[0]
    S = T // n_seq

    def body(u_ref, do_ref, w_ref, sc_ref, du_ref, dw_ref, dsc_ref):
        g, b = pl.program_id(0), pl.program_id(1)
        row = lax.broadcasted_iota(jnp.int32, (S, POOL_GD), 0)
        uf = u_ref[...].astype(F32)
        d = _pool_diff(uf, POOL_WINDOWS[0], row)
        for gi in range(1, len(POOL_WINDOWS)):
            d = jnp.where(g == gi, _pool_diff(uf, POOL_WINDOWS[gi], row), d)
        db16 = d.astype(BF16)
        w = w_ref[0]
        e = jnp.dot(db16, w, preferred_element_type=F32)
        dof = do_ref[...].astype(F32)
        dsc = jnp.sum(dof * e, axis=0, keepdims=True)
        de = (dof * sc_ref[...]).astype(BF16)
        dd = lax.dot_general(de, w, (((1,), (1,)), ((), ())), preferred_element_type=F32)
        dw = lax.dot_general(db16, de, (((0,), (0,)), ((), ())), preferred_element_type=F32)
        du = jnp.zeros_like(dd)
        for gi, wlen in enumerate(POOL_WINDOWS):
            n = jnp.minimum(row + 1, wlen).astype(F32)
            acc, k = dd / n, 1
            while k < wlen:
                acc = acc + _shift_up(acc, k, row, S)
                k *= 2
            du = jnp.where(g == gi, acc - dd, du)
        du_ref[...] = du.astype(BF16)

        @pl.when(b == 0)
        def _():
            dw_ref[0] = dw
            dsc_ref[...] = dsc

        @pl.when(b > 0)
        def _():
            dw_ref[0] += dw
            dsc_ref[...] += dsc

    uc = OFF_U // POOL_GD
    return pl.pallas_call(
        body, name=name, grid=(len(POOL_WINDOWS), n_seq),
        in_specs=[pl.BlockSpec((S, POOL_GD), lambda g, b: (b, uc + g)),
                  pl.BlockSpec((S, POOL_GD), lambda g, b: (b, g)),
                  pl.BlockSpec((1, POOL_GD, POOL_GD), lambda g, b: (g, 0, 0)),
                  pl.BlockSpec((1, POOL_GD), lambda g, b: (0, g))],
        out_specs=[pl.BlockSpec((S, POOL_GD), lambda g, b: (b, g)),
                   pl.BlockSpec((1, POOL_GD, POOL_GD), lambda g, b: (g, 0, 0)),
                   pl.BlockSpec((1, POOL_GD), lambda g, b: (0, g))],
        out_shape=[jax.ShapeDtypeStruct((T, BRANCH_W), BF16),
                   jax.ShapeDtypeStruct((len(POOL_WINDOWS), POOL_GD, POOL_GD), F32),
                   jax.ShapeDtypeStruct((1, BRANCH_W), F32)],
        compiler_params=_cp(("parallel", "arbitrary")),
    )(proj, dout, pool_w, pool_scale)


def _conv_fwd(proj, conv_w, n_seq, name):
    T = proj.shape[0]
    S = T // n_seq
    nc = BRANCH_W // LANES

    def body(cv_ref, cb_ref, cc_ref, w_ref, o_ref):
        row = lax.broadcasted_iota(jnp.int32, (S, LANES), 0)
        z = cc_ref[...].astype(F32) * cv_ref[...].astype(F32)
        w = w_ref[...]
        y = w[0:1] * _shift_down(z, 2, row) + w[1:2] * _shift_down(z, 1, row) + w[2:3] * z
        o_ref[...] = (cb_ref[...].astype(F32) * y).astype(BF16)

    def col(off):
        return pl.BlockSpec((S, LANES), lambda b, j: (b, off // LANES + j))

    return pl.pallas_call(
        body, name=name, grid=(n_seq, nc),
        in_specs=[col(OFF_CV), col(OFF_CB), col(OFF_CC), pl.BlockSpec((CONV_K, LANES), lambda b, j: (0, j))],
        out_specs=pl.BlockSpec((S, LANES), lambda b, j: (b, j)),
        out_shape=jax.ShapeDtypeStruct((T, BRANCH_W), BF16),
        compiler_params=_cp(("parallel", "parallel")),
    )(proj, proj, proj, conv_w)


def _conv_bwd(proj, dout, conv_w, n_seq, name):
    T = proj.shape[0]
    S = T // n_seq
    nc = BRANCH_W // LANES

    def body(cv_ref, cb_ref, cc_ref, do_ref, w_ref, dcv_ref, dcb_ref, dcc_ref, dw_ref):
        b = pl.program_id(1)
        row = lax.broadcasted_iota(jnp.int32, (S, LANES), 0)
        cv, cb, cc = cv_ref[...].astype(F32), cb_ref[...].astype(F32), cc_ref[...].astype(F32)
        dof = do_ref[...].astype(F32)
        w = w_ref[...]
        z = cc * cv
        z1, z2 = _shift_down(z, 1, row), _shift_down(z, 2, row)
        y = w[0:1] * z2 + w[1:2] * z1 + w[2:3] * z
        dcb_ref[...] = (dof * y).astype(BF16)
        dy = dof * cb
        dz = w[2:3] * dy + w[1:2] * _shift_up(dy, 1, row, S) + w[0:1] * _shift_up(dy, 2, row, S)
        dcc_ref[...] = (dz * cv).astype(BF16)
        dcv_ref[...] = (dz * cc).astype(BF16)
        dws = [jnp.sum(dy * zk, axis=0, keepdims=True) for zk in (z2, z1, z)]

        @pl.when(b == 0)
        def _():
            for kk in range(CONV_K):
                dw_ref[kk:kk + 1, :] = dws[kk]

        @pl.when(b > 0)
        def _():
            for kk in range(CONV_K):
                dw_ref[kk:kk + 1, :] += dws[kk]

    def col(off):
        return pl.BlockSpec((S, LANES), lambda j, b: (b, off // LANES + j))

    out = pl.BlockSpec((S, LANES), lambda j, b: (b, j))
    wsp = pl.BlockSpec((CONV_K, LANES), lambda j, b: (0, j))
    act = jax.ShapeDtypeStruct((T, BRANCH_W), BF16)
    return pl.pallas_call(
        body, name=name, grid=(nc, n_seq),
        in_specs=[col(OFF_CV), col(OFF_CB), col(OFF_CC), out, wsp],
        out_specs=[out, out, out, wsp],
        out_shape=[act, act, act, jax.ShapeDtypeStruct((CONV_K, BRANCH_W), F32)],
        compiler_params=_cp(("parallel", "arbitrary")),
    )(proj, proj, proj, dout, conv_w)


def _mix_fwd(oa, ob, oc, wpa, wpp, wpc, proj, b_gate, name):
    T = oa.shape[0]
    tm = min(256, T)

    def body(oa_ref, ob_ref, oc_ref, wa_ref, wp_ref, wc_ref, g_ref, bg_ref, o_ref):
        acc = jnp.zeros((tm, D_MODEL), F32)
        for i, (x_ref, w_ref) in enumerate(((oa_ref, wa_ref), (ob_ref, wp_ref), (oc_ref, wc_ref))):
            y = jnp.dot(x_ref[...], w_ref[...], preferred_element_type=F32)
            sl = slice(i * D_MODEL, (i + 1) * D_MODEL)
            acc = acc + _sigmoid(g_ref[:, sl].astype(F32) + bg_ref[:, sl]) * y
        o_ref[...] = acc.astype(BF16)

    br = pl.BlockSpec((tm, BRANCH_W), lambda i: (i, 0))
    wsp = pl.BlockSpec((BRANCH_W, D_MODEL), lambda i: (0, 0))
    return pl.pallas_call(
        body, name=name, grid=(T // tm,),
        in_specs=[br, br, br, wsp, wsp, wsp, pl.BlockSpec((tm, GATE_W), lambda i: (i, 0)),
                  pl.BlockSpec((1, GATE_W), lambda i: (0, 0))],
        out_specs=pl.BlockSpec((tm, D_MODEL), lambda i: (i, 0)),
        out_shape=jax.ShapeDtypeStruct((T, D_MODEL), BF16),
        compiler_params=_cp(("parallel",)),
    )(oa, ob, oc, wpa, wpp, wpc, proj, b_gate)


def _mix_bwd(oa, ob, oc, wpa, wpp, wpc, proj, b_gate, dmixed, name):
    T = oa.shape[0]
    tm = min(256, T)

    def body(oa_ref, ob_ref, oc_ref, wa_ref, wp_ref, wc_ref, g_ref, bg_ref, dm_ref,
             dya_ref, dyb_ref, dyc_ref, dg_ref, dbg_ref):
        i0 = pl.program_id(0)
        dm = dm_ref[...].astype(F32)
        parts = []
        for i, (x_ref, w_ref, dy_ref) in enumerate(((oa_ref, wa_ref, dya_ref), (ob_ref, wp_ref, dyb_ref),
                                                    (oc_ref, wc_ref, dyc_ref))):
            y = jnp.dot(x_ref[...], w_ref[...], preferred_element_type=F32)
            sl = slice(i * D_MODEL, (i + 1) * D_MODEL)
            gate = _sigmoid(g_ref[:, sl].astype(F32) + bg_ref[:, sl])
            dy_ref[...] = (dm * gate).astype(BF16)
            dgl = dm * y * gate * (1.0 - gate)
            dg_ref[:, sl] = dgl.astype(BF16)
            parts.append(jnp.sum(dgl, axis=0, keepdims=True))

        @pl.when(i0 == 0)
        def _():
            for i in range(3):
                dbg_ref[:, i * D_MODEL:(i + 1) * D_MODEL] = parts[i]

        @pl.when(i0 > 0)
        def _():
            for i in range(3):
                dbg_ref[:, i * D_MODEL:(i + 1) * D_MODEL] += parts[i]

    br = pl.BlockSpec((tm, BRANCH_W), lambda i: (i, 0))
    wsp = pl.BlockSpec((BRANCH_W, D_MODEL), lambda i: (0, 0))
    row = pl.BlockSpec((tm, D_MODEL), lambda i: (i, 0))
    gsp = pl.BlockSpec((tm, GATE_W), lambda i: (i, 0))
    bsp = pl.BlockSpec((1, GATE_W), lambda i: (0, 0))
    act = jax.ShapeDtypeStruct((T, D_MODEL), BF16)
    return pl.pallas_call(
        body, name=name, grid=(T // tm,),
        in_specs=[br, br, br, wsp, wsp, wsp, gsp, bsp, row],
        out_specs=[row, row, row, gsp, bsp],
        out_shape=[act, act, act, jax.ShapeDtypeStruct((T, GATE_W), BF16),
                   jax.ShapeDtypeStruct((1, GATE_W), F32)],
        compiler_params=_cp(("arbitrary",)),
    )(oa, ob, oc, wpa, wpp, wpc, proj, b_gate, dmixed)


GU_TILE = 256


def _gu_col(c):
    t, r = divmod(c, GU_TILE)
    return (t // 2) * GU_TILE + r + (FFN_HIDDEN if t % 2 else 0)


def _gate_up_swiglu(h, w, name):
    T, K = h.shape
    tm = min(2048, T)

    def body(h_ref, w_ref, ab_ref, s_ref):
        prod = jnp.dot(h_ref[...], w_ref[...], preferred_element_type=F32)
        ab_ref[...] = prod.astype(BF16)
        a = prod[:, :GU_TILE]
        s_ref[...] = (a * _sigmoid(a) * prod[:, GU_TILE:]).astype(BF16)

    return pl.pallas_call(
        body, name=name, grid=(T // tm, FFN_HIDDEN // GU_TILE),
        in_specs=[pl.BlockSpec((tm, K), lambda i, j: (i, 0)), pl.BlockSpec((K, 2 * GU_TILE), lambda i, j: (0, j))],
        out_specs=[pl.BlockSpec((tm, 2 * GU_TILE), lambda i, j: (i, j)), pl.BlockSpec((tm, GU_TILE), lambda i, j: (i, j))],
        out_shape=[jax.ShapeDtypeStruct((T, 2 * FFN_HIDDEN), BF16), jax.ShapeDtypeStruct((T, FFN_HIDDEN), BF16)],
        compiler_params=_cp(("parallel", "parallel")),
    )(h, w)


def _swiglu_bwd_fused(dx, w_down, ab, name):
    T, K = dx.shape
    tm = min(2048, T)

    def body(dx_ref, w_ref, ab_ref, o_ref):
        ds = lax.dot_general(dx_ref[...], w_ref[...], _NT, preferred_element_type=F32)
        a = ab_ref[:, :GU_TILE].astype(F32)
        b = ab_ref[:, GU_TILE:].astype(F32)
        sg = _sigmoid(a)
        o_ref[:, :GU_TILE] = (ds * b * sg * (1.0 + a * (1.0 - sg))).astype(BF16)
        o_ref[:, GU_TILE:] = (ds * a * sg).astype(BF16)

    pair = pl.BlockSpec((tm, 2 * GU_TILE), lambda i, j: (i, j))
    return pl.pallas_call(
        body, name=name, grid=(T // tm, FFN_HIDDEN // GU_TILE),
        in_specs=[pl.BlockSpec((tm, K), lambda i, j: (i, 0)), pl.BlockSpec((GU_TILE, K), lambda i, j: (j, 0)), pair],
        out_specs=pair, out_shape=jax.ShapeDtypeStruct((T, 2 * FFN_HIDDEN), BF16),
        compiler_params=_cp(("parallel", "parallel")),
    )(dx, w_down, ab)


def _adamw(w, g, m, v, name):
    R, C = w.shape
    tr = R
    for cand in (256, 352, 128, 64, 8):
        if R > cand and R % cand == 0:
            tr = cand
            break

    def body(w_ref, g_ref, m_ref, v_ref, d_ref, nm_ref, nv_ref):
        gv = g_ref[...]
        nm = ADAM_B1 * m_ref[...] + (1.0 - ADAM_B1) * gv
        nv = ADAM_B2 * v_ref[...] + (1.0 - ADAM_B2) * (gv * gv)
        m_hat = nm / (1.0 - ADAM_B1 ** ADAM_STEP)
        v_hat = nv / (1.0 - ADAM_B2 ** ADAM_STEP)
        d_ref[...] = -ADAM_LR * (m_hat / (jnp.sqrt(v_hat) + ADAM_EPS) + ADAM_WD * w_ref[...])
        nm_ref[...] = nm
        nv_ref[...] = nv

    blk = pl.BlockSpec((tr, C), lambda i: (i, 0))
    sh = jax.ShapeDtypeStruct((R, C), F32)
    return pl.pallas_call(
        body, name=name, grid=(R // tr,), in_specs=[blk] * 4, out_specs=[blk] * 3, out_shape=[sh] * 3,
        compiler_params=_cp(("parallel",)),
    )(w, g, m, v)


def _sum_slabs(x, name):
    n, R, C = x.shape
    tr = R
    for cand in (512, 256, 128, 64, 32, 16, 8):
        if R > cand and R % cand == 0:
            tr = cand
            break

    def body(x_ref, o_ref):
        acc = x_ref[0].astype(F32)
        for j in range(1, n):
            acc = acc + x_ref[j].astype(F32)
        o_ref[...] = acc

    return pl.pallas_call(
        body, name=name, grid=(R // tr,), in_specs=[pl.BlockSpec((n, tr, C), lambda i: (0, i, 0))],
        out_specs=pl.BlockSpec((tr, C), lambda i: (i, 0)), out_shape=jax.ShapeDtypeStruct((R, C), F32),
        compiler_params=_cp(("parallel",)),
    )(x)


def _multi_gather(xs, layers, name):
    nt = len(xs)
    shapes = [x.shape if lay is None else x.shape[1:] for x, lay in zip(xs, layers)]

    def body(*refs):
        x_refs, out_refs = refs[:nt], refs[nt:2 * nt]
        send_sems, recv_sems, local_sems = refs[2 * nt:]
        x_, y_, c_ = lax.axis_index("x"), lax.axis_index("y"), lax.axis_index("c")
        me, sibling = (x_, y_, c_), (x_, y_, 1 - c_)
        chips = [(1 - x_, y_), (x_, 1 - y_), (1 - x_, 1 - y_)]

        def own_block(t):
            return x_refs[t] if layers[t] is None else x_refs[t].at[layers[t]]

        def copy(t, k, block, to, own=False):
            px, py, pc = block
            dst = out_refs[t].at[4 * px + 2 * py + pc]
            return pltpu.make_async_remote_copy(
                src_ref=own_block(t) if own else dst, dst_ref=dst,
                send_sem=send_sems.at[t, k], recv_sem=recv_sems.at[t, k],
                device_id=to, device_id_type=pl.DeviceIdType.MESH)

        mine, first, passed = [], [], []
        for t in range(nt):
            mine.append(pltpu.make_async_copy(own_block(t), out_refs[t].at[4 * x_ + 2 * y_ + c_], local_sems.at[t]))
            mine[-1].start()
            first.append([copy(t, 1 + j, me, (*chip, c_), own=True) for j, chip in enumerate(chips)]
                         + [copy(t, 0, me, sibling, own=True)])
            for cp in first[-1]:
                cp.start()
        for t in range(nt):
            for j, chip in enumerate(chips):
                copy(t, 1 + j, (*chip, c_), me).wait_recv()
                passed.append(copy(t, 4 + j, (*chip, c_), sibling))
                passed[-1].start()
        for t in range(nt):
            copy(t, 0, sibling, me).wait_recv()
            for j, chip in enumerate(chips):
                copy(t, 4 + j, (*chip, 1 - c_), me).wait_recv()
        for cp in [c for f in first for c in f] + passed:
            cp.wait_send()
        for cp in mine:
            cp.wait()

    hbm = pl.BlockSpec(memory_space=pl.ANY)
    return pl.pallas_call(
        body, name=name, out_shape=[jax.ShapeDtypeStruct((N_DEV,) + tuple(s), x.dtype) for s, x in zip(shapes, xs)],
        in_specs=[hbm] * nt, out_specs=[hbm] * nt,
        scratch_shapes=[pltpu.SemaphoreType.DMA((nt, 7)), pltpu.SemaphoreType.DMA((nt, 7)),
                        pltpu.SemaphoreType.DMA((nt,))],
    )(*xs)


def _multi_exchange(sends, name):
    nt = len(sends)

    def body(*refs):
        s_refs, r_refs = refs[:nt], refs[nt:2 * nt]
        send_sems, recv_sems, local_sems = refs[2 * nt:]
        x_, y_, c_ = lax.axis_index("x"), lax.axis_index("y"), lax.axis_index("c")
        me = 4 * x_ + 2 * y_ + c_
        mine, out, inc = [], [], []
        for t in range(nt):
            mine.append(pltpu.make_async_copy(s_refs[t].at[me], r_refs[t].at[me], local_sems.at[t]))
            mine[-1].start()
        for k in (2, 4, 6, 3, 5, 7, 1):
            px, py, pc = x_ ^ ((k >> 2) & 1), y_ ^ ((k >> 1) & 1), c_ ^ (k & 1)
            peer = 4 * px + 2 * py + pc
            for t in range(nt):
                def copy(src, dst):
                    return pltpu.make_async_remote_copy(
                        src_ref=s_refs[t].at[src], dst_ref=r_refs[t].at[dst],
                        send_sem=send_sems.at[t, k - 1], recv_sem=recv_sems.at[t, k - 1],
                        device_id=(px, py, pc), device_id_type=pl.DeviceIdType.MESH)

                out.append(copy(peer, me))
                inc.append(copy(me, peer))
        for cp in out:
            cp.start()
        for cp in inc:
            cp.wait_recv()
        for cp in out:
            cp.wait_send()
        for cp in mine:
            cp.wait()

    hbm = pl.BlockSpec(memory_space=pl.ANY)
    return pl.pallas_call(
        body, name=name, out_shape=[jax.ShapeDtypeStruct(s.shape, s.dtype) for s in sends],
        in_specs=[hbm] * nt, out_specs=[hbm] * nt,
        scratch_shapes=[pltpu.SemaphoreType.DMA((nt, N_DEV - 1)), pltpu.SemaphoreType.DMA((nt, N_DEV - 1)),
                        pltpu.SemaphoreType.DMA((nt,))],
    )(*sends)


_HBM = pl.BlockSpec(memory_space=pltpu.HBM)
_SEM = pl.BlockSpec(memory_space=pltpu.SEMAPHORE)
_PEER_ORDER = (2, 4, 6, 3, 5, 7, 1)


def _split_copies(src_refs, land_refs, send_sems, recv_sems, layers, per_peer):
    x_, y_, c_ = lax.axis_index("x"), lax.axis_index("y"), lax.axis_index("c")
    me = 4 * x_ + 2 * y_ + c_
    copies = []
    for k in _PEER_ORDER:
        px, py, pc = x_ ^ ((k >> 2) & 1), y_ ^ ((k >> 1) & 1), c_ ^ (k & 1)
        peer = 4 * px + 2 * py + pc
        for t in range(len(src_refs)):
            if per_peer:
                src = src_refs[t].at[peer]
            else:
                src = src_refs[t] if layers[t] is None else src_refs[t].at[layers[t]]
            copies.append(pltpu.make_async_remote_copy(
                src_ref=src, dst_ref=land_refs[t].at[me],
                send_sem=send_sems.at[t * (N_DEV - 1) + k - 1], recv_sem=recv_sems.at[t * (N_DEV - 1) + k - 1],
                device_id=(px, py, pc), device_id_type=pl.DeviceIdType.MESH))
    return copies


def _own_copies(src_refs, land_refs, sems, layers, per_peer):
    nt = len(src_refs)
    me = 4 * lax.axis_index("x") + 2 * lax.axis_index("y") + lax.axis_index("c")
    copies = []
    for t in range(nt):
        if per_peer:
            src = src_refs[t].at[me]
        else:
            src = src_refs[t] if layers[t] is None else src_refs[t].at[layers[t]]
        copies.append(pltpu.make_async_copy(src, land_refs[t].at[me], sems.at[nt * (N_DEV - 1) + t]))
    return copies


def _split_start(srcs, layers, per_peer, after, name):
    nt = len(srcs)
    if per_peer:
        land_shapes = [s.shape for s in srcs]
    else:
        land_shapes = [(N_DEV,) + tuple(s.shape if lay is None else s.shape[1:]) for s, lay in zip(srcs, layers)]

    def body(*refs):
        src_refs, land_refs = refs[:nt], refs[nt:2 * nt]
        send_sems, recv_sems = refs[2 * nt + 1], refs[2 * nt + 2]
        token = refs[-1]
        for cp in _split_copies(src_refs, land_refs, send_sems, recv_sems, layers, per_peer):
            cp.start()
        for cp in _own_copies(src_refs, land_refs, send_sems, layers, per_peer):
            cp.start()
        token[...] = jnp.zeros_like(token)

    lands = [pltpu.with_memory_space_constraint(lax.empty(s, x.dtype), pltpu.HBM) for s, x in zip(land_shapes, srcs)]
    srcs = [pltpu.with_memory_space_constraint(x, pltpu.HBM) for x in srcs]
    out = pl.pallas_call(
        body, name=name,
        out_shape=(pltpu.SemaphoreType.DMA((nt * N_DEV,)), pltpu.SemaphoreType.DMA((nt * (N_DEV - 1),)),
                   *[pltpu.HBM(x.shape, x.dtype) for x in srcs], *[pltpu.HBM(s, x.dtype) for s, x in zip(land_shapes, srcs)],
                   jax.ShapeDtypeStruct((8, LANES), F32)),
        in_specs=[_HBM] * (2 * nt) + [pl.BlockSpec(memory_space=pl.ANY)],
        out_specs=(_SEM, _SEM, *([_HBM] * (2 * nt)), pl.BlockSpec(memory_space=pltpu.VMEM)),
        input_output_aliases={i: 2 + i for i in range(2 * nt)},
        compiler_params=pltpu.CompilerParams(has_side_effects=pltpu.SideEffectType.DATAFLOW_SIDE_EFFECTING),
    )(*srcs, *lands, after)
    return out[0], out[1], list(out[2:2 + nt]), list(out[2 + nt:2 + 2 * nt]), out[-1]


def _split_wait(started, layers, per_peer, after, name):
    send_sems, recv_sems, srcs, lands, _ = started
    nt = len(srcs)

    def body(*refs):
        src_refs, land_refs = refs[:nt], refs[nt:2 * nt]
        s_sems, r_sems = refs[2 * nt], refs[2 * nt + 1]
        for cp in _split_copies(src_refs, land_refs, s_sems, r_sems, layers, per_peer):
            cp.wait_send()
            cp.wait_recv()
        for cp in _own_copies(src_refs, land_refs, s_sems, layers, per_peer):
            cp.wait()

    out = pl.pallas_call(
        body, name=name,
        out_shape=tuple(pltpu.HBM(x.shape, x.dtype) for x in srcs + lands),
        in_specs=[_HBM] * (2 * nt) + [_SEM, _SEM, pl.BlockSpec(memory_space=pl.ANY)],
        out_specs=tuple([_HBM] * (2 * nt)),
        input_output_aliases={i: i for i in range(2 * nt)},
        compiler_params=pltpu.CompilerParams(has_side_effects=pltpu.SideEffectType.DATAFLOW_SIDE_EFFECTING),
    )(*srcs, *lands, send_sems, recv_sems, after)
    return list(out[nt:])


def _with_own(land, own):
    me = 4 * lax.axis_index("x") + 2 * lax.axis_index("y") + lax.axis_index("c")
    return lax.dynamic_update_slice_in_dim(land, own[None], me, axis=0)


def _runs(mapping):
    runs, c, n = [], 0, len(mapping)
    while c < n:
        if mapping[c] is None:
            c += 1
            continue
        sid, d, lo = mapping[c][0], mapping[c][1] - c, c
        while c < n and mapping[c] is not None and mapping[c][0] == sid and mapping[c][1] - c == d:
            c += 1
        runs.append((lo, c, sid, d))
    return runs


def _tile_plan(mapping, src_widths):
    runs = _runs(mapping)
    plan = []
    for t in range(len(mapping) // LANES):
        pieces = []
        for lo, hi, sid, d in runs:
            lo_t, hi_t = max(lo, t * LANES), min(hi, (t + 1) * LANES)
            if lo_t >= hi_t:
                continue
            a = ((lo_t + d) // LANES) * LANES
            win = min(2 * LANES, src_widths[sid] - a)
            shift = t * LANES + d - a
            pieces.append((sid, a, win, shift, lo_t - t * LANES, hi_t - t * LANES))
        plan.append(pieces)
    return plan


def _reblock(srcs, src_views, outs, out_views, name):
    R = srcs[0].shape[-2]
    tr = min(256, R)
    widths = {sid: srcs[ai].shape[-1] for sid, (ai, _) in src_views.items()}
    plans = [(ai, li, _tile_plan(mapping, widths)) for ai, li, mapping in out_views]
    ns = len(srcs)

    def body(*refs):
        s_refs, o_refs = refs[:ns], refs[ns:]
        cache = {}

        def shift_matrix(win, shift, lo, hi):
            key = (win, shift, lo, hi)
            if key not in cache:
                r = lax.broadcasted_iota(jnp.int32, (win, LANES), 0)
                c = lax.broadcasted_iota(jnp.int32, (win, LANES), 1)
                hit = jnp.logical_and(r - c == shift, jnp.logical_and(c >= lo, c < hi))
                cache[key] = jnp.where(hit, 1.0, 0.0).astype(BF16)
            return cache[key]

        for ai, li, plan in plans:
            for t, pieces in enumerate(plan):
                acc = None
                for sid, a, win, shift, lo, hi in pieces:
                    sa, sl = src_views[sid]
                    src = s_refs[sa][:, a:a + win] if sl is None else s_refs[sa][sl, :, a:a + win]
                    part = jnp.dot(src, shift_matrix(win, shift, lo, hi), preferred_element_type=F32)
                    acc = part if acc is None else acc + part
                val = jnp.zeros((tr, LANES), BF16) if acc is None else acc.astype(BF16)
                if li is None:
                    o_refs[ai][:, t * LANES:(t + 1) * LANES] = val
                else:
                    o_refs[ai][li, :, t * LANES:(t + 1) * LANES] = val

    def spec(shape):
        if len(shape) == 2:
            return pl.BlockSpec((tr, shape[1]), lambda i: (i, 0))
        return pl.BlockSpec((shape[0], tr, shape[2]), lambda i: (0, i, 0))

    return pl.pallas_call(
        body, name=name, grid=(R // tr,), in_specs=[spec(s.shape) for s in srcs],
        out_specs=[spec(s) for s in outs], out_shape=[jax.ShapeDtypeStruct(s, BF16) for s in outs],
        compiler_params=_cp(("parallel",)),
    )(*srcs)


SHARDED = ("w_in", "w_gate_up", "w_proj_attn", "w_proj_pool", "w_proj_conv", "w_out", "w_down")
WEIGHT_ORDER = ("attn_norm", "w_in", "b_forget", "b_gate", "w_proj_attn", "pool_w", "pool_scale", "w_proj_pool",
                "conv_w", "w_proj_conv", "w_out", "ffn_norm", "w_gate_up", "w_down", "final_norm")
IN_SHARD, IN_SHARD_PAD = IN_COLS // N_DEV, 896
GU_SHARD, GU_SHARD_PAD = 2 * FFN_HIDDEN // N_DEV, 768


def _w_in_col(c):
    if c < GATE_W:
        return c + 3592
    if c < OFF_U:
        return c - OFF_Q
    return c - OFF_U + 1544


def _w_in_full(gathered, name):
    main = [divmod(_w_in_col(c), IN_SHARD) for c in range(MAIN_COLS)]
    fcols = [divmod(1536 + c, IN_SHARD) if c < N_HEADS else None for c in range(LANES)]
    R = gathered.shape[1]
    return _reblock([gathered], {i: (0, i) for i in range(N_DEV)}, [(R, MAIN_COLS), (R, LANES)],
                    [(0, None, main), (1, None, fcols)], name)


def _w_in_slabs(dmain, dwf, name):
    inv = {_w_in_col(c): ("m", c) for c in range(MAIN_COLS)}
    inv.update({1536 + c: ("f", c) for c in range(N_HEADS)})
    views = []
    for i in range(N_DEV):
        mapping = [inv[IN_SHARD * i + j] if j < IN_SHARD else None for j in range(IN_SHARD_PAD)]
        views.append((0, i, mapping))
    R = dmain.shape[0]
    return _reblock([dmain, dwf], {"m": (0, None), "f": (1, None)}, [(N_DEV, R, IN_SHARD_PAD)], views, name)[0]


def _w_gu_full(gathered, name):
    mapping = [divmod(_gu_col(c), GU_SHARD) for c in range(2 * FFN_HIDDEN)]
    R = gathered.shape[1]
    return _reblock([gathered], {i: (0, i) for i in range(N_DEV)}, [(R, 2 * FFN_HIDDEN)], [(0, None, mapping)], name)[0]


def _w_gu_slabs(dw, name):
    inv = {_gu_col(c): c for c in range(2 * FFN_HIDDEN)}
    views = [(0, i, [("w", inv[GU_SHARD * i + j]) if j < GU_SHARD else None for j in range(GU_SHARD_PAD)])
             for i in range(N_DEV)]
    R = dw.shape[0]
    return _reblock([dw], {"w": (0, None)}, [(N_DEV, R, GU_SHARD_PAD)], views, name)[0]


def _layer_fwd(x, W, n_seq, l):
    T = x.shape[0]
    sfx = f"_l{l}"
    h1 = _rms_fwd(x, W["attn_norm"], "rms1" + sfx)
    proj = _matmul(h1, W["w_main"], mode="nn", out_dtype=BF16, name="proj_main" + sfx)
    f = _matmul(h1, W["w_f"], mode="nn", out_dtype=F32, name="proj_f" + sfx)
    qa, ka, va = _fox_prep(f, W["b_forget"], proj, n_seq, "fox_prep" + sfx)
    oa, oa32, lse = _attn_fwd2(qa, ka, va, n_seq, "attn_fwd" + sfx)
    if "late" in W:
        W.update(W.pop("late")(oa))
    ob = _pool_fwd(proj, W["pool_w"], W["pool_scale"], n_seq, "pool_fwd" + sfx)
    oc = _conv_fwd(proj, W["conv_w"], n_seq, "conv_fwd" + sfx)
    mixed = _mix_fwd(oa, ob, oc, W["w_proj_attn"], W["w_proj_pool"], W["w_proj_conv"], proj, W["b_gate"],
                     "mix_fwd" + sfx)
    x2 = _matmul(mixed, W["w_out"], mode="nn", out_dtype=F32, name="out_proj" + sfx, residual=x)
    h2 = _rms_fwd(x2, W["ffn_norm"], "rms2" + sfx)
    ab, s = _gate_up_swiglu(h2, W["w_gate_up"], "gate_up" + sfx)
    x3 = _matmul(s, W["w_down"], mode="nn", out_dtype=F32, name="down" + sfx, tm=1024, tn=1024, tk=1408,
                 residual=x2)
    saved = dict(x=x, h1=h1, proj=proj, f=f, qa=qa, ka=ka, oa=oa, oa32=oa32, lse=lse, ob=ob, oc=oc, mixed=mixed, x2=x2,
                 h2=h2, ab=ab, s=s)
    return x3, saved


def _layer_bwd(dx3, dx3b, W, sv, n_seq, l, stage=None):
    T = dx3.shape[0]
    sfx = f"_l{l}"
    G = {}
    stage = stage or (lambda l, group, G, W: W)
    dab = _swiglu_bwd_fused(dx3b, W["w_down"], sv["ab"], "d_ab" + sfx)
    G["w_down"] = _matmul(sv["s"], dx3b, mode="tn", out_dtype=BF16, name="dw_down" + sfx, tm=256, tn=1024)
    dh2 = _matmul(dab, W["w_gate_up"], mode="nt", out_dtype=BF16, name="d_h2" + sfx, tm=1024, tn=1024, tk=1408)
    G["w_gate_up"] = _matmul(sv["h2"], dab, mode="tn", out_dtype=BF16, name="dw_gate_up" + sfx, tm=1024)
    W = stage(l, "ffn", G, W)
    dx2, dx2b, G["ffn_norm"] = _rms_bwd(sv["x2"], W["ffn_norm"], dh2, dx3, "rms2_bwd" + sfx)
    dmixed = _matmul(dx2b, W["w_out"], mode="nt", out_dtype=BF16, name="d_mixed" + sfx)
    G["w_out"] = _matmul(sv["mixed"], dx2b, mode="tn", out_dtype=BF16, name="dw_out" + sfx, tm=1024)
    dya, dyb, dyc, dg, G["b_gate"] = _mix_bwd(sv["oa"], sv["ob"], sv["oc"], W["w_proj_attn"], W["w_proj_pool"],
                                              W["w_proj_conv"], sv["proj"], W["b_gate"], dmixed, "mix_bwd" + sfx)
    douts = {}
    for br, dy, o in (("attn", dya, sv["oa"]), ("pool", dyb, sv["ob"]), ("conv", dyc, sv["oc"])):
        douts[br] = _matmul(dy, W["w_proj_" + br], mode="nt", out_dtype=BF16, name=f"d_{br}_out" + sfx)
        G["w_proj_" + br] = _matmul(o, dy, mode="tn", out_dtype=BF16, name=f"dw_proj_{br}" + sfx, tm=512)
    W = stage(l, "mix", G, W)
    dcv, dcb, dcc, G["conv_w"] = _conv_bwd(sv["proj"], douts["conv"], W["conv_w"], n_seq, "conv_bwd" + sfx)
    du, G["pool_w"], G["pool_scale"] = _pool_bwd(sv["proj"], douts["pool"], W["pool_w"], W["pool_scale"], n_seq,
                                                 "pool_bwd" + sfx)
    delta = _attn_delta(douts["attn"], sv["oa32"], "attn_delta" + sfx)
    dq = _attn_bwd_dq2(sv["qa"], sv["ka"], sv["proj"], douts["attn"], sv["lse"], delta, n_seq, "attn_dq" + sfx)
    dk, dv, dFk = _attn_bwd_dkv2(sv["qa"], sv["ka"], sv["proj"], douts["attn"], sv["lse"], delta, n_seq,
                                 "attn_dkv" + sfx)
    dF = jnp.pad(dFk.reshape(N_HEADS, T).T, ((0, 0), (0, LANES - N_HEADS)))
    df, G["b_forget"] = _fox_cumsum_bwd(sv["f"], W["b_forget"], dF, n_seq, "fox_cumsum_bwd" + sfx)
    dproj = jnp.concatenate([dg, dq, dk, dv, du, dcv, dcb, dcc], axis=1)
    G["w_main"] = _matmul(sv["h1"], dproj, mode="tn", out_dtype=BF16, name="dw_main" + sfx, tm=1024)
    G["w_f"] = _matmul(sv["h1"], df, mode="tn", out_dtype=BF16, name="dw_f" + sfx, tm=1024)
    W = stage(l, "w_in", G, W)
    dh1 = _matmul(df, W["w_f"], mode="nt", out_dtype=F32, name="d_h1_f" + sfx)
    dh1 = _matmul(dproj, W["w_main"], mode="nt", out_dtype=F32, name="d_h1_main" + sfx, tm=1024, tn=1024, tk=1664,
                  residual=dh1)
    dx, dxb, G["attn_norm"] = _rms_bwd(sv["x"], W["attn_norm"], dh1, dx2, "rms1_bwd" + sfx)
    return dx, dxb, G


def _replicated_operands(rep, l):
    W = {}
    W["attn_norm"], W["ffn_norm"] = rep["attn_norm"][l], rep["ffn_norm"][l]
    W["b_forget"] = jnp.pad(rep["b_forget"][l].reshape(1, N_HEADS), ((0, 0), (0, LANES - N_HEADS)))
    W["b_gate"] = rep["b_gate"][l].reshape(1, GATE_W)
    W["pool_w"] = rep["pool_w"][l].astype(BF16)
    W["pool_scale"] = rep["pool_scale"][l].reshape(1, BRANCH_W)
    return W


def _local_step(x, target, get_W, final_norm, stage=None):
    n_seq, S, Dm = x.shape
    T = n_seq * S
    xt = x.reshape(T, Dm)
    saved, Ws = [], []
    for l in range(DEPTH):
        Ws.append(get_W(l, xt))
        xt, sv = _layer_fwd(xt, Ws[l], n_seq, l)
        saved.append(sv)
    loss, dx, dxb, g_final = _loss_head(xt, final_norm, target.reshape(T, Dm), "loss_head")
    grads = [None] * DEPTH
    for l in reversed(range(DEPTH)):
        dx, dxb, grads[l] = _layer_bwd(dx, dxb, Ws[l], saved[l], n_seq, l, stage)
    return loss, dx.reshape(n_seq, S, Dm), grads, g_final


def _padded_shards(weights):
    sh = {n: weights[n].astype(BF16) for n in SHARDED}
    sh["w_in"] = jnp.pad(sh["w_in"], ((0, 0), (0, 0), (0, IN_SHARD_PAD - IN_SHARD)))
    sh["w_gate_up"] = jnp.pad(sh["w_gate_up"], ((0, 0), (0, 0), (0, GU_SHARD_PAD - GU_SHARD)))
    return sh


def _full_operands(g, l):
    W = {}
    if "w_in" in g:
        W["w_main"], W["w_f"] = _w_in_full(g["w_in"], f"w_in_full_l{l}")
    if "w_gate_up" in g:
        W["w_gate_up"] = _w_gu_full(g["w_gate_up"], f"w_gate_up_full_l{l}")
    for n in ("w_proj_attn", "w_proj_pool", "w_proj_conv"):
        if n in g:
            W[n] = jnp.transpose(g[n], (1, 0, 2)).reshape(BRANCH_W, D_MODEL)
    if "w_out" in g:
        W["w_out"] = g["w_out"].reshape(D_MODEL, D_MODEL)
    if "w_down" in g:
        W["w_down"] = g["w_down"].reshape(FFN_HIDDEN, D_MODEL)
    return W


GRAD_GROUPS = {"ffn": ("w_down", "w_gate_up"),
               "mix": ("w_out", "w_proj_attn", "w_proj_pool", "w_proj_conv"),
               "w_in": ("w_in",)}


def _grad_slabs(G, n, l):
    if n == "w_in":
        return _w_in_slabs(G["w_main"], G["w_f"], f"w_in_slabs_l{l}")
    if n == "w_gate_up":
        return _w_gu_slabs(G["w_gate_up"], f"w_gate_up_slabs_l{l}")
    if n == "w_out":
        return G["w_out"].reshape(N_DEV, D_MODEL // N_DEV, D_MODEL)
    if n == "w_down":
        return G["w_down"].reshape(N_DEV, FFN_HIDDEN // N_DEV, D_MODEL)
    return jnp.transpose(G[n].reshape(BRANCH_W, N_DEV, D_MODEL // N_DEV), (1, 0, 2))


def _sum_layer_grads(recv, l):
    out = {n: _sum_slabs(r, f"sum_{n}_l{l}") for n, r in recv.items()}
    if "w_in" in out:
        out["w_in"] = out["w_in"][:, :IN_SHARD]
    if "w_gate_up" in out:
        out["w_gate_up"] = out["w_gate_up"][:, :GU_SHARD]
    return out


def _sum_small(xs, name):
    def body(*refs):
        for x_ref, o_ref in zip(refs[:len(xs)], refs[len(xs):]):
            acc = x_ref[0]
            for j in range(1, N_DEV):
                acc = acc + x_ref[j]
            o_ref[...] = acc

    return pl.pallas_call(
        body, name=name, out_shape=[jax.ShapeDtypeStruct(x.shape[1:], F32) for x in xs],
        compiler_params=_cp(),
    )(*xs)


def _as_2d(a):
    if a.ndim == 1:
        return a.reshape(1, -1)
    return a.reshape(-1, a.shape[-1])


def kernel(x, attn_norm, w_in, b_forget, b_gate, w_proj_attn, pool_w, pool_scale, w_proj_pool, conv_w, w_proj_conv, w_out, ffn_norm, w_gate_up, w_down, final_norm, loss_target, m_attn_norm, m_w_in, m_b_forget, m_b_gate, m_w_proj_attn, m_pool_w, m_pool_scale, m_w_proj_pool, m_conv_w, m_w_proj_conv, m_w_out, m_ffn_norm, m_w_gate_up, m_w_down, m_final_norm, v_attn_norm, v_w_in, v_b_forget, v_b_gate, v_w_proj_attn, v_pool_w, v_pool_scale, v_w_proj_pool, v_conv_w, v_w_proj_conv, v_w_out, v_ffn_norm, v_w_gate_up, v_w_down, v_final_norm):
    weights = dict(attn_norm=attn_norm, w_in=w_in, b_forget=b_forget, b_gate=b_gate, w_proj_attn=w_proj_attn,
                   pool_w=pool_w, pool_scale=pool_scale, w_proj_pool=w_proj_pool, conv_w=conv_w,
                   w_proj_conv=w_proj_conv, w_out=w_out, ffn_norm=ffn_norm, w_gate_up=w_gate_up, w_down=w_down,
                   final_norm=final_norm)
    moments_m = dict(attn_norm=m_attn_norm, w_in=m_w_in, b_forget=m_b_forget, b_gate=m_b_gate,
                     w_proj_attn=m_w_proj_attn, pool_w=m_pool_w, pool_scale=m_pool_scale, w_proj_pool=m_w_proj_pool,
                     conv_w=m_conv_w, w_proj_conv=m_w_proj_conv, w_out=m_w_out, ffn_norm=m_ffn_norm,
                     w_gate_up=m_w_gate_up, w_down=m_w_down, final_norm=m_final_norm)
    moments_v = dict(attn_norm=v_attn_norm, w_in=v_w_in, b_forget=v_b_forget, b_gate=v_b_gate,
                     w_proj_attn=v_w_proj_attn, pool_w=v_pool_w, pool_scale=v_pool_scale, w_proj_pool=v_w_proj_pool,
                     conv_w=v_conv_w, w_proj_conv=v_w_proj_conv, w_out=v_w_out, ffn_norm=v_ffn_norm,
                     w_gate_up=v_w_gate_up, w_down=v_w_down, final_norm=v_final_norm)

    sh = _padded_shards(weights)
    names = list(SHARDED)
    rest = [n for n in names if n != "w_in"]
    me = 4 * lax.axis_index("x") + 2 * lax.axis_index("y") + lax.axis_index("c")
    w_in0, conv_all = _multi_gather([sh["w_in"], conv_w], [0, None], "gather_w_in_l0")
    started, after = {}, w_in0
    for l in range(DEPTH):
        for group, gnames in (("w_in", ["w_in"]), ("rest", rest)):
            if (l, group) != (0, "w_in"):
                started[l, group] = _split_start([sh[n] for n in gnames], [l] * len(gnames), False, after,
                                                 f"gather_start_{group}_l{l}")
                after = started[l, group][4]
    last_token = after

    def get_W(l, xt):
        if l == 0:
            w_in = w_in0
        else:
            w_in = _split_wait(started[l, "w_in"], [l], False, xt, f"gather_wait_w_in_l{l}")[0]
        W = _full_operands({"w_in": w_in}, l)

        def late(after):
            lands = _split_wait(started[l, "rest"], [l] * len(rest), False, after, f"gather_wait_rest_l{l}")
            return _full_operands(dict(zip(rest, lands)), l)

        W["late"] = late
        W.update(_replicated_operands(weights, l))
        W["conv_w"] = jnp.transpose(conv_all[:, l], (1, 0, 2)).reshape(CONV_K, BRANCH_W)
        if l == 0:
            W["attn_norm"] = W["attn_norm"] + last_token[0, 0]
        return W

    exchanges = []

    def stage(l, group, G, W):
        gnames = GRAD_GROUPS[group]
        slabs = [_grad_slabs(G, n, l) for n in gnames]
        started = _split_start(slabs, None, True, slabs[0], f"exchange_start_{group}_l{l}")
        exchanges.append((l, group, gnames, slabs, started))
        tie = {"ffn": "ffn_norm", "mix": "conv_w", "w_in": "w_f"}[group]
        W = dict(W)
        W[tie] = W[tie] + started[4][0, 0].astype(W[tie].dtype)
        return W

    loss_part, grad_x, grads, g_final = _local_step(x, loss_target, get_W, final_norm, stage)
    after = grad_x
    for l, group, gnames, slabs, started in exchanges:
        lands = _split_wait(started, None, True, after, f"exchange_wait_{group}_l{l}")
        grads[l].update(_sum_layer_grads(dict(zip(gnames, lands)), l))
    gw = {n: jnp.stack([grads[l][n] for l in range(DEPTH)]) for n in SHARDED}

    small = ("attn_norm", "b_forget", "b_gate", "pool_w", "pool_scale", "ffn_norm", "conv_w")
    parts = [jnp.stack([grads[l][n] for l in range(DEPTH)]) for n in small] + [g_final, loss_part]
    gathered = _multi_gather(parts, [None] * len(parts), "gather_small_grads")
    summed = _sum_small(gathered, "sum_small_grads")
    for n, s in zip(small, summed):
        gw[n] = s
    gw["attn_norm"], gw["ffn_norm"] = gw["attn_norm"][:, 0], gw["ffn_norm"][:, 0]
    gw["b_forget"] = gw["b_forget"][:, 0, :N_HEADS]
    gw["b_gate"], gw["pool_scale"] = gw["b_gate"][:, 0], gw["pool_scale"][:, 0]
    gw["conv_w"] = lax.dynamic_slice_in_dim(gw["conv_w"], me * (BRANCH_W // N_DEV), BRANCH_W // N_DEV, axis=2)
    gw["final_norm"] = summed[-2][0]
    loss = summed[-1][0, 0]

    deltas, new_m, new_v = {}, {}, {}
    for n in WEIGHT_ORDER:
        shape = weights[n].shape
        d, nm, nv = _adamw(_as_2d(weights[n]), _as_2d(gw[n]), _as_2d(moments_m[n]), _as_2d(moments_v[n]),
                           "adamw_" + n)
        deltas[n], new_m[n], new_v[n] = d.reshape(shape), nm.reshape(shape), nv.reshape(shape)

    return (loss, grad_x, *[gw[n] for n in WEIGHT_ORDER], *[deltas[n] for n in WEIGHT_ORDER],
            *[new_m[n] for n in WEIGHT_ORDER], *[new_v[n] for n in WEIGHT_ORDER])
```

```python
import functools

import numpy as np
import jax
import jax.numpy as jnp
from jax import lax
from jax.experimental import pallas as pl
from jax.experimental.pallas import tpu as pltpu

F32 = jnp.float32
BF16 = jnp.bfloat16

N_DEV = 8
D_MODEL = 1024
DEPTH = 2
N_HEADS = 8
HEAD_DIM = 64
BRANCH_W = 512
POOL_WINDOWS = (2, 4, 8, 16)
POOL_GD = 128
CONV_K = 3
FFN_HIDDEN = 2816
GATE_W = 3 * D_MODEL
IN_COLS = 6664
MAIN_COLS = GATE_W + 7 * BRANCH_W
RMS_EPS = 1e-6
NEG_INF = -1e30

ADAM_LR = 0.001
ADAM_B1 = 0.9
ADAM_B2 = 0.999
ADAM_EPS = 1e-08
ADAM_WD = 0.01
ADAM_STEP = 10

LANES = 128
VMEM_LIMIT = 56 * 1024 * 1024
ATT_BLK = 256
CUM_BLK = 256

OFF_G, OFF_Q, OFF_K, OFF_V, OFF_U, OFF_CV, OFF_CB, OFF_CC = (
    0, 3072, 3584, 4096, 4608, 5120, 5632, 6144)


def _cp(sem=None):
    return pltpu.CompilerParams(dimension_semantics=sem, vmem_limit_bytes=VMEM_LIMIT)


def _sigmoid(z):
    return 1.0 / (1.0 + jnp.exp(-z))


def _matmul(a, b, *, mode, out_dtype, name, tm=2048, tn=512, tk=None, residual=None):
    if mode == "nn":
        (M, K), N = a.shape, b.shape[1]
    elif mode == "nt":
        (M, K), N = a.shape, b.shape[0]
    else:
        (K, M), N = a.shape, b.shape[1]
    tm, tn, tk = min(tm, M), min(tn, N), K if tk is None else min(tk, K)
    assert M % tm == 0 and N % tn == 0 and K % tk == 0, (name, M, N, K, tm, tn, tk)
    nk = K // tk
    if mode == "nn":
        a_spec = pl.BlockSpec((tm, tk), lambda i, j, k: (i, k))
        b_spec = pl.BlockSpec((tk, tn), lambda i, j, k: (k, j))
        dims = (((1,), (0,)), ((), ()))
    elif mode == "nt":
        a_spec = pl.BlockSpec((tm, tk), lambda i, j, k: (i, k))
        b_spec = pl.BlockSpec((tn, tk), lambda i, j, k: (j, k))
        dims = (((1,), (1,)), ((), ()))
    else:
        a_spec = pl.BlockSpec((tk, tm), lambda i, j, k: (k, i))
        b_spec = pl.BlockSpec((tk, tn), lambda i, j, k: (k, j))
        dims = (((0,), (0,)), ((), ()))
    o_spec = pl.BlockSpec((tm, tn), lambda i, j, k: (i, j))
    has_res = residual is not None

    def body(*refs):
        a_ref, b_ref = refs[:2]
        r_ref = refs[2] if has_res else None
        o_ref = refs[2 + has_res]

        def finish(acc):
            if has_res:
                acc = acc + r_ref[...].astype(F32)
            o_ref[...] = acc.astype(out_dtype)

        prod = lax.dot_general(a_ref[...], b_ref[...], dims, preferred_element_type=F32)
        if nk == 1:
            finish(prod)
            return
        acc_ref = refs[-1]
        k = pl.program_id(2)

        @pl.when(k == 0)
        def _():
            acc_ref[...] = prod

        @pl.when(jnp.logical_and(k > 0, k < nk - 1))
        def _():
            acc_ref[...] += prod

        @pl.when(k == nk - 1)
        def _():
            finish(acc_ref[...] + prod)

    in_specs = [a_spec, b_spec] + ([o_spec] if has_res else [])
    args = (a, b) + ((residual,) if has_res else ())
    return pl.pallas_call(
        body, name=name, grid=(M // tm, N // tn, nk), in_specs=in_specs, out_specs=o_spec,
        out_shape=jax.ShapeDtypeStruct((M, N), out_dtype),
        scratch_shapes=[pltpu.VMEM((tm, tn), F32)] if nk > 1 else [],
        compiler_params=_cp(("parallel", "parallel", "arbitrary")),
    )(*args)


def _rms_fwd(x, g, name):
    T, Dm = x.shape
    tm = min(512, T)

    def body(x_ref, g_ref, h_ref):
        xf = x_ref[...]
        r = lax.rsqrt(jnp.mean(xf * xf, axis=-1, keepdims=True) + RMS_EPS)
        h_ref[...] = ((xf * r) * g_ref[...]).astype(BF16)

    return pl.pallas_call(
        body, name=name, grid=(T // tm,),
        in_specs=[pl.BlockSpec((tm, Dm), lambda i: (i, 0)), pl.BlockSpec((1, Dm), lambda i: (0, 0))],
        out_specs=pl.BlockSpec((tm, Dm), lambda i: (i, 0)),
        out_shape=jax.ShapeDtypeStruct((T, Dm), BF16),
        compiler_params=_cp(("parallel",)),
    )(x, g.reshape(1, Dm))


def _rms_bwd(x, g, dh, dres, name):
    T, Dm = x.shape
    tm = min(512, T)

    def body(x_ref, g_ref, dh_ref, dres_ref, dx_ref, dxb_ref, dg_ref):
        i = pl.program_id(0)
        xf = x_ref[...]
        r = lax.rsqrt(jnp.mean(xf * xf, axis=-1, keepdims=True) + RMS_EPS)
        xn = xf * r
        dhf = dh_ref[...].astype(F32)
        dxn = dhf * g_ref[...]
        c = jnp.mean(dxn * xn, axis=-1, keepdims=True)
        dx = dres_ref[...] + r * (dxn - xn * c)
        dx_ref[...] = dx
        dxb_ref[...] = dx.astype(BF16)
        part = jnp.sum(dhf * xn, axis=0, keepdims=True)

        @pl.when(i == 0)
        def _():
            dg_ref[...] = part

        @pl.when(i > 0)
        def _():
            dg_ref[...] += part

    row = pl.BlockSpec((tm, Dm), lambda i: (i, 0))
    vec = pl.BlockSpec((1, Dm), lambda i: (0, 0))
    return pl.pallas_call(
        body, name=name, grid=(T // tm,), in_specs=[row, vec, row, row], out_specs=[row, row, vec],
        out_shape=[jax.ShapeDtypeStruct((T, Dm), F32), jax.ShapeDtypeStruct((T, Dm), BF16),
                   jax.ShapeDtypeStruct((1, Dm), F32)],
        compiler_params=_cp(("arbitrary",)),
    )(x, g.reshape(1, Dm), dh, dres)


def _loss_head(x, g, target, name):
    T, Dm = x.shape
    tm = min(512, T)

    def body(x_ref, g_ref, t_ref, loss_ref, dx_ref, dxb_ref, dg_ref):
        i = pl.program_id(0)
        xf = x_ref[...]
        gv = g_ref[...]
        r = lax.rsqrt(jnp.mean(xf * xf, axis=-1, keepdims=True) + RMS_EPS)
        xn = xf * r
        diff = xn * gv - t_ref[...]
        per_tok = jnp.mean(diff * diff, axis=-1, keepdims=True)
        lpart = 0.5 * jnp.sum(per_tok, axis=0, keepdims=True) + jnp.zeros((1, LANES), F32)
        dy = diff * (1.0 / Dm)
        dxn = dy * gv
        c = jnp.mean(dxn * xn, axis=-1, keepdims=True)
        dx = r * (dxn - xn * c)
        dx_ref[...] = dx
        dxb_ref[...] = dx.astype(BF16)
        part = jnp.sum(dy * xn, axis=0, keepdims=True)

        @pl.when(i == 0)
        def _():
            dg_ref[...] = part
            loss_ref[...] = lpart

        @pl.when(i > 0)
        def _():
            dg_ref[...] += part
            loss_ref[...] += lpart

    row = pl.BlockSpec((tm, Dm), lambda i: (i, 0))
    vec = pl.BlockSpec((1, Dm), lambda i: (0, 0))
    lsp = pl.BlockSpec((1, LANES), lambda i: (0, 0))
    return pl.pallas_call(
        body, name=name, grid=(T // tm,), in_specs=[row, vec, row], out_specs=[lsp, row, row, vec],
        out_shape=[jax.ShapeDtypeStruct((1, LANES), F32), jax.ShapeDtypeStruct((T, Dm), F32),
                   jax.ShapeDtypeStruct((T, Dm), BF16), jax.ShapeDtypeStruct((1, Dm), F32)],
        compiler_params=_cp(("arbitrary",)),
    )(x, g.reshape(1, Dm), target)


def _split_bf16(v):
    hi = v.astype(BF16)
    r1 = v - hi.astype(F32)
    mid = r1.astype(BF16)
    lo = (r1 - mid.astype(F32)).astype(BF16)
    return hi, mid, lo


def _tri_dot(tri, v):
    hi, mid, lo = _split_bf16(v)
    dot = functools.partial(jnp.dot, preferred_element_type=F32)
    return dot(tri, hi) + dot(tri, mid) + dot(tri, lo)


def _log_sigmoid(z):
    return jnp.minimum(z, 0.0) - jnp.log(1.0 + jnp.exp(-jnp.abs(z)))


def _fox_cumsum_fwd(f, bf, n_seq, name):
    T = f.shape[0]
    S = T // n_seq
    c = min(CUM_BLK, S)

    def body(f_ref, b_ref, out_ref):
        ri = lax.broadcasted_iota(jnp.int32, (c, c), 0)
        ci = lax.broadcasted_iota(jnp.int32, (c, c), 1)
        tri = (ri >= ci).astype(BF16)
        carry = jnp.zeros((1, LANES), F32)
        for j in range(S // c):
            lf = _log_sigmoid(f_ref[j * c:(j + 1) * c, :] + b_ref[...])
            out_ref[j * c:(j + 1) * c, :] = _tri_dot(tri, lf) + carry
            carry = carry + jnp.sum(lf, axis=0, keepdims=True)

    blk = pl.BlockSpec((S, LANES), lambda b: (b, 0))
    return pl.pallas_call(
        body, name=name, grid=(n_seq,), in_specs=[blk, pl.BlockSpec((1, LANES), lambda b: (0, 0))],
        out_specs=blk, out_shape=jax.ShapeDtypeStruct((T, LANES), F32),
        compiler_params=_cp(("parallel",)),
    )(f, bf)


def _fox_cumsum_bwd(f, bf, dF, n_seq, name):
    T = f.shape[0]
    S = T // n_seq
    c = min(CUM_BLK, S)

    def body(f_ref, b_ref, dF_ref, df_ref, db_ref):
        b = pl.program_id(0)
        ri = lax.broadcasted_iota(jnp.int32, (c, c), 0)
        ci = lax.broadcasted_iota(jnp.int32, (c, c), 1)
        tri = (ri <= ci).astype(BF16)
        carry = jnp.zeros((1, LANES), F32)
        dbp = jnp.zeros((1, LANES), F32)
        for j in reversed(range(S // c)):
            dFc = dF_ref[j * c:(j + 1) * c, :]
            dlf = _tri_dot(tri, dFc) + carry
            carry = carry + jnp.sum(dFc, axis=0, keepdims=True)
            z = f_ref[j * c:(j + 1) * c, :] + b_ref[...]
            dz = dlf * _sigmoid(-z)
            df_ref[j * c:(j + 1) * c, :] = dz.astype(BF16)
            dbp = dbp + jnp.sum(dz, axis=0, keepdims=True)

        @pl.when(b == 0)
        def _():
            db_ref[...] = dbp

        @pl.when(b > 0)
        def _():
            db_ref[...] += dbp

    blk = pl.BlockSpec((S, LANES), lambda b: (b, 0))
    vec = pl.BlockSpec((1, LANES), lambda b: (0, 0))
    return pl.pallas_call(
        body, name=name, grid=(n_seq,), in_specs=[blk, vec, blk], out_specs=[blk, vec],
        out_shape=[jax.ShapeDtypeStruct((T, LANES), BF16), jax.ShapeDtypeStruct((1, LANES), F32)],
        compiler_params=_cp(("arbitrary",)),
    )(f, bf, dF)


def _pair_masks():
    lane = lax.broadcasted_iota(jnp.int32, (1, LANES), 1)
    lo = lane < HEAD_DIM
    return lo, jnp.logical_not(lo)


def _attn_logits(q, k, fq, fk, sel, mask, scale):
    qm = jnp.where(sel, q, jnp.zeros_like(q))
    s = lax.dot_general(qm, k, (((1,), (1,)), ((), ())), preferred_element_type=F32) * scale
    s = s + fq - fk
    return jnp.where(mask, s, NEG_INF)


def _causal_mask(qi, ki, blk):
    row = qi * blk + lax.broadcasted_iota(jnp.int32, (blk, blk), 0)
    col = ki * blk + lax.broadcasted_iota(jnp.int32, (blk, blk), 1)
    return col <= row


def _attn_fwd(proj, Fq, Fk, n_seq, name):
    T = proj.shape[0]
    S = T // n_seq
    blk = min(ATT_BLK, S)
    nb = S // blk
    scale = HEAD_DIM ** -0.5
    qc, kc, vc = OFF_Q // LANES, OFF_K // LANES, OFF_V // LANES

    def body(q_ref, k_ref, v_ref, fq_ref, fk_ref, o_ref, o32_ref, lse_ref, m_s, l_s, acc_s):
        qi, ki = pl.program_id(2), pl.program_id(3)

        @pl.when(ki == 0)
        def _():
            m_s[...] = jnp.full_like(m_s, NEG_INF)
            l_s[...] = jnp.zeros_like(l_s)
            acc_s[...] = jnp.zeros_like(acc_s)

        @pl.when(ki <= qi)
        def _():
            q, k, v = q_ref[...], k_ref[...], v_ref[...]
            mask = _causal_mask(qi, ki, blk)
            for hh, sel in enumerate(_pair_masks()):
                s = _attn_logits(q, k, fq_ref[hh], fk_ref[hh], sel, mask, scale)
                m_prev = m_s[hh]
                m_new = jnp.maximum(m_prev, jnp.max(s, axis=-1, keepdims=True))
                alpha = jnp.exp(m_prev - m_new)
                p = jnp.exp(s - m_new)
                l_s[hh] = alpha * l_s[hh] + jnp.sum(p, axis=-1, keepdims=True)
                p_hi = p.astype(BF16)
                p_lo = (p - p_hi.astype(F32)).astype(BF16)
                pv = jnp.dot(p_hi, v, preferred_element_type=F32) + jnp.dot(p_lo, v, preferred_element_type=F32)
                acc_s[hh] = alpha * acc_s[hh] + pv
                m_s[hh] = m_new

        @pl.when(ki == qi)
        def _():
            lo, _ = _pair_masks()
            o = jnp.where(lo, acc_s[0] / l_s[0], acc_s[1] / l_s[1])
            o_ref[...] = o.astype(BF16)
            o32_ref[...] = o
            lse_ref[0] = m_s[0] + jnp.log(l_s[0])
            lse_ref[1] = m_s[1] + jnp.log(l_s[1])

    grid = (n_seq, N_HEADS // 2, nb, nb)
    return pl.pallas_call(
        body, name=name, grid=grid,
        in_specs=[
            pl.BlockSpec((blk, LANES), lambda b, j, qi, ki: (b * nb + qi, qc + j)),
            pl.BlockSpec((blk, LANES), lambda b, j, qi, ki: (b * nb + jnp.minimum(ki, qi), kc + j)),
            pl.BlockSpec((blk, LANES), lambda b, j, qi, ki: (b * nb + jnp.minimum(ki, qi), vc + j)),
            pl.BlockSpec((2, blk, 1), lambda b, j, qi, ki: (j, b * nb + qi, 0)),
            pl.BlockSpec((2, 1, blk), lambda b, j, qi, ki: (j, 0, b * nb + jnp.minimum(ki, qi))),
        ],
        out_specs=[
            pl.BlockSpec((blk, LANES), lambda b, j, qi, ki: (b * nb + qi, j)),
            pl.BlockSpec((blk, LANES), lambda b, j, qi, ki: (b * nb + qi, j)),
            pl.BlockSpec((2, blk, 1), lambda b, j, qi, ki: (j, b * nb + qi, 0)),
        ],
        out_shape=[jax.ShapeDtypeStruct((T, BRANCH_W), BF16), jax.ShapeDtypeStruct((T, BRANCH_W), F32),
                   jax.ShapeDtypeStruct((N_HEADS, T, 1), F32)],
        scratch_shapes=[pltpu.VMEM((2, blk, 1), F32), pltpu.VMEM((2, blk, 1), F32),
                        pltpu.VMEM((2, blk, LANES), F32)],
        compiler_params=_cp(("parallel", "parallel", "parallel", "arbitrary")),
    )(proj, proj, proj, Fq, Fk)


def _attn_delta(do, o, name):
    T = do.shape[0]
    tm = min(512, T)

    def body(do_ref, o_ref, d_ref):
        prod = do_ref[...].astype(F32) * o_ref[...].astype(F32)
        lo, hi = _pair_masks()
        for j in range(N_HEADS // 2):
            pj = prod[:, j * LANES:(j + 1) * LANES]
            d_ref[2 * j] = jnp.sum(jnp.where(lo, pj, 0.0), axis=-1, keepdims=True)
            d_ref[2 * j + 1] = jnp.sum(jnp.where(hi, pj, 0.0), axis=-1, keepdims=True)

    row = pl.BlockSpec((tm, BRANCH_W), lambda i: (i, 0))
    return pl.pallas_call(
        body, name=name, grid=(T // tm,), in_specs=[row, row],
        out_specs=pl.BlockSpec((N_HEADS, tm, 1), lambda i: (0, i, 0)),
        out_shape=jax.ShapeDtypeStruct((N_HEADS, T, 1), F32),
        compiler_params=_cp(("parallel",)),
    )(do, o)


def _attn_bwd_dq(proj, do, lse, delta, Fq, Fk, n_seq, name):
    T = proj.shape[0]
    S = T // n_seq
    blk = min(ATT_BLK, S)
    nb = S // blk
    scale = HEAD_DIM ** -0.5
    qc, kc, vc = OFF_Q // LANES, OFF_K // LANES, OFF_V // LANES

    def body(q_ref, k_ref, v_ref, do_ref, lse_ref, dl_ref, fq_ref, fk_ref, dq_ref, acc_s):
        qi, ki = pl.program_id(2), pl.program_id(3)

        @pl.when(ki == 0)
        def _():
            acc_s[...] = jnp.zeros_like(acc_s)

        @pl.when(ki <= qi)
        def _():
            q, k, v, do_ = q_ref[...], k_ref[...], v_ref[...], do_ref[...]
            mask = _causal_mask(qi, ki, blk)
            for hh, sel in enumerate(_pair_masks()):
                s = _attn_logits(q, k, fq_ref[hh], fk_ref[hh], sel, mask, scale)
                p = jnp.exp(s - lse_ref[hh])
                dom = jnp.where(sel, do_, jnp.zeros_like(do_))
                dp = lax.dot_general(dom, v, (((1,), (1,)), ((), ())), preferred_element_type=F32)
                ds = p * (dp - dl_ref[hh])
                acc_s[hh] += jnp.dot(ds.astype(BF16), k, preferred_element_type=F32)

        @pl.when(ki == qi)
        def _():
            lo, _ = _pair_masks()
            dq_ref[...] = (jnp.where(lo, acc_s[0], acc_s[1]) * scale).astype(BF16)

    qmap = lambda b, j, qi, ki: (b * nb + qi, j)
    col1 = pl.BlockSpec((2, blk, 1), lambda b, j, qi, ki: (j, b * nb + qi, 0))
    return pl.pallas_call(
        body, name=name, grid=(n_seq, N_HEADS // 2, nb, nb),
        in_specs=[
            pl.BlockSpec((blk, LANES), lambda b, j, qi, ki: (b * nb + qi, qc + j)),
            pl.BlockSpec((blk, LANES), lambda b, j, qi, ki: (b * nb + jnp.minimum(ki, qi), kc + j)),
            pl.BlockSpec((blk, LANES), lambda b, j, qi, ki: (b * nb + jnp.minimum(ki, qi), vc + j)),
            pl.BlockSpec((blk, LANES), qmap),
            col1, col1, col1,
            pl.BlockSpec((2, 1, blk), lambda b, j, qi, ki: (j, 0, b * nb + jnp.minimum(ki, qi))),
        ],
        out_specs=pl.BlockSpec((blk, LANES), qmap),
        out_shape=jax.ShapeDtypeStruct((T, BRANCH_W), BF16),
        scratch_shapes=[pltpu.VMEM((2, blk, LANES), F32)],
        compiler_params=_cp(("parallel", "parallel", "parallel", "arbitrary")),
    )(proj, proj, proj, do, lse, delta, Fq, Fk)


def _attn_bwd_dkv(proj, do, lse, delta, Fq, Fk, n_seq, name):
    T = proj.shape[0]
    S = T // n_seq
    blk = min(ATT_BLK, S)
    nb = S // blk
    scale = HEAD_DIM ** -0.5
    qc, kc, vc = OFF_Q // LANES, OFF_K // LANES, OFF_V // LANES
    tdot = functools.partial(lax.dot_general, dimension_numbers=(((0,), (0,)), ((), ())),
                             preferred_element_type=F32)

    def body(q_ref, k_ref, v_ref, do_ref, lse_ref, dl_ref, fq_ref, fk_ref, dk_ref, dv_ref, dfk_ref,
             dk_s, dv_s, df_s):
        ki, qi = pl.program_id(2), pl.program_id(3)

        @pl.when(qi == 0)
        def _():
            dk_s[...] = jnp.zeros_like(dk_s)
            dv_s[...] = jnp.zeros_like(dv_s)
            df_s[...] = jnp.zeros_like(df_s)

        @pl.when(qi >= ki)
        def _():
            q, k, v, do_ = q_ref[...], k_ref[...], v_ref[...], do_ref[...]
            mask = _causal_mask(qi, ki, blk)
            for hh, sel in enumerate(_pair_masks()):
                s = _attn_logits(q, k, fq_ref[hh], fk_ref[hh], sel, mask, scale)
                p = jnp.exp(s - lse_ref[hh])
                dv_s[hh] += tdot(p.astype(BF16), do_)
                dom = jnp.where(sel, do_, jnp.zeros_like(do_))
                dp = lax.dot_general(dom, v, (((1,), (1,)), ((), ())), preferred_element_type=F32)
                ds = p * (dp - dl_ref[hh])
                dk_s[hh] += tdot(ds.astype(BF16), q)
                df_s[hh] -= jnp.sum(ds, axis=0, keepdims=True)

        @pl.when(qi == nb - 1)
        def _():
            lo, _ = _pair_masks()
            dk_ref[...] = (jnp.where(lo, dk_s[0], dk_s[1]) * scale).astype(BF16)
            dv_ref[...] = jnp.where(lo, dv_s[0], dv_s[1]).astype(BF16)
            dfk_ref[...] = df_s[...]

    kmap = lambda b, j, ki, qi: (b * nb + ki, j)
    col1 = pl.BlockSpec((2, blk, 1), lambda b, j, ki, qi: (j, b * nb + jnp.maximum(qi, ki), 0))
    rowk = pl.BlockSpec((2, 1, blk), lambda b, j, ki, qi: (j, 0, b * nb + ki))
    return pl.pallas_call(
        body, name=name, grid=(n_seq, N_HEADS // 2, nb, nb),
        in_specs=[
            pl.BlockSpec((blk, LANES), lambda b, j, ki, qi: (b * nb + jnp.maximum(qi, ki), qc + j)),
            pl.BlockSpec((blk, LANES), lambda b, j, ki, qi: (b * nb + ki, kc + j)),
            pl.BlockSpec((blk, LANES), lambda b, j, ki, qi: (b * nb + ki, vc + j)),
            pl.BlockSpec((blk, LANES), lambda b, j, ki, qi: (b * nb + jnp.maximum(qi, ki), j)),
            col1, col1, col1, rowk,
        ],
        out_specs=[pl.BlockSpec((blk, LANES), kmap), pl.BlockSpec((blk, LANES), kmap), rowk],
        out_shape=[jax.ShapeDtypeStruct((T, BRANCH_W), BF16), jax.ShapeDtypeStruct((T, BRANCH_W), BF16),
                   jax.ShapeDtypeStruct((N_HEADS, 1, T), F32)],
        scratch_shapes=[pltpu.VMEM((2, blk, LANES), F32), pltpu.VMEM((2, blk, LANES), F32),
                        pltpu.VMEM((2, 1, blk), F32)],
        compiler_params=_cp(("parallel", "parallel", "parallel", "arbitrary")),
    )(proj, proj, proj, do, lse, delta, Fq, Fk)


AUG0 = HEAD_DIM
Q_TILE, K_CHUNK, ROW_GROUP = 512, 256, 64


def _fox_prep(f, bf, proj, n_seq, name):
    T = f.shape[0]
    S = T // n_seq
    c = min(CUM_BLK, S)

    def body(f_ref, b_ref, q_ref, k_ref, v_ref, qa_ref, ka_ref, va_ref):
        ri = lax.broadcasted_iota(jnp.int32, (c, c), 0)
        ci = lax.broadcasted_iota(jnp.int32, (c, c), 1)
        tri = (ri >= ci).astype(BF16)
        lane = lax.broadcasted_iota(jnp.int32, (c, LANES), 1)
        carry = jnp.zeros((1, LANES), F32)
        for j in range(S // c):
            rows = slice(j * c, (j + 1) * c)
            lf = _log_sigmoid(f_ref[rows, :] + b_ref[...])
            Fc = _tri_dot(tri, lf) + carry
            carry = carry + jnp.sum(lf, axis=0, keepdims=True)
            for h in range(N_HEADS):
                col = jnp.sum(jnp.where(lane == h, Fc, 0.0), axis=-1, keepdims=True)
                hi = col.astype(BF16).astype(F32)
                r1 = col - hi
                mid = r1.astype(BF16).astype(F32)
                lo = r1 - mid
                ones_q = jnp.logical_and(lane >= AUG0 + 3, lane < AUG0 + 6)
                ones_k = jnp.logical_and(lane >= AUG0, lane < AUG0 + 3)
                aug_q = jnp.where(lane == AUG0, hi, jnp.where(lane == AUG0 + 1, mid, jnp.where(
                    lane == AUG0 + 2, lo, jnp.where(ones_q, 1.0, 0.0))))
                aug_k = jnp.where(lane == AUG0 + 3, -hi, jnp.where(lane == AUG0 + 4, -mid, jnp.where(
                    lane == AUG0 + 5, -lo, jnp.where(ones_k, 1.0, 0.0))))
                pair = slice((h // 2) * LANES, (h // 2 + 1) * LANES)
                qp, kp = q_ref[rows, pair].astype(F32), k_ref[rows, pair].astype(F32)
                vp = v_ref[rows, pair].astype(F32)
                if h % 2:
                    qp, kp, vp = (pltpu.roll(a, HEAD_DIM, 1) for a in (qp, kp, vp))
                out = slice(h * LANES, (h + 1) * LANES)
                qa_ref[rows, out] = jnp.where(lane < HEAD_DIM, qp * (HEAD_DIM ** -0.5), aug_q).astype(BF16)
                ka_ref[rows, out] = jnp.where(lane < HEAD_DIM, kp, aug_k).astype(BF16)
                va_ref[rows, out] = jnp.where(lane < HEAD_DIM, vp, jnp.where(lane == AUG0, 1.0, 0.0)).astype(BF16)

    fblk = pl.BlockSpec((S, LANES), lambda b: (b, 0))
    out = pl.BlockSpec((S, N_HEADS * LANES), lambda b: (b, 0))
    sh = jax.ShapeDtypeStruct((T, N_HEADS * LANES), BF16)
    return pl.pallas_call(
        body, name=name, grid=(n_seq,),
        in_specs=[fblk, pl.BlockSpec((1, LANES), lambda b: (0, 0)),
                  pl.BlockSpec((S, BRANCH_W), lambda b: (b, OFF_Q // BRANCH_W)),
                  pl.BlockSpec((S, BRANCH_W), lambda b: (b, OFF_K // BRANCH_W)),
                  pl.BlockSpec((S, BRANCH_W), lambda b: (b, OFF_V // BRANCH_W))],
        out_specs=[out, out, out], out_shape=[sh, sh, sh],
        compiler_params=_cp(("parallel",)),
    )(f, bf, proj, proj, proj)


def _band_mask(q0, k0, nq, nk):
    row = q0 + lax.broadcasted_iota(jnp.int32, (nq, nk), 0)
    col = k0 + lax.broadcasted_iota(jnp.int32, (nq, nk), 1)
    return col <= row


_NT = (((1,), (1,)), ((), ()))
_TN = (((0,), (0,)), ((), ()))


def _attn_fwd2(qa, ka, va, n_seq, name):
    T = qa.shape[0]
    S = T // n_seq
    tq, tk, rg = min(Q_TILE, S), min(K_CHUNK, S), ROW_GROUP
    nq, per = S // tq, tq // tk

    def body(q_ref, k_ref, v_ref, o_ref, o32_ref, lse_ref, phi_s, plo_s, mp_s, m_s, acc_s):
        qi = pl.program_id(2)
        mp_s[...] = jnp.full_like(mp_s, NEG_INF)
        acc_s[...] = jnp.zeros_like(acc_s)

        def scores(kc, hh):
            k0 = pl.multiple_of(kc * tk, tk)
            hl = slice(hh * LANES, (hh + 1) * LANES)
            return k0, lax.dot_general(q_ref[:, hl], k_ref[pl.ds(k0, tk), hl], _NT, preferred_element_type=F32)

        def max_chunk(kc, masked):
            for hh in range(2):
                k0, s_all = scores(kc, hh)
                for r in range(tq // rg):
                    rows = slice(r * rg, (r + 1) * rg)
                    s = s_all[rows, :]
                    if masked:
                        s = jnp.where(_band_mask(qi * tq + r * rg, k0, rg, tk), s, NEG_INF)
                    part = s[:, :LANES]
                    for c in range(1, tk // LANES):
                        part = jnp.maximum(part, s[:, c * LANES:(c + 1) * LANES])
                    mp_s[hh, rows, :] = jnp.maximum(mp_s[hh, rows, :], part)

        def sum_chunk(kc, masked):
            for hh in range(2):
                k0, s_all = scores(kc, hh)
                hl = slice(hh * LANES, (hh + 1) * LANES)
                v = v_ref[pl.ds(k0, tk), hl]
                for r in range(tq // rg):
                    rows = slice(r * rg, (r + 1) * rg)
                    p = jnp.exp(s_all[rows, :] - m_s[hh, rows])
                    if masked:
                        p = jnp.where(_band_mask(qi * tq + r * rg, k0, rg, tk), p, 0.0)
                    p_hi = p.astype(BF16)
                    phi_s[hh, rows, :] = p_hi
                    plo_s[hh, rows, :] = (p - p_hi.astype(F32)).astype(BF16)
                acc_s[hh] += (jnp.dot(phi_s[hh], v, preferred_element_type=F32)
                              + jnp.dot(plo_s[hh], v, preferred_element_type=F32))

        def sweep(chunk):
            def unmasked(kc, carry):
                chunk(kc, False)
                return carry

            lax.fori_loop(0, qi * per, unmasked, 0)
            for d in range(per):
                chunk(qi * per + d, True)

        sweep(max_chunk)
        m_s[...] = jnp.max(mp_s[...], axis=-1, keepdims=True)
        sweep(sum_chunk)

        lane = lax.broadcasted_iota(jnp.int32, (1, LANES), 1)
        outs = []
        for hh in range(2):
            acc = acc_s[hh]
            l = jnp.sum(jnp.where(lane == AUG0, acc, 0.0), axis=-1, keepdims=True)
            lse_ref[hh] = m_s[hh] + jnp.log(l)
            outs.append(acc / l)
        o = jnp.where(lane < HEAD_DIM, outs[0], pltpu.roll(outs[1], HEAD_DIM, 1))
        o_ref[...] = o.astype(BF16)
        o32_ref[...] = o

    qmap = lambda b, j, qi: (b * nq + qi, j)
    omap = lambda b, j, qi: (b * nq + qi, j)
    kv = pl.BlockSpec((S, 2 * LANES), lambda b, j, qi: (b, j))
    return pl.pallas_call(
        body, name=name, grid=(n_seq, N_HEADS // 2, nq),
        in_specs=[pl.BlockSpec((tq, 2 * LANES), qmap), kv, kv],
        out_specs=[pl.BlockSpec((tq, LANES), omap), pl.BlockSpec((tq, LANES), omap),
                   pl.BlockSpec((2, tq, 1), lambda b, j, qi: (j, b * nq + qi, 0))],
        out_shape=[jax.ShapeDtypeStruct((T, BRANCH_W), BF16), jax.ShapeDtypeStruct((T, BRANCH_W), F32),
                   jax.ShapeDtypeStruct((N_HEADS, T, 1), F32)],
        scratch_shapes=[pltpu.VMEM((2, tq, tk), BF16), pltpu.VMEM((2, tq, tk), BF16),
                        pltpu.VMEM((2, tq, LANES), F32), pltpu.VMEM((2, tq, 1), F32),
                        pltpu.VMEM((2, tq, LANES), F32)],
        compiler_params=_cp(("parallel", "parallel", "parallel")),
    )(qa, ka, va)


def _attn_bwd_dq2(qa, ka, proj, do, lse, delta, n_seq, name):
    T = qa.shape[0]
    S = T // n_seq
    tq, tk, rg = min(Q_TILE, S), min(K_CHUNK, S), ROW_GROUP
    nq, per = S // tq, tq // tk
    vc = OFF_V // LANES

    def body(q_ref, k_ref, v_ref, do_ref, lse_ref, dl_ref, dq_ref, ds_s, acc_s):
        qi = pl.program_id(2)
        acc_s[...] = jnp.zeros_like(acc_s)
        sels = _pair_masks()

        def chunk(kc, masked):
            k0 = pl.multiple_of(kc * tk, tk)
            v = v_ref[pl.ds(k0, tk), :]
            for hh in range(2):
                hl = slice(hh * LANES, (hh + 1) * LANES)
                kh = k_ref[pl.ds(k0, tk), hl]
                s_all = lax.dot_general(q_ref[:, hl], kh, _NT, preferred_element_type=F32)
                dom = jnp.where(sels[hh], do_ref[...], jnp.zeros_like(do_ref[...]))
                dp_all = lax.dot_general(dom, v, _NT, preferred_element_type=F32)
                for r in range(tq // rg):
                    rows = slice(r * rg, (r + 1) * rg)
                    p = jnp.exp(s_all[rows, :] - lse_ref[hh, rows])
                    if masked:
                        p = jnp.where(_band_mask(qi * tq + r * rg, k0, rg, tk), p, 0.0)
                    ds_s[hh, rows, :] = (p * (dp_all[rows, :] - dl_ref[hh, rows])).astype(BF16)
                acc_s[hh] += jnp.dot(ds_s[hh], kh, preferred_element_type=F32)

        def unmasked(kc, carry):
            chunk(kc, False)
            return carry

        lax.fori_loop(0, qi * per, unmasked, 0)
        for d in range(per):
            chunk(qi * per + d, True)
        dq = jnp.where(sels[0], acc_s[0], pltpu.roll(acc_s[1], HEAD_DIM, 1))
        dq_ref[...] = (dq * (HEAD_DIM ** -0.5)).astype(BF16)

    qmap = lambda b, j, qi: (b * nq + qi, j)
    col1 = pl.BlockSpec((2, tq, 1), lambda b, j, qi: (j, b * nq + qi, 0))
    return pl.pallas_call(
        body, name=name, grid=(n_seq, N_HEADS // 2, nq),
        in_specs=[pl.BlockSpec((tq, 2 * LANES), qmap),
                  pl.BlockSpec((S, 2 * LANES), lambda b, j, qi: (b, j)),
                  pl.BlockSpec((S, LANES), lambda b, j, qi: (b, vc + j)),
                  pl.BlockSpec((tq, LANES), qmap), col1, col1],
        out_specs=pl.BlockSpec((tq, LANES), qmap),
        out_shape=jax.ShapeDtypeStruct((T, BRANCH_W), BF16),
        scratch_shapes=[pltpu.VMEM((2, tq, tk), BF16), pltpu.VMEM((2, tq, LANES), F32)],
        compiler_params=_cp(("parallel", "parallel", "parallel")),
    )(qa, ka, proj, do, lse, delta)


def _attn_bwd_dkv2(qa, ka, proj, do, lse, delta, n_seq, name):
    T = qa.shape[0]
    S = T // n_seq
    tkt, tqc, rg = min(Q_TILE, S), min(K_CHUNK, S), ROW_GROUP // 2
    nk, per, nqc = S // tkt, tkt // tqc, S // tqc
    vc = OFF_V // LANES

    def body(q_ref, k_ref, v_ref, do_ref, lse_ref, dl_ref, dk_ref, dv_ref, dfk_ref,
             p_s, ds_s, dk_s, dv_s, df_s):
        ki = pl.program_id(2)
        dk_s[...] = jnp.zeros_like(dk_s)
        dv_s[...] = jnp.zeros_like(dv_s)
        df_s[...] = jnp.zeros_like(df_s)
        sels = _pair_masks()
        v = v_ref[...]

        def chunk(qc, masked):
            q0 = pl.multiple_of(qc * tqc, tqc)
            do_ = do_ref[pl.ds(q0, tqc), :]
            for hh in range(2):
                hl = slice(hh * LANES, (hh + 1) * LANES)
                qh = q_ref[pl.ds(q0, tqc), hl]
                s_all = lax.dot_general(qh, k_ref[:, hl], _NT, preferred_element_type=F32)
                dom = jnp.where(sels[hh], do_, jnp.zeros_like(do_))
                dp_all = lax.dot_general(dom, v, _NT, preferred_element_type=F32)
                dfp = jnp.zeros((1, tkt), F32)
                for r in range(tqc // rg):
                    rows = slice(r * rg, (r + 1) * rg)
                    qrows = pl.ds(q0 + r * rg, rg)
                    p = jnp.exp(s_all[rows, :] - lse_ref[hh, qrows])
                    if masked:
                        p = jnp.where(_band_mask(q0 + r * rg, ki * tkt, rg, tkt), p, 0.0)
                    ds = p * (dp_all[rows, :] - dl_ref[hh, qrows])
                    p_s[hh, rows, :] = p.astype(BF16)
                    ds_s[hh, rows, :] = ds.astype(BF16)
                    dfp = dfp + jnp.sum(ds, axis=0, keepdims=True)
                df_s[hh] -= dfp
                dv_s[hh] += lax.dot_general(p_s[hh], do_, _TN, preferred_element_type=F32)
                dk_s[hh] += lax.dot_general(ds_s[hh], qh, _TN, preferred_element_type=F32)

        for d in range(per):
            chunk(ki * per + d, True)

        def unmasked(qc, carry):
            chunk(qc, False)
            return carry

        lax.fori_loop((ki + 1) * per, nqc, unmasked, 0)
        dk_ref[...] = jnp.where(sels[0], dk_s[0], pltpu.roll(dk_s[1], HEAD_DIM, 1)).astype(BF16)
        dv_ref[...] = jnp.where(sels[0], dv_s[0], dv_s[1]).astype(BF16)
        dfk_ref[...] = df_s[...]

    kmap = lambda b, j, ki: (b * nk + ki, j)
    col1 = pl.BlockSpec((2, S, 1), lambda b, j, ki: (j, b, 0))
    rowk = pl.BlockSpec((2, 1, tkt), lambda b, j, ki: (j, 0, b * nk + ki))
    return pl.pallas_call(
        body, name=name, grid=(n_seq, N_HEADS // 2, nk),
        in_specs=[pl.BlockSpec((S, 2 * LANES), lambda b, j, ki: (b, j)),
                  pl.BlockSpec((tkt, 2 * LANES), kmap),
                  pl.BlockSpec((tkt, LANES), lambda b, j, ki: (b * nk + ki, vc + j)),
                  pl.BlockSpec((S, LANES), lambda b, j, ki: (b, j)), col1, col1],
        out_specs=[pl.BlockSpec((tkt, LANES), kmap), pl.BlockSpec((tkt, LANES), kmap), rowk],
        out_shape=[jax.ShapeDtypeStruct((T, BRANCH_W), BF16), jax.ShapeDtypeStruct((T, BRANCH_W), BF16),
                   jax.ShapeDtypeStruct((N_HEADS, 1, T), F32)],
        scratch_shapes=[pltpu.VMEM((2, tqc, tkt), BF16), pltpu.VMEM((2, tqc, tkt), BF16),
                        pltpu.VMEM((2, tkt, LANES), F32),
                        pltpu.VMEM((2, tkt, LANES), F32), pltpu.VMEM((2, 1, tkt), F32)],
        compiler_params=_cp(("parallel", "parallel", "parallel")),
    )(qa, ka, proj, do, lse, delta)


def _attn_bwd(qa, ka, proj, do, lse, delta, n_seq, name):
    T = qa.shape[0]
    S = T // n_seq
    tq, tk, rg = min(Q_TILE, S), min(K_CHUNK, S), ROW_GROUP
    nq, per, nkc = S // tq, tq // tk, S // tk
    vc = OFF_V // LANES

    def body(q_ref, k_ref, v_ref, do_ref, lse_ref, dl_ref, dq_ref, dk_ref, dv_ref, dfk_ref,
             p_s, ds_s, dq_s, dk_s, dv_s, df_s):
        dk_s[...] = jnp.zeros_like(dk_s)
        dv_s[...] = jnp.zeros_like(dv_s)
        df_s[...] = jnp.zeros_like(df_s)
        sels = _pair_masks()

        for qi in range(nq):
            q0 = qi * tq
            do_t = do_ref[q0:q0 + tq, :]
            dq_s[...] = jnp.zeros_like(dq_s)

            def chunk(kc, masked, q0=q0, do_t=do_t):
                k0 = pl.multiple_of(kc * tk, tk)
                v = v_ref[pl.ds(k0, tk), :]
                for hh in range(2):
                    hl = slice(hh * LANES, (hh + 1) * LANES)
                    qh, kh = q_ref[q0:q0 + tq, hl], k_ref[pl.ds(k0, tk), hl]
                    s_all = lax.dot_general(qh, kh, _NT, preferred_element_type=F32)
                    dom = jnp.where(sels[hh], do_t, jnp.zeros_like(do_t))
                    dp_all = lax.dot_general(dom, v, _NT, preferred_element_type=F32)
                    dfp = jnp.zeros((1, tk), F32)
                    for r in range(tq // rg):
                        rows = slice(r * rg, (r + 1) * rg)
                        qrows = slice(q0 + r * rg, q0 + (r + 1) * rg)
                        p = jnp.exp(s_all[rows, :] - lse_ref[hh, qrows])
                        if masked:
                            p = jnp.where(_band_mask(q0 + r * rg, k0, rg, tk), p, 0.0)
                        ds = p * (dp_all[rows, :] - dl_ref[hh, qrows])
                        p_s[hh, rows, :] = p.astype(BF16)
                        ds_s[hh, rows, :] = ds.astype(BF16)
                        dfp = dfp + jnp.sum(ds, axis=0, keepdims=True)
                    df_s[hh, kc] -= dfp
                    dq_s[hh] += jnp.dot(ds_s[hh], kh, preferred_element_type=F32)
                    dv_s[hh, pl.ds(k0, tk), :] += lax.dot_general(p_s[hh], do_t, _TN, preferred_element_type=F32)
                    dk_s[hh, pl.ds(k0, tk), :] += lax.dot_general(ds_s[hh], qh, _TN, preferred_element_type=F32)

            def unmasked(kc, carry, chunk=chunk):
                chunk(kc, False)
                return carry

            lax.fori_loop(0, qi * per, unmasked, 0)
            for d in range(per):
                chunk(qi * per + d, True)
            dq = jnp.where(sels[0], dq_s[0], pltpu.roll(dq_s[1], HEAD_DIM, 1))
            dq_ref[q0:q0 + tq, :] = (dq * (HEAD_DIM ** -0.5)).astype(BF16)

        dk_ref[...] = jnp.where(sels[0], dk_s[0], pltpu.roll(dk_s[1], HEAD_DIM, 1)).astype(BF16)
        dv_ref[...] = jnp.where(sels[0], dv_s[0], dv_s[1]).astype(BF16)
        for c in range(nkc):
            dfk_ref[:, :, c * tk:(c + 1) * tk] = df_s[:, c]

    seq = lambda w: pl.BlockSpec((S, w), lambda b, j: (b, j))
    col1 = pl.BlockSpec((2, S, 1), lambda b, j: (j, b, 0))
    act = jax.ShapeDtypeStruct((T, BRANCH_W), BF16)
    return pl.pallas_call(
        body, name=name, grid=(n_seq, N_HEADS // 2),
        in_specs=[seq(2 * LANES), seq(2 * LANES), pl.BlockSpec((S, LANES), lambda b, j: (b, vc + j)), seq(LANES),
                  col1, col1],
        out_specs=[seq(LANES), seq(LANES), seq(LANES), pl.BlockSpec((2, 1, S), lambda b, j: (j, 0, b))],
        out_shape=[act, act, act, jax.ShapeDtypeStruct((N_HEADS, 1, T), F32)],
        scratch_shapes=[pltpu.VMEM((2, tq, tk), BF16), pltpu.VMEM((2, tq, tk), BF16),
                        pltpu.VMEM((2, tq, LANES), F32), pltpu.VMEM((2, S, LANES), F32),
                        pltpu.VMEM((2, S, LANES), F32), pltpu.VMEM((2, nkc, 1, tk), F32)],
        compiler_params=_cp(("parallel", "parallel")),
    )(qa, ka, proj, do, lse, delta)


def _shift_down(v, k, row):
    return jnp.where(row >= k, pltpu.roll(v, k, 0), 0.0)


def _shift_up(v, k, row, S):
    return jnp.where(row < S - k, pltpu.roll(v, S - k, 0), 0.0)


def _pool_diff(uf, w, row):
    acc, k = uf, 1
    while k < w:
        acc = acc + _shift_down(acc, k, row)
        k *= 2
    n = jnp.minimum(row + 1, w).astype(F32)
    return acc / n - uf


def _pool_fwd(proj, pool_w, pool_scale, n_seq, name):
    T = proj.shape[0]
    S = T // n_seq

    def body(u_ref, w_ref, sc_ref, o_ref):
        g = pl.program_id(1)
        row = lax.broadcasted_iota(jnp.int32, (S, POOL_GD), 0)
        uf = u_ref[...].astype(F32)
        d = _pool_diff(uf, POOL_WINDOWS[0], row)
        for gi in range(1, len(POOL_WINDOWS)):
            d = jnp.where(g == gi, _pool_diff(uf, POOL_WINDOWS[gi], row), d)
        e = jnp.dot(d.astype(BF16), w_ref[0], preferred_element_type=F32)
        o_ref[...] = (e * sc_ref[...]).astype(BF16)

    uc = OFF_U // POOL_GD
    return pl.pallas_call(
        body, name=name, grid=(n_seq, len(POOL_WINDOWS)),
        in_specs=[pl.BlockSpec((S, POOL_GD), lambda b, g: (b, uc + g)),
                  pl.BlockSpec((1, POOL_GD, POOL_GD), lambda b, g: (g, 0, 0)),
                  pl.BlockSpec((1, POOL_GD), lambda b, g: (0, g))],
        out_specs=pl.BlockSpec((S, POOL_GD), lambda b, g: (b, g)),
        out_shape=jax.ShapeDtypeStruct((T, BRANCH_W), BF16),
        compiler_params=_cp(("parallel", "parallel")),
    )(proj, pool_w, pool_scale)


def _pool_bwd(proj, dout, pool_w, pool_scale, n_seq, name):
    T = proj.shape[0]
    S = T // n_seq

    def body(u_ref, do_ref, w_ref, sc_ref, du_ref, dw_ref, dsc_ref):
        g, b = pl.program_id(0), pl.program_id(1)
        row = lax.broadcasted_iota(jnp.int32, (S, POOL_GD), 0)
        uf = u_ref[...].astype(F32)
        d = _pool_diff(uf, POOL_WINDOWS[0], row)
        for gi in range(1, len(POOL_WINDOWS)):
            d = jnp.where(g == gi, _pool_diff(uf, POOL_WINDOWS[gi], row), d)
        db16 = d.astype(BF16)
        w = w_ref[0]
        e = jnp.dot(db16, w, preferred_element_type=F32)
        dof = do_ref[...].astype(F32)
        dsc = jnp.sum(dof * e, axis=0, keepdims=True)
        de = (dof * sc_ref[...]).astype(BF16)
        dd = lax.dot_general(de, w, (((1,), (1,)), ((), ())), preferred_element_type=F32)
        dw = lax.dot_general(db16, de, (((0,), (0,)), ((), ())), preferred_element_type=F32)
        du = jnp.zeros_like(dd)
        for gi, wlen in enumerate(POOL_WINDOWS):
            n = jnp.minimum(row + 1, wlen).astype(F32)
            acc, k = dd / n, 1
            while k < wlen:
                acc = acc + _shift_up(acc, k, row, S)
                k *= 2
            du = jnp.where(g == gi, acc - dd, du)
        du_ref[...] = du.astype(BF16)

        @pl.when(b == 0)
        def _():
            dw_ref[0] = dw
            dsc_ref[...] = dsc

        @pl.when(b > 0)
        def _():
            dw_ref[0] += dw
            dsc_ref[...] += dsc

    uc = OFF_U // POOL_GD
    return pl.pallas_call(
        body, name=name, grid=(len(POOL_WINDOWS), n_seq),
        in_specs=[pl.BlockSpec((S, POOL_GD), lambda g, b: (b, uc + g)),
                  pl.BlockSpec((S, POOL_GD), lambda g, b: (b, g)),
                  pl.BlockSpec((1, POOL_GD, POOL_GD), lambda g, b: (g, 0, 0)),
                  pl.BlockSpec((1, POOL_GD), lambda g, b: (0, g))],
        out_specs=[pl.BlockSpec((S, POOL_GD), lambda g, b: (b, g)),
                   pl.BlockSpec((1, POOL_GD, POOL_GD), lambda g, b: (g, 0, 0)),
                   pl.BlockSpec((1, POOL_GD), lambda g, b: (0, g))],
        out_shape=[jax.ShapeDtypeStruct((T, BRANCH_W), BF16),
                   jax.ShapeDtypeStruct((len(POOL_WINDOWS), POOL_GD, POOL_GD), F32),
                   jax.ShapeDtypeStruct((1, BRANCH_W), F32)],
        compiler_params=_cp(("parallel", "arbitrary")),
    )(proj, dout, pool_w, pool_scale)


def _conv_fwd(proj, conv_w, n_seq, name):
    T = proj.shape[0]
    S = T // n_seq
    nc = BRANCH_W // LANES

    def body(cv_ref, cb_ref, cc_ref, w_ref, o_ref):
        row = lax.broadcasted_iota(jnp.int32, (S, LANES), 0)
        z = cc_ref[...].astype(F32) * cv_ref[...].astype(F32)
        w = w_ref[...]
        y = w[0:1] * _shift_down(z, 2, row) + w[1:2] * _shift_down(z, 1, row) + w[2:3] * z
        o_ref[...] = (cb_ref[...].astype(F32) * y).astype(BF16)

    def col(off):
        return pl.BlockSpec((S, LANES), lambda b, j: (b, off // LANES + j))

    return pl.pallas_call(
        body, name=name, grid=(n_seq, nc),
        in_specs=[col(OFF_CV), col(OFF_CB), col(OFF_CC), pl.BlockSpec((CONV_K, LANES), lambda b, j: (0, j))],
        out_specs=pl.BlockSpec((S, LANES), lambda b, j: (b, j)),
        out_shape=jax.ShapeDtypeStruct((T, BRANCH_W), BF16),
        compiler_params=_cp(("parallel", "parallel")),
    )(proj, proj, proj, conv_w)


def _conv_bwd(proj, dout, conv_w, n_seq, name):
    T = proj.shape[0]
    S = T // n_seq
    nc = BRANCH_W // LANES

    def body(cv_ref, cb_ref, cc_ref, do_ref, w_ref, dcv_ref, dcb_ref, dcc_ref, dw_ref):
        b = pl.program_id(1)
        row = lax.broadcasted_iota(jnp.int32, (S, LANES), 0)
        cv, cb, cc = cv_ref[...].astype(F32), cb_ref[...].astype(F32), cc_ref[...].astype(F32)
        dof = do_ref[...].astype(F32)
        w = w_ref[...]
        z = cc * cv
        z1, z2 = _shift_down(z, 1, row), _shift_down(z, 2, row)
        y = w[0:1] * z2 + w[1:2] * z1 + w[2:3] * z
        dcb_ref[...] = (dof * y).astype(BF16)
        dy = dof * cb
        dz = w[2:3] * dy + w[1:2] * _shift_up(dy, 1, row, S) + w[0:1] * _shift_up(dy, 2, row, S)
        dcc_ref[...] = (dz * cv).astype(BF16)
        dcv_ref[...] = (dz * cc).astype(BF16)
        dws = [jnp.sum(dy * zk, axis=0, keepdims=True) for zk in (z2, z1, z)]

        @pl.when(b == 0)
        def _():
            for kk in range(CONV_K):
                dw_ref[kk:kk + 1, :] = dws[kk]

        @pl.when(b > 0)
        def _():
            for kk in range(CONV_K):
                dw_ref[kk:kk + 1, :] += dws[kk]

    def col(off):
        return pl.BlockSpec((S, LANES), lambda j, b: (b, off // LANES + j))

    out = pl.BlockSpec((S, LANES), lambda j, b: (b, j))
    wsp = pl.BlockSpec((CONV_K, LANES), lambda j, b: (0, j))
    act = jax.ShapeDtypeStruct((T, BRANCH_W), BF16)
    return pl.pallas_call(
        body, name=name, grid=(nc, n_seq),
        in_specs=[col(OFF_CV), col(OFF_CB), col(OFF_CC), out, wsp],
        out_specs=[out, out, out, wsp],
        out_shape=[act, act, act, jax.ShapeDtypeStruct((CONV_K, BRANCH_W), F32)],
        compiler_params=_cp(("parallel", "arbitrary")),
    )(proj, proj, proj, dout, conv_w)


def _mix_fwd(oa, ob, oc, wpa, wpp, wpc, proj, b_gate, name):
    T = oa.shape[0]
    tm = min(256, T)

    def body(oa_ref, ob_ref, oc_ref, wa_ref, wp_ref, wc_ref, g_ref, bg_ref, o_ref):
        acc = jnp.zeros((tm, D_MODEL), F32)
        for i, (x_ref, w_ref) in enumerate(((oa_ref, wa_ref), (ob_ref, wp_ref), (oc_ref, wc_ref))):
            y = jnp.dot(x_ref[...], w_ref[...], preferred_element_type=F32)
            sl = slice(i * D_MODEL, (i + 1) * D_MODEL)
            acc = acc + _sigmoid(g_ref[:, sl].astype(F32) + bg_ref[:, sl]) * y
        o_ref[...] = acc.astype(BF16)

    br = pl.BlockSpec((tm, BRANCH_W), lambda i: (i, 0))
    wsp = pl.BlockSpec((BRANCH_W, D_MODEL), lambda i: (0, 0))
    return pl.pallas_call(
        body, name=name, grid=(T // tm,),
        in_specs=[br, br, br, wsp, wsp, wsp, pl.BlockSpec((tm, GATE_W), lambda i: (i, 0)),
                  pl.BlockSpec((1, GATE_W), lambda i: (0, 0))],
        out_specs=pl.BlockSpec((tm, D_MODEL), lambda i: (i, 0)),
        out_shape=jax.ShapeDtypeStruct((T, D_MODEL), BF16),
        compiler_params=_cp(("parallel",)),
    )(oa, ob, oc, wpa, wpp, wpc, proj, b_gate)


def _mix_bwd(oa, ob, oc, wpa, wpp, wpc, proj, b_gate, dmixed, name):
    T = oa.shape[0]
    tm = min(256, T)

    def body(oa_ref, ob_ref, oc_ref, wa_ref, wp_ref, wc_ref, g_ref, bg_ref, dm_ref,
             dya_ref, dyb_ref, dyc_ref, dg_ref, dbg_ref):
        i0 = pl.program_id(0)
        dm = dm_ref[...].astype(F32)
        parts = []
        for i, (x_ref, w_ref, dy_ref) in enumerate(((oa_ref, wa_ref, dya_ref), (ob_ref, wp_ref, dyb_ref),
                                                    (oc_ref, wc_ref, dyc_ref))):
            y = jnp.dot(x_ref[...], w_ref[...], preferred_element_type=F32)
            sl = slice(i * D_MODEL, (i + 1) * D_MODEL)
            gate = _sigmoid(g_ref[:, sl].astype(F32) + bg_ref[:, sl])
            dy_ref[...] = (dm * gate).astype(BF16)
            dgl = dm * y * gate * (1.0 - gate)
            dg_ref[:, sl] = dgl.astype(BF16)
            parts.append(jnp.sum(dgl, axis=0, keepdims=True))

        @pl.when(i0 == 0)
        def _():
            for i in range(3):
                dbg_ref[:, i * D_MODEL:(i + 1) * D_MODEL] = parts[i]

        @pl.when(i0 > 0)
        def _():
            for i in range(3):
                dbg_ref[:, i * D_MODEL:(i + 1) * D_MODEL] += parts[i]

    br = pl.BlockSpec((tm, BRANCH_W), lambda i: (i, 0))
    wsp = pl.BlockSpec((BRANCH_W, D_MODEL), lambda i: (0, 0))
    row = pl.BlockSpec((tm, D_MODEL), lambda i: (i, 0))
    gsp = pl.BlockSpec((tm, GATE_W), lambda i: (i, 0))
    bsp = pl.BlockSpec((1, GATE_W), lambda i: (0, 0))
    act = jax.ShapeDtypeStruct((T, D_MODEL), BF16)
    return pl.pallas_call(
        body, name=name, grid=(T // tm,),
        in_specs=[br, br, br, wsp, wsp, wsp, gsp, bsp, row],
        out_specs=[row, row, row, gsp, bsp],
        out_shape=[act, act, act, jax.ShapeDtypeStruct((T, GATE_W), BF16),
                   jax.ShapeDtypeStruct((1, GATE_W), F32)],
        compiler_params=_cp(("arbitrary",)),
    )(oa, ob, oc, wpa, wpp, wpc, proj, b_gate, dmixed)


GU_TILE = 256


def _gu_col(c):
    t, r = divmod(c, GU_TILE)
    return (t // 2) * GU_TILE + r + (FFN_HIDDEN if t % 2 else 0)


def _gate_up_swiglu(h, w, name):
    T, K = h.shape
    tm = min(2048, T)

    def body(h_ref, w_ref, ab_ref, s_ref):
        prod = jnp.dot(h_ref[...], w_ref[...], preferred_element_type=F32)
        ab_ref[...] = prod.astype(BF16)
        a = prod[:, :GU_TILE]
        s_ref[...] = (a * _sigmoid(a) * prod[:, GU_TILE:]).astype(BF16)

    return pl.pallas_call(
        body, name=name, grid=(T // tm, FFN_HIDDEN // GU_TILE),
        in_specs=[pl.BlockSpec((tm, K), lambda i, j: (i, 0)), pl.BlockSpec((K, 2 * GU_TILE), lambda i, j: (0, j))],
        out_specs=[pl.BlockSpec((tm, 2 * GU_TILE), lambda i, j: (i, j)), pl.BlockSpec((tm, GU_TILE), lambda i, j: (i, j))],
        out_shape=[jax.ShapeDtypeStruct((T, 2 * FFN_HIDDEN), BF16), jax.ShapeDtypeStruct((T, FFN_HIDDEN), BF16)],
        compiler_params=_cp(("parallel", "parallel")),
    )(h, w)


def _swiglu_bwd_fused(dx, w_down, ab, name):
    T, K = dx.shape
    tm = min(2048, T)

    def body(dx_ref, w_ref, ab_ref, o_ref):
        ds = lax.dot_general(dx_ref[...], w_ref[...], _NT, preferred_element_type=F32)
        a = ab_ref[:, :GU_TILE].astype(F32)
        b = ab_ref[:, GU_TILE:].astype(F32)
        sg = _sigmoid(a)
        o_ref[:, :GU_TILE] = (ds * b * sg * (1.0 + a * (1.0 - sg))).astype(BF16)
        o_ref[:, GU_TILE:] = (ds * a * sg).astype(BF16)

    pair = pl.BlockSpec((tm, 2 * GU_TILE), lambda i, j: (i, j))
    return pl.pallas_call(
        body, name=name, grid=(T // tm, FFN_HIDDEN // GU_TILE),
        in_specs=[pl.BlockSpec((tm, K), lambda i, j: (i, 0)), pl.BlockSpec((GU_TILE, K), lambda i, j: (j, 0)), pair],
        out_specs=pair, out_shape=jax.ShapeDtypeStruct((T, 2 * FFN_HIDDEN), BF16),
        compiler_params=_cp(("parallel", "parallel")),
    )(dx, w_down, ab)


def _adamw(w, g, m, v, name):
    R, C = w.shape
    tr = R
    for cand in (256, 352, 128, 64, 8):
        if R > cand and R % cand == 0:
            tr = cand
            break

    def body(w_ref, g_ref, m_ref, v_ref, d_ref, nm_ref, nv_ref):
        gv = g_ref[...]
        nm = ADAM_B1 * m_ref[...] + (1.0 - ADAM_B1) * gv
        nv = ADAM_B2 * v_ref[...] + (1.0 - ADAM_B2) * (gv * gv)
        m_hat = nm / (1.0 - ADAM_B1 ** ADAM_STEP)
        v_hat = nv / (1.0 - ADAM_B2 ** ADAM_STEP)
        d_ref[...] = -ADAM_LR * (m_hat / (jnp.sqrt(v_hat) + ADAM_EPS) + ADAM_WD * w_ref[...])
        nm_ref[...] = nm
        nv_ref[...] = nv

    blk = pl.BlockSpec((tr, C), lambda i: (i, 0))
    sh = jax.ShapeDtypeStruct((R, C), F32)
    return pl.pallas_call(
        body, name=name, grid=(R // tr,), in_specs=[blk] * 4, out_specs=[blk] * 3, out_shape=[sh] * 3,
        compiler_params=_cp(("parallel",)),
    )(w, g, m, v)


def _sum_slabs(x, name):
    n, R, C = x.shape
    tr = R
    for cand in (512, 256, 128, 64, 32, 16, 8):
        if R > cand and R % cand == 0:
            tr = cand
            break

    def body(x_ref, o_ref):
        acc = x_ref[0].astype(F32)
        for j in range(1, n):
            acc = acc + x_ref[j].astype(F32)
        o_ref[...] = acc

    return pl.pallas_call(
        body, name=name, grid=(R // tr,), in_specs=[pl.BlockSpec((n, tr, C), lambda i: (0, i, 0))],
        out_specs=pl.BlockSpec((tr, C), lambda i: (i, 0)), out_shape=jax.ShapeDtypeStruct((R, C), F32),
        compiler_params=_cp(("parallel",)),
    )(x)


def _multi_gather(xs, layers, name):
    nt = len(xs)
    shapes = [x.shape if lay is None else x.shape[1:] for x, lay in zip(xs, layers)]

    def body(*refs):
        x_refs, out_refs = refs[:nt], refs[nt:2 * nt]
        send_sems, recv_sems, local_sems = refs[2 * nt:]
        x_, y_, c_ = lax.axis_index("x"), lax.axis_index("y"), lax.axis_index("c")
        me, sibling = (x_, y_, c_), (x_, y_, 1 - c_)
        chips = [(1 - x_, y_), (x_, 1 - y_), (1 - x_, 1 - y_)]

        def own_block(t):
            return x_refs[t] if layers[t] is None else x_refs[t].at[layers[t]]

        def copy(t, k, block, to, own=False):
            px, py, pc = block
            dst = out_refs[t].at[4 * px + 2 * py + pc]
            return pltpu.make_async_remote_copy(
                src_ref=own_block(t) if own else dst, dst_ref=dst,
                send_sem=send_sems.at[t, k], recv_sem=recv_sems.at[t, k],
                device_id=to, device_id_type=pl.DeviceIdType.MESH)

        mine, first, passed = [], [], []
        for t in range(nt):
            mine.append(pltpu.make_async_copy(own_block(t), out_refs[t].at[4 * x_ + 2 * y_ + c_], local_sems.at[t]))
            mine[-1].start()
            first.append([copy(t, 1 + j, me, (*chip, c_), own=True) for j, chip in enumerate(chips)]
                         + [copy(t, 0, me, sibling, own=True)])
            for cp in first[-1]:
                cp.start()
        for t in range(nt):
            for j, chip in enumerate(chips):
                copy(t, 1 + j, (*chip, c_), me).wait_recv()
                passed.append(copy(t, 4 + j, (*chip, c_), sibling))
                passed[-1].start()
        for t in range(nt):
            copy(t, 0, sibling, me).wait_recv()
            for j, chip in enumerate(chips):
                copy(t, 4 + j, (*chip, 1 - c_), me).wait_recv()
        for cp in [c for f in first for c in f] + passed:
            cp.wait_send()
        for cp in mine:
            cp.wait()

    hbm = pl.BlockSpec(memory_space=pl.ANY)
    return pl.pallas_call(
        body, name=name, out_shape=[jax.ShapeDtypeStruct((N_DEV,) + tuple(s), x.dtype) for s, x in zip(shapes, xs)],
        in_specs=[hbm] * nt, out_specs=[hbm] * nt,
        scratch_shapes=[pltpu.SemaphoreType.DMA((nt, 7)), pltpu.SemaphoreType.DMA((nt, 7)),
                        pltpu.SemaphoreType.DMA((nt,))],
    )(*xs)


def _multi_exchange(sends, name):
    nt = len(sends)

    def body(*refs):
        s_refs, r_refs = refs[:nt], refs[nt:2 * nt]
        send_sems, recv_sems, local_sems = refs[2 * nt:]
        x_, y_, c_ = lax.axis_index("x"), lax.axis_index("y"), lax.axis_index("c")
        me = 4 * x_ + 2 * y_ + c_
        mine, out, inc = [], [], []
        for t in range(nt):
            mine.append(pltpu.make_async_copy(s_refs[t].at[me], r_refs[t].at[me], local_sems.at[t]))
            mine[-1].start()
        for k in (2, 4, 6, 3, 5, 7, 1):
            px, py, pc = x_ ^ ((k >> 2) & 1), y_ ^ ((k >> 1) & 1), c_ ^ (k & 1)
            peer = 4 * px + 2 * py + pc
            for t in range(nt):
                def copy(src, dst):
                    return pltpu.make_async_remote_copy(
                        src_ref=s_refs[t].at[src], dst_ref=r_refs[t].at[dst],
                        send_sem=send_sems.at[t, k - 1], recv_sem=recv_sems.at[t, k - 1],
                        device_id=(px, py, pc), device_id_type=pl.DeviceIdType.MESH)

                out.append(copy(peer, me))
                inc.append(copy(me, peer))
        for cp in out:
            cp.start()
        for cp in inc:
            cp.wait_recv()
        for cp in out:
            cp.wait_send()
        for cp in mine:
            cp.wait()

    hbm = pl.BlockSpec(memory_space=pl.ANY)
    return pl.pallas_call(
        body, name=name, out_shape=[jax.ShapeDtypeStruct(s.shape, s.dtype) for s in sends],
        in_specs=[hbm] * nt, out_specs=[hbm] * nt,
        scratch_shapes=[pltpu.SemaphoreType.DMA((nt, N_DEV - 1)), pltpu.SemaphoreType.DMA((nt, N_DEV - 1)),
                        pltpu.SemaphoreType.DMA((nt,))],
    )(*sends)


_HBM = pl.BlockSpec(memory_space=pltpu.HBM)
_SEM = pl.BlockSpec(memory_space=pltpu.SEMAPHORE)
_PEER_ORDER = (2, 4, 6, 3, 5, 7, 1)


def _split_copies(src_refs, land_refs, send_sems, recv_sems, layers, per_peer):
    x_, y_, c_ = lax.axis_index("x"), lax.axis_index("y"), lax.axis_index("c")
    me = 4 * x_ + 2 * y_ + c_
    copies = []
    for k in _PEER_ORDER:
        px, py, pc = x_ ^ ((k >> 2) & 1), y_ ^ ((k >> 1) & 1), c_ ^ (k & 1)
        peer = 4 * px + 2 * py + pc
        for t in range(len(src_refs)):
            if per_peer:
                src = src_refs[t].at[peer]
            else:
                src = src_refs[t] if layers[t] is None else src_refs[t].at[layers[t]]
            copies.append(pltpu.make_async_remote_copy(
                src_ref=src, dst_ref=land_refs[t].at[me],
                send_sem=send_sems.at[t * (N_DEV - 1) + k - 1], recv_sem=recv_sems.at[t * (N_DEV - 1) + k - 1],
                device_id=(px, py, pc), device_id_type=pl.DeviceIdType.MESH))
    return copies


def _own_copies(src_refs, land_refs, sems, layers, per_peer):
    nt = len(src_refs)
    me = 4 * lax.axis_index("x") + 2 * lax.axis_index("y") + lax.axis_index("c")
    copies = []
    for t in range(nt):
        if per_peer:
            src = src_refs[t].at[me]
        else:
            src = src_refs[t] if layers[t] is None else src_refs[t].at[layers[t]]
        copies.append(pltpu.make_async_copy(src, land_refs[t].at[me], sems.at[nt * (N_DEV - 1) + t]))
    return copies


def _split_start(srcs, layers, per_peer, after, name):
    nt = len(srcs)
    if per_peer:
        land_shapes = [s.shape for s in srcs]
    else:
        land_shapes = [(N_DEV,) + tuple(s.shape if lay is None else s.shape[1:]) for s, lay in zip(srcs, layers)]

    def body(*refs):
        src_refs, land_refs = refs[:nt], refs[nt:2 * nt]
        send_sems, recv_sems = refs[2 * nt + 1], refs[2 * nt + 2]
        token = refs[-1]
        for cp in _split_copies(src_refs, land_refs, send_sems, recv_sems, layers, per_peer):
            cp.start()
        for cp in _own_copies(src_refs, land_refs, send_sems, layers, per_peer):
            cp.start()
        token[...] = jnp.zeros_like(token)

    lands = [pltpu.with_memory_space_constraint(lax.empty(s, x.dtype), pltpu.HBM) for s, x in zip(land_shapes, srcs)]
    srcs = [pltpu.with_memory_space_constraint(x, pltpu.HBM) for x in srcs]
    out = pl.pallas_call(
        body, name=name,
        out_shape=(pltpu.SemaphoreType.DMA((nt * N_DEV,)), pltpu.SemaphoreType.DMA((nt * (N_DEV - 1),)),
                   *[pltpu.HBM(x.shape, x.dtype) for x in srcs], *[pltpu.HBM(s, x.dtype) for s, x in zip(land_shapes, srcs)],
                   jax.ShapeDtypeStruct((8, LANES), F32)),
        in_specs=[_HBM] * (2 * nt) + [pl.BlockSpec(memory_space=pl.ANY)],
        out_specs=(_SEM, _SEM, *([_HBM] * (2 * nt)), pl.BlockSpec(memory_space=pltpu.VMEM)),
        input_output_aliases={i: 2 + i for i in range(2 * nt)},
        compiler_params=pltpu.CompilerParams(has_side_effects=pltpu.SideEffectType.DATAFLOW_SIDE_EFFECTING),
    )(*srcs, *lands, after)
    return out[0], out[1], list(out[2:2 + nt]), list(out[2 + nt:2 + 2 * nt]), out[-1]


def _split_wait(started, layers, per_peer, after, name):
    send_sems, recv_sems, srcs, lands, _ = started
    nt = len(srcs)

    def body(*refs):
        src_refs, land_refs = refs[:nt], refs[nt:2 * nt]
        s_sems, r_sems = refs[2 * nt], refs[2 * nt + 1]
        for cp in _split_copies(src_refs, land_refs, s_sems, r_sems, layers, per_peer):
            cp.wait_send()
            cp.wait_recv()
        for cp in _own_copies(src_refs, land_refs, s_sems, layers, per_peer):
            cp.wait()

    out = pl.pallas_call(
        body, name=name,
        out_shape=tuple(pltpu.HBM(x.shape, x.dtype) for x in srcs + lands),
        in_specs=[_HBM] * (2 * nt) + [_SEM, _SEM, pl.BlockSpec(memory_space=pl.ANY)],
        out_specs=tuple([_HBM] * (2 * nt)),
        input_output_aliases={i: i for i in range(2 * nt)},
        compiler_params=pltpu.CompilerParams(has_side_effects=pltpu.SideEffectType.DATAFLOW_SIDE_EFFECTING),
    )(*srcs, *lands, send_sems, recv_sems, after)
    return list(out[nt:])


def _with_own(land, own):
    me = 4 * lax.axis_index("x") + 2 * lax.axis_index("y") + lax.axis_index("c")
    return lax.dynamic_update_slice_in_dim(land, own[None], me, axis=0)


def _runs(mapping):
    runs, c, n = [], 0, len(mapping)
    while c < n:
        if mapping[c] is None:
            c += 1
            continue
        sid, d, lo = mapping[c][0], mapping[c][1] - c, c
        while c < n and mapping[c] is not None and mapping[c][0] == sid and mapping[c][1] - c == d:
            c += 1
        runs.append((lo, c, sid, d))
    return runs


def _tile_plan(mapping, src_widths):
    runs = _runs(mapping)
    plan = []
    for t in range(len(mapping) // LANES):
        pieces = []
        for lo, hi, sid, d in runs:
            lo_t, hi_t = max(lo, t * LANES), min(hi, (t + 1) * LANES)
            if lo_t >= hi_t:
                continue
            a = ((lo_t + d) // LANES) * LANES
            win = min(2 * LANES, src_widths[sid] - a)
            shift = t * LANES + d - a
            pieces.append((sid, a, win, shift, lo_t - t * LANES, hi_t - t * LANES))
        plan.append(pieces)
    return plan


def _reblock(srcs, src_views, outs, out_views, name):
    R = srcs[0].shape[-2]
    tr = min(256, R)
    widths = {sid: srcs[ai].shape[-1] for sid, (ai, _) in src_views.items()}
    plans = [(ai, li, _tile_plan(mapping, widths)) for ai, li, mapping in out_views]
    ns = len(srcs)

    def body(*refs):
        s_refs, o_refs = refs[:ns], refs[ns:]
        cache = {}

        def shift_matrix(win, shift, lo, hi):
            key = (win, shift, lo, hi)
            if key not in cache:
                r = lax.broadcasted_iota(jnp.int32, (win, LANES), 0)
                c = lax.broadcasted_iota(jnp.int32, (win, LANES), 1)
                hit = jnp.logical_and(r - c == shift, jnp.logical_and(c >= lo, c < hi))
                cache[key] = jnp.where(hit, 1.0, 0.0).astype(BF16)
            return cache[key]

        for ai, li, plan in plans:
            for t, pieces in enumerate(plan):
                acc = None
                for sid, a, win, shift, lo, hi in pieces:
                    sa, sl = src_views[sid]
                    src = s_refs[sa][:, a:a + win] if sl is None else s_refs[sa][sl, :, a:a + win]
                    part = jnp.dot(src, shift_matrix(win, shift, lo, hi), preferred_element_type=F32)
                    acc = part if acc is None else acc + part
                val = jnp.zeros((tr, LANES), BF16) if acc is None else acc.astype(BF16)
                if li is None:
                    o_refs[ai][:, t * LANES:(t + 1) * LANES] = val
                else:
                    o_refs[ai][li, :, t * LANES:(t + 1) * LANES] = val

    def spec(shape):
        if len(shape) == 2:
            return pl.BlockSpec((tr, shape[1]), lambda i: (i, 0))
        return pl.BlockSpec((shape[0], tr, shape[2]), lambda i: (0, i, 0))

    return pl.pallas_call(
        body, name=name, grid=(R // tr,), in_specs=[spec(s.shape) for s in srcs],
        out_specs=[spec(s) for s in outs], out_shape=[jax.ShapeDtypeStruct(s, BF16) for s in outs],
        compiler_params=_cp(("parallel",)),
    )(*srcs)


SHARDED = ("w_in", "w_gate_up", "w_proj_attn", "w_proj_pool", "w_proj_conv", "w_out", "w_down")
WEIGHT_ORDER = ("attn_norm", "w_in", "b_forget", "b_gate", "w_proj_attn", "pool_w", "pool_scale", "w_proj_pool",
                "conv_w", "w_proj_conv", "w_out", "ffn_norm", "w_gate_up", "w_down", "final_norm")
IN_SHARD, IN_SHARD_PAD = IN_COLS // N_DEV, 896
GU_SHARD, GU_SHARD_PAD = 2 * FFN_HIDDEN // N_DEV, 768


def _w_in_col(c):
    if c < GATE_W:
        return c + 3592
    if c < OFF_U:
        return c - OFF_Q
    return c - OFF_U + 1544


def _w_in_full(gathered, name):
    main = [divmod(_w_in_col(c), IN_SHARD) for c in range(MAIN_COLS)]
    fcols = [divmod(1536 + c, IN_SHARD) if c < N_HEADS else None for c in range(LANES)]
    R = gathered.shape[1]
    return _reblock([gathered], {i: (0, i) for i in range(N_DEV)}, [(R, MAIN_COLS), (R, LANES)],
                    [(0, None, main), (1, None, fcols)], name)


def _w_in_slabs(dmain, dwf, name):
    inv = {_w_in_col(c): ("m", c) for c in range(MAIN_COLS)}
    inv.update({1536 + c: ("f", c) for c in range(N_HEADS)})
    views = []
    for i in range(N_DEV):
        mapping = [inv[IN_SHARD * i + j] if j < IN_SHARD else None for j in range(IN_SHARD_PAD)]
        views.append((0, i, mapping))
    R = dmain.shape[0]
    return _reblock([dmain, dwf], {"m": (0, None), "f": (1, None)}, [(N_DEV, R, IN_SHARD_PAD)], views, name)[0]


def _w_gu_full(gathered, name):
    mapping = [divmod(_gu_col(c), GU_SHARD) for c in range(2 * FFN_HIDDEN)]
    R = gathered.shape[1]
    return _reblock([gathered], {i: (0, i) for i in range(N_DEV)}, [(R, 2 * FFN_HIDDEN)], [(0, None, mapping)], name)[0]


def _w_gu_slabs(dw, name):
    inv = {_gu_col(c): c for c in range(2 * FFN_HIDDEN)}
    views = [(0, i, [("w", inv[GU_SHARD * i + j]) if j < GU_SHARD else None for j in range(GU_SHARD_PAD)])
             for i in range(N_DEV)]
    R = dw.shape[0]
    return _reblock([dw], {"w": (0, None)}, [(N_DEV, R, GU_SHARD_PAD)], views, name)[0]


def _layer_fwd(x, W, n_seq, l):
    T = x.shape[0]
    sfx = f"_l{l}"
    h1 = _rms_fwd(x, W["attn_norm"], "rms1" + sfx)
    proj = _matmul(h1, W["w_main"], mode="nn", out_dtype=BF16, name="proj_main" + sfx)
    f = _matmul(h1, W["w_f"], mode="nn", out_dtype=F32, name="proj_f" + sfx)
    qa, ka, va = _fox_prep(f, W["b_forget"], proj, n_seq, "fox_prep" + sfx)
    oa, oa32, lse = _attn_fwd2(qa, ka, va, n_seq, "attn_fwd" + sfx)
    if "late" in W:
        W.update(W.pop("late")(oa))
    ob = _pool_fwd(proj, W["pool_w"], W["pool_scale"], n_seq, "pool_fwd" + sfx)
    oc = _conv_fwd(proj, W["conv_w"], n_seq, "conv_fwd" + sfx)
    mixed = _mix_fwd(oa, ob, oc, W["w_proj_attn"], W["w_proj_pool"], W["w_proj_conv"], proj, W["b_gate"],
                     "mix_fwd" + sfx)
    x2 = _matmul(mixed, W["w_out"], mode="nn", out_dtype=F32, name="out_proj" + sfx, residual=x)
    h2 = _rms_fwd(x2, W["ffn_norm"], "rms2" + sfx)
    ab, s = _gate_up_swiglu(h2, W["w_gate_up"], "gate_up" + sfx)
    x3 = _matmul(s, W["w_down"], mode="nn", out_dtype=F32, name="down" + sfx, tm=1024, tn=1024, tk=1408,
                 residual=x2)
    saved = dict(x=x, h1=h1, proj=proj, f=f, qa=qa, ka=ka, oa=oa, oa32=oa32, lse=lse, ob=ob, oc=oc, mixed=mixed, x2=x2,
                 h2=h2, ab=ab, s=s)
    return x3, saved


def _layer_bwd(dx3, dx3b, W, sv, n_seq, l, stage=None):
    T = dx3.shape[0]
    sfx = f"_l{l}"
    G = {}
    stage = stage or (lambda l, group, G, W: W)
    dab = _swiglu_bwd_fused(dx3b, W["w_down"], sv["ab"], "d_ab" + sfx)
    G["w_down"] = _matmul(sv["s"], dx3b, mode="tn", out_dtype=BF16, name="dw_down" + sfx, tm=256, tn=1024)
    dh2 = _matmul(dab, W["w_gate_up"], mode="nt", out_dtype=BF16, name="d_h2" + sfx, tm=1024, tn=1024, tk=1408)
    G["w_gate_up"] = _matmul(sv["h2"], dab, mode="tn", out_dtype=BF16, name="dw_gate_up" + sfx, tm=1024)
    W = stage(l, "ffn", G, W)
    dx2, dx2b, G["ffn_norm"] = _rms_bwd(sv["x2"], W["ffn_norm"], dh2, dx3, "rms2_bwd" + sfx)
    dmixed = _matmul(dx2b, W["w_out"], mode="nt", out_dtype=BF16, name="d_mixed" + sfx)
    G["w_out"] = _matmul(sv["mixed"], dx2b, mode="tn", out_dtype=BF16, name="dw_out" + sfx, tm=1024)
    dya, dyb, dyc, dg, G["b_gate"] = _mix_bwd(sv["oa"], sv["ob"], sv["oc"], W["w_proj_attn"], W["w_proj_pool"],
                                              W["w_proj_conv"], sv["proj"], W["b_gate"], dmixed, "mix_bwd" + sfx)
    douts = {}
    for br, dy, o in (("attn", dya, sv["oa"]), ("pool", dyb, sv["ob"]), ("conv", dyc, sv["oc"])):
        douts[br] = _matmul(dy, W["w_proj_" + br], mode="nt", out_dtype=BF16, name=f"d_{br}_out" + sfx)
        G["w_proj_" + br] = _matmul(o, dy, mode="tn", out_dtype=BF16, name=f"dw_proj_{br}" + sfx, tm=512)
    W = stage(l, "mix", G, W)
    dcv, dcb, dcc, G["conv_w"] = _conv_bwd(sv["proj"], douts["conv"], W["conv_w"], n_seq, "conv_bwd" + sfx)
    du, G["pool_w"], G["pool_scale"] = _pool_bwd(sv["proj"], douts["pool"], W["pool_w"], W["pool_scale"], n_seq,
                                                 "pool_bwd" + sfx)
    delta = _attn_delta(douts["attn"], sv["oa32"], "attn_delta" + sfx)
    dq, dk, dv, dFk = _attn_bwd(sv["qa"], sv["ka"], sv["proj"], douts["attn"], sv["lse"], delta, n_seq,
                                "attn_bwd" + sfx)
    dF = jnp.pad(dFk.reshape(N_HEADS, T).T, ((0, 0), (0, LANES - N_HEADS)))
    df, G["b_forget"] = _fox_cumsum_bwd(sv["f"], W["b_forget"], dF, n_seq, "fox_cumsum_bwd" + sfx)
    dproj = jnp.concatenate([dg, dq, dk, dv, du, dcv, dcb, dcc], axis=1)
    G["w_main"] = _matmul(sv["h1"], dproj, mode="tn", out_dtype=BF16, name="dw_main" + sfx, tm=1024)
    G["w_f"] = _matmul(sv["h1"], df, mode="tn", out_dtype=BF16, name="dw_f" + sfx, tm=1024)
    W = stage(l, "w_in", G, W)
    dh1 = _matmul(df, W["w_f"], mode="nt", out_dtype=F32, name="d_h1_f" + sfx)
    dh1 = _matmul(dproj, W["w_main"], mode="nt", out_dtype=F32, name="d_h1_main" + sfx, tm=1024, tn=1024, tk=1664,
                  residual=dh1)
    dx, dxb, G["attn_norm"] = _rms_bwd(sv["x"], W["attn_norm"], dh1, dx2, "rms1_bwd" + sfx)
    return dx, dxb, G


def _replicated_operands(rep, l):
    W = {}
    W["attn_norm"], W["ffn_norm"] = rep["attn_norm"][l], rep["ffn_norm"][l]
    W["b_forget"] = jnp.pad(rep["b_forget"][l].reshape(1, N_HEADS), ((0, 0), (0, LANES - N_HEADS)))
    W["b_gate"] = rep["b_gate"][l].reshape(1, GATE_W)
    W["pool_w"] = rep["pool_w"][l].astype(BF16)
    W["pool_scale"] = rep["pool_scale"][l].reshape(1, BRANCH_W)
    return W


def _local_step(x, target, get_W, final_norm, stage=None):
    n_seq, S, Dm = x.shape
    T = n_seq * S
    xt = x.reshape(T, Dm)
    saved, Ws = [], []
    for l in range(DEPTH):
        Ws.append(get_W(l, xt))
        xt, sv = _layer_fwd(xt, Ws[l], n_seq, l)
        saved.append(sv)
    loss, dx, dxb, g_final = _loss_head(xt, final_norm, target.reshape(T, Dm), "loss_head")
    grads = [None] * DEPTH
    for l in reversed(range(DEPTH)):
        dx, dxb, grads[l] = _layer_bwd(dx, dxb, Ws[l], saved[l], n_seq, l, stage)
    return loss, dx.reshape(n_seq, S, Dm), grads, g_final


def _padded_shards(weights):
    sh = {n: weights[n].astype(BF16) for n in SHARDED}
    sh["w_in"] = jnp.pad(sh["w_in"], ((0, 0), (0, 0), (0, IN_SHARD_PAD - IN_SHARD)))
    sh["w_gate_up"] = jnp.pad(sh["w_gate_up"], ((0, 0), (0, 0), (0, GU_SHARD_PAD - GU_SHARD)))
    return sh


def _full_operands(g, l):
    W = {}
    if "w_in" in g:
        W["w_main"], W["w_f"] = _w_in_full(g["w_in"], f"w_in_full_l{l}")
    if "w_gate_up" in g:
        W["w_gate_up"] = _w_gu_full(g["w_gate_up"], f"w_gate_up_full_l{l}")
    for n in ("w_proj_attn", "w_proj_pool", "w_proj_conv"):
        if n in g:
            W[n] = jnp.transpose(g[n], (1, 0, 2)).reshape(BRANCH_W, D_MODEL)
    if "w_out" in g:
        W["w_out"] = g["w_out"].reshape(D_MODEL, D_MODEL)
    if "w_down" in g:
        W["w_down"] = g["w_down"].reshape(FFN_HIDDEN, D_MODEL)
    return W


GRAD_GROUPS = {"ffn": ("w_down", "w_gate_up"),
               "mix": ("w_out", "w_proj_attn", "w_proj_pool", "w_proj_conv"),
               "w_in": ("w_in",)}


def _grad_slabs(G, n, l):
    if n == "w_in":
        return _w_in_slabs(G["w_main"], G["w_f"], f"w_in_slabs_l{l}")
    if n == "w_gate_up":
        return _w_gu_slabs(G["w_gate_up"], f"w_gate_up_slabs_l{l}")
    if n == "w_out":
        return G["w_out"].reshape(N_DEV, D_MODEL // N_DEV, D_MODEL)
    if n == "w_down":
        return G["w_down"].reshape(N_DEV, FFN_HIDDEN // N_DEV, D_MODEL)
    return jnp.transpose(G[n].reshape(BRANCH_W, N_DEV, D_MODEL // N_DEV), (1, 0, 2))


def _sum_layer_grads(recv, l):
    out = {n: _sum_slabs(r, f"sum_{n}_l{l}") for n, r in recv.items()}
    if "w_in" in out:
        out["w_in"] = out["w_in"][:, :IN_SHARD]
    if "w_gate_up" in out:
        out["w_gate_up"] = out["w_gate_up"][:, :GU_SHARD]
    return out


def _sum_small(xs, name):
    def body(*refs):
        for x_ref, o_ref in zip(refs[:len(xs)], refs[len(xs):]):
            acc = x_ref[0]
            for j in range(1, N_DEV):
                acc = acc + x_ref[j]
            o_ref[...] = acc

    return pl.pallas_call(
        body, name=name, out_shape=[jax.ShapeDtypeStruct(x.shape[1:], F32) for x in xs],
        compiler_params=_cp(),
    )(*xs)


def _as_2d(a):
    if a.ndim == 1:
        return a.reshape(1, -1)
    return a.reshape(-1, a.shape[-1])


def kernel(x, attn_norm, w_in, b_forget, b_gate, w_proj_attn, pool_w, pool_scale, w_proj_pool, conv_w, w_proj_conv, w_out, ffn_norm, w_gate_up, w_down, final_norm, loss_target, m_attn_norm, m_w_in, m_b_forget, m_b_gate, m_w_proj_attn, m_pool_w, m_pool_scale, m_w_proj_pool, m_conv_w, m_w_proj_conv, m_w_out, m_ffn_norm, m_w_gate_up, m_w_down, m_final_norm, v_attn_norm, v_w_in, v_b_forget, v_b_gate, v_w_proj_attn, v_pool_w, v_pool_scale, v_w_proj_pool, v_conv_w, v_w_proj_conv, v_w_out, v_ffn_norm, v_w_gate_up, v_w_down, v_final_norm):
    weights = dict(attn_norm=attn_norm, w_in=w_in, b_forget=b_forget, b_gate=b_gate, w_proj_attn=w_proj_attn,
                   pool_w=pool_w, pool_scale=pool_scale, w_proj_pool=w_proj_pool, conv_w=conv_w,
                   w_proj_conv=w_proj_conv, w_out=w_out, ffn_norm=ffn_norm, w_gate_up=w_gate_up, w_down=w_down,
                   final_norm=final_norm)
    moments_m = dict(attn_norm=m_attn_norm, w_in=m_w_in, b_forget=m_b_forget, b_gate=m_b_gate,
                     w_proj_attn=m_w_proj_attn, pool_w=m_pool_w, pool_scale=m_pool_scale, w_proj_pool=m_w_proj_pool,
                     conv_w=m_conv_w, w_proj_conv=m_w_proj_conv, w_out=m_w_out, ffn_norm=m_ffn_norm,
                     w_gate_up=m_w_gate_up, w_down=m_w_down, final_norm=m_final_norm)
    moments_v = dict(attn_norm=v_attn_norm, w_in=v_w_in, b_forget=v_b_forget, b_gate=v_b_gate,
                     w_proj_attn=v_w_proj_attn, pool_w=v_pool_w, pool_scale=v_pool_scale, w_proj_pool=v_w_proj_pool,
                     conv_w=v_conv_w, w_proj_conv=v_w_proj_conv, w_out=v_w_out, ffn_norm=v_ffn_norm,
                     w_gate_up=v_w_gate_up, w_down=v_w_down, final_norm=v_final_norm)

    sh = _padded_shards(weights)
    names = list(SHARDED)
    rest = [n for n in names if n != "w_in"]
    me = 4 * lax.axis_index("x") + 2 * lax.axis_index("y") + lax.axis_index("c")
    w_in0, conv_all = _multi_gather([sh["w_in"], conv_w], [0, None], "gather_w_in_l0")
    started, after = {}, w_in0
    for l in range(DEPTH):
        for group, gnames in (("w_in", ["w_in"]), ("rest", rest)):
            if (l, group) != (0, "w_in"):
                started[l, group] = _split_start([sh[n] for n in gnames], [l] * len(gnames), False, after,
                                                 f"gather_start_{group}_l{l}")
                after = started[l, group][4]
    last_token = after

    def get_W(l, xt):
        if l == 0:
            w_in = w_in0
        else:
            w_in = _split_wait(started[l, "w_in"], [l], False, xt, f"gather_wait_w_in_l{l}")[0]
        W = _full_operands({"w_in": w_in}, l)

        def late(after):
            lands = _split_wait(started[l, "rest"], [l] * len(rest), False, after, f"gather_wait_rest_l{l}")
            return _full_operands(dict(zip(rest, lands)), l)

        W["late"] = late
        W.update(_replicated_operands(weights, l))
        W["conv_w"] = jnp.transpose(conv_all[:, l], (1, 0, 2)).reshape(CONV_K, BRANCH_W)
        if l == 0:
            W["attn_norm"] = W["attn_norm"] + last_token[0, 0]
        return W

    exchanges = []

    def stage(l, group, G, W):
        gnames = GRAD_GROUPS[group]
        slabs = [_grad_slabs(G, n, l) for n in gnames]
        started = _split_start(slabs, None, True, slabs[0], f"exchange_start_{group}_l{l}")
        exchanges.append((l, group, gnames, slabs, started))
        tie = {"ffn": "ffn_norm", "mix": "conv_w", "w_in": "w_f"}[group]
        W = dict(W)
        W[tie] = W[tie] + started[4][0, 0].astype(W[tie].dtype)
        return W

    loss_part, grad_x, grads, g_final = _local_step(x, loss_target, get_W, final_norm, stage)
    after = grad_x
    for l, group, gnames, slabs, started in exchanges:
        lands = _split_wait(started, None, True, after, f"exchange_wait_{group}_l{l}")
        grads[l].update(_sum_layer_grads(dict(zip(gnames, lands)), l))
    gw = {n: jnp.stack([grads[l][n] for l in range(DEPTH)]) for n in SHARDED}

    small = ("attn_norm", "b_forget", "b_gate", "pool_w", "pool_scale", "ffn_norm", "conv_w")
    parts = [jnp.stack([grads[l][n] for l in range(DEPTH)]) for n in small] + [g_final, loss_part]
    gathered = _multi_gather(parts, [None] * len(parts), "gather_small_grads")
    summed = _sum_small(gathered, "sum_small_grads")
    for n, s in zip(small, summed):
        gw[n] = s
    gw["attn_norm"], gw["ffn_norm"] = gw["attn_norm"][:, 0], gw["ffn_norm"][:, 0]
    gw["b_forget"] = gw["b_forget"][:, 0, :N_HEADS]
    gw["b_gate"], gw["pool_scale"] = gw["b_gate"][:, 0], gw["pool_scale"][:, 0]
    gw["conv_w"] = lax.dynamic_slice_in_dim(gw["conv_w"], me * (BRANCH_W // N_DEV), BRANCH_W // N_DEV, axis=2)
    gw["final_norm"] = summed[-2][0]
    loss = summed[-1][0, 0]

    deltas, new_m, new_v = {}, {}, {}
    for n in WEIGHT_ORDER:
        shape = weights[n].shape
        d, nm, nv = _adamw(_as_2d(weights[n]), _as_2d(gw[n]), _as_2d(moments_m[n]), _as_2d(moments_v[n]),
                           "adamw_" + n)
        deltas[n], new_m[n], new_v[n] = d.reshape(shape), nm.reshape(shape), nv.reshape(shape)

    return (loss, grad_x, *[gw[n] for n in WEIGHT_ORDER], *[deltas[n] for n in WEIGHT_ORDER],
            *[new_m[n] for n in WEIGHT_ORDER], *[new_v[n] for n in WEIGHT_ORDER])
```

```python
import functools

import numpy as np
import jax
import jax.numpy as jnp
from jax import lax
from jax.experimental import pallas as pl
from jax.experimental.pallas import tpu as pltpu

F32 = jnp.float32
BF16 = jnp.bfloat16

N_DEV = 8
D_MODEL = 1024
DEPTH = 2
N_HEADS = 8
HEAD_DIM = 64
BRANCH_W = 512
POOL_WINDOWS = (2, 4, 8, 16)
POOL_GD = 128
CONV_K = 3
FFN_HIDDEN = 2816
GATE_W = 3 * D_MODEL
IN_COLS = 6664
MAIN_COLS = GATE_W + 7 * BRANCH_W
RMS_EPS = 1e-6
NEG_INF = -1e30

ADAM_LR = 0.001
ADAM_B1 = 0.9
ADAM_B2 = 0.999
ADAM_EPS = 1e-08
ADAM_WD = 0.01
ADAM_STEP = 10

LANES = 128
VMEM_LIMIT = 56 * 1024 * 1024
ATT_BLK = 256
CUM_BLK = 256

OFF_G, OFF_Q, OFF_K, OFF_V, OFF_U, OFF_CV, OFF_CB, OFF_CC = (
    0, 3072, 3584, 4096, 4608, 5120, 5632, 6144)


def _cp(sem=None):
    return pltpu.CompilerParams(dimension_semantics=sem, vmem_limit_bytes=VMEM_LIMIT)


def _sigmoid(z):
    return 1.0 / (1.0 + jnp.exp(-z))


def _matmul(a, b, *, mode, out_dtype, name, tm=2048, tn=512, tk=None, residual=None, rms_g=None):
    if mode == "nn":
        (M, K), N = a.shape, b.shape[1]
    elif mode == "nt":
        (M, K), N = a.shape, b.shape[0]
    else:
        (K, M), N = a.shape, b.shape[1]
    tm, tn, tk = min(tm, M), min(tn, N), K if tk is None else min(tk, K)
    assert M % tm == 0 and N % tn == 0 and K % tk == 0, (name, M, N, K, tm, tn, tk)
    nk = K // tk
    if mode == "nn":
        a_spec = pl.BlockSpec((tm, tk), lambda i, j, k: (i, k))
        b_spec = pl.BlockSpec((tk, tn), lambda i, j, k: (k, j))
        dims = (((1,), (0,)), ((), ()))
    elif mode == "nt":
        a_spec = pl.BlockSpec((tm, tk), lambda i, j, k: (i, k))
        b_spec = pl.BlockSpec((tn, tk), lambda i, j, k: (j, k))
        dims = (((1,), (1,)), ((), ()))
    else:
        a_spec = pl.BlockSpec((tk, tm), lambda i, j, k: (k, i))
        b_spec = pl.BlockSpec((tk, tn), lambda i, j, k: (k, j))
        dims = (((0,), (0,)), ((), ()))
    o_spec = pl.BlockSpec((tm, tn), lambda i, j, k: (i, j))
    has_res, has_norm = residual is not None, rms_g is not None
    assert not has_norm or tn == N, (name, tn, N)

    def body(*refs):
        a_ref, b_ref = refs[:2]
        r_ref = refs[2] if has_res else None
        g_ref = refs[2 + has_res] if has_norm else None
        o_ref = refs[2 + has_res + has_norm]
        h_ref = refs[3 + has_res + has_norm] if has_norm else None

        def finish(acc):
            if has_res:
                acc = acc + r_ref[...].astype(F32)
            o_ref[...] = acc.astype(out_dtype)
            if has_norm:
                r = lax.rsqrt(jnp.mean(acc * acc, axis=-1, keepdims=True) + RMS_EPS)
                h_ref[...] = ((acc * r) * g_ref[...]).astype(BF16)

        prod = lax.dot_general(a_ref[...], b_ref[...], dims, preferred_element_type=F32)
        if nk == 1:
            finish(prod)
            return
        acc_ref = refs[-1]
        k = pl.program_id(2)

        @pl.when(k == 0)
        def _():
            acc_ref[...] = prod

        @pl.when(jnp.logical_and(k > 0, k < nk - 1))
        def _():
            acc_ref[...] += prod

        @pl.when(k == nk - 1)
        def _():
            finish(acc_ref[...] + prod)

    in_specs = [a_spec, b_spec] + ([o_spec] if has_res else [])
    args = (a, b) + ((residual,) if has_res else ())
    out_specs, out_shape = o_spec, jax.ShapeDtypeStruct((M, N), out_dtype)
    if has_norm:
        in_specs.append(pl.BlockSpec((1, N), lambda i, j, k: (0, 0)))
        args += (rms_g.reshape(1, N),)
        out_specs, out_shape = [o_spec, o_spec], [out_shape, jax.ShapeDtypeStruct((M, N), BF16)]
    return pl.pallas_call(
        body, name=name, grid=(M // tm, N // tn, nk), in_specs=in_specs, out_specs=out_specs,
        out_shape=out_shape,
        scratch_shapes=[pltpu.VMEM((tm, tn), F32)] if nk > 1 else [],
        compiler_params=_cp(("parallel", "parallel", "arbitrary")),
    )(*args)


def _rms_fwd(x, g, name):
    T, Dm = x.shape
    tm = min(512, T)

    def body(x_ref, g_ref, h_ref):
        xf = x_ref[...]
        r = lax.rsqrt(jnp.mean(xf * xf, axis=-1, keepdims=True) + RMS_EPS)
        h_ref[...] = ((xf * r) * g_ref[...]).astype(BF16)

    return pl.pallas_call(
        body, name=name, grid=(T // tm,),
        in_specs=[pl.BlockSpec((tm, Dm), lambda i: (i, 0)), pl.BlockSpec((1, Dm), lambda i: (0, 0))],
        out_specs=pl.BlockSpec((tm, Dm), lambda i: (i, 0)),
        out_shape=jax.ShapeDtypeStruct((T, Dm), BF16),
        compiler_params=_cp(("parallel",)),
    )(x, g.reshape(1, Dm))


def _rms_bwd(x, g, dh, dres, name):
    T, Dm = x.shape
    tm = min(512, T)

    def body(x_ref, g_ref, dh_ref, dres_ref, dx_ref, dxb_ref, dg_ref):
        i = pl.program_id(0)
        xf = x_ref[...]
        r = lax.rsqrt(jnp.mean(xf * xf, axis=-1, keepdims=True) + RMS_EPS)
        xn = xf * r
        dhf = dh_ref[...].astype(F32)
        dxn = dhf * g_ref[...]
        c = jnp.mean(dxn * xn, axis=-1, keepdims=True)
        dx = dres_ref[...] + r * (dxn - xn * c)
        dx_ref[...] = dx
        dxb_ref[...] = dx.astype(BF16)
        part = jnp.sum(dhf * xn, axis=0, keepdims=True)

        @pl.when(i == 0)
        def _():
            dg_ref[...] = part

        @pl.when(i > 0)
        def _():
            dg_ref[...] += part

    row = pl.BlockSpec((tm, Dm), lambda i: (i, 0))
    vec = pl.BlockSpec((1, Dm), lambda i: (0, 0))
    return pl.pallas_call(
        body, name=name, grid=(T // tm,), in_specs=[row, vec, row, row], out_specs=[row, row, vec],
        out_shape=[jax.ShapeDtypeStruct((T, Dm), F32), jax.ShapeDtypeStruct((T, Dm), BF16),
                   jax.ShapeDtypeStruct((1, Dm), F32)],
        compiler_params=_cp(("arbitrary",)),
    )(x, g.reshape(1, Dm), dh, dres)


def _loss_head(x, g, target, name):
    T, Dm = x.shape
    tm = min(512, T)

    def body(x_ref, g_ref, t_ref, loss_ref, dx_ref, dxb_ref, dg_ref):
        i = pl.program_id(0)
        xf = x_ref[...]
        gv = g_ref[...]
        r = lax.rsqrt(jnp.mean(xf * xf, axis=-1, keepdims=True) + RMS_EPS)
        xn = xf * r
        diff = xn * gv - t_ref[...]
        per_tok = jnp.mean(diff * diff, axis=-1, keepdims=True)
        lpart = 0.5 * jnp.sum(per_tok, axis=0, keepdims=True) + jnp.zeros((1, LANES), F32)
        dy = diff * (1.0 / Dm)
        dxn = dy * gv
        c = jnp.mean(dxn * xn, axis=-1, keepdims=True)
        dx = r * (dxn - xn * c)
        dx_ref[...] = dx
        dxb_ref[...] = dx.astype(BF16)
        part = jnp.sum(dy * xn, axis=0, keepdims=True)

        @pl.when(i == 0)
        def _():
            dg_ref[...] = part
            loss_ref[...] = lpart

        @pl.when(i > 0)
        def _():
            dg_ref[...] += part
            loss_ref[...] += lpart

    row = pl.BlockSpec((tm, Dm), lambda i: (i, 0))
    vec = pl.BlockSpec((1, Dm), lambda i: (0, 0))
    lsp = pl.BlockSpec((1, LANES), lambda i: (0, 0))
    return pl.pallas_call(
        body, name=name, grid=(T // tm,), in_specs=[row, vec, row], out_specs=[lsp, row, row, vec],
        out_shape=[jax.ShapeDtypeStruct((1, LANES), F32), jax.ShapeDtypeStruct((T, Dm), F32),
                   jax.ShapeDtypeStruct((T, Dm), BF16), jax.ShapeDtypeStruct((1, Dm), F32)],
        compiler_params=_cp(("arbitrary",)),
    )(x, g.reshape(1, Dm), target)


def _split_bf16(v):
    hi = v.astype(BF16)
    r1 = v - hi.astype(F32)
    mid = r1.astype(BF16)
    lo = (r1 - mid.astype(F32)).astype(BF16)
    return hi, mid, lo


def _tri_dot(tri, v):
    hi, mid, lo = _split_bf16(v)
    dot = functools.partial(jnp.dot, preferred_element_type=F32)
    return dot(tri, hi) + dot(tri, mid) + dot(tri, lo)


def _log_sigmoid(z):
    return jnp.minimum(z, 0.0) - jnp.log(1.0 + jnp.exp(-jnp.abs(z)))


def _fox_cumsum_fwd(f, bf, n_seq, name):
    T = f.shape[0]
    S = T // n_seq
    c = min(CUM_BLK, S)

    def body(f_ref, b_ref, out_ref):
        ri = lax.broadcasted_iota(jnp.int32, (c, c), 0)
        ci = lax.broadcasted_iota(jnp.int32, (c, c), 1)
        tri = (ri >= ci).astype(BF16)
        carry = jnp.zeros((1, LANES), F32)
        for j in range(S // c):
            lf = _log_sigmoid(f_ref[j * c:(j + 1) * c, :] + b_ref[...])
            out_ref[j * c:(j + 1) * c, :] = _tri_dot(tri, lf) + carry
            carry = carry + jnp.sum(lf, axis=0, keepdims=True)

    blk = pl.BlockSpec((S, LANES), lambda b: (b, 0))
    return pl.pallas_call(
        body, name=name, grid=(n_seq,), in_specs=[blk, pl.BlockSpec((1, LANES), lambda b: (0, 0))],
        out_specs=blk, out_shape=jax.ShapeDtypeStruct((T, LANES), F32),
        compiler_params=_cp(("parallel",)),
    )(f, bf)


def _fox_cumsum_bwd(f, bf, dF, n_seq, name):
    T = f.shape[0]
    S = T // n_seq
    c = min(CUM_BLK, S)

    def body(f_ref, b_ref, dF_ref, df_ref, db_ref):
        b = pl.program_id(0)
        ri = lax.broadcasted_iota(jnp.int32, (c, c), 0)
        ci = lax.broadcasted_iota(jnp.int32, (c, c), 1)
        tri = (ri <= ci).astype(BF16)
        carry = jnp.zeros((1, LANES), F32)
        dbp = jnp.zeros((1, LANES), F32)
        for j in reversed(range(S // c)):
            dFc = dF_ref[j * c:(j + 1) * c, :]
            dlf = _tri_dot(tri, dFc) + carry
            carry = carry + jnp.sum(dFc, axis=0, keepdims=True)
            z = f_ref[j * c:(j + 1) * c, :] + b_ref[...]
            dz = dlf * _sigmoid(-z)
            df_ref[j * c:(j + 1) * c, :] = dz.astype(BF16)
            dbp = dbp + jnp.sum(dz, axis=0, keepdims=True)

        @pl.when(b == 0)
        def _():
            db_ref[...] = dbp

        @pl.when(b > 0)
        def _():
            db_ref[...] += dbp

    blk = pl.BlockSpec((S, LANES), lambda b: (b, 0))
    vec = pl.BlockSpec((1, LANES), lambda b: (0, 0))
    return pl.pallas_call(
        body, name=name, grid=(n_seq,), in_specs=[blk, vec, blk], out_specs=[blk, vec],
        out_shape=[jax.ShapeDtypeStruct((T, LANES), BF16), jax.ShapeDtypeStruct((1, LANES), F32)],
        compiler_params=_cp(("arbitrary",)),
    )(f, bf, dF)


def _pair_masks():
    lane = lax.broadcasted_iota(jnp.int32, (1, LANES), 1)
    lo = lane < HEAD_DIM
    return lo, jnp.logical_not(lo)


def _attn_logits(q, k, fq, fk, sel, mask, scale):
    qm = jnp.where(sel, q, jnp.zeros_like(q))
    s = lax.dot_general(qm, k, (((1,), (1,)), ((), ())), preferred_element_type=F32) * scale
    s = s + fq - fk
    return jnp.where(mask, s, NEG_INF)


def _causal_mask(qi, ki, blk):
    row = qi * blk + lax.broadcasted_iota(jnp.int32, (blk, blk), 0)
    col = ki * blk + lax.broadcasted_iota(jnp.int32, (blk, blk), 1)
    return col <= row


def _attn_fwd(proj, Fq, Fk, n_seq, name):
    T = proj.shape[0]
    S = T // n_seq
    blk = min(ATT_BLK, S)
    nb = S // blk
    scale = HEAD_DIM ** -0.5
    qc, kc, vc = OFF_Q // LANES, OFF_K // LANES, OFF_V // LANES

    def body(q_ref, k_ref, v_ref, fq_ref, fk_ref, o_ref, o32_ref, lse_ref, m_s, l_s, acc_s):
        qi, ki = pl.program_id(2), pl.program_id(3)

        @pl.when(ki == 0)
        def _():
            m_s[...] = jnp.full_like(m_s, NEG_INF)
            l_s[...] = jnp.zeros_like(l_s)
            acc_s[...] = jnp.zeros_like(acc_s)

        @pl.when(ki <= qi)
        def _():
            q, k, v = q_ref[...], k_ref[...], v_ref[...]
            mask = _causal_mask(qi, ki, blk)
            for hh, sel in enumerate(_pair_masks()):
                s = _attn_logits(q, k, fq_ref[hh], fk_ref[hh], sel, mask, scale)
                m_prev = m_s[hh]
                m_new = jnp.maximum(m_prev, jnp.max(s, axis=-1, keepdims=True))
                alpha = jnp.exp(m_prev - m_new)
                p = jnp.exp(s - m_new)
                l_s[hh] = alpha * l_s[hh] + jnp.sum(p, axis=-1, keepdims=True)
                p_hi = p.astype(BF16)
                p_lo = (p - p_hi.astype(F32)).astype(BF16)
                pv = jnp.dot(p_hi, v, preferred_element_type=F32) + jnp.dot(p_lo, v, preferred_element_type=F32)
                acc_s[hh] = alpha * acc_s[hh] + pv
                m_s[hh] = m_new

        @pl.when(ki == qi)
        def _():
            lo, _ = _pair_masks()
            o = jnp.where(lo, acc_s[0] / l_s[0], acc_s[1] / l_s[1])
            o_ref[...] = o.astype(BF16)
            o32_ref[...] = o
            lse_ref[0] = m_s[0] + jnp.log(l_s[0])
            lse_ref[1] = m_s[1] + jnp.log(l_s[1])

    grid = (n_seq, N_HEADS // 2, nb, nb)
    return pl.pallas_call(
        body, name=name, grid=grid,
        in_specs=[
            pl.BlockSpec((blk, LANES), lambda b, j, qi, ki: (b * nb + qi, qc + j)),
            pl.BlockSpec((blk, LANES), lambda b, j, qi, ki: (b * nb + jnp.minimum(ki, qi), kc + j)),
            pl.BlockSpec((blk, LANES), lambda b, j, qi, ki: (b * nb + jnp.minimum(ki, qi), vc + j)),
            pl.BlockSpec((2, blk, 1), lambda b, j, qi, ki: (j, b * nb + qi, 0)),
            pl.BlockSpec((2, 1, blk), lambda b, j, qi, ki: (j, 0, b * nb + jnp.minimum(ki, qi))),
        ],
        out_specs=[
            pl.BlockSpec((blk, LANES), lambda b, j, qi, ki: (b * nb + qi, j)),
            pl.BlockSpec((blk, LANES), lambda b, j, qi, ki: (b * nb + qi, j)),
            pl.BlockSpec((2, blk, 1), lambda b, j, qi, ki: (j, b * nb + qi, 0)),
        ],
        out_shape=[jax.ShapeDtypeStruct((T, BRANCH_W), BF16), jax.ShapeDtypeStruct((T, BRANCH_W), F32),
                   jax.ShapeDtypeStruct((N_HEADS, T, 1), F32)],
        scratch_shapes=[pltpu.VMEM((2, blk, 1), F32), pltpu.VMEM((2, blk, 1), F32),
                        pltpu.VMEM((2, blk, LANES), F32)],
        compiler_params=_cp(("parallel", "parallel", "parallel", "arbitrary")),
    )(proj, proj, proj, Fq, Fk)


def _attn_delta(do, o, name):
    T = do.shape[0]
    tm = min(512, T)

    def body(do_ref, o_ref, d_ref):
        prod = do_ref[...].astype(F32) * o_ref[...].astype(F32)
        lo, hi = _pair_masks()
        for j in range(N_HEADS // 2):
            pj = prod[:, j * LANES:(j + 1) * LANES]
            d_ref[2 * j] = jnp.sum(jnp.where(lo, pj, 0.0), axis=-1, keepdims=True)
            d_ref[2 * j + 1] = jnp.sum(jnp.where(hi, pj, 0.0), axis=-1, keepdims=True)

    row = pl.BlockSpec((tm, BRANCH_W), lambda i: (i, 0))
    return pl.pallas_call(
        body, name=name, grid=(T // tm,), in_specs=[row, row],
        out_specs=pl.BlockSpec((N_HEADS, tm, 1), lambda i: (0, i, 0)),
        out_shape=jax.ShapeDtypeStruct((N_HEADS, T, 1), F32),
        compiler_params=_cp(("parallel",)),
    )(do, o)


def _attn_bwd_dq(proj, do, lse, delta, Fq, Fk, n_seq, name):
    T = proj.shape[0]
    S = T // n_seq
    blk = min(ATT_BLK, S)
    nb = S // blk
    scale = HEAD_DIM ** -0.5
    qc, kc, vc = OFF_Q // LANES, OFF_K // LANES, OFF_V // LANES

    def body(q_ref, k_ref, v_ref, do_ref, lse_ref, dl_ref, fq_ref, fk_ref, dq_ref, acc_s):
        qi, ki = pl.program_id(2), pl.program_id(3)

        @pl.when(ki == 0)
        def _():
            acc_s[...] = jnp.zeros_like(acc_s)

        @pl.when(ki <= qi)
        def _():
            q, k, v, do_ = q_ref[...], k_ref[...], v_ref[...], do_ref[...]
            mask = _causal_mask(qi, ki, blk)
            for hh, sel in enumerate(_pair_masks()):
                s = _attn_logits(q, k, fq_ref[hh], fk_ref[hh], sel, mask, scale)
                p = jnp.exp(s - lse_ref[hh])
                dom = jnp.where(sel, do_, jnp.zeros_like(do_))
                dp = lax.dot_general(dom, v, (((1,), (1,)), ((), ())), preferred_element_type=F32)
                ds = p * (dp - dl_ref[hh])
                acc_s[hh] += jnp.dot(ds.astype(BF16), k, preferred_element_type=F32)

        @pl.when(ki == qi)
        def _():
            lo, _ = _pair_masks()
            dq_ref[...] = (jnp.where(lo, acc_s[0], acc_s[1]) * scale).astype(BF16)

    qmap = lambda b, j, qi, ki: (b * nb + qi, j)
    col1 = pl.BlockSpec((2, blk, 1), lambda b, j, qi, ki: (j, b * nb + qi, 0))
    return pl.pallas_call(
        body, name=name, grid=(n_seq, N_HEADS // 2, nb, nb),
        in_specs=[
            pl.BlockSpec((blk, LANES), lambda b, j, qi, ki: (b * nb + qi, qc + j)),
            pl.BlockSpec((blk, LANES), lambda b, j, qi, ki: (b * nb + jnp.minimum(ki, qi), kc + j)),
            pl.BlockSpec((blk, LANES), lambda b, j, qi, ki: (b * nb + jnp.minimum(ki, qi), vc + j)),
            pl.BlockSpec((blk, LANES), qmap),
            col1, col1, col1,
            pl.BlockSpec((2, 1, blk), lambda b, j, qi, ki: (j, 0, b * nb + jnp.minimum(ki, qi))),
        ],
        out_specs=pl.BlockSpec((blk, LANES), qmap),
        out_shape=jax.ShapeDtypeStruct((T, BRANCH_W), BF16),
        scratch_shapes=[pltpu.VMEM((2, blk, LANES), F32)],
        compiler_params=_cp(("parallel", "parallel", "parallel", "arbitrary")),
    )(proj, proj, proj, do, lse, delta, Fq, Fk)


def _attn_bwd_dkv(proj, do, lse, delta, Fq, Fk, n_seq, name):
    T = proj.shape[0]
    S = T // n_seq
    blk = min(ATT_BLK, S)
    nb = S // blk
    scale = HEAD_DIM ** -0.5
    qc, kc, vc = OFF_Q // LANES, OFF_K // LANES, OFF_V // LANES
    tdot = functools.partial(lax.dot_general, dimension_numbers=(((0,), (0,)), ((), ())),
                             preferred_element_type=F32)

    def body(q_ref, k_ref, v_ref, do_ref, lse_ref, dl_ref, fq_ref, fk_ref, dk_ref, dv_ref, dfk_ref,
             dk_s, dv_s, df_s):
        ki, qi = pl.program_id(2), pl.program_id(3)

        @pl.when(qi == 0)
        def _():
            dk_s[...] = jnp.zeros_like(dk_s)
            dv_s[...] = jnp.zeros_like(dv_s)
            df_s[...] = jnp.zeros_like(df_s)

        @pl.when(qi >= ki)
        def _():
            q, k, v, do_ = q_ref[...], k_ref[...], v_ref[...], do_ref[...]
            mask = _causal_mask(qi, ki, blk)
            for hh, sel in enumerate(_pair_masks()):
                s = _attn_logits(q, k, fq_ref[hh], fk_ref[hh], sel, mask, scale)
                p = jnp.exp(s - lse_ref[hh])
                dv_s[hh] += tdot(p.astype(BF16), do_)
                dom = jnp.where(sel, do_, jnp.zeros_like(do_))
                dp = lax.dot_general(dom, v, (((1,), (1,)), ((), ())), preferred_element_type=F32)
                ds = p * (dp - dl_ref[hh])
                dk_s[hh] += tdot(ds.astype(BF16), q)
                df_s[hh] -= jnp.sum(ds, axis=0, keepdims=True)

        @pl.when(qi == nb - 1)
        def _():
            lo, _ = _pair_masks()
            dk_ref[...] = (jnp.where(lo, dk_s[0], dk_s[1]) * scale).astype(BF16)
            dv_ref[...] = jnp.where(lo, dv_s[0], dv_s[1]).astype(BF16)
            dfk_ref[...] = df_s[...]

    kmap = lambda b, j, ki, qi: (b * nb + ki, j)
    col1 = pl.BlockSpec((2, blk, 1), lambda b, j, ki, qi: (j, b * nb + jnp.maximum(qi, ki), 0))
    rowk = pl.BlockSpec((2, 1, blk), lambda b, j, ki, qi: (j, 0, b * nb + ki))
    return pl.pallas_call(
        body, name=name, grid=(n_seq, N_HEADS // 2, nb, nb),
        in_specs=[
            pl.BlockSpec((blk, LANES), lambda b, j, ki, qi: (b * nb + jnp.maximum(qi, ki), qc + j)),
            pl.BlockSpec((blk, LANES), lambda b, j, ki, qi: (b * nb + ki, kc + j)),
            pl.BlockSpec((blk, LANES), lambda b, j, ki, qi: (b * nb + ki, vc + j)),
            pl.BlockSpec((blk, LANES), lambda b, j, ki, qi: (b * nb + jnp.maximum(qi, ki), j)),
            col1, col1, col1, rowk,
        ],
        out_specs=[pl.BlockSpec((blk, LANES), kmap), pl.BlockSpec((blk, LANES), kmap), rowk],
        out_shape=[jax.ShapeDtypeStruct((T, BRANCH_W), BF16), jax.ShapeDtypeStruct((T, BRANCH_W), BF16),
                   jax.ShapeDtypeStruct((N_HEADS, 1, T), F32)],
        scratch_shapes=[pltpu.VMEM((2, blk, LANES), F32), pltpu.VMEM((2, blk, LANES), F32),
                        pltpu.VMEM((2, 1, blk), F32)],
        compiler_params=_cp(("parallel", "parallel", "parallel", "arbitrary")),
    )(proj, proj, proj, do, lse, delta, Fq, Fk)


AUG0 = HEAD_DIM
Q_TILE, K_CHUNK, ROW_GROUP = 512, 256, 64


def _fox_prep(f, bf, proj, n_seq, name):
    T = f.shape[0]
    S = T // n_seq
    c = min(CUM_BLK, S)

    def body(f_ref, b_ref, q_ref, k_ref, v_ref, qa_ref, ka_ref, va_ref):
        ri = lax.broadcasted_iota(jnp.int32, (c, c), 0)
        ci = lax.broadcasted_iota(jnp.int32, (c, c), 1)
        tri = (ri >= ci).astype(BF16)
        lane = lax.broadcasted_iota(jnp.int32, (c, LANES), 1)
        carry = jnp.zeros((1, LANES), F32)
        for j in range(S // c):
            rows = slice(j * c, (j + 1) * c)
            lf = _log_sigmoid(f_ref[rows, :] + b_ref[...])
            Fc = _tri_dot(tri, lf) + carry
            carry = carry + jnp.sum(lf, axis=0, keepdims=True)
            for h in range(N_HEADS):
                col = jnp.sum(jnp.where(lane == h, Fc, 0.0), axis=-1, keepdims=True)
                hi = col.astype(BF16).astype(F32)
                r1 = col - hi
                mid = r1.astype(BF16).astype(F32)
                lo = r1 - mid
                ones_q = jnp.logical_and(lane >= AUG0 + 3, lane < AUG0 + 6)
                ones_k = jnp.logical_and(lane >= AUG0, lane < AUG0 + 3)
                aug_q = jnp.where(lane == AUG0, hi, jnp.where(lane == AUG0 + 1, mid, jnp.where(
                    lane == AUG0 + 2, lo, jnp.where(ones_q, 1.0, 0.0))))
                aug_k = jnp.where(lane == AUG0 + 3, -hi, jnp.where(lane == AUG0 + 4, -mid, jnp.where(
                    lane == AUG0 + 5, -lo, jnp.where(ones_k, 1.0, 0.0))))
                pair = slice((h // 2) * LANES, (h // 2 + 1) * LANES)
                qp, kp = q_ref[rows, pair].astype(F32), k_ref[rows, pair].astype(F32)
                vp = v_ref[rows, pair].astype(F32)
                if h % 2:
                    qp, kp, vp = (pltpu.roll(a, HEAD_DIM, 1) for a in (qp, kp, vp))
                out = slice(h * LANES, (h + 1) * LANES)
                qa_ref[rows, out] = jnp.where(lane < HEAD_DIM, qp * (HEAD_DIM ** -0.5), aug_q).astype(BF16)
                ka_ref[rows, out] = jnp.where(lane < HEAD_DIM, kp, aug_k).astype(BF16)
                va_ref[rows, out] = jnp.where(lane < HEAD_DIM, vp, jnp.where(lane == AUG0, 1.0, 0.0)).astype(BF16)

    fblk = pl.BlockSpec((S, LANES), lambda b: (b, 0))
    out = pl.BlockSpec((S, N_HEADS * LANES), lambda b: (b, 0))
    sh = jax.ShapeDtypeStruct((T, N_HEADS * LANES), BF16)
    return pl.pallas_call(
        body, name=name, grid=(n_seq,),
        in_specs=[fblk, pl.BlockSpec((1, LANES), lambda b: (0, 0)),
                  pl.BlockSpec((S, BRANCH_W), lambda b: (b, OFF_Q // BRANCH_W)),
                  pl.BlockSpec((S, BRANCH_W), lambda b: (b, OFF_K // BRANCH_W)),
                  pl.BlockSpec((S, BRANCH_W), lambda b: (b, OFF_V // BRANCH_W))],
        out_specs=[out, out, out], out_shape=[sh, sh, sh],
        compiler_params=_cp(("parallel",)),
    )(f, bf, proj, proj, proj)


def _band_mask(q0, k0, nq, nk):
    row = q0 + lax.broadcasted_iota(jnp.int32, (nq, nk), 0)
    col = k0 + lax.broadcasted_iota(jnp.int32, (nq, nk), 1)
    return col <= row


_NT = (((1,), (1,)), ((), ()))
_TN = (((0,), (0,)), ((), ()))


def _attn_fwd2(qa, ka, va, n_seq, name):
    T = qa.shape[0]
    S = T // n_seq
    tq, tk, rg = min(Q_TILE, S), min(K_CHUNK, S), ROW_GROUP
    nq, per = S // tq, tq // tk

    def body(q_ref, k_ref, v_ref, o_ref, o32_ref, lse_ref, phi_s, plo_s, mp_s, m_s, acc_s):
        qi = pl.program_id(2)
        mp_s[...] = jnp.full_like(mp_s, NEG_INF)
        acc_s[...] = jnp.zeros_like(acc_s)

        def scores(kc, hh):
            k0 = pl.multiple_of(kc * tk, tk)
            hl = slice(hh * LANES, (hh + 1) * LANES)
            return k0, lax.dot_general(q_ref[:, hl], k_ref[pl.ds(k0, tk), hl], _NT, preferred_element_type=F32)

        def max_chunk(kc, masked):
            for hh in range(2):
                k0, s_all = scores(kc, hh)
                for r in range(tq // rg):
                    rows = slice(r * rg, (r + 1) * rg)
                    s = s_all[rows, :]
                    if masked:
                        s = jnp.where(_band_mask(qi * tq + r * rg, k0, rg, tk), s, NEG_INF)
                    part = s[:, :LANES]
                    for c in range(1, tk // LANES):
                        part = jnp.maximum(part, s[:, c * LANES:(c + 1) * LANES])
                    mp_s[hh, rows, :] = jnp.maximum(mp_s[hh, rows, :], part)

        def sum_chunk(kc, masked):
            for hh in range(2):
                k0, s_all = scores(kc, hh)
                hl = slice(hh * LANES, (hh + 1) * LANES)
                v = v_ref[pl.ds(k0, tk), hl]
                for r in range(tq // rg):
                    rows = slice(r * rg, (r + 1) * rg)
                    p = jnp.exp(s_all[rows, :] - m_s[hh, rows])
                    if masked:
                        p = jnp.where(_band_mask(qi * tq + r * rg, k0, rg, tk), p, 0.0)
                    p_hi = p.astype(BF16)
                    phi_s[hh, rows, :] = p_hi
                    plo_s[hh, rows, :] = (p - p_hi.astype(F32)).astype(BF16)
                acc_s[hh] += (jnp.dot(phi_s[hh], v, preferred_element_type=F32)
                              + jnp.dot(plo_s[hh], v, preferred_element_type=F32))

        def sweep(chunk):
            def unmasked(kc, carry):
                chunk(kc, False)
                return carry

            lax.fori_loop(0, qi * per, unmasked, 0)
            for d in range(per):
                chunk(qi * per + d, True)

        sweep(max_chunk)
        m_s[...] = jnp.max(mp_s[...], axis=-1, keepdims=True)
        sweep(sum_chunk)

        lane = lax.broadcasted_iota(jnp.int32, (1, LANES), 1)
        outs = []
        for hh in range(2):
            acc = acc_s[hh]
            l = jnp.sum(jnp.where(lane == AUG0, acc, 0.0), axis=-1, keepdims=True)
            lse_ref[hh] = m_s[hh] + jnp.log(l)
            outs.append(acc / l)
        o = jnp.where(lane < HEAD_DIM, outs[0], pltpu.roll(outs[1], HEAD_DIM, 1))
        o_ref[...] = o.astype(BF16)
        o32_ref[...] = o

    qmap = lambda b, j, qi: (b * nq + qi, j)
    omap = lambda b, j, qi: (b * nq + qi, j)
    kv = pl.BlockSpec((S, 2 * LANES), lambda b, j, qi: (b, j))
    return pl.pallas_call(
        body, name=name, grid=(n_seq, N_HEADS // 2, nq),
        in_specs=[pl.BlockSpec((tq, 2 * LANES), qmap), kv, kv],
        out_specs=[pl.BlockSpec((tq, LANES), omap), pl.BlockSpec((tq, LANES), omap),
                   pl.BlockSpec((2, tq, 1), lambda b, j, qi: (j, b * nq + qi, 0))],
        out_shape=[jax.ShapeDtypeStruct((T, BRANCH_W), BF16), jax.ShapeDtypeStruct((T, BRANCH_W), F32),
                   jax.ShapeDtypeStruct((N_HEADS, T, 1), F32)],
        scratch_shapes=[pltpu.VMEM((2, tq, tk), BF16), pltpu.VMEM((2, tq, tk), BF16),
                        pltpu.VMEM((2, tq, LANES), F32), pltpu.VMEM((2, tq, 1), F32),
                        pltpu.VMEM((2, tq, LANES), F32)],
        compiler_params=_cp(("parallel", "parallel", "parallel")),
    )(qa, ka, va)


def _attn_bwd_dq2(qa, ka, proj, do, lse, delta, n_seq, name):
    T = qa.shape[0]
    S = T // n_seq
    tq, tk, rg = min(Q_TILE, S), min(K_CHUNK, S), ROW_GROUP
    nq, per = S // tq, tq // tk
    vc = OFF_V // LANES

    def body(q_ref, k_ref, v_ref, do_ref, lse_ref, dl_ref, dq_ref, ds_s, acc_s):
        qi = pl.program_id(2)
        acc_s[...] = jnp.zeros_like(acc_s)
        sels = _pair_masks()

        def chunk(kc, masked):
            k0 = pl.multiple_of(kc * tk, tk)
            v = v_ref[pl.ds(k0, tk), :]
            for hh in range(2):
                hl = slice(hh * LANES, (hh + 1) * LANES)
                kh = k_ref[pl.ds(k0, tk), hl]
                s_all = lax.dot_general(q_ref[:, hl], kh, _NT, preferred_element_type=F32)
                dom = jnp.where(sels[hh], do_ref[...], jnp.zeros_like(do_ref[...]))
                dp_all = lax.dot_general(dom, v, _NT, preferred_element_type=F32)
                for r in range(tq // rg):
                    rows = slice(r * rg, (r + 1) * rg)
                    p = jnp.exp(s_all[rows, :] - lse_ref[hh, rows])
                    if masked:
                        p = jnp.where(_band_mask(qi * tq + r * rg, k0, rg, tk), p, 0.0)
                    ds_s[hh, rows, :] = (p * (dp_all[rows, :] - dl_ref[hh, rows])).astype(BF16)
                acc_s[hh] += jnp.dot(ds_s[hh], kh, preferred_element_type=F32)

        def unmasked(kc, carry):
            chunk(kc, False)
            return carry

        lax.fori_loop(0, qi * per, unmasked, 0)
        for d in range(per):
            chunk(qi * per + d, True)
        dq = jnp.where(sels[0], acc_s[0], pltpu.roll(acc_s[1], HEAD_DIM, 1))
        dq_ref[...] = (dq * (HEAD_DIM ** -0.5)).astype(BF16)

    qmap = lambda b, j, qi: (b * nq + qi, j)
    col1 = pl.BlockSpec((2, tq, 1), lambda b, j, qi: (j, b * nq + qi, 0))
    return pl.pallas_call(
        body, name=name, grid=(n_seq, N_HEADS // 2, nq),
        in_specs=[pl.BlockSpec((tq, 2 * LANES), qmap),
                  pl.BlockSpec((S, 2 * LANES), lambda b, j, qi: (b, j)),
                  pl.BlockSpec((S, LANES), lambda b, j, qi: (b, vc + j)),
                  pl.BlockSpec((tq, LANES), qmap), col1, col1],
        out_specs=pl.BlockSpec((tq, LANES), qmap),
        out_shape=jax.ShapeDtypeStruct((T, BRANCH_W), BF16),
        scratch_shapes=[pltpu.VMEM((2, tq, tk), BF16), pltpu.VMEM((2, tq, LANES), F32)],
        compiler_params=_cp(("parallel", "parallel", "parallel")),
    )(qa, ka, proj, do, lse, delta)


def _attn_bwd_dkv2(qa, ka, proj, do, lse, delta, n_seq, name):
    T = qa.shape[0]
    S = T // n_seq
    tkt, tqc, rg = min(Q_TILE, S), min(K_CHUNK, S), ROW_GROUP // 2
    nk, per, nqc = S // tkt, tkt // tqc, S // tqc
    vc = OFF_V // LANES

    def body(q_ref, k_ref, v_ref, do_ref, lse_ref, dl_ref, dk_ref, dv_ref, dfk_ref,
             p_s, ds_s, dk_s, dv_s, df_s):
        ki = pl.program_id(2)
        dk_s[...] = jnp.zeros_like(dk_s)
        dv_s[...] = jnp.zeros_like(dv_s)
        df_s[...] = jnp.zeros_like(df_s)
        sels = _pair_masks()
        v = v_ref[...]

        def chunk(qc, masked):
            q0 = pl.multiple_of(qc * tqc, tqc)
            do_ = do_ref[pl.ds(q0, tqc), :]
            for hh in range(2):
                hl = slice(hh * LANES, (hh + 1) * LANES)
                qh = q_ref[pl.ds(q0, tqc), hl]
                s_all = lax.dot_general(qh, k_ref[:, hl], _NT, preferred_element_type=F32)
                dom = jnp.where(sels[hh], do_, jnp.zeros_like(do_))
                dp_all = lax.dot_general(dom, v, _NT, preferred_element_type=F32)
                dfp = jnp.zeros((1, tkt), F32)
                for r in range(tqc // rg):
                    rows = slice(r * rg, (r + 1) * rg)
                    qrows = pl.ds(q0 + r * rg, rg)
                    p = jnp.exp(s_all[rows, :] - lse_ref[hh, qrows])
                    if masked:
                        p = jnp.where(_band_mask(q0 + r * rg, ki * tkt, rg, tkt), p, 0.0)
                    ds = p * (dp_all[rows, :] - dl_ref[hh, qrows])
                    p_s[hh, rows, :] = p.astype(BF16)
                    ds_s[hh, rows, :] = ds.astype(BF16)
                    dfp = dfp + jnp.sum(ds, axis=0, keepdims=True)
                df_s[hh] -= dfp
                dv_s[hh] += lax.dot_general(p_s[hh], do_, _TN, preferred_element_type=F32)
                dk_s[hh] += lax.dot_general(ds_s[hh], qh, _TN, preferred_element_type=F32)

        for d in range(per):
            chunk(ki * per + d, True)

        def unmasked(qc, carry):
            chunk(qc, False)
            return carry

        lax.fori_loop((ki + 1) * per, nqc, unmasked, 0)
        dk_ref[...] = jnp.where(sels[0], dk_s[0], pltpu.roll(dk_s[1], HEAD_DIM, 1)).astype(BF16)
        dv_ref[...] = jnp.where(sels[0], dv_s[0], dv_s[1]).astype(BF16)
        dfk_ref[...] = df_s[...]

    kmap = lambda b, j, ki: (b * nk + ki, j)
    col1 = pl.BlockSpec((2, S, 1), lambda b, j, ki: (j, b, 0))
    rowk = pl.BlockSpec((2, 1, tkt), lambda b, j, ki: (j, 0, b * nk + ki))
    return pl.pallas_call(
        body, name=name, grid=(n_seq, N_HEADS // 2, nk),
        in_specs=[pl.BlockSpec((S, 2 * LANES), lambda b, j, ki: (b, j)),
                  pl.BlockSpec((tkt, 2 * LANES), kmap),
                  pl.BlockSpec((tkt, LANES), lambda b, j, ki: (b * nk + ki, vc + j)),
                  pl.BlockSpec((S, LANES), lambda b, j, ki: (b, j)), col1, col1],
        out_specs=[pl.BlockSpec((tkt, LANES), kmap), pl.BlockSpec((tkt, LANES), kmap), rowk],
        out_shape=[jax.ShapeDtypeStruct((T, BRANCH_W), BF16), jax.ShapeDtypeStruct((T, BRANCH_W), BF16),
                   jax.ShapeDtypeStruct((N_HEADS, 1, T), F32)],
        scratch_shapes=[pltpu.VMEM((2, tqc, tkt), BF16), pltpu.VMEM((2, tqc, tkt), BF16),
                        pltpu.VMEM((2, tkt, LANES), F32),
                        pltpu.VMEM((2, tkt, LANES), F32), pltpu.VMEM((2, 1, tkt), F32)],
        compiler_params=_cp(("parallel", "parallel", "parallel")),
    )(qa, ka, proj, do, lse, delta)


def _attn_bwd(qa, ka, proj, do, o32, lse, n_seq, name):
    T = qa.shape[0]
    S = T // n_seq
    tq, tk, rg = min(Q_TILE, S), min(K_CHUNK, S), ROW_GROUP
    nq, per, nkc = S // tq, tq // tk, S // tk
    vc = OFF_V // LANES

    def body(q_ref, k_ref, v_ref, do_ref, o_ref, lse_ref, dq_ref, dk_ref, dv_ref, dfk_ref,
             p_s, ds_s, dq_s, dk_s, dv_s, df_s):
        dk_s[...] = jnp.zeros_like(dk_s)
        dv_s[...] = jnp.zeros_like(dv_s)
        df_s[...] = jnp.zeros_like(df_s)
        sels = _pair_masks()

        for qi in range(nq):
            q0 = qi * tq
            do_t = do_ref[q0:q0 + tq, :]
            dq_s[...] = jnp.zeros_like(dq_s)
            prod = do_t.astype(F32) * o_ref[q0:q0 + tq, :]
            dls = [jnp.sum(jnp.where(sel, prod, 0.0), axis=-1, keepdims=True) for sel in sels]

            def chunk(kc, masked, q0=q0, do_t=do_t, dls=dls):
                k0 = pl.multiple_of(kc * tk, tk)
                v = v_ref[pl.ds(k0, tk), :]
                for hh in range(2):
                    hl = slice(hh * LANES, (hh + 1) * LANES)
                    qh, kh = q_ref[q0:q0 + tq, hl], k_ref[pl.ds(k0, tk), hl]
                    s_all = lax.dot_general(qh, kh, _NT, preferred_element_type=F32)
                    dom = jnp.where(sels[hh], do_t, jnp.zeros_like(do_t))
                    dp_all = lax.dot_general(dom, v, _NT, preferred_element_type=F32)
                    dfp = jnp.zeros((1, tk), F32)
                    for r in range(tq // rg):
                        rows = slice(r * rg, (r + 1) * rg)
                        qrows = slice(q0 + r * rg, q0 + (r + 1) * rg)
                        p = jnp.exp(s_all[rows, :] - lse_ref[hh, qrows])
                        if masked:
                            p = jnp.where(_band_mask(q0 + r * rg, k0, rg, tk), p, 0.0)
                        ds = p * (dp_all[rows, :] - dls[hh][rows])
                        p_s[hh, rows, :] = p.astype(BF16)
                        ds_s[hh, rows, :] = ds.astype(BF16)
                        dfp = dfp + jnp.sum(ds, axis=0, keepdims=True)
                    df_s[hh, kc] -= dfp
                    dq_s[hh] += jnp.dot(ds_s[hh], kh, preferred_element_type=F32)
                    dv_s[hh, pl.ds(k0, tk), :] += lax.dot_general(p_s[hh], do_t, _TN, preferred_element_type=F32)
                    dk_s[hh, pl.ds(k0, tk), :] += lax.dot_general(ds_s[hh], qh, _TN, preferred_element_type=F32)

            def unmasked(kc, carry, chunk=chunk):
                chunk(kc, False)
                return carry

            lax.fori_loop(0, qi * per, unmasked, 0)
            for d in range(per):
                chunk(qi * per + d, True)
            dq = jnp.where(sels[0], dq_s[0], pltpu.roll(dq_s[1], HEAD_DIM, 1))
            dq_ref[q0:q0 + tq, :] = (dq * (HEAD_DIM ** -0.5)).astype(BF16)

        dk_ref[...] = jnp.where(sels[0], dk_s[0], pltpu.roll(dk_s[1], HEAD_DIM, 1)).astype(BF16)
        dv_ref[...] = jnp.where(sels[0], dv_s[0], dv_s[1]).astype(BF16)
        for c in range(nkc):
            dfk_ref[:, :, c * tk:(c + 1) * tk] = df_s[:, c]

    seq = lambda w: pl.BlockSpec((S, w), lambda b, j: (b, j))
    col1 = pl.BlockSpec((2, S, 1), lambda b, j: (j, b, 0))
    act = jax.ShapeDtypeStruct((T, BRANCH_W), BF16)
    return pl.pallas_call(
        body, name=name, grid=(n_seq, N_HEADS // 2),
        in_specs=[seq(2 * LANES), seq(2 * LANES), pl.BlockSpec((S, LANES), lambda b, j: (b, vc + j)), seq(LANES),
                  seq(LANES), col1],
        out_specs=[seq(LANES), seq(LANES), seq(LANES), pl.BlockSpec((2, 1, S), lambda b, j: (j, 0, b))],
        out_shape=[act, act, act, jax.ShapeDtypeStruct((N_HEADS, 1, T), F32)],
        scratch_shapes=[pltpu.VMEM((2, tq, tk), BF16), pltpu.VMEM((2, tq, tk), BF16),
                        pltpu.VMEM((2, tq, LANES), F32), pltpu.VMEM((2, S, LANES), F32),
                        pltpu.VMEM((2, S, LANES), F32), pltpu.VMEM((2, nkc, 1, tk), F32)],
        compiler_params=_cp(("parallel", "parallel")),
    )(qa, ka, proj, do, o32, lse)


def _shift_down(v, k, row):
    return jnp.where(row >= k, pltpu.roll(v, k, 0), 0.0)


def _shift_up(v, k, row, S):
    return jnp.where(row < S - k, pltpu.roll(v, S - k, 0), 0.0)


def _pool_diff(uf, w, row):
    acc, k = uf, 1
    while k < w:
        acc = acc + _shift_down(acc, k, row)
        k *= 2
    n = jnp.minimum(row + 1, w).astype(F32)
    return acc / n - uf


def _pool_fwd(proj, pool_w, pool_scale, n_seq, name):
    T = proj.shape[0]
    S = T // n_seq

    def body(u_ref, w_ref, sc_ref, o_ref):
        g = pl.program_id(1)
        row = lax.broadcasted_iota(jnp.int32, (S, POOL_GD), 0)
        uf = u_ref[...].astype(F32)
        d = _pool_diff(uf, POOL_WINDOWS[0], row)
        for gi in range(1, len(POOL_WINDOWS)):
            d = jnp.where(g == gi, _pool_diff(uf, POOL_WINDOWS[gi], row), d)
        e = jnp.dot(d.astype(BF16), w_ref[0], preferred_element_type=F32)
        o_ref[...] = (e * sc_ref[...]).astype(BF16)

    uc = OFF_U // POOL_GD
    return pl.pallas_call(
        body, name=name, grid=(n_seq, len(POOL_WINDOWS)),
        in_specs=[pl.BlockSpec((S, POOL_GD), lambda b, g: (b, uc + g)),
                  pl.BlockSpec((1, POOL_GD, POOL_GD), lambda b, g: (g, 0, 0)),
                  pl.BlockSpec((1, POOL_GD), lambda b, g: (0, g))],
        out_specs=pl.BlockSpec((S, POOL_GD), lambda b, g: (b, g)),
        out_shape=jax.ShapeDtypeStruct((T, BRANCH_W), BF16),
        compiler_params=_cp(("parallel", "parallel")),
    )(proj, pool_w, pool_scale)


def _pool_bwd(proj, dout, pool_w, pool_scale, n_seq, name):
    T = proj.shape[0]
    S = T // n_seq

    def body(u_ref, do_ref, w_ref, sc_ref, du_ref, dw_ref, dsc_ref):
        g, b = pl.program_id(0), pl.program_id(1)
        row = lax.broadcasted_iota(jnp.int32, (S, POOL_GD), 0)
        uf = u_ref[...].astype(F32)
        d = _pool_diff(uf, POOL_WINDOWS[0], row)
        for gi in range(1, len(POOL_WINDOWS)):
            d = jnp.where(g == gi, _pool_diff(uf, POOL_WINDOWS[gi], row), d)
        db16 = d.astype(BF16)
        w = w_ref[0]
        e = jnp.dot(db16, w, preferred_element_type=F32)
        dof = do_ref[...].astype(F32)
        dsc = jnp.sum(dof * e, axis=0, keepdims=True)
        de = (dof * sc_ref[...]).astype(BF16)
        dd = lax.dot_general(de, w, (((1,), (1,)), ((), ())), preferred_element_type=F32)
        dw = lax.dot_general(db16, de, (((0,), (0,)), ((), ())), preferred_element_type=F32)
        du = jnp.zeros_like(dd)
        for gi, wlen in enumerate(POOL_WINDOWS):
            n = jnp.minimum(row + 1, wlen).astype(F32)
            acc, k = dd / n, 1
            while k < wlen:
                acc = acc + _shift_up(acc, k, row, S)
                k *= 2
            du = jnp.where(g == gi, acc - dd, du)
        du_ref[...] = du.astype(BF16)

        @pl.when(b == 0)
        def _():
            dw_ref[0] = dw
            dsc_ref[...] = dsc

        @pl.when(b > 0)
        def _():
            dw_ref[0] += dw
            dsc_ref[...] += dsc

    uc = OFF_U // POOL_GD
    return pl.pallas_call(
        body, name=name, grid=(len(POOL_WINDOWS), n_seq),
        in_specs=[pl.BlockSpec((S, POOL_GD), lambda g, b: (b, uc + g)),
                  pl.BlockSpec((S, POOL_GD), lambda g, b: (b, g)),
                  pl.BlockSpec((1, POOL_GD, POOL_GD), lambda g, b: (g, 0, 0)),
                  pl.BlockSpec((1, POOL_GD), lambda g, b: (0, g))],
        out_specs=[pl.BlockSpec((S, POOL_GD), lambda g, b: (b, g)),
                   pl.BlockSpec((1, POOL_GD, POOL_GD), lambda g, b: (g, 0, 0)),
                   pl.BlockSpec((1, POOL_GD), lambda g, b: (0, g))],
        out_shape=[jax.ShapeDtypeStruct((T, BRANCH_W), BF16),
                   jax.ShapeDtypeStruct((len(POOL_WINDOWS), POOL_GD, POOL_GD), F32),
                   jax.ShapeDtypeStruct((1, BRANCH_W), F32)],
        compiler_params=_cp(("parallel", "arbitrary")),
    )(proj, dout, pool_w, pool_scale)


def _conv_fwd(proj, conv_w, n_seq, name):
    T = proj.shape[0]
    S = T // n_seq
    nc = BRANCH_W // LANES

    def body(cv_ref, cb_ref, cc_ref, w_ref, o_ref):
        row = lax.broadcasted_iota(jnp.int32, (S, LANES), 0)
        z = cc_ref[...].astype(F32) * cv_ref[...].astype(F32)
        w = w_ref[...]
        y = w[0:1] * _shift_down(z, 2, row) + w[1:2] * _shift_down(z, 1, row) + w[2:3] * z
        o_ref[...] = (cb_ref[...].astype(F32) * y).astype(BF16)

    def col(off):
        return pl.BlockSpec((S, LANES), lambda b, j: (b, off // LANES + j))

    return pl.pallas_call(
        body, name=name, grid=(n_seq, nc),
        in_specs=[col(OFF_CV), col(OFF_CB), col(OFF_CC), pl.BlockSpec((CONV_K, LANES), lambda b, j: (0, j))],
        out_specs=pl.BlockSpec((S, LANES), lambda b, j: (b, j)),
        out_shape=jax.ShapeDtypeStruct((T, BRANCH_W), BF16),
        compiler_params=_cp(("parallel", "parallel")),
    )(proj, proj, proj, conv_w)


def _conv_bwd(proj, dout, conv_w, n_seq, name):
    T = proj.shape[0]
    S = T // n_seq
    nc = BRANCH_W // LANES

    def body(cv_ref, cb_ref, cc_ref, do_ref, w_ref, dcv_ref, dcb_ref, dcc_ref, dw_ref):
        b = pl.program_id(1)
        row = lax.broadcasted_iota(jnp.int32, (S, LANES), 0)
        cv, cb, cc = cv_ref[...].astype(F32), cb_ref[...].astype(F32), cc_ref[...].astype(F32)
        dof = do_ref[...].astype(F32)
        w = w_ref[...]
        z = cc * cv
        z1, z2 = _shift_down(z, 1, row), _shift_down(z, 2, row)
        y = w[0:1] * z2 + w[1:2] * z1 + w[2:3] * z
        dcb_ref[...] = (dof * y).astype(BF16)
        dy = dof * cb
        dz = w[2:3] * dy + w[1:2] * _shift_up(dy, 1, row, S) + w[0:1] * _shift_up(dy, 2, row, S)
        dcc_ref[...] = (dz * cv).astype(BF16)
        dcv_ref[...] = (dz * cc).astype(BF16)
        dws = [jnp.sum(dy * zk, axis=0, keepdims=True) for zk in (z2, z1, z)]

        @pl.when(b == 0)
        def _():
            for kk in range(CONV_K):
                dw_ref[kk:kk + 1, :] = dws[kk]

        @pl.when(b > 0)
        def _():
            for kk in range(CONV_K):
                dw_ref[kk:kk + 1, :] += dws[kk]

    def col(off):
        return pl.BlockSpec((S, LANES), lambda j, b: (b, off // LANES + j))

    out = pl.BlockSpec((S, LANES), lambda j, b: (b, j))
    wsp = pl.BlockSpec((CONV_K, LANES), lambda j, b: (0, j))
    act = jax.ShapeDtypeStruct((T, BRANCH_W), BF16)
    return pl.pallas_call(
        body, name=name, grid=(nc, n_seq),
        in_specs=[col(OFF_CV), col(OFF_CB), col(OFF_CC), out, wsp],
        out_specs=[out, out, out, wsp],
        out_shape=[act, act, act, jax.ShapeDtypeStruct((CONV_K, BRANCH_W), F32)],
        compiler_params=_cp(("parallel", "arbitrary")),
    )(proj, proj, proj, dout, conv_w)


def _mix_fwd(oa, ob, oc, wpa, wpp, wpc, proj, b_gate, name):
    T = oa.shape[0]
    tm = min(256, T)

    def body(oa_ref, ob_ref, oc_ref, wa_ref, wp_ref, wc_ref, g_ref, bg_ref, o_ref):
        acc = jnp.zeros((tm, D_MODEL), F32)
        for i, (x_ref, w_ref) in enumerate(((oa_ref, wa_ref), (ob_ref, wp_ref), (oc_ref, wc_ref))):
            y = jnp.dot(x_ref[...], w_ref[...], preferred_element_type=F32)
            sl = slice(i * D_MODEL, (i + 1) * D_MODEL)
            acc = acc + _sigmoid(g_ref[:, sl].astype(F32) + bg_ref[:, sl]) * y
        o_ref[...] = acc.astype(BF16)

    br = pl.BlockSpec((tm, BRANCH_W), lambda i: (i, 0))
    wsp = pl.BlockSpec((BRANCH_W, D_MODEL), lambda i: (0, 0))
    return pl.pallas_call(
        body, name=name, grid=(T // tm,),
        in_specs=[br, br, br, wsp, wsp, wsp, pl.BlockSpec((tm, GATE_W), lambda i: (i, 0)),
                  pl.BlockSpec((1, GATE_W), lambda i: (0, 0))],
        out_specs=pl.BlockSpec((tm, D_MODEL), lambda i: (i, 0)),
        out_shape=jax.ShapeDtypeStruct((T, D_MODEL), BF16),
        compiler_params=_cp(("parallel",)),
    )(oa, ob, oc, wpa, wpp, wpc, proj, b_gate)


def _mix_bwd(oa, ob, oc, wpa, wpp, wpc, proj, b_gate, dmixed, name):
    T = oa.shape[0]
    tm = min(256, T)

    def body(oa_ref, ob_ref, oc_ref, wa_ref, wp_ref, wc_ref, g_ref, bg_ref, dm_ref,
             dya_ref, dyb_ref, dyc_ref, dg_ref, dbg_ref):
        i0 = pl.program_id(0)
        dm = dm_ref[...].astype(F32)
        parts = []
        for i, (x_ref, w_ref, dy_ref) in enumerate(((oa_ref, wa_ref, dya_ref), (ob_ref, wp_ref, dyb_ref),
                                                    (oc_ref, wc_ref, dyc_ref))):
            y = jnp.dot(x_ref[...], w_ref[...], preferred_element_type=F32)
            sl = slice(i * D_MODEL, (i + 1) * D_MODEL)
            gate = _sigmoid(g_ref[:, sl].astype(F32) + bg_ref[:, sl])
            dy_ref[...] = (dm * gate).astype(BF16)
            dgl = dm * y * gate * (1.0 - gate)
            dg_ref[:, sl] = dgl.astype(BF16)
            parts.append(jnp.sum(dgl, axis=0, keepdims=True))

        @pl.when(i0 == 0)
        def _():
            for i in range(3):
                dbg_ref[:, i * D_MODEL:(i + 1) * D_MODEL] = parts[i]

        @pl.when(i0 > 0)
        def _():
            for i in range(3):
                dbg_ref[:, i * D_MODEL:(i + 1) * D_MODEL] += parts[i]

    br = pl.BlockSpec((tm, BRANCH_W), lambda i: (i, 0))
    wsp = pl.BlockSpec((BRANCH_W, D_MODEL), lambda i: (0, 0))
    row = pl.BlockSpec((tm, D_MODEL), lambda i: (i, 0))
    gsp = pl.BlockSpec((tm, GATE_W), lambda i: (i, 0))
    bsp = pl.BlockSpec((1, GATE_W), lambda i: (0, 0))
    act = jax.ShapeDtypeStruct((T, D_MODEL), BF16)
    return pl.pallas_call(
        body, name=name, grid=(T // tm,),
        in_specs=[br, br, br, wsp, wsp, wsp, gsp, bsp, row],
        out_specs=[row, row, row, gsp, bsp],
        out_shape=[act, act, act, jax.ShapeDtypeStruct((T, GATE_W), BF16),
                   jax.ShapeDtypeStruct((1, GATE_W), F32)],
        compiler_params=_cp(("arbitrary",)),
    )(oa, ob, oc, wpa, wpp, wpc, proj, b_gate, dmixed)


GU_TILE = 256


def _gu_col(c):
    t, r = divmod(c, GU_TILE)
    return (t // 2) * GU_TILE + r + (FFN_HIDDEN if t % 2 else 0)


def _gate_up_swiglu(h, w, name):
    T, K = h.shape
    tm = min(2048, T)

    def body(h_ref, w_ref, ab_ref, s_ref):
        prod = jnp.dot(h_ref[...], w_ref[...], preferred_element_type=F32)
        ab_ref[...] = prod.astype(BF16)
        a = prod[:, :GU_TILE]
        s_ref[...] = (a * _sigmoid(a) * prod[:, GU_TILE:]).astype(BF16)

    return pl.pallas_call(
        body, name=name, grid=(T // tm, FFN_HIDDEN // GU_TILE),
        in_specs=[pl.BlockSpec((tm, K), lambda i, j: (i, 0)), pl.BlockSpec((K, 2 * GU_TILE), lambda i, j: (0, j))],
        out_specs=[pl.BlockSpec((tm, 2 * GU_TILE), lambda i, j: (i, j)), pl.BlockSpec((tm, GU_TILE), lambda i, j: (i, j))],
        out_shape=[jax.ShapeDtypeStruct((T, 2 * FFN_HIDDEN), BF16), jax.ShapeDtypeStruct((T, FFN_HIDDEN), BF16)],
        compiler_params=_cp(("parallel", "parallel")),
    )(h, w)


def _swiglu_bwd_fused(dx, w_down, ab, name):
    T, K = dx.shape
    tm = min(2048, T)

    def body(dx_ref, w_ref, ab_ref, o_ref):
        ds = lax.dot_general(dx_ref[...], w_ref[...], _NT, preferred_element_type=F32)
        a = ab_ref[:, :GU_TILE].astype(F32)
        b = ab_ref[:, GU_TILE:].astype(F32)
        sg = _sigmoid(a)
        o_ref[:, :GU_TILE] = (ds * b * sg * (1.0 + a * (1.0 - sg))).astype(BF16)
        o_ref[:, GU_TILE:] = (ds * a * sg).astype(BF16)

    pair = pl.BlockSpec((tm, 2 * GU_TILE), lambda i, j: (i, j))
    return pl.pallas_call(
        body, name=name, grid=(T // tm, FFN_HIDDEN // GU_TILE),
        in_specs=[pl.BlockSpec((tm, K), lambda i, j: (i, 0)), pl.BlockSpec((GU_TILE, K), lambda i, j: (j, 0)), pair],
        out_specs=pair, out_shape=jax.ShapeDtypeStruct((T, 2 * FFN_HIDDEN), BF16),
        compiler_params=_cp(("parallel", "parallel")),
    )(dx, w_down, ab)


def _adamw_update(w_ref, g_ref, m_ref, v_ref, d_ref, nm_ref, nv_ref):
    gv = g_ref[...]
    nm = ADAM_B1 * m_ref[...] + (1.0 - ADAM_B1) * gv
    nv = ADAM_B2 * v_ref[...] + (1.0 - ADAM_B2) * (gv * gv)
    m_hat = nm / (1.0 - ADAM_B1 ** ADAM_STEP)
    v_hat = nv / (1.0 - ADAM_B2 ** ADAM_STEP)
    d_ref[...] = -ADAM_LR * (m_hat / (jnp.sqrt(v_hat) + ADAM_EPS) + ADAM_WD * w_ref[...])
    nm_ref[...] = nm
    nv_ref[...] = nv


def _adamw_many(ws, gs, ms, vs, name):
    n = len(ws)

    def body(*refs):
        ins, outs = refs[:4 * n], refs[4 * n:]
        for t in range(n):
            _adamw_update(ins[t], ins[n + t], ins[2 * n + t], ins[3 * n + t], outs[t], outs[n + t], outs[2 * n + t])

    shapes = [jax.ShapeDtypeStruct(w.shape, F32) for w in ws]
    out = pl.pallas_call(body, name=name, out_shape=shapes * 3, compiler_params=_cp())(*ws, *gs, *ms, *vs)
    return out[:n], out[n:2 * n], out[2 * n:]


def _adamw(w, g, m, v, name):
    R, C = w.shape
    tr = R
    for cand in (256, 352, 128, 64, 8):
        if R > cand and R % cand == 0:
            tr = cand
            break

    def body(w_ref, g_ref, m_ref, v_ref, d_ref, nm_ref, nv_ref):
        _adamw_update(w_ref, g_ref, m_ref, v_ref, d_ref, nm_ref, nv_ref)

    blk = pl.BlockSpec((tr, C), lambda i: (i, 0))
    sh = jax.ShapeDtypeStruct((R, C), F32)
    return pl.pallas_call(
        body, name=name, grid=(R // tr,), in_specs=[blk] * 4, out_specs=[blk] * 3, out_shape=[sh] * 3,
        compiler_params=_cp(("parallel",)),
    )(w, g, m, v)


def _sum_slabs(x, name):
    n, R, C = x.shape
    tr = R
    for cand in (512, 256, 128, 64, 32, 16, 8):
        if R > cand and R % cand == 0:
            tr = cand
            break

    def body(x_ref, o_ref):
        acc = x_ref[0].astype(F32)
        for j in range(1, n):
            acc = acc + x_ref[j].astype(F32)
        o_ref[...] = acc

    return pl.pallas_call(
        body, name=name, grid=(R // tr,), in_specs=[pl.BlockSpec((n, tr, C), lambda i: (0, i, 0))],
        out_specs=pl.BlockSpec((tr, C), lambda i: (i, 0)), out_shape=jax.ShapeDtypeStruct((R, C), F32),
        compiler_params=_cp(("parallel",)),
    )(x)


def _multi_gather(xs, layers, name):
    nt = len(xs)
    shapes = [x.shape if lay is None else x.shape[1:] for x, lay in zip(xs, layers)]

    def body(*refs):
        x_refs, out_refs = refs[:nt], refs[nt:2 * nt]
        send_sems, recv_sems, local_sems = refs[2 * nt:]
        x_, y_, c_ = lax.axis_index("x"), lax.axis_index("y"), lax.axis_index("c")
        me, sibling = (x_, y_, c_), (x_, y_, 1 - c_)
        chips = [(1 - x_, y_), (x_, 1 - y_), (1 - x_, 1 - y_)]

        def own_block(t):
            return x_refs[t] if layers[t] is None else x_refs[t].at[layers[t]]

        def copy(t, k, block, to, own=False):
            px, py, pc = block
            dst = out_refs[t].at[4 * px + 2 * py + pc]
            return pltpu.make_async_remote_copy(
                src_ref=own_block(t) if own else dst, dst_ref=dst,
                send_sem=send_sems.at[t, k], recv_sem=recv_sems.at[t, k],
                device_id=to, device_id_type=pl.DeviceIdType.MESH)

        mine, first, passed = [], [], []
        for t in range(nt):
            mine.append(pltpu.make_async_copy(own_block(t), out_refs[t].at[4 * x_ + 2 * y_ + c_], local_sems.at[t]))
            mine[-1].start()
            first.append([copy(t, 1 + j, me, (*chip, c_), own=True) for j, chip in enumerate(chips)]
                         + [copy(t, 0, me, sibling, own=True)])
            for cp in first[-1]:
                cp.start()
        for t in range(nt):
            for j, chip in enumerate(chips):
                copy(t, 1 + j, (*chip, c_), me).wait_recv()
                passed.append(copy(t, 4 + j, (*chip, c_), sibling))
                passed[-1].start()
        for t in range(nt):
            copy(t, 0, sibling, me).wait_recv()
            for j, chip in enumerate(chips):
                copy(t, 4 + j, (*chip, 1 - c_), me).wait_recv()
        for cp in [c for f in first for c in f] + passed:
            cp.wait_send()
        for cp in mine:
            cp.wait()

    hbm = pl.BlockSpec(memory_space=pl.ANY)
    return pl.pallas_call(
        body, name=name, out_shape=[jax.ShapeDtypeStruct((N_DEV,) + tuple(s), x.dtype) for s, x in zip(shapes, xs)],
        in_specs=[hbm] * nt, out_specs=[hbm] * nt,
        scratch_shapes=[pltpu.SemaphoreType.DMA((nt, 7)), pltpu.SemaphoreType.DMA((nt, 7)),
                        pltpu.SemaphoreType.DMA((nt,))],
    )(*xs)


def _multi_exchange(sends, name):
    nt = len(sends)

    def body(*refs):
        s_refs, r_refs = refs[:nt], refs[nt:2 * nt]
        send_sems, recv_sems, local_sems = refs[2 * nt:]
        x_, y_, c_ = lax.axis_index("x"), lax.axis_index("y"), lax.axis_index("c")
        me = 4 * x_ + 2 * y_ + c_
        mine, out, inc = [], [], []
        for t in range(nt):
            mine.append(pltpu.make_async_copy(s_refs[t].at[me], r_refs[t].at[me], local_sems.at[t]))
            mine[-1].start()
        for k in (2, 4, 6, 3, 5, 7, 1):
            px, py, pc = x_ ^ ((k >> 2) & 1), y_ ^ ((k >> 1) & 1), c_ ^ (k & 1)
            peer = 4 * px + 2 * py + pc
            for t in range(nt):
                def copy(src, dst):
                    return pltpu.make_async_remote_copy(
                        src_ref=s_refs[t].at[src], dst_ref=r_refs[t].at[dst],
                        send_sem=send_sems.at[t, k - 1], recv_sem=recv_sems.at[t, k - 1],
                        device_id=(px, py, pc), device_id_type=pl.DeviceIdType.MESH)

                out.append(copy(peer, me))
                inc.append(copy(me, peer))
        for cp in out:
            cp.start()
        for cp in inc:
            cp.wait_recv()
        for cp in out:
            cp.wait_send()
        for cp in mine:
            cp.wait()

    hbm = pl.BlockSpec(memory_space=pl.ANY)
    return pl.pallas_call(
        body, name=name, out_shape=[jax.ShapeDtypeStruct(s.shape, s.dtype) for s in sends],
        in_specs=[hbm] * nt, out_specs=[hbm] * nt,
        scratch_shapes=[pltpu.SemaphoreType.DMA((nt, N_DEV - 1)), pltpu.SemaphoreType.DMA((nt, N_DEV - 1)),
                        pltpu.SemaphoreType.DMA((nt,))],
    )(*sends)


_HBM = pl.BlockSpec(memory_space=pltpu.HBM)
_SEM = pl.BlockSpec(memory_space=pltpu.SEMAPHORE)
_PEER_ORDER = (2, 4, 6, 3, 5, 7, 1)


def _split_copies(src_refs, land_refs, send_sems, recv_sems, layers, per_peer):
    x_, y_, c_ = lax.axis_index("x"), lax.axis_index("y"), lax.axis_index("c")
    me = 4 * x_ + 2 * y_ + c_
    copies = []
    for k in _PEER_ORDER:
        px, py, pc = x_ ^ ((k >> 2) & 1), y_ ^ ((k >> 1) & 1), c_ ^ (k & 1)
        peer = 4 * px + 2 * py + pc
        for t in range(len(src_refs)):
            if per_peer:
                src = src_refs[t].at[peer]
            else:
                src = src_refs[t] if layers[t] is None else src_refs[t].at[layers[t]]
            copies.append(pltpu.make_async_remote_copy(
                src_ref=src, dst_ref=land_refs[t].at[me],
                send_sem=send_sems.at[t * (N_DEV - 1) + k - 1], recv_sem=recv_sems.at[t * (N_DEV - 1) + k - 1],
                device_id=(px, py, pc), device_id_type=pl.DeviceIdType.MESH))
    return copies


def _own_copies(src_refs, land_refs, sems, layers, per_peer):
    nt = len(src_refs)
    me = 4 * lax.axis_index("x") + 2 * lax.axis_index("y") + lax.axis_index("c")
    copies = []
    for t in range(nt):
        if per_peer:
            src = src_refs[t].at[me]
        else:
            src = src_refs[t] if layers[t] is None else src_refs[t].at[layers[t]]
        copies.append(pltpu.make_async_copy(src, land_refs[t].at[me], sems.at[nt * (N_DEV - 1) + t]))
    return copies


def _split_start(srcs, layers, per_peer, after, name):
    nt = len(srcs)
    if per_peer:
        land_shapes = [s.shape for s in srcs]
    else:
        land_shapes = [(N_DEV,) + tuple(s.shape if lay is None else s.shape[1:]) for s, lay in zip(srcs, layers)]

    def body(*refs):
        src_refs, land_refs = refs[:nt], refs[nt:2 * nt]
        send_sems, recv_sems = refs[2 * nt + 1], refs[2 * nt + 2]
        token = refs[-1]
        for cp in _split_copies(src_refs, land_refs, send_sems, recv_sems, layers, per_peer):
            cp.start()
        for cp in _own_copies(src_refs, land_refs, send_sems, layers, per_peer):
            cp.start()
        token[...] = jnp.zeros_like(token)

    lands = [pltpu.with_memory_space_constraint(lax.empty(s, x.dtype), pltpu.HBM) for s, x in zip(land_shapes, srcs)]
    srcs = [pltpu.with_memory_space_constraint(x, pltpu.HBM) for x in srcs]
    out = pl.pallas_call(
        body, name=name,
        out_shape=(pltpu.SemaphoreType.DMA((nt * N_DEV,)), pltpu.SemaphoreType.DMA((nt * (N_DEV - 1),)),
                   *[pltpu.HBM(x.shape, x.dtype) for x in srcs], *[pltpu.HBM(s, x.dtype) for s, x in zip(land_shapes, srcs)],
                   jax.ShapeDtypeStruct((8, LANES), F32)),
        in_specs=[_HBM] * (2 * nt) + [pl.BlockSpec(memory_space=pl.ANY)],
        out_specs=(_SEM, _SEM, *([_HBM] * (2 * nt)), pl.BlockSpec(memory_space=pltpu.VMEM)),
        input_output_aliases={i: 2 + i for i in range(2 * nt)},
        compiler_params=pltpu.CompilerParams(has_side_effects=pltpu.SideEffectType.DATAFLOW_SIDE_EFFECTING),
    )(*srcs, *lands, after)
    return out[0], out[1], list(out[2:2 + nt]), list(out[2 + nt:2 + 2 * nt]), out[-1]


def _split_wait(started, layers, per_peer, after, name):
    send_sems, recv_sems, srcs, lands, _ = started
    nt = len(srcs)

    def body(*refs):
        src_refs, land_refs = refs[:nt], refs[nt:2 * nt]
        s_sems, r_sems = refs[2 * nt], refs[2 * nt + 1]
        for cp in _split_copies(src_refs, land_refs, s_sems, r_sems, layers, per_peer):
            cp.wait_send()
            cp.wait_recv()
        for cp in _own_copies(src_refs, land_refs, s_sems, layers, per_peer):
            cp.wait()

    out = pl.pallas_call(
        body, name=name,
        out_shape=tuple(pltpu.HBM(x.shape, x.dtype) for x in srcs + lands),
        in_specs=[_HBM] * (2 * nt) + [_SEM, _SEM, pl.BlockSpec(memory_space=pl.ANY)],
        out_specs=tuple([_HBM] * (2 * nt)),
        input_output_aliases={i: i for i in range(2 * nt)},
        compiler_params=pltpu.CompilerParams(has_side_effects=pltpu.SideEffectType.DATAFLOW_SIDE_EFFECTING),
    )(*srcs, *lands, send_sems, recv_sems, after)
    return list(out[nt:])


def _with_own(land, own):
    me = 4 * lax.axis_index("x") + 2 * lax.axis_index("y") + lax.axis_index("c")
    return lax.dynamic_update_slice_in_dim(land, own[None], me, axis=0)


def _runs(mapping):
    runs, c, n = [], 0, len(mapping)
    while c < n:
        if mapping[c] is None:
            c += 1
            continue
        sid, d, lo = mapping[c][0], mapping[c][1] - c, c
        while c < n and mapping[c] is not None and mapping[c][0] == sid and mapping[c][1] - c == d:
            c += 1
        runs.append((lo, c, sid, d))
    return runs


def _tile_plan(mapping, src_widths):
    runs = _runs(mapping)
    plan = []
    for t in range(len(mapping) // LANES):
        pieces = []
        for lo, hi, sid, d in runs:
            lo_t, hi_t = max(lo, t * LANES), min(hi, (t + 1) * LANES)
            if lo_t >= hi_t:
                continue
            a = ((lo_t + d) // LANES) * LANES
            win = min(2 * LANES, src_widths[sid] - a)
            shift = t * LANES + d - a
            pieces.append((sid, a, win, shift, lo_t - t * LANES, hi_t - t * LANES))
        plan.append(pieces)
    return plan


def _reblock(srcs, src_views, outs, out_views, name):
    R = srcs[0].shape[-2]
    tr = min(256, R)
    widths = {sid: srcs[ai].shape[-1] for sid, (ai, _) in src_views.items()}
    plans = [(ai, li, _tile_plan(mapping, widths)) for ai, li, mapping in out_views]
    ns = len(srcs)

    def body(*refs):
        s_refs, o_refs = refs[:ns], refs[ns:]
        cache = {}

        def shift_matrix(win, shift, lo, hi):
            key = (win, shift, lo, hi)
            if key not in cache:
                r = lax.broadcasted_iota(jnp.int32, (win, LANES), 0)
                c = lax.broadcasted_iota(jnp.int32, (win, LANES), 1)
                hit = jnp.logical_and(r - c == shift, jnp.logical_and(c >= lo, c < hi))
                cache[key] = jnp.where(hit, 1.0, 0.0).astype(BF16)
            return cache[key]

        for ai, li, plan in plans:
            for t, pieces in enumerate(plan):
                acc = None
                whole = len(pieces) == 1 and pieces[0][3:] == (0, 0, LANES)
                for sid, a, win, shift, lo, hi in pieces:
                    sa, sl = src_views[sid]
                    if whole:
                        win = LANES
                    src = s_refs[sa][:, a:a + win] if sl is None else s_refs[sa][sl, :, a:a + win]
                    if whole:
                        acc = src
                    else:
                        part = jnp.dot(src, shift_matrix(win, shift, lo, hi), preferred_element_type=F32)
                        acc = part if acc is None else acc + part
                val = jnp.zeros((tr, LANES), BF16) if acc is None else acc.astype(BF16)
                if li is None:
                    o_refs[ai][:, t * LANES:(t + 1) * LANES] = val
                else:
                    o_refs[ai][li, :, t * LANES:(t + 1) * LANES] = val

    def spec(shape):
        if len(shape) == 2:
            return pl.BlockSpec((tr, shape[1]), lambda i: (i, 0))
        return pl.BlockSpec((shape[0], tr, shape[2]), lambda i: (0, i, 0))

    return pl.pallas_call(
        body, name=name, grid=(R // tr,), in_specs=[spec(s.shape) for s in srcs],
        out_specs=[spec(s) for s in outs], out_shape=[jax.ShapeDtypeStruct(s, BF16) for s in outs],
        compiler_params=_cp(("parallel",)),
    )(*srcs)


SHARDED = ("w_in", "w_gate_up", "w_proj_attn", "w_proj_pool", "w_proj_conv", "w_out", "w_down")
WEIGHT_ORDER = ("attn_norm", "w_in", "b_forget", "b_gate", "w_proj_attn", "pool_w", "pool_scale", "w_proj_pool",
                "conv_w", "w_proj_conv", "w_out", "ffn_norm", "w_gate_up", "w_down", "final_norm")
IN_SHARD, IN_SHARD_PAD = IN_COLS // N_DEV, 896
GU_SHARD, GU_SHARD_PAD = 2 * FFN_HIDDEN // N_DEV, 768


def _w_in_col(c):
    if c < GATE_W:
        return c + 3592
    if c < OFF_U:
        return c - OFF_Q
    return c - OFF_U + 1544


def _w_in_full(gathered, name):
    main = [divmod(_w_in_col(c), IN_SHARD) for c in range(MAIN_COLS)]
    fcols = [divmod(1536 + c, IN_SHARD) if c < N_HEADS else None for c in range(LANES)]
    R = gathered.shape[1]
    return _reblock([gathered], {i: (0, i) for i in range(N_DEV)}, [(R, MAIN_COLS), (R, LANES)],
                    [(0, None, main), (1, None, fcols)], name)


def _w_in_slabs(dmain, dwf, name):
    inv = {_w_in_col(c): ("m", c) for c in range(MAIN_COLS)}
    inv.update({1536 + c: ("f", c) for c in range(N_HEADS)})
    views = []
    for i in range(N_DEV):
        mapping = [inv[IN_SHARD * i + j] if j < IN_SHARD else None for j in range(IN_SHARD_PAD)]
        views.append((0, i, mapping))
    R = dmain.shape[0]
    return _reblock([dmain, dwf], {"m": (0, None), "f": (1, None)}, [(N_DEV, R, IN_SHARD_PAD)], views, name)[0]


def _w_gu_full(gathered, name):
    mapping = [divmod(_gu_col(c), GU_SHARD) for c in range(2 * FFN_HIDDEN)]
    R = gathered.shape[1]
    return _reblock([gathered], {i: (0, i) for i in range(N_DEV)}, [(R, 2 * FFN_HIDDEN)], [(0, None, mapping)], name)[0]


def _w_gu_slabs(dw, name):
    inv = {_gu_col(c): c for c in range(2 * FFN_HIDDEN)}
    views = [(0, i, [("w", inv[GU_SHARD * i + j]) if j < GU_SHARD else None for j in range(GU_SHARD_PAD)])
             for i in range(N_DEV)]
    R = dw.shape[0]
    return _reblock([dw], {"w": (0, None)}, [(N_DEV, R, GU_SHARD_PAD)], views, name)[0]


def _layer_fwd(x, W, n_seq, l, h1=None, next_norm=None):
    T = x.shape[0]
    sfx = f"_l{l}"
    if h1 is None:
        h1 = _rms_fwd(x, W["attn_norm"], "rms1" + sfx)
    proj = _matmul(h1, W["w_main"], mode="nn", out_dtype=BF16, name="proj_main" + sfx)
    f = _matmul(h1, W["w_f"], mode="nn", out_dtype=F32, name="proj_f" + sfx)
    qa, ka, va = _fox_prep(f, W["b_forget"], proj, n_seq, "fox_prep" + sfx)
    oa, oa32, lse = _attn_fwd2(qa, ka, va, n_seq, "attn_fwd" + sfx)
    if "late" in W:
        W.update(W.pop("late")(oa))
    ob = _pool_fwd(proj, W["pool_w"], W["pool_scale"], n_seq, "pool_fwd" + sfx)
    oc = _conv_fwd(proj, W["conv_w"], n_seq, "conv_fwd" + sfx)
    mixed = _mix_fwd(oa, ob, oc, W["w_proj_attn"], W["w_proj_pool"], W["w_proj_conv"], proj, W["b_gate"],
                     "mix_fwd" + sfx)
    x2, h2 = _matmul(mixed, W["w_out"], mode="nn", out_dtype=F32, name="out_proj" + sfx, tm=1024, tn=1024,
                     residual=x, rms_g=W["ffn_norm"])
    ab, s = _gate_up_swiglu(h2, W["w_gate_up"], "gate_up" + sfx)
    x3 = _matmul(s, W["w_down"], mode="nn", out_dtype=F32, name="down" + sfx, tm=1024, tn=1024, tk=1408,
                 residual=x2, rms_g=next_norm)
    x3, h1_next = x3 if next_norm is not None else (x3, None)
    saved = dict(x=x, h1=h1, proj=proj, f=f, qa=qa, ka=ka, oa=oa, oa32=oa32, lse=lse, ob=ob, oc=oc, mixed=mixed, x2=x2,
                 h2=h2, ab=ab, s=s)
    return x3, saved, h1_next


def _layer_bwd(dx3, dx3b, W, sv, n_seq, l, stage=None):
    T = dx3.shape[0]
    sfx = f"_l{l}"
    G = {}
    stage = stage or (lambda l, group, G, W: W)
    dab = _swiglu_bwd_fused(dx3b, W["w_down"], sv["ab"], "d_ab" + sfx)
    G["w_down"] = _matmul(sv["s"], dx3b, mode="tn", out_dtype=BF16, name="dw_down" + sfx, tm=256, tn=1024)
    dh2 = _matmul(dab, W["w_gate_up"], mode="nt", out_dtype=BF16, name="d_h2" + sfx, tm=1024, tn=1024, tk=1408)
    G["w_gate_up"] = _matmul(sv["h2"], dab, mode="tn", out_dtype=BF16, name="dw_gate_up" + sfx, tm=1024)
    W = stage(l, "ffn", G, W)
    dx2, dx2b, G["ffn_norm"] = _rms_bwd(sv["x2"], W["ffn_norm"], dh2, dx3, "rms2_bwd" + sfx)
    dmixed = _matmul(dx2b, W["w_out"], mode="nt", out_dtype=BF16, name="d_mixed" + sfx)
    G["w_out"] = _matmul(sv["mixed"], dx2b, mode="tn", out_dtype=BF16, name="dw_out" + sfx, tm=1024)
    dya, dyb, dyc, dg, G["b_gate"] = _mix_bwd(sv["oa"], sv["ob"], sv["oc"], W["w_proj_attn"], W["w_proj_pool"],
                                              W["w_proj_conv"], sv["proj"], W["b_gate"], dmixed, "mix_bwd" + sfx)
    douts = {}
    for br, dy, o in (("attn", dya, sv["oa"]), ("pool", dyb, sv["ob"]), ("conv", dyc, sv["oc"])):
        douts[br] = _matmul(dy, W["w_proj_" + br], mode="nt", out_dtype=BF16, name=f"d_{br}_out" + sfx)
        G["w_proj_" + br] = _matmul(o, dy, mode="tn", out_dtype=BF16, name=f"dw_proj_{br}" + sfx, tm=512)
    W = stage(l, "mix", G, W)
    dcv, dcb, dcc, G["conv_w"] = _conv_bwd(sv["proj"], douts["conv"], W["conv_w"], n_seq, "conv_bwd" + sfx)
    du, G["pool_w"], G["pool_scale"] = _pool_bwd(sv["proj"], douts["pool"], W["pool_w"], W["pool_scale"], n_seq,
                                                 "pool_bwd" + sfx)
    dq, dk, dv, dFk = _attn_bwd(sv["qa"], sv["ka"], sv["proj"], douts["attn"], sv["oa32"], sv["lse"], n_seq,
                                "attn_bwd" + sfx)
    dF = jnp.pad(dFk.reshape(N_HEADS, T).T, ((0, 0), (0, LANES - N_HEADS)))
    df, G["b_forget"] = _fox_cumsum_bwd(sv["f"], W["b_forget"], dF, n_seq, "fox_cumsum_bwd" + sfx)
    dproj = jnp.concatenate([dg, dq, dk, dv, du, dcv, dcb, dcc], axis=1)
    G["w_main"] = _matmul(sv["h1"], dproj, mode="tn", out_dtype=BF16, name="dw_main" + sfx, tm=1024)
    G["w_f"] = _matmul(sv["h1"], df, mode="tn", out_dtype=BF16, name="dw_f" + sfx, tm=1024)
    W = stage(l, "w_in", G, W)
    dh1 = _matmul(df, W["w_f"], mode="nt", out_dtype=F32, name="d_h1_f" + sfx)
    dh1 = _matmul(dproj, W["w_main"], mode="nt", out_dtype=F32, name="d_h1_main" + sfx, tm=1024, tn=1024, tk=1664,
                  residual=dh1)
    dx, dxb, G["attn_norm"] = _rms_bwd(sv["x"], W["attn_norm"], dh1, dx2, "rms1_bwd" + sfx)
    return dx, dxb, G


def _replicated_operands(rep, l):
    W = {}
    W["attn_norm"], W["ffn_norm"] = rep["attn_norm"][l], rep["ffn_norm"][l]
    W["b_forget"] = jnp.pad(rep["b_forget"][l].reshape(1, N_HEADS), ((0, 0), (0, LANES - N_HEADS)))
    W["b_gate"] = rep["b_gate"][l].reshape(1, GATE_W)
    W["pool_w"] = rep["pool_w"][l].astype(BF16)
    W["pool_scale"] = rep["pool_scale"][l].reshape(1, BRANCH_W)
    return W


def _local_step(x, target, get_W, attn_norms, final_norm, stage=None):
    n_seq, S, Dm = x.shape
    T = n_seq * S
    xt = x.reshape(T, Dm)
    saved, Ws, h1 = [], [], None
    for l in range(DEPTH):
        Ws.append(get_W(l, xt))
        next_norm = attn_norms[l + 1] if l + 1 < DEPTH else None
        xt, sv, h1 = _layer_fwd(xt, Ws[l], n_seq, l, h1, next_norm)
        saved.append(sv)
    loss, dx, dxb, g_final = _loss_head(xt, final_norm, target.reshape(T, Dm), "loss_head")
    grads = [None] * DEPTH
    for l in reversed(range(DEPTH)):
        dx, dxb, grads[l] = _layer_bwd(dx, dxb, Ws[l], saved[l], n_seq, l, stage)
    return loss, dx.reshape(n_seq, S, Dm), grads, g_final


def _padded_shards(weights):
    sh = {n: weights[n].astype(BF16) for n in SHARDED}
    sh["w_in"] = jnp.pad(sh["w_in"], ((0, 0), (0, 0), (0, IN_SHARD_PAD - IN_SHARD)))
    sh["w_gate_up"] = jnp.pad(sh["w_gate_up"], ((0, 0), (0, 0), (0, GU_SHARD_PAD - GU_SHARD)))
    return sh


def _full_operands(g, l):
    W = {}
    if "w_in" in g:
        W["w_main"], W["w_f"] = _w_in_full(g["w_in"], f"w_in_full_l{l}")
    if "w_gate_up" in g:
        W["w_gate_up"] = _w_gu_full(g["w_gate_up"], f"w_gate_up_full_l{l}")
    for n in ("w_proj_attn", "w_proj_pool", "w_proj_conv"):
        if n in g:
            W[n] = jnp.transpose(g[n], (1, 0, 2)).reshape(BRANCH_W, D_MODEL)
    if "w_out" in g:
        W["w_out"] = g["w_out"].reshape(D_MODEL, D_MODEL)
    if "w_down" in g:
        W["w_down"] = g["w_down"].reshape(FFN_HIDDEN, D_MODEL)
    return W


GRAD_GROUPS = {"ffn": ("w_down", "w_gate_up"),
               "mix": ("w_out", "w_proj_attn", "w_proj_pool", "w_proj_conv"),
               "w_in": ("w_in",)}


def _grad_slabs(G, n, l):
    if n == "w_in":
        return _w_in_slabs(G["w_main"], G["w_f"], f"w_in_slabs_l{l}")
    if n == "w_gate_up":
        return _w_gu_slabs(G["w_gate_up"], f"w_gate_up_slabs_l{l}")
    if n == "w_out":
        return G["w_out"].reshape(N_DEV, D_MODEL // N_DEV, D_MODEL)
    if n == "w_down":
        return G["w_down"].reshape(N_DEV, FFN_HIDDEN // N_DEV, D_MODEL)
    return jnp.transpose(G[n].reshape(BRANCH_W, N_DEV, D_MODEL // N_DEV), (1, 0, 2))


def _sum_layer_grads(recv, l):
    out = {n: _sum_slabs(r, f"sum_{n}_l{l}") for n, r in recv.items()}
    if "w_in" in out:
        out["w_in"] = out["w_in"][:, :IN_SHARD]
    if "w_gate_up" in out:
        out["w_gate_up"] = out["w_gate_up"][:, :GU_SHARD]
    return out


def _sum_small(xs, name):
    def body(*refs):
        for x_ref, o_ref in zip(refs[:len(xs)], refs[len(xs):]):
            acc = x_ref[0]
            for j in range(1, N_DEV):
                acc = acc + x_ref[j]
            o_ref[...] = acc

    return pl.pallas_call(
        body, name=name, out_shape=[jax.ShapeDtypeStruct(x.shape[1:], F32) for x in xs],
        compiler_params=_cp(),
    )(*xs)


def _as_2d(a):
    if a.ndim == 1:
        return a.reshape(1, -1)
    return a.reshape(-1, a.shape[-1])


def kernel(x, attn_norm, w_in, b_forget, b_gate, w_proj_attn, pool_w, pool_scale, w_proj_pool, conv_w, w_proj_conv, w_out, ffn_norm, w_gate_up, w_down, final_norm, loss_target, m_attn_norm, m_w_in, m_b_forget, m_b_gate, m_w_proj_attn, m_pool_w, m_pool_scale, m_w_proj_pool, m_conv_w, m_w_proj_conv, m_w_out, m_ffn_norm, m_w_gate_up, m_w_down, m_final_norm, v_attn_norm, v_w_in, v_b_forget, v_b_gate, v_w_proj_attn, v_pool_w, v_pool_scale, v_w_proj_pool, v_conv_w, v_w_proj_conv, v_w_out, v_ffn_norm, v_w_gate_up, v_w_down, v_final_norm):
    weights = dict(attn_norm=attn_norm, w_in=w_in, b_forget=b_forget, b_gate=b_gate, w_proj_attn=w_proj_attn,
                   pool_w=pool_w, pool_scale=pool_scale, w_proj_pool=w_proj_pool, conv_w=conv_w,
                   w_proj_conv=w_proj_conv, w_out=w_out, ffn_norm=ffn_norm, w_gate_up=w_gate_up, w_down=w_down,
                   final_norm=final_norm)
    moments_m = dict(attn_norm=m_attn_norm, w_in=m_w_in, b_forget=m_b_forget, b_gate=m_b_gate,
                     w_proj_attn=m_w_proj_attn, pool_w=m_pool_w, pool_scale=m_pool_scale, w_proj_pool=m_w_proj_pool,
                     conv_w=m_conv_w, w_proj_conv=m_w_proj_conv, w_out=m_w_out, ffn_norm=m_ffn_norm,
                     w_gate_up=m_w_gate_up, w_down=m_w_down, final_norm=m_final_norm)
    moments_v = dict(attn_norm=v_attn_norm, w_in=v_w_in, b_forget=v_b_forget, b_gate=v_b_gate,
                     w_proj_attn=v_w_proj_attn, pool_w=v_pool_w, pool_scale=v_pool_scale, w_proj_pool=v_w_proj_pool,
                     conv_w=v_conv_w, w_proj_conv=v_w_proj_conv, w_out=v_w_out, ffn_norm=v_ffn_norm,
                     w_gate_up=v_w_gate_up, w_down=v_w_down, final_norm=v_final_norm)

    sh = _padded_shards(weights)
    names = list(SHARDED)
    rest = [n for n in names if n != "w_in"]
    me = 4 * lax.axis_index("x") + 2 * lax.axis_index("y") + lax.axis_index("c")
    w_in0, conv_all = _multi_gather([sh["w_in"], conv_w], [0, None], "gather_w_in_l0")
    started, after = {}, w_in0
    for l in range(DEPTH):
        for group, gnames in (("w_in", ["w_in"]), ("rest", rest)):
            if (l, group) != (0, "w_in"):
                started[l, group] = _split_start([sh[n] for n in gnames], [l] * len(gnames), False, after,
                                                 f"gather_start_{group}_l{l}")
                after = started[l, group][4]
    last_token = after

    def get_W(l, xt):
        if l == 0:
            w_in = w_in0
        else:
            w_in = _split_wait(started[l, "w_in"], [l], False, xt, f"gather_wait_w_in_l{l}")[0]
        W = _full_operands({"w_in": w_in}, l)

        def late(after):
            lands = _split_wait(started[l, "rest"], [l] * len(rest), False, after, f"gather_wait_rest_l{l}")
            return _full_operands(dict(zip(rest, lands)), l)

        W["late"] = late
        W.update(_replicated_operands(weights, l))
        W["conv_w"] = jnp.transpose(conv_all[:, l], (1, 0, 2)).reshape(CONV_K, BRANCH_W)
        if l == 0:
            W["attn_norm"] = W["attn_norm"] + last_token[0, 0]
        return W

    exchanges = []

    def stage(l, group, G, W):
        gnames = GRAD_GROUPS[group]
        slabs = [_grad_slabs(G, n, l) for n in gnames]
        started = _split_start(slabs, None, True, slabs[0], f"exchange_start_{group}_l{l}")
        exchanges.append((l, group, gnames, slabs, started))
        tie = {"ffn": "ffn_norm", "mix": "conv_w", "w_in": "w_f"}[group]
        W = dict(W)
        W[tie] = W[tie] + started[4][0, 0].astype(W[tie].dtype)
        return W

    loss_part, grad_x, grads, g_final = _local_step(x, loss_target, get_W, attn_norm, final_norm, stage)
    after = grad_x
    for l, group, gnames, slabs, started in exchanges:
        lands = _split_wait(started, None, True, after, f"exchange_wait_{group}_l{l}")
        grads[l].update(_sum_layer_grads(dict(zip(gnames, lands)), l))
    gw = {n: jnp.stack([grads[l][n] for l in range(DEPTH)]) for n in SHARDED}

    small = ("attn_norm", "b_forget", "b_gate", "pool_w", "pool_scale", "ffn_norm", "conv_w")
    parts = [jnp.stack([grads[l][n] for l in range(DEPTH)]) for n in small] + [g_final, loss_part]
    gathered = _multi_gather(parts, [None] * len(parts), "gather_small_grads")
    summed = _sum_small(gathered, "sum_small_grads")
    for n, s in zip(small, summed):
        gw[n] = s
    gw["attn_norm"], gw["ffn_norm"] = gw["attn_norm"][:, 0], gw["ffn_norm"][:, 0]
    gw["b_forget"] = gw["b_forget"][:, 0, :N_HEADS]
    gw["b_gate"], gw["pool_scale"] = gw["b_gate"][:, 0], gw["pool_scale"][:, 0]
    gw["conv_w"] = lax.dynamic_slice_in_dim(gw["conv_w"], me * (BRANCH_W // N_DEV), BRANCH_W // N_DEV, axis=2)
    gw["final_norm"] = summed[-2][0]
    loss = summed[-1][0, 0]

    deltas, new_m, new_v = {}, {}, {}
    for n in SHARDED:
        shape = weights[n].shape
        d, nm, nv = _adamw(_as_2d(weights[n]), _as_2d(gw[n]), _as_2d(moments_m[n]), _as_2d(moments_v[n]),
                           "adamw_" + n)
        deltas[n], new_m[n], new_v[n] = d.reshape(shape), nm.reshape(shape), nv.reshape(shape)
    rest_names = [n for n in WEIGHT_ORDER if n not in SHARDED]
    ds, nms, nvs = _adamw_many(*[[_as_2d(src[n]) for n in rest_names] for src in (weights, gw, moments_m, moments_v)],
                               "adamw_small")
    for n, d, nm, nv in zip(rest_names, ds, nms, nvs):
        shape = weights[n].shape
        deltas[n], new_m[n], new_v[n] = d.reshape(shape), nm.reshape(shape), nv.reshape(shape)

    return (loss, grad_x, *[gw[n] for n in WEIGHT_ORDER], *[deltas[n] for n in WEIGHT_ORDER],
            *[new_m[n] for n in WEIGHT_ORDER], *[new_v[n] for n in WEIGHT_ORDER])
```

```python
import functools

import numpy as np
import jax
import jax.numpy as jnp
from jax import lax
from jax.experimental import pallas as pl
from jax.experimental.pallas import tpu as pltpu

F32 = jnp.float32
BF16 = jnp.bfloat16

N_DEV = 8
D_MODEL = 1024
DEPTH = 2
N_HEADS = 8
HEAD_DIM = 64
BRANCH_W = 512
POOL_WINDOWS = (2, 4, 8, 16)
POOL_GD = 128
CONV_K = 3
FFN_HIDDEN = 2816
GATE_W = 3 * D_MODEL
IN_COLS = 6664
MAIN_COLS = GATE_W + 7 * BRANCH_W
RMS_EPS = 1e-6
NEG_INF = -1e30

ADAM_LR = 0.001
ADAM_B1 = 0.9
ADAM_B2 = 0.999
ADAM_EPS = 1e-08
ADAM_WD = 0.01
ADAM_STEP = 10

LANES = 128
VMEM_LIMIT = 56 * 1024 * 1024
ATT_BLK = 256
CUM_BLK = 256

OFF_G, OFF_Q, OFF_K, OFF_V, OFF_U, OFF_CV, OFF_CB, OFF_CC = (
    0, 3072, 3584, 4096, 4608, 5120, 5632, 6144)


def _cp(sem=None):
    return pltpu.CompilerParams(dimension_semantics=sem, vmem_limit_bytes=VMEM_LIMIT)


def _sigmoid(z):
    return 1.0 / (1.0 + jnp.exp(-z))


def _matmul(a, b, *, mode, out_dtype, name, tm=2048, tn=512, tk=None, residual=None, rms_g=None):
    if mode == "nn":
        (M, K), N = a.shape, b.shape[1]
    elif mode == "nt":
        (M, K), N = a.shape, b.shape[0]
    else:
        (K, M), N = a.shape, b.shape[1]
    tm, tn, tk = min(tm, M), min(tn, N), K if tk is None else min(tk, K)
    assert M % tm == 0 and N % tn == 0 and K % tk == 0, (name, M, N, K, tm, tn, tk)
    nk = K // tk
    if mode == "nn":
        a_spec = pl.BlockSpec((tm, tk), lambda i, j, k: (i, k))
        b_spec = pl.BlockSpec((tk, tn), lambda i, j, k: (k, j))
        dims = (((1,), (0,)), ((), ()))
    elif mode == "nt":
        a_spec = pl.BlockSpec((tm, tk), lambda i, j, k: (i, k))
        b_spec = pl.BlockSpec((tn, tk), lambda i, j, k: (j, k))
        dims = (((1,), (1,)), ((), ()))
    else:
        a_spec = pl.BlockSpec((tk, tm), lambda i, j, k: (k, i))
        b_spec = pl.BlockSpec((tk, tn), lambda i, j, k: (k, j))
        dims = (((0,), (0,)), ((), ()))
    o_spec = pl.BlockSpec((tm, tn), lambda i, j, k: (i, j))
    has_res, has_norm = residual is not None, rms_g is not None
    assert not has_norm or tn == N, (name, tn, N)

    def body(*refs):
        a_ref, b_ref = refs[:2]
        r_ref = refs[2] if has_res else None
        g_ref = refs[2 + has_res] if has_norm else None
        o_ref = refs[2 + has_res + has_norm]
        h_ref = refs[3 + has_res + has_norm] if has_norm else None

        def finish(acc):
            if has_res:
                acc = acc + r_ref[...].astype(F32)
            o_ref[...] = acc.astype(out_dtype)
            if has_norm:
                r = lax.rsqrt(jnp.mean(acc * acc, axis=-1, keepdims=True) + RMS_EPS)
                h_ref[...] = ((acc * r) * g_ref[...]).astype(BF16)

        prod = lax.dot_general(a_ref[...], b_ref[...], dims, preferred_element_type=F32)
        if nk == 1:
            finish(prod)
            return
        acc_ref = refs[-1]
        k = pl.program_id(2)

        @pl.when(k == 0)
        def _():
            acc_ref[...] = prod

        @pl.when(jnp.logical_and(k > 0, k < nk - 1))
        def _():
            acc_ref[...] += prod

        @pl.when(k == nk - 1)
        def _():
            finish(acc_ref[...] + prod)

    in_specs = [a_spec, b_spec] + ([o_spec] if has_res else [])
    args = (a, b) + ((residual,) if has_res else ())
    out_specs, out_shape = o_spec, jax.ShapeDtypeStruct((M, N), out_dtype)
    if has_norm:
        in_specs.append(pl.BlockSpec((1, N), lambda i, j, k: (0, 0)))
        args += (rms_g.reshape(1, N),)
        out_specs, out_shape = [o_spec, o_spec], [out_shape, jax.ShapeDtypeStruct((M, N), BF16)]
    return pl.pallas_call(
        body, name=name, grid=(M // tm, N // tn, nk), in_specs=in_specs, out_specs=out_specs,
        out_shape=out_shape,
        scratch_shapes=[pltpu.VMEM((tm, tn), F32)] if nk > 1 else [],
        compiler_params=_cp(("parallel", "parallel", "arbitrary")),
    )(*args)


def _rms_fwd(x, g, name):
    T, Dm = x.shape
    tm = min(512, T)

    def body(x_ref, g_ref, h_ref):
        xf = x_ref[...]
        r = lax.rsqrt(jnp.mean(xf * xf, axis=-1, keepdims=True) + RMS_EPS)
        h_ref[...] = ((xf * r) * g_ref[...]).astype(BF16)

    return pl.pallas_call(
        body, name=name, grid=(T // tm,),
        in_specs=[pl.BlockSpec((tm, Dm), lambda i: (i, 0)), pl.BlockSpec((1, Dm), lambda i: (0, 0))],
        out_specs=pl.BlockSpec((tm, Dm), lambda i: (i, 0)),
        out_shape=jax.ShapeDtypeStruct((T, Dm), BF16),
        compiler_params=_cp(("parallel",)),
    )(x, g.reshape(1, Dm))


def _rms_bwd(x, g, dh, dres, name):
    T, Dm = x.shape
    tm = min(512, T)

    def body(x_ref, g_ref, dh_ref, dres_ref, dx_ref, dxb_ref, dg_ref):
        i = pl.program_id(0)
        xf = x_ref[...]
        r = lax.rsqrt(jnp.mean(xf * xf, axis=-1, keepdims=True) + RMS_EPS)
        xn = xf * r
        dhf = dh_ref[...].astype(F32)
        dxn = dhf * g_ref[...]
        c = jnp.mean(dxn * xn, axis=-1, keepdims=True)
        dx = dres_ref[...] + r * (dxn - xn * c)
        dx_ref[...] = dx
        dxb_ref[...] = dx.astype(BF16)
        part = jnp.sum(dhf * xn, axis=0, keepdims=True)

        @pl.when(i == 0)
        def _():
            dg_ref[...] = part

        @pl.when(i > 0)
        def _():
            dg_ref[...] += part

    row = pl.BlockSpec((tm, Dm), lambda i: (i, 0))
    vec = pl.BlockSpec((1, Dm), lambda i: (0, 0))
    return pl.pallas_call(
        body, name=name, grid=(T // tm,), in_specs=[row, vec, row, row], out_specs=[row, row, vec],
        out_shape=[jax.ShapeDtypeStruct((T, Dm), F32), jax.ShapeDtypeStruct((T, Dm), BF16),
                   jax.ShapeDtypeStruct((1, Dm), F32)],
        compiler_params=_cp(("arbitrary",)),
    )(x, g.reshape(1, Dm), dh, dres)


def _loss_head(x, g, target, name):
    T, Dm = x.shape
    tm = min(512, T)

    def body(x_ref, g_ref, t_ref, loss_ref, dx_ref, dxb_ref, dg_ref):
        i = pl.program_id(0)
        xf = x_ref[...]
        gv = g_ref[...]
        r = lax.rsqrt(jnp.mean(xf * xf, axis=-1, keepdims=True) + RMS_EPS)
        xn = xf * r
        diff = xn * gv - t_ref[...]
        per_tok = jnp.mean(diff * diff, axis=-1, keepdims=True)
        lpart = 0.5 * jnp.sum(per_tok, axis=0, keepdims=True) + jnp.zeros((1, LANES), F32)
        dy = diff * (1.0 / Dm)
        dxn = dy * gv
        c = jnp.mean(dxn * xn, axis=-1, keepdims=True)
        dx = r * (dxn - xn * c)
        dx_ref[...] = dx
        dxb_ref[...] = dx.astype(BF16)
        part = jnp.sum(dy * xn, axis=0, keepdims=True)

        @pl.when(i == 0)
        def _():
            dg_ref[...] = part
            loss_ref[...] = lpart

        @pl.when(i > 0)
        def _():
            dg_ref[...] += part
            loss_ref[...] += lpart

    row = pl.BlockSpec((tm, Dm), lambda i: (i, 0))
    vec = pl.BlockSpec((1, Dm), lambda i: (0, 0))
    lsp = pl.BlockSpec((1, LANES), lambda i: (0, 0))
    return pl.pallas_call(
        body, name=name, grid=(T // tm,), in_specs=[row, vec, row], out_specs=[lsp, row, row, vec],
        out_shape=[jax.ShapeDtypeStruct((1, LANES), F32), jax.ShapeDtypeStruct((T, Dm), F32),
                   jax.ShapeDtypeStruct((T, Dm), BF16), jax.ShapeDtypeStruct((1, Dm), F32)],
        compiler_params=_cp(("arbitrary",)),
    )(x, g.reshape(1, Dm), target)


def _split_bf16(v):
    hi = v.astype(BF16)
    r1 = v - hi.astype(F32)
    mid = r1.astype(BF16)
    lo = (r1 - mid.astype(F32)).astype(BF16)
    return hi, mid, lo


def _tri_dot(tri, v):
    hi, mid, lo = _split_bf16(v)
    dot = functools.partial(jnp.dot, preferred_element_type=F32)
    return dot(tri, hi) + dot(tri, mid) + dot(tri, lo)


def _log_sigmoid(z):
    return jnp.minimum(z, 0.0) - jnp.log(1.0 + jnp.exp(-jnp.abs(z)))


def _fox_cumsum_fwd(f, bf, n_seq, name):
    T = f.shape[0]
    S = T // n_seq
    c = min(CUM_BLK, S)

    def body(f_ref, b_ref, out_ref):
        ri = lax.broadcasted_iota(jnp.int32, (c, c), 0)
        ci = lax.broadcasted_iota(jnp.int32, (c, c), 1)
        tri = (ri >= ci).astype(BF16)
        carry = jnp.zeros((1, LANES), F32)
        for j in range(S // c):
            lf = _log_sigmoid(f_ref[j * c:(j + 1) * c, :] + b_ref[...])
            out_ref[j * c:(j + 1) * c, :] = _tri_dot(tri, lf) + carry
            carry = carry + jnp.sum(lf, axis=0, keepdims=True)

    blk = pl.BlockSpec((S, LANES), lambda b: (b, 0))
    return pl.pallas_call(
        body, name=name, grid=(n_seq,), in_specs=[blk, pl.BlockSpec((1, LANES), lambda b: (0, 0))],
        out_specs=blk, out_shape=jax.ShapeDtypeStruct((T, LANES), F32),
        compiler_params=_cp(("parallel",)),
    )(f, bf)


def _fox_cumsum_bwd(f, bf, dF, n_seq, name):
    T = f.shape[0]
    S = T // n_seq
    c = min(CUM_BLK, S)

    def body(f_ref, b_ref, dF_ref, df_ref, db_ref):
        b = pl.program_id(0)
        ri = lax.broadcasted_iota(jnp.int32, (c, c), 0)
        ci = lax.broadcasted_iota(jnp.int32, (c, c), 1)
        tri = (ri <= ci).astype(BF16)
        carry = jnp.zeros((1, LANES), F32)
        dbp = jnp.zeros((1, LANES), F32)
        for j in reversed(range(S // c)):
            dFc = dF_ref[j * c:(j + 1) * c, :]
            dlf = _tri_dot(tri, dFc) + carry
            carry = carry + jnp.sum(dFc, axis=0, keepdims=True)
            z = f_ref[j * c:(j + 1) * c, :] + b_ref[...]
            dz = dlf * _sigmoid(-z)
            df_ref[j * c:(j + 1) * c, :] = dz.astype(BF16)
            dbp = dbp + jnp.sum(dz, axis=0, keepdims=True)

        @pl.when(b == 0)
        def _():
            db_ref[...] = dbp

        @pl.when(b > 0)
        def _():
            db_ref[...] += dbp

    blk = pl.BlockSpec((S, LANES), lambda b: (b, 0))
    vec = pl.BlockSpec((1, LANES), lambda b: (0, 0))
    return pl.pallas_call(
        body, name=name, grid=(n_seq,), in_specs=[blk, vec, blk], out_specs=[blk, vec],
        out_shape=[jax.ShapeDtypeStruct((T, LANES), BF16), jax.ShapeDtypeStruct((1, LANES), F32)],
        compiler_params=_cp(("arbitrary",)),
    )(f, bf, dF)


def _pair_masks():
    lane = lax.broadcasted_iota(jnp.int32, (1, LANES), 1)
    lo = lane < HEAD_DIM
    return lo, jnp.logical_not(lo)


def _attn_logits(q, k, fq, fk, sel, mask, scale):
    qm = jnp.where(sel, q, jnp.zeros_like(q))
    s = lax.dot_general(qm, k, (((1,), (1,)), ((), ())), preferred_element_type=F32) * scale
    s = s + fq - fk
    return jnp.where(mask, s, NEG_INF)


def _causal_mask(qi, ki, blk):
    row = qi * blk + lax.broadcasted_iota(jnp.int32, (blk, blk), 0)
    col = ki * blk + lax.broadcasted_iota(jnp.int32, (blk, blk), 1)
    return col <= row


def _attn_fwd(proj, Fq, Fk, n_seq, name):
    T = proj.shape[0]
    S = T // n_seq
    blk = min(ATT_BLK, S)
    nb = S // blk
    scale = HEAD_DIM ** -0.5
    qc, kc, vc = OFF_Q // LANES, OFF_K // LANES, OFF_V // LANES

    def body(q_ref, k_ref, v_ref, fq_ref, fk_ref, o_ref, o32_ref, lse_ref, m_s, l_s, acc_s):
        qi, ki = pl.program_id(2), pl.program_id(3)

        @pl.when(ki == 0)
        def _():
            m_s[...] = jnp.full_like(m_s, NEG_INF)
            l_s[...] = jnp.zeros_like(l_s)
            acc_s[...] = jnp.zeros_like(acc_s)

        @pl.when(ki <= qi)
        def _():
            q, k, v = q_ref[...], k_ref[...], v_ref[...]
            mask = _causal_mask(qi, ki, blk)
            for hh, sel in enumerate(_pair_masks()):
                s = _attn_logits(q, k, fq_ref[hh], fk_ref[hh], sel, mask, scale)
                m_prev = m_s[hh]
                m_new = jnp.maximum(m_prev, jnp.max(s, axis=-1, keepdims=True))
                alpha = jnp.exp(m_prev - m_new)
                p = jnp.exp(s - m_new)
                l_s[hh] = alpha * l_s[hh] + jnp.sum(p, axis=-1, keepdims=True)
                p_hi = p.astype(BF16)
                p_lo = (p - p_hi.astype(F32)).astype(BF16)
                pv = jnp.dot(p_hi, v, preferred_element_type=F32) + jnp.dot(p_lo, v, preferred_element_type=F32)
                acc_s[hh] = alpha * acc_s[hh] + pv
                m_s[hh] = m_new

        @pl.when(ki == qi)
        def _():
            lo, _ = _pair_masks()
            o = jnp.where(lo, acc_s[0] / l_s[0], acc_s[1] / l_s[1])
            o_ref[...] = o.astype(BF16)
            o32_ref[...] = o
            lse_ref[0] = m_s[0] + jnp.log(l_s[0])
            lse_ref[1] = m_s[1] + jnp.log(l_s[1])

    grid = (n_seq, N_HEADS // 2, nb, nb)
    return pl.pallas_call(
        body, name=name, grid=grid,
        in_specs=[
            pl.BlockSpec((blk, LANES), lambda b, j, qi, ki: (b * nb + qi, qc + j)),
            pl.BlockSpec((blk, LANES), lambda b, j, qi, ki: (b * nb + jnp.minimum(ki, qi), kc + j)),
            pl.BlockSpec((blk, LANES), lambda b, j, qi, ki: (b * nb + jnp.minimum(ki, qi), vc + j)),
            pl.BlockSpec((2, blk, 1), lambda b, j, qi, ki: (j, b * nb + qi, 0)),
            pl.BlockSpec((2, 1, blk), lambda b, j, qi, ki: (j, 0, b * nb + jnp.minimum(ki, qi))),
        ],
        out_specs=[
            pl.BlockSpec((blk, LANES), lambda b, j, qi, ki: (b * nb + qi, j)),
            pl.BlockSpec((blk, LANES), lambda b, j, qi, ki: (b * nb + qi, j)),
            pl.BlockSpec((2, blk, 1), lambda b, j, qi, ki: (j, b * nb + qi, 0)),
        ],
        out_shape=[jax.ShapeDtypeStruct((T, BRANCH_W), BF16), jax.ShapeDtypeStruct((T, BRANCH_W), F32),
                   jax.ShapeDtypeStruct((N_HEADS, T, 1), F32)],
        scratch_shapes=[pltpu.VMEM((2, blk, 1), F32), pltpu.VMEM((2, blk, 1), F32),
                        pltpu.VMEM((2, blk, LANES), F32)],
        compiler_params=_cp(("parallel", "parallel", "parallel", "arbitrary")),
    )(proj, proj, proj, Fq, Fk)


def _attn_delta(do, o, name):
    T = do.shape[0]
    tm = min(512, T)

    def body(do_ref, o_ref, d_ref):
        prod = do_ref[...].astype(F32) * o_ref[...].astype(F32)
        lo, hi = _pair_masks()
        for j in range(N_HEADS // 2):
            pj = prod[:, j * LANES:(j + 1) * LANES]
            d_ref[2 * j] = jnp.sum(jnp.where(lo, pj, 0.0), axis=-1, keepdims=True)
            d_ref[2 * j + 1] = jnp.sum(jnp.where(hi, pj, 0.0), axis=-1, keepdims=True)

    row = pl.BlockSpec((tm, BRANCH_W), lambda i: (i, 0))
    return pl.pallas_call(
        body, name=name, grid=(T // tm,), in_specs=[row, row],
        out_specs=pl.BlockSpec((N_HEADS, tm, 1), lambda i: (0, i, 0)),
        out_shape=jax.ShapeDtypeStruct((N_HEADS, T, 1), F32),
        compiler_params=_cp(("parallel",)),
    )(do, o)


def _attn_bwd_dq(proj, do, lse, delta, Fq, Fk, n_seq, name):
    T = proj.shape[0]
    S = T // n_seq
    blk = min(ATT_BLK, S)
    nb = S // blk
    scale = HEAD_DIM ** -0.5
    qc, kc, vc = OFF_Q // LANES, OFF_K // LANES, OFF_V // LANES

    def body(q_ref, k_ref, v_ref, do_ref, lse_ref, dl_ref, fq_ref, fk_ref, dq_ref, acc_s):
        qi, ki = pl.program_id(2), pl.program_id(3)

        @pl.when(ki == 0)
        def _():
            acc_s[...] = jnp.zeros_like(acc_s)

        @pl.when(ki <= qi)
        def _():
            q, k, v, do_ = q_ref[...], k_ref[...], v_ref[...], do_ref[...]
            mask = _causal_mask(qi, ki, blk)
            for hh, sel in enumerate(_pair_masks()):
                s = _attn_logits(q, k, fq_ref[hh], fk_ref[hh], sel, mask, scale)
                p = jnp.exp(s - lse_ref[hh])
                dom = jnp.where(sel, do_, jnp.zeros_like(do_))
                dp = lax.dot_general(dom, v, (((1,), (1,)), ((), ())), preferred_element_type=F32)
                ds = p * (dp - dl_ref[hh])
                acc_s[hh] += jnp.dot(ds.astype(BF16), k, preferred_element_type=F32)

        @pl.when(ki == qi)
        def _():
            lo, _ = _pair_masks()
            dq_ref[...] = (jnp.where(lo, acc_s[0], acc_s[1]) * scale).astype(BF16)

    qmap = lambda b, j, qi, ki: (b * nb + qi, j)
    col1 = pl.BlockSpec((2, blk, 1), lambda b, j, qi, ki: (j, b * nb + qi, 0))
    return pl.pallas_call(
        body, name=name, grid=(n_seq, N_HEADS // 2, nb, nb),
        in_specs=[
            pl.BlockSpec((blk, LANES), lambda b, j, qi, ki: (b * nb + qi, qc + j)),
            pl.BlockSpec((blk, LANES), lambda b, j, qi, ki: (b * nb + jnp.minimum(ki, qi), kc + j)),
            pl.BlockSpec((blk, LANES), lambda b, j, qi, ki: (b * nb + jnp.minimum(ki, qi), vc + j)),
            pl.BlockSpec((blk, LANES), qmap),
            col1, col1, col1,
            pl.BlockSpec((2, 1, blk), lambda b, j, qi, ki: (j, 0, b * nb + jnp.minimum(ki, qi))),
        ],
        out_specs=pl.BlockSpec((blk, LANES), qmap),
        out_shape=jax.ShapeDtypeStruct((T, BRANCH_W), BF16),
        scratch_shapes=[pltpu.VMEM((2, blk, LANES), F32)],
        compiler_params=_cp(("parallel", "parallel", "parallel", "arbitrary")),
    )(proj, proj, proj, do, lse, delta, Fq, Fk)


def _attn_bwd_dkv(proj, do, lse, delta, Fq, Fk, n_seq, name):
    T = proj.shape[0]
    S = T // n_seq
    blk = min(ATT_BLK, S)
    nb = S // blk
    scale = HEAD_DIM ** -0.5
    qc, kc, vc = OFF_Q // LANES, OFF_K // LANES, OFF_V // LANES
    tdot = functools.partial(lax.dot_general, dimension_numbers=(((0,), (0,)), ((), ())),
                             preferred_element_type=F32)

    def body(q_ref, k_ref, v_ref, do_ref, lse_ref, dl_ref, fq_ref, fk_ref, dk_ref, dv_ref, dfk_ref,
             dk_s, dv_s, df_s):
        ki, qi = pl.program_id(2), pl.program_id(3)

        @pl.when(qi == 0)
        def _():
            dk_s[...] = jnp.zeros_like(dk_s)
            dv_s[...] = jnp.zeros_like(dv_s)
            df_s[...] = jnp.zeros_like(df_s)

        @pl.when(qi >= ki)
        def _():
            q, k, v, do_ = q_ref[...], k_ref[...], v_ref[...], do_ref[...]
            mask = _causal_mask(qi, ki, blk)
            for hh, sel in enumerate(_pair_masks()):
                s = _attn_logits(q, k, fq_ref[hh], fk_ref[hh], sel, mask, scale)
                p = jnp.exp(s - lse_ref[hh])
                dv_s[hh] += tdot(p.astype(BF16), do_)
                dom = jnp.where(sel, do_, jnp.zeros_like(do_))
                dp = lax.dot_general(dom, v, (((1,), (1,)), ((), ())), preferred_element_type=F32)
                ds = p * (dp - dl_ref[hh])
                dk_s[hh] += tdot(ds.astype(BF16), q)
                df_s[hh] -= jnp.sum(ds, axis=0, keepdims=True)

        @pl.when(qi == nb - 1)
        def _():
            lo, _ = _pair_masks()
            dk_ref[...] = (jnp.where(lo, dk_s[0], dk_s[1]) * scale).astype(BF16)
            dv_ref[...] = jnp.where(lo, dv_s[0], dv_s[1]).astype(BF16)
            dfk_ref[...] = df_s[...]

    kmap = lambda b, j, ki, qi: (b * nb + ki, j)
    col1 = pl.BlockSpec((2, blk, 1), lambda b, j, ki, qi: (j, b * nb + jnp.maximum(qi, ki), 0))
    rowk = pl.BlockSpec((2, 1, blk), lambda b, j, ki, qi: (j, 0, b * nb + ki))
    return pl.pallas_call(
        body, name=name, grid=(n_seq, N_HEADS // 2, nb, nb),
        in_specs=[
            pl.BlockSpec((blk, LANES), lambda b, j, ki, qi: (b * nb + jnp.maximum(qi, ki), qc + j)),
            pl.BlockSpec((blk, LANES), lambda b, j, ki, qi: (b * nb + ki, kc + j)),
            pl.BlockSpec((blk, LANES), lambda b, j, ki, qi: (b * nb + ki, vc + j)),
            pl.BlockSpec((blk, LANES), lambda b, j, ki, qi: (b * nb + jnp.maximum(qi, ki), j)),
            col1, col1, col1, rowk,
        ],
        out_specs=[pl.BlockSpec((blk, LANES), kmap), pl.BlockSpec((blk, LANES), kmap), rowk],
        out_shape=[jax.ShapeDtypeStruct((T, BRANCH_W), BF16), jax.ShapeDtypeStruct((T, BRANCH_W), BF16),
                   jax.ShapeDtypeStruct((N_HEADS, 1, T), F32)],
        scratch_shapes=[pltpu.VMEM((2, blk, LANES), F32), pltpu.VMEM((2, blk, LANES), F32),
                        pltpu.VMEM((2, 1, blk), F32)],
        compiler_params=_cp(("parallel", "parallel", "parallel", "arbitrary")),
    )(proj, proj, proj, do, lse, delta, Fq, Fk)


AUG0 = HEAD_DIM
Q_TILE, K_CHUNK, ROW_GROUP = 512, 256, 64


def _fox_prep(f, bf, proj, n_seq, name):
    T = f.shape[0]
    S = T // n_seq
    c = min(CUM_BLK, S)

    def body(f_ref, b_ref, q_ref, k_ref, v_ref, qa_ref, ka_ref, va_ref):
        ri = lax.broadcasted_iota(jnp.int32, (c, c), 0)
        ci = lax.broadcasted_iota(jnp.int32, (c, c), 1)
        tri = (ri >= ci).astype(BF16)
        lane = lax.broadcasted_iota(jnp.int32, (c, LANES), 1)
        carry = jnp.zeros((1, LANES), F32)
        for j in range(S // c):
            rows = slice(j * c, (j + 1) * c)
            lf = _log_sigmoid(f_ref[rows, :] + b_ref[...])
            Fc = _tri_dot(tri, lf) + carry
            carry = carry + jnp.sum(lf, axis=0, keepdims=True)
            for h in range(N_HEADS):
                col = jnp.sum(jnp.where(lane == h, Fc, 0.0), axis=-1, keepdims=True)
                hi = col.astype(BF16).astype(F32)
                r1 = col - hi
                mid = r1.astype(BF16).astype(F32)
                lo = r1 - mid
                ones_q = jnp.logical_and(lane >= AUG0 + 3, lane < AUG0 + 6)
                ones_k = jnp.logical_and(lane >= AUG0, lane < AUG0 + 3)
                aug_q = jnp.where(lane == AUG0, hi, jnp.where(lane == AUG0 + 1, mid, jnp.where(
                    lane == AUG0 + 2, lo, jnp.where(ones_q, 1.0, 0.0))))
                aug_k = jnp.where(lane == AUG0 + 3, -hi, jnp.where(lane == AUG0 + 4, -mid, jnp.where(
                    lane == AUG0 + 5, -lo, jnp.where(ones_k, 1.0, 0.0))))
                pair = slice((h // 2) * LANES, (h // 2 + 1) * LANES)
                qp, kp = q_ref[rows, pair].astype(F32), k_ref[rows, pair].astype(F32)
                vp = v_ref[rows, pair].astype(F32)
                if h % 2:
                    qp, kp, vp = (pltpu.roll(a, HEAD_DIM, 1) for a in (qp, kp, vp))
                out = slice(h * LANES, (h + 1) * LANES)
                qa_ref[rows, out] = jnp.where(lane < HEAD_DIM, qp * (HEAD_DIM ** -0.5), aug_q).astype(BF16)
                ka_ref[rows, out] = jnp.where(lane < HEAD_DIM, kp, aug_k).astype(BF16)
                va_ref[rows, out] = jnp.where(lane < HEAD_DIM, vp, jnp.where(lane == AUG0, 1.0, 0.0)).astype(BF16)

    fblk = pl.BlockSpec((S, LANES), lambda b: (b, 0))
    out = pl.BlockSpec((S, N_HEADS * LANES), lambda b: (b, 0))
    sh = jax.ShapeDtypeStruct((T, N_HEADS * LANES), BF16)
    return pl.pallas_call(
        body, name=name, grid=(n_seq,),
        in_specs=[fblk, pl.BlockSpec((1, LANES), lambda b: (0, 0)),
                  pl.BlockSpec((S, BRANCH_W), lambda b: (b, OFF_Q // BRANCH_W)),
                  pl.BlockSpec((S, BRANCH_W), lambda b: (b, OFF_K // BRANCH_W)),
                  pl.BlockSpec((S, BRANCH_W), lambda b: (b, OFF_V // BRANCH_W))],
        out_specs=[out, out, out], out_shape=[sh, sh, sh],
        compiler_params=_cp(("parallel",)),
    )(f, bf, proj, proj, proj)


def _band_mask(q0, k0, nq, nk):
    row = q0 + lax.broadcasted_iota(jnp.int32, (nq, nk), 0)
    col = k0 + lax.broadcasted_iota(jnp.int32, (nq, nk), 1)
    return col <= row


_NT = (((1,), (1,)), ((), ()))
_TN = (((0,), (0,)), ((), ()))


def _attn_fwd2(qa, ka, va, n_seq, name):
    T = qa.shape[0]
    S = T // n_seq
    tq, tk, rg = min(Q_TILE, S), min(K_CHUNK, S), ROW_GROUP
    nq, per = S // tq, tq // tk

    def body(q_ref, k_ref, v_ref, o_ref, o32_ref, lse_ref, phi_s, plo_s, mp_s, m_s, acc_s):
        qi = pl.program_id(2)
        mp_s[...] = jnp.full_like(mp_s, NEG_INF)
        acc_s[...] = jnp.zeros_like(acc_s)

        def scores(kc, hh):
            k0 = pl.multiple_of(kc * tk, tk)
            hl = slice(hh * LANES, (hh + 1) * LANES)
            return k0, lax.dot_general(q_ref[:, hl], k_ref[pl.ds(k0, tk), hl], _NT, preferred_element_type=F32)

        def max_chunk(kc, masked):
            for hh in range(2):
                k0, s_all = scores(kc, hh)
                for r in range(tq // rg):
                    rows = slice(r * rg, (r + 1) * rg)
                    s = s_all[rows, :]
                    if masked:
                        s = jnp.where(_band_mask(qi * tq + r * rg, k0, rg, tk), s, NEG_INF)
                    part = s[:, :LANES]
                    for c in range(1, tk // LANES):
                        part = jnp.maximum(part, s[:, c * LANES:(c + 1) * LANES])
                    mp_s[hh, rows, :] = jnp.maximum(mp_s[hh, rows, :], part)

        def sum_chunk(kc, masked):
            for hh in range(2):
                k0, s_all = scores(kc, hh)
                hl = slice(hh * LANES, (hh + 1) * LANES)
                v = v_ref[pl.ds(k0, tk), hl]
                for r in range(tq // rg):
                    rows = slice(r * rg, (r + 1) * rg)
                    p = jnp.exp(s_all[rows, :] - m_s[hh, rows])
                    if masked:
                        p = jnp.where(_band_mask(qi * tq + r * rg, k0, rg, tk), p, 0.0)
                    p_hi = p.astype(BF16)
                    phi_s[hh, rows, :] = p_hi
                    plo_s[hh, rows, :] = (p - p_hi.astype(F32)).astype(BF16)
                acc_s[hh] += (jnp.dot(phi_s[hh], v, preferred_element_type=F32)
                              + jnp.dot(plo_s[hh], v, preferred_element_type=F32))

        def sweep(chunk):
            def unmasked(kc, carry):
                chunk(kc, False)
                return carry

            lax.fori_loop(0, qi * per, unmasked, 0)
            for d in range(per):
                chunk(qi * per + d, True)

        sweep(max_chunk)
        m_s[...] = jnp.max(mp_s[...], axis=-1, keepdims=True)
        sweep(sum_chunk)

        lane = lax.broadcasted_iota(jnp.int32, (1, LANES), 1)
        outs = []
        for hh in range(2):
            acc = acc_s[hh]
            l = jnp.sum(jnp.where(lane == AUG0, acc, 0.0), axis=-1, keepdims=True)
            lse_ref[hh] = m_s[hh] + jnp.log(l)
            outs.append(acc / l)
        o = jnp.where(lane < HEAD_DIM, outs[0], pltpu.roll(outs[1], HEAD_DIM, 1))
        o_ref[...] = o.astype(BF16)
        o32_ref[...] = o

    qmap = lambda b, j, qi: (b * nq + qi, j)
    omap = lambda b, j, qi: (b * nq + qi, j)
    kv = pl.BlockSpec((S, 2 * LANES), lambda b, j, qi: (b, j))
    return pl.pallas_call(
        body, name=name, grid=(n_seq, N_HEADS // 2, nq),
        in_specs=[pl.BlockSpec((tq, 2 * LANES), qmap), kv, kv],
        out_specs=[pl.BlockSpec((tq, LANES), omap), pl.BlockSpec((tq, LANES), omap),
                   pl.BlockSpec((2, tq, 1), lambda b, j, qi: (j, b * nq + qi, 0))],
        out_shape=[jax.ShapeDtypeStruct((T, BRANCH_W), BF16), jax.ShapeDtypeStruct((T, BRANCH_W), F32),
                   jax.ShapeDtypeStruct((N_HEADS, T, 1), F32)],
        scratch_shapes=[pltpu.VMEM((2, tq, tk), BF16), pltpu.VMEM((2, tq, tk), BF16),
                        pltpu.VMEM((2, tq, LANES), F32), pltpu.VMEM((2, tq, 1), F32),
                        pltpu.VMEM((2, tq, LANES), F32)],
        compiler_params=_cp(("parallel", "parallel", "parallel")),
    )(qa, ka, va)


def _attn_bwd_dq2(qa, ka, proj, do, lse, delta, n_seq, name):
    T = qa.shape[0]
    S = T // n_seq
    tq, tk, rg = min(Q_TILE, S), min(K_CHUNK, S), ROW_GROUP
    nq, per = S // tq, tq // tk
    vc = OFF_V // LANES

    def body(q_ref, k_ref, v_ref, do_ref, lse_ref, dl_ref, dq_ref, ds_s, acc_s):
        qi = pl.program_id(2)
        acc_s[...] = jnp.zeros_like(acc_s)
        sels = _pair_masks()

        def chunk(kc, masked):
            k0 = pl.multiple_of(kc * tk, tk)
            v = v_ref[pl.ds(k0, tk), :]
            for hh in range(2):
                hl = slice(hh * LANES, (hh + 1) * LANES)
                kh = k_ref[pl.ds(k0, tk), hl]
                s_all = lax.dot_general(q_ref[:, hl], kh, _NT, preferred_element_type=F32)
                dom = jnp.where(sels[hh], do_ref[...], jnp.zeros_like(do_ref[...]))
                dp_all = lax.dot_general(dom, v, _NT, preferred_element_type=F32)
                for r in range(tq // rg):
                    rows = slice(r * rg, (r + 1) * rg)
                    p = jnp.exp(s_all[rows, :] - lse_ref[hh, rows])
                    if masked:
                        p = jnp.where(_band_mask(qi * tq + r * rg, k0, rg, tk), p, 0.0)
                    ds_s[hh, rows, :] = (p * (dp_all[rows, :] - dl_ref[hh, rows])).astype(BF16)
                acc_s[hh] += jnp.dot(ds_s[hh], kh, preferred_element_type=F32)

        def unmasked(kc, carry):
            chunk(kc, False)
            return carry

        lax.fori_loop(0, qi * per, unmasked, 0)
        for d in range(per):
            chunk(qi * per + d, True)
        dq = jnp.where(sels[0], acc_s[0], pltpu.roll(acc_s[1], HEAD_DIM, 1))
        dq_ref[...] = (dq * (HEAD_DIM ** -0.5)).astype(BF16)

    qmap = lambda b, j, qi: (b * nq + qi, j)
    col1 = pl.BlockSpec((2, tq, 1), lambda b, j, qi: (j, b * nq + qi, 0))
    return pl.pallas_call(
        body, name=name, grid=(n_seq, N_HEADS // 2, nq),
        in_specs=[pl.BlockSpec((tq, 2 * LANES), qmap),
                  pl.BlockSpec((S, 2 * LANES), lambda b, j, qi: (b, j)),
                  pl.BlockSpec((S, LANES), lambda b, j, qi: (b, vc + j)),
                  pl.BlockSpec((tq, LANES), qmap), col1, col1],
        out_specs=pl.BlockSpec((tq, LANES), qmap),
        out_shape=jax.ShapeDtypeStruct((T, BRANCH_W), BF16),
        scratch_shapes=[pltpu.VMEM((2, tq, tk), BF16), pltpu.VMEM((2, tq, LANES), F32)],
        compiler_params=_cp(("parallel", "parallel", "parallel")),
    )(qa, ka, proj, do, lse, delta)


def _attn_bwd_dkv2(qa, ka, proj, do, lse, delta, n_seq, name):
    T = qa.shape[0]
    S = T // n_seq
    tkt, tqc, rg = min(Q_TILE, S), min(K_CHUNK, S), ROW_GROUP // 2
    nk, per, nqc = S // tkt, tkt // tqc, S // tqc
    vc = OFF_V // LANES

    def body(q_ref, k_ref, v_ref, do_ref, lse_ref, dl_ref, dk_ref, dv_ref, dfk_ref,
             p_s, ds_s, dk_s, dv_s, df_s):
        ki = pl.program_id(2)
        dk_s[...] = jnp.zeros_like(dk_s)
        dv_s[...] = jnp.zeros_like(dv_s)
        df_s[...] = jnp.zeros_like(df_s)
        sels = _pair_masks()
        v = v_ref[...]

        def chunk(qc, masked):
            q0 = pl.multiple_of(qc * tqc, tqc)
            do_ = do_ref[pl.ds(q0, tqc), :]
            for hh in range(2):
                hl = slice(hh * LANES, (hh + 1) * LANES)
                qh = q_ref[pl.ds(q0, tqc), hl]
                s_all = lax.dot_general(qh, k_ref[:, hl], _NT, preferred_element_type=F32)
                dom = jnp.where(sels[hh], do_, jnp.zeros_like(do_))
                dp_all = lax.dot_general(dom, v, _NT, preferred_element_type=F32)
                dfp = jnp.zeros((1, tkt), F32)
                for r in range(tqc // rg):
                    rows = slice(r * rg, (r + 1) * rg)
                    qrows = pl.ds(q0 + r * rg, rg)
                    p = jnp.exp(s_all[rows, :] - lse_ref[hh, qrows])
                    if masked:
                        p = jnp.where(_band_mask(q0 + r * rg, ki * tkt, rg, tkt), p, 0.0)
                    ds = p * (dp_all[rows, :] - dl_ref[hh, qrows])
                    p_s[hh, rows, :] = p.astype(BF16)
                    ds_s[hh, rows, :] = ds.astype(BF16)
                    dfp = dfp + jnp.sum(ds, axis=0, keepdims=True)
                df_s[hh] -= dfp
                dv_s[hh] += lax.dot_general(p_s[hh], do_, _TN, preferred_element_type=F32)
                dk_s[hh] += lax.dot_general(ds_s[hh], qh, _TN, preferred_element_type=F32)

        for d in range(per):
            chunk(ki * per + d, True)

        def unmasked(qc, carry):
            chunk(qc, False)
            return carry

        lax.fori_loop((ki + 1) * per, nqc, unmasked, 0)
        dk_ref[...] = jnp.where(sels[0], dk_s[0], pltpu.roll(dk_s[1], HEAD_DIM, 1)).astype(BF16)
        dv_ref[...] = jnp.where(sels[0], dv_s[0], dv_s[1]).astype(BF16)
        dfk_ref[...] = df_s[...]

    kmap = lambda b, j, ki: (b * nk + ki, j)
    col1 = pl.BlockSpec((2, S, 1), lambda b, j, ki: (j, b, 0))
    rowk = pl.BlockSpec((2, 1, tkt), lambda b, j, ki: (j, 0, b * nk + ki))
    return pl.pallas_call(
        body, name=name, grid=(n_seq, N_HEADS // 2, nk),
        in_specs=[pl.BlockSpec((S, 2 * LANES), lambda b, j, ki: (b, j)),
                  pl.BlockSpec((tkt, 2 * LANES), kmap),
                  pl.BlockSpec((tkt, LANES), lambda b, j, ki: (b * nk + ki, vc + j)),
                  pl.BlockSpec((S, LANES), lambda b, j, ki: (b, j)), col1, col1],
        out_specs=[pl.BlockSpec((tkt, LANES), kmap), pl.BlockSpec((tkt, LANES), kmap), rowk],
        out_shape=[jax.ShapeDtypeStruct((T, BRANCH_W), BF16), jax.ShapeDtypeStruct((T, BRANCH_W), BF16),
                   jax.ShapeDtypeStruct((N_HEADS, 1, T), F32)],
        scratch_shapes=[pltpu.VMEM((2, tqc, tkt), BF16), pltpu.VMEM((2, tqc, tkt), BF16),
                        pltpu.VMEM((2, tkt, LANES), F32),
                        pltpu.VMEM((2, tkt, LANES), F32), pltpu.VMEM((2, 1, tkt), F32)],
        compiler_params=_cp(("parallel", "parallel", "parallel")),
    )(qa, ka, proj, do, lse, delta)


def _attn_bwd(qa, ka, proj, do, o32, lse, n_seq, name):
    T = qa.shape[0]
    S = T // n_seq
    tq, tk, rg = min(Q_TILE, S), min(K_CHUNK, S), ROW_GROUP
    nq, per, nkc = S // tq, tq // tk, S // tk
    vc = OFF_V // LANES

    def body(q_ref, k_ref, v_ref, do_ref, o_ref, lse_ref, dq_ref, dk_ref, dv_ref, dfk_ref,
             p_s, ds_s, dq_s, dk_s, dv_s, df_s):
        dk_s[...] = jnp.zeros_like(dk_s)
        dv_s[...] = jnp.zeros_like(dv_s)
        df_s[...] = jnp.zeros_like(df_s)
        sels = _pair_masks()

        for qi in range(nq):
            q0 = qi * tq
            do_t = do_ref[q0:q0 + tq, :]
            dq_s[...] = jnp.zeros_like(dq_s)
            prod = do_t.astype(F32) * o_ref[q0:q0 + tq, :]
            dls = [jnp.sum(jnp.where(sel, prod, 0.0), axis=-1, keepdims=True) for sel in sels]

            def chunk(kc, masked, q0=q0, do_t=do_t, dls=dls):
                k0 = pl.multiple_of(kc * tk, tk)
                v = v_ref[pl.ds(k0, tk), :]
                for hh in range(2):
                    hl = slice(hh * LANES, (hh + 1) * LANES)
                    qh, kh = q_ref[q0:q0 + tq, hl], k_ref[pl.ds(k0, tk), hl]
                    s_all = lax.dot_general(qh, kh, _NT, preferred_element_type=F32)
                    dom = jnp.where(sels[hh], do_t, jnp.zeros_like(do_t))
                    dp_all = lax.dot_general(dom, v, _NT, preferred_element_type=F32)
                    dfp = jnp.zeros((1, tk), F32)
                    for r in range(tq // rg):
                        rows = slice(r * rg, (r + 1) * rg)
                        qrows = slice(q0 + r * rg, q0 + (r + 1) * rg)
                        p = jnp.exp(s_all[rows, :] - lse_ref[hh, qrows])
                        if masked:
                            p = jnp.where(_band_mask(q0 + r * rg, k0, rg, tk), p, 0.0)
                        ds = p * (dp_all[rows, :] - dls[hh][rows])
                        p_s[hh, rows, :] = p.astype(BF16)
                        ds_s[hh, rows, :] = ds.astype(BF16)
                        dfp = dfp + jnp.sum(ds, axis=0, keepdims=True)
                    df_s[hh, kc] -= dfp
                    dq_s[hh] += jnp.dot(ds_s[hh], kh, preferred_element_type=F32)
                    dv_s[hh, pl.ds(k0, tk), :] += lax.dot_general(p_s[hh], do_t, _TN, preferred_element_type=F32)
                    dk_s[hh, pl.ds(k0, tk), :] += lax.dot_general(ds_s[hh], qh, _TN, preferred_element_type=F32)

            def unmasked(kc, carry, chunk=chunk):
                chunk(kc, False)
                return carry

            lax.fori_loop(0, qi * per, unmasked, 0)
            for d in range(per):
                chunk(qi * per + d, True)
            dq = jnp.where(sels[0], dq_s[0], pltpu.roll(dq_s[1], HEAD_DIM, 1))
            dq_ref[q0:q0 + tq, :] = (dq * (HEAD_DIM ** -0.5)).astype(BF16)

        dk_ref[...] = jnp.where(sels[0], dk_s[0], pltpu.roll(dk_s[1], HEAD_DIM, 1)).astype(BF16)
        dv_ref[...] = jnp.where(sels[0], dv_s[0], dv_s[1]).astype(BF16)
        for c in range(nkc):
            dfk_ref[:, :, c * tk:(c + 1) * tk] = df_s[:, c]

    seq = lambda w: pl.BlockSpec((S, w), lambda b, j: (b, j))
    col1 = pl.BlockSpec((2, S, 1), lambda b, j: (j, b, 0))
    act = jax.ShapeDtypeStruct((T, BRANCH_W), BF16)
    return pl.pallas_call(
        body, name=name, grid=(n_seq, N_HEADS // 2),
        in_specs=[seq(2 * LANES), seq(2 * LANES), pl.BlockSpec((S, LANES), lambda b, j: (b, vc + j)), seq(LANES),
                  seq(LANES), col1],
        out_specs=[seq(LANES), seq(LANES), seq(LANES), pl.BlockSpec((2, 1, S), lambda b, j: (j, 0, b))],
        out_shape=[act, act, act, jax.ShapeDtypeStruct((N_HEADS, 1, T), F32)],
        scratch_shapes=[pltpu.VMEM((2, tq, tk), BF16), pltpu.VMEM((2, tq, tk), BF16),
                        pltpu.VMEM((2, tq, LANES), F32), pltpu.VMEM((2, S, LANES), F32),
                        pltpu.VMEM((2, S, LANES), F32), pltpu.VMEM((2, nkc, 1, tk), F32)],
        compiler_params=_cp(("parallel", "parallel")),
    )(qa, ka, proj, do, o32, lse)


def _shift_down(v, k, row):
    return jnp.where(row >= k, pltpu.roll(v, k, 0), 0.0)


def _shift_up(v, k, row, S):
    return jnp.where(row < S - k, pltpu.roll(v, S - k, 0), 0.0)


def _pool_diff(uf, w, row):
    acc, k = uf, 1
    while k < w:
        acc = acc + _shift_down(acc, k, row)
        k *= 2
    n = jnp.minimum(row + 1, w).astype(F32)
    return acc / n - uf


def _pool_fwd(proj, pool_w, pool_scale, n_seq, name):
    T = proj.shape[0]
    S = T // n_seq

    def body(u_ref, w_ref, sc_ref, o_ref, d_s):
        g = pl.program_id(1)
        row = lax.broadcasted_iota(jnp.int32, (S, POOL_GD), 0)
        uf = u_ref[...].astype(F32)
        for gi, wlen in enumerate(POOL_WINDOWS):
            @pl.when(g == gi)
            def _(wlen=wlen):
                d_s[...] = _pool_diff(uf, wlen, row).astype(BF16)
        e = jnp.dot(d_s[...], w_ref[0], preferred_element_type=F32)
        o_ref[...] = (e * sc_ref[...]).astype(BF16)

    uc = OFF_U // POOL_GD
    return pl.pallas_call(
        body, name=name, grid=(n_seq, len(POOL_WINDOWS)),
        in_specs=[pl.BlockSpec((S, POOL_GD), lambda b, g: (b, uc + g)),
                  pl.BlockSpec((1, POOL_GD, POOL_GD), lambda b, g: (g, 0, 0)),
                  pl.BlockSpec((1, POOL_GD), lambda b, g: (0, g))],
        out_specs=pl.BlockSpec((S, POOL_GD), lambda b, g: (b, g)),
        out_shape=jax.ShapeDtypeStruct((T, BRANCH_W), BF16),
        scratch_shapes=[pltpu.VMEM((S, POOL_GD), BF16)],
        compiler_params=_cp(("parallel", "parallel")),
    )(proj, pool_w, pool_scale)


def _pool_bwd(proj, dout, pool_w, pool_scale, n_seq, name):
    T = proj.shape[0]
    S = T // n_seq

    def body(u_ref, do_ref, w_ref, sc_ref, du_ref, dw_ref, dsc_ref, d_s):
        g, b = pl.program_id(0), pl.program_id(1)
        row = lax.broadcasted_iota(jnp.int32, (S, POOL_GD), 0)
        uf = u_ref[...].astype(F32)
        for gi, wlen in enumerate(POOL_WINDOWS):
            @pl.when(g == gi)
            def _(wlen=wlen):
                d_s[...] = _pool_diff(uf, wlen, row).astype(BF16)
        db16 = d_s[...]
        w = w_ref[0]
        e = jnp.dot(db16, w, preferred_element_type=F32)
        dof = do_ref[...].astype(F32)
        dsc = jnp.sum(dof * e, axis=0, keepdims=True)
        de = (dof * sc_ref[...]).astype(BF16)
        dd = lax.dot_general(de, w, (((1,), (1,)), ((), ())), preferred_element_type=F32)
        dw = lax.dot_general(db16, de, (((0,), (0,)), ((), ())), preferred_element_type=F32)
        for gi, wlen in enumerate(POOL_WINDOWS):
            @pl.when(g == gi)
            def _(wlen=wlen):
                n = jnp.minimum(row + 1, wlen).astype(F32)
                acc, k = dd / n, 1
                while k < wlen:
                    acc = acc + _shift_up(acc, k, row, S)
                    k *= 2
                du_ref[...] = (acc - dd).astype(BF16)

        @pl.when(b == 0)
        def _():
            dw_ref[0] = dw
            dsc_ref[...] = dsc

        @pl.when(b > 0)
        def _():
            dw_ref[0] += dw
            dsc_ref[...] += dsc

    uc = OFF_U // POOL_GD
    return pl.pallas_call(
        body, name=name, grid=(len(POOL_WINDOWS), n_seq),
        in_specs=[pl.BlockSpec((S, POOL_GD), lambda g, b: (b, uc + g)),
                  pl.BlockSpec((S, POOL_GD), lambda g, b: (b, g)),
                  pl.BlockSpec((1, POOL_GD, POOL_GD), lambda g, b: (g, 0, 0)),
                  pl.BlockSpec((1, POOL_GD), lambda g, b: (0, g))],
        out_specs=[pl.BlockSpec((S, POOL_GD), lambda g, b: (b, g)),
                   pl.BlockSpec((1, POOL_GD, POOL_GD), lambda g, b: (g, 0, 0)),
                   pl.BlockSpec((1, POOL_GD), lambda g, b: (0, g))],
        out_shape=[jax.ShapeDtypeStruct((T, BRANCH_W), BF16),
                   jax.ShapeDtypeStruct((len(POOL_WINDOWS), POOL_GD, POOL_GD), F32),
                   jax.ShapeDtypeStruct((1, BRANCH_W), F32)],
        scratch_shapes=[pltpu.VMEM((S, POOL_GD), BF16)],
        compiler_params=_cp(("parallel", "arbitrary")),
    )(proj, dout, pool_w, pool_scale)


def _conv_fwd(proj, conv_w, n_seq, name):
    T = proj.shape[0]
    S = T // n_seq
    nc = BRANCH_W // LANES

    def body(cv_ref, cb_ref, cc_ref, w_ref, o_ref):
        row = lax.broadcasted_iota(jnp.int32, (S, LANES), 0)
        z = cc_ref[...].astype(F32) * cv_ref[...].astype(F32)
        w = w_ref[...]
        y = w[0:1] * _shift_down(z, 2, row) + w[1:2] * _shift_down(z, 1, row) + w[2:3] * z
        o_ref[...] = (cb_ref[...].astype(F32) * y).astype(BF16)

    def col(off):
        return pl.BlockSpec((S, LANES), lambda b, j: (b, off // LANES + j))

    return pl.pallas_call(
        body, name=name, grid=(n_seq, nc),
        in_specs=[col(OFF_CV), col(OFF_CB), col(OFF_CC), pl.BlockSpec((CONV_K, LANES), lambda b, j: (0, j))],
        out_specs=pl.BlockSpec((S, LANES), lambda b, j: (b, j)),
        out_shape=jax.ShapeDtypeStruct((T, BRANCH_W), BF16),
        compiler_params=_cp(("parallel", "parallel")),
    )(proj, proj, proj, conv_w)


def _conv_bwd(proj, dout, conv_w, n_seq, name):
    T = proj.shape[0]
    S = T // n_seq
    nc = BRANCH_W // LANES

    def body(cv_ref, cb_ref, cc_ref, do_ref, w_ref, dcv_ref, dcb_ref, dcc_ref, dw_ref):
        b = pl.program_id(1)
        row = lax.broadcasted_iota(jnp.int32, (S, LANES), 0)
        cv, cb, cc = cv_ref[...].astype(F32), cb_ref[...].astype(F32), cc_ref[...].astype(F32)
        dof = do_ref[...].astype(F32)
        w = w_ref[...]
        z = cc * cv
        z1, z2 = _shift_down(z, 1, row), _shift_down(z, 2, row)
        y = w[0:1] * z2 + w[1:2] * z1 + w[2:3] * z
        dcb_ref[...] = (dof * y).astype(BF16)
        dy = dof * cb
        dz = w[2:3] * dy + w[1:2] * _shift_up(dy, 1, row, S) + w[0:1] * _shift_up(dy, 2, row, S)
        dcc_ref[...] = (dz * cv).astype(BF16)
        dcv_ref[...] = (dz * cc).astype(BF16)
        dws = [jnp.sum(dy * zk, axis=0, keepdims=True) for zk in (z2, z1, z)]

        @pl.when(b == 0)
        def _():
            for kk in range(CONV_K):
                dw_ref[kk:kk + 1, :] = dws[kk]

        @pl.when(b > 0)
        def _():
            for kk in range(CONV_K):
                dw_ref[kk:kk + 1, :] += dws[kk]

    def col(off):
        return pl.BlockSpec((S, LANES), lambda j, b: (b, off // LANES + j))

    out = pl.BlockSpec((S, LANES), lambda j, b: (b, j))
    wsp = pl.BlockSpec((CONV_K, LANES), lambda j, b: (0, j))
    act = jax.ShapeDtypeStruct((T, BRANCH_W), BF16)
    return pl.pallas_call(
        body, name=name, grid=(nc, n_seq),
        in_specs=[col(OFF_CV), col(OFF_CB), col(OFF_CC), out, wsp],
        out_specs=[out, out, out, wsp],
        out_shape=[act, act, act, jax.ShapeDtypeStruct((CONV_K, BRANCH_W), F32)],
        compiler_params=_cp(("parallel", "arbitrary")),
    )(proj, proj, proj, dout, conv_w)


def _mix_fwd(oa, ob, oc, wpa, wpp, wpc, proj, b_gate, name):
    T = oa.shape[0]
    tm = min(256, T)

    def body(oa_ref, ob_ref, oc_ref, wa_ref, wp_ref, wc_ref, g_ref, bg_ref, o_ref):
        acc = jnp.zeros((tm, D_MODEL), F32)
        for i, (x_ref, w_ref) in enumerate(((oa_ref, wa_ref), (ob_ref, wp_ref), (oc_ref, wc_ref))):
            y = jnp.dot(x_ref[...], w_ref[...], preferred_element_type=F32)
            sl = slice(i * D_MODEL, (i + 1) * D_MODEL)
            acc = acc + _sigmoid(g_ref[:, sl].astype(F32) + bg_ref[:, sl]) * y
        o_ref[...] = acc.astype(BF16)

    br = pl.BlockSpec((tm, BRANCH_W), lambda i: (i, 0))
    wsp = pl.BlockSpec((BRANCH_W, D_MODEL), lambda i: (0, 0))
    return pl.pallas_call(
        body, name=name, grid=(T // tm,),
        in_specs=[br, br, br, wsp, wsp, wsp, pl.BlockSpec((tm, GATE_W), lambda i: (i, 0)),
                  pl.BlockSpec((1, GATE_W), lambda i: (0, 0))],
        out_specs=pl.BlockSpec((tm, D_MODEL), lambda i: (i, 0)),
        out_shape=jax.ShapeDtypeStruct((T, D_MODEL), BF16),
        compiler_params=_cp(("parallel",)),
    )(oa, ob, oc, wpa, wpp, wpc, proj, b_gate)


def _mix_bwd(oa, ob, oc, wpa, wpp, wpc, proj, b_gate, dmixed, name):
    T = oa.shape[0]
    tm = min(256, T)

    def body(oa_ref, ob_ref, oc_ref, wa_ref, wp_ref, wc_ref, g_ref, bg_ref, dm_ref,
             dya_ref, dyb_ref, dyc_ref, dg_ref, dbg_ref):
        i0 = pl.program_id(0)
        dm = dm_ref[...].astype(F32)
        parts = []
        for i, (x_ref, w_ref, dy_ref) in enumerate(((oa_ref, wa_ref, dya_ref), (ob_ref, wp_ref, dyb_ref),
                                                    (oc_ref, wc_ref, dyc_ref))):
            y = jnp.dot(x_ref[...], w_ref[...], preferred_element_type=F32)
            sl = slice(i * D_MODEL, (i + 1) * D_MODEL)
            gate = _sigmoid(g_ref[:, sl].astype(F32) + bg_ref[:, sl])
            dy_ref[...] = (dm * gate).astype(BF16)
            dgl = dm * y * gate * (1.0 - gate)
            dg_ref[:, sl] = dgl.astype(BF16)
            parts.append(jnp.sum(dgl, axis=0, keepdims=True))

        @pl.when(i0 == 0)
        def _():
            for i in range(3):
                dbg_ref[:, i * D_MODEL:(i + 1) * D_MODEL] = parts[i]

        @pl.when(i0 > 0)
        def _():
            for i in range(3):
                dbg_ref[:, i * D_MODEL:(i + 1) * D_MODEL] += parts[i]

    br = pl.BlockSpec((tm, BRANCH_W), lambda i: (i, 0))
    wsp = pl.BlockSpec((BRANCH_W, D_MODEL), lambda i: (0, 0))
    row = pl.BlockSpec((tm, D_MODEL), lambda i: (i, 0))
    gsp = pl.BlockSpec((tm, GATE_W), lambda i: (i, 0))
    bsp = pl.BlockSpec((1, GATE_W), lambda i: (0, 0))
    act = jax.ShapeDtypeStruct((T, D_MODEL), BF16)
    return pl.pallas_call(
        body, name=name, grid=(T // tm,),
        in_specs=[br, br, br, wsp, wsp, wsp, gsp, bsp, row],
        out_specs=[row, row, row, gsp, bsp],
        out_shape=[act, act, act, jax.ShapeDtypeStruct((T, GATE_W), BF16),
                   jax.ShapeDtypeStruct((1, GATE_W), F32)],
        compiler_params=_cp(("arbitrary",)),
    )(oa, ob, oc, wpa, wpp, wpc, proj, b_gate, dmixed)


GU_TILE = 256


def _gu_col(c):
    t, r = divmod(c, GU_TILE)
    return (t // 2) * GU_TILE + r + (FFN_HIDDEN if t % 2 else 0)


def _gate_up_swiglu(h, w, name):
    T, K = h.shape
    tm = min(2048, T)

    def body(h_ref, w_ref, ab_ref, s_ref):
        prod = jnp.dot(h_ref[...], w_ref[...], preferred_element_type=F32)
        ab_ref[...] = prod.astype(BF16)
        a = prod[:, :GU_TILE]
        s_ref[...] = (a * _sigmoid(a) * prod[:, GU_TILE:]).astype(BF16)

    return pl.pallas_call(
        body, name=name, grid=(T // tm, FFN_HIDDEN // GU_TILE),
        in_specs=[pl.BlockSpec((tm, K), lambda i, j: (i, 0)), pl.BlockSpec((K, 2 * GU_TILE), lambda i, j: (0, j))],
        out_specs=[pl.BlockSpec((tm, 2 * GU_TILE), lambda i, j: (i, j)), pl.BlockSpec((tm, GU_TILE), lambda i, j: (i, j))],
        out_shape=[jax.ShapeDtypeStruct((T, 2 * FFN_HIDDEN), BF16), jax.ShapeDtypeStruct((T, FFN_HIDDEN), BF16)],
        compiler_params=_cp(("parallel", "parallel")),
    )(h, w)


def _swiglu_bwd_fused(dx, w_down, ab, name):
    T, K = dx.shape
    tm = min(2048, T)

    def body(dx_ref, w_ref, ab_ref, o_ref):
        ds = lax.dot_general(dx_ref[...], w_ref[...], _NT, preferred_element_type=F32)
        a = ab_ref[:, :GU_TILE].astype(F32)
        b = ab_ref[:, GU_TILE:].astype(F32)
        sg = _sigmoid(a)
        o_ref[:, :GU_TILE] = (ds * b * sg * (1.0 + a * (1.0 - sg))).astype(BF16)
        o_ref[:, GU_TILE:] = (ds * a * sg).astype(BF16)

    pair = pl.BlockSpec((tm, 2 * GU_TILE), lambda i, j: (i, j))
    return pl.pallas_call(
        body, name=name, grid=(T // tm, FFN_HIDDEN // GU_TILE),
        in_specs=[pl.BlockSpec((tm, K), lambda i, j: (i, 0)), pl.BlockSpec((GU_TILE, K), lambda i, j: (j, 0)), pair],
        out_specs=pair, out_shape=jax.ShapeDtypeStruct((T, 2 * FFN_HIDDEN), BF16),
        compiler_params=_cp(("parallel", "parallel")),
    )(dx, w_down, ab)


def _adamw_update(w_ref, g_ref, m_ref, v_ref, d_ref, nm_ref, nv_ref):
    gv = g_ref[...]
    nm = ADAM_B1 * m_ref[...] + (1.0 - ADAM_B1) * gv
    nv = ADAM_B2 * v_ref[...] + (1.0 - ADAM_B2) * (gv * gv)
    m_hat = nm / (1.0 - ADAM_B1 ** ADAM_STEP)
    v_hat = nv / (1.0 - ADAM_B2 ** ADAM_STEP)
    d_ref[...] = -ADAM_LR * (m_hat / (jnp.sqrt(v_hat) + ADAM_EPS) + ADAM_WD * w_ref[...])
    nm_ref[...] = nm
    nv_ref[...] = nv


def _adamw_many(ws, gs, ms, vs, name):
    n = len(ws)

    def body(*refs):
        ins, outs = refs[:4 * n], refs[4 * n:]
        for t in range(n):
            _adamw_update(ins[t], ins[n + t], ins[2 * n + t], ins[3 * n + t], outs[t], outs[n + t], outs[2 * n + t])

    shapes = [jax.ShapeDtypeStruct(w.shape, F32) for w in ws]
    out = pl.pallas_call(body, name=name, out_shape=shapes * 3, compiler_params=_cp())(*ws, *gs, *ms, *vs)
    return out[:n], out[n:2 * n], out[2 * n:]


def _adamw(w, g, m, v, name):
    R, C = w.shape
    tr = R
    for cand in (256, 352, 128, 64, 8):
        if R > cand and R % cand == 0:
            tr = cand
            break

    def body(w_ref, g_ref, m_ref, v_ref, d_ref, nm_ref, nv_ref):
        _adamw_update(w_ref, g_ref, m_ref, v_ref, d_ref, nm_ref, nv_ref)

    blk = pl.BlockSpec((tr, C), lambda i: (i, 0))
    sh = jax.ShapeDtypeStruct((R, C), F32)
    return pl.pallas_call(
        body, name=name, grid=(R // tr,), in_specs=[blk] * 4, out_specs=[blk] * 3, out_shape=[sh] * 3,
        compiler_params=_cp(("parallel",)),
    )(w, g, m, v)


def _sum_slabs(x, name):
    n, R, C = x.shape
    tr = R
    for cand in (512, 256, 128, 64, 32, 16, 8):
        if R > cand and R % cand == 0:
            tr = cand
            break

    def body(x_ref, o_ref):
        acc = x_ref[0].astype(F32)
        for j in range(1, n):
            acc = acc + x_ref[j].astype(F32)
        o_ref[...] = acc

    return pl.pallas_call(
        body, name=name, grid=(R // tr,), in_specs=[pl.BlockSpec((n, tr, C), lambda i: (0, i, 0))],
        out_specs=pl.BlockSpec((tr, C), lambda i: (i, 0)), out_shape=jax.ShapeDtypeStruct((R, C), F32),
        compiler_params=_cp(("parallel",)),
    )(x)


def _multi_gather(xs, layers, name):
    nt = len(xs)
    shapes = [x.shape if lay is None else x.shape[1:] for x, lay in zip(xs, layers)]

    def body(*refs):
        x_refs, out_refs = refs[:nt], refs[nt:2 * nt]
        send_sems, recv_sems, local_sems = refs[2 * nt:]
        x_, y_, c_ = lax.axis_index("x"), lax.axis_index("y"), lax.axis_index("c")
        me, sibling = (x_, y_, c_), (x_, y_, 1 - c_)
        chips = [(1 - x_, y_), (x_, 1 - y_), (1 - x_, 1 - y_)]

        def own_block(t):
            return x_refs[t] if layers[t] is None else x_refs[t].at[layers[t]]

        def copy(t, k, block, to, own=False):
            px, py, pc = block
            dst = out_refs[t].at[4 * px + 2 * py + pc]
            return pltpu.make_async_remote_copy(
                src_ref=own_block(t) if own else dst, dst_ref=dst,
                send_sem=send_sems.at[t, k], recv_sem=recv_sems.at[t, k],
                device_id=to, device_id_type=pl.DeviceIdType.MESH)

        mine, first, passed = [], [], []
        for t in range(nt):
            mine.append(pltpu.make_async_copy(own_block(t), out_refs[t].at[4 * x_ + 2 * y_ + c_], local_sems.at[t]))
            mine[-1].start()
            first.append([copy(t, 1 + j, me, (*chip, c_), own=True) for j, chip in enumerate(chips)]
                         + [copy(t, 0, me, sibling, own=True)])
            for cp in first[-1]:
                cp.start()
        for t in range(nt):
            for j, chip in enumerate(chips):
                copy(t, 1 + j, (*chip, c_), me).wait_recv()
                passed.append(copy(t, 4 + j, (*chip, c_), sibling))
                passed[-1].start()
        for t in range(nt):
            copy(t, 0, sibling, me).wait_recv()
            for j, chip in enumerate(chips):
                copy(t, 4 + j, (*chip, 1 - c_), me).wait_recv()
        for cp in [c for f in first for c in f] + passed:
            cp.wait_send()
        for cp in mine:
            cp.wait()

    hbm = pl.BlockSpec(memory_space=pl.ANY)
    return pl.pallas_call(
        body, name=name, out_shape=[jax.ShapeDtypeStruct((N_DEV,) + tuple(s), x.dtype) for s, x in zip(shapes, xs)],
        in_specs=[hbm] * nt, out_specs=[hbm] * nt,
        scratch_shapes=[pltpu.SemaphoreType.DMA((nt, 7)), pltpu.SemaphoreType.DMA((nt, 7)),
                        pltpu.SemaphoreType.DMA((nt,))],
    )(*xs)


def _multi_exchange(sends, name):
    nt = len(sends)

    def body(*refs):
        s_refs, r_refs = refs[:nt], refs[nt:2 * nt]
        send_sems, recv_sems, local_sems = refs[2 * nt:]
        x_, y_, c_ = lax.axis_index("x"), lax.axis_index("y"), lax.axis_index("c")
        me = 4 * x_ + 2 * y_ + c_
        mine, out, inc = [], [], []
        for t in range(nt):
            mine.append(pltpu.make_async_copy(s_refs[t].at[me], r_refs[t].at[me], local_sems.at[t]))
            mine[-1].start()
        for k in (2, 4, 6, 3, 5, 7, 1):
            px, py, pc = x_ ^ ((k >> 2) & 1), y_ ^ ((k >> 1) & 1), c_ ^ (k & 1)
            peer = 4 * px + 2 * py + pc
            for t in range(nt):
                def copy(src, dst):
                    return pltpu.make_async_remote_copy(
                        src_ref=s_refs[t].at[src], dst_ref=r_refs[t].at[dst],
                        send_sem=send_sems.at[t, k - 1], recv_sem=recv_sems.at[t, k - 1],
                        device_id=(px, py, pc), device_id_type=pl.DeviceIdType.MESH)

                out.append(copy(peer, me))
                inc.append(copy(me, peer))
        for cp in out:
            cp.start()
        for cp in inc:
            cp.wait_recv()
        for cp in out:
            cp.wait_send()
        for cp in mine:
            cp.wait()

    hbm = pl.BlockSpec(memory_space=pl.ANY)
    return pl.pallas_call(
        body, name=name, out_shape=[jax.ShapeDtypeStruct(s.shape, s.dtype) for s in sends],
        in_specs=[hbm] * nt, out_specs=[hbm] * nt,
        scratch_shapes=[pltpu.SemaphoreType.DMA((nt, N_DEV - 1)), pltpu.SemaphoreType.DMA((nt, N_DEV - 1)),
                        pltpu.SemaphoreType.DMA((nt,))],
    )(*sends)


_HBM = pl.BlockSpec(memory_space=pltpu.HBM)
_SEM = pl.BlockSpec(memory_space=pltpu.SEMAPHORE)
_PEER_ORDER = (2, 4, 6, 3, 5, 7, 1)


def _split_copies(src_refs, land_refs, send_sems, recv_sems, layers, per_peer):
    x_, y_, c_ = lax.axis_index("x"), lax.axis_index("y"), lax.axis_index("c")
    me = 4 * x_ + 2 * y_ + c_
    copies = []
    for k in _PEER_ORDER:
        px, py, pc = x_ ^ ((k >> 2) & 1), y_ ^ ((k >> 1) & 1), c_ ^ (k & 1)
        peer = 4 * px + 2 * py + pc
        for t in range(len(src_refs)):
            if per_peer:
                src = src_refs[t].at[peer]
            else:
                src = src_refs[t] if layers[t] is None else src_refs[t].at[layers[t]]
            copies.append(pltpu.make_async_remote_copy(
                src_ref=src, dst_ref=land_refs[t].at[me],
                send_sem=send_sems.at[t * (N_DEV - 1) + k - 1], recv_sem=recv_sems.at[t * (N_DEV - 1) + k - 1],
                device_id=(px, py, pc), device_id_type=pl.DeviceIdType.MESH))
    return copies


def _own_copies(src_refs, land_refs, sems, layers, per_peer):
    nt = len(src_refs)
    me = 4 * lax.axis_index("x") + 2 * lax.axis_index("y") + lax.axis_index("c")
    copies = []
    for t in range(nt):
        if per_peer:
            src = src_refs[t].at[me]
        else:
            src = src_refs[t] if layers[t] is None else src_refs[t].at[layers[t]]
        copies.append(pltpu.make_async_copy(src, land_refs[t].at[me], sems.at[nt * (N_DEV - 1) + t]))
    return copies


def _split_start(srcs, layers, per_peer, after, name):
    nt = len(srcs)
    if per_peer:
        land_shapes = [s.shape for s in srcs]
    else:
        land_shapes = [(N_DEV,) + tuple(s.shape if lay is None else s.shape[1:]) for s, lay in zip(srcs, layers)]

    def body(*refs):
        src_refs, land_refs = refs[:nt], refs[nt:2 * nt]
        send_sems, recv_sems = refs[2 * nt + 1], refs[2 * nt + 2]
        token = refs[-1]
        for cp in _split_copies(src_refs, land_refs, send_sems, recv_sems, layers, per_peer):
            cp.start()
        for cp in _own_copies(src_refs, land_refs, send_sems, layers, per_peer):
            cp.start()
        token[...] = jnp.zeros_like(token)

    lands = [pltpu.with_memory_space_constraint(lax.empty(s, x.dtype), pltpu.HBM) for s, x in zip(land_shapes, srcs)]
    srcs = [pltpu.with_memory_space_constraint(x, pltpu.HBM) for x in srcs]
    out = pl.pallas_call(
        body, name=name,
        out_shape=(pltpu.SemaphoreType.DMA((nt * N_DEV,)), pltpu.SemaphoreType.DMA((nt * (N_DEV - 1),)),
                   *[pltpu.HBM(x.shape, x.dtype) for x in srcs], *[pltpu.HBM(s, x.dtype) for s, x in zip(land_shapes, srcs)],
                   jax.ShapeDtypeStruct((8, LANES), F32)),
        in_specs=[_HBM] * (2 * nt) + [pl.BlockSpec(memory_space=pl.ANY)],
        out_specs=(_SEM, _SEM, *([_HBM] * (2 * nt)), pl.BlockSpec(memory_space=pltpu.VMEM)),
        input_output_aliases={i: 2 + i for i in range(2 * nt)},
        compiler_params=pltpu.CompilerParams(has_side_effects=pltpu.SideEffectType.DATAFLOW_SIDE_EFFECTING),
    )(*srcs, *lands, after)
    return out[0], out[1], list(out[2:2 + nt]), list(out[2 + nt:2 + 2 * nt]), out[-1]


def _split_wait(started, layers, per_peer, after, name):
    send_sems, recv_sems, srcs, lands, _ = started
    nt = len(srcs)

    def body(*refs):
        src_refs, land_refs = refs[:nt], refs[nt:2 * nt]
        s_sems, r_sems = refs[2 * nt], refs[2 * nt + 1]
        for cp in _split_copies(src_refs, land_refs, s_sems, r_sems, layers, per_peer):
            cp.wait_send()
            cp.wait_recv()
        for cp in _own_copies(src_refs, land_refs, s_sems, layers, per_peer):
            cp.wait()

    out = pl.pallas_call(
        body, name=name,
        out_shape=tuple(pltpu.HBM(x.shape, x.dtype) for x in srcs + lands),
        in_specs=[_HBM] * (2 * nt) + [_SEM, _SEM, pl.BlockSpec(memory_space=pl.ANY)],
        out_specs=tuple([_HBM] * (2 * nt)),
        input_output_aliases={i: i for i in range(2 * nt)},
        compiler_params=pltpu.CompilerParams(has_side_effects=pltpu.SideEffectType.DATAFLOW_SIDE_EFFECTING),
    )(*srcs, *lands, send_sems, recv_sems, after)
    return list(out[nt:])


def _with_own(land, own):
    me = 4 * lax.axis_index("x") + 2 * lax.axis_index("y") + lax.axis_index("c")
    return lax.dynamic_update_slice_in_dim(land, own[None], me, axis=0)


def _runs(mapping):
    runs, c, n = [], 0, len(mapping)
    while c < n:
        if mapping[c] is None:
            c += 1
            continue
        sid, d, lo = mapping[c][0], mapping[c][1] - c, c
        while c < n and mapping[c] is not None and mapping[c][0] == sid and mapping[c][1] - c == d:
            c += 1
        runs.append((lo, c, sid, d))
    return runs


def _tile_plan(mapping, src_widths):
    runs = _runs(mapping)
    plan = []
    for t in range(len(mapping) // LANES):
        pieces = []
        for lo, hi, sid, d in runs:
            lo_t, hi_t = max(lo, t * LANES), min(hi, (t + 1) * LANES)
            if lo_t >= hi_t:
                continue
            a = ((lo_t + d) // LANES) * LANES
            win = min(2 * LANES, src_widths[sid] - a)
            shift = t * LANES + d - a
            pieces.append((sid, a, win, shift, lo_t - t * LANES, hi_t - t * LANES))
        plan.append(pieces)
    return plan


def _reblock(srcs, src_views, outs, out_views, name):
    R = srcs[0].shape[-2]
    tr = min(512, R)
    widths = {sid: srcs[ai].shape[-1] for sid, (ai, _) in src_views.items()}
    plans = [(ai, li, _tile_plan(mapping, widths)) for ai, li, mapping in out_views]
    ns = len(srcs)

    def body(*refs):
        s_refs, o_refs = refs[:ns], refs[ns:]
        cache = {}

        def shift_matrix(win, shift, lo, hi):
            key = (win, shift, lo, hi)
            if key not in cache:
                r = lax.broadcasted_iota(jnp.int32, (win, LANES), 0)
                c = lax.broadcasted_iota(jnp.int32, (win, LANES), 1)
                hit = jnp.logical_and(r - c == shift, jnp.logical_and(c >= lo, c < hi))
                cache[key] = jnp.where(hit, 1.0, 0.0).astype(BF16)
            return cache[key]

        for ai, li, plan in plans:
            for t, pieces in enumerate(plan):
                acc = None
                whole = len(pieces) == 1 and pieces[0][3:] == (0, 0, LANES)
                for sid, a, win, shift, lo, hi in pieces:
                    sa, sl = src_views[sid]
                    if whole:
                        win = LANES
                    src = s_refs[sa][:, a:a + win] if sl is None else s_refs[sa][sl, :, a:a + win]
                    if whole:
                        acc = src
                    else:
                        part = jnp.dot(src, shift_matrix(win, shift, lo, hi), preferred_element_type=F32)
                        acc = part if acc is None else acc + part
                val = jnp.zeros((tr, LANES), BF16) if acc is None else acc.astype(BF16)
                if li is None:
                    o_refs[ai][:, t * LANES:(t + 1) * LANES] = val
                else:
                    o_refs[ai][li, :, t * LANES:(t + 1) * LANES] = val

    def spec(shape):
        if len(shape) == 2:
            return pl.BlockSpec((tr, shape[1]), lambda i: (i, 0))
        return pl.BlockSpec((shape[0], tr, shape[2]), lambda i: (0, i, 0))

    return pl.pallas_call(
        body, name=name, grid=(R // tr,), in_specs=[spec(s.shape) for s in srcs],
        out_specs=[spec(s) for s in outs], out_shape=[jax.ShapeDtypeStruct(s, BF16) for s in outs],
        compiler_params=_cp(("parallel",)),
    )(*srcs)


SHARDED = ("w_in", "w_gate_up", "w_proj_attn", "w_proj_pool", "w_proj_conv", "w_out", "w_down")
WEIGHT_ORDER = ("attn_norm", "w_in", "b_forget", "b_gate", "w_proj_attn", "pool_w", "pool_scale", "w_proj_pool",
                "conv_w", "w_proj_conv", "w_out", "ffn_norm", "w_gate_up", "w_down", "final_norm")
IN_SHARD, IN_SHARD_PAD = IN_COLS // N_DEV, 896
GU_SHARD, GU_SHARD_PAD = 2 * FFN_HIDDEN // N_DEV, 768


def _w_in_col(c):
    if c < GATE_W:
        return c + 3592
    if c < OFF_U:
        return c - OFF_Q
    return c - OFF_U + 1544


def _w_in_full(gathered, name):
    main = [divmod(_w_in_col(c), IN_SHARD) for c in range(MAIN_COLS)]
    fcols = [divmod(1536 + c, IN_SHARD) if c < N_HEADS else None for c in range(LANES)]
    R = gathered.shape[1]
    return _reblock([gathered], {i: (0, i) for i in range(N_DEV)}, [(R, MAIN_COLS), (R, LANES)],
                    [(0, None, main), (1, None, fcols)], name)


def _w_in_slabs(dmain, dwf, name):
    inv = {_w_in_col(c): ("m", c) for c in range(MAIN_COLS)}
    inv.update({1536 + c: ("f", c) for c in range(N_HEADS)})
    views = []
    for i in range(N_DEV):
        mapping = [inv[IN_SHARD * i + j] if j < IN_SHARD else None for j in range(IN_SHARD_PAD)]
        views.append((0, i, mapping))
    R = dmain.shape[0]
    return _reblock([dmain, dwf], {"m": (0, None), "f": (1, None)}, [(N_DEV, R, IN_SHARD_PAD)], views, name)[0]


def _w_gu_full(gathered, name):
    mapping = [divmod(_gu_col(c), GU_SHARD) for c in range(2 * FFN_HIDDEN)]
    R = gathered.shape[1]
    return _reblock([gathered], {i: (0, i) for i in range(N_DEV)}, [(R, 2 * FFN_HIDDEN)], [(0, None, mapping)], name)[0]


def _w_gu_slabs(dw, name):
    inv = {_gu_col(c): c for c in range(2 * FFN_HIDDEN)}
    views = [(0, i, [("w", inv[GU_SHARD * i + j]) if j < GU_SHARD else None for j in range(GU_SHARD_PAD)])
             for i in range(N_DEV)]
    R = dw.shape[0]
    return _reblock([dw], {"w": (0, None)}, [(N_DEV, R, GU_SHARD_PAD)], views, name)[0]


def _layer_fwd(x, W, n_seq, l, h1=None, next_norm=None):
    T = x.shape[0]
    sfx = f"_l{l}"
    if h1 is None:
        h1 = _rms_fwd(x, W["attn_norm"], "rms1" + sfx)
    proj = _matmul(h1, W["w_main"], mode="nn", out_dtype=BF16, name="proj_main" + sfx)
    f = _matmul(h1, W["w_f"], mode="nn", out_dtype=F32, name="proj_f" + sfx)
    qa, ka, va = _fox_prep(f, W["b_forget"], proj, n_seq, "fox_prep" + sfx)
    oa, oa32, lse = _attn_fwd2(qa, ka, va, n_seq, "attn_fwd" + sfx)
    if "late" in W:
        W.update(W.pop("late")(oa))
    ob = _pool_fwd(proj, W["pool_w"], W["pool_scale"], n_seq, "pool_fwd" + sfx)
    oc = _conv_fwd(proj, W["conv_w"], n_seq, "conv_fwd" + sfx)
    mixed = _mix_fwd(oa, ob, oc, W["w_proj_attn"], W["w_proj_pool"], W["w_proj_conv"], proj, W["b_gate"],
                     "mix_fwd" + sfx)
    x2, h2 = _matmul(mixed, W["w_out"], mode="nn", out_dtype=F32, name="out_proj" + sfx, tm=1024, tn=1024,
                     residual=x, rms_g=W["ffn_norm"])
    ab, s = _gate_up_swiglu(h2, W["w_gate_up"], "gate_up" + sfx)
    x3 = _matmul(s, W["w_down"], mode="nn", out_dtype=F32, name="down" + sfx, tm=1024, tn=1024, tk=1408,
                 residual=x2, rms_g=next_norm)
    x3, h1_next = x3 if next_norm is not None else (x3, None)
    saved = dict(x=x, h1=h1, proj=proj, f=f, qa=qa, ka=ka, oa=oa, oa32=oa32, lse=lse, ob=ob, oc=oc, mixed=mixed, x2=x2,
                 h2=h2, ab=ab, s=s)
    return x3, saved, h1_next


def _layer_bwd(dx3, dx3b, W, sv, n_seq, l, stage=None):
    T = dx3.shape[0]
    sfx = f"_l{l}"
    G = {}
    stage = stage or (lambda l, group, G, W: W)
    dab = _swiglu_bwd_fused(dx3b, W["w_down"], sv["ab"], "d_ab" + sfx)
    G["w_down"] = _matmul(sv["s"], dx3b, mode="tn", out_dtype=BF16, name="dw_down" + sfx, tm=256, tn=1024)
    dh2 = _matmul(dab, W["w_gate_up"], mode="nt", out_dtype=BF16, name="d_h2" + sfx, tm=1024, tn=1024, tk=1408)
    G["w_gate_up"] = _matmul(sv["h2"], dab, mode="tn", out_dtype=BF16, name="dw_gate_up" + sfx, tm=1024)
    W = stage(l, "ffn", G, W)
    dx2, dx2b, G["ffn_norm"] = _rms_bwd(sv["x2"], W["ffn_norm"], dh2, dx3, "rms2_bwd" + sfx)
    dmixed = _matmul(dx2b, W["w_out"], mode="nt", out_dtype=BF16, name="d_mixed" + sfx)
    G["w_out"] = _matmul(sv["mixed"], dx2b, mode="tn", out_dtype=BF16, name="dw_out" + sfx, tm=1024)
    dya, dyb, dyc, dg, G["b_gate"] = _mix_bwd(sv["oa"], sv["ob"], sv["oc"], W["w_proj_attn"], W["w_proj_pool"],
                                              W["w_proj_conv"], sv["proj"], W["b_gate"], dmixed, "mix_bwd" + sfx)
    douts = {}
    for br, dy, o in (("attn", dya, sv["oa"]), ("pool", dyb, sv["ob"]), ("conv", dyc, sv["oc"])):
        douts[br] = _matmul(dy, W["w_proj_" + br], mode="nt", out_dtype=BF16, name=f"d_{br}_out" + sfx)
        G["w_proj_" + br] = _matmul(o, dy, mode="tn", out_dtype=BF16, name=f"dw_proj_{br}" + sfx, tm=512)
    W = stage(l, "mix", G, W)
    dcv, dcb, dcc, G["conv_w"] = _conv_bwd(sv["proj"], douts["conv"], W["conv_w"], n_seq, "conv_bwd" + sfx)
    du, G["pool_w"], G["pool_scale"] = _pool_bwd(sv["proj"], douts["pool"], W["pool_w"], W["pool_scale"], n_seq,
                                                 "pool_bwd" + sfx)
    dq, dk, dv, dFk = _attn_bwd(sv["qa"], sv["ka"], sv["proj"], douts["attn"], sv["oa32"], sv["lse"], n_seq,
                                "attn_bwd" + sfx)
    dF = jnp.pad(dFk.reshape(N_HEADS, T).T, ((0, 0), (0, LANES - N_HEADS)))
    df, G["b_forget"] = _fox_cumsum_bwd(sv["f"], W["b_forget"], dF, n_seq, "fox_cumsum_bwd" + sfx)
    dproj = jnp.concatenate([dg, dq, dk, dv, du, dcv, dcb, dcc], axis=1)
    G["w_main"] = _matmul(sv["h1"], dproj, mode="tn", out_dtype=BF16, name="dw_main" + sfx, tm=1024)
    G["w_f"] = _matmul(sv["h1"], df, mode="tn", out_dtype=BF16, name="dw_f" + sfx, tm=1024)
    W = stage(l, "w_in", G, W)
    dh1 = _matmul(df, W["w_f"], mode="nt", out_dtype=F32, name="d_h1_f" + sfx)
    dh1 = _matmul(dproj, W["w_main"], mode="nt", out_dtype=F32, name="d_h1_main" + sfx, tm=1024, tn=1024, tk=1664,
                  residual=dh1)
    dx, dxb, G["attn_norm"] = _rms_bwd(sv["x"], W["attn_norm"], dh1, dx2, "rms1_bwd" + sfx)
    return dx, dxb, G


def _replicated_operands(rep, l):
    W = {}
    W["attn_norm"], W["ffn_norm"] = rep["attn_norm"][l], rep["ffn_norm"][l]
    W["b_forget"] = jnp.pad(rep["b_forget"][l].reshape(1, N_HEADS), ((0, 0), (0, LANES - N_HEADS)))
    W["b_gate"] = rep["b_gate"][l].reshape(1, GATE_W)
    W["pool_w"] = rep["pool_w"][l].astype(BF16)
    W["pool_scale"] = rep["pool_scale"][l].reshape(1, BRANCH_W)
    return W


def _local_step(x, target, get_W, attn_norms, final_norm, stage=None):
    n_seq, S, Dm = x.shape
    T = n_seq * S
    xt = x.reshape(T, Dm)
    saved, Ws, h1 = [], [], None
    for l in range(DEPTH):
        Ws.append(get_W(l, xt))
        next_norm = attn_norms[l + 1] if l + 1 < DEPTH else None
        xt, sv, h1 = _layer_fwd(xt, Ws[l], n_seq, l, h1, next_norm)
        saved.append(sv)
    loss, dx, dxb, g_final = _loss_head(xt, final_norm, target.reshape(T, Dm), "loss_head")
    grads = [None] * DEPTH
    for l in reversed(range(DEPTH)):
        dx, dxb, grads[l] = _layer_bwd(dx, dxb, Ws[l], saved[l], n_seq, l, stage)
    return loss, dx.reshape(n_seq, S, Dm), grads, g_final


def _padded_shards(weights):
    pads = {"w_in": IN_SHARD_PAD - IN_SHARD, "w_gate_up": GU_SHARD_PAD - GU_SHARD}
    return {n: jnp.pad(weights[n], ((0, 0), (0, 0), (0, pads.get(n, 0)))).astype(BF16) for n in SHARDED}


def _full_operands(g, l):
    W = {}
    if "w_in" in g:
        W["w_main"], W["w_f"] = _w_in_full(g["w_in"], f"w_in_full_l{l}")
    if "w_gate_up" in g:
        W["w_gate_up"] = _w_gu_full(g["w_gate_up"], f"w_gate_up_full_l{l}")
    for n in ("w_proj_attn", "w_proj_pool", "w_proj_conv"):
        if n in g:
            W[n] = jnp.transpose(g[n], (1, 0, 2)).reshape(BRANCH_W, D_MODEL)
    if "w_out" in g:
        W["w_out"] = g["w_out"].reshape(D_MODEL, D_MODEL)
    if "w_down" in g:
        W["w_down"] = g["w_down"].reshape(FFN_HIDDEN, D_MODEL)
    return W


GRAD_GROUPS = {"ffn": ("w_down", "w_gate_up"),
               "mix": ("w_out", "w_proj_attn", "w_proj_pool", "w_proj_conv"),
               "w_in": ("w_in",)}


def _grad_slabs(G, n, l):
    if n == "w_in":
        return _w_in_slabs(G["w_main"], G["w_f"], f"w_in_slabs_l{l}")
    if n == "w_gate_up":
        return _w_gu_slabs(G["w_gate_up"], f"w_gate_up_slabs_l{l}")
    if n == "w_out":
        return G["w_out"].reshape(N_DEV, D_MODEL // N_DEV, D_MODEL)
    if n == "w_down":
        return G["w_down"].reshape(N_DEV, FFN_HIDDEN // N_DEV, D_MODEL)
    return jnp.transpose(G[n].reshape(BRANCH_W, N_DEV, D_MODEL // N_DEV), (1, 0, 2))


def _sum_layer_grads(recv, l):
    out = {n: _sum_slabs(r, f"sum_{n}_l{l}") for n, r in recv.items()}
    if "w_in" in out:
        out["w_in"] = out["w_in"][:, :IN_SHARD]
    if "w_gate_up" in out:
        out["w_gate_up"] = out["w_gate_up"][:, :GU_SHARD]
    return out


def _sum_small(xs, name):
    def body(*refs):
        for x_ref, o_ref in zip(refs[:len(xs)], refs[len(xs):]):
            acc = x_ref[0]
            for j in range(1, N_DEV):
                acc = acc + x_ref[j]
            o_ref[...] = acc

    return pl.pallas_call(
        body, name=name, out_shape=[jax.ShapeDtypeStruct(x.shape[1:], F32) for x in xs],
        compiler_params=_cp(),
    )(*xs)


def _as_2d(a):
    if a.ndim == 1:
        return a.reshape(1, -1)
    return a.reshape(-1, a.shape[-1])


def kernel(x, attn_norm, w_in, b_forget, b_gate, w_proj_attn, pool_w, pool_scale, w_proj_pool, conv_w, w_proj_conv, w_out, ffn_norm, w_gate_up, w_down, final_norm, loss_target, m_attn_norm, m_w_in, m_b_forget, m_b_gate, m_w_proj_attn, m_pool_w, m_pool_scale, m_w_proj_pool, m_conv_w, m_w_proj_conv, m_w_out, m_ffn_norm, m_w_gate_up, m_w_down, m_final_norm, v_attn_norm, v_w_in, v_b_forget, v_b_gate, v_w_proj_attn, v_pool_w, v_pool_scale, v_w_proj_pool, v_conv_w, v_w_proj_conv, v_w_out, v_ffn_norm, v_w_gate_up, v_w_down, v_final_norm):
    weights = dict(attn_norm=attn_norm, w_in=w_in, b_forget=b_forget, b_gate=b_gate, w_proj_attn=w_proj_attn,
                   pool_w=pool_w, pool_scale=pool_scale, w_proj_pool=w_proj_pool, conv_w=conv_w,
                   w_proj_conv=w_proj_conv, w_out=w_out, ffn_norm=ffn_norm, w_gate_up=w_gate_up, w_down=w_down,
                   final_norm=final_norm)
    moments_m = dict(attn_norm=m_attn_norm, w_in=m_w_in, b_forget=m_b_forget, b_gate=m_b_gate,
                     w_proj_attn=m_w_proj_attn, pool_w=m_pool_w, pool_scale=m_pool_scale, w_proj_pool=m_w_proj_pool,
                     conv_w=m_conv_w, w_proj_conv=m_w_proj_conv, w_out=m_w_out, ffn_norm=m_ffn_norm,
                     w_gate_up=m_w_gate_up, w_down=m_w_down, final_norm=m_final_norm)
    moments_v = dict(attn_norm=v_attn_norm, w_in=v_w_in, b_forget=v_b_forget, b_gate=v_b_gate,
                     w_proj_attn=v_w_proj_attn, pool_w=v_pool_w, pool_scale=v_pool_scale, w_proj_pool=v_w_proj_pool,
                     conv_w=v_conv_w, w_proj_conv=v_w_proj_conv, w_out=v_w_out, ffn_norm=v_ffn_norm,
                     w_gate_up=v_w_gate_up, w_down=v_w_down, final_norm=v_final_norm)

    sh = _padded_shards(weights)
    names = list(SHARDED)
    rest = [n for n in names if n != "w_in"]
    me = 4 * lax.axis_index("x") + 2 * lax.axis_index("y") + lax.axis_index("c")
    w_in0, conv_all = _multi_gather([sh["w_in"], conv_w], [0, None], "gather_w_in_l0")
    started, after = {}, w_in0
    for l in range(DEPTH):
        for group, gnames in (("w_in", ["w_in"]), ("rest", rest)):
            if (l, group) != (0, "w_in"):
                started[l, group] = _split_start([sh[n] for n in gnames], [l] * len(gnames), False, after,
                                                 f"gather_start_{group}_l{l}")
                after = started[l, group][4]
    last_token = after

    def get_W(l, xt):
        if l == 0:
            w_in = w_in0
        else:
            w_in = _split_wait(started[l, "w_in"], [l], False, xt, f"gather_wait_w_in_l{l}")[0]
        W = _full_operands({"w_in": w_in}, l)

        def late(after):
            lands = _split_wait(started[l, "rest"], [l] * len(rest), False, after, f"gather_wait_rest_l{l}")
            return _full_operands(dict(zip(rest, lands)), l)

        W["late"] = late
        W.update(_replicated_operands(weights, l))
        W["conv_w"] = jnp.transpose(conv_all[:, l], (1, 0, 2)).reshape(CONV_K, BRANCH_W)
        if l == 0:
            W["attn_norm"] = W["attn_norm"] + last_token[0, 0]
        return W

    exchanges = []

    def stage(l, group, G, W):
        gnames = GRAD_GROUPS[group]
        slabs = [_grad_slabs(G, n, l) for n in gnames]
        started = _split_start(slabs, None, True, slabs[0], f"exchange_start_{group}_l{l}")
        exchanges.append((l, group, gnames, slabs, started))
        tie = {"ffn": "ffn_norm", "mix": "conv_w", "w_in": "w_f"}[group]
        W = dict(W)
        W[tie] = W[tie] + started[4][0, 0].astype(W[tie].dtype)
        return W

    loss_part, grad_x, grads, g_final = _local_step(x, loss_target, get_W, attn_norm, final_norm, stage)
    after = grad_x
    for l, group, gnames, slabs, started in exchanges:
        lands = _split_wait(started, None, True, after, f"exchange_wait_{group}_l{l}")
        grads[l].update(_sum_layer_grads(dict(zip(gnames, lands)), l))
    gw = {n: jnp.stack([grads[l][n] for l in range(DEPTH)]) for n in SHARDED}

    small = ("attn_norm", "b_forget", "b_gate", "pool_w", "pool_scale", "ffn_norm", "conv_w")
    parts = [jnp.stack([grads[l][n] for l in range(DEPTH)]) for n in small] + [g_final, loss_part]
    gathered = _multi_gather(parts, [None] * len(parts), "gather_small_grads")
    summed = _sum_small(gathered, "sum_small_grads")
    for n, s in zip(small, summed):
        gw[n] = s
    gw["attn_norm"], gw["ffn_norm"] = gw["attn_norm"][:, 0], gw["ffn_norm"][:, 0]
    gw["b_forget"] = gw["b_forget"][:, 0, :N_HEADS]
    gw["b_gate"], gw["pool_scale"] = gw["b_gate"][:, 0], gw["pool_scale"][:, 0]
    gw["conv_w"] = lax.dynamic_slice_in_dim(gw["conv_w"], me * (BRANCH_W // N_DEV), BRANCH_W // N_DEV, axis=2)
    gw["final_norm"] = summed[-2][0]
    loss = summed[-1][0, 0]

    deltas, new_m, new_v = {}, {}, {}
    for n in SHARDED:
        shape = weights[n].shape
        d, nm, nv = _adamw(_as_2d(weights[n]), _as_2d(gw[n]), _as_2d(moments_m[n]), _as_2d(moments_v[n]),
                           "adamw_" + n)
        deltas[n], new_m[n], new_v[n] = d.reshape(shape), nm.reshape(shape), nv.reshape(shape)
    rest_names = [n for n in WEIGHT_ORDER if n not in SHARDED]
    ds, nms, nvs = _adamw_many(*[[_as_2d(src[n]) for n in rest_names] for src in (weights, gw, moments_m, moments_v)],
                               "adamw_small")
    for n, d, nm, nv in zip(rest_names, ds, nms, nvs):
        shape = weights[n].shape
        deltas[n], new_m[n], new_v[n] = d.reshape(shape), nm.reshape(shape), nv.reshape(shape)

    return (loss, grad_x, *[gw[n] for n in WEIGHT_ORDER], *[deltas[n] for n in WEIGHT_ORDER],
            *[new_m[n] for n in WEIGHT_ORDER], *[new_v[n] for n in WEIGHT_ORDER])
```

```python
import functools

import numpy as np
import jax
import jax.numpy as jnp
from jax import lax
from jax.experimental import pallas as pl
from jax.experimental.pallas import tpu as pltpu

F32 = jnp.float32
BF16 = jnp.bfloat16

N_DEV = 8
D_MODEL = 1024
DEPTH = 2
N_HEADS = 8
HEAD_DIM = 64
BRANCH_W = 512
POOL_WINDOWS = (2, 4, 8, 16)
POOL_GD = 128
CONV_K = 3
FFN_HIDDEN = 2816
GATE_W = 3 * D_MODEL
IN_COLS = 6664
MAIN_COLS = GATE_W + 7 * BRANCH_W
RMS_EPS = 1e-6
NEG_INF = -1e30

ADAM_LR = 0.001
ADAM_B1 = 0.9
ADAM_B2 = 0.999
ADAM_EPS = 1e-08
ADAM_WD = 0.01
ADAM_STEP = 10

LANES = 128
VMEM_LIMIT = 56 * 1024 * 1024
ATT_BLK = 256
CUM_BLK = 256

TRIPLE = 3 * LANES
OFF_G, OFF_QKV, OFF_CONV, OFF_U = 0, 3072, 4608, 6144


def _cp(sem=None):
    return pltpu.CompilerParams(dimension_semantics=sem, vmem_limit_bytes=VMEM_LIMIT)


def _sigmoid(z):
    return 1.0 / (1.0 + jnp.exp(-z))


def _matmul(a, b, *, mode, out_dtype, name, tm=2048, tn=512, tk=None, residual=None, rms_g=None):
    if mode == "nn":
        (M, K), N = a.shape, b.shape[1]
    elif mode == "nt":
        (M, K), N = a.shape, b.shape[0]
    else:
        (K, M), N = a.shape, b.shape[1]
    tm, tn, tk = min(tm, M), min(tn, N), K if tk is None else min(tk, K)
    assert M % tm == 0 and N % tn == 0 and K % tk == 0, (name, M, N, K, tm, tn, tk)
    nk = K // tk
    if mode == "nn":
        a_spec = pl.BlockSpec((tm, tk), lambda i, j, k: (i, k))
        b_spec = pl.BlockSpec((tk, tn), lambda i, j, k: (k, j))
        dims = (((1,), (0,)), ((), ()))
    elif mode == "nt":
        a_spec = pl.BlockSpec((tm, tk), lambda i, j, k: (i, k))
        b_spec = pl.BlockSpec((tn, tk), lambda i, j, k: (j, k))
        dims = (((1,), (1,)), ((), ()))
    else:
        a_spec = pl.BlockSpec((tk, tm), lambda i, j, k: (k, i))
        b_spec = pl.BlockSpec((tk, tn), lambda i, j, k: (k, j))
        dims = (((0,), (0,)), ((), ()))
    o_spec = pl.BlockSpec((tm, tn), lambda i, j, k: (i, j))
    has_res, has_norm = residual is not None, rms_g is not None
    assert not has_norm or tn == N, (name, tn, N)

    def body(*refs):
        a_ref, b_ref = refs[:2]
        r_ref = refs[2] if has_res else None
        g_ref = refs[2 + has_res] if has_norm else None
        o_ref = refs[2 + has_res + has_norm]
        h_ref = refs[3 + has_res + has_norm] if has_norm else None

        def finish(acc):
            if has_res:
                acc = acc + r_ref[...].astype(F32)
            o_ref[...] = acc.astype(out_dtype)
            if has_norm:
                r = lax.rsqrt(jnp.mean(acc * acc, axis=-1, keepdims=True) + RMS_EPS)
                h_ref[...] = ((acc * r) * g_ref[...]).astype(BF16)

        prod = lax.dot_general(a_ref[...], b_ref[...], dims, preferred_element_type=F32)
        if nk == 1:
            finish(prod)
            return
        acc_ref = refs[-1]
        k = pl.program_id(2)

        @pl.when(k == 0)
        def _():
            acc_ref[...] = prod

        @pl.when(jnp.logical_and(k > 0, k < nk - 1))
        def _():
            acc_ref[...] += prod

        @pl.when(k == nk - 1)
        def _():
            finish(acc_ref[...] + prod)

    in_specs = [a_spec, b_spec] + ([o_spec] if has_res else [])
    args = (a, b) + ((residual,) if has_res else ())
    out_specs, out_shape = o_spec, jax.ShapeDtypeStruct((M, N), out_dtype)
    if has_norm:
        in_specs.append(pl.BlockSpec((1, N), lambda i, j, k: (0, 0)))
        args += (rms_g.reshape(1, N),)
        out_specs, out_shape = [o_spec, o_spec], [out_shape, jax.ShapeDtypeStruct((M, N), BF16)]
    return pl.pallas_call(
        body, name=name, grid=(M // tm, N // tn, nk), in_specs=in_specs, out_specs=out_specs,
        out_shape=out_shape,
        scratch_shapes=[pltpu.VMEM((tm, tn), F32)] if nk > 1 else [],
        compiler_params=_cp(("parallel", "parallel", "arbitrary")),
    )(*args)


def _rms_fwd(x, g, name):
    T, Dm = x.shape
    tm = min(512, T)

    def body(x_ref, g_ref, h_ref):
        xf = x_ref[...]
        r = lax.rsqrt(jnp.mean(xf * xf, axis=-1, keepdims=True) + RMS_EPS)
        h_ref[...] = ((xf * r) * g_ref[...]).astype(BF16)

    return pl.pallas_call(
        body, name=name, grid=(T // tm,),
        in_specs=[pl.BlockSpec((tm, Dm), lambda i: (i, 0)), pl.BlockSpec((1, Dm), lambda i: (0, 0))],
        out_specs=pl.BlockSpec((tm, Dm), lambda i: (i, 0)),
        out_shape=jax.ShapeDtypeStruct((T, Dm), BF16),
        compiler_params=_cp(("parallel",)),
    )(x, g.reshape(1, Dm))


def _rms_bwd(x, g, dh, dres, name):
    T, Dm = x.shape
    tm = min(512, T)

    def body(x_ref, g_ref, dh_ref, dres_ref, dx_ref, dxb_ref, dg_ref):
        i = pl.program_id(0)
        xf = x_ref[...]
        r = lax.rsqrt(jnp.mean(xf * xf, axis=-1, keepdims=True) + RMS_EPS)
        xn = xf * r
        dhf = dh_ref[...].astype(F32)
        dxn = dhf * g_ref[...]
        c = jnp.mean(dxn * xn, axis=-1, keepdims=True)
        dx = dres_ref[...] + r * (dxn - xn * c)
        dx_ref[...] = dx
        dxb_ref[...] = dx.astype(BF16)
        part = jnp.sum(dhf * xn, axis=0, keepdims=True)

        @pl.when(i == 0)
        def _():
            dg_ref[...] = part

        @pl.when(i > 0)
        def _():
            dg_ref[...] += part

    row = pl.BlockSpec((tm, Dm), lambda i: (i, 0))
    vec = pl.BlockSpec((1, Dm), lambda i: (0, 0))
    return pl.pallas_call(
        body, name=name, grid=(T // tm,), in_specs=[row, vec, row, row], out_specs=[row, row, vec],
        out_shape=[jax.ShapeDtypeStruct((T, Dm), F32), jax.ShapeDtypeStruct((T, Dm), BF16),
                   jax.ShapeDtypeStruct((1, Dm), F32)],
        compiler_params=_cp(("arbitrary",)),
    )(x, g.reshape(1, Dm), dh, dres)


def _loss_head(x, g, target, name):
    T, Dm = x.shape
    tm = min(512, T)

    def body(x_ref, g_ref, t_ref, loss_ref, dx_ref, dxb_ref, dg_ref):
        i = pl.program_id(0)
        xf = x_ref[...]
        gv = g_ref[...]
        r = lax.rsqrt(jnp.mean(xf * xf, axis=-1, keepdims=True) + RMS_EPS)
        xn = xf * r
        diff = xn * gv - t_ref[...]
        per_tok = jnp.mean(diff * diff, axis=-1, keepdims=True)
        lpart = 0.5 * jnp.sum(per_tok, axis=0, keepdims=True) + jnp.zeros((1, LANES), F32)
        dy = diff * (1.0 / Dm)
        dxn = dy * gv
        c = jnp.mean(dxn * xn, axis=-1, keepdims=True)
        dx = r * (dxn - xn * c)
        dx_ref[...] = dx
        dxb_ref[...] = dx.astype(BF16)
        part = jnp.sum(dy * xn, axis=0, keepdims=True)

        @pl.when(i == 0)
        def _():
            dg_ref[...] = part
            loss_ref[...] = lpart

        @pl.when(i > 0)
        def _():
            dg_ref[...] += part
            loss_ref[...] += lpart

    row = pl.BlockSpec((tm, Dm), lambda i: (i, 0))
    vec = pl.BlockSpec((1, Dm), lambda i: (0, 0))
    lsp = pl.BlockSpec((1, LANES), lambda i: (0, 0))
    return pl.pallas_call(
        body, name=name, grid=(T // tm,), in_specs=[row, vec, row], out_specs=[lsp, row, row, vec],
        out_shape=[jax.ShapeDtypeStruct((1, LANES), F32), jax.ShapeDtypeStruct((T, Dm), F32),
                   jax.ShapeDtypeStruct((T, Dm), BF16), jax.ShapeDtypeStruct((1, Dm), F32)],
        compiler_params=_cp(("arbitrary",)),
    )(x, g.reshape(1, Dm), target)


def _split_bf16(v):
    hi = v.astype(BF16)
    r1 = v - hi.astype(F32)
    mid = r1.astype(BF16)
    lo = (r1 - mid.astype(F32)).astype(BF16)
    return hi, mid, lo


def _tri_dot(tri, v):
    hi, mid, lo = _split_bf16(v)
    dot = functools.partial(jnp.dot, preferred_element_type=F32)
    return dot(tri, hi) + dot(tri, mid) + dot(tri, lo)


def _log_sigmoid(z):
    return jnp.minimum(z, 0.0) - jnp.log(1.0 + jnp.exp(-jnp.abs(z)))


def _fox_cumsum_fwd(f, bf, n_seq, name):
    T = f.shape[0]
    S = T // n_seq
    c = min(CUM_BLK, S)

    def body(f_ref, b_ref, out_ref):
        ri = lax.broadcasted_iota(jnp.int32, (c, c), 0)
        ci = lax.broadcasted_iota(jnp.int32, (c, c), 1)
        tri = (ri >= ci).astype(BF16)
        carry = jnp.zeros((1, LANES), F32)
        for j in range(S // c):
            lf = _log_sigmoid(f_ref[j * c:(j + 1) * c, :] + b_ref[...])
            out_ref[j * c:(j + 1) * c, :] = _tri_dot(tri, lf) + carry
            carry = carry + jnp.sum(lf, axis=0, keepdims=True)

    blk = pl.BlockSpec((S, LANES), lambda b: (b, 0))
    return pl.pallas_call(
        body, name=name, grid=(n_seq,), in_specs=[blk, pl.BlockSpec((1, LANES), lambda b: (0, 0))],
        out_specs=blk, out_shape=jax.ShapeDtypeStruct((T, LANES), F32),
        compiler_params=_cp(("parallel",)),
    )(f, bf)


def _fox_cumsum_bwd(f, bf, dF, n_seq, name):
    T = f.shape[0]
    S = T // n_seq
    c = min(CUM_BLK, S)

    def body(f_ref, b_ref, dF_ref, df_ref, db_ref):
        b = pl.program_id(0)
        ri = lax.broadcasted_iota(jnp.int32, (c, c), 0)
        ci = lax.broadcasted_iota(jnp.int32, (c, c), 1)
        tri = (ri <= ci).astype(BF16)
        carry = jnp.zeros((1, LANES), F32)
        dbp = jnp.zeros((1, LANES), F32)
        for j in reversed(range(S // c)):
            dFc = dF_ref[j * c:(j + 1) * c, :]
            dlf = _tri_dot(tri, dFc) + carry
            carry = carry + jnp.sum(dFc, axis=0, keepdims=True)
            z = f_ref[j * c:(j + 1) * c, :] + b_ref[...]
            dz = dlf * _sigmoid(-z)
            df_ref[j * c:(j + 1) * c, :] = dz.astype(BF16)
            dbp = dbp + jnp.sum(dz, axis=0, keepdims=True)

        @pl.when(b == 0)
        def _():
            db_ref[...] = dbp

        @pl.when(b > 0)
        def _():
            db_ref[...] += dbp

    blk = pl.BlockSpec((S, LANES), lambda b: (b, 0))
    vec = pl.BlockSpec((1, LANES), lambda b: (0, 0))
    return pl.pallas_call(
        body, name=name, grid=(n_seq,), in_specs=[blk, vec, blk], out_specs=[blk, vec],
        out_shape=[jax.ShapeDtypeStruct((T, LANES), BF16), jax.ShapeDtypeStruct((1, LANES), F32)],
        compiler_params=_cp(("arbitrary",)),
    )(f, bf, dF)


def _pair_masks():
    lane = lax.broadcasted_iota(jnp.int32, (1, LANES), 1)
    lo = lane < HEAD_DIM
    return lo, jnp.logical_not(lo)


def _attn_logits(q, k, fq, fk, sel, mask, scale):
    qm = jnp.where(sel, q, jnp.zeros_like(q))
    s = lax.dot_general(qm, k, (((1,), (1,)), ((), ())), preferred_element_type=F32) * scale
    s = s + fq - fk
    return jnp.where(mask, s, NEG_INF)


def _causal_mask(qi, ki, blk):
    row = qi * blk + lax.broadcasted_iota(jnp.int32, (blk, blk), 0)
    col = ki * blk + lax.broadcasted_iota(jnp.int32, (blk, blk), 1)
    return col <= row


def _attn_fwd(proj, Fq, Fk, n_seq, name):
    T = proj.shape[0]
    S = T // n_seq
    blk = min(ATT_BLK, S)
    nb = S // blk
    scale = HEAD_DIM ** -0.5
    qc, kc, vc = OFF_Q // LANES, OFF_K // LANES, OFF_V // LANES

    def body(q_ref, k_ref, v_ref, fq_ref, fk_ref, o_ref, o32_ref, lse_ref, m_s, l_s, acc_s):
        qi, ki = pl.program_id(2), pl.program_id(3)

        @pl.when(ki == 0)
        def _():
            m_s[...] = jnp.full_like(m_s, NEG_INF)
            l_s[...] = jnp.zeros_like(l_s)
            acc_s[...] = jnp.zeros_like(acc_s)

        @pl.when(ki <= qi)
        def _():
            q, k, v = q_ref[...], k_ref[...], v_ref[...]
            mask = _causal_mask(qi, ki, blk)
            for hh, sel in enumerate(_pair_masks()):
                s = _attn_logits(q, k, fq_ref[hh], fk_ref[hh], sel, mask, scale)
                m_prev = m_s[hh]
                m_new = jnp.maximum(m_prev, jnp.max(s, axis=-1, keepdims=True))
                alpha = jnp.exp(m_prev - m_new)
                p = jnp.exp(s - m_new)
                l_s[hh] = alpha * l_s[hh] + jnp.sum(p, axis=-1, keepdims=True)
                p_hi = p.astype(BF16)
                p_lo = (p - p_hi.astype(F32)).astype(BF16)
                pv = jnp.dot(p_hi, v, preferred_element_type=F32) + jnp.dot(p_lo, v, preferred_element_type=F32)
                acc_s[hh] = alpha * acc_s[hh] + pv
                m_s[hh] = m_new

        @pl.when(ki == qi)
        def _():
            lo, _ = _pair_masks()
            o = jnp.where(lo, acc_s[0] / l_s[0], acc_s[1] / l_s[1])
            o_ref[...] = o.astype(BF16)
            o32_ref[...] = o
            lse_ref[0] = m_s[0] + jnp.log(l_s[0])
            lse_ref[1] = m_s[1] + jnp.log(l_s[1])

    grid = (n_seq, N_HEADS // 2, nb, nb)
    return pl.pallas_call(
        body, name=name, grid=grid,
        in_specs=[
            pl.BlockSpec((blk, LANES), lambda b, j, qi, ki: (b * nb + qi, qc + j)),
            pl.BlockSpec((blk, LANES), lambda b, j, qi, ki: (b * nb + jnp.minimum(ki, qi), kc + j)),
            pl.BlockSpec((blk, LANES), lambda b, j, qi, ki: (b * nb + jnp.minimum(ki, qi), vc + j)),
            pl.BlockSpec((2, blk, 1), lambda b, j, qi, ki: (j, b * nb + qi, 0)),
            pl.BlockSpec((2, 1, blk), lambda b, j, qi, ki: (j, 0, b * nb + jnp.minimum(ki, qi))),
        ],
        out_specs=[
            pl.BlockSpec((blk, LANES), lambda b, j, qi, ki: (b * nb + qi, j)),
            pl.BlockSpec((blk, LANES), lambda b, j, qi, ki: (b * nb + qi, j)),
            pl.BlockSpec((2, blk, 1), lambda b, j, qi, ki: (j, b * nb + qi, 0)),
        ],
        out_shape=[jax.ShapeDtypeStruct((T, BRANCH_W), BF16), jax.ShapeDtypeStruct((T, BRANCH_W), F32),
                   jax.ShapeDtypeStruct((N_HEADS, T, 1), F32)],
        scratch_shapes=[pltpu.VMEM((2, blk, 1), F32), pltpu.VMEM((2, blk, 1), F32),
                        pltpu.VMEM((2, blk, LANES), F32)],
        compiler_params=_cp(("parallel", "parallel", "parallel", "arbitrary")),
    )(proj, proj, proj, Fq, Fk)


def _attn_delta(do, o, name):
    T = do.shape[0]
    tm = min(512, T)

    def body(do_ref, o_ref, d_ref):
        prod = do_ref[...].astype(F32) * o_ref[...].astype(F32)
        lo, hi = _pair_masks()
        for j in range(N_HEADS // 2):
            pj = prod[:, j * LANES:(j + 1) * LANES]
            d_ref[2 * j] = jnp.sum(jnp.where(lo, pj, 0.0), axis=-1, keepdims=True)
            d_ref[2 * j + 1] = jnp.sum(jnp.where(hi, pj, 0.0), axis=-1, keepdims=True)

    row = pl.BlockSpec((tm, BRANCH_W), lambda i: (i, 0))
    return pl.pallas_call(
        body, name=name, grid=(T // tm,), in_specs=[row, row],
        out_specs=pl.BlockSpec((N_HEADS, tm, 1), lambda i: (0, i, 0)),
        out_shape=jax.ShapeDtypeStruct((N_HEADS, T, 1), F32),
        compiler_params=_cp(("parallel",)),
    )(do, o)


def _attn_bwd_dq(proj, do, lse, delta, Fq, Fk, n_seq, name):
    T = proj.shape[0]
    S = T // n_seq
    blk = min(ATT_BLK, S)
    nb = S // blk
    scale = HEAD_DIM ** -0.5
    qc, kc, vc = OFF_Q // LANES, OFF_K // LANES, OFF_V // LANES

    def body(q_ref, k_ref, v_ref, do_ref, lse_ref, dl_ref, fq_ref, fk_ref, dq_ref, acc_s):
        qi, ki = pl.program_id(2), pl.program_id(3)

        @pl.when(ki == 0)
        def _():
            acc_s[...] = jnp.zeros_like(acc_s)

        @pl.when(ki <= qi)
        def _():
            q, k, v, do_ = q_ref[...], k_ref[...], v_ref[...], do_ref[...]
            mask = _causal_mask(qi, ki, blk)
            for hh, sel in enumerate(_pair_masks()):
                s = _attn_logits(q, k, fq_ref[hh], fk_ref[hh], sel, mask, scale)
                p = jnp.exp(s - lse_ref[hh])
                dom = jnp.where(sel, do_, jnp.zeros_like(do_))
                dp = lax.dot_general(dom, v, (((1,), (1,)), ((), ())), preferred_element_type=F32)
                ds = p * (dp - dl_ref[hh])
                acc_s[hh] += jnp.dot(ds.astype(BF16), k, preferred_element_type=F32)

        @pl.when(ki == qi)
        def _():
            lo, _ = _pair_masks()
            dq_ref[...] = (jnp.where(lo, acc_s[0], acc_s[1]) * scale).astype(BF16)

    qmap = lambda b, j, qi, ki: (b * nb + qi, j)
    col1 = pl.BlockSpec((2, blk, 1), lambda b, j, qi, ki: (j, b * nb + qi, 0))
    return pl.pallas_call(
        body, name=name, grid=(n_seq, N_HEADS // 2, nb, nb),
        in_specs=[
            pl.BlockSpec((blk, LANES), lambda b, j, qi, ki: (b * nb + qi, qc + j)),
            pl.BlockSpec((blk, LANES), lambda b, j, qi, ki: (b * nb + jnp.minimum(ki, qi), kc + j)),
            pl.BlockSpec((blk, LANES), lambda b, j, qi, ki: (b * nb + jnp.minimum(ki, qi), vc + j)),
            pl.BlockSpec((blk, LANES), qmap),
            col1, col1, col1,
            pl.BlockSpec((2, 1, blk), lambda b, j, qi, ki: (j, 0, b * nb + jnp.minimum(ki, qi))),
        ],
        out_specs=pl.BlockSpec((blk, LANES), qmap),
        out_shape=jax.ShapeDtypeStruct((T, BRANCH_W), BF16),
        scratch_shapes=[pltpu.VMEM((2, blk, LANES), F32)],
        compiler_params=_cp(("parallel", "parallel", "parallel", "arbitrary")),
    )(proj, proj, proj, do, lse, delta, Fq, Fk)


def _attn_bwd_dkv(proj, do, lse, delta, Fq, Fk, n_seq, name):
    T = proj.shape[0]
    S = T // n_seq
    blk = min(ATT_BLK, S)
    nb = S // blk
    scale = HEAD_DIM ** -0.5
    qc, kc, vc = OFF_Q // LANES, OFF_K // LANES, OFF_V // LANES
    tdot = functools.partial(lax.dot_general, dimension_numbers=(((0,), (0,)), ((), ())),
                             preferred_element_type=F32)

    def body(q_ref, k_ref, v_ref, do_ref, lse_ref, dl_ref, fq_ref, fk_ref, dk_ref, dv_ref, dfk_ref,
             dk_s, dv_s, df_s):
        ki, qi = pl.program_id(2), pl.program_id(3)

        @pl.when(qi == 0)
        def _():
            dk_s[...] = jnp.zeros_like(dk_s)
            dv_s[...] = jnp.zeros_like(dv_s)
            df_s[...] = jnp.zeros_like(df_s)

        @pl.when(qi >= ki)
        def _():
            q, k, v, do_ = q_ref[...], k_ref[...], v_ref[...], do_ref[...]
            mask = _causal_mask(qi, ki, blk)
            for hh, sel in enumerate(_pair_masks()):
                s = _attn_logits(q, k, fq_ref[hh], fk_ref[hh], sel, mask, scale)
                p = jnp.exp(s - lse_ref[hh])
                dv_s[hh] += tdot(p.astype(BF16), do_)
                dom = jnp.where(sel, do_, jnp.zeros_like(do_))
                dp = lax.dot_general(dom, v, (((1,), (1,)), ((), ())), preferred_element_type=F32)
                ds = p * (dp - dl_ref[hh])
                dk_s[hh] += tdot(ds.astype(BF16), q)
                df_s[hh] -= jnp.sum(ds, axis=0, keepdims=True)

        @pl.when(qi == nb - 1)
        def _():
            lo, _ = _pair_masks()
            dk_ref[...] = (jnp.where(lo, dk_s[0], dk_s[1]) * scale).astype(BF16)
            dv_ref[...] = jnp.where(lo, dv_s[0], dv_s[1]).astype(BF16)
            dfk_ref[...] = df_s[...]

    kmap = lambda b, j, ki, qi: (b * nb + ki, j)
    col1 = pl.BlockSpec((2, blk, 1), lambda b, j, ki, qi: (j, b * nb + jnp.maximum(qi, ki), 0))
    rowk = pl.BlockSpec((2, 1, blk), lambda b, j, ki, qi: (j, 0, b * nb + ki))
    return pl.pallas_call(
        body, name=name, grid=(n_seq, N_HEADS // 2, nb, nb),
        in_specs=[
            pl.BlockSpec((blk, LANES), lambda b, j, ki, qi: (b * nb + jnp.maximum(qi, ki), qc + j)),
            pl.BlockSpec((blk, LANES), lambda b, j, ki, qi: (b * nb + ki, kc + j)),
            pl.BlockSpec((blk, LANES), lambda b, j, ki, qi: (b * nb + ki, vc + j)),
            pl.BlockSpec((blk, LANES), lambda b, j, ki, qi: (b * nb + jnp.maximum(qi, ki), j)),
            col1, col1, col1, rowk,
        ],
        out_specs=[pl.BlockSpec((blk, LANES), kmap), pl.BlockSpec((blk, LANES), kmap), rowk],
        out_shape=[jax.ShapeDtypeStruct((T, BRANCH_W), BF16), jax.ShapeDtypeStruct((T, BRANCH_W), BF16),
                   jax.ShapeDtypeStruct((N_HEADS, 1, T), F32)],
        scratch_shapes=[pltpu.VMEM((2, blk, LANES), F32), pltpu.VMEM((2, blk, LANES), F32),
                        pltpu.VMEM((2, 1, blk), F32)],
        compiler_params=_cp(("parallel", "parallel", "parallel", "arbitrary")),
    )(proj, proj, proj, do, lse, delta, Fq, Fk)


AUG0 = HEAD_DIM
Q_TILE, K_CHUNK, ROW_GROUP = 512, 256, 64


def _fox_prep(f, bf, proj, n_seq, name):
    T = f.shape[0]
    S = T // n_seq
    c = min(CUM_BLK, S)

    def body(f_ref, b_ref, qkv_ref, qa_ref, ka_ref, va_ref):
        ri = lax.broadcasted_iota(jnp.int32, (c, c), 0)
        ci = lax.broadcasted_iota(jnp.int32, (c, c), 1)
        tri = (ri >= ci).astype(BF16)
        lane = lax.broadcasted_iota(jnp.int32, (c, LANES), 1)
        carry = jnp.zeros((1, LANES), F32)
        for j in range(S // c):
            rows = slice(j * c, (j + 1) * c)
            lf = _log_sigmoid(f_ref[rows, :] + b_ref[...])
            Fc = _tri_dot(tri, lf) + carry
            carry = carry + jnp.sum(lf, axis=0, keepdims=True)
            for h in range(N_HEADS):
                col = jnp.sum(jnp.where(lane == h, Fc, 0.0), axis=-1, keepdims=True)
                hi = col.astype(BF16).astype(F32)
                r1 = col - hi
                mid = r1.astype(BF16).astype(F32)
                lo = r1 - mid
                ones_q = jnp.logical_and(lane >= AUG0 + 3, lane < AUG0 + 6)
                ones_k = jnp.logical_and(lane >= AUG0, lane < AUG0 + 3)
                aug_q = jnp.where(lane == AUG0, hi, jnp.where(lane == AUG0 + 1, mid, jnp.where(
                    lane == AUG0 + 2, lo, jnp.where(ones_q, 1.0, 0.0))))
                aug_k = jnp.where(lane == AUG0 + 3, -hi, jnp.where(lane == AUG0 + 4, -mid, jnp.where(
                    lane == AUG0 + 5, -lo, jnp.where(ones_k, 1.0, 0.0))))
                base = (h // 2) * TRIPLE
                qp, kp, vp = (qkv_ref[rows, base + t * LANES:base + (t + 1) * LANES].astype(F32) for t in range(3))
                if h % 2:
                    qp, kp, vp = (pltpu.roll(a, HEAD_DIM, 1) for a in (qp, kp, vp))
                out = slice(h * LANES, (h + 1) * LANES)
                qa_ref[rows, out] = jnp.where(lane < HEAD_DIM, qp * (HEAD_DIM ** -0.5), aug_q).astype(BF16)
                ka_ref[rows, out] = jnp.where(lane < HEAD_DIM, kp, aug_k).astype(BF16)
                va_ref[rows, out] = jnp.where(lane < HEAD_DIM, vp, jnp.where(lane == AUG0, 1.0, 0.0)).astype(BF16)

    fblk = pl.BlockSpec((S, LANES), lambda b: (b, 0))
    out = pl.BlockSpec((S, N_HEADS * LANES), lambda b: (b, 0))
    sh = jax.ShapeDtypeStruct((T, N_HEADS * LANES), BF16)
    return pl.pallas_call(
        body, name=name, grid=(n_seq,),
        in_specs=[fblk, pl.BlockSpec((1, LANES), lambda b: (0, 0)),
                  pl.BlockSpec((S, 4 * TRIPLE), lambda b: (b, OFF_QKV // (4 * TRIPLE)))],
        out_specs=[out, out, out], out_shape=[sh, sh, sh],
        compiler_params=_cp(("parallel",)),
    )(f, bf, proj)


def _band_mask(q0, k0, nq, nk):
    row = q0 + lax.broadcasted_iota(jnp.int32, (nq, nk), 0)
    col = k0 + lax.broadcasted_iota(jnp.int32, (nq, nk), 1)
    return col <= row


_NT = (((1,), (1,)), ((), ()))
_TN = (((0,), (0,)), ((), ()))


def _attn_fwd2(qa, ka, va, n_seq, name):
    T = qa.shape[0]
    S = T // n_seq
    tq, tk, rg = min(Q_TILE, S), min(K_CHUNK, S), ROW_GROUP
    nq, per = S // tq, tq // tk

    def body(q_ref, k_ref, v_ref, o_ref, o32_ref, lse_ref, phi_s, plo_s, mp_s, m_s, acc_s):
        qi = pl.program_id(2)
        mp_s[...] = jnp.full_like(mp_s, NEG_INF)
        acc_s[...] = jnp.zeros_like(acc_s)

        def scores(kc, hh):
            k0 = pl.multiple_of(kc * tk, tk)
            hl = slice(hh * LANES, (hh + 1) * LANES)
            return k0, lax.dot_general(q_ref[:, hl], k_ref[pl.ds(k0, tk), hl], _NT, preferred_element_type=F32)

        def max_chunk(kc, masked):
            for hh in range(2):
                k0, s_all = scores(kc, hh)
                for r in range(tq // rg):
                    rows = slice(r * rg, (r + 1) * rg)
                    s = s_all[rows, :]
                    if masked:
                        s = jnp.where(_band_mask(qi * tq + r * rg, k0, rg, tk), s, NEG_INF)
                    part = s[:, :LANES]
                    for c in range(1, tk // LANES):
                        part = jnp.maximum(part, s[:, c * LANES:(c + 1) * LANES])
                    mp_s[hh, rows, :] = jnp.maximum(mp_s[hh, rows, :], part)

        def sum_chunk(kc, masked):
            for hh in range(2):
                k0, s_all = scores(kc, hh)
                hl = slice(hh * LANES, (hh + 1) * LANES)
                v = v_ref[pl.ds(k0, tk), hl]
                for r in range(tq // rg):
                    rows = slice(r * rg, (r + 1) * rg)
                    p = jnp.exp(s_all[rows, :] - m_s[hh, rows])
                    if masked:
                        p = jnp.where(_band_mask(qi * tq + r * rg, k0, rg, tk), p, 0.0)
                    p_hi = p.astype(BF16)
                    phi_s[hh, rows, :] = p_hi
                    plo_s[hh, rows, :] = (p - p_hi.astype(F32)).astype(BF16)
                acc_s[hh] += (jnp.dot(phi_s[hh], v, preferred_element_type=F32)
                              + jnp.dot(plo_s[hh], v, preferred_element_type=F32))

        def sweep(chunk):
            def unmasked(kc, carry):
                chunk(kc, False)
                return carry

            lax.fori_loop(0, qi * per, unmasked, 0)
            for d in range(per):
                chunk(qi * per + d, True)

        sweep(max_chunk)
        m_s[...] = jnp.max(mp_s[...], axis=-1, keepdims=True)
        sweep(sum_chunk)

        lane = lax.broadcasted_iota(jnp.int32, (1, LANES), 1)
        outs = []
        for hh in range(2):
            acc = acc_s[hh]
            l = jnp.sum(jnp.where(lane == AUG0, acc, 0.0), axis=-1, keepdims=True)
            lse_ref[hh] = m_s[hh] + jnp.log(l)
            outs.append(acc / l)
        o = jnp.where(lane < HEAD_DIM, outs[0], pltpu.roll(outs[1], HEAD_DIM, 1))
        o_ref[...] = o.astype(BF16)
        o32_ref[...] = o

    qmap = lambda b, j, qi: (b * nq + qi, j)
    omap = lambda b, j, qi: (b * nq + qi, j)
    kv = pl.BlockSpec((S, 2 * LANES), lambda b, j, qi: (b, j))
    return pl.pallas_call(
        body, name=name, grid=(n_seq, N_HEADS // 2, nq),
        in_specs=[pl.BlockSpec((tq, 2 * LANES), qmap), kv, kv],
        out_specs=[pl.BlockSpec((tq, LANES), omap), pl.BlockSpec((tq, LANES), omap),
                   pl.BlockSpec((2, tq, 1), lambda b, j, qi: (j, b * nq + qi, 0))],
        out_shape=[jax.ShapeDtypeStruct((T, BRANCH_W), BF16), jax.ShapeDtypeStruct((T, BRANCH_W), F32),
                   jax.ShapeDtypeStruct((N_HEADS, T, 1), F32)],
        scratch_shapes=[pltpu.VMEM((2, tq, tk), BF16), pltpu.VMEM((2, tq, tk), BF16),
                        pltpu.VMEM((2, tq, LANES), F32), pltpu.VMEM((2, tq, 1), F32),
                        pltpu.VMEM((2, tq, LANES), F32)],
        compiler_params=_cp(("parallel", "parallel", "parallel")),
    )(qa, ka, va)


def _attn_bwd_dq2(qa, ka, proj, do, lse, delta, n_seq, name):
    T = qa.shape[0]
    S = T // n_seq
    tq, tk, rg = min(Q_TILE, S), min(K_CHUNK, S), ROW_GROUP
    nq, per = S // tq, tq // tk
    vc = OFF_V // LANES

    def body(q_ref, k_ref, v_ref, do_ref, lse_ref, dl_ref, dq_ref, ds_s, acc_s):
        qi = pl.program_id(2)
        acc_s[...] = jnp.zeros_like(acc_s)
        sels = _pair_masks()

        def chunk(kc, masked):
            k0 = pl.multiple_of(kc * tk, tk)
            v = v_ref[pl.ds(k0, tk), :]
            for hh in range(2):
                hl = slice(hh * LANES, (hh + 1) * LANES)
                kh = k_ref[pl.ds(k0, tk), hl]
                s_all = lax.dot_general(q_ref[:, hl], kh, _NT, preferred_element_type=F32)
                dom = jnp.where(sels[hh], do_ref[...], jnp.zeros_like(do_ref[...]))
                dp_all = lax.dot_general(dom, v, _NT, preferred_element_type=F32)
                for r in range(tq // rg):
                    rows = slice(r * rg, (r + 1) * rg)
                    p = jnp.exp(s_all[rows, :] - lse_ref[hh, rows])
                    if masked:
                        p = jnp.where(_band_mask(qi * tq + r * rg, k0, rg, tk), p, 0.0)
                    ds_s[hh, rows, :] = (p * (dp_all[rows, :] - dl_ref[hh, rows])).astype(BF16)
                acc_s[hh] += jnp.dot(ds_s[hh], kh, preferred_element_type=F32)

        def unmasked(kc, carry):
            chunk(kc, False)
            return carry

        lax.fori_loop(0, qi * per, unmasked, 0)
        for d in range(per):
            chunk(qi * per + d, True)
        dq = jnp.where(sels[0], acc_s[0], pltpu.roll(acc_s[1], HEAD_DIM, 1))
        dq_ref[...] = (dq * (HEAD_DIM ** -0.5)).astype(BF16)

    qmap = lambda b, j, qi: (b * nq + qi, j)
    col1 = pl.BlockSpec((2, tq, 1), lambda b, j, qi: (j, b * nq + qi, 0))
    return pl.pallas_call(
        body, name=name, grid=(n_seq, N_HEADS // 2, nq),
        in_specs=[pl.BlockSpec((tq, 2 * LANES), qmap),
                  pl.BlockSpec((S, 2 * LANES), lambda b, j, qi: (b, j)),
                  pl.BlockSpec((S, LANES), lambda b, j, qi: (b, vc + j)),
                  pl.BlockSpec((tq, LANES), qmap), col1, col1],
        out_specs=pl.BlockSpec((tq, LANES), qmap),
        out_shape=jax.ShapeDtypeStruct((T, BRANCH_W), BF16),
        scratch_shapes=[pltpu.VMEM((2, tq, tk), BF16), pltpu.VMEM((2, tq, LANES), F32)],
        compiler_params=_cp(("parallel", "parallel", "parallel")),
    )(qa, ka, proj, do, lse, delta)


def _attn_bwd_dkv2(qa, ka, proj, do, lse, delta, n_seq, name):
    T = qa.shape[0]
    S = T // n_seq
    tkt, tqc, rg = min(Q_TILE, S), min(K_CHUNK, S), ROW_GROUP // 2
    nk, per, nqc = S // tkt, tkt // tqc, S // tqc
    vc = OFF_V // LANES

    def body(q_ref, k_ref, v_ref, do_ref, lse_ref, dl_ref, dk_ref, dv_ref, dfk_ref,
             p_s, ds_s, dk_s, dv_s, df_s):
        ki = pl.program_id(2)
        dk_s[...] = jnp.zeros_like(dk_s)
        dv_s[...] = jnp.zeros_like(dv_s)
        df_s[...] = jnp.zeros_like(df_s)
        sels = _pair_masks()
        v = v_ref[...]

        def chunk(qc, masked):
            q0 = pl.multiple_of(qc * tqc, tqc)
            do_ = do_ref[pl.ds(q0, tqc), :]
            for hh in range(2):
                hl = slice(hh * LANES, (hh + 1) * LANES)
                qh = q_ref[pl.ds(q0, tqc), hl]
                s_all = lax.dot_general(qh, k_ref[:, hl], _NT, preferred_element_type=F32)
                dom = jnp.where(sels[hh], do_, jnp.zeros_like(do_))
                dp_all = lax.dot_general(dom, v, _NT, preferred_element_type=F32)
                dfp = jnp.zeros((1, tkt), F32)
                for r in range(tqc // rg):
                    rows = slice(r * rg, (r + 1) * rg)
                    qrows = pl.ds(q0 + r * rg, rg)
                    p = jnp.exp(s_all[rows, :] - lse_ref[hh, qrows])
                    if masked:
                        p = jnp.where(_band_mask(q0 + r * rg, ki * tkt, rg, tkt), p, 0.0)
                    ds = p * (dp_all[rows, :] - dl_ref[hh, qrows])
                    p_s[hh, rows, :] = p.astype(BF16)
                    ds_s[hh, rows, :] = ds.astype(BF16)
                    dfp = dfp + jnp.sum(ds, axis=0, keepdims=True)
                df_s[hh] -= dfp
                dv_s[hh] += lax.dot_general(p_s[hh], do_, _TN, preferred_element_type=F32)
                dk_s[hh] += lax.dot_general(ds_s[hh], qh, _TN, preferred_element_type=F32)

        for d in range(per):
            chunk(ki * per + d, True)

        def unmasked(qc, carry):
            chunk(qc, False)
            return carry

        lax.fori_loop((ki + 1) * per, nqc, unmasked, 0)
        dk_ref[...] = jnp.where(sels[0], dk_s[0], pltpu.roll(dk_s[1], HEAD_DIM, 1)).astype(BF16)
        dv_ref[...] = jnp.where(sels[0], dv_s[0], dv_s[1]).astype(BF16)
        dfk_ref[...] = df_s[...]

    kmap = lambda b, j, ki: (b * nk + ki, j)
    col1 = pl.BlockSpec((2, S, 1), lambda b, j, ki: (j, b, 0))
    rowk = pl.BlockSpec((2, 1, tkt), lambda b, j, ki: (j, 0, b * nk + ki))
    return pl.pallas_call(
        body, name=name, grid=(n_seq, N_HEADS // 2, nk),
        in_specs=[pl.BlockSpec((S, 2 * LANES), lambda b, j, ki: (b, j)),
                  pl.BlockSpec((tkt, 2 * LANES), kmap),
                  pl.BlockSpec((tkt, LANES), lambda b, j, ki: (b * nk + ki, vc + j)),
                  pl.BlockSpec((S, LANES), lambda b, j, ki: (b, j)), col1, col1],
        out_specs=[pl.BlockSpec((tkt, LANES), kmap), pl.BlockSpec((tkt, LANES), kmap), rowk],
        out_shape=[jax.ShapeDtypeStruct((T, BRANCH_W), BF16), jax.ShapeDtypeStruct((T, BRANCH_W), BF16),
                   jax.ShapeDtypeStruct((N_HEADS, 1, T), F32)],
        scratch_shapes=[pltpu.VMEM((2, tqc, tkt), BF16), pltpu.VMEM((2, tqc, tkt), BF16),
                        pltpu.VMEM((2, tkt, LANES), F32),
                        pltpu.VMEM((2, tkt, LANES), F32), pltpu.VMEM((2, 1, tkt), F32)],
        compiler_params=_cp(("parallel", "parallel", "parallel")),
    )(qa, ka, proj, do, lse, delta)


def _attn_bwd(qa, ka, proj, do, o32, lse, dproj, n_seq, name):
    T = qa.shape[0]
    S = T // n_seq
    tq, tk, rg = min(Q_TILE, S), min(K_CHUNK, S), ROW_GROUP
    nq, per, nkc = S // tq, tq // tk, S // tk

    def body(q_ref, k_ref, v_ref, do_ref, o_ref, lse_ref, _, dqkv_ref, dfk_ref,
             p_s, ds_s, dq_s, dk_s, dv_s, df_s):
        dk_s[...] = jnp.zeros_like(dk_s)
        dv_s[...] = jnp.zeros_like(dv_s)
        df_s[...] = jnp.zeros_like(df_s)
        sels = _pair_masks()

        for qi in range(nq):
            q0 = qi * tq
            do_t = do_ref[q0:q0 + tq, :]
            dq_s[...] = jnp.zeros_like(dq_s)
            prod = do_t.astype(F32) * o_ref[q0:q0 + tq, :]
            dls = [jnp.sum(jnp.where(sel, prod, 0.0), axis=-1, keepdims=True) for sel in sels]

            def chunk(kc, masked, q0=q0, do_t=do_t, dls=dls):
                k0 = pl.multiple_of(kc * tk, tk)
                v = v_ref[pl.ds(k0, tk), :]
                for hh in range(2):
                    hl = slice(hh * LANES, (hh + 1) * LANES)
                    qh, kh = q_ref[q0:q0 + tq, hl], k_ref[pl.ds(k0, tk), hl]
                    s_all = lax.dot_general(qh, kh, _NT, preferred_element_type=F32)
                    dom = jnp.where(sels[hh], do_t, jnp.zeros_like(do_t))
                    dp_all = lax.dot_general(dom, v, _NT, preferred_element_type=F32)
                    dfp = jnp.zeros((1, tk), F32)
                    for r in range(tq // rg):
                        rows = slice(r * rg, (r + 1) * rg)
                        qrows = slice(q0 + r * rg, q0 + (r + 1) * rg)
                        p = jnp.exp(s_all[rows, :] - lse_ref[hh, qrows])
                        if masked:
                            p = jnp.where(_band_mask(q0 + r * rg, k0, rg, tk), p, 0.0)
                        ds = p * (dp_all[rows, :] - dls[hh][rows])
                        p_s[hh, rows, :] = p.astype(BF16)
                        ds_s[hh, rows, :] = ds.astype(BF16)
                        dfp = dfp + jnp.sum(ds, axis=0, keepdims=True)
                    df_s[hh, kc] -= dfp
                    dq_s[hh] += jnp.dot(ds_s[hh], kh, preferred_element_type=F32)
                    dv_s[hh, pl.ds(k0, tk), :] += lax.dot_general(p_s[hh], do_t, _TN, preferred_element_type=F32)
                    dk_s[hh, pl.ds(k0, tk), :] += lax.dot_general(ds_s[hh], qh, _TN, preferred_element_type=F32)

            def unmasked(kc, carry, chunk=chunk):
                chunk(kc, False)
                return carry

            lax.fori_loop(0, qi * per, unmasked, 0)
            for d in range(per):
                chunk(qi * per + d, True)
            dq = jnp.where(sels[0], dq_s[0], pltpu.roll(dq_s[1], HEAD_DIM, 1))
            dqkv_ref[q0:q0 + tq, :LANES] = (dq * (HEAD_DIM ** -0.5)).astype(BF16)

        dqkv_ref[:, LANES:2 * LANES] = jnp.where(sels[0], dk_s[0], pltpu.roll(dk_s[1], HEAD_DIM, 1)).astype(BF16)
        dqkv_ref[:, 2 * LANES:] = jnp.where(sels[0], dv_s[0], dv_s[1]).astype(BF16)
        for c in range(nkc):
            dfk_ref[:, :, c * tk:(c + 1) * tk] = df_s[:, c]

    seq = lambda w: pl.BlockSpec((S, w), lambda b, j: (b, j))
    col1 = pl.BlockSpec((2, S, 1), lambda b, j: (j, b, 0))
    vblk = pl.BlockSpec((S, LANES), lambda b, j: (b, OFF_QKV // LANES + 3 * j + 2))
    return pl.pallas_call(
        body, name=name, grid=(n_seq, N_HEADS // 2),
        in_specs=[seq(2 * LANES), seq(2 * LANES), vblk, seq(LANES), seq(LANES), col1,
                  pl.BlockSpec(memory_space=pl.ANY)],
        out_specs=[pl.BlockSpec((S, TRIPLE), lambda b, j: (b, OFF_QKV // TRIPLE + j)),
                   pl.BlockSpec((2, 1, S), lambda b, j: (j, 0, b))],
        out_shape=[jax.ShapeDtypeStruct(dproj.shape, BF16), jax.ShapeDtypeStruct((N_HEADS, 1, T), F32)],
        input_output_aliases={6: 0},
        scratch_shapes=[pltpu.VMEM((2, tq, tk), BF16), pltpu.VMEM((2, tq, tk), BF16),
                        pltpu.VMEM((2, tq, LANES), F32), pltpu.VMEM((2, S, LANES), F32),
                        pltpu.VMEM((2, S, LANES), F32), pltpu.VMEM((2, nkc, 1, tk), F32)],
        compiler_params=_cp(("parallel", "parallel")),
    )(qa, ka, proj, do, o32, lse, dproj)


def _shift_down(v, k, row):
    return jnp.where(row >= k, pltpu.roll(v, k, 0), 0.0)


def _shift_up(v, k, row, S):
    return jnp.where(row < S - k, pltpu.roll(v, S - k, 0), 0.0)


def _pool_diff(uf, w, row):
    acc, k = uf, 1
    while k < w:
        acc = acc + _shift_down(acc, k, row)
        k *= 2
    n = jnp.minimum(row + 1, w).astype(F32)
    return acc / n - uf


def _pool_fwd(proj, pool_w, pool_scale, n_seq, name):
    T = proj.shape[0]
    S = T // n_seq

    def body(u_ref, w_ref, sc_ref, o_ref, d_s):
        g = pl.program_id(1)
        row = lax.broadcasted_iota(jnp.int32, (S, POOL_GD), 0)
        uf = u_ref[...].astype(F32)
        for gi, wlen in enumerate(POOL_WINDOWS):
            @pl.when(g == gi)
            def _(wlen=wlen):
                d_s[...] = _pool_diff(uf, wlen, row).astype(BF16)
        e = jnp.dot(d_s[...], w_ref[0], preferred_element_type=F32)
        o_ref[...] = (e * sc_ref[...]).astype(BF16)

    uc = OFF_U // POOL_GD
    return pl.pallas_call(
        body, name=name, grid=(n_seq, len(POOL_WINDOWS)),
        in_specs=[pl.BlockSpec((S, POOL_GD), lambda b, g: (b, uc + g)),
                  pl.BlockSpec((1, POOL_GD, POOL_GD), lambda b, g: (g, 0, 0)),
                  pl.BlockSpec((1, POOL_GD), lambda b, g: (0, g))],
        out_specs=pl.BlockSpec((S, POOL_GD), lambda b, g: (b, g)),
        out_shape=jax.ShapeDtypeStruct((T, BRANCH_W), BF16),
        scratch_shapes=[pltpu.VMEM((S, POOL_GD), BF16)],
        compiler_params=_cp(("parallel", "parallel")),
    )(proj, pool_w, pool_scale)


def _pool_bwd(proj, dout, pool_w, pool_scale, dproj, n_seq, name):
    T = proj.shape[0]
    S = T // n_seq

    def body(u_ref, do_ref, w_ref, sc_ref, _, du_ref, dw_ref, dsc_ref, d_s):
        g, b = pl.program_id(0), pl.program_id(1)
        row = lax.broadcasted_iota(jnp.int32, (S, POOL_GD), 0)
        uf = u_ref[...].astype(F32)
        for gi, wlen in enumerate(POOL_WINDOWS):
            @pl.when(g == gi)
            def _(wlen=wlen):
                d_s[...] = _pool_diff(uf, wlen, row).astype(BF16)
        db16 = d_s[...]
        w = w_ref[0]
        e = jnp.dot(db16, w, preferred_element_type=F32)
        dof = do_ref[...].astype(F32)
        dsc = jnp.sum(dof * e, axis=0, keepdims=True)
        de = (dof * sc_ref[...]).astype(BF16)
        dd = lax.dot_general(de, w, (((1,), (1,)), ((), ())), preferred_element_type=F32)
        dw = lax.dot_general(db16, de, (((0,), (0,)), ((), ())), preferred_element_type=F32)
        for gi, wlen in enumerate(POOL_WINDOWS):
            @pl.when(g == gi)
            def _(wlen=wlen):
                n = jnp.minimum(row + 1, wlen).astype(F32)
                acc, k = dd / n, 1
                while k < wlen:
                    acc = acc + _shift_up(acc, k, row, S)
                    k *= 2
                du_ref[...] = (acc - dd).astype(BF16)

        @pl.when(b == 0)
        def _():
            dw_ref[0] = dw
            dsc_ref[...] = dsc

        @pl.when(b > 0)
        def _():
            dw_ref[0] += dw
            dsc_ref[...] += dsc

    uc = OFF_U // POOL_GD
    return pl.pallas_call(
        body, name=name, grid=(len(POOL_WINDOWS), n_seq),
        in_specs=[pl.BlockSpec((S, POOL_GD), lambda g, b: (b, uc + g)),
                  pl.BlockSpec((S, POOL_GD), lambda g, b: (b, g)),
                  pl.BlockSpec((1, POOL_GD, POOL_GD), lambda g, b: (g, 0, 0)),
                  pl.BlockSpec((1, POOL_GD), lambda g, b: (0, g)),
                  pl.BlockSpec(memory_space=pl.ANY)],
        out_specs=[pl.BlockSpec((S, POOL_GD), lambda g, b: (b, uc + g)),
                   pl.BlockSpec((1, POOL_GD, POOL_GD), lambda g, b: (g, 0, 0)),
                   pl.BlockSpec((1, POOL_GD), lambda g, b: (0, g))],
        out_shape=[jax.ShapeDtypeStruct(dproj.shape, BF16),
                   jax.ShapeDtypeStruct((len(POOL_WINDOWS), POOL_GD, POOL_GD), F32),
                   jax.ShapeDtypeStruct((1, BRANCH_W), F32)],
        input_output_aliases={4: 0},
        scratch_shapes=[pltpu.VMEM((S, POOL_GD), BF16)],
        compiler_params=_cp(("parallel", "arbitrary")),
    )(proj, dout, pool_w, pool_scale, dproj)


def _conv_fwd(proj, conv_w, n_seq, name):
    T = proj.shape[0]
    S = T // n_seq
    nc = BRANCH_W // LANES

    def body(c_ref, w_ref, o_ref):
        row = lax.broadcasted_iota(jnp.int32, (S, LANES), 0)
        cv, cb, cc = (c_ref[:, t * LANES:(t + 1) * LANES].astype(F32) for t in range(3))
        z = cc * cv
        w = w_ref[...]
        y = w[0:1] * _shift_down(z, 2, row) + w[1:2] * _shift_down(z, 1, row) + w[2:3] * z
        o_ref[...] = (cb * y).astype(BF16)

    return pl.pallas_call(
        body, name=name, grid=(n_seq, nc),
        in_specs=[pl.BlockSpec((S, TRIPLE), lambda b, j: (b, OFF_CONV // TRIPLE + j)),
                  pl.BlockSpec((CONV_K, LANES), lambda b, j: (0, j))],
        out_specs=pl.BlockSpec((S, LANES), lambda b, j: (b, j)),
        out_shape=jax.ShapeDtypeStruct((T, BRANCH_W), BF16),
        compiler_params=_cp(("parallel", "parallel")),
    )(proj, conv_w)


def _conv_bwd(proj, dout, conv_w, dproj, n_seq, name):
    T = proj.shape[0]
    S = T // n_seq
    nc = BRANCH_W // LANES

    def body(c_ref, do_ref, w_ref, _, dc_ref, dw_ref):
        b = pl.program_id(1)
        row = lax.broadcasted_iota(jnp.int32, (S, LANES), 0)
        cv, cb, cc = (c_ref[:, t * LANES:(t + 1) * LANES].astype(F32) for t in range(3))
        dof = do_ref[...].astype(F32)
        w = w_ref[...]
        z = cc * cv
        z1, z2 = _shift_down(z, 1, row), _shift_down(z, 2, row)
        y = w[0:1] * z2 + w[1:2] * z1 + w[2:3] * z
        dy = dof * cb
        dz = w[2:3] * dy + w[1:2] * _shift_up(dy, 1, row, S) + w[0:1] * _shift_up(dy, 2, row, S)
        dc_ref[:, :LANES] = (dz * cc).astype(BF16)
        dc_ref[:, LANES:2 * LANES] = (dof * y).astype(BF16)
        dc_ref[:, 2 * LANES:] = (dz * cv).astype(BF16)
        dws = [jnp.sum(dy * zk, axis=0, keepdims=True) for zk in (z2, z1, z)]

        @pl.when(b == 0)
        def _():
            for kk in range(CONV_K):
                dw_ref[kk:kk + 1, :] = dws[kk]

        @pl.when(b > 0)
        def _():
            for kk in range(CONV_K):
                dw_ref[kk:kk + 1, :] += dws[kk]

    triple = pl.BlockSpec((S, TRIPLE), lambda j, b: (b, OFF_CONV // TRIPLE + j))
    wsp = pl.BlockSpec((CONV_K, LANES), lambda j, b: (0, j))
    return pl.pallas_call(
        body, name=name, grid=(nc, n_seq),
        in_specs=[triple, pl.BlockSpec((S, LANES), lambda j, b: (b, j)), wsp, pl.BlockSpec(memory_space=pl.ANY)],
        out_specs=[triple, wsp],
        out_shape=[jax.ShapeDtypeStruct(dproj.shape, BF16), jax.ShapeDtypeStruct((CONV_K, BRANCH_W), F32)],
        input_output_aliases={3: 0},
        compiler_params=_cp(("parallel", "arbitrary")),
    )(proj, dout, conv_w, dproj)


def _mix_fwd(oa, ob, oc, wpa, wpp, wpc, proj, b_gate, name):
    T = oa.shape[0]
    tm = min(256, T)

    def body(oa_ref, ob_ref, oc_ref, wa_ref, wp_ref, wc_ref, g_ref, bg_ref, o_ref):
        acc = jnp.zeros((tm, D_MODEL), F32)
        for i, (x_ref, w_ref) in enumerate(((oa_ref, wa_ref), (ob_ref, wp_ref), (oc_ref, wc_ref))):
            y = jnp.dot(x_ref[...], w_ref[...], preferred_element_type=F32)
            sl = slice(i * D_MODEL, (i + 1) * D_MODEL)
            acc = acc + _sigmoid(g_ref[:, sl].astype(F32) + bg_ref[:, sl]) * y
        o_ref[...] = acc.astype(BF16)

    br = pl.BlockSpec((tm, BRANCH_W), lambda i: (i, 0))
    wsp = pl.BlockSpec((BRANCH_W, D_MODEL), lambda i: (0, 0))
    return pl.pallas_call(
        body, name=name, grid=(T // tm,),
        in_specs=[br, br, br, wsp, wsp, wsp, pl.BlockSpec((tm, GATE_W), lambda i: (i, 0)),
                  pl.BlockSpec((1, GATE_W), lambda i: (0, 0))],
        out_specs=pl.BlockSpec((tm, D_MODEL), lambda i: (i, 0)),
        out_shape=jax.ShapeDtypeStruct((T, D_MODEL), BF16),
        compiler_params=_cp(("parallel",)),
    )(oa, ob, oc, wpa, wpp, wpc, proj, b_gate)


def _mix_bwd(oa, ob, oc, wpa, wpp, wpc, proj, b_gate, dmixed, name):
    T = oa.shape[0]
    tm = min(256, T)

    def body(oa_ref, ob_ref, oc_ref, wa_ref, wp_ref, wc_ref, g_ref, bg_ref, dm_ref,
             dya_ref, dyb_ref, dyc_ref, dg_ref, dbg_ref):
        i0 = pl.program_id(0)
        dm = dm_ref[...].astype(F32)
        parts = []
        for i, (x_ref, w_ref, dy_ref) in enumerate(((oa_ref, wa_ref, dya_ref), (ob_ref, wp_ref, dyb_ref),
                                                    (oc_ref, wc_ref, dyc_ref))):
            y = jnp.dot(x_ref[...], w_ref[...], preferred_element_type=F32)
            sl = slice(i * D_MODEL, (i + 1) * D_MODEL)
            gate = _sigmoid(g_ref[:, sl].astype(F32) + bg_ref[:, sl])
            dy_ref[...] = (dm * gate).astype(BF16)
            dgl = dm * y * gate * (1.0 - gate)
            dg_ref[:, sl] = dgl.astype(BF16)
            parts.append(jnp.sum(dgl, axis=0, keepdims=True))

        @pl.when(i0 == 0)
        def _():
            for i in range(3):
                dbg_ref[:, i * D_MODEL:(i + 1) * D_MODEL] = parts[i]

        @pl.when(i0 > 0)
        def _():
            for i in range(3):
                dbg_ref[:, i * D_MODEL:(i + 1) * D_MODEL] += parts[i]

    br = pl.BlockSpec((tm, BRANCH_W), lambda i: (i, 0))
    wsp = pl.BlockSpec((BRANCH_W, D_MODEL), lambda i: (0, 0))
    row = pl.BlockSpec((tm, D_MODEL), lambda i: (i, 0))
    gsp = pl.BlockSpec((tm, GATE_W), lambda i: (i, 0))
    bsp = pl.BlockSpec((1, GATE_W), lambda i: (0, 0))
    act = jax.ShapeDtypeStruct((T, D_MODEL), BF16)
    return pl.pallas_call(
        body, name=name, grid=(T // tm,),
        in_specs=[br, br, br, wsp, wsp, wsp, gsp, bsp, row],
        out_specs=[row, row, row, gsp, bsp],
        out_shape=[act, act, act, jax.ShapeDtypeStruct((T, MAIN_COLS), BF16),
                   jax.ShapeDtypeStruct((1, GATE_W), F32)],
        compiler_params=_cp(("arbitrary",)),
    )(oa, ob, oc, wpa, wpp, wpc, proj, b_gate, dmixed)


GU_TILE = 256


def _gu_col(c):
    t, r = divmod(c, GU_TILE)
    return (t // 2) * GU_TILE + r + (FFN_HIDDEN if t % 2 else 0)


def _gate_up_swiglu(h, w, name):
    T, K = h.shape
    tm = min(2048, T)

    def body(h_ref, w_ref, ab_ref, s_ref):
        prod = jnp.dot(h_ref[...], w_ref[...], preferred_element_type=F32)
        ab_ref[...] = prod.astype(BF16)
        a = prod[:, :GU_TILE]
        s_ref[...] = (a * _sigmoid(a) * prod[:, GU_TILE:]).astype(BF16)

    return pl.pallas_call(
        body, name=name, grid=(T // tm, FFN_HIDDEN // GU_TILE),
        in_specs=[pl.BlockSpec((tm, K), lambda i, j: (i, 0)), pl.BlockSpec((K, 2 * GU_TILE), lambda i, j: (0, j))],
        out_specs=[pl.BlockSpec((tm, 2 * GU_TILE), lambda i, j: (i, j)), pl.BlockSpec((tm, GU_TILE), lambda i, j: (i, j))],
        out_shape=[jax.ShapeDtypeStruct((T, 2 * FFN_HIDDEN), BF16), jax.ShapeDtypeStruct((T, FFN_HIDDEN), BF16)],
        compiler_params=_cp(("parallel", "parallel")),
    )(h, w)


def _swiglu_bwd_fused(dx, w_down, ab, name):
    T, K = dx.shape
    tm = min(2048, T)

    def body(dx_ref, w_ref, ab_ref, o_ref):
        ds = lax.dot_general(dx_ref[...], w_ref[...], _NT, preferred_element_type=F32)
        a = ab_ref[:, :GU_TILE].astype(F32)
        b = ab_ref[:, GU_TILE:].astype(F32)
        sg = _sigmoid(a)
        o_ref[:, :GU_TILE] = (ds * b * sg * (1.0 + a * (1.0 - sg))).astype(BF16)
        o_ref[:, GU_TILE:] = (ds * a * sg).astype(BF16)

    pair = pl.BlockSpec((tm, 2 * GU_TILE), lambda i, j: (i, j))
    return pl.pallas_call(
        body, name=name, grid=(T // tm, FFN_HIDDEN // GU_TILE),
        in_specs=[pl.BlockSpec((tm, K), lambda i, j: (i, 0)), pl.BlockSpec((GU_TILE, K), lambda i, j: (j, 0)), pair],
        out_specs=pair, out_shape=jax.ShapeDtypeStruct((T, 2 * FFN_HIDDEN), BF16),
        compiler_params=_cp(("parallel", "parallel")),
    )(dx, w_down, ab)


def _adamw_update(w_ref, g_ref, m_ref, v_ref, d_ref, nm_ref, nv_ref):
    gv = g_ref[...]
    nm = ADAM_B1 * m_ref[...] + (1.0 - ADAM_B1) * gv
    nv = ADAM_B2 * v_ref[...] + (1.0 - ADAM_B2) * (gv * gv)
    m_hat = nm / (1.0 - ADAM_B1 ** ADAM_STEP)
    v_hat = nv / (1.0 - ADAM_B2 ** ADAM_STEP)
    d_ref[...] = -ADAM_LR * (m_hat / (jnp.sqrt(v_hat) + ADAM_EPS) + ADAM_WD * w_ref[...])
    nm_ref[...] = nm
    nv_ref[...] = nv


def _adamw_many(ws, gs, ms, vs, name):
    n = len(ws)

    def body(*refs):
        ins, outs = refs[:4 * n], refs[4 * n:]
        for t in range(n):
            _adamw_update(ins[t], ins[n + t], ins[2 * n + t], ins[3 * n + t], outs[t], outs[n + t], outs[2 * n + t])

    shapes = [jax.ShapeDtypeStruct(w.shape, F32) for w in ws]
    out = pl.pallas_call(body, name=name, out_shape=shapes * 3, compiler_params=_cp())(*ws, *gs, *ms, *vs)
    return out[:n], out[n:2 * n], out[2 * n:]


def _adamw(w, g, m, v, name):
    R, C = w.shape
    tr = R
    for cand in (256, 352, 128, 64, 8):
        if R > cand and R % cand == 0:
            tr = cand
            break

    def body(w_ref, g_ref, m_ref, v_ref, d_ref, nm_ref, nv_ref):
        _adamw_update(w_ref, g_ref, m_ref, v_ref, d_ref, nm_ref, nv_ref)

    blk = pl.BlockSpec((tr, C), lambda i: (i, 0))
    sh = jax.ShapeDtypeStruct((R, C), F32)
    return pl.pallas_call(
        body, name=name, grid=(R // tr,), in_specs=[blk] * 4, out_specs=[blk] * 3, out_shape=[sh] * 3,
        compiler_params=_cp(("parallel",)),
    )(w, g, m, v)


def _sum_slabs(x, name):
    n, R, C = x.shape
    tr = R
    for cand in (512, 256, 128, 64, 32, 16, 8):
        if R > cand and R % cand == 0:
            tr = cand
            break

    def body(x_ref, o_ref):
        acc = x_ref[0].astype(F32)
        for j in range(1, n):
            acc = acc + x_ref[j].astype(F32)
        o_ref[...] = acc

    return pl.pallas_call(
        body, name=name, grid=(R // tr,), in_specs=[pl.BlockSpec((n, tr, C), lambda i: (0, i, 0))],
        out_specs=pl.BlockSpec((tr, C), lambda i: (i, 0)), out_shape=jax.ShapeDtypeStruct((R, C), F32),
        compiler_params=_cp(("parallel",)),
    )(x)


def _multi_gather(xs, layers, name):
    nt = len(xs)
    shapes = [x.shape if lay is None else x.shape[1:] for x, lay in zip(xs, layers)]

    def body(*refs):
        x_refs, out_refs = refs[:nt], refs[nt:2 * nt]
        send_sems, recv_sems, local_sems = refs[2 * nt:]
        x_, y_, c_ = lax.axis_index("x"), lax.axis_index("y"), lax.axis_index("c")
        me, sibling = (x_, y_, c_), (x_, y_, 1 - c_)
        chips = [(1 - x_, y_), (x_, 1 - y_), (1 - x_, 1 - y_)]

        def own_block(t):
            return x_refs[t] if layers[t] is None else x_refs[t].at[layers[t]]

        def copy(t, k, block, to, own=False):
            px, py, pc = block
            dst = out_refs[t].at[4 * px + 2 * py + pc]
            return pltpu.make_async_remote_copy(
                src_ref=own_block(t) if own else dst, dst_ref=dst,
                send_sem=send_sems.at[t, k], recv_sem=recv_sems.at[t, k],
                device_id=to, device_id_type=pl.DeviceIdType.MESH)

        mine, first, passed = [], [], []
        for t in range(nt):
            mine.append(pltpu.make_async_copy(own_block(t), out_refs[t].at[4 * x_ + 2 * y_ + c_], local_sems.at[t]))
            mine[-1].start()
            first.append([copy(t, 1 + j, me, (*chip, c_), own=True) for j, chip in enumerate(chips)]
                         + [copy(t, 0, me, sibling, own=True)])
            for cp in first[-1]:
                cp.start()
        for t in range(nt):
            for j, chip in enumerate(chips):
                copy(t, 1 + j, (*chip, c_), me).wait_recv()
                passed.append(copy(t, 4 + j, (*chip, c_), sibling))
                passed[-1].start()
        for t in range(nt):
            copy(t, 0, sibling, me).wait_recv()
            for j, chip in enumerate(chips):
                copy(t, 4 + j, (*chip, 1 - c_), me).wait_recv()
        for cp in [c for f in first for c in f] + passed:
            cp.wait_send()
        for cp in mine:
            cp.wait()

    hbm = pl.BlockSpec(memory_space=pl.ANY)
    return pl.pallas_call(
        body, name=name, out_shape=[jax.ShapeDtypeStruct((N_DEV,) + tuple(s), x.dtype) for s, x in zip(shapes, xs)],
        in_specs=[hbm] * nt, out_specs=[hbm] * nt,
        scratch_shapes=[pltpu.SemaphoreType.DMA((nt, 7)), pltpu.SemaphoreType.DMA((nt, 7)),
                        pltpu.SemaphoreType.DMA((nt,))],
    )(*xs)


def _multi_exchange(sends, name):
    nt = len(sends)

    def body(*refs):
        s_refs, r_refs = refs[:nt], refs[nt:2 * nt]
        send_sems, recv_sems, local_sems = refs[2 * nt:]
        x_, y_, c_ = lax.axis_index("x"), lax.axis_index("y"), lax.axis_index("c")
        me = 4 * x_ + 2 * y_ + c_
        mine, out, inc = [], [], []
        for t in range(nt):
            mine.append(pltpu.make_async_copy(s_refs[t].at[me], r_refs[t].at[me], local_sems.at[t]))
            mine[-1].start()
        for k in (2, 4, 6, 3, 5, 7, 1):
            px, py, pc = x_ ^ ((k >> 2) & 1), y_ ^ ((k >> 1) & 1), c_ ^ (k & 1)
            peer = 4 * px + 2 * py + pc
            for t in range(nt):
                def copy(src, dst):
                    return pltpu.make_async_remote_copy(
                        src_ref=s_refs[t].at[src], dst_ref=r_refs[t].at[dst],
                        send_sem=send_sems.at[t, k - 1], recv_sem=recv_sems.at[t, k - 1],
                        device_id=(px, py, pc), device_id_type=pl.DeviceIdType.MESH)

                out.append(copy(peer, me))
                inc.append(copy(me, peer))
        for cp in out:
            cp.start()
        for cp in inc:
            cp.wait_recv()
        for cp in out:
            cp.wait_send()
        for cp in mine:
            cp.wait()

    hbm = pl.BlockSpec(memory_space=pl.ANY)
    return pl.pallas_call(
        body, name=name, out_shape=[jax.ShapeDtypeStruct(s.shape, s.dtype) for s in sends],
        in_specs=[hbm] * nt, out_specs=[hbm] * nt,
        scratch_shapes=[pltpu.SemaphoreType.DMA((nt, N_DEV - 1)), pltpu.SemaphoreType.DMA((nt, N_DEV - 1)),
                        pltpu.SemaphoreType.DMA((nt,))],
    )(*sends)


_HBM = pl.BlockSpec(memory_space=pltpu.HBM)
_SEM = pl.BlockSpec(memory_space=pltpu.SEMAPHORE)
_PEER_ORDER = (2, 4, 6, 3, 5, 7, 1)


def _split_copies(src_refs, land_refs, send_sems, recv_sems, layers, per_peer):
    x_, y_, c_ = lax.axis_index("x"), lax.axis_index("y"), lax.axis_index("c")
    me = 4 * x_ + 2 * y_ + c_
    copies = []
    for k in _PEER_ORDER:
        px, py, pc = x_ ^ ((k >> 2) & 1), y_ ^ ((k >> 1) & 1), c_ ^ (k & 1)
        peer = 4 * px + 2 * py + pc
        for t in range(len(src_refs)):
            if per_peer:
                src = src_refs[t].at[peer]
            else:
                src = src_refs[t] if layers[t] is None else src_refs[t].at[layers[t]]
            copies.append(pltpu.make_async_remote_copy(
                src_ref=src, dst_ref=land_refs[t].at[me],
                send_sem=send_sems.at[t * (N_DEV - 1) + k - 1], recv_sem=recv_sems.at[t * (N_DEV - 1) + k - 1],
                device_id=(px, py, pc), device_id_type=pl.DeviceIdType.MESH))
    return copies


def _own_copies(src_refs, land_refs, sems, layers, per_peer):
    nt = len(src_refs)
    me = 4 * lax.axis_index("x") + 2 * lax.axis_index("y") + lax.axis_index("c")
    copies = []
    for t in range(nt):
        if per_peer:
            src = src_refs[t].at[me]
        else:
            src = src_refs[t] if layers[t] is None else src_refs[t].at[layers[t]]
        copies.append(pltpu.make_async_copy(src, land_refs[t].at[me], sems.at[nt * (N_DEV - 1) + t]))
    return copies


def _split_start(srcs, layers, per_peer, after, name):
    nt = len(srcs)
    if per_peer:
        land_shapes = [s.shape for s in srcs]
    else:
        land_shapes = [(N_DEV,) + tuple(s.shape if lay is None else s.shape[1:]) for s, lay in zip(srcs, layers)]

    def body(*refs):
        src_refs, land_refs = refs[:nt], refs[nt:2 * nt]
        send_sems, recv_sems = refs[2 * nt + 1], refs[2 * nt + 2]
        token = refs[-1]
        for cp in _split_copies(src_refs, land_refs, send_sems, recv_sems, layers, per_peer):
            cp.start()
        for cp in _own_copies(src_refs, land_refs, send_sems, layers, per_peer):
            cp.start()
        token[...] = jnp.zeros_like(token)

    lands = [pltpu.with_memory_space_constraint(lax.empty(s, x.dtype), pltpu.HBM) for s, x in zip(land_shapes, srcs)]
    srcs = [pltpu.with_memory_space_constraint(x, pltpu.HBM) for x in srcs]
    out = pl.pallas_call(
        body, name=name,
        out_shape=(pltpu.SemaphoreType.DMA((nt * N_DEV,)), pltpu.SemaphoreType.DMA((nt * (N_DEV - 1),)),
                   *[pltpu.HBM(x.shape, x.dtype) for x in srcs], *[pltpu.HBM(s, x.dtype) for s, x in zip(land_shapes, srcs)],
                   jax.ShapeDtypeStruct((8, LANES), F32)),
        in_specs=[_HBM] * (2 * nt) + [pl.BlockSpec(memory_space=pl.ANY)],
        out_specs=(_SEM, _SEM, *([_HBM] * (2 * nt)), pl.BlockSpec(memory_space=pltpu.VMEM)),
        input_output_aliases={i: 2 + i for i in range(2 * nt)},
        compiler_params=pltpu.CompilerParams(has_side_effects=pltpu.SideEffectType.DATAFLOW_SIDE_EFFECTING),
    )(*srcs, *lands, after)
    return out[0], out[1], list(out[2:2 + nt]), list(out[2 + nt:2 + 2 * nt]), out[-1]


def _split_wait(started, layers, per_peer, after, name):
    send_sems, recv_sems, srcs, lands, _ = started
    nt = len(srcs)

    def body(*refs):
        src_refs, land_refs = refs[:nt], refs[nt:2 * nt]
        s_sems, r_sems = refs[2 * nt], refs[2 * nt + 1]
        for cp in _split_copies(src_refs, land_refs, s_sems, r_sems, layers, per_peer):
            cp.wait_send()
            cp.wait_recv()
        for cp in _own_copies(src_refs, land_refs, s_sems, layers, per_peer):
            cp.wait()

    out = pl.pallas_call(
        body, name=name,
        out_shape=tuple(pltpu.HBM(x.shape, x.dtype) for x in srcs + lands),
        in_specs=[_HBM] * (2 * nt) + [_SEM, _SEM, pl.BlockSpec(memory_space=pl.ANY)],
        out_specs=tuple([_HBM] * (2 * nt)),
        input_output_aliases={i: i for i in range(2 * nt)},
        compiler_params=pltpu.CompilerParams(has_side_effects=pltpu.SideEffectType.DATAFLOW_SIDE_EFFECTING),
    )(*srcs, *lands, send_sems, recv_sems, after)
    return list(out[nt:])


def _with_own(land, own):
    me = 4 * lax.axis_index("x") + 2 * lax.axis_index("y") + lax.axis_index("c")
    return lax.dynamic_update_slice_in_dim(land, own[None], me, axis=0)


def _runs(mapping):
    runs, c, n = [], 0, len(mapping)
    while c < n:
        if mapping[c] is None:
            c += 1
            continue
        sid, d, lo = mapping[c][0], mapping[c][1] - c, c
        while c < n and mapping[c] is not None and mapping[c][0] == sid and mapping[c][1] - c == d:
            c += 1
        runs.append((lo, c, sid, d))
    return runs


def _tile_plan(mapping, src_widths):
    runs = _runs(mapping)
    plan = []
    for t in range(len(mapping) // LANES):
        pieces = []
        for lo, hi, sid, d in runs:
            lo_t, hi_t = max(lo, t * LANES), min(hi, (t + 1) * LANES)
            if lo_t >= hi_t:
                continue
            a = ((lo_t + d) // LANES) * LANES
            win = min(2 * LANES, src_widths[sid] - a)
            shift = t * LANES + d - a
            pieces.append((sid, a, win, shift, lo_t - t * LANES, hi_t - t * LANES))
        plan.append(pieces)
    return plan


def _reblock(srcs, src_views, outs, out_views, name):
    R = srcs[0].shape[-2]
    tr = min(512, R)
    widths = {sid: srcs[ai].shape[-1] for sid, (ai, _) in src_views.items()}
    plans = [(ai, li, _tile_plan(mapping, widths)) for ai, li, mapping in out_views]
    ns = len(srcs)

    def body(*refs):
        s_refs, o_refs = refs[:ns], refs[ns:]
        cache = {}

        def shift_matrix(win, shift, lo, hi):
            key = (win, shift, lo, hi)
            if key not in cache:
                r = lax.broadcasted_iota(jnp.int32, (win, LANES), 0)
                c = lax.broadcasted_iota(jnp.int32, (win, LANES), 1)
                hit = jnp.logical_and(r - c == shift, jnp.logical_and(c >= lo, c < hi))
                cache[key] = jnp.where(hit, 1.0, 0.0).astype(BF16)
            return cache[key]

        for ai, li, plan in plans:
            for t, pieces in enumerate(plan):
                acc = None
                whole = len(pieces) == 1 and pieces[0][3:] == (0, 0, LANES)
                for sid, a, win, shift, lo, hi in pieces:
                    sa, sl = src_views[sid]
                    if whole:
                        win = LANES
                    src = s_refs[sa][:, a:a + win] if sl is None else s_refs[sa][sl, :, a:a + win]
                    if whole:
                        acc = src
                    else:
                        part = jnp.dot(src, shift_matrix(win, shift, lo, hi), preferred_element_type=F32)
                        acc = part if acc is None else acc + part
                val = jnp.zeros((tr, LANES), BF16) if acc is None else acc.astype(BF16)
                if li is None:
                    o_refs[ai][:, t * LANES:(t + 1) * LANES] = val
                else:
                    o_refs[ai][li, :, t * LANES:(t + 1) * LANES] = val

    def spec(shape):
        if len(shape) == 2:
            return pl.BlockSpec((tr, shape[1]), lambda i: (i, 0))
        return pl.BlockSpec((shape[0], tr, shape[2]), lambda i: (0, i, 0))

    return pl.pallas_call(
        body, name=name, grid=(R // tr,), in_specs=[spec(s.shape) for s in srcs],
        out_specs=[spec(s) for s in outs], out_shape=[jax.ShapeDtypeStruct(s, BF16) for s in outs],
        compiler_params=_cp(("parallel",)),
    )(*srcs)


SHARDED = ("w_in", "w_gate_up", "w_proj_attn", "w_proj_pool", "w_proj_conv", "w_out", "w_down")
WEIGHT_ORDER = ("attn_norm", "w_in", "b_forget", "b_gate", "w_proj_attn", "pool_w", "pool_scale", "w_proj_pool",
                "conv_w", "w_proj_conv", "w_out", "ffn_norm", "w_gate_up", "w_down", "final_norm")
IN_SHARD, IN_SHARD_PAD = IN_COLS // N_DEV, 896
GU_SHARD, GU_SHARD_PAD = 2 * FFN_HIDDEN // N_DEV, 768


def _w_in_col(c):
    if c < OFF_QKV:
        return c + 3592
    if c < OFF_U:
        base, off = (0, OFF_QKV) if c < OFF_CONV else (2056, OFF_CONV)
        j, t = divmod(c - off, TRIPLE)
        which, e = divmod(t, LANES)
        return base + which * BRANCH_W + j * LANES + e
    return c - OFF_U + 1544


def _w_in_full(gathered, name):
    main = [divmod(_w_in_col(c), IN_SHARD) for c in range(MAIN_COLS)]
    fcols = [divmod(1536 + c, IN_SHARD) if c < N_HEADS else None for c in range(LANES)]
    R = gathered.shape[1]
    return _reblock([gathered], {i: (0, i) for i in range(N_DEV)}, [(R, MAIN_COLS), (R, LANES)],
                    [(0, None, main), (1, None, fcols)], name)


def _w_in_slabs(dmain, dwf, name):
    inv = {_w_in_col(c): ("m", c) for c in range(MAIN_COLS)}
    inv.update({1536 + c: ("f", c) for c in range(N_HEADS)})
    views = []
    for i in range(N_DEV):
        mapping = [inv[IN_SHARD * i + j] if j < IN_SHARD else None for j in range(IN_SHARD_PAD)]
        views.append((0, i, mapping))
    R = dmain.shape[0]
    return _reblock([dmain, dwf], {"m": (0, None), "f": (1, None)}, [(N_DEV, R, IN_SHARD_PAD)], views, name)[0]


def _w_gu_full(gathered, name):
    mapping = [divmod(_gu_col(c), GU_SHARD) for c in range(2 * FFN_HIDDEN)]
    R = gathered.shape[1]
    return _reblock([gathered], {i: (0, i) for i in range(N_DEV)}, [(R, 2 * FFN_HIDDEN)], [(0, None, mapping)], name)[0]


def _w_gu_slabs(dw, name):
    inv = {_gu_col(c): c for c in range(2 * FFN_HIDDEN)}
    views = [(0, i, [("w", inv[GU_SHARD * i + j]) if j < GU_SHARD else None for j in range(GU_SHARD_PAD)])
             for i in range(N_DEV)]
    R = dw.shape[0]
    return _reblock([dw], {"w": (0, None)}, [(N_DEV, R, GU_SHARD_PAD)], views, name)[0]


def _layer_fwd(x, W, n_seq, l, h1=None, next_norm=None):
    T = x.shape[0]
    sfx = f"_l{l}"
    if h1 is None:
        h1 = _rms_fwd(x, W["attn_norm"], "rms1" + sfx)
    proj = _matmul(h1, W["w_main"], mode="nn", out_dtype=BF16, name="proj_main" + sfx)
    f = _matmul(h1, W["w_f"], mode="nn", out_dtype=F32, name="proj_f" + sfx)
    qa, ka, va = _fox_prep(f, W["b_forget"], proj, n_seq, "fox_prep" + sfx)
    oa, oa32, lse = _attn_fwd2(qa, ka, va, n_seq, "attn_fwd" + sfx)
    if "late" in W:
        W.update(W.pop("late")(oa))
    ob = _pool_fwd(proj, W["pool_w"], W["pool_scale"], n_seq, "pool_fwd" + sfx)
    oc = _conv_fwd(proj, W["conv_w"], n_seq, "conv_fwd" + sfx)
    mixed = _mix_fwd(oa, ob, oc, W["w_proj_attn"], W["w_proj_pool"], W["w_proj_conv"], proj, W["b_gate"],
                     "mix_fwd" + sfx)
    x2, h2 = _matmul(mixed, W["w_out"], mode="nn", out_dtype=F32, name="out_proj" + sfx, tm=1024, tn=1024,
                     residual=x, rms_g=W["ffn_norm"])
    ab, s = _gate_up_swiglu(h2, W["w_gate_up"], "gate_up" + sfx)
    x3 = _matmul(s, W["w_down"], mode="nn", out_dtype=F32, name="down" + sfx, tm=1024, tn=1024, tk=1408,
                 residual=x2, rms_g=next_norm)
    x3, h1_next = x3 if next_norm is not None else (x3, None)
    saved = dict(x=x, h1=h1, proj=proj, f=f, qa=qa, ka=ka, oa=oa, oa32=oa32, lse=lse, ob=ob, oc=oc, mixed=mixed, x2=x2,
                 h2=h2, ab=ab, s=s)
    return x3, saved, h1_next


def _layer_bwd(dx3, dx3b, W, sv, n_seq, l, stage=None):
    T = dx3.shape[0]
    sfx = f"_l{l}"
    G = {}
    stage = stage or (lambda l, group, G, W: W)
    dab = _swiglu_bwd_fused(dx3b, W["w_down"], sv["ab"], "d_ab" + sfx)
    G["w_down"] = _matmul(sv["s"], dx3b, mode="tn", out_dtype=BF16, name="dw_down" + sfx, tm=256, tn=1024)
    dh2 = _matmul(dab, W["w_gate_up"], mode="nt", out_dtype=BF16, name="d_h2" + sfx, tm=1024, tn=1024, tk=1408)
    G["w_gate_up"] = _matmul(sv["h2"], dab, mode="tn", out_dtype=BF16, name="dw_gate_up" + sfx, tm=1024)
    W = stage(l, "ffn", G, W)
    dx2, dx2b, G["ffn_norm"] = _rms_bwd(sv["x2"], W["ffn_norm"], dh2, dx3, "rms2_bwd" + sfx)
    dmixed = _matmul(dx2b, W["w_out"], mode="nt", out_dtype=BF16, name="d_mixed" + sfx)
    G["w_out"] = _matmul(sv["mixed"], dx2b, mode="tn", out_dtype=BF16, name="dw_out" + sfx, tm=1024)
    dya, dyb, dyc, dproj, G["b_gate"] = _mix_bwd(sv["oa"], sv["ob"], sv["oc"], W["w_proj_attn"], W["w_proj_pool"],
                                                 W["w_proj_conv"], sv["proj"], W["b_gate"], dmixed, "mix_bwd" + sfx)
    douts = {}
    for br, dy, o in (("attn", dya, sv["oa"]), ("pool", dyb, sv["ob"]), ("conv", dyc, sv["oc"])):
        douts[br] = _matmul(dy, W["w_proj_" + br], mode="nt", out_dtype=BF16, name=f"d_{br}_out" + sfx)
        G["w_proj_" + br] = _matmul(o, dy, mode="tn", out_dtype=BF16, name=f"dw_proj_{br}" + sfx, tm=512)
    W = stage(l, "mix", G, W)
    dproj, G["conv_w"] = _conv_bwd(sv["proj"], douts["conv"], W["conv_w"], dproj, n_seq, "conv_bwd" + sfx)
    dproj, G["pool_w"], G["pool_scale"] = _pool_bwd(sv["proj"], douts["pool"], W["pool_w"], W["pool_scale"], dproj,
                                                    n_seq, "pool_bwd" + sfx)
    dproj, dFk = _attn_bwd(sv["qa"], sv["ka"], sv["proj"], douts["attn"], sv["oa32"], sv["lse"], dproj, n_seq,
                           "attn_bwd" + sfx)
    dF = jnp.pad(dFk.reshape(N_HEADS, T).T, ((0, 0), (0, LANES - N_HEADS)))
    df, G["b_forget"] = _fox_cumsum_bwd(sv["f"], W["b_forget"], dF, n_seq, "fox_cumsum_bwd" + sfx)
    G["w_main"] = _matmul(sv["h1"], dproj, mode="tn", out_dtype=BF16, name="dw_main" + sfx, tm=1024)
    G["w_f"] = _matmul(sv["h1"], df, mode="tn", out_dtype=BF16, name="dw_f" + sfx, tm=1024)
    W = stage(l, "w_in", G, W)
    dh1 = _matmul(df, W["w_f"], mode="nt", out_dtype=F32, name="d_h1_f" + sfx)
    dh1 = _matmul(dproj, W["w_main"], mode="nt", out_dtype=F32, name="d_h1_main" + sfx, tm=1024, tn=1024, tk=1664,
                  residual=dh1)
    dx, dxb, G["attn_norm"] = _rms_bwd(sv["x"], W["attn_norm"], dh1, dx2, "rms1_bwd" + sfx)
    return dx, dxb, G


def _replicated_operands(rep, l):
    W = {}
    W["attn_norm"], W["ffn_norm"] = rep["attn_norm"][l], rep["ffn_norm"][l]
    W["b_forget"] = jnp.pad(rep["b_forget"][l].reshape(1, N_HEADS), ((0, 0), (0, LANES - N_HEADS)))
    W["b_gate"] = rep["b_gate"][l].reshape(1, GATE_W)
    W["pool_w"] = rep["pool_w"][l].astype(BF16)
    W["pool_scale"] = rep["pool_scale"][l].reshape(1, BRANCH_W)
    return W


def _local_step(x, target, get_W, attn_norms, final_norm, stage=None):
    n_seq, S, Dm = x.shape
    T = n_seq * S
    xt = x.reshape(T, Dm)
    saved, Ws, h1 = [], [], None
    for l in range(DEPTH):
        Ws.append(get_W(l, xt))
        next_norm = attn_norms[l + 1] if l + 1 < DEPTH else None
        xt, sv, h1 = _layer_fwd(xt, Ws[l], n_seq, l, h1, next_norm)
        saved.append(sv)
    loss, dx, dxb, g_final = _loss_head(xt, final_norm, target.reshape(T, Dm), "loss_head")
    grads = [None] * DEPTH
    for l in reversed(range(DEPTH)):
        dx, dxb, grads[l] = _layer_bwd(dx, dxb, Ws[l], saved[l], n_seq, l, stage)
    return loss, dx.reshape(n_seq, S, Dm), grads, g_final


def _padded_shards(weights):
    pads = {"w_in": IN_SHARD_PAD - IN_SHARD, "w_gate_up": GU_SHARD_PAD - GU_SHARD}
    return {n: jnp.pad(weights[n], ((0, 0), (0, 0), (0, pads.get(n, 0)))).astype(BF16) for n in SHARDED}


def _full_operands(g, l):
    W = {}
    if "w_in" in g:
        W["w_main"], W["w_f"] = _w_in_full(g["w_in"], f"w_in_full_l{l}")
    if "w_gate_up" in g:
        W["w_gate_up"] = _w_gu_full(g["w_gate_up"], f"w_gate_up_full_l{l}")
    for n in ("w_proj_attn", "w_proj_pool", "w_proj_conv"):
        if n in g:
            W[n] = jnp.transpose(g[n], (1, 0, 2)).reshape(BRANCH_W, D_MODEL)
    if "w_out" in g:
        W["w_out"] = g["w_out"].reshape(D_MODEL, D_MODEL)
    if "w_down" in g:
        W["w_down"] = g["w_down"].reshape(FFN_HIDDEN, D_MODEL)
    return W


GRAD_GROUPS = {"ffn": ("w_down", "w_gate_up"),
               "mix": ("w_out", "w_proj_attn", "w_proj_pool", "w_proj_conv"),
               "w_in": ("w_in",)}


def _grad_slabs(G, n, l):
    if n == "w_in":
        return _w_in_slabs(G["w_main"], G["w_f"], f"w_in_slabs_l{l}")
    if n == "w_gate_up":
        return _w_gu_slabs(G["w_gate_up"], f"w_gate_up_slabs_l{l}")
    if n == "w_out":
        return G["w_out"].reshape(N_DEV, D_MODEL // N_DEV, D_MODEL)
    if n == "w_down":
        return G["w_down"].reshape(N_DEV, FFN_HIDDEN // N_DEV, D_MODEL)
    return jnp.transpose(G[n].reshape(BRANCH_W, N_DEV, D_MODEL // N_DEV), (1, 0, 2))


def _sum_layer_grads(recv, l):
    out = {n: _sum_slabs(r, f"sum_{n}_l{l}") for n, r in recv.items()}
    if "w_in" in out:
        out["w_in"] = out["w_in"][:, :IN_SHARD]
    if "w_gate_up" in out:
        out["w_gate_up"] = out["w_gate_up"][:, :GU_SHARD]
    return out


def _sum_small(xs, name):
    def body(*refs):
        for x_ref, o_ref in zip(refs[:len(xs)], refs[len(xs):]):
            acc = x_ref[0]
            for j in range(1, N_DEV):
                acc = acc + x_ref[j]
            o_ref[...] = acc

    return pl.pallas_call(
        body, name=name, out_shape=[jax.ShapeDtypeStruct(x.shape[1:], F32) for x in xs],
        compiler_params=_cp(),
    )(*xs)


def _as_2d(a):
    if a.ndim == 1:
        return a.reshape(1, -1)
    return a.reshape(-1, a.shape[-1])


def kernel(x, attn_norm, w_in, b_forget, b_gate, w_proj_attn, pool_w, pool_scale, w_proj_pool, conv_w, w_proj_conv, w_out, ffn_norm, w_gate_up, w_down, final_norm, loss_target, m_attn_norm, m_w_in, m_b_forget, m_b_gate, m_w_proj_attn, m_pool_w, m_pool_scale, m_w_proj_pool, m_conv_w, m_w_proj_conv, m_w_out, m_ffn_norm, m_w_gate_up, m_w_down, m_final_norm, v_attn_norm, v_w_in, v_b_forget, v_b_gate, v_w_proj_attn, v_pool_w, v_pool_scale, v_w_proj_pool, v_conv_w, v_w_proj_conv, v_w_out, v_ffn_norm, v_w_gate_up, v_w_down, v_final_norm):
    weights = dict(attn_norm=attn_norm, w_in=w_in, b_forget=b_forget, b_gate=b_gate, w_proj_attn=w_proj_attn,
                   pool_w=pool_w, pool_scale=pool_scale, w_proj_pool=w_proj_pool, conv_w=conv_w,
                   w_proj_conv=w_proj_conv, w_out=w_out, ffn_norm=ffn_norm, w_gate_up=w_gate_up, w_down=w_down,
                   final_norm=final_norm)
    moments_m = dict(attn_norm=m_attn_norm, w_in=m_w_in, b_forget=m_b_forget, b_gate=m_b_gate,
                     w_proj_attn=m_w_proj_attn, pool_w=m_pool_w, pool_scale=m_pool_scale, w_proj_pool=m_w_proj_pool,
                     conv_w=m_conv_w, w_proj_conv=m_w_proj_conv, w_out=m_w_out, ffn_norm=m_ffn_norm,
                     w_gate_up=m_w_gate_up, w_down=m_w_down, final_norm=m_final_norm)
    moments_v = dict(attn_norm=v_attn_norm, w_in=v_w_in, b_forget=v_b_forget, b_gate=v_b_gate,
                     w_proj_attn=v_w_proj_attn, pool_w=v_pool_w, pool_scale=v_pool_scale, w_proj_pool=v_w_proj_pool,
                     conv_w=v_conv_w, w_proj_conv=v_w_proj_conv, w_out=v_w_out, ffn_norm=v_ffn_norm,
                     w_gate_up=v_w_gate_up, w_down=v_w_down, final_norm=v_final_norm)

    sh = _padded_shards(weights)
    names = list(SHARDED)
    rest = [n for n in names if n != "w_in"]
    me = 4 * lax.axis_index("x") + 2 * lax.axis_index("y") + lax.axis_index("c")
    w_in0, conv_all = _multi_gather([sh["w_in"], conv_w], [0, None], "gather_w_in_l0")
    started, after = {}, w_in0
    for l in range(DEPTH):
        for group, gnames in (("w_in", ["w_in"]), ("rest", rest)):
            if (l, group) != (0, "w_in"):
                started[l, group] = _split_start([sh[n] for n in gnames], [l] * len(gnames), False, after,
                                                 f"gather_start_{group}_l{l}")
                after = started[l, group][4]
    last_token = after

    def get_W(l, xt):
        if l == 0:
            w_in = w_in0
        else:
            w_in = _split_wait(started[l, "w_in"], [l], False, xt, f"gather_wait_w_in_l{l}")[0]
        W = _full_operands({"w_in": w_in}, l)

        def late(after):
            lands = _split_wait(started[l, "rest"], [l] * len(rest), False, after, f"gather_wait_rest_l{l}")
            return _full_operands(dict(zip(rest, lands)), l)

        W["late"] = late
        W.update(_replicated_operands(weights, l))
        W["conv_w"] = jnp.transpose(conv_all[:, l], (1, 0, 2)).reshape(CONV_K, BRANCH_W)
        if l == 0:
            W["attn_norm"] = W["attn_norm"] + last_token[0, 0]
        return W

    exchanges = []

    def stage(l, group, G, W):
        gnames = GRAD_GROUPS[group]
        slabs = [_grad_slabs(G, n, l) for n in gnames]
        started = _split_start(slabs, None, True, slabs[0], f"exchange_start_{group}_l{l}")
        exchanges.append((l, group, gnames, slabs, started))
        tie = {"ffn": "ffn_norm", "mix": "conv_w", "w_in": "w_f"}[group]
        W = dict(W)
        W[tie] = W[tie] + started[4][0, 0].astype(W[tie].dtype)
        return W

    loss_part, grad_x, grads, g_final = _local_step(x, loss_target, get_W, attn_norm, final_norm, stage)
    after = grad_x
    for l, group, gnames, slabs, started in exchanges:
        lands = _split_wait(started, None, True, after, f"exchange_wait_{group}_l{l}")
        grads[l].update(_sum_layer_grads(dict(zip(gnames, lands)), l))
    gw = {n: jnp.stack([grads[l][n] for l in range(DEPTH)]) for n in SHARDED}

    small = ("attn_norm", "b_forget", "b_gate", "pool_w", "pool_scale", "ffn_norm", "conv_w")
    parts = [jnp.stack([grads[l][n] for l in range(DEPTH)]) for n in small] + [g_final, loss_part]
    gathered = _multi_gather(parts, [None] * len(parts), "gather_small_grads")
    summed = _sum_small(gathered, "sum_small_grads")
    for n, s in zip(small, summed):
        gw[n] = s
    gw["attn_norm"], gw["ffn_norm"] = gw["attn_norm"][:, 0], gw["ffn_norm"][:, 0]
    gw["b_forget"] = gw["b_forget"][:, 0, :N_HEADS]
    gw["b_gate"], gw["pool_scale"] = gw["b_gate"][:, 0], gw["pool_scale"][:, 0]
    gw["conv_w"] = lax.dynamic_slice_in_dim(gw["conv_w"], me * (BRANCH_W // N_DEV), BRANCH_W // N_DEV, axis=2)
    gw["final_norm"] = summed[-2][0]
    loss = summed[-1][0, 0]

    deltas, new_m, new_v = {}, {}, {}
    for n in SHARDED:
        shape = weights[n].shape
        d, nm, nv = _adamw(_as_2d(weights[n]), _as_2d(gw[n]), _as_2d(moments_m[n]), _as_2d(moments_v[n]),
                           "adamw_" + n)
        deltas[n], new_m[n], new_v[n] = d.reshape(shape), nm.reshape(shape), nv.reshape(shape)
    rest_names = [n for n in WEIGHT_ORDER if n not in SHARDED]
    ds, nms, nvs = _adamw_many(*[[_as_2d(src[n]) for n in rest_names] for src in (weights, gw, moments_m, moments_v)],
                               "adamw_small")
    for n, d, nm, nv in zip(rest_names, ds, nms, nvs):
        shape = weights[n].shape
        deltas[n], new_m[n], new_v[n] = d.reshape(shape), nm.reshape(shape), nv.reshape(shape)

    return (loss, grad_x, *[gw[n] for n in WEIGHT_ORDER], *[deltas[n] for n in WEIGHT_ORDER],
            *[new_m[n] for n in WEIGHT_ORDER], *[new_v[n] for n in WEIGHT_ORDER])
```

```python
import functools

import numpy as np
import jax
import jax.numpy as jnp
from jax import lax
from jax.experimental import pallas as pl
from jax.experimental.pallas import tpu as pltpu

F32 = jnp.float32
BF16 = jnp.bfloat16

N_DEV = 8
D_MODEL = 1024
DEPTH = 2
N_HEADS = 8
HEAD_DIM = 64
BRANCH_W = 512
POOL_WINDOWS = (2, 4, 8, 16)
POOL_GD = 128
CONV_K = 3
FFN_HIDDEN = 2816
GATE_W = 3 * D_MODEL
IN_COLS = 6664
MAIN_COLS = GATE_W + 7 * BRANCH_W
RMS_EPS = 1e-6
NEG_INF = -1e30

ADAM_LR = 0.001
ADAM_B1 = 0.9
ADAM_B2 = 0.999
ADAM_EPS = 1e-08
ADAM_WD = 0.01
ADAM_STEP = 10

LANES = 128
VMEM_LIMIT = 56 * 1024 * 1024
ATT_BLK = 256
CUM_BLK = 256

TRIPLE = 3 * LANES
OFF_G, OFF_QKV, OFF_CONV, OFF_U = 0, 3072, 4608, 6144


def _cp(sem=None):
    return pltpu.CompilerParams(dimension_semantics=sem, vmem_limit_bytes=VMEM_LIMIT)


def _sigmoid(z):
    return 1.0 / (1.0 + jnp.exp(-z))


def _matmul(a, b, *, mode, out_dtype, name, tm=2048, tn=512, tk=None, residual=None, rms_g=None):
    if mode == "nn":
        (M, K), N = a.shape, b.shape[1]
    elif mode == "nt":
        (M, K), N = a.shape, b.shape[0]
    else:
        (K, M), N = a.shape, b.shape[1]
    tm, tn, tk = min(tm, M), min(tn, N), K if tk is None else min(tk, K)
    assert M % tm == 0 and N % tn == 0 and K % tk == 0, (name, M, N, K, tm, tn, tk)
    nk = K // tk
    if mode == "nn":
        a_spec = pl.BlockSpec((tm, tk), lambda i, j, k: (i, k))
        b_spec = pl.BlockSpec((tk, tn), lambda i, j, k: (k, j))
        dims = (((1,), (0,)), ((), ()))
    elif mode == "nt":
        a_spec = pl.BlockSpec((tm, tk), lambda i, j, k: (i, k))
        b_spec = pl.BlockSpec((tn, tk), lambda i, j, k: (j, k))
        dims = (((1,), (1,)), ((), ()))
    else:
        a_spec = pl.BlockSpec((tk, tm), lambda i, j, k: (k, i))
        b_spec = pl.BlockSpec((tk, tn), lambda i, j, k: (k, j))
        dims = (((0,), (0,)), ((), ()))
    o_spec = pl.BlockSpec((tm, tn), lambda i, j, k: (i, j))
    has_res, has_norm = residual is not None, rms_g is not None
    assert not has_norm or tn == N, (name, tn, N)

    def body(*refs):
        a_ref, b_ref = refs[:2]
        r_ref = refs[2] if has_res else None
        g_ref = refs[2 + has_res] if has_norm else None
        o_ref = refs[2 + has_res + has_norm]
        h_ref = refs[3 + has_res + has_norm] if has_norm else None

        def finish(acc):
            if has_res:
                acc = acc + r_ref[...].astype(F32)
            o_ref[...] = acc.astype(out_dtype)
            if has_norm:
                r = lax.rsqrt(jnp.mean(acc * acc, axis=-1, keepdims=True) + RMS_EPS)
                h_ref[...] = ((acc * r) * g_ref[...]).astype(BF16)

        prod = lax.dot_general(a_ref[...], b_ref[...], dims, preferred_element_type=F32)
        if nk == 1:
            finish(prod)
            return
        acc_ref = refs[-1]
        k = pl.program_id(2)

        @pl.when(k == 0)
        def _():
            acc_ref[...] = prod

        @pl.when(jnp.logical_and(k > 0, k < nk - 1))
        def _():
            acc_ref[...] += prod

        @pl.when(k == nk - 1)
        def _():
            finish(acc_ref[...] + prod)

    in_specs = [a_spec, b_spec] + ([o_spec] if has_res else [])
    args = (a, b) + ((residual,) if has_res else ())
    out_specs, out_shape = o_spec, jax.ShapeDtypeStruct((M, N), out_dtype)
    if has_norm:
        in_specs.append(pl.BlockSpec((1, N), lambda i, j, k: (0, 0)))
        args += (rms_g.reshape(1, N),)
        out_specs, out_shape = [o_spec, o_spec], [out_shape, jax.ShapeDtypeStruct((M, N), BF16)]
    return pl.pallas_call(
        body, name=name, grid=(M // tm, N // tn, nk), in_specs=in_specs, out_specs=out_specs,
        out_shape=out_shape,
        scratch_shapes=[pltpu.VMEM((tm, tn), F32)] if nk > 1 else [],
        compiler_params=_cp(("parallel", "parallel", "arbitrary")),
    )(*args)


def _rms_fwd(x, g, name):
    T, Dm = x.shape
    tm = min(512, T)

    def body(x_ref, g_ref, h_ref):
        xf = x_ref[...]
        r = lax.rsqrt(jnp.mean(xf * xf, axis=-1, keepdims=True) + RMS_EPS)
        h_ref[...] = ((xf * r) * g_ref[...]).astype(BF16)

    return pl.pallas_call(
        body, name=name, grid=(T // tm,),
        in_specs=[pl.BlockSpec((tm, Dm), lambda i: (i, 0)), pl.BlockSpec((1, Dm), lambda i: (0, 0))],
        out_specs=pl.BlockSpec((tm, Dm), lambda i: (i, 0)),
        out_shape=jax.ShapeDtypeStruct((T, Dm), BF16),
        compiler_params=_cp(("parallel",)),
    )(x, g.reshape(1, Dm))


def _rms_bwd(x, g, dh, dres, name):
    T, Dm = x.shape
    tm = min(512, T)

    def body(x_ref, g_ref, dh_ref, dres_ref, dx_ref, dxb_ref, dg_ref):
        i = pl.program_id(0)
        xf = x_ref[...]
        r = lax.rsqrt(jnp.mean(xf * xf, axis=-1, keepdims=True) + RMS_EPS)
        xn = xf * r
        dhf = dh_ref[...].astype(F32)
        dxn = dhf * g_ref[...]
        c = jnp.mean(dxn * xn, axis=-1, keepdims=True)
        dx = dres_ref[...] + r * (dxn - xn * c)
        dx_ref[...] = dx
        dxb_ref[...] = dx.astype(BF16)
        part = jnp.sum(dhf * xn, axis=0, keepdims=True)

        @pl.when(i == 0)
        def _():
            dg_ref[...] = part

        @pl.when(i > 0)
        def _():
            dg_ref[...] += part

    row = pl.BlockSpec((tm, Dm), lambda i: (i, 0))
    vec = pl.BlockSpec((1, Dm), lambda i: (0, 0))
    return pl.pallas_call(
        body, name=name, grid=(T // tm,), in_specs=[row, vec, row, row], out_specs=[row, row, vec],
        out_shape=[jax.ShapeDtypeStruct((T, Dm), F32), jax.ShapeDtypeStruct((T, Dm), BF16),
                   jax.ShapeDtypeStruct((1, Dm), F32)],
        compiler_params=_cp(("arbitrary",)),
    )(x, g.reshape(1, Dm), dh, dres)


def _loss_head(x, g, target, name):
    T, Dm = x.shape
    tm = min(512, T)

    def body(x_ref, g_ref, t_ref, loss_ref, dx_ref, dxb_ref, dg_ref):
        i = pl.program_id(0)
        xf = x_ref[...]
        gv = g_ref[...]
        r = lax.rsqrt(jnp.mean(xf * xf, axis=-1, keepdims=True) + RMS_EPS)
        xn = xf * r
        diff = xn * gv - t_ref[...]
        per_tok = jnp.mean(diff * diff, axis=-1, keepdims=True)
        lpart = 0.5 * jnp.sum(per_tok, axis=0, keepdims=True) + jnp.zeros((1, LANES), F32)
        dy = diff * (1.0 / Dm)
        dxn = dy * gv
        c = jnp.mean(dxn * xn, axis=-1, keepdims=True)
        dx = r * (dxn - xn * c)
        dx_ref[...] = dx
        dxb_ref[...] = dx.astype(BF16)
        part = jnp.sum(dy * xn, axis=0, keepdims=True)

        @pl.when(i == 0)
        def _():
            dg_ref[...] = part
            loss_ref[...] = lpart

        @pl.when(i > 0)
        def _():
            dg_ref[...] += part
            loss_ref[...] += lpart

    row = pl.BlockSpec((tm, Dm), lambda i: (i, 0))
    vec = pl.BlockSpec((1, Dm), lambda i: (0, 0))
    lsp = pl.BlockSpec((1, LANES), lambda i: (0, 0))
    return pl.pallas_call(
        body, name=name, grid=(T // tm,), in_specs=[row, vec, row], out_specs=[lsp, row, row, vec],
        out_shape=[jax.ShapeDtypeStruct((1, LANES), F32), jax.ShapeDtypeStruct((T, Dm), F32),
                   jax.ShapeDtypeStruct((T, Dm), BF16), jax.ShapeDtypeStruct((1, Dm), F32)],
        compiler_params=_cp(("arbitrary",)),
    )(x, g.reshape(1, Dm), target)


def _split_bf16(v):
    hi = v.astype(BF16)
    r1 = v - hi.astype(F32)
    mid = r1.astype(BF16)
    lo = (r1 - mid.astype(F32)).astype(BF16)
    return hi, mid, lo


def _tri_dot(tri, v):
    hi, mid, lo = _split_bf16(v)
    dot = functools.partial(jnp.dot, preferred_element_type=F32)
    return dot(tri, hi) + dot(tri, mid) + dot(tri, lo)


def _log_sigmoid(z):
    return jnp.minimum(z, 0.0) - jnp.log(1.0 + jnp.exp(-jnp.abs(z)))


def _fox_cumsum_fwd(f, bf, n_seq, name):
    T = f.shape[0]
    S = T // n_seq
    c = min(CUM_BLK, S)

    def body(f_ref, b_ref, out_ref):
        ri = lax.broadcasted_iota(jnp.int32, (c, c), 0)
        ci = lax.broadcasted_iota(jnp.int32, (c, c), 1)
        tri = (ri >= ci).astype(BF16)
        carry = jnp.zeros((1, LANES), F32)
        for j in range(S // c):
            lf = _log_sigmoid(f_ref[j * c:(j + 1) * c, :] + b_ref[...])
            out_ref[j * c:(j + 1) * c, :] = _tri_dot(tri, lf) + carry
            carry = carry + jnp.sum(lf, axis=0, keepdims=True)

    blk = pl.BlockSpec((S, LANES), lambda b: (b, 0))
    return pl.pallas_call(
        body, name=name, grid=(n_seq,), in_specs=[blk, pl.BlockSpec((1, LANES), lambda b: (0, 0))],
        out_specs=blk, out_shape=jax.ShapeDtypeStruct((T, LANES), F32),
        compiler_params=_cp(("parallel",)),
    )(f, bf)


def _fox_cumsum_bwd(f, bf, dF, n_seq, name):
    T = f.shape[0]
    S = T // n_seq
    c = min(CUM_BLK, S)

    def body(f_ref, b_ref, dF_ref, df_ref, db_ref):
        b = pl.program_id(0)
        ri = lax.broadcasted_iota(jnp.int32, (c, c), 0)
        ci = lax.broadcasted_iota(jnp.int32, (c, c), 1)
        tri = (ri <= ci).astype(BF16)
        carry = jnp.zeros((1, LANES), F32)
        dbp = jnp.zeros((1, LANES), F32)
        for j in reversed(range(S // c)):
            dFc = dF_ref[j * c:(j + 1) * c, :]
            dlf = _tri_dot(tri, dFc) + carry
            carry = carry + jnp.sum(dFc, axis=0, keepdims=True)
            z = f_ref[j * c:(j + 1) * c, :] + b_ref[...]
            dz = dlf * _sigmoid(-z)
            df_ref[j * c:(j + 1) * c, :] = dz.astype(BF16)
            dbp = dbp + jnp.sum(dz, axis=0, keepdims=True)

        @pl.when(b == 0)
        def _():
            db_ref[...] = dbp

        @pl.when(b > 0)
        def _():
            db_ref[...] += dbp

    blk = pl.BlockSpec((S, LANES), lambda b: (b, 0))
    vec = pl.BlockSpec((1, LANES), lambda b: (0, 0))
    return pl.pallas_call(
        body, name=name, grid=(n_seq,), in_specs=[blk, vec, blk], out_specs=[blk, vec],
        out_shape=[jax.ShapeDtypeStruct((T, LANES), BF16), jax.ShapeDtypeStruct((1, LANES), F32)],
        compiler_params=_cp(("arbitrary",)),
    )(f, bf, dF)


def _pair_masks():
    lane = lax.broadcasted_iota(jnp.int32, (1, LANES), 1)
    lo = lane < HEAD_DIM
    return lo, jnp.logical_not(lo)


def _attn_logits(q, k, fq, fk, sel, mask, scale):
    qm = jnp.where(sel, q, jnp.zeros_like(q))
    s = lax.dot_general(qm, k, (((1,), (1,)), ((), ())), preferred_element_type=F32) * scale
    s = s + fq - fk
    return jnp.where(mask, s, NEG_INF)


def _causal_mask(qi, ki, blk):
    row = qi * blk + lax.broadcasted_iota(jnp.int32, (blk, blk), 0)
    col = ki * blk + lax.broadcasted_iota(jnp.int32, (blk, blk), 1)
    return col <= row


def _attn_fwd(proj, Fq, Fk, n_seq, name):
    T = proj.shape[0]
    S = T // n_seq
    blk = min(ATT_BLK, S)
    nb = S // blk
    scale = HEAD_DIM ** -0.5
    qc, kc, vc = OFF_Q // LANES, OFF_K // LANES, OFF_V // LANES

    def body(q_ref, k_ref, v_ref, fq_ref, fk_ref, o_ref, o32_ref, lse_ref, m_s, l_s, acc_s):
        qi, ki = pl.program_id(2), pl.program_id(3)

        @pl.when(ki == 0)
        def _():
            m_s[...] = jnp.full_like(m_s, NEG_INF)
            l_s[...] = jnp.zeros_like(l_s)
            acc_s[...] = jnp.zeros_like(acc_s)

        @pl.when(ki <= qi)
        def _():
            q, k, v = q_ref[...], k_ref[...], v_ref[...]
            mask = _causal_mask(qi, ki, blk)
            for hh, sel in enumerate(_pair_masks()):
                s = _attn_logits(q, k, fq_ref[hh], fk_ref[hh], sel, mask, scale)
                m_prev = m_s[hh]
                m_new = jnp.maximum(m_prev, jnp.max(s, axis=-1, keepdims=True))
                alpha = jnp.exp(m_prev - m_new)
                p = jnp.exp(s - m_new)
                l_s[hh] = alpha * l_s[hh] + jnp.sum(p, axis=-1, keepdims=True)
                p_hi = p.astype(BF16)
                p_lo = (p - p_hi.astype(F32)).astype(BF16)
                pv = jnp.dot(p_hi, v, preferred_element_type=F32) + jnp.dot(p_lo, v, preferred_element_type=F32)
                acc_s[hh] = alpha * acc_s[hh] + pv
                m_s[hh] = m_new

        @pl.when(ki == qi)
        def _():
            lo, _ = _pair_masks()
            o = jnp.where(lo, acc_s[0] / l_s[0], acc_s[1] / l_s[1])
            o_ref[...] = o.astype(BF16)
            o32_ref[...] = o
            lse_ref[0] = m_s[0] + jnp.log(l_s[0])
            lse_ref[1] = m_s[1] + jnp.log(l_s[1])

    grid = (n_seq, N_HEADS // 2, nb, nb)
    return pl.pallas_call(
        body, name=name, grid=grid,
        in_specs=[
            pl.BlockSpec((blk, LANES), lambda b, j, qi, ki: (b * nb + qi, qc + j)),
            pl.BlockSpec((blk, LANES), lambda b, j, qi, ki: (b * nb + jnp.minimum(ki, qi), kc + j)),
            pl.BlockSpec((blk, LANES), lambda b, j, qi, ki: (b * nb + jnp.minimum(ki, qi), vc + j)),
            pl.BlockSpec((2, blk, 1), lambda b, j, qi, ki: (j, b * nb + qi, 0)),
            pl.BlockSpec((2, 1, blk), lambda b, j, qi, ki: (j, 0, b * nb + jnp.minimum(ki, qi))),
        ],
        out_specs=[
            pl.BlockSpec((blk, LANES), lambda b, j, qi, ki: (b * nb + qi, j)),
            pl.BlockSpec((blk, LANES), lambda b, j, qi, ki: (b * nb + qi, j)),
            pl.BlockSpec((2, blk, 1), lambda b, j, qi, ki: (j, b * nb + qi, 0)),
        ],
        out_shape=[jax.ShapeDtypeStruct((T, BRANCH_W), BF16), jax.ShapeDtypeStruct((T, BRANCH_W), F32),
                   jax.ShapeDtypeStruct((N_HEADS, T, 1), F32)],
        scratch_shapes=[pltpu.VMEM((2, blk, 1), F32), pltpu.VMEM((2, blk, 1), F32),
                        pltpu.VMEM((2, blk, LANES), F32)],
        compiler_params=_cp(("parallel", "parallel", "parallel", "arbitrary")),
    )(proj, proj, proj, Fq, Fk)


def _attn_delta(do, o, name):
    T = do.shape[0]
    tm = min(512, T)

    def body(do_ref, o_ref, d_ref):
        prod = do_ref[...].astype(F32) * o_ref[...].astype(F32)
        lo, hi = _pair_masks()
        for j in range(N_HEADS // 2):
            pj = prod[:, j * LANES:(j + 1) * LANES]
            d_ref[2 * j] = jnp.sum(jnp.where(lo, pj, 0.0), axis=-1, keepdims=True)
            d_ref[2 * j + 1] = jnp.sum(jnp.where(hi, pj, 0.0), axis=-1, keepdims=True)

    row = pl.BlockSpec((tm, BRANCH_W), lambda i: (i, 0))
    return pl.pallas_call(
        body, name=name, grid=(T // tm,), in_specs=[row, row],
        out_specs=pl.BlockSpec((N_HEADS, tm, 1), lambda i: (0, i, 0)),
        out_shape=jax.ShapeDtypeStruct((N_HEADS, T, 1), F32),
        compiler_params=_cp(("parallel",)),
    )(do, o)


def _attn_bwd_dq(proj, do, lse, delta, Fq, Fk, n_seq, name):
    T = proj.shape[0]
    S = T // n_seq
    blk = min(ATT_BLK, S)
    nb = S // blk
    scale = HEAD_DIM ** -0.5
    qc, kc, vc = OFF_Q // LANES, OFF_K // LANES, OFF_V // LANES

    def body(q_ref, k_ref, v_ref, do_ref, lse_ref, dl_ref, fq_ref, fk_ref, dq_ref, acc_s):
        qi, ki = pl.program_id(2), pl.program_id(3)

        @pl.when(ki == 0)
        def _():
            acc_s[...] = jnp.zeros_like(acc_s)

        @pl.when(ki <= qi)
        def _():
            q, k, v, do_ = q_ref[...], k_ref[...], v_ref[...], do_ref[...]
            mask = _causal_mask(qi, ki, blk)
            for hh, sel in enumerate(_pair_masks()):
                s = _attn_logits(q, k, fq_ref[hh], fk_ref[hh], sel, mask, scale)
                p = jnp.exp(s - lse_ref[hh])
                dom = jnp.where(sel, do_, jnp.zeros_like(do_))
                dp = lax.dot_general(dom, v, (((1,), (1,)), ((), ())), preferred_element_type=F32)
                ds = p * (dp - dl_ref[hh])
                acc_s[hh] += jnp.dot(ds.astype(BF16), k, preferred_element_type=F32)

        @pl.when(ki == qi)
        def _():
            lo, _ = _pair_masks()
            dq_ref[...] = (jnp.where(lo, acc_s[0], acc_s[1]) * scale).astype(BF16)

    qmap = lambda b, j, qi, ki: (b * nb + qi, j)
    col1 = pl.BlockSpec((2, blk, 1), lambda b, j, qi, ki: (j, b * nb + qi, 0))
    return pl.pallas_call(
        body, name=name, grid=(n_seq, N_HEADS // 2, nb, nb),
        in_specs=[
            pl.BlockSpec((blk, LANES), lambda b, j, qi, ki: (b * nb + qi, qc + j)),
            pl.BlockSpec((blk, LANES), lambda b, j, qi, ki: (b * nb + jnp.minimum(ki, qi), kc + j)),
            pl.BlockSpec((blk, LANES), lambda b, j, qi, ki: (b * nb + jnp.minimum(ki, qi), vc + j)),
            pl.BlockSpec((blk, LANES), qmap),
            col1, col1, col1,
            pl.BlockSpec((2, 1, blk), lambda b, j, qi, ki: (j, 0, b * nb + jnp.minimum(ki, qi))),
        ],
        out_specs=pl.BlockSpec((blk, LANES), qmap),
        out_shape=jax.ShapeDtypeStruct((T, BRANCH_W), BF16),
        scratch_shapes=[pltpu.VMEM((2, blk, LANES), F32)],
        compiler_params=_cp(("parallel", "parallel", "parallel", "arbitrary")),
    )(proj, proj, proj, do, lse, delta, Fq, Fk)


def _attn_bwd_dkv(proj, do, lse, delta, Fq, Fk, n_seq, name):
    T = proj.shape[0]
    S = T // n_seq
    blk = min(ATT_BLK, S)
    nb = S // blk
    scale = HEAD_DIM ** -0.5
    qc, kc, vc = OFF_Q // LANES, OFF_K // LANES, OFF_V // LANES
    tdot = functools.partial(lax.dot_general, dimension_numbers=(((0,), (0,)), ((), ())),
                             preferred_element_type=F32)

    def body(q_ref, k_ref, v_ref, do_ref, lse_ref, dl_ref, fq_ref, fk_ref, dk_ref, dv_ref, dfk_ref,
             dk_s, dv_s, df_s):
        ki, qi = pl.program_id(2), pl.program_id(3)

        @pl.when(qi == 0)
        def _():
            dk_s[...] = jnp.zeros_like(dk_s)
            dv_s[...] = jnp.zeros_like(dv_s)
            df_s[...] = jnp.zeros_like(df_s)

        @pl.when(qi >= ki)
        def _():
            q, k, v, do_ = q_ref[...], k_ref[...], v_ref[...], do_ref[...]
            mask = _causal_mask(qi, ki, blk)
            for hh, sel in enumerate(_pair_masks()):
                s = _attn_logits(q, k, fq_ref[hh], fk_ref[hh], sel, mask, scale)
                p = jnp.exp(s - lse_ref[hh])
                dv_s[hh] += tdot(p.astype(BF16), do_)
                dom = jnp.where(sel, do_, jnp.zeros_like(do_))
                dp = lax.dot_general(dom, v, (((1,), (1,)), ((), ())), preferred_element_type=F32)
                ds = p * (dp - dl_ref[hh])
                dk_s[hh] += tdot(ds.astype(BF16), q)
                df_s[hh] -= jnp.sum(ds, axis=0, keepdims=True)

        @pl.when(qi == nb - 1)
        def _():
            lo, _ = _pair_masks()
            dk_ref[...] = (jnp.where(lo, dk_s[0], dk_s[1]) * scale).astype(BF16)
            dv_ref[...] = jnp.where(lo, dv_s[0], dv_s[1]).astype(BF16)
            dfk_ref[...] = df_s[...]

    kmap = lambda b, j, ki, qi: (b * nb + ki, j)
    col1 = pl.BlockSpec((2, blk, 1), lambda b, j, ki, qi: (j, b * nb + jnp.maximum(qi, ki), 0))
    rowk = pl.BlockSpec((2, 1, blk), lambda b, j, ki, qi: (j, 0, b * nb + ki))
    return pl.pallas_call(
        body, name=name, grid=(n_seq, N_HEADS // 2, nb, nb),
        in_specs=[
            pl.BlockSpec((blk, LANES), lambda b, j, ki, qi: (b * nb + jnp.maximum(qi, ki), qc + j)),
            pl.BlockSpec((blk, LANES), lambda b, j, ki, qi: (b * nb + ki, kc + j)),
            pl.BlockSpec((blk, LANES), lambda b, j, ki, qi: (b * nb + ki, vc + j)),
            pl.BlockSpec((blk, LANES), lambda b, j, ki, qi: (b * nb + jnp.maximum(qi, ki), j)),
            col1, col1, col1, rowk,
        ],
        out_specs=[pl.BlockSpec((blk, LANES), kmap), pl.BlockSpec((blk, LANES), kmap), rowk],
        out_shape=[jax.ShapeDtypeStruct((T, BRANCH_W), BF16), jax.ShapeDtypeStruct((T, BRANCH_W), BF16),
                   jax.ShapeDtypeStruct((N_HEADS, 1, T), F32)],
        scratch_shapes=[pltpu.VMEM((2, blk, LANES), F32), pltpu.VMEM((2, blk, LANES), F32),
                        pltpu.VMEM((2, 1, blk), F32)],
        compiler_params=_cp(("parallel", "parallel", "parallel", "arbitrary")),
    )(proj, proj, proj, do, lse, delta, Fq, Fk)


AUG0 = HEAD_DIM
Q_TILE, K_CHUNK, ROW_GROUP = 512, 256, 64


def _fox_prep(f, bf, proj, n_seq, name):
    T = f.shape[0]
    S = T // n_seq
    c = min(CUM_BLK, S)

    def body(f_ref, b_ref, qkv_ref, qa_ref, ka_ref, va_ref):
        ri = lax.broadcasted_iota(jnp.int32, (c, c), 0)
        ci = lax.broadcasted_iota(jnp.int32, (c, c), 1)
        tri = (ri >= ci).astype(BF16)
        lane = lax.broadcasted_iota(jnp.int32, (c, LANES), 1)
        carry = jnp.zeros((1, LANES), F32)
        for j in range(S // c):
            rows = slice(j * c, (j + 1) * c)
            lf = _log_sigmoid(f_ref[rows, :] + b_ref[...])
            Fc = _tri_dot(tri, lf) + carry
            carry = carry + jnp.sum(lf, axis=0, keepdims=True)
            for h in range(N_HEADS):
                col = jnp.sum(jnp.where(lane == h, Fc, 0.0), axis=-1, keepdims=True)
                hi = col.astype(BF16).astype(F32)
                r1 = col - hi
                mid = r1.astype(BF16).astype(F32)
                lo = r1 - mid
                ones_q = jnp.logical_and(lane >= AUG0 + 3, lane < AUG0 + 6)
                ones_k = jnp.logical_and(lane >= AUG0, lane < AUG0 + 3)
                aug_q = jnp.where(lane == AUG0, hi, jnp.where(lane == AUG0 + 1, mid, jnp.where(
                    lane == AUG0 + 2, lo, jnp.where(ones_q, 1.0, 0.0))))
                aug_k = jnp.where(lane == AUG0 + 3, -hi, jnp.where(lane == AUG0 + 4, -mid, jnp.where(
                    lane == AUG0 + 5, -lo, jnp.where(ones_k, 1.0, 0.0))))
                base = (h // 2) * TRIPLE
                qp, kp, vp = (qkv_ref[rows, base + t * LANES:base + (t + 1) * LANES].astype(F32) for t in range(3))
                if h % 2:
                    qp, kp, vp = (pltpu.roll(a, HEAD_DIM, 1) for a in (qp, kp, vp))
                out = slice(h * LANES, (h + 1) * LANES)
                qa_ref[rows, out] = jnp.where(lane < HEAD_DIM, qp * (HEAD_DIM ** -0.5), aug_q).astype(BF16)
                ka_ref[rows, out] = jnp.where(lane < HEAD_DIM, kp, aug_k).astype(BF16)
                va_ref[rows, out] = jnp.where(lane < HEAD_DIM, vp, jnp.where(lane == AUG0, 1.0, 0.0)).astype(BF16)

    fblk = pl.BlockSpec((S, LANES), lambda b: (b, 0))
    out = pl.BlockSpec((S, N_HEADS * LANES), lambda b: (b, 0))
    sh = jax.ShapeDtypeStruct((T, N_HEADS * LANES), BF16)
    return pl.pallas_call(
        body, name=name, grid=(n_seq,),
        in_specs=[fblk, pl.BlockSpec((1, LANES), lambda b: (0, 0)),
                  pl.BlockSpec((S, 4 * TRIPLE), lambda b: (b, OFF_QKV // (4 * TRIPLE)))],
        out_specs=[out, out, out], out_shape=[sh, sh, sh],
        compiler_params=_cp(("parallel",)),
    )(f, bf, proj)


def _band_mask(q0, k0, nq, nk):
    row = q0 + lax.broadcasted_iota(jnp.int32, (nq, nk), 0)
    col = k0 + lax.broadcasted_iota(jnp.int32, (nq, nk), 1)
    return col <= row


_NT = (((1,), (1,)), ((), ()))
_TN = (((0,), (0,)), ((), ()))


def _attn_fwd2(qa, ka, va, n_seq, name):
    T = qa.shape[0]
    S = T // n_seq
    tq, tk, rg = min(Q_TILE, S), min(K_CHUNK, S), ROW_GROUP
    nq, per = S // tq, tq // tk

    def body(q_ref, k_ref, v_ref, o_ref, o32_ref, lse_ref, phi_s, plo_s, mp_s, m_s, acc_s):
        qi = pl.program_id(2)
        mp_s[...] = jnp.full_like(mp_s, NEG_INF)
        acc_s[...] = jnp.zeros_like(acc_s)

        def scores(kc, hh, r0):
            k0 = pl.multiple_of(kc * tk, tk)
            hl = slice(hh * LANES, (hh + 1) * LANES)
            return k0, lax.dot_general(q_ref[r0:, hl], k_ref[pl.ds(k0, tk), hl], _NT, preferred_element_type=F32)

        def max_chunk(kc, masked, r0):
            for hh in range(2):
                k0, s_all = scores(kc, hh, r0)
                for r in range(r0 // rg, tq // rg):
                    rows = slice(r * rg, (r + 1) * rg)
                    s = s_all[r * rg - r0:(r + 1) * rg - r0, :]
                    if masked:
                        s = jnp.where(_band_mask(qi * tq + r * rg, k0, rg, tk), s, NEG_INF)
                    part = s[:, :LANES]
                    for c in range(1, tk // LANES):
                        part = jnp.maximum(part, s[:, c * LANES:(c + 1) * LANES])
                    mp_s[hh, rows, :] = jnp.maximum(mp_s[hh, rows, :], part)

        def sum_chunk(kc, masked, r0):
            for hh in range(2):
                k0, s_all = scores(kc, hh, r0)
                hl = slice(hh * LANES, (hh + 1) * LANES)
                v = v_ref[pl.ds(k0, tk), hl]
                for r in range(r0 // rg, tq // rg):
                    rows = slice(r * rg, (r + 1) * rg)
                    p = jnp.exp(s_all[r * rg - r0:(r + 1) * rg - r0, :] - m_s[hh, rows])
                    if masked:
                        p = jnp.where(_band_mask(qi * tq + r * rg, k0, rg, tk), p, 0.0)
                    p_hi = p.astype(BF16)
                    phi_s[hh, rows, :] = p_hi
                    plo_s[hh, rows, :] = (p - p_hi.astype(F32)).astype(BF16)
                acc_s[hh, r0:, :] += (jnp.dot(phi_s[hh, r0:, :], v, preferred_element_type=F32)
                                      + jnp.dot(plo_s[hh, r0:, :], v, preferred_element_type=F32))

        def sweep(chunk):
            def unmasked(kc, carry):
                chunk(kc, False, 0)
                return carry

            lax.fori_loop(0, qi * per, unmasked, 0)
            for d in range(per):
                chunk(qi * per + d, True, d * tk)

        sweep(max_chunk)
        m_s[...] = jnp.max(mp_s[...], axis=-1, keepdims=True)
        sweep(sum_chunk)

        lane = lax.broadcasted_iota(jnp.int32, (1, LANES), 1)
        outs = []
        for hh in range(2):
            acc = acc_s[hh]
            l = jnp.sum(jnp.where(lane == AUG0, acc, 0.0), axis=-1, keepdims=True)
            lse_ref[hh] = m_s[hh] + jnp.log(l)
            outs.append(acc / l)
        o = jnp.where(lane < HEAD_DIM, outs[0], pltpu.roll(outs[1], HEAD_DIM, 1))
        o_ref[...] = o.astype(BF16)
        o32_ref[...] = o

    qmap = lambda b, j, qi: (b * nq + qi, j)
    omap = lambda b, j, qi: (b * nq + qi, j)
    kv = pl.BlockSpec((S, 2 * LANES), lambda b, j, qi: (b, j))
    return pl.pallas_call(
        body, name=name, grid=(n_seq, N_HEADS // 2, nq),
        in_specs=[pl.BlockSpec((tq, 2 * LANES), qmap), kv, kv],
        out_specs=[pl.BlockSpec((tq, LANES), omap), pl.BlockSpec((tq, LANES), omap),
                   pl.BlockSpec((2, tq, 1), lambda b, j, qi: (j, b * nq + qi, 0))],
        out_shape=[jax.ShapeDtypeStruct((T, BRANCH_W), BF16), jax.ShapeDtypeStruct((T, BRANCH_W), F32),
                   jax.ShapeDtypeStruct((N_HEADS, T, 1), F32)],
        scratch_shapes=[pltpu.VMEM((2, tq, tk), BF16), pltpu.VMEM((2, tq, tk), BF16),
                        pltpu.VMEM((2, tq, LANES), F32), pltpu.VMEM((2, tq, 1), F32),
                        pltpu.VMEM((2, tq, LANES), F32)],
        compiler_params=_cp(("parallel", "parallel", "parallel")),
    )(qa, ka, va)


def _attn_bwd_dq2(qa, ka, proj, do, lse, delta, n_seq, name):
    T = qa.shape[0]
    S = T // n_seq
    tq, tk, rg = min(Q_TILE, S), min(K_CHUNK, S), ROW_GROUP
    nq, per = S // tq, tq // tk
    vc = OFF_V // LANES

    def body(q_ref, k_ref, v_ref, do_ref, lse_ref, dl_ref, dq_ref, ds_s, acc_s):
        qi = pl.program_id(2)
        acc_s[...] = jnp.zeros_like(acc_s)
        sels = _pair_masks()

        def chunk(kc, masked):
            k0 = pl.multiple_of(kc * tk, tk)
            v = v_ref[pl.ds(k0, tk), :]
            for hh in range(2):
                hl = slice(hh * LANES, (hh + 1) * LANES)
                kh = k_ref[pl.ds(k0, tk), hl]
                s_all = lax.dot_general(q_ref[:, hl], kh, _NT, preferred_element_type=F32)
                dom = jnp.where(sels[hh], do_ref[...], jnp.zeros_like(do_ref[...]))
                dp_all = lax.dot_general(dom, v, _NT, preferred_element_type=F32)
                for r in range(tq // rg):
                    rows = slice(r * rg, (r + 1) * rg)
                    p = jnp.exp(s_all[rows, :] - lse_ref[hh, rows])
                    if masked:
                        p = jnp.where(_band_mask(qi * tq + r * rg, k0, rg, tk), p, 0.0)
                    ds_s[hh, rows, :] = (p * (dp_all[rows, :] - dl_ref[hh, rows])).astype(BF16)
                acc_s[hh] += jnp.dot(ds_s[hh], kh, preferred_element_type=F32)

        def unmasked(kc, carry):
            chunk(kc, False)
            return carry

        lax.fori_loop(0, qi * per, unmasked, 0)
        for d in range(per):
            chunk(qi * per + d, True)
        dq = jnp.where(sels[0], acc_s[0], pltpu.roll(acc_s[1], HEAD_DIM, 1))
        dq_ref[...] = (dq * (HEAD_DIM ** -0.5)).astype(BF16)

    qmap = lambda b, j, qi: (b * nq + qi, j)
    col1 = pl.BlockSpec((2, tq, 1), lambda b, j, qi: (j, b * nq + qi, 0))
    return pl.pallas_call(
        body, name=name, grid=(n_seq, N_HEADS // 2, nq),
        in_specs=[pl.BlockSpec((tq, 2 * LANES), qmap),
                  pl.BlockSpec((S, 2 * LANES), lambda b, j, qi: (b, j)),
                  pl.BlockSpec((S, LANES), lambda b, j, qi: (b, vc + j)),
                  pl.BlockSpec((tq, LANES), qmap), col1, col1],
        out_specs=pl.BlockSpec((tq, LANES), qmap),
        out_shape=jax.ShapeDtypeStruct((T, BRANCH_W), BF16),
        scratch_shapes=[pltpu.VMEM((2, tq, tk), BF16), pltpu.VMEM((2, tq, LANES), F32)],
        compiler_params=_cp(("parallel", "parallel", "parallel")),
    )(qa, ka, proj, do, lse, delta)


def _attn_bwd_dkv2(qa, ka, proj, do, lse, delta, n_seq, name):
    T = qa.shape[0]
    S = T // n_seq
    tkt, tqc, rg = min(Q_TILE, S), min(K_CHUNK, S), ROW_GROUP // 2
    nk, per, nqc = S // tkt, tkt // tqc, S // tqc
    vc = OFF_V // LANES

    def body(q_ref, k_ref, v_ref, do_ref, lse_ref, dl_ref, dk_ref, dv_ref, dfk_ref,
             p_s, ds_s, dk_s, dv_s, df_s):
        ki = pl.program_id(2)
        dk_s[...] = jnp.zeros_like(dk_s)
        dv_s[...] = jnp.zeros_like(dv_s)
        df_s[...] = jnp.zeros_like(df_s)
        sels = _pair_masks()
        v = v_ref[...]

        def chunk(qc, masked):
            q0 = pl.multiple_of(qc * tqc, tqc)
            do_ = do_ref[pl.ds(q0, tqc), :]
            for hh in range(2):
                hl = slice(hh * LANES, (hh + 1) * LANES)
                qh = q_ref[pl.ds(q0, tqc), hl]
                s_all = lax.dot_general(qh, k_ref[:, hl], _NT, preferred_element_type=F32)
                dom = jnp.where(sels[hh], do_, jnp.zeros_like(do_))
                dp_all = lax.dot_general(dom, v, _NT, preferred_element_type=F32)
                dfp = jnp.zeros((1, tkt), F32)
                for r in range(tqc // rg):
                    rows = slice(r * rg, (r + 1) * rg)
                    qrows = pl.ds(q0 + r * rg, rg)
                    p = jnp.exp(s_all[rows, :] - lse_ref[hh, qrows])
                    if masked:
                        p = jnp.where(_band_mask(q0 + r * rg, ki * tkt, rg, tkt), p, 0.0)
                    ds = p * (dp_all[rows, :] - dl_ref[hh, qrows])
                    p_s[hh, rows, :] = p.astype(BF16)
                    ds_s[hh, rows, :] = ds.astype(BF16)
                    dfp = dfp + jnp.sum(ds, axis=0, keepdims=True)
                df_s[hh] -= dfp
                dv_s[hh] += lax.dot_general(p_s[hh], do_, _TN, preferred_element_type=F32)
                dk_s[hh] += lax.dot_general(ds_s[hh], qh, _TN, preferred_element_type=F32)

        for d in range(per):
            chunk(ki * per + d, True)

        def unmasked(qc, carry):
            chunk(qc, False)
            return carry

        lax.fori_loop((ki + 1) * per, nqc, unmasked, 0)
        dk_ref[...] = jnp.where(sels[0], dk_s[0], pltpu.roll(dk_s[1], HEAD_DIM, 1)).astype(BF16)
        dv_ref[...] = jnp.where(sels[0], dv_s[0], dv_s[1]).astype(BF16)
        dfk_ref[...] = df_s[...]

    kmap = lambda b, j, ki: (b * nk + ki, j)
    col1 = pl.BlockSpec((2, S, 1), lambda b, j, ki: (j, b, 0))
    rowk = pl.BlockSpec((2, 1, tkt), lambda b, j, ki: (j, 0, b * nk + ki))
    return pl.pallas_call(
        body, name=name, grid=(n_seq, N_HEADS // 2, nk),
        in_specs=[pl.BlockSpec((S, 2 * LANES), lambda b, j, ki: (b, j)),
                  pl.BlockSpec((tkt, 2 * LANES), kmap),
                  pl.BlockSpec((tkt, LANES), lambda b, j, ki: (b * nk + ki, vc + j)),
                  pl.BlockSpec((S, LANES), lambda b, j, ki: (b, j)), col1, col1],
        out_specs=[pl.BlockSpec((tkt, LANES), kmap), pl.BlockSpec((tkt, LANES), kmap), rowk],
        out_shape=[jax.ShapeDtypeStruct((T, BRANCH_W), BF16), jax.ShapeDtypeStruct((T, BRANCH_W), BF16),
                   jax.ShapeDtypeStruct((N_HEADS, 1, T), F32)],
        scratch_shapes=[pltpu.VMEM((2, tqc, tkt), BF16), pltpu.VMEM((2, tqc, tkt), BF16),
                        pltpu.VMEM((2, tkt, LANES), F32),
                        pltpu.VMEM((2, tkt, LANES), F32), pltpu.VMEM((2, 1, tkt), F32)],
        compiler_params=_cp(("parallel", "parallel", "parallel")),
    )(qa, ka, proj, do, lse, delta)


def _attn_bwd(qa, ka, proj, do, o32, lse, dproj, n_seq, name):
    T = qa.shape[0]
    S = T // n_seq
    tq, tk, rg = min(Q_TILE, S), min(K_CHUNK, S), ROW_GROUP
    nq, per, nkc = S // tq, tq // tk, S // tk

    def body(q_ref, k_ref, v_ref, do_ref, o_ref, lse_ref, _, dqkv_ref, dfk_ref,
             p_s, ds_s, dq_s, dk_s, dv_s, df_s):
        dk_s[...] = jnp.zeros_like(dk_s)
        dv_s[...] = jnp.zeros_like(dv_s)
        df_s[...] = jnp.zeros_like(df_s)
        sels = _pair_masks()

        for qi in range(nq):
            q0 = qi * tq
            do_t = do_ref[q0:q0 + tq, :]
            dq_s[...] = jnp.zeros_like(dq_s)
            prod = do_t.astype(F32) * o_ref[q0:q0 + tq, :]
            dls = [jnp.sum(jnp.where(sel, prod, 0.0), axis=-1, keepdims=True) for sel in sels]

            def chunk(kc, masked, r0, q0=q0, do_t=do_t, dls=dls):
                k0 = pl.multiple_of(kc * tk, tk)
                v = v_ref[pl.ds(k0, tk), :]
                do_a = do_t[r0:, :]
                for hh in range(2):
                    hl = slice(hh * LANES, (hh + 1) * LANES)
                    qh, kh = q_ref[q0 + r0:q0 + tq, hl], k_ref[pl.ds(k0, tk), hl]
                    s_all = lax.dot_general(qh, kh, _NT, preferred_element_type=F32)
                    dom = jnp.where(sels[hh], do_a, jnp.zeros_like(do_a))
                    dp_all = lax.dot_general(dom, v, _NT, preferred_element_type=F32)
                    dfp = jnp.zeros((1, tk), F32)
                    for r in range(r0 // rg, tq // rg):
                        rows = slice(r * rg, (r + 1) * rg)
                        arows = slice(r * rg - r0, (r + 1) * rg - r0)
                        qrows = slice(q0 + r * rg, q0 + (r + 1) * rg)
                        p = jnp.exp(s_all[arows, :] - lse_ref[hh, qrows])
                        if masked:
                            p = jnp.where(_band_mask(q0 + r * rg, k0, rg, tk), p, 0.0)
                        ds = p * (dp_all[arows, :] - dls[hh][rows])
                        p_s[hh, rows, :] = p.astype(BF16)
                        ds_s[hh, rows, :] = ds.astype(BF16)
                        dfp = dfp + jnp.sum(ds, axis=0, keepdims=True)
                    df_s[hh, kc] -= dfp
                    dq_s[hh, r0:, :] += jnp.dot(ds_s[hh, r0:, :], kh, preferred_element_type=F32)
                    dv_s[hh, pl.ds(k0, tk), :] += lax.dot_general(p_s[hh, r0:, :], do_a, _TN,
                                                                  preferred_element_type=F32)
                    dk_s[hh, pl.ds(k0, tk), :] += lax.dot_general(ds_s[hh, r0:, :], qh, _TN,
                                                                  preferred_element_type=F32)

            def unmasked(kc, carry, chunk=chunk):
                chunk(kc, False, 0)
                return carry

            lax.fori_loop(0, qi * per, unmasked, 0)
            for d in range(per):
                chunk(qi * per + d, True, d * tk)
            dq = jnp.where(sels[0], dq_s[0], pltpu.roll(dq_s[1], HEAD_DIM, 1))
            dqkv_ref[q0:q0 + tq, :LANES] = (dq * (HEAD_DIM ** -0.5)).astype(BF16)

        dqkv_ref[:, LANES:2 * LANES] = jnp.where(sels[0], dk_s[0], pltpu.roll(dk_s[1], HEAD_DIM, 1)).astype(BF16)
        dqkv_ref[:, 2 * LANES:] = jnp.where(sels[0], dv_s[0], dv_s[1]).astype(BF16)
        for c in range(nkc):
            dfk_ref[:, :, c * tk:(c + 1) * tk] = df_s[:, c]

    seq = lambda w: pl.BlockSpec((S, w), lambda b, j: (b, j))
    col1 = pl.BlockSpec((2, S, 1), lambda b, j: (j, b, 0))
    vblk = pl.BlockSpec((S, LANES), lambda b, j: (b, OFF_QKV // LANES + 3 * j + 2))
    return pl.pallas_call(
        body, name=name, grid=(n_seq, N_HEADS // 2),
        in_specs=[seq(2 * LANES), seq(2 * LANES), vblk, seq(LANES), seq(LANES), col1,
                  pl.BlockSpec(memory_space=pl.ANY)],
        out_specs=[pl.BlockSpec((S, TRIPLE), lambda b, j: (b, OFF_QKV // TRIPLE + j)),
                   pl.BlockSpec((2, 1, S), lambda b, j: (j, 0, b))],
        out_shape=[jax.ShapeDtypeStruct(dproj.shape, BF16), jax.ShapeDtypeStruct((N_HEADS, 1, T), F32)],
        input_output_aliases={6: 0},
        scratch_shapes=[pltpu.VMEM((2, tq, tk), BF16), pltpu.VMEM((2, tq, tk), BF16),
                        pltpu.VMEM((2, tq, LANES), F32), pltpu.VMEM((2, S, LANES), F32),
                        pltpu.VMEM((2, S, LANES), F32), pltpu.VMEM((2, nkc, 1, tk), F32)],
        compiler_params=_cp(("parallel", "parallel")),
    )(qa, ka, proj, do, o32, lse, dproj)


def _shift_down(v, k, row):
    return jnp.where(row >= k, pltpu.roll(v, k, 0), 0.0)


def _shift_up(v, k, row, S):
    return jnp.where(row < S - k, pltpu.roll(v, S - k, 0), 0.0)


def _pool_diff(uf, w, row):
    acc, k = uf, 1
    while k < w:
        acc = acc + _shift_down(acc, k, row)
        k *= 2
    n = jnp.minimum(row + 1, w).astype(F32)
    return acc / n - uf


def _pool_fwd(proj, pool_w, pool_scale, n_seq, name):
    T = proj.shape[0]
    S = T // n_seq

    def body(u_ref, w_ref, sc_ref, o_ref, d_s):
        g = pl.program_id(1)
        row = lax.broadcasted_iota(jnp.int32, (S, POOL_GD), 0)
        uf = u_ref[...].astype(F32)
        for gi, wlen in enumerate(POOL_WINDOWS):
            @pl.when(g == gi)
            def _(wlen=wlen):
                d_s[...] = _pool_diff(uf, wlen, row).astype(BF16)
        e = jnp.dot(d_s[...], w_ref[0], preferred_element_type=F32)
        o_ref[...] = (e * sc_ref[...]).astype(BF16)

    uc = OFF_U // POOL_GD
    return pl.pallas_call(
        body, name=name, grid=(n_seq, len(POOL_WINDOWS)),
        in_specs=[pl.BlockSpec((S, POOL_GD), lambda b, g: (b, uc + g)),
                  pl.BlockSpec((1, POOL_GD, POOL_GD), lambda b, g: (g, 0, 0)),
                  pl.BlockSpec((1, POOL_GD), lambda b, g: (0, g))],
        out_specs=pl.BlockSpec((S, POOL_GD), lambda b, g: (b, g)),
        out_shape=jax.ShapeDtypeStruct((T, BRANCH_W), BF16),
        scratch_shapes=[pltpu.VMEM((S, POOL_GD), BF16)],
        compiler_params=_cp(("parallel", "parallel")),
    )(proj, pool_w, pool_scale)


def _pool_bwd(proj, dout, pool_w, pool_scale, dproj, n_seq, name):
    T = proj.shape[0]
    S = T // n_seq

    def body(u_ref, do_ref, w_ref, sc_ref, _, du_ref, dw_ref, dsc_ref, d_s):
        g, b = pl.program_id(0), pl.program_id(1)
        row = lax.broadcasted_iota(jnp.int32, (S, POOL_GD), 0)
        uf = u_ref[...].astype(F32)
        for gi, wlen in enumerate(POOL_WINDOWS):
            @pl.when(g == gi)
            def _(wlen=wlen):
                d_s[...] = _pool_diff(uf, wlen, row).astype(BF16)
        db16 = d_s[...]
        w = w_ref[0]
        e = jnp.dot(db16, w, preferred_element_type=F32)
        dof = do_ref[...].astype(F32)
        dsc = jnp.sum(dof * e, axis=0, keepdims=True)
        de = (dof * sc_ref[...]).astype(BF16)
        dd = lax.dot_general(de, w, (((1,), (1,)), ((), ())), preferred_element_type=F32)
        dw = lax.dot_general(db16, de, (((0,), (0,)), ((), ())), preferred_element_type=F32)
        for gi, wlen in enumerate(POOL_WINDOWS):
            @pl.when(g == gi)
            def _(wlen=wlen):
                n = jnp.minimum(row + 1, wlen).astype(F32)
                acc, k = dd / n, 1
                while k < wlen:
                    acc = acc + _shift_up(acc, k, row, S)
                    k *= 2
                du_ref[...] = (acc - dd).astype(BF16)

        @pl.when(b == 0)
        def _():
            dw_ref[0] = dw
            dsc_ref[...] = dsc

        @pl.when(b > 0)
        def _():
            dw_ref[0] += dw
            dsc_ref[...] += dsc

    uc = OFF_U // POOL_GD
    return pl.pallas_call(
        body, name=name, grid=(len(POOL_WINDOWS), n_seq),
        in_specs=[pl.BlockSpec((S, POOL_GD), lambda g, b: (b, uc + g)),
                  pl.BlockSpec((S, POOL_GD), lambda g, b: (b, g)),
                  pl.BlockSpec((1, POOL_GD, POOL_GD), lambda g, b: (g, 0, 0)),
                  pl.BlockSpec((1, POOL_GD), lambda g, b: (0, g)),
                  pl.BlockSpec(memory_space=pl.ANY)],
        out_specs=[pl.BlockSpec((S, POOL_GD), lambda g, b: (b, uc + g)),
                   pl.BlockSpec((1, POOL_GD, POOL_GD), lambda g, b: (g, 0, 0)),
                   pl.BlockSpec((1, POOL_GD), lambda g, b: (0, g))],
        out_shape=[jax.ShapeDtypeStruct(dproj.shape, BF16),
                   jax.ShapeDtypeStruct((len(POOL_WINDOWS), POOL_GD, POOL_GD), F32),
                   jax.ShapeDtypeStruct((1, BRANCH_W), F32)],
        input_output_aliases={4: 0},
        scratch_shapes=[pltpu.VMEM((S, POOL_GD), BF16)],
        compiler_params=_cp(("parallel", "arbitrary")),
    )(proj, dout, pool_w, pool_scale, dproj)


def _conv_fwd(proj, conv_w, n_seq, name):
    T = proj.shape[0]
    S = T // n_seq
    nc = BRANCH_W // LANES

    def body(c_ref, w_ref, o_ref):
        row = lax.broadcasted_iota(jnp.int32, (S, LANES), 0)
        cv, cb, cc = (c_ref[:, t * LANES:(t + 1) * LANES].astype(F32) for t in range(3))
        z = cc * cv
        w = w_ref[...]
        y = w[0:1] * _shift_down(z, 2, row) + w[1:2] * _shift_down(z, 1, row) + w[2:3] * z
        o_ref[...] = (cb * y).astype(BF16)

    return pl.pallas_call(
        body, name=name, grid=(n_seq, nc),
        in_specs=[pl.BlockSpec((S, TRIPLE), lambda b, j: (b, OFF_CONV // TRIPLE + j)),
                  pl.BlockSpec((CONV_K, LANES), lambda b, j: (0, j))],
        out_specs=pl.BlockSpec((S, LANES), lambda b, j: (b, j)),
        out_shape=jax.ShapeDtypeStruct((T, BRANCH_W), BF16),
        compiler_params=_cp(("parallel", "parallel")),
    )(proj, conv_w)


def _conv_bwd(proj, dout, conv_w, dproj, n_seq, name):
    T = proj.shape[0]
    S = T // n_seq
    nc = BRANCH_W // LANES

    def body(c_ref, do_ref, w_ref, _, dc_ref, dw_ref):
        b = pl.program_id(1)
        row = lax.broadcasted_iota(jnp.int32, (S, LANES), 0)
        cv, cb, cc = (c_ref[:, t * LANES:(t + 1) * LANES].astype(F32) for t in range(3))
        dof = do_ref[...].astype(F32)
        w = w_ref[...]
        z = cc * cv
        z1, z2 = _shift_down(z, 1, row), _shift_down(z, 2, row)
        y = w[0:1] * z2 + w[1:2] * z1 + w[2:3] * z
        dy = dof * cb
        dz = w[2:3] * dy + w[1:2] * _shift_up(dy, 1, row, S) + w[0:1] * _shift_up(dy, 2, row, S)
        dc_ref[:, :LANES] = (dz * cc).astype(BF16)
        dc_ref[:, LANES:2 * LANES] = (dof * y).astype(BF16)
        dc_ref[:, 2 * LANES:] = (dz * cv).astype(BF16)
        dws = [jnp.sum(dy * zk, axis=0, keepdims=True) for zk in (z2, z1, z)]

        @pl.when(b == 0)
        def _():
            for kk in range(CONV_K):
                dw_ref[kk:kk + 1, :] = dws[kk]

        @pl.when(b > 0)
        def _():
            for kk in range(CONV_K):
                dw_ref[kk:kk + 1, :] += dws[kk]

    triple = pl.BlockSpec((S, TRIPLE), lambda j, b: (b, OFF_CONV // TRIPLE + j))
    wsp = pl.BlockSpec((CONV_K, LANES), lambda j, b: (0, j))
    return pl.pallas_call(
        body, name=name, grid=(nc, n_seq),
        in_specs=[triple, pl.BlockSpec((S, LANES), lambda j, b: (b, j)), wsp, pl.BlockSpec(memory_space=pl.ANY)],
        out_specs=[triple, wsp],
        out_shape=[jax.ShapeDtypeStruct(dproj.shape, BF16), jax.ShapeDtypeStruct((CONV_K, BRANCH_W), F32)],
        input_output_aliases={3: 0},
        compiler_params=_cp(("parallel", "arbitrary")),
    )(proj, dout, conv_w, dproj)


def _mix_fwd(oa, ob, oc, wpa, wpp, wpc, proj, b_gate, name):
    T = oa.shape[0]
    tm = min(256, T)

    def body(oa_ref, ob_ref, oc_ref, wa_ref, wp_ref, wc_ref, g_ref, bg_ref, o_ref):
        acc = jnp.zeros((tm, D_MODEL), F32)
        for i, (x_ref, w_ref) in enumerate(((oa_ref, wa_ref), (ob_ref, wp_ref), (oc_ref, wc_ref))):
            y = jnp.dot(x_ref[...], w_ref[...], preferred_element_type=F32)
            sl = slice(i * D_MODEL, (i + 1) * D_MODEL)
            acc = acc + _sigmoid(g_ref[:, sl].astype(F32) + bg_ref[:, sl]) * y
        o_ref[...] = acc.astype(BF16)

    br = pl.BlockSpec((tm, BRANCH_W), lambda i: (i, 0))
    wsp = pl.BlockSpec((BRANCH_W, D_MODEL), lambda i: (0, 0))
    return pl.pallas_call(
        body, name=name, grid=(T // tm,),
        in_specs=[br, br, br, wsp, wsp, wsp, pl.BlockSpec((tm, GATE_W), lambda i: (i, 0)),
                  pl.BlockSpec((1, GATE_W), lambda i: (0, 0))],
        out_specs=pl.BlockSpec((tm, D_MODEL), lambda i: (i, 0)),
        out_shape=jax.ShapeDtypeStruct((T, D_MODEL), BF16),
        compiler_params=_cp(("parallel",)),
    )(oa, ob, oc, wpa, wpp, wpc, proj, b_gate)


def _mix_bwd(oa, ob, oc, wpa, wpp, wpc, proj, b_gate, dmixed, name):
    T = oa.shape[0]
    tm = min(256, T)

    def body(oa_ref, ob_ref, oc_ref, wa_ref, wp_ref, wc_ref, g_ref, bg_ref, dm_ref,
             dya_ref, dyb_ref, dyc_ref, dg_ref, dbg_ref):
        i0 = pl.program_id(0)
        dm = dm_ref[...].astype(F32)
        parts = []
        for i, (x_ref, w_ref, dy_ref) in enumerate(((oa_ref, wa_ref, dya_ref), (ob_ref, wp_ref, dyb_ref),
                                                    (oc_ref, wc_ref, dyc_ref))):
            y = jnp.dot(x_ref[...], w_ref[...], preferred_element_type=F32)
            sl = slice(i * D_MODEL, (i + 1) * D_MODEL)
            gate = _sigmoid(g_ref[:, sl].astype(F32) + bg_ref[:, sl])
            dy_ref[...] = (dm * gate).astype(BF16)
            dgl = dm * y * gate * (1.0 - gate)
            dg_ref[:, sl] = dgl.astype(BF16)
            parts.append(jnp.sum(dgl, axis=0, keepdims=True))

        @pl.when(i0 == 0)
        def _():
            for i in range(3):
                dbg_ref[:, i * D_MODEL:(i + 1) * D_MODEL] = parts[i]

        @pl.when(i0 > 0)
        def _():
            for i in range(3):
                dbg_ref[:, i * D_MODEL:(i + 1) * D_MODEL] += parts[i]

    br = pl.BlockSpec((tm, BRANCH_W), lambda i: (i, 0))
    wsp = pl.BlockSpec((BRANCH_W, D_MODEL), lambda i: (0, 0))
    row = pl.BlockSpec((tm, D_MODEL), lambda i: (i, 0))
    gsp = pl.BlockSpec((tm, GATE_W), lambda i: (i, 0))
    bsp = pl.BlockSpec((1, GATE_W), lambda i: (0, 0))
    act = jax.ShapeDtypeStruct((T, D_MODEL), BF16)
    return pl.pallas_call(
        body, name=name, grid=(T // tm,),
        in_specs=[br, br, br, wsp, wsp, wsp, gsp, bsp, row],
        out_specs=[row, row, row, gsp, bsp],
        out_shape=[act, act, act, jax.ShapeDtypeStruct((T, MAIN_COLS), BF16),
                   jax.ShapeDtypeStruct((1, GATE_W), F32)],
        compiler_params=_cp(("arbitrary",)),
    )(oa, ob, oc, wpa, wpp, wpc, proj, b_gate, dmixed)


GU_TILE = 256


def _gu_col(c):
    t, r = divmod(c, GU_TILE)
    return (t // 2) * GU_TILE + r + (FFN_HIDDEN if t % 2 else 0)


def _gate_up_swiglu(h, w, name):
    T, K = h.shape
    tm = min(2048, T)

    def body(h_ref, w_ref, ab_ref, s_ref):
        prod = jnp.dot(h_ref[...], w_ref[...], preferred_element_type=F32)
        ab_ref[...] = prod.astype(BF16)
        a = prod[:, :GU_TILE]
        s_ref[...] = (a * _sigmoid(a) * prod[:, GU_TILE:]).astype(BF16)

    return pl.pallas_call(
        body, name=name, grid=(T // tm, FFN_HIDDEN // GU_TILE),
        in_specs=[pl.BlockSpec((tm, K), lambda i, j: (i, 0)), pl.BlockSpec((K, 2 * GU_TILE), lambda i, j: (0, j))],
        out_specs=[pl.BlockSpec((tm, 2 * GU_TILE), lambda i, j: (i, j)), pl.BlockSpec((tm, GU_TILE), lambda i, j: (i, j))],
        out_shape=[jax.ShapeDtypeStruct((T, 2 * FFN_HIDDEN), BF16), jax.ShapeDtypeStruct((T, FFN_HIDDEN), BF16)],
        compiler_params=_cp(("parallel", "parallel")),
    )(h, w)


def _swiglu_bwd_fused(dx, w_down, ab, name):
    T, K = dx.shape
    tm = min(2048, T)

    def body(dx_ref, w_ref, ab_ref, o_ref):
        ds = lax.dot_general(dx_ref[...], w_ref[...], _NT, preferred_element_type=F32)
        a = ab_ref[:, :GU_TILE].astype(F32)
        b = ab_ref[:, GU_TILE:].astype(F32)
        sg = _sigmoid(a)
        o_ref[:, :GU_TILE] = (ds * b * sg * (1.0 + a * (1.0 - sg))).astype(BF16)
        o_ref[:, GU_TILE:] = (ds * a * sg).astype(BF16)

    pair = pl.BlockSpec((tm, 2 * GU_TILE), lambda i, j: (i, j))
    return pl.pallas_call(
        body, name=name, grid=(T // tm, FFN_HIDDEN // GU_TILE),
        in_specs=[pl.BlockSpec((tm, K), lambda i, j: (i, 0)), pl.BlockSpec((GU_TILE, K), lambda i, j: (j, 0)), pair],
        out_specs=pair, out_shape=jax.ShapeDtypeStruct((T, 2 * FFN_HIDDEN), BF16),
        compiler_params=_cp(("parallel", "parallel")),
    )(dx, w_down, ab)


def _adamw_update(w_ref, g_ref, m_ref, v_ref, d_ref, nm_ref, nv_ref):
    gv = g_ref[...]
    nm = ADAM_B1 * m_ref[...] + (1.0 - ADAM_B1) * gv
    nv = ADAM_B2 * v_ref[...] + (1.0 - ADAM_B2) * (gv * gv)
    m_hat = nm / (1.0 - ADAM_B1 ** ADAM_STEP)
    v_hat = nv / (1.0 - ADAM_B2 ** ADAM_STEP)
    d_ref[...] = -ADAM_LR * (m_hat / (jnp.sqrt(v_hat) + ADAM_EPS) + ADAM_WD * w_ref[...])
    nm_ref[...] = nm
    nv_ref[...] = nv


def _adamw_many(ws, gs, ms, vs, name):
    n = len(ws)

    def body(*refs):
        ins, outs = refs[:4 * n], refs[4 * n:]
        for t in range(n):
            _adamw_update(ins[t], ins[n + t], ins[2 * n + t], ins[3 * n + t], outs[t], outs[n + t], outs[2 * n + t])

    shapes = [jax.ShapeDtypeStruct(w.shape, F32) for w in ws]
    out = pl.pallas_call(body, name=name, out_shape=shapes * 3, compiler_params=_cp())(*ws, *gs, *ms, *vs)
    return out[:n], out[n:2 * n], out[2 * n:]


def _adamw(w, g, m, v, name):
    R, C = w.shape
    tr = R
    for cand in (256, 352, 128, 64, 8):
        if R > cand and R % cand == 0:
            tr = cand
            break

    def body(w_ref, g_ref, m_ref, v_ref, d_ref, nm_ref, nv_ref):
        _adamw_update(w_ref, g_ref, m_ref, v_ref, d_ref, nm_ref, nv_ref)

    blk = pl.BlockSpec((tr, C), lambda i: (i, 0))
    sh = jax.ShapeDtypeStruct((R, C), F32)
    return pl.pallas_call(
        body, name=name, grid=(R // tr,), in_specs=[blk] * 4, out_specs=[blk] * 3, out_shape=[sh] * 3,
        compiler_params=_cp(("parallel",)),
    )(w, g, m, v)


def _sum_slabs(x, name):
    n, R, C = x.shape
    tr = R
    for cand in (512, 256, 128, 64, 32, 16, 8):
        if R > cand and R % cand == 0:
            tr = cand
            break

    def body(x_ref, o_ref):
        acc = x_ref[0].astype(F32)
        for j in range(1, n):
            acc = acc + x_ref[j].astype(F32)
        o_ref[...] = acc

    return pl.pallas_call(
        body, name=name, grid=(R // tr,), in_specs=[pl.BlockSpec((n, tr, C), lambda i: (0, i, 0))],
        out_specs=pl.BlockSpec((tr, C), lambda i: (i, 0)), out_shape=jax.ShapeDtypeStruct((R, C), F32),
        compiler_params=_cp(("parallel",)),
    )(x)


def _multi_gather(xs, layers, name):
    nt = len(xs)
    shapes = [x.shape if lay is None else x.shape[1:] for x, lay in zip(xs, layers)]

    def body(*refs):
        x_refs, out_refs = refs[:nt], refs[nt:2 * nt]
        send_sems, recv_sems, local_sems = refs[2 * nt:]
        x_, y_, c_ = lax.axis_index("x"), lax.axis_index("y"), lax.axis_index("c")
        me, sibling = (x_, y_, c_), (x_, y_, 1 - c_)
        chips = [(1 - x_, y_), (x_, 1 - y_), (1 - x_, 1 - y_)]

        def own_block(t):
            return x_refs[t] if layers[t] is None else x_refs[t].at[layers[t]]

        def copy(t, k, block, to, own=False):
            px, py, pc = block
            dst = out_refs[t].at[4 * px + 2 * py + pc]
            return pltpu.make_async_remote_copy(
                src_ref=own_block(t) if own else dst, dst_ref=dst,
                send_sem=send_sems.at[t, k], recv_sem=recv_sems.at[t, k],
                device_id=to, device_id_type=pl.DeviceIdType.MESH)

        mine, first, passed = [], [], []
        for t in range(nt):
            mine.append(pltpu.make_async_copy(own_block(t), out_refs[t].at[4 * x_ + 2 * y_ + c_], local_sems.at[t]))
            mine[-1].start()
            first.append([copy(t, 1 + j, me, (*chip, c_), own=True) for j, chip in enumerate(chips)]
                         + [copy(t, 0, me, sibling, own=True)])
            for cp in first[-1]:
                cp.start()
        for t in range(nt):
            for j, chip in enumerate(chips):
                copy(t, 1 + j, (*chip, c_), me).wait_recv()
                passed.append(copy(t, 4 + j, (*chip, c_), sibling))
                passed[-1].start()
        for t in range(nt):
            copy(t, 0, sibling, me).wait_recv()
            for j, chip in enumerate(chips):
                copy(t, 4 + j, (*chip, 1 - c_), me).wait_recv()
        for cp in [c for f in first for c in f] + passed:
            cp.wait_send()
        for cp in mine:
            cp.wait()

    hbm = pl.BlockSpec(memory_space=pl.ANY)
    return pl.pallas_call(
        body, name=name, out_shape=[jax.ShapeDtypeStruct((N_DEV,) + tuple(s), x.dtype) for s, x in zip(shapes, xs)],
        in_specs=[hbm] * nt, out_specs=[hbm] * nt,
        scratch_shapes=[pltpu.SemaphoreType.DMA((nt, 7)), pltpu.SemaphoreType.DMA((nt, 7)),
                        pltpu.SemaphoreType.DMA((nt,))],
    )(*xs)


def _multi_exchange(sends, name):
    nt = len(sends)

    def body(*refs):
        s_refs, r_refs = refs[:nt], refs[nt:2 * nt]
        send_sems, recv_sems, local_sems = refs[2 * nt:]
        x_, y_, c_ = lax.axis_index("x"), lax.axis_index("y"), lax.axis_index("c")
        me = 4 * x_ + 2 * y_ + c_
        mine, out, inc = [], [], []
        for t in range(nt):
            mine.append(pltpu.make_async_copy(s_refs[t].at[me], r_refs[t].at[me], local_sems.at[t]))
            mine[-1].start()
        for k in (2, 4, 6, 3, 5, 7, 1):
            px, py, pc = x_ ^ ((k >> 2) & 1), y_ ^ ((k >> 1) & 1), c_ ^ (k & 1)
            peer = 4 * px + 2 * py + pc
            for t in range(nt):
                def copy(src, dst):
                    return pltpu.make_async_remote_copy(
                        src_ref=s_refs[t].at[src], dst_ref=r_refs[t].at[dst],
                        send_sem=send_sems.at[t, k - 1], recv_sem=recv_sems.at[t, k - 1],
                        device_id=(px, py, pc), device_id_type=pl.DeviceIdType.MESH)

                out.append(copy(peer, me))
                inc.append(copy(me, peer))
        for cp in out:
            cp.start()
        for cp in inc:
            cp.wait_recv()
        for cp in out:
            cp.wait_send()
        for cp in mine:
            cp.wait()

    hbm = pl.BlockSpec(memory_space=pl.ANY)
    return pl.pallas_call(
        body, name=name, out_shape=[jax.ShapeDtypeStruct(s.shape, s.dtype) for s in sends],
        in_specs=[hbm] * nt, out_specs=[hbm] * nt,
        scratch_shapes=[pltpu.SemaphoreType.DMA((nt, N_DEV - 1)), pltpu.SemaphoreType.DMA((nt, N_DEV - 1)),
                        pltpu.SemaphoreType.DMA((nt,))],
    )(*sends)


_HBM = pl.BlockSpec(memory_space=pltpu.HBM)
_SEM = pl.BlockSpec(memory_space=pltpu.SEMAPHORE)
_PEER_ORDER = (2, 4, 6, 3, 5, 7, 1)


def _split_copies(src_refs, land_refs, send_sems, recv_sems, layers, per_peer):
    x_, y_, c_ = lax.axis_index("x"), lax.axis_index("y"), lax.axis_index("c")
    me = 4 * x_ + 2 * y_ + c_
    copies = []
    for k in _PEER_ORDER:
        px, py, pc = x_ ^ ((k >> 2) & 1), y_ ^ ((k >> 1) & 1), c_ ^ (k & 1)
        peer = 4 * px + 2 * py + pc
        for t in range(len(src_refs)):
            if per_peer:
                src = src_refs[t].at[peer]
            else:
                src = src_refs[t] if layers[t] is None else src_refs[t].at[layers[t]]
            copies.append(pltpu.make_async_remote_copy(
                src_ref=src, dst_ref=land_refs[t].at[me],
                send_sem=send_sems.at[t * (N_DEV - 1) + k - 1], recv_sem=recv_sems.at[t * (N_DEV - 1) + k - 1],
                device_id=(px, py, pc), device_id_type=pl.DeviceIdType.MESH))
    return copies


def _own_copies(src_refs, land_refs, sems, layers, per_peer):
    nt = len(src_refs)
    me = 4 * lax.axis_index("x") + 2 * lax.axis_index("y") + lax.axis_index("c")
    copies = []
    for t in range(nt):
        if per_peer:
            src = src_refs[t].at[me]
        else:
            src = src_refs[t] if layers[t] is None else src_refs[t].at[layers[t]]
        copies.append(pltpu.make_async_copy(src, land_refs[t].at[me], sems.at[nt * (N_DEV - 1) + t]))
    return copies


def _split_start(srcs, layers, per_peer, after, name):
    nt = len(srcs)
    if per_peer:
        land_shapes = [s.shape for s in srcs]
    else:
        land_shapes = [(N_DEV,) + tuple(s.shape if lay is None else s.shape[1:]) for s, lay in zip(srcs, layers)]

    def body(*refs):
        src_refs, land_refs = refs[:nt], refs[nt:2 * nt]
        send_sems, recv_sems = refs[2 * nt + 1], refs[2 * nt + 2]
        token = refs[-1]
        for cp in _split_copies(src_refs, land_refs, send_sems, recv_sems, layers, per_peer):
            cp.start()
        for cp in _own_copies(src_refs, land_refs, send_sems, layers, per_peer):
            cp.start()
        token[...] = jnp.zeros_like(token)

    lands = [pltpu.with_memory_space_constraint(lax.empty(s, x.dtype), pltpu.HBM) for s, x in zip(land_shapes, srcs)]
    srcs = [pltpu.with_memory_space_constraint(x, pltpu.HBM) for x in srcs]
    out = pl.pallas_call(
        body, name=name,
        out_shape=(pltpu.SemaphoreType.DMA((nt * N_DEV,)), pltpu.SemaphoreType.DMA((nt * (N_DEV - 1),)),
                   *[pltpu.HBM(x.shape, x.dtype) for x in srcs], *[pltpu.HBM(s, x.dtype) for s, x in zip(land_shapes, srcs)],
                   jax.ShapeDtypeStruct((8, LANES), F32)),
        in_specs=[_HBM] * (2 * nt) + [pl.BlockSpec(memory_space=pl.ANY)],
        out_specs=(_SEM, _SEM, *([_HBM] * (2 * nt)), pl.BlockSpec(memory_space=pltpu.VMEM)),
        input_output_aliases={i: 2 + i for i in range(2 * nt)},
        compiler_params=pltpu.CompilerParams(has_side_effects=pltpu.SideEffectType.DATAFLOW_SIDE_EFFECTING),
    )(*srcs, *lands, after)
    return out[0], out[1], list(out[2:2 + nt]), list(out[2 + nt:2 + 2 * nt]), out[-1]


def _split_wait(started, layers, per_peer, after, name):
    send_sems, recv_sems, srcs, lands, _ = started
    nt = len(srcs)

    def body(*refs):
        src_refs, land_refs = refs[:nt], refs[nt:2 * nt]
        s_sems, r_sems = refs[2 * nt], refs[2 * nt + 1]
        for cp in _split_copies(src_refs, land_refs, s_sems, r_sems, layers, per_peer):
            cp.wait_send()
            cp.wait_recv()
        for cp in _own_copies(src_refs, land_refs, s_sems, layers, per_peer):
            cp.wait()

    out = pl.pallas_call(
        body, name=name,
        out_shape=tuple(pltpu.HBM(x.shape, x.dtype) for x in srcs + lands),
        in_specs=[_HBM] * (2 * nt) + [_SEM, _SEM, pl.BlockSpec(memory_space=pl.ANY)],
        out_specs=tuple([_HBM] * (2 * nt)),
        input_output_aliases={i: i for i in range(2 * nt)},
        compiler_params=pltpu.CompilerParams(has_side_effects=pltpu.SideEffectType.DATAFLOW_SIDE_EFFECTING),
    )(*srcs, *lands, send_sems, recv_sems, after)
    return list(out[nt:])


def _with_own(land, own):
    me = 4 * lax.axis_index("x") + 2 * lax.axis_index("y") + lax.axis_index("c")
    return lax.dynamic_update_slice_in_dim(land, own[None], me, axis=0)


def _runs(mapping):
    runs, c, n = [], 0, len(mapping)
    while c < n:
        if mapping[c] is None:
            c += 1
            continue
        sid, d, lo = mapping[c][0], mapping[c][1] - c, c
        while c < n and mapping[c] is not None and mapping[c][0] == sid and mapping[c][1] - c == d:
            c += 1
        runs.append((lo, c, sid, d))
    return runs


def _tile_plan(mapping, src_widths):
    runs = _runs(mapping)
    plan = []
    for t in range(len(mapping) // LANES):
        pieces = []
        for lo, hi, sid, d in runs:
            lo_t, hi_t = max(lo, t * LANES), min(hi, (t + 1) * LANES)
            if lo_t >= hi_t:
                continue
            a = ((lo_t + d) // LANES) * LANES
            win = min(2 * LANES, src_widths[sid] - a)
            shift = t * LANES + d - a
            pieces.append((sid, a, win, shift, lo_t - t * LANES, hi_t - t * LANES))
        plan.append(pieces)
    return plan


def _reblock(srcs, src_views, outs, out_views, name):
    R = srcs[0].shape[-2]
    tr = min(512, R)
    widths = {sid: srcs[ai].shape[-1] for sid, (ai, _) in src_views.items()}
    plans = [(ai, li, _tile_plan(mapping, widths)) for ai, li, mapping in out_views]
    ns = len(srcs)

    def body(*refs):
        s_refs, o_refs = refs[:ns], refs[ns:]
        cache = {}

        def shift_matrix(win, shift, lo, hi):
            key = (win, shift, lo, hi)
            if key not in cache:
                r = lax.broadcasted_iota(jnp.int32, (win, LANES), 0)
                c = lax.broadcasted_iota(jnp.int32, (win, LANES), 1)
                hit = jnp.logical_and(r - c == shift, jnp.logical_and(c >= lo, c < hi))
                cache[key] = jnp.where(hit, 1.0, 0.0).astype(BF16)
            return cache[key]

        for ai, li, plan in plans:
            for t, pieces in enumerate(plan):
                acc = None
                whole = len(pieces) == 1 and pieces[0][3:] == (0, 0, LANES)
                for sid, a, win, shift, lo, hi in pieces:
                    sa, sl = src_views[sid]
                    if whole:
                        win = LANES
                    src = s_refs[sa][:, a:a + win] if sl is None else s_refs[sa][sl, :, a:a + win]
                    if whole:
                        acc = src
                    else:
                        part = jnp.dot(src, shift_matrix(win, shift, lo, hi), preferred_element_type=F32)
                        acc = part if acc is None else acc + part
                val = jnp.zeros((tr, LANES), BF16) if acc is None else acc.astype(BF16)
                if li is None:
                    o_refs[ai][:, t * LANES:(t + 1) * LANES] = val
                else:
                    o_refs[ai][li, :, t * LANES:(t + 1) * LANES] = val

    def spec(shape):
        if len(shape) == 2:
            return pl.BlockSpec((tr, shape[1]), lambda i: (i, 0))
        return pl.BlockSpec((shape[0], tr, shape[2]), lambda i: (0, i, 0))

    return pl.pallas_call(
        body, name=name, grid=(R // tr,), in_specs=[spec(s.shape) for s in srcs],
        out_specs=[spec(s) for s in outs], out_shape=[jax.ShapeDtypeStruct(s, BF16) for s in outs],
        compiler_params=_cp(("parallel",)),
    )(*srcs)


SHARDED = ("w_in", "w_gate_up", "w_proj_attn", "w_proj_pool", "w_proj_conv", "w_out", "w_down")
WEIGHT_ORDER = ("attn_norm", "w_in", "b_forget", "b_gate", "w_proj_attn", "pool_w", "pool_scale", "w_proj_pool",
                "conv_w", "w_proj_conv", "w_out", "ffn_norm", "w_gate_up", "w_down", "final_norm")
IN_SHARD, IN_SHARD_PAD = IN_COLS // N_DEV, 896
GU_SHARD, GU_SHARD_PAD = 2 * FFN_HIDDEN // N_DEV, 768


def _w_in_col(c):
    if c < OFF_QKV:
        return c + 3592
    if c < OFF_U:
        base, off = (0, OFF_QKV) if c < OFF_CONV else (2056, OFF_CONV)
        j, t = divmod(c - off, TRIPLE)
        which, e = divmod(t, LANES)
        return base + which * BRANCH_W + j * LANES + e
    return c - OFF_U + 1544


def _w_in_full(gathered, name):
    main = [divmod(_w_in_col(c), IN_SHARD) for c in range(MAIN_COLS)]
    fcols = [divmod(1536 + c, IN_SHARD) if c < N_HEADS else None for c in range(LANES)]
    R = gathered.shape[1]
    return _reblock([gathered], {i: (0, i) for i in range(N_DEV)}, [(R, MAIN_COLS), (R, LANES)],
                    [(0, None, main), (1, None, fcols)], name)


def _w_in_slabs(dmain, dwf, name):
    inv = {_w_in_col(c): ("m", c) for c in range(MAIN_COLS)}
    inv.update({1536 + c: ("f", c) for c in range(N_HEADS)})
    views = []
    for i in range(N_DEV):
        mapping = [inv[IN_SHARD * i + j] if j < IN_SHARD else None for j in range(IN_SHARD_PAD)]
        views.append((0, i, mapping))
    R = dmain.shape[0]
    return _reblock([dmain, dwf], {"m": (0, None), "f": (1, None)}, [(N_DEV, R, IN_SHARD_PAD)], views, name)[0]


def _w_gu_full(gathered, name):
    mapping = [divmod(_gu_col(c), GU_SHARD) for c in range(2 * FFN_HIDDEN)]
    R = gathered.shape[1]
    return _reblock([gathered], {i: (0, i) for i in range(N_DEV)}, [(R, 2 * FFN_HIDDEN)], [(0, None, mapping)], name)[0]


def _w_gu_slabs(dw, name):
    inv = {_gu_col(c): c for c in range(2 * FFN_HIDDEN)}
    views = [(0, i, [("w", inv[GU_SHARD * i + j]) if j < GU_SHARD else None for j in range(GU_SHARD_PAD)])
             for i in range(N_DEV)]
    R = dw.shape[0]
    return _reblock([dw], {"w": (0, None)}, [(N_DEV, R, GU_SHARD_PAD)], views, name)[0]


def _layer_fwd(x, W, n_seq, l, h1=None, next_norm=None):
    T = x.shape[0]
    sfx = f"_l{l}"
    if h1 is None:
        h1 = _rms_fwd(x, W["attn_norm"], "rms1" + sfx)
    proj = _matmul(h1, W["w_main"], mode="nn", out_dtype=BF16, name="proj_main" + sfx)
    f = _matmul(h1, W["w_f"], mode="nn", out_dtype=F32, name="proj_f" + sfx)
    qa, ka, va = _fox_prep(f, W["b_forget"], proj, n_seq, "fox_prep" + sfx)
    oa, oa32, lse = _attn_fwd2(qa, ka, va, n_seq, "attn_fwd" + sfx)
    if "late" in W:
        W.update(W.pop("late")(oa))
    ob = _pool_fwd(proj, W["pool_w"], W["pool_scale"], n_seq, "pool_fwd" + sfx)
    oc = _conv_fwd(proj, W["conv_w"], n_seq, "conv_fwd" + sfx)
    mixed = _mix_fwd(oa, ob, oc, W["w_proj_attn"], W["w_proj_pool"], W["w_proj_conv"], proj, W["b_gate"],
                     "mix_fwd" + sfx)
    x2, h2 = _matmul(mixed, W["w_out"], mode="nn", out_dtype=F32, name="out_proj" + sfx, tm=1024, tn=1024,
                     residual=x, rms_g=W["ffn_norm"])
    ab, s = _gate_up_swiglu(h2, W["w_gate_up"], "gate_up" + sfx)
    x3 = _matmul(s, W["w_down"], mode="nn", out_dtype=F32, name="down" + sfx, tm=1024, tn=1024, tk=1408,
                 residual=x2, rms_g=next_norm)
    x3, h1_next = x3 if next_norm is not None else (x3, None)
    saved = dict(x=x, h1=h1, proj=proj, f=f, qa=qa, ka=ka, oa=oa, oa32=oa32, lse=lse, ob=ob, oc=oc, mixed=mixed, x2=x2,
                 h2=h2, ab=ab, s=s)
    return x3, saved, h1_next


def _layer_bwd(dx3, dx3b, W, sv, n_seq, l, stage=None):
    T = dx3.shape[0]
    sfx = f"_l{l}"
    G = {}
    stage = stage or (lambda l, group, G, W: W)
    dab = _swiglu_bwd_fused(dx3b, W["w_down"], sv["ab"], "d_ab" + sfx)
    G["w_down"] = _matmul(sv["s"], dx3b, mode="tn", out_dtype=BF16, name="dw_down" + sfx, tm=256, tn=1024)
    dh2 = _matmul(dab, W["w_gate_up"], mode="nt", out_dtype=BF16, name="d_h2" + sfx, tm=1024, tn=1024, tk=1408)
    G["w_gate_up"] = _matmul(sv["h2"], dab, mode="tn", out_dtype=BF16, name="dw_gate_up" + sfx, tm=1024)
    W = stage(l, "ffn", G, W)
    dx2, dx2b, G["ffn_norm"] = _rms_bwd(sv["x2"], W["ffn_norm"], dh2, dx3, "rms2_bwd" + sfx)
    dmixed = _matmul(dx2b, W["w_out"], mode="nt", out_dtype=BF16, name="d_mixed" + sfx)
    G["w_out"] = _matmul(sv["mixed"], dx2b, mode="tn", out_dtype=BF16, name="dw_out" + sfx, tm=1024)
    dya, dyb, dyc, dproj, G["b_gate"] = _mix_bwd(sv["oa"], sv["ob"], sv["oc"], W["w_proj_attn"], W["w_proj_pool"],
                                                 W["w_proj_conv"], sv["proj"], W["b_gate"], dmixed, "mix_bwd" + sfx)
    douts = {}
    for br, dy, o in (("attn", dya, sv["oa"]), ("pool", dyb, sv["ob"]), ("conv", dyc, sv["oc"])):
        douts[br] = _matmul(dy, W["w_proj_" + br], mode="nt", out_dtype=BF16, name=f"d_{br}_out" + sfx)
        G["w_proj_" + br] = _matmul(o, dy, mode="tn", out_dtype=BF16, name=f"dw_proj_{br}" + sfx, tm=512)
    W = stage(l, "mix", G, W)
    dproj, G["conv_w"] = _conv_bwd(sv["proj"], douts["conv"], W["conv_w"], dproj, n_seq, "conv_bwd" + sfx)
    dproj, G["pool_w"], G["pool_scale"] = _pool_bwd(sv["proj"], douts["pool"], W["pool_w"], W["pool_scale"], dproj,
                                                    n_seq, "pool_bwd" + sfx)
    dproj, dFk = _attn_bwd(sv["qa"], sv["ka"], sv["proj"], douts["attn"], sv["oa32"], sv["lse"], dproj, n_seq,
                           "attn_bwd" + sfx)
    dF = jnp.pad(dFk.reshape(N_HEADS, T).T, ((0, 0), (0, LANES - N_HEADS)))
    df, G["b_forget"] = _fox_cumsum_bwd(sv["f"], W["b_forget"], dF, n_seq, "fox_cumsum_bwd" + sfx)
    G["w_main"] = _matmul(sv["h1"], dproj, mode="tn", out_dtype=BF16, name="dw_main" + sfx, tm=1024)
    G["w_f"] = _matmul(sv["h1"], df, mode="tn", out_dtype=BF16, name="dw_f" + sfx, tm=1024)
    W = stage(l, "w_in", G, W)
    dh1 = _matmul(df, W["w_f"], mode="nt", out_dtype=F32, name="d_h1_f" + sfx)
    dh1 = _matmul(dproj, W["w_main"], mode="nt", out_dtype=F32, name="d_h1_main" + sfx, tm=1024, tn=1024, tk=1664,
                  residual=dh1)
    dx, dxb, G["attn_norm"] = _rms_bwd(sv["x"], W["attn_norm"], dh1, dx2, "rms1_bwd" + sfx)
    return dx, dxb, G


def _replicated_operands(rep, l):
    W = {}
    W["attn_norm"], W["ffn_norm"] = rep["attn_norm"][l], rep["ffn_norm"][l]
    W["b_forget"] = jnp.pad(rep["b_forget"][l].reshape(1, N_HEADS), ((0, 0), (0, LANES - N_HEADS)))
    W["b_gate"] = rep["b_gate"][l].reshape(1, GATE_W)
    W["pool_w"] = rep["pool_w"][l].astype(BF16)
    W["pool_scale"] = rep["pool_scale"][l].reshape(1, BRANCH_W)
    return W


def _local_step(x, target, get_W, attn_norms, final_norm, stage=None):
    n_seq, S, Dm = x.shape
    T = n_seq * S
    xt = x.reshape(T, Dm)
    saved, Ws, h1 = [], [], None
    for l in range(DEPTH):
        Ws.append(get_W(l, xt))
        next_norm = attn_norms[l + 1] if l + 1 < DEPTH else None
        xt, sv, h1 = _layer_fwd(xt, Ws[l], n_seq, l, h1, next_norm)
        saved.append(sv)
    loss, dx, dxb, g_final = _loss_head(xt, final_norm, target.reshape(T, Dm), "loss_head")
    grads = [None] * DEPTH
    for l in reversed(range(DEPTH)):
        dx, dxb, grads[l] = _layer_bwd(dx, dxb, Ws[l], saved[l], n_seq, l, stage)
    return loss, dx.reshape(n_seq, S, Dm), grads, g_final


def _padded_shards(weights):
    pads = {"w_in": IN_SHARD_PAD - IN_SHARD, "w_gate_up": GU_SHARD_PAD - GU_SHARD}
    return {n: jnp.pad(weights[n], ((0, 0), (0, 0), (0, pads.get(n, 0)))).astype(BF16) for n in SHARDED}


def _full_operands(g, l):
    W = {}
    if "w_in" in g:
        W["w_main"], W["w_f"] = _w_in_full(g["w_in"], f"w_in_full_l{l}")
    if "w_gate_up" in g:
        W["w_gate_up"] = _w_gu_full(g["w_gate_up"], f"w_gate_up_full_l{l}")
    for n in ("w_proj_attn", "w_proj_pool", "w_proj_conv"):
        if n in g:
            W[n] = jnp.transpose(g[n], (1, 0, 2)).reshape(BRANCH_W, D_MODEL)
    if "w_out" in g:
        W["w_out"] = g["w_out"].reshape(D_MODEL, D_MODEL)
    if "w_down" in g:
        W["w_down"] = g["w_down"].reshape(FFN_HIDDEN, D_MODEL)
    return W


GRAD_GROUPS = {"ffn": ("w_down", "w_gate_up"),
               "mix": ("w_out", "w_proj_attn", "w_proj_pool", "w_proj_conv"),
               "w_in": ("w_in",)}


def _grad_slabs(G, n, l):
    if n == "w_in":
        return _w_in_slabs(G["w_main"], G["w_f"], f"w_in_slabs_l{l}")
    if n == "w_gate_up":
        return _w_gu_slabs(G["w_gate_up"], f"w_gate_up_slabs_l{l}")
    if n == "w_out":
        return G["w_out"].reshape(N_DEV, D_MODEL // N_DEV, D_MODEL)
    if n == "w_down":
        return G["w_down"].reshape(N_DEV, FFN_HIDDEN // N_DEV, D_MODEL)
    return jnp.transpose(G[n].reshape(BRANCH_W, N_DEV, D_MODEL // N_DEV), (1, 0, 2))


def _sum_layer_grads(recv, l):
    out = {n: _sum_slabs(r, f"sum_{n}_l{l}") for n, r in recv.items()}
    if "w_in" in out:
        out["w_in"] = out["w_in"][:, :IN_SHARD]
    if "w_gate_up" in out:
        out["w_gate_up"] = out["w_gate_up"][:, :GU_SHARD]
    return out


def _sum_small(xs, name):
    def body(*refs):
        for x_ref, o_ref in zip(refs[:len(xs)], refs[len(xs):]):
            acc = x_ref[0]
            for j in range(1, N_DEV):
                acc = acc + x_ref[j]
            o_ref[...] = acc

    return pl.pallas_call(
        body, name=name, out_shape=[jax.ShapeDtypeStruct(x.shape[1:], F32) for x in xs],
        compiler_params=_cp(),
    )(*xs)


def _as_2d(a):
    if a.ndim == 1:
        return a.reshape(1, -1)
    return a.reshape(-1, a.shape[-1])


def kernel(x, attn_norm, w_in, b_forget, b_gate, w_proj_attn, pool_w, pool_scale, w_proj_pool, conv_w, w_proj_conv, w_out, ffn_norm, w_gate_up, w_down, final_norm, loss_target, m_attn_norm, m_w_in, m_b_forget, m_b_gate, m_w_proj_attn, m_pool_w, m_pool_scale, m_w_proj_pool, m_conv_w, m_w_proj_conv, m_w_out, m_ffn_norm, m_w_gate_up, m_w_down, m_final_norm, v_attn_norm, v_w_in, v_b_forget, v_b_gate, v_w_proj_attn, v_pool_w, v_pool_scale, v_w_proj_pool, v_conv_w, v_w_proj_conv, v_w_out, v_ffn_norm, v_w_gate_up, v_w_down, v_final_norm):
    weights = dict(attn_norm=attn_norm, w_in=w_in, b_forget=b_forget, b_gate=b_gate, w_proj_attn=w_proj_attn,
                   pool_w=pool_w, pool_scale=pool_scale, w_proj_pool=w_proj_pool, conv_w=conv_w,
                   w_proj_conv=w_proj_conv, w_out=w_out, ffn_norm=ffn_norm, w_gate_up=w_gate_up, w_down=w_down,
                   final_norm=final_norm)
    moments_m = dict(attn_norm=m_attn_norm, w_in=m_w_in, b_forget=m_b_forget, b_gate=m_b_gate,
                     w_proj_attn=m_w_proj_attn, pool_w=m_pool_w, pool_scale=m_pool_scale, w_proj_pool=m_w_proj_pool,
                     conv_w=m_conv_w, w_proj_conv=m_w_proj_conv, w_out=m_w_out, ffn_norm=m_ffn_norm,
                     w_gate_up=m_w_gate_up, w_down=m_w_down, final_norm=m_final_norm)
    moments_v = dict(attn_norm=v_attn_norm, w_in=v_w_in, b_forget=v_b_forget, b_gate=v_b_gate,
                     w_proj_attn=v_w_proj_attn, pool_w=v_pool_w, pool_scale=v_pool_scale, w_proj_pool=v_w_proj_pool,
                     conv_w=v_conv_w, w_proj_conv=v_w_proj_conv, w_out=v_w_out, ffn_norm=v_ffn_norm,
                     w_gate_up=v_w_gate_up, w_down=v_w_down, final_norm=v_final_norm)

    sh = _padded_shards(weights)
    names = list(SHARDED)
    rest = [n for n in names if n != "w_in"]
    me = 4 * lax.axis_index("x") + 2 * lax.axis_index("y") + lax.axis_index("c")
    w_in0, conv_all = _multi_gather([sh["w_in"], conv_w], [0, None], "gather_w_in_l0")
    started, after = {}, w_in0
    for l in range(DEPTH):
        for group, gnames in (("w_in", ["w_in"]), ("rest", rest)):
            if (l, group) != (0, "w_in"):
                started[l, group] = _split_start([sh[n] for n in gnames], [l] * len(gnames), False, after,
                                                 f"gather_start_{group}_l{l}")
                after = started[l, group][4]
    last_token = after

    def get_W(l, xt):
        if l == 0:
            w_in = w_in0
        else:
            w_in = _split_wait(started[l, "w_in"], [l], False, xt, f"gather_wait_w_in_l{l}")[0]
        W = _full_operands({"w_in": w_in}, l)

        def late(after):
            lands = _split_wait(started[l, "rest"], [l] * len(rest), False, after, f"gather_wait_rest_l{l}")
            return _full_operands(dict(zip(rest, lands)), l)

        W["late"] = late
        W.update(_replicated_operands(weights, l))
        W["conv_w"] = jnp.transpose(conv_all[:, l], (1, 0, 2)).reshape(CONV_K, BRANCH_W)
        if l == 0:
            W["attn_norm"] = W["attn_norm"] + last_token[0, 0]
        return W

    exchanges = []

    def stage(l, group, G, W):
        gnames = GRAD_GROUPS[group]
        slabs = [_grad_slabs(G, n, l) for n in gnames]
        started = _split_start(slabs, None, True, slabs[0], f"exchange_start_{group}_l{l}")
        exchanges.append((l, group, gnames, slabs, started))
        tie = {"ffn": "ffn_norm", "mix": "conv_w", "w_in": "w_f"}[group]
        W = dict(W)
        W[tie] = W[tie] + started[4][0, 0].astype(W[tie].dtype)
        return W

    loss_part, grad_x, grads, g_final = _local_step(x, loss_target, get_W, attn_norm, final_norm, stage)
    after = grad_x
    for l, group, gnames, slabs, started in exchanges:
        lands = _split_wait(started, None, True, after, f"exchange_wait_{group}_l{l}")
        grads[l].update(_sum_layer_grads(dict(zip(gnames, lands)), l))
    gw = {n: jnp.stack([grads[l][n] for l in range(DEPTH)]) for n in SHARDED}

    small = ("attn_norm", "b_forget", "b_gate", "pool_w", "pool_scale", "ffn_norm", "conv_w")
    parts = [jnp.stack([grads[l][n] for l in range(DEPTH)]) for n in small] + [g_final, loss_part]
    gathered = _multi_gather(parts, [None] * len(parts), "gather_small_grads")
    summed = _sum_small(gathered, "sum_small_grads")
    for n, s in zip(small, summed):
        gw[n] = s
    gw["attn_norm"], gw["ffn_norm"] = gw["attn_norm"][:, 0], gw["ffn_norm"][:, 0]
    gw["b_forget"] = gw["b_forget"][:, 0, :N_HEADS]
    gw["b_gate"], gw["pool_scale"] = gw["b_gate"][:, 0], gw["pool_scale"][:, 0]
    gw["conv_w"] = lax.dynamic_slice_in_dim(gw["conv_w"], me * (BRANCH_W // N_DEV), BRANCH_W // N_DEV, axis=2)
    gw["final_norm"] = summed[-2][0]
    loss = summed[-1][0, 0]

    deltas, new_m, new_v = {}, {}, {}
    for n in SHARDED:
        shape = weights[n].shape
        d, nm, nv = _adamw(_as_2d(weights[n]), _as_2d(gw[n]), _as_2d(moments_m[n]), _as_2d(moments_v[n]),
                           "adamw_" + n)
        deltas[n], new_m[n], new_v[n] = d.reshape(shape), nm.reshape(shape), nv.reshape(shape)
    rest_names = [n for n in WEIGHT_ORDER if n not in SHARDED]
    ds, nms, nvs = _adamw_many(*[[_as_2d(src[n]) for n in rest_names] for src in (weights, gw, moments_m, moments_v)],
                               "adamw_small")
    for n, d, nm, nv in zip(rest_names, ds, nms, nvs):
        shape = weights[n].shape
        deltas[n], new_m[n], new_v[n] = d.reshape(shape), nm.reshape(shape), nv.reshape(shape)

    return (loss, grad_x, *[gw[n] for n in WEIGHT_ORDER], *[deltas[n] for n in WEIGHT_ORDER],
            *[new_m[n] for n in WEIGHT_ORDER], *[new_v[n] for n in WEIGHT_ORDER])
```

```python
import functools

import jax
import jax.numpy as jnp
from jax import lax
from jax.experimental import pallas as pl
from jax.experimental.pallas import tpu as pltpu

F32 = jnp.float32
BF16 = jnp.bfloat16

N_DEV = 8
D_MODEL = 1024
DEPTH = 2
N_HEADS = 8
HEAD_DIM = 64
BRANCH_W = 512
POOL_WINDOWS = (2, 4, 8, 16)
POOL_GD = 128
CONV_K = 3
FFN_HIDDEN = 2816
GATE_W = 3 * D_MODEL
IN_COLS = 6664
MAIN_COLS = GATE_W + 7 * BRANCH_W
RMS_EPS = 1e-6
NEG_INF = -1e30

ADAM_LR = 0.001
ADAM_B1 = 0.9
ADAM_B2 = 0.999
ADAM_EPS = 1e-08
ADAM_WD = 0.01
ADAM_STEP = 10

LANES = 128
VMEM_LIMIT = 56 * 1024 * 1024
CUM_BLK = 256

TRIPLE = 3 * LANES
OFF_G, OFF_QKV, OFF_CONV, OFF_U = 0, 3072, 4608, 6144


def _cp(sem=None):
    return pltpu.CompilerParams(dimension_semantics=sem, vmem_limit_bytes=VMEM_LIMIT)


def _sigmoid(z):
    return 1.0 / (1.0 + jnp.exp(-z))


def _matmul(a, b, *, mode, out_dtype, name, tm=2048, tn=512, tk=None, residual=None, rms_g=None):
    if mode == "nn":
        (M, K), N = a.shape, b.shape[1]
    elif mode == "nt":
        (M, K), N = a.shape, b.shape[0]
    else:
        (K, M), N = a.shape, b.shape[1]
    tm, tn, tk = min(tm, M), min(tn, N), K if tk is None else min(tk, K)
    assert M % tm == 0 and N % tn == 0 and K % tk == 0, (name, M, N, K, tm, tn, tk)
    nk = K // tk
    if mode == "nn":
        a_spec = pl.BlockSpec((tm, tk), lambda i, j, k: (i, k))
        b_spec = pl.BlockSpec((tk, tn), lambda i, j, k: (k, j))
        dims = (((1,), (0,)), ((), ()))
    elif mode == "nt":
        a_spec = pl.BlockSpec((tm, tk), lambda i, j, k: (i, k))
        b_spec = pl.BlockSpec((tn, tk), lambda i, j, k: (j, k))
        dims = (((1,), (1,)), ((), ()))
    else:
        a_spec = pl.BlockSpec((tk, tm), lambda i, j, k: (k, i))
        b_spec = pl.BlockSpec((tk, tn), lambda i, j, k: (k, j))
        dims = (((0,), (0,)), ((), ()))
    o_spec = pl.BlockSpec((tm, tn), lambda i, j, k: (i, j))
    has_res, has_norm = residual is not None, rms_g is not None
    assert not has_norm or tn == N, (name, tn, N)

    def body(*refs):
        a_ref, b_ref = refs[:2]
        r_ref = refs[2] if has_res else None
        g_ref = refs[2 + has_res] if has_norm else None
        o_ref = refs[2 + has_res + has_norm]
        h_ref = refs[3 + has_res + has_norm] if has_norm else None

        def finish(acc):
            if has_res:
                acc = acc + r_ref[...].astype(F32)
            o_ref[...] = acc.astype(out_dtype)
            if has_norm:
                r = lax.rsqrt(jnp.mean(acc * acc, axis=-1, keepdims=True) + RMS_EPS)
                h_ref[...] = ((acc * r) * g_ref[...]).astype(BF16)

        prod = lax.dot_general(a_ref[...], b_ref[...], dims, preferred_element_type=F32)
        if nk == 1:
            finish(prod)
            return
        acc_ref = refs[-1]
        k = pl.program_id(2)

        @pl.when(k == 0)
        def _():
            acc_ref[...] = prod

        @pl.when(jnp.logical_and(k > 0, k < nk - 1))
        def _():
            acc_ref[...] += prod

        @pl.when(k == nk - 1)
        def _():
            finish(acc_ref[...] + prod)

    in_specs = [a_spec, b_spec] + ([o_spec] if has_res else [])
    args = (a, b) + ((residual,) if has_res else ())
    out_specs, out_shape = o_spec, jax.ShapeDtypeStruct((M, N), out_dtype)
    if has_norm:
        in_specs.append(pl.BlockSpec((1, N), lambda i, j, k: (0, 0)))
        args += (rms_g.reshape(1, N),)
        out_specs, out_shape = [o_spec, o_spec], [out_shape, jax.ShapeDtypeStruct((M, N), BF16)]
    return pl.pallas_call(
        body, name=name, grid=(M // tm, N // tn, nk), in_specs=in_specs, out_specs=out_specs,
        out_shape=out_shape,
        scratch_shapes=[pltpu.VMEM((tm, tn), F32)] if nk > 1 else [],
        compiler_params=_cp(("parallel", "parallel", "arbitrary")),
    )(*args)


def _rms_fwd(x, g, name):
    T, Dm = x.shape
    tm = min(512, T)

    def body(x_ref, g_ref, h_ref):
        xf = x_ref[...]
        r = lax.rsqrt(jnp.mean(xf * xf, axis=-1, keepdims=True) + RMS_EPS)
        h_ref[...] = ((xf * r) * g_ref[...]).astype(BF16)

    return pl.pallas_call(
        body, name=name, grid=(T // tm,),
        in_specs=[pl.BlockSpec((tm, Dm), lambda i: (i, 0)), pl.BlockSpec((1, Dm), lambda i: (0, 0))],
        out_specs=pl.BlockSpec((tm, Dm), lambda i: (i, 0)),
        out_shape=jax.ShapeDtypeStruct((T, Dm), BF16),
        compiler_params=_cp(("parallel",)),
    )(x, g.reshape(1, Dm))


def _rms_bwd(x, g, dh, dres, name):
    T, Dm = x.shape
    tm = min(512, T)

    def body(x_ref, g_ref, dh_ref, dres_ref, dx_ref, dxb_ref, dg_ref):
        i = pl.program_id(0)
        xf = x_ref[...]
        r = lax.rsqrt(jnp.mean(xf * xf, axis=-1, keepdims=True) + RMS_EPS)
        xn = xf * r
        dhf = dh_ref[...].astype(F32)
        dxn = dhf * g_ref[...]
        c = jnp.mean(dxn * xn, axis=-1, keepdims=True)
        dx = dres_ref[...] + r * (dxn - xn * c)
        dx_ref[...] = dx
        dxb_ref[...] = dx.astype(BF16)
        part = jnp.sum(dhf * xn, axis=0, keepdims=True)

        @pl.when(i == 0)
        def _():
            dg_ref[...] = part

        @pl.when(i > 0)
        def _():
            dg_ref[...] += part

    row = pl.BlockSpec((tm, Dm), lambda i: (i, 0))
    vec = pl.BlockSpec((1, Dm), lambda i: (0, 0))
    return pl.pallas_call(
        body, name=name, grid=(T // tm,), in_specs=[row, vec, row, row], out_specs=[row, row, vec],
        out_shape=[jax.ShapeDtypeStruct((T, Dm), F32), jax.ShapeDtypeStruct((T, Dm), BF16),
                   jax.ShapeDtypeStruct((1, Dm), F32)],
        compiler_params=_cp(("arbitrary",)),
    )(x, g.reshape(1, Dm), dh, dres)


def _loss_head(x, g, target, name):
    T, Dm = x.shape
    tm = min(512, T)

    def body(x_ref, g_ref, t_ref, loss_ref, dx_ref, dxb_ref, dg_ref):
        i = pl.program_id(0)
        xf = x_ref[...]
        gv = g_ref[...]
        r = lax.rsqrt(jnp.mean(xf * xf, axis=-1, keepdims=True) + RMS_EPS)
        xn = xf * r
        diff = xn * gv - t_ref[...]
        per_tok = jnp.mean(diff * diff, axis=-1, keepdims=True)
        lpart = 0.5 * jnp.sum(per_tok, axis=0, keepdims=True) + jnp.zeros((1, LANES), F32)
        dy = diff * (1.0 / Dm)
        dxn = dy * gv
        c = jnp.mean(dxn * xn, axis=-1, keepdims=True)
        dx = r * (dxn - xn * c)
        dx_ref[...] = dx
        dxb_ref[...] = dx.astype(BF16)
        part = jnp.sum(dy * xn, axis=0, keepdims=True)

        @pl.when(i == 0)
        def _():
            dg_ref[...] = part
            loss_ref[...] = lpart

        @pl.when(i > 0)
        def _():
            dg_ref[...] += part
            loss_ref[...] += lpart

    row = pl.BlockSpec((tm, Dm), lambda i: (i, 0))
    vec = pl.BlockSpec((1, Dm), lambda i: (0, 0))
    lsp = pl.BlockSpec((1, LANES), lambda i: (0, 0))
    return pl.pallas_call(
        body, name=name, grid=(T // tm,), in_specs=[row, vec, row], out_specs=[lsp, row, row, vec],
        out_shape=[jax.ShapeDtypeStruct((1, LANES), F32), jax.ShapeDtypeStruct((T, Dm), F32),
                   jax.ShapeDtypeStruct((T, Dm), BF16), jax.ShapeDtypeStruct((1, Dm), F32)],
        compiler_params=_cp(("arbitrary",)),
    )(x, g.reshape(1, Dm), target)


def _split_bf16(v):
    hi = v.astype(BF16)
    r1 = v - hi.astype(F32)
    mid = r1.astype(BF16)
    lo = (r1 - mid.astype(F32)).astype(BF16)
    return hi, mid, lo


def _tri_dot(tri, v):
    hi, mid, lo = _split_bf16(v)
    dot = functools.partial(jnp.dot, preferred_element_type=F32)
    return dot(tri, hi) + dot(tri, mid) + dot(tri, lo)


def _log_sigmoid(z):
    return jnp.minimum(z, 0.0) - jnp.log(1.0 + jnp.exp(-jnp.abs(z)))


def _fox_cumsum_bwd(f, bf, dF, n_seq, name):
    T = f.shape[0]
    S = T // n_seq
    c = min(CUM_BLK, S)

    def body(f_ref, b_ref, dF_ref, df_ref, db_ref):
        b = pl.program_id(0)
        ri = lax.broadcasted_iota(jnp.int32, (c, c), 0)
        ci = lax.broadcasted_iota(jnp.int32, (c, c), 1)
        tri = (ri <= ci).astype(BF16)
        carry = jnp.zeros((1, LANES), F32)
        dbp = jnp.zeros((1, LANES), F32)
        for j in reversed(range(S // c)):
            dFc = dF_ref[j * c:(j + 1) * c, :]
            dlf = _tri_dot(tri, dFc) + carry
            carry = carry + jnp.sum(dFc, axis=0, keepdims=True)
            z = f_ref[j * c:(j + 1) * c, :] + b_ref[...]
            dz = dlf * _sigmoid(-z)
            df_ref[j * c:(j + 1) * c, :] = dz.astype(BF16)
            dbp = dbp + jnp.sum(dz, axis=0, keepdims=True)

        @pl.when(b == 0)
        def _():
            db_ref[...] = dbp

        @pl.when(b > 0)
        def _():
            db_ref[...] += dbp

    blk = pl.BlockSpec((S, LANES), lambda b: (b, 0))
    vec = pl.BlockSpec((1, LANES), lambda b: (0, 0))
    return pl.pallas_call(
        body, name=name, grid=(n_seq,), in_specs=[blk, vec, blk], out_specs=[blk, vec],
        out_shape=[jax.ShapeDtypeStruct((T, LANES), BF16), jax.ShapeDtypeStruct((1, LANES), F32)],
        compiler_params=_cp(("arbitrary",)),
    )(f, bf, dF)


def _pair_masks():
    lane = lax.broadcasted_iota(jnp.int32, (1, LANES), 1)
    lo = lane < HEAD_DIM
    return lo, jnp.logical_not(lo)


AUG0 = HEAD_DIM
Q_TILE, K_CHUNK, ROW_GROUP = 512, 256, 64


def _fox_prep(f, bf, proj, n_seq, name):
    T = f.shape[0]
    S = T // n_seq
    c = min(CUM_BLK, S)

    def body(f_ref, b_ref, qkv_ref, qa_ref, ka_ref, va_ref):
        ri = lax.broadcasted_iota(jnp.int32, (c, c), 0)
        ci = lax.broadcasted_iota(jnp.int32, (c, c), 1)
        tri = (ri >= ci).astype(BF16)
        lane = lax.broadcasted_iota(jnp.int32, (c, LANES), 1)
        carry = jnp.zeros((1, LANES), F32)
        for j in range(S // c):
            rows = slice(j * c, (j + 1) * c)
            lf = _log_sigmoid(f_ref[rows, :] + b_ref[...])
            Fc = _tri_dot(tri, lf) + carry
            carry = carry + jnp.sum(lf, axis=0, keepdims=True)
            for h in range(N_HEADS):
                col = jnp.sum(jnp.where(lane == h, Fc, 0.0), axis=-1, keepdims=True)
                hi = col.astype(BF16).astype(F32)
                r1 = col - hi
                mid = r1.astype(BF16).astype(F32)
                lo = r1 - mid
                ones_q = jnp.logical_and(lane >= AUG0 + 3, lane < AUG0 + 6)
                ones_k = jnp.logical_and(lane >= AUG0, lane < AUG0 + 3)
                aug_q = jnp.where(lane == AUG0, hi, jnp.where(lane == AUG0 + 1, mid, jnp.where(
                    lane == AUG0 + 2, lo, jnp.where(ones_q, 1.0, 0.0))))
                aug_k = jnp.where(lane == AUG0 + 3, -hi, jnp.where(lane == AUG0 + 4, -mid, jnp.where(
                    lane == AUG0 + 5, -lo, jnp.where(ones_k, 1.0, 0.0))))
                base = (h // 2) * TRIPLE
                qp, kp, vp = (qkv_ref[rows, base + t * LANES:base + (t + 1) * LANES].astype(F32) for t in range(3))
                if h % 2:
                    qp, kp, vp = (pltpu.roll(a, HEAD_DIM, 1) for a in (qp, kp, vp))
                out = slice(h * LANES, (h + 1) * LANES)
                qa_ref[rows, out] = jnp.where(lane < HEAD_DIM, qp * (HEAD_DIM ** -0.5), aug_q).astype(BF16)
                ka_ref[rows, out] = jnp.where(lane < HEAD_DIM, kp, aug_k).astype(BF16)
                va_ref[rows, out] = jnp.where(lane < HEAD_DIM, vp, jnp.where(lane == AUG0, 1.0, 0.0)).astype(BF16)

    fblk = pl.BlockSpec((S, LANES), lambda b: (b, 0))
    out = pl.BlockSpec((S, N_HEADS * LANES), lambda b: (b, 0))
    sh = jax.ShapeDtypeStruct((T, N_HEADS * LANES), BF16)
    return pl.pallas_call(
        body, name=name, grid=(n_seq,),
        in_specs=[fblk, pl.BlockSpec((1, LANES), lambda b: (0, 0)),
                  pl.BlockSpec((S, 4 * TRIPLE), lambda b: (b, OFF_QKV // (4 * TRIPLE)))],
        out_specs=[out, out, out], out_shape=[sh, sh, sh],
        compiler_params=_cp(("parallel",)),
    )(f, bf, proj)


def _band_mask(q0, k0, nq, nk):
    row = q0 + lax.broadcasted_iota(jnp.int32, (nq, nk), 0)
    col = k0 + lax.broadcasted_iota(jnp.int32, (nq, nk), 1)
    return col <= row


_NT = (((1,), (1,)), ((), ()))
_TN = (((0,), (0,)), ((), ()))


def _attn_fwd2(qa, ka, va, n_seq, name):
    T = qa.shape[0]
    S = T // n_seq
    tq, tk, rg = min(Q_TILE, S), min(K_CHUNK, S), ROW_GROUP
    nq, per = S // tq, tq // tk

    def body(q_ref, k_ref, v_ref, o_ref, o32_ref, lse_ref, phi_s, plo_s, mp_s, m_s, acc_s):
        qi = pl.program_id(2)
        mp_s[...] = jnp.full_like(mp_s, NEG_INF)
        acc_s[...] = jnp.zeros_like(acc_s)

        def scores(kc, hh, r0):
            k0 = pl.multiple_of(kc * tk, tk)
            hl = slice(hh * LANES, (hh + 1) * LANES)
            return k0, lax.dot_general(q_ref[r0:, hl], k_ref[pl.ds(k0, tk), hl], _NT, preferred_element_type=F32)

        def max_chunk(kc, masked, r0):
            for hh in range(2):
                k0, s_all = scores(kc, hh, r0)
                for r in range(r0 // rg, tq // rg):
                    rows = slice(r * rg, (r + 1) * rg)
                    s = s_all[r * rg - r0:(r + 1) * rg - r0, :]
                    if masked:
                        s = jnp.where(_band_mask(qi * tq + r * rg, k0, rg, tk), s, NEG_INF)
                    part = s[:, :LANES]
                    for c in range(1, tk // LANES):
                        part = jnp.maximum(part, s[:, c * LANES:(c + 1) * LANES])
                    mp_s[hh, rows, :] = jnp.maximum(mp_s[hh, rows, :], part)

        def sum_chunk(kc, masked, r0):
            for hh in range(2):
                k0, s_all = scores(kc, hh, r0)
                hl = slice(hh * LANES, (hh + 1) * LANES)
                v = v_ref[pl.ds(k0, tk), hl]
                for r in range(r0 // rg, tq // rg):
                    rows = slice(r * rg, (r + 1) * rg)
                    p = jnp.exp(s_all[r * rg - r0:(r + 1) * rg - r0, :] - m_s[hh, rows])
                    if masked:
                        p = jnp.where(_band_mask(qi * tq + r * rg, k0, rg, tk), p, 0.0)
                    p_hi = p.astype(BF16)
                    phi_s[hh, rows, :] = p_hi
                    plo_s[hh, rows, :] = (p - p_hi.astype(F32)).astype(BF16)
                acc_s[hh, r0:, :] += (jnp.dot(phi_s[hh, r0:, :], v, preferred_element_type=F32)
                                      + jnp.dot(plo_s[hh, r0:, :], v, preferred_element_type=F32))

        def sweep(chunk):
            def unmasked(kc, carry):
                chunk(kc, False, 0)
                return carry

            lax.fori_loop(0, qi * per, unmasked, 0)
            for d in range(per):
                chunk(qi * per + d, True, d * tk)

        sweep(max_chunk)
        m_s[...] = jnp.max(mp_s[...], axis=-1, keepdims=True)
        sweep(sum_chunk)

        lane = lax.broadcasted_iota(jnp.int32, (1, LANES), 1)
        outs = []
        for hh in range(2):
            acc = acc_s[hh]
            l = jnp.sum(jnp.where(lane == AUG0, acc, 0.0), axis=-1, keepdims=True)
            lse_ref[hh] = m_s[hh] + jnp.log(l)
            outs.append(acc / l)
        o = jnp.where(lane < HEAD_DIM, outs[0], pltpu.roll(outs[1], HEAD_DIM, 1))
        o_ref[...] = o.astype(BF16)
        o32_ref[...] = o

    qmap = lambda b, j, qi: (b * nq + qi, j)
    omap = lambda b, j, qi: (b * nq + qi, j)
    kv = pl.BlockSpec((S, 2 * LANES), lambda b, j, qi: (b, j))
    return pl.pallas_call(
        body, name=name, grid=(n_seq, N_HEADS // 2, nq),
        in_specs=[pl.BlockSpec((tq, 2 * LANES), qmap), kv, kv],
        out_specs=[pl.BlockSpec((tq, LANES), omap), pl.BlockSpec((tq, LANES), omap),
                   pl.BlockSpec((2, tq, 1), lambda b, j, qi: (j, b * nq + qi, 0))],
        out_shape=[jax.ShapeDtypeStruct((T, BRANCH_W), BF16), jax.ShapeDtypeStruct((T, BRANCH_W), F32),
                   jax.ShapeDtypeStruct((N_HEADS, T, 1), F32)],
        scratch_shapes=[pltpu.VMEM((2, tq, tk), BF16), pltpu.VMEM((2, tq, tk), BF16),
                        pltpu.VMEM((2, tq, LANES), F32), pltpu.VMEM((2, tq, 1), F32),
                        pltpu.VMEM((2, tq, LANES), F32)],
        compiler_params=_cp(("parallel", "parallel", "parallel")),
    )(qa, ka, va)


def _attn_bwd(qa, ka, proj, do, o32, lse, dproj, n_seq, name):
    T = qa.shape[0]
    S = T // n_seq
    tq, tk, rg = min(Q_TILE, S), min(K_CHUNK, S), ROW_GROUP
    nq, per, nkc = S // tq, tq // tk, S // tk

    def body(q_ref, k_ref, v_ref, do_ref, o_ref, lse_ref, _, dqkv_ref, dfk_ref,
             p_s, ds_s, dq_s, dk_s, dv_s, df_s):
        dk_s[...] = jnp.zeros_like(dk_s)
        dv_s[...] = jnp.zeros_like(dv_s)
        df_s[...] = jnp.zeros_like(df_s)
        sels = _pair_masks()

        for qi in range(nq):
            q0 = qi * tq
            do_t = do_ref[q0:q0 + tq, :]
            dq_s[...] = jnp.zeros_like(dq_s)
            prod = do_t.astype(F32) * o_ref[q0:q0 + tq, :]
            dls = [jnp.sum(jnp.where(sel, prod, 0.0), axis=-1, keepdims=True) for sel in sels]

            def chunk(kc, masked, r0, q0=q0, do_t=do_t, dls=dls):
                k0 = pl.multiple_of(kc * tk, tk)
                v = v_ref[pl.ds(k0, tk), :]
                do_a = do_t[r0:, :]
                for hh in range(2):
                    hl = slice(hh * LANES, (hh + 1) * LANES)
                    qh, kh = q_ref[q0 + r0:q0 + tq, hl], k_ref[pl.ds(k0, tk), hl]
                    s_all = lax.dot_general(qh, kh, _NT, preferred_element_type=F32)
                    dom = jnp.where(sels[hh], do_a, jnp.zeros_like(do_a))
                    dp_all = lax.dot_general(dom, v, _NT, preferred_element_type=F32)
                    dfp = jnp.zeros((1, tk), F32)
                    for r in range(r0 // rg, tq // rg):
                        rows = slice(r * rg, (r + 1) * rg)
                        arows = slice(r * rg - r0, (r + 1) * rg - r0)
                        qrows = slice(q0 + r * rg, q0 + (r + 1) * rg)
                        p = jnp.exp(s_all[arows, :] - lse_ref[hh, qrows])
                        if masked:
                            p = jnp.where(_band_mask(q0 + r * rg, k0, rg, tk), p, 0.0)
                        ds = p * (dp_all[arows, :] - dls[hh][rows])
                        p_s[hh, rows, :] = p.astype(BF16)
                        ds_s[hh, rows, :] = ds.astype(BF16)
                        dfp = dfp + jnp.sum(ds, axis=0, keepdims=True)
                    df_s[hh, kc] -= dfp
                    dq_s[hh, r0:, :] += jnp.dot(ds_s[hh, r0:, :], kh, preferred_element_type=F32)
                    dv_s[hh, pl.ds(k0, tk), :] += lax.dot_general(p_s[hh, r0:, :], do_a, _TN,
                                                                  preferred_element_type=F32)
                    dk_s[hh, pl.ds(k0, tk), :] += lax.dot_general(ds_s[hh, r0:, :], qh, _TN,
                                                                  preferred_element_type=F32)

            def unmasked(kc, carry, chunk=chunk):
                chunk(kc, False, 0)
                return carry

            lax.fori_loop(0, qi * per, unmasked, 0)
            for d in range(per):
                chunk(qi * per + d, True, d * tk)
            dq = jnp.where(sels[0], dq_s[0], pltpu.roll(dq_s[1], HEAD_DIM, 1))
            dqkv_ref[q0:q0 + tq, :LANES] = (dq * (HEAD_DIM ** -0.5)).astype(BF16)

        dqkv_ref[:, LANES:2 * LANES] = jnp.where(sels[0], dk_s[0], pltpu.roll(dk_s[1], HEAD_DIM, 1)).astype(BF16)
        dqkv_ref[:, 2 * LANES:] = jnp.where(sels[0], dv_s[0], dv_s[1]).astype(BF16)
        for c in range(nkc):
            dfk_ref[:, :, c * tk:(c + 1) * tk] = df_s[:, c]

    seq = lambda w: pl.BlockSpec((S, w), lambda b, j: (b, j))
    col1 = pl.BlockSpec((2, S, 1), lambda b, j: (j, b, 0))
    vblk = pl.BlockSpec((S, LANES), lambda b, j: (b, OFF_QKV // LANES + 3 * j + 2))
    return pl.pallas_call(
        body, name=name, grid=(n_seq, N_HEADS // 2),
        in_specs=[seq(2 * LANES), seq(2 * LANES), vblk, seq(LANES), seq(LANES), col1,
                  pl.BlockSpec(memory_space=pl.ANY)],
        out_specs=[pl.BlockSpec((S, TRIPLE), lambda b, j: (b, OFF_QKV // TRIPLE + j)),
                   pl.BlockSpec((2, 1, S), lambda b, j: (j, 0, b))],
        out_shape=[jax.ShapeDtypeStruct(dproj.shape, BF16), jax.ShapeDtypeStruct((N_HEADS, 1, T), F32)],
        input_output_aliases={6: 0},
        scratch_shapes=[pltpu.VMEM((2, tq, tk), BF16), pltpu.VMEM((2, tq, tk), BF16),
                        pltpu.VMEM((2, tq, LANES), F32), pltpu.VMEM((2, S, LANES), F32),
                        pltpu.VMEM((2, S, LANES), F32), pltpu.VMEM((2, nkc, 1, tk), F32)],
        compiler_params=_cp(("parallel", "parallel")),
    )(qa, ka, proj, do, o32, lse, dproj)


def _shift_down(v, k, row):
    return jnp.where(row >= k, pltpu.roll(v, k, 0), 0.0)


def _shift_up(v, k, row, S):
    return jnp.where(row < S - k, pltpu.roll(v, S - k, 0), 0.0)


def _pool_diff(uf, w, row):
    acc, k = uf, 1
    while k < w:
        acc = acc + _shift_down(acc, k, row)
        k *= 2
    n = jnp.minimum(row + 1, w).astype(F32)
    return acc / n - uf


def _pool_fwd(proj, pool_w, pool_scale, n_seq, name):
    T = proj.shape[0]
    S = T // n_seq

    def body(u_ref, w_ref, sc_ref, o_ref, d_s):
        g = pl.program_id(1)
        row = lax.broadcasted_iota(jnp.int32, (S, POOL_GD), 0)
        uf = u_ref[...].astype(F32)
        for gi, wlen in enumerate(POOL_WINDOWS):
            @pl.when(g == gi)
            def _(wlen=wlen):
                d_s[...] = _pool_diff(uf, wlen, row).astype(BF16)
        e = jnp.dot(d_s[...], w_ref[0], preferred_element_type=F32)
        o_ref[...] = (e * sc_ref[...]).astype(BF16)

    uc = OFF_U // POOL_GD
    return pl.pallas_call(
        body, name=name, grid=(n_seq, len(POOL_WINDOWS)),
        in_specs=[pl.BlockSpec((S, POOL_GD), lambda b, g: (b, uc + g)),
                  pl.BlockSpec((1, POOL_GD, POOL_GD), lambda b, g: (g, 0, 0)),
                  pl.BlockSpec((1, POOL_GD), lambda b, g: (0, g))],
        out_specs=pl.BlockSpec((S, POOL_GD), lambda b, g: (b, g)),
        out_shape=jax.ShapeDtypeStruct((T, BRANCH_W), BF16),
        scratch_shapes=[pltpu.VMEM((S, POOL_GD), BF16)],
        compiler_params=_cp(("parallel", "parallel")),
    )(proj, pool_w, pool_scale)


def _pool_bwd(proj, dout, pool_w, pool_scale, dproj, n_seq, name):
    T = proj.shape[0]
    S = T // n_seq

    def body(u_ref, do_ref, w_ref, sc_ref, _, du_ref, dw_ref, dsc_ref, d_s):
        g, b = pl.program_id(0), pl.program_id(1)
        row = lax.broadcasted_iota(jnp.int32, (S, POOL_GD), 0)
        uf = u_ref[...].astype(F32)
        for gi, wlen in enumerate(POOL_WINDOWS):
            @pl.when(g == gi)
            def _(wlen=wlen):
                d_s[...] = _pool_diff(uf, wlen, row).astype(BF16)
        db16 = d_s[...]
        w = w_ref[0]
        e = jnp.dot(db16, w, preferred_element_type=F32)
        dof = do_ref[...].astype(F32)
        dsc = jnp.sum(dof * e, axis=0, keepdims=True)
        de = (dof * sc_ref[...]).astype(BF16)
        dd = lax.dot_general(de, w, (((1,), (1,)), ((), ())), preferred_element_type=F32)
        dw = lax.dot_general(db16, de, (((0,), (0,)), ((), ())), preferred_element_type=F32)
        for gi, wlen in enumerate(POOL_WINDOWS):
            @pl.when(g == gi)
            def _(wlen=wlen):
                n = jnp.minimum(row + 1, wlen).astype(F32)
                acc, k = dd / n, 1
                while k < wlen:
                    acc = acc + _shift_up(acc, k, row, S)
                    k *= 2
                du_ref[...] = (acc - dd).astype(BF16)

        @pl.when(b == 0)
        def _():
            dw_ref[0] = dw
            dsc_ref[...] = dsc

        @pl.when(b > 0)
        def _():
            dw_ref[0] += dw
            dsc_ref[...] += dsc

    uc = OFF_U // POOL_GD
    return pl.pallas_call(
        body, name=name, grid=(len(POOL_WINDOWS), n_seq),
        in_specs=[pl.BlockSpec((S, POOL_GD), lambda g, b: (b, uc + g)),
                  pl.BlockSpec((S, POOL_GD), lambda g, b: (b, g)),
                  pl.BlockSpec((1, POOL_GD, POOL_GD), lambda g, b: (g, 0, 0)),
                  pl.BlockSpec((1, POOL_GD), lambda g, b: (0, g)),
                  pl.BlockSpec(memory_space=pl.ANY)],
        out_specs=[pl.BlockSpec((S, POOL_GD), lambda g, b: (b, uc + g)),
                   pl.BlockSpec((1, POOL_GD, POOL_GD), lambda g, b: (g, 0, 0)),
                   pl.BlockSpec((1, POOL_GD), lambda g, b: (0, g))],
        out_shape=[jax.ShapeDtypeStruct(dproj.shape, BF16),
                   jax.ShapeDtypeStruct((len(POOL_WINDOWS), POOL_GD, POOL_GD), F32),
                   jax.ShapeDtypeStruct((1, BRANCH_W), F32)],
        input_output_aliases={4: 0},
        scratch_shapes=[pltpu.VMEM((S, POOL_GD), BF16)],
        compiler_params=_cp(("parallel", "arbitrary")),
    )(proj, dout, pool_w, pool_scale, dproj)


def _conv_fwd(proj, conv_w, n_seq, name):
    T = proj.shape[0]
    S = T // n_seq
    nc = BRANCH_W // LANES

    def body(c_ref, w_ref, o_ref):
        row = lax.broadcasted_iota(jnp.int32, (S, LANES), 0)
        cv, cb, cc = (c_ref[:, t * LANES:(t + 1) * LANES].astype(F32) for t in range(3))
        z = cc * cv
        w = w_ref[...]
        y = w[0:1] * _shift_down(z, 2, row) + w[1:2] * _shift_down(z, 1, row) + w[2:3] * z
        o_ref[...] = (cb * y).astype(BF16)

    return pl.pallas_call(
        body, name=name, grid=(n_seq, nc),
        in_specs=[pl.BlockSpec((S, TRIPLE), lambda b, j: (b, OFF_CONV // TRIPLE + j)),
                  pl.BlockSpec((CONV_K, LANES), lambda b, j: (0, j))],
        out_specs=pl.BlockSpec((S, LANES), lambda b, j: (b, j)),
        out_shape=jax.ShapeDtypeStruct((T, BRANCH_W), BF16),
        compiler_params=_cp(("parallel", "parallel")),
    )(proj, conv_w)


def _conv_bwd(proj, dout, conv_w, dproj, n_seq, name):
    T = proj.shape[0]
    S = T // n_seq
    nc = BRANCH_W // LANES

    def body(c_ref, do_ref, w_ref, _, dc_ref, dw_ref):
        b = pl.program_id(1)
        row = lax.broadcasted_iota(jnp.int32, (S, LANES), 0)
        cv, cb, cc = (c_ref[:, t * LANES:(t + 1) * LANES].astype(F32) for t in range(3))
        dof = do_ref[...].astype(F32)
        w = w_ref[...]
        z = cc * cv
        z1, z2 = _shift_down(z, 1, row), _shift_down(z, 2, row)
        y = w[0:1] * z2 + w[1:2] * z1 + w[2:3] * z
        dy = dof * cb
        dz = w[2:3] * dy + w[1:2] * _shift_up(dy, 1, row, S) + w[0:1] * _shift_up(dy, 2, row, S)
        dc_ref[:, :LANES] = (dz * cc).astype(BF16)
        dc_ref[:, LANES:2 * LANES] = (dof * y).astype(BF16)
        dc_ref[:, 2 * LANES:] = (dz * cv).astype(BF16)
        dws = [jnp.sum(dy * zk, axis=0, keepdims=True) for zk in (z2, z1, z)]

        @pl.when(b == 0)
        def _():
            for kk in range(CONV_K):
                dw_ref[kk:kk + 1, :] = dws[kk]

        @pl.when(b > 0)
        def _():
            for kk in range(CONV_K):
                dw_ref[kk:kk + 1, :] += dws[kk]

    triple = pl.BlockSpec((S, TRIPLE), lambda j, b: (b, OFF_CONV // TRIPLE + j))
    wsp = pl.BlockSpec((CONV_K, LANES), lambda j, b: (0, j))
    return pl.pallas_call(
        body, name=name, grid=(nc, n_seq),
        in_specs=[triple, pl.BlockSpec((S, LANES), lambda j, b: (b, j)), wsp, pl.BlockSpec(memory_space=pl.ANY)],
        out_specs=[triple, wsp],
        out_shape=[jax.ShapeDtypeStruct(dproj.shape, BF16), jax.ShapeDtypeStruct((CONV_K, BRANCH_W), F32)],
        input_output_aliases={3: 0},
        compiler_params=_cp(("parallel", "arbitrary")),
    )(proj, dout, conv_w, dproj)


def _mix_fwd(oa, ob, oc, wpa, wpp, wpc, proj, b_gate, name):
    T = oa.shape[0]
    tm = min(256, T)

    def body(oa_ref, ob_ref, oc_ref, wa_ref, wp_ref, wc_ref, g_ref, bg_ref, o_ref):
        acc = jnp.zeros((tm, D_MODEL), F32)
        for i, (x_ref, w_ref) in enumerate(((oa_ref, wa_ref), (ob_ref, wp_ref), (oc_ref, wc_ref))):
            y = jnp.dot(x_ref[...], w_ref[...], preferred_element_type=F32)
            sl = slice(i * D_MODEL, (i + 1) * D_MODEL)
            acc = acc + _sigmoid(g_ref[:, sl].astype(F32) + bg_ref[:, sl]) * y
        o_ref[...] = acc.astype(BF16)

    br = pl.BlockSpec((tm, BRANCH_W), lambda i: (i, 0))
    wsp = pl.BlockSpec((BRANCH_W, D_MODEL), lambda i: (0, 0))
    return pl.pallas_call(
        body, name=name, grid=(T // tm,),
        in_specs=[br, br, br, wsp, wsp, wsp, pl.BlockSpec((tm, GATE_W), lambda i: (i, 0)),
                  pl.BlockSpec((1, GATE_W), lambda i: (0, 0))],
        out_specs=pl.BlockSpec((tm, D_MODEL), lambda i: (i, 0)),
        out_shape=jax.ShapeDtypeStruct((T, D_MODEL), BF16),
        compiler_params=_cp(("parallel",)),
    )(oa, ob, oc, wpa, wpp, wpc, proj, b_gate)


def _mix_bwd(oa, ob, oc, wpa, wpp, wpc, proj, b_gate, dmixed, name):
    T = oa.shape[0]
    tm = min(256, T)

    def body(oa_ref, ob_ref, oc_ref, wa_ref, wp_ref, wc_ref, g_ref, bg_ref, dm_ref,
             dya_ref, dyb_ref, dyc_ref, dg_ref, dbg_ref):
        i0 = pl.program_id(0)
        dm = dm_ref[...].astype(F32)
        parts = []
        for i, (x_ref, w_ref, dy_ref) in enumerate(((oa_ref, wa_ref, dya_ref), (ob_ref, wp_ref, dyb_ref),
                                                    (oc_ref, wc_ref, dyc_ref))):
            y = jnp.dot(x_ref[...], w_ref[...], preferred_element_type=F32)
            sl = slice(i * D_MODEL, (i + 1) * D_MODEL)
            gate = _sigmoid(g_ref[:, sl].astype(F32) + bg_ref[:, sl])
            dy_ref[...] = (dm * gate).astype(BF16)
            dgl = dm * y * gate * (1.0 - gate)
            dg_ref[:, sl] = dgl.astype(BF16)
            parts.append(jnp.sum(dgl, axis=0, keepdims=True))

        @pl.when(i0 == 0)
        def _():
            for i in range(3):
                dbg_ref[:, i * D_MODEL:(i + 1) * D_MODEL] = parts[i]

        @pl.when(i0 > 0)
        def _():
            for i in range(3):
                dbg_ref[:, i * D_MODEL:(i + 1) * D_MODEL] += parts[i]

    br = pl.BlockSpec((tm, BRANCH_W), lambda i: (i, 0))
    wsp = pl.BlockSpec((BRANCH_W, D_MODEL), lambda i: (0, 0))
    row = pl.BlockSpec((tm, D_MODEL), lambda i: (i, 0))
    gsp = pl.BlockSpec((tm, GATE_W), lambda i: (i, 0))
    bsp = pl.BlockSpec((1, GATE_W), lambda i: (0, 0))
    act = jax.ShapeDtypeStruct((T, D_MODEL), BF16)
    return pl.pallas_call(
        body, name=name, grid=(T // tm,),
        in_specs=[br, br, br, wsp, wsp, wsp, gsp, bsp, row],
        out_specs=[row, row, row, gsp, bsp],
        out_shape=[act, act, act, jax.ShapeDtypeStruct((T, MAIN_COLS), BF16),
                   jax.ShapeDtypeStruct((1, GATE_W), F32)],
        compiler_params=_cp(("arbitrary",)),
    )(oa, ob, oc, wpa, wpp, wpc, proj, b_gate, dmixed)


GU_TILE = 256


def _gu_col(c):
    t, r = divmod(c, GU_TILE)
    return (t // 2) * GU_TILE + r + (FFN_HIDDEN if t % 2 else 0)


def _gate_up_swiglu(h, w, name):
    T, K = h.shape
    tm = min(2048, T)

    def body(h_ref, w_ref, ab_ref, s_ref):
        prod = jnp.dot(h_ref[...], w_ref[...], preferred_element_type=F32)
        ab_ref[...] = prod.astype(BF16)
        a = prod[:, :GU_TILE]
        s_ref[...] = (a * _sigmoid(a) * prod[:, GU_TILE:]).astype(BF16)

    return pl.pallas_call(
        body, name=name, grid=(T // tm, FFN_HIDDEN // GU_TILE),
        in_specs=[pl.BlockSpec((tm, K), lambda i, j: (i, 0)), pl.BlockSpec((K, 2 * GU_TILE), lambda i, j: (0, j))],
        out_specs=[pl.BlockSpec((tm, 2 * GU_TILE), lambda i, j: (i, j)), pl.BlockSpec((tm, GU_TILE), lambda i, j: (i, j))],
        out_shape=[jax.ShapeDtypeStruct((T, 2 * FFN_HIDDEN), BF16), jax.ShapeDtypeStruct((T, FFN_HIDDEN), BF16)],
        compiler_params=_cp(("parallel", "parallel")),
    )(h, w)


def _swiglu_bwd_fused(dx, w_down, ab, name):
    T, K = dx.shape
    tm = min(2048, T)

    def body(dx_ref, w_ref, ab_ref, o_ref):
        ds = lax.dot_general(dx_ref[...], w_ref[...], _NT, preferred_element_type=F32)
        a = ab_ref[:, :GU_TILE].astype(F32)
        b = ab_ref[:, GU_TILE:].astype(F32)
        sg = _sigmoid(a)
        o_ref[:, :GU_TILE] = (ds * b * sg * (1.0 + a * (1.0 - sg))).astype(BF16)
        o_ref[:, GU_TILE:] = (ds * a * sg).astype(BF16)

    pair = pl.BlockSpec((tm, 2 * GU_TILE), lambda i, j: (i, j))
    return pl.pallas_call(
        body, name=name, grid=(T // tm, FFN_HIDDEN // GU_TILE),
        in_specs=[pl.BlockSpec((tm, K), lambda i, j: (i, 0)), pl.BlockSpec((GU_TILE, K), lambda i, j: (j, 0)), pair],
        out_specs=pair, out_shape=jax.ShapeDtypeStruct((T, 2 * FFN_HIDDEN), BF16),
        compiler_params=_cp(("parallel", "parallel")),
    )(dx, w_down, ab)


def _adamw_update(w_ref, g_ref, m_ref, v_ref, d_ref, nm_ref, nv_ref):
    gv = g_ref[...]
    nm = ADAM_B1 * m_ref[...] + (1.0 - ADAM_B1) * gv
    nv = ADAM_B2 * v_ref[...] + (1.0 - ADAM_B2) * (gv * gv)
    m_hat = nm / (1.0 - ADAM_B1 ** ADAM_STEP)
    v_hat = nv / (1.0 - ADAM_B2 ** ADAM_STEP)
    d_ref[...] = -ADAM_LR * (m_hat / (jnp.sqrt(v_hat) + ADAM_EPS) + ADAM_WD * w_ref[...])
    nm_ref[...] = nm
    nv_ref[...] = nv


def _adamw_many(ws, gs, ms, vs, name):
    n = len(ws)

    def body(*refs):
        ins, outs = refs[:4 * n], refs[4 * n:]
        for t in range(n):
            _adamw_update(ins[t], ins[n + t], ins[2 * n + t], ins[3 * n + t], outs[t], outs[n + t], outs[2 * n + t])

    shapes = [jax.ShapeDtypeStruct(w.shape, F32) for w in ws]
    out = pl.pallas_call(body, name=name, out_shape=shapes * 3, compiler_params=_cp())(*ws, *gs, *ms, *vs)
    return out[:n], out[n:2 * n], out[2 * n:]


def _adamw(w, g, m, v, name):
    R, C = w.shape
    tr = R
    for cand in (256, 352, 128, 64, 8):
        if R > cand and R % cand == 0:
            tr = cand
            break

    def body(w_ref, g_ref, m_ref, v_ref, d_ref, nm_ref, nv_ref):
        _adamw_update(w_ref, g_ref, m_ref, v_ref, d_ref, nm_ref, nv_ref)

    blk = pl.BlockSpec((tr, C), lambda i: (i, 0))
    sh = jax.ShapeDtypeStruct((R, C), F32)
    return pl.pallas_call(
        body, name=name, grid=(R // tr,), in_specs=[blk] * 4, out_specs=[blk] * 3, out_shape=[sh] * 3,
        compiler_params=_cp(("parallel",)),
    )(w, g, m, v)


def _sum_slabs(x, name):
    n, R, C = x.shape
    tr = R
    for cand in (512, 256, 128, 64, 32, 16, 8):
        if R > cand and R % cand == 0:
            tr = cand
            break

    def body(x_ref, o_ref):
        acc = x_ref[0].astype(F32)
        for j in range(1, n):
            acc = acc + x_ref[j].astype(F32)
        o_ref[...] = acc

    return pl.pallas_call(
        body, name=name, grid=(R // tr,), in_specs=[pl.BlockSpec((n, tr, C), lambda i: (0, i, 0))],
        out_specs=pl.BlockSpec((tr, C), lambda i: (i, 0)), out_shape=jax.ShapeDtypeStruct((R, C), F32),
        compiler_params=_cp(("parallel",)),
    )(x)


def _multi_gather(xs, layers, name):
    nt = len(xs)
    shapes = [x.shape if lay is None else x.shape[1:] for x, lay in zip(xs, layers)]

    def body(*refs):
        x_refs, out_refs = refs[:nt], refs[nt:2 * nt]
        send_sems, recv_sems, local_sems = refs[2 * nt:]
        x_, y_, c_ = lax.axis_index("x"), lax.axis_index("y"), lax.axis_index("c")
        me, sibling = (x_, y_, c_), (x_, y_, 1 - c_)
        chips = [(1 - x_, y_), (x_, 1 - y_), (1 - x_, 1 - y_)]

        def own_block(t):
            return x_refs[t] if layers[t] is None else x_refs[t].at[layers[t]]

        def copy(t, k, block, to, own=False):
            px, py, pc = block
            dst = out_refs[t].at[4 * px + 2 * py + pc]
            return pltpu.make_async_remote_copy(
                src_ref=own_block(t) if own else dst, dst_ref=dst,
                send_sem=send_sems.at[t, k], recv_sem=recv_sems.at[t, k],
                device_id=to, device_id_type=pl.DeviceIdType.MESH)

        mine, first, passed = [], [], []
        for t in range(nt):
            mine.append(pltpu.make_async_copy(own_block(t), out_refs[t].at[4 * x_ + 2 * y_ + c_], local_sems.at[t]))
            mine[-1].start()
            first.append([copy(t, 1 + j, me, (*chip, c_), own=True) for j, chip in enumerate(chips)]
                         + [copy(t, 0, me, sibling, own=True)])
            for cp in first[-1]:
                cp.start()
        for t in range(nt):
            for j, chip in enumerate(chips):
                copy(t, 1 + j, (*chip, c_), me).wait_recv()
                passed.append(copy(t, 4 + j, (*chip, c_), sibling))
                passed[-1].start()
        for t in range(nt):
            copy(t, 0, sibling, me).wait_recv()
            for j, chip in enumerate(chips):
                copy(t, 4 + j, (*chip, 1 - c_), me).wait_recv()
        for cp in [c for f in first for c in f] + passed:
            cp.wait_send()
        for cp in mine:
            cp.wait()

    hbm = pl.BlockSpec(memory_space=pl.ANY)
    return pl.pallas_call(
        body, name=name, out_shape=[jax.ShapeDtypeStruct((N_DEV,) + tuple(s), x.dtype) for s, x in zip(shapes, xs)],
        in_specs=[hbm] * nt, out_specs=[hbm] * nt,
        scratch_shapes=[pltpu.SemaphoreType.DMA((nt, 7)), pltpu.SemaphoreType.DMA((nt, 7)),
                        pltpu.SemaphoreType.DMA((nt,))],
    )(*xs)


def _direct_gather(xs, name):
    nt = len(xs)

    def body(*refs):
        x_refs, out_refs = refs[:nt], refs[nt:2 * nt]
        send_sems, recv_sems, local_sems = refs[2 * nt:]
        x_, y_, c_ = lax.axis_index("x"), lax.axis_index("y"), lax.axis_index("c")
        me = 4 * x_ + 2 * y_ + c_
        mine, out, inc = [], [], []
        for t in range(nt):
            mine.append(pltpu.make_async_copy(x_refs[t], out_refs[t].at[me], local_sems.at[t]))
            mine[-1].start()
        for k in _PEER_ORDER:
            px, py, pc = x_ ^ ((k >> 2) & 1), y_ ^ ((k >> 1) & 1), c_ ^ (k & 1)
            peer = 4 * px + 2 * py + pc
            for t in range(nt):
                def copy(dst):
                    return pltpu.make_async_remote_copy(
                        src_ref=x_refs[t], dst_ref=out_refs[t].at[dst],
                        send_sem=send_sems.at[t, k - 1], recv_sem=recv_sems.at[t, k - 1],
                        device_id=(px, py, pc), device_id_type=pl.DeviceIdType.MESH)

                out.append(copy(me))
                inc.append(copy(peer))
        for cp in out:
            cp.start()
        for cp in inc:
            cp.wait_recv()
        for cp in out:
            cp.wait_send()
        for cp in mine:
            cp.wait()

    hbm = pl.BlockSpec(memory_space=pl.ANY)
    return pl.pallas_call(
        body, name=name, out_shape=[jax.ShapeDtypeStruct((N_DEV,) + x.shape, x.dtype) for x in xs],
        in_specs=[hbm] * nt, out_specs=[hbm] * nt,
        scratch_shapes=[pltpu.SemaphoreType.DMA((nt, N_DEV - 1)), pltpu.SemaphoreType.DMA((nt, N_DEV - 1)),
                        pltpu.SemaphoreType.DMA((nt,))],
    )(*xs)


_HBM = pl.BlockSpec(memory_space=pltpu.HBM)
_SEM = pl.BlockSpec(memory_space=pltpu.SEMAPHORE)
_PEER_ORDER = (2, 4, 6, 3, 5, 7, 1)


def _split_copies(src_refs, land_refs, send_sems, recv_sems, layers, per_peer):
    x_, y_, c_ = lax.axis_index("x"), lax.axis_index("y"), lax.axis_index("c")
    me = 4 * x_ + 2 * y_ + c_
    copies = []
    for k in _PEER_ORDER:
        px, py, pc = x_ ^ ((k >> 2) & 1), y_ ^ ((k >> 1) & 1), c_ ^ (k & 1)
        peer = 4 * px + 2 * py + pc
        for t in range(len(src_refs)):
            if per_peer:
                src = src_refs[t].at[peer]
            else:
                src = src_refs[t] if layers[t] is None else src_refs[t].at[layers[t]]
            copies.append(pltpu.make_async_remote_copy(
                src_ref=src, dst_ref=land_refs[t].at[me],
                send_sem=send_sems.at[t * (N_DEV - 1) + k - 1], recv_sem=recv_sems.at[t * (N_DEV - 1) + k - 1],
                device_id=(px, py, pc), device_id_type=pl.DeviceIdType.MESH))
    return copies


def _own_copies(src_refs, land_refs, sems, layers, per_peer):
    nt = len(src_refs)
    me = 4 * lax.axis_index("x") + 2 * lax.axis_index("y") + lax.axis_index("c")
    copies = []
    for t in range(nt):
        if per_peer:
            src = src_refs[t].at[me]
        else:
            src = src_refs[t] if layers[t] is None else src_refs[t].at[layers[t]]
        copies.append(pltpu.make_async_copy(src, land_refs[t].at[me], sems.at[nt * (N_DEV - 1) + t]))
    return copies


def _split_start(srcs, layers, per_peer, after, name):
    nt = len(srcs)
    if per_peer:
        land_shapes = [s.shape for s in srcs]
    else:
        land_shapes = [(N_DEV,) + tuple(s.shape if lay is None else s.shape[1:]) for s, lay in zip(srcs, layers)]

    def body(*refs):
        src_refs, land_refs = refs[:nt], refs[nt:2 * nt]
        send_sems, recv_sems = refs[2 * nt + 1], refs[2 * nt + 2]
        token = refs[-1]
        for cp in _split_copies(src_refs, land_refs, send_sems, recv_sems, layers, per_peer):
            cp.start()
        for cp in _own_copies(src_refs, land_refs, send_sems, layers, per_peer):
            cp.start()
        token[...] = jnp.zeros_like(token)

    lands = [pltpu.with_memory_space_constraint(lax.empty(s, x.dtype), pltpu.HBM) for s, x in zip(land_shapes, srcs)]
    srcs = [pltpu.with_memory_space_constraint(x, pltpu.HBM) for x in srcs]
    out = pl.pallas_call(
        body, name=name,
        out_shape=(pltpu.SemaphoreType.DMA((nt * N_DEV,)), pltpu.SemaphoreType.DMA((nt * (N_DEV - 1),)),
                   *[pltpu.HBM(x.shape, x.dtype) for x in srcs], *[pltpu.HBM(s, x.dtype) for s, x in zip(land_shapes, srcs)],
                   jax.ShapeDtypeStruct((8, LANES), F32)),
        in_specs=[_HBM] * (2 * nt) + [pl.BlockSpec(memory_space=pl.ANY)],
        out_specs=(_SEM, _SEM, *([_HBM] * (2 * nt)), pl.BlockSpec(memory_space=pltpu.VMEM)),
        input_output_aliases={i: 2 + i for i in range(2 * nt)},
        compiler_params=pltpu.CompilerParams(has_side_effects=pltpu.SideEffectType.DATAFLOW_SIDE_EFFECTING),
    )(*srcs, *lands, after)
    return out[0], out[1], list(out[2:2 + nt]), list(out[2 + nt:2 + 2 * nt]), out[-1]


def _split_wait(started, layers, per_peer, after, name):
    send_sems, recv_sems, srcs, lands, _ = started
    nt = len(srcs)

    def body(*refs):
        src_refs, land_refs = refs[:nt], refs[nt:2 * nt]
        s_sems, r_sems = refs[2 * nt], refs[2 * nt + 1]
        for cp in _split_copies(src_refs, land_refs, s_sems, r_sems, layers, per_peer):
            cp.wait_send()
            cp.wait_recv()
        for cp in _own_copies(src_refs, land_refs, s_sems, layers, per_peer):
            cp.wait()

    out = pl.pallas_call(
        body, name=name,
        out_shape=tuple(pltpu.HBM(x.shape, x.dtype) for x in srcs + lands),
        in_specs=[_HBM] * (2 * nt) + [_SEM, _SEM, pl.BlockSpec(memory_space=pl.ANY)],
        out_specs=tuple([_HBM] * (2 * nt)),
        input_output_aliases={i: i for i in range(2 * nt)},
        compiler_params=pltpu.CompilerParams(has_side_effects=pltpu.SideEffectType.DATAFLOW_SIDE_EFFECTING),
    )(*srcs, *lands, send_sems, recv_sems, after)
    return list(out[nt:])


def _runs(mapping):
    runs, c, n = [], 0, len(mapping)
    while c < n:
        if mapping[c] is None:
            c += 1
            continue
        sid, d, lo = mapping[c][0], mapping[c][1] - c, c
        while c < n and mapping[c] is not None and mapping[c][0] == sid and mapping[c][1] - c == d:
            c += 1
        runs.append((lo, c, sid, d))
    return runs


def _tile_plan(mapping, src_widths):
    runs = _runs(mapping)
    plan = []
    for t in range(len(mapping) // LANES):
        pieces = []
        for lo, hi, sid, d in runs:
            lo_t, hi_t = max(lo, t * LANES), min(hi, (t + 1) * LANES)
            if lo_t >= hi_t:
                continue
            a = ((lo_t + d) // LANES) * LANES
            win = min(2 * LANES, src_widths[sid] - a)
            shift = t * LANES + d - a
            pieces.append((sid, a, win, shift, lo_t - t * LANES, hi_t - t * LANES))
        plan.append(pieces)
    return plan


def _reblock(srcs, src_views, outs, out_views, name):
    R = srcs[0].shape[-2]
    tr = min(512, R)
    widths = {sid: srcs[ai].shape[-1] for sid, (ai, _) in src_views.items()}
    plans = [(ai, li, _tile_plan(mapping, widths)) for ai, li, mapping in out_views]
    ns = len(srcs)

    def body(*refs):
        s_refs, o_refs = refs[:ns], refs[ns:]
        cache = {}

        def shift_matrix(win, shift, lo, hi):
            key = (win, shift, lo, hi)
            if key not in cache:
                r = lax.broadcasted_iota(jnp.int32, (win, LANES), 0)
                c = lax.broadcasted_iota(jnp.int32, (win, LANES), 1)
                hit = jnp.logical_and(r - c == shift, jnp.logical_and(c >= lo, c < hi))
                cache[key] = jnp.where(hit, 1.0, 0.0).astype(BF16)
            return cache[key]

        for ai, li, plan in plans:
            for t, pieces in enumerate(plan):
                acc = None
                whole = len(pieces) == 1 and pieces[0][3:] == (0, 0, LANES)
                for sid, a, win, shift, lo, hi in pieces:
                    sa, sl = src_views[sid]
                    if whole:
                        win = LANES
                    src = s_refs[sa][:, a:a + win] if sl is None else s_refs[sa][sl, :, a:a + win]
                    if whole:
                        acc = src
                    else:
                        part = jnp.dot(src, shift_matrix(win, shift, lo, hi), preferred_element_type=F32)
                        acc = part if acc is None else acc + part
                val = jnp.zeros((tr, LANES), BF16) if acc is None else acc.astype(BF16)
                if li is None:
                    o_refs[ai][:, t * LANES:(t + 1) * LANES] = val
                else:
                    o_refs[ai][li, :, t * LANES:(t + 1) * LANES] = val

    def spec(shape):
        if len(shape) == 2:
            return pl.BlockSpec((tr, shape[1]), lambda i: (i, 0))
        return pl.BlockSpec((shape[0], tr, shape[2]), lambda i: (0, i, 0))

    return pl.pallas_call(
        body, name=name, grid=(R // tr,), in_specs=[spec(s.shape) for s in srcs],
        out_specs=[spec(s) for s in outs], out_shape=[jax.ShapeDtypeStruct(s, BF16) for s in outs],
        compiler_params=_cp(("parallel",)),
    )(*srcs)


SHARDED = ("w_in", "w_gate_up", "w_proj_attn", "w_proj_pool", "w_proj_conv", "w_out", "w_down")
WEIGHT_ORDER = ("attn_norm", "w_in", "b_forget", "b_gate", "w_proj_attn", "pool_w", "pool_scale", "w_proj_pool",
                "conv_w", "w_proj_conv", "w_out", "ffn_norm", "w_gate_up", "w_down", "final_norm")
IN_SHARD, IN_SHARD_PAD = IN_COLS // N_DEV, 896
GU_SHARD, GU_SHARD_PAD = 2 * FFN_HIDDEN // N_DEV, 768


def _w_in_col(c):
    if c < OFF_QKV:
        return c + 3592
    if c < OFF_U:
        base, off = (0, OFF_QKV) if c < OFF_CONV else (2056, OFF_CONV)
        j, t = divmod(c - off, TRIPLE)
        which, e = divmod(t, LANES)
        return base + which * BRANCH_W + j * LANES + e
    return c - OFF_U + 1544


def _w_in_full(gathered, name):
    main = [divmod(_w_in_col(c), IN_SHARD) for c in range(MAIN_COLS)]
    fcols = [divmod(1536 + c, IN_SHARD) if c < N_HEADS else None for c in range(LANES)]
    R = gathered.shape[1]
    return _reblock([gathered], {i: (0, i) for i in range(N_DEV)}, [(R, MAIN_COLS), (R, LANES)],
                    [(0, None, main), (1, None, fcols)], name)


def _w_in_slabs(dmain, dwf, name):
    inv = {_w_in_col(c): ("m", c) for c in range(MAIN_COLS)}
    inv.update({1536 + c: ("f", c) for c in range(N_HEADS)})
    views = []
    for i in range(N_DEV):
        mapping = [inv[IN_SHARD * i + j] if j < IN_SHARD else None for j in range(IN_SHARD_PAD)]
        views.append((0, i, mapping))
    R = dmain.shape[0]
    return _reblock([dmain, dwf], {"m": (0, None), "f": (1, None)}, [(N_DEV, R, IN_SHARD_PAD)], views, name)[0]


def _w_gu_full(gathered, name):
    mapping = [divmod(_gu_col(c), GU_SHARD) for c in range(2 * FFN_HIDDEN)]
    R = gathered.shape[1]
    return _reblock([gathered], {i: (0, i) for i in range(N_DEV)}, [(R, 2 * FFN_HIDDEN)], [(0, None, mapping)], name)[0]


def _w_gu_slabs(dw, name):
    inv = {_gu_col(c): c for c in range(2 * FFN_HIDDEN)}
    views = [(0, i, [("w", inv[GU_SHARD * i + j]) if j < GU_SHARD else None for j in range(GU_SHARD_PAD)])
             for i in range(N_DEV)]
    R = dw.shape[0]
    return _reblock([dw], {"w": (0, None)}, [(N_DEV, R, GU_SHARD_PAD)], views, name)[0]


def _layer_fwd(x, W, n_seq, l, h1=None, next_norm=None):
    T = x.shape[0]
    sfx = f"_l{l}"
    if h1 is None:
        h1 = _rms_fwd(x, W["attn_norm"], "rms1" + sfx)
    proj = _matmul(h1, W["w_main"], mode="nn", out_dtype=BF16, name="proj_main" + sfx)
    f = _matmul(h1, W["w_f"], mode="nn", out_dtype=F32, name="proj_f" + sfx)
    qa, ka, va = _fox_prep(f, W["b_forget"], proj, n_seq, "fox_prep" + sfx)
    oa, oa32, lse = _attn_fwd2(qa, ka, va, n_seq, "attn_fwd" + sfx)
    if "late" in W:
        W.update(W.pop("late")(oa))
    ob = _pool_fwd(proj, W["pool_w"], W["pool_scale"], n_seq, "pool_fwd" + sfx)
    oc = _conv_fwd(proj, W["conv_w"], n_seq, "conv_fwd" + sfx)
    mixed = _mix_fwd(oa, ob, oc, W["w_proj_attn"], W["w_proj_pool"], W["w_proj_conv"], proj, W["b_gate"],
                     "mix_fwd" + sfx)
    x2, h2 = _matmul(mixed, W["w_out"], mode="nn", out_dtype=F32, name="out_proj" + sfx, tm=1024, tn=1024,
                     residual=x, rms_g=W["ffn_norm"])
    ab, s = _gate_up_swiglu(h2, W["w_gate_up"], "gate_up" + sfx)
    x3 = _matmul(s, W["w_down"], mode="nn", out_dtype=F32, name="down" + sfx, tm=1024, tn=1024, tk=1408,
                 residual=x2, rms_g=next_norm)
    x3, h1_next = x3 if next_norm is not None else (x3, None)
    saved = dict(x=x, h1=h1, proj=proj, f=f, qa=qa, ka=ka, oa=oa, oa32=oa32, lse=lse, ob=ob, oc=oc, mixed=mixed, x2=x2,
                 h2=h2, ab=ab, s=s)
    return x3, saved, h1_next


def _layer_bwd(dx3, dx3b, W, sv, n_seq, l, stage=None):
    T = dx3.shape[0]
    sfx = f"_l{l}"
    G = {}
    stage = stage or (lambda l, group, G, W: W)
    dab = _swiglu_bwd_fused(dx3b, W["w_down"], sv["ab"], "d_ab" + sfx)
    G["w_down"] = _matmul(sv["s"], dx3b, mode="tn", out_dtype=BF16, name="dw_down" + sfx, tm=256, tn=1024)
    dh2 = _matmul(dab, W["w_gate_up"], mode="nt", out_dtype=BF16, name="d_h2" + sfx, tm=1024, tn=1024, tk=1408)
    G["w_gate_up"] = _matmul(sv["h2"], dab, mode="tn", out_dtype=BF16, name="dw_gate_up" + sfx, tm=1024)
    W = stage(l, "ffn", G, W)
    dx2, dx2b, G["ffn_norm"] = _rms_bwd(sv["x2"], W["ffn_norm"], dh2, dx3, "rms2_bwd" + sfx)
    dmixed = _matmul(dx2b, W["w_out"], mode="nt", out_dtype=BF16, name="d_mixed" + sfx)
    G["w_out"] = _matmul(sv["mixed"], dx2b, mode="tn", out_dtype=BF16, name="dw_out" + sfx, tm=1024)
    dya, dyb, dyc, dproj, G["b_gate"] = _mix_bwd(sv["oa"], sv["ob"], sv["oc"], W["w_proj_attn"], W["w_proj_pool"],
                                                 W["w_proj_conv"], sv["proj"], W["b_gate"], dmixed, "mix_bwd" + sfx)
    douts = {}
    for br, dy, o in (("attn", dya, sv["oa"]), ("pool", dyb, sv["ob"]), ("conv", dyc, sv["oc"])):
        douts[br] = _matmul(dy, W["w_proj_" + br], mode="nt", out_dtype=BF16, name=f"d_{br}_out" + sfx)
        G["w_proj_" + br] = _matmul(o, dy, mode="tn", out_dtype=BF16, name=f"dw_proj_{br}" + sfx, tm=512)
    W = stage(l, "mix", G, W)
    dproj, G["conv_w"] = _conv_bwd(sv["proj"], douts["conv"], W["conv_w"], dproj, n_seq, "conv_bwd" + sfx)
    dproj, G["pool_w"], G["pool_scale"] = _pool_bwd(sv["proj"], douts["pool"], W["pool_w"], W["pool_scale"], dproj,
                                                    n_seq, "pool_bwd" + sfx)
    dproj, dFk = _attn_bwd(sv["qa"], sv["ka"], sv["proj"], douts["attn"], sv["oa32"], sv["lse"], dproj, n_seq,
                           "attn_bwd" + sfx)
    dF = jnp.pad(dFk.reshape(N_HEADS, T).T, ((0, 0), (0, LANES - N_HEADS)))
    df, G["b_forget"] = _fox_cumsum_bwd(sv["f"], W["b_forget"], dF, n_seq, "fox_cumsum_bwd" + sfx)
    G["w_main"] = _matmul(sv["h1"], dproj, mode="tn", out_dtype=BF16, name="dw_main" + sfx, tm=1024)
    G["w_f"] = _matmul(sv["h1"], df, mode="tn", out_dtype=BF16, name="dw_f" + sfx, tm=1024)
    W = stage(l, "w_in", G, W)
    dh1 = _matmul(df, W["w_f"], mode="nt", out_dtype=F32, name="d_h1_f" + sfx)
    dh1 = _matmul(dproj, W["w_main"], mode="nt", out_dtype=BF16, name="d_h1_main" + sfx, tm=1024, tn=1024, tk=1664,
                  residual=dh1)
    dx, dxb, G["attn_norm"] = _rms_bwd(sv["x"], W["attn_norm"], dh1, dx2, "rms1_bwd" + sfx)
    return dx, dxb, G


def _replicated_operands(rep, l):
    W = {}
    W["attn_norm"], W["ffn_norm"] = rep["attn_norm"][l], rep["ffn_norm"][l]
    W["b_forget"] = jnp.pad(rep["b_forget"][l].reshape(1, N_HEADS), ((0, 0), (0, LANES - N_HEADS)))
    W["b_gate"] = rep["b_gate"][l].reshape(1, GATE_W)
    W["pool_w"] = rep["pool_w"][l].astype(BF16)
    W["pool_scale"] = rep["pool_scale"][l].reshape(1, BRANCH_W)
    return W


def _local_step(x, target, get_W, attn_norms, final_norm, stage=None):
    n_seq, S, Dm = x.shape
    T = n_seq * S
    xt = x.reshape(T, Dm)
    saved, Ws, h1 = [], [], None
    for l in range(DEPTH):
        Ws.append(get_W(l, xt))
        next_norm = attn_norms[l + 1] if l + 1 < DEPTH else None
        xt, sv, h1 = _layer_fwd(xt, Ws[l], n_seq, l, h1, next_norm)
        saved.append(sv)
    loss, dx, dxb, g_final = _loss_head(xt, final_norm, target.reshape(T, Dm), "loss_head")
    grads = [None] * DEPTH
    for l in reversed(range(DEPTH)):
        dx, dxb, grads[l] = _layer_bwd(dx, dxb, Ws[l], saved[l], n_seq, l, stage)
    return loss, dx.reshape(n_seq, S, Dm), grads, g_final


def _padded_shards(weights):
    pads = {"w_in": IN_SHARD_PAD - IN_SHARD, "w_gate_up": GU_SHARD_PAD - GU_SHARD}
    return {n: jnp.pad(weights[n], ((0, 0), (0, 0), (0, pads.get(n, 0)))).astype(BF16) for n in SHARDED}


def _full_operands(g, l):
    W = {}
    if "w_in" in g:
        W["w_main"], W["w_f"] = _w_in_full(g["w_in"], f"w_in_full_l{l}")
    if "w_gate_up" in g:
        W["w_gate_up"] = _w_gu_full(g["w_gate_up"], f"w_gate_up_full_l{l}")
    for n in ("w_proj_attn", "w_proj_pool", "w_proj_conv"):
        if n in g:
            W[n] = jnp.transpose(g[n], (1, 0, 2)).reshape(BRANCH_W, D_MODEL)
    if "w_out" in g:
        W["w_out"] = g["w_out"].reshape(D_MODEL, D_MODEL)
    if "w_down" in g:
        W["w_down"] = g["w_down"].reshape(FFN_HIDDEN, D_MODEL)
    return W


GRAD_GROUPS = {"ffn": ("w_down", "w_gate_up"),
               "mix": ("w_out", "w_proj_attn", "w_proj_pool", "w_proj_conv"),
               "w_in": ("w_in",)}


def _grad_slabs(G, n, l):
    if n == "w_in":
        return _w_in_slabs(G["w_main"], G["w_f"], f"w_in_slabs_l{l}")
    if n == "w_gate_up":
        return _w_gu_slabs(G["w_gate_up"], f"w_gate_up_slabs_l{l}")
    if n == "w_out":
        return G["w_out"].reshape(N_DEV, D_MODEL // N_DEV, D_MODEL)
    if n == "w_down":
        return G["w_down"].reshape(N_DEV, FFN_HIDDEN // N_DEV, D_MODEL)
    return jnp.transpose(G[n].reshape(BRANCH_W, N_DEV, D_MODEL // N_DEV), (1, 0, 2))


def _sum_layer_grads(recv, l):
    out = {n: _sum_slabs(r, f"sum_{n}_l{l}") for n, r in recv.items()}
    if "w_in" in out:
        out["w_in"] = out["w_in"][:, :IN_SHARD]
    if "w_gate_up" in out:
        out["w_gate_up"] = out["w_gate_up"][:, :GU_SHARD]
    return out


def _sum_small(xs, name):
    def body(*refs):
        for x_ref, o_ref in zip(refs[:len(xs)], refs[len(xs):]):
            acc = x_ref[0]
            for j in range(1, N_DEV):
                acc = acc + x_ref[j]
            o_ref[...] = acc

    return pl.pallas_call(
        body, name=name, out_shape=[jax.ShapeDtypeStruct(x.shape[1:], F32) for x in xs],
        compiler_params=_cp(),
    )(*xs)


def _as_2d(a):
    if a.ndim == 1:
        return a.reshape(1, -1)
    return a.reshape(-1, a.shape[-1])


def kernel(x, attn_norm, w_in, b_forget, b_gate, w_proj_attn, pool_w, pool_scale, w_proj_pool, conv_w, w_proj_conv, w_out, ffn_norm, w_gate_up, w_down, final_norm, loss_target, m_attn_norm, m_w_in, m_b_forget, m_b_gate, m_w_proj_attn, m_pool_w, m_pool_scale, m_w_proj_pool, m_conv_w, m_w_proj_conv, m_w_out, m_ffn_norm, m_w_gate_up, m_w_down, m_final_norm, v_attn_norm, v_w_in, v_b_forget, v_b_gate, v_w_proj_attn, v_pool_w, v_pool_scale, v_w_proj_pool, v_conv_w, v_w_proj_conv, v_w_out, v_ffn_norm, v_w_gate_up, v_w_down, v_final_norm):
    weights = dict(attn_norm=attn_norm, w_in=w_in, b_forget=b_forget, b_gate=b_gate, w_proj_attn=w_proj_attn,
                   pool_w=pool_w, pool_scale=pool_scale, w_proj_pool=w_proj_pool, conv_w=conv_w,
                   w_proj_conv=w_proj_conv, w_out=w_out, ffn_norm=ffn_norm, w_gate_up=w_gate_up, w_down=w_down,
                   final_norm=final_norm)
    moments_m = dict(attn_norm=m_attn_norm, w_in=m_w_in, b_forget=m_b_forget, b_gate=m_b_gate,
                     w_proj_attn=m_w_proj_attn, pool_w=m_pool_w, pool_scale=m_pool_scale, w_proj_pool=m_w_proj_pool,
                     conv_w=m_conv_w, w_proj_conv=m_w_proj_conv, w_out=m_w_out, ffn_norm=m_ffn_norm,
                     w_gate_up=m_w_gate_up, w_down=m_w_down, final_norm=m_final_norm)
    moments_v = dict(attn_norm=v_attn_norm, w_in=v_w_in, b_forget=v_b_forget, b_gate=v_b_gate,
                     w_proj_attn=v_w_proj_attn, pool_w=v_pool_w, pool_scale=v_pool_scale, w_proj_pool=v_w_proj_pool,
                     conv_w=v_conv_w, w_proj_conv=v_w_proj_conv, w_out=v_w_out, ffn_norm=v_ffn_norm,
                     w_gate_up=v_w_gate_up, w_down=v_w_down, final_norm=v_final_norm)

    sh = _padded_shards(weights)
    names = list(SHARDED)
    rest = [n for n in names if n != "w_in"]
    me = 4 * lax.axis_index("x") + 2 * lax.axis_index("y") + lax.axis_index("c")
    w_in0, conv_all = _multi_gather([sh["w_in"], conv_w], [0, None], "gather_w_in_l0")
    started, after = {}, w_in0
    for l in range(DEPTH):
        for group, gnames in (("w_in", ["w_in"]), ("rest", rest)):
            if (l, group) != (0, "w_in"):
                started[l, group] = _split_start([sh[n] for n in gnames], [l] * len(gnames), False, after,
                                                 f"gather_start_{group}_l{l}")
                after = started[l, group][4]
    last_token = after

    def get_W(l, xt):
        if l == 0:
            w_in = w_in0
        else:
            w_in = _split_wait(started[l, "w_in"], [l], False, xt, f"gather_wait_w_in_l{l}")[0]
        W = _full_operands({"w_in": w_in}, l)

        def late(after):
            lands = _split_wait(started[l, "rest"], [l] * len(rest), False, after, f"gather_wait_rest_l{l}")
            return _full_operands(dict(zip(rest, lands)), l)

        W["late"] = late
        W.update(_replicated_operands(weights, l))
        W["conv_w"] = jnp.transpose(conv_all[:, l], (1, 0, 2)).reshape(CONV_K, BRANCH_W)
        if l == 0:
            W["attn_norm"] = W["attn_norm"] + last_token[0, 0]
        return W

    exchanges = []

    def stage(l, group, G, W):
        gnames = GRAD_GROUPS[group]
        slabs = [_grad_slabs(G, n, l) for n in gnames]
        started = _split_start(slabs, None, True, slabs[0], f"exchange_start_{group}_l{l}")
        exchanges.append((l, group, gnames, slabs, started))
        tie = {"ffn": "ffn_norm", "mix": "conv_w", "w_in": "w_f"}[group]
        W = dict(W)
        W[tie] = W[tie] + started[4][0, 0].astype(W[tie].dtype)
        return W

    loss_part, grad_x, grads, g_final = _local_step(x, loss_target, get_W, attn_norm, final_norm, stage)
    after = grad_x
    for l, group, gnames, slabs, started in exchanges:
        lands = _split_wait(started, None, True, after, f"exchange_wait_{group}_l{l}")
        grads[l].update(_sum_layer_grads(dict(zip(gnames, lands)), l))
    gw = {n: jnp.stack([grads[l][n] for l in range(DEPTH)]) for n in SHARDED}

    small = ("attn_norm", "b_forget", "b_gate", "pool_w", "pool_scale", "ffn_norm", "conv_w")
    parts = [jnp.stack([grads[l][n] for l in range(DEPTH)]) for n in small] + [g_final, loss_part]
    gathered = _direct_gather(parts, "gather_small_grads")
    summed = _sum_small(gathered, "sum_small_grads")
    for n, s in zip(small, summed):
        gw[n] = s
    gw["attn_norm"], gw["ffn_norm"] = gw["attn_norm"][:, 0], gw["ffn_norm"][:, 0]
    gw["b_forget"] = gw["b_forget"][:, 0, :N_HEADS]
    gw["b_gate"], gw["pool_scale"] = gw["b_gate"][:, 0], gw["pool_scale"][:, 0]
    gw["conv_w"] = lax.dynamic_slice_in_dim(gw["conv_w"], me * (BRANCH_W // N_DEV), BRANCH_W // N_DEV, axis=2)
    gw["final_norm"] = summed[-2][0]
    loss = summed[-1][0, 0]

    deltas, new_m, new_v = {}, {}, {}
    for n in SHARDED:
        shape = weights[n].shape
        d, nm, nv = _adamw(_as_2d(weights[n]), _as_2d(gw[n]), _as_2d(moments_m[n]), _as_2d(moments_v[n]),
                           "adamw_" + n)
        deltas[n], new_m[n], new_v[n] = d.reshape(shape), nm.reshape(shape), nv.reshape(shape)
    rest_names = [n for n in WEIGHT_ORDER if n not in SHARDED]
    ds, nms, nvs = _adamw_many(*[[_as_2d(src[n]) for n in rest_names] for src in (weights, gw, moments_m, moments_v)],
                               "adamw_small")
    for n, d, nm, nv in zip(rest_names, ds, nms, nvs):
        shape = weights[n].shape
        deltas[n], new_m[n], new_v[n] = d.reshape(shape), nm.reshape(shape), nv.reshape(shape)

    return (loss, grad_x, *[gw[n] for n in WEIGHT_ORDER], *[deltas[n] for n in WEIGHT_ORDER],
            *[new_m[n] for n in WEIGHT_ORDER], *[new_v[n] for n in WEIGHT_ORDER])
```

```python
import functools

import jax
import jax.numpy as jnp
from jax import lax
from jax.experimental import pallas as pl
from jax.experimental.pallas import tpu as pltpu

F32 = jnp.float32
BF16 = jnp.bfloat16

N_DEV = 8
D_MODEL = 1024
DEPTH = 2
N_HEADS = 8
HEAD_DIM = 64
BRANCH_W = 512
POOL_WINDOWS = (2, 4, 8, 16)
POOL_GD = 128
CONV_K = 3
FFN_HIDDEN = 2816
GATE_W = 3 * D_MODEL
IN_COLS = 6664
MAIN_COLS = GATE_W + 7 * BRANCH_W
RMS_EPS = 1e-6
NEG_INF = -1e30

ADAM_LR = 0.001
ADAM_B1 = 0.9
ADAM_B2 = 0.999
ADAM_EPS = 1e-08
ADAM_WD = 0.01
ADAM_STEP = 10

LANES = 128
VMEM_LIMIT = 56 * 1024 * 1024
CUM_BLK = 256

TRIPLE = 3 * LANES
OFF_G, OFF_QKV, OFF_CONV, OFF_U = 0, 3072, 4608, 6144


def _cp(sem=None):
    return pltpu.CompilerParams(dimension_semantics=sem, vmem_limit_bytes=VMEM_LIMIT)


def _sigmoid(z):
    return 1.0 / (1.0 + jnp.exp(-z))


def _matmul(a, b, *, mode, out_dtype, name, tm=2048, tn=512, tk=None, residual=None, rms_g=None, side=None,
            extra=None):
    if mode == "nn":
        (M, K), N = a.shape, b.shape[1]
    elif mode == "nt":
        (M, K), N = a.shape, b.shape[0]
    else:
        (K, M), N = a.shape, b.shape[1]
    tm, tn, tk = min(tm, M), min(tn, N), K if tk is None else min(tk, K)
    assert M % tm == 0 and N % tn == 0 and K % tk == 0, (name, M, N, K, tm, tn, tk)
    nk = K // tk
    if mode == "nn":
        a_spec = pl.BlockSpec((tm, tk), lambda i, j, k: (i, k))
        b_spec = pl.BlockSpec((tk, tn), lambda i, j, k: (k, j))
        dims = (((1,), (0,)), ((), ()))
    elif mode == "nt":
        a_spec = pl.BlockSpec((tm, tk), lambda i, j, k: (i, k))
        b_spec = pl.BlockSpec((tn, tk), lambda i, j, k: (j, k))
        dims = (((1,), (1,)), ((), ()))
    else:
        a_spec = pl.BlockSpec((tk, tm), lambda i, j, k: (k, i))
        b_spec = pl.BlockSpec((tk, tn), lambda i, j, k: (k, j))
        dims = (((0,), (0,)), ((), ()))
    o_spec = pl.BlockSpec((tm, tn), lambda i, j, k: (i, j))
    has_res, has_norm, has_side, has_extra = (v is not None for v in (residual, rms_g, side, extra))
    assert not has_norm or tn == N, (name, tn, N)
    assert not has_side or (nk == 1 and mode != "nt"), name

    in_specs, args = [a_spec, b_spec], [a, b]
    out_specs, out_shape = [o_spec], [jax.ShapeDtypeStruct((M, N), out_dtype)]
    if has_res:
        in_specs.append(o_spec)
        args.append(residual)
    if has_norm:
        in_specs.append(pl.BlockSpec((1, N), lambda i, j, k: (0, 0)))
        args.append(rms_g.reshape(1, N))
        out_specs.append(o_spec)
        out_shape.append(jax.ShapeDtypeStruct((M, N), BF16))
    if has_side:
        b_side, side_dtype = side
        ns = b_side.shape[1]
        in_specs.append(pl.BlockSpec((K, ns), lambda i, j, k: (0, 0)))
        args.append(b_side)
        out_specs.append(pl.BlockSpec((tm, ns), lambda i, j, k: (i, 0)))
        out_shape.append(jax.ShapeDtypeStruct((M, ns), side_dtype))
    if has_extra:
        a2, b2 = extra
        in_specs += [pl.BlockSpec((tm, a2.shape[1]), lambda i, j, k: (i, 0)),
                     pl.BlockSpec((tn, b2.shape[1]), lambda i, j, k: (j, 0))]
        args += [a2, b2]
    n_in = len(args)

    def body(*refs):
        ins, outs = list(refs[2:n_in]), list(refs[n_in:n_in + len(out_shape)])
        a_ref, b_ref = refs[:2]
        r_ref = ins.pop(0) if has_res else None
        g_ref = ins.pop(0) if has_norm else None
        bs_ref = ins.pop(0) if has_side else None
        a2_ref, b2_ref = (ins.pop(0), ins.pop(0)) if has_extra else (None, None)
        o_ref = outs.pop(0)
        h_ref = outs.pop(0) if has_norm else None
        so_ref = outs.pop(0) if has_side else None

        def finish(acc):
            if has_res:
                acc = acc + r_ref[...].astype(F32)
            if has_extra:
                acc = acc + lax.dot_general(a2_ref[...], b2_ref[...], (((1,), (1,)), ((), ())),
                                            preferred_element_type=F32)
            o_ref[...] = acc.astype(out_dtype)
            if has_norm:
                r = lax.rsqrt(jnp.mean(acc * acc, axis=-1, keepdims=True) + RMS_EPS)
                h_ref[...] = ((acc * r) * g_ref[...]).astype(BF16)

        if has_side:
            @pl.when(pl.program_id(1) == 0)
            def _():
                side_dims = (((1,), (0,)), ((), ())) if mode == "nn" else dims
                so_ref[...] = lax.dot_general(a_ref[...], bs_ref[...], side_dims,
                                              preferred_element_type=F32).astype(so_ref.dtype)

        prod = lax.dot_general(a_ref[...], b_ref[...], dims, preferred_element_type=F32)
        if nk == 1:
            finish(prod)
            return
        acc_ref = refs[-1]
        k = pl.program_id(2)

        @pl.when(k == 0)
        def _():
            acc_ref[...] = prod

        @pl.when(jnp.logical_and(k > 0, k < nk - 1))
        def _():
            acc_ref[...] += prod

        @pl.when(k == nk - 1)
        def _():
            finish(acc_ref[...] + prod)

    single = len(out_shape) == 1
    return pl.pallas_call(
        body, name=name, grid=(M // tm, N // tn, nk), in_specs=in_specs,
        out_specs=out_specs[0] if single else out_specs, out_shape=out_shape[0] if single else out_shape,
        scratch_shapes=[pltpu.VMEM((tm, tn), F32)] if nk > 1 else [],
        compiler_params=_cp(("parallel", "arbitrary" if has_side else "parallel", "arbitrary")),
    )(*args)


def _rms_fwd(x, g, name):
    T, Dm = x.shape
    tm = min(512, T)

    def body(x_ref, g_ref, h_ref):
        xf = x_ref[...]
        r = lax.rsqrt(jnp.mean(xf * xf, axis=-1, keepdims=True) + RMS_EPS)
        h_ref[...] = ((xf * r) * g_ref[...]).astype(BF16)

    return pl.pallas_call(
        body, name=name, grid=(T // tm,),
        in_specs=[pl.BlockSpec((tm, Dm), lambda i: (i, 0)), pl.BlockSpec((1, Dm), lambda i: (0, 0))],
        out_specs=pl.BlockSpec((tm, Dm), lambda i: (i, 0)),
        out_shape=jax.ShapeDtypeStruct((T, Dm), BF16),
        compiler_params=_cp(("parallel",)),
    )(x, g.reshape(1, Dm))


def _rms_bwd(x, g, dh, dres, name):
    T, Dm = x.shape
    tm = min(512, T)

    def body(x_ref, g_ref, dh_ref, dres_ref, dx_ref, dxb_ref, dg_ref):
        i = pl.program_id(0)
        xf = x_ref[...]
        r = lax.rsqrt(jnp.mean(xf * xf, axis=-1, keepdims=True) + RMS_EPS)
        xn = xf * r
        dhf = dh_ref[...].astype(F32)
        dxn = dhf * g_ref[...]
        c = jnp.mean(dxn * xn, axis=-1, keepdims=True)
        dx = dres_ref[...] + r * (dxn - xn * c)
        dx_ref[...] = dx
        dxb_ref[...] = dx.astype(BF16)
        part = jnp.sum(dhf * xn, axis=0, keepdims=True)

        @pl.when(i == 0)
        def _():
            dg_ref[...] = part

        @pl.when(i > 0)
        def _():
            dg_ref[...] += part

    row = pl.BlockSpec((tm, Dm), lambda i: (i, 0))
    vec = pl.BlockSpec((1, Dm), lambda i: (0, 0))
    return pl.pallas_call(
        body, name=name, grid=(T // tm,), in_specs=[row, vec, row, row], out_specs=[row, row, vec],
        out_shape=[jax.ShapeDtypeStruct((T, Dm), F32), jax.ShapeDtypeStruct((T, Dm), BF16),
                   jax.ShapeDtypeStruct((1, Dm), F32)],
        compiler_params=_cp(("arbitrary",)),
    )(x, g.reshape(1, Dm), dh, dres)


def _loss_head(x, g, target, name):
    T, Dm = x.shape
    tm = min(512, T)

    def body(x_ref, g_ref, t_ref, loss_ref, dx_ref, dxb_ref, dg_ref):
        i = pl.program_id(0)
        xf = x_ref[...]
        gv = g_ref[...]
        r = lax.rsqrt(jnp.mean(xf * xf, axis=-1, keepdims=True) + RMS_EPS)
        xn = xf * r
        diff = xn * gv - t_ref[...]
        per_tok = jnp.mean(diff * diff, axis=-1, keepdims=True)
        lpart = 0.5 * jnp.sum(per_tok, axis=0, keepdims=True) + jnp.zeros((1, LANES), F32)
        dy = diff * (1.0 / Dm)
        dxn = dy * gv
        c = jnp.mean(dxn * xn, axis=-1, keepdims=True)
        dx = r * (dxn - xn * c)
        dx_ref[...] = dx
        dxb_ref[...] = dx.astype(BF16)
        part = jnp.sum(dy * xn, axis=0, keepdims=True)

        @pl.when(i == 0)
        def _():
            dg_ref[...] = part
            loss_ref[...] = lpart

        @pl.when(i > 0)
        def _():
            dg_ref[...] += part
            loss_ref[...] += lpart

    row = pl.BlockSpec((tm, Dm), lambda i: (i, 0))
    vec = pl.BlockSpec((1, Dm), lambda i: (0, 0))
    lsp = pl.BlockSpec((1, LANES), lambda i: (0, 0))
    return pl.pallas_call(
        body, name=name, grid=(T // tm,), in_specs=[row, vec, row], out_specs=[lsp, row, row, vec],
        out_shape=[jax.ShapeDtypeStruct((1, LANES), F32), jax.ShapeDtypeStruct((T, Dm), F32),
                   jax.ShapeDtypeStruct((T, Dm), BF16), jax.ShapeDtypeStruct((1, Dm), F32)],
        compiler_params=_cp(("arbitrary",)),
    )(x, g.reshape(1, Dm), target)


def _split_bf16(v):
    hi = v.astype(BF16)
    r1 = v - hi.astype(F32)
    mid = r1.astype(BF16)
    lo = (r1 - mid.astype(F32)).astype(BF16)
    return hi, mid, lo


def _tri_dot(tri, v):
    hi, mid, lo = _split_bf16(v)
    dot = functools.partial(jnp.dot, preferred_element_type=F32)
    return dot(tri, hi) + dot(tri, mid) + dot(tri, lo)


def _log_sigmoid(z):
    return jnp.minimum(z, 0.0) - jnp.log(1.0 + jnp.exp(-jnp.abs(z)))


def _fox_cumsum_bwd(f, bf, dF, n_seq, name):
    T = f.shape[0]
    S = T // n_seq
    c = min(CUM_BLK, S)

    def body(f_ref, b_ref, dF_ref, df_ref, db_ref):
        b = pl.program_id(0)
        ri = lax.broadcasted_iota(jnp.int32, (c, c), 0)
        ci = lax.broadcasted_iota(jnp.int32, (c, c), 1)
        tri = (ri <= ci).astype(BF16)
        carry = jnp.zeros((1, LANES), F32)
        dbp = jnp.zeros((1, LANES), F32)
        for j in reversed(range(S // c)):
            dFc = dF_ref[j * c:(j + 1) * c, :]
            dlf = _tri_dot(tri, dFc) + carry
            carry = carry + jnp.sum(dFc, axis=0, keepdims=True)
            z = f_ref[j * c:(j + 1) * c, :] + b_ref[...]
            dz = dlf * _sigmoid(-z)
            df_ref[j * c:(j + 1) * c, :] = dz.astype(BF16)
            dbp = dbp + jnp.sum(dz, axis=0, keepdims=True)

        @pl.when(b == 0)
        def _():
            db_ref[...] = dbp

        @pl.when(b > 0)
        def _():
            db_ref[...] += dbp

    blk = pl.BlockSpec((S, LANES), lambda b: (b, 0))
    vec = pl.BlockSpec((1, LANES), lambda b: (0, 0))
    return pl.pallas_call(
        body, name=name, grid=(n_seq,), in_specs=[blk, vec, blk], out_specs=[blk, vec],
        out_shape=[jax.ShapeDtypeStruct((T, LANES), BF16), jax.ShapeDtypeStruct((1, LANES), F32)],
        compiler_params=_cp(("arbitrary",)),
    )(f, bf, dF)


def _pair_masks():
    lane = lax.broadcasted_iota(jnp.int32, (1, LANES), 1)
    lo = lane < HEAD_DIM
    return lo, jnp.logical_not(lo)


AUG0 = HEAD_DIM
Q_TILE, K_CHUNK, ROW_GROUP = 512, 256, 64


def _fox_prep(f, bf, proj, n_seq, name):
    T = f.shape[0]
    S = T // n_seq
    c = min(CUM_BLK, S)

    def body(f_ref, b_ref, qkv_ref, qa_ref, ka_ref, va_ref):
        ri = lax.broadcasted_iota(jnp.int32, (c, c), 0)
        ci = lax.broadcasted_iota(jnp.int32, (c, c), 1)
        tri = (ri >= ci).astype(BF16)
        lane = lax.broadcasted_iota(jnp.int32, (c, LANES), 1)
        carry = jnp.zeros((1, LANES), F32)
        for j in range(S // c):
            rows = slice(j * c, (j + 1) * c)
            lf = _log_sigmoid(f_ref[rows, :] + b_ref[...])
            Fc = _tri_dot(tri, lf) + carry
            carry = carry + jnp.sum(lf, axis=0, keepdims=True)
            for h in range(N_HEADS):
                col = jnp.sum(jnp.where(lane == h, Fc, 0.0), axis=-1, keepdims=True)
                hi = col.astype(BF16).astype(F32)
                r1 = col - hi
                mid = r1.astype(BF16).astype(F32)
                lo = r1 - mid
                ones_q = jnp.logical_and(lane >= AUG0 + 3, lane < AUG0 + 6)
                ones_k = jnp.logical_and(lane >= AUG0, lane < AUG0 + 3)
                aug_q = jnp.where(lane == AUG0, hi, jnp.where(lane == AUG0 + 1, mid, jnp.where(
                    lane == AUG0 + 2, lo, jnp.where(ones_q, 1.0, 0.0))))
                aug_k = jnp.where(lane == AUG0 + 3, -hi, jnp.where(lane == AUG0 + 4, -mid, jnp.where(
                    lane == AUG0 + 5, -lo, jnp.where(ones_k, 1.0, 0.0))))
                base = (h // 2) * TRIPLE
                qp, kp, vp = (qkv_ref[rows, base + t * LANES:base + (t + 1) * LANES].astype(F32) for t in range(3))
                if h % 2:
                    qp, kp, vp = (pltpu.roll(a, HEAD_DIM, 1) for a in (qp, kp, vp))
                out = slice(h * LANES, (h + 1) * LANES)
                qa_ref[rows, out] = jnp.where(lane < HEAD_DIM, qp * (HEAD_DIM ** -0.5), aug_q).astype(BF16)
                ka_ref[rows, out] = jnp.where(lane < HEAD_DIM, kp, aug_k).astype(BF16)
                va_ref[rows, out] = jnp.where(lane < HEAD_DIM, vp, jnp.where(lane == AUG0, 1.0, 0.0)).astype(BF16)

    fblk = pl.BlockSpec((S, LANES), lambda b: (b, 0))
    out = pl.BlockSpec((S, N_HEADS * LANES), lambda b: (b, 0))
    sh = jax.ShapeDtypeStruct((T, N_HEADS * LANES), BF16)
    return pl.pallas_call(
        body, name=name, grid=(n_seq,),
        in_specs=[fblk, pl.BlockSpec((1, LANES), lambda b: (0, 0)),
                  pl.BlockSpec((S, 4 * TRIPLE), lambda b: (b, OFF_QKV // (4 * TRIPLE)))],
        out_specs=[out, out, out], out_shape=[sh, sh, sh],
        compiler_params=_cp(("parallel",)),
    )(f, bf, proj)


def _band_mask(q0, k0, nq, nk):
    row = q0 + lax.broadcasted_iota(jnp.int32, (nq, nk), 0)
    col = k0 + lax.broadcasted_iota(jnp.int32, (nq, nk), 1)
    return col <= row


_NT = (((1,), (1,)), ((), ()))
_TN = (((0,), (0,)), ((), ()))


def _attn_fwd2(qa, ka, va, n_seq, name):
    T = qa.shape[0]
    S = T // n_seq
    tq, tk, rg = min(Q_TILE, S), min(K_CHUNK, S), ROW_GROUP
    nq, per = S // tq, tq // tk

    def body(q_ref, k_ref, v_ref, o_ref, o32_ref, lse_ref, phi_s, plo_s, mp_s, m_s, acc_s):
        qi = pl.program_id(2)
        mp_s[...] = jnp.full_like(mp_s, NEG_INF)
        acc_s[...] = jnp.zeros_like(acc_s)

        def scores(kc, hh, r0):
            k0 = pl.multiple_of(kc * tk, tk)
            hl = slice(hh * LANES, (hh + 1) * LANES)
            return k0, lax.dot_general(q_ref[r0:, hl], k_ref[pl.ds(k0, tk), hl], _NT, preferred_element_type=F32)

        def max_chunk(kc, masked, r0):
            for hh in range(2):
                k0, s_all = scores(kc, hh, r0)
                for r in range(r0 // rg, tq // rg):
                    rows = slice(r * rg, (r + 1) * rg)
                    s = s_all[r * rg - r0:(r + 1) * rg - r0, :]
                    if masked:
                        s = jnp.where(_band_mask(qi * tq + r * rg, k0, rg, tk), s, NEG_INF)
                    part = s[:, :LANES]
                    for c in range(1, tk // LANES):
                        part = jnp.maximum(part, s[:, c * LANES:(c + 1) * LANES])
                    mp_s[hh, rows, :] = jnp.maximum(mp_s[hh, rows, :], part)

        def sum_chunk(kc, masked, r0):
            for hh in range(2):
                k0, s_all = scores(kc, hh, r0)
                hl = slice(hh * LANES, (hh + 1) * LANES)
                v = v_ref[pl.ds(k0, tk), hl]
                for r in range(r0 // rg, tq // rg):
                    rows = slice(r * rg, (r + 1) * rg)
                    p = jnp.exp(s_all[r * rg - r0:(r + 1) * rg - r0, :] - m_s[hh, rows])
                    if masked:
                        p = jnp.where(_band_mask(qi * tq + r * rg, k0, rg, tk), p, 0.0)
                    p_hi = p.astype(BF16)
                    phi_s[hh, rows, :] = p_hi
                    plo_s[hh, rows, :] = (p - p_hi.astype(F32)).astype(BF16)
                acc_s[hh, r0:, :] += (jnp.dot(phi_s[hh, r0:, :], v, preferred_element_type=F32)
                                      + jnp.dot(plo_s[hh, r0:, :], v, preferred_element_type=F32))

        def sweep(chunk):
            def unmasked(kc, carry):
                chunk(kc, False, 0)
                return carry

            lax.fori_loop(0, qi * per, unmasked, 0)
            for d in range(per):
                chunk(qi * per + d, True, d * tk)

        sweep(max_chunk)
        m_s[...] = jnp.max(mp_s[...], axis=-1, keepdims=True)
        sweep(sum_chunk)

        lane = lax.broadcasted_iota(jnp.int32, (1, LANES), 1)
        outs = []
        for hh in range(2):
            acc = acc_s[hh]
            l = jnp.sum(jnp.where(lane == AUG0, acc, 0.0), axis=-1, keepdims=True)
            lse_ref[hh] = m_s[hh] + jnp.log(l)
            outs.append(acc / l)
        o = jnp.where(lane < HEAD_DIM, outs[0], pltpu.roll(outs[1], HEAD_DIM, 1))
        o_ref[...] = o.astype(BF16)
        o32_ref[...] = o

    qmap = lambda b, j, qi: (b * nq + qi, j)
    omap = lambda b, j, qi: (b * nq + qi, j)
    kv = pl.BlockSpec((S, 2 * LANES), lambda b, j, qi: (b, j))
    return pl.pallas_call(
        body, name=name, grid=(n_seq, N_HEADS // 2, nq),
        in_specs=[pl.BlockSpec((tq, 2 * LANES), qmap), kv, kv],
        out_specs=[pl.BlockSpec((tq, LANES), omap), pl.BlockSpec((tq, LANES), omap),
                   pl.BlockSpec((2, tq, 1), lambda b, j, qi: (j, b * nq + qi, 0))],
        out_shape=[jax.ShapeDtypeStruct((T, BRANCH_W), BF16), jax.ShapeDtypeStruct((T, BRANCH_W), F32),
                   jax.ShapeDtypeStruct((N_HEADS, T, 1), F32)],
        scratch_shapes=[pltpu.VMEM((2, tq, tk), BF16), pltpu.VMEM((2, tq, tk), BF16),
                        pltpu.VMEM((2, tq, LANES), F32), pltpu.VMEM((2, tq, 1), F32),
                        pltpu.VMEM((2, tq, LANES), F32)],
        compiler_params=_cp(("parallel", "parallel", "parallel")),
    )(qa, ka, va)


def _attn_bwd(qa, ka, proj, do, o32, lse, dproj, n_seq, name):
    T = qa.shape[0]
    S = T // n_seq
    tq, tk, rg = min(Q_TILE, S), min(K_CHUNK, S), ROW_GROUP
    nq, per, nkc = S // tq, tq // tk, S // tk

    def body(q_ref, k_ref, v_ref, do_ref, o_ref, lse_ref, _, dqkv_ref, dfk_ref,
             p_s, ds_s, dq_s, dk_s, dv_s, df_s):
        dk_s[...] = jnp.zeros_like(dk_s)
        dv_s[...] = jnp.zeros_like(dv_s)
        df_s[...] = jnp.zeros_like(df_s)
        sels = _pair_masks()

        for qi in range(nq):
            q0 = qi * tq
            do_t = do_ref[q0:q0 + tq, :]
            dq_s[...] = jnp.zeros_like(dq_s)
            prod = do_t.astype(F32) * o_ref[q0:q0 + tq, :]
            dls = [jnp.sum(jnp.where(sel, prod, 0.0), axis=-1, keepdims=True) for sel in sels]

            def chunk(kc, masked, r0, q0=q0, do_t=do_t, dls=dls):
                k0 = pl.multiple_of(kc * tk, tk)
                v = v_ref[pl.ds(k0, tk), :]
                do_a = do_t[r0:, :]
                for hh in range(2):
                    hl = slice(hh * LANES, (hh + 1) * LANES)
                    qh, kh = q_ref[q0 + r0:q0 + tq, hl], k_ref[pl.ds(k0, tk), hl]
                    s_all = lax.dot_general(qh, kh, _NT, preferred_element_type=F32)
                    dom = jnp.where(sels[hh], do_a, jnp.zeros_like(do_a))
                    dp_all = lax.dot_general(dom, v, _NT, preferred_element_type=F32)
                    dfp = jnp.zeros((1, tk), F32)
                    for r in range(r0 // rg, tq // rg):
                        rows = slice(r * rg, (r + 1) * rg)
                        arows = slice(r * rg - r0, (r + 1) * rg - r0)
                        qrows = slice(q0 + r * rg, q0 + (r + 1) * rg)
                        p = jnp.exp(s_all[arows, :] - lse_ref[hh, qrows])
                        if masked:
                            p = jnp.where(_band_mask(q0 + r * rg, k0, rg, tk), p, 0.0)
                        ds = p * (dp_all[arows, :] - dls[hh][rows])
                        p_s[hh, rows, :] = p.astype(BF16)
                        ds_s[hh, rows, :] = ds.astype(BF16)
                        dfp = dfp + jnp.sum(ds, axis=0, keepdims=True)
                    df_s[hh, kc] -= dfp
                    dq_s[hh, r0:, :] += jnp.dot(ds_s[hh, r0:, :], kh, preferred_element_type=F32)
                    dv_s[hh, pl.ds(k0, tk), :] += lax.dot_general(p_s[hh, r0:, :], do_a, _TN,
                                                                  preferred_element_type=F32)
                    dk_s[hh, pl.ds(k0, tk), :] += lax.dot_general(ds_s[hh, r0:, :], qh, _TN,
                                                                  preferred_element_type=F32)

            def unmasked(kc, carry, chunk=chunk):
                chunk(kc, False, 0)
                return carry

            lax.fori_loop(0, qi * per, unmasked, 0)
            for d in range(per):
                chunk(qi * per + d, True, d * tk)
            dq = jnp.where(sels[0], dq_s[0], pltpu.roll(dq_s[1], HEAD_DIM, 1))
            dqkv_ref[q0:q0 + tq, :LANES] = (dq * (HEAD_DIM ** -0.5)).astype(BF16)

        dqkv_ref[:, LANES:2 * LANES] = jnp.where(sels[0], dk_s[0], pltpu.roll(dk_s[1], HEAD_DIM, 1)).astype(BF16)
        dqkv_ref[:, 2 * LANES:] = jnp.where(sels[0], dv_s[0], dv_s[1]).astype(BF16)
        for c in range(nkc):
            dfk_ref[:, :, c * tk:(c + 1) * tk] = df_s[:, c]

    seq = lambda w: pl.BlockSpec((S, w), lambda b, j: (b, j))
    col1 = pl.BlockSpec((2, S, 1), lambda b, j: (j, b, 0))
    vblk = pl.BlockSpec((S, LANES), lambda b, j: (b, OFF_QKV // LANES + 3 * j + 2))
    return pl.pallas_call(
        body, name=name, grid=(n_seq, N_HEADS // 2),
        in_specs=[seq(2 * LANES), seq(2 * LANES), vblk, seq(LANES), seq(LANES), col1,
                  pl.BlockSpec(memory_space=pl.ANY)],
        out_specs=[pl.BlockSpec((S, TRIPLE), lambda b, j: (b, OFF_QKV // TRIPLE + j)),
                   pl.BlockSpec((2, 1, S), lambda b, j: (j, 0, b))],
        out_shape=[jax.ShapeDtypeStruct(dproj.shape, BF16), jax.ShapeDtypeStruct((N_HEADS, 1, T), F32)],
        input_output_aliases={6: 0},
        scratch_shapes=[pltpu.VMEM((2, tq, tk), BF16), pltpu.VMEM((2, tq, tk), BF16),
                        pltpu.VMEM((2, tq, LANES), F32), pltpu.VMEM((2, S, LANES), F32),
                        pltpu.VMEM((2, S, LANES), F32), pltpu.VMEM((2, nkc, 1, tk), F32)],
        compiler_params=_cp(("parallel", "parallel")),
    )(qa, ka, proj, do, o32, lse, dproj)


def _shift_down(v, k, row):
    return jnp.where(row >= k, pltpu.roll(v, k, 0), 0.0)


def _shift_up(v, k, row, S):
    return jnp.where(row < S - k, pltpu.roll(v, S - k, 0), 0.0)


def _pool_diff(uf, w, row):
    acc, k = uf, 1
    while k < w:
        acc = acc + _shift_down(acc, k, row)
        k *= 2
    n = jnp.minimum(row + 1, w).astype(F32)
    return acc / n - uf


def _pool_fwd(proj, pool_w, pool_scale, n_seq, name):
    T = proj.shape[0]
    S = T // n_seq

    def body(u_ref, w_ref, sc_ref, o_ref, d_s):
        g = pl.program_id(1)
        row = lax.broadcasted_iota(jnp.int32, (S, POOL_GD), 0)
        uf = u_ref[...].astype(F32)
        for gi, wlen in enumerate(POOL_WINDOWS):
            @pl.when(g == gi)
            def _(wlen=wlen):
                d_s[...] = _pool_diff(uf, wlen, row).astype(BF16)
        e = jnp.dot(d_s[...], w_ref[0], preferred_element_type=F32)
        o_ref[...] = (e * sc_ref[...]).astype(BF16)

    uc = OFF_U // POOL_GD
    return pl.pallas_call(
        body, name=name, grid=(n_seq, len(POOL_WINDOWS)),
        in_specs=[pl.BlockSpec((S, POOL_GD), lambda b, g: (b, uc + g)),
                  pl.BlockSpec((1, POOL_GD, POOL_GD), lambda b, g: (g, 0, 0)),
                  pl.BlockSpec((1, POOL_GD), lambda b, g: (0, g))],
        out_specs=pl.BlockSpec((S, POOL_GD), lambda b, g: (b, g)),
        out_shape=jax.ShapeDtypeStruct((T, BRANCH_W), BF16),
        scratch_shapes=[pltpu.VMEM((S, POOL_GD), BF16)],
        compiler_params=_cp(("parallel", "parallel")),
    )(proj, pool_w, pool_scale)


def _pool_bwd(proj, dout, pool_w, pool_scale, dproj, n_seq, name):
    T = proj.shape[0]
    S = T // n_seq

    def body(u_ref, do_ref, w_ref, sc_ref, _, du_ref, dw_ref, dsc_ref, d_s):
        g, b = pl.program_id(0), pl.program_id(1)
        row = lax.broadcasted_iota(jnp.int32, (S, POOL_GD), 0)
        uf = u_ref[...].astype(F32)
        for gi, wlen in enumerate(POOL_WINDOWS):
            @pl.when(g == gi)
            def _(wlen=wlen):
                d_s[...] = _pool_diff(uf, wlen, row).astype(BF16)
        db16 = d_s[...]
        w = w_ref[0]
        e = jnp.dot(db16, w, preferred_element_type=F32)
        dof = do_ref[...].astype(F32)
        dsc = jnp.sum(dof * e, axis=0, keepdims=True)
        de = (dof * sc_ref[...]).astype(BF16)
        dd = lax.dot_general(de, w, (((1,), (1,)), ((), ())), preferred_element_type=F32)
        dw = lax.dot_general(db16, de, (((0,), (0,)), ((), ())), preferred_element_type=F32)
        for gi, wlen in enumerate(POOL_WINDOWS):
            @pl.when(g == gi)
            def _(wlen=wlen):
                n = jnp.minimum(row + 1, wlen).astype(F32)
                acc, k = dd / n, 1
                while k < wlen:
                    acc = acc + _shift_up(acc, k, row, S)
                    k *= 2
                du_ref[...] = (acc - dd).astype(BF16)

        @pl.when(b == 0)
        def _():
            dw_ref[0] = dw
            dsc_ref[...] = dsc

        @pl.when(b > 0)
        def _():
            dw_ref[0] += dw
            dsc_ref[...] += dsc

    uc = OFF_U // POOL_GD
    return pl.pallas_call(
        body, name=name, grid=(len(POOL_WINDOWS), n_seq),
        in_specs=[pl.BlockSpec((S, POOL_GD), lambda g, b: (b, uc + g)),
                  pl.BlockSpec((S, POOL_GD), lambda g, b: (b, g)),
                  pl.BlockSpec((1, POOL_GD, POOL_GD), lambda g, b: (g, 0, 0)),
                  pl.BlockSpec((1, POOL_GD), lambda g, b: (0, g)),
                  pl.BlockSpec(memory_space=pl.ANY)],
        out_specs=[pl.BlockSpec((S, POOL_GD), lambda g, b: (b, uc + g)),
                   pl.BlockSpec((1, POOL_GD, POOL_GD), lambda g, b: (g, 0, 0)),
                   pl.BlockSpec((1, POOL_GD), lambda g, b: (0, g))],
        out_shape=[jax.ShapeDtypeStruct(dproj.shape, BF16),
                   jax.ShapeDtypeStruct((len(POOL_WINDOWS), POOL_GD, POOL_GD), F32),
                   jax.ShapeDtypeStruct((1, BRANCH_W), F32)],
        input_output_aliases={4: 0},
        scratch_shapes=[pltpu.VMEM((S, POOL_GD), BF16)],
        compiler_params=_cp(("parallel", "arbitrary")),
    )(proj, dout, pool_w, pool_scale, dproj)


def _conv_fwd(proj, conv_w, n_seq, name):
    T = proj.shape[0]
    S = T // n_seq
    nc = BRANCH_W // LANES

    def body(c_ref, w_ref, o_ref):
        row = lax.broadcasted_iota(jnp.int32, (S, LANES), 0)
        cv, cb, cc = (c_ref[:, t * LANES:(t + 1) * LANES].astype(F32) for t in range(3))
        z = cc * cv
        w = w_ref[...]
        y = w[0:1] * _shift_down(z, 2, row) + w[1:2] * _shift_down(z, 1, row) + w[2:3] * z
        o_ref[...] = (cb * y).astype(BF16)

    return pl.pallas_call(
        body, name=name, grid=(n_seq, nc),
        in_specs=[pl.BlockSpec((S, TRIPLE), lambda b, j: (b, OFF_CONV // TRIPLE + j)),
                  pl.BlockSpec((CONV_K, LANES), lambda b, j: (0, j))],
        out_specs=pl.BlockSpec((S, LANES), lambda b, j: (b, j)),
        out_shape=jax.ShapeDtypeStruct((T, BRANCH_W), BF16),
        compiler_params=_cp(("parallel", "parallel")),
    )(proj, conv_w)


def _conv_bwd(proj, dout, conv_w, dproj, n_seq, name):
    T = proj.shape[0]
    S = T // n_seq
    nc = BRANCH_W // LANES

    def body(c_ref, do_ref, w_ref, _, dc_ref, dw_ref):
        b = pl.program_id(1)
        row = lax.broadcasted_iota(jnp.int32, (S, LANES), 0)
        cv, cb, cc = (c_ref[:, t * LANES:(t + 1) * LANES].astype(F32) for t in range(3))
        dof = do_ref[...].astype(F32)
        w = w_ref[...]
        z = cc * cv
        z1, z2 = _shift_down(z, 1, row), _shift_down(z, 2, row)
        y = w[0:1] * z2 + w[1:2] * z1 + w[2:3] * z
        dy = dof * cb
        dz = w[2:3] * dy + w[1:2] * _shift_up(dy, 1, row, S) + w[0:1] * _shift_up(dy, 2, row, S)
        dc_ref[:, :LANES] = (dz * cc).astype(BF16)
        dc_ref[:, LANES:2 * LANES] = (dof * y).astype(BF16)
        dc_ref[:, 2 * LANES:] = (dz * cv).astype(BF16)
        dws = [jnp.sum(dy * zk, axis=0, keepdims=True) for zk in (z2, z1, z)]

        @pl.when(b == 0)
        def _():
            for kk in range(CONV_K):
                dw_ref[kk:kk + 1, :] = dws[kk]

        @pl.when(b > 0)
        def _():
            for kk in range(CONV_K):
                dw_ref[kk:kk + 1, :] += dws[kk]

    triple = pl.BlockSpec((S, TRIPLE), lambda j, b: (b, OFF_CONV // TRIPLE + j))
    wsp = pl.BlockSpec((CONV_K, LANES), lambda j, b: (0, j))
    return pl.pallas_call(
        body, name=name, grid=(nc, n_seq),
        in_specs=[triple, pl.BlockSpec((S, LANES), lambda j, b: (b, j)), wsp, pl.BlockSpec(memory_space=pl.ANY)],
        out_specs=[triple, wsp],
        out_shape=[jax.ShapeDtypeStruct(dproj.shape, BF16), jax.ShapeDtypeStruct((CONV_K, BRANCH_W), F32)],
        input_output_aliases={3: 0},
        compiler_params=_cp(("parallel", "arbitrary")),
    )(proj, dout, conv_w, dproj)


def _mix_fwd(oa, ob, oc, wpa, wpp, wpc, proj, b_gate, name):
    T = oa.shape[0]
    tm = min(256, T)

    def body(oa_ref, ob_ref, oc_ref, wa_ref, wp_ref, wc_ref, g_ref, bg_ref, o_ref):
        acc = jnp.zeros((tm, D_MODEL), F32)
        for i, (x_ref, w_ref) in enumerate(((oa_ref, wa_ref), (ob_ref, wp_ref), (oc_ref, wc_ref))):
            y = jnp.dot(x_ref[...], w_ref[...], preferred_element_type=F32)
            sl = slice(i * D_MODEL, (i + 1) * D_MODEL)
            acc = acc + _sigmoid(g_ref[:, sl].astype(F32) + bg_ref[:, sl]) * y
        o_ref[...] = acc.astype(BF16)

    br = pl.BlockSpec((tm, BRANCH_W), lambda i: (i, 0))
    wsp = pl.BlockSpec((BRANCH_W, D_MODEL), lambda i: (0, 0))
    return pl.pallas_call(
        body, name=name, grid=(T // tm,),
        in_specs=[br, br, br, wsp, wsp, wsp, pl.BlockSpec((tm, GATE_W), lambda i: (i, 0)),
                  pl.BlockSpec((1, GATE_W), lambda i: (0, 0))],
        out_specs=pl.BlockSpec((tm, D_MODEL), lambda i: (i, 0)),
        out_shape=jax.ShapeDtypeStruct((T, D_MODEL), BF16),
        compiler_params=_cp(("parallel",)),
    )(oa, ob, oc, wpa, wpp, wpc, proj, b_gate)


def _mix_bwd(oa, ob, oc, wpa, wpp, wpc, proj, b_gate, dmixed, name):
    T = oa.shape[0]
    tm = min(256, T)

    def body(oa_ref, ob_ref, oc_ref, wa_ref, wp_ref, wc_ref, g_ref, bg_ref, dm_ref,
             dya_ref, dyb_ref, dyc_ref, dg_ref, dbg_ref):
        i0 = pl.program_id(0)
        dm = dm_ref[...].astype(F32)
        parts = []
        for i, (x_ref, w_ref, dy_ref) in enumerate(((oa_ref, wa_ref, dya_ref), (ob_ref, wp_ref, dyb_ref),
                                                    (oc_ref, wc_ref, dyc_ref))):
            y = jnp.dot(x_ref[...], w_ref[...], preferred_element_type=F32)
            sl = slice(i * D_MODEL, (i + 1) * D_MODEL)
            gate = _sigmoid(g_ref[:, sl].astype(F32) + bg_ref[:, sl])
            dy_ref[...] = (dm * gate).astype(BF16)
            dgl = dm * y * gate * (1.0 - gate)
            dg_ref[:, sl] = dgl.astype(BF16)
            parts.append(jnp.sum(dgl, axis=0, keepdims=True))

        @pl.when(i0 == 0)
        def _():
            for i in range(3):
                dbg_ref[:, i * D_MODEL:(i + 1) * D_MODEL] = parts[i]

        @pl.when(i0 > 0)
        def _():
            for i in range(3):
                dbg_ref[:, i * D_MODEL:(i + 1) * D_MODEL] += parts[i]

    br = pl.BlockSpec((tm, BRANCH_W), lambda i: (i, 0))
    wsp = pl.BlockSpec((BRANCH_W, D_MODEL), lambda i: (0, 0))
    row = pl.BlockSpec((tm, D_MODEL), lambda i: (i, 0))
    gsp = pl.BlockSpec((tm, GATE_W), lambda i: (i, 0))
    bsp = pl.BlockSpec((1, GATE_W), lambda i: (0, 0))
    act = jax.ShapeDtypeStruct((T, D_MODEL), BF16)
    return pl.pallas_call(
        body, name=name, grid=(T // tm,),
        in_specs=[br, br, br, wsp, wsp, wsp, gsp, bsp, row],
        out_specs=[row, row, row, gsp, bsp],
        out_shape=[act, act, act, jax.ShapeDtypeStruct((T, MAIN_COLS), BF16),
                   jax.ShapeDtypeStruct((1, GATE_W), F32)],
        compiler_params=_cp(("arbitrary",)),
    )(oa, ob, oc, wpa, wpp, wpc, proj, b_gate, dmixed)


GU_TILE = 256


def _gu_col(c):
    t, r = divmod(c, GU_TILE)
    return (t // 2) * GU_TILE + r + (FFN_HIDDEN if t % 2 else 0)


def _gate_up_swiglu(h, w, name):
    T, K = h.shape
    tm = min(2048, T)

    def body(h_ref, w_ref, ab_ref, s_ref):
        prod = jnp.dot(h_ref[...], w_ref[...], preferred_element_type=F32)
        ab_ref[...] = prod.astype(BF16)
        a = prod[:, :GU_TILE]
        s_ref[...] = (a * _sigmoid(a) * prod[:, GU_TILE:]).astype(BF16)

    return pl.pallas_call(
        body, name=name, grid=(T // tm, FFN_HIDDEN // GU_TILE),
        in_specs=[pl.BlockSpec((tm, K), lambda i, j: (i, 0)), pl.BlockSpec((K, 2 * GU_TILE), lambda i, j: (0, j))],
        out_specs=[pl.BlockSpec((tm, 2 * GU_TILE), lambda i, j: (i, j)), pl.BlockSpec((tm, GU_TILE), lambda i, j: (i, j))],
        out_shape=[jax.ShapeDtypeStruct((T, 2 * FFN_HIDDEN), BF16), jax.ShapeDtypeStruct((T, FFN_HIDDEN), BF16)],
        compiler_params=_cp(("parallel", "parallel")),
    )(h, w)


def _swiglu_bwd_fused(dx, w_down, ab, name):
    T, K = dx.shape
    tm = min(2048, T)

    def body(dx_ref, w_ref, ab_ref, o_ref):
        ds = lax.dot_general(dx_ref[...], w_ref[...], _NT, preferred_element_type=F32)
        a = ab_ref[:, :GU_TILE].astype(F32)
        b = ab_ref[:, GU_TILE:].astype(F32)
        sg = _sigmoid(a)
        o_ref[:, :GU_TILE] = (ds * b * sg * (1.0 + a * (1.0 - sg))).astype(BF16)
        o_ref[:, GU_TILE:] = (ds * a * sg).astype(BF16)

    pair = pl.BlockSpec((tm, 2 * GU_TILE), lambda i, j: (i, j))
    return pl.pallas_call(
        body, name=name, grid=(T // tm, FFN_HIDDEN // GU_TILE),
        in_specs=[pl.BlockSpec((tm, K), lambda i, j: (i, 0)), pl.BlockSpec((GU_TILE, K), lambda i, j: (j, 0)), pair],
        out_specs=pair, out_shape=jax.ShapeDtypeStruct((T, 2 * FFN_HIDDEN), BF16),
        compiler_params=_cp(("parallel", "parallel")),
    )(dx, w_down, ab)


def _adamw_update(w_ref, g_ref, m_ref, v_ref, d_ref, nm_ref, nv_ref):
    gv = g_ref[...]
    nm = ADAM_B1 * m_ref[...] + (1.0 - ADAM_B1) * gv
    nv = ADAM_B2 * v_ref[...] + (1.0 - ADAM_B2) * (gv * gv)
    m_hat = nm / (1.0 - ADAM_B1 ** ADAM_STEP)
    v_hat = nv / (1.0 - ADAM_B2 ** ADAM_STEP)
    d_ref[...] = -ADAM_LR * (m_hat / (jnp.sqrt(v_hat) + ADAM_EPS) + ADAM_WD * w_ref[...])
    nm_ref[...] = nm
    nv_ref[...] = nv


def _adamw_many(ws, gs, ms, vs, name):
    n = len(ws)

    def body(*refs):
        ins, outs = refs[:4 * n], refs[4 * n:]
        for t in range(n):
            _adamw_update(ins[t], ins[n + t], ins[2 * n + t], ins[3 * n + t], outs[t], outs[n + t], outs[2 * n + t])

    shapes = [jax.ShapeDtypeStruct(w.shape, F32) for w in ws]
    out = pl.pallas_call(body, name=name, out_shape=shapes * 3, compiler_params=_cp())(*ws, *gs, *ms, *vs)
    return out[:n], out[n:2 * n], out[2 * n:]


def _adamw(w, g, m, v, name):
    R, C = w.shape
    tr = R
    for cand in (256, 352, 128, 64, 8):
        if R > cand and R % cand == 0:
            tr = cand
            break

    def body(w_ref, g_ref, m_ref, v_ref, d_ref, nm_ref, nv_ref):
        _adamw_update(w_ref, g_ref, m_ref, v_ref, d_ref, nm_ref, nv_ref)

    blk = pl.BlockSpec((tr, C), lambda i: (i, 0))
    sh = jax.ShapeDtypeStruct((R, C), F32)
    return pl.pallas_call(
        body, name=name, grid=(R // tr,), in_specs=[blk] * 4, out_specs=[blk] * 3, out_shape=[sh] * 3,
        compiler_params=_cp(("parallel",)),
    )(w, g, m, v)


def _sum_slabs(x, name):
    n, R, C = x.shape
    tr = R
    for cand in (512, 256, 128, 64, 32, 16, 8):
        if R > cand and R % cand == 0:
            tr = cand
            break

    def body(x_ref, o_ref):
        acc = x_ref[0].astype(F32)
        for j in range(1, n):
            acc = acc + x_ref[j].astype(F32)
        o_ref[...] = acc

    return pl.pallas_call(
        body, name=name, grid=(R // tr,), in_specs=[pl.BlockSpec((n, tr, C), lambda i: (0, i, 0))],
        out_specs=pl.BlockSpec((tr, C), lambda i: (i, 0)), out_shape=jax.ShapeDtypeStruct((R, C), F32),
        compiler_params=_cp(("parallel",)),
    )(x)


def _multi_gather(xs, layers, name):
    nt = len(xs)
    shapes = [x.shape if lay is None else x.shape[1:] for x, lay in zip(xs, layers)]

    def body(*refs):
        x_refs, out_refs = refs[:nt], refs[nt:2 * nt]
        send_sems, recv_sems, local_sems = refs[2 * nt:]
        x_, y_, c_ = lax.axis_index("x"), lax.axis_index("y"), lax.axis_index("c")
        me, sibling = (x_, y_, c_), (x_, y_, 1 - c_)
        chips = [(1 - x_, y_), (x_, 1 - y_), (1 - x_, 1 - y_)]

        def own_block(t):
            return x_refs[t] if layers[t] is None else x_refs[t].at[layers[t]]

        def copy(t, k, block, to, own=False):
            px, py, pc = block
            dst = out_refs[t].at[4 * px + 2 * py + pc]
            return pltpu.make_async_remote_copy(
                src_ref=own_block(t) if own else dst, dst_ref=dst,
                send_sem=send_sems.at[t, k], recv_sem=recv_sems.at[t, k],
                device_id=to, device_id_type=pl.DeviceIdType.MESH)

        mine, first, passed = [], [], []
        for t in range(nt):
            mine.append(pltpu.make_async_copy(own_block(t), out_refs[t].at[4 * x_ + 2 * y_ + c_], local_sems.at[t]))
            mine[-1].start()
            first.append([copy(t, 1 + j, me, (*chip, c_), own=True) for j, chip in enumerate(chips)]
                         + [copy(t, 0, me, sibling, own=True)])
            for cp in first[-1]:
                cp.start()
        for t in range(nt):
            for j, chip in enumerate(chips):
                copy(t, 1 + j, (*chip, c_), me).wait_recv()
                passed.append(copy(t, 4 + j, (*chip, c_), sibling))
                passed[-1].start()
        for t in range(nt):
            copy(t, 0, sibling, me).wait_recv()
            for j, chip in enumerate(chips):
                copy(t, 4 + j, (*chip, 1 - c_), me).wait_recv()
        for cp in [c for f in first for c in f] + passed:
            cp.wait_send()
        for cp in mine:
            cp.wait()

    hbm = pl.BlockSpec(memory_space=pl.ANY)
    return pl.pallas_call(
        body, name=name, out_shape=[jax.ShapeDtypeStruct((N_DEV,) + tuple(s), x.dtype) for s, x in zip(shapes, xs)],
        in_specs=[hbm] * nt, out_specs=[hbm] * nt,
        scratch_shapes=[pltpu.SemaphoreType.DMA((nt, 7)), pltpu.SemaphoreType.DMA((nt, 7)),
                        pltpu.SemaphoreType.DMA((nt,))],
    )(*xs)


_HBM = pl.BlockSpec(memory_space=pltpu.HBM)
_SEM = pl.BlockSpec(memory_space=pltpu.SEMAPHORE)
_PEER_ORDER = (2, 4, 6, 3, 5, 7, 1)


def _split_copies(src_refs, land_refs, send_sems, recv_sems, layers, per_peer):
    x_, y_, c_ = lax.axis_index("x"), lax.axis_index("y"), lax.axis_index("c")
    me = 4 * x_ + 2 * y_ + c_
    copies = []
    for k in _PEER_ORDER:
        px, py, pc = x_ ^ ((k >> 2) & 1), y_ ^ ((k >> 1) & 1), c_ ^ (k & 1)
        peer = 4 * px + 2 * py + pc
        for t in range(len(src_refs)):
            if per_peer:
                src = src_refs[t].at[peer]
            else:
                src = src_refs[t] if layers[t] is None else src_refs[t].at[layers[t]]
            copies.append(pltpu.make_async_remote_copy(
                src_ref=src, dst_ref=land_refs[t].at[me],
                send_sem=send_sems.at[t * (N_DEV - 1) + k - 1], recv_sem=recv_sems.at[t * (N_DEV - 1) + k - 1],
                device_id=(px, py, pc), device_id_type=pl.DeviceIdType.MESH))
    return copies


def _own_copies(src_refs, land_refs, sems, layers, per_peer):
    nt = len(src_refs)
    me = 4 * lax.axis_index("x") + 2 * lax.axis_index("y") + lax.axis_index("c")
    copies = []
    for t in range(nt):
        if per_peer:
            src = src_refs[t].at[me]
        else:
            src = src_refs[t] if layers[t] is None else src_refs[t].at[layers[t]]
        copies.append(pltpu.make_async_copy(src, land_refs[t].at[me], sems.at[nt * (N_DEV - 1) + t]))
    return copies


def _split_start(srcs, layers, per_peer, after, name):
    nt = len(srcs)
    if per_peer:
        land_shapes = [s.shape for s in srcs]
    else:
        land_shapes = [(N_DEV,) + tuple(s.shape if lay is None else s.shape[1:]) for s, lay in zip(srcs, layers)]

    def body(*refs):
        src_refs, land_refs = refs[:nt], refs[nt:2 * nt]
        send_sems, recv_sems = refs[2 * nt + 1], refs[2 * nt + 2]
        token = refs[-1]
        for cp in _split_copies(src_refs, land_refs, send_sems, recv_sems, layers, per_peer):
            cp.start()
        for cp in _own_copies(src_refs, land_refs, send_sems, layers, per_peer):
            cp.start()
        token[...] = jnp.zeros_like(token)

    lands = [pltpu.with_memory_space_constraint(lax.empty(s, x.dtype), pltpu.HBM) for s, x in zip(land_shapes, srcs)]
    srcs = [pltpu.with_memory_space_constraint(x, pltpu.HBM) for x in srcs]
    out = pl.pallas_call(
        body, name=name,
        out_shape=(pltpu.SemaphoreType.DMA((nt * N_DEV,)), pltpu.SemaphoreType.DMA((nt * (N_DEV - 1),)),
                   *[pltpu.HBM(x.shape, x.dtype) for x in srcs], *[pltpu.HBM(s, x.dtype) for s, x in zip(land_shapes, srcs)],
                   jax.ShapeDtypeStruct((8, LANES), F32)),
        in_specs=[_HBM] * (2 * nt) + [pl.BlockSpec(memory_space=pl.ANY)],
        out_specs=(_SEM, _SEM, *([_HBM] * (2 * nt)), pl.BlockSpec(memory_space=pltpu.VMEM)),
        input_output_aliases={i: 2 + i for i in range(2 * nt)},
        compiler_params=pltpu.CompilerParams(has_side_effects=pltpu.SideEffectType.DATAFLOW_SIDE_EFFECTING),
    )(*srcs, *lands, after)
    return out[0], out[1], list(out[2:2 + nt]), list(out[2 + nt:2 + 2 * nt]), out[-1]


def _split_wait(started, layers, per_peer, after, name):
    send_sems, recv_sems, srcs, lands, _ = started
    nt = len(srcs)

    def body(*refs):
        src_refs, land_refs = refs[:nt], refs[nt:2 * nt]
        s_sems, r_sems = refs[2 * nt], refs[2 * nt + 1]
        for cp in _split_copies(src_refs, land_refs, s_sems, r_sems, layers, per_peer):
            cp.wait_send()
            cp.wait_recv()
        for cp in _own_copies(src_refs, land_refs, s_sems, layers, per_peer):
            cp.wait()

    out = pl.pallas_call(
        body, name=name,
        out_shape=tuple(pltpu.HBM(x.shape, x.dtype) for x in srcs + lands),
        in_specs=[_HBM] * (2 * nt) + [_SEM, _SEM, pl.BlockSpec(memory_space=pl.ANY)],
        out_specs=tuple([_HBM] * (2 * nt)),
        input_output_aliases={i: i for i in range(2 * nt)},
        compiler_params=pltpu.CompilerParams(has_side_effects=pltpu.SideEffectType.DATAFLOW_SIDE_EFFECTING),
    )(*srcs, *lands, send_sems, recv_sems, after)
    return list(out[nt:])


def _runs(mapping):
    runs, c, n = [], 0, len(mapping)
    while c < n:
        if mapping[c] is None:
            c += 1
            continue
        sid, d, lo = mapping[c][0], mapping[c][1] - c, c
        while c < n and mapping[c] is not None and mapping[c][0] == sid and mapping[c][1] - c == d:
            c += 1
        runs.append((lo, c, sid, d))
    return runs


def _tile_plan(mapping, src_widths):
    runs = _runs(mapping)
    plan = []
    for t in range(len(mapping) // LANES):
        pieces = []
        for lo, hi, sid, d in runs:
            lo_t, hi_t = max(lo, t * LANES), min(hi, (t + 1) * LANES)
            if lo_t >= hi_t:
                continue
            a = ((lo_t + d) // LANES) * LANES
            win = min(2 * LANES, src_widths[sid] - a)
            shift = t * LANES + d - a
            pieces.append((sid, a, win, shift, lo_t - t * LANES, hi_t - t * LANES))
        plan.append(pieces)
    return plan


def _reblock(srcs, src_views, outs, out_views, name):
    R = srcs[0].shape[-2]
    tr = min(512, R)
    widths = {sid: srcs[ai].shape[-1] for sid, (ai, _) in src_views.items()}
    plans = [(ai, li, _tile_plan(mapping, widths)) for ai, li, mapping in out_views]
    ns = len(srcs)

    def body(*refs):
        s_refs, o_refs = refs[:ns], refs[ns:]
        cache = {}

        def shift_matrix(win, shift, lo, hi):
            key = (win, shift, lo, hi)
            if key not in cache:
                r = lax.broadcasted_iota(jnp.int32, (win, LANES), 0)
                c = lax.broadcasted_iota(jnp.int32, (win, LANES), 1)
                hit = jnp.logical_and(r - c == shift, jnp.logical_and(c >= lo, c < hi))
                cache[key] = jnp.where(hit, 1.0, 0.0).astype(BF16)
            return cache[key]

        for ai, li, plan in plans:
            for t, pieces in enumerate(plan):
                acc = None
                whole = len(pieces) == 1 and pieces[0][3:] == (0, 0, LANES)
                for sid, a, win, shift, lo, hi in pieces:
                    sa, sl = src_views[sid]
                    if whole:
                        win = LANES
                    src = s_refs[sa][:, a:a + win] if sl is None else s_refs[sa][sl, :, a:a + win]
                    if whole:
                        acc = src
                    else:
                        part = jnp.dot(src, shift_matrix(win, shift, lo, hi), preferred_element_type=F32)
                        acc = part if acc is None else acc + part
                val = jnp.zeros((tr, LANES), BF16) if acc is None else acc.astype(BF16)
                if li is None:
                    o_refs[ai][:, t * LANES:(t + 1) * LANES] = val
                else:
                    o_refs[ai][li, :, t * LANES:(t + 1) * LANES] = val

    def spec(shape):
        if len(shape) == 2:
            return pl.BlockSpec((tr, shape[1]), lambda i: (i, 0))
        return pl.BlockSpec((shape[0], tr, shape[2]), lambda i: (0, i, 0))

    return pl.pallas_call(
        body, name=name, grid=(R // tr,), in_specs=[spec(s.shape) for s in srcs],
        out_specs=[spec(s) for s in outs], out_shape=[jax.ShapeDtypeStruct(s, BF16) for s in outs],
        compiler_params=_cp(("parallel",)),
    )(*srcs)


SHARDED = ("w_in", "w_gate_up", "w_proj_attn", "w_proj_pool", "w_proj_conv", "w_out", "w_down")
WEIGHT_ORDER = ("attn_norm", "w_in", "b_forget", "b_gate", "w_proj_attn", "pool_w", "pool_scale", "w_proj_pool",
                "conv_w", "w_proj_conv", "w_out", "ffn_norm", "w_gate_up", "w_down", "final_norm")
IN_SHARD, IN_SHARD_PAD = IN_COLS // N_DEV, 896
GU_SHARD, GU_SHARD_PAD = 2 * FFN_HIDDEN // N_DEV, 768


def _w_in_col(c):
    if c < OFF_QKV:
        return c + 3592
    if c < OFF_U:
        base, off = (0, OFF_QKV) if c < OFF_CONV else (2056, OFF_CONV)
        j, t = divmod(c - off, TRIPLE)
        which, e = divmod(t, LANES)
        return base + which * BRANCH_W + j * LANES + e
    return c - OFF_U + 1544


def _w_in_full(gathered, name):
    main = [divmod(_w_in_col(c), IN_SHARD) for c in range(MAIN_COLS)]
    fcols = [divmod(1536 + c, IN_SHARD) if c < N_HEADS else None for c in range(LANES)]
    R = gathered.shape[1]
    return _reblock([gathered], {i: (0, i) for i in range(N_DEV)}, [(R, MAIN_COLS), (R, LANES)],
                    [(0, None, main), (1, None, fcols)], name)


def _w_in_slabs(dmain, dwf, name):
    inv = {_w_in_col(c): ("m", c) for c in range(MAIN_COLS)}
    inv.update({1536 + c: ("f", c) for c in range(N_HEADS)})
    views = []
    for i in range(N_DEV):
        mapping = [inv[IN_SHARD * i + j] if j < IN_SHARD else None for j in range(IN_SHARD_PAD)]
        views.append((0, i, mapping))
    R = dmain.shape[0]
    return _reblock([dmain, dwf], {"m": (0, None), "f": (1, None)}, [(N_DEV, R, IN_SHARD_PAD)], views, name)[0]


def _w_gu_full(gathered, name):
    mapping = [divmod(_gu_col(c), GU_SHARD) for c in range(2 * FFN_HIDDEN)]
    R = gathered.shape[1]
    return _reblock([gathered], {i: (0, i) for i in range(N_DEV)}, [(R, 2 * FFN_HIDDEN)], [(0, None, mapping)], name)[0]


def _w_gu_slabs(dw, name):
    inv = {_gu_col(c): c for c in range(2 * FFN_HIDDEN)}
    views = [(0, i, [("w", inv[GU_SHARD * i + j]) if j < GU_SHARD else None for j in range(GU_SHARD_PAD)])
             for i in range(N_DEV)]
    R = dw.shape[0]
    return _reblock([dw], {"w": (0, None)}, [(N_DEV, R, GU_SHARD_PAD)], views, name)[0]


def _layer_fwd(x, W, n_seq, l, h1=None, next_norm=None):
    T = x.shape[0]
    sfx = f"_l{l}"
    if h1 is None:
        h1 = _rms_fwd(x, W["attn_norm"], "rms1" + sfx)
    proj, f = _matmul(h1, W["w_main"], mode="nn", out_dtype=BF16, name="proj_main" + sfx, side=(W["w_f"], F32))
    qa, ka, va = _fox_prep(f, W["b_forget"], proj, n_seq, "fox_prep" + sfx)
    oa, oa32, lse = _attn_fwd2(qa, ka, va, n_seq, "attn_fwd" + sfx)
    if "late" in W:
        W.update(W.pop("late")(oa))
    ob = _pool_fwd(proj, W["pool_w"], W["pool_scale"], n_seq, "pool_fwd" + sfx)
    oc = _conv_fwd(proj, W["conv_w"], n_seq, "conv_fwd" + sfx)
    mixed = _mix_fwd(oa, ob, oc, W["w_proj_attn"], W["w_proj_pool"], W["w_proj_conv"], proj, W["b_gate"],
                     "mix_fwd" + sfx)
    x2, h2 = _matmul(mixed, W["w_out"], mode="nn", out_dtype=F32, name="out_proj" + sfx, tm=1024, tn=1024,
                     residual=x, rms_g=W["ffn_norm"])
    ab, s = _gate_up_swiglu(h2, W["w_gate_up"], "gate_up" + sfx)
    x3 = _matmul(s, W["w_down"], mode="nn", out_dtype=F32, name="down" + sfx, tm=1024, tn=1024, tk=1408,
                 residual=x2, rms_g=next_norm)
    x3, h1_next = x3 if next_norm is not None else (x3, None)
    saved = dict(x=x, h1=h1, proj=proj, f=f, qa=qa, ka=ka, oa=oa, oa32=oa32, lse=lse, ob=ob, oc=oc, mixed=mixed, x2=x2,
                 h2=h2, ab=ab, s=s)
    return x3, saved, h1_next


def _layer_bwd(dx3, dx3b, W, sv, n_seq, l, stage=None):
    T = dx3.shape[0]
    sfx = f"_l{l}"
    G = {}
    stage = stage or (lambda l, group, G, W: W)
    dab = _swiglu_bwd_fused(dx3b, W["w_down"], sv["ab"], "d_ab" + sfx)
    G["w_down"] = _matmul(sv["s"], dx3b, mode="tn", out_dtype=BF16, name="dw_down" + sfx, tm=256, tn=1024)
    dh2 = _matmul(dab, W["w_gate_up"], mode="nt", out_dtype=BF16, name="d_h2" + sfx, tm=1024, tn=1024, tk=1408)
    G["w_gate_up"] = _matmul(sv["h2"], dab, mode="tn", out_dtype=BF16, name="dw_gate_up" + sfx, tm=1024)
    W = stage(l, "ffn", G, W)
    dx2, dx2b, G["ffn_norm"] = _rms_bwd(sv["x2"], W["ffn_norm"], dh2, dx3, "rms2_bwd" + sfx)
    dmixed = _matmul(dx2b, W["w_out"], mode="nt", out_dtype=BF16, name="d_mixed" + sfx)
    G["w_out"] = _matmul(sv["mixed"], dx2b, mode="tn", out_dtype=BF16, name="dw_out" + sfx, tm=1024)
    dya, dyb, dyc, dproj, G["b_gate"] = _mix_bwd(sv["oa"], sv["ob"], sv["oc"], W["w_proj_attn"], W["w_proj_pool"],
                                                 W["w_proj_conv"], sv["proj"], W["b_gate"], dmixed, "mix_bwd" + sfx)
    douts = {}
    for br, dy, o in (("attn", dya, sv["oa"]), ("pool", dyb, sv["ob"]), ("conv", dyc, sv["oc"])):
        douts[br] = _matmul(dy, W["w_proj_" + br], mode="nt", out_dtype=BF16, name=f"d_{br}_out" + sfx)
        G["w_proj_" + br] = _matmul(o, dy, mode="tn", out_dtype=BF16, name=f"dw_proj_{br}" + sfx, tm=512)
    W = stage(l, "mix", G, W)
    dproj, G["conv_w"] = _conv_bwd(sv["proj"], douts["conv"], W["conv_w"], dproj, n_seq, "conv_bwd" + sfx)
    dproj, G["pool_w"], G["pool_scale"] = _pool_bwd(sv["proj"], douts["pool"], W["pool_w"], W["pool_scale"], dproj,
                                                    n_seq, "pool_bwd" + sfx)
    dproj, dFk = _attn_bwd(sv["qa"], sv["ka"], sv["proj"], douts["attn"], sv["oa32"], sv["lse"], dproj, n_seq,
                           "attn_bwd" + sfx)
    dF = jnp.pad(dFk.reshape(N_HEADS, T).T, ((0, 0), (0, LANES - N_HEADS)))
    df, G["b_forget"] = _fox_cumsum_bwd(sv["f"], W["b_forget"], dF, n_seq, "fox_cumsum_bwd" + sfx)
    G["w_main"], G["w_f"] = _matmul(sv["h1"], dproj, mode="tn", out_dtype=BF16, name="dw_main" + sfx, tm=1024,
                                    side=(df, BF16))
    W = stage(l, "w_in", G, W)
    dh1 = _matmul(dproj, W["w_main"], mode="nt", out_dtype=BF16, name="d_h1_main" + sfx, tm=1024, tn=1024, tk=1664,
                  extra=(df, W["w_f"]))
    dx, dxb, G["attn_norm"] = _rms_bwd(sv["x"], W["attn_norm"], dh1, dx2, "rms1_bwd" + sfx)
    return dx, dxb, G


def _replicated_operands(rep, l):
    W = {}
    W["attn_norm"], W["ffn_norm"] = rep["attn_norm"][l], rep["ffn_norm"][l]
    W["b_forget"] = jnp.pad(rep["b_forget"][l].reshape(1, N_HEADS), ((0, 0), (0, LANES - N_HEADS)))
    W["b_gate"] = rep["b_gate"][l].reshape(1, GATE_W)
    W["pool_w"] = rep["pool_w"][l].astype(BF16)
    W["pool_scale"] = rep["pool_scale"][l].reshape(1, BRANCH_W)
    return W


def _local_step(x, target, get_W, attn_norms, final_norm, stage=None):
    n_seq, S, Dm = x.shape
    T = n_seq * S
    xt = x.reshape(T, Dm)
    saved, Ws, h1 = [], [], None
    for l in range(DEPTH):
        Ws.append(get_W(l, xt))
        next_norm = attn_norms[l + 1] if l + 1 < DEPTH else None
        xt, sv, h1 = _layer_fwd(xt, Ws[l], n_seq, l, h1, next_norm)
        saved.append(sv)
    loss, dx, dxb, g_final = _loss_head(xt, final_norm, target.reshape(T, Dm), "loss_head")
    grads = [None] * DEPTH
    for l in reversed(range(DEPTH)):
        dx, dxb, grads[l] = _layer_bwd(dx, dxb, Ws[l], saved[l], n_seq, l, stage)
    return loss, dx.reshape(n_seq, S, Dm), grads, g_final


def _padded_shards(weights):
    pads = {"w_in": IN_SHARD_PAD - IN_SHARD, "w_gate_up": GU_SHARD_PAD - GU_SHARD}
    return {n: jnp.pad(weights[n], ((0, 0), (0, 0), (0, pads.get(n, 0)))).astype(BF16) for n in SHARDED}


def _full_operands(g, l):
    W = {}
    if "w_in" in g:
        W["w_main"], W["w_f"] = _w_in_full(g["w_in"], f"w_in_full_l{l}")
    if "w_gate_up" in g:
        W["w_gate_up"] = _w_gu_full(g["w_gate_up"], f"w_gate_up_full_l{l}")
    for n in ("w_proj_attn", "w_proj_pool", "w_proj_conv"):
        if n in g:
            W[n] = jnp.transpose(g[n], (1, 0, 2)).reshape(BRANCH_W, D_MODEL)
    if "w_out" in g:
        W["w_out"] = g["w_out"].reshape(D_MODEL, D_MODEL)
    if "w_down" in g:
        W["w_down"] = g["w_down"].reshape(FFN_HIDDEN, D_MODEL)
    return W


GRAD_GROUPS = {"ffn": ("w_down", "w_gate_up"),
               "mix": ("w_out", "w_proj_attn", "w_proj_pool", "w_proj_conv"),
               "w_in": ("w_in",)}


def _grad_slabs(G, n, l):
    if n == "w_in":
        return _w_in_slabs(G["w_main"], G["w_f"], f"w_in_slabs_l{l}")
    if n == "w_gate_up":
        return _w_gu_slabs(G["w_gate_up"], f"w_gate_up_slabs_l{l}")
    if n == "w_out":
        return G["w_out"].reshape(N_DEV, D_MODEL // N_DEV, D_MODEL)
    if n == "w_down":
        return G["w_down"].reshape(N_DEV, FFN_HIDDEN // N_DEV, D_MODEL)
    return jnp.transpose(G[n].reshape(BRANCH_W, N_DEV, D_MODEL // N_DEV), (1, 0, 2))


def _sum_layer_grads(recv, l):
    out = {n: _sum_slabs(r, f"sum_{n}_l{l}") for n, r in recv.items()}
    if "w_in" in out:
        out["w_in"] = out["w_in"][:, :IN_SHARD]
    if "w_gate_up" in out:
        out["w_gate_up"] = out["w_gate_up"][:, :GU_SHARD]
    return out


def _sum_small(xs, name):
    def body(*refs):
        for x_ref, o_ref in zip(refs[:len(xs)], refs[len(xs):]):
            acc = x_ref[0]
            for j in range(1, N_DEV):
                acc = acc + x_ref[j]
            o_ref[...] = acc

    return pl.pallas_call(
        body, name=name, out_shape=[jax.ShapeDtypeStruct(x.shape[1:], F32) for x in xs],
        compiler_params=_cp(),
    )(*xs)


def _as_2d(a):
    if a.ndim == 1:
        return a.reshape(1, -1)
    return a.reshape(-1, a.shape[-1])


def kernel(x, attn_norm, w_in, b_forget, b_gate, w_proj_attn, pool_w, pool_scale, w_proj_pool, conv_w, w_proj_conv, w_out, ffn_norm, w_gate_up, w_down, final_norm, loss_target, m_attn_norm, m_w_in, m_b_forget, m_b_gate, m_w_proj_attn, m_pool_w, m_pool_scale, m_w_proj_pool, m_conv_w, m_w_proj_conv, m_w_out, m_ffn_norm, m_w_gate_up, m_w_down, m_final_norm, v_attn_norm, v_w_in, v_b_forget, v_b_gate, v_w_proj_attn, v_pool_w, v_pool_scale, v_w_proj_pool, v_conv_w, v_w_proj_conv, v_w_out, v_ffn_norm, v_w_gate_up, v_w_down, v_final_norm):
    weights = dict(attn_norm=attn_norm, w_in=w_in, b_forget=b_forget, b_gate=b_gate, w_proj_attn=w_proj_attn,
                   pool_w=pool_w, pool_scale=pool_scale, w_proj_pool=w_proj_pool, conv_w=conv_w,
                   w_proj_conv=w_proj_conv, w_out=w_out, ffn_norm=ffn_norm, w_gate_up=w_gate_up, w_down=w_down,
                   final_norm=final_norm)
    moments_m = dict(attn_norm=m_attn_norm, w_in=m_w_in, b_forget=m_b_forget, b_gate=m_b_gate,
                     w_proj_attn=m_w_proj_attn, pool_w=m_pool_w, pool_scale=m_pool_scale, w_proj_pool=m_w_proj_pool,
                     conv_w=m_conv_w, w_proj_conv=m_w_proj_conv, w_out=m_w_out, ffn_norm=m_ffn_norm,
                     w_gate_up=m_w_gate_up, w_down=m_w_down, final_norm=m_final_norm)
    moments_v = dict(attn_norm=v_attn_norm, w_in=v_w_in, b_forget=v_b_forget, b_gate=v_b_gate,
                     w_proj_attn=v_w_proj_attn, pool_w=v_pool_w, pool_scale=v_pool_scale, w_proj_pool=v_w_proj_pool,
                     conv_w=v_conv_w, w_proj_conv=v_w_proj_conv, w_out=v_w_out, ffn_norm=v_ffn_norm,
                     w_gate_up=v_w_gate_up, w_down=v_w_down, final_norm=v_final_norm)

    sh = _padded_shards(weights)
    names = list(SHARDED)
    rest = [n for n in names if n != "w_in"]
    me = 4 * lax.axis_index("x") + 2 * lax.axis_index("y") + lax.axis_index("c")
    w_in0, conv_all = _multi_gather([sh["w_in"], conv_w], [0, None], "gather_w_in_l0")
    started, after = {}, w_in0
    for l in range(DEPTH):
        for group, gnames in (("w_in", ["w_in"]), ("rest", rest)):
            if (l, group) != (0, "w_in"):
                started[l, group] = _split_start([sh[n] for n in gnames], [l] * len(gnames), False, after,
                                                 f"gather_start_{group}_l{l}")
                after = started[l, group][4]
    last_token = after

    def get_W(l, xt):
        if l == 0:
            w_in = w_in0
        else:
            w_in = _split_wait(started[l, "w_in"], [l], False, xt, f"gather_wait_w_in_l{l}")[0]
        W = _full_operands({"w_in": w_in}, l)

        def late(after):
            lands = _split_wait(started[l, "rest"], [l] * len(rest), False, after, f"gather_wait_rest_l{l}")
            return _full_operands(dict(zip(rest, lands)), l)

        W["late"] = late
        W.update(_replicated_operands(weights, l))
        W["conv_w"] = jnp.transpose(conv_all[:, l], (1, 0, 2)).reshape(CONV_K, BRANCH_W)
        if l == 0:
            W["attn_norm"] = W["attn_norm"] + last_token[0, 0]
        return W

    exchanges = []

    def stage(l, group, G, W):
        gnames = GRAD_GROUPS[group]
        slabs = [_grad_slabs(G, n, l) for n in gnames]
        started = _split_start(slabs, None, True, slabs[0], f"exchange_start_{group}_l{l}")
        exchanges.append((l, group, gnames, slabs, started))
        tie = {"ffn": "ffn_norm", "mix": "conv_w", "w_in": "w_f"}[group]
        W = dict(W)
        W[tie] = W[tie] + started[4][0, 0].astype(W[tie].dtype)
        return W

    loss_part, grad_x, grads, g_final = _local_step(x, loss_target, get_W, attn_norm, final_norm, stage)
    after = grad_x
    for l, group, gnames, slabs, started in exchanges:
        lands = _split_wait(started, None, True, after, f"exchange_wait_{group}_l{l}")
        grads[l].update(_sum_layer_grads(dict(zip(gnames, lands)), l))
    gw = {n: jnp.stack([grads[l][n] for l in range(DEPTH)]) for n in SHARDED}

    small = ("attn_norm", "b_forget", "b_gate", "pool_w", "pool_scale", "ffn_norm", "conv_w")
    parts = [jnp.stack([grads[l][n] for l in range(DEPTH)]) for n in small] + [g_final, loss_part]
    gathered = _multi_gather(parts, [None] * len(parts), "gather_small_grads")
    summed = _sum_small(gathered, "sum_small_grads")
    for n, s in zip(small, summed):
        gw[n] = s
    gw["attn_norm"], gw["ffn_norm"] = gw["attn_norm"][:, 0], gw["ffn_norm"][:, 0]
    gw["b_forget"] = gw["b_forget"][:, 0, :N_HEADS]
    gw["b_gate"], gw["pool_scale"] = gw["b_gate"][:, 0], gw["pool_scale"][:, 0]
    gw["conv_w"] = lax.dynamic_slice_in_dim(gw["conv_w"], me * (BRANCH_W // N_DEV), BRANCH_W // N_DEV, axis=2)
    gw["final_norm"] = summed[-2][0]
    loss = summed[-1][0, 0]

    deltas, new_m, new_v = {}, {}, {}
    for n in SHARDED:
        shape = weights[n].shape
        d, nm, nv = _adamw(_as_2d(weights[n]), _as_2d(gw[n]), _as_2d(moments_m[n]), _as_2d(moments_v[n]),
                           "adamw_" + n)
        deltas[n], new_m[n], new_v[n] = d.reshape(shape), nm.reshape(shape), nv.reshape(shape)
    rest_names = [n for n in WEIGHT_ORDER if n not in SHARDED]
    ds, nms, nvs = _adamw_many(*[[_as_2d(src[n]) for n in rest_names] for src in (weights, gw, moments_m, moments_v)],
                               "adamw_small")
    for n, d, nm, nv in zip(rest_names, ds, nms, nvs):
        shape = weights[n].shape
        deltas[n], new_m[n], new_v[n] = d.reshape(shape), nm.reshape(shape), nv.reshape(shape)

    return (loss, grad_x, *[gw[n] for n in WEIGHT_ORDER], *[deltas[n] for n in WEIGHT_ORDER],
            *[new_m[n] for n in WEIGHT_ORDER], *[new_v[n] for n in WEIGHT_ORDER])
```

```python
import functools

import jax
import jax.numpy as jnp
from jax import lax
from jax.experimental import pallas as pl
from jax.experimental.pallas import tpu as pltpu

F32 = jnp.float32
BF16 = jnp.bfloat16

N_DEV = 8
D_MODEL = 1024
DEPTH = 2
N_HEADS = 8
HEAD_DIM = 64
BRANCH_W = 512
POOL_WINDOWS = (2, 4, 8, 16)
POOL_GD = 128
CONV_K = 3
FFN_HIDDEN = 2816
GATE_W = 3 * D_MODEL
IN_COLS = 6664
MAIN_COLS = GATE_W + 7 * BRANCH_W
RMS_EPS = 1e-6
NEG_INF = -1e30

ADAM_LR = 0.001
ADAM_B1 = 0.9
ADAM_B2 = 0.999
ADAM_EPS = 1e-08
ADAM_WD = 0.01
ADAM_STEP = 10

LANES = 128
VMEM_LIMIT = 56 * 1024 * 1024
CUM_BLK = 256

TRIPLE = 3 * LANES
OFF_G, OFF_QKV, OFF_CONV, OFF_U = 0, 3072, 4608, 6144


def _cp(sem=None):
    return pltpu.CompilerParams(dimension_semantics=sem, vmem_limit_bytes=VMEM_LIMIT)


def _sigmoid(z):
    return pl.reciprocal(1.0 + jnp.exp(-z), approx=True)


def _matmul(a, b, *, mode, out_dtype, name, tm=2048, tn=512, tk=None, residual=None, rms_g=None, side=None,
            extra=None):
    if mode == "nn":
        (M, K), N = a.shape, b.shape[1]
    elif mode == "nt":
        (M, K), N = a.shape, b.shape[0]
    else:
        (K, M), N = a.shape, b.shape[1]
    tm, tn, tk = min(tm, M), min(tn, N), K if tk is None else min(tk, K)
    assert M % tm == 0 and N % tn == 0 and K % tk == 0, (name, M, N, K, tm, tn, tk)
    nk = K // tk
    if mode == "nn":
        a_spec = pl.BlockSpec((tm, tk), lambda i, j, k: (i, k))
        b_spec = pl.BlockSpec((tk, tn), lambda i, j, k: (k, j))
        dims = (((1,), (0,)), ((), ()))
    elif mode == "nt":
        a_spec = pl.BlockSpec((tm, tk), lambda i, j, k: (i, k))
        b_spec = pl.BlockSpec((tn, tk), lambda i, j, k: (j, k))
        dims = (((1,), (1,)), ((), ()))
    else:
        a_spec = pl.BlockSpec((tk, tm), lambda i, j, k: (k, i))
        b_spec = pl.BlockSpec((tk, tn), lambda i, j, k: (k, j))
        dims = (((0,), (0,)), ((), ()))
    o_spec = pl.BlockSpec((tm, tn), lambda i, j, k: (i, j))
    has_res, has_norm, has_side, has_extra = (v is not None for v in (residual, rms_g, side, extra))
    assert not has_norm or tn == N, (name, tn, N)
    assert not has_side or (nk == 1 and mode != "nt"), name

    in_specs, args = [a_spec, b_spec], [a, b]
    out_specs, out_shape = [o_spec], [jax.ShapeDtypeStruct((M, N), out_dtype)]
    if has_res:
        in_specs.append(o_spec)
        args.append(residual)
    if has_norm:
        in_specs.append(pl.BlockSpec((1, N), lambda i, j, k: (0, 0)))
        args.append(rms_g.reshape(1, N))
        out_specs.append(o_spec)
        out_shape.append(jax.ShapeDtypeStruct((M, N), BF16))
    if has_side:
        b_side, side_dtype = side
        ns = b_side.shape[1]
        in_specs.append(pl.BlockSpec((K, ns), lambda i, j, k: (0, 0)))
        args.append(b_side)
        out_specs.append(pl.BlockSpec((tm, ns), lambda i, j, k: (i, 0)))
        out_shape.append(jax.ShapeDtypeStruct((M, ns), side_dtype))
    if has_extra:
        a2, b2 = extra
        in_specs += [pl.BlockSpec((tm, a2.shape[1]), lambda i, j, k: (i, 0)),
                     pl.BlockSpec((tn, b2.shape[1]), lambda i, j, k: (j, 0))]
        args += [a2, b2]
    n_in = len(args)

    def body(*refs):
        ins, outs = list(refs[2:n_in]), list(refs[n_in:n_in + len(out_shape)])
        a_ref, b_ref = refs[:2]
        r_ref = ins.pop(0) if has_res else None
        g_ref = ins.pop(0) if has_norm else None
        bs_ref = ins.pop(0) if has_side else None
        a2_ref, b2_ref = (ins.pop(0), ins.pop(0)) if has_extra else (None, None)
        o_ref = outs.pop(0)
        h_ref = outs.pop(0) if has_norm else None
        so_ref = outs.pop(0) if has_side else None

        def finish(acc):
            if has_res:
                acc = acc + r_ref[...].astype(F32)
            if has_extra:
                acc = acc + lax.dot_general(a2_ref[...], b2_ref[...], (((1,), (1,)), ((), ())),
                                            preferred_element_type=F32)
            o_ref[...] = acc.astype(out_dtype)
            if has_norm:
                r = lax.rsqrt(jnp.mean(acc * acc, axis=-1, keepdims=True) + RMS_EPS)
                h_ref[...] = ((acc * r) * g_ref[...]).astype(BF16)

        if has_side:
            @pl.when(pl.program_id(1) == 0)
            def _():
                side_dims = (((1,), (0,)), ((), ())) if mode == "nn" else dims
                so_ref[...] = lax.dot_general(a_ref[...], bs_ref[...], side_dims,
                                              preferred_element_type=F32).astype(so_ref.dtype)

        prod = lax.dot_general(a_ref[...], b_ref[...], dims, preferred_element_type=F32)
        if nk == 1:
            finish(prod)
            return
        acc_ref = refs[-1]
        k = pl.program_id(2)

        @pl.when(k == 0)
        def _():
            acc_ref[...] = prod

        @pl.when(jnp.logical_and(k > 0, k < nk - 1))
        def _():
            acc_ref[...] += prod

        @pl.when(k == nk - 1)
        def _():
            finish(acc_ref[...] + prod)

    single = len(out_shape) == 1
    return pl.pallas_call(
        body, name=name, grid=(M // tm, N // tn, nk), in_specs=in_specs,
        out_specs=out_specs[0] if single else out_specs, out_shape=out_shape[0] if single else out_shape,
        scratch_shapes=[pltpu.VMEM((tm, tn), F32)] if nk > 1 else [],
        compiler_params=_cp(("parallel", "arbitrary" if has_side else "parallel", "arbitrary")),
    )(*args)


def _rms_fwd(x, g, name):
    T, Dm = x.shape
    tm = min(512, T)

    def body(x_ref, g_ref, h_ref):
        xf = x_ref[...]
        r = lax.rsqrt(jnp.mean(xf * xf, axis=-1, keepdims=True) + RMS_EPS)
        h_ref[...] = ((xf * r) * g_ref[...]).astype(BF16)

    return pl.pallas_call(
        body, name=name, grid=(T // tm,),
        in_specs=[pl.BlockSpec((tm, Dm), lambda i: (i, 0)), pl.BlockSpec((1, Dm), lambda i: (0, 0))],
        out_specs=pl.BlockSpec((tm, Dm), lambda i: (i, 0)),
        out_shape=jax.ShapeDtypeStruct((T, Dm), BF16),
        compiler_params=_cp(("parallel",)),
    )(x, g.reshape(1, Dm))


def _rms_bwd(x, g, dh, dres, name):
    T, Dm = x.shape
    tm = min(512, T)

    def body(x_ref, g_ref, dh_ref, dres_ref, dx_ref, dxb_ref, dg_ref):
        i = pl.program_id(0)
        xf = x_ref[...]
        r = lax.rsqrt(jnp.mean(xf * xf, axis=-1, keepdims=True) + RMS_EPS)
        xn = xf * r
        dhf = dh_ref[...].astype(F32)
        dxn = dhf * g_ref[...]
        c = jnp.mean(dxn * xn, axis=-1, keepdims=True)
        dx = dres_ref[...] + r * (dxn - xn * c)
        dx_ref[...] = dx
        dxb_ref[...] = dx.astype(BF16)
        part = jnp.sum(dhf * xn, axis=0, keepdims=True)

        @pl.when(i == 0)
        def _():
            dg_ref[...] = part

        @pl.when(i > 0)
        def _():
            dg_ref[...] += part

    row = pl.BlockSpec((tm, Dm), lambda i: (i, 0))
    vec = pl.BlockSpec((1, Dm), lambda i: (0, 0))
    return pl.pallas_call(
        body, name=name, grid=(T // tm,), in_specs=[row, vec, row, row], out_specs=[row, row, vec],
        out_shape=[jax.ShapeDtypeStruct((T, Dm), F32), jax.ShapeDtypeStruct((T, Dm), BF16),
                   jax.ShapeDtypeStruct((1, Dm), F32)],
        compiler_params=_cp(("arbitrary",)),
    )(x, g.reshape(1, Dm), dh, dres)


def _loss_head(x, g, target, name):
    T, Dm = x.shape
    tm = min(512, T)

    def body(x_ref, g_ref, t_ref, loss_ref, dx_ref, dxb_ref, dg_ref):
        i = pl.program_id(0)
        xf = x_ref[...]
        gv = g_ref[...]
        r = lax.rsqrt(jnp.mean(xf * xf, axis=-1, keepdims=True) + RMS_EPS)
        xn = xf * r
        diff = xn * gv - t_ref[...]
        per_tok = jnp.mean(diff * diff, axis=-1, keepdims=True)
        lpart = 0.5 * jnp.sum(per_tok, axis=0, keepdims=True) + jnp.zeros((1, LANES), F32)
        dy = diff * (1.0 / Dm)
        dxn = dy * gv
        c = jnp.mean(dxn * xn, axis=-1, keepdims=True)
        dx = r * (dxn - xn * c)
        dx_ref[...] = dx
        dxb_ref[...] = dx.astype(BF16)
        part = jnp.sum(dy * xn, axis=0, keepdims=True)

        @pl.when(i == 0)
        def _():
            dg_ref[...] = part
            loss_ref[...] = lpart

        @pl.when(i > 0)
        def _():
            dg_ref[...] += part
            loss_ref[...] += lpart

    row = pl.BlockSpec((tm, Dm), lambda i: (i, 0))
    vec = pl.BlockSpec((1, Dm), lambda i: (0, 0))
    lsp = pl.BlockSpec((1, LANES), lambda i: (0, 0))
    return pl.pallas_call(
        body, name=name, grid=(T // tm,), in_specs=[row, vec, row], out_specs=[lsp, row, row, vec],
        out_shape=[jax.ShapeDtypeStruct((1, LANES), F32), jax.ShapeDtypeStruct((T, Dm), F32),
                   jax.ShapeDtypeStruct((T, Dm), BF16), jax.ShapeDtypeStruct((1, Dm), F32)],
        compiler_params=_cp(("arbitrary",)),
    )(x, g.reshape(1, Dm), target)


def _split_bf16(v):
    hi = v.astype(BF16)
    r1 = v - hi.astype(F32)
    mid = r1.astype(BF16)
    lo = (r1 - mid.astype(F32)).astype(BF16)
    return hi, mid, lo


def _tri_dot(tri, v):
    hi, mid, lo = _split_bf16(v)
    dot = functools.partial(jnp.dot, preferred_element_type=F32)
    return dot(tri, hi) + dot(tri, mid) + dot(tri, lo)


def _log_sigmoid(z):
    return jnp.minimum(z, 0.0) - jnp.log(1.0 + jnp.exp(-jnp.abs(z)))


def _fox_cumsum_bwd(f, bf, dF, n_seq, name):
    T = f.shape[0]
    S = T // n_seq
    c = min(CUM_BLK, S)

    def body(f_ref, b_ref, dF_ref, df_ref, db_ref):
        b = pl.program_id(0)
        ri = lax.broadcasted_iota(jnp.int32, (c, c), 0)
        ci = lax.broadcasted_iota(jnp.int32, (c, c), 1)
        tri = (ri <= ci).astype(BF16)
        carry = jnp.zeros((1, LANES), F32)
        dbp = jnp.zeros((1, LANES), F32)
        for j in reversed(range(S // c)):
            dFc = dF_ref[j * c:(j + 1) * c, :]
            dlf = _tri_dot(tri, dFc) + carry
            carry = carry + jnp.sum(dFc, axis=0, keepdims=True)
            z = f_ref[j * c:(j + 1) * c, :] + b_ref[...]
            dz = dlf * _sigmoid(-z)
            df_ref[j * c:(j + 1) * c, :] = dz.astype(BF16)
            dbp = dbp + jnp.sum(dz, axis=0, keepdims=True)

        @pl.when(b == 0)
        def _():
            db_ref[...] = dbp

        @pl.when(b > 0)
        def _():
            db_ref[...] += dbp

    blk = pl.BlockSpec((S, LANES), lambda b: (b, 0))
    vec = pl.BlockSpec((1, LANES), lambda b: (0, 0))
    return pl.pallas_call(
        body, name=name, grid=(n_seq,), in_specs=[blk, vec, blk], out_specs=[blk, vec],
        out_shape=[jax.ShapeDtypeStruct((T, LANES), BF16), jax.ShapeDtypeStruct((1, LANES), F32)],
        compiler_params=_cp(("arbitrary",)),
    )(f, bf, dF)


def _pair_masks():
    lane = lax.broadcasted_iota(jnp.int32, (1, LANES), 1)
    lo = lane < HEAD_DIM
    return lo, jnp.logical_not(lo)


AUG0 = HEAD_DIM
Q_TILE, K_CHUNK, ROW_GROUP = 512, 256, 64


def _fox_prep(f, bf, proj, n_seq, name):
    T = f.shape[0]
    S = T // n_seq
    c = min(CUM_BLK, S)

    def body(f_ref, b_ref, qkv_ref, qa_ref, ka_ref, va_ref):
        ri = lax.broadcasted_iota(jnp.int32, (c, c), 0)
        ci = lax.broadcasted_iota(jnp.int32, (c, c), 1)
        tri = (ri >= ci).astype(BF16)
        lane = lax.broadcasted_iota(jnp.int32, (c, LANES), 1)
        carry = jnp.zeros((1, LANES), F32)
        for j in range(S // c):
            rows = slice(j * c, (j + 1) * c)
            lf = _log_sigmoid(f_ref[rows, :] + b_ref[...])
            Fc = _tri_dot(tri, lf) + carry
            carry = carry + jnp.sum(lf, axis=0, keepdims=True)
            for h in range(N_HEADS):
                col = jnp.sum(jnp.where(lane == h, Fc, 0.0), axis=-1, keepdims=True)
                hi = col.astype(BF16).astype(F32)
                r1 = col - hi
                mid = r1.astype(BF16).astype(F32)
                lo = r1 - mid
                ones_q = jnp.logical_and(lane >= AUG0 + 3, lane < AUG0 + 6)
                ones_k = jnp.logical_and(lane >= AUG0, lane < AUG0 + 3)
                aug_q = jnp.where(lane == AUG0, hi, jnp.where(lane == AUG0 + 1, mid, jnp.where(
                    lane == AUG0 + 2, lo, jnp.where(ones_q, 1.0, 0.0))))
                aug_k = jnp.where(lane == AUG0 + 3, -hi, jnp.where(lane == AUG0 + 4, -mid, jnp.where(
                    lane == AUG0 + 5, -lo, jnp.where(ones_k, 1.0, 0.0))))
                base = (h // 2) * TRIPLE
                qp, kp, vp = (qkv_ref[rows, base + t * LANES:base + (t + 1) * LANES].astype(F32) for t in range(3))
                if h % 2:
                    qp, kp, vp = (pltpu.roll(a, HEAD_DIM, 1) for a in (qp, kp, vp))
                out = slice(h * LANES, (h + 1) * LANES)
                qa_ref[rows, out] = jnp.where(lane < HEAD_DIM, qp * (HEAD_DIM ** -0.5), aug_q).astype(BF16)
                ka_ref[rows, out] = jnp.where(lane < HEAD_DIM, kp, aug_k).astype(BF16)
                va_ref[rows, out] = jnp.where(lane < HEAD_DIM, vp, jnp.where(lane == AUG0, 1.0, 0.0)).astype(BF16)

    fblk = pl.BlockSpec((S, LANES), lambda b: (b, 0))
    out = pl.BlockSpec((S, N_HEADS * LANES), lambda b: (b, 0))
    sh = jax.ShapeDtypeStruct((T, N_HEADS * LANES), BF16)
    return pl.pallas_call(
        body, name=name, grid=(n_seq,),
        in_specs=[fblk, pl.BlockSpec((1, LANES), lambda b: (0, 0)),
                  pl.BlockSpec((S, 4 * TRIPLE), lambda b: (b, OFF_QKV // (4 * TRIPLE)))],
        out_specs=[out, out, out], out_shape=[sh, sh, sh],
        compiler_params=_cp(("parallel",)),
    )(f, bf, proj)


def _band_mask(q0, k0, nq, nk):
    row = q0 + lax.broadcasted_iota(jnp.int32, (nq, nk), 0)
    col = k0 + lax.broadcasted_iota(jnp.int32, (nq, nk), 1)
    return col <= row


_NT = (((1,), (1,)), ((), ()))
_TN = (((0,), (0,)), ((), ()))


def _attn_fwd2(qa, ka, va, n_seq, name):
    T = qa.shape[0]
    S = T // n_seq
    tq, tk, rg = min(Q_TILE, S), min(K_CHUNK, S), ROW_GROUP
    nq, per = S // tq, tq // tk

    def body(q_ref, k_ref, v_ref, o_ref, o32_ref, lse_ref, phi_s, plo_s, mp_s, m_s, acc_s):
        qi = pl.program_id(2)
        mp_s[...] = jnp.full_like(mp_s, NEG_INF)
        acc_s[...] = jnp.zeros_like(acc_s)

        def scores(kc, hh, r0):
            k0 = pl.multiple_of(kc * tk, tk)
            hl = slice(hh * LANES, (hh + 1) * LANES)
            return k0, lax.dot_general(q_ref[r0:, hl], k_ref[pl.ds(k0, tk), hl], _NT, preferred_element_type=F32)

        def max_chunk(kc, masked, r0):
            for hh in range(2):
                k0, s_all = scores(kc, hh, r0)
                for r in range(r0 // rg, tq // rg):
                    rows = slice(r * rg, (r + 1) * rg)
                    s = s_all[r * rg - r0:(r + 1) * rg - r0, :]
                    if masked:
                        s = jnp.where(_band_mask(qi * tq + r * rg, k0, rg, tk), s, NEG_INF)
                    part = s[:, :LANES]
                    for c in range(1, tk // LANES):
                        part = jnp.maximum(part, s[:, c * LANES:(c + 1) * LANES])
                    mp_s[hh, rows, :] = jnp.maximum(mp_s[hh, rows, :], part)

        def sum_chunk(kc, masked, r0):
            for hh in range(2):
                k0, s_all = scores(kc, hh, r0)
                hl = slice(hh * LANES, (hh + 1) * LANES)
                v = v_ref[pl.ds(k0, tk), hl]
                for r in range(r0 // rg, tq // rg):
                    rows = slice(r * rg, (r + 1) * rg)
                    p = jnp.exp(s_all[r * rg - r0:(r + 1) * rg - r0, :] - m_s[hh, rows])
                    if masked:
                        p = jnp.where(_band_mask(qi * tq + r * rg, k0, rg, tk), p, 0.0)
                    p_hi = p.astype(BF16)
                    phi_s[hh, rows, :] = p_hi
                    plo_s[hh, rows, :] = (p - p_hi.astype(F32)).astype(BF16)
                acc_s[hh, r0:, :] += (jnp.dot(phi_s[hh, r0:, :], v, preferred_element_type=F32)
                                      + jnp.dot(plo_s[hh, r0:, :], v, preferred_element_type=F32))

        def sweep(chunk):
            def unmasked(kc, carry):
                chunk(kc, False, 0)
                return carry

            lax.fori_loop(0, qi * per, unmasked, 0)
            for d in range(per):
                chunk(qi * per + d, True, d * tk)

        sweep(max_chunk)
        m_s[...] = jnp.max(mp_s[...], axis=-1, keepdims=True)
        sweep(sum_chunk)

        lane = lax.broadcasted_iota(jnp.int32, (1, LANES), 1)
        outs = []
        for hh in range(2):
            acc = acc_s[hh]
            l = jnp.sum(jnp.where(lane == AUG0, acc, 0.0), axis=-1, keepdims=True)
            lse_ref[hh] = m_s[hh] + jnp.log(l)
            outs.append(acc / l)
        o = jnp.where(lane < HEAD_DIM, outs[0], pltpu.roll(outs[1], HEAD_DIM, 1))
        o_ref[...] = o.astype(BF16)
        o32_ref[...] = o

    qmap = lambda b, j, qi: (b * nq + qi, j)
    omap = lambda b, j, qi: (b * nq + qi, j)
    kv = pl.BlockSpec((S, 2 * LANES), lambda b, j, qi: (b, j))
    return pl.pallas_call(
        body, name=name, grid=(n_seq, N_HEADS // 2, nq),
        in_specs=[pl.BlockSpec((tq, 2 * LANES), qmap), kv, kv],
        out_specs=[pl.BlockSpec((tq, LANES), omap), pl.BlockSpec((tq, LANES), omap),
                   pl.BlockSpec((2, tq, 1), lambda b, j, qi: (j, b * nq + qi, 0))],
        out_shape=[jax.ShapeDtypeStruct((T, BRANCH_W), BF16), jax.ShapeDtypeStruct((T, BRANCH_W), F32),
                   jax.ShapeDtypeStruct((N_HEADS, T, 1), F32)],
        scratch_shapes=[pltpu.VMEM((2, tq, tk), BF16), pltpu.VMEM((2, tq, tk), BF16),
                        pltpu.VMEM((2, tq, LANES), F32), pltpu.VMEM((2, tq, 1), F32),
                        pltpu.VMEM((2, tq, LANES), F32)],
        compiler_params=_cp(("parallel", "parallel", "parallel")),
    )(qa, ka, va)


def _attn_bwd(qa, ka, proj, do, o32, lse, dproj, n_seq, name):
    T = qa.shape[0]
    S = T // n_seq
    tq, tk, rg = min(Q_TILE, S), min(K_CHUNK, S), ROW_GROUP
    nq, per, nkc = S // tq, tq // tk, S // tk

    def body(q_ref, k_ref, v_ref, do_ref, o_ref, lse_ref, _, dqkv_ref, dfk_ref,
             p_s, ds_s, dq_s, dk_s, dv_s, df_s):
        dk_s[...] = jnp.zeros_like(dk_s)
        dv_s[...] = jnp.zeros_like(dv_s)
        df_s[...] = jnp.zeros_like(df_s)
        sels = _pair_masks()

        for qi in range(nq):
            q0 = qi * tq
            do_t = do_ref[q0:q0 + tq, :]
            dq_s[...] = jnp.zeros_like(dq_s)
            prod = do_t.astype(F32) * o_ref[q0:q0 + tq, :]
            dls = [jnp.sum(jnp.where(sel, prod, 0.0), axis=-1, keepdims=True) for sel in sels]

            def chunk(kc, masked, r0, q0=q0, do_t=do_t, dls=dls):
                k0 = pl.multiple_of(kc * tk, tk)
                v = v_ref[pl.ds(k0, tk), :]
                do_a = do_t[r0:, :]
                for hh in range(2):
                    hl = slice(hh * LANES, (hh + 1) * LANES)
                    qh, kh = q_ref[q0 + r0:q0 + tq, hl], k_ref[pl.ds(k0, tk), hl]
                    s_all = lax.dot_general(qh, kh, _NT, preferred_element_type=F32)
                    dom = jnp.where(sels[hh], do_a, jnp.zeros_like(do_a))
                    dp_all = lax.dot_general(dom, v, _NT, preferred_element_type=F32)
                    dfp = jnp.zeros((1, tk), F32)
                    for r in range(r0 // rg, tq // rg):
                        rows = slice(r * rg, (r + 1) * rg)
                        arows = slice(r * rg - r0, (r + 1) * rg - r0)
                        qrows = slice(q0 + r * rg, q0 + (r + 1) * rg)
                        p = jnp.exp(s_all[arows, :] - lse_ref[hh, qrows])
                        if masked:
                            p = jnp.where(_band_mask(q0 + r * rg, k0, rg, tk), p, 0.0)
                        ds = p * (dp_all[arows, :] - dls[hh][rows])
                        p_s[hh, rows, :] = p.astype(BF16)
                        ds_s[hh, rows, :] = ds.astype(BF16)
                        dfp = dfp + jnp.sum(ds, axis=0, keepdims=True)
                    df_s[hh, kc] -= dfp
                    dq_s[hh, r0:, :] += jnp.dot(ds_s[hh, r0:, :], kh, preferred_element_type=F32)
                    dv_s[hh, pl.ds(k0, tk), :] += lax.dot_general(p_s[hh, r0:, :], do_a, _TN,
                                                                  preferred_element_type=F32)
                    dk_s[hh, pl.ds(k0, tk), :] += lax.dot_general(ds_s[hh, r0:, :], qh, _TN,
                                                                  preferred_element_type=F32)

            def unmasked(kc, carry, chunk=chunk):
                chunk(kc, False, 0)
                return carry

            lax.fori_loop(0, qi * per, unmasked, 0)
            for d in range(per):
                chunk(qi * per + d, True, d * tk)
            dq = jnp.where(sels[0], dq_s[0], pltpu.roll(dq_s[1], HEAD_DIM, 1))
            dqkv_ref[q0:q0 + tq, :LANES] = (dq * (HEAD_DIM ** -0.5)).astype(BF16)

        dqkv_ref[:, LANES:2 * LANES] = jnp.where(sels[0], dk_s[0], pltpu.roll(dk_s[1], HEAD_DIM, 1)).astype(BF16)
        dqkv_ref[:, 2 * LANES:] = jnp.where(sels[0], dv_s[0], dv_s[1]).astype(BF16)
        for c in range(nkc):
            dfk_ref[:, :, c * tk:(c + 1) * tk] = df_s[:, c]

    seq = lambda w: pl.BlockSpec((S, w), lambda b, j: (b, j))
    col1 = pl.BlockSpec((2, S, 1), lambda b, j: (j, b, 0))
    vblk = pl.BlockSpec((S, LANES), lambda b, j: (b, OFF_QKV // LANES + 3 * j + 2))
    return pl.pallas_call(
        body, name=name, grid=(n_seq, N_HEADS // 2),
        in_specs=[seq(2 * LANES), seq(2 * LANES), vblk, seq(LANES), seq(LANES), col1,
                  pl.BlockSpec(memory_space=pl.ANY)],
        out_specs=[pl.BlockSpec((S, TRIPLE), lambda b, j: (b, OFF_QKV // TRIPLE + j)),
                   pl.BlockSpec((2, 1, S), lambda b, j: (j, 0, b))],
        out_shape=[jax.ShapeDtypeStruct(dproj.shape, BF16), jax.ShapeDtypeStruct((N_HEADS, 1, T), F32)],
        input_output_aliases={6: 0},
        scratch_shapes=[pltpu.VMEM((2, tq, tk), BF16), pltpu.VMEM((2, tq, tk), BF16),
                        pltpu.VMEM((2, tq, LANES), F32), pltpu.VMEM((2, S, LANES), F32),
                        pltpu.VMEM((2, S, LANES), F32), pltpu.VMEM((2, nkc, 1, tk), F32)],
        compiler_params=_cp(("parallel", "parallel")),
    )(qa, ka, proj, do, o32, lse, dproj)


def _shift_down(v, k, row):
    return jnp.where(row >= k, pltpu.roll(v, k, 0), 0.0)


def _shift_up(v, k, row, S):
    return jnp.where(row < S - k, pltpu.roll(v, S - k, 0), 0.0)


def _pool_diff(uf, w, row):
    acc, k = uf, 1
    while k < w:
        acc = acc + _shift_down(acc, k, row)
        k *= 2
    n = jnp.minimum(row + 1, w).astype(F32)
    return acc / n - uf


def _pool_fwd(proj, pool_w, pool_scale, n_seq, name):
    T = proj.shape[0]
    S = T // n_seq

    def body(u_ref, w_ref, sc_ref, o_ref, d_s):
        g = pl.program_id(1)
        row = lax.broadcasted_iota(jnp.int32, (S, POOL_GD), 0)
        uf = u_ref[...].astype(F32)
        for gi, wlen in enumerate(POOL_WINDOWS):
            @pl.when(g == gi)
            def _(wlen=wlen):
                d_s[...] = _pool_diff(uf, wlen, row).astype(BF16)
        e = jnp.dot(d_s[...], w_ref[0], preferred_element_type=F32)
        o_ref[...] = (e * sc_ref[...]).astype(BF16)

    uc = OFF_U // POOL_GD
    return pl.pallas_call(
        body, name=name, grid=(n_seq, len(POOL_WINDOWS)),
        in_specs=[pl.BlockSpec((S, POOL_GD), lambda b, g: (b, uc + g)),
                  pl.BlockSpec((1, POOL_GD, POOL_GD), lambda b, g: (g, 0, 0)),
                  pl.BlockSpec((1, POOL_GD), lambda b, g: (0, g))],
        out_specs=pl.BlockSpec((S, POOL_GD), lambda b, g: (b, g)),
        out_shape=jax.ShapeDtypeStruct((T, BRANCH_W), BF16),
        scratch_shapes=[pltpu.VMEM((S, POOL_GD), BF16)],
        compiler_params=_cp(("parallel", "parallel")),
    )(proj, pool_w, pool_scale)


def _pool_bwd(proj, dout, pool_w, pool_scale, dproj, n_seq, name):
    T = proj.shape[0]
    S = T // n_seq

    def body(u_ref, do_ref, w_ref, sc_ref, _, du_ref, dw_ref, dsc_ref, d_s):
        g, b = pl.program_id(0), pl.program_id(1)
        row = lax.broadcasted_iota(jnp.int32, (S, POOL_GD), 0)
        uf = u_ref[...].astype(F32)
        for gi, wlen in enumerate(POOL_WINDOWS):
            @pl.when(g == gi)
            def _(wlen=wlen):
                d_s[...] = _pool_diff(uf, wlen, row).astype(BF16)
        db16 = d_s[...]
        w = w_ref[0]
        e = jnp.dot(db16, w, preferred_element_type=F32)
        dof = do_ref[...].astype(F32)
        dsc = jnp.sum(dof * e, axis=0, keepdims=True)
        de = (dof * sc_ref[...]).astype(BF16)
        dd = lax.dot_general(de, w, (((1,), (1,)), ((), ())), preferred_element_type=F32)
        dw = lax.dot_general(db16, de, (((0,), (0,)), ((), ())), preferred_element_type=F32)
        for gi, wlen in enumerate(POOL_WINDOWS):
            @pl.when(g == gi)
            def _(wlen=wlen):
                n = jnp.minimum(row + 1, wlen).astype(F32)
                acc, k = dd / n, 1
                while k < wlen:
                    acc = acc + _shift_up(acc, k, row, S)
                    k *= 2
                du_ref[...] = (acc - dd).astype(BF16)

        @pl.when(b == 0)
        def _():
            dw_ref[0] = dw
            dsc_ref[...] = dsc

        @pl.when(b > 0)
        def _():
            dw_ref[0] += dw
            dsc_ref[...] += dsc

    uc = OFF_U // POOL_GD
    return pl.pallas_call(
        body, name=name, grid=(len(POOL_WINDOWS), n_seq),
        in_specs=[pl.BlockSpec((S, POOL_GD), lambda g, b: (b, uc + g)),
                  pl.BlockSpec((S, POOL_GD), lambda g, b: (b, g)),
                  pl.BlockSpec((1, POOL_GD, POOL_GD), lambda g, b: (g, 0, 0)),
                  pl.BlockSpec((1, POOL_GD), lambda g, b: (0, g)),
                  pl.BlockSpec(memory_space=pl.ANY)],
        out_specs=[pl.BlockSpec((S, POOL_GD), lambda g, b: (b, uc + g)),
                   pl.BlockSpec((1, POOL_GD, POOL_GD), lambda g, b: (g, 0, 0)),
                   pl.BlockSpec((1, POOL_GD), lambda g, b: (0, g))],
        out_shape=[jax.ShapeDtypeStruct(dproj.shape, BF16),
                   jax.ShapeDtypeStruct((len(POOL_WINDOWS), POOL_GD, POOL_GD), F32),
                   jax.ShapeDtypeStruct((1, BRANCH_W), F32)],
        input_output_aliases={4: 0},
        scratch_shapes=[pltpu.VMEM((S, POOL_GD), BF16)],
        compiler_params=_cp(("parallel", "arbitrary")),
    )(proj, dout, pool_w, pool_scale, dproj)


def _conv_fwd(proj, conv_w, n_seq, name):
    T = proj.shape[0]
    S = T // n_seq
    nc = BRANCH_W // LANES

    def body(c_ref, w_ref, o_ref):
        row = lax.broadcasted_iota(jnp.int32, (S, LANES), 0)
        cv, cb, cc = (c_ref[:, t * LANES:(t + 1) * LANES].astype(F32) for t in range(3))
        z = cc * cv
        w = w_ref[...]
        y = w[0:1] * _shift_down(z, 2, row) + w[1:2] * _shift_down(z, 1, row) + w[2:3] * z
        o_ref[...] = (cb * y).astype(BF16)

    return pl.pallas_call(
        body, name=name, grid=(n_seq, nc),
        in_specs=[pl.BlockSpec((S, TRIPLE), lambda b, j: (b, OFF_CONV // TRIPLE + j)),
                  pl.BlockSpec((CONV_K, LANES), lambda b, j: (0, j))],
        out_specs=pl.BlockSpec((S, LANES), lambda b, j: (b, j)),
        out_shape=jax.ShapeDtypeStruct((T, BRANCH_W), BF16),
        compiler_params=_cp(("parallel", "parallel")),
    )(proj, conv_w)


def _conv_bwd(proj, dout, conv_w, dproj, n_seq, name):
    T = proj.shape[0]
    S = T // n_seq
    nc = BRANCH_W // LANES

    def body(c_ref, do_ref, w_ref, _, dc_ref, dw_ref):
        b = pl.program_id(1)
        row = lax.broadcasted_iota(jnp.int32, (S, LANES), 0)
        cv, cb, cc = (c_ref[:, t * LANES:(t + 1) * LANES].astype(F32) for t in range(3))
        dof = do_ref[...].astype(F32)
        w = w_ref[...]
        z = cc * cv
        z1, z2 = _shift_down(z, 1, row), _shift_down(z, 2, row)
        y = w[0:1] * z2 + w[1:2] * z1 + w[2:3] * z
        dy = dof * cb
        dz = w[2:3] * dy + w[1:2] * _shift_up(dy, 1, row, S) + w[0:1] * _shift_up(dy, 2, row, S)
        dc_ref[:, :LANES] = (dz * cc).astype(BF16)
        dc_ref[:, LANES:2 * LANES] = (dof * y).astype(BF16)
        dc_ref[:, 2 * LANES:] = (dz * cv).astype(BF16)
        dws = [jnp.sum(dy * zk, axis=0, keepdims=True) for zk in (z2, z1, z)]

        @pl.when(b == 0)
        def _():
            for kk in range(CONV_K):
                dw_ref[kk:kk + 1, :] = dws[kk]

        @pl.when(b > 0)
        def _():
            for kk in range(CONV_K):
                dw_ref[kk:kk + 1, :] += dws[kk]

    triple = pl.BlockSpec((S, TRIPLE), lambda j, b: (b, OFF_CONV // TRIPLE + j))
    wsp = pl.BlockSpec((CONV_K, LANES), lambda j, b: (0, j))
    return pl.pallas_call(
        body, name=name, grid=(nc, n_seq),
        in_specs=[triple, pl.BlockSpec((S, LANES), lambda j, b: (b, j)), wsp, pl.BlockSpec(memory_space=pl.ANY)],
        out_specs=[triple, wsp],
        out_shape=[jax.ShapeDtypeStruct(dproj.shape, BF16), jax.ShapeDtypeStruct((CONV_K, BRANCH_W), F32)],
        input_output_aliases={3: 0},
        compiler_params=_cp(("parallel", "arbitrary")),
    )(proj, dout, conv_w, dproj)


def _mix_fwd(oa, ob, oc, wpa, wpp, wpc, proj, b_gate, name):
    T = oa.shape[0]
    tm = min(512, T)

    def body(oa_ref, ob_ref, oc_ref, wa_ref, wp_ref, wc_ref, g_ref, bg_ref, o_ref):
        acc = jnp.zeros((tm, D_MODEL), F32)
        for i, (x_ref, w_ref) in enumerate(((oa_ref, wa_ref), (ob_ref, wp_ref), (oc_ref, wc_ref))):
            y = jnp.dot(x_ref[...], w_ref[...], preferred_element_type=F32)
            sl = slice(i * D_MODEL, (i + 1) * D_MODEL)
            acc = acc + _sigmoid(g_ref[:, sl].astype(F32) + bg_ref[:, sl]) * y
        o_ref[...] = acc.astype(BF16)

    br = pl.BlockSpec((tm, BRANCH_W), lambda i: (i, 0))
    wsp = pl.BlockSpec((BRANCH_W, D_MODEL), lambda i: (0, 0))
    return pl.pallas_call(
        body, name=name, grid=(T // tm,),
        in_specs=[br, br, br, wsp, wsp, wsp, pl.BlockSpec((tm, GATE_W), lambda i: (i, 0)),
                  pl.BlockSpec((1, GATE_W), lambda i: (0, 0))],
        out_specs=pl.BlockSpec((tm, D_MODEL), lambda i: (i, 0)),
        out_shape=jax.ShapeDtypeStruct((T, D_MODEL), BF16),
        compiler_params=_cp(("parallel",)),
    )(oa, ob, oc, wpa, wpp, wpc, proj, b_gate)


def _mix_bwd(oa, ob, oc, wpa, wpp, wpc, proj, b_gate, dmixed, name):
    T = oa.shape[0]
    tm = min(256, T)

    def body(oa_ref, ob_ref, oc_ref, wa_ref, wp_ref, wc_ref, g_ref, bg_ref, dm_ref,
             dya_ref, dyb_ref, dyc_ref, dg_ref, dbg_ref):
        i0 = pl.program_id(0)
        dm = dm_ref[...].astype(F32)
        parts = []
        for i, (x_ref, w_ref, dy_ref) in enumerate(((oa_ref, wa_ref, dya_ref), (ob_ref, wp_ref, dyb_ref),
                                                    (oc_ref, wc_ref, dyc_ref))):
            y = jnp.dot(x_ref[...], w_ref[...], preferred_element_type=F32)
            sl = slice(i * D_MODEL, (i + 1) * D_MODEL)
            gate = _sigmoid(g_ref[:, sl].astype(F32) + bg_ref[:, sl])
            dy_ref[...] = (dm * gate).astype(BF16)
            dgl = dm * y * gate * (1.0 - gate)
            dg_ref[:, sl] = dgl.astype(BF16)
            parts.append(jnp.sum(dgl, axis=0, keepdims=True))

        @pl.when(i0 == 0)
        def _():
            for i in range(3):
                dbg_ref[:, i * D_MODEL:(i + 1) * D_MODEL] = parts[i]

        @pl.when(i0 > 0)
        def _():
            for i in range(3):
                dbg_ref[:, i * D_MODEL:(i + 1) * D_MODEL] += parts[i]

    br = pl.BlockSpec((tm, BRANCH_W), lambda i: (i, 0))
    wsp = pl.BlockSpec((BRANCH_W, D_MODEL), lambda i: (0, 0))
    row = pl.BlockSpec((tm, D_MODEL), lambda i: (i, 0))
    gsp = pl.BlockSpec((tm, GATE_W), lambda i: (i, 0))
    bsp = pl.BlockSpec((1, GATE_W), lambda i: (0, 0))
    act = jax.ShapeDtypeStruct((T, D_MODEL), BF16)
    return pl.pallas_call(
        body, name=name, grid=(T // tm,),
        in_specs=[br, br, br, wsp, wsp, wsp, gsp, bsp, row],
        out_specs=[row, row, row, gsp, bsp],
        out_shape=[act, act, act, jax.ShapeDtypeStruct((T, MAIN_COLS), BF16),
                   jax.ShapeDtypeStruct((1, GATE_W), F32)],
        compiler_params=_cp(("arbitrary",)),
    )(oa, ob, oc, wpa, wpp, wpc, proj, b_gate, dmixed)


GU_TILE = 256


def _gu_col(c):
    t, r = divmod(c, GU_TILE)
    return (t // 2) * GU_TILE + r + (FFN_HIDDEN if t % 2 else 0)


def _gate_up_swiglu(h, w, name):
    T, K = h.shape
    tm = min(2048, T)

    def body(h_ref, w_ref, ab_ref, s_ref):
        prod = jnp.dot(h_ref[...], w_ref[...], preferred_element_type=F32)
        ab_ref[...] = prod.astype(BF16)
        a = prod[:, :GU_TILE]
        s_ref[...] = (a * _sigmoid(a) * prod[:, GU_TILE:]).astype(BF16)

    return pl.pallas_call(
        body, name=name, grid=(T // tm, FFN_HIDDEN // GU_TILE),
        in_specs=[pl.BlockSpec((tm, K), lambda i, j: (i, 0)), pl.BlockSpec((K, 2 * GU_TILE), lambda i, j: (0, j))],
        out_specs=[pl.BlockSpec((tm, 2 * GU_TILE), lambda i, j: (i, j)), pl.BlockSpec((tm, GU_TILE), lambda i, j: (i, j))],
        out_shape=[jax.ShapeDtypeStruct((T, 2 * FFN_HIDDEN), BF16), jax.ShapeDtypeStruct((T, FFN_HIDDEN), BF16)],
        compiler_params=_cp(("parallel", "parallel")),
    )(h, w)


def _swiglu_bwd_fused(dx, w_down, ab, name):
    T, K = dx.shape
    tm = min(2048, T)

    def body(dx_ref, w_ref, ab_ref, o_ref):
        ds = lax.dot_general(dx_ref[...], w_ref[...], _NT, preferred_element_type=F32)
        a = ab_ref[:, :GU_TILE].astype(F32)
        b = ab_ref[:, GU_TILE:].astype(F32)
        sg = _sigmoid(a)
        o_ref[:, :GU_TILE] = (ds * b * sg * (1.0 + a * (1.0 - sg))).astype(BF16)
        o_ref[:, GU_TILE:] = (ds * a * sg).astype(BF16)

    pair = pl.BlockSpec((tm, 2 * GU_TILE), lambda i, j: (i, j))
    return pl.pallas_call(
        body, name=name, grid=(T // tm, FFN_HIDDEN // GU_TILE),
        in_specs=[pl.BlockSpec((tm, K), lambda i, j: (i, 0)), pl.BlockSpec((GU_TILE, K), lambda i, j: (j, 0)), pair],
        out_specs=pair, out_shape=jax.ShapeDtypeStruct((T, 2 * FFN_HIDDEN), BF16),
        compiler_params=_cp(("parallel", "parallel")),
    )(dx, w_down, ab)


def _adamw_update(w_ref, g_ref, m_ref, v_ref, d_ref, nm_ref, nv_ref):
    gv = g_ref[...]
    nm = ADAM_B1 * m_ref[...] + (1.0 - ADAM_B1) * gv
    nv = ADAM_B2 * v_ref[...] + (1.0 - ADAM_B2) * (gv * gv)
    m_hat = nm / (1.0 - ADAM_B1 ** ADAM_STEP)
    v_hat = nv / (1.0 - ADAM_B2 ** ADAM_STEP)
    d_ref[...] = -ADAM_LR * (m_hat / (jnp.sqrt(v_hat) + ADAM_EPS) + ADAM_WD * w_ref[...])
    nm_ref[...] = nm
    nv_ref[...] = nv


def _adamw_many(ws, gs, ms, vs, name):
    n = len(ws)

    def body(*refs):
        ins, outs = refs[:4 * n], refs[4 * n:]
        for t in range(n):
            _adamw_update(ins[t], ins[n + t], ins[2 * n + t], ins[3 * n + t], outs[t], outs[n + t], outs[2 * n + t])

    shapes = [jax.ShapeDtypeStruct(w.shape, F32) for w in ws]
    out = pl.pallas_call(body, name=name, out_shape=shapes * 3, compiler_params=_cp())(*ws, *gs, *ms, *vs)
    return out[:n], out[n:2 * n], out[2 * n:]


def _adamw(w, g, m, v, name):
    R, C = w.shape
    tr = R
    for cand in (256, 352, 128, 64, 8):
        if R > cand and R % cand == 0:
            tr = cand
            break

    def body(w_ref, g_ref, m_ref, v_ref, d_ref, nm_ref, nv_ref):
        _adamw_update(w_ref, g_ref, m_ref, v_ref, d_ref, nm_ref, nv_ref)

    blk = pl.BlockSpec((tr, C), lambda i: (i, 0))
    sh = jax.ShapeDtypeStruct((R, C), F32)
    return pl.pallas_call(
        body, name=name, grid=(R // tr,), in_specs=[blk] * 4, out_specs=[blk] * 3, out_shape=[sh] * 3,
        compiler_params=_cp(("parallel",)),
    )(w, g, m, v)


def _sum_slabs(x, name):
    n, R, C = x.shape
    tr = R
    for cand in (512, 256, 128, 64, 32, 16, 8):
        if R > cand and R % cand == 0:
            tr = cand
            break

    def body(x_ref, o_ref):
        acc = x_ref[0].astype(F32)
        for j in range(1, n):
            acc = acc + x_ref[j].astype(F32)
        o_ref[...] = acc

    return pl.pallas_call(
        body, name=name, grid=(R // tr,), in_specs=[pl.BlockSpec((n, tr, C), lambda i: (0, i, 0))],
        out_specs=pl.BlockSpec((tr, C), lambda i: (i, 0)), out_shape=jax.ShapeDtypeStruct((R, C), F32),
        compiler_params=_cp(("parallel",)),
    )(x)


def _multi_gather(xs, layers, name):
    nt = len(xs)
    shapes = [x.shape if lay is None else x.shape[1:] for x, lay in zip(xs, layers)]

    def body(*refs):
        x_refs, out_refs = refs[:nt], refs[nt:2 * nt]
        send_sems, recv_sems, local_sems = refs[2 * nt:]
        x_, y_, c_ = lax.axis_index("x"), lax.axis_index("y"), lax.axis_index("c")
        me, sibling = (x_, y_, c_), (x_, y_, 1 - c_)
        chips = [(1 - x_, y_), (x_, 1 - y_), (1 - x_, 1 - y_)]

        def own_block(t):
            return x_refs[t] if layers[t] is None else x_refs[t].at[layers[t]]

        def copy(t, k, block, to, own=False):
            px, py, pc = block
            dst = out_refs[t].at[4 * px + 2 * py + pc]
            return pltpu.make_async_remote_copy(
                src_ref=own_block(t) if own else dst, dst_ref=dst,
                send_sem=send_sems.at[t, k], recv_sem=recv_sems.at[t, k],
                device_id=to, device_id_type=pl.DeviceIdType.MESH)

        mine, first, passed = [], [], []
        for t in range(nt):
            mine.append(pltpu.make_async_copy(own_block(t), out_refs[t].at[4 * x_ + 2 * y_ + c_], local_sems.at[t]))
            mine[-1].start()
            first.append([copy(t, 1 + j, me, (*chip, c_), own=True) for j, chip in enumerate(chips)]
                         + [copy(t, 0, me, sibling, own=True)])
            for cp in first[-1]:
                cp.start()
        for t in range(nt):
            for j, chip in enumerate(chips):
                copy(t, 1 + j, (*chip, c_), me).wait_recv()
                passed.append(copy(t, 4 + j, (*chip, c_), sibling))
                passed[-1].start()
        for t in range(nt):
            copy(t, 0, sibling, me).wait_recv()
            for j, chip in enumerate(chips):
                copy(t, 4 + j, (*chip, 1 - c_), me).wait_recv()
        for cp in [c for f in first for c in f] + passed:
            cp.wait_send()
        for cp in mine:
            cp.wait()

    hbm = pl.BlockSpec(memory_space=pl.ANY)
    return pl.pallas_call(
        body, name=name, out_shape=[jax.ShapeDtypeStruct((N_DEV,) + tuple(s), x.dtype) for s, x in zip(shapes, xs)],
        in_specs=[hbm] * nt, out_specs=[hbm] * nt,
        scratch_shapes=[pltpu.SemaphoreType.DMA((nt, 7)), pltpu.SemaphoreType.DMA((nt, 7)),
                        pltpu.SemaphoreType.DMA((nt,))],
    )(*xs)


_HBM = pl.BlockSpec(memory_space=pltpu.HBM)
_SEM = pl.BlockSpec(memory_space=pltpu.SEMAPHORE)
_PEER_ORDER = (2, 4, 6, 3, 5, 7, 1)


def _split_copies(src_refs, land_refs, send_sems, recv_sems, layers, per_peer):
    x_, y_, c_ = lax.axis_index("x"), lax.axis_index("y"), lax.axis_index("c")
    me = 4 * x_ + 2 * y_ + c_
    copies = []
    for k in _PEER_ORDER:
        px, py, pc = x_ ^ ((k >> 2) & 1), y_ ^ ((k >> 1) & 1), c_ ^ (k & 1)
        peer = 4 * px + 2 * py + pc
        for t in range(len(src_refs)):
            if per_peer:
                src = src_refs[t].at[peer]
            else:
                src = src_refs[t] if layers[t] is None else src_refs[t].at[layers[t]]
            copies.append(pltpu.make_async_remote_copy(
                src_ref=src, dst_ref=land_refs[t].at[me],
                send_sem=send_sems.at[t * (N_DEV - 1) + k - 1], recv_sem=recv_sems.at[t * (N_DEV - 1) + k - 1],
                device_id=(px, py, pc), device_id_type=pl.DeviceIdType.MESH))
    return copies


def _own_copies(src_refs, land_refs, sems, layers, per_peer):
    nt = len(src_refs)
    me = 4 * lax.axis_index("x") + 2 * lax.axis_index("y") + lax.axis_index("c")
    copies = []
    for t in range(nt):
        if per_peer:
            src = src_refs[t].at[me]
        else:
            src = src_refs[t] if layers[t] is None else src_refs[t].at[layers[t]]
        copies.append(pltpu.make_async_copy(src, land_refs[t].at[me], sems.at[nt * (N_DEV - 1) + t]))
    return copies


def _split_start(srcs, layers, per_peer, after, name):
    nt = len(srcs)
    if per_peer:
        land_shapes = [s.shape for s in srcs]
    else:
        land_shapes = [(N_DEV,) + tuple(s.shape if lay is None else s.shape[1:]) for s, lay in zip(srcs, layers)]

    def body(*refs):
        src_refs, land_refs = refs[:nt], refs[nt:2 * nt]
        send_sems, recv_sems = refs[2 * nt + 1], refs[2 * nt + 2]
        token = refs[-1]
        for cp in _split_copies(src_refs, land_refs, send_sems, recv_sems, layers, per_peer):
            cp.start()
        for cp in _own_copies(src_refs, land_refs, send_sems, layers, per_peer):
            cp.start()
        token[...] = jnp.zeros_like(token)

    lands = [pltpu.with_memory_space_constraint(lax.empty(s, x.dtype), pltpu.HBM) for s, x in zip(land_shapes, srcs)]
    srcs = [pltpu.with_memory_space_constraint(x, pltpu.HBM) for x in srcs]
    out = pl.pallas_call(
        body, name=name,
        out_shape=(pltpu.SemaphoreType.DMA((nt * N_DEV,)), pltpu.SemaphoreType.DMA((nt * (N_DEV - 1),)),
                   *[pltpu.HBM(x.shape, x.dtype) for x in srcs], *[pltpu.HBM(s, x.dtype) for s, x in zip(land_shapes, srcs)],
                   jax.ShapeDtypeStruct((8, LANES), F32)),
        in_specs=[_HBM] * (2 * nt) + [pl.BlockSpec(memory_space=pl.ANY)],
        out_specs=(_SEM, _SEM, *([_HBM] * (2 * nt)), pl.BlockSpec(memory_space=pltpu.VMEM)),
        input_output_aliases={i: 2 + i for i in range(2 * nt)},
        compiler_params=pltpu.CompilerParams(has_side_effects=pltpu.SideEffectType.DATAFLOW_SIDE_EFFECTING),
    )(*srcs, *lands, after)
    return out[0], out[1], list(out[2:2 + nt]), list(out[2 + nt:2 + 2 * nt]), out[-1]


def _split_wait(started, layers, per_peer, after, name):
    send_sems, recv_sems, srcs, lands, _ = started
    nt = len(srcs)

    def body(*refs):
        src_refs, land_refs = refs[:nt], refs[nt:2 * nt]
        s_sems, r_sems = refs[2 * nt], refs[2 * nt + 1]
        for cp in _split_copies(src_refs, land_refs, s_sems, r_sems, layers, per_peer):
            cp.wait_send()
            cp.wait_recv()
        for cp in _own_copies(src_refs, land_refs, s_sems, layers, per_peer):
            cp.wait()

    out = pl.pallas_call(
        body, name=name,
        out_shape=tuple(pltpu.HBM(x.shape, x.dtype) for x in srcs + lands),
        in_specs=[_HBM] * (2 * nt) + [_SEM, _SEM, pl.BlockSpec(memory_space=pl.ANY)],
        out_specs=tuple([_HBM] * (2 * nt)),
        input_output_aliases={i: i for i in range(2 * nt)},
        compiler_params=pltpu.CompilerParams(has_side_effects=pltpu.SideEffectType.DATAFLOW_SIDE_EFFECTING),
    )(*srcs, *lands, send_sems, recv_sems, after)
    return list(out[nt:])


def _runs(mapping):
    runs, c, n = [], 0, len(mapping)
    while c < n:
        if mapping[c] is None:
            c += 1
            continue
        sid, d, lo = mapping[c][0], mapping[c][1] - c, c
        while c < n and mapping[c] is not None and mapping[c][0] == sid and mapping[c][1] - c == d:
            c += 1
        runs.append((lo, c, sid, d))
    return runs


def _tile_plan(mapping, src_widths):
    runs = _runs(mapping)
    plan = []
    for t in range(len(mapping) // LANES):
        pieces = []
        for lo, hi, sid, d in runs:
            lo_t, hi_t = max(lo, t * LANES), min(hi, (t + 1) * LANES)
            if lo_t >= hi_t:
                continue
            a = ((lo_t + d) // LANES) * LANES
            win = min(2 * LANES, src_widths[sid] - a)
            shift = t * LANES + d - a
            pieces.append((sid, a, win, shift, lo_t - t * LANES, hi_t - t * LANES))
        plan.append(pieces)
    return plan


def _reblock(srcs, src_views, outs, out_views, name):
    R = srcs[0].shape[-2]
    tr = min(512, R)
    widths = {sid: srcs[ai].shape[-1] for sid, (ai, _) in src_views.items()}
    plans = [(ai, li, _tile_plan(mapping, widths)) for ai, li, mapping in out_views]
    ns = len(srcs)

    def body(*refs):
        s_refs, o_refs = refs[:ns], refs[ns:]
        cache = {}

        def shift_matrix(win, shift, lo, hi):
            key = (win, shift, lo, hi)
            if key not in cache:
                r = lax.broadcasted_iota(jnp.int32, (win, LANES), 0)
                c = lax.broadcasted_iota(jnp.int32, (win, LANES), 1)
                hit = jnp.logical_and(r - c == shift, jnp.logical_and(c >= lo, c < hi))
                cache[key] = jnp.where(hit, 1.0, 0.0).astype(BF16)
            return cache[key]

        for ai, li, plan in plans:
            for t, pieces in enumerate(plan):
                acc = None
                whole = len(pieces) == 1 and pieces[0][3:] == (0, 0, LANES)
                for sid, a, win, shift, lo, hi in pieces:
                    sa, sl = src_views[sid]
                    if whole:
                        win = LANES
                    src = s_refs[sa][:, a:a + win] if sl is None else s_refs[sa][sl, :, a:a + win]
                    if whole:
                        acc = src
                    else:
                        part = jnp.dot(src, shift_matrix(win, shift, lo, hi), preferred_element_type=F32)
                        acc = part if acc is None else acc + part
                val = jnp.zeros((tr, LANES), BF16) if acc is None else acc.astype(BF16)
                if li is None:
                    o_refs[ai][:, t * LANES:(t + 1) * LANES] = val
                else:
                    o_refs[ai][li, :, t * LANES:(t + 1) * LANES] = val

    def spec(shape):
        if len(shape) == 2:
            return pl.BlockSpec((tr, shape[1]), lambda i: (i, 0))
        return pl.BlockSpec((shape[0], tr, shape[2]), lambda i: (0, i, 0))

    return pl.pallas_call(
        body, name=name, grid=(R // tr,), in_specs=[spec(s.shape) for s in srcs],
        out_specs=[spec(s) for s in outs], out_shape=[jax.ShapeDtypeStruct(s, BF16) for s in outs],
        compiler_params=_cp(("parallel",)),
    )(*srcs)


SHARDED = ("w_in", "w_gate_up", "w_proj_attn", "w_proj_pool", "w_proj_conv", "w_out", "w_down")
WEIGHT_ORDER = ("attn_norm", "w_in", "b_forget", "b_gate", "w_proj_attn", "pool_w", "pool_scale", "w_proj_pool",
                "conv_w", "w_proj_conv", "w_out", "ffn_norm", "w_gate_up", "w_down", "final_norm")
IN_SHARD, IN_SHARD_PAD = IN_COLS // N_DEV, 896
GU_SHARD, GU_SHARD_PAD = 2 * FFN_HIDDEN // N_DEV, 768


def _w_in_col(c):
    if c < OFF_QKV:
        return c + 3592
    if c < OFF_U:
        base, off = (0, OFF_QKV) if c < OFF_CONV else (2056, OFF_CONV)
        j, t = divmod(c - off, TRIPLE)
        which, e = divmod(t, LANES)
        return base + which * BRANCH_W + j * LANES + e
    return c - OFF_U + 1544


def _w_in_full(gathered, name):
    main = [divmod(_w_in_col(c), IN_SHARD) for c in range(MAIN_COLS)]
    fcols = [divmod(1536 + c, IN_SHARD) if c < N_HEADS else None for c in range(LANES)]
    R = gathered.shape[1]
    return _reblock([gathered], {i: (0, i) for i in range(N_DEV)}, [(R, MAIN_COLS), (R, LANES)],
                    [(0, None, main), (1, None, fcols)], name)


def _w_in_slabs(dmain, dwf, name):
    inv = {_w_in_col(c): ("m", c) for c in range(MAIN_COLS)}
    inv.update({1536 + c: ("f", c) for c in range(N_HEADS)})
    views = []
    for i in range(N_DEV):
        mapping = [inv[IN_SHARD * i + j] if j < IN_SHARD else None for j in range(IN_SHARD_PAD)]
        views.append((0, i, mapping))
    R = dmain.shape[0]
    return _reblock([dmain, dwf], {"m": (0, None), "f": (1, None)}, [(N_DEV, R, IN_SHARD_PAD)], views, name)[0]


def _w_gu_full(gathered, name):
    mapping = [divmod(_gu_col(c), GU_SHARD) for c in range(2 * FFN_HIDDEN)]
    R = gathered.shape[1]
    return _reblock([gathered], {i: (0, i) for i in range(N_DEV)}, [(R, 2 * FFN_HIDDEN)], [(0, None, mapping)], name)[0]


def _w_gu_slabs(dw, name):
    inv = {_gu_col(c): c for c in range(2 * FFN_HIDDEN)}
    views = [(0, i, [("w", inv[GU_SHARD * i + j]) if j < GU_SHARD else None for j in range(GU_SHARD_PAD)])
             for i in range(N_DEV)]
    R = dw.shape[0]
    return _reblock([dw], {"w": (0, None)}, [(N_DEV, R, GU_SHARD_PAD)], views, name)[0]


def _layer_fwd(x, W, n_seq, l, h1=None, next_norm=None):
    T = x.shape[0]
    sfx = f"_l{l}"
    if h1 is None:
        h1 = _rms_fwd(x, W["attn_norm"], "rms1" + sfx)
    proj, f = _matmul(h1, W["w_main"], mode="nn", out_dtype=BF16, name="proj_main" + sfx, side=(W["w_f"], F32))
    qa, ka, va = _fox_prep(f, W["b_forget"], proj, n_seq, "fox_prep" + sfx)
    oa, oa32, lse = _attn_fwd2(qa, ka, va, n_seq, "attn_fwd" + sfx)
    if "late" in W:
        W.update(W.pop("late")(oa))
    ob = _pool_fwd(proj, W["pool_w"], W["pool_scale"], n_seq, "pool_fwd" + sfx)
    oc = _conv_fwd(proj, W["conv_w"], n_seq, "conv_fwd" + sfx)
    mixed = _mix_fwd(oa, ob, oc, W["w_proj_attn"], W["w_proj_pool"], W["w_proj_conv"], proj, W["b_gate"],
                     "mix_fwd" + sfx)
    x2, h2 = _matmul(mixed, W["w_out"], mode="nn", out_dtype=F32, name="out_proj" + sfx, tm=1024, tn=1024,
                     residual=x, rms_g=W["ffn_norm"])
    ab, s = _gate_up_swiglu(h2, W["w_gate_up"], "gate_up" + sfx)
    x3 = _matmul(s, W["w_down"], mode="nn", out_dtype=F32, name="down" + sfx, tm=1024, tn=1024, tk=1408,
                 residual=x2, rms_g=next_norm)
    x3, h1_next = x3 if next_norm is not None else (x3, None)
    saved = dict(x=x, h1=h1, proj=proj, f=f, qa=qa, ka=ka, oa=oa, oa32=oa32, lse=lse, ob=ob, oc=oc, mixed=mixed, x2=x2,
                 h2=h2, ab=ab, s=s)
    return x3, saved, h1_next


def _layer_bwd(dx3, dx3b, W, sv, n_seq, l, stage=None):
    T = dx3.shape[0]
    sfx = f"_l{l}"
    G = {}
    stage = stage or (lambda l, group, G, W: W)
    dab = _swiglu_bwd_fused(dx3b, W["w_down"], sv["ab"], "d_ab" + sfx)
    G["w_down"] = _matmul(sv["s"], dx3b, mode="tn", out_dtype=BF16, name="dw_down" + sfx, tm=256, tn=1024)
    dh2 = _matmul(dab, W["w_gate_up"], mode="nt", out_dtype=BF16, name="d_h2" + sfx, tm=1024, tn=1024, tk=1408)
    G["w_gate_up"] = _matmul(sv["h2"], dab, mode="tn", out_dtype=BF16, name="dw_gate_up" + sfx, tm=1024)
    W = stage(l, "ffn", G, W)
    dx2, dx2b, G["ffn_norm"] = _rms_bwd(sv["x2"], W["ffn_norm"], dh2, dx3, "rms2_bwd" + sfx)
    dmixed = _matmul(dx2b, W["w_out"], mode="nt", out_dtype=BF16, name="d_mixed" + sfx)
    G["w_out"] = _matmul(sv["mixed"], dx2b, mode="tn", out_dtype=BF16, name="dw_out" + sfx, tm=1024)
    dya, dyb, dyc, dproj, G["b_gate"] = _mix_bwd(sv["oa"], sv["ob"], sv["oc"], W["w_proj_attn"], W["w_proj_pool"],
                                                 W["w_proj_conv"], sv["proj"], W["b_gate"], dmixed, "mix_bwd" + sfx)
    douts = {}
    for br, dy, o in (("attn", dya, sv["oa"]), ("pool", dyb, sv["ob"]), ("conv", dyc, sv["oc"])):
        douts[br] = _matmul(dy, W["w_proj_" + br], mode="nt", out_dtype=BF16, name=f"d_{br}_out" + sfx)
        G["w_proj_" + br] = _matmul(o, dy, mode="tn", out_dtype=BF16, name=f"dw_proj_{br}" + sfx, tm=512)
    W = stage(l, "mix", G, W)
    dproj, G["conv_w"] = _conv_bwd(sv["proj"], douts["conv"], W["conv_w"], dproj, n_seq, "conv_bwd" + sfx)
    dproj, G["pool_w"], G["pool_scale"] = _pool_bwd(sv["proj"], douts["pool"], W["pool_w"], W["pool_scale"], dproj,
                                                    n_seq, "pool_bwd" + sfx)
    dproj, dFk = _attn_bwd(sv["qa"], sv["ka"], sv["proj"], douts["attn"], sv["oa32"], sv["lse"], dproj, n_seq,
                           "attn_bwd" + sfx)
    dF = jnp.pad(dFk.reshape(N_HEADS, T).T, ((0, 0), (0, LANES - N_HEADS)))
    df, G["b_forget"] = _fox_cumsum_bwd(sv["f"], W["b_forget"], dF, n_seq, "fox_cumsum_bwd" + sfx)
    G["w_main"], G["w_f"] = _matmul(sv["h1"], dproj, mode="tn", out_dtype=BF16, name="dw_main" + sfx, tm=1024,
                                    side=(df, BF16))
    W = stage(l, "w_in", G, W)
    dh1 = _matmul(dproj, W["w_main"], mode="nt", out_dtype=BF16, name="d_h1_main" + sfx, tm=1024, tn=1024, tk=1664,
                  extra=(df, W["w_f"]))
    dx, dxb, G["attn_norm"] = _rms_bwd(sv["x"], W["attn_norm"], dh1, dx2, "rms1_bwd" + sfx)
    return dx, dxb, G


def _replicated_operands(rep, l):
    W = {}
    W["attn_norm"], W["ffn_norm"] = rep["attn_norm"][l], rep["ffn_norm"][l]
    W["b_forget"] = jnp.pad(rep["b_forget"][l].reshape(1, N_HEADS), ((0, 0), (0, LANES - N_HEADS)))
    W["b_gate"] = rep["b_gate"][l].reshape(1, GATE_W)
    W["pool_w"] = rep["pool_w"][l].astype(BF16)
    W["pool_scale"] = rep["pool_scale"][l].reshape(1, BRANCH_W)
    return W


def _local_step(x, target, get_W, attn_norms, final_norm, stage=None):
    n_seq, S, Dm = x.shape
    T = n_seq * S
    xt = x.reshape(T, Dm)
    saved, Ws, h1 = [], [], None
    for l in range(DEPTH):
        Ws.append(get_W(l, xt))
        next_norm = attn_norms[l + 1] if l + 1 < DEPTH else None
        xt, sv, h1 = _layer_fwd(xt, Ws[l], n_seq, l, h1, next_norm)
        saved.append(sv)
    loss, dx, dxb, g_final = _loss_head(xt, final_norm, target.reshape(T, Dm), "loss_head")
    grads = [None] * DEPTH
    for l in reversed(range(DEPTH)):
        dx, dxb, grads[l] = _layer_bwd(dx, dxb, Ws[l], saved[l], n_seq, l, stage)
    return loss, dx.reshape(n_seq, S, Dm), grads, g_final


def _padded_shards(weights):
    pads = {"w_in": IN_SHARD_PAD - IN_SHARD, "w_gate_up": GU_SHARD_PAD - GU_SHARD}
    return {n: jnp.pad(weights[n], ((0, 0), (0, 0), (0, pads.get(n, 0)))).astype(BF16) for n in SHARDED}


def _full_operands(g, l):
    W = {}
    if "w_in" in g:
        W["w_main"], W["w_f"] = _w_in_full(g["w_in"], f"w_in_full_l{l}")
    if "w_gate_up" in g:
        W["w_gate_up"] = _w_gu_full(g["w_gate_up"], f"w_gate_up_full_l{l}")
    for n in ("w_proj_attn", "w_proj_pool", "w_proj_conv"):
        if n in g:
            W[n] = jnp.transpose(g[n], (1, 0, 2)).reshape(BRANCH_W, D_MODEL)
    if "w_out" in g:
        W["w_out"] = g["w_out"].reshape(D_MODEL, D_MODEL)
    if "w_down" in g:
        W["w_down"] = g["w_down"].reshape(FFN_HIDDEN, D_MODEL)
    return W


GRAD_GROUPS = {"ffn": ("w_down", "w_gate_up"),
               "mix": ("w_out", "w_proj_attn", "w_proj_pool", "w_proj_conv"),
               "w_in": ("w_in",)}


def _grad_slabs(G, n, l):
    if n == "w_in":
        return _w_in_slabs(G["w_main"], G["w_f"], f"w_in_slabs_l{l}")
    if n == "w_gate_up":
        return _w_gu_slabs(G["w_gate_up"], f"w_gate_up_slabs_l{l}")
    if n == "w_out":
        return G["w_out"].reshape(N_DEV, D_MODEL // N_DEV, D_MODEL)
    if n == "w_down":
        return G["w_down"].reshape(N_DEV, FFN_HIDDEN // N_DEV, D_MODEL)
    return jnp.transpose(G[n].reshape(BRANCH_W, N_DEV, D_MODEL // N_DEV), (1, 0, 2))


def _sum_layer_grads(recv, l):
    out = {n: _sum_slabs(r, f"sum_{n}_l{l}") for n, r in recv.items()}
    if "w_in" in out:
        out["w_in"] = out["w_in"][:, :IN_SHARD]
    if "w_gate_up" in out:
        out["w_gate_up"] = out["w_gate_up"][:, :GU_SHARD]
    return out


def _sum_small(xs, name):
    def body(*refs):
        for x_ref, o_ref in zip(refs[:len(xs)], refs[len(xs):]):
            acc = x_ref[0]
            for j in range(1, N_DEV):
                acc = acc + x_ref[j]
            o_ref[...] = acc

    return pl.pallas_call(
        body, name=name, out_shape=[jax.ShapeDtypeStruct(x.shape[1:], F32) for x in xs],
        compiler_params=_cp(),
    )(*xs)


def _as_2d(a):
    if a.ndim == 1:
        return a.reshape(1, -1)
    return a.reshape(-1, a.shape[-1])


def kernel(x, attn_norm, w_in, b_forget, b_gate, w_proj_attn, pool_w, pool_scale, w_proj_pool, conv_w, w_proj_conv, w_out, ffn_norm, w_gate_up, w_down, final_norm, loss_target, m_attn_norm, m_w_in, m_b_forget, m_b_gate, m_w_proj_attn, m_pool_w, m_pool_scale, m_w_proj_pool, m_conv_w, m_w_proj_conv, m_w_out, m_ffn_norm, m_w_gate_up, m_w_down, m_final_norm, v_attn_norm, v_w_in, v_b_forget, v_b_gate, v_w_proj_attn, v_pool_w, v_pool_scale, v_w_proj_pool, v_conv_w, v_w_proj_conv, v_w_out, v_ffn_norm, v_w_gate_up, v_w_down, v_final_norm):
    weights = dict(attn_norm=attn_norm, w_in=w_in, b_forget=b_forget, b_gate=b_gate, w_proj_attn=w_proj_attn,
                   pool_w=pool_w, pool_scale=pool_scale, w_proj_pool=w_proj_pool, conv_w=conv_w,
                   w_proj_conv=w_proj_conv, w_out=w_out, ffn_norm=ffn_norm, w_gate_up=w_gate_up, w_down=w_down,
                   final_norm=final_norm)
    moments_m = dict(attn_norm=m_attn_norm, w_in=m_w_in, b_forget=m_b_forget, b_gate=m_b_gate,
                     w_proj_attn=m_w_proj_attn, pool_w=m_pool_w, pool_scale=m_pool_scale, w_proj_pool=m_w_proj_pool,
                     conv_w=m_conv_w, w_proj_conv=m_w_proj_conv, w_out=m_w_out, ffn_norm=m_ffn_norm,
                     w_gate_up=m_w_gate_up, w_down=m_w_down, final_norm=m_final_norm)
    moments_v = dict(attn_norm=v_attn_norm, w_in=v_w_in, b_forget=v_b_forget, b_gate=v_b_gate,
                     w_proj_attn=v_w_proj_attn, pool_w=v_pool_w, pool_scale=v_pool_scale, w_proj_pool=v_w_proj_pool,
                     conv_w=v_conv_w, w_proj_conv=v_w_proj_conv, w_out=v_w_out, ffn_norm=v_ffn_norm,
                     w_gate_up=v_w_gate_up, w_down=v_w_down, final_norm=v_final_norm)

    sh = _padded_shards(weights)
    names = list(SHARDED)
    rest = [n for n in names if n != "w_in"]
    me = 4 * lax.axis_index("x") + 2 * lax.axis_index("y") + lax.axis_index("c")
    w_in0, conv_all = _multi_gather([sh["w_in"], conv_w], [0, None], "gather_w_in_l0")
    started, after = {}, w_in0
    for l in range(DEPTH):
        for group, gnames in (("w_in", ["w_in"]), ("rest", rest)):
            if (l, group) != (0, "w_in"):
                started[l, group] = _split_start([sh[n] for n in gnames], [l] * len(gnames), False, after,
                                                 f"gather_start_{group}_l{l}")
                after = started[l, group][4]
    last_token = after

    def get_W(l, xt):
        if l == 0:
            w_in = w_in0
        else:
            w_in = _split_wait(started[l, "w_in"], [l], False, xt, f"gather_wait_w_in_l{l}")[0]
        W = _full_operands({"w_in": w_in}, l)

        def late(after):
            lands = _split_wait(started[l, "rest"], [l] * len(rest), False, after, f"gather_wait_rest_l{l}")
            return _full_operands(dict(zip(rest, lands)), l)

        W["late"] = late
        W.update(_replicated_operands(weights, l))
        W["conv_w"] = jnp.transpose(conv_all[:, l], (1, 0, 2)).reshape(CONV_K, BRANCH_W)
        if l == 0:
            W["attn_norm"] = W["attn_norm"] + last_token[0, 0]
        return W

    exchanges = []

    def stage(l, group, G, W):
        gnames = GRAD_GROUPS[group]
        slabs = [_grad_slabs(G, n, l) for n in gnames]
        started = _split_start(slabs, None, True, slabs[0], f"exchange_start_{group}_l{l}")
        exchanges.append((l, group, gnames, slabs, started))
        tie = {"ffn": "ffn_norm", "mix": "conv_w", "w_in": "w_f"}[group]
        W = dict(W)
        W[tie] = W[tie] + started[4][0, 0].astype(W[tie].dtype)
        return W

    loss_part, grad_x, grads, g_final = _local_step(x, loss_target, get_W, attn_norm, final_norm, stage)
    after = grad_x
    for l, group, gnames, slabs, started in exchanges:
        lands = _split_wait(started, None, True, after, f"exchange_wait_{group}_l{l}")
        grads[l].update(_sum_layer_grads(dict(zip(gnames, lands)), l))
    gw = {n: jnp.stack([grads[l][n] for l in range(DEPTH)]) for n in SHARDED}

    small = ("attn_norm", "b_forget", "b_gate", "pool_w", "pool_scale", "ffn_norm", "conv_w")
    parts = [jnp.stack([grads[l][n] for l in range(DEPTH)]) for n in small] + [g_final, loss_part]
    gathered = _multi_gather(parts, [None] * len(parts), "gather_small_grads")
    summed = _sum_small(gathered, "sum_small_grads")
    for n, s in zip(small, summed):
        gw[n] = s
    gw["attn_norm"], gw["ffn_norm"] = gw["attn_norm"][:, 0], gw["ffn_norm"][:, 0]
    gw["b_forget"] = gw["b_forget"][:, 0, :N_HEADS]
    gw["b_gate"], gw["pool_scale"] = gw["b_gate"][:, 0], gw["pool_scale"][:, 0]
    gw["conv_w"] = lax.dynamic_slice_in_dim(gw["conv_w"], me * (BRANCH_W // N_DEV), BRANCH_W // N_DEV, axis=2)
    gw["final_norm"] = summed[-2][0]
    loss = summed[-1][0, 0]

    deltas, new_m, new_v = {}, {}, {}
    for n in SHARDED:
        shape = weights[n].shape
        d, nm, nv = _adamw(_as_2d(weights[n]), _as_2d(gw[n]), _as_2d(moments_m[n]), _as_2d(moments_v[n]),
                           "adamw_" + n)
        deltas[n], new_m[n], new_v[n] = d.reshape(shape), nm.reshape(shape), nv.reshape(shape)
    rest_names = [n for n in WEIGHT_ORDER if n not in SHARDED]
    ds, nms, nvs = _adamw_many(*[[_as_2d(src[n]) for n in rest_names] for src in (weights, gw, moments_m, moments_v)],
                               "adamw_small")
    for n, d, nm, nv in zip(rest_names, ds, nms, nvs):
        shape = weights[n].shape
        deltas[n], new_m[n], new_v[n] = d.reshape(shape), nm.reshape(shape), nv.reshape(shape)

    return (loss, grad_x, *[gw[n] for n in WEIGHT_ORDER], *[deltas[n] for n in WEIGHT_ORDER],
            *[new_m[n] for n in WEIGHT_ORDER], *[new_v[n] for n in WEIGHT_ORDER])
```

```python
import functools

import jax
import jax.numpy as jnp
from jax import lax
from jax.experimental import pallas as pl
from jax.experimental.pallas import tpu as pltpu

F32 = jnp.float32
BF16 = jnp.bfloat16

N_DEV = 8
D_MODEL = 1024
DEPTH = 2
N_HEADS = 8
HEAD_DIM = 64
BRANCH_W = 512
POOL_WINDOWS = (2, 4, 8, 16)
POOL_GD = 128
CONV_K = 3
FFN_HIDDEN = 2816
GATE_W = 3 * D_MODEL
IN_COLS = 6664
MAIN_COLS = GATE_W + 7 * BRANCH_W
RMS_EPS = 1e-6
NEG_INF = -1e30

ADAM_LR = 0.001
ADAM_B1 = 0.9
ADAM_B2 = 0.999
ADAM_EPS = 1e-08
ADAM_WD = 0.01
ADAM_STEP = 10

LANES = 128
VMEM_LIMIT = 56 * 1024 * 1024
CUM_BLK = 256

TRIPLE = 3 * LANES
OFF_G, OFF_QKV, OFF_CONV, OFF_U = 0, 3072, 4608, 6144


def _cp(sem=None):
    return pltpu.CompilerParams(dimension_semantics=sem, vmem_limit_bytes=VMEM_LIMIT)


def _sigmoid(z):
    return 1.0 / (1.0 + jnp.exp(-z))


def _matmul(a, b, *, mode, out_dtype, name, tm=2048, tn=512, tk=None, residual=None, rms_g=None, side=None,
            extra=None):
    if mode == "nn":
        (M, K), N = a.shape, b.shape[1]
    elif mode == "nt":
        (M, K), N = a.shape, b.shape[0]
    else:
        (K, M), N = a.shape, b.shape[1]
    tm, tn, tk = min(tm, M), min(tn, N), K if tk is None else min(tk, K)
    assert M % tm == 0 and N % tn == 0 and K % tk == 0, (name, M, N, K, tm, tn, tk)
    nk = K // tk
    if mode == "nn":
        a_spec = pl.BlockSpec((tm, tk), lambda i, j, k: (i, k))
        b_spec = pl.BlockSpec((tk, tn), lambda i, j, k: (k, j))
        dims = (((1,), (0,)), ((), ()))
    elif mode == "nt":
        a_spec = pl.BlockSpec((tm, tk), lambda i, j, k: (i, k))
        b_spec = pl.BlockSpec((tn, tk), lambda i, j, k: (j, k))
        dims = (((1,), (1,)), ((), ()))
    else:
        a_spec = pl.BlockSpec((tk, tm), lambda i, j, k: (k, i))
        b_spec = pl.BlockSpec((tk, tn), lambda i, j, k: (k, j))
        dims = (((0,), (0,)), ((), ()))
    o_spec = pl.BlockSpec((tm, tn), lambda i, j, k: (i, j))
    has_res, has_norm, has_side, has_extra = (v is not None for v in (residual, rms_g, side, extra))
    assert not has_norm or tn == N, (name, tn, N)
    assert not has_side or (nk == 1 and mode != "nt"), name

    in_specs, args = [a_spec, b_spec], [a, b]
    out_specs, out_shape = [o_spec], [jax.ShapeDtypeStruct((M, N), out_dtype)]
    if has_res:
        in_specs.append(o_spec)
        args.append(residual)
    if has_norm:
        in_specs.append(pl.BlockSpec((1, N), lambda i, j, k: (0, 0)))
        args.append(rms_g.reshape(1, N))
        out_specs.append(o_spec)
        out_shape.append(jax.ShapeDtypeStruct((M, N), BF16))
    if has_side:
        b_side, side_dtype = side
        ns = b_side.shape[1]
        in_specs.append(pl.BlockSpec((K, ns), lambda i, j, k: (0, 0)))
        args.append(b_side)
        out_specs.append(pl.BlockSpec((tm, ns), lambda i, j, k: (i, 0)))
        out_shape.append(jax.ShapeDtypeStruct((M, ns), side_dtype))
    if has_extra:
        a2, b2 = extra
        in_specs += [pl.BlockSpec((tm, a2.shape[1]), lambda i, j, k: (i, 0)),
                     pl.BlockSpec((tn, b2.shape[1]), lambda i, j, k: (j, 0))]
        args += [a2, b2]
    n_in = len(args)

    def body(*refs):
        ins, outs = list(refs[2:n_in]), list(refs[n_in:n_in + len(out_shape)])
        a_ref, b_ref = refs[:2]
        r_ref = ins.pop(0) if has_res else None
        g_ref = ins.pop(0) if has_norm else None
        bs_ref = ins.pop(0) if has_side else None
        a2_ref, b2_ref = (ins.pop(0), ins.pop(0)) if has_extra else (None, None)
        o_ref = outs.pop(0)
        h_ref = outs.pop(0) if has_norm else None
        so_ref = outs.pop(0) if has_side else None

        def finish(acc):
            if has_res:
                acc = acc + r_ref[...].astype(F32)
            if has_extra:
                acc = acc + lax.dot_general(a2_ref[...], b2_ref[...], (((1,), (1,)), ((), ())),
                                            preferred_element_type=F32)
            o_ref[...] = acc.astype(out_dtype)
            if has_norm:
                r = lax.rsqrt(jnp.mean(acc * acc, axis=-1, keepdims=True) + RMS_EPS)
                h_ref[...] = ((acc * r) * g_ref[...]).astype(BF16)

        if has_side:
            @pl.when(pl.program_id(1) == 0)
            def _():
                side_dims = (((1,), (0,)), ((), ())) if mode == "nn" else dims
                so_ref[...] = lax.dot_general(a_ref[...], bs_ref[...], side_dims,
                                              preferred_element_type=F32).astype(so_ref.dtype)

        prod = lax.dot_general(a_ref[...], b_ref[...], dims, preferred_element_type=F32)
        if nk == 1:
            finish(prod)
            return
        acc_ref = refs[-1]
        k = pl.program_id(2)

        @pl.when(k == 0)
        def _():
            acc_ref[...] = prod

        @pl.when(jnp.logical_and(k > 0, k < nk - 1))
        def _():
            acc_ref[...] += prod

        @pl.when(k == nk - 1)
        def _():
            finish(acc_ref[...] + prod)

    single = len(out_shape) == 1
    return pl.pallas_call(
        body, name=name, grid=(M // tm, N // tn, nk), in_specs=in_specs,
        out_specs=out_specs[0] if single else out_specs, out_shape=out_shape[0] if single else out_shape,
        scratch_shapes=[pltpu.VMEM((tm, tn), F32)] if nk > 1 else [],
        compiler_params=_cp(("parallel", "arbitrary" if has_side else "parallel", "arbitrary")),
    )(*args)


def _rms_fwd(x, g, name):
    T, Dm = x.shape
    tm = min(512, T)

    def body(x_ref, g_ref, h_ref):
        xf = x_ref[...]
        r = lax.rsqrt(jnp.mean(xf * xf, axis=-1, keepdims=True) + RMS_EPS)
        h_ref[...] = ((xf * r) * g_ref[...]).astype(BF16)

    return pl.pallas_call(
        body, name=name, grid=(T // tm,),
        in_specs=[pl.BlockSpec((tm, Dm), lambda i: (i, 0)), pl.BlockSpec((1, Dm), lambda i: (0, 0))],
        out_specs=pl.BlockSpec((tm, Dm), lambda i: (i, 0)),
        out_shape=jax.ShapeDtypeStruct((T, Dm), BF16),
        compiler_params=_cp(("parallel",)),
    )(x, g.reshape(1, Dm))


def _rms_bwd(x, g, dh, dres, name):
    T, Dm = x.shape
    tm = min(512, T)

    def body(x_ref, g_ref, dh_ref, dres_ref, dx_ref, dxb_ref, dg_ref):
        i = pl.program_id(0)
        xf = x_ref[...]
        r = lax.rsqrt(jnp.mean(xf * xf, axis=-1, keepdims=True) + RMS_EPS)
        xn = xf * r
        dhf = dh_ref[...].astype(F32)
        dxn = dhf * g_ref[...]
        c = jnp.mean(dxn * xn, axis=-1, keepdims=True)
        dx = dres_ref[...] + r * (dxn - xn * c)
        dx_ref[...] = dx
        dxb_ref[...] = dx.astype(BF16)
        part = jnp.sum(dhf * xn, axis=0, keepdims=True)

        @pl.when(i == 0)
        def _():
            dg_ref[...] = part

        @pl.when(i > 0)
        def _():
            dg_ref[...] += part

    row = pl.BlockSpec((tm, Dm), lambda i: (i, 0))
    vec = pl.BlockSpec((1, Dm), lambda i: (0, 0))
    return pl.pallas_call(
        body, name=name, grid=(T // tm,), in_specs=[row, vec, row, row], out_specs=[row, row, vec],
        out_shape=[jax.ShapeDtypeStruct((T, Dm), F32), jax.ShapeDtypeStruct((T, Dm), BF16),
                   jax.ShapeDtypeStruct((1, Dm), F32)],
        compiler_params=_cp(("arbitrary",)),
    )(x, g.reshape(1, Dm), dh, dres)


def _loss_head(x, g, target, name):
    T, Dm = x.shape
    tm = min(512, T)

    def body(x_ref, g_ref, t_ref, loss_ref, dx_ref, dxb_ref, dg_ref):
        i = pl.program_id(0)
        xf = x_ref[...]
        gv = g_ref[...]
        r = lax.rsqrt(jnp.mean(xf * xf, axis=-1, keepdims=True) + RMS_EPS)
        xn = xf * r
        diff = xn * gv - t_ref[...]
        per_tok = jnp.mean(diff * diff, axis=-1, keepdims=True)
        lpart = 0.5 * jnp.sum(per_tok, axis=0, keepdims=True) + jnp.zeros((1, LANES), F32)
        dy = diff * (1.0 / Dm)
        dxn = dy * gv
        c = jnp.mean(dxn * xn, axis=-1, keepdims=True)
        dx = r * (dxn - xn * c)
        dx_ref[...] = dx
        dxb_ref[...] = dx.astype(BF16)
        part = jnp.sum(dy * xn, axis=0, keepdims=True)

        @pl.when(i == 0)
        def _():
            dg_ref[...] = part
            loss_ref[...] = lpart

        @pl.when(i > 0)
        def _():
            dg_ref[...] += part
            loss_ref[...] += lpart

    row = pl.BlockSpec((tm, Dm), lambda i: (i, 0))
    vec = pl.BlockSpec((1, Dm), lambda i: (0, 0))
    lsp = pl.BlockSpec((1, LANES), lambda i: (0, 0))
    return pl.pallas_call(
        body, name=name, grid=(T // tm,), in_specs=[row, vec, row], out_specs=[lsp, row, row, vec],
        out_shape=[jax.ShapeDtypeStruct((1, LANES), F32), jax.ShapeDtypeStruct((T, Dm), F32),
                   jax.ShapeDtypeStruct((T, Dm), BF16), jax.ShapeDtypeStruct((1, Dm), F32)],
        compiler_params=_cp(("arbitrary",)),
    )(x, g.reshape(1, Dm), target)


def _split_bf16(v):
    hi = v.astype(BF16)
    r1 = v - hi.astype(F32)
    mid = r1.astype(BF16)
    lo = (r1 - mid.astype(F32)).astype(BF16)
    return hi, mid, lo


def _tri_dot(tri, v):
    hi, mid, lo = _split_bf16(v)
    dot = functools.partial(jnp.dot, preferred_element_type=F32)
    return dot(tri, hi) + dot(tri, mid) + dot(tri, lo)


def _log_sigmoid(z):
    return jnp.minimum(z, 0.0) - jnp.log(1.0 + jnp.exp(-jnp.abs(z)))


def _fox_cumsum_bwd(f, bf, dF, n_seq, name):
    T = f.shape[0]
    S = T // n_seq
    c = min(CUM_BLK, S)

    def body(f_ref, b_ref, dF_ref, df_ref, db_ref):
        b = pl.program_id(0)
        ri = lax.broadcasted_iota(jnp.int32, (c, c), 0)
        ci = lax.broadcasted_iota(jnp.int32, (c, c), 1)
        tri = (ri <= ci).astype(BF16)
        carry = jnp.zeros((1, LANES), F32)
        dbp = jnp.zeros((1, LANES), F32)
        for j in reversed(range(S // c)):
            dFc = dF_ref[j * c:(j + 1) * c, :]
            dlf = _tri_dot(tri, dFc) + carry
            carry = carry + jnp.sum(dFc, axis=0, keepdims=True)
            z = f_ref[j * c:(j + 1) * c, :] + b_ref[...]
            dz = dlf * _sigmoid(-z)
            df_ref[j * c:(j + 1) * c, :] = dz.astype(BF16)
            dbp = dbp + jnp.sum(dz, axis=0, keepdims=True)

        @pl.when(b == 0)
        def _():
            db_ref[...] = dbp

        @pl.when(b > 0)
        def _():
            db_ref[...] += dbp

    blk = pl.BlockSpec((S, LANES), lambda b: (b, 0))
    vec = pl.BlockSpec((1, LANES), lambda b: (0, 0))
    return pl.pallas_call(
        body, name=name, grid=(n_seq,), in_specs=[blk, vec, blk], out_specs=[blk, vec],
        out_shape=[jax.ShapeDtypeStruct((T, LANES), BF16), jax.ShapeDtypeStruct((1, LANES), F32)],
        compiler_params=_cp(("arbitrary",)),
    )(f, bf, dF)


def _pair_masks():
    lane = lax.broadcasted_iota(jnp.int32, (1, LANES), 1)
    lo = lane < HEAD_DIM
    return lo, jnp.logical_not(lo)


AUG0 = HEAD_DIM
Q_TILE, K_CHUNK, ROW_GROUP = 512, 256, 64


def _fox_prep(f, bf, proj, n_seq, name):
    T = f.shape[0]
    S = T // n_seq
    c = min(CUM_BLK, S)

    def body(f_ref, b_ref, qkv_ref, qa_ref, ka_ref, va_ref):
        ri = lax.broadcasted_iota(jnp.int32, (c, c), 0)
        ci = lax.broadcasted_iota(jnp.int32, (c, c), 1)
        tri = (ri >= ci).astype(BF16)
        lane = lax.broadcasted_iota(jnp.int32, (c, LANES), 1)
        carry = jnp.zeros((1, LANES), F32)
        for j in range(S // c):
            rows = slice(j * c, (j + 1) * c)
            lf = _log_sigmoid(f_ref[rows, :] + b_ref[...])
            Fc = _tri_dot(tri, lf) + carry
            carry = carry + jnp.sum(lf, axis=0, keepdims=True)
            for h in range(N_HEADS):
                col = jnp.sum(jnp.where(lane == h, Fc, 0.0), axis=-1, keepdims=True)
                hi = col.astype(BF16).astype(F32)
                r1 = col - hi
                mid = r1.astype(BF16).astype(F32)
                lo = r1 - mid
                ones_q = jnp.logical_and(lane >= AUG0 + 3, lane < AUG0 + 6)
                ones_k = jnp.logical_and(lane >= AUG0, lane < AUG0 + 3)
                aug_q = jnp.where(lane == AUG0, hi, jnp.where(lane == AUG0 + 1, mid, jnp.where(
                    lane == AUG0 + 2, lo, jnp.where(ones_q, 1.0, 0.0))))
                aug_k = jnp.where(lane == AUG0 + 3, -hi, jnp.where(lane == AUG0 + 4, -mid, jnp.where(
                    lane == AUG0 + 5, -lo, jnp.where(ones_k, 1.0, 0.0))))
                base = (h // 2) * TRIPLE
                qp, kp, vp = (qkv_ref[rows, base + t * LANES:base + (t + 1) * LANES].astype(F32) for t in range(3))
                if h % 2:
                    qp, kp, vp = (pltpu.roll(a, HEAD_DIM, 1) for a in (qp, kp, vp))
                out = slice(h * LANES, (h + 1) * LANES)
                qa_ref[rows, out] = jnp.where(lane < HEAD_DIM, qp * (HEAD_DIM ** -0.5), aug_q).astype(BF16)
                ka_ref[rows, out] = jnp.where(lane < HEAD_DIM, kp, aug_k).astype(BF16)
                va_ref[rows, out] = jnp.where(lane < HEAD_DIM, vp, jnp.where(lane == AUG0, 1.0, 0.0)).astype(BF16)

    fblk = pl.BlockSpec((S, LANES), lambda b: (b, 0))
    out = pl.BlockSpec((S, N_HEADS * LANES), lambda b: (b, 0))
    sh = jax.ShapeDtypeStruct((T, N_HEADS * LANES), BF16)
    return pl.pallas_call(
        body, name=name, grid=(n_seq,),
        in_specs=[fblk, pl.BlockSpec((1, LANES), lambda b: (0, 0)),
                  pl.BlockSpec((S, 4 * TRIPLE), lambda b: (b, OFF_QKV // (4 * TRIPLE)))],
        out_specs=[out, out, out], out_shape=[sh, sh, sh],
        compiler_params=_cp(("parallel",)),
    )(f, bf, proj)


def _band_mask(q0, k0, nq, nk):
    row = q0 + lax.broadcasted_iota(jnp.int32, (nq, nk), 0)
    col = k0 + lax.broadcasted_iota(jnp.int32, (nq, nk), 1)
    return col <= row


_NT = (((1,), (1,)), ((), ()))
_TN = (((0,), (0,)), ((), ()))


def _attn_fwd2(qa, ka, va, n_seq, name):
    T = qa.shape[0]
    S = T // n_seq
    tq, tk, rg = min(Q_TILE, S), min(K_CHUNK, S), ROW_GROUP
    nq, per = S // tq, tq // tk

    def body(q_ref, k_ref, v_ref, o_ref, o32_ref, lse_ref, phi_s, plo_s, mp_s, m_s, acc_s):
        qi = pl.program_id(2)
        mp_s[...] = jnp.full_like(mp_s, NEG_INF)
        acc_s[...] = jnp.zeros_like(acc_s)

        def scores(kc, hh, r0):
            k0 = pl.multiple_of(kc * tk, tk)
            hl = slice(hh * LANES, (hh + 1) * LANES)
            return k0, lax.dot_general(q_ref[r0:, hl], k_ref[pl.ds(k0, tk), hl], _NT, preferred_element_type=F32)

        def max_chunk(kc, masked, r0):
            for hh in range(2):
                k0, s_all = scores(kc, hh, r0)
                for r in range(r0 // rg, tq // rg):
                    rows = slice(r * rg, (r + 1) * rg)
                    s = s_all[r * rg - r0:(r + 1) * rg - r0, :]
                    if masked:
                        s = jnp.where(_band_mask(qi * tq + r * rg, k0, rg, tk), s, NEG_INF)
                    part = s[:, :LANES]
                    for c in range(1, tk // LANES):
                        part = jnp.maximum(part, s[:, c * LANES:(c + 1) * LANES])
                    mp_s[hh, rows, :] = jnp.maximum(mp_s[hh, rows, :], part)

        def sum_chunk(kc, masked, r0):
            for hh in range(2):
                k0, s_all = scores(kc, hh, r0)
                hl = slice(hh * LANES, (hh + 1) * LANES)
                v = v_ref[pl.ds(k0, tk), hl]
                for r in range(r0 // rg, tq // rg):
                    rows = slice(r * rg, (r + 1) * rg)
                    p = jnp.exp(s_all[r * rg - r0:(r + 1) * rg - r0, :] - m_s[hh, rows])
                    if masked:
                        p = jnp.where(_band_mask(qi * tq + r * rg, k0, rg, tk), p, 0.0)
                    p_hi = p.astype(BF16)
                    phi_s[hh, rows, :] = p_hi
                    plo_s[hh, rows, :] = (p - p_hi.astype(F32)).astype(BF16)
                acc_s[hh, r0:, :] += (jnp.dot(phi_s[hh, r0:, :], v, preferred_element_type=F32)
                                      + jnp.dot(plo_s[hh, r0:, :], v, preferred_element_type=F32))

        def sweep(chunk):
            def unmasked(kc, carry):
                chunk(kc, False, 0)
                return carry

            lax.fori_loop(0, qi * per, unmasked, 0)
            for d in range(per):
                chunk(qi * per + d, True, d * tk)

        sweep(max_chunk)
        m_s[...] = jnp.max(mp_s[...], axis=-1, keepdims=True)
        sweep(sum_chunk)

        lane = lax.broadcasted_iota(jnp.int32, (1, LANES), 1)
        outs = []
        for hh in range(2):
            acc = acc_s[hh]
            l = jnp.sum(jnp.where(lane == AUG0, acc, 0.0), axis=-1, keepdims=True)
            lse_ref[hh] = m_s[hh] + jnp.log(l)
            outs.append(acc / l)
        o = jnp.where(lane < HEAD_DIM, outs[0], pltpu.roll(outs[1], HEAD_DIM, 1))
        o_ref[...] = o.astype(BF16)
        o32_ref[...] = o

    qmap = lambda b, j, qi: (b * nq + qi, j)
    omap = lambda b, j, qi: (b * nq + qi, j)
    kv = pl.BlockSpec((S, 2 * LANES), lambda b, j, qi: (b, j))
    return pl.pallas_call(
        body, name=name, grid=(n_seq, N_HEADS // 2, nq),
        in_specs=[pl.BlockSpec((tq, 2 * LANES), qmap), kv, kv],
        out_specs=[pl.BlockSpec((tq, LANES), omap), pl.BlockSpec((tq, LANES), omap),
                   pl.BlockSpec((2, tq, 1), lambda b, j, qi: (j, b * nq + qi, 0))],
        out_shape=[jax.ShapeDtypeStruct((T, BRANCH_W), BF16), jax.ShapeDtypeStruct((T, BRANCH_W), F32),
                   jax.ShapeDtypeStruct((N_HEADS, T, 1), F32)],
        scratch_shapes=[pltpu.VMEM((2, tq, tk), BF16), pltpu.VMEM((2, tq, tk), BF16),
                        pltpu.VMEM((2, tq, LANES), F32), pltpu.VMEM((2, tq, 1), F32),
                        pltpu.VMEM((2, tq, LANES), F32)],
        compiler_params=_cp(("parallel", "parallel", "parallel")),
    )(qa, ka, va)


def _attn_bwd(qa, ka, proj, do, o32, lse, dproj, n_seq, name):
    T = qa.shape[0]
    S = T // n_seq
    tq, tk, rg = min(Q_TILE, S), min(K_CHUNK, S), ROW_GROUP
    nq, per, nkc = S // tq, tq // tk, S // tk

    def body(q_ref, k_ref, v_ref, do_ref, o_ref, lse_ref, _, dqkv_ref, dfk_ref,
             p_s, ds_s, dq_s, dk_s, dv_s, df_s):
        dk_s[...] = jnp.zeros_like(dk_s)
        dv_s[...] = jnp.zeros_like(dv_s)
        df_s[...] = jnp.zeros_like(df_s)
        sels = _pair_masks()

        for qi in range(nq):
            q0 = qi * tq
            do_t = do_ref[q0:q0 + tq, :]
            dq_s[...] = jnp.zeros_like(dq_s)
            prod = do_t.astype(F32) * o_ref[q0:q0 + tq, :]
            dls = [jnp.sum(jnp.where(sel, prod, 0.0), axis=-1, keepdims=True) for sel in sels]

            def chunk(kc, masked, r0, q0=q0, do_t=do_t, dls=dls):
                k0 = pl.multiple_of(kc * tk, tk)
                v = v_ref[pl.ds(k0, tk), :]
                do_a = do_t[r0:, :]
                for hh in range(2):
                    hl = slice(hh * LANES, (hh + 1) * LANES)
                    qh, kh = q_ref[q0 + r0:q0 + tq, hl], k_ref[pl.ds(k0, tk), hl]
                    s_all = lax.dot_general(qh, kh, _NT, preferred_element_type=F32)
                    dom = jnp.where(sels[hh], do_a, jnp.zeros_like(do_a))
                    dp_all = lax.dot_general(dom, v, _NT, preferred_element_type=F32)
                    dfp = jnp.zeros((1, tk), F32)
                    for r in range(r0 // rg, tq // rg):
                        rows = slice(r * rg, (r + 1) * rg)
                        arows = slice(r * rg - r0, (r + 1) * rg - r0)
                        qrows = slice(q0 + r * rg, q0 + (r + 1) * rg)
                        p = jnp.exp(s_all[arows, :] - lse_ref[hh, qrows])
                        if masked:
                            p = jnp.where(_band_mask(q0 + r * rg, k0, rg, tk), p, 0.0)
                        ds = p * (dp_all[arows, :] - dls[hh][rows])
                        p_s[hh, rows, :] = p.astype(BF16)
                        ds_s[hh, rows, :] = ds.astype(BF16)
                        dfp = dfp + jnp.sum(ds, axis=0, keepdims=True)
                    df_s[hh, kc] -= dfp
                    dq_s[hh, r0:, :] += jnp.dot(ds_s[hh, r0:, :], kh, preferred_element_type=F32)
                    dv_s[hh, pl.ds(k0, tk), :] += lax.dot_general(p_s[hh, r0:, :], do_a, _TN,
                                                                  preferred_element_type=F32)
                    dk_s[hh, pl.ds(k0, tk), :] += lax.dot_general(ds_s[hh, r0:, :], qh, _TN,
                                                                  preferred_element_type=F32)

            def unmasked(kc, carry, chunk=chunk):
                chunk(kc, False, 0)
                return carry

            lax.fori_loop(0, qi * per, unmasked, 0)
            for d in range(per):
                chunk(qi * per + d, True, d * tk)
            dq = jnp.where(sels[0], dq_s[0], pltpu.roll(dq_s[1], HEAD_DIM, 1))
            dqkv_ref[q0:q0 + tq, :LANES] = (dq * (HEAD_DIM ** -0.5)).astype(BF16)

        dqkv_ref[:, LANES:2 * LANES] = jnp.where(sels[0], dk_s[0], pltpu.roll(dk_s[1], HEAD_DIM, 1)).astype(BF16)
        dqkv_ref[:, 2 * LANES:] = jnp.where(sels[0], dv_s[0], dv_s[1]).astype(BF16)
        for c in range(nkc):
            dfk_ref[:, :, c * tk:(c + 1) * tk] = df_s[:, c]

    seq = lambda w: pl.BlockSpec((S, w), lambda b, j: (b, j))
    col1 = pl.BlockSpec((2, S, 1), lambda b, j: (j, b, 0))
    vblk = pl.BlockSpec((S, LANES), lambda b, j: (b, OFF_QKV // LANES + 3 * j + 2))
    return pl.pallas_call(
        body, name=name, grid=(n_seq, N_HEADS // 2),
        in_specs=[seq(2 * LANES), seq(2 * LANES), vblk, seq(LANES), seq(LANES), col1,
                  pl.BlockSpec(memory_space=pl.ANY)],
        out_specs=[pl.BlockSpec((S, TRIPLE), lambda b, j: (b, OFF_QKV // TRIPLE + j)),
                   pl.BlockSpec((2, 1, S), lambda b, j: (j, 0, b))],
        out_shape=[jax.ShapeDtypeStruct(dproj.shape, BF16), jax.ShapeDtypeStruct((N_HEADS, 1, T), F32)],
        input_output_aliases={6: 0},
        scratch_shapes=[pltpu.VMEM((2, tq, tk), BF16), pltpu.VMEM((2, tq, tk), BF16),
                        pltpu.VMEM((2, tq, LANES), F32), pltpu.VMEM((2, S, LANES), F32),
                        pltpu.VMEM((2, S, LANES), F32), pltpu.VMEM((2, nkc, 1, tk), F32)],
        compiler_params=_cp(("parallel", "parallel")),
    )(qa, ka, proj, do, o32, lse, dproj)


def _shift_down(v, k, row):
    return jnp.where(row >= k, pltpu.roll(v, k, 0), 0.0)


def _shift_up(v, k, row, S):
    return jnp.where(row < S - k, pltpu.roll(v, S - k, 0), 0.0)


def _pool_diff(uf, w, row):
    acc, k = uf, 1
    while k < w:
        acc = acc + _shift_down(acc, k, row)
        k *= 2
    n = jnp.minimum(row + 1, w).astype(F32)
    return acc / n - uf


def _pool_fwd(proj, pool_w, pool_scale, n_seq, name):
    T = proj.shape[0]
    S = T // n_seq

    def body(u_ref, w_ref, sc_ref, o_ref, d_s):
        g = pl.program_id(1)
        row = lax.broadcasted_iota(jnp.int32, (S, POOL_GD), 0)
        uf = u_ref[...].astype(F32)
        for gi, wlen in enumerate(POOL_WINDOWS):
            @pl.when(g == gi)
            def _(wlen=wlen):
                d_s[...] = _pool_diff(uf, wlen, row).astype(BF16)
        e = jnp.dot(d_s[...], w_ref[0], preferred_element_type=F32)
        o_ref[...] = (e * sc_ref[...]).astype(BF16)

    uc = OFF_U // POOL_GD
    return pl.pallas_call(
        body, name=name, grid=(n_seq, len(POOL_WINDOWS)),
        in_specs=[pl.BlockSpec((S, POOL_GD), lambda b, g: (b, uc + g)),
                  pl.BlockSpec((1, POOL_GD, POOL_GD), lambda b, g: (g, 0, 0)),
                  pl.BlockSpec((1, POOL_GD), lambda b, g: (0, g))],
        out_specs=pl.BlockSpec((S, POOL_GD), lambda b, g: (b, g)),
        out_shape=jax.ShapeDtypeStruct((T, BRANCH_W), BF16),
        scratch_shapes=[pltpu.VMEM((S, POOL_GD), BF16)],
        compiler_params=_cp(("parallel", "parallel")),
    )(proj, pool_w, pool_scale)


def _pool_bwd(proj, dout, pool_w, pool_scale, dproj, n_seq, name):
    T = proj.shape[0]
    S = T // n_seq

    def body(u_ref, do_ref, w_ref, sc_ref, _, du_ref, dw_ref, dsc_ref, d_s):
        g, b = pl.program_id(0), pl.program_id(1)
        row = lax.broadcasted_iota(jnp.int32, (S, POOL_GD), 0)
        uf = u_ref[...].astype(F32)
        for gi, wlen in enumerate(POOL_WINDOWS):
            @pl.when(g == gi)
            def _(wlen=wlen):
                d_s[...] = _pool_diff(uf, wlen, row).astype(BF16)
        db16 = d_s[...]
        w = w_ref[0]
        e = jnp.dot(db16, w, preferred_element_type=F32)
        dof = do_ref[...].astype(F32)
        dsc = jnp.sum(dof * e, axis=0, keepdims=True)
        de = (dof * sc_ref[...]).astype(BF16)
        dd = lax.dot_general(de, w, (((1,), (1,)), ((), ())), preferred_element_type=F32)
        dw = lax.dot_general(db16, de, (((0,), (0,)), ((), ())), preferred_element_type=F32)
        for gi, wlen in enumerate(POOL_WINDOWS):
            @pl.when(g == gi)
            def _(wlen=wlen):
                n = jnp.minimum(row + 1, wlen).astype(F32)
                acc, k = dd / n, 1
                while k < wlen:
                    acc = acc + _shift_up(acc, k, row, S)
                    k *= 2
                du_ref[...] = (acc - dd).astype(BF16)

        @pl.when(b == 0)
        def _():
            dw_ref[0] = dw
            dsc_ref[...] = dsc

        @pl.when(b > 0)
        def _():
            dw_ref[0] += dw
            dsc_ref[...] += dsc

    uc = OFF_U // POOL_GD
    return pl.pallas_call(
        body, name=name, grid=(len(POOL_WINDOWS), n_seq),
        in_specs=[pl.BlockSpec((S, POOL_GD), lambda g, b: (b, uc + g)),
                  pl.BlockSpec((S, POOL_GD), lambda g, b: (b, g)),
                  pl.BlockSpec((1, POOL_GD, POOL_GD), lambda g, b: (g, 0, 0)),
                  pl.BlockSpec((1, POOL_GD), lambda g, b: (0, g)),
                  pl.BlockSpec(memory_space=pl.ANY)],
        out_specs=[pl.BlockSpec((S, POOL_GD), lambda g, b: (b, uc + g)),
                   pl.BlockSpec((1, POOL_GD, POOL_GD), lambda g, b: (g, 0, 0)),
                   pl.BlockSpec((1, POOL_GD), lambda g, b: (0, g))],
        out_shape=[jax.ShapeDtypeStruct(dproj.shape, BF16),
                   jax.ShapeDtypeStruct((len(POOL_WINDOWS), POOL_GD, POOL_GD), F32),
                   jax.ShapeDtypeStruct((1, BRANCH_W), F32)],
        input_output_aliases={4: 0},
        scratch_shapes=[pltpu.VMEM((S, POOL_GD), BF16)],
        compiler_params=_cp(("parallel", "arbitrary")),
    )(proj, dout, pool_w, pool_scale, dproj)


def _conv_fwd(proj, conv_w, n_seq, name):
    T = proj.shape[0]
    S = T // n_seq
    nc = BRANCH_W // LANES

    def body(c_ref, w_ref, o_ref):
        row = lax.broadcasted_iota(jnp.int32, (S, LANES), 0)
        cv, cb, cc = (c_ref[:, t * LANES:(t + 1) * LANES].astype(F32) for t in range(3))
        z = cc * cv
        w = w_ref[...]
        y = w[0:1] * _shift_down(z, 2, row) + w[1:2] * _shift_down(z, 1, row) + w[2:3] * z
        o_ref[...] = (cb * y).astype(BF16)

    return pl.pallas_call(
        body, name=name, grid=(n_seq, nc),
        in_specs=[pl.BlockSpec((S, TRIPLE), lambda b, j: (b, OFF_CONV // TRIPLE + j)),
                  pl.BlockSpec((CONV_K, LANES), lambda b, j: (0, j))],
        out_specs=pl.BlockSpec((S, LANES), lambda b, j: (b, j)),
        out_shape=jax.ShapeDtypeStruct((T, BRANCH_W), BF16),
        compiler_params=_cp(("parallel", "parallel")),
    )(proj, conv_w)


def _conv_bwd(proj, dout, conv_w, dproj, n_seq, name):
    T = proj.shape[0]
    S = T // n_seq
    nc = BRANCH_W // LANES

    def body(c_ref, do_ref, w_ref, _, dc_ref, dw_ref):
        b = pl.program_id(1)
        row = lax.broadcasted_iota(jnp.int32, (S, LANES), 0)
        cv, cb, cc = (c_ref[:, t * LANES:(t + 1) * LANES].astype(F32) for t in range(3))
        dof = do_ref[...].astype(F32)
        w = w_ref[...]
        z = cc * cv
        z1, z2 = _shift_down(z, 1, row), _shift_down(z, 2, row)
        y = w[0:1] * z2 + w[1:2] * z1 + w[2:3] * z
        dy = dof * cb
        dz = w[2:3] * dy + w[1:2] * _shift_up(dy, 1, row, S) + w[0:1] * _shift_up(dy, 2, row, S)
        dc_ref[:, :LANES] = (dz * cc).astype(BF16)
        dc_ref[:, LANES:2 * LANES] = (dof * y).astype(BF16)
        dc_ref[:, 2 * LANES:] = (dz * cv).astype(BF16)
        dws = [jnp.sum(dy * zk, axis=0, keepdims=True) for zk in (z2, z1, z)]

        @pl.when(b == 0)
        def _():
            for kk in range(CONV_K):
                dw_ref[kk:kk + 1, :] = dws[kk]

        @pl.when(b > 0)
        def _():
            for kk in range(CONV_K):
                dw_ref[kk:kk + 1, :] += dws[kk]

    triple = pl.BlockSpec((S, TRIPLE), lambda j, b: (b, OFF_CONV // TRIPLE + j))
    wsp = pl.BlockSpec((CONV_K, LANES), lambda j, b: (0, j))
    return pl.pallas_call(
        body, name=name, grid=(nc, n_seq),
        in_specs=[triple, pl.BlockSpec((S, LANES), lambda j, b: (b, j)), wsp, pl.BlockSpec(memory_space=pl.ANY)],
        out_specs=[triple, wsp],
        out_shape=[jax.ShapeDtypeStruct(dproj.shape, BF16), jax.ShapeDtypeStruct((CONV_K, BRANCH_W), F32)],
        input_output_aliases={3: 0},
        compiler_params=_cp(("parallel", "arbitrary")),
    )(proj, dout, conv_w, dproj)


def _mix_fwd(oa, ob, oc, wpa, wpp, wpc, proj, b_gate, name):
    T = oa.shape[0]
    tm = min(512, T)

    def body(oa_ref, ob_ref, oc_ref, wa_ref, wp_ref, wc_ref, g_ref, bg_ref, o_ref):
        acc = jnp.zeros((tm, D_MODEL), F32)
        for i, (x_ref, w_ref) in enumerate(((oa_ref, wa_ref), (ob_ref, wp_ref), (oc_ref, wc_ref))):
            y = jnp.dot(x_ref[...], w_ref[...], preferred_element_type=F32)
            sl = slice(i * D_MODEL, (i + 1) * D_MODEL)
            acc = acc + _sigmoid(g_ref[:, sl].astype(F32) + bg_ref[:, sl]) * y
        o_ref[...] = acc.astype(BF16)

    br = pl.BlockSpec((tm, BRANCH_W), lambda i: (i, 0))
    wsp = pl.BlockSpec((BRANCH_W, D_MODEL), lambda i: (0, 0))
    return pl.pallas_call(
        body, name=name, grid=(T // tm,),
        in_specs=[br, br, br, wsp, wsp, wsp, pl.BlockSpec((tm, GATE_W), lambda i: (i, 0)),
                  pl.BlockSpec((1, GATE_W), lambda i: (0, 0))],
        out_specs=pl.BlockSpec((tm, D_MODEL), lambda i: (i, 0)),
        out_shape=jax.ShapeDtypeStruct((T, D_MODEL), BF16),
        compiler_params=_cp(("parallel",)),
    )(oa, ob, oc, wpa, wpp, wpc, proj, b_gate)


def _mix_bwd(oa, ob, oc, wpa, wpp, wpc, proj, b_gate, dmixed, name):
    T = oa.shape[0]
    tm = min(256, T)

    def body(oa_ref, ob_ref, oc_ref, wa_ref, wp_ref, wc_ref, g_ref, bg_ref, dm_ref,
             dya_ref, dyb_ref, dyc_ref, dg_ref, dbg_ref):
        i0 = pl.program_id(0)
        dm = dm_ref[...].astype(F32)
        parts = []
        for i, (x_ref, w_ref, dy_ref) in enumerate(((oa_ref, wa_ref, dya_ref), (ob_ref, wp_ref, dyb_ref),
                                                    (oc_ref, wc_ref, dyc_ref))):
            y = jnp.dot(x_ref[...], w_ref[...], preferred_element_type=F32)
            sl = slice(i * D_MODEL, (i + 1) * D_MODEL)
            gate = _sigmoid(g_ref[:, sl].astype(F32) + bg_ref[:, sl])
            dy_ref[...] = (dm * gate).astype(BF16)
            dgl = dm * y * gate * (1.0 - gate)
            dg_ref[:, sl] = dgl.astype(BF16)
            parts.append(jnp.sum(dgl, axis=0, keepdims=True))

        @pl.when(i0 == 0)
        def _():
            for i in range(3):
                dbg_ref[:, i * D_MODEL:(i + 1) * D_MODEL] = parts[i]

        @pl.when(i0 > 0)
        def _():
            for i in range(3):
                dbg_ref[:, i * D_MODEL:(i + 1) * D_MODEL] += parts[i]

    br = pl.BlockSpec((tm, BRANCH_W), lambda i: (i, 0))
    wsp = pl.BlockSpec((BRANCH_W, D_MODEL), lambda i: (0, 0))
    row = pl.BlockSpec((tm, D_MODEL), lambda i: (i, 0))
    gsp = pl.BlockSpec((tm, GATE_W), lambda i: (i, 0))
    bsp = pl.BlockSpec((1, GATE_W), lambda i: (0, 0))
    act = jax.ShapeDtypeStruct((T, D_MODEL), BF16)
    return pl.pallas_call(
        body, name=name, grid=(T // tm,),
        in_specs=[br, br, br, wsp, wsp, wsp, gsp, bsp, row],
        out_specs=[row, row, row, gsp, bsp],
        out_shape=[act, act, act, jax.ShapeDtypeStruct((T, MAIN_COLS), BF16),
                   jax.ShapeDtypeStruct((1, GATE_W), F32)],
        compiler_params=_cp(("arbitrary",)),
    )(oa, ob, oc, wpa, wpp, wpc, proj, b_gate, dmixed)


GU_TILE = 256


def _gu_col(c):
    t, r = divmod(c, GU_TILE)
    return (t // 2) * GU_TILE + r + (FFN_HIDDEN if t % 2 else 0)


def _gate_up_swiglu(h, w, name):
    T, K = h.shape
    tm = min(2048, T)

    def body(h_ref, w_ref, ab_ref, s_ref):
        prod = jnp.dot(h_ref[...], w_ref[...], preferred_element_type=F32)
        ab_ref[...] = prod.astype(BF16)
        a = prod[:, :GU_TILE]
        s_ref[...] = (a * _sigmoid(a) * prod[:, GU_TILE:]).astype(BF16)

    return pl.pallas_call(
        body, name=name, grid=(T // tm, FFN_HIDDEN // GU_TILE),
        in_specs=[pl.BlockSpec((tm, K), lambda i, j: (i, 0)), pl.BlockSpec((K, 2 * GU_TILE), lambda i, j: (0, j))],
        out_specs=[pl.BlockSpec((tm, 2 * GU_TILE), lambda i, j: (i, j)), pl.BlockSpec((tm, GU_TILE), lambda i, j: (i, j))],
        out_shape=[jax.ShapeDtypeStruct((T, 2 * FFN_HIDDEN), BF16), jax.ShapeDtypeStruct((T, FFN_HIDDEN), BF16)],
        compiler_params=_cp(("parallel", "parallel")),
    )(h, w)


def _swiglu_bwd_fused(dx, w_down, ab, name):
    T, K = dx.shape
    tm = min(2048, T)

    def body(dx_ref, w_ref, ab_ref, o_ref):
        ds = lax.dot_general(dx_ref[...], w_ref[...], _NT, preferred_element_type=F32)
        a = ab_ref[:, :GU_TILE].astype(F32)
        b = ab_ref[:, GU_TILE:].astype(F32)
        sg = _sigmoid(a)
        o_ref[:, :GU_TILE] = (ds * b * sg * (1.0 + a * (1.0 - sg))).astype(BF16)
        o_ref[:, GU_TILE:] = (ds * a * sg).astype(BF16)

    pair = pl.BlockSpec((tm, 2 * GU_TILE), lambda i, j: (i, j))
    return pl.pallas_call(
        body, name=name, grid=(T // tm, FFN_HIDDEN // GU_TILE),
        in_specs=[pl.BlockSpec((tm, K), lambda i, j: (i, 0)), pl.BlockSpec((GU_TILE, K), lambda i, j: (j, 0)), pair],
        out_specs=pair, out_shape=jax.ShapeDtypeStruct((T, 2 * FFN_HIDDEN), BF16),
        compiler_params=_cp(("parallel", "parallel")),
    )(dx, w_down, ab)


def _adamw_update(w_ref, g_ref, m_ref, v_ref, d_ref, nm_ref, nv_ref):
    gv = g_ref[...]
    nm = ADAM_B1 * m_ref[...] + (1.0 - ADAM_B1) * gv
    nv = ADAM_B2 * v_ref[...] + (1.0 - ADAM_B2) * (gv * gv)
    m_hat = nm / (1.0 - ADAM_B1 ** ADAM_STEP)
    v_hat = nv / (1.0 - ADAM_B2 ** ADAM_STEP)
    d_ref[...] = -ADAM_LR * (m_hat / (jnp.sqrt(v_hat) + ADAM_EPS) + ADAM_WD * w_ref[...])
    nm_ref[...] = nm
    nv_ref[...] = nv


def _adamw_many(ws, gs, ms, vs, name):
    n = len(ws)

    def body(*refs):
        ins, outs = refs[:4 * n], refs[4 * n:]
        for t in range(n):
            _adamw_update(ins[t], ins[n + t], ins[2 * n + t], ins[3 * n + t], outs[t], outs[n + t], outs[2 * n + t])

    shapes = [jax.ShapeDtypeStruct(w.shape, F32) for w in ws]
    out = pl.pallas_call(body, name=name, out_shape=shapes * 3, compiler_params=_cp())(*ws, *gs, *ms, *vs)
    return out[:n], out[n:2 * n], out[2 * n:]


def _adamw(w, g, m, v, name):
    R, C = w.shape
    tr = R
    for cand in (256, 352, 128, 64, 8):
        if R > cand and R % cand == 0:
            tr = cand
            break

    def body(w_ref, g_ref, m_ref, v_ref, d_ref, nm_ref, nv_ref):
        _adamw_update(w_ref, g_ref, m_ref, v_ref, d_ref, nm_ref, nv_ref)

    blk = pl.BlockSpec((tr, C), lambda i: (i, 0))
    sh = jax.ShapeDtypeStruct((R, C), F32)
    return pl.pallas_call(
        body, name=name, grid=(R // tr,), in_specs=[blk] * 4, out_specs=[blk] * 3, out_shape=[sh] * 3,
        compiler_params=_cp(("parallel",)),
    )(w, g, m, v)


def _adamw_3d(w, g, m, v, block, name):
    shape = w.shape
    grid = (shape[0] // block[0], shape[1] // block[1])
    assert shape[0] % block[0] == 0 and shape[1] % block[1] == 0 and block[2] == shape[2], (name, shape, block)

    def body(w_ref, g_ref, m_ref, v_ref, d_ref, nm_ref, nv_ref):
        _adamw_update(w_ref, g_ref, m_ref, v_ref, d_ref, nm_ref, nv_ref)

    blk = pl.BlockSpec(block, lambda i, j: (i, j, 0))
    sh = jax.ShapeDtypeStruct(shape, F32)
    return pl.pallas_call(
        body, name=name, grid=grid, in_specs=[blk] * 4, out_specs=[blk] * 3, out_shape=[sh] * 3,
        compiler_params=_cp(("parallel", "parallel")),
    )(w, g, m, v)


def _sum_slabs(x, name):
    n, R, C = x.shape
    tr = R
    for cand in (512, 256, 128, 64, 32, 16, 8):
        if R > cand and R % cand == 0:
            tr = cand
            break

    def body(x_ref, o_ref):
        acc = x_ref[0].astype(F32)
        for j in range(1, n):
            acc = acc + x_ref[j].astype(F32)
        o_ref[...] = acc

    return pl.pallas_call(
        body, name=name, grid=(R // tr,), in_specs=[pl.BlockSpec((n, tr, C), lambda i: (0, i, 0))],
        out_specs=pl.BlockSpec((tr, C), lambda i: (i, 0)), out_shape=jax.ShapeDtypeStruct((R, C), F32),
        compiler_params=_cp(("parallel",)),
    )(x)


def _multi_gather(xs, layers, name):
    nt = len(xs)
    shapes = [x.shape if lay is None else x.shape[1:] for x, lay in zip(xs, layers)]

    def body(*refs):
        x_refs, out_refs = refs[:nt], refs[nt:2 * nt]
        send_sems, recv_sems, local_sems = refs[2 * nt:]
        x_, y_, c_ = lax.axis_index("x"), lax.axis_index("y"), lax.axis_index("c")
        me, sibling = (x_, y_, c_), (x_, y_, 1 - c_)
        chips = [(1 - x_, y_), (x_, 1 - y_), (1 - x_, 1 - y_)]

        def own_block(t):
            return x_refs[t] if layers[t] is None else x_refs[t].at[layers[t]]

        def copy(t, k, block, to, own=False):
            px, py, pc = block
            dst = out_refs[t].at[4 * px + 2 * py + pc]
            return pltpu.make_async_remote_copy(
                src_ref=own_block(t) if own else dst, dst_ref=dst,
                send_sem=send_sems.at[t, k], recv_sem=recv_sems.at[t, k],
                device_id=to, device_id_type=pl.DeviceIdType.MESH)

        mine, first, passed = [], [], []
        for t in range(nt):
            mine.append(pltpu.make_async_copy(own_block(t), out_refs[t].at[4 * x_ + 2 * y_ + c_], local_sems.at[t]))
            mine[-1].start()
            first.append([copy(t, 1 + j, me, (*chip, c_), own=True) for j, chip in enumerate(chips)]
                         + [copy(t, 0, me, sibling, own=True)])
            for cp in first[-1]:
                cp.start()
        for t in range(nt):
            for j, chip in enumerate(chips):
                copy(t, 1 + j, (*chip, c_), me).wait_recv()
                passed.append(copy(t, 4 + j, (*chip, c_), sibling))
                passed[-1].start()
        for t in range(nt):
            copy(t, 0, sibling, me).wait_recv()
            for j, chip in enumerate(chips):
                copy(t, 4 + j, (*chip, 1 - c_), me).wait_recv()
        for cp in [c for f in first for c in f] + passed:
            cp.wait_send()
        for cp in mine:
            cp.wait()

    hbm = pl.BlockSpec(memory_space=pl.ANY)
    return pl.pallas_call(
        body, name=name, out_shape=[jax.ShapeDtypeStruct((N_DEV,) + tuple(s), x.dtype) for s, x in zip(shapes, xs)],
        in_specs=[hbm] * nt, out_specs=[hbm] * nt,
        scratch_shapes=[pltpu.SemaphoreType.DMA((nt, 7)), pltpu.SemaphoreType.DMA((nt, 7)),
                        pltpu.SemaphoreType.DMA((nt,))],
    )(*xs)


_HBM = pl.BlockSpec(memory_space=pltpu.HBM)
_SEM = pl.BlockSpec(memory_space=pltpu.SEMAPHORE)
_PEER_ORDER = (2, 4, 6, 3, 5, 7, 1)


def _split_copies(src_refs, land_refs, send_sems, recv_sems, layers, per_peer):
    x_, y_, c_ = lax.axis_index("x"), lax.axis_index("y"), lax.axis_index("c")
    me = 4 * x_ + 2 * y_ + c_
    copies = []
    for k in _PEER_ORDER:
        px, py, pc = x_ ^ ((k >> 2) & 1), y_ ^ ((k >> 1) & 1), c_ ^ (k & 1)
        peer = 4 * px + 2 * py + pc
        for t in range(len(src_refs)):
            if per_peer:
                src = src_refs[t].at[peer]
            else:
                src = src_refs[t] if layers[t] is None else src_refs[t].at[layers[t]]
            copies.append(pltpu.make_async_remote_copy(
                src_ref=src, dst_ref=land_refs[t].at[me],
                send_sem=send_sems.at[t * (N_DEV - 1) + k - 1], recv_sem=recv_sems.at[t * (N_DEV - 1) + k - 1],
                device_id=(px, py, pc), device_id_type=pl.DeviceIdType.MESH))
    return copies


def _own_copies(src_refs, land_refs, sems, layers, per_peer):
    nt = len(src_refs)
    me = 4 * lax.axis_index("x") + 2 * lax.axis_index("y") + lax.axis_index("c")
    copies = []
    for t in range(nt):
        if per_peer:
            src = src_refs[t].at[me]
        else:
            src = src_refs[t] if layers[t] is None else src_refs[t].at[layers[t]]
        copies.append(pltpu.make_async_copy(src, land_refs[t].at[me], sems.at[nt * (N_DEV - 1) + t]))
    return copies


def _split_start(srcs, layers, per_peer, after, name):
    nt = len(srcs)
    if per_peer:
        land_shapes = [s.shape for s in srcs]
    else:
        land_shapes = [(N_DEV,) + tuple(s.shape if lay is None else s.shape[1:]) for s, lay in zip(srcs, layers)]

    def body(*refs):
        src_refs, land_refs = refs[:nt], refs[nt:2 * nt]
        send_sems, recv_sems = refs[2 * nt + 1], refs[2 * nt + 2]
        token = refs[-1]
        for cp in _split_copies(src_refs, land_refs, send_sems, recv_sems, layers, per_peer):
            cp.start()
        for cp in _own_copies(src_refs, land_refs, send_sems, layers, per_peer):
            cp.start()
        token[...] = jnp.zeros_like(token)

    lands = [pltpu.with_memory_space_constraint(lax.empty(s, x.dtype), pltpu.HBM) for s, x in zip(land_shapes, srcs)]
    srcs = [pltpu.with_memory_space_constraint(x, pltpu.HBM) for x in srcs]
    out = pl.pallas_call(
        body, name=name,
        out_shape=(pltpu.SemaphoreType.DMA((nt * N_DEV,)), pltpu.SemaphoreType.DMA((nt * (N_DEV - 1),)),
                   *[pltpu.HBM(x.shape, x.dtype) for x in srcs], *[pltpu.HBM(s, x.dtype) for s, x in zip(land_shapes, srcs)],
                   jax.ShapeDtypeStruct((8, LANES), F32)),
        in_specs=[_HBM] * (2 * nt) + [pl.BlockSpec(memory_space=pl.ANY)],
        out_specs=(_SEM, _SEM, *([_HBM] * (2 * nt)), pl.BlockSpec(memory_space=pltpu.VMEM)),
        input_output_aliases={i: 2 + i for i in range(2 * nt)},
        compiler_params=pltpu.CompilerParams(has_side_effects=pltpu.SideEffectType.DATAFLOW_SIDE_EFFECTING),
    )(*srcs, *lands, after)
    return out[0], out[1], list(out[2:2 + nt]), list(out[2 + nt:2 + 2 * nt]), out[-1]


def _split_wait(started, layers, per_peer, after, name):
    send_sems, recv_sems, srcs, lands, _ = started
    nt = len(srcs)

    def body(*refs):
        src_refs, land_refs = refs[:nt], refs[nt:2 * nt]
        s_sems, r_sems = refs[2 * nt], refs[2 * nt + 1]
        for cp in _split_copies(src_refs, land_refs, s_sems, r_sems, layers, per_peer):
            cp.wait_send()
            cp.wait_recv()
        for cp in _own_copies(src_refs, land_refs, s_sems, layers, per_peer):
            cp.wait()

    out = pl.pallas_call(
        body, name=name,
        out_shape=tuple(pltpu.HBM(x.shape, x.dtype) for x in srcs + lands),
        in_specs=[_HBM] * (2 * nt) + [_SEM, _SEM, pl.BlockSpec(memory_space=pl.ANY)],
        out_specs=tuple([_HBM] * (2 * nt)),
        input_output_aliases={i: i for i in range(2 * nt)},
        compiler_params=pltpu.CompilerParams(has_side_effects=pltpu.SideEffectType.DATAFLOW_SIDE_EFFECTING),
    )(*srcs, *lands, send_sems, recv_sems, after)
    return list(out[nt:])


def _runs(mapping):
    runs, c, n = [], 0, len(mapping)
    while c < n:
        if mapping[c] is None:
            c += 1
            continue
        sid, d, lo = mapping[c][0], mapping[c][1] - c, c
        while c < n and mapping[c] is not None and mapping[c][0] == sid and mapping[c][1] - c == d:
            c += 1
        runs.append((lo, c, sid, d))
    return runs


def _tile_plan(mapping, src_widths):
    runs = _runs(mapping)
    plan = []
    for t in range(len(mapping) // LANES):
        pieces = []
        for lo, hi, sid, d in runs:
            lo_t, hi_t = max(lo, t * LANES), min(hi, (t + 1) * LANES)
            if lo_t >= hi_t:
                continue
            a = ((lo_t + d) // LANES) * LANES
            win = min(2 * LANES, src_widths[sid] - a)
            shift = t * LANES + d - a
            pieces.append((sid, a, win, shift, lo_t - t * LANES, hi_t - t * LANES))
        plan.append(pieces)
    return plan


def _reblock(srcs, src_views, outs, out_views, name):
    R = srcs[0].shape[-2]
    tr = min(512, R)
    widths = {sid: srcs[ai].shape[-1] for sid, (ai, _) in src_views.items()}
    plans = [(ai, li, _tile_plan(mapping, widths)) for ai, li, mapping in out_views]
    ns = len(srcs)

    def body(*refs):
        s_refs, o_refs = refs[:ns], refs[ns:]
        cache = {}

        def shift_matrix(win, shift, lo, hi):
            key = (win, shift, lo, hi)
            if key not in cache:
                r = lax.broadcasted_iota(jnp.int32, (win, LANES), 0)
                c = lax.broadcasted_iota(jnp.int32, (win, LANES), 1)
                hit = jnp.logical_and(r - c == shift, jnp.logical_and(c >= lo, c < hi))
                cache[key] = jnp.where(hit, 1.0, 0.0).astype(BF16)
            return cache[key]

        for ai, li, plan in plans:
            for t, pieces in enumerate(plan):
                acc = None
                whole = len(pieces) == 1 and pieces[0][3:] == (0, 0, LANES)
                for sid, a, win, shift, lo, hi in pieces:
                    sa, sl = src_views[sid]
                    if whole:
                        win = LANES
                    src = s_refs[sa][:, a:a + win] if sl is None else s_refs[sa][sl, :, a:a + win]
                    if whole:
                        acc = src
                    else:
                        part = jnp.dot(src, shift_matrix(win, shift, lo, hi), preferred_element_type=F32)
                        acc = part if acc is None else acc + part
                val = jnp.zeros((tr, LANES), BF16) if acc is None else acc.astype(BF16)
                if li is None:
                    o_refs[ai][:, t * LANES:(t + 1) * LANES] = val
                else:
                    o_refs[ai][li, :, t * LANES:(t + 1) * LANES] = val

    def spec(shape):
        if len(shape) == 2:
            return pl.BlockSpec((tr, shape[1]), lambda i: (i, 0))
        return pl.BlockSpec((shape[0], tr, shape[2]), lambda i: (0, i, 0))

    return pl.pallas_call(
        body, name=name, grid=(R // tr,), in_specs=[spec(s.shape) for s in srcs],
        out_specs=[spec(s) for s in outs], out_shape=[jax.ShapeDtypeStruct(s, BF16) for s in outs],
        compiler_params=_cp(("parallel",)),
    )(*srcs)


SHARDED = ("w_in", "w_gate_up", "w_proj_attn", "w_proj_pool", "w_proj_conv", "w_out", "w_down")
WEIGHT_ORDER = ("attn_norm", "w_in", "b_forget", "b_gate", "w_proj_attn", "pool_w", "pool_scale", "w_proj_pool",
                "conv_w", "w_proj_conv", "w_out", "ffn_norm", "w_gate_up", "w_down", "final_norm")
IN_SHARD, IN_SHARD_PAD = IN_COLS // N_DEV, 896
GU_SHARD, GU_SHARD_PAD = 2 * FFN_HIDDEN // N_DEV, 768


def _w_in_col(c):
    if c < OFF_QKV:
        return c + 3592
    if c < OFF_U:
        base, off = (0, OFF_QKV) if c < OFF_CONV else (2056, OFF_CONV)
        j, t = divmod(c - off, TRIPLE)
        which, e = divmod(t, LANES)
        return base + which * BRANCH_W + j * LANES + e
    return c - OFF_U + 1544


def _w_in_full(gathered, name):
    main = [divmod(_w_in_col(c), IN_SHARD) for c in range(MAIN_COLS)]
    fcols = [divmod(1536 + c, IN_SHARD) if c < N_HEADS else None for c in range(LANES)]
    R = gathered.shape[1]
    return _reblock([gathered], {i: (0, i) for i in range(N_DEV)}, [(R, MAIN_COLS), (R, LANES)],
                    [(0, None, main), (1, None, fcols)], name)


def _w_in_slabs(dmain, dwf, name):
    inv = {_w_in_col(c): ("m", c) for c in range(MAIN_COLS)}
    inv.update({1536 + c: ("f", c) for c in range(N_HEADS)})
    views = []
    for i in range(N_DEV):
        mapping = [inv[IN_SHARD * i + j] if j < IN_SHARD else None for j in range(IN_SHARD_PAD)]
        views.append((0, i, mapping))
    R = dmain.shape[0]
    return _reblock([dmain, dwf], {"m": (0, None), "f": (1, None)}, [(N_DEV, R, IN_SHARD_PAD)], views, name)[0]


def _w_gu_full(gathered, name):
    mapping = [divmod(_gu_col(c), GU_SHARD) for c in range(2 * FFN_HIDDEN)]
    R = gathered.shape[1]
    return _reblock([gathered], {i: (0, i) for i in range(N_DEV)}, [(R, 2 * FFN_HIDDEN)], [(0, None, mapping)], name)[0]


def _w_gu_slabs(dw, name):
    inv = {_gu_col(c): c for c in range(2 * FFN_HIDDEN)}
    views = [(0, i, [("w", inv[GU_SHARD * i + j]) if j < GU_SHARD else None for j in range(GU_SHARD_PAD)])
             for i in range(N_DEV)]
    R = dw.shape[0]
    return _reblock([dw], {"w": (0, None)}, [(N_DEV, R, GU_SHARD_PAD)], views, name)[0]


def _layer_fwd(x, W, n_seq, l, h1=None, next_norm=None):
    T = x.shape[0]
    sfx = f"_l{l}"
    if h1 is None:
        h1 = _rms_fwd(x, W["attn_norm"], "rms1" + sfx)
    proj, f = _matmul(h1, W["w_main"], mode="nn", out_dtype=BF16, name="proj_main" + sfx, side=(W["w_f"], F32))
    qa, ka, va = _fox_prep(f, W["b_forget"], proj, n_seq, "fox_prep" + sfx)
    oa, oa32, lse = _attn_fwd2(qa, ka, va, n_seq, "attn_fwd" + sfx)
    if "late" in W:
        W.update(W.pop("late")(oa))
    ob = _pool_fwd(proj, W["pool_w"], W["pool_scale"], n_seq, "pool_fwd" + sfx)
    oc = _conv_fwd(proj, W["conv_w"], n_seq, "conv_fwd" + sfx)
    mixed = _mix_fwd(oa, ob, oc, W["w_proj_attn"], W["w_proj_pool"], W["w_proj_conv"], proj, W["b_gate"],
                     "mix_fwd" + sfx)
    x2, h2 = _matmul(mixed, W["w_out"], mode="nn", out_dtype=F32, name="out_proj" + sfx, tm=1024, tn=1024,
                     residual=x, rms_g=W["ffn_norm"])
    ab, s = _gate_up_swiglu(h2, W["w_gate_up"], "gate_up" + sfx)
    x3 = _matmul(s, W["w_down"], mode="nn", out_dtype=F32, name="down" + sfx, tm=1024, tn=1024, tk=1408,
                 residual=x2, rms_g=next_norm)
    x3, h1_next = x3 if next_norm is not None else (x3, None)
    saved = dict(x=x, h1=h1, proj=proj, f=f, qa=qa, ka=ka, oa=oa, oa32=oa32, lse=lse, ob=ob, oc=oc, mixed=mixed, x2=x2,
                 h2=h2, ab=ab, s=s)
    return x3, saved, h1_next


def _layer_bwd(dx3, dx3b, W, sv, n_seq, l, stage=None):
    T = dx3.shape[0]
    sfx = f"_l{l}"
    G = {}
    stage = stage or (lambda l, group, G, W: W)
    dab = _swiglu_bwd_fused(dx3b, W["w_down"], sv["ab"], "d_ab" + sfx)
    G["w_down"] = _matmul(sv["s"], dx3b, mode="tn", out_dtype=BF16, name="dw_down" + sfx, tm=256, tn=1024)
    dh2 = _matmul(dab, W["w_gate_up"], mode="nt", out_dtype=BF16, name="d_h2" + sfx, tm=1024, tn=1024, tk=1408)
    G["w_gate_up"] = _matmul(sv["h2"], dab, mode="tn", out_dtype=BF16, name="dw_gate_up" + sfx, tm=1024)
    W = stage(l, "ffn", G, W)
    dx2, dx2b, G["ffn_norm"] = _rms_bwd(sv["x2"], W["ffn_norm"], dh2, dx3, "rms2_bwd" + sfx)
    dmixed = _matmul(dx2b, W["w_out"], mode="nt", out_dtype=BF16, name="d_mixed" + sfx)
    G["w_out"] = _matmul(sv["mixed"], dx2b, mode="tn", out_dtype=BF16, name="dw_out" + sfx, tm=1024)
    dya, dyb, dyc, dproj, G["b_gate"] = _mix_bwd(sv["oa"], sv["ob"], sv["oc"], W["w_proj_attn"], W["w_proj_pool"],
                                                 W["w_proj_conv"], sv["proj"], W["b_gate"], dmixed, "mix_bwd" + sfx)
    douts = {}
    for br, dy, o in (("attn", dya, sv["oa"]), ("pool", dyb, sv["ob"]), ("conv", dyc, sv["oc"])):
        douts[br] = _matmul(dy, W["w_proj_" + br], mode="nt", out_dtype=BF16, name=f"d_{br}_out" + sfx)
        G["w_proj_" + br] = _matmul(o, dy, mode="tn", out_dtype=BF16, name=f"dw_proj_{br}" + sfx, tm=512)
    W = stage(l, "mix", G, W)
    dproj, G["conv_w"] = _conv_bwd(sv["proj"], douts["conv"], W["conv_w"], dproj, n_seq, "conv_bwd" + sfx)
    dproj, G["pool_w"], G["pool_scale"] = _pool_bwd(sv["proj"], douts["pool"], W["pool_w"], W["pool_scale"], dproj,
                                                    n_seq, "pool_bwd" + sfx)
    dproj, dFk = _attn_bwd(sv["qa"], sv["ka"], sv["proj"], douts["attn"], sv["oa32"], sv["lse"], dproj, n_seq,
                           "attn_bwd" + sfx)
    dF = jnp.pad(dFk.reshape(N_HEADS, T).T, ((0, 0), (0, LANES - N_HEADS)))
    df, G["b_forget"] = _fox_cumsum_bwd(sv["f"], W["b_forget"], dF, n_seq, "fox_cumsum_bwd" + sfx)
    G["w_main"], G["w_f"] = _matmul(sv["h1"], dproj, mode="tn", out_dtype=BF16, name="dw_main" + sfx, tm=1024,
                                    side=(df, BF16))
    W = stage(l, "w_in", G, W)
    dh1 = _matmul(dproj, W["w_main"], mode="nt", out_dtype=BF16, name="d_h1_main" + sfx, tm=1024, tn=1024, tk=1664,
                  extra=(df, W["w_f"]))
    dx, dxb, G["attn_norm"] = _rms_bwd(sv["x"], W["attn_norm"], dh1, dx2, "rms1_bwd" + sfx)
    return dx, dxb, G


def _replicated_operands(rep, l):
    W = {}
    W["attn_norm"], W["ffn_norm"] = rep["attn_norm"][l], rep["ffn_norm"][l]
    W["b_forget"] = jnp.pad(rep["b_forget"][l].reshape(1, N_HEADS), ((0, 0), (0, LANES - N_HEADS)))
    W["b_gate"] = rep["b_gate"][l].reshape(1, GATE_W)
    W["pool_w"] = rep["pool_w"][l].astype(BF16)
    W["pool_scale"] = rep["pool_scale"][l].reshape(1, BRANCH_W)
    return W


def _local_step(x, target, get_W, attn_norms, final_norm, stage=None):
    n_seq, S, Dm = x.shape
    T = n_seq * S
    xt = x.reshape(T, Dm)
    saved, Ws, h1 = [], [], None
    for l in range(DEPTH):
        Ws.append(get_W(l, xt))
        next_norm = attn_norms[l + 1] if l + 1 < DEPTH else None
        xt, sv, h1 = _layer_fwd(xt, Ws[l], n_seq, l, h1, next_norm)
        saved.append(sv)
    loss, dx, dxb, g_final = _loss_head(xt, final_norm, target.reshape(T, Dm), "loss_head")
    grads = [None] * DEPTH
    for l in reversed(range(DEPTH)):
        dx, dxb, grads[l] = _layer_bwd(dx, dxb, Ws[l], saved[l], n_seq, l, stage)
    return loss, dx.reshape(n_seq, S, Dm), grads, g_final


def _padded_shards(weights):
    pads = {"w_in": IN_SHARD_PAD - IN_SHARD, "w_gate_up": GU_SHARD_PAD - GU_SHARD}
    return {n: jnp.pad(weights[n], ((0, 0), (0, 0), (0, pads.get(n, 0)))).astype(BF16) for n in SHARDED}


def _full_operands(g, l):
    W = {}
    if "w_in" in g:
        W["w_main"], W["w_f"] = _w_in_full(g["w_in"], f"w_in_full_l{l}")
    if "w_gate_up" in g:
        W["w_gate_up"] = _w_gu_full(g["w_gate_up"], f"w_gate_up_full_l{l}")
    for n in ("w_proj_attn", "w_proj_pool", "w_proj_conv"):
        if n in g:
            W[n] = jnp.transpose(g[n], (1, 0, 2)).reshape(BRANCH_W, D_MODEL)
    if "w_out" in g:
        W["w_out"] = g["w_out"].reshape(D_MODEL, D_MODEL)
    if "w_down" in g:
        W["w_down"] = g["w_down"].reshape(FFN_HIDDEN, D_MODEL)
    return W


GRAD_GROUPS = {"ffn": ("w_down", "w_gate_up"),
               "mix": ("w_out", "w_proj_attn", "w_proj_pool", "w_proj_conv"),
               "w_in": ("w_in",)}


def _grad_slabs(G, n, l):
    if n == "w_in":
        return _w_in_slabs(G["w_main"], G["w_f"], f"w_in_slabs_l{l}")
    if n == "w_gate_up":
        return _w_gu_slabs(G["w_gate_up"], f"w_gate_up_slabs_l{l}")
    if n == "w_out":
        return G["w_out"].reshape(N_DEV, D_MODEL // N_DEV, D_MODEL)
    if n == "w_down":
        return G["w_down"].reshape(N_DEV, FFN_HIDDEN // N_DEV, D_MODEL)
    return jnp.transpose(G[n].reshape(BRANCH_W, N_DEV, D_MODEL // N_DEV), (1, 0, 2))


def _sum_layer_grads(recv, l):
    out = {n: _sum_slabs(r, f"sum_{n}_l{l}") for n, r in recv.items()}
    if "w_in" in out:
        out["w_in"] = out["w_in"][:, :IN_SHARD]
    if "w_gate_up" in out:
        out["w_gate_up"] = out["w_gate_up"][:, :GU_SHARD]
    return out


def _sum_small(xs, name):
    def body(*refs):
        for x_ref, o_ref in zip(refs[:len(xs)], refs[len(xs):]):
            acc = x_ref[0]
            for j in range(1, N_DEV):
                acc = acc + x_ref[j]
            o_ref[...] = acc

    return pl.pallas_call(
        body, name=name, out_shape=[jax.ShapeDtypeStruct(x.shape[1:], F32) for x in xs],
        compiler_params=_cp(),
    )(*xs)


def _as_2d(a):
    if a.ndim == 1:
        return a.reshape(1, -1)
    return a.reshape(-1, a.shape[-1])


def kernel(x, attn_norm, w_in, b_forget, b_gate, w_proj_attn, pool_w, pool_scale, w_proj_pool, conv_w, w_proj_conv, w_out, ffn_norm, w_gate_up, w_down, final_norm, loss_target, m_attn_norm, m_w_in, m_b_forget, m_b_gate, m_w_proj_attn, m_pool_w, m_pool_scale, m_w_proj_pool, m_conv_w, m_w_proj_conv, m_w_out, m_ffn_norm, m_w_gate_up, m_w_down, m_final_norm, v_attn_norm, v_w_in, v_b_forget, v_b_gate, v_w_proj_attn, v_pool_w, v_pool_scale, v_w_proj_pool, v_conv_w, v_w_proj_conv, v_w_out, v_ffn_norm, v_w_gate_up, v_w_down, v_final_norm):
    weights = dict(attn_norm=attn_norm, w_in=w_in, b_forget=b_forget, b_gate=b_gate, w_proj_attn=w_proj_attn,
                   pool_w=pool_w, pool_scale=pool_scale, w_proj_pool=w_proj_pool, conv_w=conv_w,
                   w_proj_conv=w_proj_conv, w_out=w_out, ffn_norm=ffn_norm, w_gate_up=w_gate_up, w_down=w_down,
                   final_norm=final_norm)
    moments_m = dict(attn_norm=m_attn_norm, w_in=m_w_in, b_forget=m_b_forget, b_gate=m_b_gate,
                     w_proj_attn=m_w_proj_attn, pool_w=m_pool_w, pool_scale=m_pool_scale, w_proj_pool=m_w_proj_pool,
                     conv_w=m_conv_w, w_proj_conv=m_w_proj_conv, w_out=m_w_out, ffn_norm=m_ffn_norm,
                     w_gate_up=m_w_gate_up, w_down=m_w_down, final_norm=m_final_norm)
    moments_v = dict(attn_norm=v_attn_norm, w_in=v_w_in, b_forget=v_b_forget, b_gate=v_b_gate,
                     w_proj_attn=v_w_proj_attn, pool_w=v_pool_w, pool_scale=v_pool_scale, w_proj_pool=v_w_proj_pool,
                     conv_w=v_conv_w, w_proj_conv=v_w_proj_conv, w_out=v_w_out, ffn_norm=v_ffn_norm,
                     w_gate_up=v_w_gate_up, w_down=v_w_down, final_norm=v_final_norm)

    sh = _padded_shards(weights)
    names = list(SHARDED)
    rest = [n for n in names if n != "w_in"]
    me = 4 * lax.axis_index("x") + 2 * lax.axis_index("y") + lax.axis_index("c")
    w_in0, conv_all = _multi_gather([sh["w_in"], conv_w], [0, None], "gather_w_in_l0")
    started, after = {}, w_in0
    for l in range(DEPTH):
        for group, gnames in (("w_in", ["w_in"]), ("rest", rest)):
            if (l, group) != (0, "w_in"):
                started[l, group] = _split_start([sh[n] for n in gnames], [l] * len(gnames), False, after,
                                                 f"gather_start_{group}_l{l}")
                after = started[l, group][4]
    last_token = after

    def get_W(l, xt):
        if l == 0:
            w_in = w_in0
        else:
            w_in = _split_wait(started[l, "w_in"], [l], False, xt, f"gather_wait_w_in_l{l}")[0]
        W = _full_operands({"w_in": w_in}, l)

        def late(after):
            lands = _split_wait(started[l, "rest"], [l] * len(rest), False, after, f"gather_wait_rest_l{l}")
            return _full_operands(dict(zip(rest, lands)), l)

        W["late"] = late
        W.update(_replicated_operands(weights, l))
        W["conv_w"] = jnp.transpose(conv_all[:, l], (1, 0, 2)).reshape(CONV_K, BRANCH_W)
        if l == 0:
            W["attn_norm"] = W["attn_norm"] + last_token[0, 0]
        return W

    exchanges = []

    def stage(l, group, G, W):
        gnames = GRAD_GROUPS[group]
        slabs = [_grad_slabs(G, n, l) for n in gnames]
        started = _split_start(slabs, None, True, slabs[0], f"exchange_start_{group}_l{l}")
        exchanges.append((l, group, gnames, slabs, started))
        tie = {"ffn": "ffn_norm", "mix": "conv_w", "w_in": "w_f"}[group]
        W = dict(W)
        W[tie] = W[tie] + started[4][0, 0].astype(W[tie].dtype)
        return W

    loss_part, grad_x, grads, g_final = _local_step(x, loss_target, get_W, attn_norm, final_norm, stage)
    after = grad_x
    for l, group, gnames, slabs, started in exchanges:
        lands = _split_wait(started, None, True, after, f"exchange_wait_{group}_l{l}")
        grads[l].update(_sum_layer_grads(dict(zip(gnames, lands)), l))
    gw = {n: jnp.stack([grads[l][n] for l in range(DEPTH)]) for n in SHARDED}

    small = ("attn_norm", "b_forget", "b_gate", "pool_w", "pool_scale", "ffn_norm", "conv_w")
    parts = [jnp.stack([grads[l][n] for l in range(DEPTH)]) for n in small] + [g_final, loss_part]
    gathered = _multi_gather(parts, [None] * len(parts), "gather_small_grads")
    summed = _sum_small(gathered, "sum_small_grads")
    for n, s in zip(small, summed):
        gw[n] = s
    gw["attn_norm"], gw["ffn_norm"] = gw["attn_norm"][:, 0], gw["ffn_norm"][:, 0]
    gw["b_forget"] = gw["b_forget"][:, 0, :N_HEADS]
    gw["b_gate"], gw["pool_scale"] = gw["b_gate"][:, 0], gw["pool_scale"][:, 0]
    gw["conv_w"] = lax.dynamic_slice_in_dim(gw["conv_w"], me * (BRANCH_W // N_DEV), BRANCH_W // N_DEV, axis=2)
    gw["final_norm"] = summed[-2][0]
    loss = summed[-1][0, 0]

    deltas, new_m, new_v = {}, {}, {}
    views = {"w_in": ((2, 0, 1), (1, 2, 0), (49, DEPTH, D_MODEL)),
             "w_gate_up": ((0, 2, 1), (0, 2, 1), (1, GU_SHARD // 2, D_MODEL))}
    for n in SHARDED:
        if n in views:
            perm, inv, block = views[n]
            gt = jnp.transpose(gw[n], perm)
            d, nm, nv = _adamw_3d(jnp.transpose(weights[n], perm), gt, jnp.transpose(moments_m[n], perm),
                                  jnp.transpose(moments_v[n], perm), block, "adamw_" + n)
            deltas[n], new_m[n], new_v[n] = (jnp.transpose(a, inv) for a in (d, nm, nv))
            gw[n] = jnp.transpose(gt, inv)
            continue
        shape = weights[n].shape
        d, nm, nv = _adamw(_as_2d(weights[n]), _as_2d(gw[n]), _as_2d(moments_m[n]), _as_2d(moments_v[n]),
                           "adamw_" + n)
        deltas[n], new_m[n], new_v[n] = d.reshape(shape), nm.reshape(shape), nv.reshape(shape)
    rest_names = [n for n in WEIGHT_ORDER if n not in SHARDED]
    ds, nms, nvs = _adamw_many(*[[_as_2d(src[n]) for n in rest_names] for src in (weights, gw, moments_m, moments_v)],
                               "adamw_small")
    for n, d, nm, nv in zip(rest_names, ds, nms, nvs):
        shape = weights[n].shape
        deltas[n], new_m[n], new_v[n] = d.reshape(shape), nm.reshape(shape), nv.reshape(shape)

    return (loss, grad_x, *[gw[n] for n in WEIGHT_ORDER], *[deltas[n] for n in WEIGHT_ORDER],
            *[new_m[n] for n in WEIGHT_ORDER], *[new_v[n] for n in WEIGHT_ORDER])
```

```python
import functools

import jax
import jax.numpy as jnp
from jax import lax
from jax.experimental import pallas as pl
from jax.experimental.pallas import tpu as pltpu

F32 = jnp.float32
BF16 = jnp.bfloat16

N_DEV = 8
D_MODEL = 1024
DEPTH = 2
N_HEADS = 8
HEAD_DIM = 64
BRANCH_W = 512
POOL_WINDOWS = (2, 4, 8, 16)
POOL_GD = 128
CONV_K = 3
FFN_HIDDEN = 2816
GATE_W = 3 * D_MODEL
IN_COLS = 6664
MAIN_COLS = GATE_W + 7 * BRANCH_W
RMS_EPS = 1e-6
NEG_INF = -1e30

ADAM_LR = 0.001
ADAM_B1 = 0.9
ADAM_B2 = 0.999
ADAM_EPS = 1e-08
ADAM_WD = 0.01
ADAM_STEP = 10

LANES = 128
VMEM_LIMIT = 56 * 1024 * 1024
CUM_BLK = 256

TRIPLE = 3 * LANES
OFF_G, OFF_QKV, OFF_CONV, OFF_U = 0, 3072, 4608, 6144


def _cp(sem=None):
    return pltpu.CompilerParams(dimension_semantics=sem, vmem_limit_bytes=VMEM_LIMIT)


def _sigmoid(z):
    return 1.0 / (1.0 + jnp.exp(-z))


def _matmul(a, b, *, mode, out_dtype, name, tm=2048, tn=512, tk=None, residual=None, rms_g=None, side=None,
            extra=None):
    if mode == "nn":
        (M, K), N = a.shape, b.shape[1]
    elif mode == "nt":
        (M, K), N = a.shape, b.shape[0]
    else:
        (K, M), N = a.shape, b.shape[1]
    tm, tn, tk = min(tm, M), min(tn, N), K if tk is None else min(tk, K)
    assert M % tm == 0 and N % tn == 0 and K % tk == 0, (name, M, N, K, tm, tn, tk)
    nk = K // tk
    if mode == "nn":
        a_spec = pl.BlockSpec((tm, tk), lambda i, j, k: (i, k))
        b_spec = pl.BlockSpec((tk, tn), lambda i, j, k: (k, j))
        dims = (((1,), (0,)), ((), ()))
    elif mode == "nt":
        a_spec = pl.BlockSpec((tm, tk), lambda i, j, k: (i, k))
        b_spec = pl.BlockSpec((tn, tk), lambda i, j, k: (j, k))
        dims = (((1,), (1,)), ((), ()))
    else:
        a_spec = pl.BlockSpec((tk, tm), lambda i, j, k: (k, i))
        b_spec = pl.BlockSpec((tk, tn), lambda i, j, k: (k, j))
        dims = (((0,), (0,)), ((), ()))
    o_spec = pl.BlockSpec((tm, tn), lambda i, j, k: (i, j))
    has_res, has_norm, has_side, has_extra = (v is not None for v in (residual, rms_g, side, extra))
    assert not has_norm or tn == N, (name, tn, N)
    assert not has_side or (nk == 1 and mode != "nt"), name

    in_specs, args = [a_spec, b_spec], [a, b]
    out_specs, out_shape = [o_spec], [jax.ShapeDtypeStruct((M, N), out_dtype)]
    if has_res:
        in_specs.append(o_spec)
        args.append(residual)
    if has_norm:
        in_specs.append(pl.BlockSpec((1, N), lambda i, j, k: (0, 0)))
        args.append(rms_g.reshape(1, N))
        out_specs.append(o_spec)
        out_shape.append(jax.ShapeDtypeStruct((M, N), BF16))
    if has_side:
        b_side, side_dtype = side
        ns = b_side.shape[1]
        in_specs.append(pl.BlockSpec((K, ns), lambda i, j, k: (0, 0)))
        args.append(b_side)
        out_specs.append(pl.BlockSpec((tm, ns), lambda i, j, k: (i, 0)))
        out_shape.append(jax.ShapeDtypeStruct((M, ns), side_dtype))
    if has_extra:
        a2, b2 = extra
        in_specs += [pl.BlockSpec((tm, a2.shape[1]), lambda i, j, k: (i, 0)),
                     pl.BlockSpec((tn, b2.shape[1]), lambda i, j, k: (j, 0))]
        args += [a2, b2]
    n_in = len(args)

    def body(*refs):
        ins, outs = list(refs[2:n_in]), list(refs[n_in:n_in + len(out_shape)])
        a_ref, b_ref = refs[:2]
        r_ref = ins.pop(0) if has_res else None
        g_ref = ins.pop(0) if has_norm else None
        bs_ref = ins.pop(0) if has_side else None
        a2_ref, b2_ref = (ins.pop(0), ins.pop(0)) if has_extra else (None, None)
        o_ref = outs.pop(0)
        h_ref = outs.pop(0) if has_norm else None
        so_ref = outs.pop(0) if has_side else None

        def finish(acc):
            if has_res:
                acc = acc + r_ref[...].astype(F32)
            if has_extra:
                acc = acc + lax.dot_general(a2_ref[...], b2_ref[...], (((1,), (1,)), ((), ())),
                                            preferred_element_type=F32)
            o_ref[...] = acc.astype(out_dtype)
            if has_norm:
                r = lax.rsqrt(jnp.mean(acc * acc, axis=-1, keepdims=True) + RMS_EPS)
                h_ref[...] = ((acc * r) * g_ref[...]).astype(BF16)

        if has_side:
            @pl.when(pl.program_id(1) == 0)
            def _():
                side_dims = (((1,), (0,)), ((), ())) if mode == "nn" else dims
                so_ref[...] = lax.dot_general(a_ref[...], bs_ref[...], side_dims,
                                              preferred_element_type=F32).astype(so_ref.dtype)

        prod = lax.dot_general(a_ref[...], b_ref[...], dims, preferred_element_type=F32)
        if nk == 1:
            finish(prod)
            return
        acc_ref = refs[-1]
        k = pl.program_id(2)

        @pl.when(k == 0)
        def _():
            acc_ref[...] = prod

        @pl.when(jnp.logical_and(k > 0, k < nk - 1))
        def _():
            acc_ref[...] += prod

        @pl.when(k == nk - 1)
        def _():
            finish(acc_ref[...] + prod)

    single = len(out_shape) == 1
    return pl.pallas_call(
        body, name=name, grid=(M // tm, N // tn, nk), in_specs=in_specs,
        out_specs=out_specs[0] if single else out_specs, out_shape=out_shape[0] if single else out_shape,
        scratch_shapes=[pltpu.VMEM((tm, tn), F32)] if nk > 1 else [],
        compiler_params=_cp(("parallel", "arbitrary" if has_side else "parallel", "arbitrary")),
    )(*args)


def _rms_fwd(x, g, name):
    T, Dm = x.shape
    tm = min(512, T)

    def body(x_ref, g_ref, h_ref):
        xf = x_ref[...]
        r = lax.rsqrt(jnp.mean(xf * xf, axis=-1, keepdims=True) + RMS_EPS)
        h_ref[...] = ((xf * r) * g_ref[...]).astype(BF16)

    return pl.pallas_call(
        body, name=name, grid=(T // tm,),
        in_specs=[pl.BlockSpec((tm, Dm), lambda i: (i, 0)), pl.BlockSpec((1, Dm), lambda i: (0, 0))],
        out_specs=pl.BlockSpec((tm, Dm), lambda i: (i, 0)),
        out_shape=jax.ShapeDtypeStruct((T, Dm), BF16),
        compiler_params=_cp(("parallel",)),
    )(x, g.reshape(1, Dm))


def _rms_bwd(x, g, dh, dres, name):
    T, Dm = x.shape
    tm = min(512, T)

    def body(x_ref, g_ref, dh_ref, dres_ref, dx_ref, dxb_ref, dg_ref):
        i = pl.program_id(0)
        xf = x_ref[...]
        r = lax.rsqrt(jnp.mean(xf * xf, axis=-1, keepdims=True) + RMS_EPS)
        xn = xf * r
        dhf = dh_ref[...].astype(F32)
        dxn = dhf * g_ref[...]
        c = jnp.mean(dxn * xn, axis=-1, keepdims=True)
        dx = dres_ref[...] + r * (dxn - xn * c)
        dx_ref[...] = dx
        dxb_ref[...] = dx.astype(BF16)
        part = jnp.sum(dhf * xn, axis=0, keepdims=True)

        @pl.when(i == 0)
        def _():
            dg_ref[...] = part

        @pl.when(i > 0)
        def _():
            dg_ref[...] += part

    row = pl.BlockSpec((tm, Dm), lambda i: (i, 0))
    vec = pl.BlockSpec((1, Dm), lambda i: (0, 0))
    return pl.pallas_call(
        body, name=name, grid=(T // tm,), in_specs=[row, vec, row, row], out_specs=[row, row, vec],
        out_shape=[jax.ShapeDtypeStruct((T, Dm), F32), jax.ShapeDtypeStruct((T, Dm), BF16),
                   jax.ShapeDtypeStruct((1, Dm), F32)],
        compiler_params=_cp(("arbitrary",)),
    )(x, g.reshape(1, Dm), dh, dres)


def _loss_head(x, g, target, name):
    T, Dm = x.shape
    tm = min(512, T)

    def body(x_ref, g_ref, t_ref, loss_ref, dx_ref, dxb_ref, dg_ref):
        i = pl.program_id(0)
        xf = x_ref[...]
        gv = g_ref[...]
        r = lax.rsqrt(jnp.mean(xf * xf, axis=-1, keepdims=True) + RMS_EPS)
        xn = xf * r
        diff = xn * gv - t_ref[...]
        per_tok = jnp.mean(diff * diff, axis=-1, keepdims=True)
        lpart = 0.5 * jnp.sum(per_tok, axis=0, keepdims=True) + jnp.zeros((1, LANES), F32)
        dy = diff * (1.0 / Dm)
        dxn = dy * gv
        c = jnp.mean(dxn * xn, axis=-1, keepdims=True)
        dx = r * (dxn - xn * c)
        dx_ref[...] = dx
        dxb_ref[...] = dx.astype(BF16)
        part = jnp.sum(dy * xn, axis=0, keepdims=True)

        @pl.when(i == 0)
        def _():
            dg_ref[...] = part
            loss_ref[...] = lpart

        @pl.when(i > 0)
        def _():
            dg_ref[...] += part
            loss_ref[...] += lpart

    row = pl.BlockSpec((tm, Dm), lambda i: (i, 0))
    vec = pl.BlockSpec((1, Dm), lambda i: (0, 0))
    lsp = pl.BlockSpec((1, LANES), lambda i: (0, 0))
    return pl.pallas_call(
        body, name=name, grid=(T // tm,), in_specs=[row, vec, row], out_specs=[lsp, row, row, vec],
        out_shape=[jax.ShapeDtypeStruct((1, LANES), F32), jax.ShapeDtypeStruct((T, Dm), F32),
                   jax.ShapeDtypeStruct((T, Dm), BF16), jax.ShapeDtypeStruct((1, Dm), F32)],
        compiler_params=_cp(("arbitrary",)),
    )(x, g.reshape(1, Dm), target)


def _split_bf16(v):
    hi = v.astype(BF16)
    r1 = v - hi.astype(F32)
    mid = r1.astype(BF16)
    lo = (r1 - mid.astype(F32)).astype(BF16)
    return hi, mid, lo


def _tri_dot(tri, v):
    hi, mid, lo = _split_bf16(v)
    dot = functools.partial(jnp.dot, preferred_element_type=F32)
    return dot(tri, hi) + dot(tri, mid) + dot(tri, lo)


def _log_sigmoid(z):
    return jnp.minimum(z, 0.0) - jnp.log(1.0 + jnp.exp(-jnp.abs(z)))


def _fox_cumsum_bwd(f, bf, dF, n_seq, name):
    T = f.shape[0]
    S = T // n_seq
    c = min(CUM_BLK, S)

    def body(f_ref, b_ref, dF_ref, df_ref, db_ref):
        b = pl.program_id(0)
        ri = lax.broadcasted_iota(jnp.int32, (c, c), 0)
        ci = lax.broadcasted_iota(jnp.int32, (c, c), 1)
        tri = (ri <= ci).astype(BF16)
        carry = jnp.zeros((1, LANES), F32)
        dbp = jnp.zeros((1, LANES), F32)
        for j in reversed(range(S // c)):
            dFc = dF_ref[j * c:(j + 1) * c, :]
            dlf = _tri_dot(tri, dFc) + carry
            carry = carry + jnp.sum(dFc, axis=0, keepdims=True)
            z = f_ref[j * c:(j + 1) * c, :] + b_ref[...]
            dz = dlf * _sigmoid(-z)
            df_ref[j * c:(j + 1) * c, :] = dz.astype(BF16)
            dbp = dbp + jnp.sum(dz, axis=0, keepdims=True)

        @pl.when(b == 0)
        def _():
            db_ref[...] = dbp

        @pl.when(b > 0)
        def _():
            db_ref[...] += dbp

    blk = pl.BlockSpec((S, LANES), lambda b: (b, 0))
    vec = pl.BlockSpec((1, LANES), lambda b: (0, 0))
    return pl.pallas_call(
        body, name=name, grid=(n_seq,), in_specs=[blk, vec, blk], out_specs=[blk, vec],
        out_shape=[jax.ShapeDtypeStruct((T, LANES), BF16), jax.ShapeDtypeStruct((1, LANES), F32)],
        compiler_params=_cp(("arbitrary",)),
    )(f, bf, dF)


def _pair_masks():
    lane = lax.broadcasted_iota(jnp.int32, (1, LANES), 1)
    lo = lane < HEAD_DIM
    return lo, jnp.logical_not(lo)


AUG0 = HEAD_DIM
Q_TILE, K_CHUNK, ROW_GROUP = 512, 256, 64


def _fox_prep(f, bf, proj, n_seq, name):
    T = f.shape[0]
    S = T // n_seq
    c = min(CUM_BLK, S)

    def body(f_ref, b_ref, qkv_ref, qa_ref, ka_ref, va_ref):
        ri = lax.broadcasted_iota(jnp.int32, (c, c), 0)
        ci = lax.broadcasted_iota(jnp.int32, (c, c), 1)
        tri = (ri >= ci).astype(BF16)
        lane = lax.broadcasted_iota(jnp.int32, (c, LANES), 1)
        carry = jnp.zeros((1, LANES), F32)
        for j in range(S // c):
            rows = slice(j * c, (j + 1) * c)
            lf = _log_sigmoid(f_ref[rows, :] + b_ref[...])
            Fc = _tri_dot(tri, lf) + carry
            carry = carry + jnp.sum(lf, axis=0, keepdims=True)
            for h in range(N_HEADS):
                col = jnp.sum(jnp.where(lane == h, Fc, 0.0), axis=-1, keepdims=True)
                hi = col.astype(BF16).astype(F32)
                r1 = col - hi
                mid = r1.astype(BF16).astype(F32)
                lo = r1 - mid
                ones_q = jnp.logical_and(lane >= AUG0 + 3, lane < AUG0 + 6)
                ones_k = jnp.logical_and(lane >= AUG0, lane < AUG0 + 3)
                aug_q = jnp.where(lane == AUG0, hi, jnp.where(lane == AUG0 + 1, mid, jnp.where(
                    lane == AUG0 + 2, lo, jnp.where(ones_q, 1.0, 0.0))))
                aug_k = jnp.where(lane == AUG0 + 3, -hi, jnp.where(lane == AUG0 + 4, -mid, jnp.where(
                    lane == AUG0 + 5, -lo, jnp.where(ones_k, 1.0, 0.0))))
                base = (h // 2) * TRIPLE
                qp, kp, vp = (qkv_ref[rows, base + t * LANES:base + (t + 1) * LANES].astype(F32) for t in range(3))
                if h % 2:
                    qp, kp, vp = (pltpu.roll(a, HEAD_DIM, 1) for a in (qp, kp, vp))
                out = slice(h * LANES, (h + 1) * LANES)
                qa_ref[rows, out] = jnp.where(lane < HEAD_DIM, qp * (HEAD_DIM ** -0.5), aug_q).astype(BF16)
                ka_ref[rows, out] = jnp.where(lane < HEAD_DIM, kp, aug_k).astype(BF16)
                va_ref[rows, out] = jnp.where(lane < HEAD_DIM, vp, jnp.where(lane == AUG0, 1.0, 0.0)).astype(BF16)

    fblk = pl.BlockSpec((S, LANES), lambda b: (b, 0))
    out = pl.BlockSpec((S, N_HEADS * LANES), lambda b: (b, 0))
    sh = jax.ShapeDtypeStruct((T, N_HEADS * LANES), BF16)
    return pl.pallas_call(
        body, name=name, grid=(n_seq,),
        in_specs=[fblk, pl.BlockSpec((1, LANES), lambda b: (0, 0)),
                  pl.BlockSpec((S, 4 * TRIPLE), lambda b: (b, OFF_QKV // (4 * TRIPLE)))],
        out_specs=[out, out, out], out_shape=[sh, sh, sh],
        compiler_params=_cp(("parallel",)),
    )(f, bf, proj)


def _band_mask(q0, k0, nq, nk):
    row = q0 + lax.broadcasted_iota(jnp.int32, (nq, nk), 0)
    col = k0 + lax.broadcasted_iota(jnp.int32, (nq, nk), 1)
    return col <= row


_NT = (((1,), (1,)), ((), ()))
_TN = (((0,), (0,)), ((), ()))


def _attn_fwd2(qa, ka, va, n_seq, name):
    T = qa.shape[0]
    S = T // n_seq
    tq, tk, rg = min(Q_TILE, S), min(K_CHUNK, S), ROW_GROUP
    nq, per = S // tq, tq // tk

    def body(q_ref, k_ref, v_ref, o_ref, o32_ref, lse_ref, phi_s, plo_s, mp_s, m_s, acc_s):
        qi = pl.program_id(2)
        mp_s[...] = jnp.full_like(mp_s, NEG_INF)
        acc_s[...] = jnp.zeros_like(acc_s)

        def scores(kc, hh, r0):
            k0 = pl.multiple_of(kc * tk, tk)
            hl = slice(hh * LANES, (hh + 1) * LANES)
            return k0, lax.dot_general(q_ref[r0:, hl], k_ref[pl.ds(k0, tk), hl], _NT, preferred_element_type=F32)

        def max_chunk(kc, masked, r0):
            for hh in range(2):
                k0, s_all = scores(kc, hh, r0)
                for r in range(r0 // rg, tq // rg):
                    rows = slice(r * rg, (r + 1) * rg)
                    s = s_all[r * rg - r0:(r + 1) * rg - r0, :]
                    if masked:
                        s = jnp.where(_band_mask(qi * tq + r * rg, k0, rg, tk), s, NEG_INF)
                    part = s[:, :LANES]
                    for c in range(1, tk // LANES):
                        part = jnp.maximum(part, s[:, c * LANES:(c + 1) * LANES])
                    mp_s[hh, rows, :] = jnp.maximum(mp_s[hh, rows, :], part)

        def sum_chunk(kc, masked, r0):
            for hh in range(2):
                k0, s_all = scores(kc, hh, r0)
                hl = slice(hh * LANES, (hh + 1) * LANES)
                v = v_ref[pl.ds(k0, tk), hl]
                for r in range(r0 // rg, tq // rg):
                    rows = slice(r * rg, (r + 1) * rg)
                    p = jnp.exp(s_all[r * rg - r0:(r + 1) * rg - r0, :] - m_s[hh, rows])
                    if masked:
                        p = jnp.where(_band_mask(qi * tq + r * rg, k0, rg, tk), p, 0.0)
                    p_hi = p.astype(BF16)
                    phi_s[hh, rows, :] = p_hi
                    plo_s[hh, rows, :] = (p - p_hi.astype(F32)).astype(BF16)
                acc_s[hh, r0:, :] += (jnp.dot(phi_s[hh, r0:, :], v, preferred_element_type=F32)
                                      + jnp.dot(plo_s[hh, r0:, :], v, preferred_element_type=F32))

        def sweep(chunk):
            def unmasked(kc, carry):
                chunk(kc, False, 0)
                return carry

            lax.fori_loop(0, qi * per, unmasked, 0)
            for d in range(per):
                chunk(qi * per + d, True, d * tk)

        sweep(max_chunk)
        m_s[...] = jnp.max(mp_s[...], axis=-1, keepdims=True)
        sweep(sum_chunk)

        lane = lax.broadcasted_iota(jnp.int32, (1, LANES), 1)
        outs = []
        for hh in range(2):
            acc = acc_s[hh]
            l = jnp.sum(jnp.where(lane == AUG0, acc, 0.0), axis=-1, keepdims=True)
            lse_ref[hh] = m_s[hh] + jnp.log(l)
            outs.append(acc / l)
        o = jnp.where(lane < HEAD_DIM, outs[0], pltpu.roll(outs[1], HEAD_DIM, 1))
        o_ref[...] = o.astype(BF16)
        o32_ref[...] = o

    qmap = lambda b, j, qi: (b * nq + qi, j)
    omap = lambda b, j, qi: (b * nq + qi, j)
    kv = pl.BlockSpec((S, 2 * LANES), lambda b, j, qi: (b, j))
    return pl.pallas_call(
        body, name=name, grid=(n_seq, N_HEADS // 2, nq),
        in_specs=[pl.BlockSpec((tq, 2 * LANES), qmap), kv, kv],
        out_specs=[pl.BlockSpec((tq, LANES), omap), pl.BlockSpec((tq, LANES), omap),
                   pl.BlockSpec((2, tq, 1), lambda b, j, qi: (j, b * nq + qi, 0))],
        out_shape=[jax.ShapeDtypeStruct((T, BRANCH_W), BF16), jax.ShapeDtypeStruct((T, BRANCH_W), F32),
                   jax.ShapeDtypeStruct((N_HEADS, T, 1), F32)],
        scratch_shapes=[pltpu.VMEM((2, tq, tk), BF16), pltpu.VMEM((2, tq, tk), BF16),
                        pltpu.VMEM((2, tq, LANES), F32), pltpu.VMEM((2, tq, 1), F32),
                        pltpu.VMEM((2, tq, LANES), F32)],
        compiler_params=_cp(("parallel", "parallel", "parallel")),
    )(qa, ka, va)


def _attn_bwd(qa, ka, proj, do, o32, lse, dproj, n_seq, name):
    T = qa.shape[0]
    S = T // n_seq
    tq, tk, rg = min(Q_TILE, S), min(K_CHUNK, S), ROW_GROUP
    nq, per, nkc = S // tq, tq // tk, S // tk

    def body(q_ref, k_ref, v_ref, do_ref, o_ref, lse_ref, _, dqkv_ref, dfk_ref,
             p_s, ds_s, dq_s, dk_s, dv_s, df_s):
        dk_s[...] = jnp.zeros_like(dk_s)
        dv_s[...] = jnp.zeros_like(dv_s)
        df_s[...] = jnp.zeros_like(df_s)
        sels = _pair_masks()

        for qi in range(nq):
            q0 = qi * tq
            do_t = do_ref[q0:q0 + tq, :]
            dq_s[...] = jnp.zeros_like(dq_s)
            prod = do_t.astype(F32) * o_ref[q0:q0 + tq, :]
            dls = [jnp.sum(jnp.where(sel, prod, 0.0), axis=-1, keepdims=True) for sel in sels]

            def chunk(kc, masked, r0, q0=q0, do_t=do_t, dls=dls):
                k0 = pl.multiple_of(kc * tk, tk)
                v = v_ref[pl.ds(k0, tk), :]
                do_a = do_t[r0:, :]
                for hh in range(2):
                    hl = slice(hh * LANES, (hh + 1) * LANES)
                    qh, kh = q_ref[q0 + r0:q0 + tq, hl], k_ref[pl.ds(k0, tk), hl]
                    s_all = lax.dot_general(qh, kh, _NT, preferred_element_type=F32)
                    dom = jnp.where(sels[hh], do_a, jnp.zeros_like(do_a))
                    dp_all = lax.dot_general(dom, v, _NT, preferred_element_type=F32)
                    dfp = jnp.zeros((1, tk), F32)
                    for r in range(r0 // rg, tq // rg):
                        rows = slice(r * rg, (r + 1) * rg)
                        arows = slice(r * rg - r0, (r + 1) * rg - r0)
                        qrows = slice(q0 + r * rg, q0 + (r + 1) * rg)
                        p = jnp.exp(s_all[arows, :] - lse_ref[hh, qrows])
                        if masked:
                            p = jnp.where(_band_mask(q0 + r * rg, k0, rg, tk), p, 0.0)
                        ds = p * (dp_all[arows, :] - dls[hh][rows])
                        p_s[hh, rows, :] = p.astype(BF16)
                        ds_s[hh, rows, :] = ds.astype(BF16)
                        dfp = dfp + jnp.sum(ds, axis=0, keepdims=True)
                    df_s[hh, kc] -= dfp
                    dq_s[hh, r0:, :] += jnp.dot(ds_s[hh, r0:, :], kh, preferred_element_type=F32)
                    dv_s[hh, pl.ds(k0, tk), :] += lax.dot_general(p_s[hh, r0:, :], do_a, _TN,
                                                                  preferred_element_type=F32)
                    dk_s[hh, pl.ds(k0, tk), :] += lax.dot_general(ds_s[hh, r0:, :], qh, _TN,
                                                                  preferred_element_type=F32)

            def unmasked(kc, carry, chunk=chunk):
                chunk(kc, False, 0)
                return carry

            lax.fori_loop(0, qi * per, unmasked, 0)
            for d in range(per):
                chunk(qi * per + d, True, d * tk)
            dq = jnp.where(sels[0], dq_s[0], pltpu.roll(dq_s[1], HEAD_DIM, 1))
            dqkv_ref[q0:q0 + tq, :LANES] = (dq * (HEAD_DIM ** -0.5)).astype(BF16)

        dqkv_ref[:, LANES:2 * LANES] = jnp.where(sels[0], dk_s[0], pltpu.roll(dk_s[1], HEAD_DIM, 1)).astype(BF16)
        dqkv_ref[:, 2 * LANES:] = jnp.where(sels[0], dv_s[0], dv_s[1]).astype(BF16)
        for c in range(nkc):
            dfk_ref[:, :, c * tk:(c + 1) * tk] = df_s[:, c]

    seq = lambda w: pl.BlockSpec((S, w), lambda b, j: (b, j))
    col1 = pl.BlockSpec((2, S, 1), lambda b, j: (j, b, 0))
    vblk = pl.BlockSpec((S, LANES), lambda b, j: (b, OFF_QKV // LANES + 3 * j + 2))
    return pl.pallas_call(
        body, name=name, grid=(n_seq, N_HEADS // 2),
        in_specs=[seq(2 * LANES), seq(2 * LANES), vblk, seq(LANES), seq(LANES), col1,
                  pl.BlockSpec(memory_space=pl.ANY)],
        out_specs=[pl.BlockSpec((S, TRIPLE), lambda b, j: (b, OFF_QKV // TRIPLE + j)),
                   pl.BlockSpec((2, 1, S), lambda b, j: (j, 0, b))],
        out_shape=[jax.ShapeDtypeStruct(dproj.shape, BF16), jax.ShapeDtypeStruct((N_HEADS, 1, T), F32)],
        input_output_aliases={6: 0},
        scratch_shapes=[pltpu.VMEM((2, tq, tk), BF16), pltpu.VMEM((2, tq, tk), BF16),
                        pltpu.VMEM((2, tq, LANES), F32), pltpu.VMEM((2, S, LANES), F32),
                        pltpu.VMEM((2, S, LANES), F32), pltpu.VMEM((2, nkc, 1, tk), F32)],
        compiler_params=_cp(("parallel", "parallel")),
    )(qa, ka, proj, do, o32, lse, dproj)


def _shift_down(v, k, row):
    return jnp.where(row >= k, pltpu.roll(v, k, 0), 0.0)


def _shift_up(v, k, row, S):
    return jnp.where(row < S - k, pltpu.roll(v, S - k, 0), 0.0)


def _pool_diff(uf, w, row):
    acc, k = uf, 1
    while k < w:
        acc = acc + _shift_down(acc, k, row)
        k *= 2
    n = jnp.minimum(row + 1, w).astype(F32)
    return acc / n - uf


def _pool_fwd(proj, pool_w, pool_scale, n_seq, name):
    T = proj.shape[0]
    S = T // n_seq

    def body(u_ref, w_ref, sc_ref, o_ref, d_s):
        g = pl.program_id(1)
        row = lax.broadcasted_iota(jnp.int32, (S, POOL_GD), 0)
        uf = u_ref[...].astype(F32)
        for gi, wlen in enumerate(POOL_WINDOWS):
            @pl.when(g == gi)
            def _(wlen=wlen):
                d_s[...] = _pool_diff(uf, wlen, row).astype(BF16)
        e = jnp.dot(d_s[...], w_ref[0], preferred_element_type=F32)
        o_ref[...] = (e * sc_ref[...]).astype(BF16)

    uc = OFF_U // POOL_GD
    return pl.pallas_call(
        body, name=name, grid=(n_seq, len(POOL_WINDOWS)),
        in_specs=[pl.BlockSpec((S, POOL_GD), lambda b, g: (b, uc + g)),
                  pl.BlockSpec((1, POOL_GD, POOL_GD), lambda b, g: (g, 0, 0)),
                  pl.BlockSpec((1, POOL_GD), lambda b, g: (0, g))],
        out_specs=pl.BlockSpec((S, POOL_GD), lambda b, g: (b, g)),
        out_shape=jax.ShapeDtypeStruct((T, BRANCH_W), BF16),
        scratch_shapes=[pltpu.VMEM((S, POOL_GD), BF16)],
        compiler_params=_cp(("parallel", "parallel")),
    )(proj, pool_w, pool_scale)


def _pool_bwd(proj, dout, pool_w, pool_scale, dproj, n_seq, name):
    T = proj.shape[0]
    S = T // n_seq

    def body(u_ref, do_ref, w_ref, sc_ref, _, du_ref, dw_ref, dsc_ref, d_s):
        g, b = pl.program_id(0), pl.program_id(1)
        row = lax.broadcasted_iota(jnp.int32, (S, POOL_GD), 0)
        uf = u_ref[...].astype(F32)
        for gi, wlen in enumerate(POOL_WINDOWS):
            @pl.when(g == gi)
            def _(wlen=wlen):
                d_s[...] = _pool_diff(uf, wlen, row).astype(BF16)
        db16 = d_s[...]
        w = w_ref[0]
        e = jnp.dot(db16, w, preferred_element_type=F32)
        dof = do_ref[...].astype(F32)
        dsc = jnp.sum(dof * e, axis=0, keepdims=True)
        de = (dof * sc_ref[...]).astype(BF16)
        dd = lax.dot_general(de, w, (((1,), (1,)), ((), ())), preferred_element_type=F32)
        dw = lax.dot_general(db16, de, (((0,), (0,)), ((), ())), preferred_element_type=F32)
        for gi, wlen in enumerate(POOL_WINDOWS):
            @pl.when(g == gi)
            def _(wlen=wlen):
                n = jnp.minimum(row + 1, wlen).astype(F32)
                acc, k = dd / n, 1
                while k < wlen:
                    acc = acc + _shift_up(acc, k, row, S)
                    k *= 2
                du_ref[...] = (acc - dd).astype(BF16)

        @pl.when(b == 0)
        def _():
            dw_ref[0] = dw
            dsc_ref[...] = dsc

        @pl.when(b > 0)
        def _():
            dw_ref[0] += dw
            dsc_ref[...] += dsc

    uc = OFF_U // POOL_GD
    return pl.pallas_call(
        body, name=name, grid=(len(POOL_WINDOWS), n_seq),
        in_specs=[pl.BlockSpec((S, POOL_GD), lambda g, b: (b, uc + g)),
                  pl.BlockSpec((S, POOL_GD), lambda g, b: (b, g)),
                  pl.BlockSpec((1, POOL_GD, POOL_GD), lambda g, b: (g, 0, 0)),
                  pl.BlockSpec((1, POOL_GD), lambda g, b: (0, g)),
                  pl.BlockSpec(memory_space=pl.ANY)],
        out_specs=[pl.BlockSpec((S, POOL_GD), lambda g, b: (b, uc + g)),
                   pl.BlockSpec((1, POOL_GD, POOL_GD), lambda g, b: (g, 0, 0)),
                   pl.BlockSpec((1, POOL_GD), lambda g, b: (0, g))],
        out_shape=[jax.ShapeDtypeStruct(dproj.shape, BF16),
                   jax.ShapeDtypeStruct((len(POOL_WINDOWS), POOL_GD, POOL_GD), F32),
                   jax.ShapeDtypeStruct((1, BRANCH_W), F32)],
        input_output_aliases={4: 0},
        scratch_shapes=[pltpu.VMEM((S, POOL_GD), BF16)],
        compiler_params=_cp(("parallel", "arbitrary")),
    )(proj, dout, pool_w, pool_scale, dproj)


def _conv_fwd(proj, conv_w, n_seq, name):
    T = proj.shape[0]
    S = T // n_seq
    nc = BRANCH_W // LANES

    def body(c_ref, w_ref, o_ref):
        row = lax.broadcasted_iota(jnp.int32, (S, LANES), 0)
        cv, cb, cc = (c_ref[:, t * LANES:(t + 1) * LANES].astype(F32) for t in range(3))
        z = cc * cv
        w = w_ref[...]
        y = w[0:1] * _shift_down(z, 2, row) + w[1:2] * _shift_down(z, 1, row) + w[2:3] * z
        o_ref[...] = (cb * y).astype(BF16)

    return pl.pallas_call(
        body, name=name, grid=(n_seq, nc),
        in_specs=[pl.BlockSpec((S, TRIPLE), lambda b, j: (b, OFF_CONV // TRIPLE + j)),
                  pl.BlockSpec((CONV_K, LANES), lambda b, j: (0, j))],
        out_specs=pl.BlockSpec((S, LANES), lambda b, j: (b, j)),
        out_shape=jax.ShapeDtypeStruct((T, BRANCH_W), BF16),
        compiler_params=_cp(("parallel", "parallel")),
    )(proj, conv_w)


def _conv_bwd(proj, dout, conv_w, dproj, n_seq, name):
    T = proj.shape[0]
    S = T // n_seq
    nc = BRANCH_W // LANES

    def body(c_ref, do_ref, w_ref, _, dc_ref, dw_ref):
        b = pl.program_id(1)
        row = lax.broadcasted_iota(jnp.int32, (S, LANES), 0)
        cv, cb, cc = (c_ref[:, t * LANES:(t + 1) * LANES].astype(F32) for t in range(3))
        dof = do_ref[...].astype(F32)
        w = w_ref[...]
        z = cc * cv
        z1, z2 = _shift_down(z, 1, row), _shift_down(z, 2, row)
        y = w[0:1] * z2 + w[1:2] * z1 + w[2:3] * z
        dy = dof * cb
        dz = w[2:3] * dy + w[1:2] * _shift_up(dy, 1, row, S) + w[0:1] * _shift_up(dy, 2, row, S)
        dc_ref[:, :LANES] = (dz * cc).astype(BF16)
        dc_ref[:, LANES:2 * LANES] = (dof * y).astype(BF16)
        dc_ref[:, 2 * LANES:] = (dz * cv).astype(BF16)
        dws = [jnp.sum(dy * zk, axis=0, keepdims=True) for zk in (z2, z1, z)]

        @pl.when(b == 0)
        def _():
            for kk in range(CONV_K):
                dw_ref[kk:kk + 1, :] = dws[kk]

        @pl.when(b > 0)
        def _():
            for kk in range(CONV_K):
                dw_ref[kk:kk + 1, :] += dws[kk]

    triple = pl.BlockSpec((S, TRIPLE), lambda j, b: (b, OFF_CONV // TRIPLE + j))
    wsp = pl.BlockSpec((CONV_K, LANES), lambda j, b: (0, j))
    return pl.pallas_call(
        body, name=name, grid=(nc, n_seq),
        in_specs=[triple, pl.BlockSpec((S, LANES), lambda j, b: (b, j)), wsp, pl.BlockSpec(memory_space=pl.ANY)],
        out_specs=[triple, wsp],
        out_shape=[jax.ShapeDtypeStruct(dproj.shape, BF16), jax.ShapeDtypeStruct((CONV_K, BRANCH_W), F32)],
        input_output_aliases={3: 0},
        compiler_params=_cp(("parallel", "arbitrary")),
    )(proj, dout, conv_w, dproj)


def _mix_fwd(oa, ob, oc, wpa, wpp, wpc, proj, b_gate, name):
    T = oa.shape[0]
    tm = min(512, T)

    def body(oa_ref, ob_ref, oc_ref, wa_ref, wp_ref, wc_ref, g_ref, bg_ref, o_ref):
        acc = jnp.zeros((tm, D_MODEL), F32)
        for i, (x_ref, w_ref) in enumerate(((oa_ref, wa_ref), (ob_ref, wp_ref), (oc_ref, wc_ref))):
            y = jnp.dot(x_ref[...], w_ref[...], preferred_element_type=F32)
            sl = slice(i * D_MODEL, (i + 1) * D_MODEL)
            acc = acc + _sigmoid(g_ref[:, sl].astype(F32) + bg_ref[:, sl]) * y
        o_ref[...] = acc.astype(BF16)

    br = pl.BlockSpec((tm, BRANCH_W), lambda i: (i, 0))
    wsp = pl.BlockSpec((BRANCH_W, D_MODEL), lambda i: (0, 0))
    return pl.pallas_call(
        body, name=name, grid=(T // tm,),
        in_specs=[br, br, br, wsp, wsp, wsp, pl.BlockSpec((tm, GATE_W), lambda i: (i, 0)),
                  pl.BlockSpec((1, GATE_W), lambda i: (0, 0))],
        out_specs=pl.BlockSpec((tm, D_MODEL), lambda i: (i, 0)),
        out_shape=jax.ShapeDtypeStruct((T, D_MODEL), BF16),
        compiler_params=_cp(("parallel",)),
    )(oa, ob, oc, wpa, wpp, wpc, proj, b_gate)


def _mix_bwd(oa, ob, oc, wpa, wpp, wpc, proj, b_gate, dmixed, name):
    T = oa.shape[0]
    tm = min(256, T)

    def body(oa_ref, ob_ref, oc_ref, wa_ref, wp_ref, wc_ref, g_ref, bg_ref, dm_ref,
             dya_ref, dyb_ref, dyc_ref, dg_ref, dbg_ref):
        i0 = pl.program_id(0)
        dm = dm_ref[...].astype(F32)
        parts = []
        for i, (x_ref, w_ref, dy_ref) in enumerate(((oa_ref, wa_ref, dya_ref), (ob_ref, wp_ref, dyb_ref),
                                                    (oc_ref, wc_ref, dyc_ref))):
            y = jnp.dot(x_ref[...], w_ref[...], preferred_element_type=F32)
            sl = slice(i * D_MODEL, (i + 1) * D_MODEL)
            gate = _sigmoid(g_ref[:, sl].astype(F32) + bg_ref[:, sl])
            dy_ref[...] = (dm * gate).astype(BF16)
            dgl = dm * y * gate * (1.0 - gate)
            dg_ref[:, sl] = dgl.astype(BF16)
            parts.append(jnp.sum(dgl, axis=0, keepdims=True))

        @pl.when(i0 == 0)
        def _():
            for i in range(3):
                dbg_ref[:, i * D_MODEL:(i + 1) * D_MODEL] = parts[i]

        @pl.when(i0 > 0)
        def _():
            for i in range(3):
                dbg_ref[:, i * D_MODEL:(i + 1) * D_MODEL] += parts[i]

    br = pl.BlockSpec((tm, BRANCH_W), lambda i: (i, 0))
    wsp = pl.BlockSpec((BRANCH_W, D_MODEL), lambda i: (0, 0))
    row = pl.BlockSpec((tm, D_MODEL), lambda i: (i, 0))
    gsp = pl.BlockSpec((tm, GATE_W), lambda i: (i, 0))
    bsp = pl.BlockSpec((1, GATE_W), lambda i: (0, 0))
    act = jax.ShapeDtypeStruct((T, D_MODEL), BF16)
    return pl.pallas_call(
        body, name=name, grid=(T // tm,),
        in_specs=[br, br, br, wsp, wsp, wsp, gsp, bsp, row],
        out_specs=[row, row, row, gsp, bsp],
        out_shape=[act, act, act, jax.ShapeDtypeStruct((T, MAIN_COLS), BF16),
                   jax.ShapeDtypeStruct((1, GATE_W), F32)],
        compiler_params=_cp(("arbitrary",)),
    )(oa, ob, oc, wpa, wpp, wpc, proj, b_gate, dmixed)


GU_TILE = 256


def _gu_col(c):
    t, r = divmod(c, GU_TILE)
    return (t // 2) * GU_TILE + r + (FFN_HIDDEN if t % 2 else 0)


def _gate_up_swiglu(h, w, name):
    T, K = h.shape
    tm = min(2048, T)

    def body(h_ref, w_ref, ab_ref, s_ref):
        prod = jnp.dot(h_ref[...], w_ref[...], preferred_element_type=F32)
        ab_ref[...] = prod.astype(BF16)
        a = prod[:, :GU_TILE]
        s_ref[...] = (a * _sigmoid(a) * prod[:, GU_TILE:]).astype(BF16)

    return pl.pallas_call(
        body, name=name, grid=(T // tm, FFN_HIDDEN // GU_TILE),
        in_specs=[pl.BlockSpec((tm, K), lambda i, j: (i, 0)), pl.BlockSpec((K, 2 * GU_TILE), lambda i, j: (0, j))],
        out_specs=[pl.BlockSpec((tm, 2 * GU_TILE), lambda i, j: (i, j)), pl.BlockSpec((tm, GU_TILE), lambda i, j: (i, j))],
        out_shape=[jax.ShapeDtypeStruct((T, 2 * FFN_HIDDEN), BF16), jax.ShapeDtypeStruct((T, FFN_HIDDEN), BF16)],
        compiler_params=_cp(("parallel", "parallel")),
    )(h, w)


def _swiglu_bwd_fused(dx, w_down, ab, name):
    T, K = dx.shape
    tm = min(2048, T)

    def body(dx_ref, w_ref, ab_ref, o_ref):
        ds = lax.dot_general(dx_ref[...], w_ref[...], _NT, preferred_element_type=F32)
        a = ab_ref[:, :GU_TILE].astype(F32)
        b = ab_ref[:, GU_TILE:].astype(F32)
        sg = _sigmoid(a)
        o_ref[:, :GU_TILE] = (ds * b * sg * (1.0 + a * (1.0 - sg))).astype(BF16)
        o_ref[:, GU_TILE:] = (ds * a * sg).astype(BF16)

    pair = pl.BlockSpec((tm, 2 * GU_TILE), lambda i, j: (i, j))
    return pl.pallas_call(
        body, name=name, grid=(T // tm, FFN_HIDDEN // GU_TILE),
        in_specs=[pl.BlockSpec((tm, K), lambda i, j: (i, 0)), pl.BlockSpec((GU_TILE, K), lambda i, j: (j, 0)), pair],
        out_specs=pair, out_shape=jax.ShapeDtypeStruct((T, 2 * FFN_HIDDEN), BF16),
        compiler_params=_cp(("parallel", "parallel")),
    )(dx, w_down, ab)


def _adamw_update(w_ref, g_ref, m_ref, v_ref, d_ref, nm_ref, nv_ref):
    gv = g_ref[...]
    nm = ADAM_B1 * m_ref[...] + (1.0 - ADAM_B1) * gv
    nv = ADAM_B2 * v_ref[...] + (1.0 - ADAM_B2) * (gv * gv)
    m_hat = nm / (1.0 - ADAM_B1 ** ADAM_STEP)
    v_hat = nv / (1.0 - ADAM_B2 ** ADAM_STEP)
    d_ref[...] = -ADAM_LR * (m_hat / (jnp.sqrt(v_hat) + ADAM_EPS) + ADAM_WD * w_ref[...])
    nm_ref[...] = nm
    nv_ref[...] = nv


def _adamw_many(ws, gs, ms, vs, name):
    n = len(ws)

    def body(*refs):
        ins, outs = refs[:4 * n], refs[4 * n:]
        for t in range(n):
            _adamw_update(ins[t], ins[n + t], ins[2 * n + t], ins[3 * n + t], outs[t], outs[n + t], outs[2 * n + t])

    shapes = [jax.ShapeDtypeStruct(w.shape, F32) for w in ws]
    out = pl.pallas_call(body, name=name, out_shape=shapes * 3, compiler_params=_cp())(*ws, *gs, *ms, *vs)
    return out[:n], out[n:2 * n], out[2 * n:]


def _adamw(w, g, m, v, name):
    R, C = w.shape
    tr = R
    for cand in (256, 352, 128, 64, 8):
        if R > cand and R % cand == 0:
            tr = cand
            break

    def body(w_ref, g_ref, m_ref, v_ref, d_ref, nm_ref, nv_ref):
        _adamw_update(w_ref, g_ref, m_ref, v_ref, d_ref, nm_ref, nv_ref)

    blk = pl.BlockSpec((tr, C), lambda i: (i, 0))
    sh = jax.ShapeDtypeStruct((R, C), F32)
    return pl.pallas_call(
        body, name=name, grid=(R // tr,), in_specs=[blk] * 4, out_specs=[blk] * 3, out_shape=[sh] * 3,
        compiler_params=_cp(("parallel",)),
    )(w, g, m, v)


def _adamw_3d(w, g, m, v, block, name):
    shape = w.shape
    grid = (shape[0] // block[0], shape[1] // block[1])
    assert shape[0] % block[0] == 0 and shape[1] % block[1] == 0 and block[2] == shape[2], (name, shape, block)

    def body(w_ref, g_ref, m_ref, v_ref, d_ref, nm_ref, nv_ref):
        _adamw_update(w_ref, g_ref, m_ref, v_ref, d_ref, nm_ref, nv_ref)

    blk = pl.BlockSpec(block, lambda i, j: (i, j, 0))
    sh = jax.ShapeDtypeStruct(shape, F32)
    return pl.pallas_call(
        body, name=name, grid=grid, in_specs=[blk] * 4, out_specs=[blk] * 3, out_shape=[sh] * 3,
        compiler_params=_cp(("parallel", "parallel")),
    )(w, g, m, v)


def _sum_slabs(x, name):
    n, R, C = x.shape
    tr = R
    for cand in (512, 256, 128, 64, 32, 16, 8):
        if R > cand and R % cand == 0:
            tr = cand
            break

    def body(x_ref, o_ref):
        acc = x_ref[0].astype(F32)
        for j in range(1, n):
            acc = acc + x_ref[j].astype(F32)
        o_ref[...] = acc

    return pl.pallas_call(
        body, name=name, grid=(R // tr,), in_specs=[pl.BlockSpec((n, tr, C), lambda i: (0, i, 0))],
        out_specs=pl.BlockSpec((tr, C), lambda i: (i, 0)), out_shape=jax.ShapeDtypeStruct((R, C), F32),
        compiler_params=_cp(("parallel",)),
    )(x)


def _multi_gather(xs, layers, name):
    nt = len(xs)
    shapes = [x.shape if lay is None else x.shape[1:] for x, lay in zip(xs, layers)]

    def body(*refs):
        x_refs, out_refs = refs[:nt], refs[nt:2 * nt]
        send_sems, recv_sems, local_sems = refs[2 * nt:]
        x_, y_, c_ = lax.axis_index("x"), lax.axis_index("y"), lax.axis_index("c")
        me, sibling = (x_, y_, c_), (x_, y_, 1 - c_)
        chips = [(1 - x_, y_), (x_, 1 - y_), (1 - x_, 1 - y_)]

        def own_block(t):
            return x_refs[t] if layers[t] is None else x_refs[t].at[layers[t]]

        def copy(t, k, block, to, own=False):
            px, py, pc = block
            dst = out_refs[t].at[4 * px + 2 * py + pc]
            return pltpu.make_async_remote_copy(
                src_ref=own_block(t) if own else dst, dst_ref=dst,
                send_sem=send_sems.at[t, k], recv_sem=recv_sems.at[t, k],
                device_id=to, device_id_type=pl.DeviceIdType.MESH)

        mine, first, passed = [], [], []
        for t in range(nt):
            mine.append(pltpu.make_async_copy(own_block(t), out_refs[t].at[4 * x_ + 2 * y_ + c_], local_sems.at[t]))
            mine[-1].start()
            first.append([copy(t, 1 + j, me, (*chip, c_), own=True) for j, chip in enumerate(chips)]
                         + [copy(t, 0, me, sibling, own=True)])
            for cp in first[-1]:
                cp.start()
        for t in range(nt):
            for j, chip in enumerate(chips):
                copy(t, 1 + j, (*chip, c_), me).wait_recv()
                passed.append(copy(t, 4 + j, (*chip, c_), sibling))
                passed[-1].start()
        for t in range(nt):
            copy(t, 0, sibling, me).wait_recv()
            for j, chip in enumerate(chips):
                copy(t, 4 + j, (*chip, 1 - c_), me).wait_recv()
        for cp in [c for f in first for c in f] + passed:
            cp.wait_send()
        for cp in mine:
            cp.wait()

    hbm = pl.BlockSpec(memory_space=pl.ANY)
    return pl.pallas_call(
        body, name=name, out_shape=[jax.ShapeDtypeStruct((N_DEV,) + tuple(s), x.dtype) for s, x in zip(shapes, xs)],
        in_specs=[hbm] * nt, out_specs=[hbm] * nt,
        scratch_shapes=[pltpu.SemaphoreType.DMA((nt, 7)), pltpu.SemaphoreType.DMA((nt, 7)),
                        pltpu.SemaphoreType.DMA((nt,))],
    )(*xs)


_HBM = pl.BlockSpec(memory_space=pltpu.HBM)
_SEM = pl.BlockSpec(memory_space=pltpu.SEMAPHORE)
_PEER_ORDER = (2, 4, 6, 3, 5, 7, 1)


def _split_copies(src_refs, land_refs, send_sems, recv_sems, layers, per_peer):
    x_, y_, c_ = lax.axis_index("x"), lax.axis_index("y"), lax.axis_index("c")
    me = 4 * x_ + 2 * y_ + c_
    copies = []
    for k in _PEER_ORDER:
        px, py, pc = x_ ^ ((k >> 2) & 1), y_ ^ ((k >> 1) & 1), c_ ^ (k & 1)
        peer = 4 * px + 2 * py + pc
        for t in range(len(src_refs)):
            if per_peer:
                src = src_refs[t].at[peer]
            else:
                src = src_refs[t] if layers[t] is None else src_refs[t].at[layers[t]]
            copies.append(pltpu.make_async_remote_copy(
                src_ref=src, dst_ref=land_refs[t].at[me],
                send_sem=send_sems.at[t * (N_DEV - 1) + k - 1], recv_sem=recv_sems.at[t * (N_DEV - 1) + k - 1],
                device_id=(px, py, pc), device_id_type=pl.DeviceIdType.MESH))
    return copies


def _own_copies(src_refs, land_refs, sems, layers, per_peer):
    nt = len(src_refs)
    me = 4 * lax.axis_index("x") + 2 * lax.axis_index("y") + lax.axis_index("c")
    copies = []
    for t in range(nt):
        if per_peer:
            src = src_refs[t].at[me]
        else:
            src = src_refs[t] if layers[t] is None else src_refs[t].at[layers[t]]
        copies.append(pltpu.make_async_copy(src, land_refs[t].at[me], sems.at[nt * (N_DEV - 1) + t]))
    return copies


def _split_start(srcs, layers, per_peer, after, name):
    nt = len(srcs)
    if per_peer:
        land_shapes = [s.shape for s in srcs]
    else:
        land_shapes = [(N_DEV,) + tuple(s.shape if lay is None else s.shape[1:]) for s, lay in zip(srcs, layers)]

    def body(*refs):
        src_refs, land_refs = refs[:nt], refs[nt:2 * nt]
        send_sems, recv_sems = refs[2 * nt + 1], refs[2 * nt + 2]
        token = refs[-1]
        for cp in _split_copies(src_refs, land_refs, send_sems, recv_sems, layers, per_peer):
            cp.start()
        for cp in _own_copies(src_refs, land_refs, send_sems, layers, per_peer):
            cp.start()
        token[...] = jnp.zeros_like(token)

    lands = [pltpu.with_memory_space_constraint(lax.empty(s, x.dtype), pltpu.HBM) for s, x in zip(land_shapes, srcs)]
    srcs = [pltpu.with_memory_space_constraint(x, pltpu.HBM) for x in srcs]
    out = pl.pallas_call(
        body, name=name,
        out_shape=(pltpu.SemaphoreType.DMA((nt * N_DEV,)), pltpu.SemaphoreType.DMA((nt * (N_DEV - 1),)),
                   *[pltpu.HBM(x.shape, x.dtype) for x in srcs], *[pltpu.HBM(s, x.dtype) for s, x in zip(land_shapes, srcs)],
                   jax.ShapeDtypeStruct((8, LANES), F32)),
        in_specs=[_HBM] * (2 * nt) + [pl.BlockSpec(memory_space=pl.ANY)],
        out_specs=(_SEM, _SEM, *([_HBM] * (2 * nt)), pl.BlockSpec(memory_space=pltpu.VMEM)),
        input_output_aliases={i: 2 + i for i in range(2 * nt)},
        compiler_params=pltpu.CompilerParams(has_side_effects=pltpu.SideEffectType.DATAFLOW_SIDE_EFFECTING),
    )(*srcs, *lands, after)
    return out[0], out[1], list(out[2:2 + nt]), list(out[2 + nt:2 + 2 * nt]), out[-1]


def _split_wait(started, layers, per_peer, after, name):
    send_sems, recv_sems, srcs, lands, _ = started
    nt = len(srcs)

    def body(*refs):
        src_refs, land_refs = refs[:nt], refs[nt:2 * nt]
        s_sems, r_sems = refs[2 * nt], refs[2 * nt + 1]
        for cp in _split_copies(src_refs, land_refs, s_sems, r_sems, layers, per_peer):
            cp.wait_send()
            cp.wait_recv()
        for cp in _own_copies(src_refs, land_refs, s_sems, layers, per_peer):
            cp.wait()

    out = pl.pallas_call(
        body, name=name,
        out_shape=tuple(pltpu.HBM(x.shape, x.dtype) for x in srcs + lands),
        in_specs=[_HBM] * (2 * nt) + [_SEM, _SEM, pl.BlockSpec(memory_space=pl.ANY)],
        out_specs=tuple([_HBM] * (2 * nt)),
        input_output_aliases={i: i for i in range(2 * nt)},
        compiler_params=pltpu.CompilerParams(has_side_effects=pltpu.SideEffectType.DATAFLOW_SIDE_EFFECTING),
    )(*srcs, *lands, send_sems, recv_sems, after)
    return list(out[nt:])


def _runs(mapping):
    runs, c, n = [], 0, len(mapping)
    while c < n:
        if mapping[c] is None:
            c += 1
            continue
        sid, d, lo = mapping[c][0], mapping[c][1] - c, c
        while c < n and mapping[c] is not None and mapping[c][0] == sid and mapping[c][1] - c == d:
            c += 1
        runs.append((lo, c, sid, d))
    return runs


def _tile_plan(mapping, src_widths):
    runs = _runs(mapping)
    plan = []
    for t in range(len(mapping) // LANES):
        pieces = []
        for lo, hi, sid, d in runs:
            lo_t, hi_t = max(lo, t * LANES), min(hi, (t + 1) * LANES)
            if lo_t >= hi_t:
                continue
            a = ((lo_t + d) // LANES) * LANES
            win = min(2 * LANES, src_widths[sid] - a)
            shift = t * LANES + d - a
            pieces.append((sid, a, win, shift, lo_t - t * LANES, hi_t - t * LANES))
        plan.append(pieces)
    return plan


def _reblock(srcs, src_views, outs, out_views, name):
    R = srcs[0].shape[-2]
    tr = min(512, R)
    widths = {sid: srcs[ai].shape[-1] for sid, (ai, _) in src_views.items()}
    plans = [(ai, li, _tile_plan(mapping, widths)) for ai, li, mapping in out_views]
    ns = len(srcs)

    def body(*refs):
        s_refs, o_refs = refs[:ns], refs[ns:]
        cache = {}

        def shift_matrix(win, shift, lo, hi):
            key = (win, shift, lo, hi)
            if key not in cache:
                r = lax.broadcasted_iota(jnp.int32, (win, LANES), 0)
                c = lax.broadcasted_iota(jnp.int32, (win, LANES), 1)
                hit = jnp.logical_and(r - c == shift, jnp.logical_and(c >= lo, c < hi))
                cache[key] = jnp.where(hit, 1.0, 0.0).astype(BF16)
            return cache[key]

        for ai, li, plan in plans:
            for t, pieces in enumerate(plan):
                acc = None
                whole = len(pieces) == 1 and pieces[0][3:] == (0, 0, LANES)
                for sid, a, win, shift, lo, hi in pieces:
                    sa, sl = src_views[sid]
                    if whole:
                        win = LANES
                    src = s_refs[sa][:, a:a + win] if sl is None else s_refs[sa][sl, :, a:a + win]
                    if whole:
                        acc = src
                    else:
                        part = jnp.dot(src, shift_matrix(win, shift, lo, hi), preferred_element_type=F32)
                        acc = part if acc is None else acc + part
                val = jnp.zeros((tr, LANES), BF16) if acc is None else acc.astype(BF16)
                if li is None:
                    o_refs[ai][:, t * LANES:(t + 1) * LANES] = val
                else:
                    o_refs[ai][li, :, t * LANES:(t + 1) * LANES] = val

    def spec(shape):
        if len(shape) == 2:
            return pl.BlockSpec((tr, shape[1]), lambda i: (i, 0))
        return pl.BlockSpec((shape[0], tr, shape[2]), lambda i: (0, i, 0))

    return pl.pallas_call(
        body, name=name, grid=(R // tr,), in_specs=[spec(s.shape) for s in srcs],
        out_specs=[spec(s) for s in outs], out_shape=[jax.ShapeDtypeStruct(s, BF16) for s in outs],
        compiler_params=_cp(("parallel",)),
    )(*srcs)


SHARDED = ("w_in", "w_gate_up", "w_proj_attn", "w_proj_pool", "w_proj_conv", "w_out", "w_down")
WEIGHT_ORDER = ("attn_norm", "w_in", "b_forget", "b_gate", "w_proj_attn", "pool_w", "pool_scale", "w_proj_pool",
                "conv_w", "w_proj_conv", "w_out", "ffn_norm", "w_gate_up", "w_down", "final_norm")
IN_SHARD, IN_SHARD_PAD = IN_COLS // N_DEV, 896
GU_SHARD, GU_SHARD_PAD = 2 * FFN_HIDDEN // N_DEV, 768


def _w_in_col(c):
    if c < OFF_QKV:
        return c + 3592
    if c < OFF_U:
        base, off = (0, OFF_QKV) if c < OFF_CONV else (2056, OFF_CONV)
        j, t = divmod(c - off, TRIPLE)
        which, e = divmod(t, LANES)
        return base + which * BRANCH_W + j * LANES + e
    return c - OFF_U + 1544


def _w_in_full(gathered, name):
    main = [divmod(_w_in_col(c), IN_SHARD) for c in range(MAIN_COLS)]
    fcols = [divmod(1536 + c, IN_SHARD) if c < N_HEADS else None for c in range(LANES)]
    R = gathered.shape[1]
    return _reblock([gathered], {i: (0, i) for i in range(N_DEV)}, [(R, MAIN_COLS), (R, LANES)],
                    [(0, None, main), (1, None, fcols)], name)


def _w_in_slabs(dmain, dwf, name):
    inv = {_w_in_col(c): ("m", c) for c in range(MAIN_COLS)}
    inv.update({1536 + c: ("f", c) for c in range(N_HEADS)})
    views = []
    for i in range(N_DEV):
        mapping = [inv[IN_SHARD * i + j] if j < IN_SHARD else None for j in range(IN_SHARD_PAD)]
        views.append((0, i, mapping))
    R = dmain.shape[0]
    return _reblock([dmain, dwf], {"m": (0, None), "f": (1, None)}, [(N_DEV, R, IN_SHARD_PAD)], views, name)[0]


def _w_gu_full(gathered, name):
    mapping = [divmod(_gu_col(c), GU_SHARD) for c in range(2 * FFN_HIDDEN)]
    R = gathered.shape[1]
    return _reblock([gathered], {i: (0, i) for i in range(N_DEV)}, [(R, 2 * FFN_HIDDEN)], [(0, None, mapping)], name)[0]


def _w_gu_slabs(dw, name):
    inv = {_gu_col(c): c for c in range(2 * FFN_HIDDEN)}
    views = [(0, i, [("w", inv[GU_SHARD * i + j]) if j < GU_SHARD else None for j in range(GU_SHARD_PAD)])
             for i in range(N_DEV)]
    R = dw.shape[0]
    return _reblock([dw], {"w": (0, None)}, [(N_DEV, R, GU_SHARD_PAD)], views, name)[0]


def _layer_fwd(x, W, n_seq, l, h1=None, next_norm=None):
    T = x.shape[0]
    sfx = f"_l{l}"
    if h1 is None:
        h1 = _rms_fwd(x, W["attn_norm"], "rms1" + sfx)
    proj, f = _matmul(h1, W["w_main"], mode="nn", out_dtype=BF16, name="proj_main" + sfx, side=(W["w_f"], F32))
    qa, ka, va = _fox_prep(f, W["b_forget"], proj, n_seq, "fox_prep" + sfx)
    oa, oa32, lse = _attn_fwd2(qa, ka, va, n_seq, "attn_fwd" + sfx)
    if "late" in W:
        W.update(W.pop("late")(oa))
    ob = _pool_fwd(proj, W["pool_w"], W["pool_scale"], n_seq, "pool_fwd" + sfx)
    oc = _conv_fwd(proj, W["conv_w"], n_seq, "conv_fwd" + sfx)
    mixed = _mix_fwd(oa, ob, oc, W["w_proj_attn"], W["w_proj_pool"], W["w_proj_conv"], proj, W["b_gate"],
                     "mix_fwd" + sfx)
    x2, h2 = _matmul(mixed, W["w_out"], mode="nn", out_dtype=F32, name="out_proj" + sfx, tm=1024, tn=1024,
                     residual=x, rms_g=W["ffn_norm"])
    ab, s = _gate_up_swiglu(h2, W["w_gate_up"], "gate_up" + sfx)
    x3 = _matmul(s, W["w_down"], mode="nn", out_dtype=F32, name="down" + sfx, tm=1024, tn=1024, tk=1408,
                 residual=x2, rms_g=next_norm)
    x3, h1_next = x3 if next_norm is not None else (x3, None)
    saved = dict(x=x, h1=h1, proj=proj, f=f, qa=qa, ka=ka, oa=oa, oa32=oa32, lse=lse, ob=ob, oc=oc, mixed=mixed, x2=x2,
                 h2=h2, ab=ab, s=s)
    return x3, saved, h1_next


def _layer_bwd(dx3, dx3b, W, sv, n_seq, l, stage=None):
    T = dx3.shape[0]
    sfx = f"_l{l}"
    G = {}
    stage = stage or (lambda l, group, G, W: W)
    dab = _swiglu_bwd_fused(dx3b, W["w_down"], sv["ab"], "d_ab" + sfx)
    G["w_down"] = _matmul(sv["s"], dx3b, mode="tn", out_dtype=BF16, name="dw_down" + sfx, tm=256, tn=1024)
    dh2 = _matmul(dab, W["w_gate_up"], mode="nt", out_dtype=BF16, name="d_h2" + sfx, tm=1024, tn=1024, tk=1408)
    G["w_gate_up"] = _matmul(sv["h2"], dab, mode="tn", out_dtype=BF16, name="dw_gate_up" + sfx, tm=1024)
    W = stage(l, "ffn", G, W)
    dx2, dx2b, G["ffn_norm"] = _rms_bwd(sv["x2"], W["ffn_norm"], dh2, dx3, "rms2_bwd" + sfx)
    dmixed = _matmul(dx2b, W["w_out"], mode="nt", out_dtype=BF16, name="d_mixed" + sfx)
    G["w_out"] = _matmul(sv["mixed"], dx2b, mode="tn", out_dtype=BF16, name="dw_out" + sfx, tm=1024)
    dya, dyb, dyc, dproj, G["b_gate"] = _mix_bwd(sv["oa"], sv["ob"], sv["oc"], W["w_proj_attn"], W["w_proj_pool"],
                                                 W["w_proj_conv"], sv["proj"], W["b_gate"], dmixed, "mix_bwd" + sfx)
    douts = {}
    for br, dy, o in (("attn", dya, sv["oa"]), ("pool", dyb, sv["ob"]), ("conv", dyc, sv["oc"])):
        douts[br] = _matmul(dy, W["w_proj_" + br], mode="nt", out_dtype=BF16, name=f"d_{br}_out" + sfx)
        G["w_proj_" + br] = _matmul(o, dy, mode="tn", out_dtype=BF16, name=f"dw_proj_{br}" + sfx, tm=512)
    W = stage(l, "mix", G, W)
    dproj, G["conv_w"] = _conv_bwd(sv["proj"], douts["conv"], W["conv_w"], dproj, n_seq, "conv_bwd" + sfx)
    dproj, G["pool_w"], G["pool_scale"] = _pool_bwd(sv["proj"], douts["pool"], W["pool_w"], W["pool_scale"], dproj,
                                                    n_seq, "pool_bwd" + sfx)
    dproj, dFk = _attn_bwd(sv["qa"], sv["ka"], sv["proj"], douts["attn"], sv["oa32"], sv["lse"], dproj, n_seq,
                           "attn_bwd" + sfx)
    dF = jnp.pad(dFk.reshape(N_HEADS, T).T, ((0, 0), (0, LANES - N_HEADS)))
    df, G["b_forget"] = _fox_cumsum_bwd(sv["f"], W["b_forget"], dF, n_seq, "fox_cumsum_bwd" + sfx)
    G["w_main"], G["w_f"] = _matmul(sv["h1"], dproj, mode="tn", out_dtype=BF16, name="dw_main" + sfx, tm=1024,
                                    side=(df, BF16))
    W = stage(l, "w_in", G, W)
    dh1 = _matmul(dproj, W["w_main"], mode="nt", out_dtype=BF16, name="d_h1_main" + sfx, tm=1024, tn=1024, tk=1664,
                  extra=(df, W["w_f"]))
    dx, dxb, G["attn_norm"] = _rms_bwd(sv["x"], W["attn_norm"], dh1, dx2, "rms1_bwd" + sfx)
    return dx, dxb, G


def _replicated_operands(rep, l):
    W = {}
    W["attn_norm"], W["ffn_norm"] = rep["attn_norm"][l], rep["ffn_norm"][l]
    W["b_forget"] = jnp.pad(rep["b_forget"][l].reshape(1, N_HEADS), ((0, 0), (0, LANES - N_HEADS)))
    W["b_gate"] = rep["b_gate"][l].reshape(1, GATE_W)
    W["pool_w"] = rep["pool_w"][l].astype(BF16)
    W["pool_scale"] = rep["pool_scale"][l].reshape(1, BRANCH_W)
    return W


def _local_step(x, target, get_W, attn_norms, final_norm, stage=None):
    n_seq, S, Dm = x.shape
    T = n_seq * S
    xt = x.reshape(T, Dm)
    saved, Ws, h1 = [], [], None
    for l in range(DEPTH):
        Ws.append(get_W(l, xt))
        next_norm = attn_norms[l + 1] if l + 1 < DEPTH else None
        xt, sv, h1 = _layer_fwd(xt, Ws[l], n_seq, l, h1, next_norm)
        saved.append(sv)
    loss, dx, dxb, g_final = _loss_head(xt, final_norm, target.reshape(T, Dm), "loss_head")
    grads = [None] * DEPTH
    for l in reversed(range(DEPTH)):
        dx, dxb, grads[l] = _layer_bwd(dx, dxb, Ws[l], saved[l], n_seq, l, stage)
    return loss, dx.reshape(n_seq, S, Dm), grads, g_final


def _padded_shards(weights):
    pads = {"w_in": IN_SHARD_PAD - IN_SHARD, "w_gate_up": GU_SHARD_PAD - GU_SHARD}
    return {n: jnp.pad(weights[n], ((0, 0), (0, 0), (0, pads.get(n, 0)))).astype(BF16) for n in SHARDED}


def _full_operands(g, l):
    W = {}
    if "w_in" in g:
        W["w_main"], W["w_f"] = _w_in_full(g["w_in"], f"w_in_full_l{l}")
    if "w_gate_up" in g:
        W["w_gate_up"] = _w_gu_full(g["w_gate_up"], f"w_gate_up_full_l{l}")
    for n in ("w_proj_attn", "w_proj_pool", "w_proj_conv"):
        if n in g:
            W[n] = jnp.transpose(g[n], (1, 0, 2)).reshape(BRANCH_W, D_MODEL)
    if "w_out" in g:
        W["w_out"] = g["w_out"].reshape(D_MODEL, D_MODEL)
    if "w_down" in g:
        W["w_down"] = g["w_down"].reshape(FFN_HIDDEN, D_MODEL)
    return W


GRAD_GROUPS = {"ffn": ("w_down", "w_gate_up"),
               "mix": ("w_out", "w_proj_attn", "w_proj_pool", "w_proj_conv"),
               "w_in": ("w_in",)}


def _grad_slabs(G, n, l):
    if n == "w_in":
        return _w_in_slabs(G["w_main"], G["w_f"], f"w_in_slabs_l{l}")
    if n == "w_gate_up":
        return _w_gu_slabs(G["w_gate_up"], f"w_gate_up_slabs_l{l}")
    if n == "w_out":
        return G["w_out"].reshape(N_DEV, D_MODEL // N_DEV, D_MODEL)
    if n == "w_down":
        return G["w_down"].reshape(N_DEV, FFN_HIDDEN // N_DEV, D_MODEL)
    return jnp.transpose(G[n].reshape(BRANCH_W, N_DEV, D_MODEL // N_DEV), (1, 0, 2))


def _sum_layer_grads(recv, l):
    out = {n: _sum_slabs(r, f"sum_{n}_l{l}") for n, r in recv.items()}
    if "w_in" in out:
        out["w_in"] = out["w_in"][:, :IN_SHARD]
    if "w_gate_up" in out:
        out["w_gate_up"] = out["w_gate_up"][:, :GU_SHARD]
    return out


def _sum_small(xs, name):
    def body(*refs):
        for x_ref, o_ref in zip(refs[:len(xs)], refs[len(xs):]):
            acc = x_ref[0]
            for j in range(1, N_DEV):
                acc = acc + x_ref[j]
            o_ref[...] = acc

    return pl.pallas_call(
        body, name=name, out_shape=[jax.ShapeDtypeStruct(x.shape[1:], F32) for x in xs],
        compiler_params=_cp(),
    )(*xs)


def _as_2d(a):
    if a.ndim == 1:
        return a.reshape(1, -1)
    return a.reshape(-1, a.shape[-1])


def kernel(x, attn_norm, w_in, b_forget, b_gate, w_proj_attn, pool_w, pool_scale, w_proj_pool, conv_w, w_proj_conv, w_out, ffn_norm, w_gate_up, w_down, final_norm, loss_target, m_attn_norm, m_w_in, m_b_forget, m_b_gate, m_w_proj_attn, m_pool_w, m_pool_scale, m_w_proj_pool, m_conv_w, m_w_proj_conv, m_w_out, m_ffn_norm, m_w_gate_up, m_w_down, m_final_norm, v_attn_norm, v_w_in, v_b_forget, v_b_gate, v_w_proj_attn, v_pool_w, v_pool_scale, v_w_proj_pool, v_conv_w, v_w_proj_conv, v_w_out, v_ffn_norm, v_w_gate_up, v_w_down, v_final_norm):
    weights = dict(attn_norm=attn_norm, w_in=w_in, b_forget=b_forget, b_gate=b_gate, w_proj_attn=w_proj_attn,
                   pool_w=pool_w, pool_scale=pool_scale, w_proj_pool=w_proj_pool, conv_w=conv_w,
                   w_proj_conv=w_proj_conv, w_out=w_out, ffn_norm=ffn_norm, w_gate_up=w_gate_up, w_down=w_down,
                   final_norm=final_norm)
    moments_m = dict(attn_norm=m_attn_norm, w_in=m_w_in, b_forget=m_b_forget, b_gate=m_b_gate,
                     w_proj_attn=m_w_proj_attn, pool_w=m_pool_w, pool_scale=m_pool_scale, w_proj_pool=m_w_proj_pool,
                     conv_w=m_conv_w, w_proj_conv=m_w_proj_conv, w_out=m_w_out, ffn_norm=m_ffn_norm,
                     w_gate_up=m_w_gate_up, w_down=m_w_down, final_norm=m_final_norm)
    moments_v = dict(attn_norm=v_attn_norm, w_in=v_w_in, b_forget=v_b_forget, b_gate=v_b_gate,
                     w_proj_attn=v_w_proj_attn, pool_w=v_pool_w, pool_scale=v_pool_scale, w_proj_pool=v_w_proj_pool,
                     conv_w=v_conv_w, w_proj_conv=v_w_proj_conv, w_out=v_w_out, ffn_norm=v_ffn_norm,
                     w_gate_up=v_w_gate_up, w_down=v_w_down, final_norm=v_final_norm)

    sh = _padded_shards(weights)
    names = list(SHARDED)
    rest = [n for n in names if n != "w_in"]
    me = 4 * lax.axis_index("x") + 2 * lax.axis_index("y") + lax.axis_index("c")
    w_in0, conv_all = _multi_gather([sh["w_in"], conv_w], [0, None], "gather_w_in_l0")
    started, after = {}, w_in0
    for l in range(DEPTH):
        for group, gnames in (("w_in", ["w_in"]), ("rest", rest)):
            if (l, group) != (0, "w_in"):
                started[l, group] = _split_start([sh[n] for n in gnames], [l] * len(gnames), False, after,
                                                 f"gather_start_{group}_l{l}")
                after = started[l, group][4]
    last_token = after

    def get_W(l, xt):
        if l == 0:
            w_in = w_in0
        else:
            w_in = _split_wait(started[l, "w_in"], [l], False, xt, f"gather_wait_w_in_l{l}")[0]
        W = _full_operands({"w_in": w_in}, l)

        def late(after):
            lands = _split_wait(started[l, "rest"], [l] * len(rest), False, after, f"gather_wait_rest_l{l}")
            return _full_operands(dict(zip(rest, lands)), l)

        W["late"] = late
        W.update(_replicated_operands(weights, l))
        W["conv_w"] = jnp.transpose(conv_all[:, l], (1, 0, 2)).reshape(CONV_K, BRANCH_W)
        if l == 0:
            W["attn_norm"] = W["attn_norm"] + last_token[0, 0]
        return W

    exchanges = []

    def stage(l, group, G, W):
        gnames = GRAD_GROUPS[group]
        slabs = [_grad_slabs(G, n, l) for n in gnames]
        started = _split_start(slabs, None, True, slabs[0], f"exchange_start_{group}_l{l}")
        exchanges.append((l, group, gnames, slabs, started))
        tie = {"ffn": "ffn_norm", "mix": "conv_w", "w_in": "w_f"}[group]
        W = dict(W)
        W[tie] = W[tie] + started[4][0, 0].astype(W[tie].dtype)
        return W

    loss_part, grad_x, grads, g_final = _local_step(x, loss_target, get_W, attn_norm, final_norm, stage)

    def finish_exchange(ex, after):
        l, group, gnames, slabs, started = ex
        lands = _split_wait(started, None, True, after, f"exchange_wait_{group}_l{l}")
        grads[l].update(_sum_layer_grads(dict(zip(gnames, lands)), l))

    def zero_after(a):
        return jnp.minimum(jnp.abs(a.reshape(-1)[0]), 0.0)

    *early, last_exchange = exchanges
    for ex in early:
        finish_exchange(ex, grad_x)
    deltas, new_m, new_v, gw = {}, {}, {}, {}
    views = {"w_in": ((2, 0, 1), (1, 2, 0), (49, DEPTH, D_MODEL)),
             "w_gate_up": ((0, 2, 1), (0, 2, 1), (1, GU_SHARD // 2, D_MODEL))}

    def update_sharded(n):
        gw[n] = jnp.stack([grads[l][n] for l in range(DEPTH)])
        if n in views:
            perm, inv, block = views[n]
            gt = jnp.transpose(gw[n], perm)
            d, nm, nv = _adamw_3d(jnp.transpose(weights[n], perm), gt, jnp.transpose(moments_m[n], perm),
                                  jnp.transpose(moments_v[n], perm), block, "adamw_" + n)
            deltas[n], new_m[n], new_v[n] = (jnp.transpose(a, inv) for a in (d, nm, nv))
            gw[n] = jnp.transpose(gt, inv)
            return
        shape = weights[n].shape
        d, nm, nv = _adamw(_as_2d(weights[n]), _as_2d(gw[n]), _as_2d(moments_m[n]), _as_2d(moments_v[n]),
                           "adamw_" + n)
        deltas[n], new_m[n], new_v[n] = d.reshape(shape), nm.reshape(shape), nv.reshape(shape)

    for n in SHARDED:
        if n != "w_in":
            update_sharded(n)

    small = ("attn_norm", "b_forget", "b_gate", "pool_w", "pool_scale", "ffn_norm", "conv_w")
    loss_part = loss_part + zero_after(deltas["w_gate_up"]) + zero_after(deltas["w_down"])
    parts = [jnp.stack([grads[l][n] for l in range(DEPTH)]) for n in small] + [g_final, loss_part]
    gathered = _multi_gather(parts, [None] * len(parts), "gather_small_grads")
    summed = _sum_small(gathered, "sum_small_grads")
    for n, s in zip(small, summed):
        gw[n] = s
    gw["attn_norm"], gw["ffn_norm"] = gw["attn_norm"][:, 0], gw["ffn_norm"][:, 0]
    gw["b_forget"] = gw["b_forget"][:, 0, :N_HEADS]
    gw["b_gate"], gw["pool_scale"] = gw["b_gate"][:, 0], gw["pool_scale"][:, 0]
    gw["conv_w"] = lax.dynamic_slice_in_dim(gw["conv_w"], me * (BRANCH_W // N_DEV), BRANCH_W // N_DEV, axis=2)
    gw["final_norm"] = summed[-2][0]
    loss = summed[-1][0, 0]

    rest_names = [n for n in WEIGHT_ORDER if n not in SHARDED]
    ds, nms, nvs = _adamw_many(*[[_as_2d(src[n]) for n in rest_names] for src in (weights, gw, moments_m, moments_v)],
                               "adamw_small")
    for n, d, nm, nv in zip(rest_names, ds, nms, nvs):
        shape = weights[n].shape
        deltas[n], new_m[n], new_v[n] = d.reshape(shape), nm.reshape(shape), nv.reshape(shape)

    finish_exchange(last_exchange, deltas["pool_w"])
    update_sharded("w_in")

    return (loss, grad_x, *[gw[n] for n in WEIGHT_ORDER], *[deltas[n] for n in WEIGHT_ORDER],
            *[new_m[n] for n in WEIGHT_ORDER], *[new_v[n] for n in WEIGHT_ORDER])
```

```python
import functools

import jax
import jax.numpy as jnp
from jax import lax
from jax.experimental import pallas as pl
from jax.experimental.pallas import tpu as pltpu

F32 = jnp.float32
BF16 = jnp.bfloat16

N_DEV = 8
D_MODEL = 1024
DEPTH = 2
N_HEADS = 8
HEAD_DIM = 64
BRANCH_W = 512
POOL_WINDOWS = (2, 4, 8, 16)
POOL_GD = 128
CONV_K = 3
FFN_HIDDEN = 2816
GATE_W = 3 * D_MODEL
IN_COLS = 6664
MAIN_COLS = GATE_W + 7 * BRANCH_W
RMS_EPS = 1e-6
NEG_INF = -1e30

ADAM_LR = 0.001
ADAM_B1 = 0.9
ADAM_B2 = 0.999
ADAM_EPS = 1e-08
ADAM_WD = 0.01
ADAM_STEP = 10

LANES = 128
VMEM_LIMIT = 56 * 1024 * 1024
CUM_BLK = 256

TRIPLE = 3 * LANES
OFF_G, OFF_QKV, OFF_CONV, OFF_U = 0, 3072, 4608, 6144


def _cp(sem=None):
    return pltpu.CompilerParams(dimension_semantics=sem, vmem_limit_bytes=VMEM_LIMIT)


def _sigmoid(z):
    return 1.0 / (1.0 + jnp.exp(-z))


def _matmul(a, b, *, mode, out_dtype, name, tm=2048, tn=512, tk=None, residual=None, rms_g=None, side=None,
            extra=None):
    if mode == "nn":
        (M, K), N = a.shape, b.shape[1]
    elif mode == "nt":
        (M, K), N = a.shape, b.shape[0]
    else:
        (K, M), N = a.shape, b.shape[1]
    tm, tn, tk = min(tm, M), min(tn, N), K if tk is None else min(tk, K)
    assert M % tm == 0 and N % tn == 0 and K % tk == 0, (name, M, N, K, tm, tn, tk)
    nk = K // tk
    if mode == "nn":
        a_spec = pl.BlockSpec((tm, tk), lambda i, j, k: (i, k))
        b_spec = pl.BlockSpec((tk, tn), lambda i, j, k: (k, j))
        dims = (((1,), (0,)), ((), ()))
    elif mode == "nt":
        a_spec = pl.BlockSpec((tm, tk), lambda i, j, k: (i, k))
        b_spec = pl.BlockSpec((tn, tk), lambda i, j, k: (j, k))
        dims = (((1,), (1,)), ((), ()))
    else:
        a_spec = pl.BlockSpec((tk, tm), lambda i, j, k: (k, i))
        b_spec = pl.BlockSpec((tk, tn), lambda i, j, k: (k, j))
        dims = (((0,), (0,)), ((), ()))
    o_spec = pl.BlockSpec((tm, tn), lambda i, j, k: (i, j))
    has_res, has_norm, has_side, has_extra = (v is not None for v in (residual, rms_g, side, extra))
    assert not has_norm or tn == N, (name, tn, N)
    assert not has_side or (nk == 1 and mode != "nt"), name

    in_specs, args = [a_spec, b_spec], [a, b]
    out_specs, out_shape = [o_spec], [jax.ShapeDtypeStruct((M, N), out_dtype)]
    if has_res:
        in_specs.append(o_spec)
        args.append(residual)
    if has_norm:
        in_specs.append(pl.BlockSpec((1, N), lambda i, j, k: (0, 0)))
        args.append(rms_g.reshape(1, N))
        out_specs.append(o_spec)
        out_shape.append(jax.ShapeDtypeStruct((M, N), BF16))
    if has_side:
        b_side, side_dtype = side
        ns = b_side.shape[1]
        in_specs.append(pl.BlockSpec((K, ns), lambda i, j, k: (0, 0)))
        args.append(b_side)
        out_specs.append(pl.BlockSpec((tm, ns), lambda i, j, k: (i, 0)))
        out_shape.append(jax.ShapeDtypeStruct((M, ns), side_dtype))
    if has_extra:
        a2, b2 = extra
        in_specs += [pl.BlockSpec((tm, a2.shape[1]), lambda i, j, k: (i, 0)),
                     pl.BlockSpec((tn, b2.shape[1]), lambda i, j, k: (j, 0))]
        args += [a2, b2]
    n_in = len(args)

    def body(*refs):
        ins, outs = list(refs[2:n_in]), list(refs[n_in:n_in + len(out_shape)])
        a_ref, b_ref = refs[:2]
        r_ref = ins.pop(0) if has_res else None
        g_ref = ins.pop(0) if has_norm else None
        bs_ref = ins.pop(0) if has_side else None
        a2_ref, b2_ref = (ins.pop(0), ins.pop(0)) if has_extra else (None, None)
        o_ref = outs.pop(0)
        h_ref = outs.pop(0) if has_norm else None
        so_ref = outs.pop(0) if has_side else None

        def finish(acc):
            if has_res:
                acc = acc + r_ref[...].astype(F32)
            if has_extra:
                acc = acc + lax.dot_general(a2_ref[...], b2_ref[...], (((1,), (1,)), ((), ())),
                                            preferred_element_type=F32)
            o_ref[...] = acc.astype(out_dtype)
            if has_norm:
                r = lax.rsqrt(jnp.mean(acc * acc, axis=-1, keepdims=True) + RMS_EPS)
                h_ref[...] = ((acc * r) * g_ref[...]).astype(BF16)

        if has_side:
            @pl.when(pl.program_id(1) == 0)
            def _():
                side_dims = (((1,), (0,)), ((), ())) if mode == "nn" else dims
                so_ref[...] = lax.dot_general(a_ref[...], bs_ref[...], side_dims,
                                              preferred_element_type=F32).astype(so_ref.dtype)

        prod = lax.dot_general(a_ref[...], b_ref[...], dims, preferred_element_type=F32)
        if nk == 1:
            finish(prod)
            return
        acc_ref = refs[-1]
        k = pl.program_id(2)

        @pl.when(k == 0)
        def _():
            acc_ref[...] = prod

        @pl.when(jnp.logical_and(k > 0, k < nk - 1))
        def _():
            acc_ref[...] += prod

        @pl.when(k == nk - 1)
        def _():
            finish(acc_ref[...] + prod)

    single = len(out_shape) == 1
    return pl.pallas_call(
        body, name=name, grid=(M // tm, N // tn, nk), in_specs=in_specs,
        out_specs=out_specs[0] if single else out_specs, out_shape=out_shape[0] if single else out_shape,
        scratch_shapes=[pltpu.VMEM((tm, tn), F32)] if nk > 1 else [],
        compiler_params=_cp(("parallel", "arbitrary" if has_side else "parallel", "arbitrary")),
    )(*args)


def _rms_fwd(x, g, name):
    T, Dm = x.shape
    tm = min(512, T)

    def body(x_ref, g_ref, h_ref):
        xf = x_ref[...]
        r = lax.rsqrt(jnp.mean(xf * xf, axis=-1, keepdims=True) + RMS_EPS)
        h_ref[...] = ((xf * r) * g_ref[...]).astype(BF16)

    return pl.pallas_call(
        body, name=name, grid=(T // tm,),
        in_specs=[pl.BlockSpec((tm, Dm), lambda i: (i, 0)), pl.BlockSpec((1, Dm), lambda i: (0, 0))],
        out_specs=pl.BlockSpec((tm, Dm), lambda i: (i, 0)),
        out_shape=jax.ShapeDtypeStruct((T, Dm), BF16),
        compiler_params=_cp(("parallel",)),
    )(x, g.reshape(1, Dm))


def _rms_bwd(x, g, dh, dres, name):
    T, Dm = x.shape
    tm = min(512, T)

    def body(x_ref, g_ref, dh_ref, dres_ref, dx_ref, dxb_ref, dg_ref):
        i = pl.program_id(0)
        xf = x_ref[...]
        r = lax.rsqrt(jnp.mean(xf * xf, axis=-1, keepdims=True) + RMS_EPS)
        xn = xf * r
        dhf = dh_ref[...].astype(F32)
        dxn = dhf * g_ref[...]
        c = jnp.mean(dxn * xn, axis=-1, keepdims=True)
        dx = dres_ref[...] + r * (dxn - xn * c)
        dx_ref[...] = dx
        dxb_ref[...] = dx.astype(BF16)
        part = jnp.sum(dhf * xn, axis=0, keepdims=True)

        @pl.when(i == 0)
        def _():
            dg_ref[...] = part

        @pl.when(i > 0)
        def _():
            dg_ref[...] += part

    row = pl.BlockSpec((tm, Dm), lambda i: (i, 0))
    vec = pl.BlockSpec((1, Dm), lambda i: (0, 0))
    return pl.pallas_call(
        body, name=name, grid=(T // tm,), in_specs=[row, vec, row, row], out_specs=[row, row, vec],
        out_shape=[jax.ShapeDtypeStruct((T, Dm), F32), jax.ShapeDtypeStruct((T, Dm), BF16),
                   jax.ShapeDtypeStruct((1, Dm), F32)],
        compiler_params=_cp(("arbitrary",)),
    )(x, g.reshape(1, Dm), dh, dres)


def _loss_head(x, g, target, name):
    T, Dm = x.shape
    tm = min(512, T)

    def body(x_ref, g_ref, t_ref, loss_ref, dx_ref, dxb_ref, dg_ref):
        i = pl.program_id(0)
        xf = x_ref[...]
        gv = g_ref[...]
        r = lax.rsqrt(jnp.mean(xf * xf, axis=-1, keepdims=True) + RMS_EPS)
        xn = xf * r
        diff = xn * gv - t_ref[...]
        per_tok = jnp.mean(diff * diff, axis=-1, keepdims=True)
        lpart = 0.5 * jnp.sum(per_tok, axis=0, keepdims=True) + jnp.zeros((1, LANES), F32)
        dy = diff * (1.0 / Dm)
        dxn = dy * gv
        c = jnp.mean(dxn * xn, axis=-1, keepdims=True)
        dx = r * (dxn - xn * c)
        dx_ref[...] = dx
        dxb_ref[...] = dx.astype(BF16)
        part = jnp.sum(dy * xn, axis=0, keepdims=True)

        @pl.when(i == 0)
        def _():
            dg_ref[...] = part
            loss_ref[...] = lpart

        @pl.when(i > 0)
        def _():
            dg_ref[...] += part
            loss_ref[...] += lpart

    row = pl.BlockSpec((tm, Dm), lambda i: (i, 0))
    vec = pl.BlockSpec((1, Dm), lambda i: (0, 0))
    lsp = pl.BlockSpec((1, LANES), lambda i: (0, 0))
    return pl.pallas_call(
        body, name=name, grid=(T // tm,), in_specs=[row, vec, row], out_specs=[lsp, row, row, vec],
        out_shape=[jax.ShapeDtypeStruct((1, LANES), F32), jax.ShapeDtypeStruct((T, Dm), F32),
                   jax.ShapeDtypeStruct((T, Dm), BF16), jax.ShapeDtypeStruct((1, Dm), F32)],
        compiler_params=_cp(("arbitrary",)),
    )(x, g.reshape(1, Dm), target)


def _split_bf16(v):
    hi = v.astype(BF16)
    r1 = v - hi.astype(F32)
    mid = r1.astype(BF16)
    lo = (r1 - mid.astype(F32)).astype(BF16)
    return hi, mid, lo


def _tri_dot(tri, v):
    hi, mid, lo = _split_bf16(v)
    dot = functools.partial(jnp.dot, preferred_element_type=F32)
    return dot(tri, hi) + dot(tri, mid) + dot(tri, lo)


def _log_sigmoid(z):
    return jnp.minimum(z, 0.0) - jnp.log(1.0 + jnp.exp(-jnp.abs(z)))


def _fox_cumsum_bwd(f, bf, dF, n_seq, name):
    T = f.shape[0]
    S = T // n_seq
    c = min(CUM_BLK, S)

    def body(f_ref, b_ref, dF_ref, df_ref, db_ref):
        b = pl.program_id(0)
        ri = lax.broadcasted_iota(jnp.int32, (c, c), 0)
        ci = lax.broadcasted_iota(jnp.int32, (c, c), 1)
        tri = (ri <= ci).astype(BF16)
        carry = jnp.zeros((1, LANES), F32)
        dbp = jnp.zeros((1, LANES), F32)
        for j in reversed(range(S // c)):
            dFc = dF_ref[j * c:(j + 1) * c, :]
            dlf = _tri_dot(tri, dFc) + carry
            carry = carry + jnp.sum(dFc, axis=0, keepdims=True)
            z = f_ref[j * c:(j + 1) * c, :] + b_ref[...]
            dz = dlf * _sigmoid(-z)
            df_ref[j * c:(j + 1) * c, :] = dz.astype(BF16)
            dbp = dbp + jnp.sum(dz, axis=0, keepdims=True)

        @pl.when(b == 0)
        def _():
            db_ref[...] = dbp

        @pl.when(b > 0)
        def _():
            db_ref[...] += dbp

    blk = pl.BlockSpec((S, LANES), lambda b: (b, 0))
    vec = pl.BlockSpec((1, LANES), lambda b: (0, 0))
    return pl.pallas_call(
        body, name=name, grid=(n_seq,), in_specs=[blk, vec, blk], out_specs=[blk, vec],
        out_shape=[jax.ShapeDtypeStruct((T, LANES), BF16), jax.ShapeDtypeStruct((1, LANES), F32)],
        compiler_params=_cp(("arbitrary",)),
    )(f, bf, dF)


def _pair_masks():
    lane = lax.broadcasted_iota(jnp.int32, (1, LANES), 1)
    lo = lane < HEAD_DIM
    return lo, jnp.logical_not(lo)


AUG0 = HEAD_DIM
Q_TILE, K_CHUNK, ROW_GROUP = 512, 256, 128


def _fox_prep(f, bf, proj, n_seq, name):
    T = f.shape[0]
    S = T // n_seq
    c = min(CUM_BLK, S)

    def body(f_ref, b_ref, qkv_ref, qa_ref, ka_ref, va_ref):
        ri = lax.broadcasted_iota(jnp.int32, (c, c), 0)
        ci = lax.broadcasted_iota(jnp.int32, (c, c), 1)
        tri = (ri >= ci).astype(BF16)
        lane = lax.broadcasted_iota(jnp.int32, (c, LANES), 1)
        carry = jnp.zeros((1, LANES), F32)
        for j in range(S // c):
            rows = slice(j * c, (j + 1) * c)
            lf = _log_sigmoid(f_ref[rows, :] + b_ref[...])
            Fc = _tri_dot(tri, lf) + carry
            carry = carry + jnp.sum(lf, axis=0, keepdims=True)
            for h in range(N_HEADS):
                col = jnp.sum(jnp.where(lane == h, Fc, 0.0), axis=-1, keepdims=True)
                hi = col.astype(BF16).astype(F32)
                r1 = col - hi
                mid = r1.astype(BF16).astype(F32)
                lo = r1 - mid
                ones_q = jnp.logical_and(lane >= AUG0 + 3, lane < AUG0 + 6)
                ones_k = jnp.logical_and(lane >= AUG0, lane < AUG0 + 3)
                aug_q = jnp.where(lane == AUG0, hi, jnp.where(lane == AUG0 + 1, mid, jnp.where(
                    lane == AUG0 + 2, lo, jnp.where(ones_q, 1.0, 0.0))))
                aug_k = jnp.where(lane == AUG0 + 3, -hi, jnp.where(lane == AUG0 + 4, -mid, jnp.where(
                    lane == AUG0 + 5, -lo, jnp.where(ones_k, 1.0, 0.0))))
                base = (h // 2) * TRIPLE
                qp, kp, vp = (qkv_ref[rows, base + t * LANES:base + (t + 1) * LANES].astype(F32) for t in range(3))
                if h % 2:
                    qp, kp, vp = (pltpu.roll(a, HEAD_DIM, 1) for a in (qp, kp, vp))
                out = slice(h * LANES, (h + 1) * LANES)
                qa_ref[rows, out] = jnp.where(lane < HEAD_DIM, qp * (HEAD_DIM ** -0.5), aug_q).astype(BF16)
                ka_ref[rows, out] = jnp.where(lane < HEAD_DIM, kp, aug_k).astype(BF16)
                va_ref[rows, out] = jnp.where(lane < HEAD_DIM, vp, jnp.where(lane == AUG0, 1.0, 0.0)).astype(BF16)

    fblk = pl.BlockSpec((S, LANES), lambda b: (b, 0))
    out = pl.BlockSpec((S, N_HEADS * LANES), lambda b: (b, 0))
    sh = jax.ShapeDtypeStruct((T, N_HEADS * LANES), BF16)
    return pl.pallas_call(
        body, name=name, grid=(n_seq,),
        in_specs=[fblk, pl.BlockSpec((1, LANES), lambda b: (0, 0)),
                  pl.BlockSpec((S, 4 * TRIPLE), lambda b: (b, OFF_QKV // (4 * TRIPLE)))],
        out_specs=[out, out, out], out_shape=[sh, sh, sh],
        compiler_params=_cp(("parallel",)),
    )(f, bf, proj)


def _band_mask(q0, k0, nq, nk):
    row = q0 + lax.broadcasted_iota(jnp.int32, (nq, nk), 0)
    col = k0 + lax.broadcasted_iota(jnp.int32, (nq, nk), 1)
    return col <= row


_NT = (((1,), (1,)), ((), ()))
_TN = (((0,), (0,)), ((), ()))


def _attn_fwd2(qa, ka, va, n_seq, name):
    T = qa.shape[0]
    S = T // n_seq
    tq, tk, rg = min(Q_TILE, S), min(K_CHUNK, S), ROW_GROUP
    nq, per = S // tq, tq // tk

    def body(q_ref, k_ref, v_ref, o_ref, o32_ref, lse_ref, phi_s, plo_s, mp_s, m_s, acc_s):
        qi = pl.program_id(2)
        mp_s[...] = jnp.full_like(mp_s, NEG_INF)
        acc_s[...] = jnp.zeros_like(acc_s)

        def scores(kc, hh, r0):
            k0 = pl.multiple_of(kc * tk, tk)
            hl = slice(hh * LANES, (hh + 1) * LANES)
            return k0, lax.dot_general(q_ref[r0:, hl], k_ref[pl.ds(k0, tk), hl], _NT, preferred_element_type=F32)

        def max_chunk(kc, masked, r0):
            for hh in range(2):
                k0, s_all = scores(kc, hh, r0)
                for r in range(r0 // rg, tq // rg):
                    rows = slice(r * rg, (r + 1) * rg)
                    s = s_all[r * rg - r0:(r + 1) * rg - r0, :]
                    if masked:
                        s = jnp.where(_band_mask(qi * tq + r * rg, k0, rg, tk), s, NEG_INF)
                    part = s[:, :LANES]
                    for c in range(1, tk // LANES):
                        part = jnp.maximum(part, s[:, c * LANES:(c + 1) * LANES])
                    mp_s[hh, rows, :] = jnp.maximum(mp_s[hh, rows, :], part)

        def sum_chunk(kc, masked, r0):
            for hh in range(2):
                k0, s_all = scores(kc, hh, r0)
                hl = slice(hh * LANES, (hh + 1) * LANES)
                v = v_ref[pl.ds(k0, tk), hl]
                for r in range(r0 // rg, tq // rg):
                    rows = slice(r * rg, (r + 1) * rg)
                    p = jnp.exp(s_all[r * rg - r0:(r + 1) * rg - r0, :] - m_s[hh, rows])
                    if masked:
                        p = jnp.where(_band_mask(qi * tq + r * rg, k0, rg, tk), p, 0.0)
                    p_hi = p.astype(BF16)
                    phi_s[hh, rows, :] = p_hi
                    plo_s[hh, rows, :] = (p - p_hi.astype(F32)).astype(BF16)
                acc_s[hh, r0:, :] += (jnp.dot(phi_s[hh, r0:, :], v, preferred_element_type=F32)
                                      + jnp.dot(plo_s[hh, r0:, :], v, preferred_element_type=F32))

        def sweep(chunk):
            def unmasked(kc, carry):
                chunk(kc, False, 0)
                return carry

            lax.fori_loop(0, qi * per, unmasked, 0)
            for d in range(per):
                chunk(qi * per + d, True, d * tk)

        sweep(max_chunk)
        m_s[...] = jnp.max(mp_s[...], axis=-1, keepdims=True)
        sweep(sum_chunk)

        lane = lax.broadcasted_iota(jnp.int32, (1, LANES), 1)
        outs = []
        for hh in range(2):
            acc = acc_s[hh]
            l = jnp.sum(jnp.where(lane == AUG0, acc, 0.0), axis=-1, keepdims=True)
            lse_ref[hh] = m_s[hh] + jnp.log(l)
            outs.append(acc / l)
        o = jnp.where(lane < HEAD_DIM, outs[0], pltpu.roll(outs[1], HEAD_DIM, 1))
        o_ref[...] = o.astype(BF16)
        o32_ref[...] = o

    qmap = lambda b, j, qi: (b * nq + qi, j)
    omap = lambda b, j, qi: (b * nq + qi, j)
    kv = pl.BlockSpec((S, 2 * LANES), lambda b, j, qi: (b, j))
    return pl.pallas_call(
        body, name=name, grid=(n_seq, N_HEADS // 2, nq),
        in_specs=[pl.BlockSpec((tq, 2 * LANES), qmap), kv, kv],
        out_specs=[pl.BlockSpec((tq, LANES), omap), pl.BlockSpec((tq, LANES), omap),
                   pl.BlockSpec((2, tq, 1), lambda b, j, qi: (j, b * nq + qi, 0))],
        out_shape=[jax.ShapeDtypeStruct((T, BRANCH_W), BF16), jax.ShapeDtypeStruct((T, BRANCH_W), F32),
                   jax.ShapeDtypeStruct((N_HEADS, T, 1), F32)],
        scratch_shapes=[pltpu.VMEM((2, tq, tk), BF16), pltpu.VMEM((2, tq, tk), BF16),
                        pltpu.VMEM((2, tq, LANES), F32), pltpu.VMEM((2, tq, 1), F32),
                        pltpu.VMEM((2, tq, LANES), F32)],
        compiler_params=_cp(("parallel", "parallel", "parallel")),
    )(qa, ka, va)


def _attn_bwd(qa, ka, proj, do, o32, lse, dproj, n_seq, name):
    T = qa.shape[0]
    S = T // n_seq
    tq, tk, rg = min(Q_TILE, S), min(K_CHUNK, S), ROW_GROUP
    nq, per, nkc = S // tq, tq // tk, S // tk

    def body(q_ref, k_ref, v_ref, do_ref, o_ref, lse_ref, _, dqkv_ref, dfk_ref,
             p_s, ds_s, dq_s, dk_s, dv_s, df_s):
        dk_s[...] = jnp.zeros_like(dk_s)
        dv_s[...] = jnp.zeros_like(dv_s)
        df_s[...] = jnp.zeros_like(df_s)
        sels = _pair_masks()

        for qi in range(nq):
            q0 = qi * tq
            do_t = do_ref[q0:q0 + tq, :]
            dq_s[...] = jnp.zeros_like(dq_s)
            prod = do_t.astype(F32) * o_ref[q0:q0 + tq, :]
            dls = [jnp.sum(jnp.where(sel, prod, 0.0), axis=-1, keepdims=True) for sel in sels]

            def chunk(kc, masked, r0, q0=q0, do_t=do_t, dls=dls):
                k0 = pl.multiple_of(kc * tk, tk)
                v = v_ref[pl.ds(k0, tk), :]
                do_a = do_t[r0:, :]
                for hh in range(2):
                    hl = slice(hh * LANES, (hh + 1) * LANES)
                    qh, kh = q_ref[q0 + r0:q0 + tq, hl], k_ref[pl.ds(k0, tk), hl]
                    s_all = lax.dot_general(qh, kh, _NT, preferred_element_type=F32)
                    dom = jnp.where(sels[hh], do_a, jnp.zeros_like(do_a))
                    dp_all = lax.dot_general(dom, v, _NT, preferred_element_type=F32)
                    dfp = jnp.zeros((1, tk), F32)
                    for r in range(r0 // rg, tq // rg):
                        rows = slice(r * rg, (r + 1) * rg)
                        arows = slice(r * rg - r0, (r + 1) * rg - r0)
                        qrows = slice(q0 + r * rg, q0 + (r + 1) * rg)
                        p = jnp.exp(s_all[arows, :] - lse_ref[hh, qrows])
                        if masked:
                            p = jnp.where(_band_mask(q0 + r * rg, k0, rg, tk), p, 0.0)
                        ds = p * (dp_all[arows, :] - dls[hh][rows])
                        p_s[hh, rows, :] = p.astype(BF16)
                        ds_s[hh, rows, :] = ds.astype(BF16)
                        dfp = dfp + jnp.sum(ds, axis=0, keepdims=True)
                    df_s[hh, kc] -= dfp
                    dq_s[hh, r0:, :] += jnp.dot(ds_s[hh, r0:, :], kh, preferred_element_type=F32)
                    dv_s[hh, pl.ds(k0, tk), :] += lax.dot_general(p_s[hh, r0:, :], do_a, _TN,
                                                                  preferred_element_type=F32)
                    dk_s[hh, pl.ds(k0, tk), :] += lax.dot_general(ds_s[hh, r0:, :], qh, _TN,
                                                                  preferred_element_type=F32)

            def unmasked(kc, carry, chunk=chunk):
                chunk(kc, False, 0)
                return carry

            lax.fori_loop(0, qi * per, unmasked, 0)
            for d in range(per):
                chunk(qi * per + d, True, d * tk)
            dq = jnp.where(sels[0], dq_s[0], pltpu.roll(dq_s[1], HEAD_DIM, 1))
            dqkv_ref[q0:q0 + tq, :LANES] = (dq * (HEAD_DIM ** -0.5)).astype(BF16)

        dqkv_ref[:, LANES:2 * LANES] = jnp.where(sels[0], dk_s[0], pltpu.roll(dk_s[1], HEAD_DIM, 1)).astype(BF16)
        dqkv_ref[:, 2 * LANES:] = jnp.where(sels[0], dv_s[0], dv_s[1]).astype(BF16)
        for c in range(nkc):
            dfk_ref[:, :, c * tk:(c + 1) * tk] = df_s[:, c]

    seq = lambda w: pl.BlockSpec((S, w), lambda b, j: (b, j))
    col1 = pl.BlockSpec((2, S, 1), lambda b, j: (j, b, 0))
    vblk = pl.BlockSpec((S, LANES), lambda b, j: (b, OFF_QKV // LANES + 3 * j + 2))
    return pl.pallas_call(
        body, name=name, grid=(n_seq, N_HEADS // 2),
        in_specs=[seq(2 * LANES), seq(2 * LANES), vblk, seq(LANES), seq(LANES), col1,
                  pl.BlockSpec(memory_space=pl.ANY)],
        out_specs=[pl.BlockSpec((S, TRIPLE), lambda b, j: (b, OFF_QKV // TRIPLE + j)),
                   pl.BlockSpec((2, 1, S), lambda b, j: (j, 0, b))],
        out_shape=[jax.ShapeDtypeStruct(dproj.shape, BF16), jax.ShapeDtypeStruct((N_HEADS, 1, T), F32)],
        input_output_aliases={6: 0},
        scratch_shapes=[pltpu.VMEM((2, tq, tk), BF16), pltpu.VMEM((2, tq, tk), BF16),
                        pltpu.VMEM((2, tq, LANES), F32), pltpu.VMEM((2, S, LANES), F32),
                        pltpu.VMEM((2, S, LANES), F32), pltpu.VMEM((2, nkc, 1, tk), F32)],
        compiler_params=_cp(("parallel", "parallel")),
    )(qa, ka, proj, do, o32, lse, dproj)


def _shift_down(v, k, row):
    return jnp.where(row >= k, pltpu.roll(v, k, 0), 0.0)


def _shift_up(v, k, row, S):
    return jnp.where(row < S - k, pltpu.roll(v, S - k, 0), 0.0)


def _pool_diff(uf, w, row):
    acc, k = uf, 1
    while k < w:
        acc = acc + _shift_down(acc, k, row)
        k *= 2
    n = jnp.minimum(row + 1, w).astype(F32)
    return acc / n - uf


def _pool_fwd(proj, pool_w, pool_scale, n_seq, name):
    T = proj.shape[0]
    S = T // n_seq

    def body(u_ref, w_ref, sc_ref, o_ref, d_s):
        g = pl.program_id(1)
        row = lax.broadcasted_iota(jnp.int32, (S, POOL_GD), 0)
        uf = u_ref[...].astype(F32)
        for gi, wlen in enumerate(POOL_WINDOWS):
            @pl.when(g == gi)
            def _(wlen=wlen):
                d_s[...] = _pool_diff(uf, wlen, row).astype(BF16)
        e = jnp.dot(d_s[...], w_ref[0], preferred_element_type=F32)
        o_ref[...] = (e * sc_ref[...]).astype(BF16)

    uc = OFF_U // POOL_GD
    return pl.pallas_call(
        body, name=name, grid=(n_seq, len(POOL_WINDOWS)),
        in_specs=[pl.BlockSpec((S, POOL_GD), lambda b, g: (b, uc + g)),
                  pl.BlockSpec((1, POOL_GD, POOL_GD), lambda b, g: (g, 0, 0)),
                  pl.BlockSpec((1, POOL_GD), lambda b, g: (0, g))],
        out_specs=pl.BlockSpec((S, POOL_GD), lambda b, g: (b, g)),
        out_shape=jax.ShapeDtypeStruct((T, BRANCH_W), BF16),
        scratch_shapes=[pltpu.VMEM((S, POOL_GD), BF16)],
        compiler_params=_cp(("parallel", "parallel")),
    )(proj, pool_w, pool_scale)


def _pool_bwd(proj, dout, pool_w, pool_scale, dproj, n_seq, name):
    T = proj.shape[0]
    S = T // n_seq

    def body(u_ref, do_ref, w_ref, sc_ref, _, du_ref, dw_ref, dsc_ref, d_s):
        g, b = pl.program_id(0), pl.program_id(1)
        row = lax.broadcasted_iota(jnp.int32, (S, POOL_GD), 0)
        uf = u_ref[...].astype(F32)
        for gi, wlen in enumerate(POOL_WINDOWS):
            @pl.when(g == gi)
            def _(wlen=wlen):
                d_s[...] = _pool_diff(uf, wlen, row).astype(BF16)
        db16 = d_s[...]
        w = w_ref[0]
        e = jnp.dot(db16, w, preferred_element_type=F32)
        dof = do_ref[...].astype(F32)
        dsc = jnp.sum(dof * e, axis=0, keepdims=True)
        de = (dof * sc_ref[...]).astype(BF16)
        dd = lax.dot_general(de, w, (((1,), (1,)), ((), ())), preferred_element_type=F32)
        dw = lax.dot_general(db16, de, (((0,), (0,)), ((), ())), preferred_element_type=F32)
        for gi, wlen in enumerate(POOL_WINDOWS):
            @pl.when(g == gi)
            def _(wlen=wlen):
                n = jnp.minimum(row + 1, wlen).astype(F32)
                acc, k = dd / n, 1
                while k < wlen:
                    acc = acc + _shift_up(acc, k, row, S)
                    k *= 2
                du_ref[...] = (acc - dd).astype(BF16)

        @pl.when(b == 0)
        def _():
            dw_ref[0] = dw
            dsc_ref[...] = dsc

        @pl.when(b > 0)
        def _():
            dw_ref[0] += dw
            dsc_ref[...] += dsc

    uc = OFF_U // POOL_GD
    return pl.pallas_call(
        body, name=name, grid=(len(POOL_WINDOWS), n_seq),
        in_specs=[pl.BlockSpec((S, POOL_GD), lambda g, b: (b, uc + g)),
                  pl.BlockSpec((S, POOL_GD), lambda g, b: (b, g)),
                  pl.BlockSpec((1, POOL_GD, POOL_GD), lambda g, b: (g, 0, 0)),
                  pl.BlockSpec((1, POOL_GD), lambda g, b: (0, g)),
                  pl.BlockSpec(memory_space=pl.ANY)],
        out_specs=[pl.BlockSpec((S, POOL_GD), lambda g, b: (b, uc + g)),
                   pl.BlockSpec((1, POOL_GD, POOL_GD), lambda g, b: (g, 0, 0)),
                   pl.BlockSpec((1, POOL_GD), lambda g, b: (0, g))],
        out_shape=[jax.ShapeDtypeStruct(dproj.shape, BF16),
                   jax.ShapeDtypeStruct((len(POOL_WINDOWS), POOL_GD, POOL_GD), F32),
                   jax.ShapeDtypeStruct((1, BRANCH_W), F32)],
        input_output_aliases={4: 0},
        scratch_shapes=[pltpu.VMEM((S, POOL_GD), BF16)],
        compiler_params=_cp(("parallel", "arbitrary")),
    )(proj, dout, pool_w, pool_scale, dproj)


def _conv_fwd(proj, conv_w, n_seq, name):
    T = proj.shape[0]
    S = T // n_seq
    nc = BRANCH_W // LANES

    def body(c_ref, w_ref, o_ref):
        row = lax.broadcasted_iota(jnp.int32, (S, LANES), 0)
        cv, cb, cc = (c_ref[:, t * LANES:(t + 1) * LANES].astype(F32) for t in range(3))
        z = cc * cv
        w = w_ref[...]
        y = w[0:1] * _shift_down(z, 2, row) + w[1:2] * _shift_down(z, 1, row) + w[2:3] * z
        o_ref[...] = (cb * y).astype(BF16)

    return pl.pallas_call(
        body, name=name, grid=(n_seq, nc),
        in_specs=[pl.BlockSpec((S, TRIPLE), lambda b, j: (b, OFF_CONV // TRIPLE + j)),
                  pl.BlockSpec((CONV_K, LANES), lambda b, j: (0, j))],
        out_specs=pl.BlockSpec((S, LANES), lambda b, j: (b, j)),
        out_shape=jax.ShapeDtypeStruct((T, BRANCH_W), BF16),
        compiler_params=_cp(("parallel", "parallel")),
    )(proj, conv_w)


def _conv_bwd(proj, dout, conv_w, dproj, n_seq, name):
    T = proj.shape[0]
    S = T // n_seq
    nc = BRANCH_W // LANES

    def body(c_ref, do_ref, w_ref, _, dc_ref, dw_ref):
        b = pl.program_id(1)
        row = lax.broadcasted_iota(jnp.int32, (S, LANES), 0)
        cv, cb, cc = (c_ref[:, t * LANES:(t + 1) * LANES].astype(F32) for t in range(3))
        dof = do_ref[...].astype(F32)
        w = w_ref[...]
        z = cc * cv
        z1, z2 = _shift_down(z, 1, row), _shift_down(z, 2, row)
        y = w[0:1] * z2 + w[1:2] * z1 + w[2:3] * z
        dy = dof * cb
        dz = w[2:3] * dy + w[1:2] * _shift_up(dy, 1, row, S) + w[0:1] * _shift_up(dy, 2, row, S)
        dc_ref[:, :LANES] = (dz * cc).astype(BF16)
        dc_ref[:, LANES:2 * LANES] = (dof * y).astype(BF16)
        dc_ref[:, 2 * LANES:] = (dz * cv).astype(BF16)
        dws = [jnp.sum(dy * zk, axis=0, keepdims=True) for zk in (z2, z1, z)]

        @pl.when(b == 0)
        def _():
            for kk in range(CONV_K):
                dw_ref[kk:kk + 1, :] = dws[kk]

        @pl.when(b > 0)
        def _():
            for kk in range(CONV_K):
                dw_ref[kk:kk + 1, :] += dws[kk]

    triple = pl.BlockSpec((S, TRIPLE), lambda j, b: (b, OFF_CONV // TRIPLE + j))
    wsp = pl.BlockSpec((CONV_K, LANES), lambda j, b: (0, j))
    return pl.pallas_call(
        body, name=name, grid=(nc, n_seq),
        in_specs=[triple, pl.BlockSpec((S, LANES), lambda j, b: (b, j)), wsp, pl.BlockSpec(memory_space=pl.ANY)],
        out_specs=[triple, wsp],
        out_shape=[jax.ShapeDtypeStruct(dproj.shape, BF16), jax.ShapeDtypeStruct((CONV_K, BRANCH_W), F32)],
        input_output_aliases={3: 0},
        compiler_params=_cp(("parallel", "arbitrary")),
    )(proj, dout, conv_w, dproj)


def _mix_fwd(oa, ob, oc, wpa, wpp, wpc, proj, b_gate, name):
    T = oa.shape[0]
    tm = min(512, T)

    def body(oa_ref, ob_ref, oc_ref, wa_ref, wp_ref, wc_ref, g_ref, bg_ref, o_ref):
        acc = jnp.zeros((tm, D_MODEL), F32)
        for i, (x_ref, w_ref) in enumerate(((oa_ref, wa_ref), (ob_ref, wp_ref), (oc_ref, wc_ref))):
            y = jnp.dot(x_ref[...], w_ref[...], preferred_element_type=F32)
            sl = slice(i * D_MODEL, (i + 1) * D_MODEL)
            acc = acc + _sigmoid(g_ref[:, sl].astype(F32) + bg_ref[:, sl]) * y
        o_ref[...] = acc.astype(BF16)

    br = pl.BlockSpec((tm, BRANCH_W), lambda i: (i, 0))
    wsp = pl.BlockSpec((BRANCH_W, D_MODEL), lambda i: (0, 0))
    return pl.pallas_call(
        body, name=name, grid=(T // tm,),
        in_specs=[br, br, br, wsp, wsp, wsp, pl.BlockSpec((tm, GATE_W), lambda i: (i, 0)),
                  pl.BlockSpec((1, GATE_W), lambda i: (0, 0))],
        out_specs=pl.BlockSpec((tm, D_MODEL), lambda i: (i, 0)),
        out_shape=jax.ShapeDtypeStruct((T, D_MODEL), BF16),
        compiler_params=_cp(("parallel",)),
    )(oa, ob, oc, wpa, wpp, wpc, proj, b_gate)


def _mix_bwd(oa, ob, oc, wpa, wpp, wpc, proj, b_gate, dmixed, name):
    T = oa.shape[0]
    tm = min(256, T)

    def body(oa_ref, ob_ref, oc_ref, wa_ref, wp_ref, wc_ref, g_ref, bg_ref, dm_ref,
             dya_ref, dyb_ref, dyc_ref, dg_ref, dbg_ref):
        i0 = pl.program_id(0)
        dm = dm_ref[...].astype(F32)
        parts = []
        for i, (x_ref, w_ref, dy_ref) in enumerate(((oa_ref, wa_ref, dya_ref), (ob_ref, wp_ref, dyb_ref),
                                                    (oc_ref, wc_ref, dyc_ref))):
            y = jnp.dot(x_ref[...], w_ref[...], preferred_element_type=F32)
            sl = slice(i * D_MODEL, (i + 1) * D_MODEL)
            gate = _sigmoid(g_ref[:, sl].astype(F32) + bg_ref[:, sl])
            dy_ref[...] = (dm * gate).astype(BF16)
            dgl = dm * y * gate * (1.0 - gate)
            dg_ref[:, sl] = dgl.astype(BF16)
            parts.append(jnp.sum(dgl, axis=0, keepdims=True))

        @pl.when(i0 == 0)
        def _():
            for i in range(3):
                dbg_ref[:, i * D_MODEL:(i + 1) * D_MODEL] = parts[i]

        @pl.when(i0 > 0)
        def _():
            for i in range(3):
                dbg_ref[:, i * D_MODEL:(i + 1) * D_MODEL] += parts[i]

    br = pl.BlockSpec((tm, BRANCH_W), lambda i: (i, 0))
    wsp = pl.BlockSpec((BRANCH_W, D_MODEL), lambda i: (0, 0))
    row = pl.BlockSpec((tm, D_MODEL), lambda i: (i, 0))
    gsp = pl.BlockSpec((tm, GATE_W), lambda i: (i, 0))
    bsp = pl.BlockSpec((1, GATE_W), lambda i: (0, 0))
    act = jax.ShapeDtypeStruct((T, D_MODEL), BF16)
    return pl.pallas_call(
        body, name=name, grid=(T // tm,),
        in_specs=[br, br, br, wsp, wsp, wsp, gsp, bsp, row],
        out_specs=[row, row, row, gsp, bsp],
        out_shape=[act, act, act, jax.ShapeDtypeStruct((T, MAIN_COLS), BF16),
                   jax.ShapeDtypeStruct((1, GATE_W), F32)],
        compiler_params=_cp(("arbitrary",)),
    )(oa, ob, oc, wpa, wpp, wpc, proj, b_gate, dmixed)


GU_TILE = 256


def _gu_col(c):
    t, r = divmod(c, GU_TILE)
    return (t // 2) * GU_TILE + r + (FFN_HIDDEN if t % 2 else 0)


def _gate_up_swiglu(h, w, name):
    T, K = h.shape
    tm = min(2048, T)

    def body(h_ref, w_ref, ab_ref, s_ref):
        prod = jnp.dot(h_ref[...], w_ref[...], preferred_element_type=F32)
        ab_ref[...] = prod.astype(BF16)
        a = prod[:, :GU_TILE]
        s_ref[...] = (a * _sigmoid(a) * prod[:, GU_TILE:]).astype(BF16)

    return pl.pallas_call(
        body, name=name, grid=(T // tm, FFN_HIDDEN // GU_TILE),
        in_specs=[pl.BlockSpec((tm, K), lambda i, j: (i, 0)), pl.BlockSpec((K, 2 * GU_TILE), lambda i, j: (0, j))],
        out_specs=[pl.BlockSpec((tm, 2 * GU_TILE), lambda i, j: (i, j)), pl.BlockSpec((tm, GU_TILE), lambda i, j: (i, j))],
        out_shape=[jax.ShapeDtypeStruct((T, 2 * FFN_HIDDEN), BF16), jax.ShapeDtypeStruct((T, FFN_HIDDEN), BF16)],
        compiler_params=_cp(("parallel", "parallel")),
    )(h, w)


def _swiglu_bwd_fused(dx, w_down, ab, name):
    T, K = dx.shape
    tm = min(2048, T)

    def body(dx_ref, w_ref, ab_ref, o_ref):
        ds = lax.dot_general(dx_ref[...], w_ref[...], _NT, preferred_element_type=F32)
        a = ab_ref[:, :GU_TILE].astype(F32)
        b = ab_ref[:, GU_TILE:].astype(F32)
        sg = _sigmoid(a)
        o_ref[:, :GU_TILE] = (ds * b * sg * (1.0 + a * (1.0 - sg))).astype(BF16)
        o_ref[:, GU_TILE:] = (ds * a * sg).astype(BF16)

    pair = pl.BlockSpec((tm, 2 * GU_TILE), lambda i, j: (i, j))
    return pl.pallas_call(
        body, name=name, grid=(T // tm, FFN_HIDDEN // GU_TILE),
        in_specs=[pl.BlockSpec((tm, K), lambda i, j: (i, 0)), pl.BlockSpec((GU_TILE, K), lambda i, j: (j, 0)), pair],
        out_specs=pair, out_shape=jax.ShapeDtypeStruct((T, 2 * FFN_HIDDEN), BF16),
        compiler_params=_cp(("parallel", "parallel")),
    )(dx, w_down, ab)


def _adamw_update(w_ref, g_ref, m_ref, v_ref, d_ref, nm_ref, nv_ref):
    gv = g_ref[...]
    nm = ADAM_B1 * m_ref[...] + (1.0 - ADAM_B1) * gv
    nv = ADAM_B2 * v_ref[...] + (1.0 - ADAM_B2) * (gv * gv)
    m_hat = nm / (1.0 - ADAM_B1 ** ADAM_STEP)
    v_hat = nv / (1.0 - ADAM_B2 ** ADAM_STEP)
    d_ref[...] = -ADAM_LR * (m_hat / (jnp.sqrt(v_hat) + ADAM_EPS) + ADAM_WD * w_ref[...])
    nm_ref[...] = nm
    nv_ref[...] = nv


def _adamw_many(ws, gs, ms, vs, name):
    n = len(ws)

    def body(*refs):
        ins, outs = refs[:4 * n], refs[4 * n:]
        for t in range(n):
            _adamw_update(ins[t], ins[n + t], ins[2 * n + t], ins[3 * n + t], outs[t], outs[n + t], outs[2 * n + t])

    shapes = [jax.ShapeDtypeStruct(w.shape, F32) for w in ws]
    out = pl.pallas_call(body, name=name, out_shape=shapes * 3, compiler_params=_cp())(*ws, *gs, *ms, *vs)
    return out[:n], out[n:2 * n], out[2 * n:]


def _adamw(w, g, m, v, name):
    R, C = w.shape
    tr = R
    for cand in (256, 352, 128, 64, 8):
        if R > cand and R % cand == 0:
            tr = cand
            break

    def body(w_ref, g_ref, m_ref, v_ref, d_ref, nm_ref, nv_ref):
        _adamw_update(w_ref, g_ref, m_ref, v_ref, d_ref, nm_ref, nv_ref)

    blk = pl.BlockSpec((tr, C), lambda i: (i, 0))
    sh = jax.ShapeDtypeStruct((R, C), F32)
    return pl.pallas_call(
        body, name=name, grid=(R // tr,), in_specs=[blk] * 4, out_specs=[blk] * 3, out_shape=[sh] * 3,
        compiler_params=_cp(("parallel",)),
    )(w, g, m, v)


def _adamw_3d(w, g, m, v, block, name):
    shape = w.shape
    grid = (shape[0] // block[0], shape[1] // block[1])
    assert shape[0] % block[0] == 0 and shape[1] % block[1] == 0 and block[2] == shape[2], (name, shape, block)

    def body(w_ref, g_ref, m_ref, v_ref, d_ref, nm_ref, nv_ref):
        _adamw_update(w_ref, g_ref, m_ref, v_ref, d_ref, nm_ref, nv_ref)

    blk = pl.BlockSpec(block, lambda i, j: (i, j, 0))
    sh = jax.ShapeDtypeStruct(shape, F32)
    return pl.pallas_call(
        body, name=name, grid=grid, in_specs=[blk] * 4, out_specs=[blk] * 3, out_shape=[sh] * 3,
        compiler_params=_cp(("parallel", "parallel")),
    )(w, g, m, v)


def _sum_slabs(x, name):
    n, R, C = x.shape
    tr = R
    for cand in (512, 256, 128, 64, 32, 16, 8):
        if R > cand and R % cand == 0:
            tr = cand
            break

    def body(x_ref, o_ref):
        acc = x_ref[0].astype(F32)
        for j in range(1, n):
            acc = acc + x_ref[j].astype(F32)
        o_ref[...] = acc

    return pl.pallas_call(
        body, name=name, grid=(R // tr,), in_specs=[pl.BlockSpec((n, tr, C), lambda i: (0, i, 0))],
        out_specs=pl.BlockSpec((tr, C), lambda i: (i, 0)), out_shape=jax.ShapeDtypeStruct((R, C), F32),
        compiler_params=_cp(("parallel",)),
    )(x)


def _multi_gather(xs, layers, name):
    nt = len(xs)
    shapes = [x.shape if lay is None else x.shape[1:] for x, lay in zip(xs, layers)]

    def body(*refs):
        x_refs, out_refs = refs[:nt], refs[nt:2 * nt]
        send_sems, recv_sems, local_sems = refs[2 * nt:]
        x_, y_, c_ = lax.axis_index("x"), lax.axis_index("y"), lax.axis_index("c")
        me, sibling = (x_, y_, c_), (x_, y_, 1 - c_)
        chips = [(1 - x_, y_), (x_, 1 - y_), (1 - x_, 1 - y_)]

        def own_block(t):
            return x_refs[t] if layers[t] is None else x_refs[t].at[layers[t]]

        def copy(t, k, block, to, own=False):
            px, py, pc = block
            dst = out_refs[t].at[4 * px + 2 * py + pc]
            return pltpu.make_async_remote_copy(
                src_ref=own_block(t) if own else dst, dst_ref=dst,
                send_sem=send_sems.at[t, k], recv_sem=recv_sems.at[t, k],
                device_id=to, device_id_type=pl.DeviceIdType.MESH)

        mine, first, passed = [], [], []
        for t in range(nt):
            mine.append(pltpu.make_async_copy(own_block(t), out_refs[t].at[4 * x_ + 2 * y_ + c_], local_sems.at[t]))
            mine[-1].start()
            first.append([copy(t, 1 + j, me, (*chip, c_), own=True) for j, chip in enumerate(chips)]
                         + [copy(t, 0, me, sibling, own=True)])
            for cp in first[-1]:
                cp.start()
        for t in range(nt):
            for j, chip in enumerate(chips):
                copy(t, 1 + j, (*chip, c_), me).wait_recv()
                passed.append(copy(t, 4 + j, (*chip, c_), sibling))
                passed[-1].start()
        for t in range(nt):
            copy(t, 0, sibling, me).wait_recv()
            for j, chip in enumerate(chips):
                copy(t, 4 + j, (*chip, 1 - c_), me).wait_recv()
        for cp in [c for f in first for c in f] + passed:
            cp.wait_send()
        for cp in mine:
            cp.wait()

    hbm = pl.BlockSpec(memory_space=pl.ANY)
    return pl.pallas_call(
        body, name=name, out_shape=[jax.ShapeDtypeStruct((N_DEV,) + tuple(s), x.dtype) for s, x in zip(shapes, xs)],
        in_specs=[hbm] * nt, out_specs=[hbm] * nt,
        scratch_shapes=[pltpu.SemaphoreType.DMA((nt, 7)), pltpu.SemaphoreType.DMA((nt, 7)),
                        pltpu.SemaphoreType.DMA((nt,))],
    )(*xs)


_HBM = pl.BlockSpec(memory_space=pltpu.HBM)
_SEM = pl.BlockSpec(memory_space=pltpu.SEMAPHORE)
_PEER_ORDER = (2, 4, 6, 3, 5, 7, 1)


def _split_copies(src_refs, land_refs, send_sems, recv_sems, layers, per_peer):
    x_, y_, c_ = lax.axis_index("x"), lax.axis_index("y"), lax.axis_index("c")
    me = 4 * x_ + 2 * y_ + c_
    copies = []
    for k in _PEER_ORDER:
        px, py, pc = x_ ^ ((k >> 2) & 1), y_ ^ ((k >> 1) & 1), c_ ^ (k & 1)
        peer = 4 * px + 2 * py + pc
        for t in range(len(src_refs)):
            if per_peer:
                src = src_refs[t].at[peer]
            else:
                src = src_refs[t] if layers[t] is None else src_refs[t].at[layers[t]]
            copies.append(pltpu.make_async_remote_copy(
                src_ref=src, dst_ref=land_refs[t].at[me],
                send_sem=send_sems.at[t * (N_DEV - 1) + k - 1], recv_sem=recv_sems.at[t * (N_DEV - 1) + k - 1],
                device_id=(px, py, pc), device_id_type=pl.DeviceIdType.MESH))
    return copies


def _own_copies(src_refs, land_refs, sems, layers, per_peer):
    nt = len(src_refs)
    me = 4 * lax.axis_index("x") + 2 * lax.axis_index("y") + lax.axis_index("c")
    copies = []
    for t in range(nt):
        if per_peer:
            src = src_refs[t].at[me]
        else:
            src = src_refs[t] if layers[t] is None else src_refs[t].at[layers[t]]
        copies.append(pltpu.make_async_copy(src, land_refs[t].at[me], sems.at[nt * (N_DEV - 1) + t]))
    return copies


def _split_start(srcs, layers, per_peer, after, name):
    nt = len(srcs)
    if per_peer:
        land_shapes = [s.shape for s in srcs]
    else:
        land_shapes = [(N_DEV,) + tuple(s.shape if lay is None else s.shape[1:]) for s, lay in zip(srcs, layers)]

    def body(*refs):
        src_refs, land_refs = refs[:nt], refs[nt:2 * nt]
        send_sems, recv_sems = refs[2 * nt + 1], refs[2 * nt + 2]
        token = refs[-1]
        for cp in _split_copies(src_refs, land_refs, send_sems, recv_sems, layers, per_peer):
            cp.start()
        for cp in _own_copies(src_refs, land_refs, send_sems, layers, per_peer):
            cp.start()
        token[...] = jnp.zeros_like(token)

    lands = [pltpu.with_memory_space_constraint(lax.empty(s, x.dtype), pltpu.HBM) for s, x in zip(land_shapes, srcs)]
    srcs = [pltpu.with_memory_space_constraint(x, pltpu.HBM) for x in srcs]
    out = pl.pallas_call(
        body, name=name,
        out_shape=(pltpu.SemaphoreType.DMA((nt * N_DEV,)), pltpu.SemaphoreType.DMA((nt * (N_DEV - 1),)),
                   *[pltpu.HBM(x.shape, x.dtype) for x in srcs], *[pltpu.HBM(s, x.dtype) for s, x in zip(land_shapes, srcs)],
                   jax.ShapeDtypeStruct((8, LANES), F32)),
        in_specs=[_HBM] * (2 * nt) + [pl.BlockSpec(memory_space=pl.ANY)],
        out_specs=(_SEM, _SEM, *([_HBM] * (2 * nt)), pl.BlockSpec(memory_space=pltpu.VMEM)),
        input_output_aliases={i: 2 + i for i in range(2 * nt)},
        compiler_params=pltpu.CompilerParams(has_side_effects=pltpu.SideEffectType.DATAFLOW_SIDE_EFFECTING),
    )(*srcs, *lands, after)
    return out[0], out[1], list(out[2:2 + nt]), list(out[2 + nt:2 + 2 * nt]), out[-1]


def _split_wait(started, layers, per_peer, after, name):
    send_sems, recv_sems, srcs, lands, _ = started
    nt = len(srcs)

    def body(*refs):
        src_refs, land_refs = refs[:nt], refs[nt:2 * nt]
        s_sems, r_sems = refs[2 * nt], refs[2 * nt + 1]
        for cp in _split_copies(src_refs, land_refs, s_sems, r_sems, layers, per_peer):
            cp.wait_send()
            cp.wait_recv()
        for cp in _own_copies(src_refs, land_refs, s_sems, layers, per_peer):
            cp.wait()

    out = pl.pallas_call(
        body, name=name,
        out_shape=tuple(pltpu.HBM(x.shape, x.dtype) for x in srcs + lands),
        in_specs=[_HBM] * (2 * nt) + [_SEM, _SEM, pl.BlockSpec(memory_space=pl.ANY)],
        out_specs=tuple([_HBM] * (2 * nt)),
        input_output_aliases={i: i for i in range(2 * nt)},
        compiler_params=pltpu.CompilerParams(has_side_effects=pltpu.SideEffectType.DATAFLOW_SIDE_EFFECTING),
    )(*srcs, *lands, send_sems, recv_sems, after)
    return list(out[nt:])


def _runs(mapping):
    runs, c, n = [], 0, len(mapping)
    while c < n:
        if mapping[c] is None:
            c += 1
            continue
        sid, d, lo = mapping[c][0], mapping[c][1] - c, c
        while c < n and mapping[c] is not None and mapping[c][0] == sid and mapping[c][1] - c == d:
            c += 1
        runs.append((lo, c, sid, d))
    return runs


def _tile_plan(mapping, src_widths):
    runs = _runs(mapping)
    plan = []
    for t in range(len(mapping) // LANES):
        pieces = []
        for lo, hi, sid, d in runs:
            lo_t, hi_t = max(lo, t * LANES), min(hi, (t + 1) * LANES)
            if lo_t >= hi_t:
                continue
            a = ((lo_t + d) // LANES) * LANES
            win = min(2 * LANES, src_widths[sid] - a)
            shift = t * LANES + d - a
            pieces.append((sid, a, win, shift, lo_t - t * LANES, hi_t - t * LANES))
        plan.append(pieces)
    return plan


def _reblock(srcs, src_views, outs, out_views, name):
    R = srcs[0].shape[-2]
    tr = min(512, R)
    widths = {sid: srcs[ai].shape[-1] for sid, (ai, _) in src_views.items()}
    plans = [(ai, li, _tile_plan(mapping, widths)) for ai, li, mapping in out_views]
    ns = len(srcs)

    def body(*refs):
        s_refs, o_refs = refs[:ns], refs[ns:]
        cache = {}

        def shift_matrix(win, shift, lo, hi):
            key = (win, shift, lo, hi)
            if key not in cache:
                r = lax.broadcasted_iota(jnp.int32, (win, LANES), 0)
                c = lax.broadcasted_iota(jnp.int32, (win, LANES), 1)
                hit = jnp.logical_and(r - c == shift, jnp.logical_and(c >= lo, c < hi))
                cache[key] = jnp.where(hit, 1.0, 0.0).astype(BF16)
            return cache[key]

        for ai, li, plan in plans:
            for t, pieces in enumerate(plan):
                acc = None
                whole = len(pieces) == 1 and pieces[0][3:] == (0, 0, LANES)
                for sid, a, win, shift, lo, hi in pieces:
                    sa, sl = src_views[sid]
                    if whole:
                        win = LANES
                    src = s_refs[sa][:, a:a + win] if sl is None else s_refs[sa][sl, :, a:a + win]
                    if whole:
                        acc = src
                    else:
                        part = jnp.dot(src, shift_matrix(win, shift, lo, hi), preferred_element_type=F32)
                        acc = part if acc is None else acc + part
                val = jnp.zeros((tr, LANES), BF16) if acc is None else acc.astype(BF16)
                if li is None:
                    o_refs[ai][:, t * LANES:(t + 1) * LANES] = val
                else:
                    o_refs[ai][li, :, t * LANES:(t + 1) * LANES] = val

    def spec(shape):
        if len(shape) == 2:
            return pl.BlockSpec((tr, shape[1]), lambda i: (i, 0))
        return pl.BlockSpec((shape[0], tr, shape[2]), lambda i: (0, i, 0))

    return pl.pallas_call(
        body, name=name, grid=(R // tr,), in_specs=[spec(s.shape) for s in srcs],
        out_specs=[spec(s) for s in outs], out_shape=[jax.ShapeDtypeStruct(s, BF16) for s in outs],
        compiler_params=_cp(("parallel",)),
    )(*srcs)


SHARDED = ("w_in", "w_gate_up", "w_proj_attn", "w_proj_pool", "w_proj_conv", "w_out", "w_down")
WEIGHT_ORDER = ("attn_norm", "w_in", "b_forget", "b_gate", "w_proj_attn", "pool_w", "pool_scale", "w_proj_pool",
                "conv_w", "w_proj_conv", "w_out", "ffn_norm", "w_gate_up", "w_down", "final_norm")
IN_SHARD, IN_SHARD_PAD = IN_COLS // N_DEV, 896
GU_SHARD, GU_SHARD_PAD = 2 * FFN_HIDDEN // N_DEV, 768


def _w_in_col(c):
    if c < OFF_QKV:
        return c + 3592
    if c < OFF_U:
        base, off = (0, OFF_QKV) if c < OFF_CONV else (2056, OFF_CONV)
        j, t = divmod(c - off, TRIPLE)
        which, e = divmod(t, LANES)
        return base + which * BRANCH_W + j * LANES + e
    return c - OFF_U + 1544


def _w_in_full(gathered, name):
    main = [divmod(_w_in_col(c), IN_SHARD) for c in range(MAIN_COLS)]
    fcols = [divmod(1536 + c, IN_SHARD) if c < N_HEADS else None for c in range(LANES)]
    R = gathered.shape[1]
    return _reblock([gathered], {i: (0, i) for i in range(N_DEV)}, [(R, MAIN_COLS), (R, LANES)],
                    [(0, None, main), (1, None, fcols)], name)


def _w_in_slabs(dmain, dwf, name):
    inv = {_w_in_col(c): ("m", c) for c in range(MAIN_COLS)}
    inv.update({1536 + c: ("f", c) for c in range(N_HEADS)})
    views = []
    for i in range(N_DEV):
        mapping = [inv[IN_SHARD * i + j] if j < IN_SHARD else None for j in range(IN_SHARD_PAD)]
        views.append((0, i, mapping))
    R = dmain.shape[0]
    return _reblock([dmain, dwf], {"m": (0, None), "f": (1, None)}, [(N_DEV, R, IN_SHARD_PAD)], views, name)[0]


def _w_gu_full(gathered, name):
    mapping = [divmod(_gu_col(c), GU_SHARD) for c in range(2 * FFN_HIDDEN)]
    R = gathered.shape[1]
    return _reblock([gathered], {i: (0, i) for i in range(N_DEV)}, [(R, 2 * FFN_HIDDEN)], [(0, None, mapping)], name)[0]


def _w_gu_slabs(dw, name):
    inv = {_gu_col(c): c for c in range(2 * FFN_HIDDEN)}
    views = [(0, i, [("w", inv[GU_SHARD * i + j]) if j < GU_SHARD else None for j in range(GU_SHARD_PAD)])
             for i in range(N_DEV)]
    R = dw.shape[0]
    return _reblock([dw], {"w": (0, None)}, [(N_DEV, R, GU_SHARD_PAD)], views, name)[0]


def _layer_fwd(x, W, n_seq, l, h1=None, next_norm=None):
    T = x.shape[0]
    sfx = f"_l{l}"
    if h1 is None:
        h1 = _rms_fwd(x, W["attn_norm"], "rms1" + sfx)
    proj, f = _matmul(h1, W["w_main"], mode="nn", out_dtype=BF16, name="proj_main" + sfx, side=(W["w_f"], F32))
    qa, ka, va = _fox_prep(f, W["b_forget"], proj, n_seq, "fox_prep" + sfx)
    oa, oa32, lse = _attn_fwd2(qa, ka, va, n_seq, "attn_fwd" + sfx)
    if "late" in W:
        W.update(W.pop("late")(oa))
    ob = _pool_fwd(proj, W["pool_w"], W["pool_scale"], n_seq, "pool_fwd" + sfx)
    oc = _conv_fwd(proj, W["conv_w"], n_seq, "conv_fwd" + sfx)
    mixed = _mix_fwd(oa, ob, oc, W["w_proj_attn"], W["w_proj_pool"], W["w_proj_conv"], proj, W["b_gate"],
                     "mix_fwd" + sfx)
    x2, h2 = _matmul(mixed, W["w_out"], mode="nn", out_dtype=F32, name="out_proj" + sfx, tm=1024, tn=1024,
                     residual=x, rms_g=W["ffn_norm"])
    ab, s = _gate_up_swiglu(h2, W["w_gate_up"], "gate_up" + sfx)
    x3 = _matmul(s, W["w_down"], mode="nn", out_dtype=F32, name="down" + sfx, tm=1024, tn=1024, tk=1408,
                 residual=x2, rms_g=next_norm)
    x3, h1_next = x3 if next_norm is not None else (x3, None)
    saved = dict(x=x, h1=h1, proj=proj, f=f, qa=qa, ka=ka, oa=oa, oa32=oa32, lse=lse, ob=ob, oc=oc, mixed=mixed, x2=x2,
                 h2=h2, ab=ab, s=s)
    return x3, saved, h1_next


def _layer_bwd(dx3, dx3b, W, sv, n_seq, l, stage=None):
    T = dx3.shape[0]
    sfx = f"_l{l}"
    G = {}
    stage = stage or (lambda l, group, G, W: W)
    dab = _swiglu_bwd_fused(dx3b, W["w_down"], sv["ab"], "d_ab" + sfx)
    G["w_down"] = _matmul(sv["s"], dx3b, mode="tn", out_dtype=BF16, name="dw_down" + sfx, tm=256, tn=1024)
    dh2 = _matmul(dab, W["w_gate_up"], mode="nt", out_dtype=BF16, name="d_h2" + sfx, tm=1024, tn=1024, tk=1408)
    G["w_gate_up"] = _matmul(sv["h2"], dab, mode="tn", out_dtype=BF16, name="dw_gate_up" + sfx, tm=1024)
    W = stage(l, "ffn", G, W)
    dx2, dx2b, G["ffn_norm"] = _rms_bwd(sv["x2"], W["ffn_norm"], dh2, dx3, "rms2_bwd" + sfx)
    dmixed = _matmul(dx2b, W["w_out"], mode="nt", out_dtype=BF16, name="d_mixed" + sfx)
    G["w_out"] = _matmul(sv["mixed"], dx2b, mode="tn", out_dtype=BF16, name="dw_out" + sfx, tm=1024)
    dya, dyb, dyc, dproj, G["b_gate"] = _mix_bwd(sv["oa"], sv["ob"], sv["oc"], W["w_proj_attn"], W["w_proj_pool"],
                                                 W["w_proj_conv"], sv["proj"], W["b_gate"], dmixed, "mix_bwd" + sfx)
    douts = {}
    for br, dy, o in (("attn", dya, sv["oa"]), ("pool", dyb, sv["ob"]), ("conv", dyc, sv["oc"])):
        douts[br] = _matmul(dy, W["w_proj_" + br], mode="nt", out_dtype=BF16, name=f"d_{br}_out" + sfx)
        G["w_proj_" + br] = _matmul(o, dy, mode="tn", out_dtype=BF16, name=f"dw_proj_{br}" + sfx, tm=512)
    W = stage(l, "mix", G, W)
    dproj, G["conv_w"] = _conv_bwd(sv["proj"], douts["conv"], W["conv_w"], dproj, n_seq, "conv_bwd" + sfx)
    dproj, G["pool_w"], G["pool_scale"] = _pool_bwd(sv["proj"], douts["pool"], W["pool_w"], W["pool_scale"], dproj,
                                                    n_seq, "pool_bwd" + sfx)
    dproj, dFk = _attn_bwd(sv["qa"], sv["ka"], sv["proj"], douts["attn"], sv["oa32"], sv["lse"], dproj, n_seq,
                           "attn_bwd" + sfx)
    dF = jnp.pad(dFk.reshape(N_HEADS, T).T, ((0, 0), (0, LANES - N_HEADS)))
    df, G["b_forget"] = _fox_cumsum_bwd(sv["f"], W["b_forget"], dF, n_seq, "fox_cumsum_bwd" + sfx)
    G["w_main"], G["w_f"] = _matmul(sv["h1"], dproj, mode="tn", out_dtype=BF16, name="dw_main" + sfx, tm=1024,
                                    side=(df, BF16))
    W = stage(l, "w_in", G, W)
    dh1 = _matmul(dproj, W["w_main"], mode="nt", out_dtype=BF16, name="d_h1_main" + sfx, tm=1024, tn=1024, tk=1664,
                  extra=(df, W["w_f"]))
    dx, dxb, G["attn_norm"] = _rms_bwd(sv["x"], W["attn_norm"], dh1, dx2, "rms1_bwd" + sfx)
    return dx, dxb, G


def _replicated_operands(rep, l):
    W = {}
    W["attn_norm"], W["ffn_norm"] = rep["attn_norm"][l], rep["ffn_norm"][l]
    W["b_forget"] = jnp.pad(rep["b_forget"][l].reshape(1, N_HEADS), ((0, 0), (0, LANES - N_HEADS)))
    W["b_gate"] = rep["b_gate"][l].reshape(1, GATE_W)
    W["pool_w"] = rep["pool_w"][l].astype(BF16)
    W["pool_scale"] = rep["pool_scale"][l].reshape(1, BRANCH_W)
    return W


def _local_step(x, target, get_W, attn_norms, final_norm, stage=None):
    n_seq, S, Dm = x.shape
    T = n_seq * S
    xt = x.reshape(T, Dm)
    saved, Ws, h1 = [], [], None
    for l in range(DEPTH):
        Ws.append(get_W(l, xt))
        next_norm = attn_norms[l + 1] if l + 1 < DEPTH else None
        xt, sv, h1 = _layer_fwd(xt, Ws[l], n_seq, l, h1, next_norm)
        saved.append(sv)
    loss, dx, dxb, g_final = _loss_head(xt, final_norm, target.reshape(T, Dm), "loss_head")
    grads = [None] * DEPTH
    for l in reversed(range(DEPTH)):
        dx, dxb, grads[l] = _layer_bwd(dx, dxb, Ws[l], saved[l], n_seq, l, stage)
    return loss, dx.reshape(n_seq, S, Dm), grads, g_final


def _padded_shards(weights):
    pads = {"w_in": IN_SHARD_PAD - IN_SHARD, "w_gate_up": GU_SHARD_PAD - GU_SHARD}
    return {n: jnp.pad(weights[n], ((0, 0), (0, 0), (0, pads.get(n, 0)))).astype(BF16) for n in SHARDED}


def _full_operands(g, l):
    W = {}
    if "w_in" in g:
        W["w_main"], W["w_f"] = _w_in_full(g["w_in"], f"w_in_full_l{l}")
    if "w_gate_up" in g:
        W["w_gate_up"] = _w_gu_full(g["w_gate_up"], f"w_gate_up_full_l{l}")
    for n in ("w_proj_attn", "w_proj_pool", "w_proj_conv"):
        if n in g:
            W[n] = jnp.transpose(g[n], (1, 0, 2)).reshape(BRANCH_W, D_MODEL)
    if "w_out" in g:
        W["w_out"] = g["w_out"].reshape(D_MODEL, D_MODEL)
    if "w_down" in g:
        W["w_down"] = g["w_down"].reshape(FFN_HIDDEN, D_MODEL)
    return W


GRAD_GROUPS = {"ffn": ("w_down", "w_gate_up"),
               "mix": ("w_out", "w_proj_attn", "w_proj_pool", "w_proj_conv"),
               "w_in": ("w_in",)}


def _grad_slabs(G, n, l):
    if n == "w_in":
        return _w_in_slabs(G["w_main"], G["w_f"], f"w_in_slabs_l{l}")
    if n == "w_gate_up":
        return _w_gu_slabs(G["w_gate_up"], f"w_gate_up_slabs_l{l}")
    if n == "w_out":
        return G["w_out"].reshape(N_DEV, D_MODEL // N_DEV, D_MODEL)
    if n == "w_down":
        return G["w_down"].reshape(N_DEV, FFN_HIDDEN // N_DEV, D_MODEL)
    return jnp.transpose(G[n].reshape(BRANCH_W, N_DEV, D_MODEL // N_DEV), (1, 0, 2))


def _sum_layer_grads(recv, l):
    out = {n: _sum_slabs(r, f"sum_{n}_l{l}") for n, r in recv.items()}
    if "w_in" in out:
        out["w_in"] = out["w_in"][:, :IN_SHARD]
    if "w_gate_up" in out:
        out["w_gate_up"] = out["w_gate_up"][:, :GU_SHARD]
    return out


def _sum_small(xs, name):
    def body(*refs):
        for x_ref, o_ref in zip(refs[:len(xs)], refs[len(xs):]):
            acc = x_ref[0]
            for j in range(1, N_DEV):
                acc = acc + x_ref[j]
            o_ref[...] = acc

    return pl.pallas_call(
        body, name=name, out_shape=[jax.ShapeDtypeStruct(x.shape[1:], F32) for x in xs],
        compiler_params=_cp(),
    )(*xs)


def _as_2d(a):
    if a.ndim == 1:
        return a.reshape(1, -1)
    return a.reshape(-1, a.shape[-1])


def kernel(x, attn_norm, w_in, b_forget, b_gate, w_proj_attn, pool_w, pool_scale, w_proj_pool, conv_w, w_proj_conv, w_out, ffn_norm, w_gate_up, w_down, final_norm, loss_target, m_attn_norm, m_w_in, m_b_forget, m_b_gate, m_w_proj_attn, m_pool_w, m_pool_scale, m_w_proj_pool, m_conv_w, m_w_proj_conv, m_w_out, m_ffn_norm, m_w_gate_up, m_w_down, m_final_norm, v_attn_norm, v_w_in, v_b_forget, v_b_gate, v_w_proj_attn, v_pool_w, v_pool_scale, v_w_proj_pool, v_conv_w, v_w_proj_conv, v_w_out, v_ffn_norm, v_w_gate_up, v_w_down, v_final_norm):
    weights = dict(attn_norm=attn_norm, w_in=w_in, b_forget=b_forget, b_gate=b_gate, w_proj_attn=w_proj_attn,
                   pool_w=pool_w, pool_scale=pool_scale, w_proj_pool=w_proj_pool, conv_w=conv_w,
                   w_proj_conv=w_proj_conv, w_out=w_out, ffn_norm=ffn_norm, w_gate_up=w_gate_up, w_down=w_down,
                   final_norm=final_norm)
    moments_m = dict(attn_norm=m_attn_norm, w_in=m_w_in, b_forget=m_b_forget, b_gate=m_b_gate,
                     w_proj_attn=m_w_proj_attn, pool_w=m_pool_w, pool_scale=m_pool_scale, w_proj_pool=m_w_proj_pool,
                     conv_w=m_conv_w, w_proj_conv=m_w_proj_conv, w_out=m_w_out, ffn_norm=m_ffn_norm,
                     w_gate_up=m_w_gate_up, w_down=m_w_down, final_norm=m_final_norm)
    moments_v = dict(attn_norm=v_attn_norm, w_in=v_w_in, b_forget=v_b_forget, b_gate=v_b_gate,
                     w_proj_attn=v_w_proj_attn, pool_w=v_pool_w, pool_scale=v_pool_scale, w_proj_pool=v_w_proj_pool,
                     conv_w=v_conv_w, w_proj_conv=v_w_proj_conv, w_out=v_w_out, ffn_norm=v_ffn_norm,
                     w_gate_up=v_w_gate_up, w_down=v_w_down, final_norm=v_final_norm)

    sh = _padded_shards(weights)
    names = list(SHARDED)
    rest = [n for n in names if n != "w_in"]
    me = 4 * lax.axis_index("x") + 2 * lax.axis_index("y") + lax.axis_index("c")
    w_in0, conv_all = _multi_gather([sh["w_in"], conv_w], [0, None], "gather_w_in_l0")
    started, after = {}, w_in0
    for l in range(DEPTH):
        for group, gnames in (("w_in", ["w_in"]), ("rest", rest)):
            if (l, group) != (0, "w_in"):
                started[l, group] = _split_start([sh[n] for n in gnames], [l] * len(gnames), False, after,
                                                 f"gather_start_{group}_l{l}")
                after = started[l, group][4]
    last_token = after

    def get_W(l, xt):
        if l == 0:
            w_in = w_in0
        else:
            w_in = _split_wait(started[l, "w_in"], [l], False, xt, f"gather_wait_w_in_l{l}")[0]
        W = _full_operands({"w_in": w_in}, l)

        def late(after):
            lands = _split_wait(started[l, "rest"], [l] * len(rest), False, after, f"gather_wait_rest_l{l}")
            return _full_operands(dict(zip(rest, lands)), l)

        W["late"] = late
        W.update(_replicated_operands(weights, l))
        W["conv_w"] = jnp.transpose(conv_all[:, l], (1, 0, 2)).reshape(CONV_K, BRANCH_W)
        if l == 0:
            W["attn_norm"] = W["attn_norm"] + last_token[0, 0]
        return W

    exchanges = []

    def stage(l, group, G, W):
        gnames = GRAD_GROUPS[group]
        slabs = [_grad_slabs(G, n, l) for n in gnames]
        started = _split_start(slabs, None, True, slabs[0], f"exchange_start_{group}_l{l}")
        exchanges.append((l, group, gnames, slabs, started))
        tie = {"ffn": "ffn_norm", "mix": "conv_w", "w_in": "w_f"}[group]
        W = dict(W)
        W[tie] = W[tie] + started[4][0, 0].astype(W[tie].dtype)
        return W

    loss_part, grad_x, grads, g_final = _local_step(x, loss_target, get_W, attn_norm, final_norm, stage)

    def finish_exchange(ex, after):
        l, group, gnames, slabs, started = ex
        lands = _split_wait(started, None, True, after, f"exchange_wait_{group}_l{l}")
        grads[l].update(_sum_layer_grads(dict(zip(gnames, lands)), l))

    def zero_after(a):
        return jnp.minimum(jnp.abs(a.reshape(-1)[0]), 0.0)

    *early, last_exchange = exchanges
    for ex in early:
        finish_exchange(ex, grad_x)
    deltas, new_m, new_v, gw = {}, {}, {}, {}
    views = {"w_in": ((2, 0, 1), (1, 2, 0), (49, DEPTH, D_MODEL)),
             "w_gate_up": ((0, 2, 1), (0, 2, 1), (1, GU_SHARD // 2, D_MODEL))}

    def update_sharded(n):
        gw[n] = jnp.stack([grads[l][n] for l in range(DEPTH)])
        if n in views:
            perm, inv, block = views[n]
            gt = jnp.transpose(gw[n], perm)
            d, nm, nv = _adamw_3d(jnp.transpose(weights[n], perm), gt, jnp.transpose(moments_m[n], perm),
                                  jnp.transpose(moments_v[n], perm), block, "adamw_" + n)
            deltas[n], new_m[n], new_v[n] = (jnp.transpose(a, inv) for a in (d, nm, nv))
            gw[n] = jnp.transpose(gt, inv)
            return
        shape = weights[n].shape
        d, nm, nv = _adamw(_as_2d(weights[n]), _as_2d(gw[n]), _as_2d(moments_m[n]), _as_2d(moments_v[n]),
                           "adamw_" + n)
        deltas[n], new_m[n], new_v[n] = d.reshape(shape), nm.reshape(shape), nv.reshape(shape)

    for n in SHARDED:
        if n != "w_in":
            update_sharded(n)

    small = ("attn_norm", "b_forget", "b_gate", "pool_w", "pool_scale", "ffn_norm", "conv_w")
    loss_part = loss_part + zero_after(deltas["w_gate_up"]) + zero_after(deltas["w_down"])
    parts = [jnp.stack([grads[l][n] for l in range(DEPTH)]) for n in small] + [g_final, loss_part]
    gathered = _multi_gather(parts, [None] * len(parts), "gather_small_grads")
    summed = _sum_small(gathered, "sum_small_grads")
    for n, s in zip(small, summed):
        gw[n] = s
    gw["attn_norm"], gw["ffn_norm"] = gw["attn_norm"][:, 0], gw["ffn_norm"][:, 0]
    gw["b_forget"] = gw["b_forget"][:, 0, :N_HEADS]
    gw["b_gate"], gw["pool_scale"] = gw["b_gate"][:, 0], gw["pool_scale"][:, 0]
    gw["conv_w"] = lax.dynamic_slice_in_dim(gw["conv_w"], me * (BRANCH_W // N_DEV), BRANCH_W // N_DEV, axis=2)
    gw["final_norm"] = summed[-2][0]
    loss = summed[-1][0, 0]

    rest_names = [n for n in WEIGHT_ORDER if n not in SHARDED]
    ds, nms, nvs = _adamw_many(*[[_as_2d(src[n]) for n in rest_names] for src in (weights, gw, moments_m, moments_v)],
                               "adamw_small")
    for n, d, nm, nv in zip(rest_names, ds, nms, nvs):
        shape = weights[n].shape
        deltas[n], new_m[n], new_v[n] = d.reshape(shape), nm.reshape(shape), nv.reshape(shape)

    finish_exchange(last_exchange, deltas["pool_w"])
    update_sharded("w_in")

    return (loss, grad_x, *[gw[n] for n in WEIGHT_ORDER], *[deltas[n] for n in WEIGHT_ORDER],
            *[new_m[n] for n in WEIGHT_ORDER], *[new_v[n] for n in WEIGHT_ORDER])
```

```python
import functools

import jax
import jax.numpy as jnp
from jax import lax
from jax.experimental import pallas as pl
from jax.experimental.pallas import tpu as pltpu

F32 = jnp.float32
BF16 = jnp.bfloat16

N_DEV = 8
D_MODEL = 1024
DEPTH = 2
N_HEADS = 8
HEAD_DIM = 64
BRANCH_W = 512
POOL_WINDOWS = (2, 4, 8, 16)
POOL_GD = 128
CONV_K = 3
FFN_HIDDEN = 2816
GATE_W = 3 * D_MODEL
IN_COLS = 6664
MAIN_COLS = GATE_W + 7 * BRANCH_W
RMS_EPS = 1e-6
NEG_INF = -1e30

ADAM_LR = 0.001
ADAM_B1 = 0.9
ADAM_B2 = 0.999
ADAM_EPS = 1e-08
ADAM_WD = 0.01
ADAM_STEP = 10

LANES = 128
VMEM_LIMIT = 56 * 1024 * 1024
CUM_BLK = 256

TRIPLE = 3 * LANES
OFF_G, OFF_QKV, OFF_CONV, OFF_U = 0, 3072, 4608, 6144


def _cp(sem=None):
    return pltpu.CompilerParams(dimension_semantics=sem, vmem_limit_bytes=VMEM_LIMIT)


def _sigmoid(z):
    return 1.0 / (1.0 + jnp.exp(-z))


def _matmul(a, b, *, mode, out_dtype, name, tm=2048, tn=512, tk=None, residual=None, rms_g=None, side=None,
            extra=None):
    if mode == "nn":
        (M, K), N = a.shape, b.shape[1]
    elif mode == "nt":
        (M, K), N = a.shape, b.shape[0]
    else:
        (K, M), N = a.shape, b.shape[1]
    tm, tn, tk = min(tm, M), min(tn, N), K if tk is None else min(tk, K)
    assert M % tm == 0 and N % tn == 0 and K % tk == 0, (name, M, N, K, tm, tn, tk)
    nk = K // tk
    if mode == "nn":
        a_spec = pl.BlockSpec((tm, tk), lambda i, j, k: (i, k))
        b_spec = pl.BlockSpec((tk, tn), lambda i, j, k: (k, j))
        dims = (((1,), (0,)), ((), ()))
    elif mode == "nt":
        a_spec = pl.BlockSpec((tm, tk), lambda i, j, k: (i, k))
        b_spec = pl.BlockSpec((tn, tk), lambda i, j, k: (j, k))
        dims = (((1,), (1,)), ((), ()))
    else:
        a_spec = pl.BlockSpec((tk, tm), lambda i, j, k: (k, i))
        b_spec = pl.BlockSpec((tk, tn), lambda i, j, k: (k, j))
        dims = (((0,), (0,)), ((), ()))
    o_spec = pl.BlockSpec((tm, tn), lambda i, j, k: (i, j))
    has_res, has_norm, has_side, has_extra = (v is not None for v in (residual, rms_g, side, extra))
    assert not has_norm or tn == N, (name, tn, N)
    assert not has_side or (nk == 1 and mode != "nt"), name

    in_specs, args = [a_spec, b_spec], [a, b]
    out_specs, out_shape = [o_spec], [jax.ShapeDtypeStruct((M, N), out_dtype)]
    if has_res:
        in_specs.append(o_spec)
        args.append(residual)
    if has_norm:
        in_specs.append(pl.BlockSpec((1, N), lambda i, j, k: (0, 0)))
        args.append(rms_g.reshape(1, N))
        out_specs.append(o_spec)
        out_shape.append(jax.ShapeDtypeStruct((M, N), BF16))
    if has_side:
        b_side, side_dtype = side
        ns = b_side.shape[1]
        in_specs.append(pl.BlockSpec((K, ns), lambda i, j, k: (0, 0)))
        args.append(b_side)
        out_specs.append(pl.BlockSpec((tm, ns), lambda i, j, k: (i, 0)))
        out_shape.append(jax.ShapeDtypeStruct((M, ns), side_dtype))
    if has_extra:
        a2, b2 = extra
        in_specs += [pl.BlockSpec((tm, a2.shape[1]), lambda i, j, k: (i, 0)),
                     pl.BlockSpec((tn, b2.shape[1]), lambda i, j, k: (j, 0))]
        args += [a2, b2]
    n_in = len(args)

    def body(*refs):
        ins, outs = list(refs[2:n_in]), list(refs[n_in:n_in + len(out_shape)])
        a_ref, b_ref = refs[:2]
        r_ref = ins.pop(0) if has_res else None
        g_ref = ins.pop(0) if has_norm else None
        bs_ref = ins.pop(0) if has_side else None
        a2_ref, b2_ref = (ins.pop(0), ins.pop(0)) if has_extra else (None, None)
        o_ref = outs.pop(0)
        h_ref = outs.pop(0) if has_norm else None
        so_ref = outs.pop(0) if has_side else None

        def finish(acc):
            if has_res:
                acc = acc + r_ref[...].astype(F32)
            if has_extra:
                acc = acc + lax.dot_general(a2_ref[...], b2_ref[...], (((1,), (1,)), ((), ())),
                                            preferred_element_type=F32)
            o_ref[...] = acc.astype(out_dtype)
            if has_norm:
                r = lax.rsqrt(jnp.mean(acc * acc, axis=-1, keepdims=True) + RMS_EPS)
                h_ref[...] = ((acc * r) * g_ref[...]).astype(BF16)

        if has_side:
            @pl.when(pl.program_id(1) == 0)
            def _():
                side_dims = (((1,), (0,)), ((), ())) if mode == "nn" else dims
                so_ref[...] = lax.dot_general(a_ref[...], bs_ref[...], side_dims,
                                              preferred_element_type=F32).astype(so_ref.dtype)

        prod = lax.dot_general(a_ref[...], b_ref[...], dims, preferred_element_type=F32)
        if nk == 1:
            finish(prod)
            return
        acc_ref = refs[-1]
        k = pl.program_id(2)

        @pl.when(k == 0)
        def _():
            acc_ref[...] = prod

        @pl.when(jnp.logical_and(k > 0, k < nk - 1))
        def _():
            acc_ref[...] += prod

        @pl.when(k == nk - 1)
        def _():
            finish(acc_ref[...] + prod)

    single = len(out_shape) == 1
    return pl.pallas_call(
        body, name=name, grid=(M // tm, N // tn, nk), in_specs=in_specs,
        out_specs=out_specs[0] if single else out_specs, out_shape=out_shape[0] if single else out_shape,
        scratch_shapes=[pltpu.VMEM((tm, tn), F32)] if nk > 1 else [],
        compiler_params=_cp(("parallel", "arbitrary" if has_side else "parallel", "arbitrary")),
    )(*args)


def _rms_fwd(x, g, name):
    T, Dm = x.shape
    tm = min(512, T)

    def body(x_ref, g_ref, h_ref):
        xf = x_ref[...]
        r = lax.rsqrt(jnp.mean(xf * xf, axis=-1, keepdims=True) + RMS_EPS)
        h_ref[...] = ((xf * r) * g_ref[...]).astype(BF16)

    return pl.pallas_call(
        body, name=name, grid=(T // tm,),
        in_specs=[pl.BlockSpec((tm, Dm), lambda i: (i, 0)), pl.BlockSpec((1, Dm), lambda i: (0, 0))],
        out_specs=pl.BlockSpec((tm, Dm), lambda i: (i, 0)),
        out_shape=jax.ShapeDtypeStruct((T, Dm), BF16),
        compiler_params=_cp(("parallel",)),
    )(x, g.reshape(1, Dm))


def _rms_bwd(x, g, dh, dres, name):
    T, Dm = x.shape
    tm = min(512, T)

    def body(x_ref, g_ref, dh_ref, dres_ref, dx_ref, dxb_ref, dg_ref):
        i = pl.program_id(0)
        xf = x_ref[...]
        r = lax.rsqrt(jnp.mean(xf * xf, axis=-1, keepdims=True) + RMS_EPS)
        xn = xf * r
        dhf = dh_ref[...].astype(F32)
        dxn = dhf * g_ref[...]
        c = jnp.mean(dxn * xn, axis=-1, keepdims=True)
        dx = dres_ref[...] + r * (dxn - xn * c)
        dx_ref[...] = dx
        dxb_ref[...] = dx.astype(BF16)
        part = jnp.sum(dhf * xn, axis=0, keepdims=True)

        @pl.when(i == 0)
        def _():
            dg_ref[...] = part

        @pl.when(i > 0)
        def _():
            dg_ref[...] += part

    row = pl.BlockSpec((tm, Dm), lambda i: (i, 0))
    vec = pl.BlockSpec((1, Dm), lambda i: (0, 0))
    return pl.pallas_call(
        body, name=name, grid=(T // tm,), in_specs=[row, vec, row, row], out_specs=[row, row, vec],
        out_shape=[jax.ShapeDtypeStruct((T, Dm), F32), jax.ShapeDtypeStruct((T, Dm), BF16),
                   jax.ShapeDtypeStruct((1, Dm), F32)],
        compiler_params=_cp(("arbitrary",)),
    )(x, g.reshape(1, Dm), dh, dres)


def _loss_head(x, g, target, name):
    T, Dm = x.shape
    tm = min(512, T)

    def body(x_ref, g_ref, t_ref, loss_ref, dx_ref, dxb_ref, dg_ref):
        i = pl.program_id(0)
        xf = x_ref[...]
        gv = g_ref[...]
        r = lax.rsqrt(jnp.mean(xf * xf, axis=-1, keepdims=True) + RMS_EPS)
        xn = xf * r
        diff = xn * gv - t_ref[...]
        per_tok = jnp.mean(diff * diff, axis=-1, keepdims=True)
        lpart = 0.5 * jnp.sum(per_tok, axis=0, keepdims=True) + jnp.zeros((1, LANES), F32)
        dy = diff * (1.0 / Dm)
        dxn = dy * gv
        c = jnp.mean(dxn * xn, axis=-1, keepdims=True)
        dx = r * (dxn - xn * c)
        dx_ref[...] = dx
        dxb_ref[...] = dx.astype(BF16)
        part = jnp.sum(dy * xn, axis=0, keepdims=True)

        @pl.when(i == 0)
        def _():
            dg_ref[...] = part
            loss_ref[...] = lpart

        @pl.when(i > 0)
        def _():
            dg_ref[...] += part
            loss_ref[...] += lpart

    row = pl.BlockSpec((tm, Dm), lambda i: (i, 0))
    vec = pl.BlockSpec((1, Dm), lambda i: (0, 0))
    lsp = pl.BlockSpec((1, LANES), lambda i: (0, 0))
    return pl.pallas_call(
        body, name=name, grid=(T // tm,), in_specs=[row, vec, row], out_specs=[lsp, row, row, vec],
        out_shape=[jax.ShapeDtypeStruct((1, LANES), F32), jax.ShapeDtypeStruct((T, Dm), F32),
                   jax.ShapeDtypeStruct((T, Dm), BF16), jax.ShapeDtypeStruct((1, Dm), F32)],
        compiler_params=_cp(("arbitrary",)),
    )(x, g.reshape(1, Dm), target)


def _split_bf16(v):
    hi = v.astype(BF16)
    r1 = v - hi.astype(F32)
    mid = r1.astype(BF16)
    lo = (r1 - mid.astype(F32)).astype(BF16)
    return hi, mid, lo


def _tri_dot(tri, v):
    hi, mid, lo = _split_bf16(v)
    dot = functools.partial(jnp.dot, preferred_element_type=F32)
    return dot(tri, hi) + dot(tri, mid) + dot(tri, lo)


def _log_sigmoid(z):
    return jnp.minimum(z, 0.0) - jnp.log(1.0 + jnp.exp(-jnp.abs(z)))


def _fox_cumsum_bwd(f, bf, dF, n_seq, name):
    T = f.shape[0]
    S = T // n_seq
    c = min(CUM_BLK, S)

    def body(f_ref, b_ref, dF_ref, df_ref, db_ref):
        b = pl.program_id(0)
        ri = lax.broadcasted_iota(jnp.int32, (c, c), 0)
        ci = lax.broadcasted_iota(jnp.int32, (c, c), 1)
        tri = (ri <= ci).astype(BF16)
        carry = jnp.zeros((1, LANES), F32)
        dbp = jnp.zeros((1, LANES), F32)
        for j in reversed(range(S // c)):
            dFc = dF_ref[j * c:(j + 1) * c, :]
            dlf = _tri_dot(tri, dFc) + carry
            carry = carry + jnp.sum(dFc, axis=0, keepdims=True)
            z = f_ref[j * c:(j + 1) * c, :] + b_ref[...]
            dz = dlf * _sigmoid(-z)
            df_ref[j * c:(j + 1) * c, :] = dz.astype(BF16)
            dbp = dbp + jnp.sum(dz, axis=0, keepdims=True)

        @pl.when(b == 0)
        def _():
            db_ref[...] = dbp

        @pl.when(b > 0)
        def _():
            db_ref[...] += dbp

    blk = pl.BlockSpec((S, LANES), lambda b: (b, 0))
    vec = pl.BlockSpec((1, LANES), lambda b: (0, 0))
    return pl.pallas_call(
        body, name=name, grid=(n_seq,), in_specs=[blk, vec, blk], out_specs=[blk, vec],
        out_shape=[jax.ShapeDtypeStruct((T, LANES), BF16), jax.ShapeDtypeStruct((1, LANES), F32)],
        compiler_params=_cp(("arbitrary",)),
    )(f, bf, dF)


def _pair_masks():
    lane = lax.broadcasted_iota(jnp.int32, (1, LANES), 1)
    lo = lane < HEAD_DIM
    return lo, jnp.logical_not(lo)


AUG0 = HEAD_DIM
Q_TILE, K_CHUNK, ROW_GROUP = 512, 256, 64


def _fox_prep(f, bf, proj, n_seq, name):
    T = f.shape[0]
    S = T // n_seq
    c = min(CUM_BLK, S)

    def body(f_ref, b_ref, qkv_ref, qa_ref, ka_ref, va_ref):
        ri = lax.broadcasted_iota(jnp.int32, (c, c), 0)
        ci = lax.broadcasted_iota(jnp.int32, (c, c), 1)
        tri = (ri >= ci).astype(BF16)
        lane = lax.broadcasted_iota(jnp.int32, (c, LANES), 1)
        carry = jnp.zeros((1, LANES), F32)
        for j in range(S // c):
            rows = slice(j * c, (j + 1) * c)
            lf = _log_sigmoid(f_ref[rows, :] + b_ref[...])
            Fc = _tri_dot(tri, lf) + carry
            carry = carry + jnp.sum(lf, axis=0, keepdims=True)
            for h in range(N_HEADS):
                col = jnp.sum(jnp.where(lane == h, Fc, 0.0), axis=-1, keepdims=True)
                hi = col.astype(BF16).astype(F32)
                r1 = col - hi
                mid = r1.astype(BF16).astype(F32)
                lo = r1 - mid
                ones_q = jnp.logical_and(lane >= AUG0 + 3, lane < AUG0 + 6)
                ones_k = jnp.logical_and(lane >= AUG0, lane < AUG0 + 3)
                aug_q = jnp.where(lane == AUG0, hi, jnp.where(lane == AUG0 + 1, mid, jnp.where(
                    lane == AUG0 + 2, lo, jnp.where(ones_q, 1.0, 0.0))))
                aug_k = jnp.where(lane == AUG0 + 3, -hi, jnp.where(lane == AUG0 + 4, -mid, jnp.where(
                    lane == AUG0 + 5, -lo, jnp.where(ones_k, 1.0, 0.0))))
                base = (h // 2) * TRIPLE
                qp, kp, vp = (qkv_ref[rows, base + t * LANES:base + (t + 1) * LANES].astype(F32) for t in range(3))
                if h % 2:
                    qp, kp, vp = (pltpu.roll(a, HEAD_DIM, 1) for a in (qp, kp, vp))
                out = slice(h * LANES, (h + 1) * LANES)
                qa_ref[rows, out] = jnp.where(lane < HEAD_DIM, qp * (HEAD_DIM ** -0.5), aug_q).astype(BF16)
                ka_ref[rows, out] = jnp.where(lane < HEAD_DIM, kp, aug_k).astype(BF16)
                va_ref[rows, out] = jnp.where(lane < HEAD_DIM, vp, jnp.where(lane == AUG0, 1.0, 0.0)).astype(BF16)

    fblk = pl.BlockSpec((S, LANES), lambda b: (b, 0))
    out = pl.BlockSpec((S, N_HEADS * LANES), lambda b: (b, 0))
    sh = jax.ShapeDtypeStruct((T, N_HEADS * LANES), BF16)
    return pl.pallas_call(
        body, name=name, grid=(n_seq,),
        in_specs=[fblk, pl.BlockSpec((1, LANES), lambda b: (0, 0)),
                  pl.BlockSpec((S, 4 * TRIPLE), lambda b: (b, OFF_QKV // (4 * TRIPLE)))],
        out_specs=[out, out, out], out_shape=[sh, sh, sh],
        compiler_params=_cp(("parallel",)),
    )(f, bf, proj)


def _band_mask(q0, k0, nq, nk):
    row = q0 + lax.broadcasted_iota(jnp.int32, (nq, nk), 0)
    col = k0 + lax.broadcasted_iota(jnp.int32, (nq, nk), 1)
    return col <= row


_NT = (((1,), (1,)), ((), ()))
_TN = (((0,), (0,)), ((), ()))


def _attn_fwd2(qa, ka, va, n_seq, name):
    T = qa.shape[0]
    S = T // n_seq
    tq, tk, rg = min(Q_TILE, S), min(K_CHUNK, S), ROW_GROUP
    nq, per = S // tq, tq // tk

    def body(q_ref, k_ref, v_ref, o_ref, o32_ref, lse_ref, phi_s, plo_s, mp_s, m_s, acc_s):
        qi = pl.program_id(2)
        mp_s[...] = jnp.full_like(mp_s, NEG_INF)
        acc_s[...] = jnp.zeros_like(acc_s)

        def scores(kc, hh, r0):
            k0 = pl.multiple_of(kc * tk, tk)
            hl = slice(hh * LANES, (hh + 1) * LANES)
            return k0, lax.dot_general(q_ref[r0:, hl], k_ref[pl.ds(k0, tk), hl], _NT, preferred_element_type=F32)

        def max_chunk(kc, masked, r0):
            for hh in range(2):
                k0, s_all = scores(kc, hh, r0)
                for r in range(r0 // rg, tq // rg):
                    rows = slice(r * rg, (r + 1) * rg)
                    s = s_all[r * rg - r0:(r + 1) * rg - r0, :]
                    if masked:
                        s = jnp.where(_band_mask(qi * tq + r * rg, k0, rg, tk), s, NEG_INF)
                    part = s[:, :LANES]
                    for c in range(1, tk // LANES):
                        part = jnp.maximum(part, s[:, c * LANES:(c + 1) * LANES])
                    mp_s[hh, rows, :] = jnp.maximum(mp_s[hh, rows, :], part)

        def sum_chunk(kc, masked, r0):
            for hh in range(2):
                k0, s_all = scores(kc, hh, r0)
                hl = slice(hh * LANES, (hh + 1) * LANES)
                v = v_ref[pl.ds(k0, tk), hl]
                for r in range(r0 // rg, tq // rg):
                    rows = slice(r * rg, (r + 1) * rg)
                    p = jnp.exp(s_all[r * rg - r0:(r + 1) * rg - r0, :] - m_s[hh, rows])
                    if masked:
                        p = jnp.where(_band_mask(qi * tq + r * rg, k0, rg, tk), p, 0.0)
                    p_hi = p.astype(BF16)
                    phi_s[hh, rows, :] = p_hi
                    plo_s[hh, rows, :] = (p - p_hi.astype(F32)).astype(BF16)
                acc_s[hh, r0:, :] += (jnp.dot(phi_s[hh, r0:, :], v, preferred_element_type=F32)
                                      + jnp.dot(plo_s[hh, r0:, :], v, preferred_element_type=F32))

        def sweep(chunk):
            def unmasked(kc, carry):
                chunk(kc, False, 0)
                return carry

            lax.fori_loop(0, qi * per, unmasked, 0)
            for d in range(per):
                chunk(qi * per + d, True, d * tk)

        sweep(max_chunk)
        m_s[...] = jnp.max(mp_s[...], axis=-1, keepdims=True)
        sweep(sum_chunk)

        lane = lax.broadcasted_iota(jnp.int32, (1, LANES), 1)
        outs = []
        for hh in range(2):
            acc = acc_s[hh]
            l = jnp.sum(jnp.where(lane == AUG0, acc, 0.0), axis=-1, keepdims=True)
            lse_ref[hh] = m_s[hh] + jnp.log(l)
            outs.append(acc / l)
        o = jnp.where(lane < HEAD_DIM, outs[0], pltpu.roll(outs[1], HEAD_DIM, 1))
        o_ref[...] = o.astype(BF16)
        o32_ref[...] = o

    qmap = lambda b, j, qi: (b * nq + qi, j)
    omap = lambda b, j, qi: (b * nq + qi, j)
    kv = pl.BlockSpec((S, 2 * LANES), lambda b, j, qi: (b, j))
    return pl.pallas_call(
        body, name=name, grid=(n_seq, N_HEADS // 2, nq),
        in_specs=[pl.BlockSpec((tq, 2 * LANES), qmap), kv, kv],
        out_specs=[pl.BlockSpec((tq, LANES), omap), pl.BlockSpec((tq, LANES), omap),
                   pl.BlockSpec((2, tq, 1), lambda b, j, qi: (j, b * nq + qi, 0))],
        out_shape=[jax.ShapeDtypeStruct((T, BRANCH_W), BF16), jax.ShapeDtypeStruct((T, BRANCH_W), F32),
                   jax.ShapeDtypeStruct((N_HEADS, T, 1), F32)],
        scratch_shapes=[pltpu.VMEM((2, tq, tk), BF16), pltpu.VMEM((2, tq, tk), BF16),
                        pltpu.VMEM((2, tq, LANES), F32), pltpu.VMEM((2, tq, 1), F32),
                        pltpu.VMEM((2, tq, LANES), F32)],
        compiler_params=_cp(("parallel", "parallel", "parallel")),
    )(qa, ka, va)


def _attn_bwd(qa, ka, proj, do, o32, lse, dproj, n_seq, name):
    T = qa.shape[0]
    S = T // n_seq
    tq, tk, rg = min(Q_TILE, S), min(K_CHUNK, S), ROW_GROUP
    nq, per, nkc = S // tq, tq // tk, S // tk

    def body(q_ref, k_ref, v_ref, do_ref, o_ref, lse_ref, _, dqkv_ref, dfk_ref,
             p_s, ds_s, dq_s, dk_s, dv_s, df_s):
        dk_s[...] = jnp.zeros_like(dk_s)
        dv_s[...] = jnp.zeros_like(dv_s)
        df_s[...] = jnp.zeros_like(df_s)
        sels = _pair_masks()

        for qi in range(nq):
            q0 = qi * tq
            do_t = do_ref[q0:q0 + tq, :]
            dq_s[...] = jnp.zeros_like(dq_s)
            prod = do_t.astype(F32) * o_ref[q0:q0 + tq, :]
            dls = [jnp.sum(jnp.where(sel, prod, 0.0), axis=-1, keepdims=True) for sel in sels]

            def chunk(kc, masked, r0, q0=q0, do_t=do_t, dls=dls):
                k0 = pl.multiple_of(kc * tk, tk)
                v = v_ref[pl.ds(k0, tk), :]
                do_a = do_t[r0:, :]
                for hh in range(2):
                    hl = slice(hh * LANES, (hh + 1) * LANES)
                    qh, kh = q_ref[q0 + r0:q0 + tq, hl], k_ref[pl.ds(k0, tk), hl]
                    s_all = lax.dot_general(qh, kh, _NT, preferred_element_type=F32)
                    dom = jnp.where(sels[hh], do_a, jnp.zeros_like(do_a))
                    dp_all = lax.dot_general(dom, v, _NT, preferred_element_type=F32)
                    dfp = jnp.zeros((1, tk), F32)
                    for r in range(r0 // rg, tq // rg):
                        rows = slice(r * rg, (r + 1) * rg)
                        arows = slice(r * rg - r0, (r + 1) * rg - r0)
                        qrows = slice(q0 + r * rg, q0 + (r + 1) * rg)
                        p = jnp.exp(s_all[arows, :] - lse_ref[hh, qrows])
                        if masked:
                            p = jnp.where(_band_mask(q0 + r * rg, k0, rg, tk), p, 0.0)
                        ds = p * (dp_all[arows, :] - dls[hh][rows])
                        p_s[hh, rows, :] = p.astype(BF16)
                        ds_s[hh, rows, :] = ds.astype(BF16)
                        dfp = dfp + jnp.sum(ds, axis=0, keepdims=True)
                    df_s[hh, kc] -= dfp
                    dq_s[hh, r0:, :] += jnp.dot(ds_s[hh, r0:, :], kh, preferred_element_type=F32)
                    dv_s[hh, pl.ds(k0, tk), :] += lax.dot_general(p_s[hh, r0:, :], do_a, _TN,
                                                                  preferred_element_type=F32)
                    dk_s[hh, pl.ds(k0, tk), :] += lax.dot_general(ds_s[hh, r0:, :], qh, _TN,
                                                                  preferred_element_type=F32)

            def unmasked(kc, carry, chunk=chunk):
                chunk(kc, False, 0)
                return carry

            lax.fori_loop(0, qi * per, unmasked, 0)
            for d in range(per):
                chunk(qi * per + d, True, d * tk)
            dq = jnp.where(sels[0], dq_s[0], pltpu.roll(dq_s[1], HEAD_DIM, 1))
            dqkv_ref[q0:q0 + tq, :LANES] = (dq * (HEAD_DIM ** -0.5)).astype(BF16)

        dqkv_ref[:, LANES:2 * LANES] = jnp.where(sels[0], dk_s[0], pltpu.roll(dk_s[1], HEAD_DIM, 1)).astype(BF16)
        dqkv_ref[:, 2 * LANES:] = jnp.where(sels[0], dv_s[0], dv_s[1]).astype(BF16)
        for c in range(nkc):
            dfk_ref[:, :, c * tk:(c + 1) * tk] = df_s[:, c]

    seq = lambda w: pl.BlockSpec((S, w), lambda b, j: (b, j))
    col1 = pl.BlockSpec((2, S, 1), lambda b, j: (j, b, 0))
    vblk = pl.BlockSpec((S, LANES), lambda b, j: (b, OFF_QKV // LANES + 3 * j + 2))
    return pl.pallas_call(
        body, name=name, grid=(n_seq, N_HEADS // 2),
        in_specs=[seq(2 * LANES), seq(2 * LANES), vblk, seq(LANES), seq(LANES), col1,
                  pl.BlockSpec(memory_space=pl.ANY)],
        out_specs=[pl.BlockSpec((S, TRIPLE), lambda b, j: (b, OFF_QKV // TRIPLE + j)),
                   pl.BlockSpec((2, 1, S), lambda b, j: (j, 0, b))],
        out_shape=[jax.ShapeDtypeStruct(dproj.shape, BF16), jax.ShapeDtypeStruct((N_HEADS, 1, T), F32)],
        input_output_aliases={6: 0},
        scratch_shapes=[pltpu.VMEM((2, tq, tk), BF16), pltpu.VMEM((2, tq, tk), BF16),
                        pltpu.VMEM((2, tq, LANES), F32), pltpu.VMEM((2, S, LANES), F32),
                        pltpu.VMEM((2, S, LANES), F32), pltpu.VMEM((2, nkc, 1, tk), F32)],
        compiler_params=_cp(("parallel", "parallel")),
    )(qa, ka, proj, do, o32, lse, dproj)


def _shift_down(v, k, row):
    return jnp.where(row >= k, pltpu.roll(v, k, 0), 0.0)


def _shift_up(v, k, row, S):
    return jnp.where(row < S - k, pltpu.roll(v, S - k, 0), 0.0)


def _pool_diff(uf, w, row):
    acc, k = uf, 1
    while k < w:
        acc = acc + _shift_down(acc, k, row)
        k *= 2
    n = jnp.minimum(row + 1, w).astype(F32)
    return acc / n - uf


def _pool_fwd(proj, pool_w, pool_scale, n_seq, name):
    T = proj.shape[0]
    S = T // n_seq

    def body(u_ref, w_ref, sc_ref, o_ref, d_s):
        g = pl.program_id(1)
        row = lax.broadcasted_iota(jnp.int32, (S, POOL_GD), 0)
        uf = u_ref[...].astype(F32)
        for gi, wlen in enumerate(POOL_WINDOWS):
            @pl.when(g == gi)
            def _(wlen=wlen):
                d_s[...] = _pool_diff(uf, wlen, row).astype(BF16)
        e = jnp.dot(d_s[...], w_ref[0], preferred_element_type=F32)
        o_ref[...] = (e * sc_ref[...]).astype(BF16)

    uc = OFF_U // POOL_GD
    return pl.pallas_call(
        body, name=name, grid=(n_seq, len(POOL_WINDOWS)),
        in_specs=[pl.BlockSpec((S, POOL_GD), lambda b, g: (b, uc + g)),
                  pl.BlockSpec((1, POOL_GD, POOL_GD), lambda b, g: (g, 0, 0)),
                  pl.BlockSpec((1, POOL_GD), lambda b, g: (0, g))],
        out_specs=pl.BlockSpec((S, POOL_GD), lambda b, g: (b, g)),
        out_shape=jax.ShapeDtypeStruct((T, BRANCH_W), BF16),
        scratch_shapes=[pltpu.VMEM((S, POOL_GD), BF16)],
        compiler_params=_cp(("parallel", "parallel")),
    )(proj, pool_w, pool_scale)


def _pool_bwd(proj, dout, pool_w, pool_scale, dproj, n_seq, name):
    T = proj.shape[0]
    S = T // n_seq

    def body(u_ref, do_ref, w_ref, sc_ref, _, du_ref, dw_ref, dsc_ref, d_s):
        g, b = pl.program_id(0), pl.program_id(1)
        row = lax.broadcasted_iota(jnp.int32, (S, POOL_GD), 0)
        uf = u_ref[...].astype(F32)
        for gi, wlen in enumerate(POOL_WINDOWS):
            @pl.when(g == gi)
            def _(wlen=wlen):
                d_s[...] = _pool_diff(uf, wlen, row).astype(BF16)
        db16 = d_s[...]
        w = w_ref[0]
        e = jnp.dot(db16, w, preferred_element_type=F32)
        dof = do_ref[...].astype(F32)
        dsc = jnp.sum(dof * e, axis=0, keepdims=True)
        de = (dof * sc_ref[...]).astype(BF16)
        dd = lax.dot_general(de, w, (((1,), (1,)), ((), ())), preferred_element_type=F32)
        dw = lax.dot_general(db16, de, (((0,), (0,)), ((), ())), preferred_element_type=F32)
        for gi, wlen in enumerate(POOL_WINDOWS):
            @pl.when(g == gi)
            def _(wlen=wlen):
                n = jnp.minimum(row + 1, wlen).astype(F32)
                acc, k = dd / n, 1
                while k < wlen:
                    acc = acc + _shift_up(acc, k, row, S)
                    k *= 2
                du_ref[...] = (acc - dd).astype(BF16)

        @pl.when(b == 0)
        def _():
            dw_ref[0] = dw
            dsc_ref[...] = dsc

        @pl.when(b > 0)
        def _():
            dw_ref[0] += dw
            dsc_ref[...] += dsc

    uc = OFF_U // POOL_GD
    return pl.pallas_call(
        body, name=name, grid=(len(POOL_WINDOWS), n_seq),
        in_specs=[pl.BlockSpec((S, POOL_GD), lambda g, b: (b, uc + g)),
                  pl.BlockSpec((S, POOL_GD), lambda g, b: (b, g)),
                  pl.BlockSpec((1, POOL_GD, POOL_GD), lambda g, b: (g, 0, 0)),
                  pl.BlockSpec((1, POOL_GD), lambda g, b: (0, g)),
                  pl.BlockSpec(memory_space=pl.ANY)],
        out_specs=[pl.BlockSpec((S, POOL_GD), lambda g, b: (b, uc + g)),
                   pl.BlockSpec((1, POOL_GD, POOL_GD), lambda g, b: (g, 0, 0)),
                   pl.BlockSpec((1, POOL_GD), lambda g, b: (0, g))],
        out_shape=[jax.ShapeDtypeStruct(dproj.shape, BF16),
                   jax.ShapeDtypeStruct((len(POOL_WINDOWS), POOL_GD, POOL_GD), F32),
                   jax.ShapeDtypeStruct((1, BRANCH_W), F32)],
        input_output_aliases={4: 0},
        scratch_shapes=[pltpu.VMEM((S, POOL_GD), BF16)],
        compiler_params=_cp(("parallel", "arbitrary")),
    )(proj, dout, pool_w, pool_scale, dproj)


def _conv_fwd(proj, conv_w, n_seq, name):
    T = proj.shape[0]
    S = T // n_seq
    nc = BRANCH_W // LANES

    def body(c_ref, w_ref, o_ref):
        row = lax.broadcasted_iota(jnp.int32, (S, LANES), 0)
        cv, cb, cc = (c_ref[:, t * LANES:(t + 1) * LANES].astype(F32) for t in range(3))
        z = cc * cv
        w = w_ref[...]
        y = w[0:1] * _shift_down(z, 2, row) + w[1:2] * _shift_down(z, 1, row) + w[2:3] * z
        o_ref[...] = (cb * y).astype(BF16)

    return pl.pallas_call(
        body, name=name, grid=(n_seq, nc),
        in_specs=[pl.BlockSpec((S, TRIPLE), lambda b, j: (b, OFF_CONV // TRIPLE + j)),
                  pl.BlockSpec((CONV_K, LANES), lambda b, j: (0, j))],
        out_specs=pl.BlockSpec((S, LANES), lambda b, j: (b, j)),
        out_shape=jax.ShapeDtypeStruct((T, BRANCH_W), BF16),
        compiler_params=_cp(("parallel", "parallel")),
    )(proj, conv_w)


def _conv_bwd(proj, dout, conv_w, dproj, n_seq, name):
    T = proj.shape[0]
    S = T // n_seq
    nc = BRANCH_W // LANES

    def body(c_ref, do_ref, w_ref, _, dc_ref, dw_ref):
        b = pl.program_id(1)
        row = lax.broadcasted_iota(jnp.int32, (S, LANES), 0)
        cv, cb, cc = (c_ref[:, t * LANES:(t + 1) * LANES].astype(F32) for t in range(3))
        dof = do_ref[...].astype(F32)
        w = w_ref[...]
        z = cc * cv
        z1, z2 = _shift_down(z, 1, row), _shift_down(z, 2, row)
        y = w[0:1] * z2 + w[1:2] * z1 + w[2:3] * z
        dy = dof * cb
        dz = w[2:3] * dy + w[1:2] * _shift_up(dy, 1, row, S) + w[0:1] * _shift_up(dy, 2, row, S)
        dc_ref[:, :LANES] = (dz * cc).astype(BF16)
        dc_ref[:, LANES:2 * LANES] = (dof * y).astype(BF16)
        dc_ref[:, 2 * LANES:] = (dz * cv).astype(BF16)
        dws = [jnp.sum(dy * zk, axis=0, keepdims=True) for zk in (z2, z1, z)]

        @pl.when(b == 0)
        def _():
            for kk in range(CONV_K):
                dw_ref[kk:kk + 1, :] = dws[kk]

        @pl.when(b > 0)
        def _():
            for kk in range(CONV_K):
                dw_ref[kk:kk + 1, :] += dws[kk]

    triple = pl.BlockSpec((S, TRIPLE), lambda j, b: (b, OFF_CONV // TRIPLE + j))
    wsp = pl.BlockSpec((CONV_K, LANES), lambda j, b: (0, j))
    return pl.pallas_call(
        body, name=name, grid=(nc, n_seq),
        in_specs=[triple, pl.BlockSpec((S, LANES), lambda j, b: (b, j)), wsp, pl.BlockSpec(memory_space=pl.ANY)],
        out_specs=[triple, wsp],
        out_shape=[jax.ShapeDtypeStruct(dproj.shape, BF16), jax.ShapeDtypeStruct((CONV_K, BRANCH_W), F32)],
        input_output_aliases={3: 0},
        compiler_params=_cp(("parallel", "arbitrary")),
    )(proj, dout, conv_w, dproj)


def _mix_fwd(oa, ob, oc, wpa, wpp, wpc, proj, b_gate, name):
    T = oa.shape[0]
    tm = min(512, T)

    def body(oa_ref, ob_ref, oc_ref, wa_ref, wp_ref, wc_ref, g_ref, bg_ref, o_ref):
        acc = jnp.zeros((tm, D_MODEL), F32)
        for i, (x_ref, w_ref) in enumerate(((oa_ref, wa_ref), (ob_ref, wp_ref), (oc_ref, wc_ref))):
            y = jnp.dot(x_ref[...], w_ref[...], preferred_element_type=F32)
            sl = slice(i * D_MODEL, (i + 1) * D_MODEL)
            acc = acc + _sigmoid(g_ref[:, sl].astype(F32) + bg_ref[:, sl]) * y
        o_ref[...] = acc.astype(BF16)

    br = pl.BlockSpec((tm, BRANCH_W), lambda i: (i, 0))
    wsp = pl.BlockSpec((BRANCH_W, D_MODEL), lambda i: (0, 0))
    return pl.pallas_call(
        body, name=name, grid=(T // tm,),
        in_specs=[br, br, br, wsp, wsp, wsp, pl.BlockSpec((tm, GATE_W), lambda i: (i, 0)),
                  pl.BlockSpec((1, GATE_W), lambda i: (0, 0))],
        out_specs=pl.BlockSpec((tm, D_MODEL), lambda i: (i, 0)),
        out_shape=jax.ShapeDtypeStruct((T, D_MODEL), BF16),
        compiler_params=_cp(("parallel",)),
    )(oa, ob, oc, wpa, wpp, wpc, proj, b_gate)


def _mix_bwd(oa, ob, oc, wpa, wpp, wpc, proj, b_gate, dmixed, name):
    T = oa.shape[0]
    tm = min(256, T)

    def body(oa_ref, ob_ref, oc_ref, wa_ref, wp_ref, wc_ref, g_ref, bg_ref, dm_ref,
             dya_ref, dyb_ref, dyc_ref, dg_ref, dbg_ref):
        i0 = pl.program_id(0)
        dm = dm_ref[...].astype(F32)
        parts = []
        for i, (x_ref, w_ref, dy_ref) in enumerate(((oa_ref, wa_ref, dya_ref), (ob_ref, wp_ref, dyb_ref),
                                                    (oc_ref, wc_ref, dyc_ref))):
            y = jnp.dot(x_ref[...], w_ref[...], preferred_element_type=F32)
            sl = slice(i * D_MODEL, (i + 1) * D_MODEL)
            gate = _sigmoid(g_ref[:, sl].astype(F32) + bg_ref[:, sl])
            dy_ref[...] = (dm * gate).astype(BF16)
            dgl = dm * y * gate * (1.0 - gate)
            dg_ref[:, sl] = dgl.astype(BF16)
            parts.append(jnp.sum(dgl, axis=0, keepdims=True))

        @pl.when(i0 == 0)
        def _():
            for i in range(3):
                dbg_ref[:, i * D_MODEL:(i + 1) * D_MODEL] = parts[i]

        @pl.when(i0 > 0)
        def _():
            for i in range(3):
                dbg_ref[:, i * D_MODEL:(i + 1) * D_MODEL] += parts[i]

    br = pl.BlockSpec((tm, BRANCH_W), lambda i: (i, 0))
    wsp = pl.BlockSpec((BRANCH_W, D_MODEL), lambda i: (0, 0))
    row = pl.BlockSpec((tm, D_MODEL), lambda i: (i, 0))
    gsp = pl.BlockSpec((tm, GATE_W), lambda i: (i, 0))
    bsp = pl.BlockSpec((1, GATE_W), lambda i: (0, 0))
    act = jax.ShapeDtypeStruct((T, D_MODEL), BF16)
    return pl.pallas_call(
        body, name=name, grid=(T // tm,),
        in_specs=[br, br, br, wsp, wsp, wsp, gsp, bsp, row],
        out_specs=[row, row, row, gsp, bsp],
        out_shape=[act, act, act, jax.ShapeDtypeStruct((T, MAIN_COLS), BF16),
                   jax.ShapeDtypeStruct((1, GATE_W), F32)],
        compiler_params=_cp(("arbitrary",)),
    )(oa, ob, oc, wpa, wpp, wpc, proj, b_gate, dmixed)


GU_TILE = 256


def _gu_col(c):
    t, r = divmod(c, GU_TILE)
    return (t // 2) * GU_TILE + r + (FFN_HIDDEN if t % 2 else 0)


def _gate_up_swiglu(h, w, name):
    T, K = h.shape
    tm = min(2048, T)

    def body(h_ref, w_ref, ab_ref, s_ref):
        prod = jnp.dot(h_ref[...], w_ref[...], preferred_element_type=F32)
        ab_ref[...] = prod.astype(BF16)
        a = prod[:, :GU_TILE]
        s_ref[...] = (a * _sigmoid(a) * prod[:, GU_TILE:]).astype(BF16)

    return pl.pallas_call(
        body, name=name, grid=(T // tm, FFN_HIDDEN // GU_TILE),
        in_specs=[pl.BlockSpec((tm, K), lambda i, j: (i, 0)), pl.BlockSpec((K, 2 * GU_TILE), lambda i, j: (0, j))],
        out_specs=[pl.BlockSpec((tm, 2 * GU_TILE), lambda i, j: (i, j)), pl.BlockSpec((tm, GU_TILE), lambda i, j: (i, j))],
        out_shape=[jax.ShapeDtypeStruct((T, 2 * FFN_HIDDEN), BF16), jax.ShapeDtypeStruct((T, FFN_HIDDEN), BF16)],
        compiler_params=_cp(("parallel", "parallel")),
    )(h, w)


def _swiglu_bwd_fused(dx, w_down, ab, name):
    T, K = dx.shape
    tm = min(2048, T)

    def body(dx_ref, w_ref, ab_ref, o_ref):
        ds = lax.dot_general(dx_ref[...], w_ref[...], _NT, preferred_element_type=F32)
        a = ab_ref[:, :GU_TILE].astype(F32)
        b = ab_ref[:, GU_TILE:].astype(F32)
        sg = _sigmoid(a)
        o_ref[:, :GU_TILE] = (ds * b * sg * (1.0 + a * (1.0 - sg))).astype(BF16)
        o_ref[:, GU_TILE:] = (ds * a * sg).astype(BF16)

    pair = pl.BlockSpec((tm, 2 * GU_TILE), lambda i, j: (i, j))
    return pl.pallas_call(
        body, name=name, grid=(T // tm, FFN_HIDDEN // GU_TILE),
        in_specs=[pl.BlockSpec((tm, K), lambda i, j: (i, 0)), pl.BlockSpec((GU_TILE, K), lambda i, j: (j, 0)), pair],
        out_specs=pair, out_shape=jax.ShapeDtypeStruct((T, 2 * FFN_HIDDEN), BF16),
        compiler_params=_cp(("parallel", "parallel")),
    )(dx, w_down, ab)


def _adamw_update(w_ref, g_ref, m_ref, v_ref, d_ref, nm_ref, nv_ref):
    gv = g_ref[...]
    nm = ADAM_B1 * m_ref[...] + (1.0 - ADAM_B1) * gv
    nv = ADAM_B2 * v_ref[...] + (1.0 - ADAM_B2) * (gv * gv)
    m_hat = nm / (1.0 - ADAM_B1 ** ADAM_STEP)
    v_hat = nv / (1.0 - ADAM_B2 ** ADAM_STEP)
    d_ref[...] = -ADAM_LR * (m_hat / (jnp.sqrt(v_hat) + ADAM_EPS) + ADAM_WD * w_ref[...])
    nm_ref[...] = nm
    nv_ref[...] = nv


def _adamw_many(ws, gs, ms, vs, name):
    n = len(ws)

    def body(*refs):
        ins, outs = refs[:4 * n], refs[4 * n:]
        for t in range(n):
            _adamw_update(ins[t], ins[n + t], ins[2 * n + t], ins[3 * n + t], outs[t], outs[n + t], outs[2 * n + t])

    shapes = [jax.ShapeDtypeStruct(w.shape, F32) for w in ws]
    out = pl.pallas_call(body, name=name, out_shape=shapes * 3, compiler_params=_cp())(*ws, *gs, *ms, *vs)
    return out[:n], out[n:2 * n], out[2 * n:]


def _adamw(w, g, m, v, name):
    R, C = w.shape
    tr = R
    for cand in (256, 352, 128, 64, 8):
        if R > cand and R % cand == 0:
            tr = cand
            break

    def body(w_ref, g_ref, m_ref, v_ref, d_ref, nm_ref, nv_ref):
        _adamw_update(w_ref, g_ref, m_ref, v_ref, d_ref, nm_ref, nv_ref)

    blk = pl.BlockSpec((tr, C), lambda i: (i, 0))
    sh = jax.ShapeDtypeStruct((R, C), F32)
    return pl.pallas_call(
        body, name=name, grid=(R // tr,), in_specs=[blk] * 4, out_specs=[blk] * 3, out_shape=[sh] * 3,
        compiler_params=_cp(("parallel",)),
    )(w, g, m, v)


def _adamw_3d(w, g, m, v, block, name):
    shape = w.shape
    grid = (shape[0] // block[0], shape[1] // block[1])
    assert shape[0] % block[0] == 0 and shape[1] % block[1] == 0 and block[2] == shape[2], (name, shape, block)

    def body(w_ref, g_ref, m_ref, v_ref, d_ref, nm_ref, nv_ref):
        _adamw_update(w_ref, g_ref, m_ref, v_ref, d_ref, nm_ref, nv_ref)

    blk = pl.BlockSpec(block, lambda i, j: (i, j, 0))
    sh = jax.ShapeDtypeStruct(shape, F32)
    return pl.pallas_call(
        body, name=name, grid=grid, in_specs=[blk] * 4, out_specs=[blk] * 3, out_shape=[sh] * 3,
        compiler_params=_cp(("parallel", "parallel")),
    )(w, g, m, v)


def _sum_slabs_t(x, name):
    n, R, C = x.shape

    def body(x_ref, o_ref):
        acc = x_ref[0].astype(F32)
        for j in range(1, n):
            acc = acc + x_ref[j].astype(F32)
        o_ref[...] = acc.T

    return pl.pallas_call(
        body, name=name, grid=(C // LANES,), in_specs=[pl.BlockSpec((n, R, LANES), lambda j: (0, 0, j))],
        out_specs=pl.BlockSpec((LANES, R), lambda j: (j, 0)), out_shape=jax.ShapeDtypeStruct((C, R), F32),
        compiler_params=_cp(("parallel",)),
    )(x)


def _sum_slabs(x, name):
    n, R, C = x.shape
    tr = R
    for cand in (512, 256, 128, 64, 32, 16, 8):
        if R > cand and R % cand == 0:
            tr = cand
            break

    def body(x_ref, o_ref):
        acc = x_ref[0].astype(F32)
        for j in range(1, n):
            acc = acc + x_ref[j].astype(F32)
        o_ref[...] = acc

    return pl.pallas_call(
        body, name=name, grid=(R // tr,), in_specs=[pl.BlockSpec((n, tr, C), lambda i: (0, i, 0))],
        out_specs=pl.BlockSpec((tr, C), lambda i: (i, 0)), out_shape=jax.ShapeDtypeStruct((R, C), F32),
        compiler_params=_cp(("parallel",)),
    )(x)


def _multi_gather(xs, layers, name):
    nt = len(xs)
    shapes = [x.shape if lay is None else x.shape[1:] for x, lay in zip(xs, layers)]

    def body(*refs):
        x_refs, out_refs = refs[:nt], refs[nt:2 * nt]
        send_sems, recv_sems, local_sems = refs[2 * nt:]
        x_, y_, c_ = lax.axis_index("x"), lax.axis_index("y"), lax.axis_index("c")
        me, sibling = (x_, y_, c_), (x_, y_, 1 - c_)
        chips = [(1 - x_, y_), (x_, 1 - y_), (1 - x_, 1 - y_)]

        def own_block(t):
            return x_refs[t] if layers[t] is None else x_refs[t].at[layers[t]]

        def copy(t, k, block, to, own=False):
            px, py, pc = block
            dst = out_refs[t].at[4 * px + 2 * py + pc]
            return pltpu.make_async_remote_copy(
                src_ref=own_block(t) if own else dst, dst_ref=dst,
                send_sem=send_sems.at[t, k], recv_sem=recv_sems.at[t, k],
                device_id=to, device_id_type=pl.DeviceIdType.MESH)

        mine, first, passed = [], [], []
        for t in range(nt):
            mine.append(pltpu.make_async_copy(own_block(t), out_refs[t].at[4 * x_ + 2 * y_ + c_], local_sems.at[t]))
            mine[-1].start()
            first.append([copy(t, 1 + j, me, (*chip, c_), own=True) for j, chip in enumerate(chips)]
                         + [copy(t, 0, me, sibling, own=True)])
            for cp in first[-1]:
                cp.start()
        for t in range(nt):
            for j, chip in enumerate(chips):
                copy(t, 1 + j, (*chip, c_), me).wait_recv()
                passed.append(copy(t, 4 + j, (*chip, c_), sibling))
                passed[-1].start()
        for t in range(nt):
            copy(t, 0, sibling, me).wait_recv()
            for j, chip in enumerate(chips):
                copy(t, 4 + j, (*chip, 1 - c_), me).wait_recv()
        for cp in [c for f in first for c in f] + passed:
            cp.wait_send()
        for cp in mine:
            cp.wait()

    hbm = pl.BlockSpec(memory_space=pl.ANY)
    return pl.pallas_call(
        body, name=name, out_shape=[jax.ShapeDtypeStruct((N_DEV,) + tuple(s), x.dtype) for s, x in zip(shapes, xs)],
        in_specs=[hbm] * nt, out_specs=[hbm] * nt,
        scratch_shapes=[pltpu.SemaphoreType.DMA((nt, 7)), pltpu.SemaphoreType.DMA((nt, 7)),
                        pltpu.SemaphoreType.DMA((nt,))],
    )(*xs)


_HBM = pl.BlockSpec(memory_space=pltpu.HBM)
_SEM = pl.BlockSpec(memory_space=pltpu.SEMAPHORE)
_PEER_ORDER = (2, 4, 6, 3, 5, 7, 1)


def _split_copies(src_refs, land_refs, send_sems, recv_sems, layers, per_peer):
    x_, y_, c_ = lax.axis_index("x"), lax.axis_index("y"), lax.axis_index("c")
    me = 4 * x_ + 2 * y_ + c_
    copies = []
    for k in _PEER_ORDER:
        px, py, pc = x_ ^ ((k >> 2) & 1), y_ ^ ((k >> 1) & 1), c_ ^ (k & 1)
        peer = 4 * px + 2 * py + pc
        for t in range(len(src_refs)):
            if per_peer:
                src = src_refs[t].at[peer]
            else:
                src = src_refs[t] if layers[t] is None else src_refs[t].at[layers[t]]
            copies.append(pltpu.make_async_remote_copy(
                src_ref=src, dst_ref=land_refs[t].at[me],
                send_sem=send_sems.at[t * (N_DEV - 1) + k - 1], recv_sem=recv_sems.at[t * (N_DEV - 1) + k - 1],
                device_id=(px, py, pc), device_id_type=pl.DeviceIdType.MESH))
    return copies


def _own_copies(src_refs, land_refs, sems, layers, per_peer):
    nt = len(src_refs)
    me = 4 * lax.axis_index("x") + 2 * lax.axis_index("y") + lax.axis_index("c")
    copies = []
    for t in range(nt):
        if per_peer:
            src = src_refs[t].at[me]
        else:
            src = src_refs[t] if layers[t] is None else src_refs[t].at[layers[t]]
        copies.append(pltpu.make_async_copy(src, land_refs[t].at[me], sems.at[nt * (N_DEV - 1) + t]))
    return copies


def _split_start(srcs, layers, per_peer, after, name):
    nt = len(srcs)
    if per_peer:
        land_shapes = [s.shape for s in srcs]
    else:
        land_shapes = [(N_DEV,) + tuple(s.shape if lay is None else s.shape[1:]) for s, lay in zip(srcs, layers)]

    def body(*refs):
        src_refs, land_refs = refs[:nt], refs[nt:2 * nt]
        send_sems, recv_sems = refs[2 * nt + 1], refs[2 * nt + 2]
        token = refs[-1]
        for cp in _split_copies(src_refs, land_refs, send_sems, recv_sems, layers, per_peer):
            cp.start()
        for cp in _own_copies(src_refs, land_refs, send_sems, layers, per_peer):
            cp.start()
        token[...] = jnp.zeros_like(token)

    lands = [pltpu.with_memory_space_constraint(lax.empty(s, x.dtype), pltpu.HBM) for s, x in zip(land_shapes, srcs)]
    srcs = [pltpu.with_memory_space_constraint(x, pltpu.HBM) for x in srcs]
    out = pl.pallas_call(
        body, name=name,
        out_shape=(pltpu.SemaphoreType.DMA((nt * N_DEV,)), pltpu.SemaphoreType.DMA((nt * (N_DEV - 1),)),
                   *[pltpu.HBM(x.shape, x.dtype) for x in srcs], *[pltpu.HBM(s, x.dtype) for s, x in zip(land_shapes, srcs)],
                   jax.ShapeDtypeStruct((8, LANES), F32)),
        in_specs=[_HBM] * (2 * nt) + [pl.BlockSpec(memory_space=pl.ANY)],
        out_specs=(_SEM, _SEM, *([_HBM] * (2 * nt)), pl.BlockSpec(memory_space=pltpu.VMEM)),
        input_output_aliases={i: 2 + i for i in range(2 * nt)},
        compiler_params=pltpu.CompilerParams(has_side_effects=pltpu.SideEffectType.DATAFLOW_SIDE_EFFECTING),
    )(*srcs, *lands, after)
    return out[0], out[1], list(out[2:2 + nt]), list(out[2 + nt:2 + 2 * nt]), out[-1]


def _split_wait(started, layers, per_peer, after, name):
    send_sems, recv_sems, srcs, lands, _ = started
    nt = len(srcs)

    def body(*refs):
        src_refs, land_refs = refs[:nt], refs[nt:2 * nt]
        s_sems, r_sems = refs[2 * nt], refs[2 * nt + 1]
        for cp in _split_copies(src_refs, land_refs, s_sems, r_sems, layers, per_peer):
            cp.wait_send()
            cp.wait_recv()
        for cp in _own_copies(src_refs, land_refs, s_sems, layers, per_peer):
            cp.wait()

    out = pl.pallas_call(
        body, name=name,
        out_shape=tuple(pltpu.HBM(x.shape, x.dtype) for x in srcs + lands),
        in_specs=[_HBM] * (2 * nt) + [_SEM, _SEM, pl.BlockSpec(memory_space=pl.ANY)],
        out_specs=tuple([_HBM] * (2 * nt)),
        input_output_aliases={i: i for i in range(2 * nt)},
        compiler_params=pltpu.CompilerParams(has_side_effects=pltpu.SideEffectType.DATAFLOW_SIDE_EFFECTING),
    )(*srcs, *lands, send_sems, recv_sems, after)
    return list(out[nt:])


def _runs(mapping):
    runs, c, n = [], 0, len(mapping)
    while c < n:
        if mapping[c] is None:
            c += 1
            continue
        sid, d, lo = mapping[c][0], mapping[c][1] - c, c
        while c < n and mapping[c] is not None and mapping[c][0] == sid and mapping[c][1] - c == d:
            c += 1
        runs.append((lo, c, sid, d))
    return runs


def _tile_plan(mapping, src_widths):
    runs = _runs(mapping)
    plan = []
    for t in range(len(mapping) // LANES):
        pieces = []
        for lo, hi, sid, d in runs:
            lo_t, hi_t = max(lo, t * LANES), min(hi, (t + 1) * LANES)
            if lo_t >= hi_t:
                continue
            a = ((lo_t + d) // LANES) * LANES
            win = min(2 * LANES, src_widths[sid] - a)
            shift = t * LANES + d - a
            pieces.append((sid, a, win, shift, lo_t - t * LANES, hi_t - t * LANES))
        plan.append(pieces)
    return plan


def _reblock(srcs, src_views, outs, out_views, name):
    R = srcs[0].shape[-2]
    tr = min(512, R)
    widths = {sid: srcs[ai].shape[-1] for sid, (ai, _) in src_views.items()}
    plans = [(ai, li, _tile_plan(mapping, widths)) for ai, li, mapping in out_views]
    ns = len(srcs)

    def body(*refs):
        s_refs, o_refs = refs[:ns], refs[ns:]
        cache = {}

        def shift_matrix(win, shift, lo, hi):
            key = (win, shift, lo, hi)
            if key not in cache:
                r = lax.broadcasted_iota(jnp.int32, (win, LANES), 0)
                c = lax.broadcasted_iota(jnp.int32, (win, LANES), 1)
                hit = jnp.logical_and(r - c == shift, jnp.logical_and(c >= lo, c < hi))
                cache[key] = jnp.where(hit, 1.0, 0.0).astype(BF16)
            return cache[key]

        for ai, li, plan in plans:
            for t, pieces in enumerate(plan):
                acc = None
                whole = len(pieces) == 1 and pieces[0][3:] == (0, 0, LANES)
                for sid, a, win, shift, lo, hi in pieces:
                    sa, sl = src_views[sid]
                    if whole:
                        win = LANES
                    src = s_refs[sa][:, a:a + win] if sl is None else s_refs[sa][sl, :, a:a + win]
                    if whole:
                        acc = src
                    else:
                        part = jnp.dot(src, shift_matrix(win, shift, lo, hi), preferred_element_type=F32)
                        acc = part if acc is None else acc + part
                val = jnp.zeros((tr, LANES), BF16) if acc is None else acc.astype(BF16)
                if li is None:
                    o_refs[ai][:, t * LANES:(t + 1) * LANES] = val
                else:
                    o_refs[ai][li, :, t * LANES:(t + 1) * LANES] = val

    def spec(shape):
        if len(shape) == 2:
            return pl.BlockSpec((tr, shape[1]), lambda i: (i, 0))
        return pl.BlockSpec((shape[0], tr, shape[2]), lambda i: (0, i, 0))

    return pl.pallas_call(
        body, name=name, grid=(R // tr,), in_specs=[spec(s.shape) for s in srcs],
        out_specs=[spec(s) for s in outs], out_shape=[jax.ShapeDtypeStruct(s, BF16) for s in outs],
        compiler_params=_cp(("parallel",)),
    )(*srcs)


SHARDED = ("w_in", "w_gate_up", "w_proj_attn", "w_proj_pool", "w_proj_conv", "w_out", "w_down")
WEIGHT_ORDER = ("attn_norm", "w_in", "b_forget", "b_gate", "w_proj_attn", "pool_w", "pool_scale", "w_proj_pool",
                "conv_w", "w_proj_conv", "w_out", "ffn_norm", "w_gate_up", "w_down", "final_norm")
IN_SHARD, IN_SHARD_PAD = IN_COLS // N_DEV, 896
GU_SHARD, GU_SHARD_PAD = 2 * FFN_HIDDEN // N_DEV, 768


def _w_in_col(c):
    if c < OFF_QKV:
        return c + 3592
    if c < OFF_U:
        base, off = (0, OFF_QKV) if c < OFF_CONV else (2056, OFF_CONV)
        j, t = divmod(c - off, TRIPLE)
        which, e = divmod(t, LANES)
        return base + which * BRANCH_W + j * LANES + e
    return c - OFF_U + 1544


def _w_in_full(gathered, name):
    main = [divmod(_w_in_col(c), IN_SHARD) for c in range(MAIN_COLS)]
    fcols = [divmod(1536 + c, IN_SHARD) if c < N_HEADS else None for c in range(LANES)]
    R = gathered.shape[1]
    return _reblock([gathered], {i: (0, i) for i in range(N_DEV)}, [(R, MAIN_COLS), (R, LANES)],
                    [(0, None, main), (1, None, fcols)], name)


def _w_in_slabs(dmain, dwf, name):
    inv = {_w_in_col(c): ("m", c) for c in range(MAIN_COLS)}
    inv.update({1536 + c: ("f", c) for c in range(N_HEADS)})
    views = []
    for i in range(N_DEV):
        mapping = [inv[IN_SHARD * i + j] if j < IN_SHARD else None for j in range(IN_SHARD_PAD)]
        views.append((0, i, mapping))
    R = dmain.shape[0]
    return _reblock([dmain, dwf], {"m": (0, None), "f": (1, None)}, [(N_DEV, R, IN_SHARD_PAD)], views, name)[0]


def _w_gu_full(gathered, name):
    mapping = [divmod(_gu_col(c), GU_SHARD) for c in range(2 * FFN_HIDDEN)]
    R = gathered.shape[1]
    return _reblock([gathered], {i: (0, i) for i in range(N_DEV)}, [(R, 2 * FFN_HIDDEN)], [(0, None, mapping)], name)[0]


def _w_gu_slabs(dw, name):
    inv = {_gu_col(c): c for c in range(2 * FFN_HIDDEN)}
    views = [(0, i, [("w", inv[GU_SHARD * i + j]) if j < GU_SHARD else None for j in range(GU_SHARD_PAD)])
             for i in range(N_DEV)]
    R = dw.shape[0]
    return _reblock([dw], {"w": (0, None)}, [(N_DEV, R, GU_SHARD_PAD)], views, name)[0]


def _layer_fwd(x, W, n_seq, l, h1=None, next_norm=None):
    T = x.shape[0]
    sfx = f"_l{l}"
    if h1 is None:
        h1 = _rms_fwd(x, W["attn_norm"], "rms1" + sfx)
    proj, f = _matmul(h1, W["w_main"], mode="nn", out_dtype=BF16, name="proj_main" + sfx, side=(W["w_f"], F32))
    qa, ka, va = _fox_prep(f, W["b_forget"], proj, n_seq, "fox_prep" + sfx)
    oa, oa32, lse = _attn_fwd2(qa, ka, va, n_seq, "attn_fwd" + sfx)
    if "late" in W:
        W.update(W.pop("late")(oa))
    ob = _pool_fwd(proj, W["pool_w"], W["pool_scale"], n_seq, "pool_fwd" + sfx)
    oc = _conv_fwd(proj, W["conv_w"], n_seq, "conv_fwd" + sfx)
    mixed = _mix_fwd(oa, ob, oc, W["w_proj_attn"], W["w_proj_pool"], W["w_proj_conv"], proj, W["b_gate"],
                     "mix_fwd" + sfx)
    x2, h2 = _matmul(mixed, W["w_out"], mode="nn", out_dtype=F32, name="out_proj" + sfx, tm=1024, tn=1024,
                     residual=x, rms_g=W["ffn_norm"])
    ab, s = _gate_up_swiglu(h2, W["w_gate_up"], "gate_up" + sfx)
    x3 = _matmul(s, W["w_down"], mode="nn", out_dtype=F32, name="down" + sfx, tm=1024, tn=1024, tk=1408,
                 residual=x2, rms_g=next_norm)
    x3, h1_next = x3 if next_norm is not None else (x3, None)
    saved = dict(x=x, h1=h1, proj=proj, f=f, qa=qa, ka=ka, oa=oa, oa32=oa32, lse=lse, ob=ob, oc=oc, mixed=mixed, x2=x2,
                 h2=h2, ab=ab, s=s)
    return x3, saved, h1_next


def _layer_bwd(dx3, dx3b, W, sv, n_seq, l, stage=None):
    T = dx3.shape[0]
    sfx = f"_l{l}"
    G = {}
    stage = stage or (lambda l, group, G, W: W)
    dab = _swiglu_bwd_fused(dx3b, W["w_down"], sv["ab"], "d_ab" + sfx)
    G["w_down"] = _matmul(sv["s"], dx3b, mode="tn", out_dtype=BF16, name="dw_down" + sfx, tm=256, tn=1024)
    dh2 = _matmul(dab, W["w_gate_up"], mode="nt", out_dtype=BF16, name="d_h2" + sfx, tm=1024, tn=1024, tk=1408)
    G["w_gate_up"] = _matmul(sv["h2"], dab, mode="tn", out_dtype=BF16, name="dw_gate_up" + sfx, tm=1024)
    W = stage(l, "ffn", G, W)
    dx2, dx2b, G["ffn_norm"] = _rms_bwd(sv["x2"], W["ffn_norm"], dh2, dx3, "rms2_bwd" + sfx)
    dmixed = _matmul(dx2b, W["w_out"], mode="nt", out_dtype=BF16, name="d_mixed" + sfx)
    G["w_out"] = _matmul(sv["mixed"], dx2b, mode="tn", out_dtype=BF16, name="dw_out" + sfx, tm=1024)
    dya, dyb, dyc, dproj, G["b_gate"] = _mix_bwd(sv["oa"], sv["ob"], sv["oc"], W["w_proj_attn"], W["w_proj_pool"],
                                                 W["w_proj_conv"], sv["proj"], W["b_gate"], dmixed, "mix_bwd" + sfx)
    douts = {}
    for br, dy, o in (("attn", dya, sv["oa"]), ("pool", dyb, sv["ob"]), ("conv", dyc, sv["oc"])):
        douts[br] = _matmul(dy, W["w_proj_" + br], mode="nt", out_dtype=BF16, name=f"d_{br}_out" + sfx)
        G["w_proj_" + br] = _matmul(o, dy, mode="tn", out_dtype=BF16, name=f"dw_proj_{br}" + sfx, tm=512)
    W = stage(l, "mix", G, W)
    dproj, G["conv_w"] = _conv_bwd(sv["proj"], douts["conv"], W["conv_w"], dproj, n_seq, "conv_bwd" + sfx)
    dproj, G["pool_w"], G["pool_scale"] = _pool_bwd(sv["proj"], douts["pool"], W["pool_w"], W["pool_scale"], dproj,
                                                    n_seq, "pool_bwd" + sfx)
    dproj, dFk = _attn_bwd(sv["qa"], sv["ka"], sv["proj"], douts["attn"], sv["oa32"], sv["lse"], dproj, n_seq,
                           "attn_bwd" + sfx)
    dF = jnp.pad(dFk.reshape(N_HEADS, T).T, ((0, 0), (0, LANES - N_HEADS)))
    df, G["b_forget"] = _fox_cumsum_bwd(sv["f"], W["b_forget"], dF, n_seq, "fox_cumsum_bwd" + sfx)
    G["w_main"], G["w_f"] = _matmul(sv["h1"], dproj, mode="tn", out_dtype=BF16, name="dw_main" + sfx, tm=1024,
                                    side=(df, BF16))
    W = stage(l, "w_in", G, W)
    dh1 = _matmul(dproj, W["w_main"], mode="nt", out_dtype=BF16, name="d_h1_main" + sfx, tm=1024, tn=1024, tk=1664,
                  extra=(df, W["w_f"]))
    dx, dxb, G["attn_norm"] = _rms_bwd(sv["x"], W["attn_norm"], dh1, dx2, "rms1_bwd" + sfx)
    return dx, dxb, G


def _replicated_operands(rep, l):
    W = {}
    W["attn_norm"], W["ffn_norm"] = rep["attn_norm"][l], rep["ffn_norm"][l]
    W["b_forget"] = jnp.pad(rep["b_forget"][l].reshape(1, N_HEADS), ((0, 0), (0, LANES - N_HEADS)))
    W["b_gate"] = rep["b_gate"][l].reshape(1, GATE_W)
    W["pool_w"] = rep["pool_w"][l].astype(BF16)
    W["pool_scale"] = rep["pool_scale"][l].reshape(1, BRANCH_W)
    return W


def _local_step(x, target, get_W, attn_norms, final_norm, stage=None):
    n_seq, S, Dm = x.shape
    T = n_seq * S
    xt = x.reshape(T, Dm)
    saved, Ws, h1 = [], [], None
    for l in range(DEPTH):
        Ws.append(get_W(l, xt))
        next_norm = attn_norms[l + 1] if l + 1 < DEPTH else None
        xt, sv, h1 = _layer_fwd(xt, Ws[l], n_seq, l, h1, next_norm)
        saved.append(sv)
    loss, dx, dxb, g_final = _loss_head(xt, final_norm, target.reshape(T, Dm), "loss_head")
    grads = [None] * DEPTH
    for l in reversed(range(DEPTH)):
        dx, dxb, grads[l] = _layer_bwd(dx, dxb, Ws[l], saved[l], n_seq, l, stage)
    return loss, dx.reshape(n_seq, S, Dm), grads, g_final


def _padded_shards(weights):
    pads = {"w_in": IN_SHARD_PAD - IN_SHARD, "w_gate_up": GU_SHARD_PAD - GU_SHARD}
    return {n: jnp.pad(weights[n], ((0, 0), (0, 0), (0, pads.get(n, 0)))).astype(BF16) for n in SHARDED}


def _full_operands(g, l):
    W = {}
    if "w_in" in g:
        W["w_main"], W["w_f"] = _w_in_full(g["w_in"], f"w_in_full_l{l}")
    if "w_gate_up" in g:
        W["w_gate_up"] = _w_gu_full(g["w_gate_up"], f"w_gate_up_full_l{l}")
    for n in ("w_proj_attn", "w_proj_pool", "w_proj_conv"):
        if n in g:
            W[n] = jnp.transpose(g[n], (1, 0, 2)).reshape(BRANCH_W, D_MODEL)
    if "w_out" in g:
        W["w_out"] = g["w_out"].reshape(D_MODEL, D_MODEL)
    if "w_down" in g:
        W["w_down"] = g["w_down"].reshape(FFN_HIDDEN, D_MODEL)
    return W


GRAD_GROUPS = {"ffn": ("w_down", "w_gate_up"),
               "mix": ("w_out", "w_proj_attn", "w_proj_pool", "w_proj_conv"),
               "w_in": ("w_in",)}


def _grad_slabs(G, n, l):
    if n == "w_in":
        return _w_in_slabs(G["w_main"], G["w_f"], f"w_in_slabs_l{l}")
    if n == "w_gate_up":
        return _w_gu_slabs(G["w_gate_up"], f"w_gate_up_slabs_l{l}")
    if n == "w_out":
        return G["w_out"].reshape(N_DEV, D_MODEL // N_DEV, D_MODEL)
    if n == "w_down":
        return G["w_down"].reshape(N_DEV, FFN_HIDDEN // N_DEV, D_MODEL)
    return jnp.transpose(G[n].reshape(BRANCH_W, N_DEV, D_MODEL // N_DEV), (1, 0, 2))


def _sum_layer_grads(recv, l):
    out = {}
    for n, r in recv.items():
        if n in ("w_in", "w_gate_up"):
            out[n] = _sum_slabs_t(r, f"sum_{n}_l{l}")[:IN_SHARD if n == "w_in" else GU_SHARD]
        else:
            out[n] = _sum_slabs(r, f"sum_{n}_l{l}")
    return out


def _sum_small(xs, name):
    def body(*refs):
        for x_ref, o_ref in zip(refs[:len(xs)], refs[len(xs):]):
            acc = x_ref[0]
            for j in range(1, N_DEV):
                acc = acc + x_ref[j]
            o_ref[...] = acc

    return pl.pallas_call(
        body, name=name, out_shape=[jax.ShapeDtypeStruct(x.shape[1:], F32) for x in xs],
        compiler_params=_cp(),
    )(*xs)


def _as_2d(a):
    if a.ndim == 1:
        return a.reshape(1, -1)
    return a.reshape(-1, a.shape[-1])


def kernel(x, attn_norm, w_in, b_forget, b_gate, w_proj_attn, pool_w, pool_scale, w_proj_pool, conv_w, w_proj_conv, w_out, ffn_norm, w_gate_up, w_down, final_norm, loss_target, m_attn_norm, m_w_in, m_b_forget, m_b_gate, m_w_proj_attn, m_pool_w, m_pool_scale, m_w_proj_pool, m_conv_w, m_w_proj_conv, m_w_out, m_ffn_norm, m_w_gate_up, m_w_down, m_final_norm, v_attn_norm, v_w_in, v_b_forget, v_b_gate, v_w_proj_attn, v_pool_w, v_pool_scale, v_w_proj_pool, v_conv_w, v_w_proj_conv, v_w_out, v_ffn_norm, v_w_gate_up, v_w_down, v_final_norm):
    weights = dict(attn_norm=attn_norm, w_in=w_in, b_forget=b_forget, b_gate=b_gate, w_proj_attn=w_proj_attn,
                   pool_w=pool_w, pool_scale=pool_scale, w_proj_pool=w_proj_pool, conv_w=conv_w,
                   w_proj_conv=w_proj_conv, w_out=w_out, ffn_norm=ffn_norm, w_gate_up=w_gate_up, w_down=w_down,
                   final_norm=final_norm)
    moments_m = dict(attn_norm=m_attn_norm, w_in=m_w_in, b_forget=m_b_forget, b_gate=m_b_gate,
                     w_proj_attn=m_w_proj_attn, pool_w=m_pool_w, pool_scale=m_pool_scale, w_proj_pool=m_w_proj_pool,
                     conv_w=m_conv_w, w_proj_conv=m_w_proj_conv, w_out=m_w_out, ffn_norm=m_ffn_norm,
                     w_gate_up=m_w_gate_up, w_down=m_w_down, final_norm=m_final_norm)
    moments_v = dict(attn_norm=v_attn_norm, w_in=v_w_in, b_forget=v_b_forget, b_gate=v_b_gate,
                     w_proj_attn=v_w_proj_attn, pool_w=v_pool_w, pool_scale=v_pool_scale, w_proj_pool=v_w_proj_pool,
                     conv_w=v_conv_w, w_proj_conv=v_w_proj_conv, w_out=v_w_out, ffn_norm=v_ffn_norm,
                     w_gate_up=v_w_gate_up, w_down=v_w_down, final_norm=v_final_norm)

    sh = _padded_shards(weights)
    names = list(SHARDED)
    rest = [n for n in names if n != "w_in"]
    me = 4 * lax.axis_index("x") + 2 * lax.axis_index("y") + lax.axis_index("c")
    w_in0, conv_all = _multi_gather([sh["w_in"], conv_w], [0, None], "gather_w_in_l0")
    started, after = {}, w_in0
    for l in range(DEPTH):
        for group, gnames in (("w_in", ["w_in"]), ("rest", rest)):
            if (l, group) != (0, "w_in"):
                started[l, group] = _split_start([sh[n] for n in gnames], [l] * len(gnames), False, after,
                                                 f"gather_start_{group}_l{l}")
                after = started[l, group][4]
    last_token = after

    def get_W(l, xt):
        if l == 0:
            w_in = w_in0
        else:
            w_in = _split_wait(started[l, "w_in"], [l], False, xt, f"gather_wait_w_in_l{l}")[0]
        W = _full_operands({"w_in": w_in}, l)

        def late(after):
            lands = _split_wait(started[l, "rest"], [l] * len(rest), False, after, f"gather_wait_rest_l{l}")
            return _full_operands(dict(zip(rest, lands)), l)

        W["late"] = late
        W.update(_replicated_operands(weights, l))
        W["conv_w"] = jnp.transpose(conv_all[:, l], (1, 0, 2)).reshape(CONV_K, BRANCH_W)
        if l == 0:
            W["attn_norm"] = W["attn_norm"] + last_token[0, 0]
        return W

    exchanges = []

    def stage(l, group, G, W):
        gnames = GRAD_GROUPS[group]
        slabs = [_grad_slabs(G, n, l) for n in gnames]
        started = _split_start(slabs, None, True, slabs[0], f"exchange_start_{group}_l{l}")
        exchanges.append((l, group, gnames, slabs, started))
        tie = {"ffn": "ffn_norm", "mix": "conv_w", "w_in": "w_f"}[group]
        W = dict(W)
        W[tie] = W[tie] + started[4][0, 0].astype(W[tie].dtype)
        return W

    loss_part, grad_x, grads, g_final = _local_step(x, loss_target, get_W, attn_norm, final_norm, stage)

    def finish_exchange(ex, after):
        l, group, gnames, slabs, started = ex
        lands = _split_wait(started, None, True, after, f"exchange_wait_{group}_l{l}")
        grads[l].update(_sum_layer_grads(dict(zip(gnames, lands)), l))

    def zero_after(a):
        return jnp.minimum(jnp.abs(a.reshape(-1)[0]), 0.0)

    *early, last_exchange = exchanges
    for ex in early:
        finish_exchange(ex, grad_x)
    deltas, new_m, new_v, gw = {}, {}, {}, {}
    views = {"w_in": ((2, 0, 1), (1, 2, 0), (49, DEPTH, D_MODEL), 1),
             "w_gate_up": ((0, 2, 1), (0, 2, 1), (1, GU_SHARD // 2, D_MODEL), 0)}

    def update_sharded(n):
        if n in views:
            perm, inv, block, layer_axis = views[n]
            gt = jnp.stack([grads[l][n] for l in range(DEPTH)], axis=layer_axis)
            d, nm, nv = _adamw_3d(jnp.transpose(weights[n], perm), gt, jnp.transpose(moments_m[n], perm),
                                  jnp.transpose(moments_v[n], perm), block, "adamw_" + n)
            deltas[n], new_m[n], new_v[n] = (jnp.transpose(a, inv) for a in (d, nm, nv))
            gw[n] = jnp.transpose(gt, inv)
            return
        gw[n] = jnp.stack([grads[l][n] for l in range(DEPTH)])
        shape = weights[n].shape
        d, nm, nv = _adamw(_as_2d(weights[n]), _as_2d(gw[n]), _as_2d(moments_m[n]), _as_2d(moments_v[n]),
                           "adamw_" + n)
        deltas[n], new_m[n], new_v[n] = d.reshape(shape), nm.reshape(shape), nv.reshape(shape)

    for n in SHARDED:
        if n != "w_in":
            update_sharded(n)

    small = ("attn_norm", "b_forget", "b_gate", "pool_w", "pool_scale", "ffn_norm", "conv_w")
    loss_part = loss_part + zero_after(deltas["w_gate_up"]) + zero_after(deltas["w_down"])
    parts = [jnp.stack([grads[l][n] for l in range(DEPTH)]) for n in small] + [g_final, loss_part]
    gathered = _multi_gather(parts, [None] * len(parts), "gather_small_grads")
    summed = _sum_small(gathered, "sum_small_grads")
    for n, s in zip(small, summed):
        gw[n] = s
    gw["attn_norm"], gw["ffn_norm"] = gw["attn_norm"][:, 0], gw["ffn_norm"][:, 0]
    gw["b_forget"] = gw["b_forget"][:, 0, :N_HEADS]
    gw["b_gate"], gw["pool_scale"] = gw["b_gate"][:, 0], gw["pool_scale"][:, 0]
    gw["conv_w"] = lax.dynamic_slice_in_dim(gw["conv_w"], me * (BRANCH_W // N_DEV), BRANCH_W // N_DEV, axis=2)
    gw["final_norm"] = summed[-2][0]
    loss = summed[-1][0, 0]

    rest_names = [n for n in WEIGHT_ORDER if n not in SHARDED]
    ds, nms, nvs = _adamw_many(*[[_as_2d(src[n]) for n in rest_names] for src in (weights, gw, moments_m, moments_v)],
                               "adamw_small")
    for n, d, nm, nv in zip(rest_names, ds, nms, nvs):
        shape = weights[n].shape
        deltas[n], new_m[n], new_v[n] = d.reshape(shape), nm.reshape(shape), nv.reshape(shape)

    finish_exchange(last_exchange, deltas["pool_w"])
    update_sharded("w_in")

    return (loss, grad_x, *[gw[n] for n in WEIGHT_ORDER], *[deltas[n] for n in WEIGHT_ORDER],
            *[new_m[n] for n in WEIGHT_ORDER], *[new_v[n] for n in WEIGHT_ORDER])
```

```python
import functools

import jax
import jax.numpy as jnp
from jax import lax
from jax.experimental import pallas as pl
from jax.experimental.pallas import tpu as pltpu

F32 = jnp.float32
BF16 = jnp.bfloat16

N_DEV = 8
D_MODEL = 1024
DEPTH = 2
N_HEADS = 8
HEAD_DIM = 64
BRANCH_W = 512
POOL_WINDOWS = (2, 4, 8, 16)
POOL_GD = 128
CONV_K = 3
FFN_HIDDEN = 2816
GATE_W = 3 * D_MODEL
IN_COLS = 6664
MAIN_COLS = GATE_W + 7 * BRANCH_W
RMS_EPS = 1e-6
NEG_INF = -1e30

ADAM_LR = 0.001
ADAM_B1 = 0.9
ADAM_B2 = 0.999
ADAM_EPS = 1e-08
ADAM_WD = 0.01
ADAM_STEP = 10

LANES = 128
VMEM_LIMIT = 56 * 1024 * 1024
CUM_BLK = 256

TRIPLE = 3 * LANES
OFF_G, OFF_QKV, OFF_CONV, OFF_U = 0, 3072, 4608, 6144


def _cp(sem=None):
    return pltpu.CompilerParams(dimension_semantics=sem, vmem_limit_bytes=VMEM_LIMIT)


def _sigmoid(z):
    return 1.0 / (1.0 + jnp.exp(-z))


def _matmul(a, b, *, mode, out_dtype, name, tm=2048, tn=512, tk=None, residual=None, rms_g=None, side=None,
            extra=None):
    if mode == "nn":
        (M, K), N = a.shape, b.shape[1]
    elif mode == "nt":
        (M, K), N = a.shape, b.shape[0]
    else:
        (K, M), N = a.shape, b.shape[1]
    tm, tn, tk = min(tm, M), min(tn, N), K if tk is None else min(tk, K)
    assert M % tm == 0 and N % tn == 0 and K % tk == 0, (name, M, N, K, tm, tn, tk)
    nk = K // tk
    if mode == "nn":
        a_spec = pl.BlockSpec((tm, tk), lambda i, j, k: (i, k))
        b_spec = pl.BlockSpec((tk, tn), lambda i, j, k: (k, j))
        dims = (((1,), (0,)), ((), ()))
    elif mode == "nt":
        a_spec = pl.BlockSpec((tm, tk), lambda i, j, k: (i, k))
        b_spec = pl.BlockSpec((tn, tk), lambda i, j, k: (j, k))
        dims = (((1,), (1,)), ((), ()))
    else:
        a_spec = pl.BlockSpec((tk, tm), lambda i, j, k: (k, i))
        b_spec = pl.BlockSpec((tk, tn), lambda i, j, k: (k, j))
        dims = (((0,), (0,)), ((), ()))
    o_spec = pl.BlockSpec((tm, tn), lambda i, j, k: (i, j))
    has_res, has_norm, has_side, has_extra = (v is not None for v in (residual, rms_g, side, extra))
    assert not has_norm or tn == N, (name, tn, N)
    assert not has_side or (nk == 1 and mode != "nt"), name

    in_specs, args = [a_spec, b_spec], [a, b]
    out_specs, out_shape = [o_spec], [jax.ShapeDtypeStruct((M, N), out_dtype)]
    if has_res:
        in_specs.append(o_spec)
        args.append(residual)
    if has_norm:
        in_specs.append(pl.BlockSpec((1, N), lambda i, j, k: (0, 0)))
        args.append(rms_g.reshape(1, N))
        out_specs.append(o_spec)
        out_shape.append(jax.ShapeDtypeStruct((M, N), BF16))
    if has_side:
        b_side, side_dtype = side
        ns = b_side.shape[1]
        in_specs.append(pl.BlockSpec((K, ns), lambda i, j, k: (0, 0)))
        args.append(b_side)
        out_specs.append(pl.BlockSpec((tm, ns), lambda i, j, k: (i, 0)))
        out_shape.append(jax.ShapeDtypeStruct((M, ns), side_dtype))
    if has_extra:
        a2, b2 = extra
        in_specs += [pl.BlockSpec((tm, a2.shape[1]), lambda i, j, k: (i, 0)),
                     pl.BlockSpec((tn, b2.shape[1]), lambda i, j, k: (j, 0))]
        args += [a2, b2]
    n_in = len(args)

    def body(*refs):
        ins, outs = list(refs[2:n_in]), list(refs[n_in:n_in + len(out_shape)])
        a_ref, b_ref = refs[:2]
        r_ref = ins.pop(0) if has_res else None
        g_ref = ins.pop(0) if has_norm else None
        bs_ref = ins.pop(0) if has_side else None
        a2_ref, b2_ref = (ins.pop(0), ins.pop(0)) if has_extra else (None, None)
        o_ref = outs.pop(0)
        h_ref = outs.pop(0) if has_norm else None
        so_ref = outs.pop(0) if has_side else None

        def finish(acc):
            if has_res:
                acc = acc + r_ref[...].astype(F32)
            if has_extra:
                acc = acc + lax.dot_general(a2_ref[...], b2_ref[...], (((1,), (1,)), ((), ())),
                                            preferred_element_type=F32)
            o_ref[...] = acc.astype(out_dtype)
            if has_norm:
                r = lax.rsqrt(jnp.mean(acc * acc, axis=-1, keepdims=True) + RMS_EPS)
                h_ref[...] = ((acc * r) * g_ref[...]).astype(BF16)

        if has_side:
            @pl.when(pl.program_id(1) == 0)
            def _():
                side_dims = (((1,), (0,)), ((), ())) if mode == "nn" else dims
                so_ref[...] = lax.dot_general(a_ref[...], bs_ref[...], side_dims,
                                              preferred_element_type=F32).astype(so_ref.dtype)

        prod = lax.dot_general(a_ref[...], b_ref[...], dims, preferred_element_type=F32)
        if nk == 1:
            finish(prod)
            return
        acc_ref = refs[-1]
        k = pl.program_id(2)

        @pl.when(k == 0)
        def _():
            acc_ref[...] = prod

        @pl.when(jnp.logical_and(k > 0, k < nk - 1))
        def _():
            acc_ref[...] += prod

        @pl.when(k == nk - 1)
        def _():
            finish(acc_ref[...] + prod)

    single = len(out_shape) == 1
    return pl.pallas_call(
        body, name=name, grid=(M // tm, N // tn, nk), in_specs=in_specs,
        out_specs=out_specs[0] if single else out_specs, out_shape=out_shape[0] if single else out_shape,
        scratch_shapes=[pltpu.VMEM((tm, tn), F32)] if nk > 1 else [],
        compiler_params=_cp(("parallel", "arbitrary" if has_side else "parallel", "arbitrary")),
    )(*args)


def _rms_fwd(x, g, name):
    T, Dm = x.shape
    tm = min(512, T)

    def body(x_ref, g_ref, h_ref):
        xf = x_ref[...]
        r = lax.rsqrt(jnp.mean(xf * xf, axis=-1, keepdims=True) + RMS_EPS)
        h_ref[...] = ((xf * r) * g_ref[...]).astype(BF16)

    return pl.pallas_call(
        body, name=name, grid=(T // tm,),
        in_specs=[pl.BlockSpec((tm, Dm), lambda i: (i, 0)), pl.BlockSpec((1, Dm), lambda i: (0, 0))],
        out_specs=pl.BlockSpec((tm, Dm), lambda i: (i, 0)),
        out_shape=jax.ShapeDtypeStruct((T, Dm), BF16),
        compiler_params=_cp(("parallel",)),
    )(x, g.reshape(1, Dm))


def _rms_bwd(x, g, dh, dres, name):
    T, Dm = x.shape
    tm = min(512, T)

    def body(x_ref, g_ref, dh_ref, dres_ref, dx_ref, dxb_ref, dg_ref):
        i = pl.program_id(0)
        xf = x_ref[...]
        r = lax.rsqrt(jnp.mean(xf * xf, axis=-1, keepdims=True) + RMS_EPS)
        xn = xf * r
        dhf = dh_ref[...].astype(F32)
        dxn = dhf * g_ref[...]
        c = jnp.mean(dxn * xn, axis=-1, keepdims=True)
        dx = dres_ref[...] + r * (dxn - xn * c)
        dx_ref[...] = dx
        dxb_ref[...] = dx.astype(BF16)
        part = jnp.sum(dhf * xn, axis=0, keepdims=True)

        @pl.when(i == 0)
        def _():
            dg_ref[...] = part

        @pl.when(i > 0)
        def _():
            dg_ref[...] += part

    row = pl.BlockSpec((tm, Dm), lambda i: (i, 0))
    vec = pl.BlockSpec((1, Dm), lambda i: (0, 0))
    return pl.pallas_call(
        body, name=name, grid=(T // tm,), in_specs=[row, vec, row, row], out_specs=[row, row, vec],
        out_shape=[jax.ShapeDtypeStruct((T, Dm), F32), jax.ShapeDtypeStruct((T, Dm), BF16),
                   jax.ShapeDtypeStruct((1, Dm), F32)],
        compiler_params=_cp(("arbitrary",)),
    )(x, g.reshape(1, Dm), dh, dres)


def _loss_head(x, g, target, name):
    T, Dm = x.shape
    tm = min(512, T)

    def body(x_ref, g_ref, t_ref, loss_ref, dx_ref, dxb_ref, dg_ref):
        i = pl.program_id(0)
        xf = x_ref[...]
        gv = g_ref[...]
        r = lax.rsqrt(jnp.mean(xf * xf, axis=-1, keepdims=True) + RMS_EPS)
        xn = xf * r
        diff = xn * gv - t_ref[...]
        per_tok = jnp.mean(diff * diff, axis=-1, keepdims=True)
        lpart = 0.5 * jnp.sum(per_tok, axis=0, keepdims=True) + jnp.zeros((1, LANES), F32)
        dy = diff * (1.0 / Dm)
        dxn = dy * gv
        c = jnp.mean(dxn * xn, axis=-1, keepdims=True)
        dx = r * (dxn - xn * c)
        dx_ref[...] = dx
        dxb_ref[...] = dx.astype(BF16)
        part = jnp.sum(dy * xn, axis=0, keepdims=True)

        @pl.when(i == 0)
        def _():
            dg_ref[...] = part
            loss_ref[...] = lpart

        @pl.when(i > 0)
        def _():
            dg_ref[...] += part
            loss_ref[...] += lpart

    row = pl.BlockSpec((tm, Dm), lambda i: (i, 0))
    vec = pl.BlockSpec((1, Dm), lambda i: (0, 0))
    lsp = pl.BlockSpec((1, LANES), lambda i: (0, 0))
    return pl.pallas_call(
        body, name=name, grid=(T // tm,), in_specs=[row, vec, row], out_specs=[lsp, row, row, vec],
        out_shape=[jax.ShapeDtypeStruct((1, LANES), F32), jax.ShapeDtypeStruct((T, Dm), F32),
                   jax.ShapeDtypeStruct((T, Dm), BF16), jax.ShapeDtypeStruct((1, Dm), F32)],
        compiler_params=_cp(("arbitrary",)),
    )(x, g.reshape(1, Dm), target)


def _split_bf16(v):
    hi = v.astype(BF16)
    r1 = v - hi.astype(F32)
    mid = r1.astype(BF16)
    lo = (r1 - mid.astype(F32)).astype(BF16)
    return hi, mid, lo


def _tri_dot(tri, v):
    hi, mid, lo = _split_bf16(v)
    dot = functools.partial(jnp.dot, preferred_element_type=F32)
    return dot(tri, hi) + dot(tri, mid) + dot(tri, lo)


def _log_sigmoid(z):
    return jnp.minimum(z, 0.0) - jnp.log(1.0 + jnp.exp(-jnp.abs(z)))


def _fox_cumsum_bwd(f, bf, dF, n_seq, name):
    T = f.shape[0]
    S = T // n_seq
    c = min(CUM_BLK, S)

    def body(f_ref, b_ref, dF_ref, df_ref, db_ref):
        b = pl.program_id(0)
        ri = lax.broadcasted_iota(jnp.int32, (c, c), 0)
        ci = lax.broadcasted_iota(jnp.int32, (c, c), 1)
        tri = (ri <= ci).astype(BF16)
        carry = jnp.zeros((1, LANES), F32)
        dbp = jnp.zeros((1, LANES), F32)
        for j in reversed(range(S // c)):
            dFc = dF_ref[j * c:(j + 1) * c, :]
            dlf = _tri_dot(tri, dFc) + carry
            carry = carry + jnp.sum(dFc, axis=0, keepdims=True)
            z = f_ref[j * c:(j + 1) * c, :] + b_ref[...]
            dz = dlf * _sigmoid(-z)
            df_ref[j * c:(j + 1) * c, :] = dz.astype(BF16)
            dbp = dbp + jnp.sum(dz, axis=0, keepdims=True)

        @pl.when(b == 0)
        def _():
            db_ref[...] = dbp

        @pl.when(b > 0)
        def _():
            db_ref[...] += dbp

    blk = pl.BlockSpec((S, LANES), lambda b: (b, 0))
    vec = pl.BlockSpec((1, LANES), lambda b: (0, 0))
    return pl.pallas_call(
        body, name=name, grid=(n_seq,), in_specs=[blk, vec, blk], out_specs=[blk, vec],
        out_shape=[jax.ShapeDtypeStruct((T, LANES), BF16), jax.ShapeDtypeStruct((1, LANES), F32)],
        compiler_params=_cp(("arbitrary",)),
    )(f, bf, dF)


def _pair_masks():
    lane = lax.broadcasted_iota(jnp.int32, (1, LANES), 1)
    lo = lane < HEAD_DIM
    return lo, jnp.logical_not(lo)


AUG0 = HEAD_DIM
Q_TILE, K_CHUNK, ROW_GROUP = 512, 256, 64


def _fox_prep(f, bf, proj, n_seq, name):
    T = f.shape[0]
    S = T // n_seq
    c = min(CUM_BLK, S)

    def body(f_ref, b_ref, qkv_ref, qa_ref, ka_ref, va_ref):
        ri = lax.broadcasted_iota(jnp.int32, (c, c), 0)
        ci = lax.broadcasted_iota(jnp.int32, (c, c), 1)
        tri = (ri >= ci).astype(BF16)
        lane = lax.broadcasted_iota(jnp.int32, (c, LANES), 1)
        carry = jnp.zeros((1, LANES), F32)
        for j in range(S // c):
            rows = slice(j * c, (j + 1) * c)
            lf = _log_sigmoid(f_ref[rows, :] + b_ref[...])
            Fc = _tri_dot(tri, lf) + carry
            carry = carry + jnp.sum(lf, axis=0, keepdims=True)
            for h in range(N_HEADS):
                col = jnp.sum(jnp.where(lane == h, Fc, 0.0), axis=-1, keepdims=True)
                hi = col.astype(BF16).astype(F32)
                r1 = col - hi
                mid = r1.astype(BF16).astype(F32)
                lo = r1 - mid
                ones_q = jnp.logical_and(lane >= AUG0 + 3, lane < AUG0 + 6)
                ones_k = jnp.logical_and(lane >= AUG0, lane < AUG0 + 3)
                aug_q = jnp.where(lane == AUG0, hi, jnp.where(lane == AUG0 + 1, mid, jnp.where(
                    lane == AUG0 + 2, lo, jnp.where(ones_q, 1.0, 0.0))))
                aug_k = jnp.where(lane == AUG0 + 3, -hi, jnp.where(lane == AUG0 + 4, -mid, jnp.where(
                    lane == AUG0 + 5, -lo, jnp.where(ones_k, 1.0, 0.0))))
                base = (h // 2) * TRIPLE
                qp, kp, vp = (qkv_ref[rows, base + t * LANES:base + (t + 1) * LANES].astype(F32) for t in range(3))
                if h % 2:
                    qp, kp, vp = (pltpu.roll(a, HEAD_DIM, 1) for a in (qp, kp, vp))
                out = slice(h * LANES, (h + 1) * LANES)
                qa_ref[rows, out] = jnp.where(lane < HEAD_DIM, qp * (HEAD_DIM ** -0.5), aug_q).astype(BF16)
                ka_ref[rows, out] = jnp.where(lane < HEAD_DIM, kp, aug_k).astype(BF16)
                va_ref[rows, out] = jnp.where(lane < HEAD_DIM, vp, jnp.where(lane == AUG0, 1.0, 0.0)).astype(BF16)

    fblk = pl.BlockSpec((S, LANES), lambda b: (b, 0))
    out = pl.BlockSpec((S, N_HEADS * LANES), lambda b: (b, 0))
    sh = jax.ShapeDtypeStruct((T, N_HEADS * LANES), BF16)
    return pl.pallas_call(
        body, name=name, grid=(n_seq,),
        in_specs=[fblk, pl.BlockSpec((1, LANES), lambda b: (0, 0)),
                  pl.BlockSpec((S, 4 * TRIPLE), lambda b: (b, OFF_QKV // (4 * TRIPLE)))],
        out_specs=[out, out, out], out_shape=[sh, sh, sh],
        compiler_params=_cp(("parallel",)),
    )(f, bf, proj)


def _band_mask(q0, k0, nq, nk):
    row = q0 + lax.broadcasted_iota(jnp.int32, (nq, nk), 0)
    col = k0 + lax.broadcasted_iota(jnp.int32, (nq, nk), 1)
    return col <= row


_NT = (((1,), (1,)), ((), ()))
_TN = (((0,), (0,)), ((), ()))


def _attn_fwd2(qa, ka, va, n_seq, name):
    T = qa.shape[0]
    S = T // n_seq
    tq, tk, rg = min(Q_TILE, S), min(K_CHUNK, S), ROW_GROUP
    nq, per = S // tq, tq // tk

    def body(q_ref, k_ref, v_ref, o_ref, o32_ref, lse_ref, phi_s, plo_s, mp_s, m_s, acc_s):
        qi = pl.program_id(2)
        mp_s[...] = jnp.full_like(mp_s, NEG_INF)
        acc_s[...] = jnp.zeros_like(acc_s)

        def scores(kc, hh, r0):
            k0 = pl.multiple_of(kc * tk, tk)
            hl = slice(hh * LANES, (hh + 1) * LANES)
            return k0, lax.dot_general(q_ref[r0:, hl], k_ref[pl.ds(k0, tk), hl], _NT, preferred_element_type=F32)

        def max_chunk(kc, masked, r0):
            for hh in range(2):
                k0, s_all = scores(kc, hh, r0)
                for r in range(r0 // rg, tq // rg):
                    rows = slice(r * rg, (r + 1) * rg)
                    s = s_all[r * rg - r0:(r + 1) * rg - r0, :]
                    if masked:
                        s = jnp.where(_band_mask(qi * tq + r * rg, k0, rg, tk), s, NEG_INF)
                    part = s[:, :LANES]
                    for c in range(1, tk // LANES):
                        part = jnp.maximum(part, s[:, c * LANES:(c + 1) * LANES])
                    mp_s[hh, rows, :] = jnp.maximum(mp_s[hh, rows, :], part)

        def sum_chunk(kc, masked, r0):
            for hh in range(2):
                k0, s_all = scores(kc, hh, r0)
                hl = slice(hh * LANES, (hh + 1) * LANES)
                v = v_ref[pl.ds(k0, tk), hl]
                for r in range(r0 // rg, tq // rg):
                    rows = slice(r * rg, (r + 1) * rg)
                    p = jnp.exp(s_all[r * rg - r0:(r + 1) * rg - r0, :] - m_s[hh, rows])
                    if masked:
                        p = jnp.where(_band_mask(qi * tq + r * rg, k0, rg, tk), p, 0.0)
                    p_hi = p.astype(BF16)
                    phi_s[hh, rows, :] = p_hi
                    plo_s[hh, rows, :] = (p - p_hi.astype(F32)).astype(BF16)
                acc_s[hh, r0:, :] += (jnp.dot(phi_s[hh, r0:, :], v, preferred_element_type=F32)
                                      + jnp.dot(plo_s[hh, r0:, :], v, preferred_element_type=F32))

        def sweep(chunk):
            def unmasked(kc, carry):
                chunk(kc, False, 0)
                return carry

            lax.fori_loop(0, qi * per, unmasked, 0)
            for d in range(per):
                chunk(qi * per + d, True, d * tk)

        sweep(max_chunk)
        m_s[...] = jnp.max(mp_s[...], axis=-1, keepdims=True)
        sweep(sum_chunk)

        lane = lax.broadcasted_iota(jnp.int32, (1, LANES), 1)
        outs = []
        for hh in range(2):
            acc = acc_s[hh]
            l = jnp.sum(jnp.where(lane == AUG0, acc, 0.0), axis=-1, keepdims=True)
            lse_ref[hh] = m_s[hh] + jnp.log(l)
            outs.append(acc / l)
        o = jnp.where(lane < HEAD_DIM, outs[0], pltpu.roll(outs[1], HEAD_DIM, 1))
        o_ref[...] = o.astype(BF16)
        o32_ref[...] = o

    qmap = lambda b, j, qi: (b * nq + qi, j)
    omap = lambda b, j, qi: (b * nq + qi, j)
    kv = pl.BlockSpec((S, 2 * LANES), lambda b, j, qi: (b, j))
    return pl.pallas_call(
        body, name=name, grid=(n_seq, N_HEADS // 2, nq),
        in_specs=[pl.BlockSpec((tq, 2 * LANES), qmap), kv, kv],
        out_specs=[pl.BlockSpec((tq, LANES), omap), pl.BlockSpec((tq, LANES), omap),
                   pl.BlockSpec((2, tq, 1), lambda b, j, qi: (j, b * nq + qi, 0))],
        out_shape=[jax.ShapeDtypeStruct((T, BRANCH_W), BF16), jax.ShapeDtypeStruct((T, BRANCH_W), F32),
                   jax.ShapeDtypeStruct((N_HEADS, T, 1), F32)],
        scratch_shapes=[pltpu.VMEM((2, tq, tk), BF16), pltpu.VMEM((2, tq, tk), BF16),
                        pltpu.VMEM((2, tq, LANES), F32), pltpu.VMEM((2, tq, 1), F32),
                        pltpu.VMEM((2, tq, LANES), F32)],
        compiler_params=_cp(("parallel", "parallel", "parallel")),
    )(qa, ka, va)


def _attn_bwd(qa, ka, proj, do, o32, lse, dproj, n_seq, name):
    T = qa.shape[0]
    S = T // n_seq
    tq, tk, rg = min(Q_TILE, S), min(K_CHUNK, S), ROW_GROUP
    nq, per, nkc = S // tq, tq // tk, S // tk

    def body(q_ref, k_ref, v_ref, do_ref, o_ref, lse_ref, _, dqkv_ref, dfk_ref,
             p_s, ds_s, dq_s, dk_s, dv_s, df_s):
        dk_s[...] = jnp.zeros_like(dk_s)
        dv_s[...] = jnp.zeros_like(dv_s)
        df_s[...] = jnp.zeros_like(df_s)
        sels = _pair_masks()

        for qi in range(nq):
            q0 = qi * tq
            do_t = do_ref[q0:q0 + tq, :]
            dq_s[...] = jnp.zeros_like(dq_s)
            prod = do_t.astype(F32) * o_ref[q0:q0 + tq, :]
            dls = [jnp.sum(jnp.where(sel, prod, 0.0), axis=-1, keepdims=True) for sel in sels]

            def chunk(kc, masked, r0, q0=q0, do_t=do_t, dls=dls):
                k0 = pl.multiple_of(kc * tk, tk)
                v = v_ref[pl.ds(k0, tk), :]
                do_a = do_t[r0:, :]
                for hh in range(2):
                    hl = slice(hh * LANES, (hh + 1) * LANES)
                    qh, kh = q_ref[q0 + r0:q0 + tq, hl], k_ref[pl.ds(k0, tk), hl]
                    s_all = lax.dot_general(qh, kh, _NT, preferred_element_type=F32)
                    dom = jnp.where(sels[hh], do_a, jnp.zeros_like(do_a))
                    dp_all = lax.dot_general(dom, v, _NT, preferred_element_type=F32)
                    dfp = jnp.zeros((1, tk), F32)
                    for r in range(r0 // rg, tq // rg):
                        rows = slice(r * rg, (r + 1) * rg)
                        arows = slice(r * rg - r0, (r + 1) * rg - r0)
                        qrows = slice(q0 + r * rg, q0 + (r + 1) * rg)
                        p = jnp.exp(s_all[arows, :] - lse_ref[hh, qrows])
                        if masked:
                            p = jnp.where(_band_mask(q0 + r * rg, k0, rg, tk), p, 0.0)
                        ds = p * (dp_all[arows, :] - dls[hh][rows])
                        p_s[hh, rows, :] = p.astype(BF16)
                        ds_s[hh, rows, :] = ds.astype(BF16)
                        dfp = dfp + jnp.sum(ds, axis=0, keepdims=True)
                    df_s[hh, kc] -= dfp
                    dq_s[hh, r0:, :] += jnp.dot(ds_s[hh, r0:, :], kh, preferred_element_type=F32)
                    dv_s[hh, pl.ds(k0, tk), :] += lax.dot_general(p_s[hh, r0:, :], do_a, _TN,
                                                                  preferred_element_type=F32)
                    dk_s[hh, pl.ds(k0, tk), :] += lax.dot_general(ds_s[hh, r0:, :], qh, _TN,
                                                                  preferred_element_type=F32)

            def unmasked(kc, carry, chunk=chunk):
                chunk(kc, False, 0)
                return carry

            lax.fori_loop(0, qi * per, unmasked, 0)
            for d in range(per):
                chunk(qi * per + d, True, d * tk)
            dq = jnp.where(sels[0], dq_s[0], pltpu.roll(dq_s[1], HEAD_DIM, 1))
            dqkv_ref[q0:q0 + tq, :LANES] = (dq * (HEAD_DIM ** -0.5)).astype(BF16)

        dqkv_ref[:, LANES:2 * LANES] = jnp.where(sels[0], dk_s[0], pltpu.roll(dk_s[1], HEAD_DIM, 1)).astype(BF16)
        dqkv_ref[:, 2 * LANES:] = jnp.where(sels[0], dv_s[0], dv_s[1]).astype(BF16)
        for c in range(nkc):
            dfk_ref[:, :, c * tk:(c + 1) * tk] = df_s[:, c]

    seq = lambda w: pl.BlockSpec((S, w), lambda b, j: (b, j))
    col1 = pl.BlockSpec((2, S, 1), lambda b, j: (j, b, 0))
    vblk = pl.BlockSpec((S, LANES), lambda b, j: (b, OFF_QKV // LANES + 3 * j + 2))
    return pl.pallas_call(
        body, name=name, grid=(n_seq, N_HEADS // 2),
        in_specs=[seq(2 * LANES), seq(2 * LANES), vblk, seq(LANES), seq(LANES), col1,
                  pl.BlockSpec(memory_space=pl.ANY)],
        out_specs=[pl.BlockSpec((S, TRIPLE), lambda b, j: (b, OFF_QKV // TRIPLE + j)),
                   pl.BlockSpec((2, 1, S), lambda b, j: (j, 0, b))],
        out_shape=[jax.ShapeDtypeStruct(dproj.shape, BF16), jax.ShapeDtypeStruct((N_HEADS, 1, T), F32)],
        input_output_aliases={6: 0},
        scratch_shapes=[pltpu.VMEM((2, tq, tk), BF16), pltpu.VMEM((2, tq, tk), BF16),
                        pltpu.VMEM((2, tq, LANES), F32), pltpu.VMEM((2, S, LANES), F32),
                        pltpu.VMEM((2, S, LANES), F32), pltpu.VMEM((2, nkc, 1, tk), F32)],
        compiler_params=_cp(("parallel", "parallel")),
    )(qa, ka, proj, do, o32, lse, dproj)


def _shift_down(v, k, row):
    return jnp.where(row >= k, pltpu.roll(v, k, 0), 0.0)


def _shift_up(v, k, row, S):
    return jnp.where(row < S - k, pltpu.roll(v, S - k, 0), 0.0)


def _pool_diff(uf, w, row):
    acc, k = uf, 1
    while k < w:
        acc = acc + _shift_down(acc, k, row)
        k *= 2
    n = jnp.minimum(row + 1, w).astype(F32)
    return acc / n - uf


def _pool_fwd(proj, pool_w, pool_scale, n_seq, name):
    T = proj.shape[0]
    S = T // n_seq

    def body(u_ref, w_ref, sc_ref, o_ref, d_s):
        g = pl.program_id(1)
        row = lax.broadcasted_iota(jnp.int32, (S, POOL_GD), 0)
        uf = u_ref[...].astype(F32)
        for gi, wlen in enumerate(POOL_WINDOWS):
            @pl.when(g == gi)
            def _(wlen=wlen):
                d_s[...] = _pool_diff(uf, wlen, row).astype(BF16)
        e = jnp.dot(d_s[...], w_ref[0], preferred_element_type=F32)
        o_ref[...] = (e * sc_ref[...]).astype(BF16)

    uc = OFF_U // POOL_GD
    return pl.pallas_call(
        body, name=name, grid=(n_seq, len(POOL_WINDOWS)),
        in_specs=[pl.BlockSpec((S, POOL_GD), lambda b, g: (b, uc + g)),
                  pl.BlockSpec((1, POOL_GD, POOL_GD), lambda b, g: (g, 0, 0)),
                  pl.BlockSpec((1, POOL_GD), lambda b, g: (0, g))],
        out_specs=pl.BlockSpec((S, POOL_GD), lambda b, g: (b, g)),
        out_shape=jax.ShapeDtypeStruct((T, BRANCH_W), BF16),
        scratch_shapes=[pltpu.VMEM((S, POOL_GD), BF16)],
        compiler_params=_cp(("parallel", "parallel")),
    )(proj, pool_w, pool_scale)


def _pool_bwd(proj, dout, pool_w, pool_scale, dproj, n_seq, name):
    T = proj.shape[0]
    S = T // n_seq

    def body(u_ref, do_ref, w_ref, sc_ref, _, du_ref, dw_ref, dsc_ref, d_s):
        g, b = pl.program_id(0), pl.program_id(1)
        row = lax.broadcasted_iota(jnp.int32, (S, POOL_GD), 0)
        uf = u_ref[...].astype(F32)
        for gi, wlen in enumerate(POOL_WINDOWS):
            @pl.when(g == gi)
            def _(wlen=wlen):
                d_s[...] = _pool_diff(uf, wlen, row).astype(BF16)
        db16 = d_s[...]
        w = w_ref[0]
        e = jnp.dot(db16, w, preferred_element_type=F32)
        dof = do_ref[...].astype(F32)
        dsc = jnp.sum(dof * e, axis=0, keepdims=True)
        de = (dof * sc_ref[...]).astype(BF16)
        dd = lax.dot_general(de, w, (((1,), (1,)), ((), ())), preferred_element_type=F32)
        dw = lax.dot_general(db16, de, (((0,), (0,)), ((), ())), preferred_element_type=F32)
        for gi, wlen in enumerate(POOL_WINDOWS):
            @pl.when(g == gi)
            def _(wlen=wlen):
                n = jnp.minimum(row + 1, wlen).astype(F32)
                acc, k = dd / n, 1
                while k < wlen:
                    acc = acc + _shift_up(acc, k, row, S)
                    k *= 2
                du_ref[...] = (acc - dd).astype(BF16)

        @pl.when(b == 0)
        def _():
            dw_ref[0] = dw
            dsc_ref[...] = dsc

        @pl.when(b > 0)
        def _():
            dw_ref[0] += dw
            dsc_ref[...] += dsc

    uc = OFF_U // POOL_GD
    return pl.pallas_call(
        body, name=name, grid=(len(POOL_WINDOWS), n_seq),
        in_specs=[pl.BlockSpec((S, POOL_GD), lambda g, b: (b, uc + g)),
                  pl.BlockSpec((S, POOL_GD), lambda g, b: (b, g)),
                  pl.BlockSpec((1, POOL_GD, POOL_GD), lambda g, b: (g, 0, 0)),
                  pl.BlockSpec((1, POOL_GD), lambda g, b: (0, g)),
                  pl.BlockSpec(memory_space=pl.ANY)],
        out_specs=[pl.BlockSpec((S, POOL_GD), lambda g, b: (b, uc + g)),
                   pl.BlockSpec((1, POOL_GD, POOL_GD), lambda g, b: (g, 0, 0)),
                   pl.BlockSpec((1, POOL_GD), lambda g, b: (0, g))],
        out_shape=[jax.ShapeDtypeStruct(dproj.shape, BF16),
                   jax.ShapeDtypeStruct((len(POOL_WINDOWS), POOL_GD, POOL_GD), F32),
                   jax.ShapeDtypeStruct((1, BRANCH_W), F32)],
        input_output_aliases={4: 0},
        scratch_shapes=[pltpu.VMEM((S, POOL_GD), BF16)],
        compiler_params=_cp(("parallel", "arbitrary")),
    )(proj, dout, pool_w, pool_scale, dproj)


def _conv_fwd(proj, conv_w, n_seq, name):
    T = proj.shape[0]
    S = T // n_seq
    nc = BRANCH_W // LANES

    def body(c_ref, w_ref, o_ref):
        row = lax.broadcasted_iota(jnp.int32, (S, LANES), 0)
        cv, cb, cc = (c_ref[:, t * LANES:(t + 1) * LANES].astype(F32) for t in range(3))
        z = cc * cv
        w = w_ref[...]
        y = w[0:1] * _shift_down(z, 2, row) + w[1:2] * _shift_down(z, 1, row) + w[2:3] * z
        o_ref[...] = (cb * y).astype(BF16)

    return pl.pallas_call(
        body, name=name, grid=(n_seq, nc),
        in_specs=[pl.BlockSpec((S, TRIPLE), lambda b, j: (b, OFF_CONV // TRIPLE + j)),
                  pl.BlockSpec((CONV_K, LANES), lambda b, j: (0, j))],
        out_specs=pl.BlockSpec((S, LANES), lambda b, j: (b, j)),
        out_shape=jax.ShapeDtypeStruct((T, BRANCH_W), BF16),
        compiler_params=_cp(("parallel", "parallel")),
    )(proj, conv_w)


def _conv_bwd(proj, dout, conv_w, dproj, n_seq, name):
    T = proj.shape[0]
    S = T // n_seq
    nc = BRANCH_W // LANES

    def body(c_ref, do_ref, w_ref, _, dc_ref, dw_ref):
        b = pl.program_id(1)
        row = lax.broadcasted_iota(jnp.int32, (S, LANES), 0)
        cv, cb, cc = (c_ref[:, t * LANES:(t + 1) * LANES].astype(F32) for t in range(3))
        dof = do_ref[...].astype(F32)
        w = w_ref[...]
        z = cc * cv
        z1, z2 = _shift_down(z, 1, row), _shift_down(z, 2, row)
        y = w[0:1] * z2 + w[1:2] * z1 + w[2:3] * z
        dy = dof * cb
        dz = w[2:3] * dy + w[1:2] * _shift_up(dy, 1, row, S) + w[0:1] * _shift_up(dy, 2, row, S)
        dc_ref[:, :LANES] = (dz * cc).astype(BF16)
        dc_ref[:, LANES:2 * LANES] = (dof * y).astype(BF16)
        dc_ref[:, 2 * LANES:] = (dz * cv).astype(BF16)
        dws = [jnp.sum(dy * zk, axis=0, keepdims=True) for zk in (z2, z1, z)]

        @pl.when(b == 0)
        def _():
            for kk in range(CONV_K):
                dw_ref[kk:kk + 1, :] = dws[kk]

        @pl.when(b > 0)
        def _():
            for kk in range(CONV_K):
                dw_ref[kk:kk + 1, :] += dws[kk]

    triple = pl.BlockSpec((S, TRIPLE), lambda j, b: (b, OFF_CONV // TRIPLE + j))
    wsp = pl.BlockSpec((CONV_K, LANES), lambda j, b: (0, j))
    return pl.pallas_call(
        body, name=name, grid=(nc, n_seq),
        in_specs=[triple, pl.BlockSpec((S, LANES), lambda j, b: (b, j)), wsp, pl.BlockSpec(memory_space=pl.ANY)],
        out_specs=[triple, wsp],
        out_shape=[jax.ShapeDtypeStruct(dproj.shape, BF16), jax.ShapeDtypeStruct((CONV_K, BRANCH_W), F32)],
        input_output_aliases={3: 0},
        compiler_params=_cp(("parallel", "arbitrary")),
    )(proj, dout, conv_w, dproj)


def _mix_fwd(oa, ob, oc, wpa, wpp, wpc, proj, b_gate, name):
    T = oa.shape[0]
    tm = min(512, T)

    def body(oa_ref, ob_ref, oc_ref, wa_ref, wp_ref, wc_ref, g_ref, bg_ref, o_ref):
        acc = jnp.zeros((tm, D_MODEL), F32)
        for i, (x_ref, w_ref) in enumerate(((oa_ref, wa_ref), (ob_ref, wp_ref), (oc_ref, wc_ref))):
            y = jnp.dot(x_ref[...], w_ref[...], preferred_element_type=F32)
            sl = slice(i * D_MODEL, (i + 1) * D_MODEL)
            acc = acc + _sigmoid(g_ref[:, sl].astype(F32) + bg_ref[:, sl]) * y
        o_ref[...] = acc.astype(BF16)

    br = pl.BlockSpec((tm, BRANCH_W), lambda i: (i, 0))
    wsp = pl.BlockSpec((BRANCH_W, D_MODEL), lambda i: (0, 0))
    return pl.pallas_call(
        body, name=name, grid=(T // tm,),
        in_specs=[br, br, br, wsp, wsp, wsp, pl.BlockSpec((tm, GATE_W), lambda i: (i, 0)),
                  pl.BlockSpec((1, GATE_W), lambda i: (0, 0))],
        out_specs=pl.BlockSpec((tm, D_MODEL), lambda i: (i, 0)),
        out_shape=jax.ShapeDtypeStruct((T, D_MODEL), BF16),
        compiler_params=_cp(("parallel",)),
    )(oa, ob, oc, wpa, wpp, wpc, proj, b_gate)


def _mix_bwd(oa, ob, oc, wpa, wpp, wpc, proj, b_gate, dmixed, name):
    T = oa.shape[0]
    tm = min(256, T)

    def body(oa_ref, ob_ref, oc_ref, wa_ref, wp_ref, wc_ref, g_ref, bg_ref, dm_ref,
             dya_ref, dyb_ref, dyc_ref, dg_ref, dbg_ref):
        i0 = pl.program_id(0)
        dm = dm_ref[...].astype(F32)
        parts = []
        for i, (x_ref, w_ref, dy_ref) in enumerate(((oa_ref, wa_ref, dya_ref), (ob_ref, wp_ref, dyb_ref),
                                                    (oc_ref, wc_ref, dyc_ref))):
            y = jnp.dot(x_ref[...], w_ref[...], preferred_element_type=F32)
            sl = slice(i * D_MODEL, (i + 1) * D_MODEL)
            gate = _sigmoid(g_ref[:, sl].astype(F32) + bg_ref[:, sl])
            dy_ref[...] = (dm * gate).astype(BF16)
            dgl = dm * y * gate * (1.0 - gate)
            dg_ref[:, sl] = dgl.astype(BF16)
            parts.append(jnp.sum(dgl, axis=0, keepdims=True))

        @pl.when(i0 == 0)
        def _():
            for i in range(3):
                dbg_ref[:, i * D_MODEL:(i + 1) * D_MODEL] = parts[i]

        @pl.when(i0 > 0)
        def _():
            for i in range(3):
                dbg_ref[:, i * D_MODEL:(i + 1) * D_MODEL] += parts[i]

    br = pl.BlockSpec((tm, BRANCH_W), lambda i: (i, 0))
    wsp = pl.BlockSpec((BRANCH_W, D_MODEL), lambda i: (0, 0))
    row = pl.BlockSpec((tm, D_MODEL), lambda i: (i, 0))
    gsp = pl.BlockSpec((tm, GATE_W), lambda i: (i, 0))
    bsp = pl.BlockSpec((1, GATE_W), lambda i: (0, 0))
    act = jax.ShapeDtypeStruct((T, D_MODEL), BF16)
    return pl.pallas_call(
        body, name=name, grid=(T // tm,),
        in_specs=[br, br, br, wsp, wsp, wsp, gsp, bsp, row],
        out_specs=[row, row, row, gsp, bsp],
        out_shape=[act, act, act, jax.ShapeDtypeStruct((T, MAIN_COLS), BF16),
                   jax.ShapeDtypeStruct((1, GATE_W), F32)],
        compiler_params=_cp(("arbitrary",)),
    )(oa, ob, oc, wpa, wpp, wpc, proj, b_gate, dmixed)


GU_TILE = 256


def _gu_col(c):
    t, r = divmod(c, GU_TILE)
    return (t // 2) * GU_TILE + r + (FFN_HIDDEN if t % 2 else 0)


def _gate_up_swiglu(h, w, name):
    T, K = h.shape
    tm = min(2048, T)

    def body(h_ref, w_ref, ab_ref, s_ref):
        prod = jnp.dot(h_ref[...], w_ref[...], preferred_element_type=F32)
        ab_ref[...] = prod.astype(BF16)
        a = prod[:, :GU_TILE]
        s_ref[...] = (a * _sigmoid(a) * prod[:, GU_TILE:]).astype(BF16)

    return pl.pallas_call(
        body, name=name, grid=(T // tm, FFN_HIDDEN // GU_TILE),
        in_specs=[pl.BlockSpec((tm, K), lambda i, j: (i, 0)), pl.BlockSpec((K, 2 * GU_TILE), lambda i, j: (0, j))],
        out_specs=[pl.BlockSpec((tm, 2 * GU_TILE), lambda i, j: (i, j)), pl.BlockSpec((tm, GU_TILE), lambda i, j: (i, j))],
        out_shape=[jax.ShapeDtypeStruct((T, 2 * FFN_HIDDEN), BF16), jax.ShapeDtypeStruct((T, FFN_HIDDEN), BF16)],
        compiler_params=_cp(("parallel", "parallel")),
    )(h, w)


def _swiglu_bwd_fused(dx, w_down, ab, name):
    T, K = dx.shape
    tm = min(2048, T)

    def body(dx_ref, w_ref, ab_ref, o_ref):
        ds = lax.dot_general(dx_ref[...], w_ref[...], _NT, preferred_element_type=F32)
        a = ab_ref[:, :GU_TILE].astype(F32)
        b = ab_ref[:, GU_TILE:].astype(F32)
        sg = _sigmoid(a)
        o_ref[:, :GU_TILE] = (ds * b * sg * (1.0 + a * (1.0 - sg))).astype(BF16)
        o_ref[:, GU_TILE:] = (ds * a * sg).astype(BF16)

    pair = pl.BlockSpec((tm, 2 * GU_TILE), lambda i, j: (i, j))
    return pl.pallas_call(
        body, name=name, grid=(T // tm, FFN_HIDDEN // GU_TILE),
        in_specs=[pl.BlockSpec((tm, K), lambda i, j: (i, 0)), pl.BlockSpec((GU_TILE, K), lambda i, j: (j, 0)), pair],
        out_specs=pair, out_shape=jax.ShapeDtypeStruct((T, 2 * FFN_HIDDEN), BF16),
        compiler_params=_cp(("parallel", "parallel")),
    )(dx, w_down, ab)


def _adamw_update(w_ref, g_ref, m_ref, v_ref, d_ref, nm_ref, nv_ref):
    gv = g_ref[...]
    nm = ADAM_B1 * m_ref[...] + (1.0 - ADAM_B1) * gv
    nv = ADAM_B2 * v_ref[...] + (1.0 - ADAM_B2) * (gv * gv)
    m_hat = nm / (1.0 - ADAM_B1 ** ADAM_STEP)
    v_hat = nv / (1.0 - ADAM_B2 ** ADAM_STEP)
    d_ref[...] = -ADAM_LR * (m_hat / (jnp.sqrt(v_hat) + ADAM_EPS) + ADAM_WD * w_ref[...])
    nm_ref[...] = nm
    nv_ref[...] = nv


def _adamw_many(ws, gs, ms, vs, name):
    n = len(ws)

    def body(*refs):
        ins, outs = refs[:4 * n], refs[4 * n:]
        for t in range(n):
            _adamw_update(ins[t], ins[n + t], ins[2 * n + t], ins[3 * n + t], outs[t], outs[n + t], outs[2 * n + t])

    shapes = [jax.ShapeDtypeStruct(w.shape, F32) for w in ws]
    out = pl.pallas_call(body, name=name, out_shape=shapes * 3, compiler_params=_cp())(*ws, *gs, *ms, *vs)
    return out[:n], out[n:2 * n], out[2 * n:]


def _adamw(w, g, m, v, name):
    R, C = w.shape
    tr = R
    for cand in (256, 352, 128, 64, 8):
        if R > cand and R % cand == 0:
            tr = cand
            break

    def body(w_ref, g_ref, m_ref, v_ref, d_ref, nm_ref, nv_ref):
        _adamw_update(w_ref, g_ref, m_ref, v_ref, d_ref, nm_ref, nv_ref)

    blk = pl.BlockSpec((tr, C), lambda i: (i, 0))
    sh = jax.ShapeDtypeStruct((R, C), F32)
    return pl.pallas_call(
        body, name=name, grid=(R // tr,), in_specs=[blk] * 4, out_specs=[blk] * 3, out_shape=[sh] * 3,
        compiler_params=_cp(("parallel",)),
    )(w, g, m, v)


def _adamw_3d(w, g, m, v, block, name):
    shape = w.shape
    grid = (shape[0] // block[0], shape[1] // block[1])
    assert shape[0] % block[0] == 0 and shape[1] % block[1] == 0 and block[2] == shape[2], (name, shape, block)

    def body(w_ref, g_ref, m_ref, v_ref, d_ref, nm_ref, nv_ref):
        _adamw_update(w_ref, g_ref, m_ref, v_ref, d_ref, nm_ref, nv_ref)

    blk = pl.BlockSpec(block, lambda i, j: (i, j, 0))
    sh = jax.ShapeDtypeStruct(shape, F32)
    return pl.pallas_call(
        body, name=name, grid=grid, in_specs=[blk] * 4, out_specs=[blk] * 3, out_shape=[sh] * 3,
        compiler_params=_cp(("parallel", "parallel")),
    )(w, g, m, v)


def _sum_slabs_t(x, name):
    n, R, C = x.shape

    def body(x_ref, o_ref):
        acc = x_ref[0].astype(F32)
        for j in range(1, n):
            acc = acc + x_ref[j].astype(F32)
        o_ref[...] = acc.T

    return pl.pallas_call(
        body, name=name, grid=(C // LANES,), in_specs=[pl.BlockSpec((n, R, LANES), lambda j: (0, 0, j))],
        out_specs=pl.BlockSpec((LANES, R), lambda j: (j, 0)), out_shape=jax.ShapeDtypeStruct((C, R), F32),
        compiler_params=_cp(("parallel",)),
    )(x)


def _sum_slabs(x, name):
    n, R, C = x.shape
    tr = R
    for cand in (512, 256, 128, 64, 32, 16, 8):
        if R > cand and R % cand == 0:
            tr = cand
            break

    def body(x_ref, o_ref):
        acc = x_ref[0].astype(F32)
        for j in range(1, n):
            acc = acc + x_ref[j].astype(F32)
        o_ref[...] = acc

    return pl.pallas_call(
        body, name=name, grid=(R // tr,), in_specs=[pl.BlockSpec((n, tr, C), lambda i: (0, i, 0))],
        out_specs=pl.BlockSpec((tr, C), lambda i: (i, 0)), out_shape=jax.ShapeDtypeStruct((R, C), F32),
        compiler_params=_cp(("parallel",)),
    )(x)


def _multi_gather(xs, layers, name):
    nt = len(xs)
    shapes = [x.shape if lay is None else x.shape[1:] for x, lay in zip(xs, layers)]

    def body(*refs):
        x_refs, out_refs = refs[:nt], refs[nt:2 * nt]
        send_sems, recv_sems, local_sems = refs[2 * nt:]
        x_, y_, c_ = lax.axis_index("x"), lax.axis_index("y"), lax.axis_index("c")
        me, sibling = (x_, y_, c_), (x_, y_, 1 - c_)
        chips = [(1 - x_, y_), (x_, 1 - y_), (1 - x_, 1 - y_)]

        def own_block(t):
            return x_refs[t] if layers[t] is None else x_refs[t].at[layers[t]]

        def copy(t, k, block, to, own=False):
            px, py, pc = block
            dst = out_refs[t].at[4 * px + 2 * py + pc]
            return pltpu.make_async_remote_copy(
                src_ref=own_block(t) if own else dst, dst_ref=dst,
                send_sem=send_sems.at[t, k], recv_sem=recv_sems.at[t, k],
                device_id=to, device_id_type=pl.DeviceIdType.MESH)

        mine, first, passed = [], [], []
        for t in range(nt):
            mine.append(pltpu.make_async_copy(own_block(t), out_refs[t].at[4 * x_ + 2 * y_ + c_], local_sems.at[t]))
            mine[-1].start()
            first.append([copy(t, 1 + j, me, (*chip, c_), own=True) for j, chip in enumerate(chips)]
                         + [copy(t, 0, me, sibling, own=True)])
            for cp in first[-1]:
                cp.start()
        for t in range(nt):
            for j, chip in enumerate(chips):
                copy(t, 1 + j, (*chip, c_), me).wait_recv()
                passed.append(copy(t, 4 + j, (*chip, c_), sibling))
                passed[-1].start()
        for t in range(nt):
            copy(t, 0, sibling, me).wait_recv()
            for j, chip in enumerate(chips):
                copy(t, 4 + j, (*chip, 1 - c_), me).wait_recv()
        for cp in [c for f in first for c in f] + passed:
            cp.wait_send()
        for cp in mine:
            cp.wait()

    hbm = pl.BlockSpec(memory_space=pl.ANY)
    return pl.pallas_call(
        body, name=name, out_shape=[jax.ShapeDtypeStruct((N_DEV,) + tuple(s), x.dtype) for s, x in zip(shapes, xs)],
        in_specs=[hbm] * nt, out_specs=[hbm] * nt,
        scratch_shapes=[pltpu.SemaphoreType.DMA((nt, 7)), pltpu.SemaphoreType.DMA((nt, 7)),
                        pltpu.SemaphoreType.DMA((nt,))],
    )(*xs)


_HBM = pl.BlockSpec(memory_space=pltpu.HBM)
_SEM = pl.BlockSpec(memory_space=pltpu.SEMAPHORE)
_PEER_ORDER = (2, 4, 6, 3, 5, 7, 1)


def _split_copies(src_refs, land_refs, send_sems, recv_sems, layers, per_peer):
    x_, y_, c_ = lax.axis_index("x"), lax.axis_index("y"), lax.axis_index("c")
    me = 4 * x_ + 2 * y_ + c_
    copies = []
    for k in _PEER_ORDER:
        px, py, pc = x_ ^ ((k >> 2) & 1), y_ ^ ((k >> 1) & 1), c_ ^ (k & 1)
        peer = 4 * px + 2 * py + pc
        for t in range(len(src_refs)):
            if per_peer:
                src = src_refs[t].at[peer]
            else:
                src = src_refs[t] if layers[t] is None else src_refs[t].at[layers[t]]
            copies.append(pltpu.make_async_remote_copy(
                src_ref=src, dst_ref=land_refs[t].at[me],
                send_sem=send_sems.at[t * (N_DEV - 1) + k - 1], recv_sem=recv_sems.at[t * (N_DEV - 1) + k - 1],
                device_id=(px, py, pc), device_id_type=pl.DeviceIdType.MESH))
    return copies


def _own_copies(src_refs, land_refs, sems, layers, per_peer):
    nt = len(src_refs)
    me = 4 * lax.axis_index("x") + 2 * lax.axis_index("y") + lax.axis_index("c")
    copies = []
    for t in range(nt):
        if per_peer:
            src = src_refs[t].at[me]
        else:
            src = src_refs[t] if layers[t] is None else src_refs[t].at[layers[t]]
        copies.append(pltpu.make_async_copy(src, land_refs[t].at[me], sems.at[nt * (N_DEV - 1) + t]))
    return copies


def _split_start(srcs, layers, per_peer, after, name):
    nt = len(srcs)
    if per_peer:
        land_shapes = [s.shape for s in srcs]
    else:
        land_shapes = [(N_DEV,) + tuple(s.shape if lay is None else s.shape[1:]) for s, lay in zip(srcs, layers)]

    def body(*refs):
        src_refs, land_refs = refs[:nt], refs[nt:2 * nt]
        send_sems, recv_sems = refs[2 * nt + 1], refs[2 * nt + 2]
        token = refs[-1]
        for cp in _split_copies(src_refs, land_refs, send_sems, recv_sems, layers, per_peer):
            cp.start()
        for cp in _own_copies(src_refs, land_refs, send_sems, layers, per_peer):
            cp.start()
        token[...] = jnp.zeros_like(token)

    lands = [pltpu.with_memory_space_constraint(lax.empty(s, x.dtype), pltpu.HBM) for s, x in zip(land_shapes, srcs)]
    srcs = [pltpu.with_memory_space_constraint(x, pltpu.HBM) for x in srcs]
    out = pl.pallas_call(
        body, name=name,
        out_shape=(pltpu.SemaphoreType.DMA((nt * N_DEV,)), pltpu.SemaphoreType.DMA((nt * (N_DEV - 1),)),
                   *[pltpu.HBM(x.shape, x.dtype) for x in srcs], *[pltpu.HBM(s, x.dtype) for s, x in zip(land_shapes, srcs)],
                   jax.ShapeDtypeStruct((8, LANES), F32)),
        in_specs=[_HBM] * (2 * nt) + [pl.BlockSpec(memory_space=pl.ANY)],
        out_specs=(_SEM, _SEM, *([_HBM] * (2 * nt)), pl.BlockSpec(memory_space=pltpu.VMEM)),
        input_output_aliases={i: 2 + i for i in range(2 * nt)},
        compiler_params=pltpu.CompilerParams(has_side_effects=pltpu.SideEffectType.DATAFLOW_SIDE_EFFECTING),
    )(*srcs, *lands, after)
    return out[0], out[1], list(out[2:2 + nt]), list(out[2 + nt:2 + 2 * nt]), out[-1]


def _split_wait(started, layers, per_peer, after, name):
    send_sems, recv_sems, srcs, lands, _ = started
    nt = len(srcs)

    def body(*refs):
        src_refs, land_refs = refs[:nt], refs[nt:2 * nt]
        s_sems, r_sems = refs[2 * nt], refs[2 * nt + 1]
        for cp in _split_copies(src_refs, land_refs, s_sems, r_sems, layers, per_peer):
            cp.wait_send()
            cp.wait_recv()
        for cp in _own_copies(src_refs, land_refs, s_sems, layers, per_peer):
            cp.wait()

    out = pl.pallas_call(
        body, name=name,
        out_shape=tuple(pltpu.HBM(x.shape, x.dtype) for x in srcs + lands),
        in_specs=[_HBM] * (2 * nt) + [_SEM, _SEM, pl.BlockSpec(memory_space=pl.ANY)],
        out_specs=tuple([_HBM] * (2 * nt)),
        input_output_aliases={i: i for i in range(2 * nt)},
        compiler_params=pltpu.CompilerParams(has_side_effects=pltpu.SideEffectType.DATAFLOW_SIDE_EFFECTING),
    )(*srcs, *lands, send_sems, recv_sems, after)
    return list(out[nt:])


def _runs(mapping):
    runs, c, n = [], 0, len(mapping)
    while c < n:
        if mapping[c] is None:
            c += 1
            continue
        sid, d, lo = mapping[c][0], mapping[c][1] - c, c
        while c < n and mapping[c] is not None and mapping[c][0] == sid and mapping[c][1] - c == d:
            c += 1
        runs.append((lo, c, sid, d))
    return runs


def _tile_plan(mapping, src_widths):
    runs = _runs(mapping)
    plan = []
    for t in range(len(mapping) // LANES):
        pieces = []
        for lo, hi, sid, d in runs:
            lo_t, hi_t = max(lo, t * LANES), min(hi, (t + 1) * LANES)
            if lo_t >= hi_t:
                continue
            a = ((lo_t + d) // LANES) * LANES
            win = min(2 * LANES, src_widths[sid] - a)
            shift = t * LANES + d - a
            pieces.append((sid, a, win, shift, lo_t - t * LANES, hi_t - t * LANES))
        plan.append(pieces)
    return plan


def _reblock(srcs, src_views, outs, out_views, name):
    R = srcs[0].shape[-2]
    tr = min(512, R)
    widths = {sid: srcs[ai].shape[-1] for sid, (ai, _) in src_views.items()}
    plans = [(ai, li, _tile_plan(mapping, widths)) for ai, li, mapping in out_views]
    ns = len(srcs)

    def body(*refs):
        s_refs, o_refs = refs[:ns], refs[ns:]
        cache = {}

        def shift_matrix(win, shift, lo, hi):
            key = (win, shift, lo, hi)
            if key not in cache:
                r = lax.broadcasted_iota(jnp.int32, (win, LANES), 0)
                c = lax.broadcasted_iota(jnp.int32, (win, LANES), 1)
                hit = jnp.logical_and(r - c == shift, jnp.logical_and(c >= lo, c < hi))
                cache[key] = jnp.where(hit, 1.0, 0.0).astype(BF16)
            return cache[key]

        for ai, li, plan in plans:
            for t, pieces in enumerate(plan):
                acc = None
                whole = len(pieces) == 1 and pieces[0][3:] == (0, 0, LANES)
                for sid, a, win, shift, lo, hi in pieces:
                    sa, sl = src_views[sid]
                    if whole:
                        win = LANES
                    src = s_refs[sa][:, a:a + win] if sl is None else s_refs[sa][sl, :, a:a + win]
                    if whole:
                        acc = src
                    else:
                        part = jnp.dot(src, shift_matrix(win, shift, lo, hi), preferred_element_type=F32)
                        acc = part if acc is None else acc + part
                val = jnp.zeros((tr, LANES), BF16) if acc is None else acc.astype(BF16)
                if li is None:
                    o_refs[ai][:, t * LANES:(t + 1) * LANES] = val
                else:
                    o_refs[ai][li, :, t * LANES:(t + 1) * LANES] = val

    def spec(shape):
        if len(shape) == 2:
            return pl.BlockSpec((tr, shape[1]), lambda i: (i, 0))
        return pl.BlockSpec((shape[0], tr, shape[2]), lambda i: (0, i, 0))

    return pl.pallas_call(
        body, name=name, grid=(R // tr,), in_specs=[spec(s.shape) for s in srcs],
        out_specs=[spec(s) for s in outs], out_shape=[jax.ShapeDtypeStruct(s, BF16) for s in outs],
        compiler_params=_cp(("parallel",)),
    )(*srcs)


SHARDED = ("w_in", "w_gate_up", "w_proj_attn", "w_proj_pool", "w_proj_conv", "w_out", "w_down")
WEIGHT_ORDER = ("attn_norm", "w_in", "b_forget", "b_gate", "w_proj_attn", "pool_w", "pool_scale", "w_proj_pool",
                "conv_w", "w_proj_conv", "w_out", "ffn_norm", "w_gate_up", "w_down", "final_norm")
IN_SHARD, IN_SHARD_PAD = IN_COLS // N_DEV, 896
GU_SHARD, GU_SHARD_PAD = 2 * FFN_HIDDEN // N_DEV, 768


def _w_in_col(c):
    if c < OFF_QKV:
        return c + 3592
    if c < OFF_U:
        base, off = (0, OFF_QKV) if c < OFF_CONV else (2056, OFF_CONV)
        j, t = divmod(c - off, TRIPLE)
        which, e = divmod(t, LANES)
        return base + which * BRANCH_W + j * LANES + e
    return c - OFF_U + 1544


def _w_in_full(gathered, name):
    main = [divmod(_w_in_col(c), IN_SHARD) for c in range(MAIN_COLS)]
    fcols = [divmod(1536 + c, IN_SHARD) if c < N_HEADS else None for c in range(LANES)]
    R = gathered.shape[1]
    return _reblock([gathered], {i: (0, i) for i in range(N_DEV)}, [(R, MAIN_COLS), (R, LANES)],
                    [(0, None, main), (1, None, fcols)], name)


def _w_in_slabs(dmain, dwf, name):
    inv = {_w_in_col(c): ("m", c) for c in range(MAIN_COLS)}
    inv.update({1536 + c: ("f", c) for c in range(N_HEADS)})
    views = []
    for i in range(N_DEV):
        mapping = [inv[IN_SHARD * i + j] if j < IN_SHARD else None for j in range(IN_SHARD_PAD)]
        views.append((0, i, mapping))
    R = dmain.shape[0]
    return _reblock([dmain, dwf], {"m": (0, None), "f": (1, None)}, [(N_DEV, R, IN_SHARD_PAD)], views, name)[0]


def _w_gu_full(gathered, name):
    mapping = [divmod(_gu_col(c), GU_SHARD) for c in range(2 * FFN_HIDDEN)]
    R = gathered.shape[1]
    return _reblock([gathered], {i: (0, i) for i in range(N_DEV)}, [(R, 2 * FFN_HIDDEN)], [(0, None, mapping)], name)[0]


def _w_gu_slabs(dw, name):
    inv = {_gu_col(c): c for c in range(2 * FFN_HIDDEN)}
    views = [(0, i, [("w", inv[GU_SHARD * i + j]) if j < GU_SHARD else None for j in range(GU_SHARD_PAD)])
             for i in range(N_DEV)]
    R = dw.shape[0]
    return _reblock([dw], {"w": (0, None)}, [(N_DEV, R, GU_SHARD_PAD)], views, name)[0]


def _layer_fwd(x, W, n_seq, l, h1=None, next_norm=None):
    T = x.shape[0]
    sfx = f"_l{l}"
    if h1 is None:
        h1 = _rms_fwd(x, W["attn_norm"], "rms1" + sfx)
    proj, f = _matmul(h1, W["w_main"], mode="nn", out_dtype=BF16, name="proj_main" + sfx, side=(W["w_f"], F32))
    qa, ka, va = _fox_prep(f, W["b_forget"], proj, n_seq, "fox_prep" + sfx)
    oa, oa32, lse = _attn_fwd2(qa, ka, va, n_seq, "attn_fwd" + sfx)
    if "late" in W:
        W.update(W.pop("late")(oa))
    ob = _pool_fwd(proj, W["pool_w"], W["pool_scale"], n_seq, "pool_fwd" + sfx)
    oc = _conv_fwd(proj, W["conv_w"], n_seq, "conv_fwd" + sfx)
    mixed = _mix_fwd(oa, ob, oc, W["w_proj_attn"], W["w_proj_pool"], W["w_proj_conv"], proj, W["b_gate"],
                     "mix_fwd" + sfx)
    x2, h2 = _matmul(mixed, W["w_out"], mode="nn", out_dtype=F32, name="out_proj" + sfx, tm=1024, tn=1024,
                     residual=x, rms_g=W["ffn_norm"])
    ab, s = _gate_up_swiglu(h2, W["w_gate_up"], "gate_up" + sfx)
    x3 = _matmul(s, W["w_down"], mode="nn", out_dtype=F32, name="down" + sfx, tm=1024, tn=1024, tk=1408,
                 residual=x2, rms_g=next_norm)
    x3, h1_next = x3 if next_norm is not None else (x3, None)
    saved = dict(x=x, h1=h1, proj=proj, f=f, qa=qa, ka=ka, oa=oa, oa32=oa32, lse=lse, ob=ob, oc=oc, mixed=mixed, x2=x2,
                 h2=h2, ab=ab, s=s)
    return x3, saved, h1_next


def _layer_bwd(dx3, dx3b, W, sv, n_seq, l, stage=None):
    T = dx3.shape[0]
    sfx = f"_l{l}"
    G = {}
    stage = stage or (lambda l, group, G, W: W)
    dab = _swiglu_bwd_fused(dx3b, W["w_down"], sv["ab"], "d_ab" + sfx)
    G["w_down"] = _matmul(sv["s"], dx3b, mode="tn", out_dtype=BF16, name="dw_down" + sfx, tm=256, tn=1024)
    dh2 = _matmul(dab, W["w_gate_up"], mode="nt", out_dtype=BF16, name="d_h2" + sfx, tm=1024, tn=1024, tk=1408)
    G["w_gate_up"] = _matmul(sv["h2"], dab, mode="tn", out_dtype=BF16, name="dw_gate_up" + sfx, tm=1024)
    W = stage(l, "ffn", G, W)
    dx2, dx2b, G["ffn_norm"] = _rms_bwd(sv["x2"], W["ffn_norm"], dh2, dx3, "rms2_bwd" + sfx)
    dmixed = _matmul(dx2b, W["w_out"], mode="nt", out_dtype=BF16, name="d_mixed" + sfx)
    G["w_out"] = _matmul(sv["mixed"], dx2b, mode="tn", out_dtype=BF16, name="dw_out" + sfx, tm=1024)
    dya, dyb, dyc, dproj, G["b_gate"] = _mix_bwd(sv["oa"], sv["ob"], sv["oc"], W["w_proj_attn"], W["w_proj_pool"],
                                                 W["w_proj_conv"], sv["proj"], W["b_gate"], dmixed, "mix_bwd" + sfx)
    douts = {}
    for br, dy, o in (("attn", dya, sv["oa"]), ("pool", dyb, sv["ob"]), ("conv", dyc, sv["oc"])):
        douts[br] = _matmul(dy, W["w_proj_" + br], mode="nt", out_dtype=BF16, name=f"d_{br}_out" + sfx)
        G["w_proj_" + br] = _matmul(o, dy, mode="tn", out_dtype=BF16, name=f"dw_proj_{br}" + sfx, tm=512)
    W = stage(l, "mix", G, W)
    dproj, G["conv_w"] = _conv_bwd(sv["proj"], douts["conv"], W["conv_w"], dproj, n_seq, "conv_bwd" + sfx)
    dproj, G["pool_w"], G["pool_scale"] = _pool_bwd(sv["proj"], douts["pool"], W["pool_w"], W["pool_scale"], dproj,
                                                    n_seq, "pool_bwd" + sfx)
    dproj, dFk = _attn_bwd(sv["qa"], sv["ka"], sv["proj"], douts["attn"], sv["oa32"], sv["lse"], dproj, n_seq,
                           "attn_bwd" + sfx)
    dF = jnp.pad(dFk.reshape(N_HEADS, T).T, ((0, 0), (0, LANES - N_HEADS)))
    df, G["b_forget"] = _fox_cumsum_bwd(sv["f"], W["b_forget"], dF, n_seq, "fox_cumsum_bwd" + sfx)
    G["w_main"], G["w_f"] = _matmul(sv["h1"], dproj, mode="tn", out_dtype=BF16, name="dw_main" + sfx, tm=1024,
                                    side=(df, BF16))
    W = stage(l, "w_in", G, W)
    dh1 = _matmul(dproj, W["w_main"], mode="nt", out_dtype=BF16, name="d_h1_main" + sfx, tm=1024, tn=1024, tk=1664,
                  extra=(df, W["w_f"]))
    dx, dxb, G["attn_norm"] = _rms_bwd(sv["x"], W["attn_norm"], dh1, dx2, "rms1_bwd" + sfx)
    return dx, dxb, G


def _replicated_operands(rep, l):
    W = {}
    W["attn_norm"], W["ffn_norm"] = rep["attn_norm"][l], rep["ffn_norm"][l]
    W["b_forget"] = jnp.pad(rep["b_forget"][l].reshape(1, N_HEADS), ((0, 0), (0, LANES - N_HEADS)))
    W["b_gate"] = rep["b_gate"][l].reshape(1, GATE_W)
    W["pool_w"] = rep["pool_w"][l].astype(BF16)
    W["pool_scale"] = rep["pool_scale"][l].reshape(1, BRANCH_W)
    return W


def _local_step(x, target, get_W, attn_norms, final_norm, stage=None):
    n_seq, S, Dm = x.shape
    T = n_seq * S
    xt = x.reshape(T, Dm)
    saved, Ws, h1 = [], [], None
    for l in range(DEPTH):
        Ws.append(get_W(l, xt))
        next_norm = attn_norms[l + 1] if l + 1 < DEPTH else None
        xt, sv, h1 = _layer_fwd(xt, Ws[l], n_seq, l, h1, next_norm)
        saved.append(sv)
    loss, dx, dxb, g_final = _loss_head(xt, final_norm, target.reshape(T, Dm), "loss_head")
    grads = [None] * DEPTH
    for l in reversed(range(DEPTH)):
        dx, dxb, grads[l] = _layer_bwd(dx, dxb, Ws[l], saved[l], n_seq, l, stage)
    return loss, dx.reshape(n_seq, S, Dm), grads, g_final


def _padded_shards(weights):
    pads = {"w_in": IN_SHARD_PAD - IN_SHARD, "w_gate_up": GU_SHARD_PAD - GU_SHARD}
    return {n: jnp.pad(weights[n], ((0, 0), (0, 0), (0, pads.get(n, 0)))).astype(BF16) for n in SHARDED}


def _full_operands(g, l):
    W = {}
    if "w_in" in g:
        W["w_main"], W["w_f"] = _w_in_full(g["w_in"], f"w_in_full_l{l}")
    if "w_gate_up" in g:
        W["w_gate_up"] = _w_gu_full(g["w_gate_up"], f"w_gate_up_full_l{l}")
    for n in ("w_proj_attn", "w_proj_pool", "w_proj_conv"):
        if n in g:
            W[n] = jnp.transpose(g[n], (1, 0, 2)).reshape(BRANCH_W, D_MODEL)
    if "w_out" in g:
        W["w_out"] = g["w_out"].reshape(D_MODEL, D_MODEL)
    if "w_down" in g:
        W["w_down"] = g["w_down"].reshape(FFN_HIDDEN, D_MODEL)
    return W


GRAD_GROUPS = {"ffn": ("w_down", "w_gate_up"),
               "mix": ("w_out", "w_proj_attn", "w_proj_pool", "w_proj_conv"),
               "w_in": ("w_in",)}


def _grad_slabs(G, n, l):
    if n == "w_in":
        return _w_in_slabs(G["w_main"], G["w_f"], f"w_in_slabs_l{l}")
    if n == "w_gate_up":
        return _w_gu_slabs(G["w_gate_up"], f"w_gate_up_slabs_l{l}")
    if n == "w_out":
        return G["w_out"].reshape(N_DEV, D_MODEL // N_DEV, D_MODEL)
    if n == "w_down":
        return G["w_down"].reshape(N_DEV, FFN_HIDDEN // N_DEV, D_MODEL)
    return jnp.transpose(G[n].reshape(BRANCH_W, N_DEV, D_MODEL // N_DEV), (1, 0, 2))


def _sum_layer_grads(recv, l):
    out = {}
    for n, r in recv.items():
        if n in ("w_in", "w_gate_up"):
            out[n] = _sum_slabs_t(r, f"sum_{n}_l{l}")[:IN_SHARD if n == "w_in" else GU_SHARD]
        else:
            out[n] = _sum_slabs(r, f"sum_{n}_l{l}")
    return out


def _sum_small(xs, name):
    def body(*refs):
        for x_ref, o_ref in zip(refs[:len(xs)], refs[len(xs):]):
            acc = x_ref[0]
            for j in range(1, N_DEV):
                acc = acc + x_ref[j]
            o_ref[...] = acc

    return pl.pallas_call(
        body, name=name, out_shape=[jax.ShapeDtypeStruct(x.shape[1:], F32) for x in xs],
        compiler_params=_cp(),
    )(*xs)


def _as_2d(a):
    if a.ndim == 1:
        return a.reshape(1, -1)
    return a.reshape(-1, a.shape[-1])


def kernel(x, attn_norm, w_in, b_forget, b_gate, w_proj_attn, pool_w, pool_scale, w_proj_pool, conv_w, w_proj_conv, w_out, ffn_norm, w_gate_up, w_down, final_norm, loss_target, m_attn_norm, m_w_in, m_b_forget, m_b_gate, m_w_proj_attn, m_pool_w, m_pool_scale, m_w_proj_pool, m_conv_w, m_w_proj_conv, m_w_out, m_ffn_norm, m_w_gate_up, m_w_down, m_final_norm, v_attn_norm, v_w_in, v_b_forget, v_b_gate, v_w_proj_attn, v_pool_w, v_pool_scale, v_w_proj_pool, v_conv_w, v_w_proj_conv, v_w_out, v_ffn_norm, v_w_gate_up, v_w_down, v_final_norm):
    weights = dict(attn_norm=attn_norm, w_in=w_in, b_forget=b_forget, b_gate=b_gate, w_proj_attn=w_proj_attn,
                   pool_w=pool_w, pool_scale=pool_scale, w_proj_pool=w_proj_pool, conv_w=conv_w,
                   w_proj_conv=w_proj_conv, w_out=w_out, ffn_norm=ffn_norm, w_gate_up=w_gate_up, w_down=w_down,
                   final_norm=final_norm)
    moments_m = dict(attn_norm=m_attn_norm, w_in=m_w_in, b_forget=m_b_forget, b_gate=m_b_gate,
                     w_proj_attn=m_w_proj_attn, pool_w=m_pool_w, pool_scale=m_pool_scale, w_proj_pool=m_w_proj_pool,
                     conv_w=m_conv_w, w_proj_conv=m_w_proj_conv, w_out=m_w_out, ffn_norm=m_ffn_norm,
                     w_gate_up=m_w_gate_up, w_down=m_w_down, final_norm=m_final_norm)
    moments_v = dict(attn_norm=v_attn_norm, w_in=v_w_in, b_forget=v_b_forget, b_gate=v_b_gate,
                     w_proj_attn=v_w_proj_attn, pool_w=v_pool_w, pool_scale=v_pool_scale, w_proj_pool=v_w_proj_pool,
                     conv_w=v_conv_w, w_proj_conv=v_w_proj_conv, w_out=v_w_out, ffn_norm=v_ffn_norm,
                     w_gate_up=v_w_gate_up, w_down=v_w_down, final_norm=v_final_norm)

    sh = _padded_shards(weights)
    names = list(SHARDED)
    rest = [n for n in names if n != "w_in"]
    me = 4 * lax.axis_index("x") + 2 * lax.axis_index("y") + lax.axis_index("c")
    w_in0, conv_all = _multi_gather([sh["w_in"], conv_w], [0, None], "gather_w_in_l0")
    started, after = {}, w_in0
    for l in range(DEPTH):
        for group, gnames in (("w_in", ["w_in"]), ("rest", rest)):
            if (l, group) != (0, "w_in"):
                started[l, group] = _split_start([sh[n] for n in gnames], [l] * len(gnames), False, after,
                                                 f"gather_start_{group}_l{l}")
                after = started[l, group][4]
    last_token = after

    def get_W(l, xt):
        if l == 0:
            w_in = w_in0
        else:
            w_in = _split_wait(started[l, "w_in"], [l], False, xt, f"gather_wait_w_in_l{l}")[0]
        W = _full_operands({"w_in": w_in}, l)

        def late(after):
            lands = _split_wait(started[l, "rest"], [l] * len(rest), False, after, f"gather_wait_rest_l{l}")
            return _full_operands(dict(zip(rest, lands)), l)

        W["late"] = late
        W.update(_replicated_operands(weights, l))
        W["conv_w"] = jnp.transpose(conv_all[:, l], (1, 0, 2)).reshape(CONV_K, BRANCH_W)
        if l == 0:
            W["attn_norm"] = W["attn_norm"] + last_token[0, 0]
        return W

    exchanges = []

    def stage(l, group, G, W):
        gnames = GRAD_GROUPS[group]
        slabs = [_grad_slabs(G, n, l) for n in gnames]
        started = _split_start(slabs, None, True, slabs[0][0, :8], f"exchange_start_{group}_l{l}")
        exchanges.append((l, group, gnames, slabs, started))
        tie = {"ffn": "ffn_norm", "mix": "conv_w", "w_in": "w_f"}[group]
        W = dict(W)
        W[tie] = W[tie] + started[4][0, 0].astype(W[tie].dtype)
        return W

    loss_part, grad_x, grads, g_final = _local_step(x, loss_target, get_W, attn_norm, final_norm, stage)

    def finish_exchange(ex, after):
        l, group, gnames, slabs, started = ex
        lands = _split_wait(started, None, True, after, f"exchange_wait_{group}_l{l}")
        grads[l].update(_sum_layer_grads(dict(zip(gnames, lands)), l))

    def zero_after(a):
        return jnp.minimum(jnp.abs(a.reshape(-1)[0]), 0.0)

    *early, last_exchange = exchanges
    for ex in early:
        finish_exchange(ex, grad_x)
    deltas, new_m, new_v, gw = {}, {}, {}, {}
    views = {"w_in": ((2, 0, 1), (1, 2, 0), (49, DEPTH, D_MODEL), 1),
             "w_gate_up": ((0, 2, 1), (0, 2, 1), (1, GU_SHARD // 2, D_MODEL), 0)}

    def update_sharded(n):
        if n in views:
            perm, inv, block, layer_axis = views[n]
            gt = jnp.stack([grads[l][n] for l in range(DEPTH)], axis=layer_axis)
            d, nm, nv = _adamw_3d(jnp.transpose(weights[n], perm), gt, jnp.transpose(moments_m[n], perm),
                                  jnp.transpose(moments_v[n], perm), block, "adamw_" + n)
            deltas[n], new_m[n], new_v[n] = (jnp.transpose(a, inv) for a in (d, nm, nv))
            gw[n] = jnp.transpose(gt, inv)
            return
        gw[n] = jnp.stack([grads[l][n] for l in range(DEPTH)])
        shape = weights[n].shape
        d, nm, nv = _adamw(_as_2d(weights[n]), _as_2d(gw[n]), _as_2d(moments_m[n]), _as_2d(moments_v[n]),
                           "adamw_" + n)
        deltas[n], new_m[n], new_v[n] = d.reshape(shape), nm.reshape(shape), nv.reshape(shape)

    for n in SHARDED:
        if n != "w_in":
            update_sharded(n)

    small = ("attn_norm", "b_forget", "b_gate", "pool_w", "pool_scale", "ffn_norm", "conv_w")
    loss_part = loss_part + zero_after(deltas["w_gate_up"]) + zero_after(deltas["w_down"])
    parts = [jnp.stack([grads[l][n] for l in range(DEPTH)]) for n in small] + [g_final, loss_part]
    gathered = _multi_gather(parts, [None] * len(parts), "gather_small_grads")
    summed = _sum_small(gathered, "sum_small_grads")
    for n, s in zip(small, summed):
        gw[n] = s
    gw["attn_norm"], gw["ffn_norm"] = gw["attn_norm"][:, 0], gw["ffn_norm"][:, 0]
    gw["b_forget"] = gw["b_forget"][:, 0, :N_HEADS]
    gw["b_gate"], gw["pool_scale"] = gw["b_gate"][:, 0], gw["pool_scale"][:, 0]
    gw["conv_w"] = lax.dynamic_slice_in_dim(gw["conv_w"], me * (BRANCH_W // N_DEV), BRANCH_W // N_DEV, axis=2)
    gw["final_norm"] = summed[-2][0]
    loss = summed[-1][0, 0]

    rest_names = [n for n in WEIGHT_ORDER if n not in SHARDED]
    ds, nms, nvs = _adamw_many(*[[_as_2d(src[n]) for n in rest_names] for src in (weights, gw, moments_m, moments_v)],
                               "adamw_small")
    for n, d, nm, nv in zip(rest_names, ds, nms, nvs):
        shape = weights[n].shape
        deltas[n], new_m[n], new_v[n] = d.reshape(shape), nm.reshape(shape), nv.reshape(shape)

    finish_exchange(last_exchange, deltas["pool_w"])
    update_sharded("w_in")

    return (loss, grad_x, *[gw[n] for n in WEIGHT_ORDER], *[deltas[n] for n in WEIGHT_ORDER],
            *[new_m[n] for n in WEIGHT_ORDER], *[new_v[n] for n in WEIGHT_ORDER])
```

```python
import functools

import jax
import jax.numpy as jnp
from jax import lax
from jax.experimental import pallas as pl
from jax.experimental.pallas import tpu as pltpu

F32 = jnp.float32
BF16 = jnp.bfloat16

N_DEV = 8
D_MODEL = 1024
DEPTH = 2
N_HEADS = 8
HEAD_DIM = 64
BRANCH_W = 512
POOL_WINDOWS = (2, 4, 8, 16)
POOL_GD = 128
CONV_K = 3
FFN_HIDDEN = 2816
GATE_W = 3 * D_MODEL
IN_COLS = 6664
MAIN_COLS = GATE_W + 7 * BRANCH_W
RMS_EPS = 1e-6
NEG_INF = -1e30

ADAM_LR = 0.001
ADAM_B1 = 0.9
ADAM_B2 = 0.999
ADAM_EPS = 1e-08
ADAM_WD = 0.01
ADAM_STEP = 10

LANES = 128
VMEM_LIMIT = 56 * 1024 * 1024
CUM_BLK = 256

TRIPLE = 3 * LANES
OFF_G, OFF_QKV, OFF_CONV, OFF_U = 0, 3072, 4608, 6144


def _cp(sem=None):
    return pltpu.CompilerParams(dimension_semantics=sem, vmem_limit_bytes=VMEM_LIMIT)


def _sigmoid(z):
    return 1.0 / (1.0 + jnp.exp(-z))


def _matmul(a, b, *, mode, out_dtype, name, tm=2048, tn=512, tk=None, residual=None, rms_g=None, side=None,
            extra=None):
    if mode == "nn":
        (M, K), N = a.shape, b.shape[1]
    elif mode == "nt":
        (M, K), N = a.shape, b.shape[0]
    else:
        (K, M), N = a.shape, b.shape[1]
    tm, tn, tk = min(tm, M), min(tn, N), K if tk is None else min(tk, K)
    assert M % tm == 0 and N % tn == 0 and K % tk == 0, (name, M, N, K, tm, tn, tk)
    nk = K // tk
    if mode == "nn":
        a_spec = pl.BlockSpec((tm, tk), lambda i, j, k: (i, k))
        b_spec = pl.BlockSpec((tk, tn), lambda i, j, k: (k, j))
        dims = (((1,), (0,)), ((), ()))
    elif mode == "nt":
        a_spec = pl.BlockSpec((tm, tk), lambda i, j, k: (i, k))
        b_spec = pl.BlockSpec((tn, tk), lambda i, j, k: (j, k))
        dims = (((1,), (1,)), ((), ()))
    else:
        a_spec = pl.BlockSpec((tk, tm), lambda i, j, k: (k, i))
        b_spec = pl.BlockSpec((tk, tn), lambda i, j, k: (k, j))
        dims = (((0,), (0,)), ((), ()))
    o_spec = pl.BlockSpec((tm, tn), lambda i, j, k: (i, j))
    has_res, has_norm, has_side, has_extra = (v is not None for v in (residual, rms_g, side, extra))
    assert not has_norm or tn == N, (name, tn, N)
    assert not has_side or (nk == 1 and mode != "nt"), name

    in_specs, args = [a_spec, b_spec], [a, b]
    out_specs, out_shape = [o_spec], [jax.ShapeDtypeStruct((M, N), out_dtype)]
    if has_res:
        in_specs.append(o_spec)
        args.append(residual)
    if has_norm:
        in_specs.append(pl.BlockSpec((1, N), lambda i, j, k: (0, 0)))
        args.append(rms_g.reshape(1, N))
        out_specs.append(o_spec)
        out_shape.append(jax.ShapeDtypeStruct((M, N), BF16))
    if has_side:
        b_side, side_dtype = side
        ns = b_side.shape[1]
        in_specs.append(pl.BlockSpec((K, ns), lambda i, j, k: (0, 0)))
        args.append(b_side)
        out_specs.append(pl.BlockSpec((tm, ns), lambda i, j, k: (i, 0)))
        out_shape.append(jax.ShapeDtypeStruct((M, ns), side_dtype))
    if has_extra:
        a2, b2 = extra
        in_specs += [pl.BlockSpec((tm, a2.shape[1]), lambda i, j, k: (i, 0)),
                     pl.BlockSpec((tn, b2.shape[1]), lambda i, j, k: (j, 0))]
        args += [a2, b2]
    n_in = len(args)

    def body(*refs):
        ins, outs = list(refs[2:n_in]), list(refs[n_in:n_in + len(out_shape)])
        a_ref, b_ref = refs[:2]
        r_ref = ins.pop(0) if has_res else None
        g_ref = ins.pop(0) if has_norm else None
        bs_ref = ins.pop(0) if has_side else None
        a2_ref, b2_ref = (ins.pop(0), ins.pop(0)) if has_extra else (None, None)
        o_ref = outs.pop(0)
        h_ref = outs.pop(0) if has_norm else None
        so_ref = outs.pop(0) if has_side else None

        def finish(acc):
            if has_res:
                acc = acc + r_ref[...].astype(F32)
            if has_extra:
                acc = acc + lax.dot_general(a2_ref[...], b2_ref[...], (((1,), (1,)), ((), ())),
                                            preferred_element_type=F32)
            o_ref[...] = acc.astype(out_dtype)
            if has_norm:
                r = lax.rsqrt(jnp.mean(acc * acc, axis=-1, keepdims=True) + RMS_EPS)
                h_ref[...] = ((acc * r) * g_ref[...]).astype(BF16)

        if has_side:
            @pl.when(pl.program_id(1) == 0)
            def _():
                side_dims = (((1,), (0,)), ((), ())) if mode == "nn" else dims
                so_ref[...] = lax.dot_general(a_ref[...], bs_ref[...], side_dims,
                                              preferred_element_type=F32).astype(so_ref.dtype)

        prod = lax.dot_general(a_ref[...], b_ref[...], dims, preferred_element_type=F32)
        if nk == 1:
            finish(prod)
            return
        acc_ref = refs[-1]
        k = pl.program_id(2)

        @pl.when(k == 0)
        def _():
            acc_ref[...] = prod

        @pl.when(jnp.logical_and(k > 0, k < nk - 1))
        def _():
            acc_ref[...] += prod

        @pl.when(k == nk - 1)
        def _():
            finish(acc_ref[...] + prod)

    single = len(out_shape) == 1
    return pl.pallas_call(
        body, name=name, grid=(M // tm, N // tn, nk), in_specs=in_specs,
        out_specs=out_specs[0] if single else out_specs, out_shape=out_shape[0] if single else out_shape,
        scratch_shapes=[pltpu.VMEM((tm, tn), F32)] if nk > 1 else [],
        compiler_params=_cp(("parallel", "arbitrary" if has_side else "parallel", "arbitrary")),
    )(*args)


def _rms_fwd(x, g, name):
    T, Dm = x.shape
    tm = min(512, T)

    def body(x_ref, g_ref, h_ref):
        xf = x_ref[...]
        r = lax.rsqrt(jnp.mean(xf * xf, axis=-1, keepdims=True) + RMS_EPS)
        h_ref[...] = ((xf * r) * g_ref[...]).astype(BF16)

    return pl.pallas_call(
        body, name=name, grid=(T // tm,),
        in_specs=[pl.BlockSpec((tm, Dm), lambda i: (i, 0)), pl.BlockSpec((1, Dm), lambda i: (0, 0))],
        out_specs=pl.BlockSpec((tm, Dm), lambda i: (i, 0)),
        out_shape=jax.ShapeDtypeStruct((T, Dm), BF16),
        compiler_params=_cp(("parallel",)),
    )(x, g.reshape(1, Dm))


def _rms_bwd(x, g, dh, dres, name):
    T, Dm = x.shape
    tm = min(512, T)

    def body(x_ref, g_ref, dh_ref, dres_ref, dx_ref, dxb_ref, dg_ref):
        i = pl.program_id(0)
        xf = x_ref[...]
        r = lax.rsqrt(jnp.mean(xf * xf, axis=-1, keepdims=True) + RMS_EPS)
        xn = xf * r
        dhf = dh_ref[...].astype(F32)
        dxn = dhf * g_ref[...]
        c = jnp.mean(dxn * xn, axis=-1, keepdims=True)
        dx = dres_ref[...] + r * (dxn - xn * c)
        dx_ref[...] = dx
        dxb_ref[...] = dx.astype(BF16)
        part = jnp.sum(dhf * xn, axis=0, keepdims=True)

        @pl.when(i == 0)
        def _():
            dg_ref[...] = part

        @pl.when(i > 0)
        def _():
            dg_ref[...] += part

    row = pl.BlockSpec((tm, Dm), lambda i: (i, 0))
    vec = pl.BlockSpec((1, Dm), lambda i: (0, 0))
    return pl.pallas_call(
        body, name=name, grid=(T // tm,), in_specs=[row, vec, row, row], out_specs=[row, row, vec],
        out_shape=[jax.ShapeDtypeStruct((T, Dm), F32), jax.ShapeDtypeStruct((T, Dm), BF16),
                   jax.ShapeDtypeStruct((1, Dm), F32)],
        compiler_params=_cp(("arbitrary",)),
    )(x, g.reshape(1, Dm), dh, dres)


def _loss_head(x, g, target, name):
    T, Dm = x.shape
    tm = min(512, T)

    def body(x_ref, g_ref, t_ref, loss_ref, dx_ref, dxb_ref, dg_ref):
        i = pl.program_id(0)
        xf = x_ref[...]
        gv = g_ref[...]
        r = lax.rsqrt(jnp.mean(xf * xf, axis=-1, keepdims=True) + RMS_EPS)
        xn = xf * r
        diff = xn * gv - t_ref[...]
        per_tok = jnp.mean(diff * diff, axis=-1, keepdims=True)
        lpart = 0.5 * jnp.sum(per_tok, axis=0, keepdims=True) + jnp.zeros((1, LANES), F32)
        dy = diff * (1.0 / Dm)
        dxn = dy * gv
        c = jnp.mean(dxn * xn, axis=-1, keepdims=True)
        dx = r * (dxn - xn * c)
        dx_ref[...] = dx
        dxb_ref[...] = dx.astype(BF16)
        part = jnp.sum(dy * xn, axis=0, keepdims=True)

        @pl.when(i == 0)
        def _():
            dg_ref[...] = part
            loss_ref[...] = lpart

        @pl.when(i > 0)
        def _():
            dg_ref[...] += part
            loss_ref[...] += lpart

    row = pl.BlockSpec((tm, Dm), lambda i: (i, 0))
    vec = pl.BlockSpec((1, Dm), lambda i: (0, 0))
    lsp = pl.BlockSpec((1, LANES), lambda i: (0, 0))
    return pl.pallas_call(
        body, name=name, grid=(T // tm,), in_specs=[row, vec, row], out_specs=[lsp, row, row, vec],
        out_shape=[jax.ShapeDtypeStruct((1, LANES), F32), jax.ShapeDtypeStruct((T, Dm), F32),
                   jax.ShapeDtypeStruct((T, Dm), BF16), jax.ShapeDtypeStruct((1, Dm), F32)],
        compiler_params=_cp(("arbitrary",)),
    )(x, g.reshape(1, Dm), target)


def _split_bf16(v):
    hi = v.astype(BF16)
    r1 = v - hi.astype(F32)
    mid = r1.astype(BF16)
    lo = (r1 - mid.astype(F32)).astype(BF16)
    return hi, mid, lo


def _tri_dot(tri, v):
    hi, mid, lo = _split_bf16(v)
    dot = functools.partial(jnp.dot, preferred_element_type=F32)
    return dot(tri, hi) + dot(tri, mid) + dot(tri, lo)


def _log_sigmoid(z):
    return jnp.minimum(z, 0.0) - jnp.log(1.0 + jnp.exp(-jnp.abs(z)))


def _fox_cumsum_bwd(f, bf, dF, n_seq, name):
    T = f.shape[0]
    S = T // n_seq
    c = min(CUM_BLK, S)

    def body(f_ref, b_ref, dF_ref, df_ref, db_ref):
        b = pl.program_id(0)
        ri = lax.broadcasted_iota(jnp.int32, (c, c), 0)
        ci = lax.broadcasted_iota(jnp.int32, (c, c), 1)
        tri = (ri <= ci).astype(BF16)
        carry = jnp.zeros((1, LANES), F32)
        dbp = jnp.zeros((1, LANES), F32)
        for j in reversed(range(S // c)):
            dFc = dF_ref[j * c:(j + 1) * c, :]
            dlf = _tri_dot(tri, dFc) + carry
            carry = carry + jnp.sum(dFc, axis=0, keepdims=True)
            z = f_ref[j * c:(j + 1) * c, :] + b_ref[...]
            dz = dlf * _sigmoid(-z)
            df_ref[j * c:(j + 1) * c, :] = dz.astype(BF16)
            dbp = dbp + jnp.sum(dz, axis=0, keepdims=True)

        @pl.when(b == 0)
        def _():
            db_ref[...] = dbp

        @pl.when(b > 0)
        def _():
            db_ref[...] += dbp

    blk = pl.BlockSpec((S, LANES), lambda b: (b, 0))
    vec = pl.BlockSpec((1, LANES), lambda b: (0, 0))
    return pl.pallas_call(
        body, name=name, grid=(n_seq,), in_specs=[blk, vec, blk], out_specs=[blk, vec],
        out_shape=[jax.ShapeDtypeStruct((T, LANES), BF16), jax.ShapeDtypeStruct((1, LANES), F32)],
        compiler_params=_cp(("arbitrary",)),
    )(f, bf, dF)


def _pair_masks():
    lane = lax.broadcasted_iota(jnp.int32, (1, LANES), 1)
    lo = lane < HEAD_DIM
    return lo, jnp.logical_not(lo)


AUG0 = HEAD_DIM
Q_TILE, K_CHUNK, ROW_GROUP = 512, 256, 64


def _fox_prep(f, bf, proj, n_seq, name):
    T = f.shape[0]
    S = T // n_seq
    c = min(CUM_BLK, S)

    def body(f_ref, b_ref, qkv_ref, qa_ref, ka_ref, va_ref):
        ri = lax.broadcasted_iota(jnp.int32, (c, c), 0)
        ci = lax.broadcasted_iota(jnp.int32, (c, c), 1)
        tri = (ri >= ci).astype(BF16)
        lane = lax.broadcasted_iota(jnp.int32, (c, LANES), 1)
        carry = jnp.zeros((1, LANES), F32)
        for j in range(S // c):
            rows = slice(j * c, (j + 1) * c)
            lf = _log_sigmoid(f_ref[rows, :] + b_ref[...])
            Fc = _tri_dot(tri, lf) + carry
            carry = carry + jnp.sum(lf, axis=0, keepdims=True)
            for h in range(N_HEADS):
                col = jnp.sum(jnp.where(lane == h, Fc, 0.0), axis=-1, keepdims=True)
                hi = col.astype(BF16).astype(F32)
                r1 = col - hi
                mid = r1.astype(BF16).astype(F32)
                lo = r1 - mid
                ones_q = jnp.logical_and(lane >= AUG0 + 3, lane < AUG0 + 6)
                ones_k = jnp.logical_and(lane >= AUG0, lane < AUG0 + 3)
                aug_q = jnp.where(lane == AUG0, hi, jnp.where(lane == AUG0 + 1, mid, jnp.where(
                    lane == AUG0 + 2, lo, jnp.where(ones_q, 1.0, 0.0))))
                aug_k = jnp.where(lane == AUG0 + 3, -hi, jnp.where(lane == AUG0 + 4, -mid, jnp.where(
                    lane == AUG0 + 5, -lo, jnp.where(ones_k, 1.0, 0.0))))
                base = (h // 2) * TRIPLE
                qp, kp, vp = (qkv_ref[rows, base + t * LANES:base + (t + 1) * LANES].astype(F32) for t in range(3))
                if h % 2:
                    qp, kp, vp = (pltpu.roll(a, HEAD_DIM, 1) for a in (qp, kp, vp))
                out = slice(h * LANES, (h + 1) * LANES)
                qa_ref[rows, out] = jnp.where(lane < HEAD_DIM, qp * (HEAD_DIM ** -0.5), aug_q).astype(BF16)
                ka_ref[rows, out] = jnp.where(lane < HEAD_DIM, kp, aug_k).astype(BF16)
                va_ref[rows, out] = jnp.where(lane < HEAD_DIM, vp, jnp.where(lane == AUG0, 1.0, 0.0)).astype(BF16)

    fblk = pl.BlockSpec((S, LANES), lambda b: (b, 0))
    out = pl.BlockSpec((S, N_HEADS * LANES), lambda b: (b, 0))
    sh = jax.ShapeDtypeStruct((T, N_HEADS * LANES), BF16)
    return pl.pallas_call(
        body, name=name, grid=(n_seq,),
        in_specs=[fblk, pl.BlockSpec((1, LANES), lambda b: (0, 0)),
                  pl.BlockSpec((S, 4 * TRIPLE), lambda b: (b, OFF_QKV // (4 * TRIPLE)))],
        out_specs=[out, out, out], out_shape=[sh, sh, sh],
        compiler_params=_cp(("parallel",)),
    )(f, bf, proj)


def _band_mask(q0, k0, nq, nk):
    row = q0 + lax.broadcasted_iota(jnp.int32, (nq, nk), 0)
    col = k0 + lax.broadcasted_iota(jnp.int32, (nq, nk), 1)
    return col <= row


_NT = (((1,), (1,)), ((), ()))
_TN = (((0,), (0,)), ((), ()))


def _attn_fwd2(qa, ka, va, n_seq, name):
    T = qa.shape[0]
    S = T // n_seq
    tq, tk, rg = min(Q_TILE, S), min(K_CHUNK, S), ROW_GROUP
    nq, per = S // tq, tq // tk

    def body(q_ref, k_ref, v_ref, o_ref, o32_ref, lse_ref, phi_s, plo_s, mp_s, m_s, acc_s):
        qi = pl.program_id(2)
        mp_s[...] = jnp.full_like(mp_s, NEG_INF)
        acc_s[...] = jnp.zeros_like(acc_s)

        def scores(kc, hh, r0):
            k0 = pl.multiple_of(kc * tk, tk)
            hl = slice(hh * LANES, (hh + 1) * LANES)
            return k0, lax.dot_general(q_ref[r0:, hl], k_ref[pl.ds(k0, tk), hl], _NT, preferred_element_type=F32)

        def max_chunk(kc, masked, r0):
            for hh in range(2):
                k0, s_all = scores(kc, hh, r0)
                for r in range(r0 // rg, tq // rg):
                    rows = slice(r * rg, (r + 1) * rg)
                    s = s_all[r * rg - r0:(r + 1) * rg - r0, :]
                    if masked:
                        s = jnp.where(_band_mask(qi * tq + r * rg, k0, rg, tk), s, NEG_INF)
                    part = s[:, :LANES]
                    for c in range(1, tk // LANES):
                        part = jnp.maximum(part, s[:, c * LANES:(c + 1) * LANES])
                    mp_s[hh, rows, :] = jnp.maximum(mp_s[hh, rows, :], part)

        def sum_chunk(kc, masked, r0):
            for hh in range(2):
                k0, s_all = scores(kc, hh, r0)
                hl = slice(hh * LANES, (hh + 1) * LANES)
                v = v_ref[pl.ds(k0, tk), hl]
                for r in range(r0 // rg, tq // rg):
                    rows = slice(r * rg, (r + 1) * rg)
                    p = jnp.exp(s_all[r * rg - r0:(r + 1) * rg - r0, :] - m_s[hh, rows])
                    if masked:
                        p = jnp.where(_band_mask(qi * tq + r * rg, k0, rg, tk), p, 0.0)
                    p_hi = p.astype(BF16)
                    phi_s[hh, rows, :] = p_hi
                    plo_s[hh, rows, :] = (p - p_hi.astype(F32)).astype(BF16)
                acc_s[hh, r0:, :] += (jnp.dot(phi_s[hh, r0:, :], v, preferred_element_type=F32)
                                      + jnp.dot(plo_s[hh, r0:, :], v, preferred_element_type=F32))

        def sweep(chunk):
            def unmasked(kc, carry):
                chunk(kc, False, 0)
                return carry

            lax.fori_loop(0, qi * per, unmasked, 0)
            for d in range(per):
                chunk(qi * per + d, True, d * tk)

        sweep(max_chunk)
        m_s[...] = jnp.max(mp_s[...], axis=-1, keepdims=True)
        sweep(sum_chunk)

        lane = lax.broadcasted_iota(jnp.int32, (1, LANES), 1)
        outs = []
        for hh in range(2):
            acc = acc_s[hh]
            l = jnp.sum(jnp.where(lane == AUG0, acc, 0.0), axis=-1, keepdims=True)
            lse_ref[hh] = m_s[hh] + jnp.log(l)
            outs.append(acc / l)
        o = jnp.where(lane < HEAD_DIM, outs[0], pltpu.roll(outs[1], HEAD_DIM, 1))
        o_ref[...] = o.astype(BF16)
        o32_ref[...] = o

    qmap = lambda b, j, qi: (b * nq + qi, j)
    omap = lambda b, j, qi: (b * nq + qi, j)
    kv = pl.BlockSpec((S, 2 * LANES), lambda b, j, qi: (b, j))
    return pl.pallas_call(
        body, name=name, grid=(n_seq, N_HEADS // 2, nq),
        in_specs=[pl.BlockSpec((tq, 2 * LANES), qmap), kv, kv],
        out_specs=[pl.BlockSpec((tq, LANES), omap), pl.BlockSpec((tq, LANES), omap),
                   pl.BlockSpec((2, tq, 1), lambda b, j, qi: (j, b * nq + qi, 0))],
        out_shape=[jax.ShapeDtypeStruct((T, BRANCH_W), BF16), jax.ShapeDtypeStruct((T, BRANCH_W), F32),
                   jax.ShapeDtypeStruct((N_HEADS, T, 1), F32)],
        scratch_shapes=[pltpu.VMEM((2, tq, tk), BF16), pltpu.VMEM((2, tq, tk), BF16),
                        pltpu.VMEM((2, tq, LANES), F32), pltpu.VMEM((2, tq, 1), F32),
                        pltpu.VMEM((2, tq, LANES), F32)],
        compiler_params=_cp(("parallel", "parallel", "parallel")),
    )(qa, ka, va)


def _attn_bwd(qa, ka, proj, do, o32, lse, dproj, n_seq, name):
    T = qa.shape[0]
    S = T // n_seq
    tq, tk, rg = min(Q_TILE, S), min(K_CHUNK, S), ROW_GROUP
    nq, per, nkc = S // tq, tq // tk, S // tk

    def body(q_ref, k_ref, v_ref, do_ref, o_ref, lse_ref, _, dqkv_ref, dfk_ref,
             p_s, ds_s, dq_s, dk_s, dv_s, df_s):
        dk_s[...] = jnp.zeros_like(dk_s)
        dv_s[...] = jnp.zeros_like(dv_s)
        df_s[...] = jnp.zeros_like(df_s)
        sels = _pair_masks()

        for qi in range(nq):
            q0 = qi * tq
            do_t = do_ref[q0:q0 + tq, :]
            dq_s[...] = jnp.zeros_like(dq_s)
            prod = do_t.astype(F32) * o_ref[q0:q0 + tq, :]
            dls = [jnp.sum(jnp.where(sel, prod, 0.0), axis=-1, keepdims=True) for sel in sels]

            def chunk(kc, masked, r0, q0=q0, do_t=do_t, dls=dls):
                k0 = pl.multiple_of(kc * tk, tk)
                v = v_ref[pl.ds(k0, tk), :]
                do_a = do_t[r0:, :]
                for hh in range(2):
                    hl = slice(hh * LANES, (hh + 1) * LANES)
                    qh, kh = q_ref[q0 + r0:q0 + tq, hl], k_ref[pl.ds(k0, tk), hl]
                    s_all = lax.dot_general(qh, kh, _NT, preferred_element_type=F32)
                    dom = jnp.where(sels[hh], do_a, jnp.zeros_like(do_a))
                    dp_all = lax.dot_general(dom, v, _NT, preferred_element_type=F32)
                    dfp = jnp.zeros((1, tk), F32)
                    for r in range(r0 // rg, tq // rg):
                        rows = slice(r * rg, (r + 1) * rg)
                        arows = slice(r * rg - r0, (r + 1) * rg - r0)
                        qrows = slice(q0 + r * rg, q0 + (r + 1) * rg)
                        p = jnp.exp(s_all[arows, :] - lse_ref[hh, qrows])
                        if masked:
                            p = jnp.where(_band_mask(q0 + r * rg, k0, rg, tk), p, 0.0)
                        ds = p * (dp_all[arows, :] - dls[hh][rows])
                        p_s[hh, rows, :] = p.astype(BF16)
                        ds_s[hh, rows, :] = ds.astype(BF16)
                        dfp = dfp + jnp.sum(ds, axis=0, keepdims=True)
                    df_s[hh, kc] -= dfp
                    dq_s[hh, r0:, :] += jnp.dot(ds_s[hh, r0:, :], kh, preferred_element_type=F32)
                    dv_s[hh, pl.ds(k0, tk), :] += lax.dot_general(p_s[hh, r0:, :], do_a, _TN,
                                                                  preferred_element_type=F32)
                    dk_s[hh, pl.ds(k0, tk), :] += lax.dot_general(ds_s[hh, r0:, :], qh, _TN,
                                                                  preferred_element_type=F32)

            def unmasked(kc, carry, chunk=chunk):
                chunk(kc, False, 0)
                return carry

            lax.fori_loop(0, qi * per, unmasked, 0)
            for d in range(per):
                chunk(qi * per + d, True, d * tk)
            dq = jnp.where(sels[0], dq_s[0], pltpu.roll(dq_s[1], HEAD_DIM, 1))
            dqkv_ref[q0:q0 + tq, :LANES] = (dq * (HEAD_DIM ** -0.5)).astype(BF16)

        dqkv_ref[:, LANES:2 * LANES] = jnp.where(sels[0], dk_s[0], pltpu.roll(dk_s[1], HEAD_DIM, 1)).astype(BF16)
        dqkv_ref[:, 2 * LANES:] = jnp.where(sels[0], dv_s[0], dv_s[1]).astype(BF16)
        for c in range(nkc):
            dfk_ref[:, :, c * tk:(c + 1) * tk] = df_s[:, c]

    seq = lambda w: pl.BlockSpec((S, w), lambda b, j: (b, j))
    col1 = pl.BlockSpec((2, S, 1), lambda b, j: (j, b, 0))
    vblk = pl.BlockSpec((S, LANES), lambda b, j: (b, OFF_QKV // LANES + 3 * j + 2))
    return pl.pallas_call(
        body, name=name, grid=(n_seq, N_HEADS // 2),
        in_specs=[seq(2 * LANES), seq(2 * LANES), vblk, seq(LANES), seq(LANES), col1,
                  pl.BlockSpec(memory_space=pl.ANY)],
        out_specs=[pl.BlockSpec((S, TRIPLE), lambda b, j: (b, OFF_QKV // TRIPLE + j)),
                   pl.BlockSpec((2, 1, S), lambda b, j: (j, 0, b))],
        out_shape=[jax.ShapeDtypeStruct(dproj.shape, BF16), jax.ShapeDtypeStruct((N_HEADS, 1, T), F32)],
        input_output_aliases={6: 0},
        scratch_shapes=[pltpu.VMEM((2, tq, tk), BF16), pltpu.VMEM((2, tq, tk), BF16),
                        pltpu.VMEM((2, tq, LANES), F32), pltpu.VMEM((2, S, LANES), F32),
                        pltpu.VMEM((2, S, LANES), F32), pltpu.VMEM((2, nkc, 1, tk), F32)],
        compiler_params=_cp(("parallel", "parallel")),
    )(qa, ka, proj, do, o32, lse, dproj)


def _shift_down(v, k, row):
    return jnp.where(row >= k, pltpu.roll(v, k, 0), 0.0)


def _shift_up(v, k, row, S):
    return jnp.where(row < S - k, pltpu.roll(v, S - k, 0), 0.0)


def _pool_diff(uf, w, row):
    acc, k = uf, 1
    while k < w:
        acc = acc + _shift_down(acc, k, row)
        k *= 2
    n = jnp.minimum(row + 1, w).astype(F32)
    return acc / n - uf


def _pool_fwd(proj, pool_w, pool_scale, n_seq, name):
    T = proj.shape[0]
    S = T // n_seq

    def body(u_ref, w_ref, sc_ref, o_ref, d_s):
        g = pl.program_id(1)
        row = lax.broadcasted_iota(jnp.int32, (S, POOL_GD), 0)
        uf = u_ref[...].astype(F32)
        for gi, wlen in enumerate(POOL_WINDOWS):
            @pl.when(g == gi)
            def _(wlen=wlen):
                d_s[...] = _pool_diff(uf, wlen, row).astype(BF16)
        e = jnp.dot(d_s[...], w_ref[0], preferred_element_type=F32)
        o_ref[...] = (e * sc_ref[...]).astype(BF16)

    uc = OFF_U // POOL_GD
    return pl.pallas_call(
        body, name=name, grid=(n_seq, len(POOL_WINDOWS)),
        in_specs=[pl.BlockSpec((S, POOL_GD), lambda b, g: (b, uc + g)),
                  pl.BlockSpec((1, POOL_GD, POOL_GD), lambda b, g: (g, 0, 0)),
                  pl.BlockSpec((1, POOL_GD), lambda b, g: (0, g))],
        out_specs=pl.BlockSpec((S, POOL_GD), lambda b, g: (b, g)),
        out_shape=jax.ShapeDtypeStruct((T, BRANCH_W), BF16),
        scratch_shapes=[pltpu.VMEM((S, POOL_GD), BF16)],
        compiler_params=_cp(("parallel", "parallel")),
    )(proj, pool_w, pool_scale)


def _pool_bwd(proj, dout, pool_w, pool_scale, dproj, n_seq, name):
    T = proj.shape[0]
    S = T // n_seq

    def body(u_ref, do_ref, w_ref, sc_ref, _, du_ref, dw_ref, dsc_ref, d_s):
        g, b = pl.program_id(0), pl.program_id(1)
        row = lax.broadcasted_iota(jnp.int32, (S, POOL_GD), 0)
        uf = u_ref[...].astype(F32)
        for gi, wlen in enumerate(POOL_WINDOWS):
            @pl.when(g == gi)
            def _(wlen=wlen):
                d_s[...] = _pool_diff(uf, wlen, row).astype(BF16)
        db16 = d_s[...]
        w = w_ref[0]
        e = jnp.dot(db16, w, preferred_element_type=F32)
        dof = do_ref[...].astype(F32)
        dsc = jnp.sum(dof * e, axis=0, keepdims=True)
        de = (dof * sc_ref[...]).astype(BF16)
        dd = lax.dot_general(de, w, (((1,), (1,)), ((), ())), preferred_element_type=F32)
        dw = lax.dot_general(db16, de, (((0,), (0,)), ((), ())), preferred_element_type=F32)
        for gi, wlen in enumerate(POOL_WINDOWS):
            @pl.when(g == gi)
            def _(wlen=wlen):
                n = jnp.minimum(row + 1, wlen).astype(F32)
                acc, k = dd / n, 1
                while k < wlen:
                    acc = acc + _shift_up(acc, k, row, S)
                    k *= 2
                du_ref[...] = (acc - dd).astype(BF16)

        @pl.when(b == 0)
        def _():
            dw_ref[0] = dw
            dsc_ref[...] = dsc

        @pl.when(b > 0)
        def _():
            dw_ref[0] += dw
            dsc_ref[...] += dsc

    uc = OFF_U // POOL_GD
    return pl.pallas_call(
        body, name=name, grid=(len(POOL_WINDOWS), n_seq),
        in_specs=[pl.BlockSpec((S, POOL_GD), lambda g, b: (b, uc + g)),
                  pl.BlockSpec((S, POOL_GD), lambda g, b: (b, g)),
                  pl.BlockSpec((1, POOL_GD, POOL_GD), lambda g, b: (g, 0, 0)),
                  pl.BlockSpec((1, POOL_GD), lambda g, b: (0, g)),
                  pl.BlockSpec(memory_space=pl.ANY)],
        out_specs=[pl.BlockSpec((S, POOL_GD), lambda g, b: (b, uc + g)),
                   pl.BlockSpec((1, POOL_GD, POOL_GD), lambda g, b: (g, 0, 0)),
                   pl.BlockSpec((1, POOL_GD), lambda g, b: (0, g))],
        out_shape=[jax.ShapeDtypeStruct(dproj.shape, BF16),
                   jax.ShapeDtypeStruct((len(POOL_WINDOWS), POOL_GD, POOL_GD), F32),
                   jax.ShapeDtypeStruct((1, BRANCH_W), F32)],
        input_output_aliases={4: 0},
        scratch_shapes=[pltpu.VMEM((S, POOL_GD), BF16)],
        compiler_params=_cp(("parallel", "arbitrary")),
    )(proj, dout, pool_w, pool_scale, dproj)


def _conv_fwd(proj, conv_w, n_seq, name):
    T = proj.shape[0]
    S = T // n_seq
    nc = BRANCH_W // LANES

    def body(c_ref, w_ref, o_ref):
        row = lax.broadcasted_iota(jnp.int32, (S, LANES), 0)
        cv, cb, cc = (c_ref[:, t * LANES:(t + 1) * LANES].astype(F32) for t in range(3))
        z = cc * cv
        w = w_ref[...]
        y = w[0:1] * _shift_down(z, 2, row) + w[1:2] * _shift_down(z, 1, row) + w[2:3] * z
        o_ref[...] = (cb * y).astype(BF16)

    return pl.pallas_call(
        body, name=name, grid=(n_seq, nc),
        in_specs=[pl.BlockSpec((S, TRIPLE), lambda b, j: (b, OFF_CONV // TRIPLE + j)),
                  pl.BlockSpec((CONV_K, LANES), lambda b, j: (0, j))],
        out_specs=pl.BlockSpec((S, LANES), lambda b, j: (b, j)),
        out_shape=jax.ShapeDtypeStruct((T, BRANCH_W), BF16),
        compiler_params=_cp(("parallel", "parallel")),
    )(proj, conv_w)


def _conv_bwd(proj, dout, conv_w, dproj, n_seq, name):
    T = proj.shape[0]
    S = T // n_seq
    nc = BRANCH_W // LANES

    def body(c_ref, do_ref, w_ref, _, dc_ref, dw_ref):
        b = pl.program_id(1)
        row = lax.broadcasted_iota(jnp.int32, (S, LANES), 0)
        cv, cb, cc = (c_ref[:, t * LANES:(t + 1) * LANES].astype(F32) for t in range(3))
        dof = do_ref[...].astype(F32)
        w = w_ref[...]
        z = cc * cv
        z1, z2 = _shift_down(z, 1, row), _shift_down(z, 2, row)
        y = w[0:1] * z2 + w[1:2] * z1 + w[2:3] * z
        dy = dof * cb
        dz = w[2:3] * dy + w[1:2] * _shift_up(dy, 1, row, S) + w[0:1] * _shift_up(dy, 2, row, S)
        dc_ref[:, :LANES] = (dz * cc).astype(BF16)
        dc_ref[:, LANES:2 * LANES] = (dof * y).astype(BF16)
        dc_ref[:, 2 * LANES:] = (dz * cv).astype(BF16)
        dws = [jnp.sum(dy * zk, axis=0, keepdims=True) for zk in (z2, z1, z)]

        @pl.when(b == 0)
        def _():
            for kk in range(CONV_K):
                dw_ref[kk:kk + 1, :] = dws[kk]

        @pl.when(b > 0)
        def _():
            for kk in range(CONV_K):
                dw_ref[kk:kk + 1, :] += dws[kk]

    triple = pl.BlockSpec((S, TRIPLE), lambda j, b: (b, OFF_CONV // TRIPLE + j))
    wsp = pl.BlockSpec((CONV_K, LANES), lambda j, b: (0, j))
    return pl.pallas_call(
        body, name=name, grid=(nc, n_seq),
        in_specs=[triple, pl.BlockSpec((S, LANES), lambda j, b: (b, j)), wsp, pl.BlockSpec(memory_space=pl.ANY)],
        out_specs=[triple, wsp],
        out_shape=[jax.ShapeDtypeStruct(dproj.shape, BF16), jax.ShapeDtypeStruct((CONV_K, BRANCH_W), F32)],
        input_output_aliases={3: 0},
        compiler_params=_cp(("parallel", "arbitrary")),
    )(proj, dout, conv_w, dproj)


def _mix_fwd(oa, ob, oc, wpa, wpp, wpc, proj, b_gate, name):
    T = oa.shape[0]
    tm = min(512, T)

    def body(oa_ref, ob_ref, oc_ref, wa_ref, wp_ref, wc_ref, g_ref, bg_ref, o_ref):
        acc = jnp.zeros((tm, D_MODEL), F32)
        for i, (x_ref, w_ref) in enumerate(((oa_ref, wa_ref), (ob_ref, wp_ref), (oc_ref, wc_ref))):
            y = jnp.dot(x_ref[...], w_ref[...], preferred_element_type=F32)
            sl = slice(i * D_MODEL, (i + 1) * D_MODEL)
            acc = acc + _sigmoid(g_ref[:, sl].astype(F32) + bg_ref[:, sl]) * y
        o_ref[...] = acc.astype(BF16)

    br = pl.BlockSpec((tm, BRANCH_W), lambda i: (i, 0))
    wsp = pl.BlockSpec((BRANCH_W, D_MODEL), lambda i: (0, 0))
    return pl.pallas_call(
        body, name=name, grid=(T // tm,),
        in_specs=[br, br, br, wsp, wsp, wsp, pl.BlockSpec((tm, GATE_W), lambda i: (i, 0)),
                  pl.BlockSpec((1, GATE_W), lambda i: (0, 0))],
        out_specs=pl.BlockSpec((tm, D_MODEL), lambda i: (i, 0)),
        out_shape=jax.ShapeDtypeStruct((T, D_MODEL), BF16),
        compiler_params=_cp(("parallel",)),
    )(oa, ob, oc, wpa, wpp, wpc, proj, b_gate)


def _mix_bwd(oa, ob, oc, wpa, wpp, wpc, proj, b_gate, dmixed, name):
    T = oa.shape[0]
    tm = min(256, T)

    def body(oa_ref, ob_ref, oc_ref, wa_ref, wp_ref, wc_ref, g_ref, bg_ref, dm_ref,
             dya_ref, dyb_ref, dyc_ref, dg_ref, dbg_ref):
        i0 = pl.program_id(0)
        dm = dm_ref[...].astype(F32)
        parts = []
        for i, (x_ref, w_ref, dy_ref) in enumerate(((oa_ref, wa_ref, dya_ref), (ob_ref, wp_ref, dyb_ref),
                                                    (oc_ref, wc_ref, dyc_ref))):
            y = jnp.dot(x_ref[...], w_ref[...], preferred_element_type=F32)
            sl = slice(i * D_MODEL, (i + 1) * D_MODEL)
            gate = _sigmoid(g_ref[:, sl].astype(F32) + bg_ref[:, sl])
            dy_ref[...] = (dm * gate).astype(BF16)
            dgl = dm * y * gate * (1.0 - gate)
            dg_ref[:, sl] = dgl.astype(BF16)
            parts.append(jnp.sum(dgl, axis=0, keepdims=True))

        @pl.when(i0 == 0)
        def _():
            for i in range(3):
                dbg_ref[:, i * D_MODEL:(i + 1) * D_MODEL] = parts[i]

        @pl.when(i0 > 0)
        def _():
            for i in range(3):
                dbg_ref[:, i * D_MODEL:(i + 1) * D_MODEL] += parts[i]

    br = pl.BlockSpec((tm, BRANCH_W), lambda i: (i, 0))
    wsp = pl.BlockSpec((BRANCH_W, D_MODEL), lambda i: (0, 0))
    row = pl.BlockSpec((tm, D_MODEL), lambda i: (i, 0))
    gsp = pl.BlockSpec((tm, GATE_W), lambda i: (i, 0))
    bsp = pl.BlockSpec((1, GATE_W), lambda i: (0, 0))
    act = jax.ShapeDtypeStruct((T, D_MODEL), BF16)
    return pl.pallas_call(
        body, name=name, grid=(T // tm,),
        in_specs=[br, br, br, wsp, wsp, wsp, gsp, bsp, row],
        out_specs=[row, row, row, gsp, bsp],
        out_shape=[act, act, act, jax.ShapeDtypeStruct((T, MAIN_COLS), BF16),
                   jax.ShapeDtypeStruct((1, GATE_W), F32)],
        compiler_params=_cp(("arbitrary",)),
    )(oa, ob, oc, wpa, wpp, wpc, proj, b_gate, dmixed)


GU_TILE = 256


def _gu_col(c):
    t, r = divmod(c, GU_TILE)
    return (t // 2) * GU_TILE + r + (FFN_HIDDEN if t % 2 else 0)


def _gate_up_swiglu(h, w, name):
    T, K = h.shape
    tm = min(2048, T)

    def body(h_ref, w_ref, ab_ref, s_ref):
        prod = jnp.dot(h_ref[...], w_ref[...], preferred_element_type=F32)
        ab_ref[...] = prod.astype(BF16)
        a = prod[:, :GU_TILE]
        s_ref[...] = (a * _sigmoid(a) * prod[:, GU_TILE:]).astype(BF16)

    return pl.pallas_call(
        body, name=name, grid=(T // tm, FFN_HIDDEN // GU_TILE),
        in_specs=[pl.BlockSpec((tm, K), lambda i, j: (i, 0)), pl.BlockSpec((K, 2 * GU_TILE), lambda i, j: (0, j))],
        out_specs=[pl.BlockSpec((tm, 2 * GU_TILE), lambda i, j: (i, j)), pl.BlockSpec((tm, GU_TILE), lambda i, j: (i, j))],
        out_shape=[jax.ShapeDtypeStruct((T, 2 * FFN_HIDDEN), BF16), jax.ShapeDtypeStruct((T, FFN_HIDDEN), BF16)],
        compiler_params=_cp(("parallel", "parallel")),
    )(h, w)


def _swiglu_bwd_fused(dx, w_down, ab, name):
    T, K = dx.shape
    tm = min(2048, T)

    def body(dx_ref, w_ref, ab_ref, o_ref):
        ds = lax.dot_general(dx_ref[...], w_ref[...], _NT, preferred_element_type=F32)
        a = ab_ref[:, :GU_TILE].astype(F32)
        b = ab_ref[:, GU_TILE:].astype(F32)
        sg = _sigmoid(a)
        o_ref[:, :GU_TILE] = (ds * b * sg * (1.0 + a * (1.0 - sg))).astype(BF16)
        o_ref[:, GU_TILE:] = (ds * a * sg).astype(BF16)

    pair = pl.BlockSpec((tm, 2 * GU_TILE), lambda i, j: (i, j))
    return pl.pallas_call(
        body, name=name, grid=(T // tm, FFN_HIDDEN // GU_TILE),
        in_specs=[pl.BlockSpec((tm, K), lambda i, j: (i, 0)), pl.BlockSpec((GU_TILE, K), lambda i, j: (j, 0)), pair],
        out_specs=pair, out_shape=jax.ShapeDtypeStruct((T, 2 * FFN_HIDDEN), BF16),
        compiler_params=_cp(("parallel", "parallel")),
    )(dx, w_down, ab)


def _adamw_update(w_ref, g_ref, m_ref, v_ref, d_ref, nm_ref, nv_ref):
    gv = g_ref[...]
    nm = ADAM_B1 * m_ref[...] + (1.0 - ADAM_B1) * gv
    nv = ADAM_B2 * v_ref[...] + (1.0 - ADAM_B2) * (gv * gv)
    m_hat = nm / (1.0 - ADAM_B1 ** ADAM_STEP)
    v_hat = nv / (1.0 - ADAM_B2 ** ADAM_STEP)
    d_ref[...] = -ADAM_LR * (m_hat / (jnp.sqrt(v_hat) + ADAM_EPS) + ADAM_WD * w_ref[...])
    nm_ref[...] = nm
    nv_ref[...] = nv


def _adamw_many(ws, gs, ms, vs, name):
    n = len(ws)

    def body(*refs):
        ins, outs = refs[:4 * n], refs[4 * n:]
        for t in range(n):
            _adamw_update(ins[t], ins[n + t], ins[2 * n + t], ins[3 * n + t], outs[t], outs[n + t], outs[2 * n + t])

    shapes = [jax.ShapeDtypeStruct(w.shape, F32) for w in ws]
    out = pl.pallas_call(body, name=name, out_shape=shapes * 3, compiler_params=_cp())(*ws, *gs, *ms, *vs)
    return out[:n], out[n:2 * n], out[2 * n:]


def _adamw(w, g, m, v, name):
    R, C = w.shape
    tr = R
    for cand in (256, 352, 128, 64, 8):
        if R > cand and R % cand == 0:
            tr = cand
            break

    def body(w_ref, g_ref, m_ref, v_ref, d_ref, nm_ref, nv_ref):
        _adamw_update(w_ref, g_ref, m_ref, v_ref, d_ref, nm_ref, nv_ref)

    blk = pl.BlockSpec((tr, C), lambda i: (i, 0))
    sh = jax.ShapeDtypeStruct((R, C), F32)
    return pl.pallas_call(
        body, name=name, grid=(R // tr,), in_specs=[blk] * 4, out_specs=[blk] * 3, out_shape=[sh] * 3,
        compiler_params=_cp(("parallel",)),
    )(w, g, m, v)


def _adamw_3d(w, g, m, v, block, name):
    shape = w.shape
    grid = (shape[0] // block[0], shape[1] // block[1])
    assert shape[0] % block[0] == 0 and shape[1] % block[1] == 0 and block[2] == shape[2], (name, shape, block)

    def body(w_ref, g_ref, m_ref, v_ref, d_ref, nm_ref, nv_ref):
        _adamw_update(w_ref, g_ref, m_ref, v_ref, d_ref, nm_ref, nv_ref)

    blk = pl.BlockSpec(block, lambda i, j: (i, j, 0))
    sh = jax.ShapeDtypeStruct(shape, F32)
    return pl.pallas_call(
        body, name=name, grid=grid, in_specs=[blk] * 4, out_specs=[blk] * 3, out_shape=[sh] * 3,
        compiler_params=_cp(("parallel", "parallel")),
    )(w, g, m, v)


def _sum_slabs_t(x, name):
    n, R, C = x.shape

    def body(x_ref, o_ref):
        acc = x_ref[0].astype(F32)
        for j in range(1, n):
            acc = acc + x_ref[j].astype(F32)
        o_ref[...] = acc.T

    return pl.pallas_call(
        body, name=name, grid=(C // LANES,), in_specs=[pl.BlockSpec((n, R, LANES), lambda j: (0, 0, j))],
        out_specs=pl.BlockSpec((LANES, R), lambda j: (j, 0)), out_shape=jax.ShapeDtypeStruct((C, R), F32),
        compiler_params=_cp(("parallel",)),
    )(x)


def _sum_slabs(x, name):
    n, R, C = x.shape
    tr = R
    for cand in (512, 256, 128, 64, 32, 16, 8):
        if R > cand and R % cand == 0:
            tr = cand
            break

    def body(x_ref, o_ref):
        acc = x_ref[0].astype(F32)
        for j in range(1, n):
            acc = acc + x_ref[j].astype(F32)
        o_ref[...] = acc

    return pl.pallas_call(
        body, name=name, grid=(R // tr,), in_specs=[pl.BlockSpec((n, tr, C), lambda i: (0, i, 0))],
        out_specs=pl.BlockSpec((tr, C), lambda i: (i, 0)), out_shape=jax.ShapeDtypeStruct((R, C), F32),
        compiler_params=_cp(("parallel",)),
    )(x)


def _multi_gather(xs, layers, name):
    nt = len(xs)
    shapes = [x.shape if lay is None else x.shape[1:] for x, lay in zip(xs, layers)]

    def body(*refs):
        x_refs, out_refs = refs[:nt], refs[nt:2 * nt]
        send_sems, recv_sems, local_sems = refs[2 * nt:]
        x_, y_, c_ = lax.axis_index("x"), lax.axis_index("y"), lax.axis_index("c")
        me, sibling = (x_, y_, c_), (x_, y_, 1 - c_)
        chips = [(1 - x_, y_), (x_, 1 - y_), (1 - x_, 1 - y_)]

        def own_block(t):
            return x_refs[t] if layers[t] is None else x_refs[t].at[layers[t]]

        def copy(t, k, block, to, own=False):
            px, py, pc = block
            dst = out_refs[t].at[4 * px + 2 * py + pc]
            return pltpu.make_async_remote_copy(
                src_ref=own_block(t) if own else dst, dst_ref=dst,
                send_sem=send_sems.at[t, k], recv_sem=recv_sems.at[t, k],
                device_id=to, device_id_type=pl.DeviceIdType.MESH)

        mine, first, passed = [], [], []
        for t in range(nt):
            mine.append(pltpu.make_async_copy(own_block(t), out_refs[t].at[4 * x_ + 2 * y_ + c_], local_sems.at[t]))
            mine[-1].start()
            first.append([copy(t, 1 + j, me, (*chip, c_), own=True) for j, chip in enumerate(chips)]
                         + [copy(t, 0, me, sibling, own=True)])
            for cp in first[-1]:
                cp.start()
        for t in range(nt):
            for j, chip in enumerate(chips):
                copy(t, 1 + j, (*chip, c_), me).wait_recv()
                passed.append(copy(t, 4 + j, (*chip, c_), sibling))
                passed[-1].start()
        for t in range(nt):
            copy(t, 0, sibling, me).wait_recv()
            for j, chip in enumerate(chips):
                copy(t, 4 + j, (*chip, 1 - c_), me).wait_recv()
        for cp in [c for f in first for c in f] + passed:
            cp.wait_send()
        for cp in mine:
            cp.wait()

    hbm = pl.BlockSpec(memory_space=pl.ANY)
    return pl.pallas_call(
        body, name=name, out_shape=[jax.ShapeDtypeStruct((N_DEV,) + tuple(s), x.dtype) for s, x in zip(shapes, xs)],
        in_specs=[hbm] * nt, out_specs=[hbm] * nt,
        scratch_shapes=[pltpu.SemaphoreType.DMA((nt, 7)), pltpu.SemaphoreType.DMA((nt, 7)),
                        pltpu.SemaphoreType.DMA((nt,))],
    )(*xs)


_HBM = pl.BlockSpec(memory_space=pltpu.HBM)
_SEM = pl.BlockSpec(memory_space=pltpu.SEMAPHORE)
_PEER_ORDER = (2, 4, 6, 3, 5, 7, 1)


def _split_copies(src_refs, land_refs, send_sems, recv_sems, layers, per_peer):
    x_, y_, c_ = lax.axis_index("x"), lax.axis_index("y"), lax.axis_index("c")
    me = 4 * x_ + 2 * y_ + c_
    copies = []
    for k in _PEER_ORDER:
        px, py, pc = x_ ^ ((k >> 2) & 1), y_ ^ ((k >> 1) & 1), c_ ^ (k & 1)
        peer = 4 * px + 2 * py + pc
        for t in range(len(src_refs)):
            if per_peer:
                src = src_refs[t].at[peer]
            else:
                src = src_refs[t] if layers[t] is None else src_refs[t].at[layers[t]]
            copies.append(pltpu.make_async_remote_copy(
                src_ref=src, dst_ref=land_refs[t].at[me],
                send_sem=send_sems.at[t * (N_DEV - 1) + k - 1], recv_sem=recv_sems.at[t * (N_DEV - 1) + k - 1],
                device_id=(px, py, pc), device_id_type=pl.DeviceIdType.MESH))
    return copies


def _own_copies(src_refs, land_refs, sems, layers, per_peer):
    nt = len(src_refs)
    me = 4 * lax.axis_index("x") + 2 * lax.axis_index("y") + lax.axis_index("c")
    copies = []
    for t in range(nt):
        if per_peer:
            src = src_refs[t].at[me]
        else:
            src = src_refs[t] if layers[t] is None else src_refs[t].at[layers[t]]
        copies.append(pltpu.make_async_copy(src, land_refs[t].at[me], sems.at[nt * (N_DEV - 1) + t]))
    return copies


def _split_start(srcs, layers, per_peer, after, name):
    nt = len(srcs)
    if per_peer:
        land_shapes = [s.shape for s in srcs]
    else:
        land_shapes = [(N_DEV,) + tuple(s.shape if lay is None else s.shape[1:]) for s, lay in zip(srcs, layers)]

    def body(*refs):
        src_refs, land_refs = refs[:nt], refs[nt:2 * nt]
        send_sems, recv_sems = refs[2 * nt + 1], refs[2 * nt + 2]
        token = refs[-1]
        for cp in _split_copies(src_refs, land_refs, send_sems, recv_sems, layers, per_peer):
            cp.start()
        for cp in _own_copies(src_refs, land_refs, send_sems, layers, per_peer):
            cp.start()
        token[...] = jnp.zeros_like(token)

    lands = [pltpu.with_memory_space_constraint(lax.empty(s, x.dtype), pltpu.HBM) for s, x in zip(land_shapes, srcs)]
    srcs = [pltpu.with_memory_space_constraint(x, pltpu.HBM) for x in srcs]
    out = pl.pallas_call(
        body, name=name,
        out_shape=(pltpu.SemaphoreType.DMA((nt * N_DEV,)), pltpu.SemaphoreType.DMA((nt * (N_DEV - 1),)),
                   *[pltpu.HBM(x.shape, x.dtype) for x in srcs], *[pltpu.HBM(s, x.dtype) for s, x in zip(land_shapes, srcs)],
                   jax.ShapeDtypeStruct((8, LANES), F32)),
        in_specs=[_HBM] * (2 * nt) + [pl.BlockSpec(memory_space=pl.ANY)],
        out_specs=(_SEM, _SEM, *([_HBM] * (2 * nt)), pl.BlockSpec(memory_space=pltpu.VMEM)),
        input_output_aliases={i: 2 + i for i in range(2 * nt)},
        compiler_params=pltpu.CompilerParams(has_side_effects=pltpu.SideEffectType.DATAFLOW_SIDE_EFFECTING),
    )(*srcs, *lands, after)
    return out[0], out[1], list(out[2:2 + nt]), list(out[2 + nt:2 + 2 * nt]), out[-1]


def _split_wait(started, layers, per_peer, after, name):
    send_sems, recv_sems, srcs, lands, _ = started
    nt = len(srcs)

    def body(*refs):
        src_refs, land_refs = refs[:nt], refs[nt:2 * nt]
        s_sems, r_sems = refs[2 * nt], refs[2 * nt + 1]
        for cp in _split_copies(src_refs, land_refs, s_sems, r_sems, layers, per_peer):
            cp.wait_send()
            cp.wait_recv()
        for cp in _own_copies(src_refs, land_refs, s_sems, layers, per_peer):
            cp.wait()

    out = pl.pallas_call(
        body, name=name,
        out_shape=tuple(pltpu.HBM(x.shape, x.dtype) for x in srcs + lands),
        in_specs=[_HBM] * (2 * nt) + [_SEM, _SEM, pl.BlockSpec(memory_space=pl.ANY)],
        out_specs=tuple([_HBM] * (2 * nt)),
        input_output_aliases={i: i for i in range(2 * nt)},
        compiler_params=pltpu.CompilerParams(has_side_effects=pltpu.SideEffectType.DATAFLOW_SIDE_EFFECTING),
    )(*srcs, *lands, send_sems, recv_sems, after)
    return list(out[nt:])


def _runs(mapping):
    runs, c, n = [], 0, len(mapping)
    while c < n:
        if mapping[c] is None:
            c += 1
            continue
        sid, d, lo = mapping[c][0], mapping[c][1] - c, c
        while c < n and mapping[c] is not None and mapping[c][0] == sid and mapping[c][1] - c == d:
            c += 1
        runs.append((lo, c, sid, d))
    return runs


def _tile_plan(mapping, src_widths):
    runs = _runs(mapping)
    plan = []
    for t in range(len(mapping) // LANES):
        pieces = []
        for lo, hi, sid, d in runs:
            lo_t, hi_t = max(lo, t * LANES), min(hi, (t + 1) * LANES)
            if lo_t >= hi_t:
                continue
            a = ((lo_t + d) // LANES) * LANES
            win = min(2 * LANES, src_widths[sid] - a)
            shift = t * LANES + d - a
            pieces.append((sid, a, win, shift, lo_t - t * LANES, hi_t - t * LANES))
        plan.append(pieces)
    return plan


def _reblock(srcs, src_views, outs, out_views, name):
    R = srcs[0].shape[-2]
    tr = min(512, R)
    widths = {sid: srcs[ai].shape[-1] for sid, (ai, _) in src_views.items()}
    plans = [(ai, li, _tile_plan(mapping, widths)) for ai, li, mapping in out_views]
    ns = len(srcs)

    def body(*refs):
        s_refs, o_refs = refs[:ns], refs[ns:]
        cache = {}

        def shift_matrix(win, shift, lo, hi):
            key = (win, shift, lo, hi)
            if key not in cache:
                r = lax.broadcasted_iota(jnp.int32, (win, LANES), 0)
                c = lax.broadcasted_iota(jnp.int32, (win, LANES), 1)
                hit = jnp.logical_and(r - c == shift, jnp.logical_and(c >= lo, c < hi))
                cache[key] = jnp.where(hit, 1.0, 0.0).astype(BF16)
            return cache[key]

        for ai, li, plan in plans:
            for t, pieces in enumerate(plan):
                acc = None
                whole = len(pieces) == 1 and pieces[0][3:] == (0, 0, LANES)
                for sid, a, win, shift, lo, hi in pieces:
                    sa, sl = src_views[sid]
                    if whole:
                        win = LANES
                    src = s_refs[sa][:, a:a + win] if sl is None else s_refs[sa][sl, :, a:a + win]
                    if whole:
                        acc = src
                    else:
                        part = jnp.dot(src, shift_matrix(win, shift, lo, hi), preferred_element_type=F32)
                        acc = part if acc is None else acc + part
                val = jnp.zeros((tr, LANES), BF16) if acc is None else acc.astype(BF16)
                if li is None:
                    o_refs[ai][:, t * LANES:(t + 1) * LANES] = val
                else:
                    o_refs[ai][li, :, t * LANES:(t + 1) * LANES] = val

    def spec(shape):
        if len(shape) == 2:
            return pl.BlockSpec((tr, shape[1]), lambda i: (i, 0))
        return pl.BlockSpec((shape[0], tr, shape[2]), lambda i: (0, i, 0))

    return pl.pallas_call(
        body, name=name, grid=(R // tr,), in_specs=[spec(s.shape) for s in srcs],
        out_specs=[spec(s) for s in outs], out_shape=[jax.ShapeDtypeStruct(s, BF16) for s in outs],
        compiler_params=_cp(("parallel",)),
    )(*srcs)


SHARDED = ("w_in", "w_gate_up", "w_proj_attn", "w_proj_pool", "w_proj_conv", "w_out", "w_down")
WEIGHT_ORDER = ("attn_norm", "w_in", "b_forget", "b_gate", "w_proj_attn", "pool_w", "pool_scale", "w_proj_pool",
                "conv_w", "w_proj_conv", "w_out", "ffn_norm", "w_gate_up", "w_down", "final_norm")
IN_SHARD, IN_SHARD_PAD = IN_COLS // N_DEV, 896
GU_SHARD, GU_SHARD_PAD = 2 * FFN_HIDDEN // N_DEV, 768


def _w_in_col(c):
    if c < OFF_QKV:
        return c + 3592
    if c < OFF_U:
        base, off = (0, OFF_QKV) if c < OFF_CONV else (2056, OFF_CONV)
        j, t = divmod(c - off, TRIPLE)
        which, e = divmod(t, LANES)
        return base + which * BRANCH_W + j * LANES + e
    return c - OFF_U + 1544


def _w_in_full(gathered, name):
    main = [divmod(_w_in_col(c), IN_SHARD) for c in range(MAIN_COLS)]
    fcols = [divmod(1536 + c, IN_SHARD) if c < N_HEADS else None for c in range(LANES)]
    R = gathered.shape[1]
    return _reblock([gathered], {i: (0, i) for i in range(N_DEV)}, [(R, MAIN_COLS), (R, LANES)],
                    [(0, None, main), (1, None, fcols)], name)


def _w_in_slabs(dmain, dwf, name):
    inv = {_w_in_col(c): ("m", c) for c in range(MAIN_COLS)}
    inv.update({1536 + c: ("f", c) for c in range(N_HEADS)})
    views = []
    for i in range(N_DEV):
        mapping = [inv[IN_SHARD * i + j] if j < IN_SHARD else None for j in range(IN_SHARD_PAD)]
        views.append((0, i, mapping))
    R = dmain.shape[0]
    return _reblock([dmain, dwf], {"m": (0, None), "f": (1, None)}, [(N_DEV, R, IN_SHARD_PAD)], views, name)[0]


def _w_gu_full(gathered, name):
    mapping = [divmod(_gu_col(c), GU_SHARD) for c in range(2 * FFN_HIDDEN)]
    R = gathered.shape[1]
    return _reblock([gathered], {i: (0, i) for i in range(N_DEV)}, [(R, 2 * FFN_HIDDEN)], [(0, None, mapping)], name)[0]


def _w_gu_slabs(dw, name):
    inv = {_gu_col(c): c for c in range(2 * FFN_HIDDEN)}
    views = [(0, i, [("w", inv[GU_SHARD * i + j]) if j < GU_SHARD else None for j in range(GU_SHARD_PAD)])
             for i in range(N_DEV)]
    R = dw.shape[0]
    return _reblock([dw], {"w": (0, None)}, [(N_DEV, R, GU_SHARD_PAD)], views, name)[0]


def _layer_fwd(x, W, n_seq, l, h1=None, next_norm=None):
    T = x.shape[0]
    sfx = f"_l{l}"
    if h1 is None:
        h1 = _rms_fwd(x, W["attn_norm"], "rms1" + sfx)
    proj, f = _matmul(h1, W["w_main"], mode="nn", out_dtype=BF16, name="proj_main" + sfx, side=(W["w_f"], F32))
    qa, ka, va = _fox_prep(f, W["b_forget"], proj, n_seq, "fox_prep" + sfx)
    oa, oa32, lse = _attn_fwd2(qa, ka, va, n_seq, "attn_fwd" + sfx)
    if "late" in W:
        W.update(W.pop("late")(oa))
    ob = _pool_fwd(proj, W["pool_w"], W["pool_scale"], n_seq, "pool_fwd" + sfx)
    oc = _conv_fwd(proj, W["conv_w"], n_seq, "conv_fwd" + sfx)
    mixed = _mix_fwd(oa, ob, oc, W["w_proj_attn"], W["w_proj_pool"], W["w_proj_conv"], proj, W["b_gate"],
                     "mix_fwd" + sfx)
    x2, h2 = _matmul(mixed, W["w_out"], mode="nn", out_dtype=F32, name="out_proj" + sfx, tm=1024, tn=1024,
                     residual=x, rms_g=W["ffn_norm"])
    ab, s = _gate_up_swiglu(h2, W["w_gate_up"], "gate_up" + sfx)
    x3 = _matmul(s, W["w_down"], mode="nn", out_dtype=F32, name="down" + sfx, tm=1024, tn=1024, tk=1408,
                 residual=x2, rms_g=next_norm)
    x3, h1_next = x3 if next_norm is not None else (x3, None)
    saved = dict(x=x, h1=h1, proj=proj, f=f, qa=qa, ka=ka, oa=oa, oa32=oa32, lse=lse, ob=ob, oc=oc, mixed=mixed, x2=x2,
                 h2=h2, ab=ab, s=s)
    return x3, saved, h1_next


def _layer_bwd(dx3, dx3b, W, sv, n_seq, l, stage=None):
    T = dx3.shape[0]
    sfx = f"_l{l}"
    G = {}
    stage = stage or (lambda l, group, G, W: W)
    dab = _swiglu_bwd_fused(dx3b, W["w_down"], sv["ab"], "d_ab" + sfx)
    G["w_down"] = _matmul(sv["s"], dx3b, mode="tn", out_dtype=BF16, name="dw_down" + sfx, tm=256, tn=1024)
    dh2 = _matmul(dab, W["w_gate_up"], mode="nt", out_dtype=BF16, name="d_h2" + sfx, tm=1024, tn=1024, tk=1408)
    G["w_gate_up"] = _matmul(sv["h2"], dab, mode="tn", out_dtype=BF16, name="dw_gate_up" + sfx, tm=1024)
    W = stage(l, "ffn", G, W)
    dx2, dx2b, G["ffn_norm"] = _rms_bwd(sv["x2"], W["ffn_norm"], dh2, dx3, "rms2_bwd" + sfx)
    dmixed = _matmul(dx2b, W["w_out"], mode="nt", out_dtype=BF16, name="d_mixed" + sfx)
    G["w_out"] = _matmul(sv["mixed"], dx2b, mode="tn", out_dtype=BF16, name="dw_out" + sfx, tm=1024)
    dya, dyb, dyc, dproj, G["b_gate"] = _mix_bwd(sv["oa"], sv["ob"], sv["oc"], W["w_proj_attn"], W["w_proj_pool"],
                                                 W["w_proj_conv"], sv["proj"], W["b_gate"], dmixed, "mix_bwd" + sfx)
    douts = {}
    for br, dy, o in (("attn", dya, sv["oa"]), ("pool", dyb, sv["ob"]), ("conv", dyc, sv["oc"])):
        douts[br] = _matmul(dy, W["w_proj_" + br], mode="nt", out_dtype=BF16, name=f"d_{br}_out" + sfx)
        G["w_proj_" + br] = _matmul(o, dy, mode="tn", out_dtype=BF16, name=f"dw_proj_{br}" + sfx, tm=512)
    W = stage(l, "mix", G, W)
    dproj, G["conv_w"] = _conv_bwd(sv["proj"], douts["conv"], W["conv_w"], dproj, n_seq, "conv_bwd" + sfx)
    dproj, G["pool_w"], G["pool_scale"] = _pool_bwd(sv["proj"], douts["pool"], W["pool_w"], W["pool_scale"], dproj,
                                                    n_seq, "pool_bwd" + sfx)
    dproj, dFk = _attn_bwd(sv["qa"], sv["ka"], sv["proj"], douts["attn"], sv["oa32"], sv["lse"], dproj, n_seq,
                           "attn_bwd" + sfx)
    dF = jnp.pad(dFk.reshape(N_HEADS, T).T, ((0, 0), (0, LANES - N_HEADS)))
    df, G["b_forget"] = _fox_cumsum_bwd(sv["f"], W["b_forget"], dF, n_seq, "fox_cumsum_bwd" + sfx)
    G["w_main"], G["w_f"] = _matmul(sv["h1"], dproj, mode="tn", out_dtype=BF16, name="dw_main" + sfx, tm=1024,
                                    side=(df, BF16))
    W = stage(l, "w_in", G, W)
    dh1 = _matmul(dproj, W["w_main"], mode="nt", out_dtype=BF16, name="d_h1_main" + sfx, tm=1024, tn=1024, tk=1664,
                  extra=(df, W["w_f"]))
    dx, dxb, G["attn_norm"] = _rms_bwd(sv["x"], W["attn_norm"], dh1, dx2, "rms1_bwd" + sfx)
    return dx, dxb, G


def _replicated_operands(rep, l):
    W = {}
    W["attn_norm"], W["ffn_norm"] = rep["attn_norm"][l], rep["ffn_norm"][l]
    W["b_forget"] = jnp.pad(rep["b_forget"][l].reshape(1, N_HEADS), ((0, 0), (0, LANES - N_HEADS)))
    W["b_gate"] = rep["b_gate"][l].reshape(1, GATE_W)
    W["pool_w"] = rep["pool_w"][l].astype(BF16)
    W["pool_scale"] = rep["pool_scale"][l].reshape(1, BRANCH_W)
    return W


def _local_step(x, target, get_W, attn_norms, final_norm, stage=None):
    n_seq, S, Dm = x.shape
    T = n_seq * S
    xt = x.reshape(T, Dm)
    saved, Ws, h1 = [], [], None
    for l in range(DEPTH):
        Ws.append(get_W(l, xt))
        next_norm = attn_norms[l + 1] if l + 1 < DEPTH else None
        xt, sv, h1 = _layer_fwd(xt, Ws[l], n_seq, l, h1, next_norm)
        saved.append(sv)
    loss, dx, dxb, g_final = _loss_head(xt, final_norm, target.reshape(T, Dm), "loss_head")
    grads = [None] * DEPTH
    for l in reversed(range(DEPTH)):
        dx, dxb, grads[l] = _layer_bwd(dx, dxb, Ws[l], saved[l], n_seq, l, stage)
    return loss, dx.reshape(n_seq, S, Dm), grads, g_final


def _padded_shards(weights):
    pads = {"w_in": IN_SHARD_PAD - IN_SHARD, "w_gate_up": GU_SHARD_PAD - GU_SHARD}
    return {n: jnp.pad(weights[n], ((0, 0), (0, 0), (0, pads.get(n, 0)))).astype(BF16) for n in SHARDED}


def _full_operands(g, l):
    W = {}
    if "w_in" in g:
        W["w_main"], W["w_f"] = _w_in_full(g["w_in"], f"w_in_full_l{l}")
    if "w_gate_up" in g:
        W["w_gate_up"] = _w_gu_full(g["w_gate_up"], f"w_gate_up_full_l{l}")
    for n in ("w_proj_attn", "w_proj_pool", "w_proj_conv"):
        if n in g:
            W[n] = jnp.transpose(g[n], (1, 0, 2)).reshape(BRANCH_W, D_MODEL)
    if "w_out" in g:
        W["w_out"] = g["w_out"].reshape(D_MODEL, D_MODEL)
    if "w_down" in g:
        W["w_down"] = g["w_down"].reshape(FFN_HIDDEN, D_MODEL)
    return W


GRAD_GROUPS = {"ffn": ("w_down", "w_gate_up"),
               "mix": ("w_out", "w_proj_attn", "w_proj_pool", "w_proj_conv"),
               "w_in": ("w_in",)}


def _grad_slabs(G, n, l):
    if n == "w_in":
        return _w_in_slabs(G["w_main"], G["w_f"], f"w_in_slabs_l{l}")
    if n == "w_gate_up":
        return _w_gu_slabs(G["w_gate_up"], f"w_gate_up_slabs_l{l}")
    if n == "w_out":
        return G["w_out"].reshape(N_DEV, D_MODEL // N_DEV, D_MODEL)
    if n == "w_down":
        return G["w_down"].reshape(N_DEV, FFN_HIDDEN // N_DEV, D_MODEL)
    return jnp.transpose(G[n].reshape(BRANCH_W, N_DEV, D_MODEL // N_DEV), (1, 0, 2))


def _sum_layer_grads(recv, l):
    out = {}
    for n, r in recv.items():
        if n in ("w_in", "w_gate_up"):
            out[n] = _sum_slabs_t(r, f"sum_{n}_l{l}")[:IN_SHARD if n == "w_in" else GU_SHARD]
        else:
            out[n] = _sum_slabs(r, f"sum_{n}_l{l}")
    return out


def _sum_small(xs, name):
    def body(*refs):
        for x_ref, o_ref in zip(refs[:len(xs)], refs[len(xs):]):
            acc = x_ref[0]
            for j in range(1, N_DEV):
                acc = acc + x_ref[j]
            o_ref[...] = acc

    return pl.pallas_call(
        body, name=name, out_shape=[jax.ShapeDtypeStruct(x.shape[1:], F32) for x in xs],
        compiler_params=_cp(),
    )(*xs)


def _as_2d(a):
    if a.ndim == 1:
        return a.reshape(1, -1)
    return a.reshape(-1, a.shape[-1])


def kernel(x, attn_norm, w_in, b_forget, b_gate, w_proj_attn, pool_w, pool_scale, w_proj_pool, conv_w, w_proj_conv, w_out, ffn_norm, w_gate_up, w_down, final_norm, loss_target, m_attn_norm, m_w_in, m_b_forget, m_b_gate, m_w_proj_attn, m_pool_w, m_pool_scale, m_w_proj_pool, m_conv_w, m_w_proj_conv, m_w_out, m_ffn_norm, m_w_gate_up, m_w_down, m_final_norm, v_attn_norm, v_w_in, v_b_forget, v_b_gate, v_w_proj_attn, v_pool_w, v_pool_scale, v_w_proj_pool, v_conv_w, v_w_proj_conv, v_w_out, v_ffn_norm, v_w_gate_up, v_w_down, v_final_norm):
    weights = dict(attn_norm=attn_norm, w_in=w_in, b_forget=b_forget, b_gate=b_gate, w_proj_attn=w_proj_attn,
                   pool_w=pool_w, pool_scale=pool_scale, w_proj_pool=w_proj_pool, conv_w=conv_w,
                   w_proj_conv=w_proj_conv, w_out=w_out, ffn_norm=ffn_norm, w_gate_up=w_gate_up, w_down=w_down,
                   final_norm=final_norm)
    moments_m = dict(attn_norm=m_attn_norm, w_in=m_w_in, b_forget=m_b_forget, b_gate=m_b_gate,
                     w_proj_attn=m_w_proj_attn, pool_w=m_pool_w, pool_scale=m_pool_scale, w_proj_pool=m_w_proj_pool,
                     conv_w=m_conv_w, w_proj_conv=m_w_proj_conv, w_out=m_w_out, ffn_norm=m_ffn_norm,
                     w_gate_up=m_w_gate_up, w_down=m_w_down, final_norm=m_final_norm)
    moments_v = dict(attn_norm=v_attn_norm, w_in=v_w_in, b_forget=v_b_forget, b_gate=v_b_gate,
                     w_proj_attn=v_w_proj_attn, pool_w=v_pool_w, pool_scale=v_pool_scale, w_proj_pool=v_w_proj_pool,
                     conv_w=v_conv_w, w_proj_conv=v_w_proj_conv, w_out=v_w_out, ffn_norm=v_ffn_norm,
                     w_gate_up=v_w_gate_up, w_down=v_w_down, final_norm=v_final_norm)

    sh = _padded_shards(weights)
    names = list(SHARDED)
    rest = [n for n in names if n != "w_in"]
    me = 4 * lax.axis_index("x") + 2 * lax.axis_index("y") + lax.axis_index("c")
    w_in0, conv_all = _multi_gather([sh["w_in"], conv_w], [0, None], "gather_w_in_l0")
    started, after = {}, w_in0
    for l in range(DEPTH):
        for group, gnames in (("w_in", ["w_in"]), ("rest", rest)):
            if (l, group) != (0, "w_in"):
                started[l, group] = _split_start([sh[n] for n in gnames], [l] * len(gnames), False, after,
                                                 f"gather_start_{group}_l{l}")
                after = started[l, group][4]
    last_token = after

    def get_W(l, xt):
        if l == 0:
            w_in = w_in0
        else:
            w_in = _split_wait(started[l, "w_in"], [l], False, xt, f"gather_wait_w_in_l{l}")[0]
        W = _full_operands({"w_in": w_in}, l)

        def late(after):
            lands = _split_wait(started[l, "rest"], [l] * len(rest), False, after, f"gather_wait_rest_l{l}")
            return _full_operands(dict(zip(rest, lands)), l)

        W["late"] = late
        W.update(_replicated_operands(weights, l))
        W["conv_w"] = jnp.transpose(conv_all[:, l], (1, 0, 2)).reshape(CONV_K, BRANCH_W)
        if l == 0:
            W["attn_norm"] = W["attn_norm"] + last_token[0, 0]
        return W

    exchanges = []

    def stage(l, group, G, W):
        gnames = GRAD_GROUPS[group]
        slabs = [_grad_slabs(G, n, l) for n in gnames]
        started = _split_start(slabs, None, True, slabs[0][0, :8], f"exchange_start_{group}_l{l}")
        exchanges.append((l, group, gnames, slabs, started))
        tie = {"ffn": "ffn_norm", "mix": "conv_w", "w_in": "w_f"}[group]
        W = dict(W)
        W[tie] = W[tie] + started[4][0, 0].astype(W[tie].dtype)
        return W

    loss_part, grad_x, grads, g_final = _local_step(x, loss_target, get_W, attn_norm, final_norm, stage)

    def finish_exchange(ex, after):
        l, group, gnames, slabs, started = ex
        lands = _split_wait(started, None, True, after, f"exchange_wait_{group}_l{l}")
        grads[l].update(_sum_layer_grads(dict(zip(gnames, lands)), l))

    def zero_after(a):
        return jnp.minimum(jnp.abs(a[(0,) * a.ndim]), 0.0)

    *early, last_exchange = exchanges
    for ex in early:
        finish_exchange(ex, grad_x)
    deltas, new_m, new_v, gw = {}, {}, {}, {}
    views = {"w_in": ((2, 0, 1), (1, 2, 0), (49, DEPTH, D_MODEL), 1),
             "w_gate_up": ((0, 2, 1), (0, 2, 1), (1, GU_SHARD // 2, D_MODEL), 0)}

    def update_sharded(n):
        if n in views:
            perm, inv, block, layer_axis = views[n]
            gt = jnp.stack([grads[l][n] for l in range(DEPTH)], axis=layer_axis)
            d, nm, nv = _adamw_3d(jnp.transpose(weights[n], perm), gt, jnp.transpose(moments_m[n], perm),
                                  jnp.transpose(moments_v[n], perm), block, "adamw_" + n)
            deltas[n], new_m[n], new_v[n] = (jnp.transpose(a, inv) for a in (d, nm, nv))
            gw[n] = jnp.transpose(gt, inv)
            return
        gw[n] = jnp.stack([grads[l][n] for l in range(DEPTH)])
        shape = weights[n].shape
        d, nm, nv = _adamw(_as_2d(weights[n]), _as_2d(gw[n]), _as_2d(moments_m[n]), _as_2d(moments_v[n]),
                           "adamw_" + n)
        deltas[n], new_m[n], new_v[n] = d.reshape(shape), nm.reshape(shape), nv.reshape(shape)

    for n in SHARDED:
        if n != "w_in":
            update_sharded(n)

    small = ("attn_norm", "b_forget", "b_gate", "pool_w", "pool_scale", "ffn_norm", "conv_w")
    loss_part = loss_part + zero_after(deltas["w_gate_up"]) + zero_after(deltas["w_down"])
    parts = [jnp.stack([grads[l][n] for l in range(DEPTH)]) for n in small] + [g_final, loss_part]
    gathered = _multi_gather(parts, [None] * len(parts), "gather_small_grads")
    summed = _sum_small(gathered, "sum_small_grads")
    for n, s in zip(small, summed):
        gw[n] = s
    gw["attn_norm"], gw["ffn_norm"] = gw["attn_norm"][:, 0], gw["ffn_norm"][:, 0]
    gw["b_forget"] = gw["b_forget"][:, 0, :N_HEADS]
    gw["b_gate"], gw["pool_scale"] = gw["b_gate"][:, 0], gw["pool_scale"][:, 0]
    gw["conv_w"] = lax.dynamic_slice_in_dim(gw["conv_w"], me * (BRANCH_W // N_DEV), BRANCH_W // N_DEV, axis=2)
    gw["final_norm"] = summed[-2][0]
    loss = summed[-1][0, 0]

    rest_names = [n for n in WEIGHT_ORDER if n not in SHARDED]
    ds, nms, nvs = _adamw_many(*[[_as_2d(src[n]) for n in rest_names] for src in (weights, gw, moments_m, moments_v)],
                               "adamw_small")
    for n, d, nm, nv in zip(rest_names, ds, nms, nvs):
        shape = weights[n].shape
        deltas[n], new_m[n], new_v[n] = d.reshape(shape), nm.reshape(shape), nv.reshape(shape)

    finish_exchange(last_exchange, deltas["pool_w"])
    update_sharded("w_in")

    return (loss, grad_x, *[gw[n] for n in WEIGHT_ORDER], *[deltas[n] for n in WEIGHT_ORDER],
            *[new_m[n] for n in WEIGHT_ORDER], *[new_v[n] for n in WEIGHT_ORDER])
```

```python
import functools

import jax
import jax.numpy as jnp
from jax import lax
from jax.experimental import pallas as pl
from jax.experimental.pallas import tpu as pltpu

F32 = jnp.float32
BF16 = jnp.bfloat16

N_DEV = 8
D_MODEL = 1024
DEPTH = 2
N_HEADS = 8
HEAD_DIM = 64
BRANCH_W = 512
POOL_WINDOWS = (2, 4, 8, 16)
POOL_GD = 128
CONV_K = 3
FFN_HIDDEN = 2816
GATE_W = 3 * D_MODEL
IN_COLS = 6664
MAIN_COLS = GATE_W + 7 * BRANCH_W
RMS_EPS = 1e-6
NEG_INF = -1e30

ADAM_LR = 0.001
ADAM_B1 = 0.9
ADAM_B2 = 0.999
ADAM_EPS = 1e-08
ADAM_WD = 0.01
ADAM_STEP = 10

LANES = 128
VMEM_LIMIT = 56 * 1024 * 1024
CUM_BLK = 256

TRIPLE = 3 * LANES
OFF_G, OFF_QKV, OFF_CONV, OFF_U = 0, 3072, 4608, 6144


def _cp(sem=None):
    return pltpu.CompilerParams(dimension_semantics=sem, vmem_limit_bytes=VMEM_LIMIT)


def _sigmoid(z):
    return 1.0 / (1.0 + jnp.exp(-z))


def _matmul(a, b, *, mode, out_dtype, name, tm=2048, tn=512, tk=None, residual=None, rms_g=None, side=None,
            extra=None):
    if mode == "nn":
        (M, K), N = a.shape, b.shape[1]
    elif mode == "nt":
        (M, K), N = a.shape, b.shape[0]
    else:
        (K, M), N = a.shape, b.shape[1]
    tm, tn, tk = min(tm, M), min(tn, N), K if tk is None else min(tk, K)
    assert M % tm == 0 and N % tn == 0 and K % tk == 0, (name, M, N, K, tm, tn, tk)
    nk = K // tk
    if mode == "nn":
        a_spec = pl.BlockSpec((tm, tk), lambda i, j, k: (i, k))
        b_spec = pl.BlockSpec((tk, tn), lambda i, j, k: (k, j))
        dims = (((1,), (0,)), ((), ()))
    elif mode == "nt":
        a_spec = pl.BlockSpec((tm, tk), lambda i, j, k: (i, k))
        b_spec = pl.BlockSpec((tn, tk), lambda i, j, k: (j, k))
        dims = (((1,), (1,)), ((), ()))
    else:
        a_spec = pl.BlockSpec((tk, tm), lambda i, j, k: (k, i))
        b_spec = pl.BlockSpec((tk, tn), lambda i, j, k: (k, j))
        dims = (((0,), (0,)), ((), ()))
    o_spec = pl.BlockSpec((tm, tn), lambda i, j, k: (i, j))
    has_res, has_norm, has_side, has_extra = (v is not None for v in (residual, rms_g, side, extra))
    assert not has_norm or tn == N, (name, tn, N)
    assert not has_side or (nk == 1 and mode != "nt"), name

    in_specs, args = [a_spec, b_spec], [a, b]
    out_specs, out_shape = [o_spec], [jax.ShapeDtypeStruct((M, N), out_dtype)]
    if has_res:
        in_specs.append(o_spec)
        args.append(residual)
    if has_norm:
        in_specs.append(pl.BlockSpec((1, N), lambda i, j, k: (0, 0)))
        args.append(rms_g.reshape(1, N))
        out_specs.append(o_spec)
        out_shape.append(jax.ShapeDtypeStruct((M, N), BF16))
    if has_side:
        b_side, side_dtype = side
        ns = b_side.shape[1]
        in_specs.append(pl.BlockSpec((K, ns), lambda i, j, k: (0, 0)))
        args.append(b_side)
        out_specs.append(pl.BlockSpec((tm, ns), lambda i, j, k: (i, 0)))
        out_shape.append(jax.ShapeDtypeStruct((M, ns), side_dtype))
    if has_extra:
        a2, b2 = extra
        in_specs += [pl.BlockSpec((tm, a2.shape[1]), lambda i, j, k: (i, 0)),
                     pl.BlockSpec((tn, b2.shape[1]), lambda i, j, k: (j, 0))]
        args += [a2, b2]
    n_in = len(args)

    def body(*refs):
        ins, outs = list(refs[2:n_in]), list(refs[n_in:n_in + len(out_shape)])
        a_ref, b_ref = refs[:2]
        r_ref = ins.pop(0) if has_res else None
        g_ref = ins.pop(0) if has_norm else None
        bs_ref = ins.pop(0) if has_side else None
        a2_ref, b2_ref = (ins.pop(0), ins.pop(0)) if has_extra else (None, None)
        o_ref = outs.pop(0)
        h_ref = outs.pop(0) if has_norm else None
        so_ref = outs.pop(0) if has_side else None

        def finish(acc):
            if has_res:
                acc = acc + r_ref[...].astype(F32)
            if has_extra:
                acc = acc + lax.dot_general(a2_ref[...], b2_ref[...], (((1,), (1,)), ((), ())),
                                            preferred_element_type=F32)
            o_ref[...] = acc.astype(out_dtype)
            if has_norm:
                r = lax.rsqrt(jnp.mean(acc * acc, axis=-1, keepdims=True) + RMS_EPS)
                h_ref[...] = ((acc * r) * g_ref[...]).astype(BF16)

        if has_side:
            @pl.when(pl.program_id(1) == 0)
            def _():
                side_dims = (((1,), (0,)), ((), ())) if mode == "nn" else dims
                so_ref[...] = lax.dot_general(a_ref[...], bs_ref[...], side_dims,
                                              preferred_element_type=F32).astype(so_ref.dtype)

        prod = lax.dot_general(a_ref[...], b_ref[...], dims, preferred_element_type=F32)
        if nk == 1:
            finish(prod)
            return
        acc_ref = refs[-1]
        k = pl.program_id(2)

        @pl.when(k == 0)
        def _():
            acc_ref[...] = prod

        @pl.when(jnp.logical_and(k > 0, k < nk - 1))
        def _():
            acc_ref[...] += prod

        @pl.when(k == nk - 1)
        def _():
            finish(acc_ref[...] + prod)

    single = len(out_shape) == 1
    return pl.pallas_call(
        body, name=name, grid=(M // tm, N // tn, nk), in_specs=in_specs,
        out_specs=out_specs[0] if single else out_specs, out_shape=out_shape[0] if single else out_shape,
        scratch_shapes=[pltpu.VMEM((tm, tn), F32)] if nk > 1 else [],
        compiler_params=_cp(("parallel", "arbitrary" if has_side else "parallel", "arbitrary")),
    )(*args)


def _rms_fwd(x, g, name):
    T, Dm = x.shape
    tm = min(512, T)

    def body(x_ref, g_ref, h_ref):
        xf = x_ref[...]
        r = lax.rsqrt(jnp.mean(xf * xf, axis=-1, keepdims=True) + RMS_EPS)
        h_ref[...] = ((xf * r) * g_ref[...]).astype(BF16)

    return pl.pallas_call(
        body, name=name, grid=(T // tm,),
        in_specs=[pl.BlockSpec((tm, Dm), lambda i: (i, 0)), pl.BlockSpec((1, Dm), lambda i: (0, 0))],
        out_specs=pl.BlockSpec((tm, Dm), lambda i: (i, 0)),
        out_shape=jax.ShapeDtypeStruct((T, Dm), BF16),
        compiler_params=_cp(("parallel",)),
    )(x, g.reshape(1, Dm))


def _rms_bwd(x, g, dh, dres, name):
    T, Dm = x.shape
    tm = min(512, T)

    def body(x_ref, g_ref, dh_ref, dres_ref, dx_ref, dxb_ref, dg_ref):
        i = pl.program_id(0)
        xf = x_ref[...]
        r = lax.rsqrt(jnp.mean(xf * xf, axis=-1, keepdims=True) + RMS_EPS)
        xn = xf * r
        dhf = dh_ref[...].astype(F32)
        dxn = dhf * g_ref[...]
        c = jnp.mean(dxn * xn, axis=-1, keepdims=True)
        dx = dres_ref[...] + r * (dxn - xn * c)
        dx_ref[...] = dx
        dxb_ref[...] = dx.astype(BF16)
        part = jnp.sum(dhf * xn, axis=0, keepdims=True)

        @pl.when(i == 0)
        def _():
            dg_ref[...] = part

        @pl.when(i > 0)
        def _():
            dg_ref[...] += part

    row = pl.BlockSpec((tm, Dm), lambda i: (i, 0))
    vec = pl.BlockSpec((1, Dm), lambda i: (0, 0))
    return pl.pallas_call(
        body, name=name, grid=(T // tm,), in_specs=[row, vec, row, row], out_specs=[row, row, vec],
        out_shape=[jax.ShapeDtypeStruct((T, Dm), F32), jax.ShapeDtypeStruct((T, Dm), BF16),
                   jax.ShapeDtypeStruct((1, Dm), F32)],
        compiler_params=_cp(("arbitrary",)),
    )(x, g.reshape(1, Dm), dh, dres)


def _loss_head(x, g, target, name):
    T, Dm = x.shape
    tm = min(512, T)

    def body(x_ref, g_ref, t_ref, loss_ref, dx_ref, dxb_ref, dg_ref):
        i = pl.program_id(0)
        xf = x_ref[...]
        gv = g_ref[...]
        r = lax.rsqrt(jnp.mean(xf * xf, axis=-1, keepdims=True) + RMS_EPS)
        xn = xf * r
        diff = xn * gv - t_ref[...]
        per_tok = jnp.mean(diff * diff, axis=-1, keepdims=True)
        lpart = 0.5 * jnp.sum(per_tok, axis=0, keepdims=True) + jnp.zeros((1, LANES), F32)
        dy = diff * (1.0 / Dm)
        dxn = dy * gv
        c = jnp.mean(dxn * xn, axis=-1, keepdims=True)
        dx = r * (dxn - xn * c)
        dx_ref[...] = dx
        dxb_ref[...] = dx.astype(BF16)
        part = jnp.sum(dy * xn, axis=0, keepdims=True)

        @pl.when(i == 0)
        def _():
            dg_ref[...] = part
            loss_ref[...] = lpart

        @pl.when(i > 0)
        def _():
            dg_ref[...] += part
            loss_ref[...] += lpart

    row = pl.BlockSpec((tm, Dm), lambda i: (i, 0))
    vec = pl.BlockSpec((1, Dm), lambda i: (0, 0))
    lsp = pl.BlockSpec((1, LANES), lambda i: (0, 0))
    return pl.pallas_call(
        body, name=name, grid=(T // tm,), in_specs=[row, vec, row], out_specs=[lsp, row, row, vec],
        out_shape=[jax.ShapeDtypeStruct((1, LANES), F32), jax.ShapeDtypeStruct((T, Dm), F32),
                   jax.ShapeDtypeStruct((T, Dm), BF16), jax.ShapeDtypeStruct((1, Dm), F32)],
        compiler_params=_cp(("arbitrary",)),
    )(x, g.reshape(1, Dm), target)


def _split_bf16(v):
    hi = v.astype(BF16)
    r1 = v - hi.astype(F32)
    mid = r1.astype(BF16)
    lo = (r1 - mid.astype(F32)).astype(BF16)
    return hi, mid, lo


def _tri_dot(tri, v):
    hi, mid, lo = _split_bf16(v)
    dot = functools.partial(jnp.dot, preferred_element_type=F32)
    return dot(tri, hi) + dot(tri, mid) + dot(tri, lo)


def _log_sigmoid(z):
    return jnp.minimum(z, 0.0) - jnp.log(1.0 + jnp.exp(-jnp.abs(z)))


def _fox_cumsum_bwd(f, bf, dF, n_seq, name):
    T = f.shape[0]
    S = T // n_seq
    c = min(CUM_BLK, S)

    def body(f_ref, b_ref, dF_ref, df_ref, db_ref):
        b = pl.program_id(0)
        ri = lax.broadcasted_iota(jnp.int32, (c, c), 0)
        ci = lax.broadcasted_iota(jnp.int32, (c, c), 1)
        tri = (ri <= ci).astype(BF16)
        carry = jnp.zeros((1, LANES), F32)
        dbp = jnp.zeros((1, LANES), F32)
        for j in reversed(range(S // c)):
            dFc = dF_ref[j * c:(j + 1) * c, :]
            dlf = _tri_dot(tri, dFc) + carry
            carry = carry + jnp.sum(dFc, axis=0, keepdims=True)
            z = f_ref[j * c:(j + 1) * c, :] + b_ref[...]
            dz = dlf * _sigmoid(-z)
            df_ref[j * c:(j + 1) * c, :] = dz.astype(BF16)
            dbp = dbp + jnp.sum(dz, axis=0, keepdims=True)

        @pl.when(b == 0)
        def _():
            db_ref[...] = dbp

        @pl.when(b > 0)
        def _():
            db_ref[...] += dbp

    blk = pl.BlockSpec((S, LANES), lambda b: (b, 0))
    vec = pl.BlockSpec((1, LANES), lambda b: (0, 0))
    return pl.pallas_call(
        body, name=name, grid=(n_seq,), in_specs=[blk, vec, blk], out_specs=[blk, vec],
        out_shape=[jax.ShapeDtypeStruct((T, LANES), BF16), jax.ShapeDtypeStruct((1, LANES), F32)],
        compiler_params=_cp(("arbitrary",)),
    )(f, bf, dF)


def _pair_masks():
    lane = lax.broadcasted_iota(jnp.int32, (1, LANES), 1)
    lo = lane < HEAD_DIM
    return lo, jnp.logical_not(lo)


AUG0 = HEAD_DIM
Q_TILE, K_CHUNK, ROW_GROUP = 1024, 256, 64


def _fox_prep(f, bf, proj, n_seq, name):
    T = f.shape[0]
    S = T // n_seq
    c = min(CUM_BLK, S)

    def body(f_ref, b_ref, qkv_ref, qa_ref, ka_ref, va_ref):
        ri = lax.broadcasted_iota(jnp.int32, (c, c), 0)
        ci = lax.broadcasted_iota(jnp.int32, (c, c), 1)
        tri = (ri >= ci).astype(BF16)
        lane = lax.broadcasted_iota(jnp.int32, (c, LANES), 1)
        carry = jnp.zeros((1, LANES), F32)
        for j in range(S // c):
            rows = slice(j * c, (j + 1) * c)
            lf = _log_sigmoid(f_ref[rows, :] + b_ref[...])
            Fc = _tri_dot(tri, lf) + carry
            carry = carry + jnp.sum(lf, axis=0, keepdims=True)
            for h in range(N_HEADS):
                col = jnp.sum(jnp.where(lane == h, Fc, 0.0), axis=-1, keepdims=True)
                hi = col.astype(BF16).astype(F32)
                r1 = col - hi
                mid = r1.astype(BF16).astype(F32)
                lo = r1 - mid
                ones_q = jnp.logical_and(lane >= AUG0 + 3, lane < AUG0 + 6)
                ones_k = jnp.logical_and(lane >= AUG0, lane < AUG0 + 3)
                aug_q = jnp.where(lane == AUG0, hi, jnp.where(lane == AUG0 + 1, mid, jnp.where(
                    lane == AUG0 + 2, lo, jnp.where(ones_q, 1.0, 0.0))))
                aug_k = jnp.where(lane == AUG0 + 3, -hi, jnp.where(lane == AUG0 + 4, -mid, jnp.where(
                    lane == AUG0 + 5, -lo, jnp.where(ones_k, 1.0, 0.0))))
                base = (h // 2) * TRIPLE
                qp, kp, vp = (qkv_ref[rows, base + t * LANES:base + (t + 1) * LANES].astype(F32) for t in range(3))
                if h % 2:
                    qp, kp, vp = (pltpu.roll(a, HEAD_DIM, 1) for a in (qp, kp, vp))
                out = slice(h * LANES, (h + 1) * LANES)
                qa_ref[rows, out] = jnp.where(lane < HEAD_DIM, qp * (HEAD_DIM ** -0.5), aug_q).astype(BF16)
                ka_ref[rows, out] = jnp.where(lane < HEAD_DIM, kp, aug_k).astype(BF16)
                va_ref[rows, out] = jnp.where(lane < HEAD_DIM, vp, jnp.where(lane == AUG0, 1.0, 0.0)).astype(BF16)

    fblk = pl.BlockSpec((S, LANES), lambda b: (b, 0))
    out = pl.BlockSpec((S, N_HEADS * LANES), lambda b: (b, 0))
    sh = jax.ShapeDtypeStruct((T, N_HEADS * LANES), BF16)
    return pl.pallas_call(
        body, name=name, grid=(n_seq,),
        in_specs=[fblk, pl.BlockSpec((1, LANES), lambda b: (0, 0)),
                  pl.BlockSpec((S, 4 * TRIPLE), lambda b: (b, OFF_QKV // (4 * TRIPLE)))],
        out_specs=[out, out, out], out_shape=[sh, sh, sh],
        compiler_params=_cp(("parallel",)),
    )(f, bf, proj)


def _band_mask(q0, k0, nq, nk):
    row = q0 + lax.broadcasted_iota(jnp.int32, (nq, nk), 0)
    col = k0 + lax.broadcasted_iota(jnp.int32, (nq, nk), 1)
    return col <= row


_NT = (((1,), (1,)), ((), ()))
_TN = (((0,), (0,)), ((), ()))


def _attn_fwd2(qa, ka, va, n_seq, name):
    T = qa.shape[0]
    S = T // n_seq
    tq, tk, rg = min(Q_TILE, S), min(K_CHUNK, S), ROW_GROUP
    nq, per = S // tq, tq // tk

    def body(q_ref, k_ref, v_ref, o_ref, o32_ref, lse_ref, phi_s, plo_s, mp_s, m_s, acc_s):
        qi = pl.program_id(2)
        mp_s[...] = jnp.full_like(mp_s, NEG_INF)
        acc_s[...] = jnp.zeros_like(acc_s)

        def scores(kc, hh, r0):
            k0 = pl.multiple_of(kc * tk, tk)
            hl = slice(hh * LANES, (hh + 1) * LANES)
            return k0, lax.dot_general(q_ref[r0:, hl], k_ref[pl.ds(k0, tk), hl], _NT, preferred_element_type=F32)

        def max_chunk(kc, masked, r0):
            for hh in range(2):
                k0, s_all = scores(kc, hh, r0)
                for r in range(r0 // rg, tq // rg):
                    rows = slice(r * rg, (r + 1) * rg)
                    s = s_all[r * rg - r0:(r + 1) * rg - r0, :]
                    if masked:
                        s = jnp.where(_band_mask(qi * tq + r * rg, k0, rg, tk), s, NEG_INF)
                    part = s[:, :LANES]
                    for c in range(1, tk // LANES):
                        part = jnp.maximum(part, s[:, c * LANES:(c + 1) * LANES])
                    mp_s[hh, rows, :] = jnp.maximum(mp_s[hh, rows, :], part)

        def sum_chunk(kc, masked, r0):
            for hh in range(2):
                k0, s_all = scores(kc, hh, r0)
                hl = slice(hh * LANES, (hh + 1) * LANES)
                v = v_ref[pl.ds(k0, tk), hl]
                for r in range(r0 // rg, tq // rg):
                    rows = slice(r * rg, (r + 1) * rg)
                    p = jnp.exp(s_all[r * rg - r0:(r + 1) * rg - r0, :] - m_s[hh, rows])
                    if masked:
                        p = jnp.where(_band_mask(qi * tq + r * rg, k0, rg, tk), p, 0.0)
                    p_hi = p.astype(BF16)
                    phi_s[hh, rows, :] = p_hi
                    plo_s[hh, rows, :] = (p - p_hi.astype(F32)).astype(BF16)
                acc_s[hh, r0:, :] += (jnp.dot(phi_s[hh, r0:, :], v, preferred_element_type=F32)
                                      + jnp.dot(plo_s[hh, r0:, :], v, preferred_element_type=F32))

        def sweep(chunk):
            def unmasked(kc, carry):
                chunk(kc, False, 0)
                return carry

            lax.fori_loop(0, qi * per, unmasked, 0)
            for d in range(per):
                chunk(qi * per + d, True, d * tk)

        sweep(max_chunk)
        m_s[...] = jnp.max(mp_s[...], axis=-1, keepdims=True)
        sweep(sum_chunk)

        lane = lax.broadcasted_iota(jnp.int32, (1, LANES), 1)
        outs = []
        for hh in range(2):
            acc = acc_s[hh]
            l = jnp.sum(jnp.where(lane == AUG0, acc, 0.0), axis=-1, keepdims=True)
            lse_ref[hh] = m_s[hh] + jnp.log(l)
            outs.append(acc / l)
        o = jnp.where(lane < HEAD_DIM, outs[0], pltpu.roll(outs[1], HEAD_DIM, 1))
        o_ref[...] = o.astype(BF16)
        o32_ref[...] = o

    qmap = lambda b, j, qi: (b * nq + qi, j)
    omap = lambda b, j, qi: (b * nq + qi, j)
    kv = pl.BlockSpec((S, 2 * LANES), lambda b, j, qi: (b, j))
    return pl.pallas_call(
        body, name=name, grid=(n_seq, N_HEADS // 2, nq),
        in_specs=[pl.BlockSpec((tq, 2 * LANES), qmap), kv, kv],
        out_specs=[pl.BlockSpec((tq, LANES), omap), pl.BlockSpec((tq, LANES), omap),
                   pl.BlockSpec((2, tq, 1), lambda b, j, qi: (j, b * nq + qi, 0))],
        out_shape=[jax.ShapeDtypeStruct((T, BRANCH_W), BF16), jax.ShapeDtypeStruct((T, BRANCH_W), F32),
                   jax.ShapeDtypeStruct((N_HEADS, T, 1), F32)],
        scratch_shapes=[pltpu.VMEM((2, tq, tk), BF16), pltpu.VMEM((2, tq, tk), BF16),
                        pltpu.VMEM((2, tq, LANES), F32), pltpu.VMEM((2, tq, 1), F32),
                        pltpu.VMEM((2, tq, LANES), F32)],
        compiler_params=_cp(("parallel", "parallel", "parallel")),
    )(qa, ka, va)


def _attn_bwd(qa, ka, proj, do, o32, lse, dproj, n_seq, name):
    T = qa.shape[0]
    S = T // n_seq
    tq, tk, rg = min(Q_TILE, S), min(K_CHUNK, S), ROW_GROUP
    nq, per, nkc = S // tq, tq // tk, S // tk

    def body(q_ref, k_ref, v_ref, do_ref, o_ref, lse_ref, _, dqkv_ref, dfk_ref,
             p_s, ds_s, dq_s, dk_s, dv_s, df_s):
        dk_s[...] = jnp.zeros_like(dk_s)
        dv_s[...] = jnp.zeros_like(dv_s)
        df_s[...] = jnp.zeros_like(df_s)
        sels = _pair_masks()

        for qi in range(nq):
            q0 = qi * tq
            do_t = do_ref[q0:q0 + tq, :]
            dq_s[...] = jnp.zeros_like(dq_s)
            prod = do_t.astype(F32) * o_ref[q0:q0 + tq, :]
            dls = [jnp.sum(jnp.where(sel, prod, 0.0), axis=-1, keepdims=True) for sel in sels]

            def chunk(kc, masked, r0, q0=q0, do_t=do_t, dls=dls):
                k0 = pl.multiple_of(kc * tk, tk)
                v = v_ref[pl.ds(k0, tk), :]
                do_a = do_t[r0:, :]
                for hh in range(2):
                    hl = slice(hh * LANES, (hh + 1) * LANES)
                    qh, kh = q_ref[q0 + r0:q0 + tq, hl], k_ref[pl.ds(k0, tk), hl]
                    s_all = lax.dot_general(qh, kh, _NT, preferred_element_type=F32)
                    dom = jnp.where(sels[hh], do_a, jnp.zeros_like(do_a))
                    dp_all = lax.dot_general(dom, v, _NT, preferred_element_type=F32)
                    dfp = jnp.zeros((1, tk), F32)
                    for r in range(r0 // rg, tq // rg):
                        rows = slice(r * rg, (r + 1) * rg)
                        arows = slice(r * rg - r0, (r + 1) * rg - r0)
                        qrows = slice(q0 + r * rg, q0 + (r + 1) * rg)
                        p = jnp.exp(s_all[arows, :] - lse_ref[hh, qrows])
                        if masked:
                            p = jnp.where(_band_mask(q0 + r * rg, k0, rg, tk), p, 0.0)
                        ds = p * (dp_all[arows, :] - dls[hh][rows])
                        p_s[hh, rows, :] = p.astype(BF16)
                        ds_s[hh, rows, :] = ds.astype(BF16)
                        dfp = dfp + jnp.sum(ds, axis=0, keepdims=True)
                    df_s[hh, kc] -= dfp
                    dq_s[hh, r0:, :] += jnp.dot(ds_s[hh, r0:, :], kh, preferred_element_type=F32)
                    dv_s[hh, pl.ds(k0, tk), :] += lax.dot_general(p_s[hh, r0:, :], do_a, _TN,
                                                                  preferred_element_type=F32)
                    dk_s[hh, pl.ds(k0, tk), :] += lax.dot_general(ds_s[hh, r0:, :], qh, _TN,
                                                                  preferred_element_type=F32)

            def unmasked(kc, carry, chunk=chunk):
                chunk(kc, False, 0)
                return carry

            lax.fori_loop(0, qi * per, unmasked, 0)
            for d in range(per):
                chunk(qi * per + d, True, d * tk)
            dq = jnp.where(sels[0], dq_s[0], pltpu.roll(dq_s[1], HEAD_DIM, 1))
            dqkv_ref[q0:q0 + tq, :LANES] = (dq * (HEAD_DIM ** -0.5)).astype(BF16)

        dqkv_ref[:, LANES:2 * LANES] = jnp.where(sels[0], dk_s[0], pltpu.roll(dk_s[1], HEAD_DIM, 1)).astype(BF16)
        dqkv_ref[:, 2 * LANES:] = jnp.where(sels[0], dv_s[0], dv_s[1]).astype(BF16)
        for c in range(nkc):
            dfk_ref[:, :, c * tk:(c + 1) * tk] = df_s[:, c]

    seq = lambda w: pl.BlockSpec((S, w), lambda b, j: (b, j))
    col1 = pl.BlockSpec((2, S, 1), lambda b, j: (j, b, 0))
    vblk = pl.BlockSpec((S, LANES), lambda b, j: (b, OFF_QKV // LANES + 3 * j + 2))
    return pl.pallas_call(
        body, name=name, grid=(n_seq, N_HEADS // 2),
        in_specs=[seq(2 * LANES), seq(2 * LANES), vblk, seq(LANES), seq(LANES), col1,
                  pl.BlockSpec(memory_space=pl.ANY)],
        out_specs=[pl.BlockSpec((S, TRIPLE), lambda b, j: (b, OFF_QKV // TRIPLE + j)),
                   pl.BlockSpec((2, 1, S), lambda b, j: (j, 0, b))],
        out_shape=[jax.ShapeDtypeStruct(dproj.shape, BF16), jax.ShapeDtypeStruct((N_HEADS, 1, T), F32)],
        input_output_aliases={6: 0},
        scratch_shapes=[pltpu.VMEM((2, tq, tk), BF16), pltpu.VMEM((2, tq, tk), BF16),
                        pltpu.VMEM((2, tq, LANES), F32), pltpu.VMEM((2, S, LANES), F32),
                        pltpu.VMEM((2, S, LANES), F32), pltpu.VMEM((2, nkc, 1, tk), F32)],
        compiler_params=_cp(("parallel", "parallel")),
    )(qa, ka, proj, do, o32, lse, dproj)


def _shift_down(v, k, row):
    return jnp.where(row >= k, pltpu.roll(v, k, 0), 0.0)


def _shift_up(v, k, row, S):
    return jnp.where(row < S - k, pltpu.roll(v, S - k, 0), 0.0)


def _pool_diff(uf, w, row):
    acc, k = uf, 1
    while k < w:
        acc = acc + _shift_down(acc, k, row)
        k *= 2
    n = jnp.minimum(row + 1, w).astype(F32)
    return acc / n - uf


def _pool_fwd(proj, pool_w, pool_scale, n_seq, name):
    T = proj.shape[0]
    S = T // n_seq

    def body(u_ref, w_ref, sc_ref, o_ref, d_s):
        g = pl.program_id(1)
        row = lax.broadcasted_iota(jnp.int32, (S, POOL_GD), 0)
        uf = u_ref[...].astype(F32)
        for gi, wlen in enumerate(POOL_WINDOWS):
            @pl.when(g == gi)
            def _(wlen=wlen):
                d_s[...] = _pool_diff(uf, wlen, row).astype(BF16)
        e = jnp.dot(d_s[...], w_ref[0], preferred_element_type=F32)
        o_ref[...] = (e * sc_ref[...]).astype(BF16)

    uc = OFF_U // POOL_GD
    return pl.pallas_call(
        body, name=name, grid=(n_seq, len(POOL_WINDOWS)),
        in_specs=[pl.BlockSpec((S, POOL_GD), lambda b, g: (b, uc + g)),
                  pl.BlockSpec((1, POOL_GD, POOL_GD), lambda b, g: (g, 0, 0)),
                  pl.BlockSpec((1, POOL_GD), lambda b, g: (0, g))],
        out_specs=pl.BlockSpec((S, POOL_GD), lambda b, g: (b, g)),
        out_shape=jax.ShapeDtypeStruct((T, BRANCH_W), BF16),
        scratch_shapes=[pltpu.VMEM((S, POOL_GD), BF16)],
        compiler_params=_cp(("parallel", "parallel")),
    )(proj, pool_w, pool_scale)


def _pool_bwd(proj, dout, pool_w, pool_scale, dproj, n_seq, name):
    T = proj.shape[0]
    S = T // n_seq

    def body(u_ref, do_ref, w_ref, sc_ref, _, du_ref, dw_ref, dsc_ref, d_s):
        g, b = pl.program_id(0), pl.program_id(1)
        row = lax.broadcasted_iota(jnp.int32, (S, POOL_GD), 0)
        uf = u_ref[...].astype(F32)
        for gi, wlen in enumerate(POOL_WINDOWS):
            @pl.when(g == gi)
            def _(wlen=wlen):
                d_s[...] = _pool_diff(uf, wlen, row).astype(BF16)
        db16 = d_s[...]
        w = w_ref[0]
        e = jnp.dot(db16, w, preferred_element_type=F32)
        dof = do_ref[...].astype(F32)
        dsc = jnp.sum(dof * e, axis=0, keepdims=True)
        de = (dof * sc_ref[...]).astype(BF16)
        dd = lax.dot_general(de, w, (((1,), (1,)), ((), ())), preferred_element_type=F32)
        dw = lax.dot_general(db16, de, (((0,), (0,)), ((), ())), preferred_element_type=F32)
        for gi, wlen in enumerate(POOL_WINDOWS):
            @pl.when(g == gi)
            def _(wlen=wlen):
                n = jnp.minimum(row + 1, wlen).astype(F32)
                acc, k = dd / n, 1
                while k < wlen:
                    acc = acc + _shift_up(acc, k, row, S)
                    k *= 2
                du_ref[...] = (acc - dd).astype(BF16)

        @pl.when(b == 0)
        def _():
            dw_ref[0] = dw
            dsc_ref[...] = dsc

        @pl.when(b > 0)
        def _():
            dw_ref[0] += dw
            dsc_ref[...] += dsc

    uc = OFF_U // POOL_GD
    return pl.pallas_call(
        body, name=name, grid=(len(POOL_WINDOWS), n_seq),
        in_specs=[pl.BlockSpec((S, POOL_GD), lambda g, b: (b, uc + g)),
                  pl.BlockSpec((S, POOL_GD), lambda g, b: (b, g)),
                  pl.BlockSpec((1, POOL_GD, POOL_GD), lambda g, b: (g, 0, 0)),
                  pl.BlockSpec((1, POOL_GD), lambda g, b: (0, g)),
                  pl.BlockSpec(memory_space=pl.ANY)],
        out_specs=[pl.BlockSpec((S, POOL_GD), lambda g, b: (b, uc + g)),
                   pl.BlockSpec((1, POOL_GD, POOL_GD), lambda g, b: (g, 0, 0)),
                   pl.BlockSpec((1, POOL_GD), lambda g, b: (0, g))],
        out_shape=[jax.ShapeDtypeStruct(dproj.shape, BF16),
                   jax.ShapeDtypeStruct((len(POOL_WINDOWS), POOL_GD, POOL_GD), F32),
                   jax.ShapeDtypeStruct((1, BRANCH_W), F32)],
        input_output_aliases={4: 0},
        scratch_shapes=[pltpu.VMEM((S, POOL_GD), BF16)],
        compiler_params=_cp(("parallel", "arbitrary")),
    )(proj, dout, pool_w, pool_scale, dproj)


def _conv_fwd(proj, conv_w, n_seq, name):
    T = proj.shape[0]
    S = T // n_seq
    nc = BRANCH_W // LANES

    def body(c_ref, w_ref, o_ref):
        row = lax.broadcasted_iota(jnp.int32, (S, LANES), 0)
        cv, cb, cc = (c_ref[:, t * LANES:(t + 1) * LANES].astype(F32) for t in range(3))
        z = cc * cv
        w = w_ref[...]
        y = w[0:1] * _shift_down(z, 2, row) + w[1:2] * _shift_down(z, 1, row) + w[2:3] * z
        o_ref[...] = (cb * y).astype(BF16)

    return pl.pallas_call(
        body, name=name, grid=(n_seq, nc),
        in_specs=[pl.BlockSpec((S, TRIPLE), lambda b, j: (b, OFF_CONV // TRIPLE + j)),
                  pl.BlockSpec((CONV_K, LANES), lambda b, j: (0, j))],
        out_specs=pl.BlockSpec((S, LANES), lambda b, j: (b, j)),
        out_shape=jax.ShapeDtypeStruct((T, BRANCH_W), BF16),
        compiler_params=_cp(("parallel", "parallel")),
    )(proj, conv_w)


def _conv_bwd(proj, dout, conv_w, dproj, n_seq, name):
    T = proj.shape[0]
    S = T // n_seq
    nc = BRANCH_W // LANES

    def body(c_ref, do_ref, w_ref, _, dc_ref, dw_ref):
        b = pl.program_id(1)
        row = lax.broadcasted_iota(jnp.int32, (S, LANES), 0)
        cv, cb, cc = (c_ref[:, t * LANES:(t + 1) * LANES].astype(F32) for t in range(3))
        dof = do_ref[...].astype(F32)
        w = w_ref[...]
        z = cc * cv
        z1, z2 = _shift_down(z, 1, row), _shift_down(z, 2, row)
        y = w[0:1] * z2 + w[1:2] * z1 + w[2:3] * z
        dy = dof * cb
        dz = w[2:3] * dy + w[1:2] * _shift_up(dy, 1, row, S) + w[0:1] * _shift_up(dy, 2, row, S)
        dc_ref[:, :LANES] = (dz * cc).astype(BF16)
        dc_ref[:, LANES:2 * LANES] = (dof * y).astype(BF16)
        dc_ref[:, 2 * LANES:] = (dz * cv).astype(BF16)
        dws = [jnp.sum(dy * zk, axis=0, keepdims=True) for zk in (z2, z1, z)]

        @pl.when(b == 0)
        def _():
            for kk in range(CONV_K):
                dw_ref[kk:kk + 1, :] = dws[kk]

        @pl.when(b > 0)
        def _():
            for kk in range(CONV_K):
                dw_ref[kk:kk + 1, :] += dws[kk]

    triple = pl.BlockSpec((S, TRIPLE), lambda j, b: (b, OFF_CONV // TRIPLE + j))
    wsp = pl.BlockSpec((CONV_K, LANES), lambda j, b: (0, j))
    return pl.pallas_call(
        body, name=name, grid=(nc, n_seq),
        in_specs=[triple, pl.BlockSpec((S, LANES), lambda j, b: (b, j)), wsp, pl.BlockSpec(memory_space=pl.ANY)],
        out_specs=[triple, wsp],
        out_shape=[jax.ShapeDtypeStruct(dproj.shape, BF16), jax.ShapeDtypeStruct((CONV_K, BRANCH_W), F32)],
        input_output_aliases={3: 0},
        compiler_params=_cp(("parallel", "arbitrary")),
    )(proj, dout, conv_w, dproj)


def _mix_fwd(oa, ob, oc, wpa, wpp, wpc, proj, b_gate, name):
    T = oa.shape[0]
    tm = min(512, T)

    def body(oa_ref, ob_ref, oc_ref, wa_ref, wp_ref, wc_ref, g_ref, bg_ref, o_ref):
        acc = jnp.zeros((tm, D_MODEL), F32)
        for i, (x_ref, w_ref) in enumerate(((oa_ref, wa_ref), (ob_ref, wp_ref), (oc_ref, wc_ref))):
            y = jnp.dot(x_ref[...], w_ref[...], preferred_element_type=F32)
            sl = slice(i * D_MODEL, (i + 1) * D_MODEL)
            acc = acc + _sigmoid(g_ref[:, sl].astype(F32) + bg_ref[:, sl]) * y
        o_ref[...] = acc.astype(BF16)

    br = pl.BlockSpec((tm, BRANCH_W), lambda i: (i, 0))
    wsp = pl.BlockSpec((BRANCH_W, D_MODEL), lambda i: (0, 0))
    return pl.pallas_call(
        body, name=name, grid=(T // tm,),
        in_specs=[br, br, br, wsp, wsp, wsp, pl.BlockSpec((tm, GATE_W), lambda i: (i, 0)),
                  pl.BlockSpec((1, GATE_W), lambda i: (0, 0))],
        out_specs=pl.BlockSpec((tm, D_MODEL), lambda i: (i, 0)),
        out_shape=jax.ShapeDtypeStruct((T, D_MODEL), BF16),
        compiler_params=_cp(("parallel",)),
    )(oa, ob, oc, wpa, wpp, wpc, proj, b_gate)


def _mix_bwd(oa, ob, oc, wpa, wpp, wpc, proj, b_gate, dmixed, name):
    T = oa.shape[0]
    tm = min(256, T)

    def body(oa_ref, ob_ref, oc_ref, wa_ref, wp_ref, wc_ref, g_ref, bg_ref, dm_ref,
             dya_ref, dyb_ref, dyc_ref, dg_ref, dbg_ref):
        i0 = pl.program_id(0)
        dm = dm_ref[...].astype(F32)
        parts = []
        for i, (x_ref, w_ref, dy_ref) in enumerate(((oa_ref, wa_ref, dya_ref), (ob_ref, wp_ref, dyb_ref),
                                                    (oc_ref, wc_ref, dyc_ref))):
            y = jnp.dot(x_ref[...], w_ref[...], preferred_element_type=F32)
            sl = slice(i * D_MODEL, (i + 1) * D_MODEL)
            gate = _sigmoid(g_ref[:, sl].astype(F32) + bg_ref[:, sl])
            dy_ref[...] = (dm * gate).astype(BF16)
            dgl = dm * y * gate * (1.0 - gate)
            dg_ref[:, sl] = dgl.astype(BF16)
            parts.append(jnp.sum(dgl, axis=0, keepdims=True))

        @pl.when(i0 == 0)
        def _():
            for i in range(3):
                dbg_ref[:, i * D_MODEL:(i + 1) * D_MODEL] = parts[i]

        @pl.when(i0 > 0)
        def _():
            for i in range(3):
                dbg_ref[:, i * D_MODEL:(i + 1) * D_MODEL] += parts[i]

    br = pl.BlockSpec((tm, BRANCH_W), lambda i: (i, 0))
    wsp = pl.BlockSpec((BRANCH_W, D_MODEL), lambda i: (0, 0))
    row = pl.BlockSpec((tm, D_MODEL), lambda i: (i, 0))
    gsp = pl.BlockSpec((tm, GATE_W), lambda i: (i, 0))
    bsp = pl.BlockSpec((1, GATE_W), lambda i: (0, 0))
    act = jax.ShapeDtypeStruct((T, D_MODEL), BF16)
    return pl.pallas_call(
        body, name=name, grid=(T // tm,),
        in_specs=[br, br, br, wsp, wsp, wsp, gsp, bsp, row],
        out_specs=[row, row, row, gsp, bsp],
        out_shape=[act, act, act, jax.ShapeDtypeStruct((T, MAIN_COLS), BF16),
                   jax.ShapeDtypeStruct((1, GATE_W), F32)],
        compiler_params=_cp(("arbitrary",)),
    )(oa, ob, oc, wpa, wpp, wpc, proj, b_gate, dmixed)


GU_TILE = 256


def _gu_col(c):
    t, r = divmod(c, GU_TILE)
    return (t // 2) * GU_TILE + r + (FFN_HIDDEN if t % 2 else 0)


def _gate_up_swiglu(h, w, name):
    T, K = h.shape
    tm = min(2048, T)

    def body(h_ref, w_ref, ab_ref, s_ref):
        prod = jnp.dot(h_ref[...], w_ref[...], preferred_element_type=F32)
        ab_ref[...] = prod.astype(BF16)
        a = prod[:, :GU_TILE]
        s_ref[...] = (a * _sigmoid(a) * prod[:, GU_TILE:]).astype(BF16)

    return pl.pallas_call(
        body, name=name, grid=(T // tm, FFN_HIDDEN // GU_TILE),
        in_specs=[pl.BlockSpec((tm, K), lambda i, j: (i, 0)), pl.BlockSpec((K, 2 * GU_TILE), lambda i, j: (0, j))],
        out_specs=[pl.BlockSpec((tm, 2 * GU_TILE), lambda i, j: (i, j)), pl.BlockSpec((tm, GU_TILE), lambda i, j: (i, j))],
        out_shape=[jax.ShapeDtypeStruct((T, 2 * FFN_HIDDEN), BF16), jax.ShapeDtypeStruct((T, FFN_HIDDEN), BF16)],
        compiler_params=_cp(("parallel", "parallel")),
    )(h, w)


def _swiglu_bwd_fused(dx, w_down, ab, name):
    T, K = dx.shape
    tm = min(2048, T)

    def body(dx_ref, w_ref, ab_ref, o_ref):
        ds = lax.dot_general(dx_ref[...], w_ref[...], _NT, preferred_element_type=F32)
        a = ab_ref[:, :GU_TILE].astype(F32)
        b = ab_ref[:, GU_TILE:].astype(F32)
        sg = _sigmoid(a)
        o_ref[:, :GU_TILE] = (ds * b * sg * (1.0 + a * (1.0 - sg))).astype(BF16)
        o_ref[:, GU_TILE:] = (ds * a * sg).astype(BF16)

    pair = pl.BlockSpec((tm, 2 * GU_TILE), lambda i, j: (i, j))
    return pl.pallas_call(
        body, name=name, grid=(T // tm, FFN_HIDDEN // GU_TILE),
        in_specs=[pl.BlockSpec((tm, K), lambda i, j: (i, 0)), pl.BlockSpec((GU_TILE, K), lambda i, j: (j, 0)), pair],
        out_specs=pair, out_shape=jax.ShapeDtypeStruct((T, 2 * FFN_HIDDEN), BF16),
        compiler_params=_cp(("parallel", "parallel")),
    )(dx, w_down, ab)


def _adamw_update(w_ref, g_ref, m_ref, v_ref, d_ref, nm_ref, nv_ref):
    gv = g_ref[...]
    nm = ADAM_B1 * m_ref[...] + (1.0 - ADAM_B1) * gv
    nv = ADAM_B2 * v_ref[...] + (1.0 - ADAM_B2) * (gv * gv)
    m_hat = nm / (1.0 - ADAM_B1 ** ADAM_STEP)
    v_hat = nv / (1.0 - ADAM_B2 ** ADAM_STEP)
    d_ref[...] = -ADAM_LR * (m_hat / (jnp.sqrt(v_hat) + ADAM_EPS) + ADAM_WD * w_ref[...])
    nm_ref[...] = nm
    nv_ref[...] = nv


def _adamw_many(ws, gs, ms, vs, name):
    n = len(ws)

    def body(*refs):
        ins, outs = refs[:4 * n], refs[4 * n:]
        for t in range(n):
            _adamw_update(ins[t], ins[n + t], ins[2 * n + t], ins[3 * n + t], outs[t], outs[n + t], outs[2 * n + t])

    shapes = [jax.ShapeDtypeStruct(w.shape, F32) for w in ws]
    out = pl.pallas_call(body, name=name, out_shape=shapes * 3, compiler_params=_cp())(*ws, *gs, *ms, *vs)
    return out[:n], out[n:2 * n], out[2 * n:]


def _adamw(w, g, m, v, name):
    R, C = w.shape
    tr = R
    for cand in (256, 352, 128, 64, 8):
        if R > cand and R % cand == 0:
            tr = cand
            break

    def body(w_ref, g_ref, m_ref, v_ref, d_ref, nm_ref, nv_ref):
        _adamw_update(w_ref, g_ref, m_ref, v_ref, d_ref, nm_ref, nv_ref)

    blk = pl.BlockSpec((tr, C), lambda i: (i, 0))
    sh = jax.ShapeDtypeStruct((R, C), F32)
    return pl.pallas_call(
        body, name=name, grid=(R // tr,), in_specs=[blk] * 4, out_specs=[blk] * 3, out_shape=[sh] * 3,
        compiler_params=_cp(("parallel",)),
    )(w, g, m, v)


def _adamw_3d(w, g, m, v, block, name):
    shape = w.shape
    grid = (shape[0] // block[0], shape[1] // block[1])
    assert shape[0] % block[0] == 0 and shape[1] % block[1] == 0 and block[2] == shape[2], (name, shape, block)

    def body(w_ref, g_ref, m_ref, v_ref, d_ref, nm_ref, nv_ref):
        _adamw_update(w_ref, g_ref, m_ref, v_ref, d_ref, nm_ref, nv_ref)

    blk = pl.BlockSpec(block, lambda i, j: (i, j, 0))
    sh = jax.ShapeDtypeStruct(shape, F32)
    return pl.pallas_call(
        body, name=name, grid=grid, in_specs=[blk] * 4, out_specs=[blk] * 3, out_shape=[sh] * 3,
        compiler_params=_cp(("parallel", "parallel")),
    )(w, g, m, v)


def _sum_slabs_t(x, name):
    n, R, C = x.shape

    def body(x_ref, o_ref):
        acc = x_ref[0].astype(F32)
        for j in range(1, n):
            acc = acc + x_ref[j].astype(F32)
        o_ref[...] = acc.T

    return pl.pallas_call(
        body, name=name, grid=(C // LANES,), in_specs=[pl.BlockSpec((n, R, LANES), lambda j: (0, 0, j))],
        out_specs=pl.BlockSpec((LANES, R), lambda j: (j, 0)), out_shape=jax.ShapeDtypeStruct((C, R), F32),
        compiler_params=_cp(("parallel",)),
    )(x)


def _sum_slabs(x, name):
    n, R, C = x.shape
    tr = R
    for cand in (512, 256, 128, 64, 32, 16, 8):
        if R > cand and R % cand == 0:
            tr = cand
            break

    def body(x_ref, o_ref):
        acc = x_ref[0].astype(F32)
        for j in range(1, n):
            acc = acc + x_ref[j].astype(F32)
        o_ref[...] = acc

    return pl.pallas_call(
        body, name=name, grid=(R // tr,), in_specs=[pl.BlockSpec((n, tr, C), lambda i: (0, i, 0))],
        out_specs=pl.BlockSpec((tr, C), lambda i: (i, 0)), out_shape=jax.ShapeDtypeStruct((R, C), F32),
        compiler_params=_cp(("parallel",)),
    )(x)


def _multi_gather(xs, layers, name):
    nt = len(xs)
    shapes = [x.shape if lay is None else x.shape[1:] for x, lay in zip(xs, layers)]

    def body(*refs):
        x_refs, out_refs = refs[:nt], refs[nt:2 * nt]
        send_sems, recv_sems, local_sems = refs[2 * nt:]
        x_, y_, c_ = lax.axis_index("x"), lax.axis_index("y"), lax.axis_index("c")
        me, sibling = (x_, y_, c_), (x_, y_, 1 - c_)
        chips = [(1 - x_, y_), (x_, 1 - y_), (1 - x_, 1 - y_)]

        def own_block(t):
            return x_refs[t] if layers[t] is None else x_refs[t].at[layers[t]]

        def copy(t, k, block, to, own=False):
            px, py, pc = block
            dst = out_refs[t].at[4 * px + 2 * py + pc]
            return pltpu.make_async_remote_copy(
                src_ref=own_block(t) if own else dst, dst_ref=dst,
                send_sem=send_sems.at[t, k], recv_sem=recv_sems.at[t, k],
                device_id=to, device_id_type=pl.DeviceIdType.MESH)

        mine, first, passed = [], [], []
        for t in range(nt):
            mine.append(pltpu.make_async_copy(own_block(t), out_refs[t].at[4 * x_ + 2 * y_ + c_], local_sems.at[t]))
            mine[-1].start()
            first.append([copy(t, 1 + j, me, (*chip, c_), own=True) for j, chip in enumerate(chips)]
                         + [copy(t, 0, me, sibling, own=True)])
            for cp in first[-1]:
                cp.start()
        for t in range(nt):
            for j, chip in enumerate(chips):
                copy(t, 1 + j, (*chip, c_), me).wait_recv()
                passed.append(copy(t, 4 + j, (*chip, c_), sibling))
                passed[-1].start()
        for t in range(nt):
            copy(t, 0, sibling, me).wait_recv()
            for j, chip in enumerate(chips):
                copy(t, 4 + j, (*chip, 1 - c_), me).wait_recv()
        for cp in [c for f in first for c in f] + passed:
            cp.wait_send()
        for cp in mine:
            cp.wait()

    hbm = pl.BlockSpec(memory_space=pl.ANY)
    return pl.pallas_call(
        body, name=name, out_shape=[jax.ShapeDtypeStruct((N_DEV,) + tuple(s), x.dtype) for s, x in zip(shapes, xs)],
        in_specs=[hbm] * nt, out_specs=[hbm] * nt,
        scratch_shapes=[pltpu.SemaphoreType.DMA((nt, 7)), pltpu.SemaphoreType.DMA((nt, 7)),
                        pltpu.SemaphoreType.DMA((nt,))],
    )(*xs)


_HBM = pl.BlockSpec(memory_space=pltpu.HBM)
_SEM = pl.BlockSpec(memory_space=pltpu.SEMAPHORE)
_PEER_ORDER = (2, 4, 6, 3, 5, 7, 1)


def _split_copies(src_refs, land_refs, send_sems, recv_sems, layers, per_peer):
    x_, y_, c_ = lax.axis_index("x"), lax.axis_index("y"), lax.axis_index("c")
    me = 4 * x_ + 2 * y_ + c_
    copies = []
    for k in _PEER_ORDER:
        px, py, pc = x_ ^ ((k >> 2) & 1), y_ ^ ((k >> 1) & 1), c_ ^ (k & 1)
        peer = 4 * px + 2 * py + pc
        for t in range(len(src_refs)):
            if per_peer:
                src = src_refs[t].at[peer]
            else:
                src = src_refs[t] if layers[t] is None else src_refs[t].at[layers[t]]
            copies.append(pltpu.make_async_remote_copy(
                src_ref=src, dst_ref=land_refs[t].at[me],
                send_sem=send_sems.at[t * (N_DEV - 1) + k - 1], recv_sem=recv_sems.at[t * (N_DEV - 1) + k - 1],
                device_id=(px, py, pc), device_id_type=pl.DeviceIdType.MESH))
    return copies


def _own_copies(src_refs, land_refs, sems, layers, per_peer):
    nt = len(src_refs)
    me = 4 * lax.axis_index("x") + 2 * lax.axis_index("y") + lax.axis_index("c")
    copies = []
    for t in range(nt):
        if per_peer:
            src = src_refs[t].at[me]
        else:
            src = src_refs[t] if layers[t] is None else src_refs[t].at[layers[t]]
        copies.append(pltpu.make_async_copy(src, land_refs[t].at[me], sems.at[nt * (N_DEV - 1) + t]))
    return copies


def _split_start(srcs, layers, per_peer, after, name):
    nt = len(srcs)
    if per_peer:
        land_shapes = [s.shape for s in srcs]
    else:
        land_shapes = [(N_DEV,) + tuple(s.shape if lay is None else s.shape[1:]) for s, lay in zip(srcs, layers)]

    def body(*refs):
        src_refs, land_refs = refs[:nt], refs[nt:2 * nt]
        send_sems, recv_sems = refs[2 * nt + 1], refs[2 * nt + 2]
        token = refs[-1]
        for cp in _split_copies(src_refs, land_refs, send_sems, recv_sems, layers, per_peer):
            cp.start()
        for cp in _own_copies(src_refs, land_refs, send_sems, layers, per_peer):
            cp.start()
        token[...] = jnp.zeros_like(token)

    lands = [pltpu.with_memory_space_constraint(lax.empty(s, x.dtype), pltpu.HBM) for s, x in zip(land_shapes, srcs)]
    srcs = [pltpu.with_memory_space_constraint(x, pltpu.HBM) for x in srcs]
    out = pl.pallas_call(
        body, name=name,
        out_shape=(pltpu.SemaphoreType.DMA((nt * N_DEV,)), pltpu.SemaphoreType.DMA((nt * (N_DEV - 1),)),
                   *[pltpu.HBM(x.shape, x.dtype) for x in srcs], *[pltpu.HBM(s, x.dtype) for s, x in zip(land_shapes, srcs)],
                   jax.ShapeDtypeStruct((8, LANES), F32)),
        in_specs=[_HBM] * (2 * nt) + [pl.BlockSpec(memory_space=pl.ANY)],
        out_specs=(_SEM, _SEM, *([_HBM] * (2 * nt)), pl.BlockSpec(memory_space=pltpu.VMEM)),
        input_output_aliases={i: 2 + i for i in range(2 * nt)},
        compiler_params=pltpu.CompilerParams(has_side_effects=pltpu.SideEffectType.DATAFLOW_SIDE_EFFECTING),
    )(*srcs, *lands, after)
    return out[0], out[1], list(out[2:2 + nt]), list(out[2 + nt:2 + 2 * nt]), out[-1]


def _split_wait(started, layers, per_peer, after, name):
    send_sems, recv_sems, srcs, lands, _ = started
    nt = len(srcs)

    def body(*refs):
        src_refs, land_refs = refs[:nt], refs[nt:2 * nt]
        s_sems, r_sems = refs[2 * nt], refs[2 * nt + 1]
        for cp in _split_copies(src_refs, land_refs, s_sems, r_sems, layers, per_peer):
            cp.wait_send()
            cp.wait_recv()
        for cp in _own_copies(src_refs, land_refs, s_sems, layers, per_peer):
            cp.wait()

    out = pl.pallas_call(
        body, name=name,
        out_shape=tuple(pltpu.HBM(x.shape, x.dtype) for x in srcs + lands),
        in_specs=[_HBM] * (2 * nt) + [_SEM, _SEM, pl.BlockSpec(memory_space=pl.ANY)],
        out_specs=tuple([_HBM] * (2 * nt)),
        input_output_aliases={i: i for i in range(2 * nt)},
        compiler_params=pltpu.CompilerParams(has_side_effects=pltpu.SideEffectType.DATAFLOW_SIDE_EFFECTING),
    )(*srcs, *lands, send_sems, recv_sems, after)
    return list(out[nt:])


def _runs(mapping):
    runs, c, n = [], 0, len(mapping)
    while c < n:
        if mapping[c] is None:
            c += 1
            continue
        sid, d, lo = mapping[c][0], mapping[c][1] - c, c
        while c < n and mapping[c] is not None and mapping[c][0] == sid and mapping[c][1] - c == d:
            c += 1
        runs.append((lo, c, sid, d))
    return runs


def _tile_plan(mapping, src_widths):
    runs = _runs(mapping)
    plan = []
    for t in range(len(mapping) // LANES):
        pieces = []
        for lo, hi, sid, d in runs:
            lo_t, hi_t = max(lo, t * LANES), min(hi, (t + 1) * LANES)
            if lo_t >= hi_t:
                continue
            a = ((lo_t + d) // LANES) * LANES
            win = min(2 * LANES, src_widths[sid] - a)
            shift = t * LANES + d - a
            pieces.append((sid, a, win, shift, lo_t - t * LANES, hi_t - t * LANES))
        plan.append(pieces)
    return plan


def _reblock(srcs, src_views, outs, out_views, name):
    R = srcs[0].shape[-2]
    tr = min(512, R)
    widths = {sid: srcs[ai].shape[-1] for sid, (ai, _) in src_views.items()}
    plans = [(ai, li, _tile_plan(mapping, widths)) for ai, li, mapping in out_views]
    ns = len(srcs)

    def body(*refs):
        s_refs, o_refs = refs[:ns], refs[ns:]
        cache = {}

        def shift_matrix(win, shift, lo, hi):
            key = (win, shift, lo, hi)
            if key not in cache:
                r = lax.broadcasted_iota(jnp.int32, (win, LANES), 0)
                c = lax.broadcasted_iota(jnp.int32, (win, LANES), 1)
                hit = jnp.logical_and(r - c == shift, jnp.logical_and(c >= lo, c < hi))
                cache[key] = jnp.where(hit, 1.0, 0.0).astype(BF16)
            return cache[key]

        for ai, li, plan in plans:
            for t, pieces in enumerate(plan):
                acc = None
                whole = len(pieces) == 1 and pieces[0][3:] == (0, 0, LANES)
                for sid, a, win, shift, lo, hi in pieces:
                    sa, sl = src_views[sid]
                    if whole:
                        win = LANES
                    src = s_refs[sa][:, a:a + win] if sl is None else s_refs[sa][sl, :, a:a + win]
                    if whole:
                        acc = src
                    else:
                        part = jnp.dot(src, shift_matrix(win, shift, lo, hi), preferred_element_type=F32)
                        acc = part if acc is None else acc + part
                val = jnp.zeros((tr, LANES), BF16) if acc is None else acc.astype(BF16)
                if li is None:
                    o_refs[ai][:, t * LANES:(t + 1) * LANES] = val
                else:
                    o_refs[ai][li, :, t * LANES:(t + 1) * LANES] = val

    def spec(shape):
        if len(shape) == 2:
            return pl.BlockSpec((tr, shape[1]), lambda i: (i, 0))
        return pl.BlockSpec((shape[0], tr, shape[2]), lambda i: (0, i, 0))

    return pl.pallas_call(
        body, name=name, grid=(R // tr,), in_specs=[spec(s.shape) for s in srcs],
        out_specs=[spec(s) for s in outs], out_shape=[jax.ShapeDtypeStruct(s, BF16) for s in outs],
        compiler_params=_cp(("parallel",)),
    )(*srcs)


SHARDED = ("w_in", "w_gate_up", "w_proj_attn", "w_proj_pool", "w_proj_conv", "w_out", "w_down")
WEIGHT_ORDER = ("attn_norm", "w_in", "b_forget", "b_gate", "w_proj_attn", "pool_w", "pool_scale", "w_proj_pool",
                "conv_w", "w_proj_conv", "w_out", "ffn_norm", "w_gate_up", "w_down", "final_norm")
IN_SHARD, IN_SHARD_PAD = IN_COLS // N_DEV, 896
GU_SHARD, GU_SHARD_PAD = 2 * FFN_HIDDEN // N_DEV, 768


def _w_in_col(c):
    if c < OFF_QKV:
        return c + 3592
    if c < OFF_U:
        base, off = (0, OFF_QKV) if c < OFF_CONV else (2056, OFF_CONV)
        j, t = divmod(c - off, TRIPLE)
        which, e = divmod(t, LANES)
        return base + which * BRANCH_W + j * LANES + e
    return c - OFF_U + 1544


def _w_in_full(gathered, name):
    main = [divmod(_w_in_col(c), IN_SHARD) for c in range(MAIN_COLS)]
    fcols = [divmod(1536 + c, IN_SHARD) if c < N_HEADS else None for c in range(LANES)]
    R = gathered.shape[1]
    return _reblock([gathered], {i: (0, i) for i in range(N_DEV)}, [(R, MAIN_COLS), (R, LANES)],
                    [(0, None, main), (1, None, fcols)], name)


def _w_in_slabs(dmain, dwf, name):
    inv = {_w_in_col(c): ("m", c) for c in range(MAIN_COLS)}
    inv.update({1536 + c: ("f", c) for c in range(N_HEADS)})
    views = []
    for i in range(N_DEV):
        mapping = [inv[IN_SHARD * i + j] if j < IN_SHARD else None for j in range(IN_SHARD_PAD)]
        views.append((0, i, mapping))
    R = dmain.shape[0]
    return _reblock([dmain, dwf], {"m": (0, None), "f": (1, None)}, [(N_DEV, R, IN_SHARD_PAD)], views, name)[0]


def _w_gu_full(gathered, name):
    mapping = [divmod(_gu_col(c), GU_SHARD) for c in range(2 * FFN_HIDDEN)]
    R = gathered.shape[1]
    return _reblock([gathered], {i: (0, i) for i in range(N_DEV)}, [(R, 2 * FFN_HIDDEN)], [(0, None, mapping)], name)[0]


def _w_gu_slabs(dw, name):
    inv = {_gu_col(c): c for c in range(2 * FFN_HIDDEN)}
    views = [(0, i, [("w", inv[GU_SHARD * i + j]) if j < GU_SHARD else None for j in range(GU_SHARD_PAD)])
             for i in range(N_DEV)]
    R = dw.shape[0]
    return _reblock([dw], {"w": (0, None)}, [(N_DEV, R, GU_SHARD_PAD)], views, name)[0]


def _layer_fwd(x, W, n_seq, l, h1=None, next_norm=None):
    T = x.shape[0]
    sfx = f"_l{l}"
    if h1 is None:
        h1 = _rms_fwd(x, W["attn_norm"], "rms1" + sfx)
    proj, f = _matmul(h1, W["w_main"], mode="nn", out_dtype=BF16, name="proj_main" + sfx, side=(W["w_f"], F32))
    qa, ka, va = _fox_prep(f, W["b_forget"], proj, n_seq, "fox_prep" + sfx)
    oa, oa32, lse = _attn_fwd2(qa, ka, va, n_seq, "attn_fwd" + sfx)
    if "late" in W:
        W.update(W.pop("late")(oa))
    ob = _pool_fwd(proj, W["pool_w"], W["pool_scale"], n_seq, "pool_fwd" + sfx)
    oc = _conv_fwd(proj, W["conv_w"], n_seq, "conv_fwd" + sfx)
    mixed = _mix_fwd(oa, ob, oc, W["w_proj_attn"], W["w_proj_pool"], W["w_proj_conv"], proj, W["b_gate"],
                     "mix_fwd" + sfx)
    x2, h2 = _matmul(mixed, W["w_out"], mode="nn", out_dtype=F32, name="out_proj" + sfx, tm=1024, tn=1024,
                     residual=x, rms_g=W["ffn_norm"])
    ab, s = _gate_up_swiglu(h2, W["w_gate_up"], "gate_up" + sfx)
    x3 = _matmul(s, W["w_down"], mode="nn", out_dtype=F32, name="down" + sfx, tm=1024, tn=1024, tk=1408,
                 residual=x2, rms_g=next_norm)
    x3, h1_next = x3 if next_norm is not None else (x3, None)
    saved = dict(x=x, h1=h1, proj=proj, f=f, qa=qa, ka=ka, oa=oa, oa32=oa32, lse=lse, ob=ob, oc=oc, mixed=mixed, x2=x2,
                 h2=h2, ab=ab, s=s)
    return x3, saved, h1_next


def _layer_bwd(dx3, dx3b, W, sv, n_seq, l, stage=None):
    T = dx3.shape[0]
    sfx = f"_l{l}"
    G = {}
    stage = stage or (lambda l, group, G, W: W)
    dab = _swiglu_bwd_fused(dx3b, W["w_down"], sv["ab"], "d_ab" + sfx)
    G["w_down"] = _matmul(sv["s"], dx3b, mode="tn", out_dtype=BF16, name="dw_down" + sfx, tm=256, tn=1024)
    dh2 = _matmul(dab, W["w_gate_up"], mode="nt", out_dtype=BF16, name="d_h2" + sfx, tm=1024, tn=1024, tk=1408)
    G["w_gate_up"] = _matmul(sv["h2"], dab, mode="tn", out_dtype=BF16, name="dw_gate_up" + sfx, tm=1024)
    W = stage(l, "ffn", G, W)
    dx2, dx2b, G["ffn_norm"] = _rms_bwd(sv["x2"], W["ffn_norm"], dh2, dx3, "rms2_bwd" + sfx)
    dmixed = _matmul(dx2b, W["w_out"], mode="nt", out_dtype=BF16, name="d_mixed" + sfx)
    G["w_out"] = _matmul(sv["mixed"], dx2b, mode="tn", out_dtype=BF16, name="dw_out" + sfx, tm=1024)
    dya, dyb, dyc, dproj, G["b_gate"] = _mix_bwd(sv["oa"], sv["ob"], sv["oc"], W["w_proj_attn"], W["w_proj_pool"],
                                                 W["w_proj_conv"], sv["proj"], W["b_gate"], dmixed, "mix_bwd" + sfx)
    douts = {}
    for br, dy, o in (("attn", dya, sv["oa"]), ("pool", dyb, sv["ob"]), ("conv", dyc, sv["oc"])):
        douts[br] = _matmul(dy, W["w_proj_" + br], mode="nt", out_dtype=BF16, name=f"d_{br}_out" + sfx)
        G["w_proj_" + br] = _matmul(o, dy, mode="tn", out_dtype=BF16, name=f"dw_proj_{br}" + sfx, tm=512)
    W = stage(l, "mix", G, W)
    dproj, G["conv_w"] = _conv_bwd(sv["proj"], douts["conv"], W["conv_w"], dproj, n_seq, "conv_bwd" + sfx)
    dproj, G["pool_w"], G["pool_scale"] = _pool_bwd(sv["proj"], douts["pool"], W["pool_w"], W["pool_scale"], dproj,
                                                    n_seq, "pool_bwd" + sfx)
    dproj, dFk = _attn_bwd(sv["qa"], sv["ka"], sv["proj"], douts["attn"], sv["oa32"], sv["lse"], dproj, n_seq,
                           "attn_bwd" + sfx)
    dF = jnp.pad(dFk.reshape(N_HEADS, T).T, ((0, 0), (0, LANES - N_HEADS)))
    df, G["b_forget"] = _fox_cumsum_bwd(sv["f"], W["b_forget"], dF, n_seq, "fox_cumsum_bwd" + sfx)
    G["w_main"], G["w_f"] = _matmul(sv["h1"], dproj, mode="tn", out_dtype=BF16, name="dw_main" + sfx, tm=1024,
                                    side=(df, BF16))
    W = stage(l, "w_in", G, W)
    dh1 = _matmul(dproj, W["w_main"], mode="nt", out_dtype=BF16, name="d_h1_main" + sfx, tm=1024, tn=1024, tk=1664,
                  extra=(df, W["w_f"]))
    dx, dxb, G["attn_norm"] = _rms_bwd(sv["x"], W["attn_norm"], dh1, dx2, "rms1_bwd" + sfx)
    return dx, dxb, G


def _replicated_operands(rep, l):
    W = {}
    W["attn_norm"], W["ffn_norm"] = rep["attn_norm"][l], rep["ffn_norm"][l]
    W["b_forget"] = jnp.pad(rep["b_forget"][l].reshape(1, N_HEADS), ((0, 0), (0, LANES - N_HEADS)))
    W["b_gate"] = rep["b_gate"][l].reshape(1, GATE_W)
    W["pool_w"] = rep["pool_w"][l].astype(BF16)
    W["pool_scale"] = rep["pool_scale"][l].reshape(1, BRANCH_W)
    return W


def _local_step(x, target, get_W, attn_norms, final_norm, stage=None):
    n_seq, S, Dm = x.shape
    T = n_seq * S
    xt = x.reshape(T, Dm)
    saved, Ws, h1 = [], [], None
    for l in range(DEPTH):
        Ws.append(get_W(l, xt))
        next_norm = attn_norms[l + 1] if l + 1 < DEPTH else None
        xt, sv, h1 = _layer_fwd(xt, Ws[l], n_seq, l, h1, next_norm)
        saved.append(sv)
    loss, dx, dxb, g_final = _loss_head(xt, final_norm, target.reshape(T, Dm), "loss_head")
    grads = [None] * DEPTH
    for l in reversed(range(DEPTH)):
        dx, dxb, grads[l] = _layer_bwd(dx, dxb, Ws[l], saved[l], n_seq, l, stage)
    return loss, dx.reshape(n_seq, S, Dm), grads, g_final


def _padded_shards(weights):
    pads = {"w_in": IN_SHARD_PAD - IN_SHARD, "w_gate_up": GU_SHARD_PAD - GU_SHARD}
    return {n: jnp.pad(weights[n], ((0, 0), (0, 0), (0, pads.get(n, 0)))).astype(BF16) for n in SHARDED}


def _full_operands(g, l):
    W = {}
    if "w_in" in g:
        W["w_main"], W["w_f"] = _w_in_full(g["w_in"], f"w_in_full_l{l}")
    if "w_gate_up" in g:
        W["w_gate_up"] = _w_gu_full(g["w_gate_up"], f"w_gate_up_full_l{l}")
    for n in ("w_proj_attn", "w_proj_pool", "w_proj_conv"):
        if n in g:
            W[n] = jnp.transpose(g[n], (1, 0, 2)).reshape(BRANCH_W, D_MODEL)
    if "w_out" in g:
        W["w_out"] = g["w_out"].reshape(D_MODEL, D_MODEL)
    if "w_down" in g:
        W["w_down"] = g["w_down"].reshape(FFN_HIDDEN, D_MODEL)
    return W


GRAD_GROUPS = {"ffn": ("w_down", "w_gate_up"),
               "mix": ("w_out", "w_proj_attn", "w_proj_pool", "w_proj_conv"),
               "w_in": ("w_in",)}


def _grad_slabs(G, n, l):
    if n == "w_in":
        return _w_in_slabs(G["w_main"], G["w_f"], f"w_in_slabs_l{l}")
    if n == "w_gate_up":
        return _w_gu_slabs(G["w_gate_up"], f"w_gate_up_slabs_l{l}")
    if n == "w_out":
        return G["w_out"].reshape(N_DEV, D_MODEL // N_DEV, D_MODEL)
    if n == "w_down":
        return G["w_down"].reshape(N_DEV, FFN_HIDDEN // N_DEV, D_MODEL)
    return jnp.transpose(G[n].reshape(BRANCH_W, N_DEV, D_MODEL // N_DEV), (1, 0, 2))


def _sum_layer_grads(recv, l):
    out = {}
    for n, r in recv.items():
        if n in ("w_in", "w_gate_up"):
            out[n] = _sum_slabs_t(r, f"sum_{n}_l{l}")[:IN_SHARD if n == "w_in" else GU_SHARD]
        else:
            out[n] = _sum_slabs(r, f"sum_{n}_l{l}")
    return out


def _sum_small(xs, name):
    def body(*refs):
        for x_ref, o_ref in zip(refs[:len(xs)], refs[len(xs):]):
            acc = x_ref[0]
            for j in range(1, N_DEV):
                acc = acc + x_ref[j]
            o_ref[...] = acc

    return pl.pallas_call(
        body, name=name, out_shape=[jax.ShapeDtypeStruct(x.shape[1:], F32) for x in xs],
        compiler_params=_cp(),
    )(*xs)


def _as_2d(a):
    if a.ndim == 1:
        return a.reshape(1, -1)
    return a.reshape(-1, a.shape[-1])


def kernel(x, attn_norm, w_in, b_forget, b_gate, w_proj_attn, pool_w, pool_scale, w_proj_pool, conv_w, w_proj_conv, w_out, ffn_norm, w_gate_up, w_down, final_norm, loss_target, m_attn_norm, m_w_in, m_b_forget, m_b_gate, m_w_proj_attn, m_pool_w, m_pool_scale, m_w_proj_pool, m_conv_w, m_w_proj_conv, m_w_out, m_ffn_norm, m_w_gate_up, m_w_down, m_final_norm, v_attn_norm, v_w_in, v_b_forget, v_b_gate, v_w_proj_attn, v_pool_w, v_pool_scale, v_w_proj_pool, v_conv_w, v_w_proj_conv, v_w_out, v_ffn_norm, v_w_gate_up, v_w_down, v_final_norm):
    weights = dict(attn_norm=attn_norm, w_in=w_in, b_forget=b_forget, b_gate=b_gate, w_proj_attn=w_proj_attn,
                   pool_w=pool_w, pool_scale=pool_scale, w_proj_pool=w_proj_pool, conv_w=conv_w,
                   w_proj_conv=w_proj_conv, w_out=w_out, ffn_norm=ffn_norm, w_gate_up=w_gate_up, w_down=w_down,
                   final_norm=final_norm)
    moments_m = dict(attn_norm=m_attn_norm, w_in=m_w_in, b_forget=m_b_forget, b_gate=m_b_gate,
                     w_proj_attn=m_w_proj_attn, pool_w=m_pool_w, pool_scale=m_pool_scale, w_proj_pool=m_w_proj_pool,
                     conv_w=m_conv_w, w_proj_conv=m_w_proj_conv, w_out=m_w_out, ffn_norm=m_ffn_norm,
                     w_gate_up=m_w_gate_up, w_down=m_w_down, final_norm=m_final_norm)
    moments_v = dict(attn_norm=v_attn_norm, w_in=v_w_in, b_forget=v_b_forget, b_gate=v_b_gate,
                     w_proj_attn=v_w_proj_attn, pool_w=v_pool_w, pool_scale=v_pool_scale, w_proj_pool=v_w_proj_pool,
                     conv_w=v_conv_w, w_proj_conv=v_w_proj_conv, w_out=v_w_out, ffn_norm=v_ffn_norm,
                     w_gate_up=v_w_gate_up, w_down=v_w_down, final_norm=v_final_norm)

    sh = _padded_shards(weights)
    names = list(SHARDED)
    rest = [n for n in names if n != "w_in"]
    me = 4 * lax.axis_index("x") + 2 * lax.axis_index("y") + lax.axis_index("c")
    w_in0, conv_all = _multi_gather([sh["w_in"], conv_w], [0, None], "gather_w_in_l0")
    started, after = {}, w_in0
    for l in range(DEPTH):
        for group, gnames in (("w_in", ["w_in"]), ("rest", rest)):
            if (l, group) != (0, "w_in"):
                started[l, group] = _split_start([sh[n] for n in gnames], [l] * len(gnames), False, after,
                                                 f"gather_start_{group}_l{l}")
                after = started[l, group][4]
    last_token = after

    def get_W(l, xt):
        if l == 0:
            w_in = w_in0
        else:
            w_in = _split_wait(started[l, "w_in"], [l], False, xt, f"gather_wait_w_in_l{l}")[0]
        W = _full_operands({"w_in": w_in}, l)

        def late(after):
            lands = _split_wait(started[l, "rest"], [l] * len(rest), False, after, f"gather_wait_rest_l{l}")
            return _full_operands(dict(zip(rest, lands)), l)

        W["late"] = late
        W.update(_replicated_operands(weights, l))
        W["conv_w"] = jnp.transpose(conv_all[:, l], (1, 0, 2)).reshape(CONV_K, BRANCH_W)
        if l == 0:
            W["attn_norm"] = W["attn_norm"] + last_token[0, 0]
        return W

    exchanges = []

    def stage(l, group, G, W):
        gnames = GRAD_GROUPS[group]
        slabs = [_grad_slabs(G, n, l) for n in gnames]
        started = _split_start(slabs, None, True, slabs[0][0, :8], f"exchange_start_{group}_l{l}")
        exchanges.append((l, group, gnames, slabs, started))
        tie = {"ffn": "ffn_norm", "mix": "conv_w", "w_in": "w_f"}[group]
        W = dict(W)
        W[tie] = W[tie] + started[4][0, 0].astype(W[tie].dtype)
        return W

    loss_part, grad_x, grads, g_final = _local_step(x, loss_target, get_W, attn_norm, final_norm, stage)

    def finish_exchange(ex, after):
        l, group, gnames, slabs, started = ex
        lands = _split_wait(started, None, True, after, f"exchange_wait_{group}_l{l}")
        grads[l].update(_sum_layer_grads(dict(zip(gnames, lands)), l))

    def zero_after(a):
        return jnp.minimum(jnp.abs(a[(0,) * a.ndim]), 0.0)

    *early, last_exchange = exchanges
    for ex in early:
        finish_exchange(ex, grad_x)
    deltas, new_m, new_v, gw = {}, {}, {}, {}
    views = {"w_in": ((2, 0, 1), (1, 2, 0), (49, DEPTH, D_MODEL), 1),
             "w_gate_up": ((0, 2, 1), (0, 2, 1), (1, GU_SHARD // 2, D_MODEL), 0)}

    def update_sharded(n):
        if n in views:
            perm, inv, block, layer_axis = views[n]
            gt = jnp.stack([grads[l][n] for l in range(DEPTH)], axis=layer_axis)
            d, nm, nv = _adamw_3d(jnp.transpose(weights[n], perm), gt, jnp.transpose(moments_m[n], perm),
                                  jnp.transpose(moments_v[n], perm), block, "adamw_" + n)
            deltas[n], new_m[n], new_v[n] = (jnp.transpose(a, inv) for a in (d, nm, nv))
            gw[n] = jnp.transpose(gt, inv)
            return
        gw[n] = jnp.stack([grads[l][n] for l in range(DEPTH)])
        shape = weights[n].shape
        d, nm, nv = _adamw(_as_2d(weights[n]), _as_2d(gw[n]), _as_2d(moments_m[n]), _as_2d(moments_v[n]),
                           "adamw_" + n)
        deltas[n], new_m[n], new_v[n] = d.reshape(shape), nm.reshape(shape), nv.reshape(shape)

    for n in SHARDED:
        if n != "w_in":
            update_sharded(n)

    small = ("attn_norm", "b_forget", "b_gate", "pool_w", "pool_scale", "ffn_norm", "conv_w")
    loss_part = loss_part + zero_after(deltas["w_gate_up"]) + zero_after(deltas["w_down"])
    parts = [jnp.stack([grads[l][n] for l in range(DEPTH)]) for n in small] + [g_final, loss_part]
    gathered = _multi_gather(parts, [None] * len(parts), "gather_small_grads")
    summed = _sum_small(gathered, "sum_small_grads")
    for n, s in zip(small, summed):
        gw[n] = s
    gw["attn_norm"], gw["ffn_norm"] = gw["attn_norm"][:, 0], gw["ffn_norm"][:, 0]
    gw["b_forget"] = gw["b_forget"][:, 0, :N_HEADS]
    gw["b_gate"], gw["pool_scale"] = gw["b_gate"][:, 0], gw["pool_scale"][:, 0]
    gw["conv_w"] = lax.dynamic_slice_in_dim(gw["conv_w"], me * (BRANCH_W // N_DEV), BRANCH_W // N_DEV, axis=2)
    gw["final_norm"] = summed[-2][0]
    loss = summed[-1][0, 0]

    rest_names = [n for n in WEIGHT_ORDER if n not in SHARDED]
    ds, nms, nvs = _adamw_many(*[[_as_2d(src[n]) for n in rest_names] for src in (weights, gw, moments_m, moments_v)],
                               "adamw_small")
    for n, d, nm, nv in zip(rest_names, ds, nms, nvs):
        shape = weights[n].shape
        deltas[n], new_m[n], new_v[n] = d.reshape(shape), nm.reshape(shape), nv.reshape(shape)

    finish_exchange(last_exchange, deltas["pool_w"])
    update_sharded("w_in")

    return (loss, grad_x, *[gw[n] for n in WEIGHT_ORDER], *[deltas[n] for n in WEIGHT_ORDER],
            *[new_m[n] for n in WEIGHT_ORDER], *[new_v[n] for n in WEIGHT_ORDER])
```

```python
import functools

import jax
import jax.numpy as jnp
from jax import lax
from jax.experimental import pallas as pl
from jax.experimental.pallas import tpu as pltpu

F32 = jnp.float32
BF16 = jnp.bfloat16

N_DEV = 8
D_MODEL = 1024
DEPTH = 2
N_HEADS = 8
HEAD_DIM = 64
BRANCH_W = 512
POOL_WINDOWS = (2, 4, 8, 16)
POOL_GD = 128
CONV_K = 3
FFN_HIDDEN = 2816
GATE_W = 3 * D_MODEL
IN_COLS = 6664
MAIN_COLS = GATE_W + 7 * BRANCH_W
RMS_EPS = 1e-6
NEG_INF = -1e30

ADAM_LR = 0.001
ADAM_B1 = 0.9
ADAM_B2 = 0.999
ADAM_EPS = 1e-08
ADAM_WD = 0.01
ADAM_STEP = 10

LANES = 128
VMEM_LIMIT = 56 * 1024 * 1024
CUM_BLK = 256

TRIPLE = 3 * LANES
OFF_G, OFF_QKV, OFF_CONV, OFF_U = 0, 3072, 4608, 6144


def _cp(sem=None):
    return pltpu.CompilerParams(dimension_semantics=sem, vmem_limit_bytes=VMEM_LIMIT)


def _sigmoid(z):
    return 1.0 / (1.0 + jnp.exp(-z))


def _matmul(a, b, *, mode, out_dtype, name, tm=2048, tn=512, tk=None, residual=None, rms_g=None, side=None,
            extra=None):
    if mode == "nn":
        (M, K), N = a.shape, b.shape[1]
    elif mode == "nt":
        (M, K), N = a.shape, b.shape[0]
    else:
        (K, M), N = a.shape, b.shape[1]
    tm, tn, tk = min(tm, M), min(tn, N), K if tk is None else min(tk, K)
    assert M % tm == 0 and N % tn == 0 and K % tk == 0, (name, M, N, K, tm, tn, tk)
    nk = K // tk
    if mode == "nn":
        a_spec = pl.BlockSpec((tm, tk), lambda i, j, k: (i, k))
        b_spec = pl.BlockSpec((tk, tn), lambda i, j, k: (k, j))
        dims = (((1,), (0,)), ((), ()))
    elif mode == "nt":
        a_spec = pl.BlockSpec((tm, tk), lambda i, j, k: (i, k))
        b_spec = pl.BlockSpec((tn, tk), lambda i, j, k: (j, k))
        dims = (((1,), (1,)), ((), ()))
    else:
        a_spec = pl.BlockSpec((tk, tm), lambda i, j, k: (k, i))
        b_spec = pl.BlockSpec((tk, tn), lambda i, j, k: (k, j))
        dims = (((0,), (0,)), ((), ()))
    o_spec = pl.BlockSpec((tm, tn), lambda i, j, k: (i, j))
    has_res, has_norm, has_side, has_extra = (v is not None for v in (residual, rms_g, side, extra))
    assert not has_norm or tn == N, (name, tn, N)
    assert not has_side or (nk == 1 and mode != "nt"), name

    in_specs, args = [a_spec, b_spec], [a, b]
    out_specs, out_shape = [o_spec], [jax.ShapeDtypeStruct((M, N), out_dtype)]
    if has_res:
        in_specs.append(o_spec)
        args.append(residual)
    if has_norm:
        in_specs.append(pl.BlockSpec((1, N), lambda i, j, k: (0, 0)))
        args.append(rms_g.reshape(1, N))
        out_specs.append(o_spec)
        out_shape.append(jax.ShapeDtypeStruct((M, N), BF16))
    if has_side:
        b_side, side_dtype = side
        ns = b_side.shape[1]
        in_specs.append(pl.BlockSpec((K, ns), lambda i, j, k: (0, 0)))
        args.append(b_side)
        out_specs.append(pl.BlockSpec((tm, ns), lambda i, j, k: (i, 0)))
        out_shape.append(jax.ShapeDtypeStruct((M, ns), side_dtype))
    if has_extra:
        a2, b2 = extra
        in_specs += [pl.BlockSpec((tm, a2.shape[1]), lambda i, j, k: (i, 0)),
                     pl.BlockSpec((tn, b2.shape[1]), lambda i, j, k: (j, 0))]
        args += [a2, b2]
    n_in = len(args)

    def body(*refs):
        ins, outs = list(refs[2:n_in]), list(refs[n_in:n_in + len(out_shape)])
        a_ref, b_ref = refs[:2]
        r_ref = ins.pop(0) if has_res else None
        g_ref = ins.pop(0) if has_norm else None
        bs_ref = ins.pop(0) if has_side else None
        a2_ref, b2_ref = (ins.pop(0), ins.pop(0)) if has_extra else (None, None)
        o_ref = outs.pop(0)
        h_ref = outs.pop(0) if has_norm else None
        so_ref = outs.pop(0) if has_side else None

        def finish(acc):
            if has_res:
                acc = acc + r_ref[...].astype(F32)
            if has_extra:
                acc = acc + lax.dot_general(a2_ref[...], b2_ref[...], (((1,), (1,)), ((), ())),
                                            preferred_element_type=F32)
            o_ref[...] = acc.astype(out_dtype)
            if has_norm:
                r = lax.rsqrt(jnp.mean(acc * acc, axis=-1, keepdims=True) + RMS_EPS)
                h_ref[...] = ((acc * r) * g_ref[...]).astype(BF16)

        if has_side:
            @pl.when(pl.program_id(1) == 0)
            def _():
                side_dims = (((1,), (0,)), ((), ())) if mode == "nn" else dims
                so_ref[...] = lax.dot_general(a_ref[...], bs_ref[...], side_dims,
                                              preferred_element_type=F32).astype(so_ref.dtype)

        prod = lax.dot_general(a_ref[...], b_ref[...], dims, preferred_element_type=F32)
        if nk == 1:
            finish(prod)
            return
        acc_ref = refs[-1]
        k = pl.program_id(2)

        @pl.when(k == 0)
        def _():
            acc_ref[...] = prod

        @pl.when(jnp.logical_and(k > 0, k < nk - 1))
        def _():
            acc_ref[...] += prod

        @pl.when(k == nk - 1)
        def _():
            finish(acc_ref[...] + prod)

    single = len(out_shape) == 1
    return pl.pallas_call(
        body, name=name, grid=(M // tm, N // tn, nk), in_specs=in_specs,
        out_specs=out_specs[0] if single else out_specs, out_shape=out_shape[0] if single else out_shape,
        scratch_shapes=[pltpu.VMEM((tm, tn), F32)] if nk > 1 else [],
        compiler_params=_cp(("parallel", "arbitrary" if has_side else "parallel", "arbitrary")),
    )(*args)


def _rms_fwd(x, g, name):
    T, Dm = x.shape
    tm = min(512, T)

    def body(x_ref, g_ref, h_ref):
        xf = x_ref[...]
        r = lax.rsqrt(jnp.mean(xf * xf, axis=-1, keepdims=True) + RMS_EPS)
        h_ref[...] = ((xf * r) * g_ref[...]).astype(BF16)

    return pl.pallas_call(
        body, name=name, grid=(T // tm,),
        in_specs=[pl.BlockSpec((tm, Dm), lambda i: (i, 0)), pl.BlockSpec((1, Dm), lambda i: (0, 0))],
        out_specs=pl.BlockSpec((tm, Dm), lambda i: (i, 0)),
        out_shape=jax.ShapeDtypeStruct((T, Dm), BF16),
        compiler_params=_cp(("parallel",)),
    )(x, g.reshape(1, Dm))


def _rms_bwd(x, g, dh, dres, name):
    T, Dm = x.shape
    tm = min(512, T)

    def body(x_ref, g_ref, dh_ref, dres_ref, dx_ref, dxb_ref, dg_ref):
        i = pl.program_id(0)
        xf = x_ref[...]
        r = lax.rsqrt(jnp.mean(xf * xf, axis=-1, keepdims=True) + RMS_EPS)
        xn = xf * r
        dhf = dh_ref[...].astype(F32)
        dxn = dhf * g_ref[...]
        c = jnp.mean(dxn * xn, axis=-1, keepdims=True)
        dx = dres_ref[...] + r * (dxn - xn * c)
        dx_ref[...] = dx
        dxb_ref[...] = dx.astype(BF16)
        part = jnp.sum(dhf * xn, axis=0, keepdims=True)

        @pl.when(i == 0)
        def _():
            dg_ref[...] = part

        @pl.when(i > 0)
        def _():
            dg_ref[...] += part

    row = pl.BlockSpec((tm, Dm), lambda i: (i, 0))
    vec = pl.BlockSpec((1, Dm), lambda i: (0, 0))
    return pl.pallas_call(
        body, name=name, grid=(T // tm,), in_specs=[row, vec, row, row], out_specs=[row, row, vec],
        out_shape=[jax.ShapeDtypeStruct((T, Dm), F32), jax.ShapeDtypeStruct((T, Dm), BF16),
                   jax.ShapeDtypeStruct((1, Dm), F32)],
        compiler_params=_cp(("arbitrary",)),
    )(x, g.reshape(1, Dm), dh, dres)


def _loss_head(x, g, target, name):
    T, Dm = x.shape
    tm = min(512, T)

    def body(x_ref, g_ref, t_ref, loss_ref, dx_ref, dxb_ref, dg_ref):
        i = pl.program_id(0)
        xf = x_ref[...]
        gv = g_ref[...]
        r = lax.rsqrt(jnp.mean(xf * xf, axis=-1, keepdims=True) + RMS_EPS)
        xn = xf * r
        diff = xn * gv - t_ref[...]
        per_tok = jnp.mean(diff * diff, axis=-1, keepdims=True)
        lpart = 0.5 * jnp.sum(per_tok, axis=0, keepdims=True) + jnp.zeros((1, LANES), F32)
        dy = diff * (1.0 / Dm)
        dxn = dy * gv
        c = jnp.mean(dxn * xn, axis=-1, keepdims=True)
        dx = r * (dxn - xn * c)
        dx_ref[...] = dx
        dxb_ref[...] = dx.astype(BF16)
        part = jnp.sum(dy * xn, axis=0, keepdims=True)

        @pl.when(i == 0)
        def _():
            dg_ref[...] = part
            loss_ref[...] = lpart

        @pl.when(i > 0)
        def _():
            dg_ref[...] += part
            loss_ref[...] += lpart

    row = pl.BlockSpec((tm, Dm), lambda i: (i, 0))
    vec = pl.BlockSpec((1, Dm), lambda i: (0, 0))
    lsp = pl.BlockSpec((1, LANES), lambda i: (0, 0))
    return pl.pallas_call(
        body, name=name, grid=(T // tm,), in_specs=[row, vec, row], out_specs=[lsp, row, row, vec],
        out_shape=[jax.ShapeDtypeStruct((1, LANES), F32), jax.ShapeDtypeStruct((T, Dm), F32),
                   jax.ShapeDtypeStruct((T, Dm), BF16), jax.ShapeDtypeStruct((1, Dm), F32)],
        compiler_params=_cp(("arbitrary",)),
    )(x, g.reshape(1, Dm), target)


def _split_bf16(v):
    hi = v.astype(BF16)
    r1 = v - hi.astype(F32)
    mid = r1.astype(BF16)
    lo = (r1 - mid.astype(F32)).astype(BF16)
    return hi, mid, lo


def _tri_dot(tri, v):
    hi, mid, lo = _split_bf16(v)
    dot = functools.partial(jnp.dot, preferred_element_type=F32)
    return dot(tri, hi) + dot(tri, mid) + dot(tri, lo)


def _log_sigmoid(z):
    return jnp.minimum(z, 0.0) - jnp.log(1.0 + jnp.exp(-jnp.abs(z)))


def _fox_cumsum_bwd(f, bf, dF, n_seq, name):
    T = f.shape[0]
    S = T // n_seq
    c = min(CUM_BLK, S)

    def body(f_ref, b_ref, dF_ref, df_ref, db_ref):
        b = pl.program_id(0)
        ri = lax.broadcasted_iota(jnp.int32, (c, c), 0)
        ci = lax.broadcasted_iota(jnp.int32, (c, c), 1)
        tri = (ri <= ci).astype(BF16)
        carry = jnp.zeros((1, LANES), F32)
        dbp = jnp.zeros((1, LANES), F32)
        for j in reversed(range(S // c)):
            dFc = dF_ref[j * c:(j + 1) * c, :]
            dlf = _tri_dot(tri, dFc) + carry
            carry = carry + jnp.sum(dFc, axis=0, keepdims=True)
            z = f_ref[j * c:(j + 1) * c, :] + b_ref[...]
            dz = dlf * _sigmoid(-z)
            df_ref[j * c:(j + 1) * c, :] = dz.astype(BF16)
            dbp = dbp + jnp.sum(dz, axis=0, keepdims=True)

        @pl.when(b == 0)
        def _():
            db_ref[...] = dbp

        @pl.when(b > 0)
        def _():
            db_ref[...] += dbp

    blk = pl.BlockSpec((S, LANES), lambda b: (b, 0))
    vec = pl.BlockSpec((1, LANES), lambda b: (0, 0))
    return pl.pallas_call(
        body, name=name, grid=(n_seq,), in_specs=[blk, vec, blk], out_specs=[blk, vec],
        out_shape=[jax.ShapeDtypeStruct((T, LANES), BF16), jax.ShapeDtypeStruct((1, LANES), F32)],
        compiler_params=_cp(("arbitrary",)),
    )(f, bf, dF)


def _pair_masks():
    lane = lax.broadcasted_iota(jnp.int32, (1, LANES), 1)
    lo = lane < HEAD_DIM
    return lo, jnp.logical_not(lo)


AUG0 = HEAD_DIM
Q_TILE, K_CHUNK, ROW_GROUP = 2048, 256, 64


def _fox_prep(f, bf, proj, n_seq, name):
    T = f.shape[0]
    S = T // n_seq
    c = min(CUM_BLK, S)

    def body(f_ref, b_ref, qkv_ref, qa_ref, ka_ref, va_ref):
        ri = lax.broadcasted_iota(jnp.int32, (c, c), 0)
        ci = lax.broadcasted_iota(jnp.int32, (c, c), 1)
        tri = (ri >= ci).astype(BF16)
        lane = lax.broadcasted_iota(jnp.int32, (c, LANES), 1)
        carry = jnp.zeros((1, LANES), F32)
        for j in range(S // c):
            rows = slice(j * c, (j + 1) * c)
            lf = _log_sigmoid(f_ref[rows, :] + b_ref[...])
            Fc = _tri_dot(tri, lf) + carry
            carry = carry + jnp.sum(lf, axis=0, keepdims=True)
            for h in range(N_HEADS):
                col = jnp.sum(jnp.where(lane == h, Fc, 0.0), axis=-1, keepdims=True)
                hi = col.astype(BF16).astype(F32)
                r1 = col - hi
                mid = r1.astype(BF16).astype(F32)
                lo = r1 - mid
                ones_q = jnp.logical_and(lane >= AUG0 + 3, lane < AUG0 + 6)
                ones_k = jnp.logical_and(lane >= AUG0, lane < AUG0 + 3)
                aug_q = jnp.where(lane == AUG0, hi, jnp.where(lane == AUG0 + 1, mid, jnp.where(
                    lane == AUG0 + 2, lo, jnp.where(ones_q, 1.0, 0.0))))
                aug_k = jnp.where(lane == AUG0 + 3, -hi, jnp.where(lane == AUG0 + 4, -mid, jnp.where(
                    lane == AUG0 + 5, -lo, jnp.where(ones_k, 1.0, 0.0))))
                base = (h // 2) * TRIPLE
                qp, kp, vp = (qkv_ref[rows, base + t * LANES:base + (t + 1) * LANES].astype(F32) for t in range(3))
                if h % 2:
                    qp, kp, vp = (pltpu.roll(a, HEAD_DIM, 1) for a in (qp, kp, vp))
                out = slice(h * LANES, (h + 1) * LANES)
                qa_ref[rows, out] = jnp.where(lane < HEAD_DIM, qp * (HEAD_DIM ** -0.5), aug_q).astype(BF16)
                ka_ref[rows, out] = jnp.where(lane < HEAD_DIM, kp, aug_k).astype(BF16)
                va_ref[rows, out] = jnp.where(lane < HEAD_DIM, vp, jnp.where(lane == AUG0, 1.0, 0.0)).astype(BF16)

    fblk = pl.BlockSpec((S, LANES), lambda b: (b, 0))
    out = pl.BlockSpec((S, N_HEADS * LANES), lambda b: (b, 0))
    sh = jax.ShapeDtypeStruct((T, N_HEADS * LANES), BF16)
    return pl.pallas_call(
        body, name=name, grid=(n_seq,),
        in_specs=[fblk, pl.BlockSpec((1, LANES), lambda b: (0, 0)),
                  pl.BlockSpec((S, 4 * TRIPLE), lambda b: (b, OFF_QKV // (4 * TRIPLE)))],
        out_specs=[out, out, out], out_shape=[sh, sh, sh],
        compiler_params=_cp(("parallel",)),
    )(f, bf, proj)


def _band_mask(q0, k0, nq, nk):
    row = q0 + lax.broadcasted_iota(jnp.int32, (nq, nk), 0)
    col = k0 + lax.broadcasted_iota(jnp.int32, (nq, nk), 1)
    return col <= row


_NT = (((1,), (1,)), ((), ()))
_TN = (((0,), (0,)), ((), ()))


def _attn_fwd2(qa, ka, va, n_seq, name):
    T = qa.shape[0]
    S = T // n_seq
    tq, tk, rg = min(Q_TILE, S), min(K_CHUNK, S), ROW_GROUP
    nq, per = S // tq, tq // tk

    def body(q_ref, k_ref, v_ref, o_ref, o32_ref, lse_ref, phi_s, plo_s, mp_s, m_s, acc_s):
        qi = pl.program_id(2)
        mp_s[...] = jnp.full_like(mp_s, NEG_INF)
        acc_s[...] = jnp.zeros_like(acc_s)

        def scores(kc, hh, r0):
            k0 = pl.multiple_of(kc * tk, tk)
            hl = slice(hh * LANES, (hh + 1) * LANES)
            return k0, lax.dot_general(q_ref[r0:, hl], k_ref[pl.ds(k0, tk), hl], _NT, preferred_element_type=F32)

        def max_chunk(kc, masked, r0):
            for hh in range(2):
                k0, s_all = scores(kc, hh, r0)
                for r in range(r0 // rg, tq // rg):
                    rows = slice(r * rg, (r + 1) * rg)
                    s = s_all[r * rg - r0:(r + 1) * rg - r0, :]
                    if masked:
                        s = jnp.where(_band_mask(qi * tq + r * rg, k0, rg, tk), s, NEG_INF)
                    part = s[:, :LANES]
                    for c in range(1, tk // LANES):
                        part = jnp.maximum(part, s[:, c * LANES:(c + 1) * LANES])
                    mp_s[hh, rows, :] = jnp.maximum(mp_s[hh, rows, :], part)

        def sum_chunk(kc, masked, r0):
            for hh in range(2):
                k0, s_all = scores(kc, hh, r0)
                hl = slice(hh * LANES, (hh + 1) * LANES)
                v = v_ref[pl.ds(k0, tk), hl]
                for r in range(r0 // rg, tq // rg):
                    rows = slice(r * rg, (r + 1) * rg)
                    p = jnp.exp(s_all[r * rg - r0:(r + 1) * rg - r0, :] - m_s[hh, rows])
                    if masked:
                        p = jnp.where(_band_mask(qi * tq + r * rg, k0, rg, tk), p, 0.0)
                    p_hi = p.astype(BF16)
                    phi_s[hh, rows, :] = p_hi
                    plo_s[hh, rows, :] = (p - p_hi.astype(F32)).astype(BF16)
                acc_s[hh, r0:, :] += (jnp.dot(phi_s[hh, r0:, :], v, preferred_element_type=F32)
                                      + jnp.dot(plo_s[hh, r0:, :], v, preferred_element_type=F32))

        def sweep(chunk):
            def unmasked(kc, carry):
                chunk(kc, False, 0)
                return carry

            lax.fori_loop(0, qi * per, unmasked, 0)
            for d in range(per):
                chunk(qi * per + d, True, d * tk)

        sweep(max_chunk)
        m_s[...] = jnp.max(mp_s[...], axis=-1, keepdims=True)
        sweep(sum_chunk)

        lane = lax.broadcasted_iota(jnp.int32, (1, LANES), 1)
        outs = []
        for hh in range(2):
            acc = acc_s[hh]
            l = jnp.sum(jnp.where(lane == AUG0, acc, 0.0), axis=-1, keepdims=True)
            lse_ref[hh] = m_s[hh] + jnp.log(l)
            outs.append(acc / l)
        o = jnp.where(lane < HEAD_DIM, outs[0], pltpu.roll(outs[1], HEAD_DIM, 1))
        o_ref[...] = o.astype(BF16)
        o32_ref[...] = o

    qmap = lambda b, j, qi: (b * nq + qi, j)
    omap = lambda b, j, qi: (b * nq + qi, j)
    kv = pl.BlockSpec((S, 2 * LANES), lambda b, j, qi: (b, j))
    return pl.pallas_call(
        body, name=name, grid=(n_seq, N_HEADS // 2, nq),
        in_specs=[pl.BlockSpec((tq, 2 * LANES), qmap), kv, kv],
        out_specs=[pl.BlockSpec((tq, LANES), omap), pl.BlockSpec((tq, LANES), omap),
                   pl.BlockSpec((2, tq, 1), lambda b, j, qi: (j, b * nq + qi, 0))],
        out_shape=[jax.ShapeDtypeStruct((T, BRANCH_W), BF16), jax.ShapeDtypeStruct((T, BRANCH_W), F32),
                   jax.ShapeDtypeStruct((N_HEADS, T, 1), F32)],
        scratch_shapes=[pltpu.VMEM((2, tq, tk), BF16), pltpu.VMEM((2, tq, tk), BF16),
                        pltpu.VMEM((2, tq, LANES), F32), pltpu.VMEM((2, tq, 1), F32),
                        pltpu.VMEM((2, tq, LANES), F32)],
        compiler_params=_cp(("parallel", "parallel", "parallel")),
    )(qa, ka, va)


def _attn_bwd(qa, ka, proj, do, o32, lse, dproj, n_seq, name):
    T = qa.shape[0]
    S = T // n_seq
    tq, tk, rg = min(Q_TILE, S), min(K_CHUNK, S), ROW_GROUP
    nq, per, nkc = S // tq, tq // tk, S // tk

    def body(q_ref, k_ref, v_ref, do_ref, o_ref, lse_ref, _, dqkv_ref, dfk_ref,
             p_s, ds_s, dq_s, dk_s, dv_s, df_s):
        dk_s[...] = jnp.zeros_like(dk_s)
        dv_s[...] = jnp.zeros_like(dv_s)
        df_s[...] = jnp.zeros_like(df_s)
        sels = _pair_masks()

        for qi in range(nq):
            q0 = qi * tq
            do_t = do_ref[q0:q0 + tq, :]
            dq_s[...] = jnp.zeros_like(dq_s)
            prod = do_t.astype(F32) * o_ref[q0:q0 + tq, :]
            dls = [jnp.sum(jnp.where(sel, prod, 0.0), axis=-1, keepdims=True) for sel in sels]

            def chunk(kc, masked, r0, q0=q0, do_t=do_t, dls=dls):
                k0 = pl.multiple_of(kc * tk, tk)
                v = v_ref[pl.ds(k0, tk), :]
                do_a = do_t[r0:, :]
                for hh in range(2):
                    hl = slice(hh * LANES, (hh + 1) * LANES)
                    qh, kh = q_ref[q0 + r0:q0 + tq, hl], k_ref[pl.ds(k0, tk), hl]
                    s_all = lax.dot_general(qh, kh, _NT, preferred_element_type=F32)
                    dom = jnp.where(sels[hh], do_a, jnp.zeros_like(do_a))
                    dp_all = lax.dot_general(dom, v, _NT, preferred_element_type=F32)
                    dfp = jnp.zeros((1, tk), F32)
                    for r in range(r0 // rg, tq // rg):
                        rows = slice(r * rg, (r + 1) * rg)
                        arows = slice(r * rg - r0, (r + 1) * rg - r0)
                        qrows = slice(q0 + r * rg, q0 + (r + 1) * rg)
                        p = jnp.exp(s_all[arows, :] - lse_ref[hh, qrows])
                        if masked:
                            p = jnp.where(_band_mask(q0 + r * rg, k0, rg, tk), p, 0.0)
                        ds = p * (dp_all[arows, :] - dls[hh][rows])
                        p_s[hh, rows, :] = p.astype(BF16)
                        ds_s[hh, rows, :] = ds.astype(BF16)
                        dfp = dfp + jnp.sum(ds, axis=0, keepdims=True)
                    df_s[hh, kc] -= dfp
                    dq_s[hh, r0:, :] += jnp.dot(ds_s[hh, r0:, :], kh, preferred_element_type=F32)
                    dv_s[hh, pl.ds(k0, tk), :] += lax.dot_general(p_s[hh, r0:, :], do_a, _TN,
                                                                  preferred_element_type=F32)
                    dk_s[hh, pl.ds(k0, tk), :] += lax.dot_general(ds_s[hh, r0:, :], qh, _TN,
                                                                  preferred_element_type=F32)

            def unmasked(kc, carry, chunk=chunk):
                chunk(kc, False, 0)
                return carry

            lax.fori_loop(0, qi * per, unmasked, 0)
            for d in range(per):
                chunk(qi * per + d, True, d * tk)
            dq = jnp.where(sels[0], dq_s[0], pltpu.roll(dq_s[1], HEAD_DIM, 1))
            dqkv_ref[q0:q0 + tq, :LANES] = (dq * (HEAD_DIM ** -0.5)).astype(BF16)

        dqkv_ref[:, LANES:2 * LANES] = jnp.where(sels[0], dk_s[0], pltpu.roll(dk_s[1], HEAD_DIM, 1)).astype(BF16)
        dqkv_ref[:, 2 * LANES:] = jnp.where(sels[0], dv_s[0], dv_s[1]).astype(BF16)
        for c in range(nkc):
            dfk_ref[:, :, c * tk:(c + 1) * tk] = df_s[:, c]

    seq = lambda w: pl.BlockSpec((S, w), lambda b, j: (b, j))
    col1 = pl.BlockSpec((2, S, 1), lambda b, j: (j, b, 0))
    vblk = pl.BlockSpec((S, LANES), lambda b, j: (b, OFF_QKV // LANES + 3 * j + 2))
    return pl.pallas_call(
        body, name=name, grid=(n_seq, N_HEADS // 2),
        in_specs=[seq(2 * LANES), seq(2 * LANES), vblk, seq(LANES), seq(LANES), col1,
                  pl.BlockSpec(memory_space=pl.ANY)],
        out_specs=[pl.BlockSpec((S, TRIPLE), lambda b, j: (b, OFF_QKV // TRIPLE + j)),
                   pl.BlockSpec((2, 1, S), lambda b, j: (j, 0, b))],
        out_shape=[jax.ShapeDtypeStruct(dproj.shape, BF16), jax.ShapeDtypeStruct((N_HEADS, 1, T), F32)],
        input_output_aliases={6: 0},
        scratch_shapes=[pltpu.VMEM((2, tq, tk), BF16), pltpu.VMEM((2, tq, tk), BF16),
                        pltpu.VMEM((2, tq, LANES), F32), pltpu.VMEM((2, S, LANES), F32),
                        pltpu.VMEM((2, S, LANES), F32), pltpu.VMEM((2, nkc, 1, tk), F32)],
        compiler_params=_cp(("parallel", "parallel")),
    )(qa, ka, proj, do, o32, lse, dproj)


def _shift_down(v, k, row):
    return jnp.where(row >= k, pltpu.roll(v, k, 0), 0.0)


def _shift_up(v, k, row, S):
    return jnp.where(row < S - k, pltpu.roll(v, S - k, 0), 0.0)


def _pool_diff(uf, w, row):
    acc, k = uf, 1
    while k < w:
        acc = acc + _shift_down(acc, k, row)
        k *= 2
    n = jnp.minimum(row + 1, w).astype(F32)
    return acc / n - uf


def _pool_fwd(proj, pool_w, pool_scale, n_seq, name):
    T = proj.shape[0]
    S = T // n_seq

    def body(u_ref, w_ref, sc_ref, o_ref, d_s):
        g = pl.program_id(1)
        row = lax.broadcasted_iota(jnp.int32, (S, POOL_GD), 0)
        uf = u_ref[...].astype(F32)
        for gi, wlen in enumerate(POOL_WINDOWS):
            @pl.when(g == gi)
            def _(wlen=wlen):
                d_s[...] = _pool_diff(uf, wlen, row).astype(BF16)
        e = jnp.dot(d_s[...], w_ref[0], preferred_element_type=F32)
        o_ref[...] = (e * sc_ref[...]).astype(BF16)

    uc = OFF_U // POOL_GD
    return pl.pallas_call(
        body, name=name, grid=(n_seq, len(POOL_WINDOWS)),
        in_specs=[pl.BlockSpec((S, POOL_GD), lambda b, g: (b, uc + g)),
                  pl.BlockSpec((1, POOL_GD, POOL_GD), lambda b, g: (g, 0, 0)),
                  pl.BlockSpec((1, POOL_GD), lambda b, g: (0, g))],
        out_specs=pl.BlockSpec((S, POOL_GD), lambda b, g: (b, g)),
        out_shape=jax.ShapeDtypeStruct((T, BRANCH_W), BF16),
        scratch_shapes=[pltpu.VMEM((S, POOL_GD), BF16)],
        compiler_params=_cp(("parallel", "parallel")),
    )(proj, pool_w, pool_scale)


def _pool_bwd(proj, dout, pool_w, pool_scale, dproj, n_seq, name):
    T = proj.shape[0]
    S = T // n_seq

    def body(u_ref, do_ref, w_ref, sc_ref, _, du_ref, dw_ref, dsc_ref, d_s):
        g, b = pl.program_id(0), pl.program_id(1)
        row = lax.broadcasted_iota(jnp.int32, (S, POOL_GD), 0)
        uf = u_ref[...].astype(F32)
        for gi, wlen in enumerate(POOL_WINDOWS):
            @pl.when(g == gi)
            def _(wlen=wlen):
                d_s[...] = _pool_diff(uf, wlen, row).astype(BF16)
        db16 = d_s[...]
        w = w_ref[0]
        e = jnp.dot(db16, w, preferred_element_type=F32)
        dof = do_ref[...].astype(F32)
        dsc = jnp.sum(dof * e, axis=0, keepdims=True)
        de = (dof * sc_ref[...]).astype(BF16)
        dd = lax.dot_general(de, w, (((1,), (1,)), ((), ())), preferred_element_type=F32)
        dw = lax.dot_general(db16, de, (((0,), (0,)), ((), ())), preferred_element_type=F32)
        for gi, wlen in enumerate(POOL_WINDOWS):
            @pl.when(g == gi)
            def _(wlen=wlen):
                n = jnp.minimum(row + 1, wlen).astype(F32)
                acc, k = dd / n, 1
                while k < wlen:
                    acc = acc + _shift_up(acc, k, row, S)
                    k *= 2
                du_ref[...] = (acc - dd).astype(BF16)

        @pl.when(b == 0)
        def _():
            dw_ref[0] = dw
            dsc_ref[...] = dsc

        @pl.when(b > 0)
        def _():
            dw_ref[0] += dw
            dsc_ref[...] += dsc

    uc = OFF_U // POOL_GD
    return pl.pallas_call(
        body, name=name, grid=(len(POOL_WINDOWS), n_seq),
        in_specs=[pl.BlockSpec((S, POOL_GD), lambda g, b: (b, uc + g)),
                  pl.BlockSpec((S, POOL_GD), lambda g, b: (b, g)),
                  pl.BlockSpec((1, POOL_GD, POOL_GD), lambda g, b: (g, 0, 0)),
                  pl.BlockSpec((1, POOL_GD), lambda g, b: (0, g)),
                  pl.BlockSpec(memory_space=pl.ANY)],
        out_specs=[pl.BlockSpec((S, POOL_GD), lambda g, b: (b, uc + g)),
                   pl.BlockSpec((1, POOL_GD, POOL_GD), lambda g, b: (g, 0, 0)),
                   pl.BlockSpec((1, POOL_GD), lambda g, b: (0, g))],
        out_shape=[jax.ShapeDtypeStruct(dproj.shape, BF16),
                   jax.ShapeDtypeStruct((len(POOL_WINDOWS), POOL_GD, POOL_GD), F32),
                   jax.ShapeDtypeStruct((1, BRANCH_W), F32)],
        input_output_aliases={4: 0},
        scratch_shapes=[pltpu.VMEM((S, POOL_GD), BF16)],
        compiler_params=_cp(("parallel", "arbitrary")),
    )(proj, dout, pool_w, pool_scale, dproj)


def _conv_fwd(proj, conv_w, n_seq, name):
    T = proj.shape[0]
    S = T // n_seq
    nc = BRANCH_W // LANES

    def body(c_ref, w_ref, o_ref):
        row = lax.broadcasted_iota(jnp.int32, (S, LANES), 0)
        cv, cb, cc = (c_ref[:, t * LANES:(t + 1) * LANES].astype(F32) for t in range(3))
        z = cc * cv
        w = w_ref[...]
        y = w[0:1] * _shift_down(z, 2, row) + w[1:2] * _shift_down(z, 1, row) + w[2:3] * z
        o_ref[...] = (cb * y).astype(BF16)

    return pl.pallas_call(
        body, name=name, grid=(n_seq, nc),
        in_specs=[pl.BlockSpec((S, TRIPLE), lambda b, j: (b, OFF_CONV // TRIPLE + j)),
                  pl.BlockSpec((CONV_K, LANES), lambda b, j: (0, j))],
        out_specs=pl.BlockSpec((S, LANES), lambda b, j: (b, j)),
        out_shape=jax.ShapeDtypeStruct((T, BRANCH_W), BF16),
        compiler_params=_cp(("parallel", "parallel")),
    )(proj, conv_w)


def _conv_bwd(proj, dout, conv_w, dproj, n_seq, name):
    T = proj.shape[0]
    S = T // n_seq
    nc = BRANCH_W // LANES

    def body(c_ref, do_ref, w_ref, _, dc_ref, dw_ref):
        b = pl.program_id(1)
        row = lax.broadcasted_iota(jnp.int32, (S, LANES), 0)
        cv, cb, cc = (c_ref[:, t * LANES:(t + 1) * LANES].astype(F32) for t in range(3))
        dof = do_ref[...].astype(F32)
        w = w_ref[...]
        z = cc * cv
        z1, z2 = _shift_down(z, 1, row), _shift_down(z, 2, row)
        y = w[0:1] * z2 + w[1:2] * z1 + w[2:3] * z
        dy = dof * cb
        dz = w[2:3] * dy + w[1:2] * _shift_up(dy, 1, row, S) + w[0:1] * _shift_up(dy, 2, row, S)
        dc_ref[:, :LANES] = (dz * cc).astype(BF16)
        dc_ref[:, LANES:2 * LANES] = (dof * y).astype(BF16)
        dc_ref[:, 2 * LANES:] = (dz * cv).astype(BF16)
        dws = [jnp.sum(dy * zk, axis=0, keepdims=True) for zk in (z2, z1, z)]

        @pl.when(b == 0)
        def _():
            for kk in range(CONV_K):
                dw_ref[kk:kk + 1, :] = dws[kk]

        @pl.when(b > 0)
        def _():
            for kk in range(CONV_K):
                dw_ref[kk:kk + 1, :] += dws[kk]

    triple = pl.BlockSpec((S, TRIPLE), lambda j, b: (b, OFF_CONV // TRIPLE + j))
    wsp = pl.BlockSpec((CONV_K, LANES), lambda j, b: (0, j))
    return pl.pallas_call(
        body, name=name, grid=(nc, n_seq),
        in_specs=[triple, pl.BlockSpec((S, LANES), lambda j, b: (b, j)), wsp, pl.BlockSpec(memory_space=pl.ANY)],
        out_specs=[triple, wsp],
        out_shape=[jax.ShapeDtypeStruct(dproj.shape, BF16), jax.ShapeDtypeStruct((CONV_K, BRANCH_W), F32)],
        input_output_aliases={3: 0},
        compiler_params=_cp(("parallel", "arbitrary")),
    )(proj, dout, conv_w, dproj)


def _mix_fwd(oa, ob, oc, wpa, wpp, wpc, proj, b_gate, name):
    T = oa.shape[0]
    tm = min(512, T)

    def body(oa_ref, ob_ref, oc_ref, wa_ref, wp_ref, wc_ref, g_ref, bg_ref, o_ref):
        acc = jnp.zeros((tm, D_MODEL), F32)
        for i, (x_ref, w_ref) in enumerate(((oa_ref, wa_ref), (ob_ref, wp_ref), (oc_ref, wc_ref))):
            y = jnp.dot(x_ref[...], w_ref[...], preferred_element_type=F32)
            sl = slice(i * D_MODEL, (i + 1) * D_MODEL)
            acc = acc + _sigmoid(g_ref[:, sl].astype(F32) + bg_ref[:, sl]) * y
        o_ref[...] = acc.astype(BF16)

    br = pl.BlockSpec((tm, BRANCH_W), lambda i: (i, 0))
    wsp = pl.BlockSpec((BRANCH_W, D_MODEL), lambda i: (0, 0))
    return pl.pallas_call(
        body, name=name, grid=(T // tm,),
        in_specs=[br, br, br, wsp, wsp, wsp, pl.BlockSpec((tm, GATE_W), lambda i: (i, 0)),
                  pl.BlockSpec((1, GATE_W), lambda i: (0, 0))],
        out_specs=pl.BlockSpec((tm, D_MODEL), lambda i: (i, 0)),
        out_shape=jax.ShapeDtypeStruct((T, D_MODEL), BF16),
        compiler_params=_cp(("parallel",)),
    )(oa, ob, oc, wpa, wpp, wpc, proj, b_gate)


def _mix_bwd(oa, ob, oc, wpa, wpp, wpc, proj, b_gate, dmixed, name):
    T = oa.shape[0]
    tm = min(256, T)

    def body(oa_ref, ob_ref, oc_ref, wa_ref, wp_ref, wc_ref, g_ref, bg_ref, dm_ref,
             dya_ref, dyb_ref, dyc_ref, dg_ref, dbg_ref):
        i0 = pl.program_id(0)
        dm = dm_ref[...].astype(F32)
        parts = []
        for i, (x_ref, w_ref, dy_ref) in enumerate(((oa_ref, wa_ref, dya_ref), (ob_ref, wp_ref, dyb_ref),
                                                    (oc_ref, wc_ref, dyc_ref))):
            y = jnp.dot(x_ref[...], w_ref[...], preferred_element_type=F32)
            sl = slice(i * D_MODEL, (i + 1) * D_MODEL)
            gate = _sigmoid(g_ref[:, sl].astype(F32) + bg_ref[:, sl])
            dy_ref[...] = (dm * gate).astype(BF16)
            dgl = dm * y * gate * (1.0 - gate)
            dg_ref[:, sl] = dgl.astype(BF16)
            parts.append(jnp.sum(dgl, axis=0, keepdims=True))

        @pl.when(i0 == 0)
        def _():
            for i in range(3):
                dbg_ref[:, i * D_MODEL:(i + 1) * D_MODEL] = parts[i]

        @pl.when(i0 > 0)
        def _():
            for i in range(3):
                dbg_ref[:, i * D_MODEL:(i + 1) * D_MODEL] += parts[i]

    br = pl.BlockSpec((tm, BRANCH_W), lambda i: (i, 0))
    wsp = pl.BlockSpec((BRANCH_W, D_MODEL), lambda i: (0, 0))
    row = pl.BlockSpec((tm, D_MODEL), lambda i: (i, 0))
    gsp = pl.BlockSpec((tm, GATE_W), lambda i: (i, 0))
    bsp = pl.BlockSpec((1, GATE_W), lambda i: (0, 0))
    act = jax.ShapeDtypeStruct((T, D_MODEL), BF16)
    return pl.pallas_call(
        body, name=name, grid=(T // tm,),
        in_specs=[br, br, br, wsp, wsp, wsp, gsp, bsp, row],
        out_specs=[row, row, row, gsp, bsp],
        out_shape=[act, act, act, jax.ShapeDtypeStruct((T, MAIN_COLS), BF16),
                   jax.ShapeDtypeStruct((1, GATE_W), F32)],
        compiler_params=_cp(("arbitrary",)),
    )(oa, ob, oc, wpa, wpp, wpc, proj, b_gate, dmixed)


GU_TILE = 256


def _gu_col(c):
    t, r = divmod(c, GU_TILE)
    return (t // 2) * GU_TILE + r + (FFN_HIDDEN if t % 2 else 0)


def _gate_up_swiglu(h, w, name):
    T, K = h.shape
    tm = min(2048, T)

    def body(h_ref, w_ref, ab_ref, s_ref):
        prod = jnp.dot(h_ref[...], w_ref[...], preferred_element_type=F32)
        ab_ref[...] = prod.astype(BF16)
        a = prod[:, :GU_TILE]
        s_ref[...] = (a * _sigmoid(a) * prod[:, GU_TILE:]).astype(BF16)

    return pl.pallas_call(
        body, name=name, grid=(T // tm, FFN_HIDDEN // GU_TILE),
        in_specs=[pl.BlockSpec((tm, K), lambda i, j: (i, 0)), pl.BlockSpec((K, 2 * GU_TILE), lambda i, j: (0, j))],
        out_specs=[pl.BlockSpec((tm, 2 * GU_TILE), lambda i, j: (i, j)), pl.BlockSpec((tm, GU_TILE), lambda i, j: (i, j))],
        out_shape=[jax.ShapeDtypeStruct((T, 2 * FFN_HIDDEN), BF16), jax.ShapeDtypeStruct((T, FFN_HIDDEN), BF16)],
        compiler_params=_cp(("parallel", "parallel")),
    )(h, w)


def _swiglu_bwd_fused(dx, w_down, ab, name):
    T, K = dx.shape
    tm = min(2048, T)

    def body(dx_ref, w_ref, ab_ref, o_ref):
        ds = lax.dot_general(dx_ref[...], w_ref[...], _NT, preferred_element_type=F32)
        a = ab_ref[:, :GU_TILE].astype(F32)
        b = ab_ref[:, GU_TILE:].astype(F32)
        sg = _sigmoid(a)
        o_ref[:, :GU_TILE] = (ds * b * sg * (1.0 + a * (1.0 - sg))).astype(BF16)
        o_ref[:, GU_TILE:] = (ds * a * sg).astype(BF16)

    pair = pl.BlockSpec((tm, 2 * GU_TILE), lambda i, j: (i, j))
    return pl.pallas_call(
        body, name=name, grid=(T // tm, FFN_HIDDEN // GU_TILE),
        in_specs=[pl.BlockSpec((tm, K), lambda i, j: (i, 0)), pl.BlockSpec((GU_TILE, K), lambda i, j: (j, 0)), pair],
        out_specs=pair, out_shape=jax.ShapeDtypeStruct((T, 2 * FFN_HIDDEN), BF16),
        compiler_params=_cp(("parallel", "parallel")),
    )(dx, w_down, ab)


def _adamw_update(w_ref, g_ref, m_ref, v_ref, d_ref, nm_ref, nv_ref):
    gv = g_ref[...]
    nm = ADAM_B1 * m_ref[...] + (1.0 - ADAM_B1) * gv
    nv = ADAM_B2 * v_ref[...] + (1.0 - ADAM_B2) * (gv * gv)
    m_hat = nm / (1.0 - ADAM_B1 ** ADAM_STEP)
    v_hat = nv / (1.0 - ADAM_B2 ** ADAM_STEP)
    d_ref[...] = -ADAM_LR * (m_hat / (jnp.sqrt(v_hat) + ADAM_EPS) + ADAM_WD * w_ref[...])
    nm_ref[...] = nm
    nv_ref[...] = nv


def _adamw_many(ws, gs, ms, vs, name):
    n = len(ws)

    def body(*refs):
        ins, outs = refs[:4 * n], refs[4 * n:]
        for t in range(n):
            _adamw_update(ins[t], ins[n + t], ins[2 * n + t], ins[3 * n + t], outs[t], outs[n + t], outs[2 * n + t])

    shapes = [jax.ShapeDtypeStruct(w.shape, F32) for w in ws]
    out = pl.pallas_call(body, name=name, out_shape=shapes * 3, compiler_params=_cp())(*ws, *gs, *ms, *vs)
    return out[:n], out[n:2 * n], out[2 * n:]


def _adamw(w, g, m, v, name):
    R, C = w.shape
    tr = R
    for cand in (256, 352, 128, 64, 8):
        if R > cand and R % cand == 0:
            tr = cand
            break

    def body(w_ref, g_ref, m_ref, v_ref, d_ref, nm_ref, nv_ref):
        _adamw_update(w_ref, g_ref, m_ref, v_ref, d_ref, nm_ref, nv_ref)

    blk = pl.BlockSpec((tr, C), lambda i: (i, 0))
    sh = jax.ShapeDtypeStruct((R, C), F32)
    return pl.pallas_call(
        body, name=name, grid=(R // tr,), in_specs=[blk] * 4, out_specs=[blk] * 3, out_shape=[sh] * 3,
        compiler_params=_cp(("parallel",)),
    )(w, g, m, v)


def _adamw_3d(w, g, m, v, block, name):
    shape = w.shape
    grid = (shape[0] // block[0], shape[1] // block[1])
    assert shape[0] % block[0] == 0 and shape[1] % block[1] == 0 and block[2] == shape[2], (name, shape, block)

    def body(w_ref, g_ref, m_ref, v_ref, d_ref, nm_ref, nv_ref):
        _adamw_update(w_ref, g_ref, m_ref, v_ref, d_ref, nm_ref, nv_ref)

    blk = pl.BlockSpec(block, lambda i, j: (i, j, 0))
    sh = jax.ShapeDtypeStruct(shape, F32)
    return pl.pallas_call(
        body, name=name, grid=grid, in_specs=[blk] * 4, out_specs=[blk] * 3, out_shape=[sh] * 3,
        compiler_params=_cp(("parallel", "parallel")),
    )(w, g, m, v)


def _sum_slabs_t(x, name):
    n, R, C = x.shape

    def body(x_ref, o_ref):
        acc = x_ref[0].astype(F32)
        for j in range(1, n):
            acc = acc + x_ref[j].astype(F32)
        o_ref[...] = acc.T

    return pl.pallas_call(
        body, name=name, grid=(C // LANES,), in_specs=[pl.BlockSpec((n, R, LANES), lambda j: (0, 0, j))],
        out_specs=pl.BlockSpec((LANES, R), lambda j: (j, 0)), out_shape=jax.ShapeDtypeStruct((C, R), F32),
        compiler_params=_cp(("parallel",)),
    )(x)


def _sum_slabs(x, name):
    n, R, C = x.shape
    tr = R
    for cand in (512, 256, 128, 64, 32, 16, 8):
        if R > cand and R % cand == 0:
            tr = cand
            break

    def body(x_ref, o_ref):
        acc = x_ref[0].astype(F32)
        for j in range(1, n):
            acc = acc + x_ref[j].astype(F32)
        o_ref[...] = acc

    return pl.pallas_call(
        body, name=name, grid=(R // tr,), in_specs=[pl.BlockSpec((n, tr, C), lambda i: (0, i, 0))],
        out_specs=pl.BlockSpec((tr, C), lambda i: (i, 0)), out_shape=jax.ShapeDtypeStruct((R, C), F32),
        compiler_params=_cp(("parallel",)),
    )(x)


def _multi_gather(xs, layers, name):
    nt = len(xs)
    shapes = [x.shape if lay is None else x.shape[1:] for x, lay in zip(xs, layers)]

    def body(*refs):
        x_refs, out_refs = refs[:nt], refs[nt:2 * nt]
        send_sems, recv_sems, local_sems = refs[2 * nt:]
        x_, y_, c_ = lax.axis_index("x"), lax.axis_index("y"), lax.axis_index("c")
        me, sibling = (x_, y_, c_), (x_, y_, 1 - c_)
        chips = [(1 - x_, y_), (x_, 1 - y_), (1 - x_, 1 - y_)]

        def own_block(t):
            return x_refs[t] if layers[t] is None else x_refs[t].at[layers[t]]

        def copy(t, k, block, to, own=False):
            px, py, pc = block
            dst = out_refs[t].at[4 * px + 2 * py + pc]
            return pltpu.make_async_remote_copy(
                src_ref=own_block(t) if own else dst, dst_ref=dst,
                send_sem=send_sems.at[t, k], recv_sem=recv_sems.at[t, k],
                device_id=to, device_id_type=pl.DeviceIdType.MESH)

        mine, first, passed = [], [], []
        for t in range(nt):
            mine.append(pltpu.make_async_copy(own_block(t), out_refs[t].at[4 * x_ + 2 * y_ + c_], local_sems.at[t]))
            mine[-1].start()
            first.append([copy(t, 1 + j, me, (*chip, c_), own=True) for j, chip in enumerate(chips)]
                         + [copy(t, 0, me, sibling, own=True)])
            for cp in first[-1]:
                cp.start()
        for t in range(nt):
            for j, chip in enumerate(chips):
                copy(t, 1 + j, (*chip, c_), me).wait_recv()
                passed.append(copy(t, 4 + j, (*chip, c_), sibling))
                passed[-1].start()
        for t in range(nt):
            copy(t, 0, sibling, me).wait_recv()
            for j, chip in enumerate(chips):
                copy(t, 4 + j, (*chip, 1 - c_), me).wait_recv()
        for cp in [c for f in first for c in f] + passed:
            cp.wait_send()
        for cp in mine:
            cp.wait()

    hbm = pl.BlockSpec(memory_space=pl.ANY)
    return pl.pallas_call(
        body, name=name, out_shape=[jax.ShapeDtypeStruct((N_DEV,) + tuple(s), x.dtype) for s, x in zip(shapes, xs)],
        in_specs=[hbm] * nt, out_specs=[hbm] * nt,
        scratch_shapes=[pltpu.SemaphoreType.DMA((nt, 7)), pltpu.SemaphoreType.DMA((nt, 7)),
                        pltpu.SemaphoreType.DMA((nt,))],
    )(*xs)


_HBM = pl.BlockSpec(memory_space=pltpu.HBM)
_SEM = pl.BlockSpec(memory_space=pltpu.SEMAPHORE)
_PEER_ORDER = (2, 4, 6, 3, 5, 7, 1)


def _split_copies(src_refs, land_refs, send_sems, recv_sems, layers, per_peer):
    x_, y_, c_ = lax.axis_index("x"), lax.axis_index("y"), lax.axis_index("c")
    me = 4 * x_ + 2 * y_ + c_
    copies = []
    for k in _PEER_ORDER:
        px, py, pc = x_ ^ ((k >> 2) & 1), y_ ^ ((k >> 1) & 1), c_ ^ (k & 1)
        peer = 4 * px + 2 * py + pc
        for t in range(len(src_refs)):
            if per_peer:
                src = src_refs[t].at[peer]
            else:
                src = src_refs[t] if layers[t] is None else src_refs[t].at[layers[t]]
            copies.append(pltpu.make_async_remote_copy(
                src_ref=src, dst_ref=land_refs[t].at[me],
                send_sem=send_sems.at[t * (N_DEV - 1) + k - 1], recv_sem=recv_sems.at[t * (N_DEV - 1) + k - 1],
                device_id=(px, py, pc), device_id_type=pl.DeviceIdType.MESH))
    return copies


def _own_copies(src_refs, land_refs, sems, layers, per_peer):
    nt = len(src_refs)
    me = 4 * lax.axis_index("x") + 2 * lax.axis_index("y") + lax.axis_index("c")
    copies = []
    for t in range(nt):
        if per_peer:
            src = src_refs[t].at[me]
        else:
            src = src_refs[t] if layers[t] is None else src_refs[t].at[layers[t]]
        copies.append(pltpu.make_async_copy(src, land_refs[t].at[me], sems.at[nt * (N_DEV - 1) + t]))
    return copies


def _split_start(srcs, layers, per_peer, after, name):
    nt = len(srcs)
    if per_peer:
        land_shapes = [s.shape for s in srcs]
    else:
        land_shapes = [(N_DEV,) + tuple(s.shape if lay is None else s.shape[1:]) for s, lay in zip(srcs, layers)]

    def body(*refs):
        src_refs, land_refs = refs[:nt], refs[nt:2 * nt]
        send_sems, recv_sems = refs[2 * nt + 1], refs[2 * nt + 2]
        token = refs[-1]
        for cp in _split_copies(src_refs, land_refs, send_sems, recv_sems, layers, per_peer):
            cp.start()
        for cp in _own_copies(src_refs, land_refs, send_sems, layers, per_peer):
            cp.start()
        token[...] = jnp.zeros_like(token)

    lands = [pltpu.with_memory_space_constraint(lax.empty(s, x.dtype), pltpu.HBM) for s, x in zip(land_shapes, srcs)]
    srcs = [pltpu.with_memory_space_constraint(x, pltpu.HBM) for x in srcs]
    out = pl.pallas_call(
        body, name=name,
        out_shape=(pltpu.SemaphoreType.DMA((nt * N_DEV,)), pltpu.SemaphoreType.DMA((nt * (N_DEV - 1),)),
                   *[pltpu.HBM(x.shape, x.dtype) for x in srcs], *[pltpu.HBM(s, x.dtype) for s, x in zip(land_shapes, srcs)],
                   jax.ShapeDtypeStruct((8, LANES), F32)),
        in_specs=[_HBM] * (2 * nt) + [pl.BlockSpec(memory_space=pl.ANY)],
        out_specs=(_SEM, _SEM, *([_HBM] * (2 * nt)), pl.BlockSpec(memory_space=pltpu.VMEM)),
        input_output_aliases={i: 2 + i for i in range(2 * nt)},
        compiler_params=pltpu.CompilerParams(has_side_effects=pltpu.SideEffectType.DATAFLOW_SIDE_EFFECTING),
    )(*srcs, *lands, after)
    return out[0], out[1], list(out[2:2 + nt]), list(out[2 + nt:2 + 2 * nt]), out[-1]


def _split_wait(started, layers, per_peer, after, name):
    send_sems, recv_sems, srcs, lands, _ = started
    nt = len(srcs)

    def body(*refs):
        src_refs, land_refs = refs[:nt], refs[nt:2 * nt]
        s_sems, r_sems = refs[2 * nt], refs[2 * nt + 1]
        for cp in _split_copies(src_refs, land_refs, s_sems, r_sems, layers, per_peer):
            cp.wait_send()
            cp.wait_recv()
        for cp in _own_copies(src_refs, land_refs, s_sems, layers, per_peer):
            cp.wait()

    out = pl.pallas_call(
        body, name=name,
        out_shape=tuple(pltpu.HBM(x.shape, x.dtype) for x in srcs + lands),
        in_specs=[_HBM] * (2 * nt) + [_SEM, _SEM, pl.BlockSpec(memory_space=pl.ANY)],
        out_specs=tuple([_HBM] * (2 * nt)),
        input_output_aliases={i: i for i in range(2 * nt)},
        compiler_params=pltpu.CompilerParams(has_side_effects=pltpu.SideEffectType.DATAFLOW_SIDE_EFFECTING),
    )(*srcs, *lands, send_sems, recv_sems, after)
    return list(out[nt:])


def _runs(mapping):
    runs, c, n = [], 0, len(mapping)
    while c < n:
        if mapping[c] is None:
            c += 1
            continue
        sid, d, lo = mapping[c][0], mapping[c][1] - c, c
        while c < n and mapping[c] is not None and mapping[c][0] == sid and mapping[c][1] - c == d:
            c += 1
        runs.append((lo, c, sid, d))
    return runs


def _tile_plan(mapping, src_widths):
    runs = _runs(mapping)
    plan = []
    for t in range(len(mapping) // LANES):
        pieces = []
        for lo, hi, sid, d in runs:
            lo_t, hi_t = max(lo, t * LANES), min(hi, (t + 1) * LANES)
            if lo_t >= hi_t:
                continue
            a = ((lo_t + d) // LANES) * LANES
            win = min(2 * LANES, src_widths[sid] - a)
            shift = t * LANES + d - a
            pieces.append((sid, a, win, shift, lo_t - t * LANES, hi_t - t * LANES))
        plan.append(pieces)
    return plan


def _reblock(srcs, src_views, outs, out_views, name):
    R = srcs[0].shape[-2]
    tr = min(512, R)
    widths = {sid: srcs[ai].shape[-1] for sid, (ai, _) in src_views.items()}
    plans = [(ai, li, _tile_plan(mapping, widths)) for ai, li, mapping in out_views]
    ns = len(srcs)

    def body(*refs):
        s_refs, o_refs = refs[:ns], refs[ns:]
        cache = {}

        def shift_matrix(win, shift, lo, hi):
            key = (win, shift, lo, hi)
            if key not in cache:
                r = lax.broadcasted_iota(jnp.int32, (win, LANES), 0)
                c = lax.broadcasted_iota(jnp.int32, (win, LANES), 1)
                hit = jnp.logical_and(r - c == shift, jnp.logical_and(c >= lo, c < hi))
                cache[key] = jnp.where(hit, 1.0, 0.0).astype(BF16)
            return cache[key]

        for ai, li, plan in plans:
            for t, pieces in enumerate(plan):
                acc = None
                whole = len(pieces) == 1 and pieces[0][3:] == (0, 0, LANES)
                for sid, a, win, shift, lo, hi in pieces:
                    sa, sl = src_views[sid]
                    if whole:
                        win = LANES
                    src = s_refs[sa][:, a:a + win] if sl is None else s_refs[sa][sl, :, a:a + win]
                    if whole:
                        acc = src
                    else:
                        part = jnp.dot(src, shift_matrix(win, shift, lo, hi), preferred_element_type=F32)
                        acc = part if acc is None else acc + part
                val = jnp.zeros((tr, LANES), BF16) if acc is None else acc.astype(BF16)
                if li is None:
                    o_refs[ai][:, t * LANES:(t + 1) * LANES] = val
                else:
                    o_refs[ai][li, :, t * LANES:(t + 1) * LANES] = val

    def spec(shape):
        if len(shape) == 2:
            return pl.BlockSpec((tr, shape[1]), lambda i: (i, 0))
        return pl.BlockSpec((shape[0], tr, shape[2]), lambda i: (0, i, 0))

    return pl.pallas_call(
        body, name=name, grid=(R // tr,), in_specs=[spec(s.shape) for s in srcs],
        out_specs=[spec(s) for s in outs], out_shape=[jax.ShapeDtypeStruct(s, BF16) for s in outs],
        compiler_params=_cp(("parallel",)),
    )(*srcs)


SHARDED = ("w_in", "w_gate_up", "w_proj_attn", "w_proj_pool", "w_proj_conv", "w_out", "w_down")
WEIGHT_ORDER = ("attn_norm", "w_in", "b_forget", "b_gate", "w_proj_attn", "pool_w", "pool_scale", "w_proj_pool",
                "conv_w", "w_proj_conv", "w_out", "ffn_norm", "w_gate_up", "w_down", "final_norm")
IN_SHARD, IN_SHARD_PAD = IN_COLS // N_DEV, 896
GU_SHARD, GU_SHARD_PAD = 2 * FFN_HIDDEN // N_DEV, 768


def _w_in_col(c):
    if c < OFF_QKV:
        return c + 3592
    if c < OFF_U:
        base, off = (0, OFF_QKV) if c < OFF_CONV else (2056, OFF_CONV)
        j, t = divmod(c - off, TRIPLE)
        which, e = divmod(t, LANES)
        return base + which * BRANCH_W + j * LANES + e
    return c - OFF_U + 1544


def _w_in_full(gathered, name):
    main = [divmod(_w_in_col(c), IN_SHARD) for c in range(MAIN_COLS)]
    fcols = [divmod(1536 + c, IN_SHARD) if c < N_HEADS else None for c in range(LANES)]
    R = gathered.shape[1]
    return _reblock([gathered], {i: (0, i) for i in range(N_DEV)}, [(R, MAIN_COLS), (R, LANES)],
                    [(0, None, main), (1, None, fcols)], name)


def _w_in_slabs(dmain, dwf, name):
    inv = {_w_in_col(c): ("m", c) for c in range(MAIN_COLS)}
    inv.update({1536 + c: ("f", c) for c in range(N_HEADS)})
    views = []
    for i in range(N_DEV):
        mapping = [inv[IN_SHARD * i + j] if j < IN_SHARD else None for j in range(IN_SHARD_PAD)]
        views.append((0, i, mapping))
    R = dmain.shape[0]
    return _reblock([dmain, dwf], {"m": (0, None), "f": (1, None)}, [(N_DEV, R, IN_SHARD_PAD)], views, name)[0]


def _w_gu_full(gathered, name):
    mapping = [divmod(_gu_col(c), GU_SHARD) for c in range(2 * FFN_HIDDEN)]
    R = gathered.shape[1]
    return _reblock([gathered], {i: (0, i) for i in range(N_DEV)}, [(R, 2 * FFN_HIDDEN)], [(0, None, mapping)], name)[0]


def _w_gu_slabs(dw, name):
    inv = {_gu_col(c): c for c in range(2 * FFN_HIDDEN)}
    views = [(0, i, [("w", inv[GU_SHARD * i + j]) if j < GU_SHARD else None for j in range(GU_SHARD_PAD)])
             for i in range(N_DEV)]
    R = dw.shape[0]
    return _reblock([dw], {"w": (0, None)}, [(N_DEV, R, GU_SHARD_PAD)], views, name)[0]


def _layer_fwd(x, W, n_seq, l, h1=None, next_norm=None):
    T = x.shape[0]
    sfx = f"_l{l}"
    if h1 is None:
        h1 = _rms_fwd(x, W["attn_norm"], "rms1" + sfx)
    proj, f = _matmul(h1, W["w_main"], mode="nn", out_dtype=BF16, name="proj_main" + sfx, side=(W["w_f"], F32))
    qa, ka, va = _fox_prep(f, W["b_forget"], proj, n_seq, "fox_prep" + sfx)
    oa, oa32, lse = _attn_fwd2(qa, ka, va, n_seq, "attn_fwd" + sfx)
    if "late" in W:
        W.update(W.pop("late")(oa))
    ob = _pool_fwd(proj, W["pool_w"], W["pool_scale"], n_seq, "pool_fwd" + sfx)
    oc = _conv_fwd(proj, W["conv_w"], n_seq, "conv_fwd" + sfx)
    mixed = _mix_fwd(oa, ob, oc, W["w_proj_attn"], W["w_proj_pool"], W["w_proj_conv"], proj, W["b_gate"],
                     "mix_fwd" + sfx)
    x2, h2 = _matmul(mixed, W["w_out"], mode="nn", out_dtype=F32, name="out_proj" + sfx, tm=1024, tn=1024,
                     residual=x, rms_g=W["ffn_norm"])
    ab, s = _gate_up_swiglu(h2, W["w_gate_up"], "gate_up" + sfx)
    x3 = _matmul(s, W["w_down"], mode="nn", out_dtype=F32, name="down" + sfx, tm=1024, tn=1024, tk=1408,
                 residual=x2, rms_g=next_norm)
    x3, h1_next = x3 if next_norm is not None else (x3, None)
    saved = dict(x=x, h1=h1, proj=proj, f=f, qa=qa, ka=ka, oa=oa, oa32=oa32, lse=lse, ob=ob, oc=oc, mixed=mixed, x2=x2,
                 h2=h2, ab=ab, s=s)
    return x3, saved, h1_next


def _layer_bwd(dx3, dx3b, W, sv, n_seq, l, stage=None):
    T = dx3.shape[0]
    sfx = f"_l{l}"
    G = {}
    stage = stage or (lambda l, group, G, W: W)
    dab = _swiglu_bwd_fused(dx3b, W["w_down"], sv["ab"], "d_ab" + sfx)
    G["w_down"] = _matmul(sv["s"], dx3b, mode="tn", out_dtype=BF16, name="dw_down" + sfx, tm=256, tn=1024)
    dh2 = _matmul(dab, W["w_gate_up"], mode="nt", out_dtype=BF16, name="d_h2" + sfx, tm=1024, tn=1024, tk=1408)
    G["w_gate_up"] = _matmul(sv["h2"], dab, mode="tn", out_dtype=BF16, name="dw_gate_up" + sfx, tm=1024)
    W = stage(l, "ffn", G, W)
    dx2, dx2b, G["ffn_norm"] = _rms_bwd(sv["x2"], W["ffn_norm"], dh2, dx3, "rms2_bwd" + sfx)
    dmixed = _matmul(dx2b, W["w_out"], mode="nt", out_dtype=BF16, name="d_mixed" + sfx)
    G["w_out"] = _matmul(sv["mixed"], dx2b, mode="tn", out_dtype=BF16, name="dw_out" + sfx, tm=1024)
    dya, dyb, dyc, dproj, G["b_gate"] = _mix_bwd(sv["oa"], sv["ob"], sv["oc"], W["w_proj_attn"], W["w_proj_pool"],
                                                 W["w_proj_conv"], sv["proj"], W["b_gate"], dmixed, "mix_bwd" + sfx)
    douts = {}
    for br, dy, o in (("attn", dya, sv["oa"]), ("pool", dyb, sv["ob"]), ("conv", dyc, sv["oc"])):
        douts[br] = _matmul(dy, W["w_proj_" + br], mode="nt", out_dtype=BF16, name=f"d_{br}_out" + sfx)
        G["w_proj_" + br] = _matmul(o, dy, mode="tn", out_dtype=BF16, name=f"dw_proj_{br}" + sfx, tm=512)
    W = stage(l, "mix", G, W)
    dproj, G["conv_w"] = _conv_bwd(sv["proj"], douts["conv"], W["conv_w"], dproj, n_seq, "conv_bwd" + sfx)
    dproj, G["pool_w"], G["pool_scale"] = _pool_bwd(sv["proj"], douts["pool"], W["pool_w"], W["pool_scale"], dproj,
                                                    n_seq, "pool_bwd" + sfx)
    dproj, dFk = _attn_bwd(sv["qa"], sv["ka"], sv["proj"], douts["attn"], sv["oa32"], sv["lse"], dproj, n_seq,
                           "attn_bwd" + sfx)
    dF = jnp.pad(dFk.reshape(N_HEADS, T).T, ((0, 0), (0, LANES - N_HEADS)))
    df, G["b_forget"] = _fox_cumsum_bwd(sv["f"], W["b_forget"], dF, n_seq, "fox_cumsum_bwd" + sfx)
    G["w_main"], G["w_f"] = _matmul(sv["h1"], dproj, mode="tn", out_dtype=BF16, name="dw_main" + sfx, tm=1024,
                                    side=(df, BF16))
    W = stage(l, "w_in", G, W)
    dh1 = _matmul(dproj, W["w_main"], mode="nt", out_dtype=BF16, name="d_h1_main" + sfx, tm=1024, tn=1024, tk=1664,
                  extra=(df, W["w_f"]))
    dx, dxb, G["attn_norm"] = _rms_bwd(sv["x"], W["attn_norm"], dh1, dx2, "rms1_bwd" + sfx)
    return dx, dxb, G


def _replicated_operands(rep, l):
    W = {}
    W["attn_norm"], W["ffn_norm"] = rep["attn_norm"][l], rep["ffn_norm"][l]
    W["b_forget"] = jnp.pad(rep["b_forget"][l].reshape(1, N_HEADS), ((0, 0), (0, LANES - N_HEADS)))
    W["b_gate"] = rep["b_gate"][l].reshape(1, GATE_W)
    W["pool_w"] = rep["pool_w"][l].astype(BF16)
    W["pool_scale"] = rep["pool_scale"][l].reshape(1, BRANCH_W)
    return W


def _local_step(x, target, get_W, attn_norms, final_norm, stage=None):
    n_seq, S, Dm = x.shape
    T = n_seq * S
    xt = x.reshape(T, Dm)
    saved, Ws, h1 = [], [], None
    for l in range(DEPTH):
        Ws.append(get_W(l, xt))
        next_norm = attn_norms[l + 1] if l + 1 < DEPTH else None
        xt, sv, h1 = _layer_fwd(xt, Ws[l], n_seq, l, h1, next_norm)
        saved.append(sv)
    loss, dx, dxb, g_final = _loss_head(xt, final_norm, target.reshape(T, Dm), "loss_head")
    grads = [None] * DEPTH
    for l in reversed(range(DEPTH)):
        dx, dxb, grads[l] = _layer_bwd(dx, dxb, Ws[l], saved[l], n_seq, l, stage)
    return loss, dx.reshape(n_seq, S, Dm), grads, g_final


def _padded_shards(weights):
    pads = {"w_in": IN_SHARD_PAD - IN_SHARD, "w_gate_up": GU_SHARD_PAD - GU_SHARD}
    return {n: jnp.pad(weights[n], ((0, 0), (0, 0), (0, pads.get(n, 0)))).astype(BF16) for n in SHARDED}


def _full_operands(g, l):
    W = {}
    if "w_in" in g:
        W["w_main"], W["w_f"] = _w_in_full(g["w_in"], f"w_in_full_l{l}")
    if "w_gate_up" in g:
        W["w_gate_up"] = _w_gu_full(g["w_gate_up"], f"w_gate_up_full_l{l}")
    for n in ("w_proj_attn", "w_proj_pool", "w_proj_conv"):
        if n in g:
            W[n] = jnp.transpose(g[n], (1, 0, 2)).reshape(BRANCH_W, D_MODEL)
    if "w_out" in g:
        W["w_out"] = g["w_out"].reshape(D_MODEL, D_MODEL)
    if "w_down" in g:
        W["w_down"] = g["w_down"].reshape(FFN_HIDDEN, D_MODEL)
    return W


GRAD_GROUPS = {"ffn": ("w_down", "w_gate_up"),
               "mix": ("w_out", "w_proj_attn", "w_proj_pool", "w_proj_conv"),
               "w_in": ("w_in",)}


def _grad_slabs(G, n, l):
    if n == "w_in":
        return _w_in_slabs(G["w_main"], G["w_f"], f"w_in_slabs_l{l}")
    if n == "w_gate_up":
        return _w_gu_slabs(G["w_gate_up"], f"w_gate_up_slabs_l{l}")
    if n == "w_out":
        return G["w_out"].reshape(N_DEV, D_MODEL // N_DEV, D_MODEL)
    if n == "w_down":
        return G["w_down"].reshape(N_DEV, FFN_HIDDEN // N_DEV, D_MODEL)
    return jnp.transpose(G[n].reshape(BRANCH_W, N_DEV, D_MODEL // N_DEV), (1, 0, 2))


def _sum_layer_grads(recv, l):
    out = {}
    for n, r in recv.items():
        if n in ("w_in", "w_gate_up"):
            out[n] = _sum_slabs_t(r, f"sum_{n}_l{l}")[:IN_SHARD if n == "w_in" else GU_SHARD]
        else:
            out[n] = _sum_slabs(r, f"sum_{n}_l{l}")
    return out


def _sum_small(xs, name):
    def body(*refs):
        for x_ref, o_ref in zip(refs[:len(xs)], refs[len(xs):]):
            acc = x_ref[0]
            for j in range(1, N_DEV):
                acc = acc + x_ref[j]
            o_ref[...] = acc

    return pl.pallas_call(
        body, name=name, out_shape=[jax.ShapeDtypeStruct(x.shape[1:], F32) for x in xs],
        compiler_params=_cp(),
    )(*xs)


def _as_2d(a):
    if a.ndim == 1:
        return a.reshape(1, -1)
    return a.reshape(-1, a.shape[-1])


def kernel(x, attn_norm, w_in, b_forget, b_gate, w_proj_attn, pool_w, pool_scale, w_proj_pool, conv_w, w_proj_conv, w_out, ffn_norm, w_gate_up, w_down, final_norm, loss_target, m_attn_norm, m_w_in, m_b_forget, m_b_gate, m_w_proj_attn, m_pool_w, m_pool_scale, m_w_proj_pool, m_conv_w, m_w_proj_conv, m_w_out, m_ffn_norm, m_w_gate_up, m_w_down, m_final_norm, v_attn_norm, v_w_in, v_b_forget, v_b_gate, v_w_proj_attn, v_pool_w, v_pool_scale, v_w_proj_pool, v_conv_w, v_w_proj_conv, v_w_out, v_ffn_norm, v_w_gate_up, v_w_down, v_final_norm):
    weights = dict(attn_norm=attn_norm, w_in=w_in, b_forget=b_forget, b_gate=b_gate, w_proj_attn=w_proj_attn,
                   pool_w=pool_w, pool_scale=pool_scale, w_proj_pool=w_proj_pool, conv_w=conv_w,
                   w_proj_conv=w_proj_conv, w_out=w_out, ffn_norm=ffn_norm, w_gate_up=w_gate_up, w_down=w_down,
                   final_norm=final_norm)
    moments_m = dict(attn_norm=m_attn_norm, w_in=m_w_in, b_forget=m_b_forget, b_gate=m_b_gate,
                     w_proj_attn=m_w_proj_attn, pool_w=m_pool_w, pool_scale=m_pool_scale, w_proj_pool=m_w_proj_pool,
                     conv_w=m_conv_w, w_proj_conv=m_w_proj_conv, w_out=m_w_out, ffn_norm=m_ffn_norm,
                     w_gate_up=m_w_gate_up, w_down=m_w_down, final_norm=m_final_norm)
    moments_v = dict(attn_norm=v_attn_norm, w_in=v_w_in, b_forget=v_b_forget, b_gate=v_b_gate,
                     w_proj_attn=v_w_proj_attn, pool_w=v_pool_w, pool_scale=v_pool_scale, w_proj_pool=v_w_proj_pool,
                     conv_w=v_conv_w, w_proj_conv=v_w_proj_conv, w_out=v_w_out, ffn_norm=v_ffn_norm,
                     w_gate_up=v_w_gate_up, w_down=v_w_down, final_norm=v_final_norm)

    sh = _padded_shards(weights)
    names = list(SHARDED)
    rest = [n for n in names if n != "w_in"]
    me = 4 * lax.axis_index("x") + 2 * lax.axis_index("y") + lax.axis_index("c")
    w_in0, conv_all = _multi_gather([sh["w_in"], conv_w], [0, None], "gather_w_in_l0")
    started, after = {}, w_in0
    for l in range(DEPTH):
        for group, gnames in (("w_in", ["w_in"]), ("rest", rest)):
            if (l, group) != (0, "w_in"):
                started[l, group] = _split_start([sh[n] for n in gnames], [l] * len(gnames), False, after,
                                                 f"gather_start_{group}_l{l}")
                after = started[l, group][4]
    last_token = after

    def get_W(l, xt):
        if l == 0:
            w_in = w_in0
        else:
            w_in = _split_wait(started[l, "w_in"], [l], False, xt, f"gather_wait_w_in_l{l}")[0]
        W = _full_operands({"w_in": w_in}, l)

        def late(after):
            lands = _split_wait(started[l, "rest"], [l] * len(rest), False, after, f"gather_wait_rest_l{l}")
            return _full_operands(dict(zip(rest, lands)), l)

        W["late"] = late
        W.update(_replicated_operands(weights, l))
        W["conv_w"] = jnp.transpose(conv_all[:, l], (1, 0, 2)).reshape(CONV_K, BRANCH_W)
        if l == 0:
            W["attn_norm"] = W["attn_norm"] + last_token[0, 0]
        return W

    exchanges = []

    def stage(l, group, G, W):
        gnames = GRAD_GROUPS[group]
        slabs = [_grad_slabs(G, n, l) for n in gnames]
        started = _split_start(slabs, None, True, slabs[0][0, :8], f"exchange_start_{group}_l{l}")
        exchanges.append((l, group, gnames, slabs, started))
        tie = {"ffn": "ffn_norm", "mix": "conv_w", "w_in": "w_f"}[group]
        W = dict(W)
        W[tie] = W[tie] + started[4][0, 0].astype(W[tie].dtype)
        return W

    loss_part, grad_x, grads, g_final = _local_step(x, loss_target, get_W, attn_norm, final_norm, stage)

    def finish_exchange(ex, after):
        l, group, gnames, slabs, started = ex
        lands = _split_wait(started, None, True, after, f"exchange_wait_{group}_l{l}")
        grads[l].update(_sum_layer_grads(dict(zip(gnames, lands)), l))

    def zero_after(a):
        return jnp.minimum(jnp.abs(a[(0,) * a.ndim]), 0.0)

    *early, last_exchange = exchanges
    for ex in early:
        finish_exchange(ex, grad_x)
    deltas, new_m, new_v, gw = {}, {}, {}, {}
    views = {"w_in": ((2, 0, 1), (1, 2, 0), (49, DEPTH, D_MODEL), 1),
             "w_gate_up": ((0, 2, 1), (0, 2, 1), (1, GU_SHARD // 2, D_MODEL), 0)}

    def update_sharded(n):
        if n in views:
            perm, inv, block, layer_axis = views[n]
            gt = jnp.stack([grads[l][n] for l in range(DEPTH)], axis=layer_axis)
            d, nm, nv = _adamw_3d(jnp.transpose(weights[n], perm), gt, jnp.transpose(moments_m[n], perm),
                                  jnp.transpose(moments_v[n], perm), block, "adamw_" + n)
            deltas[n], new_m[n], new_v[n] = (jnp.transpose(a, inv) for a in (d, nm, nv))
            gw[n] = jnp.transpose(gt, inv)
            return
        gw[n] = jnp.stack([grads[l][n] for l in range(DEPTH)])
        shape = weights[n].shape
        d, nm, nv = _adamw(_as_2d(weights[n]), _as_2d(gw[n]), _as_2d(moments_m[n]), _as_2d(moments_v[n]),
                           "adamw_" + n)
        deltas[n], new_m[n], new_v[n] = d.reshape(shape), nm.reshape(shape), nv.reshape(shape)

    for n in SHARDED:
        if n != "w_in":
            update_sharded(n)

    small = ("attn_norm", "b_forget", "b_gate", "pool_w", "pool_scale", "ffn_norm", "conv_w")
    loss_part = loss_part + zero_after(deltas["w_gate_up"]) + zero_after(deltas["w_down"])
    parts = [jnp.stack([grads[l][n] for l in range(DEPTH)]) for n in small] + [g_final, loss_part]
    gathered = _multi_gather(parts, [None] * len(parts), "gather_small_grads")
    summed = _sum_small(gathered, "sum_small_grads")
    for n, s in zip(small, summed):
        gw[n] = s
    gw["attn_norm"], gw["ffn_norm"] = gw["attn_norm"][:, 0], gw["ffn_norm"][:, 0]
    gw["b_forget"] = gw["b_forget"][:, 0, :N_HEADS]
    gw["b_gate"], gw["pool_scale"] = gw["b_gate"][:, 0], gw["pool_scale"][:, 0]
    gw["conv_w"] = lax.dynamic_slice_in_dim(gw["conv_w"], me * (BRANCH_W // N_DEV), BRANCH_W // N_DEV, axis=2)
    gw["final_norm"] = summed[-2][0]
    loss = summed[-1][0, 0]

    rest_names = [n for n in WEIGHT_ORDER if n not in SHARDED]
    ds, nms, nvs = _adamw_many(*[[_as_2d(src[n]) for n in rest_names] for src in (weights, gw, moments_m, moments_v)],
                               "adamw_small")
    for n, d, nm, nv in zip(rest_names, ds, nms, nvs):
        shape = weights[n].shape
        deltas[n], new_m[n], new_v[n] = d.reshape(shape), nm.reshape(shape), nv.reshape(shape)

    finish_exchange(last_exchange, deltas["pool_w"])
    update_sharded("w_in")

    return (loss, grad_x, *[gw[n] for n in WEIGHT_ORDER], *[deltas[n] for n in WEIGHT_ORDER],
            *[new_m[n] for n in WEIGHT_ORDER], *[new_v[n] for n in WEIGHT_ORDER])
```

```python
import functools

import jax
import jax.numpy as jnp
from jax import lax
from jax.experimental import pallas as pl
from jax.experimental.pallas import tpu as pltpu

F32 = jnp.float32
BF16 = jnp.bfloat16

N_DEV = 8
D_MODEL = 1024
DEPTH = 2
N_HEADS = 8
HEAD_DIM = 64
BRANCH_W = 512
POOL_WINDOWS = (2, 4, 8, 16)
POOL_GD = 128
CONV_K = 3
FFN_HIDDEN = 2816
GATE_W = 3 * D_MODEL
IN_COLS = 6664
MAIN_COLS = GATE_W + 7 * BRANCH_W
RMS_EPS = 1e-6
NEG_INF = -1e30

ADAM_LR = 0.001
ADAM_B1 = 0.9
ADAM_B2 = 0.999
ADAM_EPS = 1e-08
ADAM_WD = 0.01
ADAM_STEP = 10

LANES = 128
VMEM_LIMIT = 56 * 1024 * 1024
CUM_BLK = 256

TRIPLE = 3 * LANES
OFF_G, OFF_QKV, OFF_CONV, OFF_U = 0, 3072, 4608, 6144


def _cp(sem=None):
    return pltpu.CompilerParams(dimension_semantics=sem, vmem_limit_bytes=VMEM_LIMIT)


def _sigmoid(z):
    return 1.0 / (1.0 + jnp.exp(-z))


def _matmul(a, b, *, mode, out_dtype, name, tm=2048, tn=512, tk=None, residual=None, rms_g=None, side=None,
            extra=None):
    if mode == "nn":
        (M, K), N = a.shape, b.shape[1]
    elif mode == "nt":
        (M, K), N = a.shape, b.shape[0]
    else:
        (K, M), N = a.shape, b.shape[1]
    tm, tn, tk = min(tm, M), min(tn, N), K if tk is None else min(tk, K)
    assert M % tm == 0 and N % tn == 0 and K % tk == 0, (name, M, N, K, tm, tn, tk)
    nk = K // tk
    if mode == "nn":
        a_spec = pl.BlockSpec((tm, tk), lambda i, j, k: (i, k))
        b_spec = pl.BlockSpec((tk, tn), lambda i, j, k: (k, j))
        dims = (((1,), (0,)), ((), ()))
    elif mode == "nt":
        a_spec = pl.BlockSpec((tm, tk), lambda i, j, k: (i, k))
        b_spec = pl.BlockSpec((tn, tk), lambda i, j, k: (j, k))
        dims = (((1,), (1,)), ((), ()))
    else:
        a_spec = pl.BlockSpec((tk, tm), lambda i, j, k: (k, i))
        b_spec = pl.BlockSpec((tk, tn), lambda i, j, k: (k, j))
        dims = (((0,), (0,)), ((), ()))
    o_spec = pl.BlockSpec((tm, tn), lambda i, j, k: (i, j))
    has_res, has_norm, has_side, has_extra = (v is not None for v in (residual, rms_g, side, extra))
    assert not has_norm or tn == N, (name, tn, N)
    assert not has_side or (nk == 1 and mode != "nt"), name

    in_specs, args = [a_spec, b_spec], [a, b]
    out_specs, out_shape = [o_spec], [jax.ShapeDtypeStruct((M, N), out_dtype)]
    if has_res:
        in_specs.append(o_spec)
        args.append(residual)
    if has_norm:
        in_specs.append(pl.BlockSpec((1, N), lambda i, j, k: (0, 0)))
        args.append(rms_g.reshape(1, N))
        out_specs.append(o_spec)
        out_shape.append(jax.ShapeDtypeStruct((M, N), BF16))
    if has_side:
        b_side, side_dtype = side
        ns = b_side.shape[1]
        in_specs.append(pl.BlockSpec((K, ns), lambda i, j, k: (0, 0)))
        args.append(b_side)
        out_specs.append(pl.BlockSpec((tm, ns), lambda i, j, k: (i, 0)))
        out_shape.append(jax.ShapeDtypeStruct((M, ns), side_dtype))
    if has_extra:
        a2, b2 = extra
        in_specs += [pl.BlockSpec((tm, a2.shape[1]), lambda i, j, k: (i, 0)),
                     pl.BlockSpec((tn, b2.shape[1]), lambda i, j, k: (j, 0))]
        args += [a2, b2]
    n_in = len(args)

    def body(*refs):
        ins, outs = list(refs[2:n_in]), list(refs[n_in:n_in + len(out_shape)])
        a_ref, b_ref = refs[:2]
        r_ref = ins.pop(0) if has_res else None
        g_ref = ins.pop(0) if has_norm else None
        bs_ref = ins.pop(0) if has_side else None
        a2_ref, b2_ref = (ins.pop(0), ins.pop(0)) if has_extra else (None, None)
        o_ref = outs.pop(0)
        h_ref = outs.pop(0) if has_norm else None
        so_ref = outs.pop(0) if has_side else None

        def finish(acc):
            if has_res:
                acc = acc + r_ref[...].astype(F32)
            if has_extra:
                acc = acc + lax.dot_general(a2_ref[...], b2_ref[...], (((1,), (1,)), ((), ())),
                                            preferred_element_type=F32)
            o_ref[...] = acc.astype(out_dtype)
            if has_norm:
                r = lax.rsqrt(jnp.mean(acc * acc, axis=-1, keepdims=True) + RMS_EPS)
                h_ref[...] = ((acc * r) * g_ref[...]).astype(BF16)

        if has_side:
            @pl.when(pl.program_id(1) == 0)
            def _():
                side_dims = (((1,), (0,)), ((), ())) if mode == "nn" else dims
                so_ref[...] = lax.dot_general(a_ref[...], bs_ref[...], side_dims,
                                              preferred_element_type=F32).astype(so_ref.dtype)

        prod = lax.dot_general(a_ref[...], b_ref[...], dims, preferred_element_type=F32)
        if nk == 1:
            finish(prod)
            return
        acc_ref = refs[-1]
        k = pl.program_id(2)

        @pl.when(k == 0)
        def _():
            acc_ref[...] = prod

        @pl.when(jnp.logical_and(k > 0, k < nk - 1))
        def _():
            acc_ref[...] += prod

        @pl.when(k == nk - 1)
        def _():
            finish(acc_ref[...] + prod)

    single = len(out_shape) == 1
    return pl.pallas_call(
        body, name=name, grid=(M // tm, N // tn, nk), in_specs=in_specs,
        out_specs=out_specs[0] if single else out_specs, out_shape=out_shape[0] if single else out_shape,
        scratch_shapes=[pltpu.VMEM((tm, tn), F32)] if nk > 1 else [],
        compiler_params=_cp(("parallel", "arbitrary" if has_side else "parallel", "arbitrary")),
    )(*args)


def _rms_fwd(x, g, name):
    T, Dm = x.shape
    tm = min(512, T)

    def body(x_ref, g_ref, h_ref):
        xf = x_ref[...]
        r = lax.rsqrt(jnp.mean(xf * xf, axis=-1, keepdims=True) + RMS_EPS)
        h_ref[...] = ((xf * r) * g_ref[...]).astype(BF16)

    return pl.pallas_call(
        body, name=name, grid=(T // tm,),
        in_specs=[pl.BlockSpec((tm, Dm), lambda i: (i, 0)), pl.BlockSpec((1, Dm), lambda i: (0, 0))],
        out_specs=pl.BlockSpec((tm, Dm), lambda i: (i, 0)),
        out_shape=jax.ShapeDtypeStruct((T, Dm), BF16),
        compiler_params=_cp(("parallel",)),
    )(x, g.reshape(1, Dm))


def _rms_bwd(x, g, dh, dres, name):
    T, Dm = x.shape
    tm = min(512, T)

    def body(x_ref, g_ref, dh_ref, dres_ref, dx_ref, dxb_ref, dg_ref):
        i = pl.program_id(0)
        xf = x_ref[...]
        r = lax.rsqrt(jnp.mean(xf * xf, axis=-1, keepdims=True) + RMS_EPS)
        xn = xf * r
        dhf = dh_ref[...].astype(F32)
        dxn = dhf * g_ref[...]
        c = jnp.mean(dxn * xn, axis=-1, keepdims=True)
        dx = dres_ref[...] + r * (dxn - xn * c)
        dx_ref[...] = dx
        dxb_ref[...] = dx.astype(BF16)
        part = jnp.sum(dhf * xn, axis=0, keepdims=True)

        @pl.when(i == 0)
        def _():
            dg_ref[...] = part

        @pl.when(i > 0)
        def _():
            dg_ref[...] += part

    row = pl.BlockSpec((tm, Dm), lambda i: (i, 0))
    vec = pl.BlockSpec((1, Dm), lambda i: (0, 0))
    return pl.pallas_call(
        body, name=name, grid=(T // tm,), in_specs=[row, vec, row, row], out_specs=[row, row, vec],
        out_shape=[jax.ShapeDtypeStruct((T, Dm), F32), jax.ShapeDtypeStruct((T, Dm), BF16),
                   jax.ShapeDtypeStruct((1, Dm), F32)],
        compiler_params=_cp(("arbitrary",)),
    )(x, g.reshape(1, Dm), dh, dres)


def _loss_head(x, g, target, name):
    T, Dm = x.shape
    tm = min(512, T)

    def body(x_ref, g_ref, t_ref, loss_ref, dx_ref, dxb_ref, dg_ref):
        i = pl.program_id(0)
        xf = x_ref[...]
        gv = g_ref[...]
        r = lax.rsqrt(jnp.mean(xf * xf, axis=-1, keepdims=True) + RMS_EPS)
        xn = xf * r
        diff = xn * gv - t_ref[...]
        per_tok = jnp.mean(diff * diff, axis=-1, keepdims=True)
        lpart = 0.5 * jnp.sum(per_tok, axis=0, keepdims=True) + jnp.zeros((1, LANES), F32)
        dy = diff * (1.0 / Dm)
        dxn = dy * gv
        c = jnp.mean(dxn * xn, axis=-1, keepdims=True)
        dx = r * (dxn - xn * c)
        dx_ref[...] = dx
        dxb_ref[...] = dx.astype(BF16)
        part = jnp.sum(dy * xn, axis=0, keepdims=True)

        @pl.when(i == 0)
        def _():
            dg_ref[...] = part
            loss_ref[...] = lpart

        @pl.when(i > 0)
        def _():
            dg_ref[...] += part
            loss_ref[...] += lpart

    row = pl.BlockSpec((tm, Dm), lambda i: (i, 0))
    vec = pl.BlockSpec((1, Dm), lambda i: (0, 0))
    lsp = pl.BlockSpec((1, LANES), lambda i: (0, 0))
    return pl.pallas_call(
        body, name=name, grid=(T // tm,), in_specs=[row, vec, row], out_specs=[lsp, row, row, vec],
        out_shape=[jax.ShapeDtypeStruct((1, LANES), F32), jax.ShapeDtypeStruct((T, Dm), F32),
                   jax.ShapeDtypeStruct((T, Dm), BF16), jax.ShapeDtypeStruct((1, Dm), F32)],
        compiler_params=_cp(("arbitrary",)),
    )(x, g.reshape(1, Dm), target)


def _split_bf16(v):
    hi = v.astype(BF16)
    r1 = v - hi.astype(F32)
    mid = r1.astype(BF16)
    lo = (r1 - mid.astype(F32)).astype(BF16)
    return hi, mid, lo


def _tri_dot(tri, v):
    hi, mid, lo = _split_bf16(v)
    dot = functools.partial(jnp.dot, preferred_element_type=F32)
    return dot(tri, hi) + dot(tri, mid) + dot(tri, lo)


def _log_sigmoid(z):
    return jnp.minimum(z, 0.0) - jnp.log(1.0 + jnp.exp(-jnp.abs(z)))


def _fox_cumsum_bwd(f, bf, dF, n_seq, name):
    T = f.shape[0]
    S = T // n_seq
    c = min(CUM_BLK, S)

    def body(f_ref, b_ref, dF_ref, df_ref, db_ref):
        b = pl.program_id(0)
        ri = lax.broadcasted_iota(jnp.int32, (c, c), 0)
        ci = lax.broadcasted_iota(jnp.int32, (c, c), 1)
        tri = (ri <= ci).astype(BF16)
        carry = jnp.zeros((1, LANES), F32)
        dbp = jnp.zeros((1, LANES), F32)
        for j in reversed(range(S // c)):
            dFc = dF_ref[j * c:(j + 1) * c, :]
            dlf = _tri_dot(tri, dFc) + carry
            carry = carry + jnp.sum(dFc, axis=0, keepdims=True)
            z = f_ref[j * c:(j + 1) * c, :] + b_ref[...]
            dz = dlf * _sigmoid(-z)
            df_ref[j * c:(j + 1) * c, :] = dz.astype(BF16)
            dbp = dbp + jnp.sum(dz, axis=0, keepdims=True)

        @pl.when(b == 0)
        def _():
            db_ref[...] = dbp

        @pl.when(b > 0)
        def _():
            db_ref[...] += dbp

    blk = pl.BlockSpec((S, LANES), lambda b: (b, 0))
    vec = pl.BlockSpec((1, LANES), lambda b: (0, 0))
    return pl.pallas_call(
        body, name=name, grid=(n_seq,), in_specs=[blk, vec, blk], out_specs=[blk, vec],
        out_shape=[jax.ShapeDtypeStruct((T, LANES), BF16), jax.ShapeDtypeStruct((1, LANES), F32)],
        compiler_params=_cp(("arbitrary",)),
    )(f, bf, dF)


def _pair_masks():
    lane = lax.broadcasted_iota(jnp.int32, (1, LANES), 1)
    lo = lane < HEAD_DIM
    return lo, jnp.logical_not(lo)


AUG0 = HEAD_DIM
Q_TILE, K_CHUNK, ROW_GROUP = 2048, 256, 64


def _fox_prep(f, bf, proj, n_seq, name):
    T = f.shape[0]
    S = T // n_seq
    c = min(CUM_BLK, S)

    def body(f_ref, b_ref, qkv_ref, qa_ref, ka_ref, va_ref):
        ri = lax.broadcasted_iota(jnp.int32, (c, c), 0)
        ci = lax.broadcasted_iota(jnp.int32, (c, c), 1)
        tri = (ri >= ci).astype(BF16)
        lane = lax.broadcasted_iota(jnp.int32, (c, LANES), 1)
        carry = jnp.zeros((1, LANES), F32)
        for j in range(S // c):
            rows = slice(j * c, (j + 1) * c)
            lf = _log_sigmoid(f_ref[rows, :] + b_ref[...])
            Fc = _tri_dot(tri, lf) + carry
            carry = carry + jnp.sum(lf, axis=0, keepdims=True)
            for h in range(N_HEADS):
                col = jnp.sum(jnp.where(lane == h, Fc, 0.0), axis=-1, keepdims=True)
                hi = col.astype(BF16).astype(F32)
                r1 = col - hi
                mid = r1.astype(BF16).astype(F32)
                lo = r1 - mid
                ones_q = jnp.logical_and(lane >= AUG0 + 3, lane < AUG0 + 6)
                ones_k = jnp.logical_and(lane >= AUG0, lane < AUG0 + 3)
                aug_q = jnp.where(lane == AUG0, hi, jnp.where(lane == AUG0 + 1, mid, jnp.where(
                    lane == AUG0 + 2, lo, jnp.where(ones_q, 1.0, 0.0))))
                aug_k = jnp.where(lane == AUG0 + 3, -hi, jnp.where(lane == AUG0 + 4, -mid, jnp.where(
                    lane == AUG0 + 5, -lo, jnp.where(ones_k, 1.0, 0.0))))
                base = (h // 2) * TRIPLE
                qp, kp, vp = (qkv_ref[rows, base + t * LANES:base + (t + 1) * LANES].astype(F32) for t in range(3))
                if h % 2:
                    qp, kp, vp = (pltpu.roll(a, HEAD_DIM, 1) for a in (qp, kp, vp))
                out = slice(h * LANES, (h + 1) * LANES)
                qa_ref[rows, out] = jnp.where(lane < HEAD_DIM, qp * (HEAD_DIM ** -0.5), aug_q).astype(BF16)
                ka_ref[rows, out] = jnp.where(lane < HEAD_DIM, kp, aug_k).astype(BF16)
                va_ref[rows, out] = jnp.where(lane < HEAD_DIM, vp, jnp.where(lane == AUG0, 1.0, 0.0)).astype(BF16)

    fblk = pl.BlockSpec((S, LANES), lambda b: (b, 0))
    out = pl.BlockSpec((S, N_HEADS * LANES), lambda b: (b, 0))
    sh = jax.ShapeDtypeStruct((T, N_HEADS * LANES), BF16)
    return pl.pallas_call(
        body, name=name, grid=(n_seq,),
        in_specs=[fblk, pl.BlockSpec((1, LANES), lambda b: (0, 0)),
                  pl.BlockSpec((S, 4 * TRIPLE), lambda b: (b, OFF_QKV // (4 * TRIPLE)))],
        out_specs=[out, out, out], out_shape=[sh, sh, sh],
        compiler_params=_cp(("parallel",)),
    )(f, bf, proj)


def _band_mask(q0, k0, nq, nk):
    row = q0 + lax.broadcasted_iota(jnp.int32, (nq, nk), 0)
    col = k0 + lax.broadcasted_iota(jnp.int32, (nq, nk), 1)
    return col <= row


_NT = (((1,), (1,)), ((), ()))
_TN = (((0,), (0,)), ((), ()))


def _attn_fwd2(qa, ka, va, n_seq, name):
    T = qa.shape[0]
    S = T // n_seq
    tq, tk, rg = min(Q_TILE, S), min(K_CHUNK, S), ROW_GROUP
    nq, per = S // tq, tq // tk

    def body(q_ref, k_ref, v_ref, o_ref, o32_ref, lse_ref, phi_s, plo_s, mp_s, m_s, acc_s):
        qi = pl.program_id(2)
        mp_s[...] = jnp.full_like(mp_s, NEG_INF)
        acc_s[...] = jnp.zeros_like(acc_s)

        def scores(kc, hh, r0):
            k0 = pl.multiple_of(kc * tk, tk)
            hl = slice(hh * LANES, (hh + 1) * LANES)
            return k0, lax.dot_general(q_ref[r0:, hl], k_ref[pl.ds(k0, tk), hl], _NT, preferred_element_type=F32)

        def max_chunk(kc, masked, r0):
            for hh in range(2):
                k0, s_all = scores(kc, hh, r0)
                for r in range(r0 // rg, tq // rg):
                    rows = slice(r * rg, (r + 1) * rg)
                    s = s_all[r * rg - r0:(r + 1) * rg - r0, :]
                    if masked and r * rg < r0 + tk:
                        s = jnp.where(_band_mask(qi * tq + r * rg, k0, rg, tk), s, NEG_INF)
                    part = s[:, :LANES]
                    for c in range(1, tk // LANES):
                        part = jnp.maximum(part, s[:, c * LANES:(c + 1) * LANES])
                    mp_s[hh, rows, :] = jnp.maximum(mp_s[hh, rows, :], part)

        def sum_chunk(kc, masked, r0):
            for hh in range(2):
                k0, s_all = scores(kc, hh, r0)
                hl = slice(hh * LANES, (hh + 1) * LANES)
                v = v_ref[pl.ds(k0, tk), hl]
                for r in range(r0 // rg, tq // rg):
                    rows = slice(r * rg, (r + 1) * rg)
                    p = jnp.exp(s_all[r * rg - r0:(r + 1) * rg - r0, :] - m_s[hh, rows])
                    if masked and r * rg < r0 + tk:
                        p = jnp.where(_band_mask(qi * tq + r * rg, k0, rg, tk), p, 0.0)
                    p_hi = p.astype(BF16)
                    phi_s[hh, rows, :] = p_hi
                    plo_s[hh, rows, :] = (p - p_hi.astype(F32)).astype(BF16)
                acc_s[hh, r0:, :] += (jnp.dot(phi_s[hh, r0:, :], v, preferred_element_type=F32)
                                      + jnp.dot(plo_s[hh, r0:, :], v, preferred_element_type=F32))

        def sweep(chunk):
            def unmasked(kc, carry):
                chunk(kc, False, 0)
                return carry

            lax.fori_loop(0, qi * per, unmasked, 0)
            for d in range(per):
                chunk(qi * per + d, True, d * tk)

        sweep(max_chunk)
        m_s[...] = jnp.max(mp_s[...], axis=-1, keepdims=True)
        sweep(sum_chunk)

        lane = lax.broadcasted_iota(jnp.int32, (1, LANES), 1)
        outs = []
        for hh in range(2):
            acc = acc_s[hh]
            l = jnp.sum(jnp.where(lane == AUG0, acc, 0.0), axis=-1, keepdims=True)
            lse_ref[hh] = m_s[hh] + jnp.log(l)
            outs.append(acc / l)
        o = jnp.where(lane < HEAD_DIM, outs[0], pltpu.roll(outs[1], HEAD_DIM, 1))
        o_ref[...] = o.astype(BF16)
        o32_ref[...] = o

    qmap = lambda b, j, qi: (b * nq + qi, j)
    omap = lambda b, j, qi: (b * nq + qi, j)
    kv = pl.BlockSpec((S, 2 * LANES), lambda b, j, qi: (b, j))
    return pl.pallas_call(
        body, name=name, grid=(n_seq, N_HEADS // 2, nq),
        in_specs=[pl.BlockSpec((tq, 2 * LANES), qmap), kv, kv],
        out_specs=[pl.BlockSpec((tq, LANES), omap), pl.BlockSpec((tq, LANES), omap),
                   pl.BlockSpec((2, tq, 1), lambda b, j, qi: (j, b * nq + qi, 0))],
        out_shape=[jax.ShapeDtypeStruct((T, BRANCH_W), BF16), jax.ShapeDtypeStruct((T, BRANCH_W), F32),
                   jax.ShapeDtypeStruct((N_HEADS, T, 1), F32)],
        scratch_shapes=[pltpu.VMEM((2, tq, tk), BF16), pltpu.VMEM((2, tq, tk), BF16),
                        pltpu.VMEM((2, tq, LANES), F32), pltpu.VMEM((2, tq, 1), F32),
                        pltpu.VMEM((2, tq, LANES), F32)],
        compiler_params=_cp(("parallel", "parallel", "parallel")),
    )(qa, ka, va)


def _attn_bwd(qa, ka, proj, do, o32, lse, dproj, n_seq, name):
    T = qa.shape[0]
    S = T // n_seq
    tq, tk, rg = min(Q_TILE, S), min(K_CHUNK, S), ROW_GROUP
    nq, per, nkc = S // tq, tq // tk, S // tk

    def body(q_ref, k_ref, v_ref, do_ref, o_ref, lse_ref, _, dqkv_ref, dfk_ref,
             p_s, ds_s, dq_s, dk_s, dv_s, df_s):
        dk_s[...] = jnp.zeros_like(dk_s)
        dv_s[...] = jnp.zeros_like(dv_s)
        df_s[...] = jnp.zeros_like(df_s)
        sels = _pair_masks()

        for qi in range(nq):
            q0 = qi * tq
            do_t = do_ref[q0:q0 + tq, :]
            dq_s[...] = jnp.zeros_like(dq_s)
            prod = do_t.astype(F32) * o_ref[q0:q0 + tq, :]
            dls = [jnp.sum(jnp.where(sel, prod, 0.0), axis=-1, keepdims=True) for sel in sels]

            def chunk(kc, masked, r0, q0=q0, do_t=do_t, dls=dls):
                k0 = pl.multiple_of(kc * tk, tk)
                v = v_ref[pl.ds(k0, tk), :]
                do_a = do_t[r0:, :]
                for hh in range(2):
                    hl = slice(hh * LANES, (hh + 1) * LANES)
                    qh, kh = q_ref[q0 + r0:q0 + tq, hl], k_ref[pl.ds(k0, tk), hl]
                    s_all = lax.dot_general(qh, kh, _NT, preferred_element_type=F32)
                    dom = jnp.where(sels[hh], do_a, jnp.zeros_like(do_a))
                    dp_all = lax.dot_general(dom, v, _NT, preferred_element_type=F32)
                    dfp = jnp.zeros((1, tk), F32)
                    for r in range(r0 // rg, tq // rg):
                        rows = slice(r * rg, (r + 1) * rg)
                        arows = slice(r * rg - r0, (r + 1) * rg - r0)
                        qrows = slice(q0 + r * rg, q0 + (r + 1) * rg)
                        p = jnp.exp(s_all[arows, :] - lse_ref[hh, qrows])
                        if masked and r * rg < r0 + tk:
                            p = jnp.where(_band_mask(q0 + r * rg, k0, rg, tk), p, 0.0)
                        ds = p * (dp_all[arows, :] - dls[hh][rows])
                        p_s[hh, rows, :] = p.astype(BF16)
                        ds_s[hh, rows, :] = ds.astype(BF16)
                        dfp = dfp + jnp.sum(ds, axis=0, keepdims=True)
                    df_s[hh, kc] -= dfp
                    dq_s[hh, r0:, :] += jnp.dot(ds_s[hh, r0:, :], kh, preferred_element_type=F32)
                    dv_s[hh, pl.ds(k0, tk), :] += lax.dot_general(p_s[hh, r0:, :], do_a, _TN,
                                                                  preferred_element_type=F32)
                    dk_s[hh, pl.ds(k0, tk), :] += lax.dot_general(ds_s[hh, r0:, :], qh, _TN,
                                                                  preferred_element_type=F32)

            def unmasked(kc, carry, chunk=chunk):
                chunk(kc, False, 0)
                return carry

            lax.fori_loop(0, qi * per, unmasked, 0)
            for d in range(per):
                chunk(qi * per + d, True, d * tk)
            dq = jnp.where(sels[0], dq_s[0], pltpu.roll(dq_s[1], HEAD_DIM, 1))
            dqkv_ref[q0:q0 + tq, :LANES] = (dq * (HEAD_DIM ** -0.5)).astype(BF16)

        dqkv_ref[:, LANES:2 * LANES] = jnp.where(sels[0], dk_s[0], pltpu.roll(dk_s[1], HEAD_DIM, 1)).astype(BF16)
        dqkv_ref[:, 2 * LANES:] = jnp.where(sels[0], dv_s[0], dv_s[1]).astype(BF16)
        for c in range(nkc):
            dfk_ref[:, :, c * tk:(c + 1) * tk] = df_s[:, c]

    seq = lambda w: pl.BlockSpec((S, w), lambda b, j: (b, j))
    col1 = pl.BlockSpec((2, S, 1), lambda b, j: (j, b, 0))
    vblk = pl.BlockSpec((S, LANES), lambda b, j: (b, OFF_QKV // LANES + 3 * j + 2))
    return pl.pallas_call(
        body, name=name, grid=(n_seq, N_HEADS // 2),
        in_specs=[seq(2 * LANES), seq(2 * LANES), vblk, seq(LANES), seq(LANES), col1,
                  pl.BlockSpec(memory_space=pl.ANY)],
        out_specs=[pl.BlockSpec((S, TRIPLE), lambda b, j: (b, OFF_QKV // TRIPLE + j)),
                   pl.BlockSpec((2, 1, S), lambda b, j: (j, 0, b))],
        out_shape=[jax.ShapeDtypeStruct(dproj.shape, BF16), jax.ShapeDtypeStruct((N_HEADS, 1, T), F32)],
        input_output_aliases={6: 0},
        scratch_shapes=[pltpu.VMEM((2, tq, tk), BF16), pltpu.VMEM((2, tq, tk), BF16),
                        pltpu.VMEM((2, tq, LANES), F32), pltpu.VMEM((2, S, LANES), F32),
                        pltpu.VMEM((2, S, LANES), F32), pltpu.VMEM((2, nkc, 1, tk), F32)],
        compiler_params=_cp(("parallel", "parallel")),
    )(qa, ka, proj, do, o32, lse, dproj)


def _shift_down(v, k, row):
    return jnp.where(row >= k, pltpu.roll(v, k, 0), 0.0)


def _shift_up(v, k, row, S):
    return jnp.where(row < S - k, pltpu.roll(v, S - k, 0), 0.0)


def _pool_diff(uf, w, row):
    acc, k = uf, 1
    while k < w:
        acc = acc + _shift_down(acc, k, row)
        k *= 2
    n = jnp.minimum(row + 1, w).astype(F32)
    return acc / n - uf


def _pool_fwd(proj, pool_w, pool_scale, n_seq, name):
    T = proj.shape[0]
    S = T // n_seq

    def body(u_ref, w_ref, sc_ref, o_ref, d_s):
        g = pl.program_id(1)
        row = lax.broadcasted_iota(jnp.int32, (S, POOL_GD), 0)
        uf = u_ref[...].astype(F32)
        for gi, wlen in enumerate(POOL_WINDOWS):
            @pl.when(g == gi)
            def _(wlen=wlen):
                d_s[...] = _pool_diff(uf, wlen, row).astype(BF16)
        e = jnp.dot(d_s[...], w_ref[0], preferred_element_type=F32)
        o_ref[...] = (e * sc_ref[...]).astype(BF16)

    uc = OFF_U // POOL_GD
    return pl.pallas_call(
        body, name=name, grid=(n_seq, len(POOL_WINDOWS)),
        in_specs=[pl.BlockSpec((S, POOL_GD), lambda b, g: (b, uc + g)),
                  pl.BlockSpec((1, POOL_GD, POOL_GD), lambda b, g: (g, 0, 0)),
                  pl.BlockSpec((1, POOL_GD), lambda b, g: (0, g))],
        out_specs=pl.BlockSpec((S, POOL_GD), lambda b, g: (b, g)),
        out_shape=jax.ShapeDtypeStruct((T, BRANCH_W), BF16),
        scratch_shapes=[pltpu.VMEM((S, POOL_GD), BF16)],
        compiler_params=_cp(("parallel", "parallel")),
    )(proj, pool_w, pool_scale)


def _pool_bwd(proj, dout, pool_w, pool_scale, dproj, n_seq, name):
    T = proj.shape[0]
    S = T // n_seq

    def body(u_ref, do_ref, w_ref, sc_ref, _, du_ref, dw_ref, dsc_ref, d_s):
        g, b = pl.program_id(0), pl.program_id(1)
        row = lax.broadcasted_iota(jnp.int32, (S, POOL_GD), 0)
        uf = u_ref[...].astype(F32)
        for gi, wlen in enumerate(POOL_WINDOWS):
            @pl.when(g == gi)
            def _(wlen=wlen):
                d_s[...] = _pool_diff(uf, wlen, row).astype(BF16)
        db16 = d_s[...]
        w = w_ref[0]
        e = jnp.dot(db16, w, preferred_element_type=F32)
        dof = do_ref[...].astype(F32)
        dsc = jnp.sum(dof * e, axis=0, keepdims=True)
        de = (dof * sc_ref[...]).astype(BF16)
        dd = lax.dot_general(de, w, (((1,), (1,)), ((), ())), preferred_element_type=F32)
        dw = lax.dot_general(db16, de, (((0,), (0,)), ((), ())), preferred_element_type=F32)
        for gi, wlen in enumerate(POOL_WINDOWS):
            @pl.when(g == gi)
            def _(wlen=wlen):
                n = jnp.minimum(row + 1, wlen).astype(F32)
                acc, k = dd / n, 1
                while k < wlen:
                    acc = acc + _shift_up(acc, k, row, S)
                    k *= 2
                du_ref[...] = (acc - dd).astype(BF16)

        @pl.when(b == 0)
        def _():
            dw_ref[0] = dw
            dsc_ref[...] = dsc

        @pl.when(b > 0)
        def _():
            dw_ref[0] += dw
            dsc_ref[...] += dsc

    uc = OFF_U // POOL_GD
    return pl.pallas_call(
        body, name=name, grid=(len(POOL_WINDOWS), n_seq),
        in_specs=[pl.BlockSpec((S, POOL_GD), lambda g, b: (b, uc + g)),
                  pl.BlockSpec((S, POOL_GD), lambda g, b: (b, g)),
                  pl.BlockSpec((1, POOL_GD, POOL_GD), lambda g, b: (g, 0, 0)),
                  pl.BlockSpec((1, POOL_GD), lambda g, b: (0, g)),
                  pl.BlockSpec(memory_space=pl.ANY)],
        out_specs=[pl.BlockSpec((S, POOL_GD), lambda g, b: (b, uc + g)),
                   pl.BlockSpec((1, POOL_GD, POOL_GD), lambda g, b: (g, 0, 0)),
                   pl.BlockSpec((1, POOL_GD), lambda g, b: (0, g))],
        out_shape=[jax.ShapeDtypeStruct(dproj.shape, BF16),
                   jax.ShapeDtypeStruct((len(POOL_WINDOWS), POOL_GD, POOL_GD), F32),
                   jax.ShapeDtypeStruct((1, BRANCH_W), F32)],
        input_output_aliases={4: 0},
        scratch_shapes=[pltpu.VMEM((S, POOL_GD), BF16)],
        compiler_params=_cp(("parallel", "arbitrary")),
    )(proj, dout, pool_w, pool_scale, dproj)


def _conv_fwd(proj, conv_w, n_seq, name):
    T = proj.shape[0]
    S = T // n_seq
    nc = BRANCH_W // LANES

    def body(c_ref, w_ref, o_ref):
        row = lax.broadcasted_iota(jnp.int32, (S, LANES), 0)
        cv, cb, cc = (c_ref[:, t * LANES:(t + 1) * LANES].astype(F32) for t in range(3))
        z = cc * cv
        w = w_ref[...]
        y = w[0:1] * _shift_down(z, 2, row) + w[1:2] * _shift_down(z, 1, row) + w[2:3] * z
        o_ref[...] = (cb * y).astype(BF16)

    return pl.pallas_call(
        body, name=name, grid=(n_seq, nc),
        in_specs=[pl.BlockSpec((S, TRIPLE), lambda b, j: (b, OFF_CONV // TRIPLE + j)),
                  pl.BlockSpec((CONV_K, LANES), lambda b, j: (0, j))],
        out_specs=pl.BlockSpec((S, LANES), lambda b, j: (b, j)),
        out_shape=jax.ShapeDtypeStruct((T, BRANCH_W), BF16),
        compiler_params=_cp(("parallel", "parallel")),
    )(proj, conv_w)


def _conv_bwd(proj, dout, conv_w, dproj, n_seq, name):
    T = proj.shape[0]
    S = T // n_seq
    nc = BRANCH_W // LANES

    def body(c_ref, do_ref, w_ref, _, dc_ref, dw_ref):
        b = pl.program_id(1)
        row = lax.broadcasted_iota(jnp.int32, (S, LANES), 0)
        cv, cb, cc = (c_ref[:, t * LANES:(t + 1) * LANES].astype(F32) for t in range(3))
        dof = do_ref[...].astype(F32)
        w = w_ref[...]
        z = cc * cv
        z1, z2 = _shift_down(z, 1, row), _shift_down(z, 2, row)
        y = w[0:1] * z2 + w[1:2] * z1 + w[2:3] * z
        dy = dof * cb
        dz = w[2:3] * dy + w[1:2] * _shift_up(dy, 1, row, S) + w[0:1] * _shift_up(dy, 2, row, S)
        dc_ref[:, :LANES] = (dz * cc).astype(BF16)
        dc_ref[:, LANES:2 * LANES] = (dof * y).astype(BF16)
        dc_ref[:, 2 * LANES:] = (dz * cv).astype(BF16)
        dws = [jnp.sum(dy * zk, axis=0, keepdims=True) for zk in (z2, z1, z)]

        @pl.when(b == 0)
        def _():
            for kk in range(CONV_K):
                dw_ref[kk:kk + 1, :] = dws[kk]

        @pl.when(b > 0)
        def _():
            for kk in range(CONV_K):
                dw_ref[kk:kk + 1, :] += dws[kk]

    triple = pl.BlockSpec((S, TRIPLE), lambda j, b: (b, OFF_CONV // TRIPLE + j))
    wsp = pl.BlockSpec((CONV_K, LANES), lambda j, b: (0, j))
    return pl.pallas_call(
        body, name=name, grid=(nc, n_seq),
        in_specs=[triple, pl.BlockSpec((S, LANES), lambda j, b: (b, j)), wsp, pl.BlockSpec(memory_space=pl.ANY)],
        out_specs=[triple, wsp],
        out_shape=[jax.ShapeDtypeStruct(dproj.shape, BF16), jax.ShapeDtypeStruct((CONV_K, BRANCH_W), F32)],
        input_output_aliases={3: 0},
        compiler_params=_cp(("parallel", "arbitrary")),
    )(proj, dout, conv_w, dproj)


def _mix_fwd(oa, ob, oc, wpa, wpp, wpc, proj, b_gate, name):
    T = oa.shape[0]
    tm = min(512, T)

    def body(oa_ref, ob_ref, oc_ref, wa_ref, wp_ref, wc_ref, g_ref, bg_ref, o_ref):
        acc = jnp.zeros((tm, D_MODEL), F32)
        for i, (x_ref, w_ref) in enumerate(((oa_ref, wa_ref), (ob_ref, wp_ref), (oc_ref, wc_ref))):
            y = jnp.dot(x_ref[...], w_ref[...], preferred_element_type=F32)
            sl = slice(i * D_MODEL, (i + 1) * D_MODEL)
            acc = acc + _sigmoid(g_ref[:, sl].astype(F32) + bg_ref[:, sl]) * y
        o_ref[...] = acc.astype(BF16)

    br = pl.BlockSpec((tm, BRANCH_W), lambda i: (i, 0))
    wsp = pl.BlockSpec((BRANCH_W, D_MODEL), lambda i: (0, 0))
    return pl.pallas_call(
        body, name=name, grid=(T // tm,),
        in_specs=[br, br, br, wsp, wsp, wsp, pl.BlockSpec((tm, GATE_W), lambda i: (i, 0)),
                  pl.BlockSpec((1, GATE_W), lambda i: (0, 0))],
        out_specs=pl.BlockSpec((tm, D_MODEL), lambda i: (i, 0)),
        out_shape=jax.ShapeDtypeStruct((T, D_MODEL), BF16),
        compiler_params=_cp(("parallel",)),
    )(oa, ob, oc, wpa, wpp, wpc, proj, b_gate)


def _mix_bwd(oa, ob, oc, wpa, wpp, wpc, proj, b_gate, dmixed, name):
    T = oa.shape[0]
    tm = min(256, T)

    def body(oa_ref, ob_ref, oc_ref, wa_ref, wp_ref, wc_ref, g_ref, bg_ref, dm_ref,
             dya_ref, dyb_ref, dyc_ref, dg_ref, dbg_ref):
        i0 = pl.program_id(0)
        dm = dm_ref[...].astype(F32)
        parts = []
        for i, (x_ref, w_ref, dy_ref) in enumerate(((oa_ref, wa_ref, dya_ref), (ob_ref, wp_ref, dyb_ref),
                                                    (oc_ref, wc_ref, dyc_ref))):
            y = jnp.dot(x_ref[...], w_ref[...], preferred_element_type=F32)
            sl = slice(i * D_MODEL, (i + 1) * D_MODEL)
            gate = _sigmoid(g_ref[:, sl].astype(F32) + bg_ref[:, sl])
            dy_ref[...] = (dm * gate).astype(BF16)
            dgl = dm * y * gate * (1.0 - gate)
            dg_ref[:, sl] = dgl.astype(BF16)
            parts.append(jnp.sum(dgl, axis=0, keepdims=True))

        @pl.when(i0 == 0)
        def _():
            for i in range(3):
                dbg_ref[:, i * D_MODEL:(i + 1) * D_MODEL] = parts[i]

        @pl.when(i0 > 0)
        def _():
            for i in range(3):
                dbg_ref[:, i * D_MODEL:(i + 1) * D_MODEL] += parts[i]

    br = pl.BlockSpec((tm, BRANCH_W), lambda i: (i, 0))
    wsp = pl.BlockSpec((BRANCH_W, D_MODEL), lambda i: (0, 0))
    row = pl.BlockSpec((tm, D_MODEL), lambda i: (i, 0))
    gsp = pl.BlockSpec((tm, GATE_W), lambda i: (i, 0))
    bsp = pl.BlockSpec((1, GATE_W), lambda i: (0, 0))
    act = jax.ShapeDtypeStruct((T, D_MODEL), BF16)
    return pl.pallas_call(
        body, name=name, grid=(T // tm,),
        in_specs=[br, br, br, wsp, wsp, wsp, gsp, bsp, row],
        out_specs=[row, row, row, gsp, bsp],
        out_shape=[act, act, act, jax.ShapeDtypeStruct((T, MAIN_COLS), BF16),
                   jax.ShapeDtypeStruct((1, GATE_W), F32)],
        compiler_params=_cp(("arbitrary",)),
    )(oa, ob, oc, wpa, wpp, wpc, proj, b_gate, dmixed)


GU_TILE = 256


def _gu_col(c):
    t, r = divmod(c, GU_TILE)
    return (t // 2) * GU_TILE + r + (FFN_HIDDEN if t % 2 else 0)


def _gate_up_swiglu(h, w, name):
    T, K = h.shape
    tm = min(2048, T)

    def body(h_ref, w_ref, ab_ref, s_ref):
        prod = jnp.dot(h_ref[...], w_ref[...], preferred_element_type=F32)
        ab_ref[...] = prod.astype(BF16)
        a = prod[:, :GU_TILE]
        s_ref[...] = (a * _sigmoid(a) * prod[:, GU_TILE:]).astype(BF16)

    return pl.pallas_call(
        body, name=name, grid=(T // tm, FFN_HIDDEN // GU_TILE),
        in_specs=[pl.BlockSpec((tm, K), lambda i, j: (i, 0)), pl.BlockSpec((K, 2 * GU_TILE), lambda i, j: (0, j))],
        out_specs=[pl.BlockSpec((tm, 2 * GU_TILE), lambda i, j: (i, j)), pl.BlockSpec((tm, GU_TILE), lambda i, j: (i, j))],
        out_shape=[jax.ShapeDtypeStruct((T, 2 * FFN_HIDDEN), BF16), jax.ShapeDtypeStruct((T, FFN_HIDDEN), BF16)],
        compiler_params=_cp(("parallel", "parallel")),
    )(h, w)


def _swiglu_bwd_fused(dx, w_down, ab, name):
    T, K = dx.shape
    tm = min(2048, T)

    def body(dx_ref, w_ref, ab_ref, o_ref):
        ds = lax.dot_general(dx_ref[...], w_ref[...], _NT, preferred_element_type=F32)
        a = ab_ref[:, :GU_TILE].astype(F32)
        b = ab_ref[:, GU_TILE:].astype(F32)
        sg = _sigmoid(a)
        o_ref[:, :GU_TILE] = (ds * b * sg * (1.0 + a * (1.0 - sg))).astype(BF16)
        o_ref[:, GU_TILE:] = (ds * a * sg).astype(BF16)

    pair = pl.BlockSpec((tm, 2 * GU_TILE), lambda i, j: (i, j))
    return pl.pallas_call(
        body, name=name, grid=(T // tm, FFN_HIDDEN // GU_TILE),
        in_specs=[pl.BlockSpec((tm, K), lambda i, j: (i, 0)), pl.BlockSpec((GU_TILE, K), lambda i, j: (j, 0)), pair],
        out_specs=pair, out_shape=jax.ShapeDtypeStruct((T, 2 * FFN_HIDDEN), BF16),
        compiler_params=_cp(("parallel", "parallel")),
    )(dx, w_down, ab)


def _adamw_update(w_ref, g_ref, m_ref, v_ref, d_ref, nm_ref, nv_ref):
    gv = g_ref[...]
    nm = ADAM_B1 * m_ref[...] + (1.0 - ADAM_B1) * gv
    nv = ADAM_B2 * v_ref[...] + (1.0 - ADAM_B2) * (gv * gv)
    m_hat = nm / (1.0 - ADAM_B1 ** ADAM_STEP)
    v_hat = nv / (1.0 - ADAM_B2 ** ADAM_STEP)
    d_ref[...] = -ADAM_LR * (m_hat / (jnp.sqrt(v_hat) + ADAM_EPS) + ADAM_WD * w_ref[...])
    nm_ref[...] = nm
    nv_ref[...] = nv


def _adamw_many(ws, gs, ms, vs, name):
    n = len(ws)

    def body(*refs):
        ins, outs = refs[:4 * n], refs[4 * n:]
        for t in range(n):
            _adamw_update(ins[t], ins[n + t], ins[2 * n + t], ins[3 * n + t], outs[t], outs[n + t], outs[2 * n + t])

    shapes = [jax.ShapeDtypeStruct(w.shape, F32) for w in ws]
    out = pl.pallas_call(body, name=name, out_shape=shapes * 3, compiler_params=_cp())(*ws, *gs, *ms, *vs)
    return out[:n], out[n:2 * n], out[2 * n:]


def _adamw(w, g, m, v, name):
    R, C = w.shape
    tr = R
    for cand in (256, 352, 128, 64, 8):
        if R > cand and R % cand == 0:
            tr = cand
            break

    def body(w_ref, g_ref, m_ref, v_ref, d_ref, nm_ref, nv_ref):
        _adamw_update(w_ref, g_ref, m_ref, v_ref, d_ref, nm_ref, nv_ref)

    blk = pl.BlockSpec((tr, C), lambda i: (i, 0))
    sh = jax.ShapeDtypeStruct((R, C), F32)
    return pl.pallas_call(
        body, name=name, grid=(R // tr,), in_specs=[blk] * 4, out_specs=[blk] * 3, out_shape=[sh] * 3,
        compiler_params=_cp(("parallel",)),
    )(w, g, m, v)


def _adamw_3d(w, g, m, v, block, name):
    shape = w.shape
    grid = (shape[0] // block[0], shape[1] // block[1])
    assert shape[0] % block[0] == 0 and shape[1] % block[1] == 0 and block[2] == shape[2], (name, shape, block)

    def body(w_ref, g_ref, m_ref, v_ref, d_ref, nm_ref, nv_ref):
        _adamw_update(w_ref, g_ref, m_ref, v_ref, d_ref, nm_ref, nv_ref)

    blk = pl.BlockSpec(block, lambda i, j: (i, j, 0))
    sh = jax.ShapeDtypeStruct(shape, F32)
    return pl.pallas_call(
        body, name=name, grid=grid, in_specs=[blk] * 4, out_specs=[blk] * 3, out_shape=[sh] * 3,
        compiler_params=_cp(("parallel", "parallel")),
    )(w, g, m, v)


def _sum_slabs_t(x, name):
    n, R, C = x.shape

    def body(x_ref, o_ref):
        acc = x_ref[0].astype(F32)
        for j in range(1, n):
            acc = acc + x_ref[j].astype(F32)
        o_ref[...] = acc.T

    return pl.pallas_call(
        body, name=name, grid=(C // LANES,), in_specs=[pl.BlockSpec((n, R, LANES), lambda j: (0, 0, j))],
        out_specs=pl.BlockSpec((LANES, R), lambda j: (j, 0)), out_shape=jax.ShapeDtypeStruct((C, R), F32),
        compiler_params=_cp(("parallel",)),
    )(x)


def _sum_slabs(x, name):
    n, R, C = x.shape
    tr = R
    for cand in (512, 256, 128, 64, 32, 16, 8):
        if R > cand and R % cand == 0:
            tr = cand
            break

    def body(x_ref, o_ref):
        acc = x_ref[0].astype(F32)
        for j in range(1, n):
            acc = acc + x_ref[j].astype(F32)
        o_ref[...] = acc

    return pl.pallas_call(
        body, name=name, grid=(R // tr,), in_specs=[pl.BlockSpec((n, tr, C), lambda i: (0, i, 0))],
        out_specs=pl.BlockSpec((tr, C), lambda i: (i, 0)), out_shape=jax.ShapeDtypeStruct((R, C), F32),
        compiler_params=_cp(("parallel",)),
    )(x)


def _multi_gather(xs, layers, name):
    nt = len(xs)
    shapes = [x.shape if lay is None else x.shape[1:] for x, lay in zip(xs, layers)]

    def body(*refs):
        x_refs, out_refs = refs[:nt], refs[nt:2 * nt]
        send_sems, recv_sems, local_sems = refs[2 * nt:]
        x_, y_, c_ = lax.axis_index("x"), lax.axis_index("y"), lax.axis_index("c")
        me, sibling = (x_, y_, c_), (x_, y_, 1 - c_)
        chips = [(1 - x_, y_), (x_, 1 - y_), (1 - x_, 1 - y_)]

        def own_block(t):
            return x_refs[t] if layers[t] is None else x_refs[t].at[layers[t]]

        def copy(t, k, block, to, own=False):
            px, py, pc = block
            dst = out_refs[t].at[4 * px + 2 * py + pc]
            return pltpu.make_async_remote_copy(
                src_ref=own_block(t) if own else dst, dst_ref=dst,
                send_sem=send_sems.at[t, k], recv_sem=recv_sems.at[t, k],
                device_id=to, device_id_type=pl.DeviceIdType.MESH)

        mine, first, passed = [], [], []
        for t in range(nt):
            mine.append(pltpu.make_async_copy(own_block(t), out_refs[t].at[4 * x_ + 2 * y_ + c_], local_sems.at[t]))
            mine[-1].start()
            first.append([copy(t, 1 + j, me, (*chip, c_), own=True) for j, chip in enumerate(chips)]
                         + [copy(t, 0, me, sibling, own=True)])
            for cp in first[-1]:
                cp.start()
        for t in range(nt):
            for j, chip in enumerate(chips):
                copy(t, 1 + j, (*chip, c_), me).wait_recv()
                passed.append(copy(t, 4 + j, (*chip, c_), sibling))
                passed[-1].start()
        for t in range(nt):
            copy(t, 0, sibling, me).wait_recv()
            for j, chip in enumerate(chips):
                copy(t, 4 + j, (*chip, 1 - c_), me).wait_recv()
        for cp in [c for f in first for c in f] + passed:
            cp.wait_send()
        for cp in mine:
            cp.wait()

    hbm = pl.BlockSpec(memory_space=pl.ANY)
    return pl.pallas_call(
        body, name=name, out_shape=[jax.ShapeDtypeStruct((N_DEV,) + tuple(s), x.dtype) for s, x in zip(shapes, xs)],
        in_specs=[hbm] * nt, out_specs=[hbm] * nt,
        scratch_shapes=[pltpu.SemaphoreType.DMA((nt, 7)), pltpu.SemaphoreType.DMA((nt, 7)),
                        pltpu.SemaphoreType.DMA((nt,))],
    )(*xs)


_HBM = pl.BlockSpec(memory_space=pltpu.HBM)
_SEM = pl.BlockSpec(memory_space=pltpu.SEMAPHORE)
_PEER_ORDER = (2, 4, 6, 3, 5, 7, 1)


def _split_copies(src_refs, land_refs, send_sems, recv_sems, layers, per_peer):
    x_, y_, c_ = lax.axis_index("x"), lax.axis_index("y"), lax.axis_index("c")
    me = 4 * x_ + 2 * y_ + c_
    copies = []
    for k in _PEER_ORDER:
        px, py, pc = x_ ^ ((k >> 2) & 1), y_ ^ ((k >> 1) & 1), c_ ^ (k & 1)
        peer = 4 * px + 2 * py + pc
        for t in range(len(src_refs)):
            if per_peer:
                src = src_refs[t].at[peer]
            else:
                src = src_refs[t] if layers[t] is None else src_refs[t].at[layers[t]]
            copies.append(pltpu.make_async_remote_copy(
                src_ref=src, dst_ref=land_refs[t].at[me],
                send_sem=send_sems.at[t * (N_DEV - 1) + k - 1], recv_sem=recv_sems.at[t * (N_DEV - 1) + k - 1],
                device_id=(px, py, pc), device_id_type=pl.DeviceIdType.MESH))
    return copies


def _own_copies(src_refs, land_refs, sems, layers, per_peer):
    nt = len(src_refs)
    me = 4 * lax.axis_index("x") + 2 * lax.axis_index("y") + lax.axis_index("c")
    copies = []
    for t in range(nt):
        if per_peer:
            src = src_refs[t].at[me]
        else:
            src = src_refs[t] if layers[t] is None else src_refs[t].at[layers[t]]
        copies.append(pltpu.make_async_copy(src, land_refs[t].at[me], sems.at[nt * (N_DEV - 1) + t]))
    return copies


def _split_start(srcs, layers, per_peer, after, name):
    nt = len(srcs)
    if per_peer:
        land_shapes = [s.shape for s in srcs]
    else:
        land_shapes = [(N_DEV,) + tuple(s.shape if lay is None else s.shape[1:]) for s, lay in zip(srcs, layers)]

    def body(*refs):
        src_refs, land_refs = refs[:nt], refs[nt:2 * nt]
        send_sems, recv_sems = refs[2 * nt + 1], refs[2 * nt + 2]
        token = refs[-1]
        for cp in _split_copies(src_refs, land_refs, send_sems, recv_sems, layers, per_peer):
            cp.start()
        for cp in _own_copies(src_refs, land_refs, send_sems, layers, per_peer):
            cp.start()
        token[...] = jnp.zeros_like(token)

    lands = [pltpu.with_memory_space_constraint(lax.empty(s, x.dtype), pltpu.HBM) for s, x in zip(land_shapes, srcs)]
    srcs = [pltpu.with_memory_space_constraint(x, pltpu.HBM) for x in srcs]
    out = pl.pallas_call(
        body, name=name,
        out_shape=(pltpu.SemaphoreType.DMA((nt * N_DEV,)), pltpu.SemaphoreType.DMA((nt * (N_DEV - 1),)),
                   *[pltpu.HBM(x.shape, x.dtype) for x in srcs], *[pltpu.HBM(s, x.dtype) for s, x in zip(land_shapes, srcs)],
                   jax.ShapeDtypeStruct((8, LANES), F32)),
        in_specs=[_HBM] * (2 * nt) + [pl.BlockSpec(memory_space=pl.ANY)],
        out_specs=(_SEM, _SEM, *([_HBM] * (2 * nt)), pl.BlockSpec(memory_space=pltpu.VMEM)),
        input_output_aliases={i: 2 + i for i in range(2 * nt)},
        compiler_params=pltpu.CompilerParams(has_side_effects=pltpu.SideEffectType.DATAFLOW_SIDE_EFFECTING),
    )(*srcs, *lands, after)
    return out[0], out[1], list(out[2:2 + nt]), list(out[2 + nt:2 + 2 * nt]), out[-1]


def _split_wait(started, layers, per_peer, after, name):
    send_sems, recv_sems, srcs, lands, _ = started
    nt = len(srcs)

    def body(*refs):
        src_refs, land_refs = refs[:nt], refs[nt:2 * nt]
        s_sems, r_sems = refs[2 * nt], refs[2 * nt + 1]
        for cp in _split_copies(src_refs, land_refs, s_sems, r_sems, layers, per_peer):
            cp.wait_send()
            cp.wait_recv()
        for cp in _own_copies(src_refs, land_refs, s_sems, layers, per_peer):
            cp.wait()

    out = pl.pallas_call(
        body, name=name,
        out_shape=tuple(pltpu.HBM(x.shape, x.dtype) for x in srcs + lands),
        in_specs=[_HBM] * (2 * nt) + [_SEM, _SEM, pl.BlockSpec(memory_space=pl.ANY)],
        out_specs=tuple([_HBM] * (2 * nt)),
        input_output_aliases={i: i for i in range(2 * nt)},
        compiler_params=pltpu.CompilerParams(has_side_effects=pltpu.SideEffectType.DATAFLOW_SIDE_EFFECTING),
    )(*srcs, *lands, send_sems, recv_sems, after)
    return list(out[nt:])


def _runs(mapping):
    runs, c, n = [], 0, len(mapping)
    while c < n:
        if mapping[c] is None:
            c += 1
            continue
        sid, d, lo = mapping[c][0], mapping[c][1] - c, c
        while c < n and mapping[c] is not None and mapping[c][0] == sid and mapping[c][1] - c == d:
            c += 1
        runs.append((lo, c, sid, d))
    return runs


def _tile_plan(mapping, src_widths):
    runs = _runs(mapping)
    plan = []
    for t in range(len(mapping) // LANES):
        pieces = []
        for lo, hi, sid, d in runs:
            lo_t, hi_t = max(lo, t * LANES), min(hi, (t + 1) * LANES)
            if lo_t >= hi_t:
                continue
            a = ((lo_t + d) // LANES) * LANES
            win = min(2 * LANES, src_widths[sid] - a)
            shift = t * LANES + d - a
            pieces.append((sid, a, win, shift, lo_t - t * LANES, hi_t - t * LANES))
        plan.append(pieces)
    return plan


def _reblock(srcs, src_views, outs, out_views, name):
    R = srcs[0].shape[-2]
    tr = min(512, R)
    widths = {sid: srcs[ai].shape[-1] for sid, (ai, _) in src_views.items()}
    plans = [(ai, li, _tile_plan(mapping, widths)) for ai, li, mapping in out_views]
    ns = len(srcs)

    def body(*refs):
        s_refs, o_refs = refs[:ns], refs[ns:]
        cache = {}

        def shift_matrix(win, shift, lo, hi):
            key = (win, shift, lo, hi)
            if key not in cache:
                r = lax.broadcasted_iota(jnp.int32, (win, LANES), 0)
                c = lax.broadcasted_iota(jnp.int32, (win, LANES), 1)
                hit = jnp.logical_and(r - c == shift, jnp.logical_and(c >= lo, c < hi))
                cache[key] = jnp.where(hit, 1.0, 0.0).astype(BF16)
            return cache[key]

        for ai, li, plan in plans:
            for t, pieces in enumerate(plan):
                acc = None
                whole = len(pieces) == 1 and pieces[0][3:] == (0, 0, LANES)
                for sid, a, win, shift, lo, hi in pieces:
                    sa, sl = src_views[sid]
                    if whole:
                        win = LANES
                    src = s_refs[sa][:, a:a + win] if sl is None else s_refs[sa][sl, :, a:a + win]
                    if whole:
                        acc = src
                    else:
                        part = jnp.dot(src, shift_matrix(win, shift, lo, hi), preferred_element_type=F32)
                        acc = part if acc is None else acc + part
                val = jnp.zeros((tr, LANES), BF16) if acc is None else acc.astype(BF16)
                if li is None:
                    o_refs[ai][:, t * LANES:(t + 1) * LANES] = val
                else:
                    o_refs[ai][li, :, t * LANES:(t + 1) * LANES] = val

    def spec(shape):
        if len(shape) == 2:
            return pl.BlockSpec((tr, shape[1]), lambda i: (i, 0))
        return pl.BlockSpec((shape[0], tr, shape[2]), lambda i: (0, i, 0))

    return pl.pallas_call(
        body, name=name, grid=(R // tr,), in_specs=[spec(s.shape) for s in srcs],
        out_specs=[spec(s) for s in outs], out_shape=[jax.ShapeDtypeStruct(s, BF16) for s in outs],
        compiler_params=_cp(("parallel",)),
    )(*srcs)


SHARDED = ("w_in", "w_gate_up", "w_proj_attn", "w_proj_pool", "w_proj_conv", "w_out", "w_down")
WEIGHT_ORDER = ("attn_norm", "w_in", "b_forget", "b_gate", "w_proj_attn", "pool_w", "pool_scale", "w_proj_pool",
                "conv_w", "w_proj_conv", "w_out", "ffn_norm", "w_gate_up", "w_down", "final_norm")
IN_SHARD, IN_SHARD_PAD = IN_COLS // N_DEV, 896
GU_SHARD, GU_SHARD_PAD = 2 * FFN_HIDDEN // N_DEV, 768


def _w_in_col(c):
    if c < OFF_QKV:
        return c + 3592
    if c < OFF_U:
        base, off = (0, OFF_QKV) if c < OFF_CONV else (2056, OFF_CONV)
        j, t = divmod(c - off, TRIPLE)
        which, e = divmod(t, LANES)
        return base + which * BRANCH_W + j * LANES + e
    return c - OFF_U + 1544


def _w_in_full(gathered, name):
    main = [divmod(_w_in_col(c), IN_SHARD) for c in range(MAIN_COLS)]
    fcols = [divmod(1536 + c, IN_SHARD) if c < N_HEADS else None for c in range(LANES)]
    R = gathered.shape[1]
    return _reblock([gathered], {i: (0, i) for i in range(N_DEV)}, [(R, MAIN_COLS), (R, LANES)],
                    [(0, None, main), (1, None, fcols)], name)


def _w_in_slabs(dmain, dwf, name):
    inv = {_w_in_col(c): ("m", c) for c in range(MAIN_COLS)}
    inv.update({1536 + c: ("f", c) for c in range(N_HEADS)})
    views = []
    for i in range(N_DEV):
        mapping = [inv[IN_SHARD * i + j] if j < IN_SHARD else None for j in range(IN_SHARD_PAD)]
        views.append((0, i, mapping))
    R = dmain.shape[0]
    return _reblock([dmain, dwf], {"m": (0, None), "f": (1, None)}, [(N_DEV, R, IN_SHARD_PAD)], views, name)[0]


def _w_gu_full(gathered, name):
    mapping = [divmod(_gu_col(c), GU_SHARD) for c in range(2 * FFN_HIDDEN)]
    R = gathered.shape[1]
    return _reblock([gathered], {i: (0, i) for i in range(N_DEV)}, [(R, 2 * FFN_HIDDEN)], [(0, None, mapping)], name)[0]


def _w_gu_slabs(dw, name):
    inv = {_gu_col(c): c for c in range(2 * FFN_HIDDEN)}
    views = [(0, i, [("w", inv[GU_SHARD * i + j]) if j < GU_SHARD else None for j in range(GU_SHARD_PAD)])
             for i in range(N_DEV)]
    R = dw.shape[0]
    return _reblock([dw], {"w": (0, None)}, [(N_DEV, R, GU_SHARD_PAD)], views, name)[0]


def _layer_fwd(x, W, n_seq, l, h1=None, next_norm=None):
    T = x.shape[0]
    sfx = f"_l{l}"
    if h1 is None:
        h1 = _rms_fwd(x, W["attn_norm"], "rms1" + sfx)
    proj, f = _matmul(h1, W["w_main"], mode="nn", out_dtype=BF16, name="proj_main" + sfx, side=(W["w_f"], F32))
    qa, ka, va = _fox_prep(f, W["b_forget"], proj, n_seq, "fox_prep" + sfx)
    oa, oa32, lse = _attn_fwd2(qa, ka, va, n_seq, "attn_fwd" + sfx)
    if "late" in W:
        W.update(W.pop("late")(oa))
    ob = _pool_fwd(proj, W["pool_w"], W["pool_scale"], n_seq, "pool_fwd" + sfx)
    oc = _conv_fwd(proj, W["conv_w"], n_seq, "conv_fwd" + sfx)
    mixed = _mix_fwd(oa, ob, oc, W["w_proj_attn"], W["w_proj_pool"], W["w_proj_conv"], proj, W["b_gate"],
                     "mix_fwd" + sfx)
    x2, h2 = _matmul(mixed, W["w_out"], mode="nn", out_dtype=F32, name="out_proj" + sfx, tm=1024, tn=1024,
                     residual=x, rms_g=W["ffn_norm"])
    ab, s = _gate_up_swiglu(h2, W["w_gate_up"], "gate_up" + sfx)
    x3 = _matmul(s, W["w_down"], mode="nn", out_dtype=F32, name="down" + sfx, tm=1024, tn=1024, tk=1408,
                 residual=x2, rms_g=next_norm)
    x3, h1_next = x3 if next_norm is not None else (x3, None)
    saved = dict(x=x, h1=h1, proj=proj, f=f, qa=qa, ka=ka, oa=oa, oa32=oa32, lse=lse, ob=ob, oc=oc, mixed=mixed, x2=x2,
                 h2=h2, ab=ab, s=s)
    return x3, saved, h1_next


def _layer_bwd(dx3, dx3b, W, sv, n_seq, l, stage=None):
    T = dx3.shape[0]
    sfx = f"_l{l}"
    G = {}
    stage = stage or (lambda l, group, G, W: W)
    dab = _swiglu_bwd_fused(dx3b, W["w_down"], sv["ab"], "d_ab" + sfx)
    G["w_down"] = _matmul(sv["s"], dx3b, mode="tn", out_dtype=BF16, name="dw_down" + sfx, tm=256, tn=1024)
    dh2 = _matmul(dab, W["w_gate_up"], mode="nt", out_dtype=BF16, name="d_h2" + sfx, tm=1024, tn=1024, tk=1408)
    G["w_gate_up"] = _matmul(sv["h2"], dab, mode="tn", out_dtype=BF16, name="dw_gate_up" + sfx, tm=1024)
    W = stage(l, "ffn", G, W)
    dx2, dx2b, G["ffn_norm"] = _rms_bwd(sv["x2"], W["ffn_norm"], dh2, dx3, "rms2_bwd" + sfx)
    dmixed = _matmul(dx2b, W["w_out"], mode="nt", out_dtype=BF16, name="d_mixed" + sfx)
    G["w_out"] = _matmul(sv["mixed"], dx2b, mode="tn", out_dtype=BF16, name="dw_out" + sfx, tm=1024)
    dya, dyb, dyc, dproj, G["b_gate"] = _mix_bwd(sv["oa"], sv["ob"], sv["oc"], W["w_proj_attn"], W["w_proj_pool"],
                                                 W["w_proj_conv"], sv["proj"], W["b_gate"], dmixed, "mix_bwd" + sfx)
    douts = {}
    for br, dy, o in (("attn", dya, sv["oa"]), ("pool", dyb, sv["ob"]), ("conv", dyc, sv["oc"])):
        douts[br] = _matmul(dy, W["w_proj_" + br], mode="nt", out_dtype=BF16, name=f"d_{br}_out" + sfx)
        G["w_proj_" + br] = _matmul(o, dy, mode="tn", out_dtype=BF16, name=f"dw_proj_{br}" + sfx, tm=512)
    W = stage(l, "mix", G, W)
    dproj, G["conv_w"] = _conv_bwd(sv["proj"], douts["conv"], W["conv_w"], dproj, n_seq, "conv_bwd" + sfx)
    dproj, G["pool_w"], G["pool_scale"] = _pool_bwd(sv["proj"], douts["pool"], W["pool_w"], W["pool_scale"], dproj,
                                                    n_seq, "pool_bwd" + sfx)
    dproj, dFk = _attn_bwd(sv["qa"], sv["ka"], sv["proj"], douts["attn"], sv["oa32"], sv["lse"], dproj, n_seq,
                           "attn_bwd" + sfx)
    dF = jnp.pad(dFk.reshape(N_HEADS, T).T, ((0, 0), (0, LANES - N_HEADS)))
    df, G["b_forget"] = _fox_cumsum_bwd(sv["f"], W["b_forget"], dF, n_seq, "fox_cumsum_bwd" + sfx)
    G["w_main"], G["w_f"] = _matmul(sv["h1"], dproj, mode="tn", out_dtype=BF16, name="dw_main" + sfx, tm=1024,
                                    side=(df, BF16))
    W = stage(l, "w_in", G, W)
    dh1 = _matmul(dproj, W["w_main"], mode="nt", out_dtype=BF16, name="d_h1_main" + sfx, tm=1024, tn=1024, tk=1664,
                  extra=(df, W["w_f"]))
    dx, dxb, G["attn_norm"] = _rms_bwd(sv["x"], W["attn_norm"], dh1, dx2, "rms1_bwd" + sfx)
    return dx, dxb, G


def _replicated_operands(rep, l):
    W = {}
    W["attn_norm"], W["ffn_norm"] = rep["attn_norm"][l], rep["ffn_norm"][l]
    W["b_forget"] = jnp.pad(rep["b_forget"][l].reshape(1, N_HEADS), ((0, 0), (0, LANES - N_HEADS)))
    W["b_gate"] = rep["b_gate"][l].reshape(1, GATE_W)
    W["pool_w"] = rep["pool_w"][l].astype(BF16)
    W["pool_scale"] = rep["pool_scale"][l].reshape(1, BRANCH_W)
    return W


def _local_step(x, target, get_W, attn_norms, final_norm, stage=None):
    n_seq, S, Dm = x.shape
    T = n_seq * S
    xt = x.reshape(T, Dm)
    saved, Ws, h1 = [], [], None
    for l in range(DEPTH):
        Ws.append(get_W(l, xt))
        next_norm = attn_norms[l + 1] if l + 1 < DEPTH else None
        xt, sv, h1 = _layer_fwd(xt, Ws[l], n_seq, l, h1, next_norm)
        saved.append(sv)
    loss, dx, dxb, g_final = _loss_head(xt, final_norm, target.reshape(T, Dm), "loss_head")
    grads = [None] * DEPTH
    for l in reversed(range(DEPTH)):
        dx, dxb, grads[l] = _layer_bwd(dx, dxb, Ws[l], saved[l], n_seq, l, stage)
    return loss, dx.reshape(n_seq, S, Dm), grads, g_final


def _padded_shards(weights):
    pads = {"w_in": IN_SHARD_PAD - IN_SHARD, "w_gate_up": GU_SHARD_PAD - GU_SHARD}
    return {n: jnp.pad(weights[n], ((0, 0), (0, 0), (0, pads.get(n, 0)))).astype(BF16) for n in SHARDED}


def _full_operands(g, l):
    W = {}
    if "w_in" in g:
        W["w_main"], W["w_f"] = _w_in_full(g["w_in"], f"w_in_full_l{l}")
    if "w_gate_up" in g:
        W["w_gate_up"] = _w_gu_full(g["w_gate_up"], f"w_gate_up_full_l{l}")
    for n in ("w_proj_attn", "w_proj_pool", "w_proj_conv"):
        if n in g:
            W[n] = jnp.transpose(g[n], (1, 0, 2)).reshape(BRANCH_W, D_MODEL)
    if "w_out" in g:
        W["w_out"] = g["w_out"].reshape(D_MODEL, D_MODEL)
    if "w_down" in g:
        W["w_down"] = g["w_down"].reshape(FFN_HIDDEN, D_MODEL)
    return W


GRAD_GROUPS = {"ffn": ("w_down", "w_gate_up"),
               "mix": ("w_out", "w_proj_attn", "w_proj_pool", "w_proj_conv"),
               "w_in": ("w_in",)}


def _grad_slabs(G, n, l):
    if n == "w_in":
        return _w_in_slabs(G["w_main"], G["w_f"], f"w_in_slabs_l{l}")
    if n == "w_gate_up":
        return _w_gu_slabs(G["w_gate_up"], f"w_gate_up_slabs_l{l}")
    if n == "w_out":
        return G["w_out"].reshape(N_DEV, D_MODEL // N_DEV, D_MODEL)
    if n == "w_down":
        return G["w_down"].reshape(N_DEV, FFN_HIDDEN // N_DEV, D_MODEL)
    return jnp.transpose(G[n].reshape(BRANCH_W, N_DEV, D_MODEL // N_DEV), (1, 0, 2))


def _sum_layer_grads(recv, l):
    out = {}
    for n, r in recv.items():
        if n in ("w_in", "w_gate_up"):
            out[n] = _sum_slabs_t(r, f"sum_{n}_l{l}")[:IN_SHARD if n == "w_in" else GU_SHARD]
        else:
            out[n] = _sum_slabs(r, f"sum_{n}_l{l}")
    return out


def _sum_small(xs, name):
    def body(*refs):
        for x_ref, o_ref in zip(refs[:len(xs)], refs[len(xs):]):
            acc = x_ref[0]
            for j in range(1, N_DEV):
                acc = acc + x_ref[j]
            o_ref[...] = acc

    return pl.pallas_call(
        body, name=name, out_shape=[jax.ShapeDtypeStruct(x.shape[1:], F32) for x in xs],
        compiler_params=_cp(),
    )(*xs)


def _as_2d(a):
    if a.ndim == 1:
        return a.reshape(1, -1)
    return a.reshape(-1, a.shape[-1])


def kernel(x, attn_norm, w_in, b_forget, b_gate, w_proj_attn, pool_w, pool_scale, w_proj_pool, conv_w, w_proj_conv, w_out, ffn_norm, w_gate_up, w_down, final_norm, loss_target, m_attn_norm, m_w_in, m_b_forget, m_b_gate, m_w_proj_attn, m_pool_w, m_pool_scale, m_w_proj_pool, m_conv_w, m_w_proj_conv, m_w_out, m_ffn_norm, m_w_gate_up, m_w_down, m_final_norm, v_attn_norm, v_w_in, v_b_forget, v_b_gate, v_w_proj_attn, v_pool_w, v_pool_scale, v_w_proj_pool, v_conv_w, v_w_proj_conv, v_w_out, v_ffn_norm, v_w_gate_up, v_w_down, v_final_norm):
    weights = dict(attn_norm=attn_norm, w_in=w_in, b_forget=b_forget, b_gate=b_gate, w_proj_attn=w_proj_attn,
                   pool_w=pool_w, pool_scale=pool_scale, w_proj_pool=w_proj_pool, conv_w=conv_w,
                   w_proj_conv=w_proj_conv, w_out=w_out, ffn_norm=ffn_norm, w_gate_up=w_gate_up, w_down=w_down,
                   final_norm=final_norm)
    moments_m = dict(attn_norm=m_attn_norm, w_in=m_w_in, b_forget=m_b_forget, b_gate=m_b_gate,
                     w_proj_attn=m_w_proj_attn, pool_w=m_pool_w, pool_scale=m_pool_scale, w_proj_pool=m_w_proj_pool,
                     conv_w=m_conv_w, w_proj_conv=m_w_proj_conv, w_out=m_w_out, ffn_norm=m_ffn_norm,
                     w_gate_up=m_w_gate_up, w_down=m_w_down, final_norm=m_final_norm)
    moments_v = dict(attn_norm=v_attn_norm, w_in=v_w_in, b_forget=v_b_forget, b_gate=v_b_gate,
                     w_proj_attn=v_w_proj_attn, pool_w=v_pool_w, pool_scale=v_pool_scale, w_proj_pool=v_w_proj_pool,
                     conv_w=v_conv_w, w_proj_conv=v_w_proj_conv, w_out=v_w_out, ffn_norm=v_ffn_norm,
                     w_gate_up=v_w_gate_up, w_down=v_w_down, final_norm=v_final_norm)

    sh = _padded_shards(weights)
    names = list(SHARDED)
    rest = [n for n in names if n != "w_in"]
    me = 4 * lax.axis_index("x") + 2 * lax.axis_index("y") + lax.axis_index("c")
    w_in0, conv_all = _multi_gather([sh["w_in"], conv_w], [0, None], "gather_w_in_l0")
    started, after = {}, w_in0
    for l in range(DEPTH):
        for group, gnames in (("w_in", ["w_in"]), ("rest", rest)):
            if (l, group) != (0, "w_in"):
                started[l, group] = _split_start([sh[n] for n in gnames], [l] * len(gnames), False, after,
                                                 f"gather_start_{group}_l{l}")
                after = started[l, group][4]
    last_token = after

    def get_W(l, xt):
        if l == 0:
            w_in = w_in0
        else:
            w_in = _split_wait(started[l, "w_in"], [l], False, xt, f"gather_wait_w_in_l{l}")[0]
        W = _full_operands({"w_in": w_in}, l)

        def late(after):
            lands = _split_wait(started[l, "rest"], [l] * len(rest), False, after, f"gather_wait_rest_l{l}")
            return _full_operands(dict(zip(rest, lands)), l)

        W["late"] = late
        W.update(_replicated_operands(weights, l))
        W["conv_w"] = jnp.transpose(conv_all[:, l], (1, 0, 2)).reshape(CONV_K, BRANCH_W)
        if l == 0:
            W["attn_norm"] = W["attn_norm"] + last_token[0, 0]
        return W

    exchanges = []

    def stage(l, group, G, W):
        gnames = GRAD_GROUPS[group]
        slabs = [_grad_slabs(G, n, l) for n in gnames]
        started = _split_start(slabs, None, True, slabs[0][0, :8], f"exchange_start_{group}_l{l}")
        exchanges.append((l, group, gnames, slabs, started))
        tie = {"ffn": "ffn_norm", "mix": "conv_w", "w_in": "w_f"}[group]
        W = dict(W)
        W[tie] = W[tie] + started[4][0, 0].astype(W[tie].dtype)
        return W

    loss_part, grad_x, grads, g_final = _local_step(x, loss_target, get_W, attn_norm, final_norm, stage)

    def finish_exchange(ex, after):
        l, group, gnames, slabs, started = ex
        lands = _split_wait(started, None, True, after, f"exchange_wait_{group}_l{l}")
        grads[l].update(_sum_layer_grads(dict(zip(gnames, lands)), l))

    def zero_after(a):
        return jnp.minimum(jnp.abs(a[(0,) * a.ndim]), 0.0)

    *early, last_exchange = exchanges
    for ex in early:
        finish_exchange(ex, grad_x)
    deltas, new_m, new_v, gw = {}, {}, {}, {}
    views = {"w_in": ((2, 0, 1), (1, 2, 0), (49, DEPTH, D_MODEL), 1),
             "w_gate_up": ((0, 2, 1), (0, 2, 1), (1, GU_SHARD // 2, D_MODEL), 0)}

    def update_sharded(n):
        if n in views:
            perm, inv, block, layer_axis = views[n]
            gt = jnp.stack([grads[l][n] for l in range(DEPTH)], axis=layer_axis)
            d, nm, nv = _adamw_3d(jnp.transpose(weights[n], perm), gt, jnp.transpose(moments_m[n], perm),
                                  jnp.transpose(moments_v[n], perm), block, "adamw_" + n)
            deltas[n], new_m[n], new_v[n] = (jnp.transpose(a, inv) for a in (d, nm, nv))
            gw[n] = jnp.transpose(gt, inv)
            return
        gw[n] = jnp.stack([grads[l][n] for l in range(DEPTH)])
        shape = weights[n].shape
        d, nm, nv = _adamw(_as_2d(weights[n]), _as_2d(gw[n]), _as_2d(moments_m[n]), _as_2d(moments_v[n]),
                           "adamw_" + n)
        deltas[n], new_m[n], new_v[n] = d.reshape(shape), nm.reshape(shape), nv.reshape(shape)

    for n in SHARDED:
        if n != "w_in":
            update_sharded(n)

    small = ("attn_norm", "b_forget", "b_gate", "pool_w", "pool_scale", "ffn_norm", "conv_w")
    loss_part = loss_part + zero_after(deltas["w_gate_up"]) + zero_after(deltas["w_down"])
    parts = [jnp.stack([grads[l][n] for l in range(DEPTH)]) for n in small] + [g_final, loss_part]
    gathered = _multi_gather(parts, [None] * len(parts), "gather_small_grads")
    summed = _sum_small(gathered, "sum_small_grads")
    for n, s in zip(small, summed):
        gw[n] = s
    gw["attn_norm"], gw["ffn_norm"] = gw["attn_norm"][:, 0], gw["ffn_norm"][:, 0]
    gw["b_forget"] = gw["b_forget"][:, 0, :N_HEADS]
    gw["b_gate"], gw["pool_scale"] = gw["b_gate"][:, 0], gw["pool_scale"][:, 0]
    gw["conv_w"] = lax.dynamic_slice_in_dim(gw["conv_w"], me * (BRANCH_W // N_DEV), BRANCH_W // N_DEV, axis=2)
    gw["final_norm"] = summed[-2][0]
    loss = summed[-1][0, 0]

    rest_names = [n for n in WEIGHT_ORDER if n not in SHARDED]
    ds, nms, nvs = _adamw_many(*[[_as_2d(src[n]) for n in rest_names] for src in (weights, gw, moments_m, moments_v)],
                               "adamw_small")
    for n, d, nm, nv in zip(rest_names, ds, nms, nvs):
        shape = weights[n].shape
        deltas[n], new_m[n], new_v[n] = d.reshape(shape), nm.reshape(shape), nv.reshape(shape)

    finish_exchange(last_exchange, deltas["pool_w"])
    update_sharded("w_in")

    return (loss, grad_x, *[gw[n] for n in WEIGHT_ORDER], *[deltas[n] for n in WEIGHT_ORDER],
            *[new_m[n] for n in WEIGHT_ORDER], *[new_v[n] for n in WEIGHT_ORDER])
```
